```python
import jax, jax.numpy as jnp
from jax import lax
import numpy as np

D_MODEL = 1024
BATCH = 8
SEQ = 4096
DEPTH = 1

MEM_LEN = 256
HG_HEADS = 4
HG_DK = 128
HG_DV = 128
HG_WIDTH = HG_HEADS * HG_DV
HG_CHUNK = 64
SG_GROUPS = 4
SG_DIM = 128
SG_WIDTH = SG_GROUPS * SG_DIM
SG_CHUNK = 128
MIX_WIDTH = HG_WIDTH + SG_WIDTH
IN_WIDTH = 4 * HG_WIDTH + 2 * SG_WIDTH
X_HEADS = 4
X_HEAD_DIM = D_MODEL // X_HEADS
D_FF = 2816
ALPHA = (2.0 * DEPTH) ** 0.25
BETA = (8.0 * DEPTH) ** -0.25
LN_EPS = 1e-5

kernel_name = "hybrid_hgrn2_sgu_macaron_deepnorm"


def _layer_norm(x, g, b):
    xf = x.astype(jnp.float32)
    mu = jnp.mean(xf, axis=-1, keepdims=True)
    var = jnp.mean(jnp.square(xf - mu), axis=-1, keepdims=True)
    return ((xf - mu) * lax.rsqrt(var + LN_EPS) * g + b).astype(x.dtype)


def _rms_norm(x, g):
    xf = x.astype(jnp.float32)
    return xf * lax.rsqrt(jnp.mean(jnp.square(xf), axis=-1, keepdims=True) + LN_EPS) * g


def _swiglu(x, w_gate, w_up, w_down):
    return (jax.nn.silu(x @ w_gate) * (x @ w_up)) @ w_down


def _hgrn2(q, fz, iv, lb):
    B, T, H, DK = q.shape
    DV = iv.shape[-1]
    n_chunks = T // HG_CHUNK
    q = q.astype(jnp.float32)
    fz = fz.astype(jnp.float32)
    iv = iv.astype(jnp.float32)
    log_f = jnp.log(lb + (1.0 - lb) * jax.nn.sigmoid(fz))
    k = (1.0 - lb) * jax.nn.sigmoid(-fz)

    def chunks(a):
        return a.reshape(B, n_chunks, HG_CHUNK, H, a.shape[-1]).transpose(1, 0, 3, 2, 4)

    causal = jnp.tril(jnp.ones((HG_CHUNK, HG_CHUNK), dtype=bool))[:, :, None]

    def step(S, inp):
        qc, kc, vc, lfc = inp
        b = jnp.cumsum(lfc, axis=2)
        diff = b[:, :, :, None, :] - b[:, :, None, :, :]
        decay = jnp.where(causal, jnp.exp(jnp.where(causal, diff, 0.0)), 0.0)
        scores = jnp.einsum('bhtd,bhsd,bhtsd->bhts', qc, kc, decay)
        o = (jnp.einsum('bhts,bhsv->bhtv', scores, vc)
             + jnp.einsum('bhtd,bhdv->bhtv', qc * jnp.exp(b), S))
        b_end = b[:, :, -1:, :]
        S = (jnp.exp(b_end[:, :, 0, :])[..., None] * S
             + jnp.einsum('bhsd,bhsv->bhdv', kc * jnp.exp(b_end - b), vc))
        return S, o

    S0 = jnp.zeros((B, H, DK, DV), jnp.float32)
    _, o = lax.scan(step, S0, (chunks(q), chunks(k), chunks(iv), chunks(log_f)))
    return o.transpose(1, 0, 3, 2, 4).reshape(B, T, H, DV)


def _spatial_gating(uv, w_s, b_s, ln_g, ln_b):
    B, T, _ = uv.shape
    u, v = jnp.split(uv, 2, axis=-1)
    v = _layer_norm(v.reshape(B, T, SG_GROUPS, SG_DIM), ln_g, ln_b)
    v = v.reshape(B, T // SG_CHUNK, SG_CHUNK, SG_GROUPS, SG_DIM)
    causal = jnp.tril(jnp.ones((SG_CHUNK, SG_CHUNK), dtype=bool))
    w = jnp.where(causal, w_s, 0.0)
    s = jnp.einsum('gts,bnsgc->bntgc', w, v) + b_s.T[None, None, :, :, None]
    return u * s.reshape(B, T, SG_WIDTH)


def _token_mixers(h, w_in, lb, hg_norm_g, sg_ln_g, sg_ln_b, sg_w_s, sg_b_s, w_out):
    B, T, _ = h.shape
    proj = h @ w_in
    q, fz, iv, g, uv = jnp.split(
        proj, [HG_WIDTH, 2 * HG_WIDTH, 3 * HG_WIDTH, 4 * HG_WIDTH], axis=-1)
    heads = lambda a: a.reshape(B, T, HG_HEADS, -1)
    o = _hgrn2(heads(q), heads(fz), heads(iv), lb)
    o = _rms_norm(o, hg_norm_g) * jax.nn.silu(heads(g).astype(jnp.float32))
    o_a = o.reshape(B, T, HG_WIDTH).astype(h.dtype)
    o_b = _spatial_gating(jax.nn.gelu(uv), sg_w_s, sg_b_s, sg_ln_g, sg_ln_b)
    return jnp.concatenate([o_a, o_b], axis=-1) @ w_out


def _memory_cross_attention(h, mem, mem_g, mem_b, wq, wk, wv, wo):
    B, T, _ = h.shape
    M = mem.shape[1]
    m = _layer_norm(mem, mem_g, mem_b)
    q = (h @ wq).reshape(B, T, X_HEADS, X_HEAD_DIM)
    k = (m @ wk).reshape(B, M, X_HEADS, X_HEAD_DIM)
    v = (m @ wv).reshape(B, M, X_HEADS, X_HEAD_DIM)
    s = jnp.einsum('bthd,bmhd->bhtm', q.astype(jnp.float32), k.astype(jnp.float32)) * (X_HEAD_DIM ** -0.5)
    p = jax.nn.softmax(s, axis=-1).astype(h.dtype)
    o = jnp.einsum('bhtm,bmhd->bthd', p, v).reshape(B, T, D_MODEL)
    return o @ wo


def _fwd_setup_inputs(seed: int = 0) -> dict:
    key = jax.random.key(seed)
    ks = iter(jax.random.split(key, 48))
    L = DEPTH

    def nrm(shape, scale):
        return jax.random.normal(next(ks), shape, jnp.float32) * scale

    def gain(shape):
        return 1.0 + nrm(shape, 0.05)

    def bias(shape):
        return nrm(shape, 0.01)

    d_in = D_MODEL ** -0.5
    f_in = D_FF ** -0.5
    return {
        "x": nrm((BATCH, SEQ, D_MODEL), 1.0),
        "mem": nrm((BATCH, MEM_LEN, D_MODEL), 1.0),
        "ffn1_w_gate": nrm((L, D_MODEL, D_FF), d_in),
        "ffn1_w_up": nrm((L, D_MODEL, D_FF), d_in),
        "ffn1_w_down": nrm((L, D_FF, D_MODEL), f_in * BETA),
        "ln1_g": gain((L, D_MODEL)),
        "ln1_b": bias((L, D_MODEL)),
        "w_in": nrm((L, D_MODEL, IN_WIDTH), d_in),
        "hg_lb_logits": nrm((DEPTH + 1, HG_HEADS, HG_DK), 0.5),
        "hg_norm_g": gain((L, HG_DV)),
        "sg_ln_g": gain((L, SG_GROUPS, SG_DIM)),
        "sg_ln_b": bias((L, SG_GROUPS, SG_DIM)),
        "sg_w_s": nrm((L, SG_GROUPS, SG_CHUNK, SG_CHUNK), SG_CHUNK ** -0.5),
        "sg_b_s": gain((L, SG_GROUPS, SG_CHUNK)),
        "w_out": nrm((L, MIX_WIDTH, D_MODEL), (MIX_WIDTH ** -0.5) * BETA),
        "ln2_g": gain((L, D_MODEL)),
        "ln2_b": bias((L, D_MODEL)),
        "mem_ln_g": gain((L, D_MODEL)),
        "mem_ln_b": bias((L, D_MODEL)),
        "xa_w_q": nrm((L, D_MODEL, D_MODEL), d_in),
        "xa_w_k": nrm((L, D_MODEL, D_MODEL), d_in),
        "xa_w_v": nrm((L, D_MODEL, D_MODEL), d_in * BETA),
        "xa_w_o": nrm((L, D_MODEL, D_MODEL), d_in * BETA),
        "ln3_g": gain((L, D_MODEL)),
        "ln3_b": bias((L, D_MODEL)),
        "ffn2_w_gate": nrm((L, D_MODEL, D_FF), d_in),
        "ffn2_w_up": nrm((L, D_MODEL, D_FF), d_in),
        "ffn2_w_down": nrm((L, D_FF, D_MODEL), f_in * BETA),
        "ln4_g": gain((L, D_MODEL)),
        "ln4_b": bias((L, D_MODEL)),
    }


def _fwd_reference(x, mem, ffn1_w_gate, ffn1_w_up, ffn1_w_down, ln1_g, ln1_b,
              w_in, hg_lb_logits, hg_norm_g, sg_ln_g, sg_ln_b, sg_w_s, sg_b_s,
              w_out, ln2_g, ln2_b, mem_ln_g, mem_ln_b, xa_w_q, xa_w_k, xa_w_v,
              xa_w_o, ln3_g, ln3_b, ffn2_w_gate, ffn2_w_up, ffn2_w_down,
              ln4_g, ln4_b):
    lower_bounds = jnp.cumsum(jax.nn.softmax(hg_lb_logits.astype(jnp.float32), axis=0), axis=0)
    h = x
    for l in range(DEPTH):
        h = _layer_norm(ALPHA * h + 0.5 * _swiglu(h, ffn1_w_gate[l], ffn1_w_up[l], ffn1_w_down[l]),
                        ln1_g[l], ln1_b[l])
        mix = _token_mixers(h, w_in[l], lower_bounds[l], hg_norm_g[l], sg_ln_g[l], sg_ln_b[l],
                            sg_w_s[l], sg_b_s[l], w_out[l])
        h = _layer_norm(ALPHA * h + mix, ln2_g[l], ln2_b[l])
        xa = _memory_cross_attention(h, mem, mem_ln_g[l], mem_ln_b[l], xa_w_q[l], xa_w_k[l],
                                     xa_w_v[l], xa_w_o[l])
        h = _layer_norm(ALPHA * h + xa, ln3_g[l], ln3_b[l])
        h = _layer_norm(ALPHA * h + 0.5 * _swiglu(h, ffn2_w_gate[l], ffn2_w_up[l], ffn2_w_down[l]),
                        ln4_g[l], ln4_b[l])
    return h


import jax as _jax
import jax.numpy as _jnp

TWIN_FORMAT = 'train_step'
FWD_PARAMS = ['x', 'mem', 'ffn1_w_gate', 'ffn1_w_up', 'ffn1_w_down', 'ln1_g', 'ln1_b', 'w_in', 'hg_lb_logits', 'hg_norm_g', 'sg_ln_g', 'sg_ln_b', 'sg_w_s', 'sg_b_s', 'w_out', 'ln2_g', 'ln2_b', 'mem_ln_g', 'mem_ln_b', 'xa_w_q', 'xa_w_k', 'xa_w_v', 'xa_w_o', 'ln3_g', 'ln3_b', 'ffn2_w_gate', 'ffn2_w_up', 'ffn2_w_down', 'ln4_g', 'ln4_b']
TWIN_WEIGHTS = ['ffn1_w_gate', 'ffn1_w_up', 'ffn1_w_down', 'ln1_g', 'ln1_b', 'w_in', 'hg_lb_logits', 'hg_norm_g', 'sg_ln_g', 'sg_ln_b', 'sg_w_s', 'sg_b_s', 'w_out', 'ln2_g', 'ln2_b', 'mem_ln_g', 'mem_ln_b', 'xa_w_q', 'xa_w_k', 'xa_w_v', 'xa_w_o', 'ln3_g', 'ln3_b', 'ffn2_w_gate', 'ffn2_w_up', 'ffn2_w_down', 'ln4_g', 'ln4_b']
TWIN_DIFF_INPUT = 'x'
TWIN_INPUTS = ['x', 'mem', 'ffn1_w_gate', 'ffn1_w_up', 'ffn1_w_down', 'ln1_g', 'ln1_b', 'w_in', 'hg_lb_logits', 'hg_norm_g', 'sg_ln_g', 'sg_ln_b', 'sg_w_s', 'sg_b_s', 'w_out', 'ln2_g', 'ln2_b', 'mem_ln_g', 'mem_ln_b', 'xa_w_q', 'xa_w_k', 'xa_w_v', 'xa_w_o', 'ln3_g', 'ln3_b', 'ffn2_w_gate', 'ffn2_w_up', 'ffn2_w_down', 'ln4_g', 'ln4_b', 'loss_target', 'm_ffn1_w_gate', 'm_ffn1_w_up', 'm_ffn1_w_down', 'm_ln1_g', 'm_ln1_b', 'm_w_in', 'm_hg_lb_logits', 'm_hg_norm_g', 'm_sg_ln_g', 'm_sg_ln_b', 'm_sg_w_s', 'm_sg_b_s', 'm_w_out', 'm_ln2_g', 'm_ln2_b', 'm_mem_ln_g', 'm_mem_ln_b', 'm_xa_w_q', 'm_xa_w_k', 'm_xa_w_v', 'm_xa_w_o', 'm_ln3_g', 'm_ln3_b', 'm_ffn2_w_gate', 'm_ffn2_w_up', 'm_ffn2_w_down', 'm_ln4_g', 'm_ln4_b', 'v_ffn1_w_gate', 'v_ffn1_w_up', 'v_ffn1_w_down', 'v_ln1_g', 'v_ln1_b', 'v_w_in', 'v_hg_lb_logits', 'v_hg_norm_g', 'v_sg_ln_g', 'v_sg_ln_b', 'v_sg_w_s', 'v_sg_b_s', 'v_w_out', 'v_ln2_g', 'v_ln2_b', 'v_mem_ln_g', 'v_mem_ln_b', 'v_xa_w_q', 'v_xa_w_k', 'v_xa_w_v', 'v_xa_w_o', 'v_ln3_g', 'v_ln3_b', 'v_ffn2_w_gate', 'v_ffn2_w_up', 'v_ffn2_w_down', 'v_ln4_g', 'v_ln4_b']
TWIN_OUTPUTS = ['loss', 'grad_x', 'grad_ffn1_w_gate', 'grad_ffn1_w_up', 'grad_ffn1_w_down', 'grad_ln1_g', 'grad_ln1_b', 'grad_w_in', 'grad_hg_lb_logits', 'grad_hg_norm_g', 'grad_sg_ln_g', 'grad_sg_ln_b', 'grad_sg_w_s', 'grad_sg_b_s', 'grad_w_out', 'grad_ln2_g', 'grad_ln2_b', 'grad_mem_ln_g', 'grad_mem_ln_b', 'grad_xa_w_q', 'grad_xa_w_k', 'grad_xa_w_v', 'grad_xa_w_o', 'grad_ln3_g', 'grad_ln3_b', 'grad_ffn2_w_gate', 'grad_ffn2_w_up', 'grad_ffn2_w_down', 'grad_ln4_g', 'grad_ln4_b', 'delta_ffn1_w_gate', 'delta_ffn1_w_up', 'delta_ffn1_w_down', 'delta_ln1_g', 'delta_ln1_b', 'delta_w_in', 'delta_hg_lb_logits', 'delta_hg_norm_g', 'delta_sg_ln_g', 'delta_sg_ln_b', 'delta_sg_w_s', 'delta_sg_b_s', 'delta_w_out', 'delta_ln2_g', 'delta_ln2_b', 'delta_mem_ln_g', 'delta_mem_ln_b', 'delta_xa_w_q', 'delta_xa_w_k', 'delta_xa_w_v', 'delta_xa_w_o', 'delta_ln3_g', 'delta_ln3_b', 'delta_ffn2_w_gate', 'delta_ffn2_w_up', 'delta_ffn2_w_down', 'delta_ln4_g', 'delta_ln4_b', 'new_m_ffn1_w_gate', 'new_m_ffn1_w_up', 'new_m_ffn1_w_down', 'new_m_ln1_g', 'new_m_ln1_b', 'new_m_w_in', 'new_m_hg_lb_logits', 'new_m_hg_norm_g', 'new_m_sg_ln_g', 'new_m_sg_ln_b', 'new_m_sg_w_s', 'new_m_sg_b_s', 'new_m_w_out', 'new_m_ln2_g', 'new_m_ln2_b', 'new_m_mem_ln_g', 'new_m_mem_ln_b', 'new_m_xa_w_q', 'new_m_xa_w_k', 'new_m_xa_w_v', 'new_m_xa_w_o', 'new_m_ln3_g', 'new_m_ln3_b', 'new_m_ffn2_w_gate', 'new_m_ffn2_w_up', 'new_m_ffn2_w_down', 'new_m_ln4_g', 'new_m_ln4_b', 'new_v_ffn1_w_gate', 'new_v_ffn1_w_up', 'new_v_ffn1_w_down', 'new_v_ln1_g', 'new_v_ln1_b', 'new_v_w_in', 'new_v_hg_lb_logits', 'new_v_hg_norm_g', 'new_v_sg_ln_g', 'new_v_sg_ln_b', 'new_v_sg_w_s', 'new_v_sg_b_s', 'new_v_w_out', 'new_v_ln2_g', 'new_v_ln2_b', 'new_v_mem_ln_g', 'new_v_mem_ln_b', 'new_v_xa_w_q', 'new_v_xa_w_k', 'new_v_xa_w_v', 'new_v_xa_w_o', 'new_v_ln3_g', 'new_v_ln3_b', 'new_v_ffn2_w_gate', 'new_v_ffn2_w_up', 'new_v_ffn2_w_down', 'new_v_ln4_g', 'new_v_ln4_b']
TWIN_LEAF_KINDS = {'loss': 'loss', 'grad_x': 'grad_x', 'grad_ffn1_w_gate': 'grad_w', 'grad_ffn1_w_up': 'grad_w', 'grad_ffn1_w_down': 'grad_w', 'grad_ln1_g': 'grad_w', 'grad_ln1_b': 'grad_w', 'grad_w_in': 'grad_w', 'grad_hg_lb_logits': 'grad_w', 'grad_hg_norm_g': 'grad_w', 'grad_sg_ln_g': 'grad_w', 'grad_sg_ln_b': 'grad_w', 'grad_sg_w_s': 'grad_w', 'grad_sg_b_s': 'grad_w', 'grad_w_out': 'grad_w', 'grad_ln2_g': 'grad_w', 'grad_ln2_b': 'grad_w', 'grad_mem_ln_g': 'grad_w', 'grad_mem_ln_b': 'grad_w', 'grad_xa_w_q': 'grad_w', 'grad_xa_w_k': 'grad_w', 'grad_xa_w_v': 'grad_w', 'grad_xa_w_o': 'grad_w', 'grad_ln3_g': 'grad_w', 'grad_ln3_b': 'grad_w', 'grad_ffn2_w_gate': 'grad_w', 'grad_ffn2_w_up': 'grad_w', 'grad_ffn2_w_down': 'grad_w', 'grad_ln4_g': 'grad_w', 'grad_ln4_b': 'grad_w', 'delta_ffn1_w_gate': 'delta_w', 'delta_ffn1_w_up': 'delta_w', 'delta_ffn1_w_down': 'delta_w', 'delta_ln1_g': 'delta_w', 'delta_ln1_b': 'delta_w', 'delta_w_in': 'delta_w', 'delta_hg_lb_logits': 'delta_w', 'delta_hg_norm_g': 'delta_w', 'delta_sg_ln_g': 'delta_w', 'delta_sg_ln_b': 'delta_w', 'delta_sg_w_s': 'delta_w', 'delta_sg_b_s': 'delta_w', 'delta_w_out': 'delta_w', 'delta_ln2_g': 'delta_w', 'delta_ln2_b': 'delta_w', 'delta_mem_ln_g': 'delta_w', 'delta_mem_ln_b': 'delta_w', 'delta_xa_w_q': 'delta_w', 'delta_xa_w_k': 'delta_w', 'delta_xa_w_v': 'delta_w', 'delta_xa_w_o': 'delta_w', 'delta_ln3_g': 'delta_w', 'delta_ln3_b': 'delta_w', 'delta_ffn2_w_gate': 'delta_w', 'delta_ffn2_w_up': 'delta_w', 'delta_ffn2_w_down': 'delta_w', 'delta_ln4_g': 'delta_w', 'delta_ln4_b': 'delta_w', 'new_m_ffn1_w_gate': 'new_m', 'new_m_ffn1_w_up': 'new_m', 'new_m_ffn1_w_down': 'new_m', 'new_m_ln1_g': 'new_m', 'new_m_ln1_b': 'new_m', 'new_m_w_in': 'new_m', 'new_m_hg_lb_logits': 'new_m', 'new_m_hg_norm_g': 'new_m', 'new_m_sg_ln_g': 'new_m', 'new_m_sg_ln_b': 'new_m', 'new_m_sg_w_s': 'new_m', 'new_m_sg_b_s': 'new_m', 'new_m_w_out': 'new_m', 'new_m_ln2_g': 'new_m', 'new_m_ln2_b': 'new_m', 'new_m_mem_ln_g': 'new_m', 'new_m_mem_ln_b': 'new_m', 'new_m_xa_w_q': 'new_m', 'new_m_xa_w_k': 'new_m', 'new_m_xa_w_v': 'new_m', 'new_m_xa_w_o': 'new_m', 'new_m_ln3_g': 'new_m', 'new_m_ln3_b': 'new_m', 'new_m_ffn2_w_gate': 'new_m', 'new_m_ffn2_w_up': 'new_m', 'new_m_ffn2_w_down': 'new_m', 'new_m_ln4_g': 'new_m', 'new_m_ln4_b': 'new_m', 'new_v_ffn1_w_gate': 'new_v', 'new_v_ffn1_w_up': 'new_v', 'new_v_ffn1_w_down': 'new_v', 'new_v_ln1_g': 'new_v', 'new_v_ln1_b': 'new_v', 'new_v_w_in': 'new_v', 'new_v_hg_lb_logits': 'new_v', 'new_v_hg_norm_g': 'new_v', 'new_v_sg_ln_g': 'new_v', 'new_v_sg_ln_b': 'new_v', 'new_v_sg_w_s': 'new_v', 'new_v_sg_b_s': 'new_v', 'new_v_w_out': 'new_v', 'new_v_ln2_g': 'new_v', 'new_v_ln2_b': 'new_v', 'new_v_mem_ln_g': 'new_v', 'new_v_mem_ln_b': 'new_v', 'new_v_xa_w_q': 'new_v', 'new_v_xa_w_k': 'new_v', 'new_v_xa_w_v': 'new_v', 'new_v_xa_w_o': 'new_v', 'new_v_ln3_g': 'new_v', 'new_v_ln3_b': 'new_v', 'new_v_ffn2_w_gate': 'new_v', 'new_v_ffn2_w_up': 'new_v', 'new_v_ffn2_w_down': 'new_v', 'new_v_ln4_g': 'new_v', 'new_v_ln4_b': 'new_v'}


def _forward(args):
    return _fwd_reference(*[args[k] for k in FWD_PARAMS])


def _output_shape():
    def fwd():
        inp = _fwd_setup_inputs(0)
        return _fwd_reference(*[inp[k] for k in FWD_PARAMS])
    out = _jax.eval_shape(fwd)
    return out.shape, out.dtype

N_MICROBATCH = 1
ADAM_LR = 0.001
ADAM_B1 = 0.9
ADAM_B2 = 0.999
ADAM_EPS = 1e-08
ADAM_WD = 0.01
ADAM_STEP = 10
PER_EXAMPLE_BATCH_AXIS = {'x': 0, 'mem': 0, 'loss_target': 0}
SHARED_INPUTS = []
_WEIGHT_DTYPES = {'ffn1_w_gate': _jnp.float32, 'ffn1_w_up': _jnp.float32, 'ffn1_w_down': _jnp.float32, 'ln1_g': _jnp.float32, 'ln1_b': _jnp.float32, 'w_in': _jnp.float32, 'hg_lb_logits': _jnp.float32, 'hg_norm_g': _jnp.float32, 'sg_ln_g': _jnp.float32, 'sg_ln_b': _jnp.float32, 'sg_w_s': _jnp.float32, 'sg_b_s': _jnp.float32, 'w_out': _jnp.float32, 'ln2_g': _jnp.float32, 'ln2_b': _jnp.float32, 'mem_ln_g': _jnp.float32, 'mem_ln_b': _jnp.float32, 'xa_w_q': _jnp.float32, 'xa_w_k': _jnp.float32, 'xa_w_v': _jnp.float32, 'xa_w_o': _jnp.float32, 'ln3_g': _jnp.float32, 'ln3_b': _jnp.float32, 'ffn2_w_gate': _jnp.float32, 'ffn2_w_up': _jnp.float32, 'ffn2_w_down': _jnp.float32, 'ln4_g': _jnp.float32, 'ln4_b': _jnp.float32}
MOMENT_SCALE = {'ffn1_w_gate': 1.794346e-02, 'ffn1_w_up': 1.744560e-02, 'ffn1_w_down': 4.866909e-02, 'ln1_g': 2.817801e+00, 'ln1_b': 4.005732e-01, 'w_in': 5.667557e-02, 'hg_lb_logits': 3.195640e-02, 'hg_norm_g': 1.030753e-01, 'sg_ln_g': 4.489833e-02, 'sg_ln_b': 3.915751e-02, 'sg_w_s': 3.918653e-02, 'sg_b_s': 5.635032e-02, 'w_out': 1.141727e-01, 'ln2_g': 3.161013e+00, 'ln2_b': 4.043257e-01, 'mem_ln_g': 8.558015e-03, 'mem_ln_b': 1.165932e-01, 'xa_w_q': 5.689117e-03, 'xa_w_k': 5.707541e-03, 'xa_w_v': 1.045471e-02, 'xa_w_o': 1.031947e-02, 'ln3_g': 3.157564e+00, 'ln3_b': 4.048513e-01, 'ffn2_w_gate': 1.635205e-02, 'ffn2_w_up': 1.598126e-02, 'ffn2_w_down': 4.457283e-02, 'ln4_g': 3.241741e+01, 'ln4_b': 2.927440e+00}


def _to_microbatches(a, axis):
    t = _jnp.moveaxis(a, axis, 0)
    t = t.reshape((N_MICROBATCH, t.shape[0] // N_MICROBATCH) + t.shape[1:])
    return _jnp.moveaxis(t, 1, axis + 1)


def setup_inputs(seed: int = 0) -> dict:
    inp = _fwd_setup_inputs(seed)
    key = _jax.random.fold_in(_jax.random.key(seed), 7919)
    shape, _ = _output_shape()
    out = dict(inp)
    out["loss_target"] = _jax.random.normal(_jax.random.fold_in(key, 0), shape, _jnp.float32)
    for i, name in enumerate(TWIN_WEIGHTS):
        w = inp[name].astype(_jnp.float32)
        if MOMENT_SCALE is None:
            s = _jnp.sqrt(_jnp.mean(_jnp.square(w)) + 1e-30)
        else:
            s = MOMENT_SCALE[name]
        km, kv = _jax.random.split(_jax.random.fold_in(key, i + 1))
        out[name] = w
        out["m_" + name] = s * _jax.random.normal(km, w.shape, _jnp.float32)
        out["v_" + name] = (s * s) * _jax.random.uniform(kv, w.shape, _jnp.float32, 0.5, 1.5)
    if N_MICROBATCH > 1:
        for name, axis in PER_EXAMPLE_BATCH_AXIS.items():
            out[name] = _to_microbatches(out[name], axis)
    return {'x': out['x'], 'mem': out['mem'], 'ffn1_w_gate': out['ffn1_w_gate'], 'ffn1_w_up': out['ffn1_w_up'], 'ffn1_w_down': out['ffn1_w_down'], 'ln1_g': out['ln1_g'], 'ln1_b': out['ln1_b'], 'w_in': out['w_in'], 'hg_lb_logits': out['hg_lb_logits'], 'hg_norm_g': out['hg_norm_g'], 'sg_ln_g': out['sg_ln_g'], 'sg_ln_b': out['sg_ln_b'], 'sg_w_s': out['sg_w_s'], 'sg_b_s': out['sg_b_s'], 'w_out': out['w_out'], 'ln2_g': out['ln2_g'], 'ln2_b': out['ln2_b'], 'mem_ln_g': out['mem_ln_g'], 'mem_ln_b': out['mem_ln_b'], 'xa_w_q': out['xa_w_q'], 'xa_w_k': out['xa_w_k'], 'xa_w_v': out['xa_w_v'], 'xa_w_o': out['xa_w_o'], 'ln3_g': out['ln3_g'], 'ln3_b': out['ln3_b'], 'ffn2_w_gate': out['ffn2_w_gate'], 'ffn2_w_up': out['ffn2_w_up'], 'ffn2_w_down': out['ffn2_w_down'], 'ln4_g': out['ln4_g'], 'ln4_b': out['ln4_b'], 'loss_target': out['loss_target'], 'm_ffn1_w_gate': out['m_ffn1_w_gate'], 'm_ffn1_w_up': out['m_ffn1_w_up'], 'm_ffn1_w_down': out['m_ffn1_w_down'], 'm_ln1_g': out['m_ln1_g'], 'm_ln1_b': out['m_ln1_b'], 'm_w_in': out['m_w_in'], 'm_hg_lb_logits': out['m_hg_lb_logits'], 'm_hg_norm_g': out['m_hg_norm_g'], 'm_sg_ln_g': out['m_sg_ln_g'], 'm_sg_ln_b': out['m_sg_ln_b'], 'm_sg_w_s': out['m_sg_w_s'], 'm_sg_b_s': out['m_sg_b_s'], 'm_w_out': out['m_w_out'], 'm_ln2_g': out['m_ln2_g'], 'm_ln2_b': out['m_ln2_b'], 'm_mem_ln_g': out['m_mem_ln_g'], 'm_mem_ln_b': out['m_mem_ln_b'], 'm_xa_w_q': out['m_xa_w_q'], 'm_xa_w_k': out['m_xa_w_k'], 'm_xa_w_v': out['m_xa_w_v'], 'm_xa_w_o': out['m_xa_w_o'], 'm_ln3_g': out['m_ln3_g'], 'm_ln3_b': out['m_ln3_b'], 'm_ffn2_w_gate': out['m_ffn2_w_gate'], 'm_ffn2_w_up': out['m_ffn2_w_up'], 'm_ffn2_w_down': out['m_ffn2_w_down'], 'm_ln4_g': out['m_ln4_g'], 'm_ln4_b': out['m_ln4_b'], 'v_ffn1_w_gate': out['v_ffn1_w_gate'], 'v_ffn1_w_up': out['v_ffn1_w_up'], 'v_ffn1_w_down': out['v_ffn1_w_down'], 'v_ln1_g': out['v_ln1_g'], 'v_ln1_b': out['v_ln1_b'], 'v_w_in': out['v_w_in'], 'v_hg_lb_logits': out['v_hg_lb_logits'], 'v_hg_norm_g': out['v_hg_norm_g'], 'v_sg_ln_g': out['v_sg_ln_g'], 'v_sg_ln_b': out['v_sg_ln_b'], 'v_sg_w_s': out['v_sg_w_s'], 'v_sg_b_s': out['v_sg_b_s'], 'v_w_out': out['v_w_out'], 'v_ln2_g': out['v_ln2_g'], 'v_ln2_b': out['v_ln2_b'], 'v_mem_ln_g': out['v_mem_ln_g'], 'v_mem_ln_b': out['v_mem_ln_b'], 'v_xa_w_q': out['v_xa_w_q'], 'v_xa_w_k': out['v_xa_w_k'], 'v_xa_w_v': out['v_xa_w_v'], 'v_xa_w_o': out['v_xa_w_o'], 'v_ln3_g': out['v_ln3_g'], 'v_ln3_b': out['v_ln3_b'], 'v_ffn2_w_gate': out['v_ffn2_w_gate'], 'v_ffn2_w_up': out['v_ffn2_w_up'], 'v_ffn2_w_down': out['v_ffn2_w_down'], 'v_ln4_g': out['v_ln4_g'], 'v_ln4_b': out['v_ln4_b']}


def _loss(weights, diff, rest, loss_target):
    with _jax.named_scope("forward"):
        args = {**rest, TWIN_DIFF_INPUT: diff, **{k: w.astype(_WEIGHT_DTYPES[k]) for k, w in weights.items()}}
        y = _forward(args)
    with _jax.named_scope("loss_head"):
        err = _jnp.square(y.astype(_jnp.float32) - loss_target)
        return 0.5 * _jnp.sum(_jnp.mean(err, axis=-1)) if err.ndim else 0.5 * err


def _adamw(w, g, m, v):
    m = ADAM_B1 * m + (1.0 - ADAM_B1) * g
    v = ADAM_B2 * v + (1.0 - ADAM_B2) * _jnp.square(g)
    m_hat = m / (1.0 - ADAM_B1 ** ADAM_STEP)
    v_hat = v / (1.0 - ADAM_B2 ** ADAM_STEP)
    delta = -ADAM_LR * (m_hat / (_jnp.sqrt(v_hat) + ADAM_EPS) + ADAM_WD * w)
    return delta, m, v


def reference(x, mem, ffn1_w_gate, ffn1_w_up, ffn1_w_down, ln1_g, ln1_b, w_in, hg_lb_logits, hg_norm_g, sg_ln_g, sg_ln_b, sg_w_s, sg_b_s, w_out, ln2_g, ln2_b, mem_ln_g, mem_ln_b, xa_w_q, xa_w_k, xa_w_v, xa_w_o, ln3_g, ln3_b, ffn2_w_gate, ffn2_w_up, ffn2_w_down, ln4_g, ln4_b, loss_target, m_ffn1_w_gate, m_ffn1_w_up, m_ffn1_w_down, m_ln1_g, m_ln1_b, m_w_in, m_hg_lb_logits, m_hg_norm_g, m_sg_ln_g, m_sg_ln_b, m_sg_w_s, m_sg_b_s, m_w_out, m_ln2_g, m_ln2_b, m_mem_ln_g, m_mem_ln_b, m_xa_w_q, m_xa_w_k, m_xa_w_v, m_xa_w_o, m_ln3_g, m_ln3_b, m_ffn2_w_gate, m_ffn2_w_up, m_ffn2_w_down, m_ln4_g, m_ln4_b, v_ffn1_w_gate, v_ffn1_w_up, v_ffn1_w_down, v_ln1_g, v_ln1_b, v_w_in, v_hg_lb_logits, v_hg_norm_g, v_sg_ln_g, v_sg_ln_b, v_sg_w_s, v_sg_b_s, v_w_out, v_ln2_g, v_ln2_b, v_mem_ln_g, v_mem_ln_b, v_xa_w_q, v_xa_w_k, v_xa_w_v, v_xa_w_o, v_ln3_g, v_ln3_b, v_ffn2_w_gate, v_ffn2_w_up, v_ffn2_w_down, v_ln4_g, v_ln4_b):
    given = dict(x=x, mem=mem, ffn1_w_gate=ffn1_w_gate, ffn1_w_up=ffn1_w_up, ffn1_w_down=ffn1_w_down, ln1_g=ln1_g, ln1_b=ln1_b, w_in=w_in, hg_lb_logits=hg_lb_logits, hg_norm_g=hg_norm_g, sg_ln_g=sg_ln_g, sg_ln_b=sg_ln_b, sg_w_s=sg_w_s, sg_b_s=sg_b_s, w_out=w_out, ln2_g=ln2_g, ln2_b=ln2_b, mem_ln_g=mem_ln_g, mem_ln_b=mem_ln_b, xa_w_q=xa_w_q, xa_w_k=xa_w_k, xa_w_v=xa_w_v, xa_w_o=xa_w_o, ln3_g=ln3_g, ln3_b=ln3_b, ffn2_w_gate=ffn2_w_gate, ffn2_w_up=ffn2_w_up, ffn2_w_down=ffn2_w_down, ln4_g=ln4_g, ln4_b=ln4_b, loss_target=loss_target, m_ffn1_w_gate=m_ffn1_w_gate, m_ffn1_w_up=m_ffn1_w_up, m_ffn1_w_down=m_ffn1_w_down, m_ln1_g=m_ln1_g, m_ln1_b=m_ln1_b, m_w_in=m_w_in, m_hg_lb_logits=m_hg_lb_logits, m_hg_norm_g=m_hg_norm_g, m_sg_ln_g=m_sg_ln_g, m_sg_ln_b=m_sg_ln_b, m_sg_w_s=m_sg_w_s, m_sg_b_s=m_sg_b_s, m_w_out=m_w_out, m_ln2_g=m_ln2_g, m_ln2_b=m_ln2_b, m_mem_ln_g=m_mem_ln_g, m_mem_ln_b=m_mem_ln_b, m_xa_w_q=m_xa_w_q, m_xa_w_k=m_xa_w_k, m_xa_w_v=m_xa_w_v, m_xa_w_o=m_xa_w_o, m_ln3_g=m_ln3_g, m_ln3_b=m_ln3_b, m_ffn2_w_gate=m_ffn2_w_gate, m_ffn2_w_up=m_ffn2_w_up, m_ffn2_w_down=m_ffn2_w_down, m_ln4_g=m_ln4_g, m_ln4_b=m_ln4_b, v_ffn1_w_gate=v_ffn1_w_gate, v_ffn1_w_up=v_ffn1_w_up, v_ffn1_w_down=v_ffn1_w_down, v_ln1_g=v_ln1_g, v_ln1_b=v_ln1_b, v_w_in=v_w_in, v_hg_lb_logits=v_hg_lb_logits, v_hg_norm_g=v_hg_norm_g, v_sg_ln_g=v_sg_ln_g, v_sg_ln_b=v_sg_ln_b, v_sg_w_s=v_sg_w_s, v_sg_b_s=v_sg_b_s, v_w_out=v_w_out, v_ln2_g=v_ln2_g, v_ln2_b=v_ln2_b, v_mem_ln_g=v_mem_ln_g, v_mem_ln_b=v_mem_ln_b, v_xa_w_q=v_xa_w_q, v_xa_w_k=v_xa_w_k, v_xa_w_v=v_xa_w_v, v_xa_w_o=v_xa_w_o, v_ln3_g=v_ln3_g, v_ln3_b=v_ln3_b, v_ffn2_w_gate=v_ffn2_w_gate, v_ffn2_w_up=v_ffn2_w_up, v_ffn2_w_down=v_ffn2_w_down, v_ln4_g=v_ln4_g, v_ln4_b=v_ln4_b)
    weights = {n: given[n] for n in TWIN_WEIGHTS}
    shared = {n: given[n] for n in SHARED_INPUTS}
    per_example = {n: given[n] for n in ['x', 'mem']}
    grad_fn = _jax.value_and_grad(_loss, argnums=(0, 1))

    def one_microbatch(ex, loss_target):
        ex = dict(ex)
        diff = ex.pop(TWIN_DIFF_INPUT)
        return grad_fn(weights, diff, {**shared, **ex}, loss_target)

    if N_MICROBATCH == 1:
        loss, (grad_w, grad_x) = one_microbatch(per_example, given["loss_target"])
    else:
        def body(carry, xs):
            loss_sum, grad_sum = carry
            l_k, (gw_k, gx_k) = one_microbatch(xs[0], xs[1])
            with _jax.named_scope("update"):
                return (loss_sum + l_k, _jax.tree.map(_jnp.add, grad_sum, gw_k)), gx_k

        init = (_jnp.zeros((), _jnp.float32), _jax.tree.map(_jnp.zeros_like, weights))
        (loss, grad_w), grad_x = _jax.lax.scan(body, init, (per_example, given["loss_target"]))
    with _jax.named_scope("update"):
        delta_w, new_m, new_v = {}, {}, {}
        for n in TWIN_WEIGHTS:
            delta_w[n], new_m[n], new_v[n] = _adamw(weights[n], grad_w[n], given["m_" + n], given["v_" + n])
    return (loss, grad_x, *[grad_w[n] for n in TWIN_WEIGHTS], *[delta_w[n] for n in TWIN_WEIGHTS],
            *[new_m[n] for n in TWIN_WEIGHTS], *[new_v[n] for n in TWIN_WEIGHTS])
```

```python
import functools

import jax
import jax.numpy as jnp
from jax import lax
from jax.experimental import pallas as pl
from jax.experimental.pallas import tpu as pltpu

F32 = jnp.float32
BF16 = jnp.bfloat16

N_DEV = 8
ALPHA = 2.0 ** 0.25
LN_EPS = 1e-5
HG_HEADS = 4
HG_DIM = 128
SG_GROUPS = 4
SG_DIM = 128
SG_CHUNK = 128
X_HEADS = 4
HG_BLOCK = 16
ADAM_LR = 0.001
ADAM_B1 = 0.9
ADAM_B2 = 0.999
ADAM_EPS = 1e-08
ADAM_WD = 0.01
ADAM_STEP = 10
VMEM_LIMIT_V7X = 48 * 1024 * 1024
SMALL_ROWS = 616
MESH_ID = pl.DeviceIdType.MESH
ANY = pl.BlockSpec(memory_space=pl.ANY)


def _params(n_axes):
    return pltpu.CompilerParams(dimension_semantics=("arbitrary",) * n_axes, vmem_limit_bytes=VMEM_LIMIT_V7X)


def _dot(a, b):
    return jnp.dot(a, b, preferred_element_type=F32)


def _dot_nt(a, b):
    return lax.dot_general(a, b, (((1,), (1,)), ((), ())), preferred_element_type=F32)


def _dot_tn(a, b):
    return lax.dot_general(a, b, (((0,), (0,)), ((), ())), preferred_element_type=F32)


def _sigmoid(x):
    return 1.0 / (1.0 + jnp.exp(-x))


def _silu_and_grad(a):
    sig = _sigmoid(a)
    return a * sig, sig * (1.0 + a * (1.0 - sig))


_GELU_C = 0.7978845608028654


def _gelu_and_grad(x):
    inner = _GELU_C * (x + 0.044715 * x * x * x)
    t = jnp.tanh(inner)
    val = 0.5 * x * (1.0 + t)
    grad = 0.5 * (1.0 + t) + 0.5 * x * (1.0 - t * t) * _GELU_C * (1.0 + 3.0 * 0.044715 * x * x)
    return val, grad


def _ln_fwd(y, g, b):
    mu = jnp.mean(y, axis=-1, keepdims=True)
    yc = y - mu
    var = jnp.mean(yc * yc, axis=-1, keepdims=True)
    rstd = lax.rsqrt(var + LN_EPS)
    xhat = yc * rstd
    return xhat * g + b, xhat, rstd


def _ln_bwd(dh, xhat, rstd, g):
    dxh = dh * g
    m1 = jnp.mean(dxh, axis=-1, keepdims=True)
    m2 = jnp.mean(dxh * xhat, axis=-1, keepdims=True)
    dy = rstd * (dxh - m1 - xhat * m2)
    dg = jnp.sum(dh * xhat, axis=0, keepdims=True)
    db = jnp.sum(dh, axis=0, keepdims=True)
    return dy, dg, db


def _split3(x):
    hi = x.astype(BF16)
    r1 = x - hi.astype(F32)
    mid = r1.astype(BF16)
    lo = (r1 - mid.astype(F32)).astype(BF16)
    return hi, mid, lo


def _mask_dot(mask, x):
    hi, mid, lo = _split3(x)
    return _dot(mask, hi) + _dot(mask, mid) + _dot(mask, lo)


def _block_masks(n):
    r = lax.broadcasted_iota(jnp.int32, (n, n), 0)
    c = lax.broadcasted_iota(jnp.int32, (n, n), 1)
    assert HG_BLOCK & (HG_BLOCK - 1) == 0
    same = (r & -HG_BLOCK) == (c & -HG_BLOCK)
    one = jnp.ones((n, n), BF16)
    zero = jnp.zeros((n, n), BF16)
    lower = jnp.where(same & (c <= r), one, zero)
    upper = jnp.where(same & (c >= r), one, zero)
    whole = jnp.where(same, one, zero)
    return lower, upper, whole


def _row_tile(t):
    return min(t, 512)


def _ffn_up(hb, wg, wu, name):
    t, d = hb.shape
    nb, _, fs = wg.shape
    tm = _row_tile(t)

    def body(h_ref, wg_ref, wu_ref, a_ref, b_ref, s_ref):
        h = h_ref[...]
        a = _dot(h, wg_ref[...])
        b = _dot(h, wu_ref[...])
        a_ref[...] = a
        b_ref[...] = b
        s_ref[...] = (a * _sigmoid(a) * b).astype(BF16)

    act = pl.BlockSpec((None, tm, fs), lambda j, i: (j, i, 0))
    wsp = pl.BlockSpec((None, d, fs), lambda j, i: (j, 0, 0))
    return pl.pallas_call(
        body,
        grid=(nb, t // tm),
        in_specs=[pl.BlockSpec((tm, d), lambda j, i: (i, 0)), wsp, wsp],
        out_specs=[act, act, act],
        out_shape=[jax.ShapeDtypeStruct((nb, t, fs), F32), jax.ShapeDtypeStruct((nb, t, fs), F32),
                   jax.ShapeDtypeStruct((nb, t, fs), BF16)],
        compiler_params=_params(2),
        name=name,
    )(hb, wg, wu)


def _mm_res_ln(lhs, w, res, g, b, coef, name):
    nk, t, kb = lhs.shape
    d = w.shape[2]
    tm = _row_tile(t)

    def body(l_ref, w_ref, r_ref, g_ref, b_ref, h_ref, hb_ref, xh_ref, rs_ref, acc):
        k = pl.program_id(1)

        @pl.when(k == 0)
        def _():
            acc[...] = jnp.zeros_like(acc)

        acc[...] += _dot(l_ref[...], w_ref[...])

        @pl.when(k == nk - 1)
        def _():
            y = ALPHA * r_ref[...] + coef * acc[...]
            h, xhat, rstd = _ln_fwd(y, g_ref[...], b_ref[...])
            h_ref[...] = h
            hb_ref[...] = h.astype(BF16)
            xh_ref[...] = xhat
            rs_ref[...] = rstd

    row = pl.BlockSpec((tm, d), lambda i, k: (i, 0))
    vec = pl.BlockSpec((1, d), lambda i, k: (0, 0))
    return pl.pallas_call(
        body,
        grid=(t // tm, nk),
        in_specs=[pl.BlockSpec((None, tm, kb), lambda i, k: (k, i, 0)),
                  pl.BlockSpec((None, kb, d), lambda i, k: (k, 0, 0)), row, vec, vec],
        out_specs=[row, row, row, pl.BlockSpec((tm, 1), lambda i, k: (i, 0))],
        out_shape=[jax.ShapeDtypeStruct((t, d), F32), jax.ShapeDtypeStruct((t, d), BF16),
                   jax.ShapeDtypeStruct((t, d), F32), jax.ShapeDtypeStruct((t, 1), F32)],
        scratch_shapes=[pltpu.VMEM((tm, d), F32)],
        compiler_params=_params(2),
        name=name,
    )(lhs, w, res, g, b)


def _mm_nn(lhs, w, name):
    t, kd = lhs.shape
    nb, _, n = w.shape
    tm = _row_tile(t)

    def body(l_ref, w_ref, o_ref):
        o_ref[...] = _dot(l_ref[...], w_ref[...])

    return pl.pallas_call(
        body,
        grid=(nb, t // tm),
        in_specs=[pl.BlockSpec((tm, kd), lambda j, i: (i, 0)), pl.BlockSpec((None, kd, n), lambda j, i: (j, 0, 0))],
        out_specs=pl.BlockSpec((tm, n), lambda j, i: (i, j)),
        out_shape=jax.ShapeDtypeStruct((t, nb * n), F32),
        compiler_params=_params(2),
        name=name,
    )(lhs, w)


def _lower_bound(lg):
    m = jnp.max(lg, axis=0, keepdims=True)
    e = jnp.exp(lg - m)
    return e[0:1, :] / jnp.sum(e, axis=0, keepdims=True)


def _forget_terms(fz, lb):
    e = jnp.exp(-jnp.abs(fz))
    r = 1.0 / (1.0 + e)
    pos = fz >= 0.0
    sig = jnp.where(pos, r, e * r)
    nsig = jnp.where(pos, e * r, r)
    f = lb + (1.0 - lb) * sig
    k = (1.0 - lb) * nsig
    return sig, nsig, f, k


def _hg_tile(t):
    return min(t, 256)


def _hgrn_fwd(proj, logits, gn):
    t = proj.shape[0]
    ct = _hg_tile(t)
    nct = t // ct
    nblk = ct // HG_BLOCK
    nh = HG_HEADS

    def body(q_ref, fz_ref, iv_ref, gg_ref, lg_ref, gn_ref, oraw_ref, oa_ref, st_ref,
             state, qt_s, kt_s, k_s, b_s, dec_s):
        c = pl.program_id(1)

        @pl.when(c == 0)
        def _():
            state[...] = jnp.zeros_like(state)

        lb = _lower_bound(lg_ref[...])
        q = q_ref[...]
        _, _, f, k = _forget_terms(fz_ref[...], lb)
        logf = jnp.log(f)
        lower, _, whole = _block_masks(ct)
        b = _mask_dot(lower, logf)
        bend = _mask_dot(whole, logf)
        qt_s[...] = (q * jnp.exp(b)).astype(BF16)
        kt_s[...] = (k * jnp.exp(bend - b)).astype(BF16)
        k_s[...] = k
        b_s[...] = b
        dec_s[...] = jnp.exp(bend)
        tidx = lax.broadcasted_iota(jnp.int32, (HG_BLOCK, HG_DIM), 0)

        def blk(i, carry):
            r0 = pl.multiple_of(i * HG_BLOCK, HG_BLOCK)
            rows = pl.ds(r0, HG_BLOCK)
            st = state[...]
            st_ref[i] = st
            v = iv_ref[rows, :]
            qq = q_ref[rows, :]
            kk = k_s[rows, :]
            bb = b_s[rows, :]
            o = _dot_nt(qt_s[rows, :], st.astype(BF16))
            for s in range(HG_BLOCK):
                e = jnp.where(tidx >= s, jnp.exp(jnp.minimum(bb - bb[s:s + 1, :], 0.0)), 0.0)
                acol = jnp.sum(qq * kk[s:s + 1, :] * e, axis=1, keepdims=True)
                o = o + acol * v[s:s + 1, :]
            oraw_ref[rows, :] = o
            state[...] = st * dec_s[pl.ds(r0, 1), :] + _dot_tn(v.astype(BF16), kt_s[rows, :])
            return carry

        lax.fori_loop(0, nblk, blk, 0)
        oraw = oraw_ref[...]
        r = lax.rsqrt(jnp.mean(oraw * oraw, axis=-1, keepdims=True) + LN_EPS)
        gg = gg_ref[...]
        oa_ref[...] = (oraw * r * gn_ref[...] * gg * _sigmoid(gg)).astype(BF16)

    def slab(off):
        return pl.BlockSpec((ct, HG_DIM), lambda h, c: (c, off + h))

    out_slab = pl.BlockSpec((ct, HG_DIM), lambda h, c: (c, h))
    return pl.pallas_call(
        body,
        grid=(nh, nct),
        in_specs=[slab(0), slab(nh), slab(2 * nh), slab(3 * nh),
                  pl.BlockSpec((None, 2, HG_DIM), lambda h, c: (h, 0, 0)),
                  pl.BlockSpec((1, HG_DIM), lambda h, c: (0, 0))],
        out_specs=[out_slab, out_slab, pl.BlockSpec((None, nblk, HG_DIM, HG_DIM), lambda h, c: (h, c, 0, 0))],
        out_shape=[jax.ShapeDtypeStruct((t, nh * HG_DIM), F32), jax.ShapeDtypeStruct((t, nh * HG_DIM), BF16),
                   jax.ShapeDtypeStruct((nh, t // HG_BLOCK, HG_DIM, HG_DIM), F32)],
        scratch_shapes=[pltpu.VMEM((HG_DIM, HG_DIM), F32), pltpu.VMEM((ct, HG_DIM), BF16),
                        pltpu.VMEM((ct, HG_DIM), BF16), pltpu.VMEM((ct, HG_DIM), F32),
                        pltpu.VMEM((ct, HG_DIM), F32), pltpu.VMEM((ct, HG_DIM), F32)],
        compiler_params=_params(2),
        name="hgrn_fwd",
    )(proj, proj, proj, proj, logits, gn)


def _sg_tile(t):
    return min(t, 512)


def _sgu_chunk_fwd(u, v, ln_g, ln_b, wm, bs):
    ua, dua = _gelu_and_grad(u)
    va, dva = _gelu_and_grad(v)
    vn, xhat, rstd = _ln_fwd(va, ln_g, ln_b)
    s = _dot(wm, vn.astype(BF16)) + bs
    return ua, dua, dva, vn, xhat, rstd, s


def _tril_weight(w_ref):
    n = SG_CHUNK
    r = lax.broadcasted_iota(jnp.int32, (n, n), 0)
    c = lax.broadcasted_iota(jnp.int32, (n, n), 1)
    return jnp.where(c <= r, w_ref[...], 0.0)


def _sgu_fwd(proj, ln_g, ln_b, w_s, b_col):
    t = proj.shape[0]
    ct = _sg_tile(t)
    ng = SG_GROUPS
    off_u = 4 * HG_HEADS
    off_v = off_u + ng

    def body(u_ref, v_ref, g_ref, b_ref, w_ref, bs_ref, o_ref):
        wm = _tril_weight(w_ref).astype(BF16)
        for n in range(ct // SG_CHUNK):
            rows = slice(n * SG_CHUNK, (n + 1) * SG_CHUNK)
            ua, _, _, _, _, _, s = _sgu_chunk_fwd(u_ref[rows, :], v_ref[rows, :], g_ref[...], b_ref[...], wm, bs_ref[...])
            o_ref[rows, :] = (ua * s).astype(BF16)

    vec = pl.BlockSpec((None, 1, SG_DIM), lambda g, c: (g, 0, 0))
    return pl.pallas_call(
        body,
        grid=(ng, t // ct),
        in_specs=[pl.BlockSpec((ct, SG_DIM), lambda g, c: (c, off_u + g)),
                  pl.BlockSpec((ct, SG_DIM), lambda g, c: (c, off_v + g)), vec, vec,
                  pl.BlockSpec((None, SG_CHUNK, SG_CHUNK), lambda g, c: (g, 0, 0)),
                  pl.BlockSpec((None, SG_CHUNK, 1), lambda g, c: (g, 0, 0))],
        out_specs=pl.BlockSpec((ct, SG_DIM), lambda g, c: (c, g)),
        out_shape=jax.ShapeDtypeStruct((t, ng * SG_DIM), BF16),
        compiler_params=_params(2),
        name="sgu_fwd",
    )(proj, proj, ln_g, ln_b, w_s, b_col)


def _mem_kv(mem, g, b, wk, wv):
    m_len, d = mem.shape

    def body(m_ref, g_ref, b_ref, wk_ref, wv_ref, mb_ref, xh_ref, rs_ref, k_ref, v_ref):
        m, xhat, rstd = _ln_fwd(m_ref[...], g_ref[...], b_ref[...])
        mb = m.astype(BF16)
        mb_ref[...] = mb
        xh_ref[...] = xhat
        rs_ref[...] = rstd
        k_ref[...] = _dot(mb, wk_ref[...]).astype(BF16)
        v_ref[...] = _dot(mb, wv_ref[...]).astype(BF16)

    return pl.pallas_call(
        body,
        out_shape=[jax.ShapeDtypeStruct((m_len, d), BF16), jax.ShapeDtypeStruct((m_len, d), F32),
                   jax.ShapeDtypeStruct((m_len, 1), F32), jax.ShapeDtypeStruct((m_len, d), BF16),
                   jax.ShapeDtypeStruct((m_len, d), BF16)],
        compiler_params=pltpu.CompilerParams(vmem_limit_bytes=VMEM_LIMIT_V7X),
        name="mem_kv",
    )(mem, g, b, wk, wv)


def _softmax_rows(s):
    m = jnp.max(s, axis=-1, keepdims=True)
    p = jnp.exp(s - m)
    return p / jnp.sum(p, axis=-1, keepdims=True)


def _attn_fwd(hb, wq, kb, vb):
    t, d = hb.shape
    tm = _row_tile(t)
    dh = d // X_HEADS
    scale = dh ** -0.5

    def body(h_ref, wq_ref, k_ref, v_ref, q_ref, o_ref):
        q = _dot(h_ref[...], wq_ref[...]).astype(BF16)
        q_ref[...] = q
        for hd in range(X_HEADS):
            sl = slice(hd * dh, (hd + 1) * dh)
            p = _softmax_rows(_dot_nt(q[:, sl], k_ref[:, sl]) * scale)
            o_ref[:, sl] = _dot(p.astype(BF16), v_ref[:, sl]).astype(BF16)

    row = pl.BlockSpec((tm, d), lambda i: (i, 0))
    full = lambda a: pl.BlockSpec(a.shape, lambda i: (0, 0))
    return pl.pallas_call(
        body,
        grid=(t // tm,),
        in_specs=[row, full(wq), full(kb), full(vb)],
        out_specs=[row, row],
        out_shape=[jax.ShapeDtypeStruct((t, d), BF16), jax.ShapeDtypeStruct((t, d), BF16)],
        compiler_params=_params(1),
        name="attn_fwd",
    )(hb, wq, kb, vb)


def _loss_ln_bwd(h, xhat, rstd, g, target):
    t, d = h.shape
    tm = _row_tile(t)
    nt = t // tm

    def body(h_ref, xh_ref, rs_ref, g_ref, t_ref, loss_ref, dy_ref, dyb_ref, dg_ref, db_ref, lacc):
        i = pl.program_id(0)

        @pl.when(i == 0)
        def _():
            lacc[...] = jnp.zeros_like(lacc)
            dg_ref[...] = jnp.zeros_like(dg_ref)
            db_ref[...] = jnp.zeros_like(db_ref)

        err = h_ref[...] - t_ref[...]
        lacc[...] += jnp.sum(err * err, axis=0, keepdims=True)
        dy, dg, db = _ln_bwd(err * (1.0 / d), xh_ref[...], rs_ref[...], g_ref[...])
        dy_ref[...] = dy
        dyb_ref[...] = dy.astype(BF16)
        dg_ref[...] += dg
        db_ref[...] += db

        @pl.when(i == nt - 1)
        def _():
            loss_ref[...] = jnp.zeros_like(loss_ref) + jnp.sum(lacc[...], axis=1, keepdims=True) * (0.5 / d)

    row = pl.BlockSpec((tm, d), lambda i: (i, 0))
    vec = pl.BlockSpec((1, d), lambda i: (0, 0))
    return pl.pallas_call(
        body,
        grid=(nt,),
        in_specs=[row, row, pl.BlockSpec((tm, 1), lambda i: (i, 0)), vec, row],
        out_specs=[pl.BlockSpec((1, 128), lambda i: (0, 0)), row, row, vec, vec],
        out_shape=[jax.ShapeDtypeStruct((1, 128), F32), jax.ShapeDtypeStruct((t, d), F32),
                   jax.ShapeDtypeStruct((t, d), BF16), jax.ShapeDtypeStruct((1, d), F32),
                   jax.ShapeDtypeStruct((1, d), F32)],
        scratch_shapes=[pltpu.VMEM((1, d), F32)],
        compiler_params=_params(1),
        name="loss_ln_bwd",
    )(h, xhat, rstd, g, target)


def _ffn_bwd_act(dyb, wd, a, b, coef, name):
    t, d = dyb.shape
    nb, fs, _ = wd.shape
    tm = _row_tile(t)

    def body(dy_ref, wd_ref, a_ref, b_ref, da_ref, db_ref):
        ds = _dot_nt(dy_ref[...], wd_ref[...]) * coef
        silu, dsilu = _silu_and_grad(a_ref[...])
        da_ref[...] = (ds * b_ref[...] * dsilu).astype(BF16)
        db_ref[...] = (ds * silu).astype(BF16)

    act = pl.BlockSpec((None, tm, fs), lambda j, i: (j, i, 0))
    return pl.pallas_call(
        body,
        grid=(nb, t // tm),
        in_specs=[pl.BlockSpec((tm, d), lambda j, i: (i, 0)), pl.BlockSpec((None, fs, d), lambda j, i: (j, 0, 0)),
                  act, act],
        out_specs=[act, act],
        out_shape=[jax.ShapeDtypeStruct((nb, t, fs), BF16), jax.ShapeDtypeStruct((nb, t, fs), BF16)],
        compiler_params=_params(2),
        name=name,
    )(dyb, wd, a, b)


def _mm_tn(a, b, a_mode, b_mode, nj, m, n, name, scale=1.0):
    t = a.shape[-2]
    tt = _row_tile(t)
    nt = t // tt

    def spec(mode, w):
        if mode == "lead":
            return pl.BlockSpec((None, tt, w), lambda j, k: (j, k, 0))
        if mode == "shared":
            return pl.BlockSpec((tt, w), lambda j, k: (k, 0))
        return pl.BlockSpec((tt, w), lambda j, k: (k, j))

    def body(a_ref, b_ref, o_ref, acc):
        k = pl.program_id(1)

        @pl.when(k == 0)
        def _():
            acc[...] = jnp.zeros_like(acc)

        acc[...] += _dot_tn(a_ref[...], b_ref[...])

        @pl.when(k == nt - 1)
        def _():
            o_ref[...] = (acc[...] * scale).astype(BF16)

    return pl.pallas_call(
        body,
        grid=(nj, nt),
        in_specs=[spec(a_mode, m), spec(b_mode, n)],
        out_specs=pl.BlockSpec((None, m, n), lambda j, k: (j, 0, 0)),
        out_shape=jax.ShapeDtypeStruct((nj, m, n), BF16),
        scratch_shapes=[pltpu.VMEM((m, n), F32)],
        compiler_params=_params(2),
        name=name,
    )(a, b)


def _mm_nt(lhs, w, name):
    t, d = lhs.shape
    nb, kb, _ = w.shape
    tm = _row_tile(t)

    def body(l_ref, w_ref, o_ref):
        o_ref[...] = _dot_nt(l_ref[...], w_ref[...])

    return pl.pallas_call(
        body,
        grid=(nb, t // tm),
        in_specs=[pl.BlockSpec((tm, d), lambda j, i: (i, 0)), pl.BlockSpec((None, kb, d), lambda j, i: (j, 0, 0))],
        out_specs=pl.BlockSpec((tm, kb), lambda j, i: (i, j)),
        out_shape=jax.ShapeDtypeStruct((t, nb * kb), F32),
        compiler_params=_params(2),
        name=name,
    )(lhs, w)


def _dx_ln(dy, pairs, ln, name):
    t, d = dy.shape
    tm = _row_tile(t)
    nt = t // tm
    nk = pairs[0][1].shape[0]
    npair = len(pairs)

    def body(*refs):
        dy_ref = refs[0]
        pr = refs[1:1 + 2 * npair]
        pos = 1 + 2 * npair
        if ln is not None:
            xh_ref, rs_ref, g_ref = refs[pos:pos + 3]
            pos += 3
            dyo_ref, dyb_ref, dg_ref, db_ref = refs[pos:pos + 4]
            acc = refs[pos + 4]
        else:
            dh_ref = refs[pos]
            acc = refs[pos + 1]
        i = pl.program_id(0)
        k = pl.program_id(1)

        @pl.when(k == 0)
        def _():
            acc[...] = ALPHA * dy_ref[...]

        for p in range(npair):
            acc[...] += _dot_nt(pr[2 * p][...], pr[2 * p + 1][...])

        if ln is not None:
            @pl.when((i == 0) & (k == 0))
            def _():
                dg_ref[...] = jnp.zeros_like(dg_ref)
                db_ref[...] = jnp.zeros_like(db_ref)

            @pl.when(k == nk - 1)
            def _():
                dyp, dg, db = _ln_bwd(acc[...], xh_ref[...], rs_ref[...], g_ref[...])
                dyo_ref[...] = dyp
                dyb_ref[...] = dyp.astype(BF16)
                dg_ref[...] += dg
                db_ref[...] += db
        else:
            @pl.when(k == nk - 1)
            def _():
                dh_ref[...] = acc[...]

    row = pl.BlockSpec((tm, d), lambda i, k: (i, 0))
    vec = pl.BlockSpec((1, d), lambda i, k: (0, 0))
    in_specs = [row]
    args = [dy]
    for lhs, w, mode in pairs:
        kb = w.shape[2]
        if mode == "lead":
            in_specs.append(pl.BlockSpec((None, tm, kb), lambda i, k: (k, i, 0)))
        else:
            in_specs.append(pl.BlockSpec((tm, kb), lambda i, k: (i, k)))
        in_specs.append(pl.BlockSpec((None, d, kb), lambda i, k: (k, 0, 0)))
        args += [lhs, w]
    if ln is not None:
        in_specs += [row, pl.BlockSpec((tm, 1), lambda i, k: (i, 0)), vec]
        args += list(ln)
        out_specs = [row, row, vec, vec]
        out_shape = [jax.ShapeDtypeStruct((t, d), F32), jax.ShapeDtypeStruct((t, d), BF16),
                     jax.ShapeDtypeStruct((1, d), F32), jax.ShapeDtypeStruct((1, d), F32)]
    else:
        out_specs = row
        out_shape = jax.ShapeDtypeStruct((t, d), F32)
    return pl.pallas_call(
        body,
        grid=(nt, nk),
        in_specs=in_specs,
        out_specs=out_specs,
        out_shape=out_shape,
        scratch_shapes=[pltpu.VMEM((tm, d), F32)],
        compiler_params=_params(2),
        name=name,
    )(*args)


def _hgrn_bwd(proj, oraw, dmix, states, logits, gn):
    t = proj.shape[0]
    ct = _hg_tile(t)
    nct = t // ct
    nblk = ct // HG_BLOCK
    nh = HG_HEADS

    def body(q_ref, fz_ref, iv_ref, gg_ref, or_ref, do_ref, st_ref, lg_ref, gn_ref,
             dq_ref, dfz_ref, div_ref, dgg_ref, dlg_ref, dgn_ref,
             dstate, qt_s, kt_s, k_s, b_s, eb_s, ekb_s, dec_s, dor_s, dbl_s, gr_s, dk_s, dlb_acc):
        c = pl.program_id(1)

        @pl.when(c == 0)
        def _():
            dstate[...] = jnp.zeros_like(dstate)
            dlb_acc[...] = jnp.zeros_like(dlb_acc)
            dgn_ref[...] = jnp.zeros_like(dgn_ref)

        lb = _lower_bound(lg_ref[...])
        q = q_ref[...]
        sig, nsig, f, k = _forget_terms(fz_ref[...], lb)
        logf = jnp.log(f)
        lower, upper, whole = _block_masks(ct)
        b = _mask_dot(lower, logf)
        bend = _mask_dot(whole, logf)
        eb = jnp.exp(b)
        ekb = jnp.exp(bend - b)
        qt_s[...] = (q * eb).astype(BF16)
        kt_s[...] = (k * ekb).astype(BF16)
        k_s[...] = k
        b_s[...] = b
        eb_s[...] = eb
        ekb_s[...] = ekb
        dec_s[...] = jnp.exp(bend)
        oraw = or_ref[...]
        r = lax.rsqrt(jnp.mean(oraw * oraw, axis=-1, keepdims=True) + LN_EPS)
        on = oraw * r
        gg = gg_ref[...]
        silu, dsilu = _silu_and_grad(gg)
        doa = do_ref[...]
        gnv = gn_ref[...]
        dgg_ref[...] = (doa * on * gnv * dsilu).astype(BF16)
        dyn = doa * silu
        dgn_ref[...] += jnp.sum(dyn * on, axis=0, keepdims=True)
        don = dyn * gnv
        dor_s[...] = r * (don - on * jnp.mean(don * on, axis=-1, keepdims=True))
        tidx = lax.broadcasted_iota(jnp.int32, (HG_BLOCK, HG_DIM), 0)

        def blk(ii, carry):
            i = nblk - 1 - ii
            r0 = pl.multiple_of(i * HG_BLOCK, HG_BLOCK)
            rows = pl.ds(r0, HG_BLOCK)
            st = st_ref[i]
            dst = dstate[...]
            dstb = dst.astype(BF16)
            do = dor_s[rows, :]
            dob = do.astype(BF16)
            v = iv_ref[rows, :]
            vb = v.astype(BF16)
            qq = q_ref[rows, :]
            kk = k_s[rows, :]
            bb = b_s[rows, :]
            qt = qt_s[rows, :]
            kt = kt_s[rows, :]
            dec = dec_s[pl.ds(r0, 1), :]
            dkt = _dot(vb, dstb)
            dq = _dot(dob, st.astype(BF16)) * eb_s[rows, :]
            dk = dkt * ekb_s[rows, :]
            dv = _dot_nt(kt, dstb)
            gend = jnp.sum(kk * dk, axis=0, keepdims=True) + dec * jnp.sum(dst * st, axis=0, keepdims=True)
            for s in range(HG_BLOCK):
                ks = kk[s:s + 1, :]
                e = jnp.where(tidx >= s, jnp.exp(jnp.minimum(bb - bb[s:s + 1, :], 0.0)), 0.0)
                qe = qq * e
                acol = jnp.sum(qe * ks, axis=1, keepdims=True)
                dacol = jnp.sum(do * v[s:s + 1, :], axis=1, keepdims=True)
                dq = dq + dacol * (ks * e)
                dk_row = jnp.sum(dacol * qe, axis=0, keepdims=True)
                dv_row = jnp.sum(acol * do, axis=0, keepdims=True)
                dk = dk + jnp.where(tidx == s, dk_row, 0.0)
                dv = dv + jnp.where(tidx == s, dv_row, 0.0)
            dq_ref[rows, :] = dq.astype(BF16)
            div_ref[rows, :] = dv.astype(BF16)
            dk_s[rows, :] = dk
            dbl_s[rows, :] = qq * dq - kk * dk
            gr_s[rows, :] = jnp.zeros((HG_BLOCK, HG_DIM), F32) + gend
            dstate[...] = dst * dec + _dot_tn(dob, qt)
            return carry

        lax.fori_loop(0, nblk, blk, 0)
        dlogf = _mask_dot(upper, dbl_s[...]) + gr_s[...]
        dk = dk_s[...]
        dfz_ref[...] = ((dlogf / f - dk) * ((1.0 - lb) * sig * nsig)).astype(BF16)
        dlb_acc[...] += jnp.sum((dlogf / f - dk) * nsig, axis=0, keepdims=True)

        @pl.when(c == nct - 1)
        def _():
            dl0 = dlb_acc[...] * lb * (1.0 - lb)
            layer = lax.broadcasted_iota(jnp.int32, (2, HG_DIM), 0)
            dlg_ref[...] = jnp.where(layer == 0, dl0, -dl0)

    def slab(off):
        return pl.BlockSpec((ct, HG_DIM), lambda h, c: (nct - 1 - c, off + h))

    out_slab = pl.BlockSpec((ct, HG_DIM), lambda h, c: (nct - 1 - c, h))
    tile_f32 = pltpu.VMEM((ct, HG_DIM), F32)
    tile_b16 = pltpu.VMEM((ct, HG_DIM), BF16)
    slab_shape = jax.ShapeDtypeStruct((t, nh * HG_DIM), BF16)
    return pl.pallas_call(
        body,
        grid=(nh, nct),
        in_specs=[slab(0), slab(nh), slab(2 * nh), slab(3 * nh), slab(0), slab(0),
                  pl.BlockSpec((None, nblk, HG_DIM, HG_DIM), lambda h, c: (h, nct - 1 - c, 0, 0)),
                  pl.BlockSpec((None, 2, HG_DIM), lambda h, c: (h, 0, 0)),
                  pl.BlockSpec((1, HG_DIM), lambda h, c: (0, 0))],
        out_specs=[out_slab, out_slab, out_slab, out_slab,
                   pl.BlockSpec((None, 2, HG_DIM), lambda h, c: (h, 0, 0)),
                   pl.BlockSpec((None, 1, HG_DIM), lambda h, c: (h, 0, 0))],
        out_shape=[slab_shape, slab_shape, slab_shape, slab_shape,
                   jax.ShapeDtypeStruct((nh, 2, HG_DIM), F32), jax.ShapeDtypeStruct((nh, 1, HG_DIM), F32)],
        scratch_shapes=[pltpu.VMEM((HG_DIM, HG_DIM), F32), tile_b16, tile_b16, tile_f32, tile_f32, tile_f32, tile_f32,
                        tile_f32, tile_f32, tile_f32, tile_f32, tile_f32, pltpu.VMEM((1, HG_DIM), F32)],
        compiler_params=_params(2),
        name="hgrn_bwd",
    )(proj, proj, proj, proj, oraw, dmix, states, logits, gn)


def _sgu_bwd(proj, dmix, ln_g, ln_b, w_s, w_t, b_col):
    t = proj.shape[0]
    ct = _sg_tile(t)
    nct = t // ct
    ng = SG_GROUPS
    off_u = 4 * HG_HEADS
    off_v = off_u + ng
    n = SG_CHUNK

    def body(u_ref, v_ref, do_ref, g_ref, b_ref, w_ref, wt_ref, bs_ref, du_ref, dv_ref, dg_ref, db_ref, dw_ref, dbs_ref):
        c = pl.program_id(1)

        @pl.when(c == 0)
        def _():
            dg_ref[...] = jnp.zeros_like(dg_ref)
            db_ref[...] = jnp.zeros_like(db_ref)
            dw_ref[...] = jnp.zeros_like(dw_ref)
            dbs_ref[...] = jnp.zeros_like(dbs_ref)

        r = lax.broadcasted_iota(jnp.int32, (n, n), 0)
        cc = lax.broadcasted_iota(jnp.int32, (n, n), 1)
        wm = jnp.where(cc <= r, w_ref[...], 0.0).astype(BF16)
        wmt = jnp.where(r <= cc, wt_ref[...], 0.0).astype(BF16)
        for ci in range(ct // n):
            rows = slice(ci * n, (ci + 1) * n)
            ua, dua, dva, vn, xhat, rstd, s = _sgu_chunk_fwd(u_ref[rows, :], v_ref[rows, :], g_ref[...], b_ref[...],
                                                             wm, bs_ref[...])
            do = do_ref[rows, :]
            du_ref[rows, :] = (do * s * dua).astype(BF16)
            ds = do * ua
            dsb = ds.astype(BF16)
            dbs_ref[...] += jnp.sum(ds, axis=1, keepdims=True)
            dw_ref[...] += _dot_nt(dsb, vn.astype(BF16))
            dvn = _dot(wmt, dsb)
            dva_in, dg, db = _ln_bwd(dvn, xhat, rstd, g_ref[...])
            dg_ref[...] += dg
            db_ref[...] += db
            dv_ref[rows, :] = (dva_in * dva).astype(BF16)

        @pl.when(c == nct - 1)
        def _():
            dw_ref[...] = jnp.where(cc <= r, dw_ref[...], 0.0)

    vec = pl.BlockSpec((None, 1, SG_DIM), lambda g, c: (g, 0, 0))
    mat = pl.BlockSpec((None, n, n), lambda g, c: (g, 0, 0))
    col = pl.BlockSpec((None, n, 1), lambda g, c: (g, 0, 0))
    out_slab = pl.BlockSpec((ct, SG_DIM), lambda g, c: (c, g))
    return pl.pallas_call(
        body,
        grid=(ng, nct),
        in_specs=[pl.BlockSpec((ct, SG_DIM), lambda g, c: (c, off_u + g)),
                  pl.BlockSpec((ct, SG_DIM), lambda g, c: (c, off_v + g)),
                  pl.BlockSpec((ct, SG_DIM), lambda g, c: (c, ng + g)), vec, vec, mat, mat, col],
        out_specs=[out_slab, out_slab, vec, vec, mat, col],
        out_shape=[jax.ShapeDtypeStruct((t, ng * SG_DIM), BF16), jax.ShapeDtypeStruct((t, ng * SG_DIM), BF16),
                   jax.ShapeDtypeStruct((ng, 1, SG_DIM), F32), jax.ShapeDtypeStruct((ng, 1, SG_DIM), F32),
                   jax.ShapeDtypeStruct((ng, n, n), F32), jax.ShapeDtypeStruct((ng, n, 1), F32)],
        compiler_params=_params(2),
        name="sgu_bwd",
    )(proj, proj, dmix, ln_g, ln_b, w_s, w_t, b_col)


def _attn_bwd(dyb, wo, qb, kb, vb):
    t, d = dyb.shape
    m_len = kb.shape[0]
    tm = _row_tile(t)
    dh = d // X_HEADS
    scale = dh ** -0.5

    def body(dy_ref, wo_ref, q_ref, k_ref, v_ref, dq_ref, dk_ref, dv_ref):
        i = pl.program_id(0)

        @pl.when(i == 0)
        def _():
            dk_ref[...] = jnp.zeros_like(dk_ref)
            dv_ref[...] = jnp.zeros_like(dv_ref)

        do = _dot_nt(dy_ref[...], wo_ref[...]).astype(BF16)
        for hd in range(X_HEADS):
            sl = slice(hd * dh, (hd + 1) * dh)
            qh = q_ref[:, sl]
            p = _softmax_rows(_dot_nt(qh, k_ref[:, sl]) * scale)
            doh = do[:, sl]
            dp = _dot_nt(doh, v_ref[:, sl])
            ds = (p * (dp - jnp.sum(dp * p, axis=-1, keepdims=True)) * scale).astype(BF16)
            dq_ref[:, sl] = _dot(ds, k_ref[:, sl]).astype(BF16)
            dk_ref[:, sl] += _dot_tn(ds, qh)
            dv_ref[:, sl] += _dot_tn(p.astype(BF16), doh)

    row = pl.BlockSpec((tm, d), lambda i: (i, 0))
    full = lambda a: pl.BlockSpec(a.shape, lambda i: (0, 0))
    kv = pl.BlockSpec((m_len, d), lambda i: (0, 0))
    return pl.pallas_call(
        body,
        grid=(t // tm,),
        in_specs=[row, full(wo), row, full(kb), full(vb)],
        out_specs=[row, kv, kv],
        out_shape=[jax.ShapeDtypeStruct((t, d), BF16), jax.ShapeDtypeStruct((m_len, d), F32),
                   jax.ShapeDtypeStruct((m_len, d), F32)],
        compiler_params=_params(1),
        name="attn_bwd",
    )(dyb, wo, qb, kb, vb)


def _mem_bwd(dk, dv, mb, xhat, rstd, g, wk, wv):
    m_len, d = dk.shape

    def body(dk_ref, dv_ref, mb_ref, xh_ref, rs_ref, g_ref, wk_ref, wv_ref, gwk_ref, gwv_ref, dg_ref, db_ref):
        dkb = dk_ref[...].astype(BF16)
        dvb = dv_ref[...].astype(BF16)
        mb_v = mb_ref[...]
        gwk_ref[...] = _dot_tn(mb_v, dkb).astype(BF16)
        gwv_ref[...] = _dot_tn(mb_v, dvb).astype(BF16)
        dm = _dot_nt(dkb, wk_ref[...]) + _dot_nt(dvb, wv_ref[...])
        _, dg, db = _ln_bwd(dm, xh_ref[...], rs_ref[...], g_ref[...])
        dg_ref[...] = dg
        db_ref[...] = db

    return pl.pallas_call(
        body,
        out_shape=[jax.ShapeDtypeStruct((d, d), BF16), jax.ShapeDtypeStruct((d, d), BF16),
                   jax.ShapeDtypeStruct((1, d), F32), jax.ShapeDtypeStruct((1, d), F32)],
        compiler_params=pltpu.CompilerParams(vmem_limit_bytes=VMEM_LIMIT_V7X),
        name="mem_bwd",
    )(dk, dv, mb, xhat, rstd, g, wk, wv)


def _adamw(w, g, m, v):
    m = ADAM_B1 * m + (1.0 - ADAM_B1) * g
    v = ADAM_B2 * v + (1.0 - ADAM_B2) * (g * g)
    m_hat = m / (1.0 - ADAM_B1 ** ADAM_STEP)
    v_hat = v / (1.0 - ADAM_B2 ** ADAM_STEP)
    delta = -ADAM_LR * (m_hat / (jnp.sqrt(v_hat) + ADAM_EPS) + ADAM_WD * w)
    return delta, m, v


def _adam_rows(rows):
    for cand in (256, 176, 128, 88, 64, 32, 16, 8):
        if rows % cand == 0:
            return cand
    return rows


def _adam_sharded(parts, w, m, v, name):
    rows, cols = w.shape
    tr = _adam_rows(rows)
    dt = parts.dtype

    def body(p_ref, w_ref, m_ref, v_ref, g_ref, d_ref, nm_ref, nv_ref):
        g = p_ref[0].astype(F32)
        for s in range(1, N_DEV):
            g = g + p_ref[s].astype(F32)
        delta, nm, nv = _adamw(w_ref[...], g, m_ref[...], v_ref[...])
        g_ref[...] = g
        d_ref[...] = delta
        nm_ref[...] = nm
        nv_ref[...] = nv

    blk = pl.BlockSpec((tr, cols), lambda i: (i, 0))
    shp = jax.ShapeDtypeStruct((rows, cols), F32)
    del dt
    return pl.pallas_call(
        body,
        grid=(rows // tr,),
        in_specs=[pl.BlockSpec((N_DEV, tr, cols), lambda i: (0, i, 0)), blk, blk, blk],
        out_specs=[blk, blk, blk, blk],
        out_shape=[shp, shp, shp, shp],
        compiler_params=_params(1),
        name=name,
    )(parts, w, m, v)


def _mesh_pos():
    return lax.axis_index("x"), lax.axis_index("y"), lax.axis_index("c")


def _all_gather(shards, name):
    n = len(shards)

    def body(*refs):
        ins = refs[:n]
        outs = refs[n:2 * n]
        send_sems, recv_sems, local_sems = refs[2 * n:]
        x, y, c = _mesh_pos()
        me = (x, y, c)
        sibling = (x, y, 1 - c)
        chips = [(1 - x, y), (x, 1 - y), (1 - x, 1 - y)]

        def slot(w, pos):
            return outs[w].at[4 * pos[0] + 2 * pos[1] + pos[2]]

        def copy(w, k, block, to, src=None):
            return pltpu.make_async_remote_copy(
                src_ref=slot(w, block) if src is None else src, dst_ref=slot(w, block),
                send_sem=send_sems.at[w, k], recv_sem=recv_sems.at[w, k], device_id=to, device_id_type=MESH_ID)

        mine = [pltpu.make_async_copy(ins[w], slot(w, me), local_sems.at[w]) for w in range(n)]
        first = []
        for w in range(n):
            mine[w].start()
            first.append(copy(w, 0, me, sibling, src=ins[w]))
            first += [copy(w, 1 + j, me, (*chip, c), src=ins[w]) for j, chip in enumerate(chips)]
        for cp in first:
            cp.start()
        passed = []
        for w in range(n):
            for j, chip in enumerate(chips):
                copy(w, 1 + j, (*chip, c), me).wait_recv()
                fwd = copy(w, 4 + j, (*chip, c), sibling)
                fwd.start()
                passed.append(fwd)
        for w in range(n):
            copy(w, 0, sibling, me).wait_recv()
            for j, chip in enumerate(chips):
                copy(w, 4 + j, (*chip, 1 - c), me).wait_recv()
        for cp in first + passed:
            cp.wait_send()
        for cp in mine:
            cp.wait()

    return pl.pallas_call(
        body,
        in_specs=[ANY] * n,
        out_specs=[ANY] * n,
        out_shape=[jax.ShapeDtypeStruct((N_DEV,) + s.shape, s.dtype) for s in shards],
        scratch_shapes=[pltpu.SemaphoreType.DMA((n, 7)), pltpu.SemaphoreType.DMA((n, 7)),
                        pltpu.SemaphoreType.DMA((n,))],
        name=name,
    )(*shards)


def _exchange_blocks(parts, name):
    n = len(parts)

    def body(*refs):
        ins = refs[:n]
        outs = refs[n:2 * n]
        send_sems, recv_sems, local_sems = refs[2 * n:]
        x, y, c = _mesh_pos()
        me = 4 * x + 2 * y + c
        copies = []
        local = []
        for w in range(n):
            cp = pltpu.make_async_copy(ins[w].at[me], outs[w].at[me], local_sems.at[w])
            cp.start()
            local.append(cp)
            for k in range(1, N_DEV):
                px, py, pc = x ^ (k >> 2), y ^ ((k >> 1) & 1), c ^ (k & 1)
                peer = 4 * px + 2 * py + pc
                cp = pltpu.make_async_remote_copy(
                    src_ref=ins[w].at[peer], dst_ref=outs[w].at[me],
                    send_sem=send_sems.at[w, k - 1], recv_sem=recv_sems.at[w, k - 1],
                    device_id=(px, py, pc), device_id_type=MESH_ID)
                cp.start()
                copies.append(cp)
        for cp in copies:
            cp.wait()
        for cp in local:
            cp.wait()

    return pl.pallas_call(
        body,
        in_specs=[ANY] * n,
        out_specs=[ANY] * n,
        out_shape=[jax.ShapeDtypeStruct(p.shape, p.dtype) for p in parts],
        scratch_shapes=[pltpu.SemaphoreType.DMA((n, 7)), pltpu.SemaphoreType.DMA((n, 7)),
                        pltpu.SemaphoreType.DMA((n,))],
        name=name,
    )(*parts)


_SMALL_NAMES = ("ln1_g", "ln1_b", "hg_lb_logits", "hg_norm_g", "sg_ln_g", "sg_ln_b", "sg_w_s", "sg_b_s",
                "ln2_g", "ln2_b", "mem_ln_g", "mem_ln_b", "ln3_g", "ln3_b", "ln4_g", "ln4_b")


def _pack_small(tree):
    flat = jnp.concatenate([tree[k].reshape(-1).astype(F32) for k in _SMALL_NAMES])
    rows = flat.shape[0] // 128
    return jnp.pad(flat.reshape(rows, 128), ((0, SMALL_ROWS - rows), (0, 0)))


def _unpack_small(packed, like):
    flat = packed.reshape(-1)
    out = {}
    pos = 0
    for k in _SMALL_NAMES:
        size = like[k].size
        out[k] = flat[pos:pos + size].reshape(like[k].shape)
        pos += size
    return out


def _adam_small(gathered, w, m, v):
    def body(p_ref, w_ref, m_ref, v_ref, g_ref, d_ref, nm_ref, nv_ref):
        g = p_ref[0]
        for s in range(1, N_DEV):
            g = g + p_ref[s]
        delta, nm, nv = _adamw(w_ref[...], g, m_ref[...], v_ref[...])
        g_ref[...] = g
        d_ref[...] = delta
        nm_ref[...] = nm
        nv_ref[...] = nv

    shp = jax.ShapeDtypeStruct(w.shape, F32)
    return pl.pallas_call(body, out_shape=[shp, shp, shp, shp], name="adam_small")(gathered, w, m, v)


def _local_step(x, mem, target, wts, small):
    t, d = x.shape
    row = lambda a: a.reshape(1, -1)
    ln = {k: row(small[k]) for k in ("ln1_g", "ln1_b", "ln2_g", "ln2_b", "ln3_g", "ln3_b", "ln4_g", "ln4_b",
                                      "mem_ln_g", "mem_ln_b", "hg_norm_g")}
    logits = jnp.swapaxes(small["hg_lb_logits"], 0, 1)
    sg_g = small["sg_ln_g"].reshape(SG_GROUPS, 1, SG_DIM)
    sg_b = small["sg_ln_b"].reshape(SG_GROUPS, 1, SG_DIM)
    sg_w = small["sg_w_s"].reshape(SG_GROUPS, SG_CHUNK, SG_CHUNK)
    sg_wt = jnp.swapaxes(sg_w, 1, 2)
    sg_bs = small["sg_b_s"].reshape(SG_GROUPS, SG_CHUNK, 1)

    w_out = wts["w_out"].reshape(2, d // 2, d)
    wq = wts["xa_w_q"].reshape(d, d)
    wk = wts["xa_w_k"].reshape(d, d)
    wv = wts["xa_w_v"].reshape(d, d)
    wo = wts["xa_w_o"].reshape(d, d)

    xb = x.astype(BF16)
    a1, b1, s1 = _ffn_up(xb, wts["ffn1_w_gate"], wts["ffn1_w_up"], "ffn1_up")
    h1, h1b, xh1, rs1 = _mm_res_ln(s1, wts["ffn1_w_down"], x, ln["ln1_g"], ln["ln1_b"], 0.5, "ffn1_down_ln")
    proj = _mm_nn(h1b, wts["w_in"], "mix_in")
    oraw, oa, states = _hgrn_fwd(proj, logits, ln["hg_norm_g"])
    ob = _sgu_fwd(proj, sg_g, sg_b, sg_w, sg_bs)
    oab = jnp.stack([oa, ob])
    h2, h2b, xh2, rs2 = _mm_res_ln(oab, w_out, h1, ln["ln2_g"], ln["ln2_b"], 1.0, "mix_out_ln")
    mb, mxh, mrs, kb, vb = _mem_kv(mem, ln["mem_ln_g"], ln["mem_ln_b"], wk, wv)
    qb, ob_att = _attn_fwd(h2b, wq, kb, vb)
    h3, h3b, xh3, rs3 = _mm_res_ln(ob_att.reshape(1, t, d), wo.reshape(1, d, d), h2, ln["ln3_g"], ln["ln3_b"], 1.0,
                                   "attn_out_ln")
    a2, b2, s2 = _ffn_up(h3b, wts["ffn2_w_gate"], wts["ffn2_w_up"], "ffn2_up")
    h4, _, xh4, rs4 = _mm_res_ln(s2, wts["ffn2_w_down"], h3, ln["ln4_g"], ln["ln4_b"], 0.5, "ffn2_down_ln")

    gs = {}
    gw = {}
    loss, dy4, dy4b, gs["ln4_g"], gs["ln4_b"] = _loss_ln_bwd(h4, xh4, rs4, ln["ln4_g"], target)
    nb, _, fs = wts["ffn2_w_gate"].shape
    gw["ffn2_w_down"] = _mm_tn(s2, dy4b, "lead", "shared", nb, fs, d, "g_ffn2_down", scale=0.5)
    da2, db2 = _ffn_bwd_act(dy4b, wts["ffn2_w_down"], a2, b2, 0.5, "ffn2_bwd_act")
    gw["ffn2_w_gate"] = _mm_tn(h3b, da2, "shared", "lead", nb, d, fs, "g_ffn2_gate")
    gw["ffn2_w_up"] = _mm_tn(h3b, db2, "shared", "lead", nb, d, fs, "g_ffn2_up")
    dy3, dy3b, gs["ln3_g"], gs["ln3_b"] = _dx_ln(
        dy4, [(da2, wts["ffn2_w_gate"], "lead"), (db2, wts["ffn2_w_up"], "lead")], (xh3, rs3, ln["ln3_g"]), "ffn2_dx_ln")

    gw["xa_w_o"] = _mm_tn(ob_att, dy3b, "col", "shared", 2, d // 2, d, "g_xa_o")
    dqb, dk, dv = _attn_bwd(dy3b, wo, qb, kb, vb)
    gw["xa_w_q"] = _mm_tn(h2b, dqb, "col", "shared", 2, d // 2, d, "g_xa_q")
    gw["xa_w_k"], gw["xa_w_v"], gs["mem_ln_g"], gs["mem_ln_b"] = _mem_bwd(dk, dv, mb, mxh, mrs, ln["mem_ln_g"], wk, wv)
    dy2, dy2b, gs["ln2_g"], gs["ln2_b"] = _dx_ln(dy3, [(dqb, wq.reshape(1, d, d), "col")], (xh2, rs2, ln["ln2_g"]),
                                                 "attn_dx_ln")

    gw["w_out"] = _mm_tn(oab, dy2b, "lead", "shared", 2, d // 2, d, "g_w_out")
    dmix = _mm_nt(dy2b, w_out, "mix_out_bwd")
    dq, dfz, div, dgg, dlg, dgn = _hgrn_bwd(proj, oraw, dmix, states, logits, ln["hg_norm_g"])
    du, dvv, gs["sg_ln_g"], gs["sg_ln_b"], gs["sg_w_s"], gs["sg_b_s"] = _sgu_bwd(proj, dmix, sg_g, sg_b, sg_w, sg_wt, sg_bs)
    gs["hg_lb_logits"] = jnp.swapaxes(dlg, 0, 1)
    gs["hg_norm_g"] = jnp.sum(dgn, axis=0)
    dproj = jnp.concatenate([dq, dfz, div, dgg, du, dvv], axis=1)
    pw = wts["w_in"].shape[2]
    gw["w_in"] = _mm_tn(h1b, dproj, "shared", "col", nb, d, pw, "g_w_in")
    dy1, dy1b, gs["ln1_g"], gs["ln1_b"] = _dx_ln(dy2, [(dproj, wts["w_in"], "col")], (xh1, rs1, ln["ln1_g"]), "mix_dx_ln")

    gw["ffn1_w_down"] = _mm_tn(s1, dy1b, "lead", "shared", nb, fs, d, "g_ffn1_down", scale=0.5)
    da1, db1 = _ffn_bwd_act(dy1b, wts["ffn1_w_down"], a1, b1, 0.5, "ffn1_bwd_act")
    gw["ffn1_w_gate"] = _mm_tn(xb, da1, "shared", "lead", nb, d, fs, "g_ffn1_gate")
    gw["ffn1_w_up"] = _mm_tn(xb, db1, "shared", "lead", nb, d, fs, "g_ffn1_up")
    grad_x = _dx_ln(dy1, [(da1, wts["ffn1_w_gate"], "lead"), (db1, wts["ffn1_w_up"], "lead")], None, "ffn1_dx")
    return loss, grad_x, gw, gs


_BIG_NAMES = ("ffn1_w_gate", "ffn1_w_up", "ffn1_w_down", "w_in", "w_out", "xa_w_q", "xa_w_k", "xa_w_v", "xa_w_o",
              "ffn2_w_gate", "ffn2_w_up", "ffn2_w_down")
_WEIGHT_NAMES = ("ffn1_w_gate", "ffn1_w_up", "ffn1_w_down", "ln1_g", "ln1_b", "w_in", "hg_lb_logits", "hg_norm_g",
                 "sg_ln_g", "sg_ln_b", "sg_w_s", "sg_b_s", "w_out", "ln2_g", "ln2_b", "mem_ln_g", "mem_ln_b",
                 "xa_w_q", "xa_w_k", "xa_w_v", "xa_w_o", "ln3_g", "ln3_b", "ffn2_w_gate", "ffn2_w_up", "ffn2_w_down",
                 "ln4_g", "ln4_b")


def kernel(x, mem, ffn1_w_gate, ffn1_w_up, ffn1_w_down, ln1_g, ln1_b, w_in, hg_lb_logits, hg_norm_g, sg_ln_g, sg_ln_b, sg_w_s, sg_b_s, w_out, ln2_g, ln2_b, mem_ln_g, mem_ln_b, xa_w_q, xa_w_k, xa_w_v, xa_w_o, ln3_g, ln3_b, ffn2_w_gate, ffn2_w_up, ffn2_w_down, ln4_g, ln4_b, loss_target, m_ffn1_w_gate, m_ffn1_w_up, m_ffn1_w_down, m_ln1_g, m_ln1_b, m_w_in, m_hg_lb_logits, m_hg_norm_g, m_sg_ln_g, m_sg_ln_b, m_sg_w_s, m_sg_b_s, m_w_out, m_ln2_g, m_ln2_b, m_mem_ln_g, m_mem_ln_b, m_xa_w_q, m_xa_w_k, m_xa_w_v, m_xa_w_o, m_ln3_g, m_ln3_b, m_ffn2_w_gate, m_ffn2_w_up, m_ffn2_w_down, m_ln4_g, m_ln4_b, v_ffn1_w_gate, v_ffn1_w_up, v_ffn1_w_down, v_ln1_g, v_ln1_b, v_w_in, v_hg_lb_logits, v_hg_norm_g, v_sg_ln_g, v_sg_ln_b, v_sg_w_s, v_sg_b_s, v_w_out, v_ln2_g, v_ln2_b, v_mem_ln_g, v_mem_ln_b, v_xa_w_q, v_xa_w_k, v_xa_w_v, v_xa_w_o, v_ln3_g, v_ln3_b, v_ffn2_w_gate, v_ffn2_w_up, v_ffn2_w_down, v_ln4_g, v_ln4_b):
    args = dict(locals())
    w = {k: args[k] for k in _WEIGHT_NAMES}
    m = {k: args["m_" + k] for k in _WEIGHT_NAMES}
    v = {k: args["v_" + k] for k in _WEIGHT_NAMES}

    shards = [w[k][0].astype(BF16) for k in _BIG_NAMES]
    gathered = _all_gather(shards, "all_gather_weights")
    wts = dict(zip(_BIG_NAMES, gathered))
    small = {k: w[k][0] if k not in ("hg_lb_logits",) else w[k] for k in _SMALL_NAMES}

    loss, grad_x, gw, gs = _local_step(x[0], mem[0], loss_target[0], wts, small)

    small_like = {k: w[k] for k in _SMALL_NAMES}
    gs_full = {k: gs[k].reshape(small_like[k].shape) for k in _SMALL_NAMES}
    parts = [gw[k].reshape(wts[k].shape) for k in _BIG_NAMES] + [jnp.broadcast_to(_pack_small(gs_full), (N_DEV, SMALL_ROWS, 128))]
    received = _exchange_blocks(parts, "exchange_gradients")

    out_g, out_d, out_m, out_v = {}, {}, {}, {}
    for k, part in zip(_BIG_NAMES, received[:-1]):
        g_k, d_k, m_k, v_k = _adam_sharded(part, w[k][0], m[k][0], v[k][0], "adam_" + k)
        out_g[k], out_d[k], out_m[k], out_v[k] = g_k[None], d_k[None], m_k[None], v_k[None]
    packed = _adam_small(received[-1], _pack_small(small_like), _pack_small({k: m[k] for k in _SMALL_NAMES}),
                         _pack_small({k: v[k] for k in _SMALL_NAMES}))
    for dst, pk in zip((out_g, out_d, out_m, out_v), packed):
        dst.update(_unpack_small(pk, small_like))

    loss_all = lax.psum(loss[0, 0], ("x", "y", "c"))
    return (loss_all, grad_x[None], *[out_g[k] for k in _WEIGHT_NAMES], *[out_d[k] for k in _WEIGHT_NAMES],
            *[out_m[k] for k in _WEIGHT_NAMES], *[out_v[k] for k in _WEIGHT_NAMES])
```

```python
import jax
import jax.numpy as jnp
from jax import lax
from jax.experimental import pallas as pl
from jax.experimental.pallas import tpu as pltpu

F32 = jnp.float32
BF16 = jnp.bfloat16

N_DEV = 8
ALPHA = 2.0 ** 0.25
LN_EPS = 1e-5
HG_HEADS = 4
HG_DIM = 128
SG_GROUPS = 4
SG_DIM = 128
SG_CHUNK = 128
X_HEADS = 4
HG_BLOCK = 16
ADAM_LR = 0.001
ADAM_B1 = 0.9
ADAM_B2 = 0.999
ADAM_EPS = 1e-08
ADAM_WD = 0.01
ADAM_STEP = 10
VMEM_LIMIT_V7X = 48 * 1024 * 1024
MXU_WIDTH_V7X = 256
LANES = 128
SMALL_ROWS = 624
MESH_ID = pl.DeviceIdType.MESH
ANY = pl.BlockSpec(memory_space=pl.ANY)
HBM = pl.BlockSpec(memory_space=pltpu.HBM)
SEM = pl.BlockSpec(memory_space=pltpu.SEMAPHORE)
DATAFLOW = pltpu.SideEffectType.DATAFLOW_SIDE_EFFECTING


def _params(n_axes):
    return pltpu.CompilerParams(dimension_semantics=("arbitrary",) * n_axes, vmem_limit_bytes=VMEM_LIMIT_V7X)


def _dot(a, b):
    return jnp.dot(a, b, preferred_element_type=F32)


def _dot_nt(a, b):
    return lax.dot_general(a, b, (((1,), (1,)), ((), ())), preferred_element_type=F32)


def _dot_tn(a, b):
    return lax.dot_general(a, b, (((0,), (0,)), ((), ())), preferred_element_type=F32)


def _sigmoid(x):
    return 1.0 / (1.0 + jnp.exp(-x))


def _silu_and_grad(a):
    sig = _sigmoid(a)
    return a * sig, sig * (1.0 + a * (1.0 - sig))


_GELU_C = 0.7978845608028654


def _gelu_and_grad(x):
    inner = _GELU_C * (x + 0.044715 * x * x * x)
    t = jnp.tanh(inner)
    val = 0.5 * x * (1.0 + t)
    grad = 0.5 * (1.0 + t) + 0.5 * x * (1.0 - t * t) * _GELU_C * (1.0 + 3.0 * 0.044715 * x * x)
    return val, grad


def _ln_fwd(y, g, b):
    mu = jnp.mean(y, axis=-1, keepdims=True)
    yc = y - mu
    var = jnp.mean(yc * yc, axis=-1, keepdims=True)
    rstd = lax.rsqrt(var + LN_EPS)
    xhat = yc * rstd
    return xhat * g + b, xhat, rstd


def _ln_bwd(dh, xhat, rstd, g):
    dxh = dh * g
    m1 = jnp.mean(dxh, axis=-1, keepdims=True)
    m2 = jnp.mean(dxh * xhat, axis=-1, keepdims=True)
    dy = rstd * (dxh - m1 - xhat * m2)
    dg = jnp.sum(dh * xhat, axis=0, keepdims=True)
    db = jnp.sum(dh, axis=0, keepdims=True)
    return dy, dg, db


def _split3(x):
    hi = x.astype(BF16)
    r1 = x - hi.astype(F32)
    mid = r1.astype(BF16)
    lo = (r1 - mid.astype(F32)).astype(BF16)
    return hi, mid, lo


def _mask_dot(mask, x):
    hi, mid, lo = _split3(x)
    return _dot(mask, hi) + _dot(mask, mid) + _dot(mask, lo)


def _block_masks(n):
    r = lax.broadcasted_iota(jnp.int32, (n, n), 0)
    c = lax.broadcasted_iota(jnp.int32, (n, n), 1)
    assert HG_BLOCK & (HG_BLOCK - 1) == 0
    same = (r & -HG_BLOCK) == (c & -HG_BLOCK)
    one = jnp.ones((n, n), BF16)
    zero = jnp.zeros((n, n), BF16)
    lower = jnp.where(same & (c <= r), one, zero)
    upper = jnp.where(same & (c >= r), one, zero)
    whole = jnp.where(same, one, zero)
    return lower, upper, whole


def _row_tile(t):
    return min(t, 512)


def _col_tile(n):
    for cand in (512, 256, 128):
        if n % cand == 0:
            return cand
    return n


def _drop_deps(body, n_in, n_deps):
    if n_deps == 0:
        return body
    return lambda *refs: body(*refs[:n_in], *refs[n_in + n_deps:])


def _ffn_up(hb, wg, wu, name, deps=()):
    t, d = hb.shape
    f = wg.shape[1]
    tm = _row_tile(t)
    tn = _col_tile(f)

    def body(h_ref, wg_ref, wu_ref, a_ref, b_ref, s_ref):
        h = h_ref[...]
        a = _dot(h, wg_ref[...])
        b = _dot(h, wu_ref[...])
        a_ref[...] = a
        b_ref[...] = b
        s_ref[...] = (a * _sigmoid(a) * b).astype(BF16)

    act = pl.BlockSpec((tm, tn), lambda j, i: (i, j))
    wsp = pl.BlockSpec((d, tn), lambda j, i: (0, j))
    return pl.pallas_call(
        _drop_deps(body, 3, len(deps)),
        grid=(f // tn, t // tm),
        in_specs=[pl.BlockSpec((tm, d), lambda j, i: (i, 0)), wsp, wsp] + [ANY] * len(deps),
        out_specs=[act, act, act],
        out_shape=[jax.ShapeDtypeStruct((t, f), F32), jax.ShapeDtypeStruct((t, f), F32),
                   jax.ShapeDtypeStruct((t, f), BF16)],
        compiler_params=_params(2),
        name=name,
    )(hb, wg, wu, *deps)


def _mm_res_ln(lhs, w, res, g, b, coef, name):
    t, kd = lhs.shape
    d = w.shape[1]
    tm = _row_tile(t)
    tk = _col_tile(kd)
    nk = kd // tk

    def body(l_ref, w_ref, r_ref, g_ref, b_ref, h_ref, hb_ref, xh_ref, rs_ref, acc):
        k = pl.program_id(1)

        @pl.when(k == 0)
        def _():
            acc[...] = jnp.zeros_like(acc)

        acc[...] += _dot(l_ref[...], w_ref[...])

        @pl.when(k == nk - 1)
        def _():
            y = ALPHA * r_ref[...] + coef * acc[...]
            h, xhat, rstd = _ln_fwd(y, g_ref[...], b_ref[...])
            h_ref[...] = h
            hb_ref[...] = h.astype(BF16)
            xh_ref[...] = xhat
            rs_ref[...] = rstd

    row = pl.BlockSpec((tm, d), lambda i, k: (i, 0))
    vec = pl.BlockSpec((1, d), lambda i, k: (0, 0))
    return pl.pallas_call(
        body,
        grid=(t // tm, nk),
        in_specs=[pl.BlockSpec((tm, tk), lambda i, k: (i, k)), pl.BlockSpec((tk, d), lambda i, k: (k, 0)),
                  row, vec, vec],
        out_specs=[row, row, row, pl.BlockSpec((tm, 1), lambda i, k: (i, 0))],
        out_shape=[jax.ShapeDtypeStruct((t, d), F32), jax.ShapeDtypeStruct((t, d), BF16),
                   jax.ShapeDtypeStruct((t, d), F32), jax.ShapeDtypeStruct((t, 1), F32)],
        scratch_shapes=[pltpu.VMEM((tm, d), F32)],
        compiler_params=_params(2),
        name=name,
    )(lhs, w, res, g, b)


def _mm_nn(lhs, w, name):
    t, kd = lhs.shape
    n = w.shape[1]
    tm = _row_tile(t)
    tn = _col_tile(n)

    def body(l_ref, w_ref, o_ref):
        o_ref[...] = _dot(l_ref[...], w_ref[...])

    return pl.pallas_call(
        body,
        grid=(n // tn, t // tm),
        in_specs=[pl.BlockSpec((tm, kd), lambda j, i: (i, 0)), pl.BlockSpec((kd, tn), lambda j, i: (0, j))],
        out_specs=pl.BlockSpec((tm, tn), lambda j, i: (i, j)),
        out_shape=jax.ShapeDtypeStruct((t, n), F32),
        compiler_params=_params(2),
        name=name,
    )(lhs, w)


def _lower_bound(lg):
    m = jnp.max(lg, axis=0, keepdims=True)
    e = jnp.exp(lg - m)
    return e[0:1, :] / jnp.sum(e, axis=0, keepdims=True)


def _forget_terms(fz, lb):
    e = jnp.exp(-jnp.abs(fz))
    r = 1.0 / (1.0 + e)
    pos = fz >= 0.0
    sig = jnp.where(pos, r, e * r)
    nsig = jnp.where(pos, e * r, r)
    f = lb + (1.0 - lb) * sig
    k = (1.0 - lb) * nsig
    return sig, nsig, f, k


def _hg_tile(t):
    return min(t, 256)


def _hgrn_fwd(proj, logits, gn):
    t = proj.shape[0]
    ct = _hg_tile(t)
    nct = t // ct
    nblk = ct // HG_BLOCK
    nh = HG_HEADS

    def body(q_ref, fz_ref, iv_ref, gg_ref, lg_ref, gn_ref, oraw_ref, oa_ref, st_ref,
             state, qt_s, kt_s, k_s, b_s, dec_s):
        c = pl.program_id(1)

        @pl.when(c == 0)
        def _():
            state[...] = jnp.zeros_like(state)

        lb = _lower_bound(lg_ref[...])
        q = q_ref[...]
        _, _, f, k = _forget_terms(fz_ref[...], lb)
        logf = jnp.log(f)
        lower, _, whole = _block_masks(ct)
        b = _mask_dot(lower, logf)
        bend = _mask_dot(whole, logf)
        qt_s[...] = (q * jnp.exp(b)).astype(BF16)
        kt_s[...] = (k * jnp.exp(bend - b)).astype(BF16)
        k_s[...] = k
        b_s[...] = b
        dec_s[...] = jnp.exp(bend)
        tidx = lax.broadcasted_iota(jnp.int32, (HG_BLOCK, HG_DIM), 0)

        def blk(i, carry):
            r0 = pl.multiple_of(i * HG_BLOCK, HG_BLOCK)
            rows = pl.ds(r0, HG_BLOCK)
            st = state[...]
            st_ref[i] = st
            v = iv_ref[rows, :]
            qq = q_ref[rows, :]
            kk = k_s[rows, :]
            bb = b_s[rows, :]
            o = _dot_nt(qt_s[rows, :], st.astype(BF16))
            for s in range(HG_BLOCK):
                e = jnp.where(tidx >= s, jnp.exp(jnp.minimum(bb - bb[s:s + 1, :], 0.0)), 0.0)
                acol = jnp.sum(qq * kk[s:s + 1, :] * e, axis=1, keepdims=True)
                o = o + acol * v[s:s + 1, :]
            oraw_ref[rows, :] = o
            state[...] = st * dec_s[pl.ds(r0, 1), :] + _dot_tn(v.astype(BF16), kt_s[rows, :])
            return carry

        lax.fori_loop(0, nblk, blk, 0)
        oraw = oraw_ref[...]
        r = lax.rsqrt(jnp.mean(oraw * oraw, axis=-1, keepdims=True) + LN_EPS)
        gg = gg_ref[...]
        oa_ref[...] = (oraw * r * gn_ref[...] * gg * _sigmoid(gg)).astype(BF16)

    def slab(off):
        return pl.BlockSpec((ct, HG_DIM), lambda h, c: (c, off + h))

    out_slab = pl.BlockSpec((ct, HG_DIM), lambda h, c: (c, h))
    return pl.pallas_call(
        body,
        grid=(nh, nct),
        in_specs=[slab(0), slab(nh), slab(2 * nh), slab(3 * nh),
                  pl.BlockSpec((None, 2, HG_DIM), lambda h, c: (h, 0, 0)),
                  pl.BlockSpec((1, HG_DIM), lambda h, c: (0, 0))],
        out_specs=[out_slab, out_slab, pl.BlockSpec((None, nblk, HG_DIM, HG_DIM), lambda h, c: (h, c, 0, 0))],
        out_shape=[jax.ShapeDtypeStruct((t, nh * HG_DIM), F32),
                   jax.ShapeDtypeStruct((t, (nh + SG_GROUPS) * HG_DIM), BF16),
                   jax.ShapeDtypeStruct((nh, t // HG_BLOCK, HG_DIM, HG_DIM), F32)],
        scratch_shapes=[pltpu.VMEM((HG_DIM, HG_DIM), F32), pltpu.VMEM((ct, HG_DIM), BF16),
                        pltpu.VMEM((ct, HG_DIM), BF16), pltpu.VMEM((ct, HG_DIM), F32),
                        pltpu.VMEM((ct, HG_DIM), F32), pltpu.VMEM((ct, HG_DIM), F32)],
        compiler_params=_params(2),
        name="hgrn_fwd",
    )(proj, proj, proj, proj, logits, gn)


def _sg_tile(t):
    return min(t, 512)


def _sgu_chunk_fwd(u, v, ln_g, ln_b, wm, bs):
    ua, dua = _gelu_and_grad(u)
    va, dva = _gelu_and_grad(v)
    vn, xhat, rstd = _ln_fwd(va, ln_g, ln_b)
    s = _dot(wm, vn.astype(BF16)) + bs
    return ua, dua, dva, vn, xhat, rstd, s


def _tril_weight(w_ref):
    n = SG_CHUNK
    r = lax.broadcasted_iota(jnp.int32, (n, n), 0)
    c = lax.broadcasted_iota(jnp.int32, (n, n), 1)
    return jnp.where(c <= r, w_ref[...], 0.0)


def _sgu_fwd(proj, mix, ln_g, ln_b, w_s, b_col):
    t = proj.shape[0]
    ct = _sg_tile(t)
    ng = SG_GROUPS
    off_u = 4 * HG_HEADS
    off_v = off_u + ng

    def body(u_ref, v_ref, g_ref, b_ref, w_ref, bs_ref, mix_ref, o_ref):
        del mix_ref
        wm = _tril_weight(w_ref).astype(BF16)
        for n in range(ct // SG_CHUNK):
            rows = slice(n * SG_CHUNK, (n + 1) * SG_CHUNK)
            ua, _, _, _, _, _, s = _sgu_chunk_fwd(u_ref[rows, :], v_ref[rows, :], g_ref[...], b_ref[...], wm, bs_ref[...])
            o_ref[rows, :] = (ua * s).astype(BF16)

    vec = pl.BlockSpec((None, 1, SG_DIM), lambda g, c: (g, 0, 0))
    return pl.pallas_call(
        body,
        grid=(ng, t // ct),
        in_specs=[pl.BlockSpec((ct, SG_DIM), lambda g, c: (c, off_u + g)),
                  pl.BlockSpec((ct, SG_DIM), lambda g, c: (c, off_v + g)), vec, vec,
                  pl.BlockSpec((None, SG_CHUNK, SG_CHUNK), lambda g, c: (g, 0, 0)),
                  pl.BlockSpec((None, SG_CHUNK, 1), lambda g, c: (g, 0, 0)), ANY],
        out_specs=pl.BlockSpec((ct, SG_DIM), lambda g, c: (c, HG_HEADS + g)),
        out_shape=jax.ShapeDtypeStruct(mix.shape, mix.dtype),
        input_output_aliases={6: 0},
        compiler_params=_params(2),
        name="sgu_fwd",
    )(proj, proj, ln_g, ln_b, w_s, b_col, mix)


def _mem_kv(mem, g, b, wk, wv):
    m_len, d = mem.shape

    def body(m_ref, g_ref, b_ref, wk_ref, wv_ref, mb_ref, xh_ref, rs_ref, k_ref, v_ref):
        m, xhat, rstd = _ln_fwd(m_ref[...], g_ref[...], b_ref[...])
        mb = m.astype(BF16)
        mb_ref[...] = mb
        xh_ref[...] = xhat
        rs_ref[...] = rstd
        k_ref[...] = _dot(mb, wk_ref[...]).astype(BF16)
        v_ref[...] = _dot(mb, wv_ref[...]).astype(BF16)

    return pl.pallas_call(
        body,
        out_shape=[jax.ShapeDtypeStruct((m_len, d), BF16), jax.ShapeDtypeStruct((m_len, d), F32),
                   jax.ShapeDtypeStruct((m_len, 1), F32), jax.ShapeDtypeStruct((m_len, d), BF16),
                   jax.ShapeDtypeStruct((m_len, d), BF16)],
        compiler_params=pltpu.CompilerParams(vmem_limit_bytes=VMEM_LIMIT_V7X),
        name="mem_kv",
    )(mem, g, b, wk, wv)


def _softmax_rows(s):
    m = jnp.max(s, axis=-1, keepdims=True)
    p = jnp.exp(s - m)
    return p / jnp.sum(p, axis=-1, keepdims=True)


def _attn_fwd(hb, wq, kb, vb):
    t, d = hb.shape
    tm = _row_tile(t)
    dh = d // X_HEADS
    scale = dh ** -0.5

    def body(h_ref, wq_ref, k_ref, v_ref, q_ref, o_ref):
        q = _dot(h_ref[...], wq_ref[...]).astype(BF16)
        q_ref[...] = q
        for hd in range(X_HEADS):
            sl = slice(hd * dh, (hd + 1) * dh)
            p = _softmax_rows(_dot_nt(q[:, sl], k_ref[:, sl]) * scale)
            o_ref[:, sl] = _dot(p.astype(BF16), v_ref[:, sl]).astype(BF16)

    row = pl.BlockSpec((tm, d), lambda i: (i, 0))
    full = lambda a: pl.BlockSpec(a.shape, lambda i: (0, 0))
    return pl.pallas_call(
        body,
        grid=(t // tm,),
        in_specs=[row, full(wq), full(kb), full(vb)],
        out_specs=[row, row],
        out_shape=[jax.ShapeDtypeStruct((t, d), BF16), jax.ShapeDtypeStruct((t, d), BF16)],
        compiler_params=_params(1),
        name="attn_fwd",
    )(hb, wq, kb, vb)


def _loss_ln_bwd(h, xhat, rstd, g, target):
    t, d = h.shape
    tm = _row_tile(t)
    nt = t // tm

    def body(h_ref, xh_ref, rs_ref, g_ref, t_ref, loss_ref, dy_ref, dyb_ref, dg_ref, db_ref, lacc):
        i = pl.program_id(0)

        @pl.when(i == 0)
        def _():
            lacc[...] = jnp.zeros_like(lacc)
            dg_ref[...] = jnp.zeros_like(dg_ref)
            db_ref[...] = jnp.zeros_like(db_ref)

        err = h_ref[...] - t_ref[...]
        lacc[...] += jnp.sum(err * err, axis=0, keepdims=True)
        dy, dg, db = _ln_bwd(err * (1.0 / d), xh_ref[...], rs_ref[...], g_ref[...])
        dy_ref[...] = dy
        dyb_ref[...] = dy.astype(BF16)
        dg_ref[...] += dg
        db_ref[...] += db

        @pl.when(i == nt - 1)
        def _():
            loss_ref[...] = jnp.zeros_like(loss_ref) + jnp.sum(lacc[...], axis=1, keepdims=True) * (0.5 / d)

    row = pl.BlockSpec((tm, d), lambda i: (i, 0))
    vec = pl.BlockSpec((1, d), lambda i: (0, 0))
    return pl.pallas_call(
        body,
        grid=(nt,),
        in_specs=[row, row, pl.BlockSpec((tm, 1), lambda i: (i, 0)), vec, row],
        out_specs=[pl.BlockSpec((1, LANES), lambda i: (0, 0)), row, row, vec, vec],
        out_shape=[jax.ShapeDtypeStruct((1, LANES), F32), jax.ShapeDtypeStruct((t, d), F32),
                   jax.ShapeDtypeStruct((t, d), BF16), jax.ShapeDtypeStruct((1, d), F32),
                   jax.ShapeDtypeStruct((1, d), F32)],
        scratch_shapes=[pltpu.VMEM((1, d), F32)],
        compiler_params=_params(1),
        name="loss_ln_bwd",
    )(h, xhat, rstd, g, target)


def _ffn_bwd_act(dyb, wd, a, b, coef, name, deps=()):
    t, d = dyb.shape
    f = wd.shape[0]
    tm = _row_tile(t)
    tn = _col_tile(f)

    def body(dy_ref, wd_ref, a_ref, b_ref, da_ref, db_ref):
        ds = _dot_nt(dy_ref[...], wd_ref[...]) * coef
        silu, dsilu = _silu_and_grad(a_ref[...])
        da_ref[...] = (ds * b_ref[...] * dsilu).astype(BF16)
        db_ref[...] = (ds * silu).astype(BF16)

    act = pl.BlockSpec((tm, tn), lambda j, i: (i, j))
    return pl.pallas_call(
        _drop_deps(body, 4, len(deps)),
        grid=(f // tn, t // tm),
        in_specs=[pl.BlockSpec((tm, d), lambda j, i: (i, 0)), pl.BlockSpec((tn, d), lambda j, i: (j, 0)),
                  act, act] + [ANY] * len(deps),
        out_specs=[act, act],
        out_shape=[jax.ShapeDtypeStruct((t, f), BF16), jax.ShapeDtypeStruct((t, f), BF16)],
        compiler_params=_params(2),
        name=name,
    )(dyb, wd, a, b, *deps)


def _mm_tn(a, b, name, scale=1.0, deps=()):
    t, m = a.shape
    n = b.shape[1]
    tt = _row_tile(t)
    nt = t // tt
    tm_o = _col_tile(m) if m >= n else m
    tn_o = n if m >= n else _col_tile(n)

    def body(a_ref, b_ref, o_ref, acc):
        k = pl.program_id(2)

        @pl.when(k == 0)
        def _():
            acc[...] = jnp.zeros_like(acc)

        acc[...] += _dot_tn(a_ref[...], b_ref[...])

        @pl.when(k == nt - 1)
        def _():
            o_ref[...] = (acc[...] * scale).astype(BF16)

    return pl.pallas_call(
        _drop_deps(body, 2, len(deps)),
        grid=(m // tm_o, n // tn_o, nt),
        in_specs=[pl.BlockSpec((tt, tm_o), lambda i, j, k: (k, i)), pl.BlockSpec((tt, tn_o), lambda i, j, k: (k, j))]
        + [ANY] * len(deps),
        out_specs=pl.BlockSpec((tm_o, tn_o), lambda i, j, k: (i, j)),
        out_shape=jax.ShapeDtypeStruct((m, n), BF16),
        scratch_shapes=[pltpu.VMEM((tm_o, tn_o), F32)],
        compiler_params=_params(3),
        name=name,
    )(a, b, *deps)


def _mm_nt(lhs, w, name):
    t, d = lhs.shape
    kd = w.shape[0]
    tm = _row_tile(t)
    tk = _col_tile(kd)

    def body(l_ref, w_ref, o_ref):
        o_ref[...] = _dot_nt(l_ref[...], w_ref[...])

    return pl.pallas_call(
        body,
        grid=(kd // tk, t // tm),
        in_specs=[pl.BlockSpec((tm, d), lambda j, i: (i, 0)), pl.BlockSpec((tk, d), lambda j, i: (j, 0))],
        out_specs=pl.BlockSpec((tm, tk), lambda j, i: (i, j)),
        out_shape=jax.ShapeDtypeStruct((t, kd), F32),
        compiler_params=_params(2),
        name=name,
    )(lhs, w)


def _dx_ln(dy, pairs, ln, name, deps=()):
    t, d = dy.shape
    tm = _row_tile(t)
    nt = t // tm
    kd = pairs[0][0].shape[1]
    tk = _col_tile(kd)
    nk = kd // tk
    npair = len(pairs)
    n_in = 1 + 2 * npair + (3 if ln is not None else 0)

    def body(*refs):
        dy_ref = refs[0]
        pr = refs[1:1 + 2 * npair]
        pos = 1 + 2 * npair
        if ln is not None:
            xh_ref, rs_ref, g_ref = refs[pos:pos + 3]
            pos += 3
            dyo_ref, dyb_ref, dg_ref, db_ref = refs[pos:pos + 4]
            acc = refs[pos + 4]
        else:
            dh_ref = refs[pos]
            acc = refs[pos + 1]
        i = pl.program_id(0)
        k = pl.program_id(1)

        @pl.when(k == 0)
        def _():
            acc[...] = ALPHA * dy_ref[...]

        for p in range(npair):
            acc[...] += _dot_nt(pr[2 * p][...], pr[2 * p + 1][...])

        if ln is not None:
            @pl.when((i == 0) & (k == 0))
            def _():
                dg_ref[...] = jnp.zeros_like(dg_ref)
                db_ref[...] = jnp.zeros_like(db_ref)

            @pl.when(k == nk - 1)
            def _():
                dyp, dg, db = _ln_bwd(acc[...], xh_ref[...], rs_ref[...], g_ref[...])
                dyo_ref[...] = dyp
                dyb_ref[...] = dyp.astype(BF16)
                dg_ref[...] += dg
                db_ref[...] += db
        else:
            @pl.when(k == nk - 1)
            def _():
                dh_ref[...] = acc[...]

    row = pl.BlockSpec((tm, d), lambda i, k: (i, 0))
    vec = pl.BlockSpec((1, d), lambda i, k: (0, 0))
    in_specs = [row]
    args = [dy]
    for lhs, w in pairs:
        in_specs += [pl.BlockSpec((tm, tk), lambda i, k: (i, k)), pl.BlockSpec((d, tk), lambda i, k: (0, k))]
        args += [lhs, w]
    if ln is not None:
        in_specs += [row, pl.BlockSpec((tm, 1), lambda i, k: (i, 0)), vec]
        args += list(ln)
        out_specs = [row, row, vec, vec]
        out_shape = [jax.ShapeDtypeStruct((t, d), F32), jax.ShapeDtypeStruct((t, d), BF16),
                     jax.ShapeDtypeStruct((1, d), F32), jax.ShapeDtypeStruct((1, d), F32)]
    else:
        out_specs = row
        out_shape = jax.ShapeDtypeStruct((t, d), F32)
    return pl.pallas_call(
        _drop_deps(body, n_in, len(deps)),
        grid=(nt, nk),
        in_specs=in_specs + [ANY] * len(deps),
        out_specs=out_specs,
        out_shape=out_shape,
        scratch_shapes=[pltpu.VMEM((tm, d), F32)],
        compiler_params=_params(2),
        name=name,
    )(*args, *deps)


def _hgrn_bwd(proj, oraw, dmix, states, logits, gn):
    t = proj.shape[0]
    ct = _hg_tile(t)
    nct = t // ct
    nblk = ct // HG_BLOCK
    nh = HG_HEADS

    def body(q_ref, fz_ref, iv_ref, gg_ref, or_ref, do_ref, st_ref, lg_ref, gn_ref,
             dq_ref, dfz_ref, div_ref, dgg_ref, dlg_ref, dgn_ref,
             dstate, qt_s, kt_s, k_s, b_s, eb_s, ekb_s, dec_s, dor_s, dbl_s, gr_s, dk_s, dlb_acc):
        c = pl.program_id(1)

        @pl.when(c == 0)
        def _():
            dstate[...] = jnp.zeros_like(dstate)
            dlb_acc[...] = jnp.zeros_like(dlb_acc)
            dgn_ref[...] = jnp.zeros_like(dgn_ref)

        lb = _lower_bound(lg_ref[...])
        q = q_ref[...]
        sig, nsig, f, k = _forget_terms(fz_ref[...], lb)
        logf = jnp.log(f)
        lower, upper, whole = _block_masks(ct)
        b = _mask_dot(lower, logf)
        bend = _mask_dot(whole, logf)
        eb = jnp.exp(b)
        ekb = jnp.exp(bend - b)
        qt_s[...] = (q * eb).astype(BF16)
        kt_s[...] = (k * ekb).astype(BF16)
        k_s[...] = k
        b_s[...] = b
        eb_s[...] = eb
        ekb_s[...] = ekb
        dec_s[...] = jnp.exp(bend)
        oraw = or_ref[...]
        r = lax.rsqrt(jnp.mean(oraw * oraw, axis=-1, keepdims=True) + LN_EPS)
        on = oraw * r
        gg = gg_ref[...]
        silu, dsilu = _silu_and_grad(gg)
        doa = do_ref[...]
        gnv = gn_ref[...]
        dgg_ref[...] = (doa * on * gnv * dsilu).astype(BF16)
        dyn = doa * silu
        dgn_ref[...] += jnp.sum(dyn * on, axis=0, keepdims=True)
        don = dyn * gnv
        dor_s[...] = r * (don - on * jnp.mean(don * on, axis=-1, keepdims=True))
        tidx = lax.broadcasted_iota(jnp.int32, (HG_BLOCK, HG_DIM), 0)

        def blk(ii, carry):
            i = nblk - 1 - ii
            r0 = pl.multiple_of(i * HG_BLOCK, HG_BLOCK)
            rows = pl.ds(r0, HG_BLOCK)
            st = st_ref[i]
            dst = dstate[...]
            dstb = dst.astype(BF16)
            do = dor_s[rows, :]
            dob = do.astype(BF16)
            v = iv_ref[rows, :]
            vb = v.astype(BF16)
            qq = q_ref[rows, :]
            kk = k_s[rows, :]
            bb = b_s[rows, :]
            qt = qt_s[rows, :]
            kt = kt_s[rows, :]
            dec = dec_s[pl.ds(r0, 1), :]
            dkt = _dot(vb, dstb)
            dq = _dot(dob, st.astype(BF16)) * eb_s[rows, :]
            dk = dkt * ekb_s[rows, :]
            dv = _dot_nt(kt, dstb)
            gend = jnp.sum(kk * dk, axis=0, keepdims=True) + dec * jnp.sum(dst * st, axis=0, keepdims=True)
            for s in range(HG_BLOCK):
                ks = kk[s:s + 1, :]
                e = jnp.where(tidx >= s, jnp.exp(jnp.minimum(bb - bb[s:s + 1, :], 0.0)), 0.0)
                qe = qq * e
                acol = jnp.sum(qe * ks, axis=1, keepdims=True)
                dacol = jnp.sum(do * v[s:s + 1, :], axis=1, keepdims=True)
                dq = dq + dacol * (ks * e)
                dk_row = jnp.sum(dacol * qe, axis=0, keepdims=True)
                dv_row = jnp.sum(acol * do, axis=0, keepdims=True)
                dk = dk + jnp.where(tidx == s, dk_row, 0.0)
                dv = dv + jnp.where(tidx == s, dv_row, 0.0)
            dq_ref[rows, :] = dq.astype(BF16)
            div_ref[rows, :] = dv.astype(BF16)
            dk_s[rows, :] = dk
            dbl_s[rows, :] = qq * dq - kk * dk
            gr_s[rows, :] = jnp.zeros((HG_BLOCK, HG_DIM), F32) + gend
            dstate[...] = dst * dec + _dot_tn(dob, qt)
            return carry

        lax.fori_loop(0, nblk, blk, 0)
        dlogf = _mask_dot(upper, dbl_s[...]) + gr_s[...]
        dk = dk_s[...]
        dfz_ref[...] = ((dlogf / f - dk) * ((1.0 - lb) * sig * nsig)).astype(BF16)
        dlb_acc[...] += jnp.sum((dlogf / f - dk) * nsig, axis=0, keepdims=True)

        @pl.when(c == nct - 1)
        def _():
            dl0 = dlb_acc[...] * lb * (1.0 - lb)
            layer = lax.broadcasted_iota(jnp.int32, (2, HG_DIM), 0)
            dlg_ref[...] = jnp.where(layer == 0, dl0, -dl0)

    def slab(off):
        return pl.BlockSpec((ct, HG_DIM), lambda h, c: (nct - 1 - c, off + h))

    out_slab = pl.BlockSpec((ct, HG_DIM), lambda h, c: (nct - 1 - c, h))
    tile_f32 = pltpu.VMEM((ct, HG_DIM), F32)
    tile_b16 = pltpu.VMEM((ct, HG_DIM), BF16)
    slab_shape = jax.ShapeDtypeStruct((t, nh * HG_DIM), BF16)
    return pl.pallas_call(
        body,
        grid=(nh, nct),
        in_specs=[slab(0), slab(nh), slab(2 * nh), slab(3 * nh), slab(0), slab(0),
                  pl.BlockSpec((None, nblk, HG_DIM, HG_DIM), lambda h, c: (h, nct - 1 - c, 0, 0)),
                  pl.BlockSpec((None, 2, HG_DIM), lambda h, c: (h, 0, 0)),
                  pl.BlockSpec((1, HG_DIM), lambda h, c: (0, 0))],
        out_specs=[out_slab, out_slab, out_slab, out_slab,
                   pl.BlockSpec((None, 2, HG_DIM), lambda h, c: (h, 0, 0)),
                   pl.BlockSpec((None, 1, HG_DIM), lambda h, c: (h, 0, 0))],
        out_shape=[slab_shape, slab_shape, slab_shape, slab_shape,
                   jax.ShapeDtypeStruct((nh, 2, HG_DIM), F32), jax.ShapeDtypeStruct((nh, 1, HG_DIM), F32)],
        scratch_shapes=[pltpu.VMEM((HG_DIM, HG_DIM), F32), tile_b16, tile_b16, tile_f32, tile_f32, tile_f32, tile_f32,
                        tile_f32, tile_f32, tile_f32, tile_f32, tile_f32, pltpu.VMEM((1, HG_DIM), F32)],
        compiler_params=_params(2),
        name="hgrn_bwd",
    )(proj, proj, proj, proj, oraw, dmix, states, logits, gn)


def _sgu_bwd(proj, dmix, ln_g, ln_b, w_s, w_t, b_col):
    t = proj.shape[0]
    ct = _sg_tile(t)
    nct = t // ct
    ng = SG_GROUPS
    off_u = 4 * HG_HEADS
    off_v = off_u + ng
    n = SG_CHUNK

    def body(u_ref, v_ref, do_ref, g_ref, b_ref, w_ref, wt_ref, bs_ref, du_ref, dv_ref, dg_ref, db_ref, dw_ref, dbs_ref):
        c = pl.program_id(1)

        @pl.when(c == 0)
        def _():
            dg_ref[...] = jnp.zeros_like(dg_ref)
            db_ref[...] = jnp.zeros_like(db_ref)
            dw_ref[...] = jnp.zeros_like(dw_ref)
            dbs_ref[...] = jnp.zeros_like(dbs_ref)

        r = lax.broadcasted_iota(jnp.int32, (n, n), 0)
        cc = lax.broadcasted_iota(jnp.int32, (n, n), 1)
        wm = jnp.where(cc <= r, w_ref[...], 0.0).astype(BF16)
        wmt = jnp.where(r <= cc, wt_ref[...], 0.0).astype(BF16)
        for ci in range(ct // n):
            rows = slice(ci * n, (ci + 1) * n)
            ua, dua, dva, vn, xhat, rstd, s = _sgu_chunk_fwd(u_ref[rows, :], v_ref[rows, :], g_ref[...], b_ref[...],
                                                             wm, bs_ref[...])
            do = do_ref[rows, :]
            du_ref[rows, :] = (do * s * dua).astype(BF16)
            ds = do * ua
            dsb = ds.astype(BF16)
            dbs_ref[...] += jnp.sum(ds, axis=1, keepdims=True)
            dw_ref[...] += _dot_nt(dsb, vn.astype(BF16))
            dvn = _dot(wmt, dsb)
            dva_in, dg, db = _ln_bwd(dvn, xhat, rstd, g_ref[...])
            dg_ref[...] += dg
            db_ref[...] += db
            dv_ref[rows, :] = (dva_in * dva).astype(BF16)

        @pl.when(c == nct - 1)
        def _():
            dw_ref[...] = jnp.where(cc <= r, dw_ref[...], 0.0)

    vec = pl.BlockSpec((None, 1, SG_DIM), lambda g, c: (g, 0, 0))
    mat = pl.BlockSpec((None, n, n), lambda g, c: (g, 0, 0))
    col = pl.BlockSpec((None, n, 1), lambda g, c: (g, 0, 0))
    out_slab = pl.BlockSpec((ct, SG_DIM), lambda g, c: (c, g))
    return pl.pallas_call(
        body,
        grid=(ng, nct),
        in_specs=[pl.BlockSpec((ct, SG_DIM), lambda g, c: (c, off_u + g)),
                  pl.BlockSpec((ct, SG_DIM), lambda g, c: (c, off_v + g)),
                  pl.BlockSpec((ct, SG_DIM), lambda g, c: (c, ng + g)), vec, vec, mat, mat, col],
        out_specs=[out_slab, out_slab, vec, vec, mat, col],
        out_shape=[jax.ShapeDtypeStruct((t, ng * SG_DIM), BF16), jax.ShapeDtypeStruct((t, ng * SG_DIM), BF16),
                   jax.ShapeDtypeStruct((ng, 1, SG_DIM), F32), jax.ShapeDtypeStruct((ng, 1, SG_DIM), F32),
                   jax.ShapeDtypeStruct((ng, n, n), F32), jax.ShapeDtypeStruct((ng, n, 1), F32)],
        compiler_params=_params(2),
        name="sgu_bwd",
    )(proj, proj, dmix, ln_g, ln_b, w_s, w_t, b_col)


def _attn_bwd(dyb, wo, qb, kb, vb):
    t, d = dyb.shape
    m_len = kb.shape[0]
    tm = _row_tile(t)
    dh = d // X_HEADS
    scale = dh ** -0.5

    def body(dy_ref, wo_ref, q_ref, k_ref, v_ref, dq_ref, dk_ref, dv_ref):
        i = pl.program_id(0)

        @pl.when(i == 0)
        def _():
            dk_ref[...] = jnp.zeros_like(dk_ref)
            dv_ref[...] = jnp.zeros_like(dv_ref)

        do = _dot_nt(dy_ref[...], wo_ref[...]).astype(BF16)
        for hd in range(X_HEADS):
            sl = slice(hd * dh, (hd + 1) * dh)
            qh = q_ref[:, sl]
            p = _softmax_rows(_dot_nt(qh, k_ref[:, sl]) * scale)
            doh = do[:, sl]
            dp = _dot_nt(doh, v_ref[:, sl])
            ds = (p * (dp - jnp.sum(dp * p, axis=-1, keepdims=True)) * scale).astype(BF16)
            dq_ref[:, sl] = _dot(ds, k_ref[:, sl]).astype(BF16)
            dk_ref[:, sl] += _dot_tn(ds, qh)
            dv_ref[:, sl] += _dot_tn(p.astype(BF16), doh)

    row = pl.BlockSpec((tm, d), lambda i: (i, 0))
    full = lambda a: pl.BlockSpec(a.shape, lambda i: (0, 0))
    kv = pl.BlockSpec((m_len, d), lambda i: (0, 0))
    return pl.pallas_call(
        body,
        grid=(t // tm,),
        in_specs=[row, full(wo), row, full(kb), full(vb)],
        out_specs=[row, kv, kv],
        out_shape=[jax.ShapeDtypeStruct((t, d), BF16), jax.ShapeDtypeStruct((m_len, d), F32),
                   jax.ShapeDtypeStruct((m_len, d), F32)],
        compiler_params=_params(1),
        name="attn_bwd",
    )(dyb, wo, qb, kb, vb)


def _mem_bwd(dk, dv, mb, xhat, rstd, g, wk, wv):
    m_len, d = dk.shape

    def body(dk_ref, dv_ref, mb_ref, xh_ref, rs_ref, g_ref, wk_ref, wv_ref, gwk_ref, gwv_ref, dg_ref, db_ref):
        dkb = dk_ref[...].astype(BF16)
        dvb = dv_ref[...].astype(BF16)
        mb_v = mb_ref[...]
        gwk_ref[...] = _dot_tn(mb_v, dkb).astype(BF16)
        gwv_ref[...] = _dot_tn(mb_v, dvb).astype(BF16)
        dm = _dot_nt(dkb, wk_ref[...]) + _dot_nt(dvb, wv_ref[...])
        _, dg, db = _ln_bwd(dm, xh_ref[...], rs_ref[...], g_ref[...])
        dg_ref[...] = dg
        db_ref[...] = db

    return pl.pallas_call(
        body,
        out_shape=[jax.ShapeDtypeStruct((d, d), BF16), jax.ShapeDtypeStruct((d, d), BF16),
                   jax.ShapeDtypeStruct((1, d), F32), jax.ShapeDtypeStruct((1, d), F32)],
        compiler_params=pltpu.CompilerParams(vmem_limit_bytes=VMEM_LIMIT_V7X),
        name="mem_bwd",
    )(dk, dv, mb, xhat, rstd, g, wk, wv)


def _adamw(w, g, m, v):
    m = ADAM_B1 * m + (1.0 - ADAM_B1) * g
    v = ADAM_B2 * v + (1.0 - ADAM_B2) * (g * g)
    m_hat = m / (1.0 - ADAM_B1 ** ADAM_STEP)
    v_hat = v / (1.0 - ADAM_B2 ** ADAM_STEP)
    delta = -ADAM_LR * (m_hat / (jnp.sqrt(v_hat) + ADAM_EPS) + ADAM_WD * w)
    return delta, m, v


def _slot_sum(ref):
    g = ref[0].astype(F32)
    for s in range(1, N_DEV):
        g = g + ref[s].astype(F32)
    return g


def _adam_sharded(lands, w, m, v, axis, name):
    rows, cols = w.shape
    if axis == 1:
        tr = 256 if rows % 256 == 0 else rows
        grid = (rows // tr,)
        wblk = pl.BlockSpec((tr, cols), lambda i: (i, 0))
        lblk = [pl.BlockSpec((N_DEV, tr, a.shape[2]), lambda i: (0, i, 0)) for a in lands]
    else:
        tc = _col_tile(cols)
        grid = (cols // tc,)
        wblk = pl.BlockSpec((rows, tc), lambda i: (0, i))
        lblk = [pl.BlockSpec((N_DEV, a.shape[1], tc), lambda i: (0, 0, i)) for a in lands]
    nl = len(lands)

    def body(*refs):
        w_ref, m_ref, v_ref = refs[nl:nl + 3]
        g_ref, d_ref, nm_ref, nv_ref = refs[nl + 3:]
        g = _slot_sum(refs[0])
        if nl == 2:
            tail = _slot_sum(refs[1])
            if axis == 1:
                g = jnp.concatenate([g, tail[:, :cols - g.shape[1]]], axis=1)
            else:
                g = jnp.concatenate([g, tail[:rows - g.shape[0], :]], axis=0)
        delta, nm, nv = _adamw(w_ref[...], g, m_ref[...], v_ref[...])
        g_ref[...] = g
        d_ref[...] = delta
        nm_ref[...] = nm
        nv_ref[...] = nv

    shp = jax.ShapeDtypeStruct((rows, cols), F32)
    return pl.pallas_call(
        body,
        grid=grid,
        in_specs=lblk + [wblk, wblk, wblk],
        out_specs=[wblk, wblk, wblk, wblk],
        out_shape=[shp, shp, shp, shp],
        compiler_params=_params(1),
        name=name,
    )(*lands, w, m, v)


def _mesh_pos():
    return lax.axis_index("x"), lax.axis_index("y"), lax.axis_index("c")


def _peer(k):
    x, y, c = _mesh_pos()
    pos = (x ^ (k >> 2), y ^ ((k >> 1) & 1), c ^ (k & 1))
    return pos, 4 * pos[0] + 2 * pos[1] + pos[2]


def _sem_index(row, k):
    return row * (N_DEV - 1) + k - 1


def _window(ref, axis, start, size):
    align = 16 if axis == 0 else LANES
    start = pl.multiple_of(start, align)
    return ref.at[pl.ds(start, size), :] if axis == 0 else ref.at[:, pl.ds(start, size)]


def _piece_refs(piece, srcs, lands, me, peer):
    kind, si, li, axis, base, stride, shape = piece
    if kind == "gather":
        return srcs[si], _window(lands[li], axis, base + stride * me, shape[axis])
    return _window(srcs[si], axis, base + stride * peer, shape[axis]), lands[li].at[me]


def _place_own(srcs, land_shapes, pieces, name):
    ns = len(srcs)
    nl = len(land_shapes)

    def body(*refs):
        s_refs = refs[:ns]
        l_refs = refs[ns:ns + nl]
        sems = refs[ns + nl]
        x, y, c = _mesh_pos()
        me = 4 * x + 2 * y + c
        copies = []
        for p, piece in enumerate(pieces):
            src, dst = _piece_refs(piece, s_refs, l_refs, me, me)
            cp = pltpu.make_async_copy(src, dst, sems.at[p])
            cp.start()
            copies.append(cp)
        for cp in copies:
            cp.wait()

    out = pl.pallas_call(
        body,
        in_specs=[ANY] * ns,
        out_specs=[ANY] * nl,
        out_shape=list(land_shapes),
        scratch_shapes=[pltpu.SemaphoreType.DMA((len(pieces),))],
        name=name,
    )(*srcs)
    return list(out)


def _comm_start(srcs, lands, pieces, groups, name, after=()):
    ns, nl, na, ng = len(srcs), len(lands), len(after), len(groups)

    def body(*refs):
        s_refs = refs[:ns]
        l_refs = refs[ns:ns + nl]
        outs = refs[ns + nl + na:]
        sems = outs[:2 * ng]
        token = outs[-1]
        x, y, c = _mesh_pos()
        me = 4 * x + 2 * y + c
        for g, members in enumerate(groups):
            for row, p in enumerate(members):
                for k in range(1, N_DEV):
                    pos, peer = _peer(k)
                    src, dst = _piece_refs(pieces[p], s_refs, l_refs, me, peer)
                    pltpu.make_async_remote_copy(src_ref=src, dst_ref=dst, send_sem=sems[2 * g].at[_sem_index(row, k)],
                                                 recv_sem=sems[2 * g + 1].at[_sem_index(row, k)], device_id=pos,
                                                 device_id_type=MESH_ID).start()
        token[...] = jnp.zeros_like(token)

    sem_shapes = []
    for members in groups:
        sem_shapes += [pltpu.SemaphoreType.DMA((len(members) * (N_DEV - 1),))] * 2
    hbm_of = lambda a: pltpu.HBM(a.shape, a.dtype)
    out = pl.pallas_call(
        body,
        in_specs=[HBM] * (ns + nl) + [ANY] * na,
        out_specs=[SEM] * (2 * ng) + [HBM] * (ns + nl) + [pl.BlockSpec(memory_space=pltpu.VMEM)],
        out_shape=sem_shapes + [hbm_of(a) for a in srcs] + [hbm_of(a) for a in lands]
        + [jax.ShapeDtypeStruct((8, LANES), F32)],
        input_output_aliases={i: 2 * ng + i for i in range(ns + nl)},
        compiler_params=pltpu.CompilerParams(has_side_effects=DATAFLOW),
        name=name,
    )(*[pltpu.with_memory_space_constraint(a, pltpu.HBM) for a in list(srcs) + list(lands)], *after)
    sems = [(out[2 * g], out[2 * g + 1]) for g in range(ng)]
    return sems, list(out[2 * ng:2 * ng + ns]), list(out[2 * ng + ns:2 * ng + ns + nl]), out[-1]


def _comm_wait(srcs, lands, pieces, members, sems, after, name):
    ns, nl, na = len(srcs), len(lands), len(after)

    def body(*refs):
        s_refs = refs[:ns]
        l_refs = refs[ns:ns + nl]
        send_sems, recv_sems = refs[ns + nl:ns + nl + 2]
        x, y, c = _mesh_pos()
        me = 4 * x + 2 * y + c
        for row, p in enumerate(members):
            for k in range(1, N_DEV):
                pos, peer = _peer(k)
                src, dst = _piece_refs(pieces[p], s_refs, l_refs, me, peer)
                cp = pltpu.make_async_remote_copy(src_ref=src, dst_ref=dst, send_sem=send_sems.at[_sem_index(row, k)],
                                                  recv_sem=recv_sems.at[_sem_index(row, k)], device_id=pos,
                                                  device_id_type=MESH_ID)
                cp.wait_send()
                cp.wait_recv()

    hbm_of = lambda a: pltpu.HBM(a.shape, a.dtype)
    out = pl.pallas_call(
        body,
        in_specs=[HBM] * (ns + nl) + [SEM, SEM] + [ANY] * na,
        out_specs=[HBM] * (ns + nl),
        out_shape=[hbm_of(a) for a in srcs] + [hbm_of(a) for a in lands],
        input_output_aliases={i: i for i in range(ns + nl)},
        compiler_params=pltpu.CompilerParams(has_side_effects=DATAFLOW),
        name=name,
    )(*srcs, *lands, sems[0], sems[1], *after)
    return list(out[ns:])


_SMALL_NAMES = ("ln1_g", "ln1_b", "hg_lb_logits", "hg_norm_g", "sg_ln_g", "sg_ln_b", "sg_w_s", "sg_b_s",
                "ln2_g", "ln2_b", "mem_ln_g", "mem_ln_b", "ln3_g", "ln3_b", "ln4_g", "ln4_b")


def _pack_small(tree):
    flat = jnp.concatenate([tree[k].reshape(-1).astype(F32) for k in _SMALL_NAMES])
    rows = flat.shape[0] // LANES
    return jnp.pad(flat.reshape(rows, LANES), ((0, SMALL_ROWS - rows), (0, 0)))


def _unpack_small(packed, like):
    flat = packed.reshape(-1)
    out = {}
    pos = 0
    for k in _SMALL_NAMES:
        size = like[k].size
        out[k] = flat[pos:pos + size].reshape(like[k].shape)
        pos += size
    return out


def _adam_small(gathered, w, m, v):
    def body(p_ref, w_ref, m_ref, v_ref, g_ref, d_ref, nm_ref, nv_ref):
        g = _slot_sum(p_ref)
        delta, nm, nv = _adamw(w_ref[...], g, m_ref[...], v_ref[...])
        g_ref[...] = g
        d_ref[...] = delta
        nm_ref[...] = nm
        nv_ref[...] = nv

    shp = jax.ShapeDtypeStruct(w.shape, F32)
    return pl.pallas_call(body, out_shape=[shp, shp, shp, shp], name="adam_small")(gathered, w, m, v)


_COL_FFN = ("ffn1_w_gate", "ffn1_w_up", "ffn2_w_gate", "ffn2_w_up")
_ROW_FFN = ("ffn1_w_down", "ffn2_w_down")
_ROW_SQ = ("w_out", "xa_w_q", "xa_w_k", "xa_w_v", "xa_w_o")
_BIG_NAMES = ("ffn1_w_gate", "ffn1_w_up", "ffn1_w_down", "w_in", "w_out", "xa_w_q", "xa_w_k", "xa_w_v", "xa_w_o",
              "ffn2_w_gate", "ffn2_w_up", "ffn2_w_down")


def _ffn_split(fs):
    main = (fs // MXU_WIDTH_V7X) * MXU_WIDTH_V7X
    tail = fs - main
    tail_pad = -(-tail // LANES) * LANES
    assert main > 0 and tail > 0
    return main, tail, tail_pad


def _layout(name, shard_shape):
    r, c = shard_shape
    if name in _COL_FFN:
        main, tail, pad = _ffn_split(c)
        return (r, N_DEV * (main + pad)), [(1, 0, main, (r, main), (0, main)),
                                           (1, N_DEV * main, pad, (r, pad), (main, c))]
    if name in _ROW_FFN:
        main, tail, pad = _ffn_split(r)
        return (N_DEV * (main + pad), c), [(0, 0, main, (main, c), (0, main)),
                                           (0, N_DEV * main, pad, (pad, c), (main, r))]
    if name == "w_in":
        return (r, N_DEV * c), [(1, 0, c, (r, c), (0, c))]
    return (N_DEV * r, c), [(0, 0, r, (r, c), (0, r))]


def _shard_pieces(name, shard):
    out = []
    for axis, _, _, shape, (lo, hi) in _layout(name, shard.shape)[1]:
        part = shard[lo:hi, :] if axis == 0 else shard[:, lo:hi]
        pad = [(0, shape[0] - part.shape[0]), (0, shape[1] - part.shape[1])]
        out.append(jnp.pad(part, pad).astype(BF16))
    return out


def _gather_plan(names, shards):
    srcs, land_shapes, pieces, index = [], [], [], {}
    for li, name in enumerate(names):
        shape2d, parts = _layout(name, shards[name].shape)
        land_shapes.append(jax.ShapeDtypeStruct(shape2d, BF16))
        index[name] = []
        for (axis, base, stride, shape, _), src in zip(parts, _shard_pieces(name, shards[name])):
            index[name].append(len(pieces))
            pieces.append(("gather", len(srcs), li, axis, base, stride, shape))
            srcs.append(src)
    return srcs, land_shapes, pieces, index


def _scatter_plan(names, grads, shard_shapes):
    srcs, land_shapes, pieces, index = [], [], [], {}
    for si, name in enumerate(names):
        _, parts = _layout(name, shard_shapes[name])
        srcs.append(grads[name])
        index[name] = []
        for axis, base, stride, shape, _ in parts:
            index[name].append(len(land_shapes))
            pieces.append(("scatter", si, len(land_shapes), axis, base, stride, shape))
            land_shapes.append(jax.ShapeDtypeStruct((N_DEV,) + shape, grads[name].dtype))
    return srcs, land_shapes, pieces, index


def _small_views(small):
    row = lambda a: a.reshape(1, -1)
    ln = {k: row(small[k]) for k in ("ln1_g", "ln1_b", "ln2_g", "ln2_b", "ln3_g", "ln3_b", "ln4_g", "ln4_b",
                                      "mem_ln_g", "mem_ln_b", "hg_norm_g")}
    sg_w = small["sg_w_s"].reshape(SG_GROUPS, SG_CHUNK, SG_CHUNK)
    sg = dict(logits=jnp.swapaxes(small["hg_lb_logits"], 0, 1),
              g=small["sg_ln_g"].reshape(SG_GROUPS, 1, SG_DIM), b=small["sg_ln_b"].reshape(SG_GROUPS, 1, SG_DIM),
              w=sg_w, wt=jnp.swapaxes(sg_w, 1, 2), bs=small["sg_b_s"].reshape(SG_GROUPS, SG_CHUNK, 1))
    return ln, sg


def _forward(x, mem, get_w, small, first_deps=()):
    ln, sg = _small_views(small)
    xb = x.astype(BF16)
    a1, b1, s1 = _ffn_up(xb, get_w("ffn1_w_gate", ()), get_w("ffn1_w_up", ()), "ffn1_up", deps=first_deps)
    h1, h1b, xh1, rs1 = _mm_res_ln(s1, get_w("ffn1_w_down", (s1,)), x, ln["ln1_g"], ln["ln1_b"], 0.5, "ffn1_down_ln")
    proj = _mm_nn(h1b, get_w("w_in", (s1,)), "mix_in")
    oraw, mix, states = _hgrn_fwd(proj, sg["logits"], ln["hg_norm_g"])
    mix = _sgu_fwd(proj, mix, sg["g"], sg["b"], sg["w"], sg["bs"])
    h2, h2b, xh2, rs2 = _mm_res_ln(mix, get_w("w_out", (s1,)), h1, ln["ln2_g"], ln["ln2_b"], 1.0, "mix_out_ln")
    mb, mxh, mrs, kb, vb = _mem_kv(mem, ln["mem_ln_g"], ln["mem_ln_b"], get_w("xa_w_k", (h2b,)), get_w("xa_w_v", (h2b,)))
    qb, att = _attn_fwd(h2b, get_w("xa_w_q", (h2b,)), kb, vb)
    h3, h3b, xh3, rs3 = _mm_res_ln(att, get_w("xa_w_o", (h2b,)), h2, ln["ln3_g"], ln["ln3_b"], 1.0, "attn_out_ln")
    a2, b2, s2 = _ffn_up(h3b, get_w("ffn2_w_gate", (h2b,)), get_w("ffn2_w_up", (h2b,)), "ffn2_up")
    h4, _, xh4, rs4 = _mm_res_ln(s2, get_w("ffn2_w_down", (h2b,)), h3, ln["ln4_g"], ln["ln4_b"], 0.5, "ffn2_down_ln")
    return dict(xb=xb, a1=a1, b1=b1, s1=s1, h1b=h1b, xh1=xh1, rs1=rs1, proj=proj, oraw=oraw, mix=mix, states=states,
                h2b=h2b, xh2=xh2, rs2=rs2, mb=mb, mxh=mxh, mrs=mrs, kb=kb, vb=vb, qb=qb, att=att, h3b=h3b, xh3=xh3,
                rs3=rs3, a2=a2, b2=b2, s2=s2, h4=h4, xh4=xh4, rs4=rs4)


def _backward(sv, target, wt, small, send):
    ln, sg = _small_views(small)
    gs = {}
    loss, dy4, dy4b, gs["ln4_g"], gs["ln4_b"] = _loss_ln_bwd(sv["h4"], sv["xh4"], sv["rs4"], ln["ln4_g"], target)
    g_down2 = _mm_tn(sv["s2"], dy4b, "g_ffn2_down", scale=0.5)
    da2, db2 = _ffn_bwd_act(dy4b, wt["ffn2_w_down"], sv["a2"], sv["b2"], 0.5, "ffn2_bwd_act")
    g_gate2 = _mm_tn(sv["h3b"], da2, "g_ffn2_gate")
    g_up2 = _mm_tn(sv["h3b"], db2, "g_ffn2_up")
    tok = send(("ffn2_w_down", "ffn2_w_gate", "ffn2_w_up"), (g_down2, g_gate2, g_up2))
    dy3, dy3b, gs["ln3_g"], gs["ln3_b"] = _dx_ln(dy4, [(da2, wt["ffn2_w_gate"]), (db2, wt["ffn2_w_up"])],
                                                 (sv["xh3"], sv["rs3"], ln["ln3_g"]), "ffn2_dx_ln", deps=(tok,))

    g_o = _mm_tn(sv["att"], dy3b, "g_xa_o")
    dqb, dk, dv = _attn_bwd(dy3b, wt["xa_w_o"], sv["qb"], sv["kb"], sv["vb"])
    g_q = _mm_tn(sv["h2b"], dqb, "g_xa_q")
    g_k, g_v, gs["mem_ln_g"], gs["mem_ln_b"] = _mem_bwd(dk, dv, sv["mb"], sv["mxh"], sv["mrs"], ln["mem_ln_g"],
                                                        wt["xa_w_k"], wt["xa_w_v"])
    tok = send(("xa_w_o", "xa_w_q", "xa_w_k", "xa_w_v"), (g_o, g_q, g_k, g_v))
    dy2, dy2b, gs["ln2_g"], gs["ln2_b"] = _dx_ln(dy3, [(dqb, wt["xa_w_q"])], (sv["xh2"], sv["rs2"], ln["ln2_g"]),
                                                 "attn_dx_ln", deps=(tok,))

    g_out = _mm_tn(sv["mix"], dy2b, "g_w_out")
    dmix = _mm_nt(dy2b, wt["w_out"], "mix_out_bwd")
    dq, dfz, div, dgg, dlg, dgn = _hgrn_bwd(sv["proj"], sv["oraw"], dmix, sv["states"], sg["logits"], ln["hg_norm_g"])
    du, dvv, gs["sg_ln_g"], gs["sg_ln_b"], gs["sg_w_s"], gs["sg_b_s"] = _sgu_bwd(
        sv["proj"], dmix, sg["g"], sg["b"], sg["w"], sg["wt"], sg["bs"])
    gs["hg_lb_logits"] = jnp.swapaxes(dlg, 0, 1)
    gs["hg_norm_g"] = jnp.sum(dgn, axis=0)
    dproj = jnp.concatenate([dq, dfz, div, dgg, du, dvv], axis=1)
    g_in = _mm_tn(sv["h1b"], dproj, "g_w_in")
    tok = send(("w_out", "w_in"), (g_out, g_in))
    dy1, dy1b, gs["ln1_g"], gs["ln1_b"] = _dx_ln(dy2, [(dproj, wt["w_in"])], (sv["xh1"], sv["rs1"], ln["ln1_g"]),
                                                 "mix_dx_ln", deps=(tok,))

    g_down1 = _mm_tn(sv["s1"], dy1b, "g_ffn1_down", scale=0.5)
    da1, db1 = _ffn_bwd_act(dy1b, wt["ffn1_w_down"], sv["a1"], sv["b1"], 0.5, "ffn1_bwd_act")
    g_gate1 = _mm_tn(sv["xb"], da1, "g_ffn1_gate")
    g_up1 = _mm_tn(sv["xb"], db1, "g_ffn1_up")
    tok = send(("ffn1_w_down", "ffn1_w_gate", "ffn1_w_up"), (g_down1, g_gate1, g_up1))
    grad_x = _dx_ln(dy1, [(da1, wt["ffn1_w_gate"]), (db1, wt["ffn1_w_up"])], None, "ffn1_dx", deps=(tok,))
    return loss, grad_x, gs


_WEIGHT_NAMES = ("ffn1_w_gate", "ffn1_w_up", "ffn1_w_down", "ln1_g", "ln1_b", "w_in", "hg_lb_logits", "hg_norm_g",
                 "sg_ln_g", "sg_ln_b", "sg_w_s", "sg_b_s", "w_out", "ln2_g", "ln2_b", "mem_ln_g", "mem_ln_b",
                 "xa_w_q", "xa_w_k", "xa_w_v", "xa_w_o", "ln3_g", "ln3_b", "ffn2_w_gate", "ffn2_w_up", "ffn2_w_down",
                 "ln4_g", "ln4_b")
_FIRST = ("ffn1_w_gate", "ffn1_w_up")
_SECOND = ("ffn1_w_down", "w_in", "w_out")
_THIRD = ("xa_w_k", "xa_w_v", "xa_w_q", "xa_w_o", "ffn2_w_gate", "ffn2_w_up", "ffn2_w_down")


def kernel(x, mem, ffn1_w_gate, ffn1_w_up, ffn1_w_down, ln1_g, ln1_b, w_in, hg_lb_logits, hg_norm_g, sg_ln_g, sg_ln_b, sg_w_s, sg_b_s, w_out, ln2_g, ln2_b, mem_ln_g, mem_ln_b, xa_w_q, xa_w_k, xa_w_v, xa_w_o, ln3_g, ln3_b, ffn2_w_gate, ffn2_w_up, ffn2_w_down, ln4_g, ln4_b, loss_target, m_ffn1_w_gate, m_ffn1_w_up, m_ffn1_w_down, m_ln1_g, m_ln1_b, m_w_in, m_hg_lb_logits, m_hg_norm_g, m_sg_ln_g, m_sg_ln_b, m_sg_w_s, m_sg_b_s, m_w_out, m_ln2_g, m_ln2_b, m_mem_ln_g, m_mem_ln_b, m_xa_w_q, m_xa_w_k, m_xa_w_v, m_xa_w_o, m_ln3_g, m_ln3_b, m_ffn2_w_gate, m_ffn2_w_up, m_ffn2_w_down, m_ln4_g, m_ln4_b, v_ffn1_w_gate, v_ffn1_w_up, v_ffn1_w_down, v_ln1_g, v_ln1_b, v_w_in, v_hg_lb_logits, v_hg_norm_g, v_sg_ln_g, v_sg_ln_b, v_sg_w_s, v_sg_b_s, v_w_out, v_ln2_g, v_ln2_b, v_mem_ln_g, v_mem_ln_b, v_xa_w_q, v_xa_w_k, v_xa_w_v, v_xa_w_o, v_ln3_g, v_ln3_b, v_ffn2_w_gate, v_ffn2_w_up, v_ffn2_w_down, v_ln4_g, v_ln4_b):
    args = dict(locals())
    w = {k: args[k] for k in _WEIGHT_NAMES}
    m = {k: args["m_" + k] for k in _WEIGHT_NAMES}
    v = {k: args["v_" + k] for k in _WEIGHT_NAMES}
    shards = {k: w[k][0] for k in _BIG_NAMES}
    shard_shapes = {k: shards[k].shape for k in _BIG_NAMES}
    small = {k: (w[k][0] if k != "hg_lb_logits" else w[k]) for k in _SMALL_NAMES}

    srcs1, shapes1, pieces1, idx1 = _gather_plan(_FIRST, shards)
    lands1 = _place_own(srcs1, shapes1, pieces1, "gather_first_own")
    sems1, srcs1, lands1, _ = _comm_start(srcs1, lands1, pieces1, [list(range(len(pieces1)))], "gather_first_start")
    lands1 = _comm_wait(srcs1, lands1, pieces1, list(range(len(pieces1))), sems1[0], (), "gather_first_wait")
    wt = dict(zip(_FIRST, lands1))

    rest = _SECOND + _THIRD
    srcs2, shapes2, pieces2, idx2 = _gather_plan(rest, shards)
    lands2 = _place_own(srcs2, shapes2, pieces2, "gather_rest_own")
    groups2 = [[p for k in names for p in idx2[k]] for names in (_SECOND, _THIRD)]
    sems2, srcs2, lands2, tok2 = _comm_start(srcs2, lands2, pieces2, groups2, "gather_rest_start", after=tuple(lands1))
    pending = {}
    for gi, names in enumerate((_SECOND, _THIRD)):
        for k in names:
            pending[k] = gi

    def get_w(name, after):
        if name in pending:
            gi = pending[name]
            names = (_SECOND, _THIRD)[gi]
            li = [rest.index(k) for k in names]
            si = sorted({pieces2[p][1] for p in groups2[gi]})
            remap = {s: i for i, s in enumerate(si)}
            lmap = {l: i for i, l in enumerate(li)}
            sub = [(pc[0], remap[pc[1]], lmap[pc[2]]) + pc[3:] for pc in (pieces2[p] for p in groups2[gi])]
            got = _comm_wait([srcs2[s] for s in si], [lands2[l] for l in li], sub, list(range(len(sub))), sems2[gi],
                             after, "gather_rest_wait_%d" % gi)
            for k, arr in zip(names, got):
                wt[k] = arr
                del pending[k]
        return wt[name]

    sv = _forward(x[0], mem[0], get_w, small, first_deps=(tok2,))

    sent = []

    def send(names, grads):
        srcs, shapes, pieces, idx = _scatter_plan(names, dict(zip(names, grads)), shard_shapes)
        lands = _place_own(srcs, shapes, pieces, "grads_own_%d" % len(sent))
        sems, srcs, lands, tok = _comm_start(srcs, lands, pieces, [list(range(len(pieces)))],
                                             "grads_start_%d" % len(sent))
        sent.append((names, srcs, lands, pieces, idx, sems[0]))
        return tok

    loss, grad_x, gs = _backward(sv, loss_target[0], wt, small, send)

    small_like = {k: w[k] for k in _SMALL_NAMES}
    packed_g = _pack_small({k: gs[k].reshape(small_like[k].shape) for k in _SMALL_NAMES})
    small_src = jnp.broadcast_to(packed_g, (N_DEV, SMALL_ROWS, LANES)).reshape(N_DEV * SMALL_ROWS, LANES)
    sp = [("scatter", 0, 0, 0, 0, SMALL_ROWS, (SMALL_ROWS, LANES))]
    sshape = [jax.ShapeDtypeStruct((N_DEV, SMALL_ROWS, LANES), F32)]
    sl = _place_own([small_src], sshape, sp, "small_own")
    ssem, ssrc, sl, _ = _comm_start([small_src], sl, sp, [[0]], "small_start")

    out_g, out_d, out_m, out_v = {}, {}, {}, {}
    after = (grad_x,)
    for n_sent, (names, srcs, lands, pieces, idx, sems) in enumerate(sent):
        lands = _comm_wait(srcs, lands, pieces, list(range(len(pieces))), sems, after, "grads_wait_%d" % n_sent)
        for k in names:
            axis = 1 if (k in _COL_FFN or k == "w_in") else 0
            res = _adam_sharded([lands[i] for i in idx[k]], w[k][0], m[k][0], v[k][0], axis, "adam_" + k)
            out_g[k], out_d[k], out_m[k], out_v[k] = [r[None] for r in res]
        after = (out_v[names[-1]],)
    sl = _comm_wait(ssrc, sl, sp, [0], ssem[0], after, "small_wait")
    packed = _adam_small(sl[0], _pack_small(small_like), _pack_small({k: m[k] for k in _SMALL_NAMES}),
                         _pack_small({k: v[k] for k in _SMALL_NAMES}))
    for dst, pk in zip((out_g, out_d, out_m, out_v), packed):
        dst.update(_unpack_small(pk, small_like))

    loss_all = lax.psum(loss[0, 0], ("x", "y", "c"))
    return (loss_all, grad_x[None], *[out_g[k] for k in _WEIGHT_NAMES], *[out_d[k] for k in _WEIGHT_NAMES],
            *[out_m[k] for k in _WEIGHT_NAMES], *[out_v[k] for k in _WEIGHT_NAMES])
```

```python
import jax
import jax.numpy as jnp
from jax import lax
from jax.experimental import pallas as pl
from jax.experimental.pallas import tpu as pltpu

F32 = jnp.float32
BF16 = jnp.bfloat16

N_DEV = 8
ALPHA = 2.0 ** 0.25
LN_EPS = 1e-5
HG_HEADS = 4
HG_DIM = 128
SG_GROUPS = 4
SG_DIM = 128
SG_CHUNK = 128
X_HEADS = 4
HG_BLOCK = 16
ADAM_LR = 0.001
ADAM_B1 = 0.9
ADAM_B2 = 0.999
ADAM_EPS = 1e-08
ADAM_WD = 0.01
ADAM_STEP = 10
VMEM_LIMIT_V7X = 48 * 1024 * 1024
MXU_WIDTH_V7X = 256
LANES = 128
SMALL_ROWS = 624
MESH_ID = pl.DeviceIdType.MESH
ANY = pl.BlockSpec(memory_space=pl.ANY)
HBM = pl.BlockSpec(memory_space=pltpu.HBM)
SEM = pl.BlockSpec(memory_space=pltpu.SEMAPHORE)
DATAFLOW = pltpu.SideEffectType.DATAFLOW_SIDE_EFFECTING


def _params(n_axes):
    return pltpu.CompilerParams(dimension_semantics=("arbitrary",) * n_axes, vmem_limit_bytes=VMEM_LIMIT_V7X)


def _dot(a, b):
    return jnp.dot(a, b, preferred_element_type=F32)


def _dot_nt(a, b):
    return lax.dot_general(a, b, (((1,), (1,)), ((), ())), preferred_element_type=F32)


def _dot_tn(a, b):
    return lax.dot_general(a, b, (((0,), (0,)), ((), ())), preferred_element_type=F32)


def _sigmoid(x):
    return 1.0 / (1.0 + jnp.exp(-x))


def _silu_and_grad(a):
    sig = _sigmoid(a)
    return a * sig, sig * (1.0 + a * (1.0 - sig))


_GELU_C = 0.7978845608028654


def _gelu_and_grad(x):
    inner = _GELU_C * (x + 0.044715 * x * x * x)
    t = jnp.tanh(inner)
    val = 0.5 * x * (1.0 + t)
    grad = 0.5 * (1.0 + t) + 0.5 * x * (1.0 - t * t) * _GELU_C * (1.0 + 3.0 * 0.044715 * x * x)
    return val, grad


def _ln_fwd(y, g, b):
    mu = jnp.mean(y, axis=-1, keepdims=True)
    yc = y - mu
    var = jnp.mean(yc * yc, axis=-1, keepdims=True)
    rstd = lax.rsqrt(var + LN_EPS)
    xhat = yc * rstd
    return xhat * g + b, xhat, rstd


def _ln_bwd(dh, xhat, rstd, g):
    dxh = dh * g
    m1 = jnp.mean(dxh, axis=-1, keepdims=True)
    m2 = jnp.mean(dxh * xhat, axis=-1, keepdims=True)
    dy = rstd * (dxh - m1 - xhat * m2)
    dg = jnp.sum(dh * xhat, axis=0, keepdims=True)
    db = jnp.sum(dh, axis=0, keepdims=True)
    return dy, dg, db


def _split3(x):
    hi = x.astype(BF16)
    r1 = x - hi.astype(F32)
    mid = r1.astype(BF16)
    lo = (r1 - mid.astype(F32)).astype(BF16)
    return hi, mid, lo


def _mask_dot(mask, x):
    hi, mid, lo = _split3(x)
    return _dot(mask, hi) + _dot(mask, mid) + _dot(mask, lo)


def _block_masks(n):
    r = lax.broadcasted_iota(jnp.int32, (n, n), 0)
    c = lax.broadcasted_iota(jnp.int32, (n, n), 1)
    assert HG_BLOCK & (HG_BLOCK - 1) == 0
    same = (r & -HG_BLOCK) == (c & -HG_BLOCK)
    one = jnp.ones((n, n), BF16)
    zero = jnp.zeros((n, n), BF16)
    lower = jnp.where(same & (c <= r), one, zero)
    upper = jnp.where(same & (c >= r), one, zero)
    whole = jnp.where(same, one, zero)
    return lower, upper, whole


def _row_tile(t):
    return min(t, 512)


def _col_tile(n):
    for cand in (512, 256, 128):
        if n % cand == 0:
            return cand
    return n


def _drop_deps(body, n_in, n_deps):
    if n_deps == 0:
        return body
    return lambda *refs: body(*refs[:n_in], *refs[n_in + n_deps:])


def _ffn_up(hb, wg, wu, name, deps=()):
    t, d = hb.shape
    f = wg.shape[1]
    tm = _row_tile(t)
    tn = _col_tile(f)

    def body(h_ref, wg_ref, wu_ref, a_ref, b_ref, s_ref):
        h = h_ref[...]
        a = _dot(h, wg_ref[...])
        b = _dot(h, wu_ref[...])
        a_ref[...] = a
        b_ref[...] = b
        s_ref[...] = (a * _sigmoid(a) * b).astype(BF16)

    act = pl.BlockSpec((tm, tn), lambda j, i: (i, j))
    wsp = pl.BlockSpec((d, tn), lambda j, i: (0, j))
    return pl.pallas_call(
        _drop_deps(body, 3, len(deps)),
        grid=(f // tn, t // tm),
        in_specs=[pl.BlockSpec((tm, d), lambda j, i: (i, 0)), wsp, wsp] + [ANY] * len(deps),
        out_specs=[act, act, act],
        out_shape=[jax.ShapeDtypeStruct((t, f), F32), jax.ShapeDtypeStruct((t, f), F32),
                   jax.ShapeDtypeStruct((t, f), BF16)],
        compiler_params=_params(2),
        name=name,
    )(hb, wg, wu, *deps)


def _mm_res_ln(lhs, w, res, g, b, coef, name):
    t, kd = lhs.shape
    d = w.shape[1]
    tm = _row_tile(t)
    tk = _col_tile(kd)
    nk = kd // tk

    def body(l_ref, w_ref, r_ref, g_ref, b_ref, h_ref, hb_ref, xh_ref, rs_ref, acc):
        k = pl.program_id(1)

        @pl.when(k == 0)
        def _():
            acc[...] = jnp.zeros_like(acc)

        acc[...] += _dot(l_ref[...], w_ref[...])

        @pl.when(k == nk - 1)
        def _():
            y = ALPHA * r_ref[...] + coef * acc[...]
            h, xhat, rstd = _ln_fwd(y, g_ref[...], b_ref[...])
            h_ref[...] = h
            hb_ref[...] = h.astype(BF16)
            xh_ref[...] = xhat
            rs_ref[...] = rstd

    row = pl.BlockSpec((tm, d), lambda i, k: (i, 0))
    vec = pl.BlockSpec((1, d), lambda i, k: (0, 0))
    return pl.pallas_call(
        body,
        grid=(t // tm, nk),
        in_specs=[pl.BlockSpec((tm, tk), lambda i, k: (i, k)), pl.BlockSpec((tk, d), lambda i, k: (k, 0)),
                  row, vec, vec],
        out_specs=[row, row, row, pl.BlockSpec((tm, 1), lambda i, k: (i, 0))],
        out_shape=[jax.ShapeDtypeStruct((t, d), F32), jax.ShapeDtypeStruct((t, d), BF16),
                   jax.ShapeDtypeStruct((t, d), F32), jax.ShapeDtypeStruct((t, 1), F32)],
        scratch_shapes=[pltpu.VMEM((tm, d), F32)],
        compiler_params=_params(2),
        name=name,
    )(lhs, w, res, g, b)


def _mm_nn(lhs, w, name):
    t, kd = lhs.shape
    n = w.shape[1]
    tm = _row_tile(t)
    tn = _col_tile(n)

    def body(l_ref, w_ref, o_ref):
        o_ref[...] = _dot(l_ref[...], w_ref[...])

    return pl.pallas_call(
        body,
        grid=(n // tn, t // tm),
        in_specs=[pl.BlockSpec((tm, kd), lambda j, i: (i, 0)), pl.BlockSpec((kd, tn), lambda j, i: (0, j))],
        out_specs=pl.BlockSpec((tm, tn), lambda j, i: (i, j)),
        out_shape=jax.ShapeDtypeStruct((t, n), F32),
        compiler_params=_params(2),
        name=name,
    )(lhs, w)


def _lower_bound(lg):
    m = jnp.max(lg, axis=0, keepdims=True)
    e = jnp.exp(lg - m)
    return e[0:1, :] / jnp.sum(e, axis=0, keepdims=True)


def _forget_terms(fz, lb):
    e = jnp.exp(-jnp.abs(fz))
    r = 1.0 / (1.0 + e)
    pos = fz >= 0.0
    sig = jnp.where(pos, r, e * r)
    nsig = jnp.where(pos, e * r, r)
    f = lb + (1.0 - lb) * sig
    k = (1.0 - lb) * nsig
    return sig, nsig, f, k


def _hg_tile(t):
    return min(t, 256)


def _hgrn_fwd(proj, logits, gn):
    t = proj.shape[0]
    ct = _hg_tile(t)
    nct = t // ct
    nblk = ct // HG_BLOCK
    nh = HG_HEADS

    def body(q_ref, fz_ref, iv_ref, gg_ref, lg_ref, gn_ref, oraw_ref, oa_ref, st_ref,
             state, qt_s, kt_s, k_s, b_s, dec_s):
        c = pl.program_id(1)

        @pl.when(c == 0)
        def _():
            state[...] = jnp.zeros_like(state)

        lb = _lower_bound(lg_ref[...])
        q = q_ref[...]
        _, _, f, k = _forget_terms(fz_ref[...], lb)
        logf = jnp.log(f)
        lower, _, whole = _block_masks(ct)
        b = _mask_dot(lower, logf)
        bend = _mask_dot(whole, logf)
        qt_s[...] = (q * jnp.exp(b)).astype(BF16)
        kt_s[...] = (k * jnp.exp(bend - b)).astype(BF16)
        k_s[...] = k
        b_s[...] = b
        dec_s[...] = jnp.exp(bend)
        tidx = lax.broadcasted_iota(jnp.int32, (HG_BLOCK, HG_DIM), 0)

        def blk(i, carry):
            r0 = pl.multiple_of(i * HG_BLOCK, HG_BLOCK)
            rows = pl.ds(r0, HG_BLOCK)
            st = state[...]
            st_ref[i] = st
            v = iv_ref[rows, :]
            qq = q_ref[rows, :]
            kk = k_s[rows, :]
            bb = b_s[rows, :]
            o = _dot_nt(qt_s[rows, :], st.astype(BF16))
            for s in range(HG_BLOCK):
                e = jnp.where(tidx >= s, jnp.exp(jnp.minimum(bb - bb[s:s + 1, :], 0.0)), 0.0)
                acol = jnp.sum(qq * kk[s:s + 1, :] * e, axis=1, keepdims=True)
                o = o + acol * v[s:s + 1, :]
            oraw_ref[rows, :] = o
            state[...] = st * dec_s[pl.ds(r0, 1), :] + _dot_tn(v.astype(BF16), kt_s[rows, :])
            return carry

        lax.fori_loop(0, nblk, blk, 0)
        oraw = oraw_ref[...]
        r = lax.rsqrt(jnp.mean(oraw * oraw, axis=-1, keepdims=True) + LN_EPS)
        gg = gg_ref[...]
        oa_ref[...] = (oraw * r * gn_ref[...] * gg * _sigmoid(gg)).astype(BF16)

    def slab(off):
        return pl.BlockSpec((ct, HG_DIM), lambda h, c: (c, off + h))

    out_slab = pl.BlockSpec((ct, HG_DIM), lambda h, c: (c, h))
    return pl.pallas_call(
        body,
        grid=(nh, nct),
        in_specs=[slab(0), slab(nh), slab(2 * nh), slab(3 * nh),
                  pl.BlockSpec((None, 2, HG_DIM), lambda h, c: (h, 0, 0)),
                  pl.BlockSpec((1, HG_DIM), lambda h, c: (0, 0))],
        out_specs=[out_slab, out_slab, pl.BlockSpec((None, nblk, HG_DIM, HG_DIM), lambda h, c: (h, c, 0, 0))],
        out_shape=[jax.ShapeDtypeStruct((t, nh * HG_DIM), F32),
                   jax.ShapeDtypeStruct((t, (nh + SG_GROUPS) * HG_DIM), BF16),
                   jax.ShapeDtypeStruct((nh, t // HG_BLOCK, HG_DIM, HG_DIM), F32)],
        scratch_shapes=[pltpu.VMEM((HG_DIM, HG_DIM), F32), pltpu.VMEM((ct, HG_DIM), BF16),
                        pltpu.VMEM((ct, HG_DIM), BF16), pltpu.VMEM((ct, HG_DIM), F32),
                        pltpu.VMEM((ct, HG_DIM), F32), pltpu.VMEM((ct, HG_DIM), F32)],
        compiler_params=_params(2),
        name="hgrn_fwd",
    )(proj, proj, proj, proj, logits, gn)


def _sg_tile(t):
    return min(t, 512)


def _sgu_chunk_fwd(u, v, ln_g, ln_b, wm, bs):
    ua, dua = _gelu_and_grad(u)
    va, dva = _gelu_and_grad(v)
    vn, xhat, rstd = _ln_fwd(va, ln_g, ln_b)
    s = _dot(wm, vn.astype(BF16)) + bs
    return ua, dua, dva, vn, xhat, rstd, s


def _tril_weight(w_ref):
    n = SG_CHUNK
    r = lax.broadcasted_iota(jnp.int32, (n, n), 0)
    c = lax.broadcasted_iota(jnp.int32, (n, n), 1)
    return jnp.where(c <= r, w_ref[...], 0.0)


def _sgu_fwd(proj, mix, ln_g, ln_b, w_s, b_col):
    t = proj.shape[0]
    ct = _sg_tile(t)
    ng = SG_GROUPS
    off_u = 4 * HG_HEADS
    off_v = off_u + ng

    def body(u_ref, v_ref, g_ref, b_ref, w_ref, bs_ref, mix_ref, o_ref):
        del mix_ref
        wm = _tril_weight(w_ref).astype(BF16)
        for n in range(ct // SG_CHUNK):
            rows = slice(n * SG_CHUNK, (n + 1) * SG_CHUNK)
            ua, _, _, _, _, _, s = _sgu_chunk_fwd(u_ref[rows, :], v_ref[rows, :], g_ref[...], b_ref[...], wm, bs_ref[...])
            o_ref[rows, :] = (ua * s).astype(BF16)

    vec = pl.BlockSpec((None, 1, SG_DIM), lambda g, c: (g, 0, 0))
    return pl.pallas_call(
        body,
        grid=(ng, t // ct),
        in_specs=[pl.BlockSpec((ct, SG_DIM), lambda g, c: (c, off_u + g)),
                  pl.BlockSpec((ct, SG_DIM), lambda g, c: (c, off_v + g)), vec, vec,
                  pl.BlockSpec((None, SG_CHUNK, SG_CHUNK), lambda g, c: (g, 0, 0)),
                  pl.BlockSpec((None, SG_CHUNK, 1), lambda g, c: (g, 0, 0)), ANY],
        out_specs=pl.BlockSpec((ct, SG_DIM), lambda g, c: (c, HG_HEADS + g)),
        out_shape=jax.ShapeDtypeStruct(mix.shape, mix.dtype),
        input_output_aliases={6: 0},
        compiler_params=_params(2),
        name="sgu_fwd",
    )(proj, proj, ln_g, ln_b, w_s, b_col, mix)


def _mem_kv(mem, g, b, wk, wv):
    m_len, d = mem.shape

    def body(m_ref, g_ref, b_ref, wk_ref, wv_ref, mb_ref, xh_ref, rs_ref, k_ref, v_ref):
        m, xhat, rstd = _ln_fwd(m_ref[...], g_ref[...], b_ref[...])
        mb = m.astype(BF16)
        mb_ref[...] = mb
        xh_ref[...] = xhat
        rs_ref[...] = rstd
        k_ref[...] = _dot(mb, wk_ref[...]).astype(BF16)
        v_ref[...] = _dot(mb, wv_ref[...]).astype(BF16)

    return pl.pallas_call(
        body,
        out_shape=[jax.ShapeDtypeStruct((m_len, d), BF16), jax.ShapeDtypeStruct((m_len, d), F32),
                   jax.ShapeDtypeStruct((m_len, 1), F32), jax.ShapeDtypeStruct((m_len, d), BF16),
                   jax.ShapeDtypeStruct((m_len, d), BF16)],
        compiler_params=pltpu.CompilerParams(vmem_limit_bytes=VMEM_LIMIT_V7X),
        name="mem_kv",
    )(mem, g, b, wk, wv)


def _softmax_rows(s):
    m = jnp.max(s, axis=-1, keepdims=True)
    p = jnp.exp(s - m)
    return p / jnp.sum(p, axis=-1, keepdims=True)


def _attn_fwd(hb, wq, kb, vb):
    t, d = hb.shape
    tm = _row_tile(t)
    dh = d // X_HEADS
    scale = dh ** -0.5

    def body(h_ref, wq_ref, k_ref, v_ref, q_ref, o_ref):
        q = _dot(h_ref[...], wq_ref[...]).astype(BF16)
        q_ref[...] = q
        for hd in range(X_HEADS):
            sl = slice(hd * dh, (hd + 1) * dh)
            p = _softmax_rows(_dot_nt(q[:, sl], k_ref[:, sl]) * scale)
            o_ref[:, sl] = _dot(p.astype(BF16), v_ref[:, sl]).astype(BF16)

    row = pl.BlockSpec((tm, d), lambda i: (i, 0))
    full = lambda a: pl.BlockSpec(a.shape, lambda i: (0, 0))
    return pl.pallas_call(
        body,
        grid=(t // tm,),
        in_specs=[row, full(wq), full(kb), full(vb)],
        out_specs=[row, row],
        out_shape=[jax.ShapeDtypeStruct((t, d), BF16), jax.ShapeDtypeStruct((t, d), BF16)],
        compiler_params=_params(1),
        name="attn_fwd",
    )(hb, wq, kb, vb)


def _loss_ln_bwd(h, xhat, rstd, g, target):
    t, d = h.shape
    tm = _row_tile(t)
    nt = t // tm

    def body(h_ref, xh_ref, rs_ref, g_ref, t_ref, loss_ref, dy_ref, dyb_ref, dg_ref, db_ref, lacc):
        i = pl.program_id(0)

        @pl.when(i == 0)
        def _():
            lacc[...] = jnp.zeros_like(lacc)
            dg_ref[...] = jnp.zeros_like(dg_ref)
            db_ref[...] = jnp.zeros_like(db_ref)

        err = h_ref[...] - t_ref[...]
        lacc[...] += jnp.sum(err * err, axis=0, keepdims=True)
        dy, dg, db = _ln_bwd(err * (1.0 / d), xh_ref[...], rs_ref[...], g_ref[...])
        dy_ref[...] = dy
        dyb_ref[...] = dy.astype(BF16)
        dg_ref[...] += dg
        db_ref[...] += db

        @pl.when(i == nt - 1)
        def _():
            loss_ref[...] = jnp.zeros_like(loss_ref) + jnp.sum(lacc[...], axis=1, keepdims=True) * (0.5 / d)

    row = pl.BlockSpec((tm, d), lambda i: (i, 0))
    vec = pl.BlockSpec((1, d), lambda i: (0, 0))
    return pl.pallas_call(
        body,
        grid=(nt,),
        in_specs=[row, row, pl.BlockSpec((tm, 1), lambda i: (i, 0)), vec, row],
        out_specs=[pl.BlockSpec((1, LANES), lambda i: (0, 0)), row, row, vec, vec],
        out_shape=[jax.ShapeDtypeStruct((1, LANES), F32), jax.ShapeDtypeStruct((t, d), F32),
                   jax.ShapeDtypeStruct((t, d), BF16), jax.ShapeDtypeStruct((1, d), F32),
                   jax.ShapeDtypeStruct((1, d), F32)],
        scratch_shapes=[pltpu.VMEM((1, d), F32)],
        compiler_params=_params(1),
        name="loss_ln_bwd",
    )(h, xhat, rstd, g, target)


def _ffn_bwd_act(dyb, wd, a, b, coef, name, deps=()):
    t, d = dyb.shape
    f = wd.shape[0]
    tm = _row_tile(t)
    tn = _col_tile(f)

    def body(dy_ref, wd_ref, a_ref, b_ref, da_ref, db_ref):
        ds = _dot_nt(dy_ref[...], wd_ref[...]) * coef
        silu, dsilu = _silu_and_grad(a_ref[...])
        da_ref[...] = (ds * b_ref[...] * dsilu).astype(BF16)
        db_ref[...] = (ds * silu).astype(BF16)

    act = pl.BlockSpec((tm, tn), lambda j, i: (i, j))
    return pl.pallas_call(
        _drop_deps(body, 4, len(deps)),
        grid=(f // tn, t // tm),
        in_specs=[pl.BlockSpec((tm, d), lambda j, i: (i, 0)), pl.BlockSpec((tn, d), lambda j, i: (j, 0)),
                  act, act] + [ANY] * len(deps),
        out_specs=[act, act],
        out_shape=[jax.ShapeDtypeStruct((t, f), BF16), jax.ShapeDtypeStruct((t, f), BF16)],
        compiler_params=_params(2),
        name=name,
    )(dyb, wd, a, b, *deps)


def _mm_tn(a, b, name, scale=1.0, deps=()):
    t, m = a.shape
    n = b.shape[1]
    tt = _row_tile(t)
    nt = t // tt
    tm_o = _col_tile(m) if m >= n else m
    tn_o = n if m >= n else _col_tile(n)

    def body(a_ref, b_ref, o_ref, acc):
        k = pl.program_id(2)

        @pl.when(k == 0)
        def _():
            acc[...] = jnp.zeros_like(acc)

        acc[...] += _dot_tn(a_ref[...], b_ref[...])

        @pl.when(k == nt - 1)
        def _():
            o_ref[...] = (acc[...] * scale).astype(BF16)

    return pl.pallas_call(
        _drop_deps(body, 2, len(deps)),
        grid=(m // tm_o, n // tn_o, nt),
        in_specs=[pl.BlockSpec((tt, tm_o), lambda i, j, k: (k, i)), pl.BlockSpec((tt, tn_o), lambda i, j, k: (k, j))]
        + [ANY] * len(deps),
        out_specs=pl.BlockSpec((tm_o, tn_o), lambda i, j, k: (i, j)),
        out_shape=jax.ShapeDtypeStruct((m, n), BF16),
        scratch_shapes=[pltpu.VMEM((tm_o, tn_o), F32)],
        compiler_params=_params(3),
        name=name,
    )(a, b, *deps)


def _mm_nt(lhs, w, name):
    t, d = lhs.shape
    kd = w.shape[0]
    tm = _row_tile(t)
    tk = _col_tile(kd)

    def body(l_ref, w_ref, o_ref):
        o_ref[...] = _dot_nt(l_ref[...], w_ref[...])

    return pl.pallas_call(
        body,
        grid=(kd // tk, t // tm),
        in_specs=[pl.BlockSpec((tm, d), lambda j, i: (i, 0)), pl.BlockSpec((tk, d), lambda j, i: (j, 0))],
        out_specs=pl.BlockSpec((tm, tk), lambda j, i: (i, j)),
        out_shape=jax.ShapeDtypeStruct((t, kd), F32),
        compiler_params=_params(2),
        name=name,
    )(lhs, w)


def _dx_ln(dy, pairs, ln, name, deps=()):
    t, d = dy.shape
    tm = _row_tile(t)
    nt = t // tm
    kd = pairs[0][0].shape[1]
    tk = _col_tile(kd)
    nk = kd // tk
    npair = len(pairs)
    n_in = 1 + 2 * npair + (3 if ln is not None else 0)

    def body(*refs):
        dy_ref = refs[0]
        pr = refs[1:1 + 2 * npair]
        pos = 1 + 2 * npair
        if ln is not None:
            xh_ref, rs_ref, g_ref = refs[pos:pos + 3]
            pos += 3
            dyo_ref, dyb_ref, dg_ref, db_ref = refs[pos:pos + 4]
            acc = refs[pos + 4]
        else:
            dh_ref = refs[pos]
            acc = refs[pos + 1]
        i = pl.program_id(0)
        k = pl.program_id(1)

        @pl.when(k == 0)
        def _():
            acc[...] = ALPHA * dy_ref[...]

        for p in range(npair):
            acc[...] += _dot_nt(pr[2 * p][...], pr[2 * p + 1][...])

        if ln is not None:
            @pl.when((i == 0) & (k == 0))
            def _():
                dg_ref[...] = jnp.zeros_like(dg_ref)
                db_ref[...] = jnp.zeros_like(db_ref)

            @pl.when(k == nk - 1)
            def _():
                dyp, dg, db = _ln_bwd(acc[...], xh_ref[...], rs_ref[...], g_ref[...])
                dyo_ref[...] = dyp
                dyb_ref[...] = dyp.astype(BF16)
                dg_ref[...] += dg
                db_ref[...] += db
        else:
            @pl.when(k == nk - 1)
            def _():
                dh_ref[...] = acc[...]

    row = pl.BlockSpec((tm, d), lambda i, k: (i, 0))
    vec = pl.BlockSpec((1, d), lambda i, k: (0, 0))
    in_specs = [row]
    args = [dy]
    for lhs, w in pairs:
        in_specs += [pl.BlockSpec((tm, tk), lambda i, k: (i, k)), pl.BlockSpec((d, tk), lambda i, k: (0, k))]
        args += [lhs, w]
    if ln is not None:
        in_specs += [row, pl.BlockSpec((tm, 1), lambda i, k: (i, 0)), vec]
        args += list(ln)
        out_specs = [row, row, vec, vec]
        out_shape = [jax.ShapeDtypeStruct((t, d), F32), jax.ShapeDtypeStruct((t, d), BF16),
                     jax.ShapeDtypeStruct((1, d), F32), jax.ShapeDtypeStruct((1, d), F32)]
    else:
        out_specs = row
        out_shape = jax.ShapeDtypeStruct((t, d), F32)
    return pl.pallas_call(
        _drop_deps(body, n_in, len(deps)),
        grid=(nt, nk),
        in_specs=in_specs + [ANY] * len(deps),
        out_specs=out_specs,
        out_shape=out_shape,
        scratch_shapes=[pltpu.VMEM((tm, d), F32)],
        compiler_params=_params(2),
        name=name,
    )(*args, *deps)


def _hgrn_bwd(proj, oraw, dmix, states, logits, gn):
    t = proj.shape[0]
    ct = _hg_tile(t)
    nct = t // ct
    nblk = ct // HG_BLOCK
    nh = HG_HEADS

    def body(q_ref, fz_ref, iv_ref, gg_ref, or_ref, do_ref, st_ref, lg_ref, gn_ref,
             dq_ref, dfz_ref, div_ref, dgg_ref, dlg_ref, dgn_ref,
             dstate, qt_s, kt_s, k_s, b_s, eb_s, ekb_s, dec_s, dor_s, dbl_s, gr_s, dk_s, dlb_acc):
        c = pl.program_id(1)

        @pl.when(c == 0)
        def _():
            dstate[...] = jnp.zeros_like(dstate)
            dlb_acc[...] = jnp.zeros_like(dlb_acc)
            dgn_ref[...] = jnp.zeros_like(dgn_ref)

        lb = _lower_bound(lg_ref[...])
        q = q_ref[...]
        sig, nsig, f, k = _forget_terms(fz_ref[...], lb)
        logf = jnp.log(f)
        lower, upper, whole = _block_masks(ct)
        b = _mask_dot(lower, logf)
        bend = _mask_dot(whole, logf)
        eb = jnp.exp(b)
        ekb = jnp.exp(bend - b)
        qt_s[...] = (q * eb).astype(BF16)
        kt_s[...] = (k * ekb).astype(BF16)
        k_s[...] = k
        b_s[...] = b
        eb_s[...] = eb
        ekb_s[...] = ekb
        dec_s[...] = jnp.exp(bend)
        oraw = or_ref[...]
        r = lax.rsqrt(jnp.mean(oraw * oraw, axis=-1, keepdims=True) + LN_EPS)
        on = oraw * r
        gg = gg_ref[...]
        silu, dsilu = _silu_and_grad(gg)
        doa = do_ref[...]
        gnv = gn_ref[...]
        dgg_ref[...] = (doa * on * gnv * dsilu).astype(BF16)
        dyn = doa * silu
        dgn_ref[...] += jnp.sum(dyn * on, axis=0, keepdims=True)
        don = dyn * gnv
        dor_s[...] = r * (don - on * jnp.mean(don * on, axis=-1, keepdims=True))
        tidx = lax.broadcasted_iota(jnp.int32, (HG_BLOCK, HG_DIM), 0)

        def blk(ii, carry):
            i = nblk - 1 - ii
            r0 = pl.multiple_of(i * HG_BLOCK, HG_BLOCK)
            rows = pl.ds(r0, HG_BLOCK)
            st = st_ref[i]
            dst = dstate[...]
            dstb = dst.astype(BF16)
            do = dor_s[rows, :]
            dob = do.astype(BF16)
            v = iv_ref[rows, :]
            vb = v.astype(BF16)
            qq = q_ref[rows, :]
            kk = k_s[rows, :]
            bb = b_s[rows, :]
            qt = qt_s[rows, :]
            kt = kt_s[rows, :]
            dec = dec_s[pl.ds(r0, 1), :]
            dkt = _dot(vb, dstb)
            dq = _dot(dob, st.astype(BF16)) * eb_s[rows, :]
            dk = dkt * ekb_s[rows, :]
            dv = _dot_nt(kt, dstb)
            gend = jnp.sum(kk * dk, axis=0, keepdims=True) + dec * jnp.sum(dst * st, axis=0, keepdims=True)
            for s in range(HG_BLOCK):
                ks = kk[s:s + 1, :]
                e = jnp.where(tidx >= s, jnp.exp(jnp.minimum(bb - bb[s:s + 1, :], 0.0)), 0.0)
                qe = qq * e
                acol = jnp.sum(qe * ks, axis=1, keepdims=True)
                dacol = jnp.sum(do * v[s:s + 1, :], axis=1, keepdims=True)
                dq = dq + dacol * (ks * e)
                dk_row = jnp.sum(dacol * qe, axis=0, keepdims=True)
                dv_row = jnp.sum(acol * do, axis=0, keepdims=True)
                dk = dk + jnp.where(tidx == s, dk_row, 0.0)
                dv = dv + jnp.where(tidx == s, dv_row, 0.0)
            dq_ref[rows, :] = dq.astype(BF16)
            div_ref[rows, :] = dv.astype(BF16)
            dk_s[rows, :] = dk
            dbl_s[rows, :] = qq * dq - kk * dk
            gr_s[rows, :] = jnp.zeros((HG_BLOCK, HG_DIM), F32) + gend
            dstate[...] = dst * dec + _dot_tn(dob, qt)
            return carry

        lax.fori_loop(0, nblk, blk, 0)
        dlogf = _mask_dot(upper, dbl_s[...]) + gr_s[...]
        dk = dk_s[...]
        dfz_ref[...] = ((dlogf / f - dk) * ((1.0 - lb) * sig * nsig)).astype(BF16)
        dlb_acc[...] += jnp.sum((dlogf / f - dk) * nsig, axis=0, keepdims=True)

        @pl.when(c == nct - 1)
        def _():
            dl0 = dlb_acc[...] * lb * (1.0 - lb)
            layer = lax.broadcasted_iota(jnp.int32, (2, HG_DIM), 0)
            dlg_ref[...] = jnp.where(layer == 0, dl0, -dl0)

    def slab(off):
        return pl.BlockSpec((ct, HG_DIM), lambda h, c: (nct - 1 - c, off + h))

    out_slab = pl.BlockSpec((ct, HG_DIM), lambda h, c: (nct - 1 - c, h))
    tile_f32 = pltpu.VMEM((ct, HG_DIM), F32)
    tile_b16 = pltpu.VMEM((ct, HG_DIM), BF16)
    slab_shape = jax.ShapeDtypeStruct((t, nh * HG_DIM), BF16)
    return pl.pallas_call(
        body,
        grid=(nh, nct),
        in_specs=[slab(0), slab(nh), slab(2 * nh), slab(3 * nh), slab(0), slab(0),
                  pl.BlockSpec((None, nblk, HG_DIM, HG_DIM), lambda h, c: (h, nct - 1 - c, 0, 0)),
                  pl.BlockSpec((None, 2, HG_DIM), lambda h, c: (h, 0, 0)),
                  pl.BlockSpec((1, HG_DIM), lambda h, c: (0, 0))],
        out_specs=[out_slab, out_slab, out_slab, out_slab,
                   pl.BlockSpec((None, 2, HG_DIM), lambda h, c: (h, 0, 0)),
                   pl.BlockSpec((None, 1, HG_DIM), lambda h, c: (h, 0, 0))],
        out_shape=[slab_shape, slab_shape, slab_shape, slab_shape,
                   jax.ShapeDtypeStruct((nh, 2, HG_DIM), F32), jax.ShapeDtypeStruct((nh, 1, HG_DIM), F32)],
        scratch_shapes=[pltpu.VMEM((HG_DIM, HG_DIM), F32), tile_b16, tile_b16, tile_f32, tile_f32, tile_f32, tile_f32,
                        tile_f32, tile_f32, tile_f32, tile_f32, tile_f32, pltpu.VMEM((1, HG_DIM), F32)],
        compiler_params=_params(2),
        name="hgrn_bwd",
    )(proj, proj, proj, proj, oraw, dmix, states, logits, gn)


def _sgu_bwd(proj, dmix, ln_g, ln_b, w_s, w_t, b_col):
    t = proj.shape[0]
    ct = _sg_tile(t)
    nct = t // ct
    ng = SG_GROUPS
    off_u = 4 * HG_HEADS
    off_v = off_u + ng
    n = SG_CHUNK

    def body(u_ref, v_ref, do_ref, g_ref, b_ref, w_ref, wt_ref, bs_ref, du_ref, dv_ref, dg_ref, db_ref, dw_ref, dbs_ref):
        c = pl.program_id(1)

        @pl.when(c == 0)
        def _():
            dg_ref[...] = jnp.zeros_like(dg_ref)
            db_ref[...] = jnp.zeros_like(db_ref)
            dw_ref[...] = jnp.zeros_like(dw_ref)
            dbs_ref[...] = jnp.zeros_like(dbs_ref)

        r = lax.broadcasted_iota(jnp.int32, (n, n), 0)
        cc = lax.broadcasted_iota(jnp.int32, (n, n), 1)
        wm = jnp.where(cc <= r, w_ref[...], 0.0).astype(BF16)
        wmt = jnp.where(r <= cc, wt_ref[...], 0.0).astype(BF16)
        for ci in range(ct // n):
            rows = slice(ci * n, (ci + 1) * n)
            ua, dua, dva, vn, xhat, rstd, s = _sgu_chunk_fwd(u_ref[rows, :], v_ref[rows, :], g_ref[...], b_ref[...],
                                                             wm, bs_ref[...])
            do = do_ref[rows, :]
            du_ref[rows, :] = (do * s * dua).astype(BF16)
            ds = do * ua
            dsb = ds.astype(BF16)
            dbs_ref[...] += jnp.sum(ds, axis=1, keepdims=True)
            dw_ref[...] += _dot_nt(dsb, vn.astype(BF16))
            dvn = _dot(wmt, dsb)
            dva_in, dg, db = _ln_bwd(dvn, xhat, rstd, g_ref[...])
            dg_ref[...] += dg
            db_ref[...] += db
            dv_ref[rows, :] = (dva_in * dva).astype(BF16)

        @pl.when(c == nct - 1)
        def _():
            dw_ref[...] = jnp.where(cc <= r, dw_ref[...], 0.0)

    vec = pl.BlockSpec((None, 1, SG_DIM), lambda g, c: (g, 0, 0))
    mat = pl.BlockSpec((None, n, n), lambda g, c: (g, 0, 0))
    col = pl.BlockSpec((None, n, 1), lambda g, c: (g, 0, 0))
    out_slab = pl.BlockSpec((ct, SG_DIM), lambda g, c: (c, g))
    return pl.pallas_call(
        body,
        grid=(ng, nct),
        in_specs=[pl.BlockSpec((ct, SG_DIM), lambda g, c: (c, off_u + g)),
                  pl.BlockSpec((ct, SG_DIM), lambda g, c: (c, off_v + g)),
                  pl.BlockSpec((ct, SG_DIM), lambda g, c: (c, ng + g)), vec, vec, mat, mat, col],
        out_specs=[out_slab, out_slab, vec, vec, mat, col],
        out_shape=[jax.ShapeDtypeStruct((t, ng * SG_DIM), BF16), jax.ShapeDtypeStruct((t, ng * SG_DIM), BF16),
                   jax.ShapeDtypeStruct((ng, 1, SG_DIM), F32), jax.ShapeDtypeStruct((ng, 1, SG_DIM), F32),
                   jax.ShapeDtypeStruct((ng, n, n), F32), jax.ShapeDtypeStruct((ng, n, 1), F32)],
        compiler_params=_params(2),
        name="sgu_bwd",
    )(proj, proj, dmix, ln_g, ln_b, w_s, w_t, b_col)


def _attn_bwd(dyb, wo, qb, kb, vb):
    t, d = dyb.shape
    m_len = kb.shape[0]
    tm = _row_tile(t)
    dh = d // X_HEADS
    scale = dh ** -0.5

    def body(dy_ref, wo_ref, q_ref, k_ref, v_ref, dq_ref, dk_ref, dv_ref):
        i = pl.program_id(0)

        @pl.when(i == 0)
        def _():
            dk_ref[...] = jnp.zeros_like(dk_ref)
            dv_ref[...] = jnp.zeros_like(dv_ref)

        do = _dot_nt(dy_ref[...], wo_ref[...]).astype(BF16)
        for hd in range(X_HEADS):
            sl = slice(hd * dh, (hd + 1) * dh)
            qh = q_ref[:, sl]
            p = _softmax_rows(_dot_nt(qh, k_ref[:, sl]) * scale)
            doh = do[:, sl]
            dp = _dot_nt(doh, v_ref[:, sl])
            ds = (p * (dp - jnp.sum(dp * p, axis=-1, keepdims=True)) * scale).astype(BF16)
            dq_ref[:, sl] = _dot(ds, k_ref[:, sl]).astype(BF16)
            dk_ref[:, sl] += _dot_tn(ds, qh)
            dv_ref[:, sl] += _dot_tn(p.astype(BF16), doh)

    row = pl.BlockSpec((tm, d), lambda i: (i, 0))
    full = lambda a: pl.BlockSpec(a.shape, lambda i: (0, 0))
    kv = pl.BlockSpec((m_len, d), lambda i: (0, 0))
    return pl.pallas_call(
        body,
        grid=(t // tm,),
        in_specs=[row, full(wo), row, full(kb), full(vb)],
        out_specs=[row, kv, kv],
        out_shape=[jax.ShapeDtypeStruct((t, d), BF16), jax.ShapeDtypeStruct((m_len, d), F32),
                   jax.ShapeDtypeStruct((m_len, d), F32)],
        compiler_params=_params(1),
        name="attn_bwd",
    )(dyb, wo, qb, kb, vb)


def _mem_bwd(dk, dv, mb, xhat, rstd, g, wk, wv):
    m_len, d = dk.shape

    def body(dk_ref, dv_ref, mb_ref, xh_ref, rs_ref, g_ref, wk_ref, wv_ref, gwk_ref, gwv_ref, dg_ref, db_ref):
        dkb = dk_ref[...].astype(BF16)
        dvb = dv_ref[...].astype(BF16)
        mb_v = mb_ref[...]
        gwk_ref[...] = _dot_tn(mb_v, dkb).astype(BF16)
        gwv_ref[...] = _dot_tn(mb_v, dvb).astype(BF16)
        dm = _dot_nt(dkb, wk_ref[...]) + _dot_nt(dvb, wv_ref[...])
        _, dg, db = _ln_bwd(dm, xh_ref[...], rs_ref[...], g_ref[...])
        dg_ref[...] = dg
        db_ref[...] = db

    return pl.pallas_call(
        body,
        out_shape=[jax.ShapeDtypeStruct((d, d), BF16), jax.ShapeDtypeStruct((d, d), BF16),
                   jax.ShapeDtypeStruct((1, d), F32), jax.ShapeDtypeStruct((1, d), F32)],
        compiler_params=pltpu.CompilerParams(vmem_limit_bytes=VMEM_LIMIT_V7X),
        name="mem_bwd",
    )(dk, dv, mb, xhat, rstd, g, wk, wv)


def _adamw(w, g, m, v):
    m = ADAM_B1 * m + (1.0 - ADAM_B1) * g
    v = ADAM_B2 * v + (1.0 - ADAM_B2) * (g * g)
    m_hat = m / (1.0 - ADAM_B1 ** ADAM_STEP)
    v_hat = v / (1.0 - ADAM_B2 ** ADAM_STEP)
    delta = -ADAM_LR * (m_hat / (jnp.sqrt(v_hat) + ADAM_EPS) + ADAM_WD * w)
    return delta, m, v


def _slot_sum(ref):
    g = ref[0].astype(F32)
    for s in range(1, N_DEV):
        g = g + ref[s].astype(F32)
    return g


def _adam_sharded(lands, w, m, v, axis, name):
    rows, cols = w.shape
    if axis == 1:
        tr = 256 if rows % 256 == 0 else rows
        grid = (rows // tr,)
        wblk = pl.BlockSpec((tr, cols), lambda i: (i, 0))
        lblk = [pl.BlockSpec((N_DEV, tr, a.shape[2]), lambda i: (0, i, 0)) for a in lands]
    else:
        tc = _col_tile(cols)
        grid = (cols // tc,)
        wblk = pl.BlockSpec((rows, tc), lambda i: (0, i))
        lblk = [pl.BlockSpec((N_DEV, a.shape[1], tc), lambda i: (0, 0, i)) for a in lands]
    nl = len(lands)

    def body(*refs):
        w_ref, m_ref, v_ref = refs[nl:nl + 3]
        g_ref, d_ref, nm_ref, nv_ref = refs[nl + 3:]
        g = _slot_sum(refs[0])
        if nl == 2:
            tail = _slot_sum(refs[1])
            if axis == 1:
                g = jnp.concatenate([g, tail[:, :cols - g.shape[1]]], axis=1)
            else:
                g = jnp.concatenate([g, tail[:rows - g.shape[0], :]], axis=0)
        delta, nm, nv = _adamw(w_ref[...], g, m_ref[...], v_ref[...])
        g_ref[...] = g
        d_ref[...] = delta
        nm_ref[...] = nm
        nv_ref[...] = nv

    shp = jax.ShapeDtypeStruct((rows, cols), F32)
    return pl.pallas_call(
        body,
        grid=grid,
        in_specs=lblk + [wblk, wblk, wblk],
        out_specs=[wblk, wblk, wblk, wblk],
        out_shape=[shp, shp, shp, shp],
        compiler_params=_params(1),
        name=name,
    )(*lands, w, m, v)


def _mesh_pos():
    return lax.axis_index("x"), lax.axis_index("y"), lax.axis_index("c")


def _peer(k):
    x, y, c = _mesh_pos()
    pos = (x ^ (k >> 2), y ^ ((k >> 1) & 1), c ^ (k & 1))
    return pos, 4 * pos[0] + 2 * pos[1] + pos[2]


def _sem_index(row, k):
    return row * (N_DEV - 1) + k - 1


def _window(ref, axis, start, size):
    align = 16 if axis == 0 else LANES
    start = pl.multiple_of(start, align)
    return ref.at[pl.ds(start, size), :] if axis == 0 else ref.at[:, pl.ds(start, size)]


def _piece_refs(piece, srcs, lands, me, peer):
    kind, si, li, axis, base, stride, shape = piece
    if kind == "gather":
        return srcs[si], _window(lands[li], axis, base + stride * me, shape[axis])
    return _window(srcs[si], axis, base + stride * peer, shape[axis]), lands[li].at[me]


def _place_own(srcs, land_shapes, pieces):
    x, y, c = _mesh_pos()
    me = 4 * x + 2 * y + c
    lands = [lax.empty(s.shape, s.dtype) for s in land_shapes]
    for kind, si, li, axis, base, stride, shape in pieces:
        off = base + stride * me
        at = (off, 0) if axis == 0 else (0, off)
        if kind == "gather":
            lands[li] = lax.dynamic_update_slice(lands[li], srcs[si], at)
        else:
            block = lax.dynamic_slice(srcs[si], at, shape)
            lands[li] = lax.dynamic_update_slice(lands[li], block[None], (me, 0, 0))
    return lands


def _comm_start(srcs, lands, pieces, groups, name, after=()):
    ns, nl, na, ng = len(srcs), len(lands), len(after), len(groups)

    def body(*refs):
        s_refs = refs[:ns]
        l_refs = refs[ns:ns + nl]
        outs = refs[ns + nl + na:]
        sems = outs[:2 * ng]
        token = outs[-1]
        x, y, c = _mesh_pos()
        me = 4 * x + 2 * y + c
        for g, members in enumerate(groups):
            for row, p in enumerate(members):
                for k in range(1, N_DEV):
                    pos, peer = _peer(k)
                    src, dst = _piece_refs(pieces[p], s_refs, l_refs, me, peer)
                    pltpu.make_async_remote_copy(src_ref=src, dst_ref=dst, send_sem=sems[2 * g].at[_sem_index(row, k)],
                                                 recv_sem=sems[2 * g + 1].at[_sem_index(row, k)], device_id=pos,
                                                 device_id_type=MESH_ID).start()
        token[...] = jnp.zeros_like(token)

    sem_shapes = []
    for members in groups:
        sem_shapes += [pltpu.SemaphoreType.DMA((len(members) * (N_DEV - 1),))] * 2
    hbm_of = lambda a: pltpu.HBM(a.shape, a.dtype)
    out = pl.pallas_call(
        body,
        in_specs=[HBM] * (ns + nl) + [ANY] * na,
        out_specs=[SEM] * (2 * ng) + [HBM] * (ns + nl) + [pl.BlockSpec(memory_space=pltpu.VMEM)],
        out_shape=sem_shapes + [hbm_of(a) for a in srcs] + [hbm_of(a) for a in lands]
        + [jax.ShapeDtypeStruct((8, LANES), F32)],
        input_output_aliases={i: 2 * ng + i for i in range(ns + nl)},
        compiler_params=pltpu.CompilerParams(has_side_effects=DATAFLOW),
        name=name,
    )(*[pltpu.with_memory_space_constraint(a, pltpu.HBM) for a in list(srcs) + list(lands)], *after)
    sems = [(out[2 * g], out[2 * g + 1]) for g in range(ng)]
    return sems, list(out[2 * ng:2 * ng + ns]), list(out[2 * ng + ns:2 * ng + ns + nl]), out[-1]


def _comm_wait(srcs, lands, pieces, members, sems, after, name):
    ns, nl, na = len(srcs), len(lands), len(after)

    def body(*refs):
        s_refs = refs[:ns]
        l_refs = refs[ns:ns + nl]
        send_sems, recv_sems = refs[ns + nl:ns + nl + 2]
        x, y, c = _mesh_pos()
        me = 4 * x + 2 * y + c
        for row, p in enumerate(members):
            for k in range(1, N_DEV):
                pos, peer = _peer(k)
                src, dst = _piece_refs(pieces[p], s_refs, l_refs, me, peer)
                cp = pltpu.make_async_remote_copy(src_ref=src, dst_ref=dst, send_sem=send_sems.at[_sem_index(row, k)],
                                                  recv_sem=recv_sems.at[_sem_index(row, k)], device_id=pos,
                                                  device_id_type=MESH_ID)
                cp.wait_send()
                cp.wait_recv()

    hbm_of = lambda a: pltpu.HBM(a.shape, a.dtype)
    out = pl.pallas_call(
        body,
        in_specs=[HBM] * (ns + nl) + [SEM, SEM] + [ANY] * na,
        out_specs=[HBM] * (ns + nl),
        out_shape=[hbm_of(a) for a in srcs] + [hbm_of(a) for a in lands],
        input_output_aliases={i: i for i in range(ns + nl)},
        compiler_params=pltpu.CompilerParams(has_side_effects=DATAFLOW),
        name=name,
    )(*srcs, *lands, sems[0], sems[1], *after)
    return list(out[ns:])


_SMALL_NAMES = ("ln1_g", "ln1_b", "hg_lb_logits", "hg_norm_g", "sg_ln_g", "sg_ln_b", "sg_w_s", "sg_b_s",
                "ln2_g", "ln2_b", "mem_ln_g", "mem_ln_b", "ln3_g", "ln3_b", "ln4_g", "ln4_b")


def _pack_small(tree):
    flat = jnp.concatenate([tree[k].reshape(-1).astype(F32) for k in _SMALL_NAMES])
    rows = flat.shape[0] // LANES
    return jnp.pad(flat.reshape(rows, LANES), ((0, SMALL_ROWS - rows), (0, 0)))


def _unpack_small(packed, like):
    flat = packed.reshape(-1)
    out = {}
    pos = 0
    for k in _SMALL_NAMES:
        size = like[k].size
        out[k] = flat[pos:pos + size].reshape(like[k].shape)
        pos += size
    return out


def _adam_small(gathered, w, m, v):
    def body(p_ref, w_ref, m_ref, v_ref, g_ref, d_ref, nm_ref, nv_ref):
        g = _slot_sum(p_ref)
        delta, nm, nv = _adamw(w_ref[...], g, m_ref[...], v_ref[...])
        g_ref[...] = g
        d_ref[...] = delta
        nm_ref[...] = nm
        nv_ref[...] = nv

    shp = jax.ShapeDtypeStruct(w.shape, F32)
    return pl.pallas_call(body, out_shape=[shp, shp, shp, shp], name="adam_small")(gathered, w, m, v)


_COL_FFN = ("ffn1_w_gate", "ffn1_w_up", "ffn2_w_gate", "ffn2_w_up")
_ROW_FFN = ("ffn1_w_down", "ffn2_w_down")
_ROW_SQ = ("w_out", "xa_w_q", "xa_w_k", "xa_w_v", "xa_w_o")
_BIG_NAMES = ("ffn1_w_gate", "ffn1_w_up", "ffn1_w_down", "w_in", "w_out", "xa_w_q", "xa_w_k", "xa_w_v", "xa_w_o",
              "ffn2_w_gate", "ffn2_w_up", "ffn2_w_down")


def _ffn_split(fs):
    main = (fs // MXU_WIDTH_V7X) * MXU_WIDTH_V7X
    tail = fs - main
    tail_pad = -(-tail // LANES) * LANES
    assert main > 0 and tail > 0
    return main, tail, tail_pad


def _layout(name, shard_shape):
    r, c = shard_shape
    if name in _COL_FFN:
        main, tail, pad = _ffn_split(c)
        return (r, N_DEV * (main + pad)), [(1, 0, main, (r, main), (0, main)),
                                           (1, N_DEV * main, pad, (r, pad), (main, c))]
    if name in _ROW_FFN:
        main, tail, pad = _ffn_split(r)
        return (N_DEV * (main + pad), c), [(0, 0, main, (main, c), (0, main)),
                                           (0, N_DEV * main, pad, (pad, c), (main, r))]
    if name == "w_in":
        return (r, N_DEV * c), [(1, 0, c, (r, c), (0, c))]
    return (N_DEV * r, c), [(0, 0, r, (r, c), (0, r))]


def _shard_pieces(name, shard):
    out = []
    for axis, _, _, shape, (lo, hi) in _layout(name, shard.shape)[1]:
        part = shard[lo:hi, :] if axis == 0 else shard[:, lo:hi]
        pad = [(0, shape[0] - part.shape[0]), (0, shape[1] - part.shape[1])]
        out.append(jnp.pad(part, pad).astype(BF16))
    return out


def _gather_plan(names, shards):
    srcs, land_shapes, pieces, index = [], [], [], {}
    for li, name in enumerate(names):
        shape2d, parts = _layout(name, shards[name].shape)
        land_shapes.append(jax.ShapeDtypeStruct(shape2d, BF16))
        index[name] = []
        for (axis, base, stride, shape, _), src in zip(parts, _shard_pieces(name, shards[name])):
            index[name].append(len(pieces))
            pieces.append(("gather", len(srcs), li, axis, base, stride, shape))
            srcs.append(src)
    return srcs, land_shapes, pieces, index


def _scatter_plan(names, grads, shard_shapes):
    srcs, land_shapes, pieces, index = [], [], [], {}
    for si, name in enumerate(names):
        _, parts = _layout(name, shard_shapes[name])
        srcs.append(grads[name])
        index[name] = []
        for axis, base, stride, shape, _ in parts:
            index[name].append(len(land_shapes))
            pieces.append(("scatter", si, len(land_shapes), axis, base, stride, shape))
            land_shapes.append(jax.ShapeDtypeStruct((N_DEV,) + shape, grads[name].dtype))
    return srcs, land_shapes, pieces, index


def _small_views(small):
    row = lambda a: a.reshape(1, -1)
    ln = {k: row(small[k]) for k in ("ln1_g", "ln1_b", "ln2_g", "ln2_b", "ln3_g", "ln3_b", "ln4_g", "ln4_b",
                                      "mem_ln_g", "mem_ln_b", "hg_norm_g")}
    sg_w = small["sg_w_s"].reshape(SG_GROUPS, SG_CHUNK, SG_CHUNK)
    sg = dict(logits=jnp.swapaxes(small["hg_lb_logits"], 0, 1),
              g=small["sg_ln_g"].reshape(SG_GROUPS, 1, SG_DIM), b=small["sg_ln_b"].reshape(SG_GROUPS, 1, SG_DIM),
              w=sg_w, wt=jnp.swapaxes(sg_w, 1, 2), bs=small["sg_b_s"].reshape(SG_GROUPS, SG_CHUNK, 1))
    return ln, sg


def _forward(x, mem, get_w, small, first_deps=()):
    ln, sg = _small_views(small)
    xb = x.astype(BF16)
    a1, b1, s1 = _ffn_up(xb, get_w("ffn1_w_gate", ()), get_w("ffn1_w_up", ()), "ffn1_up", deps=first_deps)
    h1, h1b, xh1, rs1 = _mm_res_ln(s1, get_w("ffn1_w_down", (s1,)), x, ln["ln1_g"], ln["ln1_b"], 0.5, "ffn1_down_ln")
    proj = _mm_nn(h1b, get_w("w_in", (s1,)), "mix_in")
    oraw, mix, states = _hgrn_fwd(proj, sg["logits"], ln["hg_norm_g"])
    mix = _sgu_fwd(proj, mix, sg["g"], sg["b"], sg["w"], sg["bs"])
    h2, h2b, xh2, rs2 = _mm_res_ln(mix, get_w("w_out", (s1,)), h1, ln["ln2_g"], ln["ln2_b"], 1.0, "mix_out_ln")
    mb, mxh, mrs, kb, vb = _mem_kv(mem, ln["mem_ln_g"], ln["mem_ln_b"], get_w("xa_w_k", (h2b,)), get_w("xa_w_v", (h2b,)))
    qb, att = _attn_fwd(h2b, get_w("xa_w_q", (h2b,)), kb, vb)
    h3, h3b, xh3, rs3 = _mm_res_ln(att, get_w("xa_w_o", (h2b,)), h2, ln["ln3_g"], ln["ln3_b"], 1.0, "attn_out_ln")
    a2, b2, s2 = _ffn_up(h3b, get_w("ffn2_w_gate", (h2b,)), get_w("ffn2_w_up", (h2b,)), "ffn2_up")
    h4, _, xh4, rs4 = _mm_res_ln(s2, get_w("ffn2_w_down", (h2b,)), h3, ln["ln4_g"], ln["ln4_b"], 0.5, "ffn2_down_ln")
    return dict(xb=xb, a1=a1, b1=b1, s1=s1, h1b=h1b, xh1=xh1, rs1=rs1, proj=proj, oraw=oraw, mix=mix, states=states,
                h2b=h2b, xh2=xh2, rs2=rs2, mb=mb, mxh=mxh, mrs=mrs, kb=kb, vb=vb, qb=qb, att=att, h3b=h3b, xh3=xh3,
                rs3=rs3, a2=a2, b2=b2, s2=s2, h4=h4, xh4=xh4, rs4=rs4)


def _backward(sv, target, wt, small, send):
    ln, sg = _small_views(small)
    gs = {}
    loss, dy4, dy4b, gs["ln4_g"], gs["ln4_b"] = _loss_ln_bwd(sv["h4"], sv["xh4"], sv["rs4"], ln["ln4_g"], target)
    g_down2 = _mm_tn(sv["s2"], dy4b, "g_ffn2_down", scale=0.5)
    da2, db2 = _ffn_bwd_act(dy4b, wt["ffn2_w_down"], sv["a2"], sv["b2"], 0.5, "ffn2_bwd_act")
    g_gate2 = _mm_tn(sv["h3b"], da2, "g_ffn2_gate")
    g_up2 = _mm_tn(sv["h3b"], db2, "g_ffn2_up")
    tok = send(("ffn2_w_down", "ffn2_w_gate", "ffn2_w_up"), (g_down2, g_gate2, g_up2))
    dy3, dy3b, gs["ln3_g"], gs["ln3_b"] = _dx_ln(dy4, [(da2, wt["ffn2_w_gate"]), (db2, wt["ffn2_w_up"])],
                                                 (sv["xh3"], sv["rs3"], ln["ln3_g"]), "ffn2_dx_ln", deps=(tok,))

    g_o = _mm_tn(sv["att"], dy3b, "g_xa_o")
    dqb, dk, dv = _attn_bwd(dy3b, wt["xa_w_o"], sv["qb"], sv["kb"], sv["vb"])
    g_q = _mm_tn(sv["h2b"], dqb, "g_xa_q")
    g_k, g_v, gs["mem_ln_g"], gs["mem_ln_b"] = _mem_bwd(dk, dv, sv["mb"], sv["mxh"], sv["mrs"], ln["mem_ln_g"],
                                                        wt["xa_w_k"], wt["xa_w_v"])
    tok = send(("xa_w_o", "xa_w_q", "xa_w_k", "xa_w_v"), (g_o, g_q, g_k, g_v))
    dy2, dy2b, gs["ln2_g"], gs["ln2_b"] = _dx_ln(dy3, [(dqb, wt["xa_w_q"])], (sv["xh2"], sv["rs2"], ln["ln2_g"]),
                                                 "attn_dx_ln", deps=(tok,))

    g_out = _mm_tn(sv["mix"], dy2b, "g_w_out")
    dmix = _mm_nt(dy2b, wt["w_out"], "mix_out_bwd")
    dq, dfz, div, dgg, dlg, dgn = _hgrn_bwd(sv["proj"], sv["oraw"], dmix, sv["states"], sg["logits"], ln["hg_norm_g"])
    du, dvv, gs["sg_ln_g"], gs["sg_ln_b"], gs["sg_w_s"], gs["sg_b_s"] = _sgu_bwd(
        sv["proj"], dmix, sg["g"], sg["b"], sg["w"], sg["wt"], sg["bs"])
    gs["hg_lb_logits"] = jnp.swapaxes(dlg, 0, 1)
    gs["hg_norm_g"] = jnp.sum(dgn, axis=0)
    dproj = jnp.concatenate([dq, dfz, div, dgg, du, dvv], axis=1)
    g_in = _mm_tn(sv["h1b"], dproj, "g_w_in")
    tok = send(("w_out", "w_in"), (g_out, g_in))
    dy1, dy1b, gs["ln1_g"], gs["ln1_b"] = _dx_ln(dy2, [(dproj, wt["w_in"])], (sv["xh1"], sv["rs1"], ln["ln1_g"]),
                                                 "mix_dx_ln", deps=(tok,))

    g_down1 = _mm_tn(sv["s1"], dy1b, "g_ffn1_down", scale=0.5)
    da1, db1 = _ffn_bwd_act(dy1b, wt["ffn1_w_down"], sv["a1"], sv["b1"], 0.5, "ffn1_bwd_act")
    g_gate1 = _mm_tn(sv["xb"], da1, "g_ffn1_gate")
    g_up1 = _mm_tn(sv["xb"], db1, "g_ffn1_up")
    tok = send(("ffn1_w_down", "ffn1_w_gate", "ffn1_w_up"), (g_down1, g_gate1, g_up1))
    grad_x = _dx_ln(dy1, [(da1, wt["ffn1_w_gate"]), (db1, wt["ffn1_w_up"])], None, "ffn1_dx", deps=(tok,))
    return loss, grad_x, gs


_WEIGHT_NAMES = ("ffn1_w_gate", "ffn1_w_up", "ffn1_w_down", "ln1_g", "ln1_b", "w_in", "hg_lb_logits", "hg_norm_g",
                 "sg_ln_g", "sg_ln_b", "sg_w_s", "sg_b_s", "w_out", "ln2_g", "ln2_b", "mem_ln_g", "mem_ln_b",
                 "xa_w_q", "xa_w_k", "xa_w_v", "xa_w_o", "ln3_g", "ln3_b", "ffn2_w_gate", "ffn2_w_up", "ffn2_w_down",
                 "ln4_g", "ln4_b")
_FIRST = ("ffn1_w_gate", "ffn1_w_up")
_SECOND = ("ffn1_w_down", "w_in", "w_out")
_THIRD = ("xa_w_k", "xa_w_v", "xa_w_q", "xa_w_o", "ffn2_w_gate", "ffn2_w_up", "ffn2_w_down")


def kernel(x, mem, ffn1_w_gate, ffn1_w_up, ffn1_w_down, ln1_g, ln1_b, w_in, hg_lb_logits, hg_norm_g, sg_ln_g, sg_ln_b, sg_w_s, sg_b_s, w_out, ln2_g, ln2_b, mem_ln_g, mem_ln_b, xa_w_q, xa_w_k, xa_w_v, xa_w_o, ln3_g, ln3_b, ffn2_w_gate, ffn2_w_up, ffn2_w_down, ln4_g, ln4_b, loss_target, m_ffn1_w_gate, m_ffn1_w_up, m_ffn1_w_down, m_ln1_g, m_ln1_b, m_w_in, m_hg_lb_logits, m_hg_norm_g, m_sg_ln_g, m_sg_ln_b, m_sg_w_s, m_sg_b_s, m_w_out, m_ln2_g, m_ln2_b, m_mem_ln_g, m_mem_ln_b, m_xa_w_q, m_xa_w_k, m_xa_w_v, m_xa_w_o, m_ln3_g, m_ln3_b, m_ffn2_w_gate, m_ffn2_w_up, m_ffn2_w_down, m_ln4_g, m_ln4_b, v_ffn1_w_gate, v_ffn1_w_up, v_ffn1_w_down, v_ln1_g, v_ln1_b, v_w_in, v_hg_lb_logits, v_hg_norm_g, v_sg_ln_g, v_sg_ln_b, v_sg_w_s, v_sg_b_s, v_w_out, v_ln2_g, v_ln2_b, v_mem_ln_g, v_mem_ln_b, v_xa_w_q, v_xa_w_k, v_xa_w_v, v_xa_w_o, v_ln3_g, v_ln3_b, v_ffn2_w_gate, v_ffn2_w_up, v_ffn2_w_down, v_ln4_g, v_ln4_b):
    args = dict(locals())
    w = {k: args[k] for k in _WEIGHT_NAMES}
    m = {k: args["m_" + k] for k in _WEIGHT_NAMES}
    v = {k: args["v_" + k] for k in _WEIGHT_NAMES}
    shards = {k: w[k][0] for k in _BIG_NAMES}
    shard_shapes = {k: shards[k].shape for k in _BIG_NAMES}
    small = {k: (w[k][0] if k != "hg_lb_logits" else w[k]) for k in _SMALL_NAMES}

    srcs1, shapes1, pieces1, idx1 = _gather_plan(_FIRST, shards)
    lands1 = _place_own(srcs1, shapes1, pieces1)
    sems1, srcs1, lands1, _ = _comm_start(srcs1, lands1, pieces1, [list(range(len(pieces1)))], "gather_first_start")
    lands1 = _comm_wait(srcs1, lands1, pieces1, list(range(len(pieces1))), sems1[0], (), "gather_first_wait")
    wt = dict(zip(_FIRST, lands1))

    rest = _SECOND + _THIRD
    srcs2, shapes2, pieces2, idx2 = _gather_plan(rest, shards)
    lands2 = _place_own(srcs2, shapes2, pieces2)
    groups2 = [[p for k in names for p in idx2[k]] for names in (_SECOND, _THIRD)]
    sems2, srcs2, lands2, tok2 = _comm_start(srcs2, lands2, pieces2, groups2, "gather_rest_start", after=tuple(lands1))
    pending = {}
    for gi, names in enumerate((_SECOND, _THIRD)):
        for k in names:
            pending[k] = gi

    def get_w(name, after):
        if name in pending:
            gi = pending[name]
            names = (_SECOND, _THIRD)[gi]
            li = [rest.index(k) for k in names]
            si = sorted({pieces2[p][1] for p in groups2[gi]})
            remap = {s: i for i, s in enumerate(si)}
            lmap = {l: i for i, l in enumerate(li)}
            sub = [(pc[0], remap[pc[1]], lmap[pc[2]]) + pc[3:] for pc in (pieces2[p] for p in groups2[gi])]
            got = _comm_wait([srcs2[s] for s in si], [lands2[l] for l in li], sub, list(range(len(sub))), sems2[gi],
                             after, "gather_rest_wait_%d" % gi)
            for k, arr in zip(names, got):
                wt[k] = arr
                del pending[k]
        return wt[name]

    sv = _forward(x[0], mem[0], get_w, small, first_deps=(tok2,))

    sent = []

    def send(names, grads):
        srcs, shapes, pieces, idx = _scatter_plan(names, dict(zip(names, grads)), shard_shapes)
        lands = _place_own(srcs, shapes, pieces)
        sems, srcs, lands, tok = _comm_start(srcs, lands, pieces, [list(range(len(pieces)))],
                                             "grads_start_%d" % len(sent))
        sent.append((names, srcs, lands, pieces, idx, sems[0]))
        return tok

    loss, grad_x, gs = _backward(sv, loss_target[0], wt, small, send)

    small_like = {k: w[k] for k in _SMALL_NAMES}
    packed_g = _pack_small({k: gs[k].reshape(small_like[k].shape) for k in _SMALL_NAMES})
    small_src = jnp.broadcast_to(packed_g, (N_DEV, SMALL_ROWS, LANES)).reshape(N_DEV * SMALL_ROWS, LANES)
    sp = [("scatter", 0, 0, 0, 0, SMALL_ROWS, (SMALL_ROWS, LANES))]
    sshape = [jax.ShapeDtypeStruct((N_DEV, SMALL_ROWS, LANES), F32)]
    sl = _place_own([small_src], sshape, sp)
    ssem, ssrc, sl, _ = _comm_start([small_src], sl, sp, [[0]], "small_start")

    out_g, out_d, out_m, out_v = {}, {}, {}, {}
    after = (grad_x,)
    for n_sent, (names, srcs, lands, pieces, idx, sems) in enumerate(sent):
        lands = _comm_wait(srcs, lands, pieces, list(range(len(pieces))), sems, after, "grads_wait_%d" % n_sent)
        for k in names:
            axis = 1 if (k in _COL_FFN or k == "w_in") else 0
            res = _adam_sharded([lands[i] for i in idx[k]], w[k][0], m[k][0], v[k][0], axis, "adam_" + k)
            out_g[k], out_d[k], out_m[k], out_v[k] = [r[None] for r in res]
        after = (out_v[names[-1]],)
    sl = _comm_wait(ssrc, sl, sp, [0], ssem[0], after, "small_wait")
    packed = _adam_small(sl[0], _pack_small(small_like), _pack_small({k: m[k] for k in _SMALL_NAMES}),
                         _pack_small({k: v[k] for k in _SMALL_NAMES}))
    for dst, pk in zip((out_g, out_d, out_m, out_v), packed):
        dst.update(_unpack_small(pk, small_like))

    loss_all = lax.psum(loss[0, 0], ("x", "y", "c"))
    return (loss_all, grad_x[None], *[out_g[k] for k in _WEIGHT_NAMES], *[out_d[k] for k in _WEIGHT_NAMES],
            *[out_m[k] for k in _WEIGHT_NAMES], *[out_v[k] for k in _WEIGHT_NAMES])
```

```python
import itertools

import jax
import jax.numpy as jnp
from jax import lax
from jax.experimental import pallas as pl
from jax.experimental.pallas import tpu as pltpu

F32 = jnp.float32
BF16 = jnp.bfloat16

N_DEV = 8
ALPHA = 2.0 ** 0.25
LN_EPS = 1e-5
HG_HEADS = 4
HG_DIM = 128
SG_GROUPS = 4
SG_DIM = 128
SG_CHUNK = 128
X_HEADS = 4
HG_BLOCK = 16
HG_UNROLL = 4
ADAM_LR = 0.001
ADAM_B1 = 0.9
ADAM_B2 = 0.999
ADAM_EPS = 1e-08
ADAM_WD = 0.01
ADAM_STEP = 10
VMEM_LIMIT_V7X = 48 * 1024 * 1024
MXU_WIDTH_V7X = 256
LANES = 128
MESH_ID = pl.DeviceIdType.MESH
ANY = pl.BlockSpec(memory_space=pl.ANY)
HBM = pl.BlockSpec(memory_space=pltpu.HBM)
SEM = pl.BlockSpec(memory_space=pltpu.SEMAPHORE)
DATAFLOW = pltpu.SideEffectType.DATAFLOW_SIDE_EFFECTING


def _params(n_axes):
    return pltpu.CompilerParams(dimension_semantics=("arbitrary",) * n_axes, vmem_limit_bytes=VMEM_LIMIT_V7X)


def _dot(a, b):
    return jnp.dot(a, b, preferred_element_type=F32)


def _dot_nt(a, b):
    return lax.dot_general(a, b, (((1,), (1,)), ((), ())), preferred_element_type=F32)


def _dot_tn(a, b):
    return lax.dot_general(a, b, (((0,), (0,)), ((), ())), preferred_element_type=F32)


def _sigmoid(x):
    return 1.0 / (1.0 + jnp.exp(-x))


def _silu_and_grad(a):
    sig = _sigmoid(a)
    return a * sig, sig * (1.0 + a * (1.0 - sig))


_GELU_C = 0.7978845608028654


def _gelu_and_grad(x):
    inner = _GELU_C * (x + 0.044715 * x * x * x)
    t = jnp.tanh(inner)
    val = 0.5 * x * (1.0 + t)
    grad = 0.5 * (1.0 + t) + 0.5 * x * (1.0 - t * t) * _GELU_C * (1.0 + 3.0 * 0.044715 * x * x)
    return val, grad


def _ln_fwd(y, g, b):
    mu = jnp.mean(y, axis=-1, keepdims=True)
    yc = y - mu
    var = jnp.mean(yc * yc, axis=-1, keepdims=True)
    rstd = lax.rsqrt(var + LN_EPS)
    xhat = yc * rstd
    return xhat * g + b, xhat, rstd


def _ln_bwd(dh, xhat, rstd, g):
    dxh = dh * g
    m1 = jnp.mean(dxh, axis=-1, keepdims=True)
    m2 = jnp.mean(dxh * xhat, axis=-1, keepdims=True)
    dy = rstd * (dxh - m1 - xhat * m2)
    dg = jnp.sum(dh * xhat, axis=0, keepdims=True)
    db = jnp.sum(dh, axis=0, keepdims=True)
    return dy, dg, db


def _split3(x):
    hi = x.astype(BF16)
    r1 = x - hi.astype(F32)
    mid = r1.astype(BF16)
    lo = (r1 - mid.astype(F32)).astype(BF16)
    return hi, mid, lo


def _mask_dot(mask, x):
    hi, mid, lo = _split3(x)
    return _dot(mask, hi) + _dot(mask, mid) + _dot(mask, lo)


def _block_masks(n):
    r = lax.broadcasted_iota(jnp.int32, (n, n), 0)
    c = lax.broadcasted_iota(jnp.int32, (n, n), 1)
    assert HG_BLOCK & (HG_BLOCK - 1) == 0
    same = (r & -HG_BLOCK) == (c & -HG_BLOCK)
    one = jnp.ones((n, n), BF16)
    zero = jnp.zeros((n, n), BF16)
    lower = jnp.where(same & (c <= r), one, zero)
    upper = jnp.where(same & (c >= r), one, zero)
    whole = jnp.where(same, one, zero)
    return lower, upper, whole


def _row_tile(t):
    return min(t, 512)


def _col_tile(n):
    for cand in (512, 256, 128):
        if n % cand == 0:
            return cand
    return n


def _drop_deps(body, n_in, n_deps):
    if n_deps == 0:
        return body
    return lambda *refs: body(*refs[:n_in], *refs[n_in + n_deps:])


def _ffn_up(hb, wg, wu, name, deps=()):
    t, d = hb.shape
    f = wg.shape[1]
    tm = _row_tile(t)
    tn = _col_tile(f)

    def body(h_ref, wg_ref, wu_ref, a_ref, b_ref, s_ref):
        h = h_ref[...]
        a = _dot(h, wg_ref[...])
        b = _dot(h, wu_ref[...])
        a_ref[...] = a.astype(BF16)
        b_ref[...] = b.astype(BF16)
        s_ref[...] = (a * _sigmoid(a) * b).astype(BF16)

    act = pl.BlockSpec((tm, tn), lambda j, i: (i, j))
    wsp = pl.BlockSpec((d, tn), lambda j, i: (0, j))
    return pl.pallas_call(
        _drop_deps(body, 3, len(deps)),
        grid=(f // tn, t // tm),
        in_specs=[pl.BlockSpec((tm, d), lambda j, i: (i, 0)), wsp, wsp] + [ANY] * len(deps),
        out_specs=[act, act, act],
        out_shape=[jax.ShapeDtypeStruct((t, f), BF16)] * 3,
        compiler_params=_params(2),
        name=name,
    )(hb, wg, wu, *deps)


def _mm_res_ln(lhs, w, res, g, b, coef, name):
    t, kd = lhs.shape
    d = w.shape[1]
    tm = _row_tile(t)
    tk = _col_tile(kd)
    nk = kd // tk

    def body(l_ref, w_ref, r_ref, g_ref, b_ref, h_ref, hb_ref, xh_ref, rs_ref, acc):
        k = pl.program_id(1)

        @pl.when(k == 0)
        def _():
            acc[...] = jnp.zeros_like(acc)

        acc[...] += _dot(l_ref[...], w_ref[...])

        @pl.when(k == nk - 1)
        def _():
            y = ALPHA * r_ref[...] + coef * acc[...]
            h, xhat, rstd = _ln_fwd(y, g_ref[...], b_ref[...])
            h_ref[...] = h
            hb_ref[...] = h.astype(BF16)
            xh_ref[...] = xhat
            rs_ref[...] = rstd

    row = pl.BlockSpec((tm, d), lambda i, k: (i, 0))
    vec = pl.BlockSpec((1, d), lambda i, k: (0, 0))
    return pl.pallas_call(
        body,
        grid=(t // tm, nk),
        in_specs=[pl.BlockSpec((tm, tk), lambda i, k: (i, k)), pl.BlockSpec((tk, d), lambda i, k: (k, 0)),
                  row, vec, vec],
        out_specs=[row, row, row, pl.BlockSpec((tm, 1), lambda i, k: (i, 0))],
        out_shape=[jax.ShapeDtypeStruct((t, d), F32), jax.ShapeDtypeStruct((t, d), BF16),
                   jax.ShapeDtypeStruct((t, d), F32), jax.ShapeDtypeStruct((t, 1), F32)],
        scratch_shapes=[pltpu.VMEM((tm, d), F32)],
        compiler_params=_params(2),
        name=name,
    )(lhs, w, res, g, b)


def _mm_nn(lhs, w, name):
    t, kd = lhs.shape
    n = w.shape[1]
    tm = _row_tile(t)
    tn = _col_tile(n)

    def body(l_ref, w_ref, o_ref):
        o_ref[...] = _dot(l_ref[...], w_ref[...])

    return pl.pallas_call(
        body,
        grid=(n // tn, t // tm),
        in_specs=[pl.BlockSpec((tm, kd), lambda j, i: (i, 0)), pl.BlockSpec((kd, tn), lambda j, i: (0, j))],
        out_specs=pl.BlockSpec((tm, tn), lambda j, i: (i, j)),
        out_shape=jax.ShapeDtypeStruct((t, n), F32),
        compiler_params=_params(2),
        name=name,
    )(lhs, w)


def _lower_bound(lg):
    m = jnp.max(lg, axis=0, keepdims=True)
    e = jnp.exp(lg - m)
    return e[0:1, :] / jnp.sum(e, axis=0, keepdims=True)


def _forget_terms(fz, lb):
    e = jnp.exp(-jnp.abs(fz))
    r = 1.0 / (1.0 + e)
    pos = fz >= 0.0
    sig = jnp.where(pos, r, e * r)
    nsig = jnp.where(pos, e * r, r)
    f = lb + (1.0 - lb) * sig
    k = (1.0 - lb) * nsig
    return sig, nsig, f, k


def _hg_tile(t):
    return min(t, 256)


def _hgrn_fwd(proj, logits, gn):
    t = proj.shape[0]
    ct = _hg_tile(t)
    nct = t // ct
    nblk = ct // HG_BLOCK
    nh = HG_HEADS

    def body(q_ref, fz_ref, iv_ref, gg_ref, lg_ref, gn_ref, oraw_ref, oa_ref, st_ref,
             state, qt_s, kt_s, k_s, b_s, dec_s):
        c = pl.program_id(1)

        @pl.when(c == 0)
        def _():
            state[...] = jnp.zeros_like(state)

        lb = _lower_bound(lg_ref[...])
        q = q_ref[...]
        _, _, f, k = _forget_terms(fz_ref[...], lb)
        logf = jnp.log(f)
        lower, _, whole = _block_masks(ct)
        b = _mask_dot(lower, logf)
        bend = _mask_dot(whole, logf)
        qt_s[...] = (q * jnp.exp(b)).astype(BF16)
        kt_s[...] = (k * jnp.exp(bend - b)).astype(BF16)
        k_s[...] = k
        b_s[...] = b
        dec_s[...] = jnp.exp(bend)
        tidx = lax.broadcasted_iota(jnp.int32, (HG_BLOCK, HG_DIM), 0)

        def blk(i, carry):
            r0 = pl.multiple_of(i * HG_BLOCK, HG_BLOCK)
            rows = pl.ds(r0, HG_BLOCK)
            st = state[...]
            st_ref[i] = st
            v = iv_ref[rows, :]
            qq = q_ref[rows, :]
            kk = k_s[rows, :]
            bb = b_s[rows, :]
            o = _dot_nt(qt_s[rows, :], st.astype(BF16))
            for s in range(HG_BLOCK):
                e = jnp.where(tidx >= s, jnp.exp(jnp.minimum(bb - bb[s:s + 1, :], 0.0)), 0.0)
                acol = jnp.sum(qq * kk[s:s + 1, :] * e, axis=1, keepdims=True)
                o = o + acol * v[s:s + 1, :]
            oraw_ref[rows, :] = o
            state[...] = st * dec_s[pl.ds(r0, 1), :] + _dot_tn(v.astype(BF16), kt_s[rows, :])
            return carry

        lax.fori_loop(0, nblk, blk, 0, unroll=2 * HG_UNROLL)
        oraw = oraw_ref[...]
        r = lax.rsqrt(jnp.mean(oraw * oraw, axis=-1, keepdims=True) + LN_EPS)
        gg = gg_ref[...]
        oa_ref[...] = (oraw * r * gn_ref[...] * gg * _sigmoid(gg)).astype(BF16)

    def slab(off):
        return pl.BlockSpec((ct, HG_DIM), lambda h, c: (c, off + h))

    out_slab = pl.BlockSpec((ct, HG_DIM), lambda h, c: (c, h))
    return pl.pallas_call(
        body,
        grid=(nh, nct),
        in_specs=[slab(0), slab(nh), slab(2 * nh), slab(3 * nh),
                  pl.BlockSpec((None, 2, HG_DIM), lambda h, c: (h, 0, 0)),
                  pl.BlockSpec((1, HG_DIM), lambda h, c: (0, 0))],
        out_specs=[out_slab, out_slab, pl.BlockSpec((None, nblk, HG_DIM, HG_DIM), lambda h, c: (h, c, 0, 0))],
        out_shape=[jax.ShapeDtypeStruct((t, nh * HG_DIM), F32),
                   jax.ShapeDtypeStruct((t, (nh + SG_GROUPS) * HG_DIM), BF16),
                   jax.ShapeDtypeStruct((nh, t // HG_BLOCK, HG_DIM, HG_DIM), F32)],
        scratch_shapes=[pltpu.VMEM((HG_DIM, HG_DIM), F32), pltpu.VMEM((ct, HG_DIM), BF16),
                        pltpu.VMEM((ct, HG_DIM), BF16), pltpu.VMEM((ct, HG_DIM), F32),
                        pltpu.VMEM((ct, HG_DIM), F32), pltpu.VMEM((ct, HG_DIM), F32)],
        compiler_params=_params(2),
        name="hgrn_fwd",
    )(proj, proj, proj, proj, logits, gn)


def _sg_tile(t):
    return min(t, 512)


def _sgu_chunk_fwd(u, v, ln_g, ln_b, wm, bs):
    ua, dua = _gelu_and_grad(u)
    va, dva = _gelu_and_grad(v)
    vn, xhat, rstd = _ln_fwd(va, ln_g, ln_b)
    s = _dot(wm, vn.astype(BF16)) + bs
    return ua, dua, dva, vn, xhat, rstd, s


def _tril_weight(w_ref):
    n = SG_CHUNK
    r = lax.broadcasted_iota(jnp.int32, (n, n), 0)
    c = lax.broadcasted_iota(jnp.int32, (n, n), 1)
    return jnp.where(c <= r, w_ref[...], 0.0)


def _sgu_fwd(proj, mix, ln_g, ln_b, w_s, b_col):
    t = proj.shape[0]
    ct = _sg_tile(t)
    ng = SG_GROUPS
    off_u = 4 * HG_HEADS
    off_v = off_u + ng

    def body(u_ref, v_ref, g_ref, b_ref, w_ref, bs_ref, mix_ref, o_ref):
        del mix_ref
        wm = _tril_weight(w_ref).astype(BF16)
        for n in range(ct // SG_CHUNK):
            rows = slice(n * SG_CHUNK, (n + 1) * SG_CHUNK)
            ua, _, _, _, _, _, s = _sgu_chunk_fwd(u_ref[rows, :], v_ref[rows, :], g_ref[...], b_ref[...], wm, bs_ref[...])
            o_ref[rows, :] = (ua * s).astype(BF16)

    vec = pl.BlockSpec((None, 1, SG_DIM), lambda g, c: (g, 0, 0))
    return pl.pallas_call(
        body,
        grid=(ng, t // ct),
        in_specs=[pl.BlockSpec((ct, SG_DIM), lambda g, c: (c, off_u + g)),
                  pl.BlockSpec((ct, SG_DIM), lambda g, c: (c, off_v + g)), vec, vec,
                  pl.BlockSpec((None, SG_CHUNK, SG_CHUNK), lambda g, c: (g, 0, 0)),
                  pl.BlockSpec((None, SG_CHUNK, 1), lambda g, c: (g, 0, 0)), ANY],
        out_specs=pl.BlockSpec((ct, SG_DIM), lambda g, c: (c, HG_HEADS + g)),
        out_shape=jax.ShapeDtypeStruct(mix.shape, mix.dtype),
        input_output_aliases={6: 0},
        compiler_params=_params(2),
        name="sgu_fwd",
    )(proj, proj, ln_g, ln_b, w_s, b_col, mix)


def _mem_kv(mem, g, b, wk, wv):
    m_len, d = mem.shape

    def body(m_ref, g_ref, b_ref, wk_ref, wv_ref, mb_ref, xh_ref, rs_ref, k_ref, v_ref):
        m, xhat, rstd = _ln_fwd(m_ref[...], g_ref[...], b_ref[...])
        mb = m.astype(BF16)
        mb_ref[...] = mb
        xh_ref[...] = xhat
        rs_ref[...] = rstd
        k_ref[...] = _dot(mb, wk_ref[...]).astype(BF16)
        v_ref[...] = _dot(mb, wv_ref[...]).astype(BF16)

    return pl.pallas_call(
        body,
        out_shape=[jax.ShapeDtypeStruct((m_len, d), BF16), jax.ShapeDtypeStruct((m_len, d), F32),
                   jax.ShapeDtypeStruct((m_len, 1), F32), jax.ShapeDtypeStruct((m_len, d), BF16),
                   jax.ShapeDtypeStruct((m_len, d), BF16)],
        compiler_params=pltpu.CompilerParams(vmem_limit_bytes=VMEM_LIMIT_V7X),
        name="mem_kv",
    )(mem, g, b, wk, wv)


def _softmax_rows(s):
    m = jnp.max(s, axis=-1, keepdims=True)
    p = jnp.exp(s - m)
    return p / jnp.sum(p, axis=-1, keepdims=True)


def _attn_fwd(hb, wq, kb, vb):
    t, d = hb.shape
    tm = _row_tile(t)
    dh = d // X_HEADS
    scale = dh ** -0.5

    def body(h_ref, wq_ref, k_ref, v_ref, q_ref, o_ref):
        q = _dot(h_ref[...], wq_ref[...]).astype(BF16)
        q_ref[...] = q
        for hd in range(X_HEADS):
            sl = slice(hd * dh, (hd + 1) * dh)
            p = _softmax_rows(_dot_nt(q[:, sl], k_ref[:, sl]) * scale)
            o_ref[:, sl] = _dot(p.astype(BF16), v_ref[:, sl]).astype(BF16)

    row = pl.BlockSpec((tm, d), lambda i: (i, 0))
    full = lambda a: pl.BlockSpec(a.shape, lambda i: (0, 0))
    return pl.pallas_call(
        body,
        grid=(t // tm,),
        in_specs=[row, full(wq), full(kb), full(vb)],
        out_specs=[row, row],
        out_shape=[jax.ShapeDtypeStruct((t, d), BF16), jax.ShapeDtypeStruct((t, d), BF16)],
        compiler_params=_params(1),
        name="attn_fwd",
    )(hb, wq, kb, vb)


def _loss_ln_bwd(h, xhat, rstd, g, target):
    t, d = h.shape
    tm = _row_tile(t)
    nt = t // tm

    def body(h_ref, xh_ref, rs_ref, g_ref, t_ref, loss_ref, dy_ref, dyb_ref, dg_ref, db_ref, lacc):
        i = pl.program_id(0)

        @pl.when(i == 0)
        def _():
            lacc[...] = jnp.zeros_like(lacc)
            dg_ref[...] = jnp.zeros_like(dg_ref)
            db_ref[...] = jnp.zeros_like(db_ref)

        err = h_ref[...] - t_ref[...]
        lacc[...] += jnp.sum(err * err, axis=0, keepdims=True)
        dy, dg, db = _ln_bwd(err * (1.0 / d), xh_ref[...], rs_ref[...], g_ref[...])
        dy_ref[...] = dy
        dyb_ref[...] = dy.astype(BF16)
        dg_ref[...] += dg
        db_ref[...] += db

        @pl.when(i == nt - 1)
        def _():
            loss_ref[...] = jnp.zeros_like(loss_ref) + jnp.sum(lacc[...], axis=1, keepdims=True) * (0.5 / d)

    row = pl.BlockSpec((tm, d), lambda i: (i, 0))
    vec = pl.BlockSpec((1, d), lambda i: (0, 0))
    return pl.pallas_call(
        body,
        grid=(nt,),
        in_specs=[row, row, pl.BlockSpec((tm, 1), lambda i: (i, 0)), vec, row],
        out_specs=[pl.BlockSpec((1, LANES), lambda i: (0, 0)), row, row, vec, vec],
        out_shape=[jax.ShapeDtypeStruct((1, LANES), F32), jax.ShapeDtypeStruct((t, d), F32),
                   jax.ShapeDtypeStruct((t, d), BF16), jax.ShapeDtypeStruct((1, d), F32),
                   jax.ShapeDtypeStruct((1, d), F32)],
        scratch_shapes=[pltpu.VMEM((1, d), F32)],
        compiler_params=_params(1),
        name="loss_ln_bwd",
    )(h, xhat, rstd, g, target)


def _ffn_bwd_act(dyb, wd, a, b, coef, name, deps=()):
    t, d = dyb.shape
    f = wd.shape[0]
    tm = _row_tile(t)
    tn = _col_tile(f)

    def body(dy_ref, wd_ref, a_ref, b_ref, da_ref, db_ref):
        ds = _dot_nt(dy_ref[...], wd_ref[...]) * coef
        silu, dsilu = _silu_and_grad(a_ref[...].astype(F32))
        da_ref[...] = (ds * b_ref[...].astype(F32) * dsilu).astype(BF16)
        db_ref[...] = (ds * silu).astype(BF16)

    act = pl.BlockSpec((tm, tn), lambda j, i: (i, j))
    return pl.pallas_call(
        _drop_deps(body, 4, len(deps)),
        grid=(f // tn, t // tm),
        in_specs=[pl.BlockSpec((tm, d), lambda j, i: (i, 0)), pl.BlockSpec((tn, d), lambda j, i: (j, 0)),
                  act, act] + [ANY] * len(deps),
        out_specs=[act, act],
        out_shape=[jax.ShapeDtypeStruct((t, f), BF16), jax.ShapeDtypeStruct((t, f), BF16)],
        compiler_params=_params(2),
        name=name,
    )(dyb, wd, a, b, *deps)


def _mm_tn(a, b, name, scale=1.0, deps=()):
    t, m = a.shape
    n = b.shape[1]
    tt = _row_tile(t)
    nt = t // tt
    tm_o = _col_tile(m) if m >= n else m
    tn_o = n if m >= n else _col_tile(n)

    def body(a_ref, b_ref, o_ref, acc):
        k = pl.program_id(2)

        @pl.when(k == 0)
        def _():
            acc[...] = jnp.zeros_like(acc)

        acc[...] += _dot_tn(a_ref[...], b_ref[...])

        @pl.when(k == nt - 1)
        def _():
            o_ref[...] = (acc[...] * scale).astype(BF16)

    return pl.pallas_call(
        _drop_deps(body, 2, len(deps)),
        grid=(m // tm_o, n // tn_o, nt),
        in_specs=[pl.BlockSpec((tt, tm_o), lambda i, j, k: (k, i)), pl.BlockSpec((tt, tn_o), lambda i, j, k: (k, j))]
        + [ANY] * len(deps),
        out_specs=pl.BlockSpec((tm_o, tn_o), lambda i, j, k: (i, j)),
        out_shape=jax.ShapeDtypeStruct((m, n), BF16),
        scratch_shapes=[pltpu.VMEM((tm_o, tn_o), F32)],
        compiler_params=_params(3),
        name=name,
    )(a, b, *deps)


def _mm_nt(lhs, w, name):
    t, d = lhs.shape
    kd = w.shape[0]
    tm = _row_tile(t)
    tk = _col_tile(kd)

    def body(l_ref, w_ref, o_ref):
        o_ref[...] = _dot_nt(l_ref[...], w_ref[...])

    return pl.pallas_call(
        body,
        grid=(kd // tk, t // tm),
        in_specs=[pl.BlockSpec((tm, d), lambda j, i: (i, 0)), pl.BlockSpec((tk, d), lambda j, i: (j, 0))],
        out_specs=pl.BlockSpec((tm, tk), lambda j, i: (i, j)),
        out_shape=jax.ShapeDtypeStruct((t, kd), F32),
        compiler_params=_params(2),
        name=name,
    )(lhs, w)


def _dx_ln(dy, pairs, ln, name, deps=()):
    t, d = dy.shape
    tm = _row_tile(t)
    nt = t // tm
    kd = pairs[0][0].shape[1]
    tk = _col_tile(kd)
    nk = kd // tk
    npair = len(pairs)
    n_in = 1 + 2 * npair + (3 if ln is not None else 0)

    def body(*refs):
        dy_ref = refs[0]
        pr = refs[1:1 + 2 * npair]
        pos = 1 + 2 * npair
        if ln is not None:
            xh_ref, rs_ref, g_ref = refs[pos:pos + 3]
            pos += 3
            dyo_ref, dyb_ref, dg_ref, db_ref = refs[pos:pos + 4]
            acc = refs[pos + 4]
        else:
            dh_ref = refs[pos]
            acc = refs[pos + 1]
        i = pl.program_id(0)
        k = pl.program_id(1)

        @pl.when(k == 0)
        def _():
            acc[...] = ALPHA * dy_ref[...]

        for p in range(npair):
            acc[...] += _dot_nt(pr[2 * p][...], pr[2 * p + 1][...])

        if ln is not None:
            @pl.when((i == 0) & (k == 0))
            def _():
                dg_ref[...] = jnp.zeros_like(dg_ref)
                db_ref[...] = jnp.zeros_like(db_ref)

            @pl.when(k == nk - 1)
            def _():
                dyp, dg, db = _ln_bwd(acc[...], xh_ref[...], rs_ref[...], g_ref[...])
                dyo_ref[...] = dyp
                dyb_ref[...] = dyp.astype(BF16)
                dg_ref[...] += dg
                db_ref[...] += db
        else:
            @pl.when(k == nk - 1)
            def _():
                dh_ref[...] = acc[...]

    row = pl.BlockSpec((tm, d), lambda i, k: (i, 0))
    vec = pl.BlockSpec((1, d), lambda i, k: (0, 0))
    in_specs = [row]
    args = [dy]
    for lhs, w in pairs:
        in_specs += [pl.BlockSpec((tm, tk), lambda i, k: (i, k)), pl.BlockSpec((d, tk), lambda i, k: (0, k))]
        args += [lhs, w]
    if ln is not None:
        in_specs += [row, pl.BlockSpec((tm, 1), lambda i, k: (i, 0)), vec]
        args += list(ln)
        out_specs = [row, row, vec, vec]
        out_shape = [jax.ShapeDtypeStruct((t, d), F32), jax.ShapeDtypeStruct((t, d), BF16),
                     jax.ShapeDtypeStruct((1, d), F32), jax.ShapeDtypeStruct((1, d), F32)]
    else:
        out_specs = row
        out_shape = jax.ShapeDtypeStruct((t, d), F32)
    return pl.pallas_call(
        _drop_deps(body, n_in, len(deps)),
        grid=(nt, nk),
        in_specs=in_specs + [ANY] * len(deps),
        out_specs=out_specs,
        out_shape=out_shape,
        scratch_shapes=[pltpu.VMEM((tm, d), F32)],
        compiler_params=_params(2),
        name=name,
    )(*args, *deps)


def _hgrn_bwd(proj, oraw, dmix, states, logits, gn):
    t = proj.shape[0]
    ct = _hg_tile(t)
    nct = t // ct
    nblk = ct // HG_BLOCK
    nh = HG_HEADS

    def body(q_ref, fz_ref, iv_ref, gg_ref, or_ref, do_ref, st_ref, lg_ref, gn_ref,
             dq_ref, dfz_ref, div_ref, dgg_ref, dlg_ref, dgn_ref,
             dstate, qt_s, kt_s, k_s, b_s, eb_s, ekb_s, dec_s, dor_s, dbl_s, gr_s, dk_s, dlb_acc):
        c = pl.program_id(1)

        @pl.when(c == 0)
        def _():
            dstate[...] = jnp.zeros_like(dstate)
            dlb_acc[...] = jnp.zeros_like(dlb_acc)
            dgn_ref[...] = jnp.zeros_like(dgn_ref)

        lb = _lower_bound(lg_ref[...])
        q = q_ref[...]
        sig, nsig, f, k = _forget_terms(fz_ref[...], lb)
        logf = jnp.log(f)
        lower, upper, whole = _block_masks(ct)
        b = _mask_dot(lower, logf)
        bend = _mask_dot(whole, logf)
        eb = jnp.exp(b)
        ekb = jnp.exp(bend - b)
        qt_s[...] = (q * eb).astype(BF16)
        kt_s[...] = (k * ekb).astype(BF16)
        k_s[...] = k
        b_s[...] = b
        eb_s[...] = eb
        ekb_s[...] = ekb
        dec_s[...] = jnp.exp(bend)
        oraw = or_ref[...]
        r = lax.rsqrt(jnp.mean(oraw * oraw, axis=-1, keepdims=True) + LN_EPS)
        on = oraw * r
        gg = gg_ref[...]
        silu, dsilu = _silu_and_grad(gg)
        doa = do_ref[...]
        gnv = gn_ref[...]
        dgg_ref[...] = (doa * on * gnv * dsilu).astype(BF16)
        dyn = doa * silu
        dgn_ref[...] += jnp.sum(dyn * on, axis=0, keepdims=True)
        don = dyn * gnv
        dor_s[...] = r * (don - on * jnp.mean(don * on, axis=-1, keepdims=True))
        tidx = lax.broadcasted_iota(jnp.int32, (HG_BLOCK, HG_DIM), 0)

        def blk(ii, carry):
            i = nblk - 1 - ii
            r0 = pl.multiple_of(i * HG_BLOCK, HG_BLOCK)
            rows = pl.ds(r0, HG_BLOCK)
            st = st_ref[i]
            dst = dstate[...]
            dstb = dst.astype(BF16)
            do = dor_s[rows, :]
            dob = do.astype(BF16)
            v = iv_ref[rows, :]
            vb = v.astype(BF16)
            qq = q_ref[rows, :]
            kk = k_s[rows, :]
            bb = b_s[rows, :]
            qt = qt_s[rows, :]
            kt = kt_s[rows, :]
            dec = dec_s[pl.ds(r0, 1), :]
            dkt = _dot(vb, dstb)
            dq = _dot(dob, st.astype(BF16)) * eb_s[rows, :]
            dk = dkt * ekb_s[rows, :]
            dv = _dot_nt(kt, dstb)
            gend = jnp.sum(kk * dk, axis=0, keepdims=True) + dec * jnp.sum(dst * st, axis=0, keepdims=True)
            for s in range(HG_BLOCK):
                ks = kk[s:s + 1, :]
                e = jnp.where(tidx >= s, jnp.exp(jnp.minimum(bb - bb[s:s + 1, :], 0.0)), 0.0)
                qe = qq * e
                acol = jnp.sum(qe * ks, axis=1, keepdims=True)
                dacol = jnp.sum(do * v[s:s + 1, :], axis=1, keepdims=True)
                dq = dq + dacol * (ks * e)
                dk_row = jnp.sum(dacol * qe, axis=0, keepdims=True)
                dv_row = jnp.sum(acol * do, axis=0, keepdims=True)
                dk = dk + jnp.where(tidx == s, dk_row, 0.0)
                dv = dv + jnp.where(tidx == s, dv_row, 0.0)
            dq_ref[rows, :] = dq.astype(BF16)
            div_ref[rows, :] = dv.astype(BF16)
            dk_s[rows, :] = dk
            dbl_s[rows, :] = qq * dq - kk * dk
            gr_s[rows, :] = jnp.zeros((HG_BLOCK, HG_DIM), F32) + gend
            dstate[...] = dst * dec + _dot_tn(dob, qt)
            return carry

        lax.fori_loop(0, nblk, blk, 0, unroll=HG_UNROLL)
        dlogf = _mask_dot(upper, dbl_s[...]) + gr_s[...]
        dk = dk_s[...]
        dfz_ref[...] = ((dlogf / f - dk) * ((1.0 - lb) * sig * nsig)).astype(BF16)
        dlb_acc[...] += jnp.sum((dlogf / f - dk) * nsig, axis=0, keepdims=True)

        @pl.when(c == nct - 1)
        def _():
            dl0 = dlb_acc[...] * lb * (1.0 - lb)
            layer = lax.broadcasted_iota(jnp.int32, (2, HG_DIM), 0)
            dlg_ref[...] = jnp.where(layer == 0, dl0, -dl0)

    def slab(off):
        return pl.BlockSpec((ct, HG_DIM), lambda h, c: (nct - 1 - c, off + h))

    out_slab = pl.BlockSpec((ct, HG_DIM), lambda h, c: (nct - 1 - c, h))
    tile_f32 = pltpu.VMEM((ct, HG_DIM), F32)
    tile_b16 = pltpu.VMEM((ct, HG_DIM), BF16)
    slab_shape = jax.ShapeDtypeStruct((t, nh * HG_DIM), BF16)
    return pl.pallas_call(
        body,
        grid=(nh, nct),
        in_specs=[slab(0), slab(nh), slab(2 * nh), slab(3 * nh), slab(0), slab(0),
                  pl.BlockSpec((None, nblk, HG_DIM, HG_DIM), lambda h, c: (h, nct - 1 - c, 0, 0)),
                  pl.BlockSpec((None, 2, HG_DIM), lambda h, c: (h, 0, 0)),
                  pl.BlockSpec((1, HG_DIM), lambda h, c: (0, 0))],
        out_specs=[out_slab, out_slab, out_slab, out_slab,
                   pl.BlockSpec((None, 2, HG_DIM), lambda h, c: (h, 0, 0)),
                   pl.BlockSpec((None, 1, HG_DIM), lambda h, c: (h, 0, 0))],
        out_shape=[slab_shape, slab_shape, slab_shape, slab_shape,
                   jax.ShapeDtypeStruct((nh, 2, HG_DIM), F32), jax.ShapeDtypeStruct((nh, 1, HG_DIM), F32)],
        scratch_shapes=[pltpu.VMEM((HG_DIM, HG_DIM), F32), tile_b16, tile_b16, tile_f32, tile_f32, tile_f32, tile_f32,
                        tile_f32, tile_f32, tile_f32, tile_f32, tile_f32, pltpu.VMEM((1, HG_DIM), F32)],
        compiler_params=_params(2),
        name="hgrn_bwd",
    )(proj, proj, proj, proj, oraw, dmix, states, logits, gn)


def _sgu_bwd(proj, dmix, ln_g, ln_b, w_s, w_t, b_col):
    t = proj.shape[0]
    ct = _sg_tile(t)
    nct = t // ct
    ng = SG_GROUPS
    off_u = 4 * HG_HEADS
    off_v = off_u + ng
    n = SG_CHUNK

    def body(u_ref, v_ref, do_ref, g_ref, b_ref, w_ref, wt_ref, bs_ref, du_ref, dv_ref, dg_ref, db_ref, dw_ref, dbs_ref):
        c = pl.program_id(1)

        @pl.when(c == 0)
        def _():
            dg_ref[...] = jnp.zeros_like(dg_ref)
            db_ref[...] = jnp.zeros_like(db_ref)
            dw_ref[...] = jnp.zeros_like(dw_ref)
            dbs_ref[...] = jnp.zeros_like(dbs_ref)

        r = lax.broadcasted_iota(jnp.int32, (n, n), 0)
        cc = lax.broadcasted_iota(jnp.int32, (n, n), 1)
        wm = jnp.where(cc <= r, w_ref[...], 0.0).astype(BF16)
        wmt = jnp.where(r <= cc, wt_ref[...], 0.0).astype(BF16)
        for ci in range(ct // n):
            rows = slice(ci * n, (ci + 1) * n)
            ua, dua, dva, vn, xhat, rstd, s = _sgu_chunk_fwd(u_ref[rows, :], v_ref[rows, :], g_ref[...], b_ref[...],
                                                             wm, bs_ref[...])
            do = do_ref[rows, :]
            du_ref[rows, :] = (do * s * dua).astype(BF16)
            ds = do * ua
            dsb = ds.astype(BF16)
            dbs_ref[...] += jnp.sum(ds, axis=1, keepdims=True)
            dw_ref[...] += _dot_nt(dsb, vn.astype(BF16))
            dvn = _dot(wmt, dsb)
            dva_in, dg, db = _ln_bwd(dvn, xhat, rstd, g_ref[...])
            dg_ref[...] += dg
            db_ref[...] += db
            dv_ref[rows, :] = (dva_in * dva).astype(BF16)

        @pl.when(c == nct - 1)
        def _():
            dw_ref[...] = jnp.where(cc <= r, dw_ref[...], 0.0)

    vec = pl.BlockSpec((None, 1, SG_DIM), lambda g, c: (g, 0, 0))
    mat = pl.BlockSpec((None, n, n), lambda g, c: (g, 0, 0))
    col = pl.BlockSpec((None, n, 1), lambda g, c: (g, 0, 0))
    out_slab = pl.BlockSpec((ct, SG_DIM), lambda g, c: (c, g))
    return pl.pallas_call(
        body,
        grid=(ng, nct),
        in_specs=[pl.BlockSpec((ct, SG_DIM), lambda g, c: (c, off_u + g)),
                  pl.BlockSpec((ct, SG_DIM), lambda g, c: (c, off_v + g)),
                  pl.BlockSpec((ct, SG_DIM), lambda g, c: (c, ng + g)), vec, vec, mat, mat, col],
        out_specs=[out_slab, out_slab, vec, vec, mat, col],
        out_shape=[jax.ShapeDtypeStruct((t, ng * SG_DIM), BF16), jax.ShapeDtypeStruct((t, ng * SG_DIM), BF16),
                   jax.ShapeDtypeStruct((ng, 1, SG_DIM), F32), jax.ShapeDtypeStruct((ng, 1, SG_DIM), F32),
                   jax.ShapeDtypeStruct((ng, n, n), F32), jax.ShapeDtypeStruct((ng, n, 1), F32)],
        compiler_params=_params(2),
        name="sgu_bwd",
    )(proj, proj, dmix, ln_g, ln_b, w_s, w_t, b_col)


def _attn_bwd(dyb, wo, qb, kb, vb):
    t, d = dyb.shape
    m_len = kb.shape[0]
    tm = _row_tile(t)
    dh = d // X_HEADS
    scale = dh ** -0.5

    def body(dy_ref, wo_ref, q_ref, k_ref, v_ref, dq_ref, dk_ref, dv_ref):
        i = pl.program_id(0)

        @pl.when(i == 0)
        def _():
            dk_ref[...] = jnp.zeros_like(dk_ref)
            dv_ref[...] = jnp.zeros_like(dv_ref)

        do = _dot_nt(dy_ref[...], wo_ref[...]).astype(BF16)
        for hd in range(X_HEADS):
            sl = slice(hd * dh, (hd + 1) * dh)
            qh = q_ref[:, sl]
            p = _softmax_rows(_dot_nt(qh, k_ref[:, sl]) * scale)
            doh = do[:, sl]
            dp = _dot_nt(doh, v_ref[:, sl])
            ds = (p * (dp - jnp.sum(dp * p, axis=-1, keepdims=True)) * scale).astype(BF16)
            dq_ref[:, sl] = _dot(ds, k_ref[:, sl]).astype(BF16)
            dk_ref[:, sl] += _dot_tn(ds, qh)
            dv_ref[:, sl] += _dot_tn(p.astype(BF16), doh)

    row = pl.BlockSpec((tm, d), lambda i: (i, 0))
    full = lambda a: pl.BlockSpec(a.shape, lambda i: (0, 0))
    kv = pl.BlockSpec((m_len, d), lambda i: (0, 0))
    return pl.pallas_call(
        body,
        grid=(t // tm,),
        in_specs=[row, full(wo), row, full(kb), full(vb)],
        out_specs=[row, kv, kv],
        out_shape=[jax.ShapeDtypeStruct((t, d), BF16), jax.ShapeDtypeStruct((m_len, d), F32),
                   jax.ShapeDtypeStruct((m_len, d), F32)],
        compiler_params=_params(1),
        name="attn_bwd",
    )(dyb, wo, qb, kb, vb)


def _mem_bwd(dk, dv, mb, xhat, rstd, g, wk, wv):
    m_len, d = dk.shape

    def body(dk_ref, dv_ref, mb_ref, xh_ref, rs_ref, g_ref, wk_ref, wv_ref, gwk_ref, gwv_ref, dg_ref, db_ref):
        dkb = dk_ref[...].astype(BF16)
        dvb = dv_ref[...].astype(BF16)
        mb_v = mb_ref[...]
        gwk_ref[...] = _dot_tn(mb_v, dkb).astype(BF16)
        gwv_ref[...] = _dot_tn(mb_v, dvb).astype(BF16)
        dm = _dot_nt(dkb, wk_ref[...]) + _dot_nt(dvb, wv_ref[...])
        _, dg, db = _ln_bwd(dm, xh_ref[...], rs_ref[...], g_ref[...])
        dg_ref[...] = dg
        db_ref[...] = db

    return pl.pallas_call(
        body,
        out_shape=[jax.ShapeDtypeStruct((d, d), BF16), jax.ShapeDtypeStruct((d, d), BF16),
                   jax.ShapeDtypeStruct((1, d), F32), jax.ShapeDtypeStruct((1, d), F32)],
        compiler_params=pltpu.CompilerParams(vmem_limit_bytes=VMEM_LIMIT_V7X),
        name="mem_bwd",
    )(dk, dv, mb, xhat, rstd, g, wk, wv)


def _adamw(w, g, m, v):
    m = ADAM_B1 * m + (1.0 - ADAM_B1) * g
    v = ADAM_B2 * v + (1.0 - ADAM_B2) * (g * g)
    m_hat = m / (1.0 - ADAM_B1 ** ADAM_STEP)
    v_hat = v / (1.0 - ADAM_B2 ** ADAM_STEP)
    delta = -ADAM_LR * (m_hat / (jnp.sqrt(v_hat) + ADAM_EPS) + ADAM_WD * w)
    return delta, m, v


def _slot_sum(ref):
    g = ref[0].astype(F32)
    for s in range(1, N_DEV):
        g = g + ref[s].astype(F32)
    return g


def _adam_sharded(lands, w, m, v, axis, name):
    rows, cols = w.shape
    if axis == 1:
        tr = 256 if rows % 256 == 0 else rows
        grid = (rows // tr,)
        wblk = pl.BlockSpec((tr, cols), lambda i: (i, 0))
        lblk = [pl.BlockSpec((N_DEV, tr, a.shape[2]), lambda i: (0, i, 0)) for a in lands]
    else:
        tc = _col_tile(cols)
        grid = (cols // tc,)
        wblk = pl.BlockSpec((rows, tc), lambda i: (0, i))
        lblk = [pl.BlockSpec((N_DEV, a.shape[1], tc), lambda i: (0, 0, i)) for a in lands]
    nl = len(lands)

    def body(*refs):
        w_ref, m_ref, v_ref = refs[nl:nl + 3]
        g_ref, d_ref, nm_ref, nv_ref = refs[nl + 3:]
        g = _slot_sum(refs[0])
        if nl == 2:
            tail = _slot_sum(refs[1])
            if axis == 1:
                g = jnp.concatenate([g, tail[:, :cols - g.shape[1]]], axis=1)
            else:
                g = jnp.concatenate([g, tail[:rows - g.shape[0], :]], axis=0)
        delta, nm, nv = _adamw(w_ref[...], g, m_ref[...], v_ref[...])
        g_ref[...] = g
        d_ref[...] = delta
        nm_ref[...] = nm
        nv_ref[...] = nv

    shp = jax.ShapeDtypeStruct((rows, cols), F32)
    return pl.pallas_call(
        body,
        grid=grid,
        in_specs=lblk + [wblk, wblk, wblk],
        out_specs=[wblk, wblk, wblk, wblk],
        out_shape=[shp, shp, shp, shp],
        compiler_params=_params(1),
        name=name,
    )(*lands, w, m, v)


def _mesh_pos():
    return lax.axis_index("x"), lax.axis_index("y"), lax.axis_index("c")


def _peer(k):
    x, y, c = _mesh_pos()
    pos = (x ^ (k >> 2), y ^ ((k >> 1) & 1), c ^ (k & 1))
    return pos, 4 * pos[0] + 2 * pos[1] + pos[2]


def _sem_index(row, k):
    return row * (N_DEV - 1) + k - 1


def _window(ref, axis, start, size):
    align = 16 if axis == 0 else LANES
    start = pl.multiple_of(start, align)
    return ref.at[pl.ds(start, size), :] if axis == 0 else ref.at[:, pl.ds(start, size)]


def _piece_refs(piece, srcs, lands, me, peer):
    kind, si, li, axis, base, stride, shape = piece
    if kind == "gather":
        return srcs[si], _window(lands[li], axis, base + stride * me, shape[axis])
    return _window(srcs[si], axis, base + stride * peer, shape[axis]), lands[li].at[me]


def _place_own(srcs, land_shapes, pieces, name):
    ns, nl, npc = len(srcs), len(land_shapes), len(pieces)

    def body(*refs):
        s_refs = refs[:ns]
        l_refs = refs[ns:ns + nl]
        bufs = refs[ns + nl:ns + nl + npc]
        sems = refs[ns + nl + npc]
        x, y, c = _mesh_pos()
        me = 4 * x + 2 * y + c
        loads = []
        for p, piece in enumerate(pieces):
            src, dst = _piece_refs(piece, s_refs, l_refs, me, me)
            cp = pltpu.make_async_copy(src, bufs[p], sems.at[0, p])
            cp.start()
            loads.append((cp, dst))
        stores = []
        for p, (cp, dst) in enumerate(loads):
            cp.wait()
            out = pltpu.make_async_copy(bufs[p], dst, sems.at[1, p])
            out.start()
            stores.append(out)
        for out in stores:
            out.wait()

    out = pl.pallas_call(
        body,
        in_specs=[ANY] * ns,
        out_specs=[ANY] * nl,
        out_shape=list(land_shapes),
        scratch_shapes=[pltpu.VMEM(pc[6], srcs[pc[1]].dtype) for pc in pieces] + [pltpu.SemaphoreType.DMA((2, npc))],
        compiler_params=pltpu.CompilerParams(vmem_limit_bytes=VMEM_LIMIT_V7X),
        name=name,
    )(*srcs)
    return list(out)


def _comm_start(srcs, lands, pieces, groups, name, after=()):
    ns, nl, na, ng = len(srcs), len(lands), len(after), len(groups)

    def body(*refs):
        s_refs = refs[:ns]
        l_refs = refs[ns:ns + nl]
        outs = refs[ns + nl + na:]
        sems = outs[:2 * ng]
        token = outs[-1]
        x, y, c = _mesh_pos()
        me = 4 * x + 2 * y + c
        for g, members in enumerate(groups):
            for row, p in enumerate(members):
                for k in range(1, N_DEV):
                    pos, peer = _peer(k)
                    src, dst = _piece_refs(pieces[p], s_refs, l_refs, me, peer)
                    pltpu.make_async_remote_copy(src_ref=src, dst_ref=dst, send_sem=sems[2 * g].at[_sem_index(row, k)],
                                                 recv_sem=sems[2 * g + 1].at[_sem_index(row, k)], device_id=pos,
                                                 device_id_type=MESH_ID).start()
        token[...] = jnp.zeros_like(token)

    sem_shapes = []
    for members in groups:
        sem_shapes += [pltpu.SemaphoreType.DMA((len(members) * (N_DEV - 1),))] * 2
    hbm_of = lambda a: pltpu.HBM(a.shape, a.dtype)
    out = pl.pallas_call(
        body,
        in_specs=[HBM] * (ns + nl) + [ANY] * na,
        out_specs=[SEM] * (2 * ng) + [HBM] * (ns + nl) + [pl.BlockSpec(memory_space=pltpu.VMEM)],
        out_shape=sem_shapes + [hbm_of(a) for a in srcs] + [hbm_of(a) for a in lands]
        + [jax.ShapeDtypeStruct((8, LANES), F32)],
        input_output_aliases={i: 2 * ng + i for i in range(ns + nl)},
        compiler_params=pltpu.CompilerParams(has_side_effects=DATAFLOW),
        name=name,
    )(*[pltpu.with_memory_space_constraint(a, pltpu.HBM) for a in list(srcs) + list(lands)], *after)
    sems = [(out[2 * g], out[2 * g + 1]) for g in range(ng)]
    return sems, list(out[2 * ng:2 * ng + ns]), list(out[2 * ng + ns:2 * ng + ns + nl]), out[-1]


def _comm_wait(srcs, lands, pieces, members, sems, after, name):
    ns, nl, na = len(srcs), len(lands), len(after)

    def body(*refs):
        s_refs = refs[:ns]
        l_refs = refs[ns:ns + nl]
        send_sems, recv_sems = refs[ns + nl:ns + nl + 2]
        x, y, c = _mesh_pos()
        me = 4 * x + 2 * y + c
        for row, p in enumerate(members):
            for k in range(1, N_DEV):
                pos, peer = _peer(k)
                src, dst = _piece_refs(pieces[p], s_refs, l_refs, me, peer)
                cp = pltpu.make_async_remote_copy(src_ref=src, dst_ref=dst, send_sem=send_sems.at[_sem_index(row, k)],
                                                  recv_sem=recv_sems.at[_sem_index(row, k)], device_id=pos,
                                                  device_id_type=MESH_ID)
                cp.wait_send()
                cp.wait_recv()

    hbm_of = lambda a: pltpu.HBM(a.shape, a.dtype)
    out = pl.pallas_call(
        body,
        in_specs=[HBM] * (ns + nl) + [SEM, SEM] + [ANY] * na,
        out_specs=[HBM] * (ns + nl),
        out_shape=[hbm_of(a) for a in srcs] + [hbm_of(a) for a in lands],
        input_output_aliases={i: i for i in range(ns + nl)},
        compiler_params=pltpu.CompilerParams(has_side_effects=DATAFLOW),
        name=name,
    )(*srcs, *lands, sems[0], sems[1], *after)
    return list(out[ns:])


_SMALL_NAMES = ("ln1_g", "ln1_b", "hg_lb_logits", "hg_norm_g", "sg_ln_g", "sg_ln_b", "sg_w_s", "sg_b_s",
                "ln2_g", "ln2_b", "mem_ln_g", "mem_ln_b", "ln3_g", "ln3_b", "ln4_g", "ln4_b")


_VEC_NAMES = ("ln1_g", "ln1_b", "ln2_g", "ln2_b", "mem_ln_g", "mem_ln_b", "ln3_g", "ln3_b", "ln4_g", "ln4_b")
_ROW_NAMES = ("hg_lb_logits", "hg_norm_g", "sg_ln_g", "sg_ln_b", "sg_b_s", "sg_w_s")
VEC_ROWS = 16


def _row_plan(shapes):
    plan, pos = {}, 0
    for k in _ROW_NAMES:
        shp = shapes[k]
        slabs, off = [], pos
        for idx in itertools.product(*[range(dim) for dim in shp[:-2]]):
            slabs.append((idx, off, shp[-2]))
            off += shp[-2]
        plan[k] = (pos, slabs)
        pos = -(-off // 8) * 8
    return plan, -(-pos // 16) * 16


def _pack_small_grads(gs, shapes):
    vec = jnp.concatenate([gs[k].reshape(1, -1) for k in _VEC_NAMES], axis=0)
    vec = jnp.pad(vec, ((0, VEC_ROWS - vec.shape[0]), (0, 0)))
    plan, total = _row_plan(shapes)
    parts, pos = [], 0
    for k in _ROW_NAMES:
        first, slabs = plan[k]
        rows = gs[k].reshape(-1, LANES)
        end = slabs[-1][1] + slabs[-1][2]
        nxt = -(-end // 8) * 8
        parts.append(jnp.pad(rows, ((0, nxt - first - rows.shape[0]), (0, 0))))
        pos = nxt
    parts.append(jnp.zeros((total - pos, LANES), F32))
    return vec, jnp.concatenate(parts, axis=0)


def _adam_small(land_vec, land_rows, w, m, v):
    names = _VEC_NAMES + _ROW_NAMES
    n = len(names)
    shapes = {k: w[k].shape for k in names}
    plan, _ = _row_plan(shapes)

    def body(*refs):
        lv_ref, lr_ref = refs[:2]
        w_refs, m_refs, v_refs = refs[2:2 + n], refs[2 + n:2 + 2 * n], refs[2 + 2 * n:2 + 3 * n]
        outs = refs[2 + 3 * n:2 + 7 * n]
        gv_s, gr_s = refs[2 + 7 * n:]
        gv_s[...] = _slot_sum(lv_ref)
        gr_s[...] = _slot_sum(lr_ref)
        for p, k in enumerate(names):
            if k in _VEC_NAMES:
                row = _VEC_NAMES.index(k)
                slabs = [((), None, None)]
            else:
                slabs = plan[k][1]
            for idx, off, rows in slabs:
                g = gv_s[row:row + 1, :] if off is None else gr_s[off:off + rows, :]
                sel = idx + (slice(None), slice(None))
                delta, nm, nv = _adamw(w_refs[p][sel], g, m_refs[p][sel], v_refs[p][sel])
                for o, val in zip(range(4), (g, delta, nm, nv)):
                    outs[o * n + p][sel] = val

    flat = lambda tree: [tree[k] for k in names]
    shp = [jax.ShapeDtypeStruct(shapes[k], F32) for k in names]
    out = pl.pallas_call(
        body,
        out_shape=shp * 4,
        scratch_shapes=[pltpu.VMEM(land_vec.shape[1:], F32), pltpu.VMEM(land_rows.shape[1:], F32)],
        name="adam_small",
    )(land_vec, land_rows, *flat(w), *flat(m), *flat(v))
    return [dict(zip(names, out[o * n:(o + 1) * n])) for o in range(4)]


_COL_FFN = ("ffn1_w_gate", "ffn1_w_up", "ffn2_w_gate", "ffn2_w_up")
_ROW_FFN = ("ffn1_w_down", "ffn2_w_down")
_ROW_SQ = ("w_out", "xa_w_q", "xa_w_k", "xa_w_v", "xa_w_o")
_BIG_NAMES = ("ffn1_w_gate", "ffn1_w_up", "ffn1_w_down", "w_in", "w_out", "xa_w_q", "xa_w_k", "xa_w_v", "xa_w_o",
              "ffn2_w_gate", "ffn2_w_up", "ffn2_w_down")


def _ffn_split(fs):
    main = (fs // MXU_WIDTH_V7X) * MXU_WIDTH_V7X
    tail = fs - main
    tail_pad = -(-tail // LANES) * LANES
    assert main > 0 and tail > 0
    return main, tail, tail_pad


def _layout(name, shard_shape):
    r, c = shard_shape
    if name in _COL_FFN:
        main, tail, pad = _ffn_split(c)
        return (r, N_DEV * (main + pad)), [(1, 0, main, (r, main), (0, main)),
                                           (1, N_DEV * main, pad, (r, pad), (main, c))]
    if name in _ROW_FFN:
        main, tail, pad = _ffn_split(r)
        return (N_DEV * (main + pad), c), [(0, 0, main, (main, c), (0, main)),
                                           (0, N_DEV * main, pad, (pad, c), (main, r))]
    if name == "w_in":
        return (r, N_DEV * c), [(1, 0, c, (r, c), (0, c))]
    return (N_DEV * r, c), [(0, 0, r, (r, c), (0, r))]


def _shard_pieces(name, shard):
    out = []
    for axis, _, _, shape, (lo, hi) in _layout(name, shard.shape)[1]:
        part = shard[lo:hi, :] if axis == 0 else shard[:, lo:hi]
        pad = [(0, shape[0] - part.shape[0]), (0, shape[1] - part.shape[1])]
        out.append(jnp.pad(part, pad).astype(BF16))
    return out


def _gather_plan(names, shards):
    srcs, land_shapes, pieces, index = [], [], [], {}
    for li, name in enumerate(names):
        shape2d, parts = _layout(name, shards[name].shape)
        land_shapes.append(jax.ShapeDtypeStruct(shape2d, BF16))
        index[name] = []
        for (axis, base, stride, shape, _), src in zip(parts, _shard_pieces(name, shards[name])):
            index[name].append(len(pieces))
            pieces.append(("gather", len(srcs), li, axis, base, stride, shape))
            srcs.append(src)
    return srcs, land_shapes, pieces, index


def _scatter_plan(names, grads, shard_shapes):
    srcs, land_shapes, pieces, index = [], [], [], {}
    for si, name in enumerate(names):
        _, parts = _layout(name, shard_shapes[name])
        srcs.append(grads[name])
        index[name] = []
        for axis, base, stride, shape, _ in parts:
            index[name].append(len(land_shapes))
            pieces.append(("scatter", si, len(land_shapes), axis, base, stride, shape))
            land_shapes.append(jax.ShapeDtypeStruct((N_DEV,) + shape, grads[name].dtype))
    return srcs, land_shapes, pieces, index


def _small_views(small):
    row = lambda a: a.reshape(1, -1)
    ln = {k: row(small[k]) for k in ("ln1_g", "ln1_b", "ln2_g", "ln2_b", "ln3_g", "ln3_b", "ln4_g", "ln4_b",
                                      "mem_ln_g", "mem_ln_b", "hg_norm_g")}
    sg_w = small["sg_w_s"].reshape(SG_GROUPS, SG_CHUNK, SG_CHUNK)
    sg = dict(logits=jnp.swapaxes(small["hg_lb_logits"], 0, 1),
              g=small["sg_ln_g"].reshape(SG_GROUPS, 1, SG_DIM), b=small["sg_ln_b"].reshape(SG_GROUPS, 1, SG_DIM),
              w=sg_w, wt=jnp.swapaxes(sg_w, 1, 2), bs=small["sg_b_s"].reshape(SG_GROUPS, SG_CHUNK, 1))
    return ln, sg


def _forward(x, mem, get_w, small, first_deps=()):
    ln, sg = _small_views(small)
    xb = x.astype(BF16)
    a1, b1, s1 = _ffn_up(xb, get_w("ffn1_w_gate", ()), get_w("ffn1_w_up", ()), "ffn1_up", deps=first_deps)
    h1, h1b, xh1, rs1 = _mm_res_ln(s1, get_w("ffn1_w_down", (s1,)), x, ln["ln1_g"], ln["ln1_b"], 0.5, "ffn1_down_ln")
    proj = _mm_nn(h1b, get_w("w_in", (s1,)), "mix_in")
    oraw, mix, states = _hgrn_fwd(proj, sg["logits"], ln["hg_norm_g"])
    mix = _sgu_fwd(proj, mix, sg["g"], sg["b"], sg["w"], sg["bs"])
    h2, h2b, xh2, rs2 = _mm_res_ln(mix, get_w("w_out", (s1,)), h1, ln["ln2_g"], ln["ln2_b"], 1.0, "mix_out_ln")
    mb, mxh, mrs, kb, vb = _mem_kv(mem, ln["mem_ln_g"], ln["mem_ln_b"], get_w("xa_w_k", (h2b,)), get_w("xa_w_v", (h2b,)))
    qb, att = _attn_fwd(h2b, get_w("xa_w_q", (h2b,)), kb, vb)
    h3, h3b, xh3, rs3 = _mm_res_ln(att, get_w("xa_w_o", (h2b,)), h2, ln["ln3_g"], ln["ln3_b"], 1.0, "attn_out_ln")
    a2, b2, s2 = _ffn_up(h3b, get_w("ffn2_w_gate", (h2b,)), get_w("ffn2_w_up", (h2b,)), "ffn2_up")
    h4, _, xh4, rs4 = _mm_res_ln(s2, get_w("ffn2_w_down", (h2b,)), h3, ln["ln4_g"], ln["ln4_b"], 0.5, "ffn2_down_ln")
    return dict(xb=xb, a1=a1, b1=b1, s1=s1, h1b=h1b, xh1=xh1, rs1=rs1, proj=proj, oraw=oraw, mix=mix, states=states,
                h2b=h2b, xh2=xh2, rs2=rs2, mb=mb, mxh=mxh, mrs=mrs, kb=kb, vb=vb, qb=qb, att=att, h3b=h3b, xh3=xh3,
                rs3=rs3, a2=a2, b2=b2, s2=s2, h4=h4, xh4=xh4, rs4=rs4)


def _backward(sv, target, wt, small, send):
    ln, sg = _small_views(small)
    gs = {}
    loss, dy4, dy4b, gs["ln4_g"], gs["ln4_b"] = _loss_ln_bwd(sv["h4"], sv["xh4"], sv["rs4"], ln["ln4_g"], target)
    g_down2 = _mm_tn(sv["s2"], dy4b, "g_ffn2_down", scale=0.5)
    da2, db2 = _ffn_bwd_act(dy4b, wt["ffn2_w_down"], sv["a2"], sv["b2"], 0.5, "ffn2_bwd_act")
    g_gate2 = _mm_tn(sv["h3b"], da2, "g_ffn2_gate")
    g_up2 = _mm_tn(sv["h3b"], db2, "g_ffn2_up")
    tok = send(("ffn2_w_down", "ffn2_w_gate", "ffn2_w_up"), (g_down2, g_gate2, g_up2))
    dy3, dy3b, gs["ln3_g"], gs["ln3_b"] = _dx_ln(dy4, [(da2, wt["ffn2_w_gate"]), (db2, wt["ffn2_w_up"])],
                                                 (sv["xh3"], sv["rs3"], ln["ln3_g"]), "ffn2_dx_ln", deps=(tok,))

    g_o = _mm_tn(sv["att"], dy3b, "g_xa_o")
    dqb, dk, dv = _attn_bwd(dy3b, wt["xa_w_o"], sv["qb"], sv["kb"], sv["vb"])
    g_q = _mm_tn(sv["h2b"], dqb, "g_xa_q")
    g_k, g_v, gs["mem_ln_g"], gs["mem_ln_b"] = _mem_bwd(dk, dv, sv["mb"], sv["mxh"], sv["mrs"], ln["mem_ln_g"],
                                                        wt["xa_w_k"], wt["xa_w_v"])
    tok = send(("xa_w_o", "xa_w_q", "xa_w_k", "xa_w_v"), (g_o, g_q, g_k, g_v))
    dy2, dy2b, gs["ln2_g"], gs["ln2_b"] = _dx_ln(dy3, [(dqb, wt["xa_w_q"])], (sv["xh2"], sv["rs2"], ln["ln2_g"]),
                                                 "attn_dx_ln", deps=(tok,))

    g_out = _mm_tn(sv["mix"], dy2b, "g_w_out")
    dmix = _mm_nt(dy2b, wt["w_out"], "mix_out_bwd")
    dq, dfz, div, dgg, dlg, dgn = _hgrn_bwd(sv["proj"], sv["oraw"], dmix, sv["states"], sg["logits"], ln["hg_norm_g"])
    du, dvv, gs["sg_ln_g"], gs["sg_ln_b"], gs["sg_w_s"], gs["sg_b_s"] = _sgu_bwd(
        sv["proj"], dmix, sg["g"], sg["b"], sg["w"], sg["wt"], sg["bs"])
    gs["hg_lb_logits"] = jnp.swapaxes(dlg, 0, 1)
    gs["hg_norm_g"] = jnp.sum(dgn, axis=0)
    dproj = jnp.concatenate([dq, dfz, div, dgg, du, dvv], axis=1)
    g_in = _mm_tn(sv["h1b"], dproj, "g_w_in")
    tok = send(("w_out", "w_in"), (g_out, g_in))
    dy1, dy1b, gs["ln1_g"], gs["ln1_b"] = _dx_ln(dy2, [(dproj, wt["w_in"])], (sv["xh1"], sv["rs1"], ln["ln1_g"]),
                                                 "mix_dx_ln", deps=(tok,))

    g_down1 = _mm_tn(sv["s1"], dy1b, "g_ffn1_down", scale=0.5)
    da1, db1 = _ffn_bwd_act(dy1b, wt["ffn1_w_down"], sv["a1"], sv["b1"], 0.5, "ffn1_bwd_act")
    g_gate1 = _mm_tn(sv["xb"], da1, "g_ffn1_gate")
    g_up1 = _mm_tn(sv["xb"], db1, "g_ffn1_up")
    tok = send(("ffn1_w_down", "ffn1_w_gate", "ffn1_w_up"), (g_down1, g_gate1, g_up1))
    grad_x = _dx_ln(dy1, [(da1, wt["ffn1_w_gate"]), (db1, wt["ffn1_w_up"])], None, "ffn1_dx", deps=(tok,))
    return loss, grad_x, gs


_WEIGHT_NAMES = ("ffn1_w_gate", "ffn1_w_up", "ffn1_w_down", "ln1_g", "ln1_b", "w_in", "hg_lb_logits", "hg_norm_g",
                 "sg_ln_g", "sg_ln_b", "sg_w_s", "sg_b_s", "w_out", "ln2_g", "ln2_b", "mem_ln_g", "mem_ln_b",
                 "xa_w_q", "xa_w_k", "xa_w_v", "xa_w_o", "ln3_g", "ln3_b", "ffn2_w_gate", "ffn2_w_up", "ffn2_w_down",
                 "ln4_g", "ln4_b")
_FIRST = ("ffn1_w_gate", "ffn1_w_up")
_SECOND = ("ffn1_w_down", "w_in", "w_out")
_THIRD = ("xa_w_k", "xa_w_v", "xa_w_q", "xa_w_o", "ffn2_w_gate", "ffn2_w_up", "ffn2_w_down")


def kernel(x, mem, ffn1_w_gate, ffn1_w_up, ffn1_w_down, ln1_g, ln1_b, w_in, hg_lb_logits, hg_norm_g, sg_ln_g, sg_ln_b, sg_w_s, sg_b_s, w_out, ln2_g, ln2_b, mem_ln_g, mem_ln_b, xa_w_q, xa_w_k, xa_w_v, xa_w_o, ln3_g, ln3_b, ffn2_w_gate, ffn2_w_up, ffn2_w_down, ln4_g, ln4_b, loss_target, m_ffn1_w_gate, m_ffn1_w_up, m_ffn1_w_down, m_ln1_g, m_ln1_b, m_w_in, m_hg_lb_logits, m_hg_norm_g, m_sg_ln_g, m_sg_ln_b, m_sg_w_s, m_sg_b_s, m_w_out, m_ln2_g, m_ln2_b, m_mem_ln_g, m_mem_ln_b, m_xa_w_q, m_xa_w_k, m_xa_w_v, m_xa_w_o, m_ln3_g, m_ln3_b, m_ffn2_w_gate, m_ffn2_w_up, m_ffn2_w_down, m_ln4_g, m_ln4_b, v_ffn1_w_gate, v_ffn1_w_up, v_ffn1_w_down, v_ln1_g, v_ln1_b, v_w_in, v_hg_lb_logits, v_hg_norm_g, v_sg_ln_g, v_sg_ln_b, v_sg_w_s, v_sg_b_s, v_w_out, v_ln2_g, v_ln2_b, v_mem_ln_g, v_mem_ln_b, v_xa_w_q, v_xa_w_k, v_xa_w_v, v_xa_w_o, v_ln3_g, v_ln3_b, v_ffn2_w_gate, v_ffn2_w_up, v_ffn2_w_down, v_ln4_g, v_ln4_b):
    args = dict(locals())
    w = {k: args[k] for k in _WEIGHT_NAMES}
    m = {k: args["m_" + k] for k in _WEIGHT_NAMES}
    v = {k: args["v_" + k] for k in _WEIGHT_NAMES}
    shards = {k: w[k][0] for k in _BIG_NAMES}
    shard_shapes = {k: shards[k].shape for k in _BIG_NAMES}
    small = {k: (w[k][0] if k != "hg_lb_logits" else w[k]) for k in _SMALL_NAMES}

    srcs1, shapes1, pieces1, idx1 = _gather_plan(_FIRST, shards)
    lands1 = _place_own(srcs1, shapes1, pieces1, "gather_first_own")
    sems1, srcs1, lands1, _ = _comm_start(srcs1, lands1, pieces1, [list(range(len(pieces1)))], "gather_first_start")
    lands1 = _comm_wait(srcs1, lands1, pieces1, list(range(len(pieces1))), sems1[0], (), "gather_first_wait")
    wt = dict(zip(_FIRST, lands1))

    rest = _SECOND + _THIRD
    srcs2, shapes2, pieces2, idx2 = _gather_plan(rest, shards)
    lands2 = _place_own(srcs2, shapes2, pieces2, "gather_rest_own")
    groups2 = [[p for k in names for p in idx2[k]] for names in (_SECOND, _THIRD)]
    sems2, srcs2, lands2, tok2 = _comm_start(srcs2, lands2, pieces2, groups2, "gather_rest_start", after=tuple(lands1))
    pending = {}
    for gi, names in enumerate((_SECOND, _THIRD)):
        for k in names:
            pending[k] = gi

    def get_w(name, after):
        if name in pending:
            gi = pending[name]
            names = (_SECOND, _THIRD)[gi]
            li = [rest.index(k) for k in names]
            si = sorted({pieces2[p][1] for p in groups2[gi]})
            remap = {s: i for i, s in enumerate(si)}
            lmap = {l: i for i, l in enumerate(li)}
            sub = [(pc[0], remap[pc[1]], lmap[pc[2]]) + pc[3:] for pc in (pieces2[p] for p in groups2[gi])]
            got = _comm_wait([srcs2[s] for s in si], [lands2[l] for l in li], sub, list(range(len(sub))), sems2[gi],
                             after, "gather_rest_wait_%d" % gi)
            for k, arr in zip(names, got):
                wt[k] = arr
                del pending[k]
        return wt[name]

    sv = _forward(x[0], mem[0], get_w, small, first_deps=(tok2,))

    sent = []

    def send(names, grads):
        srcs, shapes, pieces, idx = _scatter_plan(names, dict(zip(names, grads)), shard_shapes)
        lands = _place_own(srcs, shapes, pieces, "grads_own_%d" % len(sent))
        sems, srcs, lands, tok = _comm_start(srcs, lands, pieces, [list(range(len(pieces)))],
                                             "grads_start_%d" % len(sent))
        sent.append((names, srcs, lands, pieces, idx, sems[0]))
        return tok

    loss, grad_x, gs = _backward(sv, loss_target[0], wt, small, send)

    ssrc = list(_pack_small_grads(gs, {k: w[k].shape for k in _SMALL_NAMES}))
    sp = [("scatter", i, i, 0, 0, 0, a.shape) for i, a in enumerate(ssrc)]
    sshape = [jax.ShapeDtypeStruct((N_DEV,) + a.shape, F32) for a in ssrc]
    sl = _place_own(ssrc, sshape, sp, "small_own")
    ssem, ssrc, sl, _ = _comm_start(ssrc, sl, sp, [[0, 1]], "small_start")

    out_g, out_d, out_m, out_v = {}, {}, {}, {}
    after = (grad_x,)
    for n_sent, (names, srcs, lands, pieces, idx, sems) in enumerate(sent):
        lands = _comm_wait(srcs, lands, pieces, list(range(len(pieces))), sems, after, "grads_wait_%d" % n_sent)
        for k in names:
            axis = 1 if (k in _COL_FFN or k == "w_in") else 0
            res = _adam_sharded([lands[i] for i in idx[k]], w[k][0], m[k][0], v[k][0], axis, "adam_" + k)
            out_g[k], out_d[k], out_m[k], out_v[k] = [r[None] for r in res]
        after = (out_v[names[-1]],)
    sl = _comm_wait(ssrc, sl, sp, [0, 1], ssem[0], after, "small_wait")
    for dst, res in zip((out_g, out_d, out_m, out_v), _adam_small(sl[0], sl[1], w, m, v)):
        dst.update(res)

    loss_all = lax.psum(loss[0, 0], ("x", "y", "c"))
    return (loss_all, grad_x[None], *[out_g[k] for k in _WEIGHT_NAMES], *[out_d[k] for k in _WEIGHT_NAMES],
            *[out_m[k] for k in _WEIGHT_NAMES], *[out_v[k] for k in _WEIGHT_NAMES])
```

```python
import itertools

import jax
import jax.numpy as jnp
from jax import lax
from jax.experimental import pallas as pl
from jax.experimental.pallas import tpu as pltpu

F32 = jnp.float32
BF16 = jnp.bfloat16

N_DEV = 8
ALPHA = 2.0 ** 0.25
LN_EPS = 1e-5
HG_HEADS = 4
HG_DIM = 128
SG_GROUPS = 4
SG_DIM = 128
SG_CHUNK = 128
X_HEADS = 4
HG_BLOCK = 16
HG_UNROLL = 4
ADAM_LR = 0.001
ADAM_B1 = 0.9
ADAM_B2 = 0.999
ADAM_EPS = 1e-08
ADAM_WD = 0.01
ADAM_STEP = 10
VMEM_LIMIT_V7X = 48 * 1024 * 1024
MXU_WIDTH_V7X = 256
LANES = 128
MESH_ID = pl.DeviceIdType.MESH
ANY = pl.BlockSpec(memory_space=pl.ANY)
HBM = pl.BlockSpec(memory_space=pltpu.HBM)
SEM = pl.BlockSpec(memory_space=pltpu.SEMAPHORE)
DATAFLOW = pltpu.SideEffectType.DATAFLOW_SIDE_EFFECTING


def _params(n_axes):
    return pltpu.CompilerParams(dimension_semantics=("arbitrary",) * n_axes, vmem_limit_bytes=VMEM_LIMIT_V7X)


def _dot(a, b):
    return jnp.dot(a, b, preferred_element_type=F32)


def _dot_nt(a, b):
    return lax.dot_general(a, b, (((1,), (1,)), ((), ())), preferred_element_type=F32)


def _dot_tn(a, b):
    return lax.dot_general(a, b, (((0,), (0,)), ((), ())), preferred_element_type=F32)


def _sigmoid(x):
    return 1.0 / (1.0 + jnp.exp(-x))


def _silu_and_grad(a):
    sig = _sigmoid(a)
    return a * sig, sig * (1.0 + a * (1.0 - sig))


_GELU_C = 0.7978845608028654


def _gelu_and_grad(x):
    inner = _GELU_C * (x + 0.044715 * x * x * x)
    t = jnp.tanh(inner)
    val = 0.5 * x * (1.0 + t)
    grad = 0.5 * (1.0 + t) + 0.5 * x * (1.0 - t * t) * _GELU_C * (1.0 + 3.0 * 0.044715 * x * x)
    return val, grad


def _ln_fwd(y, g, b):
    mu = jnp.mean(y, axis=-1, keepdims=True)
    yc = y - mu
    var = jnp.mean(yc * yc, axis=-1, keepdims=True)
    rstd = lax.rsqrt(var + LN_EPS)
    xhat = yc * rstd
    return xhat * g + b, xhat, rstd


def _ln_bwd(dh, xhat, rstd, g):
    dxh = dh * g
    m1 = jnp.mean(dxh, axis=-1, keepdims=True)
    m2 = jnp.mean(dxh * xhat, axis=-1, keepdims=True)
    dy = rstd * (dxh - m1 - xhat * m2)
    dg = jnp.sum(dh * xhat, axis=0, keepdims=True)
    db = jnp.sum(dh, axis=0, keepdims=True)
    return dy, dg, db


def _split3(x):
    hi = x.astype(BF16)
    r1 = x - hi.astype(F32)
    mid = r1.astype(BF16)
    lo = (r1 - mid.astype(F32)).astype(BF16)
    return hi, mid, lo


def _mask_dot(mask, x):
    hi, mid, lo = _split3(x)
    return _dot(mask, hi) + _dot(mask, mid) + _dot(mask, lo)


def _block_masks(n):
    r = lax.broadcasted_iota(jnp.int32, (n, n), 0)
    c = lax.broadcasted_iota(jnp.int32, (n, n), 1)
    assert HG_BLOCK & (HG_BLOCK - 1) == 0
    same = (r & -HG_BLOCK) == (c & -HG_BLOCK)
    one = jnp.ones((n, n), BF16)
    zero = jnp.zeros((n, n), BF16)
    lower = jnp.where(same & (c <= r), one, zero)
    upper = jnp.where(same & (c >= r), one, zero)
    whole = jnp.where(same, one, zero)
    return lower, upper, whole


def _row_tile(t):
    return min(t, 512)


def _col_tile(n):
    for cand in (512, 256, 128):
        if n % cand == 0:
            return cand
    return n


def _resident(w):
    return pl.BlockSpec(w.shape, lambda *_: (0, 0), pipeline_mode=pl.Buffered(1))


def _drop_deps(body, n_in, n_deps):
    if n_deps == 0:
        return body
    return lambda *refs: body(*refs[:n_in], *refs[n_in + n_deps:])


def _ffn_up(hb, wg, wu, name, deps=()):
    t, d = hb.shape
    f = wg.shape[1]
    tm = min(t, 256)
    tn = _col_tile(f)

    def body(h_ref, wg_ref, wu_ref, a_ref, b_ref, s_ref):
        h = h_ref[...]
        for c in range(f // tn):
            cols = slice(c * tn, (c + 1) * tn)
            a = _dot(h, wg_ref[:, cols])
            b = _dot(h, wu_ref[:, cols])
            a_ref[:, cols] = a.astype(BF16)
            b_ref[:, cols] = b.astype(BF16)
            s_ref[:, cols] = (a * _sigmoid(a) * b).astype(BF16)

    act = pl.BlockSpec((tm, f), lambda i: (i, 0))
    return pl.pallas_call(
        _drop_deps(body, 3, len(deps)),
        grid=(t // tm,),
        in_specs=[pl.BlockSpec((tm, d), lambda i: (i, 0)), _resident(wg), _resident(wu)] + [ANY] * len(deps),
        out_specs=[act, act, act],
        out_shape=[jax.ShapeDtypeStruct((t, f), BF16)] * 3,
        compiler_params=_params(1),
        name=name,
    )(hb, wg, wu, *deps)


def _mm_res_ln(lhs, w, res, g, b, coef, name):
    t, kd = lhs.shape
    d = w.shape[1]
    tm = _row_tile(t)

    def body(l_ref, w_ref, r_ref, g_ref, b_ref, h_ref, hb_ref, xh_ref, rs_ref):
        y = ALPHA * r_ref[...] + coef * _dot(l_ref[...], w_ref[...])
        h, xhat, rstd = _ln_fwd(y, g_ref[...], b_ref[...])
        h_ref[...] = h
        hb_ref[...] = h.astype(BF16)
        xh_ref[...] = xhat
        rs_ref[...] = rstd

    row = pl.BlockSpec((tm, d), lambda i: (i, 0))
    vec = pl.BlockSpec((1, d), lambda i: (0, 0))
    return pl.pallas_call(
        body,
        grid=(t // tm,),
        in_specs=[pl.BlockSpec((tm, kd), lambda i: (i, 0)), _resident(w), row, vec, vec],
        out_specs=[row, row, row, pl.BlockSpec((tm, 1), lambda i: (i, 0))],
        out_shape=[jax.ShapeDtypeStruct((t, d), F32), jax.ShapeDtypeStruct((t, d), BF16),
                   jax.ShapeDtypeStruct((t, d), F32), jax.ShapeDtypeStruct((t, 1), F32)],
        compiler_params=_params(1),
        name=name,
    )(lhs, w, res, g, b)


def _mm_nn(lhs, w, name):
    t, kd = lhs.shape
    n = w.shape[1]
    tm = _row_tile(t)
    tn = _col_tile(n)

    def body(l_ref, w_ref, o_ref):
        lhs_v = l_ref[...]
        for c in range(n // tn):
            cols = slice(c * tn, (c + 1) * tn)
            o_ref[:, cols] = _dot(lhs_v, w_ref[:, cols])

    return pl.pallas_call(
        body,
        grid=(t // tm,),
        in_specs=[pl.BlockSpec((tm, kd), lambda i: (i, 0)), _resident(w)],
        out_specs=pl.BlockSpec((tm, n), lambda i: (i, 0)),
        out_shape=jax.ShapeDtypeStruct((t, n), F32),
        compiler_params=_params(1),
        name=name,
    )(lhs, w)


def _lower_bound(lg):
    m = jnp.max(lg, axis=0, keepdims=True)
    e = jnp.exp(lg - m)
    return e[0:1, :] / jnp.sum(e, axis=0, keepdims=True)


def _forget_terms(fz, lb):
    e = jnp.exp(-jnp.abs(fz))
    r = 1.0 / (1.0 + e)
    pos = fz >= 0.0
    sig = jnp.where(pos, r, e * r)
    nsig = jnp.where(pos, e * r, r)
    f = lb + (1.0 - lb) * sig
    k = (1.0 - lb) * nsig
    return sig, nsig, f, k


def _hg_tile(t):
    return min(t, 256)


def _hgrn_fwd(proj, logits, gn):
    t = proj.shape[0]
    ct = _hg_tile(t)
    nct = t // ct
    nblk = ct // HG_BLOCK
    nh = HG_HEADS

    def body(q_ref, fz_ref, iv_ref, gg_ref, lg_ref, gn_ref, oraw_ref, oa_ref, st_ref,
             state, qt_s, kt_s, k_s, b_s, dec_s):
        c = pl.program_id(1)

        @pl.when(c == 0)
        def _():
            state[...] = jnp.zeros_like(state)

        lb = _lower_bound(lg_ref[...])
        q = q_ref[...]
        _, _, f, k = _forget_terms(fz_ref[...], lb)
        logf = jnp.log(f)
        lower, _, whole = _block_masks(ct)
        b = _mask_dot(lower, logf)
        bend = _mask_dot(whole, logf)
        qt_s[...] = (q * jnp.exp(b)).astype(BF16)
        kt_s[...] = (k * jnp.exp(bend - b)).astype(BF16)
        k_s[...] = k
        b_s[...] = b
        dec_s[...] = jnp.exp(bend)
        tidx = lax.broadcasted_iota(jnp.int32, (HG_BLOCK, HG_DIM), 0)

        def blk(i, carry):
            r0 = pl.multiple_of(i * HG_BLOCK, HG_BLOCK)
            rows = pl.ds(r0, HG_BLOCK)
            st = state[...]
            st_ref[i] = st
            v = iv_ref[rows, :]
            qq = q_ref[rows, :]
            kk = k_s[rows, :]
            bb = b_s[rows, :]
            o = _dot_nt(qt_s[rows, :], st.astype(BF16))
            for s in range(HG_BLOCK):
                e = jnp.where(tidx >= s, jnp.exp(jnp.minimum(bb - bb[s:s + 1, :], 0.0)), 0.0)
                acol = jnp.sum(qq * kk[s:s + 1, :] * e, axis=1, keepdims=True)
                o = o + acol * v[s:s + 1, :]
            oraw_ref[rows, :] = o
            state[...] = st * dec_s[pl.ds(r0, 1), :] + _dot_tn(v.astype(BF16), kt_s[rows, :])
            return carry

        lax.fori_loop(0, nblk, blk, 0, unroll=2 * HG_UNROLL)
        oraw = oraw_ref[...]
        r = lax.rsqrt(jnp.mean(oraw * oraw, axis=-1, keepdims=True) + LN_EPS)
        gg = gg_ref[...]
        oa_ref[...] = (oraw * r * gn_ref[...] * gg * _sigmoid(gg)).astype(BF16)

    def slab(off):
        return pl.BlockSpec((ct, HG_DIM), lambda h, c: (c, off + h))

    out_slab = pl.BlockSpec((ct, HG_DIM), lambda h, c: (c, h))
    return pl.pallas_call(
        body,
        grid=(nh, nct),
        in_specs=[slab(0), slab(nh), slab(2 * nh), slab(3 * nh),
                  pl.BlockSpec((None, 2, HG_DIM), lambda h, c: (h, 0, 0)),
                  pl.BlockSpec((1, HG_DIM), lambda h, c: (0, 0))],
        out_specs=[out_slab, out_slab, pl.BlockSpec((None, nblk, HG_DIM, HG_DIM), lambda h, c: (h, c, 0, 0))],
        out_shape=[jax.ShapeDtypeStruct((t, nh * HG_DIM), F32),
                   jax.ShapeDtypeStruct((t, (nh + SG_GROUPS) * HG_DIM), BF16),
                   jax.ShapeDtypeStruct((nh, t // HG_BLOCK, HG_DIM, HG_DIM), F32)],
        scratch_shapes=[pltpu.VMEM((HG_DIM, HG_DIM), F32), pltpu.VMEM((ct, HG_DIM), BF16),
                        pltpu.VMEM((ct, HG_DIM), BF16), pltpu.VMEM((ct, HG_DIM), F32),
                        pltpu.VMEM((ct, HG_DIM), F32), pltpu.VMEM((ct, HG_DIM), F32)],
        compiler_params=_params(2),
        name="hgrn_fwd",
    )(proj, proj, proj, proj, logits, gn)


def _sg_tile(t):
    return min(t, 512)


def _sgu_chunk_fwd(u, v, ln_g, ln_b, wm, bs):
    ua, dua = _gelu_and_grad(u)
    va, dva = _gelu_and_grad(v)
    vn, xhat, rstd = _ln_fwd(va, ln_g, ln_b)
    s = _dot(wm, vn.astype(BF16)) + bs
    return ua, dua, dva, vn, xhat, rstd, s


def _tril_weight(w_ref):
    n = SG_CHUNK
    r = lax.broadcasted_iota(jnp.int32, (n, n), 0)
    c = lax.broadcasted_iota(jnp.int32, (n, n), 1)
    return jnp.where(c <= r, w_ref[...], 0.0)


def _sgu_fwd(proj, mix, ln_g, ln_b, w_s, b_col):
    t = proj.shape[0]
    ct = _sg_tile(t)
    ng = SG_GROUPS
    off_u = 4 * HG_HEADS
    off_v = off_u + ng

    def body(u_ref, v_ref, g_ref, b_ref, w_ref, bs_ref, mix_ref, o_ref):
        del mix_ref
        wm = _tril_weight(w_ref).astype(BF16)
        for n in range(ct // SG_CHUNK):
            rows = slice(n * SG_CHUNK, (n + 1) * SG_CHUNK)
            ua, _, _, _, _, _, s = _sgu_chunk_fwd(u_ref[rows, :], v_ref[rows, :], g_ref[...], b_ref[...], wm, bs_ref[...])
            o_ref[rows, :] = (ua * s).astype(BF16)

    vec = pl.BlockSpec((None, 1, SG_DIM), lambda g, c: (g, 0, 0))
    return pl.pallas_call(
        body,
        grid=(ng, t // ct),
        in_specs=[pl.BlockSpec((ct, SG_DIM), lambda g, c: (c, off_u + g)),
                  pl.BlockSpec((ct, SG_DIM), lambda g, c: (c, off_v + g)), vec, vec,
                  pl.BlockSpec((None, SG_CHUNK, SG_CHUNK), lambda g, c: (g, 0, 0)),
                  pl.BlockSpec((None, SG_CHUNK, 1), lambda g, c: (g, 0, 0)), ANY],
        out_specs=pl.BlockSpec((ct, SG_DIM), lambda g, c: (c, HG_HEADS + g)),
        out_shape=jax.ShapeDtypeStruct(mix.shape, mix.dtype),
        input_output_aliases={6: 0},
        compiler_params=_params(2),
        name="sgu_fwd",
    )(proj, proj, ln_g, ln_b, w_s, b_col, mix)


def _mem_kv(mem, g, b, wk, wv):
    m_len, d = mem.shape

    def body(m_ref, g_ref, b_ref, wk_ref, wv_ref, mb_ref, xh_ref, rs_ref, k_ref, v_ref):
        m, xhat, rstd = _ln_fwd(m_ref[...], g_ref[...], b_ref[...])
        mb = m.astype(BF16)
        mb_ref[...] = mb
        xh_ref[...] = xhat
        rs_ref[...] = rstd
        k_ref[...] = _dot(mb, wk_ref[...]).astype(BF16)
        v_ref[...] = _dot(mb, wv_ref[...]).astype(BF16)

    return pl.pallas_call(
        body,
        out_shape=[jax.ShapeDtypeStruct((m_len, d), BF16), jax.ShapeDtypeStruct((m_len, d), F32),
                   jax.ShapeDtypeStruct((m_len, 1), F32), jax.ShapeDtypeStruct((m_len, d), BF16),
                   jax.ShapeDtypeStruct((m_len, d), BF16)],
        compiler_params=pltpu.CompilerParams(vmem_limit_bytes=VMEM_LIMIT_V7X),
        name="mem_kv",
    )(mem, g, b, wk, wv)


def _softmax_rows(s):
    m = jnp.max(s, axis=-1, keepdims=True)
    p = jnp.exp(s - m)
    return p / jnp.sum(p, axis=-1, keepdims=True)


def _attn_fwd(hb, wq, kb, vb):
    t, d = hb.shape
    tm = _row_tile(t)
    dh = d // X_HEADS
    scale = dh ** -0.5

    def body(h_ref, wq_ref, k_ref, v_ref, q_ref, o_ref):
        q = _dot(h_ref[...], wq_ref[...]).astype(BF16)
        q_ref[...] = q
        for hd in range(X_HEADS):
            sl = slice(hd * dh, (hd + 1) * dh)
            p = _softmax_rows(_dot_nt(q[:, sl], k_ref[:, sl]) * scale)
            o_ref[:, sl] = _dot(p.astype(BF16), v_ref[:, sl]).astype(BF16)

    row = pl.BlockSpec((tm, d), lambda i: (i, 0))
    full = lambda a: pl.BlockSpec(a.shape, lambda i: (0, 0))
    return pl.pallas_call(
        body,
        grid=(t // tm,),
        in_specs=[row, full(wq), full(kb), full(vb)],
        out_specs=[row, row],
        out_shape=[jax.ShapeDtypeStruct((t, d), BF16), jax.ShapeDtypeStruct((t, d), BF16)],
        compiler_params=_params(1),
        name="attn_fwd",
    )(hb, wq, kb, vb)


def _loss_ln_bwd(h, xhat, rstd, g, target):
    t, d = h.shape
    tm = _row_tile(t)
    nt = t // tm

    def body(h_ref, xh_ref, rs_ref, g_ref, t_ref, loss_ref, dy_ref, dyb_ref, dg_ref, db_ref, lacc):
        i = pl.program_id(0)

        @pl.when(i == 0)
        def _():
            lacc[...] = jnp.zeros_like(lacc)
            dg_ref[...] = jnp.zeros_like(dg_ref)
            db_ref[...] = jnp.zeros_like(db_ref)

        err = h_ref[...] - t_ref[...]
        lacc[...] += jnp.sum(err * err, axis=0, keepdims=True)
        dy, dg, db = _ln_bwd(err * (1.0 / d), xh_ref[...], rs_ref[...], g_ref[...])
        dy_ref[...] = dy
        dyb_ref[...] = dy.astype(BF16)
        dg_ref[...] += dg
        db_ref[...] += db

        @pl.when(i == nt - 1)
        def _():
            loss_ref[...] = jnp.zeros_like(loss_ref) + jnp.sum(lacc[...], axis=1, keepdims=True) * (0.5 / d)

    row = pl.BlockSpec((tm, d), lambda i: (i, 0))
    vec = pl.BlockSpec((1, d), lambda i: (0, 0))
    return pl.pallas_call(
        body,
        grid=(nt,),
        in_specs=[row, row, pl.BlockSpec((tm, 1), lambda i: (i, 0)), vec, row],
        out_specs=[pl.BlockSpec((1, LANES), lambda i: (0, 0)), row, row, vec, vec],
        out_shape=[jax.ShapeDtypeStruct((1, LANES), F32), jax.ShapeDtypeStruct((t, d), F32),
                   jax.ShapeDtypeStruct((t, d), BF16), jax.ShapeDtypeStruct((1, d), F32),
                   jax.ShapeDtypeStruct((1, d), F32)],
        scratch_shapes=[pltpu.VMEM((1, d), F32)],
        compiler_params=_params(1),
        name="loss_ln_bwd",
    )(h, xhat, rstd, g, target)


def _ffn_bwd_act(dyb, wd, a, b, coef, name, deps=()):
    t, d = dyb.shape
    f = wd.shape[0]
    tm = _row_tile(t)
    tn = _col_tile(f)

    def body(dy_ref, wd_ref, a_ref, b_ref, da_ref, db_ref):
        dy = dy_ref[...]
        for c in range(f // tn):
            cols = slice(c * tn, (c + 1) * tn)
            ds = _dot_nt(dy, wd_ref[cols, :]) * coef
            silu, dsilu = _silu_and_grad(a_ref[:, cols].astype(F32))
            da_ref[:, cols] = (ds * b_ref[:, cols].astype(F32) * dsilu).astype(BF16)
            db_ref[:, cols] = (ds * silu).astype(BF16)

    act = pl.BlockSpec((tm, f), lambda i: (i, 0))
    return pl.pallas_call(
        _drop_deps(body, 4, len(deps)),
        grid=(t // tm,),
        in_specs=[pl.BlockSpec((tm, d), lambda i: (i, 0)), _resident(wd), act, act] + [ANY] * len(deps),
        out_specs=[act, act],
        out_shape=[jax.ShapeDtypeStruct((t, f), BF16), jax.ShapeDtypeStruct((t, f), BF16)],
        compiler_params=_params(1),
        name=name,
    )(dyb, wd, a, b, *deps)


def _mm_tn(a, b, name, scale=1.0, deps=()):
    t, m = a.shape
    n = b.shape[1]
    tt = _row_tile(t)
    nt = t // tt
    tm_o = m // 2 if (m > n and m * n > 2 ** 21) else m
    tn_o = n // 2 if (n > m and m * n > 2 ** 21) else n

    def body(a_ref, b_ref, o_ref, acc):
        k = pl.program_id(2)

        @pl.when(k == 0)
        def _():
            acc[...] = jnp.zeros_like(acc)

        acc[...] += _dot_tn(a_ref[...], b_ref[...])

        @pl.when(k == nt - 1)
        def _():
            o_ref[...] = (acc[...] * scale).astype(BF16)

    return pl.pallas_call(
        _drop_deps(body, 2, len(deps)),
        grid=(m // tm_o, n // tn_o, nt),
        in_specs=[pl.BlockSpec((tt, tm_o), lambda i, j, k: (k, i)), pl.BlockSpec((tt, tn_o), lambda i, j, k: (k, j))]
        + [ANY] * len(deps),
        out_specs=pl.BlockSpec((tm_o, tn_o), lambda i, j, k: (i, j)),
        out_shape=jax.ShapeDtypeStruct((m, n), BF16),
        scratch_shapes=[pltpu.VMEM((tm_o, tn_o), F32)],
        compiler_params=_params(3),
        name=name,
    )(a, b, *deps)


def _mm_nt(lhs, w, name):
    t, d = lhs.shape
    kd = w.shape[0]
    tm = _row_tile(t)

    def body(l_ref, w_ref, o_ref):
        o_ref[...] = _dot_nt(l_ref[...], w_ref[...])

    return pl.pallas_call(
        body,
        grid=(t // tm,),
        in_specs=[pl.BlockSpec((tm, d), lambda i: (i, 0)), _resident(w)],
        out_specs=pl.BlockSpec((tm, kd), lambda i: (i, 0)),
        out_shape=jax.ShapeDtypeStruct((t, kd), F32),
        compiler_params=_params(1),
        name=name,
    )(lhs, w)


def _dx_ln(dy, pairs, ln, name, deps=()):
    t, d = dy.shape
    npair = len(pairs)
    tm = min(t, 512 // npair)
    nt = t // tm
    n_in = 1 + 2 * npair + (3 if ln is not None else 0)

    def body(*refs):
        dy_ref = refs[0]
        pr = refs[1:1 + 2 * npair]
        pos = 1 + 2 * npair
        dh = ALPHA * dy_ref[...]
        for p in range(npair):
            dh = dh + _dot_nt(pr[2 * p][...], pr[2 * p + 1][...])
        if ln is not None:
            xh_ref, rs_ref, g_ref = refs[pos:pos + 3]
            dyo_ref, dyb_ref, dg_ref, db_ref = refs[pos + 3:pos + 7]

            @pl.when(pl.program_id(0) == 0)
            def _():
                dg_ref[...] = jnp.zeros_like(dg_ref)
                db_ref[...] = jnp.zeros_like(db_ref)

            dyp, dg, db = _ln_bwd(dh, xh_ref[...], rs_ref[...], g_ref[...])
            dyo_ref[...] = dyp
            dyb_ref[...] = dyp.astype(BF16)
            dg_ref[...] += dg
            db_ref[...] += db
        else:
            refs[pos][...] = dh

    row = pl.BlockSpec((tm, d), lambda i: (i, 0))
    vec = pl.BlockSpec((1, d), lambda i: (0, 0))
    in_specs = [row]
    args = [dy]
    for lhs, w in pairs:
        in_specs += [pl.BlockSpec((tm, lhs.shape[1]), lambda i: (i, 0)), _resident(w)]
        args += [lhs, w]
    if ln is not None:
        in_specs += [row, pl.BlockSpec((tm, 1), lambda i: (i, 0)), vec]
        args += list(ln)
        out_specs = [row, row, vec, vec]
        out_shape = [jax.ShapeDtypeStruct((t, d), F32), jax.ShapeDtypeStruct((t, d), BF16),
                     jax.ShapeDtypeStruct((1, d), F32), jax.ShapeDtypeStruct((1, d), F32)]
    else:
        out_specs = row
        out_shape = jax.ShapeDtypeStruct((t, d), F32)
    return pl.pallas_call(
        _drop_deps(body, n_in, len(deps)),
        grid=(nt,),
        in_specs=in_specs + [ANY] * len(deps),
        out_specs=out_specs,
        out_shape=out_shape,
        compiler_params=_params(1),
        name=name,
    )(*args, *deps)


def _hgrn_bwd(proj, oraw, dmix, states, logits, gn):
    t = proj.shape[0]
    ct = _hg_tile(t)
    nct = t // ct
    nblk = ct // HG_BLOCK
    nh = HG_HEADS

    def body(q_ref, fz_ref, iv_ref, gg_ref, or_ref, do_ref, st_ref, lg_ref, gn_ref,
             dq_ref, dfz_ref, div_ref, dgg_ref, dlg_ref, dgn_ref,
             dstate, qt_s, kt_s, k_s, b_s, eb_s, ekb_s, dec_s, dor_s, dbl_s, gr_s, dk_s, dlb_acc):
        c = pl.program_id(1)

        @pl.when(c == 0)
        def _():
            dstate[...] = jnp.zeros_like(dstate)
            dlb_acc[...] = jnp.zeros_like(dlb_acc)
            dgn_ref[...] = jnp.zeros_like(dgn_ref)

        lb = _lower_bound(lg_ref[...])
        q = q_ref[...]
        sig, nsig, f, k = _forget_terms(fz_ref[...], lb)
        logf = jnp.log(f)
        lower, upper, whole = _block_masks(ct)
        b = _mask_dot(lower, logf)
        bend = _mask_dot(whole, logf)
        eb = jnp.exp(b)
        ekb = jnp.exp(bend - b)
        qt_s[...] = (q * eb).astype(BF16)
        kt_s[...] = (k * ekb).astype(BF16)
        k_s[...] = k
        b_s[...] = b
        eb_s[...] = eb
        ekb_s[...] = ekb
        dec_s[...] = jnp.exp(bend)
        oraw = or_ref[...]
        r = lax.rsqrt(jnp.mean(oraw * oraw, axis=-1, keepdims=True) + LN_EPS)
        on = oraw * r
        gg = gg_ref[...]
        silu, dsilu = _silu_and_grad(gg)
        doa = do_ref[...]
        gnv = gn_ref[...]
        dgg_ref[...] = (doa * on * gnv * dsilu).astype(BF16)
        dyn = doa * silu
        dgn_ref[...] += jnp.sum(dyn * on, axis=0, keepdims=True)
        don = dyn * gnv
        dor_s[...] = r * (don - on * jnp.mean(don * on, axis=-1, keepdims=True))
        tidx = lax.broadcasted_iota(jnp.int32, (HG_BLOCK, HG_DIM), 0)

        def blk(ii, carry):
            i = nblk - 1 - ii
            r0 = pl.multiple_of(i * HG_BLOCK, HG_BLOCK)
            rows = pl.ds(r0, HG_BLOCK)
            st = st_ref[i]
            dst = dstate[...]
            dstb = dst.astype(BF16)
            do = dor_s[rows, :]
            dob = do.astype(BF16)
            v = iv_ref[rows, :]
            vb = v.astype(BF16)
            qq = q_ref[rows, :]
            kk = k_s[rows, :]
            bb = b_s[rows, :]
            qt = qt_s[rows, :]
            kt = kt_s[rows, :]
            dec = dec_s[pl.ds(r0, 1), :]
            dkt = _dot(vb, dstb)
            dq = _dot(dob, st.astype(BF16)) * eb_s[rows, :]
            dk = dkt * ekb_s[rows, :]
            dv = _dot_nt(kt, dstb)
            gend = jnp.sum(kk * dk, axis=0, keepdims=True) + dec * jnp.sum(dst * st, axis=0, keepdims=True)
            for s in range(HG_BLOCK):
                ks = kk[s:s + 1, :]
                e = jnp.where(tidx >= s, jnp.exp(jnp.minimum(bb - bb[s:s + 1, :], 0.0)), 0.0)
                qe = qq * e
                acol = jnp.sum(qe * ks, axis=1, keepdims=True)
                dacol = jnp.sum(do * v[s:s + 1, :], axis=1, keepdims=True)
                dq = dq + dacol * (ks * e)
                dk_row = jnp.sum(dacol * qe, axis=0, keepdims=True)
                dv_row = jnp.sum(acol * do, axis=0, keepdims=True)
                dk = dk + jnp.where(tidx == s, dk_row, 0.0)
                dv = dv + jnp.where(tidx == s, dv_row, 0.0)
            dq_ref[rows, :] = dq.astype(BF16)
            div_ref[rows, :] = dv.astype(BF16)
            dk_s[rows, :] = dk
            dbl_s[rows, :] = qq * dq - kk * dk
            gr_s[rows, :] = jnp.zeros((HG_BLOCK, HG_DIM), F32) + gend
            dstate[...] = dst * dec + _dot_tn(dob, qt)
            return carry

        lax.fori_loop(0, nblk, blk, 0, unroll=HG_UNROLL)
        dlogf = _mask_dot(upper, dbl_s[...]) + gr_s[...]
        dk = dk_s[...]
        dfz_ref[...] = ((dlogf / f - dk) * ((1.0 - lb) * sig * nsig)).astype(BF16)
        dlb_acc[...] += jnp.sum((dlogf / f - dk) * nsig, axis=0, keepdims=True)

        @pl.when(c == nct - 1)
        def _():
            dl0 = dlb_acc[...] * lb * (1.0 - lb)
            layer = lax.broadcasted_iota(jnp.int32, (2, HG_DIM), 0)
            dlg_ref[...] = jnp.where(layer == 0, dl0, -dl0)

    def slab(off):
        return pl.BlockSpec((ct, HG_DIM), lambda h, c: (nct - 1 - c, off + h))

    out_slab = pl.BlockSpec((ct, HG_DIM), lambda h, c: (nct - 1 - c, h))
    tile_f32 = pltpu.VMEM((ct, HG_DIM), F32)
    tile_b16 = pltpu.VMEM((ct, HG_DIM), BF16)
    slab_shape = jax.ShapeDtypeStruct((t, nh * HG_DIM), BF16)
    return pl.pallas_call(
        body,
        grid=(nh, nct),
        in_specs=[slab(0), slab(nh), slab(2 * nh), slab(3 * nh), slab(0), slab(0),
                  pl.BlockSpec((None, nblk, HG_DIM, HG_DIM), lambda h, c: (h, nct - 1 - c, 0, 0)),
                  pl.BlockSpec((None, 2, HG_DIM), lambda h, c: (h, 0, 0)),
                  pl.BlockSpec((1, HG_DIM), lambda h, c: (0, 0))],
        out_specs=[out_slab, out_slab, out_slab, out_slab,
                   pl.BlockSpec((None, 2, HG_DIM), lambda h, c: (h, 0, 0)),
                   pl.BlockSpec((None, 1, HG_DIM), lambda h, c: (h, 0, 0))],
        out_shape=[slab_shape, slab_shape, slab_shape, slab_shape,
                   jax.ShapeDtypeStruct((nh, 2, HG_DIM), F32), jax.ShapeDtypeStruct((nh, 1, HG_DIM), F32)],
        scratch_shapes=[pltpu.VMEM((HG_DIM, HG_DIM), F32), tile_b16, tile_b16, tile_f32, tile_f32, tile_f32, tile_f32,
                        tile_f32, tile_f32, tile_f32, tile_f32, tile_f32, pltpu.VMEM((1, HG_DIM), F32)],
        compiler_params=_params(2),
        name="hgrn_bwd",
    )(proj, proj, proj, proj, oraw, dmix, states, logits, gn)


def _sgu_bwd(proj, dmix, ln_g, ln_b, w_s, w_t, b_col):
    t = proj.shape[0]
    ct = _sg_tile(t)
    nct = t // ct
    ng = SG_GROUPS
    off_u = 4 * HG_HEADS
    off_v = off_u + ng
    n = SG_CHUNK

    def body(u_ref, v_ref, do_ref, g_ref, b_ref, w_ref, wt_ref, bs_ref, du_ref, dv_ref, dg_ref, db_ref, dw_ref, dbs_ref):
        c = pl.program_id(1)

        @pl.when(c == 0)
        def _():
            dg_ref[...] = jnp.zeros_like(dg_ref)
            db_ref[...] = jnp.zeros_like(db_ref)
            dw_ref[...] = jnp.zeros_like(dw_ref)
            dbs_ref[...] = jnp.zeros_like(dbs_ref)

        r = lax.broadcasted_iota(jnp.int32, (n, n), 0)
        cc = lax.broadcasted_iota(jnp.int32, (n, n), 1)
        wm = jnp.where(cc <= r, w_ref[...], 0.0).astype(BF16)
        wmt = jnp.where(r <= cc, wt_ref[...], 0.0).astype(BF16)
        for ci in range(ct // n):
            rows = slice(ci * n, (ci + 1) * n)
            ua, dua, dva, vn, xhat, rstd, s = _sgu_chunk_fwd(u_ref[rows, :], v_ref[rows, :], g_ref[...], b_ref[...],
                                                             wm, bs_ref[...])
            do = do_ref[rows, :]
            du_ref[rows, :] = (do * s * dua).astype(BF16)
            ds = do * ua
            dsb = ds.astype(BF16)
            dbs_ref[...] += jnp.sum(ds, axis=1, keepdims=True)
            dw_ref[...] += _dot_nt(dsb, vn.astype(BF16))
            dvn = _dot(wmt, dsb)
            dva_in, dg, db = _ln_bwd(dvn, xhat, rstd, g_ref[...])
            dg_ref[...] += dg
            db_ref[...] += db
            dv_ref[rows, :] = (dva_in * dva).astype(BF16)

        @pl.when(c == nct - 1)
        def _():
            dw_ref[...] = jnp.where(cc <= r, dw_ref[...], 0.0)

    vec = pl.BlockSpec((None, 1, SG_DIM), lambda g, c: (g, 0, 0))
    mat = pl.BlockSpec((None, n, n), lambda g, c: (g, 0, 0))
    col = pl.BlockSpec((None, n, 1), lambda g, c: (g, 0, 0))
    out_slab = pl.BlockSpec((ct, SG_DIM), lambda g, c: (c, g))
    return pl.pallas_call(
        body,
        grid=(ng, nct),
        in_specs=[pl.BlockSpec((ct, SG_DIM), lambda g, c: (c, off_u + g)),
                  pl.BlockSpec((ct, SG_DIM), lambda g, c: (c, off_v + g)),
                  pl.BlockSpec((ct, SG_DIM), lambda g, c: (c, ng + g)), vec, vec, mat, mat, col],
        out_specs=[out_slab, out_slab, vec, vec, mat, col],
        out_shape=[jax.ShapeDtypeStruct((t, ng * SG_DIM), BF16), jax.ShapeDtypeStruct((t, ng * SG_DIM), BF16),
                   jax.ShapeDtypeStruct((ng, 1, SG_DIM), F32), jax.ShapeDtypeStruct((ng, 1, SG_DIM), F32),
                   jax.ShapeDtypeStruct((ng, n, n), F32), jax.ShapeDtypeStruct((ng, n, 1), F32)],
        compiler_params=_params(2),
        name="sgu_bwd",
    )(proj, proj, dmix, ln_g, ln_b, w_s, w_t, b_col)


def _attn_bwd(dyb, wo, qb, kb, vb):
    t, d = dyb.shape
    m_len = kb.shape[0]
    tm = _row_tile(t)
    dh = d // X_HEADS
    scale = dh ** -0.5

    def body(dy_ref, wo_ref, q_ref, k_ref, v_ref, dq_ref, dk_ref, dv_ref):
        i = pl.program_id(0)

        @pl.when(i == 0)
        def _():
            dk_ref[...] = jnp.zeros_like(dk_ref)
            dv_ref[...] = jnp.zeros_like(dv_ref)

        do = _dot_nt(dy_ref[...], wo_ref[...]).astype(BF16)
        for hd in range(X_HEADS):
            sl = slice(hd * dh, (hd + 1) * dh)
            qh = q_ref[:, sl]
            p = _softmax_rows(_dot_nt(qh, k_ref[:, sl]) * scale)
            doh = do[:, sl]
            dp = _dot_nt(doh, v_ref[:, sl])
            ds = (p * (dp - jnp.sum(dp * p, axis=-1, keepdims=True)) * scale).astype(BF16)
            dq_ref[:, sl] = _dot(ds, k_ref[:, sl]).astype(BF16)
            dk_ref[:, sl] += _dot_tn(ds, qh)
            dv_ref[:, sl] += _dot_tn(p.astype(BF16), doh)

    row = pl.BlockSpec((tm, d), lambda i: (i, 0))
    full = lambda a: pl.BlockSpec(a.shape, lambda i: (0, 0))
    kv = pl.BlockSpec((m_len, d), lambda i: (0, 0))
    return pl.pallas_call(
        body,
        grid=(t // tm,),
        in_specs=[row, full(wo), row, full(kb), full(vb)],
        out_specs=[row, kv, kv],
        out_shape=[jax.ShapeDtypeStruct((t, d), BF16), jax.ShapeDtypeStruct((m_len, d), F32),
                   jax.ShapeDtypeStruct((m_len, d), F32)],
        compiler_params=_params(1),
        name="attn_bwd",
    )(dyb, wo, qb, kb, vb)


def _mem_bwd(dk, dv, mb, xhat, rstd, g, wk, wv):
    m_len, d = dk.shape

    def body(dk_ref, dv_ref, mb_ref, xh_ref, rs_ref, g_ref, wk_ref, wv_ref, gwk_ref, gwv_ref, dg_ref, db_ref):
        dkb = dk_ref[...].astype(BF16)
        dvb = dv_ref[...].astype(BF16)
        mb_v = mb_ref[...]
        gwk_ref[...] = _dot_tn(mb_v, dkb).astype(BF16)
        gwv_ref[...] = _dot_tn(mb_v, dvb).astype(BF16)
        dm = _dot_nt(dkb, wk_ref[...]) + _dot_nt(dvb, wv_ref[...])
        _, dg, db = _ln_bwd(dm, xh_ref[...], rs_ref[...], g_ref[...])
        dg_ref[...] = dg
        db_ref[...] = db

    return pl.pallas_call(
        body,
        out_shape=[jax.ShapeDtypeStruct((d, d), BF16), jax.ShapeDtypeStruct((d, d), BF16),
                   jax.ShapeDtypeStruct((1, d), F32), jax.ShapeDtypeStruct((1, d), F32)],
        compiler_params=pltpu.CompilerParams(vmem_limit_bytes=VMEM_LIMIT_V7X),
        name="mem_bwd",
    )(dk, dv, mb, xhat, rstd, g, wk, wv)


def _adamw(w, g, m, v):
    m = ADAM_B1 * m + (1.0 - ADAM_B1) * g
    v = ADAM_B2 * v + (1.0 - ADAM_B2) * (g * g)
    m_hat = m / (1.0 - ADAM_B1 ** ADAM_STEP)
    v_hat = v / (1.0 - ADAM_B2 ** ADAM_STEP)
    delta = -ADAM_LR * (m_hat / (jnp.sqrt(v_hat) + ADAM_EPS) + ADAM_WD * w)
    return delta, m, v


def _slot_sum(ref):
    g = ref[0].astype(F32)
    for s in range(1, N_DEV):
        g = g + ref[s].astype(F32)
    return g


def _adam_sharded(lands, w, m, v, axis, name):
    rows, cols = w.shape
    if axis == 1:
        tr = 256 if rows % 256 == 0 else rows
        grid = (rows // tr,)
        wblk = pl.BlockSpec((tr, cols), lambda i: (i, 0))
        lblk = [pl.BlockSpec((N_DEV, tr, a.shape[2]), lambda i: (0, i, 0)) for a in lands]
    else:
        tc = _col_tile(cols)
        grid = (cols // tc,)
        wblk = pl.BlockSpec((rows, tc), lambda i: (0, i))
        lblk = [pl.BlockSpec((N_DEV, a.shape[1], tc), lambda i: (0, 0, i)) for a in lands]
    nl = len(lands)

    def body(*refs):
        w_ref, m_ref, v_ref = refs[nl:nl + 3]
        g_ref, d_ref, nm_ref, nv_ref = refs[nl + 3:]
        g = _slot_sum(refs[0])
        if nl == 2:
            tail = _slot_sum(refs[1])
            if axis == 1:
                g = jnp.concatenate([g, tail[:, :cols - g.shape[1]]], axis=1)
            else:
                g = jnp.concatenate([g, tail[:rows - g.shape[0], :]], axis=0)
        delta, nm, nv = _adamw(w_ref[...], g, m_ref[...], v_ref[...])
        g_ref[...] = g
        d_ref[...] = delta
        nm_ref[...] = nm
        nv_ref[...] = nv

    shp = jax.ShapeDtypeStruct((rows, cols), F32)
    return pl.pallas_call(
        body,
        grid=grid,
        in_specs=lblk + [wblk, wblk, wblk],
        out_specs=[wblk, wblk, wblk, wblk],
        out_shape=[shp, shp, shp, shp],
        compiler_params=_params(1),
        name=name,
    )(*lands, w, m, v)


def _mesh_pos():
    return lax.axis_index("x"), lax.axis_index("y"), lax.axis_index("c")


def _peer(k):
    x, y, c = _mesh_pos()
    pos = (x ^ (k >> 2), y ^ ((k >> 1) & 1), c ^ (k & 1))
    return pos, 4 * pos[0] + 2 * pos[1] + pos[2]


def _sem_index(row, k):
    return row * (N_DEV - 1) + k - 1


def _window(ref, axis, start, size):
    align = 16 if axis == 0 else LANES
    start = pl.multiple_of(start, align)
    return ref.at[pl.ds(start, size), :] if axis == 0 else ref.at[:, pl.ds(start, size)]


def _piece_refs(piece, srcs, lands, me, peer):
    kind, si, li, axis, base, stride, shape = piece
    if kind == "gather":
        return srcs[si], _window(lands[li], axis, base + stride * me, shape[axis])
    return _window(srcs[si], axis, base + stride * peer, shape[axis]), lands[li].at[me]


def _place_own(srcs, land_shapes, pieces, name):
    ns, nl, npc = len(srcs), len(land_shapes), len(pieces)

    def body(*refs):
        s_refs = refs[:ns]
        l_refs = refs[ns:ns + nl]
        bufs = refs[ns + nl:ns + nl + npc]
        sems = refs[ns + nl + npc]
        x, y, c = _mesh_pos()
        me = 4 * x + 2 * y + c
        loads = []
        for p, piece in enumerate(pieces):
            src, dst = _piece_refs(piece, s_refs, l_refs, me, me)
            cp = pltpu.make_async_copy(src, bufs[p], sems.at[0, p])
            cp.start()
            loads.append((cp, dst))
        stores = []
        for p, (cp, dst) in enumerate(loads):
            cp.wait()
            out = pltpu.make_async_copy(bufs[p], dst, sems.at[1, p])
            out.start()
            stores.append(out)
        for out in stores:
            out.wait()

    out = pl.pallas_call(
        body,
        in_specs=[ANY] * ns,
        out_specs=[ANY] * nl,
        out_shape=list(land_shapes),
        scratch_shapes=[pltpu.VMEM(pc[6], srcs[pc[1]].dtype) for pc in pieces] + [pltpu.SemaphoreType.DMA((2, npc))],
        compiler_params=pltpu.CompilerParams(vmem_limit_bytes=VMEM_LIMIT_V7X),
        name=name,
    )(*srcs)
    return list(out)


def _comm_start(srcs, lands, pieces, groups, name, after=()):
    ns, nl, na, ng = len(srcs), len(lands), len(after), len(groups)

    def body(*refs):
        s_refs = refs[:ns]
        l_refs = refs[ns:ns + nl]
        outs = refs[ns + nl + na:]
        sems = outs[:2 * ng]
        token = outs[-1]
        x, y, c = _mesh_pos()
        me = 4 * x + 2 * y + c
        for g, members in enumerate(groups):
            for row, p in enumerate(members):
                for k in range(1, N_DEV):
                    pos, peer = _peer(k)
                    src, dst = _piece_refs(pieces[p], s_refs, l_refs, me, peer)
                    pltpu.make_async_remote_copy(src_ref=src, dst_ref=dst, send_sem=sems[2 * g].at[_sem_index(row, k)],
                                                 recv_sem=sems[2 * g + 1].at[_sem_index(row, k)], device_id=pos,
                                                 device_id_type=MESH_ID).start()
        token[...] = jnp.zeros_like(token)

    sem_shapes = []
    for members in groups:
        sem_shapes += [pltpu.SemaphoreType.DMA((len(members) * (N_DEV - 1),))] * 2
    hbm_of = lambda a: pltpu.HBM(a.shape, a.dtype)
    out = pl.pallas_call(
        body,
        in_specs=[HBM] * (ns + nl) + [ANY] * na,
        out_specs=[SEM] * (2 * ng) + [HBM] * (ns + nl) + [pl.BlockSpec(memory_space=pltpu.VMEM)],
        out_shape=sem_shapes + [hbm_of(a) for a in srcs] + [hbm_of(a) for a in lands]
        + [jax.ShapeDtypeStruct((8, LANES), F32)],
        input_output_aliases={i: 2 * ng + i for i in range(ns + nl)},
        compiler_params=pltpu.CompilerParams(has_side_effects=DATAFLOW),
        name=name,
    )(*[pltpu.with_memory_space_constraint(a, pltpu.HBM) for a in list(srcs) + list(lands)], *after)
    sems = [(out[2 * g], out[2 * g + 1]) for g in range(ng)]
    return sems, list(out[2 * ng:2 * ng + ns]), list(out[2 * ng + ns:2 * ng + ns + nl]), out[-1]


def _comm_wait(srcs, lands, pieces, members, sems, after, name):
    ns, nl, na = len(srcs), len(lands), len(after)

    def body(*refs):
        s_refs = refs[:ns]
        l_refs = refs[ns:ns + nl]
        send_sems, recv_sems = refs[ns + nl:ns + nl + 2]
        x, y, c = _mesh_pos()
        me = 4 * x + 2 * y + c
        for row, p in enumerate(members):
            for k in range(1, N_DEV):
                pos, peer = _peer(k)
                src, dst = _piece_refs(pieces[p], s_refs, l_refs, me, peer)
                cp = pltpu.make_async_remote_copy(src_ref=src, dst_ref=dst, send_sem=send_sems.at[_sem_index(row, k)],
                                                  recv_sem=recv_sems.at[_sem_index(row, k)], device_id=pos,
                                                  device_id_type=MESH_ID)
                cp.wait_send()
                cp.wait_recv()

    hbm_of = lambda a: pltpu.HBM(a.shape, a.dtype)
    out = pl.pallas_call(
        body,
        in_specs=[HBM] * (ns + nl) + [SEM, SEM] + [ANY] * na,
        out_specs=[HBM] * (ns + nl),
        out_shape=[hbm_of(a) for a in srcs] + [hbm_of(a) for a in lands],
        input_output_aliases={i: i for i in range(ns + nl)},
        compiler_params=pltpu.CompilerParams(has_side_effects=DATAFLOW),
        name=name,
    )(*srcs, *lands, sems[0], sems[1], *after)
    return list(out[ns:])


_SMALL_NAMES = ("ln1_g", "ln1_b", "hg_lb_logits", "hg_norm_g", "sg_ln_g", "sg_ln_b", "sg_w_s", "sg_b_s",
                "ln2_g", "ln2_b", "mem_ln_g", "mem_ln_b", "ln3_g", "ln3_b", "ln4_g", "ln4_b")


_VEC_NAMES = ("ln1_g", "ln1_b", "ln2_g", "ln2_b", "mem_ln_g", "mem_ln_b", "ln3_g", "ln3_b", "ln4_g", "ln4_b")
_ROW_NAMES = ("hg_lb_logits", "hg_norm_g", "sg_ln_g", "sg_ln_b", "sg_b_s", "sg_w_s")
VEC_ROWS = 16


def _row_plan(shapes):
    plan, pos = {}, 0
    for k in _ROW_NAMES:
        shp = shapes[k]
        slabs, off = [], pos
        for idx in itertools.product(*[range(dim) for dim in shp[:-2]]):
            slabs.append((idx, off, shp[-2]))
            off += shp[-2]
        plan[k] = (pos, slabs)
        pos = -(-off // 8) * 8
    return plan, -(-pos // 16) * 16


def _pack_small_grads(gs, shapes):
    vec = jnp.concatenate([gs[k].reshape(1, -1) for k in _VEC_NAMES], axis=0)
    vec = jnp.pad(vec, ((0, VEC_ROWS - vec.shape[0]), (0, 0)))
    plan, total = _row_plan(shapes)
    parts, pos = [], 0
    for k in _ROW_NAMES:
        first, slabs = plan[k]
        rows = gs[k].reshape(-1, LANES)
        end = slabs[-1][1] + slabs[-1][2]
        nxt = -(-end // 8) * 8
        parts.append(jnp.pad(rows, ((0, nxt - first - rows.shape[0]), (0, 0))))
        pos = nxt
    parts.append(jnp.zeros((total - pos, LANES), F32))
    return vec, jnp.concatenate(parts, axis=0)


def _adam_small(land_vec, land_rows, w, m, v):
    names = _VEC_NAMES + _ROW_NAMES
    n = len(names)
    shapes = {k: w[k].shape for k in names}
    plan, _ = _row_plan(shapes)

    def body(*refs):
        lv_ref, lr_ref = refs[:2]
        w_refs, m_refs, v_refs = refs[2:2 + n], refs[2 + n:2 + 2 * n], refs[2 + 2 * n:2 + 3 * n]
        outs = refs[2 + 3 * n:2 + 7 * n]
        gv_s, gr_s = refs[2 + 7 * n:]
        gv_s[...] = _slot_sum(lv_ref)
        gr_s[...] = _slot_sum(lr_ref)
        for p, k in enumerate(names):
            if k in _VEC_NAMES:
                row = _VEC_NAMES.index(k)
                slabs = [((), None, None)]
            else:
                slabs = plan[k][1]
            for idx, off, rows in slabs:
                g = gv_s[row:row + 1, :] if off is None else gr_s[off:off + rows, :]
                sel = idx + (slice(None), slice(None))
                delta, nm, nv = _adamw(w_refs[p][sel], g, m_refs[p][sel], v_refs[p][sel])
                for o, val in zip(range(4), (g, delta, nm, nv)):
                    outs[o * n + p][sel] = val

    flat = lambda tree: [tree[k] for k in names]
    shp = [jax.ShapeDtypeStruct(shapes[k], F32) for k in names]
    out = pl.pallas_call(
        body,
        out_shape=shp * 4,
        scratch_shapes=[pltpu.VMEM(land_vec.shape[1:], F32), pltpu.VMEM(land_rows.shape[1:], F32)],
        name="adam_small",
    )(land_vec, land_rows, *flat(w), *flat(m), *flat(v))
    return [dict(zip(names, out[o * n:(o + 1) * n])) for o in range(4)]


_COL_FFN = ("ffn1_w_gate", "ffn1_w_up", "ffn2_w_gate", "ffn2_w_up")
_ROW_FFN = ("ffn1_w_down", "ffn2_w_down")
_ROW_SQ = ("w_out", "xa_w_q", "xa_w_k", "xa_w_v", "xa_w_o")
_BIG_NAMES = ("ffn1_w_gate", "ffn1_w_up", "ffn1_w_down", "w_in", "w_out", "xa_w_q", "xa_w_k", "xa_w_v", "xa_w_o",
              "ffn2_w_gate", "ffn2_w_up", "ffn2_w_down")


def _ffn_split(fs):
    main = (fs // MXU_WIDTH_V7X) * MXU_WIDTH_V7X
    tail = fs - main
    tail_pad = -(-tail // LANES) * LANES
    assert main > 0 and tail > 0
    return main, tail, tail_pad


def _layout(name, shard_shape):
    r, c = shard_shape
    if name in _COL_FFN:
        main, tail, pad = _ffn_split(c)
        return (r, N_DEV * (main + pad)), [(1, 0, main, (r, main), (0, main)),
                                           (1, N_DEV * main, pad, (r, pad), (main, c))]
    if name in _ROW_FFN:
        main, tail, pad = _ffn_split(r)
        return (N_DEV * (main + pad), c), [(0, 0, main, (main, c), (0, main)),
                                           (0, N_DEV * main, pad, (pad, c), (main, r))]
    if name == "w_in":
        return (r, N_DEV * c), [(1, 0, c, (r, c), (0, c))]
    return (N_DEV * r, c), [(0, 0, r, (r, c), (0, r))]


def _shard_pieces(name, shard):
    out = []
    for axis, _, _, shape, (lo, hi) in _layout(name, shard.shape)[1]:
        part = shard[lo:hi, :] if axis == 0 else shard[:, lo:hi]
        pad = [(0, shape[0] - part.shape[0]), (0, shape[1] - part.shape[1])]
        out.append(jnp.pad(part, pad).astype(BF16))
    return out


def _gather_plan(names, shards):
    srcs, land_shapes, pieces, index = [], [], [], {}
    for li, name in enumerate(names):
        shape2d, parts = _layout(name, shards[name].shape)
        land_shapes.append(jax.ShapeDtypeStruct(shape2d, BF16))
        index[name] = []
        for (axis, base, stride, shape, _), src in zip(parts, _shard_pieces(name, shards[name])):
            index[name].append(len(pieces))
            pieces.append(("gather", len(srcs), li, axis, base, stride, shape))
            srcs.append(src)
    return srcs, land_shapes, pieces, index


def _scatter_plan(names, grads, shard_shapes):
    srcs, land_shapes, pieces, index = [], [], [], {}
    for si, name in enumerate(names):
        _, parts = _layout(name, shard_shapes[name])
        srcs.append(grads[name])
        index[name] = []
        for axis, base, stride, shape, _ in parts:
            index[name].append(len(land_shapes))
            pieces.append(("scatter", si, len(land_shapes), axis, base, stride, shape))
            land_shapes.append(jax.ShapeDtypeStruct((N_DEV,) + shape, grads[name].dtype))
    return srcs, land_shapes, pieces, index


def _small_views(small):
    row = lambda a: a.reshape(1, -1)
    ln = {k: row(small[k]) for k in ("ln1_g", "ln1_b", "ln2_g", "ln2_b", "ln3_g", "ln3_b", "ln4_g", "ln4_b",
                                      "mem_ln_g", "mem_ln_b", "hg_norm_g")}
    sg_w = small["sg_w_s"].reshape(SG_GROUPS, SG_CHUNK, SG_CHUNK)
    sg = dict(logits=jnp.swapaxes(small["hg_lb_logits"], 0, 1),
              g=small["sg_ln_g"].reshape(SG_GROUPS, 1, SG_DIM), b=small["sg_ln_b"].reshape(SG_GROUPS, 1, SG_DIM),
              w=sg_w, wt=jnp.swapaxes(sg_w, 1, 2), bs=small["sg_b_s"].reshape(SG_GROUPS, SG_CHUNK, 1))
    return ln, sg


def _forward(x, mem, get_w, small, first_deps=()):
    ln, sg = _small_views(small)
    xb = x.astype(BF16)
    a1, b1, s1 = _ffn_up(xb, get_w("ffn1_w_gate", ()), get_w("ffn1_w_up", ()), "ffn1_up", deps=first_deps)
    h1, h1b, xh1, rs1 = _mm_res_ln(s1, get_w("ffn1_w_down", (s1,)), x, ln["ln1_g"], ln["ln1_b"], 0.5, "ffn1_down_ln")
    proj = _mm_nn(h1b, get_w("w_in", (s1,)), "mix_in")
    oraw, mix, states = _hgrn_fwd(proj, sg["logits"], ln["hg_norm_g"])
    mix = _sgu_fwd(proj, mix, sg["g"], sg["b"], sg["w"], sg["bs"])
    h2, h2b, xh2, rs2 = _mm_res_ln(mix, get_w("w_out", (s1,)), h1, ln["ln2_g"], ln["ln2_b"], 1.0, "mix_out_ln")
    mb, mxh, mrs, kb, vb = _mem_kv(mem, ln["mem_ln_g"], ln["mem_ln_b"], get_w("xa_w_k", (h2b,)), get_w("xa_w_v", (h2b,)))
    qb, att = _attn_fwd(h2b, get_w("xa_w_q", (h2b,)), kb, vb)
    h3, h3b, xh3, rs3 = _mm_res_ln(att, get_w("xa_w_o", (h2b,)), h2, ln["ln3_g"], ln["ln3_b"], 1.0, "attn_out_ln")
    a2, b2, s2 = _ffn_up(h3b, get_w("ffn2_w_gate", (h2b,)), get_w("ffn2_w_up", (h2b,)), "ffn2_up")
    h4, _, xh4, rs4 = _mm_res_ln(s2, get_w("ffn2_w_down", (h2b,)), h3, ln["ln4_g"], ln["ln4_b"], 0.5, "ffn2_down_ln")
    return dict(xb=xb, a1=a1, b1=b1, s1=s1, h1b=h1b, xh1=xh1, rs1=rs1, proj=proj, oraw=oraw, mix=mix, states=states,
                h2b=h2b, xh2=xh2, rs2=rs2, mb=mb, mxh=mxh, mrs=mrs, kb=kb, vb=vb, qb=qb, att=att, h3b=h3b, xh3=xh3,
                rs3=rs3, a2=a2, b2=b2, s2=s2, h4=h4, xh4=xh4, rs4=rs4)


def _backward(sv, target, wt, small, send):
    ln, sg = _small_views(small)
    gs = {}
    loss, dy4, dy4b, gs["ln4_g"], gs["ln4_b"] = _loss_ln_bwd(sv["h4"], sv["xh4"], sv["rs4"], ln["ln4_g"], target)
    g_down2 = _mm_tn(sv["s2"], dy4b, "g_ffn2_down", scale=0.5)
    da2, db2 = _ffn_bwd_act(dy4b, wt["ffn2_w_down"], sv["a2"], sv["b2"], 0.5, "ffn2_bwd_act")
    g_gate2 = _mm_tn(sv["h3b"], da2, "g_ffn2_gate")
    g_up2 = _mm_tn(sv["h3b"], db2, "g_ffn2_up")
    tok = send(("ffn2_w_down", "ffn2_w_gate", "ffn2_w_up"), (g_down2, g_gate2, g_up2))
    dy3, dy3b, gs["ln3_g"], gs["ln3_b"] = _dx_ln(dy4, [(da2, wt["ffn2_w_gate"]), (db2, wt["ffn2_w_up"])],
                                                 (sv["xh3"], sv["rs3"], ln["ln3_g"]), "ffn2_dx_ln", deps=(tok,))

    g_o = _mm_tn(sv["att"], dy3b, "g_xa_o")
    dqb, dk, dv = _attn_bwd(dy3b, wt["xa_w_o"], sv["qb"], sv["kb"], sv["vb"])
    g_q = _mm_tn(sv["h2b"], dqb, "g_xa_q")
    g_k, g_v, gs["mem_ln_g"], gs["mem_ln_b"] = _mem_bwd(dk, dv, sv["mb"], sv["mxh"], sv["mrs"], ln["mem_ln_g"],
                                                        wt["xa_w_k"], wt["xa_w_v"])
    tok = send(("xa_w_o", "xa_w_q", "xa_w_k", "xa_w_v"), (g_o, g_q, g_k, g_v))
    dy2, dy2b, gs["ln2_g"], gs["ln2_b"] = _dx_ln(dy3, [(dqb, wt["xa_w_q"])], (sv["xh2"], sv["rs2"], ln["ln2_g"]),
                                                 "attn_dx_ln", deps=(tok,))

    g_out = _mm_tn(sv["mix"], dy2b, "g_w_out")
    dmix = _mm_nt(dy2b, wt["w_out"], "mix_out_bwd")
    dq, dfz, div, dgg, dlg, dgn = _hgrn_bwd(sv["proj"], sv["oraw"], dmix, sv["states"], sg["logits"], ln["hg_norm_g"])
    du, dvv, gs["sg_ln_g"], gs["sg_ln_b"], gs["sg_w_s"], gs["sg_b_s"] = _sgu_bwd(
        sv["proj"], dmix, sg["g"], sg["b"], sg["w"], sg["wt"], sg["bs"])
    gs["hg_lb_logits"] = jnp.swapaxes(dlg, 0, 1)
    gs["hg_norm_g"] = jnp.sum(dgn, axis=0)
    dproj = jnp.concatenate([dq, dfz, div, dgg, du, dvv], axis=1)
    g_in = _mm_tn(sv["h1b"], dproj, "g_w_in")
    tok = send(("w_out", "w_in"), (g_out, g_in))
    dy1, dy1b, gs["ln1_g"], gs["ln1_b"] = _dx_ln(dy2, [(dproj, wt["w_in"])], (sv["xh1"], sv["rs1"], ln["ln1_g"]),
                                                 "mix_dx_ln", deps=(tok,))

    g_down1 = _mm_tn(sv["s1"], dy1b, "g_ffn1_down", scale=0.5)
    da1, db1 = _ffn_bwd_act(dy1b, wt["ffn1_w_down"], sv["a1"], sv["b1"], 0.5, "ffn1_bwd_act")
    g_gate1 = _mm_tn(sv["xb"], da1, "g_ffn1_gate")
    g_up1 = _mm_tn(sv["xb"], db1, "g_ffn1_up")
    tok = send(("ffn1_w_down", "ffn1_w_gate", "ffn1_w_up"), (g_down1, g_gate1, g_up1))
    grad_x = _dx_ln(dy1, [(da1, wt["ffn1_w_gate"]), (db1, wt["ffn1_w_up"])], None, "ffn1_dx", deps=(tok,))
    return loss, grad_x, gs


_WEIGHT_NAMES = ("ffn1_w_gate", "ffn1_w_up", "ffn1_w_down", "ln1_g", "ln1_b", "w_in", "hg_lb_logits", "hg_norm_g",
                 "sg_ln_g", "sg_ln_b", "sg_w_s", "sg_b_s", "w_out", "ln2_g", "ln2_b", "mem_ln_g", "mem_ln_b",
                 "xa_w_q", "xa_w_k", "xa_w_v", "xa_w_o", "ln3_g", "ln3_b", "ffn2_w_gate", "ffn2_w_up", "ffn2_w_down",
                 "ln4_g", "ln4_b")
_FIRST = ("ffn1_w_gate", "ffn1_w_up")
_SECOND = ("ffn1_w_down", "w_in", "w_out")
_THIRD = ("xa_w_k", "xa_w_v", "xa_w_q", "xa_w_o", "ffn2_w_gate", "ffn2_w_up", "ffn2_w_down")


def kernel(x, mem, ffn1_w_gate, ffn1_w_up, ffn1_w_down, ln1_g, ln1_b, w_in, hg_lb_logits, hg_norm_g, sg_ln_g, sg_ln_b, sg_w_s, sg_b_s, w_out, ln2_g, ln2_b, mem_ln_g, mem_ln_b, xa_w_q, xa_w_k, xa_w_v, xa_w_o, ln3_g, ln3_b, ffn2_w_gate, ffn2_w_up, ffn2_w_down, ln4_g, ln4_b, loss_target, m_ffn1_w_gate, m_ffn1_w_up, m_ffn1_w_down, m_ln1_g, m_ln1_b, m_w_in, m_hg_lb_logits, m_hg_norm_g, m_sg_ln_g, m_sg_ln_b, m_sg_w_s, m_sg_b_s, m_w_out, m_ln2_g, m_ln2_b, m_mem_ln_g, m_mem_ln_b, m_xa_w_q, m_xa_w_k, m_xa_w_v, m_xa_w_o, m_ln3_g, m_ln3_b, m_ffn2_w_gate, m_ffn2_w_up, m_ffn2_w_down, m_ln4_g, m_ln4_b, v_ffn1_w_gate, v_ffn1_w_up, v_ffn1_w_down, v_ln1_g, v_ln1_b, v_w_in, v_hg_lb_logits, v_hg_norm_g, v_sg_ln_g, v_sg_ln_b, v_sg_w_s, v_sg_b_s, v_w_out, v_ln2_g, v_ln2_b, v_mem_ln_g, v_mem_ln_b, v_xa_w_q, v_xa_w_k, v_xa_w_v, v_xa_w_o, v_ln3_g, v_ln3_b, v_ffn2_w_gate, v_ffn2_w_up, v_ffn2_w_down, v_ln4_g, v_ln4_b):
    args = dict(locals())
    w = {k: args[k] for k in _WEIGHT_NAMES}
    m = {k: args["m_" + k] for k in _WEIGHT_NAMES}
    v = {k: args["v_" + k] for k in _WEIGHT_NAMES}
    shards = {k: w[k][0] for k in _BIG_NAMES}
    shard_shapes = {k: shards[k].shape for k in _BIG_NAMES}
    small = {k: (w[k][0] if k != "hg_lb_logits" else w[k]) for k in _SMALL_NAMES}

    srcs1, shapes1, pieces1, idx1 = _gather_plan(_FIRST, shards)
    lands1 = _place_own(srcs1, shapes1, pieces1, "gather_first_own")
    sems1, srcs1, lands1, _ = _comm_start(srcs1, lands1, pieces1, [list(range(len(pieces1)))], "gather_first_start")
    lands1 = _comm_wait(srcs1, lands1, pieces1, list(range(len(pieces1))), sems1[0], (), "gather_first_wait")
    wt = dict(zip(_FIRST, lands1))

    rest = _SECOND + _THIRD
    srcs2, shapes2, pieces2, idx2 = _gather_plan(rest, shards)
    lands2 = _place_own(srcs2, shapes2, pieces2, "gather_rest_own")
    groups2 = [[p for k in names for p in idx2[k]] for names in (_SECOND, _THIRD)]
    sems2, srcs2, lands2, tok2 = _comm_start(srcs2, lands2, pieces2, groups2, "gather_rest_start", after=tuple(lands1))
    pending = {}
    for gi, names in enumerate((_SECOND, _THIRD)):
        for k in names:
            pending[k] = gi

    def get_w(name, after):
        if name in pending:
            gi = pending[name]
            names = (_SECOND, _THIRD)[gi]
            li = [rest.index(k) for k in names]
            si = sorted({pieces2[p][1] for p in groups2[gi]})
            remap = {s: i for i, s in enumerate(si)}
            lmap = {l: i for i, l in enumerate(li)}
            sub = [(pc[0], remap[pc[1]], lmap[pc[2]]) + pc[3:] for pc in (pieces2[p] for p in groups2[gi])]
            got = _comm_wait([srcs2[s] for s in si], [lands2[l] for l in li], sub, list(range(len(sub))), sems2[gi],
                             after, "gather_rest_wait_%d" % gi)
            for k, arr in zip(names, got):
                wt[k] = arr
                del pending[k]
        return wt[name]

    sv = _forward(x[0], mem[0], get_w, small, first_deps=(tok2,))

    sent = []

    def send(names, grads):
        srcs, shapes, pieces, idx = _scatter_plan(names, dict(zip(names, grads)), shard_shapes)
        lands = _place_own(srcs, shapes, pieces, "grads_own_%d" % len(sent))
        sems, srcs, lands, tok = _comm_start(srcs, lands, pieces, [list(range(len(pieces)))],
                                             "grads_start_%d" % len(sent))
        sent.append((names, srcs, lands, pieces, idx, sems[0]))
        return tok

    loss, grad_x, gs = _backward(sv, loss_target[0], wt, small, send)

    ssrc = list(_pack_small_grads(gs, {k: w[k].shape for k in _SMALL_NAMES}))
    sp = [("scatter", i, i, 0, 0, 0, a.shape) for i, a in enumerate(ssrc)]
    sshape = [jax.ShapeDtypeStruct((N_DEV,) + a.shape, F32) for a in ssrc]
    sl = _place_own(ssrc, sshape, sp, "small_own")
    ssem, ssrc, sl, _ = _comm_start(ssrc, sl, sp, [[0, 1]], "small_start")

    out_g, out_d, out_m, out_v = {}, {}, {}, {}
    after = (grad_x,)
    for n_sent, (names, srcs, lands, pieces, idx, sems) in enumerate(sent):
        lands = _comm_wait(srcs, lands, pieces, list(range(len(pieces))), sems, after, "grads_wait_%d" % n_sent)
        for k in names:
            axis = 1 if (k in _COL_FFN or k == "w_in") else 0
            res = _adam_sharded([lands[i] for i in idx[k]], w[k][0], m[k][0], v[k][0], axis, "adam_" + k)
            out_g[k], out_d[k], out_m[k], out_v[k] = [r[None] for r in res]
        after = (out_v[names[-1]],)
    sl = _comm_wait(ssrc, sl, sp, [0, 1], ssem[0], after, "small_wait")
    for dst, res in zip((out_g, out_d, out_m, out_v), _adam_small(sl[0], sl[1], w, m, v)):
        dst.update(res)

    loss_all = lax.psum(loss[0, 0], ("x", "y", "c"))
    return (loss_all, grad_x[None], *[out_g[k] for k in _WEIGHT_NAMES], *[out_d[k] for k in _WEIGHT_NAMES],
            *[out_m[k] for k in _WEIGHT_NAMES], *[out_v[k] for k in _WEIGHT_NAMES])
```

```python
import itertools

import jax
import jax.numpy as jnp
from jax import lax
from jax.experimental import pallas as pl
from jax.experimental.pallas import tpu as pltpu

F32 = jnp.float32
BF16 = jnp.bfloat16

N_DEV = 8
ALPHA = 2.0 ** 0.25
LN_EPS = 1e-5
HG_HEADS = 4
HG_DIM = 128
SG_GROUPS = 4
SG_DIM = 128
SG_CHUNK = 128
X_HEADS = 4
HG_BLOCK = 16
HG_UNROLL = 4
ADAM_LR = 0.001
ADAM_B1 = 0.9
ADAM_B2 = 0.999
ADAM_EPS = 1e-08
ADAM_WD = 0.01
ADAM_STEP = 10
VMEM_LIMIT_V7X = 48 * 1024 * 1024
MXU_WIDTH_V7X = 256
LANES = 128
MESH_ID = pl.DeviceIdType.MESH
ANY = pl.BlockSpec(memory_space=pl.ANY)
HBM = pl.BlockSpec(memory_space=pltpu.HBM)
SEM = pl.BlockSpec(memory_space=pltpu.SEMAPHORE)
DATAFLOW = pltpu.SideEffectType.DATAFLOW_SIDE_EFFECTING


def _params(n_axes):
    return pltpu.CompilerParams(dimension_semantics=("arbitrary",) * n_axes, vmem_limit_bytes=VMEM_LIMIT_V7X)


def _dot(a, b):
    return jnp.dot(a, b, preferred_element_type=F32)


def _dot_nt(a, b):
    return lax.dot_general(a, b, (((1,), (1,)), ((), ())), preferred_element_type=F32)


def _dot_tn(a, b):
    return lax.dot_general(a, b, (((0,), (0,)), ((), ())), preferred_element_type=F32)


def _sigmoid(x):
    return 1.0 / (1.0 + jnp.exp(-x))


def _silu_and_grad(a):
    sig = _sigmoid(a)
    return a * sig, sig * (1.0 + a * (1.0 - sig))


_GELU_C = 0.7978845608028654


def _gelu_and_grad(x):
    inner = _GELU_C * (x + 0.044715 * x * x * x)
    t = jnp.tanh(inner)
    val = 0.5 * x * (1.0 + t)
    grad = 0.5 * (1.0 + t) + 0.5 * x * (1.0 - t * t) * _GELU_C * (1.0 + 3.0 * 0.044715 * x * x)
    return val, grad


def _ln_fwd(y, g, b):
    mu = jnp.mean(y, axis=-1, keepdims=True)
    yc = y - mu
    var = jnp.mean(yc * yc, axis=-1, keepdims=True)
    rstd = lax.rsqrt(var + LN_EPS)
    xhat = yc * rstd
    return xhat * g + b, xhat, rstd


def _ln_bwd(dh, xhat, rstd, g):
    dxh = dh * g
    m1 = jnp.mean(dxh, axis=-1, keepdims=True)
    m2 = jnp.mean(dxh * xhat, axis=-1, keepdims=True)
    dy = rstd * (dxh - m1 - xhat * m2)
    dg = jnp.sum(dh * xhat, axis=0, keepdims=True)
    db = jnp.sum(dh, axis=0, keepdims=True)
    return dy, dg, db


def _split3(x):
    hi = x.astype(BF16)
    r1 = x - hi.astype(F32)
    mid = r1.astype(BF16)
    lo = (r1 - mid.astype(F32)).astype(BF16)
    return hi, mid, lo


def _mask_dot(mask, x):
    hi, mid, lo = _split3(x)
    return _dot(mask, hi) + _dot(mask, mid) + _dot(mask, lo)


def _block_masks(n):
    r = lax.broadcasted_iota(jnp.int32, (n, n), 0)
    c = lax.broadcasted_iota(jnp.int32, (n, n), 1)
    assert HG_BLOCK & (HG_BLOCK - 1) == 0
    same = (r & -HG_BLOCK) == (c & -HG_BLOCK)
    one = jnp.ones((n, n), BF16)
    zero = jnp.zeros((n, n), BF16)
    lower = jnp.where(same & (c <= r), one, zero)
    upper = jnp.where(same & (c >= r), one, zero)
    whole = jnp.where(same, one, zero)
    return lower, upper, whole


def _row_tile(t):
    return min(t, 512)


def _col_tile(n):
    for cand in (512, 256, 128):
        if n % cand == 0:
            return cand
    return n


def _resident(w):
    return pl.BlockSpec(w.shape, lambda *_: (0, 0), pipeline_mode=pl.Buffered(1))


def _drop_deps(body, n_in, n_deps):
    if n_deps == 0:
        return body
    return lambda *refs: body(*refs[:n_in], *refs[n_in + n_deps:])


def _ffn_up(hb, wg, wu, name, deps=()):
    t, d = hb.shape
    f = wg.shape[1]
    tm = min(t, 256)
    tn = _col_tile(f)

    def body(h_ref, wg_ref, wu_ref, a_ref, b_ref, s_ref):
        h = h_ref[...]
        for c in range(f // tn):
            cols = slice(c * tn, (c + 1) * tn)
            a = _dot(h, wg_ref[:, cols])
            b = _dot(h, wu_ref[:, cols])
            a_ref[:, cols] = a.astype(BF16)
            b_ref[:, cols] = b.astype(BF16)
            s_ref[:, cols] = (a * _sigmoid(a) * b).astype(BF16)

    act = pl.BlockSpec((tm, f), lambda i: (i, 0))
    return pl.pallas_call(
        _drop_deps(body, 3, len(deps)),
        grid=(t // tm,),
        in_specs=[pl.BlockSpec((tm, d), lambda i: (i, 0)), _resident(wg), _resident(wu)] + [ANY] * len(deps),
        out_specs=[act, act, act],
        out_shape=[jax.ShapeDtypeStruct((t, f), BF16)] * 3,
        compiler_params=_params(1),
        name=name,
    )(hb, wg, wu, *deps)


def _mm_res_ln(lhs, w, res, g, b, coef, name):
    t, kd = lhs.shape
    d = w.shape[1]
    tm = _row_tile(t)

    def body(l_ref, w_ref, r_ref, g_ref, b_ref, h_ref, hb_ref, xh_ref, rs_ref):
        y = ALPHA * r_ref[...] + coef * _dot(l_ref[...], w_ref[...])
        h, xhat, rstd = _ln_fwd(y, g_ref[...], b_ref[...])
        h_ref[...] = h
        hb_ref[...] = h.astype(BF16)
        xh_ref[...] = xhat
        rs_ref[...] = rstd

    row = pl.BlockSpec((tm, d), lambda i: (i, 0))
    vec = pl.BlockSpec((1, d), lambda i: (0, 0))
    return pl.pallas_call(
        body,
        grid=(t // tm,),
        in_specs=[pl.BlockSpec((tm, kd), lambda i: (i, 0)), _resident(w), row, vec, vec],
        out_specs=[row, row, row, pl.BlockSpec((tm, 1), lambda i: (i, 0))],
        out_shape=[jax.ShapeDtypeStruct((t, d), F32), jax.ShapeDtypeStruct((t, d), BF16),
                   jax.ShapeDtypeStruct((t, d), F32), jax.ShapeDtypeStruct((t, 1), F32)],
        compiler_params=_params(1),
        name=name,
    )(lhs, w, res, g, b)


def _mm_nn(lhs, w, name):
    t, kd = lhs.shape
    n = w.shape[1]
    tm = _row_tile(t)
    tn = _col_tile(n)

    def body(l_ref, w_ref, o_ref):
        lhs_v = l_ref[...]
        for c in range(n // tn):
            cols = slice(c * tn, (c + 1) * tn)
            o_ref[:, cols] = _dot(lhs_v, w_ref[:, cols])

    return pl.pallas_call(
        body,
        grid=(t // tm,),
        in_specs=[pl.BlockSpec((tm, kd), lambda i: (i, 0)), _resident(w)],
        out_specs=pl.BlockSpec((tm, n), lambda i: (i, 0)),
        out_shape=jax.ShapeDtypeStruct((t, n), F32),
        compiler_params=_params(1),
        name=name,
    )(lhs, w)


def _lower_bound(lg):
    m = jnp.max(lg, axis=0, keepdims=True)
    e = jnp.exp(lg - m)
    return e[0:1, :] / jnp.sum(e, axis=0, keepdims=True)


def _forget_terms(fz, lb):
    e = jnp.exp(-jnp.abs(fz))
    r = 1.0 / (1.0 + e)
    pos = fz >= 0.0
    sig = jnp.where(pos, r, e * r)
    nsig = jnp.where(pos, e * r, r)
    f = lb + (1.0 - lb) * sig
    k = (1.0 - lb) * nsig
    return sig, nsig, f, k


def _hg_tile(t):
    return min(t, 256)


def _hgrn_fwd(proj, logits, gn):
    t = proj.shape[0]
    ct = _hg_tile(t)
    nct = t // ct
    nblk = ct // HG_BLOCK
    nh = HG_HEADS

    def body(q_ref, fz_ref, iv_ref, gg_ref, lg_ref, gn_ref, oraw_ref, oa_ref, st_ref,
             state, qt_s, kt_s, k_s, b_s, dec_s):
        c = pl.program_id(1)

        @pl.when(c == 0)
        def _():
            state[...] = jnp.zeros_like(state)

        lb = _lower_bound(lg_ref[...])
        q = q_ref[...]
        _, _, f, k = _forget_terms(fz_ref[...], lb)
        logf = jnp.log(f)
        lower, _, whole = _block_masks(ct)
        b = _mask_dot(lower, logf)
        bend = _mask_dot(whole, logf)
        qt_s[...] = (q * jnp.exp(b)).astype(BF16)
        kt_s[...] = (k * jnp.exp(bend - b)).astype(BF16)
        k_s[...] = k
        b_s[...] = b
        dec_s[...] = jnp.exp(bend)
        tidx = lax.broadcasted_iota(jnp.int32, (HG_BLOCK, HG_DIM), 0)

        def blk(i, carry):
            r0 = pl.multiple_of(i * HG_BLOCK, HG_BLOCK)
            rows = pl.ds(r0, HG_BLOCK)
            st = state[...]
            st_ref[i] = st
            v = iv_ref[rows, :]
            qq = q_ref[rows, :]
            kk = k_s[rows, :]
            bb = b_s[rows, :]
            o = _dot_nt(qt_s[rows, :], st.astype(BF16))
            for s in range(HG_BLOCK):
                e = jnp.where(tidx >= s, jnp.exp(jnp.minimum(bb - bb[s:s + 1, :], 0.0)), 0.0)
                acol = jnp.sum(qq * kk[s:s + 1, :] * e, axis=1, keepdims=True)
                o = o + acol * v[s:s + 1, :]
            oraw_ref[rows, :] = o
            state[...] = st * dec_s[pl.ds(r0, 1), :] + _dot_tn(v.astype(BF16), kt_s[rows, :])
            return carry

        lax.fori_loop(0, nblk, blk, 0, unroll=2 * HG_UNROLL)
        oraw = oraw_ref[...]
        r = lax.rsqrt(jnp.mean(oraw * oraw, axis=-1, keepdims=True) + LN_EPS)
        gg = gg_ref[...]
        oa_ref[...] = (oraw * r * gn_ref[...] * gg * _sigmoid(gg)).astype(BF16)

    def slab(off):
        return pl.BlockSpec((ct, HG_DIM), lambda h, c: (c, off + h))

    out_slab = pl.BlockSpec((ct, HG_DIM), lambda h, c: (c, h))
    return pl.pallas_call(
        body,
        grid=(nh, nct),
        in_specs=[slab(0), slab(nh), slab(2 * nh), slab(3 * nh),
                  pl.BlockSpec((None, 2, HG_DIM), lambda h, c: (h, 0, 0)),
                  pl.BlockSpec((1, HG_DIM), lambda h, c: (0, 0))],
        out_specs=[out_slab, out_slab, pl.BlockSpec((None, nblk, HG_DIM, HG_DIM), lambda h, c: (h, c, 0, 0))],
        out_shape=[jax.ShapeDtypeStruct((t, nh * HG_DIM), F32),
                   jax.ShapeDtypeStruct((t, (nh + SG_GROUPS) * HG_DIM), BF16),
                   jax.ShapeDtypeStruct((nh, t // HG_BLOCK, HG_DIM, HG_DIM), F32)],
        scratch_shapes=[pltpu.VMEM((HG_DIM, HG_DIM), F32), pltpu.VMEM((ct, HG_DIM), BF16),
                        pltpu.VMEM((ct, HG_DIM), BF16), pltpu.VMEM((ct, HG_DIM), F32),
                        pltpu.VMEM((ct, HG_DIM), F32), pltpu.VMEM((ct, HG_DIM), F32)],
        compiler_params=_params(2),
        name="hgrn_fwd",
    )(proj, proj, proj, proj, logits, gn)


def _sg_tile(t):
    return min(t, 512)


def _sgu_chunk_fwd(u, v, ln_g, ln_b, wm, bs):
    ua, dua = _gelu_and_grad(u)
    va, dva = _gelu_and_grad(v)
    vn, xhat, rstd = _ln_fwd(va, ln_g, ln_b)
    s = _dot(wm, vn.astype(BF16)) + bs
    return ua, dua, dva, vn, xhat, rstd, s


def _tril_weight(w_ref):
    n = SG_CHUNK
    r = lax.broadcasted_iota(jnp.int32, (n, n), 0)
    c = lax.broadcasted_iota(jnp.int32, (n, n), 1)
    return jnp.where(c <= r, w_ref[...], 0.0)


def _sgu_fwd(proj, mix, ln_g, ln_b, w_s, b_col):
    t = proj.shape[0]
    ct = _sg_tile(t)
    ng = SG_GROUPS
    off_u = 4 * HG_HEADS
    off_v = off_u + ng

    def body(u_ref, v_ref, g_ref, b_ref, w_ref, bs_ref, mix_ref, o_ref):
        del mix_ref
        wm = _tril_weight(w_ref).astype(BF16)
        for n in range(ct // SG_CHUNK):
            rows = slice(n * SG_CHUNK, (n + 1) * SG_CHUNK)
            ua, _, _, _, _, _, s = _sgu_chunk_fwd(u_ref[rows, :], v_ref[rows, :], g_ref[...], b_ref[...], wm, bs_ref[...])
            o_ref[rows, :] = (ua * s).astype(BF16)

    vec = pl.BlockSpec((None, 1, SG_DIM), lambda g, c: (g, 0, 0))
    return pl.pallas_call(
        body,
        grid=(ng, t // ct),
        in_specs=[pl.BlockSpec((ct, SG_DIM), lambda g, c: (c, off_u + g)),
                  pl.BlockSpec((ct, SG_DIM), lambda g, c: (c, off_v + g)), vec, vec,
                  pl.BlockSpec((None, SG_CHUNK, SG_CHUNK), lambda g, c: (g, 0, 0)),
                  pl.BlockSpec((None, SG_CHUNK, 1), lambda g, c: (g, 0, 0)), ANY],
        out_specs=pl.BlockSpec((ct, SG_DIM), lambda g, c: (c, HG_HEADS + g)),
        out_shape=jax.ShapeDtypeStruct(mix.shape, mix.dtype),
        input_output_aliases={6: 0},
        compiler_params=_params(2),
        name="sgu_fwd",
    )(proj, proj, ln_g, ln_b, w_s, b_col, mix)


def _mem_kv(mem, g, b, wk, wv):
    m_len, d = mem.shape

    def body(m_ref, g_ref, b_ref, wk_ref, wv_ref, mb_ref, xh_ref, rs_ref, k_ref, v_ref):
        m, xhat, rstd = _ln_fwd(m_ref[...], g_ref[...], b_ref[...])
        mb = m.astype(BF16)
        mb_ref[...] = mb
        xh_ref[...] = xhat
        rs_ref[...] = rstd
        k_ref[...] = _dot(mb, wk_ref[...]).astype(BF16)
        v_ref[...] = _dot(mb, wv_ref[...]).astype(BF16)

    return pl.pallas_call(
        body,
        out_shape=[jax.ShapeDtypeStruct((m_len, d), BF16), jax.ShapeDtypeStruct((m_len, d), F32),
                   jax.ShapeDtypeStruct((m_len, 1), F32), jax.ShapeDtypeStruct((m_len, d), BF16),
                   jax.ShapeDtypeStruct((m_len, d), BF16)],
        compiler_params=pltpu.CompilerParams(vmem_limit_bytes=VMEM_LIMIT_V7X),
        name="mem_kv",
    )(mem, g, b, wk, wv)


def _softmax_rows(s):
    m = jnp.max(s, axis=-1, keepdims=True)
    p = jnp.exp(s - m)
    return p / jnp.sum(p, axis=-1, keepdims=True)


def _attn_fwd(hb, wq, kb, vb):
    t, d = hb.shape
    tm = _row_tile(t)
    dh = d // X_HEADS
    scale = dh ** -0.5

    def body(h_ref, wq_ref, k_ref, v_ref, q_ref, o_ref):
        q = _dot(h_ref[...], wq_ref[...]).astype(BF16)
        q_ref[...] = q
        for hd in range(X_HEADS):
            sl = slice(hd * dh, (hd + 1) * dh)
            p = _softmax_rows(_dot_nt(q[:, sl], k_ref[:, sl]) * scale)
            o_ref[:, sl] = _dot(p.astype(BF16), v_ref[:, sl]).astype(BF16)

    row = pl.BlockSpec((tm, d), lambda i: (i, 0))
    full = lambda a: pl.BlockSpec(a.shape, lambda i: (0, 0))
    return pl.pallas_call(
        body,
        grid=(t // tm,),
        in_specs=[row, full(wq), full(kb), full(vb)],
        out_specs=[row, row],
        out_shape=[jax.ShapeDtypeStruct((t, d), BF16), jax.ShapeDtypeStruct((t, d), BF16)],
        compiler_params=_params(1),
        name="attn_fwd",
    )(hb, wq, kb, vb)


def _loss_ln_bwd(h, xhat, rstd, g, target):
    t, d = h.shape
    tm = _row_tile(t)
    nt = t // tm

    def body(h_ref, xh_ref, rs_ref, g_ref, t_ref, loss_ref, dy_ref, dyb_ref, dg_ref, db_ref, lacc):
        i = pl.program_id(0)

        @pl.when(i == 0)
        def _():
            lacc[...] = jnp.zeros_like(lacc)
            dg_ref[...] = jnp.zeros_like(dg_ref)
            db_ref[...] = jnp.zeros_like(db_ref)

        err = h_ref[...] - t_ref[...]
        lacc[...] += jnp.sum(err * err, axis=0, keepdims=True)
        dy, dg, db = _ln_bwd(err * (1.0 / d), xh_ref[...], rs_ref[...], g_ref[...])
        dy_ref[...] = dy
        dyb_ref[...] = dy.astype(BF16)
        dg_ref[...] += dg
        db_ref[...] += db

        @pl.when(i == nt - 1)
        def _():
            loss_ref[...] = jnp.zeros_like(loss_ref) + jnp.sum(lacc[...], axis=1, keepdims=True) * (0.5 / d)

    row = pl.BlockSpec((tm, d), lambda i: (i, 0))
    vec = pl.BlockSpec((1, d), lambda i: (0, 0))
    return pl.pallas_call(
        body,
        grid=(nt,),
        in_specs=[row, row, pl.BlockSpec((tm, 1), lambda i: (i, 0)), vec, row],
        out_specs=[pl.BlockSpec((1, LANES), lambda i: (0, 0)), row, row, vec, vec],
        out_shape=[jax.ShapeDtypeStruct((1, LANES), F32), jax.ShapeDtypeStruct((t, d), F32),
                   jax.ShapeDtypeStruct((t, d), BF16), jax.ShapeDtypeStruct((1, d), F32),
                   jax.ShapeDtypeStruct((1, d), F32)],
        scratch_shapes=[pltpu.VMEM((1, d), F32)],
        compiler_params=_params(1),
        name="loss_ln_bwd",
    )(h, xhat, rstd, g, target)


def _ffn_bwd_act(dyb, wd, a, b, coef, name, deps=()):
    t, d = dyb.shape
    f = wd.shape[0]
    tm = _row_tile(t)
    tn = _col_tile(f)

    def body(dy_ref, wd_ref, a_ref, b_ref, da_ref, db_ref):
        dy = dy_ref[...]
        for c in range(f // tn):
            cols = slice(c * tn, (c + 1) * tn)
            ds = _dot_nt(dy, wd_ref[cols, :]) * coef
            silu, dsilu = _silu_and_grad(a_ref[:, cols].astype(F32))
            da_ref[:, cols] = (ds * b_ref[:, cols].astype(F32) * dsilu).astype(BF16)
            db_ref[:, cols] = (ds * silu).astype(BF16)

    act = pl.BlockSpec((tm, f), lambda i: (i, 0))
    return pl.pallas_call(
        _drop_deps(body, 4, len(deps)),
        grid=(t // tm,),
        in_specs=[pl.BlockSpec((tm, d), lambda i: (i, 0)), _resident(wd), act, act] + [ANY] * len(deps),
        out_specs=[act, act],
        out_shape=[jax.ShapeDtypeStruct((t, f), BF16), jax.ShapeDtypeStruct((t, f), BF16)],
        compiler_params=_params(1),
        name=name,
    )(dyb, wd, a, b, *deps)


def _mm_tn(a, b, name, scale=1.0, deps=()):
    t, m = a.shape
    n = b.shape[1]
    tt = _row_tile(t)
    nt = t // tt
    tm_o = m // 2 if (m > n and m * n > 2 ** 21) else m
    tn_o = n // 2 if (n > m and m * n > 2 ** 21) else n

    def body(a_ref, b_ref, o_ref, acc):
        k = pl.program_id(2)

        @pl.when(k == 0)
        def _():
            acc[...] = jnp.zeros_like(acc)

        acc[...] += _dot_tn(a_ref[...], b_ref[...])

        @pl.when(k == nt - 1)
        def _():
            o_ref[...] = (acc[...] * scale).astype(BF16)

    return pl.pallas_call(
        _drop_deps(body, 2, len(deps)),
        grid=(m // tm_o, n // tn_o, nt),
        in_specs=[pl.BlockSpec((tt, tm_o), lambda i, j, k: (k, i)), pl.BlockSpec((tt, tn_o), lambda i, j, k: (k, j))]
        + [ANY] * len(deps),
        out_specs=pl.BlockSpec((tm_o, tn_o), lambda i, j, k: (i, j)),
        out_shape=jax.ShapeDtypeStruct((m, n), BF16),
        scratch_shapes=[pltpu.VMEM((tm_o, tn_o), F32)],
        compiler_params=_params(3),
        name=name,
    )(a, b, *deps)


def _mm_nt(lhs, w, name):
    t, d = lhs.shape
    kd = w.shape[0]
    tm = _row_tile(t)

    def body(l_ref, w_ref, o_ref):
        o_ref[...] = _dot_nt(l_ref[...], w_ref[...])

    return pl.pallas_call(
        body,
        grid=(t // tm,),
        in_specs=[pl.BlockSpec((tm, d), lambda i: (i, 0)), _resident(w)],
        out_specs=pl.BlockSpec((tm, kd), lambda i: (i, 0)),
        out_shape=jax.ShapeDtypeStruct((t, kd), F32),
        compiler_params=_params(1),
        name=name,
    )(lhs, w)


def _dx_ln(dy, pairs, ln, name, deps=()):
    t, d = dy.shape
    npair = len(pairs)
    tm = min(t, 512 // npair)
    nt = t // tm
    n_in = 1 + 2 * npair + (3 if ln is not None else 0)

    def body(*refs):
        dy_ref = refs[0]
        pr = refs[1:1 + 2 * npair]
        pos = 1 + 2 * npair
        dh = ALPHA * dy_ref[...]
        for p in range(npair):
            dh = dh + _dot_nt(pr[2 * p][...], pr[2 * p + 1][...])
        if ln is not None:
            xh_ref, rs_ref, g_ref = refs[pos:pos + 3]
            dyo_ref, dyb_ref, dg_ref, db_ref = refs[pos + 3:pos + 7]

            @pl.when(pl.program_id(0) == 0)
            def _():
                dg_ref[...] = jnp.zeros_like(dg_ref)
                db_ref[...] = jnp.zeros_like(db_ref)

            dyp, dg, db = _ln_bwd(dh, xh_ref[...], rs_ref[...], g_ref[...])
            dyo_ref[...] = dyp
            dyb_ref[...] = dyp.astype(BF16)
            dg_ref[...] += dg
            db_ref[...] += db
        else:
            refs[pos][...] = dh

    row = pl.BlockSpec((tm, d), lambda i: (i, 0))
    vec = pl.BlockSpec((1, d), lambda i: (0, 0))
    in_specs = [row]
    args = [dy]
    for lhs, w in pairs:
        in_specs += [pl.BlockSpec((tm, lhs.shape[1]), lambda i: (i, 0)), _resident(w)]
        args += [lhs, w]
    if ln is not None:
        in_specs += [row, pl.BlockSpec((tm, 1), lambda i: (i, 0)), vec]
        args += list(ln)
        out_specs = [row, row, vec, vec]
        out_shape = [jax.ShapeDtypeStruct((t, d), F32), jax.ShapeDtypeStruct((t, d), BF16),
                     jax.ShapeDtypeStruct((1, d), F32), jax.ShapeDtypeStruct((1, d), F32)]
    else:
        out_specs = row
        out_shape = jax.ShapeDtypeStruct((t, d), F32)
    return pl.pallas_call(
        _drop_deps(body, n_in, len(deps)),
        grid=(nt,),
        in_specs=in_specs + [ANY] * len(deps),
        out_specs=out_specs,
        out_shape=out_shape,
        compiler_params=_params(1),
        name=name,
    )(*args, *deps)


def _hgrn_bwd(proj, oraw, dmix, states, logits, gn):
    t = proj.shape[0]
    ct = _hg_tile(t)
    nct = t // ct
    nblk = ct // HG_BLOCK
    nh = HG_HEADS

    def body(q_ref, fz_ref, iv_ref, gg_ref, or_ref, do_ref, st_ref, lg_ref, gn_ref,
             dq_ref, dfz_ref, div_ref, dgg_ref, dlg_ref, dgn_ref,
             dstate, qt_s, kt_s, k_s, b_s, eb_s, ekb_s, dec_s, dor_s, dbl_s, gr_s, dk_s, dlb_acc):
        c = pl.program_id(1)

        @pl.when(c == 0)
        def _():
            dstate[...] = jnp.zeros_like(dstate)
            dlb_acc[...] = jnp.zeros_like(dlb_acc)
            dgn_ref[...] = jnp.zeros_like(dgn_ref)

        lb = _lower_bound(lg_ref[...])
        q = q_ref[...]
        sig, nsig, f, k = _forget_terms(fz_ref[...], lb)
        logf = jnp.log(f)
        lower, upper, whole = _block_masks(ct)
        b = _mask_dot(lower, logf)
        bend = _mask_dot(whole, logf)
        eb = jnp.exp(b)
        ekb = jnp.exp(bend - b)
        qt_s[...] = (q * eb).astype(BF16)
        kt_s[...] = (k * ekb).astype(BF16)
        k_s[...] = k
        b_s[...] = b
        eb_s[...] = eb
        ekb_s[...] = ekb
        dec_s[...] = jnp.exp(bend)
        oraw = or_ref[...]
        r = lax.rsqrt(jnp.mean(oraw * oraw, axis=-1, keepdims=True) + LN_EPS)
        on = oraw * r
        gg = gg_ref[...]
        silu, dsilu = _silu_and_grad(gg)
        doa = do_ref[...]
        gnv = gn_ref[...]
        dgg_ref[...] = (doa * on * gnv * dsilu).astype(BF16)
        dyn = doa * silu
        dgn_ref[...] += jnp.sum(dyn * on, axis=0, keepdims=True)
        don = dyn * gnv
        dor_s[...] = r * (don - on * jnp.mean(don * on, axis=-1, keepdims=True))
        tidx = lax.broadcasted_iota(jnp.int32, (HG_BLOCK, HG_DIM), 0)

        def blk(ii, carry):
            i = nblk - 1 - ii
            r0 = pl.multiple_of(i * HG_BLOCK, HG_BLOCK)
            rows = pl.ds(r0, HG_BLOCK)
            st = st_ref[i]
            dst = dstate[...]
            dstb = dst.astype(BF16)
            do = dor_s[rows, :]
            dob = do.astype(BF16)
            v = iv_ref[rows, :]
            vb = v.astype(BF16)
            qq = q_ref[rows, :]
            kk = k_s[rows, :]
            bb = b_s[rows, :]
            qt = qt_s[rows, :]
            kt = kt_s[rows, :]
            dec = dec_s[pl.ds(r0, 1), :]
            dkt = _dot(vb, dstb)
            dq = _dot(dob, st.astype(BF16)) * eb_s[rows, :]
            dk = dkt * ekb_s[rows, :]
            dv = _dot_nt(kt, dstb)
            gend = jnp.sum(kk * dk, axis=0, keepdims=True) + dec * jnp.sum(dst * st, axis=0, keepdims=True)
            for s in range(HG_BLOCK):
                ks = kk[s:s + 1, :]
                e = jnp.where(tidx >= s, jnp.exp(jnp.minimum(bb - bb[s:s + 1, :], 0.0)), 0.0)
                qe = qq * e
                acol = jnp.sum(qe * ks, axis=1, keepdims=True)
                dacol = jnp.sum(do * v[s:s + 1, :], axis=1, keepdims=True)
                dq = dq + dacol * (ks * e)
                dk_row = jnp.sum(dacol * qe, axis=0, keepdims=True)
                dv_row = jnp.sum(acol * do, axis=0, keepdims=True)
                dk = dk + jnp.where(tidx == s, dk_row, 0.0)
                dv = dv + jnp.where(tidx == s, dv_row, 0.0)
            dq_ref[rows, :] = dq.astype(BF16)
            div_ref[rows, :] = dv.astype(BF16)
            dk_s[rows, :] = dk
            dbl_s[rows, :] = qq * dq - kk * dk
            gr_s[rows, :] = jnp.zeros((HG_BLOCK, HG_DIM), F32) + gend
            dstate[...] = dst * dec + _dot_tn(dob, qt)
            return carry

        lax.fori_loop(0, nblk, blk, 0, unroll=HG_UNROLL)
        dlogf = _mask_dot(upper, dbl_s[...]) + gr_s[...]
        dk = dk_s[...]
        dfz_ref[...] = ((dlogf / f - dk) * ((1.0 - lb) * sig * nsig)).astype(BF16)
        dlb_acc[...] += jnp.sum((dlogf / f - dk) * nsig, axis=0, keepdims=True)

        @pl.when(c == nct - 1)
        def _():
            dl0 = dlb_acc[...] * lb * (1.0 - lb)
            layer = lax.broadcasted_iota(jnp.int32, (2, HG_DIM), 0)
            dlg_ref[...] = jnp.where(layer == 0, dl0, -dl0)

    def slab(off):
        return pl.BlockSpec((ct, HG_DIM), lambda h, c: (nct - 1 - c, off + h))

    out_slab = pl.BlockSpec((ct, HG_DIM), lambda h, c: (nct - 1 - c, h))
    tile_f32 = pltpu.VMEM((ct, HG_DIM), F32)
    tile_b16 = pltpu.VMEM((ct, HG_DIM), BF16)
    slab_shape = jax.ShapeDtypeStruct((t, nh * HG_DIM), BF16)
    return pl.pallas_call(
        body,
        grid=(nh, nct),
        in_specs=[slab(0), slab(nh), slab(2 * nh), slab(3 * nh), slab(0), slab(0),
                  pl.BlockSpec((None, nblk, HG_DIM, HG_DIM), lambda h, c: (h, nct - 1 - c, 0, 0)),
                  pl.BlockSpec((None, 2, HG_DIM), lambda h, c: (h, 0, 0)),
                  pl.BlockSpec((1, HG_DIM), lambda h, c: (0, 0))],
        out_specs=[out_slab, out_slab, out_slab, out_slab,
                   pl.BlockSpec((None, 2, HG_DIM), lambda h, c: (h, 0, 0)),
                   pl.BlockSpec((None, 1, HG_DIM), lambda h, c: (h, 0, 0))],
        out_shape=[slab_shape, slab_shape, slab_shape, slab_shape,
                   jax.ShapeDtypeStruct((nh, 2, HG_DIM), F32), jax.ShapeDtypeStruct((nh, 1, HG_DIM), F32)],
        scratch_shapes=[pltpu.VMEM((HG_DIM, HG_DIM), F32), tile_b16, tile_b16, tile_f32, tile_f32, tile_f32, tile_f32,
                        tile_f32, tile_f32, tile_f32, tile_f32, tile_f32, pltpu.VMEM((1, HG_DIM), F32)],
        compiler_params=_params(2),
        name="hgrn_bwd",
    )(proj, proj, proj, proj, oraw, dmix, states, logits, gn)


def _sgu_bwd(proj, dmix, ln_g, ln_b, w_s, w_t, b_col):
    t = proj.shape[0]
    ct = _sg_tile(t)
    nct = t // ct
    ng = SG_GROUPS
    off_u = 4 * HG_HEADS
    off_v = off_u + ng
    n = SG_CHUNK

    def body(u_ref, v_ref, do_ref, g_ref, b_ref, w_ref, wt_ref, bs_ref, du_ref, dv_ref, dg_ref, db_ref, dw_ref, dbs_ref):
        c = pl.program_id(1)

        @pl.when(c == 0)
        def _():
            dg_ref[...] = jnp.zeros_like(dg_ref)
            db_ref[...] = jnp.zeros_like(db_ref)
            dw_ref[...] = jnp.zeros_like(dw_ref)
            dbs_ref[...] = jnp.zeros_like(dbs_ref)

        r = lax.broadcasted_iota(jnp.int32, (n, n), 0)
        cc = lax.broadcasted_iota(jnp.int32, (n, n), 1)
        wm = jnp.where(cc <= r, w_ref[...], 0.0).astype(BF16)
        wmt = jnp.where(r <= cc, wt_ref[...], 0.0).astype(BF16)
        for ci in range(ct // n):
            rows = slice(ci * n, (ci + 1) * n)
            ua, dua, dva, vn, xhat, rstd, s = _sgu_chunk_fwd(u_ref[rows, :], v_ref[rows, :], g_ref[...], b_ref[...],
                                                             wm, bs_ref[...])
            do = do_ref[rows, :]
            du_ref[rows, :] = (do * s * dua).astype(BF16)
            ds = do * ua
            dsb = ds.astype(BF16)
            dbs_ref[...] += jnp.sum(ds, axis=1, keepdims=True)
            dw_ref[...] += _dot_nt(dsb, vn.astype(BF16))
            dvn = _dot(wmt, dsb)
            dva_in, dg, db = _ln_bwd(dvn, xhat, rstd, g_ref[...])
            dg_ref[...] += dg
            db_ref[...] += db
            dv_ref[rows, :] = (dva_in * dva).astype(BF16)

        @pl.when(c == nct - 1)
        def _():
            dw_ref[...] = jnp.where(cc <= r, dw_ref[...], 0.0)

    vec = pl.BlockSpec((None, 1, SG_DIM), lambda g, c: (g, 0, 0))
    mat = pl.BlockSpec((None, n, n), lambda g, c: (g, 0, 0))
    col = pl.BlockSpec((None, n, 1), lambda g, c: (g, 0, 0))
    out_slab = pl.BlockSpec((ct, SG_DIM), lambda g, c: (c, g))
    return pl.pallas_call(
        body,
        grid=(ng, nct),
        in_specs=[pl.BlockSpec((ct, SG_DIM), lambda g, c: (c, off_u + g)),
                  pl.BlockSpec((ct, SG_DIM), lambda g, c: (c, off_v + g)),
                  pl.BlockSpec((ct, SG_DIM), lambda g, c: (c, ng + g)), vec, vec, mat, mat, col],
        out_specs=[out_slab, out_slab, vec, vec, mat, col],
        out_shape=[jax.ShapeDtypeStruct((t, ng * SG_DIM), BF16), jax.ShapeDtypeStruct((t, ng * SG_DIM), BF16),
                   jax.ShapeDtypeStruct((ng, 1, SG_DIM), F32), jax.ShapeDtypeStruct((ng, 1, SG_DIM), F32),
                   jax.ShapeDtypeStruct((ng, n, n), F32), jax.ShapeDtypeStruct((ng, n, 1), F32)],
        compiler_params=_params(2),
        name="sgu_bwd",
    )(proj, proj, dmix, ln_g, ln_b, w_s, w_t, b_col)


def _attn_bwd(dyb, wo, qb, kb, vb):
    t, d = dyb.shape
    m_len = kb.shape[0]
    tm = _row_tile(t)
    dh = d // X_HEADS
    scale = dh ** -0.5

    def body(dy_ref, wo_ref, q_ref, k_ref, v_ref, dq_ref, dk_ref, dv_ref):
        i = pl.program_id(0)

        @pl.when(i == 0)
        def _():
            dk_ref[...] = jnp.zeros_like(dk_ref)
            dv_ref[...] = jnp.zeros_like(dv_ref)

        do = _dot_nt(dy_ref[...], wo_ref[...]).astype(BF16)
        for hd in range(X_HEADS):
            sl = slice(hd * dh, (hd + 1) * dh)
            qh = q_ref[:, sl]
            p = _softmax_rows(_dot_nt(qh, k_ref[:, sl]) * scale)
            doh = do[:, sl]
            dp = _dot_nt(doh, v_ref[:, sl])
            ds = (p * (dp - jnp.sum(dp * p, axis=-1, keepdims=True)) * scale).astype(BF16)
            dq_ref[:, sl] = _dot(ds, k_ref[:, sl]).astype(BF16)
            dk_ref[:, sl] += _dot_tn(ds, qh)
            dv_ref[:, sl] += _dot_tn(p.astype(BF16), doh)

    row = pl.BlockSpec((tm, d), lambda i: (i, 0))
    full = lambda a: pl.BlockSpec(a.shape, lambda i: (0, 0))
    kv = pl.BlockSpec((m_len, d), lambda i: (0, 0))
    return pl.pallas_call(
        body,
        grid=(t // tm,),
        in_specs=[row, full(wo), row, full(kb), full(vb)],
        out_specs=[row, kv, kv],
        out_shape=[jax.ShapeDtypeStruct((t, d), BF16), jax.ShapeDtypeStruct((m_len, d), F32),
                   jax.ShapeDtypeStruct((m_len, d), F32)],
        compiler_params=_params(1),
        name="attn_bwd",
    )(dyb, wo, qb, kb, vb)


def _mem_bwd(dk, dv, mb, xhat, rstd, g, wk, wv):
    m_len, d = dk.shape

    def body(dk_ref, dv_ref, mb_ref, xh_ref, rs_ref, g_ref, wk_ref, wv_ref, gwk_ref, gwv_ref, dg_ref, db_ref):
        dkb = dk_ref[...].astype(BF16)
        dvb = dv_ref[...].astype(BF16)
        mb_v = mb_ref[...]
        gwk_ref[...] = _dot_tn(mb_v, dkb).astype(BF16)
        gwv_ref[...] = _dot_tn(mb_v, dvb).astype(BF16)
        dm = _dot_nt(dkb, wk_ref[...]) + _dot_nt(dvb, wv_ref[...])
        _, dg, db = _ln_bwd(dm, xh_ref[...], rs_ref[...], g_ref[...])
        dg_ref[...] = dg
        db_ref[...] = db

    return pl.pallas_call(
        body,
        out_shape=[jax.ShapeDtypeStruct((d, d), BF16), jax.ShapeDtypeStruct((d, d), BF16),
                   jax.ShapeDtypeStruct((1, d), F32), jax.ShapeDtypeStruct((1, d), F32)],
        compiler_params=pltpu.CompilerParams(vmem_limit_bytes=VMEM_LIMIT_V7X),
        name="mem_bwd",
    )(dk, dv, mb, xhat, rstd, g, wk, wv)


def _adamw(w, g, m, v):
    m = ADAM_B1 * m + (1.0 - ADAM_B1) * g
    v = ADAM_B2 * v + (1.0 - ADAM_B2) * (g * g)
    m_hat = m / (1.0 - ADAM_B1 ** ADAM_STEP)
    v_hat = v / (1.0 - ADAM_B2 ** ADAM_STEP)
    delta = -ADAM_LR * (m_hat / (jnp.sqrt(v_hat) + ADAM_EPS) + ADAM_WD * w)
    return delta, m, v


def _slot_sum(ref):
    g = ref[0].astype(F32)
    for s in range(1, N_DEV):
        g = g + ref[s].astype(F32)
    return g


def _adam_sharded(lands, w, m, v, axis, name):
    rows, cols = w.shape
    nl = len(lands)
    transposed = axis == 1 and nl == 2
    if transposed:
        rows, cols = cols, rows
        tr = 256
        grid = (rows // tr,)
        wblk = pl.BlockSpec((cols, tr), lambda i: (0, i))
        lblk = [pl.BlockSpec((N_DEV, tr, a.shape[2]), lambda i: (0, i, 0)) for a in lands]
    elif axis == 1:
        tr = 256 if rows % 256 == 0 else rows
        grid = (rows // tr,)
        wblk = pl.BlockSpec((tr, cols), lambda i: (i, 0))
        lblk = [pl.BlockSpec((N_DEV, tr, a.shape[2]), lambda i: (0, i, 0)) for a in lands]
    else:
        tc = _col_tile(cols)
        grid = (cols // tc,)
        wblk = pl.BlockSpec((rows, tc), lambda i: (0, i))
        lblk = [pl.BlockSpec((N_DEV, a.shape[1], tc), lambda i: (0, 0, i)) for a in lands]

    def body(*refs):
        w_ref, m_ref, v_ref = refs[nl:nl + 3]
        g_ref, d_ref, nm_ref, nv_ref = refs[nl + 3:]
        g = _slot_sum(refs[0])
        if nl == 2:
            tail = _slot_sum(refs[1])
            if transposed:
                g = jnp.concatenate([g.T, tail.T[:cols - g.shape[1], :]], axis=0)
            elif axis == 1:
                g = jnp.concatenate([g, tail[:, :cols - g.shape[1]]], axis=1)
            else:
                g = jnp.concatenate([g, tail[:rows - g.shape[0], :]], axis=0)
        delta, nm, nv = _adamw(w_ref[...], g, m_ref[...], v_ref[...])
        g_ref[...] = g
        d_ref[...] = delta
        nm_ref[...] = nm
        nv_ref[...] = nv

    shp = jax.ShapeDtypeStruct(w.shape, F32)
    return pl.pallas_call(
        body,
        grid=grid,
        in_specs=lblk + [wblk, wblk, wblk],
        out_specs=[wblk, wblk, wblk, wblk],
        out_shape=[shp, shp, shp, shp],
        compiler_params=_params(1),
        name=name,
    )(*lands, w, m, v)


def _mesh_pos():
    return lax.axis_index("x"), lax.axis_index("y"), lax.axis_index("c")


def _peer(k):
    x, y, c = _mesh_pos()
    pos = (x ^ (k >> 2), y ^ ((k >> 1) & 1), c ^ (k & 1))
    return pos, 4 * pos[0] + 2 * pos[1] + pos[2]


def _sem_index(row, k):
    return row * (N_DEV - 1) + k - 1


def _window(ref, axis, start, size):
    align = 16 if axis == 0 else LANES
    start = pl.multiple_of(start, align)
    return ref.at[pl.ds(start, size), :] if axis == 0 else ref.at[:, pl.ds(start, size)]


def _piece_refs(piece, srcs, lands, me, peer):
    kind, si, li, axis, base, stride, shape = piece
    if kind == "gather":
        return srcs[si], _window(lands[li], axis, base + stride * me, shape[axis])
    return _window(srcs[si], axis, base + stride * peer, shape[axis]), lands[li].at[me]


def _place_own(srcs, land_shapes, pieces, name):
    ns, nl, npc = len(srcs), len(land_shapes), len(pieces)

    def body(*refs):
        s_refs = refs[:ns]
        l_refs = refs[ns:ns + nl]
        bufs = refs[ns + nl:ns + nl + npc]
        sems = refs[ns + nl + npc]
        x, y, c = _mesh_pos()
        me = 4 * x + 2 * y + c
        loads = []
        for p, piece in enumerate(pieces):
            src, dst = _piece_refs(piece, s_refs, l_refs, me, me)
            cp = pltpu.make_async_copy(src, bufs[p], sems.at[0, p])
            cp.start()
            loads.append((cp, dst))
        stores = []
        for p, (cp, dst) in enumerate(loads):
            cp.wait()
            out = pltpu.make_async_copy(bufs[p], dst, sems.at[1, p])
            out.start()
            stores.append(out)
        for out in stores:
            out.wait()

    out = pl.pallas_call(
        body,
        in_specs=[ANY] * ns,
        out_specs=[ANY] * nl,
        out_shape=list(land_shapes),
        scratch_shapes=[pltpu.VMEM(pc[6], srcs[pc[1]].dtype) for pc in pieces] + [pltpu.SemaphoreType.DMA((2, npc))],
        compiler_params=pltpu.CompilerParams(vmem_limit_bytes=VMEM_LIMIT_V7X),
        name=name,
    )(*srcs)
    return list(out)


def _comm_start(srcs, lands, pieces, groups, name, after=()):
    ns, nl, na, ng = len(srcs), len(lands), len(after), len(groups)

    def body(*refs):
        s_refs = refs[:ns]
        l_refs = refs[ns:ns + nl]
        outs = refs[ns + nl + na:]
        sems = outs[:2 * ng]
        token = outs[-1]
        x, y, c = _mesh_pos()
        me = 4 * x + 2 * y + c
        for g, members in enumerate(groups):
            for row, p in enumerate(members):
                for k in range(1, N_DEV):
                    pos, peer = _peer(k)
                    src, dst = _piece_refs(pieces[p], s_refs, l_refs, me, peer)
                    pltpu.make_async_remote_copy(src_ref=src, dst_ref=dst, send_sem=sems[2 * g].at[_sem_index(row, k)],
                                                 recv_sem=sems[2 * g + 1].at[_sem_index(row, k)], device_id=pos,
                                                 device_id_type=MESH_ID).start()
        token[...] = jnp.zeros_like(token)

    sem_shapes = []
    for members in groups:
        sem_shapes += [pltpu.SemaphoreType.DMA((len(members) * (N_DEV - 1),))] * 2
    hbm_of = lambda a: pltpu.HBM(a.shape, a.dtype)
    out = pl.pallas_call(
        body,
        in_specs=[HBM] * (ns + nl) + [ANY] * na,
        out_specs=[SEM] * (2 * ng) + [HBM] * (ns + nl) + [pl.BlockSpec(memory_space=pltpu.VMEM)],
        out_shape=sem_shapes + [hbm_of(a) for a in srcs] + [hbm_of(a) for a in lands]
        + [jax.ShapeDtypeStruct((8, LANES), F32)],
        input_output_aliases={i: 2 * ng + i for i in range(ns + nl)},
        compiler_params=pltpu.CompilerParams(has_side_effects=DATAFLOW),
        name=name,
    )(*[pltpu.with_memory_space_constraint(a, pltpu.HBM) for a in list(srcs) + list(lands)], *after)
    sems = [(out[2 * g], out[2 * g + 1]) for g in range(ng)]
    return sems, list(out[2 * ng:2 * ng + ns]), list(out[2 * ng + ns:2 * ng + ns + nl]), out[-1]


def _comm_wait(srcs, lands, pieces, members, sems, after, name):
    ns, nl, na = len(srcs), len(lands), len(after)

    def body(*refs):
        s_refs = refs[:ns]
        l_refs = refs[ns:ns + nl]
        send_sems, recv_sems = refs[ns + nl:ns + nl + 2]
        x, y, c = _mesh_pos()
        me = 4 * x + 2 * y + c
        for row, p in enumerate(members):
            for k in range(1, N_DEV):
                pos, peer = _peer(k)
                src, dst = _piece_refs(pieces[p], s_refs, l_refs, me, peer)
                cp = pltpu.make_async_remote_copy(src_ref=src, dst_ref=dst, send_sem=send_sems.at[_sem_index(row, k)],
                                                  recv_sem=recv_sems.at[_sem_index(row, k)], device_id=pos,
                                                  device_id_type=MESH_ID)
                cp.wait_send()
                cp.wait_recv()

    hbm_of = lambda a: pltpu.HBM(a.shape, a.dtype)
    out = pl.pallas_call(
        body,
        in_specs=[HBM] * (ns + nl) + [SEM, SEM] + [ANY] * na,
        out_specs=[HBM] * (ns + nl),
        out_shape=[hbm_of(a) for a in srcs] + [hbm_of(a) for a in lands],
        input_output_aliases={i: i for i in range(ns + nl)},
        compiler_params=pltpu.CompilerParams(has_side_effects=DATAFLOW),
        name=name,
    )(*srcs, *lands, sems[0], sems[1], *after)
    return list(out[ns:])


_SMALL_NAMES = ("ln1_g", "ln1_b", "hg_lb_logits", "hg_norm_g", "sg_ln_g", "sg_ln_b", "sg_w_s", "sg_b_s",
                "ln2_g", "ln2_b", "mem_ln_g", "mem_ln_b", "ln3_g", "ln3_b", "ln4_g", "ln4_b")


_VEC_NAMES = ("ln1_g", "ln1_b", "ln2_g", "ln2_b", "mem_ln_g", "mem_ln_b", "ln3_g", "ln3_b", "ln4_g", "ln4_b")
_ROW_NAMES = ("hg_lb_logits", "hg_norm_g", "sg_ln_g", "sg_ln_b", "sg_b_s", "sg_w_s")
VEC_ROWS = 16


def _row_plan(shapes):
    plan, pos = {}, 0
    for k in _ROW_NAMES:
        shp = shapes[k]
        slabs, off = [], pos
        for idx in itertools.product(*[range(dim) for dim in shp[:-2]]):
            slabs.append((idx, off, shp[-2]))
            off += shp[-2]
        plan[k] = (pos, slabs)
        pos = -(-off // 8) * 8
    return plan, -(-pos // 16) * 16


def _pack_small_grads(gs, shapes):
    vec = jnp.concatenate([gs[k].reshape(1, -1) for k in _VEC_NAMES], axis=0)
    vec = jnp.pad(vec, ((0, VEC_ROWS - vec.shape[0]), (0, 0)))
    plan, total = _row_plan(shapes)
    parts, pos = [], 0
    for k in _ROW_NAMES:
        first, slabs = plan[k]
        rows = gs[k].reshape(-1, LANES)
        end = slabs[-1][1] + slabs[-1][2]
        nxt = -(-end // 8) * 8
        parts.append(jnp.pad(rows, ((0, nxt - first - rows.shape[0]), (0, 0))))
        pos = nxt
    parts.append(jnp.zeros((total - pos, LANES), F32))
    return vec, jnp.concatenate(parts, axis=0)


def _adam_small(land_vec, land_rows, w, m, v):
    names = _VEC_NAMES + _ROW_NAMES
    n = len(names)
    shapes = {k: w[k].shape for k in names}
    plan, _ = _row_plan(shapes)

    def body(*refs):
        lv_ref, lr_ref = refs[:2]
        w_refs, m_refs, v_refs = refs[2:2 + n], refs[2 + n:2 + 2 * n], refs[2 + 2 * n:2 + 3 * n]
        outs = refs[2 + 3 * n:2 + 7 * n]
        gv_s, gr_s = refs[2 + 7 * n:]
        gv_s[...] = _slot_sum(lv_ref)
        gr_s[...] = _slot_sum(lr_ref)
        for p, k in enumerate(names):
            if k in _VEC_NAMES:
                row = _VEC_NAMES.index(k)
                slabs = [((), None, None)]
            else:
                slabs = plan[k][1]
            for idx, off, rows in slabs:
                g = gv_s[row:row + 1, :] if off is None else gr_s[off:off + rows, :]
                sel = idx + (slice(None), slice(None))
                delta, nm, nv = _adamw(w_refs[p][sel], g, m_refs[p][sel], v_refs[p][sel])
                for o, val in zip(range(4), (g, delta, nm, nv)):
                    outs[o * n + p][sel] = val

    flat = lambda tree: [tree[k] for k in names]
    shp = [jax.ShapeDtypeStruct(shapes[k], F32) for k in names]
    out = pl.pallas_call(
        body,
        out_shape=shp * 4,
        scratch_shapes=[pltpu.VMEM(land_vec.shape[1:], F32), pltpu.VMEM(land_rows.shape[1:], F32)],
        name="adam_small",
    )(land_vec, land_rows, *flat(w), *flat(m), *flat(v))
    return [dict(zip(names, out[o * n:(o + 1) * n])) for o in range(4)]


_COL_FFN = ("ffn1_w_gate", "ffn1_w_up", "ffn2_w_gate", "ffn2_w_up")
_ROW_FFN = ("ffn1_w_down", "ffn2_w_down")
_ROW_SQ = ("w_out", "xa_w_q", "xa_w_k", "xa_w_v", "xa_w_o")
_BIG_NAMES = ("ffn1_w_gate", "ffn1_w_up", "ffn1_w_down", "w_in", "w_out", "xa_w_q", "xa_w_k", "xa_w_v", "xa_w_o",
              "ffn2_w_gate", "ffn2_w_up", "ffn2_w_down")


def _ffn_split(fs):
    main = (fs // MXU_WIDTH_V7X) * MXU_WIDTH_V7X
    tail = fs - main
    tail_pad = -(-tail // LANES) * LANES
    assert main > 0 and tail > 0
    return main, tail, tail_pad


def _layout(name, shard_shape):
    r, c = shard_shape
    if name in _COL_FFN:
        main, tail, pad = _ffn_split(c)
        return (r, N_DEV * (main + pad)), [(1, 0, main, (r, main), (0, main)),
                                           (1, N_DEV * main, pad, (r, pad), (main, c))]
    if name in _ROW_FFN:
        main, tail, pad = _ffn_split(r)
        return (N_DEV * (main + pad), c), [(0, 0, main, (main, c), (0, main)),
                                           (0, N_DEV * main, pad, (pad, c), (main, r))]
    if name == "w_in":
        return (r, N_DEV * c), [(1, 0, c, (r, c), (0, c))]
    return (N_DEV * r, c), [(0, 0, r, (r, c), (0, r))]


def _shard_pieces(name, shard):
    out = []
    for axis, _, _, shape, (lo, hi) in _layout(name, shard.shape)[1]:
        part = shard[lo:hi, :] if axis == 0 else shard[:, lo:hi]
        pad = [(0, shape[0] - part.shape[0]), (0, shape[1] - part.shape[1])]
        out.append(jnp.pad(part, pad).astype(BF16))
    return out


def _gather_plan(names, shards):
    srcs, land_shapes, pieces, index = [], [], [], {}
    for li, name in enumerate(names):
        shape2d, parts = _layout(name, shards[name].shape)
        land_shapes.append(jax.ShapeDtypeStruct(shape2d, BF16))
        index[name] = []
        for (axis, base, stride, shape, _), src in zip(parts, _shard_pieces(name, shards[name])):
            index[name].append(len(pieces))
            pieces.append(("gather", len(srcs), li, axis, base, stride, shape))
            srcs.append(src)
    return srcs, land_shapes, pieces, index


def _scatter_plan(names, grads, shard_shapes):
    srcs, land_shapes, pieces, index = [], [], [], {}
    for si, name in enumerate(names):
        _, parts = _layout(name, shard_shapes[name])
        srcs.append(grads[name])
        index[name] = []
        for axis, base, stride, shape, _ in parts:
            index[name].append(len(land_shapes))
            pieces.append(("scatter", si, len(land_shapes), axis, base, stride, shape))
            land_shapes.append(jax.ShapeDtypeStruct((N_DEV,) + shape, grads[name].dtype))
    return srcs, land_shapes, pieces, index


def _small_views(small):
    row = lambda a: a.reshape(1, -1)
    ln = {k: row(small[k]) for k in ("ln1_g", "ln1_b", "ln2_g", "ln2_b", "ln3_g", "ln3_b", "ln4_g", "ln4_b",
                                      "mem_ln_g", "mem_ln_b", "hg_norm_g")}
    sg_w = small["sg_w_s"].reshape(SG_GROUPS, SG_CHUNK, SG_CHUNK)
    sg = dict(logits=jnp.swapaxes(small["hg_lb_logits"], 0, 1),
              g=small["sg_ln_g"].reshape(SG_GROUPS, 1, SG_DIM), b=small["sg_ln_b"].reshape(SG_GROUPS, 1, SG_DIM),
              w=sg_w, wt=jnp.swapaxes(sg_w, 1, 2), bs=small["sg_b_s"].reshape(SG_GROUPS, SG_CHUNK, 1))
    return ln, sg


def _forward(x, mem, get_w, small, first_deps=()):
    ln, sg = _small_views(small)
    xb = x.astype(BF16)
    a1, b1, s1 = _ffn_up(xb, get_w("ffn1_w_gate", ()), get_w("ffn1_w_up", ()), "ffn1_up", deps=first_deps)
    h1, h1b, xh1, rs1 = _mm_res_ln(s1, get_w("ffn1_w_down", (s1,)), x, ln["ln1_g"], ln["ln1_b"], 0.5, "ffn1_down_ln")
    proj = _mm_nn(h1b, get_w("w_in", (h1b,)), "mix_in")
    oraw, mix, states = _hgrn_fwd(proj, sg["logits"], ln["hg_norm_g"])
    mix = _sgu_fwd(proj, mix, sg["g"], sg["b"], sg["w"], sg["bs"])
    h2, h2b, xh2, rs2 = _mm_res_ln(mix, get_w("w_out", (mix,)), h1, ln["ln2_g"], ln["ln2_b"], 1.0, "mix_out_ln")
    mb, mxh, mrs, kb, vb = _mem_kv(mem, ln["mem_ln_g"], ln["mem_ln_b"], get_w("xa_w_k", (h2b,)), get_w("xa_w_v", (h2b,)))
    qb, att = _attn_fwd(h2b, get_w("xa_w_q", (kb,)), kb, vb)
    h3, h3b, xh3, rs3 = _mm_res_ln(att, get_w("xa_w_o", (att,)), h2, ln["ln3_g"], ln["ln3_b"], 1.0, "attn_out_ln")
    a2, b2, s2 = _ffn_up(h3b, get_w("ffn2_w_gate", (h3b,)), get_w("ffn2_w_up", (h3b,)), "ffn2_up")
    h4, _, xh4, rs4 = _mm_res_ln(s2, get_w("ffn2_w_down", (s2,)), h3, ln["ln4_g"], ln["ln4_b"], 0.5, "ffn2_down_ln")
    return dict(xb=xb, a1=a1, b1=b1, s1=s1, h1b=h1b, xh1=xh1, rs1=rs1, proj=proj, oraw=oraw, mix=mix, states=states,
                h2b=h2b, xh2=xh2, rs2=rs2, mb=mb, mxh=mxh, mrs=mrs, kb=kb, vb=vb, qb=qb, att=att, h3b=h3b, xh3=xh3,
                rs3=rs3, a2=a2, b2=b2, s2=s2, h4=h4, xh4=xh4, rs4=rs4)


def _backward(sv, target, wt, small, send):
    ln, sg = _small_views(small)
    gs = {}
    loss, dy4, dy4b, gs["ln4_g"], gs["ln4_b"] = _loss_ln_bwd(sv["h4"], sv["xh4"], sv["rs4"], ln["ln4_g"], target)
    g_down2 = _mm_tn(sv["s2"], dy4b, "g_ffn2_down", scale=0.5)
    da2, db2 = _ffn_bwd_act(dy4b, wt["ffn2_w_down"], sv["a2"], sv["b2"], 0.5, "ffn2_bwd_act")
    g_gate2 = _mm_tn(sv["h3b"], da2, "g_ffn2_gate")
    g_up2 = _mm_tn(sv["h3b"], db2, "g_ffn2_up")
    tok = send(("ffn2_w_down", "ffn2_w_gate", "ffn2_w_up"), (g_down2, g_gate2, g_up2))
    dy3, dy3b, gs["ln3_g"], gs["ln3_b"] = _dx_ln(dy4, [(da2, wt["ffn2_w_gate"]), (db2, wt["ffn2_w_up"])],
                                                 (sv["xh3"], sv["rs3"], ln["ln3_g"]), "ffn2_dx_ln", deps=(tok,))

    g_o = _mm_tn(sv["att"], dy3b, "g_xa_o")
    dqb, dk, dv = _attn_bwd(dy3b, wt["xa_w_o"], sv["qb"], sv["kb"], sv["vb"])
    g_q = _mm_tn(sv["h2b"], dqb, "g_xa_q")
    g_k, g_v, gs["mem_ln_g"], gs["mem_ln_b"] = _mem_bwd(dk, dv, sv["mb"], sv["mxh"], sv["mrs"], ln["mem_ln_g"],
                                                        wt["xa_w_k"], wt["xa_w_v"])
    tok = send(("xa_w_o", "xa_w_q", "xa_w_k", "xa_w_v"), (g_o, g_q, g_k, g_v))
    dy2, dy2b, gs["ln2_g"], gs["ln2_b"] = _dx_ln(dy3, [(dqb, wt["xa_w_q"])], (sv["xh2"], sv["rs2"], ln["ln2_g"]),
                                                 "attn_dx_ln", deps=(tok,))

    g_out = _mm_tn(sv["mix"], dy2b, "g_w_out")
    dmix = _mm_nt(dy2b, wt["w_out"], "mix_out_bwd")
    dq, dfz, div, dgg, dlg, dgn = _hgrn_bwd(sv["proj"], sv["oraw"], dmix, sv["states"], sg["logits"], ln["hg_norm_g"])
    du, dvv, gs["sg_ln_g"], gs["sg_ln_b"], gs["sg_w_s"], gs["sg_b_s"] = _sgu_bwd(
        sv["proj"], dmix, sg["g"], sg["b"], sg["w"], sg["wt"], sg["bs"])
    gs["hg_lb_logits"] = jnp.swapaxes(dlg, 0, 1)
    gs["hg_norm_g"] = jnp.sum(dgn, axis=0)
    dproj = jnp.concatenate([dq, dfz, div, dgg, du, dvv], axis=1)
    g_in = _mm_tn(sv["h1b"], dproj, "g_w_in")
    tok = send(("w_out", "w_in"), (g_out, g_in))
    dy1, dy1b, gs["ln1_g"], gs["ln1_b"] = _dx_ln(dy2, [(dproj, wt["w_in"])], (sv["xh1"], sv["rs1"], ln["ln1_g"]),
                                                 "mix_dx_ln", deps=(tok,))

    g_down1 = _mm_tn(sv["s1"], dy1b, "g_ffn1_down", scale=0.5)
    tok = send(("ffn1_w_down",), (g_down1,))
    da1, db1 = _ffn_bwd_act(dy1b, wt["ffn1_w_down"], sv["a1"], sv["b1"], 0.5, "ffn1_bwd_act", deps=(tok,))
    g_gate1 = _mm_tn(sv["xb"], da1, "g_ffn1_gate")
    tok = send(("ffn1_w_gate",), (g_gate1,))
    g_up1 = _mm_tn(sv["xb"], db1, "g_ffn1_up", deps=(tok,))
    tok = send(("ffn1_w_up",), (g_up1,))
    grad_x = _dx_ln(dy1, [(da1, wt["ffn1_w_gate"]), (db1, wt["ffn1_w_up"])], None, "ffn1_dx", deps=(tok,))
    return loss, grad_x, gs


_WEIGHT_NAMES = ("ffn1_w_gate", "ffn1_w_up", "ffn1_w_down", "ln1_g", "ln1_b", "w_in", "hg_lb_logits", "hg_norm_g",
                 "sg_ln_g", "sg_ln_b", "sg_w_s", "sg_b_s", "w_out", "ln2_g", "ln2_b", "mem_ln_g", "mem_ln_b",
                 "xa_w_q", "xa_w_k", "xa_w_v", "xa_w_o", "ln3_g", "ln3_b", "ffn2_w_gate", "ffn2_w_up", "ffn2_w_down",
                 "ln4_g", "ln4_b")
_FIRST = ("ffn1_w_gate", "ffn1_w_up")
_SECOND = ("ffn1_w_down", "w_in", "w_out")
_THIRD = ("xa_w_k", "xa_w_v", "xa_w_q", "xa_w_o", "ffn2_w_gate", "ffn2_w_up", "ffn2_w_down")


def kernel(x, mem, ffn1_w_gate, ffn1_w_up, ffn1_w_down, ln1_g, ln1_b, w_in, hg_lb_logits, hg_norm_g, sg_ln_g, sg_ln_b, sg_w_s, sg_b_s, w_out, ln2_g, ln2_b, mem_ln_g, mem_ln_b, xa_w_q, xa_w_k, xa_w_v, xa_w_o, ln3_g, ln3_b, ffn2_w_gate, ffn2_w_up, ffn2_w_down, ln4_g, ln4_b, loss_target, m_ffn1_w_gate, m_ffn1_w_up, m_ffn1_w_down, m_ln1_g, m_ln1_b, m_w_in, m_hg_lb_logits, m_hg_norm_g, m_sg_ln_g, m_sg_ln_b, m_sg_w_s, m_sg_b_s, m_w_out, m_ln2_g, m_ln2_b, m_mem_ln_g, m_mem_ln_b, m_xa_w_q, m_xa_w_k, m_xa_w_v, m_xa_w_o, m_ln3_g, m_ln3_b, m_ffn2_w_gate, m_ffn2_w_up, m_ffn2_w_down, m_ln4_g, m_ln4_b, v_ffn1_w_gate, v_ffn1_w_up, v_ffn1_w_down, v_ln1_g, v_ln1_b, v_w_in, v_hg_lb_logits, v_hg_norm_g, v_sg_ln_g, v_sg_ln_b, v_sg_w_s, v_sg_b_s, v_w_out, v_ln2_g, v_ln2_b, v_mem_ln_g, v_mem_ln_b, v_xa_w_q, v_xa_w_k, v_xa_w_v, v_xa_w_o, v_ln3_g, v_ln3_b, v_ffn2_w_gate, v_ffn2_w_up, v_ffn2_w_down, v_ln4_g, v_ln4_b):
    args = dict(locals())
    w = {k: args[k] for k in _WEIGHT_NAMES}
    m = {k: args["m_" + k] for k in _WEIGHT_NAMES}
    v = {k: args["v_" + k] for k in _WEIGHT_NAMES}
    shards = {k: w[k][0] for k in _BIG_NAMES}
    shard_shapes = {k: shards[k].shape for k in _BIG_NAMES}
    small = {k: (w[k][0] if k != "hg_lb_logits" else w[k]) for k in _SMALL_NAMES}

    srcs1, shapes1, pieces1, idx1 = _gather_plan(_FIRST, shards)
    lands1 = _place_own(srcs1, shapes1, pieces1, "gather_first_own")
    sems1, srcs1, lands1, _ = _comm_start(srcs1, lands1, pieces1, [list(range(len(pieces1)))], "gather_first_start")
    lands1 = _comm_wait(srcs1, lands1, pieces1, list(range(len(pieces1))), sems1[0], (), "gather_first_wait")
    wt = dict(zip(_FIRST, lands1))

    rest = _SECOND + _THIRD
    srcs2, shapes2, pieces2, idx2 = _gather_plan(rest, shards)
    lands2 = _place_own(srcs2, shapes2, pieces2, "gather_rest_own")
    groups2 = [list(idx2[k]) for k in rest]
    sems2, srcs2, lands2, tok2 = _comm_start(srcs2, lands2, pieces2, groups2, "gather_rest_start", after=tuple(lands1))
    pending = {k: gi for gi, k in enumerate(rest)}

    def get_w(name, after):
        if name in pending:
            gi = pending.pop(name)
            si = [pieces2[p][1] for p in groups2[gi]]
            sub = [(pieces2[p][0], row, 0) + pieces2[p][3:] for row, p in enumerate(groups2[gi])]
            wt[name] = _comm_wait([srcs2[s] for s in si], [lands2[gi]], sub, list(range(len(sub))), sems2[gi],
                                  after, "gather_wait_" + name)[0]
        return wt[name]

    sv = _forward(x[0], mem[0], get_w, small, first_deps=(tok2,))

    sent = []

    def send(names, grads):
        srcs, shapes, pieces, idx = _scatter_plan(names, dict(zip(names, grads)), shard_shapes)
        lands = _place_own(srcs, shapes, pieces, "grads_own_%d" % len(sent))
        sems, srcs, lands, tok = _comm_start(srcs, lands, pieces, [list(range(len(pieces)))],
                                             "grads_start_%d" % len(sent))
        sent.append((names, srcs, lands, pieces, idx, sems[0]))
        return tok

    loss, grad_x, gs = _backward(sv, loss_target[0], wt, small, send)

    ssrc = list(_pack_small_grads(gs, {k: w[k].shape for k in _SMALL_NAMES}))
    sp = [("scatter", i, i, 0, 0, 0, a.shape) for i, a in enumerate(ssrc)]
    sshape = [jax.ShapeDtypeStruct((N_DEV,) + a.shape, F32) for a in ssrc]
    sl = _place_own(ssrc, sshape, sp, "small_own")
    ssem, ssrc, sl, _ = _comm_start(ssrc, sl, sp, [[0, 1]], "small_start")

    out_g, out_d, out_m, out_v = {}, {}, {}, {}
    after = (grad_x,)
    for n_sent, (names, srcs, lands, pieces, idx, sems) in enumerate(sent):
        lands = _comm_wait(srcs, lands, pieces, list(range(len(pieces))), sems, after, "grads_wait_%d" % n_sent)
        for k in names:
            axis = 1 if (k in _COL_FFN or k == "w_in") else 0
            if k in _COL_FFN:
                res = _adam_sharded([lands[i] for i in idx[k]], w[k][0].T, m[k][0].T, v[k][0].T, axis, "adam_" + k)
                res = [r.T for r in res]
            else:
                res = _adam_sharded([lands[i] for i in idx[k]], w[k][0], m[k][0], v[k][0], axis, "adam_" + k)
            out_g[k], out_d[k], out_m[k], out_v[k] = [r[None] for r in res]
        after = (out_v[names[-1]],)
    sl = _comm_wait(ssrc, sl, sp, [0, 1], ssem[0], after, "small_wait")
    for dst, res in zip((out_g, out_d, out_m, out_v), _adam_small(sl[0], sl[1], w, m, v)):
        dst.update(res)

    loss_all = lax.psum(loss[0, 0], ("x", "y", "c"))
    return (loss_all, grad_x[None], *[out_g[k] for k in _WEIGHT_NAMES], *[out_d[k] for k in _WEIGHT_NAMES],
            *[out_m[k] for k in _WEIGHT_NAMES], *[out_v[k] for k in _WEIGHT_NAMES])
```

```python
import itertools

import jax
import jax.numpy as jnp
from jax import lax
from jax.experimental import pallas as pl
from jax.experimental.pallas import tpu as pltpu

F32 = jnp.float32
BF16 = jnp.bfloat16

N_DEV = 8
ALPHA = 2.0 ** 0.25
LN_EPS = 1e-5
HG_HEADS = 4
HG_DIM = 128
SG_GROUPS = 4
SG_DIM = 128
SG_CHUNK = 128
X_HEADS = 4
HG_BLOCK = 16
HG_UNROLL = 4
ADAM_LR = 0.001
ADAM_B1 = 0.9
ADAM_B2 = 0.999
ADAM_EPS = 1e-08
ADAM_WD = 0.01
ADAM_STEP = 10
VMEM_LIMIT_V7X = 48 * 1024 * 1024
MXU_WIDTH_V7X = 256
LANES = 128
MESH_ID = pl.DeviceIdType.MESH
ANY = pl.BlockSpec(memory_space=pl.ANY)
HBM = pl.BlockSpec(memory_space=pltpu.HBM)
SEM = pl.BlockSpec(memory_space=pltpu.SEMAPHORE)
DATAFLOW = pltpu.SideEffectType.DATAFLOW_SIDE_EFFECTING


def _params(n_axes):
    return pltpu.CompilerParams(dimension_semantics=("arbitrary",) * n_axes, vmem_limit_bytes=VMEM_LIMIT_V7X)


def _dot(a, b):
    return jnp.dot(a, b, preferred_element_type=F32)


def _dot_nt(a, b):
    return lax.dot_general(a, b, (((1,), (1,)), ((), ())), preferred_element_type=F32)


def _dot_tn(a, b):
    return lax.dot_general(a, b, (((0,), (0,)), ((), ())), preferred_element_type=F32)


def _sigmoid(x):
    return 1.0 / (1.0 + jnp.exp(-x))


def _silu_and_grad(a):
    sig = _sigmoid(a)
    return a * sig, sig * (1.0 + a * (1.0 - sig))


_GELU_C = 0.7978845608028654


def _gelu_and_grad(x):
    inner = _GELU_C * (x + 0.044715 * x * x * x)
    t = jnp.tanh(inner)
    val = 0.5 * x * (1.0 + t)
    grad = 0.5 * (1.0 + t) + 0.5 * x * (1.0 - t * t) * _GELU_C * (1.0 + 3.0 * 0.044715 * x * x)
    return val, grad


def _ln_fwd(y, g, b):
    mu = jnp.mean(y, axis=-1, keepdims=True)
    yc = y - mu
    var = jnp.mean(yc * yc, axis=-1, keepdims=True)
    rstd = lax.rsqrt(var + LN_EPS)
    xhat = yc * rstd
    return xhat * g + b, xhat, rstd


def _ln_bwd(dh, xhat, rstd, g):
    dxh = dh * g
    m1 = jnp.mean(dxh, axis=-1, keepdims=True)
    m2 = jnp.mean(dxh * xhat, axis=-1, keepdims=True)
    dy = rstd * (dxh - m1 - xhat * m2)
    dg = jnp.sum(dh * xhat, axis=0, keepdims=True)
    db = jnp.sum(dh, axis=0, keepdims=True)
    return dy, dg, db


def _split3(x):
    hi = x.astype(BF16)
    r1 = x - hi.astype(F32)
    mid = r1.astype(BF16)
    lo = (r1 - mid.astype(F32)).astype(BF16)
    return hi, mid, lo


def _mask_dot(mask, x):
    hi, mid, lo = _split3(x)
    return _dot(mask, hi) + _dot(mask, mid) + _dot(mask, lo)


def _block_masks(n):
    r = lax.broadcasted_iota(jnp.int32, (n, n), 0)
    c = lax.broadcasted_iota(jnp.int32, (n, n), 1)
    assert HG_BLOCK & (HG_BLOCK - 1) == 0
    same = (r & -HG_BLOCK) == (c & -HG_BLOCK)
    one = jnp.ones((n, n), BF16)
    zero = jnp.zeros((n, n), BF16)
    lower = jnp.where(same & (c <= r), one, zero)
    upper = jnp.where(same & (c >= r), one, zero)
    whole = jnp.where(same, one, zero)
    return lower, upper, whole


def _row_tile(t):
    return min(t, 512)


def _col_tile(n):
    for cand in (512, 256, 128):
        if n % cand == 0:
            return cand
    return n


def _resident(w):
    return pl.BlockSpec(w.shape, lambda *_: (0, 0), pipeline_mode=pl.Buffered(1))


def _drop_deps(body, n_in, n_deps):
    if n_deps == 0:
        return body
    return lambda *refs: body(*refs[:n_in], *refs[n_in + n_deps:])


def _ffn_up(hb, wg, wu, name, deps=()):
    t, d = hb.shape
    f = wg.shape[1]
    tm = min(t, 256)
    tn = _col_tile(f)

    def body(h_ref, wg_ref, wu_ref, a_ref, b_ref, s_ref):
        h = h_ref[...]
        for c in range(f // tn):
            cols = slice(c * tn, (c + 1) * tn)
            a = _dot(h, wg_ref[:, cols])
            b = _dot(h, wu_ref[:, cols])
            a_ref[:, cols] = a.astype(BF16)
            b_ref[:, cols] = b.astype(BF16)
            s_ref[:, cols] = (a * _sigmoid(a) * b).astype(BF16)

    act = pl.BlockSpec((tm, f), lambda i: (i, 0))
    return pl.pallas_call(
        _drop_deps(body, 3, len(deps)),
        grid=(t // tm,),
        in_specs=[pl.BlockSpec((tm, d), lambda i: (i, 0)), _resident(wg), _resident(wu)] + [ANY] * len(deps),
        out_specs=[act, act, act],
        out_shape=[jax.ShapeDtypeStruct((t, f), BF16)] * 3,
        compiler_params=_params(1),
        name=name,
    )(hb, wg, wu, *deps)


def _mm_res_ln(lhs, w, res, g, b, coef, name):
    t, kd = lhs.shape
    d = w.shape[1]
    tm = _row_tile(t)

    def body(l_ref, w_ref, r_ref, g_ref, b_ref, h_ref, hb_ref, xh_ref, rs_ref):
        y = ALPHA * r_ref[...] + coef * _dot(l_ref[...], w_ref[...])
        h, xhat, rstd = _ln_fwd(y, g_ref[...], b_ref[...])
        h_ref[...] = h
        hb_ref[...] = h.astype(BF16)
        xh_ref[...] = xhat
        rs_ref[...] = rstd

    row = pl.BlockSpec((tm, d), lambda i: (i, 0))
    vec = pl.BlockSpec((1, d), lambda i: (0, 0))
    return pl.pallas_call(
        body,
        grid=(t // tm,),
        in_specs=[pl.BlockSpec((tm, kd), lambda i: (i, 0)), _resident(w), row, vec, vec],
        out_specs=[row, row, row, pl.BlockSpec((tm, 1), lambda i: (i, 0))],
        out_shape=[jax.ShapeDtypeStruct((t, d), F32), jax.ShapeDtypeStruct((t, d), BF16),
                   jax.ShapeDtypeStruct((t, d), F32), jax.ShapeDtypeStruct((t, 1), F32)],
        compiler_params=_params(1),
        name=name,
    )(lhs, w, res, g, b)


def _mm_nn(lhs, w, name):
    t, kd = lhs.shape
    n = w.shape[1]
    tm = _row_tile(t)
    tn = _col_tile(n)

    def body(l_ref, w_ref, o_ref):
        lhs_v = l_ref[...]
        for c in range(n // tn):
            cols = slice(c * tn, (c + 1) * tn)
            o_ref[:, cols] = _dot(lhs_v, w_ref[:, cols])

    return pl.pallas_call(
        body,
        grid=(t // tm,),
        in_specs=[pl.BlockSpec((tm, kd), lambda i: (i, 0)), _resident(w)],
        out_specs=pl.BlockSpec((tm, n), lambda i: (i, 0)),
        out_shape=jax.ShapeDtypeStruct((t, n), F32),
        compiler_params=_params(1),
        name=name,
    )(lhs, w)


def _lower_bound(lg):
    m = jnp.max(lg, axis=0, keepdims=True)
    e = jnp.exp(lg - m)
    return e[0:1, :] / jnp.sum(e, axis=0, keepdims=True)


def _forget_terms(fz, lb):
    e = jnp.exp(-jnp.abs(fz))
    r = 1.0 / (1.0 + e)
    pos = fz >= 0.0
    sig = jnp.where(pos, r, e * r)
    nsig = jnp.where(pos, e * r, r)
    f = lb + (1.0 - lb) * sig
    k = (1.0 - lb) * nsig
    return sig, nsig, f, k


def _hg_tile(t):
    return min(t, 256)


def _hgrn_fwd(proj, logits, gn):
    t = proj.shape[0]
    ct = _hg_tile(t)
    nct = t // ct
    nblk = ct // HG_BLOCK
    nh = HG_HEADS

    def body(q_ref, fz_ref, iv_ref, gg_ref, lg_ref, gn_ref, oraw_ref, oa_ref, st_ref,
             state, qt_s, kt_s, k_s, b_s, dec_s):
        c = pl.program_id(1)

        @pl.when(c == 0)
        def _():
            state[...] = jnp.zeros_like(state)

        lb = _lower_bound(lg_ref[...])
        q = q_ref[...]
        _, _, f, k = _forget_terms(fz_ref[...], lb)
        logf = jnp.log(f)
        lower, _, whole = _block_masks(ct)
        b = _mask_dot(lower, logf)
        bend = _mask_dot(whole, logf)
        qt_s[...] = (q * jnp.exp(b)).astype(BF16)
        kt_s[...] = (k * jnp.exp(bend - b)).astype(BF16)
        k_s[...] = k
        b_s[...] = b
        dec_s[...] = jnp.exp(bend)
        tidx = lax.broadcasted_iota(jnp.int32, (HG_BLOCK, HG_DIM), 0)

        def blk(i, carry):
            r0 = pl.multiple_of(i * HG_BLOCK, HG_BLOCK)
            rows = pl.ds(r0, HG_BLOCK)
            st = state[...]
            st_ref[i] = st
            v = iv_ref[rows, :]
            qq = q_ref[rows, :]
            kk = k_s[rows, :]
            bb = b_s[rows, :]
            o = _dot_nt(qt_s[rows, :], st.astype(BF16))
            for s in range(HG_BLOCK):
                e = jnp.where(tidx >= s, jnp.exp(jnp.minimum(bb - bb[s:s + 1, :], 0.0)), 0.0)
                acol = jnp.sum(qq * kk[s:s + 1, :] * e, axis=1, keepdims=True)
                o = o + acol * v[s:s + 1, :]
            oraw_ref[rows, :] = o
            state[...] = st * dec_s[pl.ds(r0, 1), :] + _dot_tn(v.astype(BF16), kt_s[rows, :])
            return carry

        lax.fori_loop(0, nblk, blk, 0, unroll=2 * HG_UNROLL)
        oraw = oraw_ref[...]
        r = lax.rsqrt(jnp.mean(oraw * oraw, axis=-1, keepdims=True) + LN_EPS)
        gg = gg_ref[...]
        oa_ref[...] = (oraw * r * gn_ref[...] * gg * _sigmoid(gg)).astype(BF16)

    def slab(off):
        return pl.BlockSpec((ct, HG_DIM), lambda h, c: (c, off + h))

    out_slab = pl.BlockSpec((ct, HG_DIM), lambda h, c: (c, h))
    return pl.pallas_call(
        body,
        grid=(nh, nct),
        in_specs=[slab(0), slab(nh), slab(2 * nh), slab(3 * nh),
                  pl.BlockSpec((None, 2, HG_DIM), lambda h, c: (h, 0, 0)),
                  pl.BlockSpec((1, HG_DIM), lambda h, c: (0, 0))],
        out_specs=[out_slab, out_slab, pl.BlockSpec((None, nblk, HG_DIM, HG_DIM), lambda h, c: (h, c, 0, 0))],
        out_shape=[jax.ShapeDtypeStruct((t, nh * HG_DIM), F32),
                   jax.ShapeDtypeStruct((t, (nh + SG_GROUPS) * HG_DIM), BF16),
                   jax.ShapeDtypeStruct((nh, t // HG_BLOCK, HG_DIM, HG_DIM), F32)],
        scratch_shapes=[pltpu.VMEM((HG_DIM, HG_DIM), F32), pltpu.VMEM((ct, HG_DIM), BF16),
                        pltpu.VMEM((ct, HG_DIM), BF16), pltpu.VMEM((ct, HG_DIM), F32),
                        pltpu.VMEM((ct, HG_DIM), F32), pltpu.VMEM((ct, HG_DIM), F32)],
        compiler_params=_params(2),
        name="hgrn_fwd",
    )(proj, proj, proj, proj, logits, gn)


def _sg_tile(t):
    return min(t, 512)


def _sgu_chunk_fwd(u, v, ln_g, ln_b, wm, bs):
    ua, dua = _gelu_and_grad(u)
    va, dva = _gelu_and_grad(v)
    vn, xhat, rstd = _ln_fwd(va, ln_g, ln_b)
    s = _dot(wm, vn.astype(BF16)) + bs
    return ua, dua, dva, vn, xhat, rstd, s


def _tril_weight(w_ref):
    n = SG_CHUNK
    r = lax.broadcasted_iota(jnp.int32, (n, n), 0)
    c = lax.broadcasted_iota(jnp.int32, (n, n), 1)
    return jnp.where(c <= r, w_ref[...], 0.0)


def _sgu_fwd(proj, mix, ln_g, ln_b, w_s, b_col):
    t = proj.shape[0]
    ct = _sg_tile(t)
    ng = SG_GROUPS
    off_u = 4 * HG_HEADS
    off_v = off_u + ng

    def body(u_ref, v_ref, g_ref, b_ref, w_ref, bs_ref, mix_ref, o_ref):
        del mix_ref
        wm = _tril_weight(w_ref).astype(BF16)
        for n in range(ct // SG_CHUNK):
            rows = slice(n * SG_CHUNK, (n + 1) * SG_CHUNK)
            ua, _, _, _, _, _, s = _sgu_chunk_fwd(u_ref[rows, :], v_ref[rows, :], g_ref[...], b_ref[...], wm, bs_ref[...])
            o_ref[rows, :] = (ua * s).astype(BF16)

    vec = pl.BlockSpec((None, 1, SG_DIM), lambda g, c: (g, 0, 0))
    return pl.pallas_call(
        body,
        grid=(ng, t // ct),
        in_specs=[pl.BlockSpec((ct, SG_DIM), lambda g, c: (c, off_u + g)),
                  pl.BlockSpec((ct, SG_DIM), lambda g, c: (c, off_v + g)), vec, vec,
                  pl.BlockSpec((None, SG_CHUNK, SG_CHUNK), lambda g, c: (g, 0, 0)),
                  pl.BlockSpec((None, SG_CHUNK, 1), lambda g, c: (g, 0, 0)), ANY],
        out_specs=pl.BlockSpec((ct, SG_DIM), lambda g, c: (c, HG_HEADS + g)),
        out_shape=jax.ShapeDtypeStruct(mix.shape, mix.dtype),
        input_output_aliases={6: 0},
        compiler_params=_params(2),
        name="sgu_fwd",
    )(proj, proj, ln_g, ln_b, w_s, b_col, mix)


def _mem_kv(mem, g, b, wk, wv):
    m_len, d = mem.shape

    def body(m_ref, g_ref, b_ref, wk_ref, wv_ref, mb_ref, xh_ref, rs_ref, k_ref, v_ref):
        m, xhat, rstd = _ln_fwd(m_ref[...], g_ref[...], b_ref[...])
        mb = m.astype(BF16)
        mb_ref[...] = mb
        xh_ref[...] = xhat
        rs_ref[...] = rstd
        k_ref[...] = _dot(mb, wk_ref[...]).astype(BF16)
        v_ref[...] = _dot(mb, wv_ref[...]).astype(BF16)

    return pl.pallas_call(
        body,
        out_shape=[jax.ShapeDtypeStruct((m_len, d), BF16), jax.ShapeDtypeStruct((m_len, d), F32),
                   jax.ShapeDtypeStruct((m_len, 1), F32), jax.ShapeDtypeStruct((m_len, d), BF16),
                   jax.ShapeDtypeStruct((m_len, d), BF16)],
        compiler_params=pltpu.CompilerParams(vmem_limit_bytes=VMEM_LIMIT_V7X),
        name="mem_kv",
    )(mem, g, b, wk, wv)


def _softmax_rows(s):
    m = jnp.max(s, axis=-1, keepdims=True)
    p = jnp.exp(s - m)
    return p / jnp.sum(p, axis=-1, keepdims=True)


def _attn_fwd(hb, wq, kb, vb):
    t, d = hb.shape
    tm = _row_tile(t)
    dh = d // X_HEADS
    scale = dh ** -0.5

    def body(h_ref, wq_ref, k_ref, v_ref, q_ref, o_ref):
        q = _dot(h_ref[...], wq_ref[...]).astype(BF16)
        q_ref[...] = q
        for hd in range(X_HEADS):
            sl = slice(hd * dh, (hd + 1) * dh)
            p = _softmax_rows(_dot_nt(q[:, sl], k_ref[:, sl]) * scale)
            o_ref[:, sl] = _dot(p.astype(BF16), v_ref[:, sl]).astype(BF16)

    row = pl.BlockSpec((tm, d), lambda i: (i, 0))
    full = lambda a: pl.BlockSpec(a.shape, lambda i: (0, 0))
    return pl.pallas_call(
        body,
        grid=(t // tm,),
        in_specs=[row, full(wq), full(kb), full(vb)],
        out_specs=[row, row],
        out_shape=[jax.ShapeDtypeStruct((t, d), BF16), jax.ShapeDtypeStruct((t, d), BF16)],
        compiler_params=_params(1),
        name="attn_fwd",
    )(hb, wq, kb, vb)


def _loss_ln_bwd(h, xhat, rstd, g, target):
    t, d = h.shape
    tm = _row_tile(t)
    nt = t // tm

    def body(h_ref, xh_ref, rs_ref, g_ref, t_ref, loss_ref, dy_ref, dyb_ref, dg_ref, db_ref, lacc):
        i = pl.program_id(0)

        @pl.when(i == 0)
        def _():
            lacc[...] = jnp.zeros_like(lacc)
            dg_ref[...] = jnp.zeros_like(dg_ref)
            db_ref[...] = jnp.zeros_like(db_ref)

        err = h_ref[...] - t_ref[...]
        lacc[...] += jnp.sum(err * err, axis=0, keepdims=True)
        dy, dg, db = _ln_bwd(err * (1.0 / d), xh_ref[...], rs_ref[...], g_ref[...])
        dy_ref[...] = dy
        dyb_ref[...] = dy.astype(BF16)
        dg_ref[...] += dg
        db_ref[...] += db

        @pl.when(i == nt - 1)
        def _():
            loss_ref[...] = jnp.zeros_like(loss_ref) + jnp.sum(lacc[...], axis=1, keepdims=True) * (0.5 / d)

    row = pl.BlockSpec((tm, d), lambda i: (i, 0))
    vec = pl.BlockSpec((1, d), lambda i: (0, 0))
    return pl.pallas_call(
        body,
        grid=(nt,),
        in_specs=[row, row, pl.BlockSpec((tm, 1), lambda i: (i, 0)), vec, row],
        out_specs=[pl.BlockSpec((1, LANES), lambda i: (0, 0)), row, row, vec, vec],
        out_shape=[jax.ShapeDtypeStruct((1, LANES), F32), jax.ShapeDtypeStruct((t, d), F32),
                   jax.ShapeDtypeStruct((t, d), BF16), jax.ShapeDtypeStruct((1, d), F32),
                   jax.ShapeDtypeStruct((1, d), F32)],
        scratch_shapes=[pltpu.VMEM((1, d), F32)],
        compiler_params=_params(1),
        name="loss_ln_bwd",
    )(h, xhat, rstd, g, target)


def _ffn_bwd_act(dyb, wd, a, b, coef, name, deps=()):
    t, d = dyb.shape
    f = wd.shape[0]
    tm = _row_tile(t)
    tn = _col_tile(f)

    def body(dy_ref, wd_ref, a_ref, b_ref, da_ref, db_ref):
        dy = dy_ref[...]
        for c in range(f // tn):
            cols = slice(c * tn, (c + 1) * tn)
            ds = _dot_nt(dy, wd_ref[cols, :]) * coef
            silu, dsilu = _silu_and_grad(a_ref[:, cols].astype(F32))
            da_ref[:, cols] = (ds * b_ref[:, cols].astype(F32) * dsilu).astype(BF16)
            db_ref[:, cols] = (ds * silu).astype(BF16)

    act = pl.BlockSpec((tm, f), lambda i: (i, 0))
    return pl.pallas_call(
        _drop_deps(body, 4, len(deps)),
        grid=(t // tm,),
        in_specs=[pl.BlockSpec((tm, d), lambda i: (i, 0)), _resident(wd), act, act] + [ANY] * len(deps),
        out_specs=[act, act],
        out_shape=[jax.ShapeDtypeStruct((t, f), BF16), jax.ShapeDtypeStruct((t, f), BF16)],
        compiler_params=_params(1),
        name=name,
    )(dyb, wd, a, b, *deps)


def _mm_tn(a, b, name, scale=1.0, deps=()):
    t, m = a.shape
    n = b.shape[1]
    tt = _row_tile(t)
    nt = t // tt
    tm_o = m // 2 if (m > n and m * n > 2 ** 21) else m
    tn_o = n // 2 if (n > m and m * n > 2 ** 21) else n

    def body(a_ref, b_ref, o_ref, acc):
        k = pl.program_id(2)

        @pl.when(k == 0)
        def _():
            acc[...] = jnp.zeros_like(acc)

        acc[...] += _dot_tn(a_ref[...], b_ref[...])

        @pl.when(k == nt - 1)
        def _():
            o_ref[...] = (acc[...] * scale).astype(BF16)

    return pl.pallas_call(
        _drop_deps(body, 2, len(deps)),
        grid=(m // tm_o, n // tn_o, nt),
        in_specs=[pl.BlockSpec((tt, tm_o), lambda i, j, k: (k, i)), pl.BlockSpec((tt, tn_o), lambda i, j, k: (k, j))]
        + [ANY] * len(deps),
        out_specs=pl.BlockSpec((tm_o, tn_o), lambda i, j, k: (i, j)),
        out_shape=jax.ShapeDtypeStruct((m, n), BF16),
        scratch_shapes=[pltpu.VMEM((tm_o, tn_o), F32)],
        compiler_params=_params(3),
        name=name,
    )(a, b, *deps)


def _mm_nt(lhs, w, name):
    t, d = lhs.shape
    kd = w.shape[0]
    tm = _row_tile(t)

    def body(l_ref, w_ref, o_ref):
        o_ref[...] = _dot_nt(l_ref[...], w_ref[...])

    return pl.pallas_call(
        body,
        grid=(t // tm,),
        in_specs=[pl.BlockSpec((tm, d), lambda i: (i, 0)), _resident(w)],
        out_specs=pl.BlockSpec((tm, kd), lambda i: (i, 0)),
        out_shape=jax.ShapeDtypeStruct((t, kd), F32),
        compiler_params=_params(1),
        name=name,
    )(lhs, w)


def _dx_ln(dy, pairs, ln, name, deps=()):
    t, d = dy.shape
    npair = len(pairs)
    tm = min(t, 512 // npair)
    nt = t // tm
    n_in = 1 + 2 * npair + (3 if ln is not None else 0)

    def body(*refs):
        dy_ref = refs[0]
        pr = refs[1:1 + 2 * npair]
        pos = 1 + 2 * npair
        dh = ALPHA * dy_ref[...]
        for p in range(npair):
            dh = dh + _dot_nt(pr[2 * p][...], pr[2 * p + 1][...])
        if ln is not None:
            xh_ref, rs_ref, g_ref = refs[pos:pos + 3]
            dyo_ref, dyb_ref, dg_ref, db_ref = refs[pos + 3:pos + 7]

            @pl.when(pl.program_id(0) == 0)
            def _():
                dg_ref[...] = jnp.zeros_like(dg_ref)
                db_ref[...] = jnp.zeros_like(db_ref)

            dyp, dg, db = _ln_bwd(dh, xh_ref[...], rs_ref[...], g_ref[...])
            dyo_ref[...] = dyp
            dyb_ref[...] = dyp.astype(BF16)
            dg_ref[...] += dg
            db_ref[...] += db
        else:
            refs[pos][...] = dh

    row = pl.BlockSpec((tm, d), lambda i: (i, 0))
    vec = pl.BlockSpec((1, d), lambda i: (0, 0))
    in_specs = [row]
    args = [dy]
    for lhs, w in pairs:
        in_specs += [pl.BlockSpec((tm, lhs.shape[1]), lambda i: (i, 0)), _resident(w)]
        args += [lhs, w]
    if ln is not None:
        in_specs += [row, pl.BlockSpec((tm, 1), lambda i: (i, 0)), vec]
        args += list(ln)
        out_specs = [row, row, vec, vec]
        out_shape = [jax.ShapeDtypeStruct((t, d), F32), jax.ShapeDtypeStruct((t, d), BF16),
                     jax.ShapeDtypeStruct((1, d), F32), jax.ShapeDtypeStruct((1, d), F32)]
    else:
        out_specs = row
        out_shape = jax.ShapeDtypeStruct((t, d), F32)
    return pl.pallas_call(
        _drop_deps(body, n_in, len(deps)),
        grid=(nt,),
        in_specs=in_specs + [ANY] * len(deps),
        out_specs=out_specs,
        out_shape=out_shape,
        compiler_params=_params(1),
        name=name,
    )(*args, *deps)


def _hgrn_bwd(proj, oraw, dmix, states, logits, gn):
    t = proj.shape[0]
    ct = _hg_tile(t)
    nct = t // ct
    nblk = ct // HG_BLOCK
    nh = HG_HEADS

    def body(q_ref, fz_ref, iv_ref, gg_ref, or_ref, do_ref, st_ref, lg_ref, gn_ref,
             dq_ref, dfz_ref, div_ref, dgg_ref, dlg_ref, dgn_ref,
             dstate, qt_s, kt_s, k_s, b_s, eb_s, ekb_s, dec_s, dor_s, dbl_s, gr_s, dk_s, dlb_acc):
        c = pl.program_id(1)

        @pl.when(c == 0)
        def _():
            dstate[...] = jnp.zeros_like(dstate)
            dlb_acc[...] = jnp.zeros_like(dlb_acc)
            dgn_ref[...] = jnp.zeros_like(dgn_ref)

        lb = _lower_bound(lg_ref[...])
        q = q_ref[...]
        sig, nsig, f, k = _forget_terms(fz_ref[...], lb)
        logf = jnp.log(f)
        lower, upper, whole = _block_masks(ct)
        b = _mask_dot(lower, logf)
        bend = _mask_dot(whole, logf)
        eb = jnp.exp(b)
        ekb = jnp.exp(bend - b)
        qt_s[...] = (q * eb).astype(BF16)
        kt_s[...] = (k * ekb).astype(BF16)
        k_s[...] = k
        b_s[...] = b
        eb_s[...] = eb
        ekb_s[...] = ekb
        dec_s[...] = jnp.exp(bend)
        oraw = or_ref[...]
        r = lax.rsqrt(jnp.mean(oraw * oraw, axis=-1, keepdims=True) + LN_EPS)
        on = oraw * r
        gg = gg_ref[...]
        silu, dsilu = _silu_and_grad(gg)
        doa = do_ref[...]
        gnv = gn_ref[...]
        dgg_ref[...] = (doa * on * gnv * dsilu).astype(BF16)
        dyn = doa * silu
        dgn_ref[...] += jnp.sum(dyn * on, axis=0, keepdims=True)
        don = dyn * gnv
        dor_s[...] = r * (don - on * jnp.mean(don * on, axis=-1, keepdims=True))
        tidx = lax.broadcasted_iota(jnp.int32, (HG_BLOCK, HG_DIM), 0)

        def blk(ii, carry):
            i = nblk - 1 - ii
            r0 = pl.multiple_of(i * HG_BLOCK, HG_BLOCK)
            rows = pl.ds(r0, HG_BLOCK)
            st = st_ref[i]
            dst = dstate[...]
            dstb = dst.astype(BF16)
            do = dor_s[rows, :]
            dob = do.astype(BF16)
            v = iv_ref[rows, :]
            vb = v.astype(BF16)
            qq = q_ref[rows, :]
            kk = k_s[rows, :]
            bb = b_s[rows, :]
            qt = qt_s[rows, :]
            kt = kt_s[rows, :]
            dec = dec_s[pl.ds(r0, 1), :]
            dkt = _dot(vb, dstb)
            dq = _dot(dob, st.astype(BF16)) * eb_s[rows, :]
            dk = dkt * ekb_s[rows, :]
            dv = _dot_nt(kt, dstb)
            gend = jnp.sum(kk * dk, axis=0, keepdims=True) + dec * jnp.sum(dst * st, axis=0, keepdims=True)
            for s in range(HG_BLOCK):
                ks = kk[s:s + 1, :]
                e = jnp.where(tidx >= s, jnp.exp(jnp.minimum(bb - bb[s:s + 1, :], 0.0)), 0.0)
                qe = qq * e
                acol = jnp.sum(qe * ks, axis=1, keepdims=True)
                dacol = jnp.sum(do * v[s:s + 1, :], axis=1, keepdims=True)
                dq = dq + dacol * (ks * e)
                dk_row = jnp.sum(dacol * qe, axis=0, keepdims=True)
                dv_row = jnp.sum(acol * do, axis=0, keepdims=True)
                dk = dk + jnp.where(tidx == s, dk_row, 0.0)
                dv = dv + jnp.where(tidx == s, dv_row, 0.0)
            dq_ref[rows, :] = dq.astype(BF16)
            div_ref[rows, :] = dv.astype(BF16)
            dk_s[rows, :] = dk
            dbl_s[rows, :] = qq * dq - kk * dk
            gr_s[rows, :] = jnp.zeros((HG_BLOCK, HG_DIM), F32) + gend
            dstate[...] = dst * dec + _dot_tn(dob, qt)
            return carry

        lax.fori_loop(0, nblk, blk, 0, unroll=HG_UNROLL)
        dlogf = _mask_dot(upper, dbl_s[...]) + gr_s[...]
        dk = dk_s[...]
        dfz_ref[...] = ((dlogf / f - dk) * ((1.0 - lb) * sig * nsig)).astype(BF16)
        dlb_acc[...] += jnp.sum((dlogf / f - dk) * nsig, axis=0, keepdims=True)

        @pl.when(c == nct - 1)
        def _():
            dl0 = dlb_acc[...] * lb * (1.0 - lb)
            layer = lax.broadcasted_iota(jnp.int32, (2, HG_DIM), 0)
            dlg_ref[...] = jnp.where(layer == 0, dl0, -dl0)

    def slab(off):
        return pl.BlockSpec((ct, HG_DIM), lambda h, c: (nct - 1 - c, off + h))

    out_slab = pl.BlockSpec((ct, HG_DIM), lambda h, c: (nct - 1 - c, h))
    tile_f32 = pltpu.VMEM((ct, HG_DIM), F32)
    tile_b16 = pltpu.VMEM((ct, HG_DIM), BF16)
    slab_shape = jax.ShapeDtypeStruct((t, nh * HG_DIM), BF16)
    return pl.pallas_call(
        body,
        grid=(nh, nct),
        in_specs=[slab(0), slab(nh), slab(2 * nh), slab(3 * nh), slab(0), slab(0),
                  pl.BlockSpec((None, nblk, HG_DIM, HG_DIM), lambda h, c: (h, nct - 1 - c, 0, 0)),
                  pl.BlockSpec((None, 2, HG_DIM), lambda h, c: (h, 0, 0)),
                  pl.BlockSpec((1, HG_DIM), lambda h, c: (0, 0))],
        out_specs=[out_slab, out_slab, out_slab, out_slab,
                   pl.BlockSpec((None, 2, HG_DIM), lambda h, c: (h, 0, 0)),
                   pl.BlockSpec((None, 1, HG_DIM), lambda h, c: (h, 0, 0))],
        out_shape=[slab_shape, slab_shape, slab_shape, slab_shape,
                   jax.ShapeDtypeStruct((nh, 2, HG_DIM), F32), jax.ShapeDtypeStruct((nh, 1, HG_DIM), F32)],
        scratch_shapes=[pltpu.VMEM((HG_DIM, HG_DIM), F32), tile_b16, tile_b16, tile_f32, tile_f32, tile_f32, tile_f32,
                        tile_f32, tile_f32, tile_f32, tile_f32, tile_f32, pltpu.VMEM((1, HG_DIM), F32)],
        compiler_params=_params(2),
        name="hgrn_bwd",
    )(proj, proj, proj, proj, oraw, dmix, states, logits, gn)


def _sgu_bwd(proj, dmix, ln_g, ln_b, w_s, w_t, b_col):
    t = proj.shape[0]
    ct = _sg_tile(t)
    nct = t // ct
    ng = SG_GROUPS
    off_u = 4 * HG_HEADS
    off_v = off_u + ng
    n = SG_CHUNK

    def body(u_ref, v_ref, do_ref, g_ref, b_ref, w_ref, wt_ref, bs_ref, du_ref, dv_ref, dg_ref, db_ref, dw_ref, dbs_ref):
        c = pl.program_id(1)

        @pl.when(c == 0)
        def _():
            dg_ref[...] = jnp.zeros_like(dg_ref)
            db_ref[...] = jnp.zeros_like(db_ref)
            dw_ref[...] = jnp.zeros_like(dw_ref)
            dbs_ref[...] = jnp.zeros_like(dbs_ref)

        r = lax.broadcasted_iota(jnp.int32, (n, n), 0)
        cc = lax.broadcasted_iota(jnp.int32, (n, n), 1)
        wm = jnp.where(cc <= r, w_ref[...], 0.0).astype(BF16)
        wmt = jnp.where(r <= cc, wt_ref[...], 0.0).astype(BF16)
        for ci in range(ct // n):
            rows = slice(ci * n, (ci + 1) * n)
            ua, dua, dva, vn, xhat, rstd, s = _sgu_chunk_fwd(u_ref[rows, :], v_ref[rows, :], g_ref[...], b_ref[...],
                                                             wm, bs_ref[...])
            do = do_ref[rows, :]
            du_ref[rows, :] = (do * s * dua).astype(BF16)
            ds = do * ua
            dsb = ds.astype(BF16)
            dbs_ref[...] += jnp.sum(ds, axis=1, keepdims=True)
            dw_ref[...] += _dot_nt(dsb, vn.astype(BF16))
            dvn = _dot(wmt, dsb)
            dva_in, dg, db = _ln_bwd(dvn, xhat, rstd, g_ref[...])
            dg_ref[...] += dg
            db_ref[...] += db
            dv_ref[rows, :] = (dva_in * dva).astype(BF16)

        @pl.when(c == nct - 1)
        def _():
            dw_ref[...] = jnp.where(cc <= r, dw_ref[...], 0.0)

    vec = pl.BlockSpec((None, 1, SG_DIM), lambda g, c: (g, 0, 0))
    mat = pl.BlockSpec((None, n, n), lambda g, c: (g, 0, 0))
    col = pl.BlockSpec((None, n, 1), lambda g, c: (g, 0, 0))
    out_slab = pl.BlockSpec((ct, SG_DIM), lambda g, c: (c, g))
    return pl.pallas_call(
        body,
        grid=(ng, nct),
        in_specs=[pl.BlockSpec((ct, SG_DIM), lambda g, c: (c, off_u + g)),
                  pl.BlockSpec((ct, SG_DIM), lambda g, c: (c, off_v + g)),
                  pl.BlockSpec((ct, SG_DIM), lambda g, c: (c, ng + g)), vec, vec, mat, mat, col],
        out_specs=[out_slab, out_slab, vec, vec, mat, col],
        out_shape=[jax.ShapeDtypeStruct((t, ng * SG_DIM), BF16), jax.ShapeDtypeStruct((t, ng * SG_DIM), BF16),
                   jax.ShapeDtypeStruct((ng, 1, SG_DIM), F32), jax.ShapeDtypeStruct((ng, 1, SG_DIM), F32),
                   jax.ShapeDtypeStruct((ng, n, n), F32), jax.ShapeDtypeStruct((ng, n, 1), F32)],
        compiler_params=_params(2),
        name="sgu_bwd",
    )(proj, proj, dmix, ln_g, ln_b, w_s, w_t, b_col)


def _attn_bwd(dyb, wo, qb, kb, vb):
    t, d = dyb.shape
    m_len = kb.shape[0]
    tm = _row_tile(t)
    dh = d // X_HEADS
    scale = dh ** -0.5

    def body(dy_ref, wo_ref, q_ref, k_ref, v_ref, dq_ref, dk_ref, dv_ref):
        i = pl.program_id(0)

        @pl.when(i == 0)
        def _():
            dk_ref[...] = jnp.zeros_like(dk_ref)
            dv_ref[...] = jnp.zeros_like(dv_ref)

        do = _dot_nt(dy_ref[...], wo_ref[...]).astype(BF16)
        for hd in range(X_HEADS):
            sl = slice(hd * dh, (hd + 1) * dh)
            qh = q_ref[:, sl]
            p = _softmax_rows(_dot_nt(qh, k_ref[:, sl]) * scale)
            doh = do[:, sl]
            dp = _dot_nt(doh, v_ref[:, sl])
            ds = (p * (dp - jnp.sum(dp * p, axis=-1, keepdims=True)) * scale).astype(BF16)
            dq_ref[:, sl] = _dot(ds, k_ref[:, sl]).astype(BF16)
            dk_ref[:, sl] += _dot_tn(ds, qh)
            dv_ref[:, sl] += _dot_tn(p.astype(BF16), doh)

    row = pl.BlockSpec((tm, d), lambda i: (i, 0))
    full = lambda a: pl.BlockSpec(a.shape, lambda i: (0, 0))
    kv = pl.BlockSpec((m_len, d), lambda i: (0, 0))
    return pl.pallas_call(
        body,
        grid=(t // tm,),
        in_specs=[row, full(wo), row, full(kb), full(vb)],
        out_specs=[row, kv, kv],
        out_shape=[jax.ShapeDtypeStruct((t, d), BF16), jax.ShapeDtypeStruct((m_len, d), F32),
                   jax.ShapeDtypeStruct((m_len, d), F32)],
        compiler_params=_params(1),
        name="attn_bwd",
    )(dyb, wo, qb, kb, vb)


def _mem_bwd(dk, dv, mb, xhat, rstd, g, wk, wv):
    m_len, d = dk.shape

    def body(dk_ref, dv_ref, mb_ref, xh_ref, rs_ref, g_ref, wk_ref, wv_ref, gwk_ref, gwv_ref, dg_ref, db_ref):
        dkb = dk_ref[...].astype(BF16)
        dvb = dv_ref[...].astype(BF16)
        mb_v = mb_ref[...]
        gwk_ref[...] = _dot_tn(mb_v, dkb).astype(BF16)
        gwv_ref[...] = _dot_tn(mb_v, dvb).astype(BF16)
        dm = _dot_nt(dkb, wk_ref[...]) + _dot_nt(dvb, wv_ref[...])
        _, dg, db = _ln_bwd(dm, xh_ref[...], rs_ref[...], g_ref[...])
        dg_ref[...] = dg
        db_ref[...] = db

    return pl.pallas_call(
        body,
        out_shape=[jax.ShapeDtypeStruct((d, d), BF16), jax.ShapeDtypeStruct((d, d), BF16),
                   jax.ShapeDtypeStruct((1, d), F32), jax.ShapeDtypeStruct((1, d), F32)],
        compiler_params=pltpu.CompilerParams(vmem_limit_bytes=VMEM_LIMIT_V7X),
        name="mem_bwd",
    )(dk, dv, mb, xhat, rstd, g, wk, wv)


def _adamw(w, g, m, v):
    m = ADAM_B1 * m + (1.0 - ADAM_B1) * g
    v = ADAM_B2 * v + (1.0 - ADAM_B2) * (g * g)
    m_hat = m / (1.0 - ADAM_B1 ** ADAM_STEP)
    v_hat = v / (1.0 - ADAM_B2 ** ADAM_STEP)
    delta = -ADAM_LR * (m_hat / (jnp.sqrt(v_hat) + ADAM_EPS) + ADAM_WD * w)
    return delta, m, v


def _slot_sum(ref):
    g = ref[0].astype(F32)
    for s in range(1, N_DEV):
        g = g + ref[s].astype(F32)
    return g


def _adam_sharded(lands, w, m, v, axis, name):
    rows, cols = w.shape
    nl = len(lands)
    transposed = axis == 1 and nl == 2
    if transposed:
        rows, cols = cols, rows
        tr = 256
        grid = (rows // tr,)
        wblk = pl.BlockSpec((cols, tr), lambda i: (0, i))
        lblk = [pl.BlockSpec((N_DEV, tr, a.shape[2]), lambda i: (0, i, 0)) for a in lands]
    elif axis == 1:
        tr = 256 if rows % 256 == 0 else rows
        grid = (rows // tr,)
        wblk = pl.BlockSpec((tr, cols), lambda i: (i, 0))
        lblk = [pl.BlockSpec((N_DEV, tr, a.shape[2]), lambda i: (0, i, 0)) for a in lands]
    else:
        tc = _col_tile(cols)
        grid = (cols // tc,)
        wblk = pl.BlockSpec((rows, tc), lambda i: (0, i))
        lblk = [pl.BlockSpec((N_DEV, a.shape[1], tc), lambda i: (0, 0, i)) for a in lands]

    def body(*refs):
        w_ref, m_ref, v_ref = refs[nl:nl + 3]
        g_ref, d_ref, nm_ref, nv_ref = refs[nl + 3:]
        g = _slot_sum(refs[0])
        if nl == 2:
            tail = _slot_sum(refs[1])
            if transposed:
                g = jnp.concatenate([g.T, tail.T[:cols - g.shape[1], :]], axis=0)
            elif axis == 1:
                g = jnp.concatenate([g, tail[:, :cols - g.shape[1]]], axis=1)
            else:
                g = jnp.concatenate([g, tail[:rows - g.shape[0], :]], axis=0)
        delta, nm, nv = _adamw(w_ref[...], g, m_ref[...], v_ref[...])
        g_ref[...] = g
        d_ref[...] = delta
        nm_ref[...] = nm
        nv_ref[...] = nv

    shp = jax.ShapeDtypeStruct(w.shape, F32)
    return pl.pallas_call(
        body,
        grid=grid,
        in_specs=lblk + [wblk, wblk, wblk],
        out_specs=[wblk, wblk, wblk, wblk],
        out_shape=[shp, shp, shp, shp],
        compiler_params=_params(1),
        name=name,
    )(*lands, w, m, v)


def _mesh_pos():
    return lax.axis_index("x"), lax.axis_index("y"), lax.axis_index("c")


def _peer(k):
    x, y, c = _mesh_pos()
    pos = (x ^ (k >> 2), y ^ ((k >> 1) & 1), c ^ (k & 1))
    return pos, 4 * pos[0] + 2 * pos[1] + pos[2]


def _sem_index(row, k):
    return row * (N_DEV - 1) + k - 1


def _window(ref, axis, start, size):
    align = 16 if axis == 0 else LANES
    start = pl.multiple_of(start, align)
    return ref.at[pl.ds(start, size), :] if axis == 0 else ref.at[:, pl.ds(start, size)]


def _piece_refs(piece, srcs, lands, me, peer):
    kind, si, li, axis, base, stride, shape = piece
    if kind == "gather":
        return srcs[si], _window(lands[li], axis, base + stride * me, shape[axis])
    return _window(srcs[si], axis, base + stride * peer, shape[axis]), lands[li].at[me]


def _place_own(srcs, land_shapes, pieces, name):
    ns, nl, npc = len(srcs), len(land_shapes), len(pieces)

    def body(*refs):
        s_refs = refs[:ns]
        l_refs = refs[ns:ns + nl]
        bufs = refs[ns + nl:ns + nl + npc]
        sems = refs[ns + nl + npc]
        x, y, c = _mesh_pos()
        me = 4 * x + 2 * y + c
        loads = []
        for p, piece in enumerate(pieces):
            src, dst = _piece_refs(piece, s_refs, l_refs, me, me)
            cp = pltpu.make_async_copy(src, bufs[p], sems.at[0, p])
            cp.start()
            loads.append((cp, dst))
        stores = []
        for p, (cp, dst) in enumerate(loads):
            cp.wait()
            out = pltpu.make_async_copy(bufs[p], dst, sems.at[1, p])
            out.start()
            stores.append(out)
        for out in stores:
            out.wait()

    out = pl.pallas_call(
        body,
        in_specs=[ANY] * ns,
        out_specs=[ANY] * nl,
        out_shape=list(land_shapes),
        scratch_shapes=[pltpu.VMEM(pc[6], srcs[pc[1]].dtype) for pc in pieces] + [pltpu.SemaphoreType.DMA((2, npc))],
        compiler_params=pltpu.CompilerParams(vmem_limit_bytes=VMEM_LIMIT_V7X),
        name=name,
    )(*srcs)
    return list(out)


def _comm_start(srcs, lands, pieces, groups, name, after=()):
    ns, nl, na, ng = len(srcs), len(lands), len(after), len(groups)

    def body(*refs):
        s_refs = refs[:ns]
        l_refs = refs[ns:ns + nl]
        outs = refs[ns + nl + na:]
        sems = outs[:2 * ng]
        token = outs[-1]
        x, y, c = _mesh_pos()
        me = 4 * x + 2 * y + c
        for g, members in enumerate(groups):
            for row, p in enumerate(members):
                for k in range(1, N_DEV):
                    pos, peer = _peer(k)
                    src, dst = _piece_refs(pieces[p], s_refs, l_refs, me, peer)
                    pltpu.make_async_remote_copy(src_ref=src, dst_ref=dst, send_sem=sems[2 * g].at[_sem_index(row, k)],
                                                 recv_sem=sems[2 * g + 1].at[_sem_index(row, k)], device_id=pos,
                                                 device_id_type=MESH_ID).start()
        token[...] = jnp.zeros_like(token)

    sem_shapes = []
    for members in groups:
        sem_shapes += [pltpu.SemaphoreType.DMA((len(members) * (N_DEV - 1),))] * 2
    hbm_of = lambda a: pltpu.HBM(a.shape, a.dtype)
    out = pl.pallas_call(
        body,
        in_specs=[HBM] * (ns + nl) + [ANY] * na,
        out_specs=[SEM] * (2 * ng) + [HBM] * (ns + nl) + [pl.BlockSpec(memory_space=pltpu.VMEM)],
        out_shape=sem_shapes + [hbm_of(a) for a in srcs] + [hbm_of(a) for a in lands]
        + [jax.ShapeDtypeStruct((8, LANES), F32)],
        input_output_aliases={i: 2 * ng + i for i in range(ns + nl)},
        compiler_params=pltpu.CompilerParams(has_side_effects=DATAFLOW),
        name=name,
    )(*[pltpu.with_memory_space_constraint(a, pltpu.HBM) for a in list(srcs) + list(lands)], *after)
    sems = [(out[2 * g], out[2 * g + 1]) for g in range(ng)]
    return sems, list(out[2 * ng:2 * ng + ns]), list(out[2 * ng + ns:2 * ng + ns + nl]), out[-1]


def _comm_wait(srcs, lands, pieces, members, sems, after, name):
    ns, nl, na = len(srcs), len(lands), len(after)

    def body(*refs):
        s_refs = refs[:ns]
        l_refs = refs[ns:ns + nl]
        send_sems, recv_sems = refs[ns + nl:ns + nl + 2]
        x, y, c = _mesh_pos()
        me = 4 * x + 2 * y + c
        for row, p in enumerate(members):
            for k in range(1, N_DEV):
                pos, peer = _peer(k)
                src, dst = _piece_refs(pieces[p], s_refs, l_refs, me, peer)
                cp = pltpu.make_async_remote_copy(src_ref=src, dst_ref=dst, send_sem=send_sems.at[_sem_index(row, k)],
                                                  recv_sem=recv_sems.at[_sem_index(row, k)], device_id=pos,
                                                  device_id_type=MESH_ID)
                cp.wait_send()
                cp.wait_recv()

    hbm_of = lambda a: pltpu.HBM(a.shape, a.dtype)
    out = pl.pallas_call(
        body,
        in_specs=[HBM] * (ns + nl) + [SEM, SEM] + [ANY] * na,
        out_specs=[HBM] * (ns + nl),
        out_shape=[hbm_of(a) for a in srcs] + [hbm_of(a) for a in lands],
        input_output_aliases={i: i for i in range(ns + nl)},
        compiler_params=pltpu.CompilerParams(has_side_effects=DATAFLOW),
        name=name,
    )(*srcs, *lands, sems[0], sems[1], *after)
    return list(out[ns:])


_CHIP_FLIPS = (2, 4, 6)
_SIBLING = 1


def _gather2_first(srcs, lands, pieces, name):
    ns, nl, npc = len(srcs), len(lands), len(pieces)

    def body(*refs):
        s_refs, l_refs = refs[:ns], refs[ns:ns + nl]
        send_sems, recv_sib, recv_ici = refs[ns + nl:ns + nl + 3]
        token = refs[-1]
        x, y, c = _mesh_pos()
        me = 4 * x + 2 * y + c
        for p, piece in enumerate(pieces):
            for j, k in enumerate((_SIBLING,) + _CHIP_FLIPS):
                pos, peer = _peer(k)
                src, dst = _piece_refs(piece, s_refs, l_refs, me, peer)
                recv = recv_sib.at[p] if j == 0 else recv_ici.at[3 * p + j - 1]
                pltpu.make_async_remote_copy(src_ref=src, dst_ref=dst, send_sem=send_sems.at[4 * p + j], recv_sem=recv,
                                             device_id=pos, device_id_type=MESH_ID).start()
        token[...] = jnp.zeros_like(token)

    hbm_of = lambda a: pltpu.HBM(a.shape, a.dtype)
    dma = lambda n: pltpu.SemaphoreType.DMA((n,))
    out = pl.pallas_call(
        body,
        in_specs=[HBM] * (ns + nl),
        out_specs=[SEM] * 3 + [HBM] * (ns + nl) + [pl.BlockSpec(memory_space=pltpu.VMEM)],
        out_shape=[dma(4 * npc), dma(npc), dma(3 * npc)] + [hbm_of(a) for a in srcs] + [hbm_of(a) for a in lands]
        + [jax.ShapeDtypeStruct((8, LANES), F32)],
        input_output_aliases={i: 3 + i for i in range(ns + nl)},
        compiler_params=pltpu.CompilerParams(has_side_effects=DATAFLOW),
        name=name,
    )(*[pltpu.with_memory_space_constraint(a, pltpu.HBM) for a in list(srcs) + list(lands)])
    return out[0], out[1], out[2], list(out[3:3 + ns]), list(out[3 + ns:3 + ns + nl]), out[-1]


def _landed_block(piece, lands, owner):
    _, _, li, axis, base, stride, shape = piece
    return _window(lands[li], axis, base + stride * owner, shape[axis])


def _gather2_pass(lands, pieces, recv_ici, after, name):
    nl, npc, na = len(lands), len(pieces), len(after)

    def body(*refs):
        l_refs = refs[:nl]
        recv_sems = refs[nl]
        send_fwd, recv_fwd = refs[nl + 1 + na:nl + 3 + na]
        sib, _ = _peer(_SIBLING)
        for p, piece in enumerate(pieces):
            for j, k in enumerate(_CHIP_FLIPS):
                pos, owner = _peer(k)
                block = _landed_block(piece, l_refs, owner)
                pltpu.make_async_remote_copy(src_ref=block, dst_ref=block, send_sem=send_fwd.at[3 * p + j],
                                             recv_sem=recv_sems.at[3 * p + j], device_id=pos,
                                             device_id_type=MESH_ID).wait_recv()
                pltpu.make_async_remote_copy(src_ref=block, dst_ref=block, send_sem=send_fwd.at[3 * p + j],
                                             recv_sem=recv_fwd.at[3 * p + j], device_id=sib,
                                             device_id_type=MESH_ID).start()

    hbm_of = lambda a: pltpu.HBM(a.shape, a.dtype)
    dma = lambda n: pltpu.SemaphoreType.DMA((n,))
    out = pl.pallas_call(
        body,
        in_specs=[HBM] * nl + [SEM] + [ANY] * na,
        out_specs=[SEM, SEM] + [HBM] * nl,
        out_shape=[dma(3 * npc), dma(3 * npc)] + [hbm_of(a) for a in lands],
        input_output_aliases={i: 2 + i for i in range(nl)},
        compiler_params=pltpu.CompilerParams(has_side_effects=DATAFLOW),
        name=name,
    )(*lands, recv_ici, *after)
    return out[0], out[1], list(out[2:])


def _gather2_last(srcs, lands, pieces, send_sems, recv_sib, send_fwd, recv_fwd, name):
    ns, nl = len(srcs), len(lands)

    def body(*refs):
        s_refs, l_refs = refs[:ns], refs[ns:ns + nl]
        send_a, recv_s, send_f, recv_f = refs[ns + nl:ns + nl + 4]
        x, y, c = _mesh_pos()
        me = 4 * x + 2 * y + c
        sib, sib_index = _peer(_SIBLING)
        for p, piece in enumerate(pieces):
            src, dst = _piece_refs(piece, s_refs, l_refs, me, sib_index)
            own = lambda s_sem, r_sem: pltpu.make_async_remote_copy(
                src_ref=src, dst_ref=dst, send_sem=s_sem, recv_sem=r_sem, device_id=sib, device_id_type=MESH_ID)
            own(send_a.at[4 * p], recv_s.at[p]).wait_recv()
            for j in range(4):
                own(send_a.at[4 * p + j], recv_s.at[p]).wait_send()
            for j in range(3):
                fwd = own(send_f.at[3 * p + j], recv_f.at[3 * p + j])
                fwd.wait_recv()
                fwd.wait_send()

    hbm_of = lambda a: pltpu.HBM(a.shape, a.dtype)
    out = pl.pallas_call(
        body,
        in_specs=[HBM] * (ns + nl) + [SEM] * 4,
        out_specs=[HBM] * (ns + nl),
        out_shape=[hbm_of(a) for a in srcs] + [hbm_of(a) for a in lands],
        input_output_aliases={i: i for i in range(ns + nl)},
        compiler_params=pltpu.CompilerParams(has_side_effects=DATAFLOW),
        name=name,
    )(*srcs, *lands, send_sems, recv_sib, send_fwd, recv_fwd)
    return list(out[ns:])


_SMALL_NAMES = ("ln1_g", "ln1_b", "hg_lb_logits", "hg_norm_g", "sg_ln_g", "sg_ln_b", "sg_w_s", "sg_b_s",
                "ln2_g", "ln2_b", "mem_ln_g", "mem_ln_b", "ln3_g", "ln3_b", "ln4_g", "ln4_b")


_VEC_NAMES = ("ln1_g", "ln1_b", "ln2_g", "ln2_b", "mem_ln_g", "mem_ln_b", "ln3_g", "ln3_b", "ln4_g", "ln4_b")
_ROW_NAMES = ("hg_lb_logits", "hg_norm_g", "sg_ln_g", "sg_ln_b", "sg_b_s", "sg_w_s")
VEC_ROWS = 16


def _row_plan(shapes):
    plan, pos = {}, 0
    for k in _ROW_NAMES:
        shp = shapes[k]
        slabs, off = [], pos
        for idx in itertools.product(*[range(dim) for dim in shp[:-2]]):
            slabs.append((idx, off, shp[-2]))
            off += shp[-2]
        plan[k] = (pos, slabs)
        pos = -(-off // 8) * 8
    return plan, -(-pos // 16) * 16


def _pack_small_grads(gs, shapes):
    vec = jnp.concatenate([gs[k].reshape(1, -1) for k in _VEC_NAMES], axis=0)
    vec = jnp.pad(vec, ((0, VEC_ROWS - vec.shape[0]), (0, 0)))
    plan, total = _row_plan(shapes)
    parts, pos = [], 0
    for k in _ROW_NAMES:
        first, slabs = plan[k]
        rows = gs[k].reshape(-1, LANES)
        end = slabs[-1][1] + slabs[-1][2]
        nxt = -(-end // 8) * 8
        parts.append(jnp.pad(rows, ((0, nxt - first - rows.shape[0]), (0, 0))))
        pos = nxt
    parts.append(jnp.zeros((total - pos, LANES), F32))
    return vec, jnp.concatenate(parts, axis=0)


def _adam_small(land_vec, land_rows, w, m, v):
    names = _VEC_NAMES + _ROW_NAMES
    n = len(names)
    shapes = {k: w[k].shape for k in names}
    plan, _ = _row_plan(shapes)

    def body(*refs):
        lv_ref, lr_ref = refs[:2]
        w_refs, m_refs, v_refs = refs[2:2 + n], refs[2 + n:2 + 2 * n], refs[2 + 2 * n:2 + 3 * n]
        outs = refs[2 + 3 * n:2 + 7 * n]
        gv_s, gr_s = refs[2 + 7 * n:]
        gv_s[...] = _slot_sum(lv_ref)
        gr_s[...] = _slot_sum(lr_ref)
        for p, k in enumerate(names):
            if k in _VEC_NAMES:
                row = _VEC_NAMES.index(k)
                slabs = [((), None, None)]
            else:
                slabs = plan[k][1]
            for idx, off, rows in slabs:
                g = gv_s[row:row + 1, :] if off is None else gr_s[off:off + rows, :]
                sel = idx + (slice(None), slice(None))
                delta, nm, nv = _adamw(w_refs[p][sel], g, m_refs[p][sel], v_refs[p][sel])
                for o, val in zip(range(4), (g, delta, nm, nv)):
                    outs[o * n + p][sel] = val

    flat = lambda tree: [tree[k] for k in names]
    shp = [jax.ShapeDtypeStruct(shapes[k], F32) for k in names]
    out = pl.pallas_call(
        body,
        out_shape=shp * 4,
        scratch_shapes=[pltpu.VMEM(land_vec.shape[1:], F32), pltpu.VMEM(land_rows.shape[1:], F32)],
        name="adam_small",
    )(land_vec, land_rows, *flat(w), *flat(m), *flat(v))
    return [dict(zip(names, out[o * n:(o + 1) * n])) for o in range(4)]


_COL_FFN = ("ffn1_w_gate", "ffn1_w_up", "ffn2_w_gate", "ffn2_w_up")
_ROW_FFN = ("ffn1_w_down", "ffn2_w_down")
_ROW_SQ = ("w_out", "xa_w_q", "xa_w_k", "xa_w_v", "xa_w_o")
_BIG_NAMES = ("ffn1_w_gate", "ffn1_w_up", "ffn1_w_down", "w_in", "w_out", "xa_w_q", "xa_w_k", "xa_w_v", "xa_w_o",
              "ffn2_w_gate", "ffn2_w_up", "ffn2_w_down")


def _ffn_split(fs):
    main = (fs // MXU_WIDTH_V7X) * MXU_WIDTH_V7X
    tail = fs - main
    tail_pad = -(-tail // LANES) * LANES
    assert main > 0 and tail > 0
    return main, tail, tail_pad


def _layout(name, shard_shape):
    r, c = shard_shape
    if name in _COL_FFN:
        main, tail, pad = _ffn_split(c)
        return (r, N_DEV * (main + pad)), [(1, 0, main, (r, main), (0, main)),
                                           (1, N_DEV * main, pad, (r, pad), (main, c))]
    if name in _ROW_FFN:
        main, tail, pad = _ffn_split(r)
        return (N_DEV * (main + pad), c), [(0, 0, main, (main, c), (0, main)),
                                           (0, N_DEV * main, pad, (pad, c), (main, r))]
    if name == "w_in":
        return (r, N_DEV * c), [(1, 0, c, (r, c), (0, c))]
    return (N_DEV * r, c), [(0, 0, r, (r, c), (0, r))]


def _shard_pieces(name, shard):
    out = []
    for axis, _, _, shape, (lo, hi) in _layout(name, shard.shape)[1]:
        part = shard[lo:hi, :] if axis == 0 else shard[:, lo:hi]
        pad = [(0, shape[0] - part.shape[0]), (0, shape[1] - part.shape[1])]
        out.append(jnp.pad(part, pad).astype(BF16))
    return out


def _gather_plan(names, shards):
    srcs, land_shapes, pieces, index = [], [], [], {}
    for li, name in enumerate(names):
        shape2d, parts = _layout(name, shards[name].shape)
        land_shapes.append(jax.ShapeDtypeStruct(shape2d, BF16))
        index[name] = []
        for (axis, base, stride, shape, _), src in zip(parts, _shard_pieces(name, shards[name])):
            index[name].append(len(pieces))
            pieces.append(("gather", len(srcs), li, axis, base, stride, shape))
            srcs.append(src)
    return srcs, land_shapes, pieces, index


def _scatter_plan(names, grads, shard_shapes):
    srcs, land_shapes, pieces, index = [], [], [], {}
    for si, name in enumerate(names):
        _, parts = _layout(name, shard_shapes[name])
        srcs.append(grads[name])
        index[name] = []
        for axis, base, stride, shape, _ in parts:
            index[name].append(len(land_shapes))
            pieces.append(("scatter", si, len(land_shapes), axis, base, stride, shape))
            land_shapes.append(jax.ShapeDtypeStruct((N_DEV,) + shape, grads[name].dtype))
    return srcs, land_shapes, pieces, index


def _small_views(small):
    row = lambda a: a.reshape(1, -1)
    ln = {k: row(small[k]) for k in ("ln1_g", "ln1_b", "ln2_g", "ln2_b", "ln3_g", "ln3_b", "ln4_g", "ln4_b",
                                      "mem_ln_g", "mem_ln_b", "hg_norm_g")}
    sg_w = small["sg_w_s"].reshape(SG_GROUPS, SG_CHUNK, SG_CHUNK)
    sg = dict(logits=jnp.swapaxes(small["hg_lb_logits"], 0, 1),
              g=small["sg_ln_g"].reshape(SG_GROUPS, 1, SG_DIM), b=small["sg_ln_b"].reshape(SG_GROUPS, 1, SG_DIM),
              w=sg_w, wt=jnp.swapaxes(sg_w, 1, 2), bs=small["sg_b_s"].reshape(SG_GROUPS, SG_CHUNK, 1))
    return ln, sg


def _forward(x, mem, get_w, small, first_deps=()):
    ln, sg = _small_views(small)
    xb = x.astype(BF16)
    a1, b1, s1 = _ffn_up(xb, get_w("ffn1_w_gate", ()), get_w("ffn1_w_up", ()), "ffn1_up", deps=first_deps)
    h1, h1b, xh1, rs1 = _mm_res_ln(s1, get_w("ffn1_w_down", (s1,)), x, ln["ln1_g"], ln["ln1_b"], 0.5, "ffn1_down_ln")
    proj = _mm_nn(h1b, get_w("w_in", (h1b,)), "mix_in")
    oraw, mix, states = _hgrn_fwd(proj, sg["logits"], ln["hg_norm_g"])
    mix = _sgu_fwd(proj, mix, sg["g"], sg["b"], sg["w"], sg["bs"])
    h2, h2b, xh2, rs2 = _mm_res_ln(mix, get_w("w_out", (mix,)), h1, ln["ln2_g"], ln["ln2_b"], 1.0, "mix_out_ln")
    mb, mxh, mrs, kb, vb = _mem_kv(mem, ln["mem_ln_g"], ln["mem_ln_b"], get_w("xa_w_k", (h2b,)), get_w("xa_w_v", (h2b,)))
    qb, att = _attn_fwd(h2b, get_w("xa_w_q", (kb,)), kb, vb)
    h3, h3b, xh3, rs3 = _mm_res_ln(att, get_w("xa_w_o", (att,)), h2, ln["ln3_g"], ln["ln3_b"], 1.0, "attn_out_ln")
    a2, b2, s2 = _ffn_up(h3b, get_w("ffn2_w_gate", (h3b,)), get_w("ffn2_w_up", (h3b,)), "ffn2_up")
    h4, _, xh4, rs4 = _mm_res_ln(s2, get_w("ffn2_w_down", (s2,)), h3, ln["ln4_g"], ln["ln4_b"], 0.5, "ffn2_down_ln")
    return dict(xb=xb, a1=a1, b1=b1, s1=s1, h1b=h1b, xh1=xh1, rs1=rs1, proj=proj, oraw=oraw, mix=mix, states=states,
                h2b=h2b, xh2=xh2, rs2=rs2, mb=mb, mxh=mxh, mrs=mrs, kb=kb, vb=vb, qb=qb, att=att, h3b=h3b, xh3=xh3,
                rs3=rs3, a2=a2, b2=b2, s2=s2, h4=h4, xh4=xh4, rs4=rs4)


def _backward(sv, target, wt, small, send):
    ln, sg = _small_views(small)
    gs = {}
    loss, dy4, dy4b, gs["ln4_g"], gs["ln4_b"] = _loss_ln_bwd(sv["h4"], sv["xh4"], sv["rs4"], ln["ln4_g"], target)
    g_down2 = _mm_tn(sv["s2"], dy4b, "g_ffn2_down", scale=0.5)
    da2, db2 = _ffn_bwd_act(dy4b, wt["ffn2_w_down"], sv["a2"], sv["b2"], 0.5, "ffn2_bwd_act")
    g_gate2 = _mm_tn(sv["h3b"], da2, "g_ffn2_gate")
    g_up2 = _mm_tn(sv["h3b"], db2, "g_ffn2_up")
    tok = send(("ffn2_w_down", "ffn2_w_gate", "ffn2_w_up"), (g_down2, g_gate2, g_up2))
    dy3, dy3b, gs["ln3_g"], gs["ln3_b"] = _dx_ln(dy4, [(da2, wt["ffn2_w_gate"]), (db2, wt["ffn2_w_up"])],
                                                 (sv["xh3"], sv["rs3"], ln["ln3_g"]), "ffn2_dx_ln", deps=(tok,))

    g_o = _mm_tn(sv["att"], dy3b, "g_xa_o")
    dqb, dk, dv = _attn_bwd(dy3b, wt["xa_w_o"], sv["qb"], sv["kb"], sv["vb"])
    g_q = _mm_tn(sv["h2b"], dqb, "g_xa_q")
    g_k, g_v, gs["mem_ln_g"], gs["mem_ln_b"] = _mem_bwd(dk, dv, sv["mb"], sv["mxh"], sv["mrs"], ln["mem_ln_g"],
                                                        wt["xa_w_k"], wt["xa_w_v"])
    tok = send(("xa_w_o", "xa_w_q", "xa_w_k", "xa_w_v"), (g_o, g_q, g_k, g_v))
    dy2, dy2b, gs["ln2_g"], gs["ln2_b"] = _dx_ln(dy3, [(dqb, wt["xa_w_q"])], (sv["xh2"], sv["rs2"], ln["ln2_g"]),
                                                 "attn_dx_ln", deps=(tok,))

    g_out = _mm_tn(sv["mix"], dy2b, "g_w_out")
    dmix = _mm_nt(dy2b, wt["w_out"], "mix_out_bwd")
    dq, dfz, div, dgg, dlg, dgn = _hgrn_bwd(sv["proj"], sv["oraw"], dmix, sv["states"], sg["logits"], ln["hg_norm_g"])
    du, dvv, gs["sg_ln_g"], gs["sg_ln_b"], gs["sg_w_s"], gs["sg_b_s"] = _sgu_bwd(
        sv["proj"], dmix, sg["g"], sg["b"], sg["w"], sg["wt"], sg["bs"])
    gs["hg_lb_logits"] = jnp.swapaxes(dlg, 0, 1)
    gs["hg_norm_g"] = jnp.sum(dgn, axis=0)
    dproj = jnp.concatenate([dq, dfz, div, dgg, du, dvv], axis=1)
    g_in = _mm_tn(sv["h1b"], dproj, "g_w_in")
    tok = send(("w_out", "w_in"), (g_out, g_in))
    dy1, dy1b, gs["ln1_g"], gs["ln1_b"] = _dx_ln(dy2, [(dproj, wt["w_in"])], (sv["xh1"], sv["rs1"], ln["ln1_g"]),
                                                 "mix_dx_ln", deps=(tok,))

    g_down1 = _mm_tn(sv["s1"], dy1b, "g_ffn1_down", scale=0.5)
    tok = send(("ffn1_w_down",), (g_down1,))
    da1, db1 = _ffn_bwd_act(dy1b, wt["ffn1_w_down"], sv["a1"], sv["b1"], 0.5, "ffn1_bwd_act", deps=(tok,))
    g_gate1 = _mm_tn(sv["xb"], da1, "g_ffn1_gate")
    tok = send(("ffn1_w_gate",), (g_gate1,))
    g_up1 = _mm_tn(sv["xb"], db1, "g_ffn1_up", deps=(tok,))
    tok = send(("ffn1_w_up",), (g_up1,))
    grad_x = _dx_ln(dy1, [(da1, wt["ffn1_w_gate"]), (db1, wt["ffn1_w_up"])], None, "ffn1_dx", deps=(tok,))
    return loss, grad_x, gs


_WEIGHT_NAMES = ("ffn1_w_gate", "ffn1_w_up", "ffn1_w_down", "ln1_g", "ln1_b", "w_in", "hg_lb_logits", "hg_norm_g",
                 "sg_ln_g", "sg_ln_b", "sg_w_s", "sg_b_s", "w_out", "ln2_g", "ln2_b", "mem_ln_g", "mem_ln_b",
                 "xa_w_q", "xa_w_k", "xa_w_v", "xa_w_o", "ln3_g", "ln3_b", "ffn2_w_gate", "ffn2_w_up", "ffn2_w_down",
                 "ln4_g", "ln4_b")
_FIRST = ("ffn1_w_gate", "ffn1_w_up")
_SECOND = ("ffn1_w_down", "w_in", "w_out")
_THIRD = ("xa_w_k", "xa_w_v", "xa_w_q", "xa_w_o", "ffn2_w_gate", "ffn2_w_up", "ffn2_w_down")


def kernel(x, mem, ffn1_w_gate, ffn1_w_up, ffn1_w_down, ln1_g, ln1_b, w_in, hg_lb_logits, hg_norm_g, sg_ln_g, sg_ln_b, sg_w_s, sg_b_s, w_out, ln2_g, ln2_b, mem_ln_g, mem_ln_b, xa_w_q, xa_w_k, xa_w_v, xa_w_o, ln3_g, ln3_b, ffn2_w_gate, ffn2_w_up, ffn2_w_down, ln4_g, ln4_b, loss_target, m_ffn1_w_gate, m_ffn1_w_up, m_ffn1_w_down, m_ln1_g, m_ln1_b, m_w_in, m_hg_lb_logits, m_hg_norm_g, m_sg_ln_g, m_sg_ln_b, m_sg_w_s, m_sg_b_s, m_w_out, m_ln2_g, m_ln2_b, m_mem_ln_g, m_mem_ln_b, m_xa_w_q, m_xa_w_k, m_xa_w_v, m_xa_w_o, m_ln3_g, m_ln3_b, m_ffn2_w_gate, m_ffn2_w_up, m_ffn2_w_down, m_ln4_g, m_ln4_b, v_ffn1_w_gate, v_ffn1_w_up, v_ffn1_w_down, v_ln1_g, v_ln1_b, v_w_in, v_hg_lb_logits, v_hg_norm_g, v_sg_ln_g, v_sg_ln_b, v_sg_w_s, v_sg_b_s, v_w_out, v_ln2_g, v_ln2_b, v_mem_ln_g, v_mem_ln_b, v_xa_w_q, v_xa_w_k, v_xa_w_v, v_xa_w_o, v_ln3_g, v_ln3_b, v_ffn2_w_gate, v_ffn2_w_up, v_ffn2_w_down, v_ln4_g, v_ln4_b):
    args = dict(locals())
    w = {k: args[k] for k in _WEIGHT_NAMES}
    m = {k: args["m_" + k] for k in _WEIGHT_NAMES}
    v = {k: args["v_" + k] for k in _WEIGHT_NAMES}
    shards = {k: w[k][0] for k in _BIG_NAMES}
    shard_shapes = {k: shards[k].shape for k in _BIG_NAMES}
    small = {k: (w[k][0] if k != "hg_lb_logits" else w[k]) for k in _SMALL_NAMES}

    srcs1, shapes1, pieces1, idx1 = _gather_plan(_FIRST, shards)
    lands1 = _place_own(srcs1, shapes1, pieces1, "gather_first_own")
    send1, rsib1, rici1, srcs1, lands1, _ = _gather2_first(srcs1, lands1, pieces1, "gather_first_start")
    sfwd1, rfwd1, lands1 = _gather2_pass(lands1, pieces1, rici1, (), "gather_first_pass")
    lands1 = _gather2_last(srcs1, lands1, pieces1, send1, rsib1, sfwd1, rfwd1, "gather_first_wait")
    wt = dict(zip(_FIRST, lands1))

    rest = _SECOND + _THIRD
    srcs2, shapes2, pieces2, idx2 = _gather_plan(rest, shards)
    lands2 = _place_own(srcs2, shapes2, pieces2, "gather_rest_own")
    groups2 = [list(idx2[k]) for k in rest]
    sems2, srcs2, lands2, tok2 = _comm_start(srcs2, lands2, pieces2, groups2, "gather_rest_start", after=tuple(lands1))
    pending = {k: gi for gi, k in enumerate(rest)}

    def get_w(name, after):
        if name in pending:
            gi = pending.pop(name)
            si = [pieces2[p][1] for p in groups2[gi]]
            sub = [(pieces2[p][0], row, 0) + pieces2[p][3:] for row, p in enumerate(groups2[gi])]
            wt[name] = _comm_wait([srcs2[s] for s in si], [lands2[gi]], sub, list(range(len(sub))), sems2[gi],
                                  after, "gather_wait_" + name)[0]
        return wt[name]

    sv = _forward(x[0], mem[0], get_w, small, first_deps=(tok2,))

    sent = []

    def send(names, grads):
        srcs, shapes, pieces, idx = _scatter_plan(names, dict(zip(names, grads)), shard_shapes)
        lands = _place_own(srcs, shapes, pieces, "grads_own_%d" % len(sent))
        sems, srcs, lands, tok = _comm_start(srcs, lands, pieces, [list(range(len(pieces)))],
                                             "grads_start_%d" % len(sent))
        sent.append((names, srcs, lands, pieces, idx, sems[0]))
        return tok

    loss, grad_x, gs = _backward(sv, loss_target[0], wt, small, send)

    ssrc = list(_pack_small_grads(gs, {k: w[k].shape for k in _SMALL_NAMES}))
    sp = [("scatter", i, i, 0, 0, 0, a.shape) for i, a in enumerate(ssrc)]
    sshape = [jax.ShapeDtypeStruct((N_DEV,) + a.shape, F32) for a in ssrc]
    sl = _place_own(ssrc, sshape, sp, "small_own")
    ssem, ssrc, sl, _ = _comm_start(ssrc, sl, sp, [[0, 1]], "small_start")

    out_g, out_d, out_m, out_v = {}, {}, {}, {}
    after = (grad_x,)
    for n_sent, (names, srcs, lands, pieces, idx, sems) in enumerate(sent):
        lands = _comm_wait(srcs, lands, pieces, list(range(len(pieces))), sems, after, "grads_wait_%d" % n_sent)
        for k in names:
            axis = 1 if (k in _COL_FFN or k == "w_in") else 0
            if k in _COL_FFN:
                res = _adam_sharded([lands[i] for i in idx[k]], w[k][0].T, m[k][0].T, v[k][0].T, axis, "adam_" + k)
                res = [r.T for r in res]
            else:
                res = _adam_sharded([lands[i] for i in idx[k]], w[k][0], m[k][0], v[k][0], axis, "adam_" + k)
            out_g[k], out_d[k], out_m[k], out_v[k] = [r[None] for r in res]
        after = (out_v[names[-1]],)
    sl = _comm_wait(ssrc, sl, sp, [0, 1], ssem[0], after, "small_wait")
    for dst, res in zip((out_g, out_d, out_m, out_v), _adam_small(sl[0], sl[1], w, m, v)):
        dst.update(res)

    loss_all = lax.psum(loss[0, 0], ("x", "y", "c"))
    return (loss_all, grad_x[None], *[out_g[k] for k in _WEIGHT_NAMES], *[out_d[k] for k in _WEIGHT_NAMES],
            *[out_m[k] for k in _WEIGHT_NAMES], *[out_v[k] for k in _WEIGHT_NAMES])
```

```python
import itertools

import jax
import jax.numpy as jnp
import numpy as np
from jax import lax
from jax.experimental import pallas as pl
from jax.experimental.pallas import tpu as pltpu

F32 = jnp.float32
BF16 = jnp.bfloat16

N_DEV = 8
ALPHA = 2.0 ** 0.25
LN_EPS = 1e-5
HG_HEADS = 4
HG_DIM = 128
SG_GROUPS = 4
SG_DIM = 128
SG_CHUNK = 128
X_HEADS = 4
HG_BLOCK = 16
HG_UNROLL = 4
ADAM_LR = 0.001
ADAM_B1 = 0.9
ADAM_B2 = 0.999
ADAM_EPS = 1e-08
ADAM_WD = 0.01
ADAM_STEP = 10
VMEM_LIMIT_V7X = 48 * 1024 * 1024
MXU_WIDTH_V7X = 256
LANES = 128
MESH_ID = pl.DeviceIdType.MESH
ANY = pl.BlockSpec(memory_space=pl.ANY)
HBM = pl.BlockSpec(memory_space=pltpu.HBM)
SEM = pl.BlockSpec(memory_space=pltpu.SEMAPHORE)
DATAFLOW = pltpu.SideEffectType.DATAFLOW_SIDE_EFFECTING


def _params(n_axes):
    return pltpu.CompilerParams(dimension_semantics=("arbitrary",) * n_axes, vmem_limit_bytes=VMEM_LIMIT_V7X)


def _dot(a, b):
    return jnp.dot(a, b, preferred_element_type=F32)


def _dot_nt(a, b):
    return lax.dot_general(a, b, (((1,), (1,)), ((), ())), preferred_element_type=F32)


def _dot_tn(a, b):
    return lax.dot_general(a, b, (((0,), (0,)), ((), ())), preferred_element_type=F32)


def _sigmoid(x):
    return 1.0 / (1.0 + jnp.exp(-x))


def _silu_and_grad(a):
    sig = _sigmoid(a)
    return a * sig, sig * (1.0 + a * (1.0 - sig))


_GELU_C = 0.7978845608028654


def _gelu_and_grad(x):
    inner = _GELU_C * (x + 0.044715 * x * x * x)
    t = jnp.tanh(inner)
    val = 0.5 * x * (1.0 + t)
    grad = 0.5 * (1.0 + t) + 0.5 * x * (1.0 - t * t) * _GELU_C * (1.0 + 3.0 * 0.044715 * x * x)
    return val, grad


def _ln_fwd(y, g, b):
    mu = jnp.mean(y, axis=-1, keepdims=True)
    yc = y - mu
    var = jnp.mean(yc * yc, axis=-1, keepdims=True)
    rstd = lax.rsqrt(var + LN_EPS)
    xhat = yc * rstd
    return xhat * g + b, xhat, rstd


def _ln_bwd(dh, xhat, rstd, g):
    dxh = dh * g
    m1 = jnp.mean(dxh, axis=-1, keepdims=True)
    m2 = jnp.mean(dxh * xhat, axis=-1, keepdims=True)
    dy = rstd * (dxh - m1 - xhat * m2)
    dg = jnp.sum(dh * xhat, axis=0, keepdims=True)
    db = jnp.sum(dh, axis=0, keepdims=True)
    return dy, dg, db


def _mask_dot(mask, x):
    hi = x.astype(BF16)
    lo = (x - hi.astype(F32)).astype(BF16)
    return _dot(mask, hi) + _dot(mask, lo)


def _block_masks(n):
    r = np.arange(n)[:, None]
    c = np.arange(n)[None, :]
    same = (r // HG_BLOCK) == (c // HG_BLOCK)
    return jnp.asarray(np.stack([same & (c <= r), same & (c >= r), same]), BF16)


def _row_tile(t):
    return min(t, 512)


def _col_tile(n):
    for cand in (512, 256, 128):
        if n % cand == 0:
            return cand
    return n


def _resident(w):
    return pl.BlockSpec(w.shape, lambda *_: (0, 0), pipeline_mode=pl.Buffered(1))


def _drop_deps(body, n_in, n_deps):
    if n_deps == 0:
        return body
    return lambda *refs: body(*refs[:n_in], *refs[n_in + n_deps:])


def _ffn_up(hb, wg, wu, name, deps=()):
    t, d = hb.shape
    f = wg.shape[1]
    tm = min(t, 256)
    tn = _col_tile(f)

    def body(h_ref, wg_ref, wu_ref, a_ref, b_ref, s_ref):
        h = h_ref[...]
        for c in range(f // tn):
            cols = slice(c * tn, (c + 1) * tn)
            a = _dot(h, wg_ref[:, cols])
            b = _dot(h, wu_ref[:, cols])
            a_ref[:, cols] = a.astype(BF16)
            b_ref[:, cols] = b.astype(BF16)
            s_ref[:, cols] = (a * _sigmoid(a) * b).astype(BF16)

    act = pl.BlockSpec((tm, f), lambda i: (i, 0))
    return pl.pallas_call(
        _drop_deps(body, 3, len(deps)),
        grid=(t // tm,),
        in_specs=[pl.BlockSpec((tm, d), lambda i: (i, 0)), _resident(wg), _resident(wu)] + [ANY] * len(deps),
        out_specs=[act, act, act],
        out_shape=[jax.ShapeDtypeStruct((t, f), BF16)] * 3,
        compiler_params=_params(1),
        name=name,
    )(hb, wg, wu, *deps)


def _mm_res_ln(lhs, w, res, g, b, coef, name, target=None):
    t, kd = lhs.shape
    d = w.shape[1]
    tm = _row_tile(t)
    nt = t // tm
    from_norm = isinstance(res, tuple)
    n_res = 3 if from_norm else 1

    def body(*refs):
        l_ref, w_ref = refs[:2]
        r_refs = refs[2:2 + n_res]
        g_ref, b_ref = refs[2 + n_res:4 + n_res]
        rest = refs[4 + n_res:]
        prev = r_refs[0][...] * r_refs[1][...] + r_refs[2][...] if from_norm else r_refs[0][...]
        y = ALPHA * prev + coef * _dot(l_ref[...], w_ref[...])
        h, xhat, rstd = _ln_fwd(y, g_ref[...], b_ref[...])
        if target is None:
            hb_ref, xh_ref, rs_ref = rest
            hb_ref[...] = h.astype(BF16)
            xh_ref[...] = xhat
            rs_ref[...] = rstd
            return
        t_ref, loss_ref, dy_ref, dyb_ref, dg_ref, db_ref, lacc = rest
        i = pl.program_id(0)

        @pl.when(i == 0)
        def _():
            lacc[...] = jnp.zeros_like(lacc)
            dg_ref[...] = jnp.zeros_like(dg_ref)
            db_ref[...] = jnp.zeros_like(db_ref)

        err = h - t_ref[...]
        lacc[...] += jnp.sum(err * err, axis=0, keepdims=True)
        dy, dg, db = _ln_bwd(err * (1.0 / d), xhat, rstd, g_ref[...])
        dy_ref[...] = dy
        dyb_ref[...] = dy.astype(BF16)
        dg_ref[...] += dg
        db_ref[...] += db

        @pl.when(i == nt - 1)
        def _():
            loss_ref[...] = jnp.zeros_like(loss_ref) + jnp.sum(lacc[...], axis=1, keepdims=True) * (0.5 / d)

    row = pl.BlockSpec((tm, d), lambda i: (i, 0))
    vec = pl.BlockSpec((1, d), lambda i: (0, 0))
    res_specs = [row, vec, vec] if from_norm else [row]
    res_args = list(res) if from_norm else [res]
    in_specs = [pl.BlockSpec((tm, kd), lambda i: (i, 0)), _resident(w)] + res_specs + [vec, vec]
    args = [lhs, w] + res_args + [g, b]
    if target is None:
        out_specs = [row, row, pl.BlockSpec((tm, 1), lambda i: (i, 0))]
        out_shape = [jax.ShapeDtypeStruct((t, d), BF16), jax.ShapeDtypeStruct((t, d), F32),
                     jax.ShapeDtypeStruct((t, 1), F32)]
        scratch = []
    else:
        in_specs.append(row)
        args.append(target)
        out_specs = [pl.BlockSpec((1, LANES), lambda i: (0, 0)), row, row, vec, vec]
        out_shape = [jax.ShapeDtypeStruct((1, LANES), F32), jax.ShapeDtypeStruct((t, d), F32),
                     jax.ShapeDtypeStruct((t, d), BF16), jax.ShapeDtypeStruct((1, d), F32),
                     jax.ShapeDtypeStruct((1, d), F32)]
        scratch = [pltpu.VMEM((1, d), F32)]
    return pl.pallas_call(
        body,
        grid=(nt,),
        in_specs=in_specs,
        out_specs=out_specs,
        out_shape=out_shape,
        scratch_shapes=scratch,
        compiler_params=_params(1),
        name=name,
    )(*args)


def _mm_nn(lhs, w, name):
    t, kd = lhs.shape
    n = w.shape[1]
    tm = _row_tile(t)
    tn = _col_tile(n)

    def body(l_ref, w_ref, o_ref):
        lhs_v = l_ref[...]
        for c in range(n // tn):
            cols = slice(c * tn, (c + 1) * tn)
            o_ref[:, cols] = _dot(lhs_v, w_ref[:, cols])

    return pl.pallas_call(
        body,
        grid=(t // tm,),
        in_specs=[pl.BlockSpec((tm, kd), lambda i: (i, 0)), _resident(w)],
        out_specs=pl.BlockSpec((tm, n), lambda i: (i, 0)),
        out_shape=jax.ShapeDtypeStruct((t, n), F32),
        compiler_params=_params(1),
        name=name,
    )(lhs, w)


def _lower_bound(lg):
    m = jnp.max(lg, axis=0, keepdims=True)
    e = jnp.exp(lg - m)
    return e[0:1, :] / jnp.sum(e, axis=0, keepdims=True)


def _forget_terms(fz, lb):
    e = jnp.exp(-jnp.abs(fz))
    r = 1.0 / (1.0 + e)
    pos = fz >= 0.0
    sig = jnp.where(pos, r, e * r)
    nsig = jnp.where(pos, e * r, r)
    f = lb + (1.0 - lb) * sig
    k = (1.0 - lb) * nsig
    return sig, nsig, f, k


def _hg_tile(t):
    return min(t, 256)


HG_HALF = HG_BLOCK // 2
NEG_BIG = -1e30


def _halves(a):
    return a[:HG_HALF, :], a[HG_HALF:, :]


def _causal_halves(s):
    return (0, 1) if s < HG_HALF else (1,)


def _decay_from(b_half, b_s, s, h, tidx):
    first = s - h * HG_HALF
    diff = b_half - b_s
    if first > 0:
        diff = jnp.where(tidx >= first, diff, NEG_BIG)
    return jnp.exp(diff)


def _hgrn_fwd(proj, logits, gn):
    t = proj.shape[0]
    ct = _hg_tile(t)
    nct = t // ct
    nblk = ct // HG_BLOCK
    nh = HG_HEADS

    def body(q_ref, fz_ref, iv_ref, gg_ref, lg_ref, gn_ref, mask_ref, oraw_ref, oa_ref, st_ref,
             state, qt_s, kt_s, k_s, b_s, dec_s):
        c = pl.program_id(1)

        @pl.when(c == 0)
        def _():
            state[...] = jnp.zeros_like(state)

        lb = _lower_bound(lg_ref[...])
        q = q_ref[...]
        _, _, f, k = _forget_terms(fz_ref[...], lb)
        logf = jnp.log(f)
        b = _mask_dot(mask_ref[0], logf)
        bend = _mask_dot(mask_ref[2], logf)
        qt_s[...] = (q * jnp.exp(b)).astype(BF16)
        kt_s[...] = (k * jnp.exp(bend - b)).astype(BF16)
        k_s[...] = k
        b_s[...] = b
        dec_s[...] = jnp.exp(bend)
        tidx = lax.broadcasted_iota(jnp.int32, (HG_HALF, HG_DIM), 0)

        def blk(i, carry):
            r0 = pl.multiple_of(i * HG_BLOCK, HG_BLOCK)
            rows = pl.ds(r0, HG_BLOCK)
            st = state[...]
            st_ref[i] = st
            v = iv_ref[rows, :]
            qq = q_ref[rows, :]
            kk = k_s[rows, :]
            bb = b_s[rows, :]
            o = list(_halves(_dot_nt(qt_s[rows, :], st.astype(BF16))))
            qh, bh = _halves(qq), _halves(bb)
            for s in range(HG_BLOCK):
                ks, vs = kk[s:s + 1, :], v[s:s + 1, :]
                for h in _causal_halves(s):
                    e = _decay_from(bh[h], bb[s:s + 1, :], s, h, tidx)
                    acol = jnp.sum(qh[h] * (ks * e), axis=1, keepdims=True)
                    o[h] = o[h] + acol * vs
            oraw_ref[rows, :] = jnp.concatenate(o, axis=0)
            state[...] = st * dec_s[pl.ds(r0, 1), :] + _dot_tn(v.astype(BF16), kt_s[rows, :])
            return carry

        lax.fori_loop(0, nblk, blk, 0, unroll=2 * HG_UNROLL)
        oraw = oraw_ref[...]
        r = lax.rsqrt(jnp.mean(oraw * oraw, axis=-1, keepdims=True) + LN_EPS)
        gg = gg_ref[...]
        oa_ref[...] = (oraw * r * gn_ref[...] * gg * _sigmoid(gg)).astype(BF16)

    def slab(off):
        return pl.BlockSpec((ct, HG_DIM), lambda h, c: (c, off + h))

    out_slab = pl.BlockSpec((ct, HG_DIM), lambda h, c: (c, h))
    return pl.pallas_call(
        body,
        grid=(nh, nct),
        in_specs=[slab(0), slab(nh), slab(2 * nh), slab(3 * nh),
                  pl.BlockSpec((None, 2, HG_DIM), lambda h, c: (h, 0, 0)),
                  pl.BlockSpec((1, HG_DIM), lambda h, c: (0, 0)),
                  pl.BlockSpec((3, ct, ct), lambda h, c: (0, 0, 0))],
        out_specs=[out_slab, out_slab, pl.BlockSpec((None, nblk, HG_DIM, HG_DIM), lambda h, c: (h, c, 0, 0))],
        out_shape=[jax.ShapeDtypeStruct((t, nh * HG_DIM), F32),
                   jax.ShapeDtypeStruct((t, (nh + SG_GROUPS) * HG_DIM), BF16),
                   jax.ShapeDtypeStruct((nh, t // HG_BLOCK, HG_DIM, HG_DIM), F32)],
        scratch_shapes=[pltpu.VMEM((HG_DIM, HG_DIM), F32), pltpu.VMEM((ct, HG_DIM), BF16),
                        pltpu.VMEM((ct, HG_DIM), BF16), pltpu.VMEM((ct, HG_DIM), F32),
                        pltpu.VMEM((ct, HG_DIM), F32), pltpu.VMEM((ct, HG_DIM), F32)],
        compiler_params=_params(2),
        name="hgrn_fwd",
    )(proj, proj, proj, proj, logits, gn, _block_masks(ct))


def _sg_tile(t):
    return min(t, 512)


def _sgu_chunk_fwd(u, v, ln_g, ln_b, wm, bs):
    ua, dua = _gelu_and_grad(u)
    va, dva = _gelu_and_grad(v)
    vn, xhat, rstd = _ln_fwd(va, ln_g, ln_b)
    s = _dot(wm, vn.astype(BF16)) + bs
    return ua, dua, dva, vn, xhat, rstd, s


def _tril_weight(w_ref):
    n = SG_CHUNK
    r = lax.broadcasted_iota(jnp.int32, (n, n), 0)
    c = lax.broadcasted_iota(jnp.int32, (n, n), 1)
    return jnp.where(c <= r, w_ref[...], 0.0)


def _sgu_fwd(proj, mix, ln_g, ln_b, w_s, b_col):
    t = proj.shape[0]
    ct = _sg_tile(t)
    ng = SG_GROUPS
    off_u = 4 * HG_HEADS
    off_v = off_u + ng

    def body(u_ref, v_ref, g_ref, b_ref, w_ref, bs_ref, mix_ref, o_ref):
        del mix_ref
        wm = _tril_weight(w_ref).astype(BF16)
        for n in range(ct // SG_CHUNK):
            rows = slice(n * SG_CHUNK, (n + 1) * SG_CHUNK)
            ua, _, _, _, _, _, s = _sgu_chunk_fwd(u_ref[rows, :], v_ref[rows, :], g_ref[...], b_ref[...], wm, bs_ref[...])
            o_ref[rows, :] = (ua * s).astype(BF16)

    vec = pl.BlockSpec((None, 1, SG_DIM), lambda g, c: (g, 0, 0))
    return pl.pallas_call(
        body,
        grid=(ng, t // ct),
        in_specs=[pl.BlockSpec((ct, SG_DIM), lambda g, c: (c, off_u + g)),
                  pl.BlockSpec((ct, SG_DIM), lambda g, c: (c, off_v + g)), vec, vec,
                  pl.BlockSpec((None, SG_CHUNK, SG_CHUNK), lambda g, c: (g, 0, 0)),
                  pl.BlockSpec((None, SG_CHUNK, 1), lambda g, c: (g, 0, 0)), ANY],
        out_specs=pl.BlockSpec((ct, SG_DIM), lambda g, c: (c, HG_HEADS + g)),
        out_shape=jax.ShapeDtypeStruct(mix.shape, mix.dtype),
        input_output_aliases={6: 0},
        compiler_params=_params(2),
        name="sgu_fwd",
    )(proj, proj, ln_g, ln_b, w_s, b_col, mix)


def _mem_kv(mem, g, b, wk, wv):
    m_len, d = mem.shape

    def body(m_ref, g_ref, b_ref, wk_ref, wv_ref, mb_ref, xh_ref, rs_ref, k_ref, v_ref):
        m, xhat, rstd = _ln_fwd(m_ref[...], g_ref[...], b_ref[...])
        mb = m.astype(BF16)
        mb_ref[...] = mb
        xh_ref[...] = xhat
        rs_ref[...] = rstd
        k_ref[...] = _dot(mb, wk_ref[...]).astype(BF16)
        v_ref[...] = _dot(mb, wv_ref[...]).astype(BF16)

    return pl.pallas_call(
        body,
        out_shape=[jax.ShapeDtypeStruct((m_len, d), BF16), jax.ShapeDtypeStruct((m_len, d), F32),
                   jax.ShapeDtypeStruct((m_len, 1), F32), jax.ShapeDtypeStruct((m_len, d), BF16),
                   jax.ShapeDtypeStruct((m_len, d), BF16)],
        compiler_params=pltpu.CompilerParams(vmem_limit_bytes=VMEM_LIMIT_V7X),
        name="mem_kv",
    )(mem, g, b, wk, wv)


def _softmax_rows(s):
    m = jnp.max(s, axis=-1, keepdims=True)
    p = jnp.exp(s - m)
    return p / jnp.sum(p, axis=-1, keepdims=True)


def _attn_fwd(hb, wq, kb, vb):
    t, d = hb.shape
    tm = _row_tile(t)
    dh = d // X_HEADS
    scale = dh ** -0.5

    def body(h_ref, wq_ref, k_ref, v_ref, q_ref, o_ref):
        q = _dot(h_ref[...], wq_ref[...]).astype(BF16)
        q_ref[...] = q
        for hd in range(X_HEADS):
            sl = slice(hd * dh, (hd + 1) * dh)
            p = _softmax_rows(_dot_nt(q[:, sl], k_ref[:, sl]) * scale)
            o_ref[:, sl] = _dot(p.astype(BF16), v_ref[:, sl]).astype(BF16)

    row = pl.BlockSpec((tm, d), lambda i: (i, 0))
    full = lambda a: pl.BlockSpec(a.shape, lambda i: (0, 0))
    return pl.pallas_call(
        body,
        grid=(t // tm,),
        in_specs=[row, full(wq), full(kb), full(vb)],
        out_specs=[row, row],
        out_shape=[jax.ShapeDtypeStruct((t, d), BF16), jax.ShapeDtypeStruct((t, d), BF16)],
        compiler_params=_params(1),
        name="attn_fwd",
    )(hb, wq, kb, vb)


def _ffn_bwd_act(dyb, wd, a, b, coef, name, deps=()):
    t, d = dyb.shape
    f = wd.shape[0]
    tm = _row_tile(t)
    tn = _col_tile(f)

    def body(dy_ref, wd_ref, a_ref, b_ref, da_ref, db_ref):
        dy = dy_ref[...]
        for c in range(f // tn):
            cols = slice(c * tn, (c + 1) * tn)
            ds = _dot_nt(dy, wd_ref[cols, :]) * coef
            silu, dsilu = _silu_and_grad(a_ref[:, cols].astype(F32))
            da_ref[:, cols] = (ds * b_ref[:, cols].astype(F32) * dsilu).astype(BF16)
            db_ref[:, cols] = (ds * silu).astype(BF16)

    act = pl.BlockSpec((tm, f), lambda i: (i, 0))
    return pl.pallas_call(
        _drop_deps(body, 4, len(deps)),
        grid=(t // tm,),
        in_specs=[pl.BlockSpec((tm, d), lambda i: (i, 0)), _resident(wd), act, act] + [ANY] * len(deps),
        out_specs=[act, act],
        out_shape=[jax.ShapeDtypeStruct((t, f), BF16), jax.ShapeDtypeStruct((t, f), BF16)],
        compiler_params=_params(1),
        name=name,
    )(dyb, wd, a, b, *deps)


def _mm_tn(a, b, name, scale=1.0, deps=()):
    t, m = a.shape
    n = b.shape[1]
    tt = _row_tile(t)
    nt = t // tt
    tm_o = m // 2 if (m > n and m * n > 2 ** 21) else m
    tn_o = n // 2 if (n > m and m * n > 2 ** 21) else n

    def body(a_ref, b_ref, o_ref, acc):
        k = pl.program_id(2)

        @pl.when(k == 0)
        def _():
            acc[...] = jnp.zeros_like(acc)

        acc[...] += _dot_tn(a_ref[...], b_ref[...])

        @pl.when(k == nt - 1)
        def _():
            o_ref[...] = (acc[...] * scale).astype(BF16)

    return pl.pallas_call(
        _drop_deps(body, 2, len(deps)),
        grid=(m // tm_o, n // tn_o, nt),
        in_specs=[pl.BlockSpec((tt, tm_o), lambda i, j, k: (k, i)), pl.BlockSpec((tt, tn_o), lambda i, j, k: (k, j))]
        + [ANY] * len(deps),
        out_specs=pl.BlockSpec((tm_o, tn_o), lambda i, j, k: (i, j)),
        out_shape=jax.ShapeDtypeStruct((m, n), BF16),
        scratch_shapes=[pltpu.VMEM((tm_o, tn_o), F32)],
        compiler_params=_params(3),
        name=name,
    )(a, b, *deps)


def _mm_nt(lhs, w, name):
    t, d = lhs.shape
    kd = w.shape[0]
    tm = _row_tile(t)

    def body(l_ref, w_ref, o_ref):
        o_ref[...] = _dot_nt(l_ref[...], w_ref[...])

    return pl.pallas_call(
        body,
        grid=(t // tm,),
        in_specs=[pl.BlockSpec((tm, d), lambda i: (i, 0)), _resident(w)],
        out_specs=pl.BlockSpec((tm, kd), lambda i: (i, 0)),
        out_shape=jax.ShapeDtypeStruct((t, kd), F32),
        compiler_params=_params(1),
        name=name,
    )(lhs, w)


def _dx_ln(dy, pairs, ln, name, deps=()):
    t, d = dy.shape
    npair = len(pairs)
    tm = min(t, 512 // npair)
    nt = t // tm
    n_in = 1 + 2 * npair + (3 if ln is not None else 0)

    def body(*refs):
        dy_ref = refs[0]
        pr = refs[1:1 + 2 * npair]
        pos = 1 + 2 * npair
        dh = ALPHA * dy_ref[...]
        for p in range(npair):
            dh = dh + _dot_nt(pr[2 * p][...], pr[2 * p + 1][...])
        if ln is not None:
            xh_ref, rs_ref, g_ref = refs[pos:pos + 3]
            dyo_ref, dyb_ref, dg_ref, db_ref = refs[pos + 3:pos + 7]

            @pl.when(pl.program_id(0) == 0)
            def _():
                dg_ref[...] = jnp.zeros_like(dg_ref)
                db_ref[...] = jnp.zeros_like(db_ref)

            dyp, dg, db = _ln_bwd(dh, xh_ref[...], rs_ref[...], g_ref[...])
            dyo_ref[...] = dyp
            dyb_ref[...] = dyp.astype(BF16)
            dg_ref[...] += dg
            db_ref[...] += db
        else:
            refs[pos][...] = dh

    row = pl.BlockSpec((tm, d), lambda i: (i, 0))
    vec = pl.BlockSpec((1, d), lambda i: (0, 0))
    in_specs = [row]
    args = [dy]
    for lhs, w in pairs:
        in_specs += [pl.BlockSpec((tm, lhs.shape[1]), lambda i: (i, 0)), _resident(w)]
        args += [lhs, w]
    if ln is not None:
        in_specs += [row, pl.BlockSpec((tm, 1), lambda i: (i, 0)), vec]
        args += list(ln)
        out_specs = [row, row, vec, vec]
        out_shape = [jax.ShapeDtypeStruct((t, d), F32), jax.ShapeDtypeStruct((t, d), BF16),
                     jax.ShapeDtypeStruct((1, d), F32), jax.ShapeDtypeStruct((1, d), F32)]
    else:
        out_specs = row
        out_shape = jax.ShapeDtypeStruct((t, d), F32)
    return pl.pallas_call(
        _drop_deps(body, n_in, len(deps)),
        grid=(nt,),
        in_specs=in_specs + [ANY] * len(deps),
        out_specs=out_specs,
        out_shape=out_shape,
        compiler_params=_params(1),
        name=name,
    )(*args, *deps)


def _hgrn_bwd(proj, oraw, dmix, states, logits, gn):
    t = proj.shape[0]
    ct = _hg_tile(t)
    nct = t // ct
    nblk = ct // HG_BLOCK
    nh = HG_HEADS

    def body(q_ref, fz_ref, iv_ref, gg_ref, or_ref, do_ref, st_ref, lg_ref, gn_ref, mask_ref,
             dq_ref, dfz_ref, div_ref, dgg_ref, dlg_ref, dgn_ref,
             dstate, qt_s, kt_s, k_s, b_s, eb_s, ekb_s, dec_s, dor_s, dbl_s, gr_s, dk_s, dlb_acc):
        c = pl.program_id(1)

        @pl.when(c == 0)
        def _():
            dstate[...] = jnp.zeros_like(dstate)
            dlb_acc[...] = jnp.zeros_like(dlb_acc)
            dgn_ref[...] = jnp.zeros_like(dgn_ref)

        lb = _lower_bound(lg_ref[...])
        q = q_ref[...]
        sig, nsig, f, k = _forget_terms(fz_ref[...], lb)
        logf = jnp.log(f)
        b = _mask_dot(mask_ref[0], logf)
        bend = _mask_dot(mask_ref[2], logf)
        eb = jnp.exp(b)
        ekb = jnp.exp(bend - b)
        qt_s[...] = (q * eb).astype(BF16)
        kt_s[...] = (k * ekb).astype(BF16)
        k_s[...] = k
        b_s[...] = b
        eb_s[...] = eb
        ekb_s[...] = ekb
        dec_s[...] = jnp.exp(bend)
        oraw = or_ref[...]
        r = lax.rsqrt(jnp.mean(oraw * oraw, axis=-1, keepdims=True) + LN_EPS)
        on = oraw * r
        gg = gg_ref[...]
        silu, dsilu = _silu_and_grad(gg)
        doa = do_ref[...]
        gnv = gn_ref[...]
        dgg_ref[...] = (doa * on * gnv * dsilu).astype(BF16)
        dyn = doa * silu
        dgn_ref[...] += jnp.sum(dyn * on, axis=0, keepdims=True)
        don = dyn * gnv
        dor_s[...] = r * (don - on * jnp.mean(don * on, axis=-1, keepdims=True))
        tidx = lax.broadcasted_iota(jnp.int32, (HG_HALF, HG_DIM), 0)

        def blk(ii, carry):
            i = nblk - 1 - ii
            r0 = pl.multiple_of(i * HG_BLOCK, HG_BLOCK)
            rows = pl.ds(r0, HG_BLOCK)
            st = st_ref[i]
            dst = dstate[...]
            dstb = dst.astype(BF16)
            do = dor_s[rows, :]
            dob = do.astype(BF16)
            v = iv_ref[rows, :]
            vb = v.astype(BF16)
            qq = q_ref[rows, :]
            kk = k_s[rows, :]
            bb = b_s[rows, :]
            qt = qt_s[rows, :]
            kt = kt_s[rows, :]
            dec = dec_s[pl.ds(r0, 1), :]
            dkt = _dot(vb, dstb)
            dq = _dot(dob, st.astype(BF16)) * eb_s[rows, :]
            dk = dkt * ekb_s[rows, :]
            dv = _dot_nt(kt, dstb)
            gend = jnp.sum(kk * dk, axis=0, keepdims=True) + dec * jnp.sum(dst * st, axis=0, keepdims=True)
            qh, bh, doh = _halves(qq), _halves(bb), _halves(do)
            dqh, dkh, dvh = list(_halves(dq)), list(_halves(dk)), list(_halves(dv))
            for s in range(HG_BLOCK):
                ks, vs = kk[s:s + 1, :], v[s:s + 1, :]
                dk_part = dv_part = None
                for h in _causal_halves(s):
                    e = _decay_from(bh[h], bb[s:s + 1, :], s, h, tidx)
                    ke = ks * e
                    acol = jnp.sum(qh[h] * ke, axis=1, keepdims=True)
                    dacol = jnp.sum(doh[h] * vs, axis=1, keepdims=True)
                    dqh[h] = dqh[h] + dacol * ke
                    pk = dacol * (qh[h] * e)
                    pv = acol * doh[h]
                    dk_part = pk if dk_part is None else dk_part + pk
                    dv_part = pv if dv_part is None else dv_part + pv
                hs, row = divmod(s, HG_HALF)
                dkh[hs] = dkh[hs] + jnp.where(tidx == row, jnp.sum(dk_part, axis=0, keepdims=True), 0.0)
                dvh[hs] = dvh[hs] + jnp.where(tidx == row, jnp.sum(dv_part, axis=0, keepdims=True), 0.0)
            dq = jnp.concatenate(dqh, axis=0)
            dk = jnp.concatenate(dkh, axis=0)
            dv = jnp.concatenate(dvh, axis=0)
            dq_ref[rows, :] = dq.astype(BF16)
            div_ref[rows, :] = dv.astype(BF16)
            dk_s[rows, :] = dk
            dbl_s[rows, :] = qq * dq - kk * dk
            gr_s[rows, :] = jnp.zeros((HG_BLOCK, HG_DIM), F32) + gend
            dstate[...] = dst * dec + _dot_tn(dob, qt)
            return carry

        lax.fori_loop(0, nblk, blk, 0, unroll=HG_UNROLL)
        dlogf = _mask_dot(mask_ref[1], dbl_s[...]) + gr_s[...]
        dk = dk_s[...]
        dfz_ref[...] = ((dlogf / f - dk) * ((1.0 - lb) * sig * nsig)).astype(BF16)
        dlb_acc[...] += jnp.sum((dlogf / f - dk) * nsig, axis=0, keepdims=True)

        @pl.when(c == nct - 1)
        def _():
            dl0 = dlb_acc[...] * lb * (1.0 - lb)
            layer = lax.broadcasted_iota(jnp.int32, (2, HG_DIM), 0)
            dlg_ref[...] = jnp.where(layer == 0, dl0, -dl0)

    def slab(off):
        return pl.BlockSpec((ct, HG_DIM), lambda h, c: (nct - 1 - c, off + h))

    out_slab = pl.BlockSpec((ct, HG_DIM), lambda h, c: (nct - 1 - c, h))
    tile_f32 = pltpu.VMEM((ct, HG_DIM), F32)
    tile_b16 = pltpu.VMEM((ct, HG_DIM), BF16)
    slab_shape = jax.ShapeDtypeStruct((t, nh * HG_DIM), BF16)
    return pl.pallas_call(
        body,
        grid=(nh, nct),
        in_specs=[slab(0), slab(nh), slab(2 * nh), slab(3 * nh), slab(0), slab(0),
                  pl.BlockSpec((None, nblk, HG_DIM, HG_DIM), lambda h, c: (h, nct - 1 - c, 0, 0)),
                  pl.BlockSpec((None, 2, HG_DIM), lambda h, c: (h, 0, 0)),
                  pl.BlockSpec((1, HG_DIM), lambda h, c: (0, 0)),
                  pl.BlockSpec((3, ct, ct), lambda h, c: (0, 0, 0))],
        out_specs=[out_slab, out_slab, out_slab, out_slab,
                   pl.BlockSpec((None, 2, HG_DIM), lambda h, c: (h, 0, 0)),
                   pl.BlockSpec((None, 1, HG_DIM), lambda h, c: (h, 0, 0))],
        out_shape=[slab_shape, slab_shape, slab_shape, slab_shape,
                   jax.ShapeDtypeStruct((nh, 2, HG_DIM), F32), jax.ShapeDtypeStruct((nh, 1, HG_DIM), F32)],
        scratch_shapes=[pltpu.VMEM((HG_DIM, HG_DIM), F32), tile_b16, tile_b16, tile_f32, tile_f32, tile_f32, tile_f32,
                        tile_f32, tile_f32, tile_f32, tile_f32, tile_f32, pltpu.VMEM((1, HG_DIM), F32)],
        compiler_params=_params(2),
        name="hgrn_bwd",
    )(proj, proj, proj, proj, oraw, dmix, states, logits, gn, _block_masks(ct))


def _sgu_bwd(proj, dmix, ln_g, ln_b, w_s, w_t, b_col):
    t = proj.shape[0]
    ct = _sg_tile(t)
    nct = t // ct
    ng = SG_GROUPS
    off_u = 4 * HG_HEADS
    off_v = off_u + ng
    n = SG_CHUNK

    def body(u_ref, v_ref, do_ref, g_ref, b_ref, w_ref, wt_ref, bs_ref, du_ref, dv_ref, dg_ref, db_ref, dw_ref, dbs_ref):
        c = pl.program_id(1)

        @pl.when(c == 0)
        def _():
            dg_ref[...] = jnp.zeros_like(dg_ref)
            db_ref[...] = jnp.zeros_like(db_ref)
            dw_ref[...] = jnp.zeros_like(dw_ref)
            dbs_ref[...] = jnp.zeros_like(dbs_ref)

        r = lax.broadcasted_iota(jnp.int32, (n, n), 0)
        cc = lax.broadcasted_iota(jnp.int32, (n, n), 1)
        wm = jnp.where(cc <= r, w_ref[...], 0.0).astype(BF16)
        wmt = jnp.where(r <= cc, wt_ref[...], 0.0).astype(BF16)
        for ci in range(ct // n):
            rows = slice(ci * n, (ci + 1) * n)
            ua, dua, dva, vn, xhat, rstd, s = _sgu_chunk_fwd(u_ref[rows, :], v_ref[rows, :], g_ref[...], b_ref[...],
                                                             wm, bs_ref[...])
            do = do_ref[rows, :]
            du_ref[rows, :] = (do * s * dua).astype(BF16)
            ds = do * ua
            dsb = ds.astype(BF16)
            dbs_ref[...] += jnp.sum(ds, axis=1, keepdims=True)
            dw_ref[...] += _dot_nt(dsb, vn.astype(BF16))
            dvn = _dot(wmt, dsb)
            dva_in, dg, db = _ln_bwd(dvn, xhat, rstd, g_ref[...])
            dg_ref[...] += dg
            db_ref[...] += db
            dv_ref[rows, :] = (dva_in * dva).astype(BF16)

        @pl.when(c == nct - 1)
        def _():
            dw_ref[...] = jnp.where(cc <= r, dw_ref[...], 0.0)

    vec = pl.BlockSpec((None, 1, SG_DIM), lambda g, c: (g, 0, 0))
    mat = pl.BlockSpec((None, n, n), lambda g, c: (g, 0, 0))
    col = pl.BlockSpec((None, n, 1), lambda g, c: (g, 0, 0))
    out_slab = pl.BlockSpec((ct, SG_DIM), lambda g, c: (c, g))
    return pl.pallas_call(
        body,
        grid=(ng, nct),
        in_specs=[pl.BlockSpec((ct, SG_DIM), lambda g, c: (c, off_u + g)),
                  pl.BlockSpec((ct, SG_DIM), lambda g, c: (c, off_v + g)),
                  pl.BlockSpec((ct, SG_DIM), lambda g, c: (c, ng + g)), vec, vec, mat, mat, col],
        out_specs=[out_slab, out_slab, vec, vec, mat, col],
        out_shape=[jax.ShapeDtypeStruct((t, ng * SG_DIM), BF16), jax.ShapeDtypeStruct((t, ng * SG_DIM), BF16),
                   jax.ShapeDtypeStruct((ng, 1, SG_DIM), F32), jax.ShapeDtypeStruct((ng, 1, SG_DIM), F32),
                   jax.ShapeDtypeStruct((ng, n, n), F32), jax.ShapeDtypeStruct((ng, n, 1), F32)],
        compiler_params=_params(2),
        name="sgu_bwd",
    )(proj, proj, dmix, ln_g, ln_b, w_s, w_t, b_col)


def _attn_bwd(dyb, wo, qb, kb, vb):
    t, d = dyb.shape
    m_len = kb.shape[0]
    tm = _row_tile(t)
    dh = d // X_HEADS
    scale = dh ** -0.5

    def body(dy_ref, wo_ref, q_ref, k_ref, v_ref, dq_ref, dk_ref, dv_ref):
        i = pl.program_id(0)

        @pl.when(i == 0)
        def _():
            dk_ref[...] = jnp.zeros_like(dk_ref)
            dv_ref[...] = jnp.zeros_like(dv_ref)

        do = _dot_nt(dy_ref[...], wo_ref[...]).astype(BF16)
        for hd in range(X_HEADS):
            sl = slice(hd * dh, (hd + 1) * dh)
            qh = q_ref[:, sl]
            p = _softmax_rows(_dot_nt(qh, k_ref[:, sl]) * scale)
            doh = do[:, sl]
            dp = _dot_nt(doh, v_ref[:, sl])
            ds = (p * (dp - jnp.sum(dp * p, axis=-1, keepdims=True)) * scale).astype(BF16)
            dq_ref[:, sl] = _dot(ds, k_ref[:, sl]).astype(BF16)
            dk_ref[:, sl] += _dot_tn(ds, qh)
            dv_ref[:, sl] += _dot_tn(p.astype(BF16), doh)

    row = pl.BlockSpec((tm, d), lambda i: (i, 0))
    full = lambda a: pl.BlockSpec(a.shape, lambda i: (0, 0))
    kv = pl.BlockSpec((m_len, d), lambda i: (0, 0))
    return pl.pallas_call(
        body,
        grid=(t // tm,),
        in_specs=[row, full(wo), row, full(kb), full(vb)],
        out_specs=[row, kv, kv],
        out_shape=[jax.ShapeDtypeStruct((t, d), BF16), jax.ShapeDtypeStruct((m_len, d), F32),
                   jax.ShapeDtypeStruct((m_len, d), F32)],
        compiler_params=_params(1),
        name="attn_bwd",
    )(dyb, wo, qb, kb, vb)


def _mem_bwd(dk, dv, mb, xhat, rstd, g, wk, wv):
    m_len, d = dk.shape

    def body(dk_ref, dv_ref, mb_ref, xh_ref, rs_ref, g_ref, wk_ref, wv_ref, gwk_ref, gwv_ref, dg_ref, db_ref):
        dkb = dk_ref[...].astype(BF16)
        dvb = dv_ref[...].astype(BF16)
        mb_v = mb_ref[...]
        gwk_ref[...] = _dot_tn(mb_v, dkb).astype(BF16)
        gwv_ref[...] = _dot_tn(mb_v, dvb).astype(BF16)
        dm = _dot_nt(dkb, wk_ref[...]) + _dot_nt(dvb, wv_ref[...])
        _, dg, db = _ln_bwd(dm, xh_ref[...], rs_ref[...], g_ref[...])
        dg_ref[...] = dg
        db_ref[...] = db

    return pl.pallas_call(
        body,
        out_shape=[jax.ShapeDtypeStruct((d, d), BF16), jax.ShapeDtypeStruct((d, d), BF16),
                   jax.ShapeDtypeStruct((1, d), F32), jax.ShapeDtypeStruct((1, d), F32)],
        compiler_params=pltpu.CompilerParams(vmem_limit_bytes=VMEM_LIMIT_V7X),
        name="mem_bwd",
    )(dk, dv, mb, xhat, rstd, g, wk, wv)


def _adamw(w, g, m, v):
    m = ADAM_B1 * m + (1.0 - ADAM_B1) * g
    v = ADAM_B2 * v + (1.0 - ADAM_B2) * (g * g)
    m_hat = m / (1.0 - ADAM_B1 ** ADAM_STEP)
    v_hat = v / (1.0 - ADAM_B2 ** ADAM_STEP)
    delta = -ADAM_LR * (m_hat / (jnp.sqrt(v_hat) + ADAM_EPS) + ADAM_WD * w)
    return delta, m, v


def _slot_sum(ref):
    g = ref[0].astype(F32)
    for s in range(1, N_DEV):
        g = g + ref[s].astype(F32)
    return g


def _adam_sharded(lands, w, m, v, axis, name):
    rows, cols = w.shape
    nl = len(lands)
    transposed = axis == 1 and nl == 2
    if transposed:
        rows, cols = cols, rows
        tr = 256
        grid = (rows // tr,)
        wblk = pl.BlockSpec((cols, tr), lambda i: (0, i))
        lblk = [pl.BlockSpec((N_DEV, tr, a.shape[2]), lambda i: (0, i, 0)) for a in lands]
    elif axis == 1:
        tr = 256 if rows % 256 == 0 else rows
        grid = (rows // tr,)
        wblk = pl.BlockSpec((tr, cols), lambda i: (i, 0))
        lblk = [pl.BlockSpec((N_DEV, tr, a.shape[2]), lambda i: (0, i, 0)) for a in lands]
    else:
        tc = _col_tile(cols)
        grid = (cols // tc,)
        wblk = pl.BlockSpec((rows, tc), lambda i: (0, i))
        lblk = [pl.BlockSpec((N_DEV, a.shape[1], tc), lambda i: (0, 0, i)) for a in lands]

    def body(*refs):
        w_ref, m_ref, v_ref = refs[nl:nl + 3]
        g_ref, d_ref, nm_ref, nv_ref = refs[nl + 3:]
        g = _slot_sum(refs[0])
        if nl == 2:
            tail = _slot_sum(refs[1])
            if transposed:
                g = jnp.concatenate([g.T, tail.T[:cols - g.shape[1], :]], axis=0)
            elif axis == 1:
                g = jnp.concatenate([g, tail[:, :cols - g.shape[1]]], axis=1)
            else:
                g = jnp.concatenate([g, tail[:rows - g.shape[0], :]], axis=0)
        delta, nm, nv = _adamw(w_ref[...], g, m_ref[...], v_ref[...])
        g_ref[...] = g
        d_ref[...] = delta
        nm_ref[...] = nm
        nv_ref[...] = nv

    shp = jax.ShapeDtypeStruct(w.shape, F32)
    return pl.pallas_call(
        body,
        grid=grid,
        in_specs=lblk + [wblk, wblk, wblk],
        out_specs=[wblk, wblk, wblk, wblk],
        out_shape=[shp, shp, shp, shp],
        compiler_params=_params(1),
        name=name,
    )(*lands, w, m, v)


def _mesh_pos():
    return lax.axis_index("x"), lax.axis_index("y"), lax.axis_index("c")


def _peer(k):
    x, y, c = _mesh_pos()
    pos = (x ^ (k >> 2), y ^ ((k >> 1) & 1), c ^ (k & 1))
    return pos, 4 * pos[0] + 2 * pos[1] + pos[2]


def _sem_index(row, k):
    return row * (N_DEV - 1) + k - 1


def _window(ref, axis, start, size):
    align = 16 if axis == 0 else LANES
    start = pl.multiple_of(start, align)
    return ref.at[pl.ds(start, size), :] if axis == 0 else ref.at[:, pl.ds(start, size)]


def _piece_refs(piece, srcs, lands, me, peer):
    kind, si, li, axis, base, stride, shape = piece
    if kind == "gather":
        return srcs[si], _window(lands[li], axis, base + stride * me, shape[axis])
    return _window(srcs[si], axis, base + stride * peer, shape[axis]), lands[li].at[me]


def _place_own(srcs, land_shapes, pieces, name):
    ns, nl, npc = len(srcs), len(land_shapes), len(pieces)

    def body(*refs):
        s_refs = refs[:ns]
        l_refs = refs[ns:ns + nl]
        bufs = refs[ns + nl:ns + nl + npc]
        sems = refs[ns + nl + npc]
        x, y, c = _mesh_pos()
        me = 4 * x + 2 * y + c
        loads = []
        for p, piece in enumerate(pieces):
            src, dst = _piece_refs(piece, s_refs, l_refs, me, me)
            cp = pltpu.make_async_copy(src, bufs[p], sems.at[0, p])
            cp.start()
            loads.append((cp, dst))
        stores = []
        for p, (cp, dst) in enumerate(loads):
            cp.wait()
            out = pltpu.make_async_copy(bufs[p], dst, sems.at[1, p])
            out.start()
            stores.append(out)
        for out in stores:
            out.wait()

    out = pl.pallas_call(
        body,
        in_specs=[ANY] * ns,
        out_specs=[ANY] * nl,
        out_shape=list(land_shapes),
        scratch_shapes=[pltpu.VMEM(pc[6], srcs[pc[1]].dtype) for pc in pieces] + [pltpu.SemaphoreType.DMA((2, npc))],
        compiler_params=pltpu.CompilerParams(vmem_limit_bytes=VMEM_LIMIT_V7X),
        name=name,
    )(*srcs)
    return list(out)


def _comm_start(srcs, lands, pieces, groups, name, after=()):
    ns, nl, na, ng = len(srcs), len(lands), len(after), len(groups)

    def body(*refs):
        s_refs = refs[:ns]
        l_refs = refs[ns:ns + nl]
        outs = refs[ns + nl + na:]
        sems = outs[:2 * ng]
        token = outs[-1]
        x, y, c = _mesh_pos()
        me = 4 * x + 2 * y + c
        for g, members in enumerate(groups):
            for row, p in enumerate(members):
                for k in range(1, N_DEV):
                    pos, peer = _peer(k)
                    src, dst = _piece_refs(pieces[p], s_refs, l_refs, me, peer)
                    pltpu.make_async_remote_copy(src_ref=src, dst_ref=dst, send_sem=sems[2 * g].at[_sem_index(row, k)],
                                                 recv_sem=sems[2 * g + 1].at[_sem_index(row, k)], device_id=pos,
                                                 device_id_type=MESH_ID).start()
        token[...] = jnp.zeros_like(token)

    sem_shapes = []
    for members in groups:
        sem_shapes += [pltpu.SemaphoreType.DMA((len(members) * (N_DEV - 1),))] * 2
    hbm_of = lambda a: pltpu.HBM(a.shape, a.dtype)
    out = pl.pallas_call(
        body,
        in_specs=[HBM] * (ns + nl) + [ANY] * na,
        out_specs=[SEM] * (2 * ng) + [HBM] * (ns + nl) + [pl.BlockSpec(memory_space=pltpu.VMEM)],
        out_shape=sem_shapes + [hbm_of(a) for a in srcs] + [hbm_of(a) for a in lands]
        + [jax.ShapeDtypeStruct((8, LANES), F32)],
        input_output_aliases={i: 2 * ng + i for i in range(ns + nl)},
        compiler_params=pltpu.CompilerParams(has_side_effects=DATAFLOW),
        name=name,
    )(*[pltpu.with_memory_space_constraint(a, pltpu.HBM) for a in list(srcs) + list(lands)], *after)
    sems = [(out[2 * g], out[2 * g + 1]) for g in range(ng)]
    return sems, list(out[2 * ng:2 * ng + ns]), list(out[2 * ng + ns:2 * ng + ns + nl]), out[-1]


def _comm_wait(srcs, lands, pieces, members, sems, after, name):
    ns, nl, na = len(srcs), len(lands), len(after)

    def body(*refs):
        s_refs = refs[:ns]
        l_refs = refs[ns:ns + nl]
        send_sems, recv_sems = refs[ns + nl:ns + nl + 2]
        x, y, c = _mesh_pos()
        me = 4 * x + 2 * y + c
        for row, p in enumerate(members):
            for k in range(1, N_DEV):
                pos, peer = _peer(k)
                src, dst = _piece_refs(pieces[p], s_refs, l_refs, me, peer)
                cp = pltpu.make_async_remote_copy(src_ref=src, dst_ref=dst, send_sem=send_sems.at[_sem_index(row, k)],
                                                  recv_sem=recv_sems.at[_sem_index(row, k)], device_id=pos,
                                                  device_id_type=MESH_ID)
                cp.wait_send()
                cp.wait_recv()

    hbm_of = lambda a: pltpu.HBM(a.shape, a.dtype)
    out = pl.pallas_call(
        body,
        in_specs=[HBM] * (ns + nl) + [SEM, SEM] + [ANY] * na,
        out_specs=[HBM] * (ns + nl),
        out_shape=[hbm_of(a) for a in srcs] + [hbm_of(a) for a in lands],
        input_output_aliases={i: i for i in range(ns + nl)},
        compiler_params=pltpu.CompilerParams(has_side_effects=DATAFLOW),
        name=name,
    )(*srcs, *lands, sems[0], sems[1], *after)
    return list(out[ns:])


_CHIP_FLIPS = (2, 4, 6)
_SIBLING = 1


def _gather2_first(srcs, lands, pieces, name):
    ns, nl, npc = len(srcs), len(lands), len(pieces)

    def body(*refs):
        s_refs, l_refs = refs[:ns], refs[ns:ns + nl]
        send_sems, recv_sib, recv_ici = refs[ns + nl:ns + nl + 3]
        token = refs[-1]
        x, y, c = _mesh_pos()
        me = 4 * x + 2 * y + c
        for p, piece in enumerate(pieces):
            for j, k in enumerate((_SIBLING,) + _CHIP_FLIPS):
                pos, peer = _peer(k)
                src, dst = _piece_refs(piece, s_refs, l_refs, me, peer)
                recv = recv_sib.at[p] if j == 0 else recv_ici.at[3 * p + j - 1]
                pltpu.make_async_remote_copy(src_ref=src, dst_ref=dst, send_sem=send_sems.at[4 * p + j], recv_sem=recv,
                                             device_id=pos, device_id_type=MESH_ID).start()
        token[...] = jnp.zeros_like(token)

    hbm_of = lambda a: pltpu.HBM(a.shape, a.dtype)
    dma = lambda n: pltpu.SemaphoreType.DMA((n,))
    out = pl.pallas_call(
        body,
        in_specs=[HBM] * (ns + nl),
        out_specs=[SEM] * 3 + [HBM] * (ns + nl) + [pl.BlockSpec(memory_space=pltpu.VMEM)],
        out_shape=[dma(4 * npc), dma(npc), dma(3 * npc)] + [hbm_of(a) for a in srcs] + [hbm_of(a) for a in lands]
        + [jax.ShapeDtypeStruct((8, LANES), F32)],
        input_output_aliases={i: 3 + i for i in range(ns + nl)},
        compiler_params=pltpu.CompilerParams(has_side_effects=DATAFLOW),
        name=name,
    )(*[pltpu.with_memory_space_constraint(a, pltpu.HBM) for a in list(srcs) + list(lands)])
    return out[0], out[1], out[2], list(out[3:3 + ns]), list(out[3 + ns:3 + ns + nl]), out[-1]


def _landed_block(piece, lands, owner):
    _, _, li, axis, base, stride, shape = piece
    return _window(lands[li], axis, base + stride * owner, shape[axis])


def _gather2_pass(lands, pieces, recv_ici, after, name):
    nl, npc, na = len(lands), len(pieces), len(after)

    def body(*refs):
        l_refs = refs[:nl]
        recv_sems = refs[nl]
        send_fwd, recv_fwd = refs[nl + 1 + na:nl + 3 + na]
        sib, _ = _peer(_SIBLING)
        for p, piece in enumerate(pieces):
            for j, k in enumerate(_CHIP_FLIPS):
                pos, owner = _peer(k)
                block = _landed_block(piece, l_refs, owner)
                pltpu.make_async_remote_copy(src_ref=block, dst_ref=block, send_sem=send_fwd.at[3 * p + j],
                                             recv_sem=recv_sems.at[3 * p + j], device_id=pos,
                                             device_id_type=MESH_ID).wait_recv()
                pltpu.make_async_remote_copy(src_ref=block, dst_ref=block, send_sem=send_fwd.at[3 * p + j],
                                             recv_sem=recv_fwd.at[3 * p + j], device_id=sib,
                                             device_id_type=MESH_ID).start()

    hbm_of = lambda a: pltpu.HBM(a.shape, a.dtype)
    dma = lambda n: pltpu.SemaphoreType.DMA((n,))
    out = pl.pallas_call(
        body,
        in_specs=[HBM] * nl + [SEM] + [ANY] * na,
        out_specs=[SEM, SEM] + [HBM] * nl,
        out_shape=[dma(3 * npc), dma(3 * npc)] + [hbm_of(a) for a in lands],
        input_output_aliases={i: 2 + i for i in range(nl)},
        compiler_params=pltpu.CompilerParams(has_side_effects=DATAFLOW),
        name=name,
    )(*lands, recv_ici, *after)
    return out[0], out[1], list(out[2:])


def _gather2_last(srcs, lands, pieces, send_sems, recv_sib, send_fwd, recv_fwd, name):
    ns, nl = len(srcs), len(lands)

    def body(*refs):
        s_refs, l_refs = refs[:ns], refs[ns:ns + nl]
        send_a, recv_s, send_f, recv_f = refs[ns + nl:ns + nl + 4]
        x, y, c = _mesh_pos()
        me = 4 * x + 2 * y + c
        sib, sib_index = _peer(_SIBLING)
        for p, piece in enumerate(pieces):
            src, dst = _piece_refs(piece, s_refs, l_refs, me, sib_index)
            own = lambda s_sem, r_sem: pltpu.make_async_remote_copy(
                src_ref=src, dst_ref=dst, send_sem=s_sem, recv_sem=r_sem, device_id=sib, device_id_type=MESH_ID)
            own(send_a.at[4 * p], recv_s.at[p]).wait_recv()
            for j in range(4):
                own(send_a.at[4 * p + j], recv_s.at[p]).wait_send()
            for j in range(3):
                fwd = own(send_f.at[3 * p + j], recv_f.at[3 * p + j])
                fwd.wait_recv()
                fwd.wait_send()

    hbm_of = lambda a: pltpu.HBM(a.shape, a.dtype)
    out = pl.pallas_call(
        body,
        in_specs=[HBM] * (ns + nl) + [SEM] * 4,
        out_specs=[HBM] * (ns + nl),
        out_shape=[hbm_of(a) for a in srcs] + [hbm_of(a) for a in lands],
        input_output_aliases={i: i for i in range(ns + nl)},
        compiler_params=pltpu.CompilerParams(has_side_effects=DATAFLOW),
        name=name,
    )(*srcs, *lands, send_sems, recv_sib, send_fwd, recv_fwd)
    return list(out[ns:])


_SMALL_NAMES = ("ln1_g", "ln1_b", "hg_lb_logits", "hg_norm_g", "sg_ln_g", "sg_ln_b", "sg_w_s", "sg_b_s",
                "ln2_g", "ln2_b", "mem_ln_g", "mem_ln_b", "ln3_g", "ln3_b", "ln4_g", "ln4_b")


_VEC_NAMES = ("ln1_g", "ln1_b", "ln2_g", "ln2_b", "mem_ln_g", "mem_ln_b", "ln3_g", "ln3_b", "ln4_g", "ln4_b")
_ROW_NAMES = ("hg_lb_logits", "hg_norm_g", "sg_ln_g", "sg_ln_b", "sg_b_s", "sg_w_s")
VEC_ROWS = 16


def _row_plan(shapes):
    plan, pos = {}, 0
    for k in _ROW_NAMES:
        shp = shapes[k]
        slabs, off = [], pos
        for idx in itertools.product(*[range(dim) for dim in shp[:-2]]):
            slabs.append((idx, off, shp[-2]))
            off += shp[-2]
        plan[k] = (pos, slabs)
        pos = -(-off // 8) * 8
    return plan, -(-pos // 16) * 16


def _pack_small_grads(gs, shapes):
    vec = jnp.concatenate([gs[k].reshape(1, -1) for k in _VEC_NAMES], axis=0)
    vec = jnp.pad(vec, ((0, VEC_ROWS - vec.shape[0]), (0, 0)))
    plan, total = _row_plan(shapes)
    parts, pos = [], 0
    for k in _ROW_NAMES:
        first, slabs = plan[k]
        rows = gs[k].reshape(-1, LANES)
        end = slabs[-1][1] + slabs[-1][2]
        nxt = -(-end // 8) * 8
        parts.append(jnp.pad(rows, ((0, nxt - first - rows.shape[0]), (0, 0))))
        pos = nxt
    parts.append(jnp.zeros((total - pos, LANES), F32))
    return vec, jnp.concatenate(parts, axis=0)


def _adam_small(land_vec, land_rows, w, m, v):
    names = _VEC_NAMES + _ROW_NAMES
    n = len(names)
    shapes = {k: w[k].shape for k in names}
    plan, _ = _row_plan(shapes)

    def body(*refs):
        lv_ref, lr_ref = refs[:2]
        w_refs, m_refs, v_refs = refs[2:2 + n], refs[2 + n:2 + 2 * n], refs[2 + 2 * n:2 + 3 * n]
        outs = refs[2 + 3 * n:2 + 7 * n]
        gv_s, gr_s = refs[2 + 7 * n:]
        gv_s[...] = _slot_sum(lv_ref)
        gr_s[...] = _slot_sum(lr_ref)
        for p, k in enumerate(names):
            if k in _VEC_NAMES:
                row = _VEC_NAMES.index(k)
                slabs = [((), None, None)]
            else:
                slabs = plan[k][1]
            for idx, off, rows in slabs:
                g = gv_s[row:row + 1, :] if off is None else gr_s[off:off + rows, :]
                sel = idx + (slice(None), slice(None))
                delta, nm, nv = _adamw(w_refs[p][sel], g, m_refs[p][sel], v_refs[p][sel])
                for o, val in zip(range(4), (g, delta, nm, nv)):
                    outs[o * n + p][sel] = val

    flat = lambda tree: [tree[k] for k in names]
    shp = [jax.ShapeDtypeStruct(shapes[k], F32) for k in names]
    out = pl.pallas_call(
        body,
        out_shape=shp * 4,
        scratch_shapes=[pltpu.VMEM(land_vec.shape[1:], F32), pltpu.VMEM(land_rows.shape[1:], F32)],
        name="adam_small",
    )(land_vec, land_rows, *flat(w), *flat(m), *flat(v))
    return [dict(zip(names, out[o * n:(o + 1) * n])) for o in range(4)]


_COL_FFN = ("ffn1_w_gate", "ffn1_w_up", "ffn2_w_gate", "ffn2_w_up")
_ROW_FFN = ("ffn1_w_down", "ffn2_w_down")
_ROW_SQ = ("w_out", "xa_w_q", "xa_w_k", "xa_w_v", "xa_w_o")
_BIG_NAMES = ("ffn1_w_gate", "ffn1_w_up", "ffn1_w_down", "w_in", "w_out", "xa_w_q", "xa_w_k", "xa_w_v", "xa_w_o",
              "ffn2_w_gate", "ffn2_w_up", "ffn2_w_down")


def _ffn_split(fs):
    main = (fs // MXU_WIDTH_V7X) * MXU_WIDTH_V7X
    tail = fs - main
    tail_pad = -(-tail // LANES) * LANES
    assert main > 0 and tail > 0
    return main, tail, tail_pad


def _layout(name, shard_shape):
    r, c = shard_shape
    if name in _COL_FFN:
        main, tail, pad = _ffn_split(c)
        return (r, N_DEV * (main + pad)), [(1, 0, main, (r, main), (0, main)),
                                           (1, N_DEV * main, pad, (r, pad), (main, c))]
    if name in _ROW_FFN:
        main, tail, pad = _ffn_split(r)
        return (N_DEV * (main + pad), c), [(0, 0, main, (main, c), (0, main)),
                                           (0, N_DEV * main, pad, (pad, c), (main, r))]
    if name == "w_in":
        return (r, N_DEV * c), [(1, 0, c, (r, c), (0, c))]
    return (N_DEV * r, c), [(0, 0, r, (r, c), (0, r))]


def _shard_pieces(name, shard):
    out = []
    for axis, _, _, shape, (lo, hi) in _layout(name, shard.shape)[1]:
        part = shard[lo:hi, :] if axis == 0 else shard[:, lo:hi]
        pad = [(0, shape[0] - part.shape[0]), (0, shape[1] - part.shape[1])]
        out.append(jnp.pad(part, pad).astype(BF16))
    return out


def _gather_plan(names, shards):
    srcs, land_shapes, pieces, index = [], [], [], {}
    for li, name in enumerate(names):
        shape2d, parts = _layout(name, shards[name].shape)
        land_shapes.append(jax.ShapeDtypeStruct(shape2d, BF16))
        index[name] = []
        for (axis, base, stride, shape, _), src in zip(parts, _shard_pieces(name, shards[name])):
            index[name].append(len(pieces))
            pieces.append(("gather", len(srcs), li, axis, base, stride, shape))
            srcs.append(src)
    return srcs, land_shapes, pieces, index


def _scatter_plan(names, grads, shard_shapes):
    srcs, land_shapes, pieces, index = [], [], [], {}
    for si, name in enumerate(names):
        _, parts = _layout(name, shard_shapes[name])
        srcs.append(grads[name])
        index[name] = []
        for axis, base, stride, shape, _ in parts:
            index[name].append(len(land_shapes))
            pieces.append(("scatter", si, len(land_shapes), axis, base, stride, shape))
            land_shapes.append(jax.ShapeDtypeStruct((N_DEV,) + shape, grads[name].dtype))
    return srcs, land_shapes, pieces, index


def _small_views(small):
    row = lambda a: a.reshape(1, -1)
    ln = {k: row(small[k]) for k in ("ln1_g", "ln1_b", "ln2_g", "ln2_b", "ln3_g", "ln3_b", "ln4_g", "ln4_b",
                                      "mem_ln_g", "mem_ln_b", "hg_norm_g")}
    sg_w = small["sg_w_s"].reshape(SG_GROUPS, SG_CHUNK, SG_CHUNK)
    sg = dict(logits=jnp.swapaxes(small["hg_lb_logits"], 0, 1),
              g=small["sg_ln_g"].reshape(SG_GROUPS, 1, SG_DIM), b=small["sg_ln_b"].reshape(SG_GROUPS, 1, SG_DIM),
              w=sg_w, wt=jnp.swapaxes(sg_w, 1, 2), bs=small["sg_b_s"].reshape(SG_GROUPS, SG_CHUNK, 1))
    return ln, sg


def _forward(x, mem, target, get_w, small, first_deps=()):
    ln, sg = _small_views(small)
    xb = x.astype(BF16)
    a1, b1, s1 = _ffn_up(xb, get_w("ffn1_w_gate", ()), get_w("ffn1_w_up", ()), "ffn1_up", deps=first_deps)
    h1b, xh1, rs1 = _mm_res_ln(s1, get_w("ffn1_w_down", (s1,)), x, ln["ln1_g"], ln["ln1_b"], 0.5, "ffn1_down_ln")
    proj = _mm_nn(h1b, get_w("w_in", (h1b,)), "mix_in")
    oraw, mix, states = _hgrn_fwd(proj, sg["logits"], ln["hg_norm_g"])
    mix = _sgu_fwd(proj, mix, sg["g"], sg["b"], sg["w"], sg["bs"])
    h2b, xh2, rs2 = _mm_res_ln(mix, get_w("w_out", (mix,)), (xh1, ln["ln1_g"], ln["ln1_b"]), ln["ln2_g"], ln["ln2_b"],
                               1.0, "mix_out_ln")
    mb, mxh, mrs, kb, vb = _mem_kv(mem, ln["mem_ln_g"], ln["mem_ln_b"], get_w("xa_w_k", (h2b,)), get_w("xa_w_v", (h2b,)))
    qb, att = _attn_fwd(h2b, get_w("xa_w_q", (kb,)), kb, vb)
    h3b, xh3, rs3 = _mm_res_ln(att, get_w("xa_w_o", (att,)), (xh2, ln["ln2_g"], ln["ln2_b"]), ln["ln3_g"], ln["ln3_b"],
                               1.0, "attn_out_ln")
    a2, b2, s2 = _ffn_up(h3b, get_w("ffn2_w_gate", (h3b,)), get_w("ffn2_w_up", (h3b,)), "ffn2_up")
    loss, dy4, dy4b, dg4, db4 = _mm_res_ln(s2, get_w("ffn2_w_down", (s2,)), (xh3, ln["ln3_g"], ln["ln3_b"]),
                                           ln["ln4_g"], ln["ln4_b"], 0.5, "ffn2_down_ln_loss", target=target)
    return dict(xb=xb, a1=a1, b1=b1, s1=s1, h1b=h1b, xh1=xh1, rs1=rs1, proj=proj, oraw=oraw, mix=mix, states=states,
                h2b=h2b, xh2=xh2, rs2=rs2, mb=mb, mxh=mxh, mrs=mrs, kb=kb, vb=vb, qb=qb, att=att, h3b=h3b, xh3=xh3,
                rs3=rs3, a2=a2, b2=b2, s2=s2, loss=loss, dy4=dy4, dy4b=dy4b, dg4=dg4, db4=db4)


def _backward(sv, wt, small, send):
    ln, sg = _small_views(small)
    gs = {"ln4_g": sv["dg4"], "ln4_b": sv["db4"]}
    loss, dy4, dy4b = sv["loss"], sv["dy4"], sv["dy4b"]
    g_down2 = _mm_tn(sv["s2"], dy4b, "g_ffn2_down", scale=0.5)
    da2, db2 = _ffn_bwd_act(dy4b, wt["ffn2_w_down"], sv["a2"], sv["b2"], 0.5, "ffn2_bwd_act")
    g_gate2 = _mm_tn(sv["h3b"], da2, "g_ffn2_gate")
    g_up2 = _mm_tn(sv["h3b"], db2, "g_ffn2_up")
    tok = send(("ffn2_w_down", "ffn2_w_gate", "ffn2_w_up"), (g_down2, g_gate2, g_up2))
    dy3, dy3b, gs["ln3_g"], gs["ln3_b"] = _dx_ln(dy4, [(da2, wt["ffn2_w_gate"]), (db2, wt["ffn2_w_up"])],
                                                 (sv["xh3"], sv["rs3"], ln["ln3_g"]), "ffn2_dx_ln", deps=(tok,))

    g_o = _mm_tn(sv["att"], dy3b, "g_xa_o")
    dqb, dk, dv = _attn_bwd(dy3b, wt["xa_w_o"], sv["qb"], sv["kb"], sv["vb"])
    g_q = _mm_tn(sv["h2b"], dqb, "g_xa_q")
    g_k, g_v, gs["mem_ln_g"], gs["mem_ln_b"] = _mem_bwd(dk, dv, sv["mb"], sv["mxh"], sv["mrs"], ln["mem_ln_g"],
                                                        wt["xa_w_k"], wt["xa_w_v"])
    tok = send(("xa_w_o", "xa_w_q", "xa_w_k", "xa_w_v"), (g_o, g_q, g_k, g_v))
    dy2, dy2b, gs["ln2_g"], gs["ln2_b"] = _dx_ln(dy3, [(dqb, wt["xa_w_q"])], (sv["xh2"], sv["rs2"], ln["ln2_g"]),
                                                 "attn_dx_ln", deps=(tok,))

    g_out = _mm_tn(sv["mix"], dy2b, "g_w_out")
    dmix = _mm_nt(dy2b, wt["w_out"], "mix_out_bwd")
    dq, dfz, div, dgg, dlg, dgn = _hgrn_bwd(sv["proj"], sv["oraw"], dmix, sv["states"], sg["logits"], ln["hg_norm_g"])
    du, dvv, gs["sg_ln_g"], gs["sg_ln_b"], gs["sg_w_s"], gs["sg_b_s"] = _sgu_bwd(
        sv["proj"], dmix, sg["g"], sg["b"], sg["w"], sg["wt"], sg["bs"])
    gs["hg_lb_logits"] = jnp.swapaxes(dlg, 0, 1)
    gs["hg_norm_g"] = jnp.sum(dgn, axis=0)
    dproj = jnp.concatenate([dq, dfz, div, dgg, du, dvv], axis=1)
    g_in = _mm_tn(sv["h1b"], dproj, "g_w_in")
    tok = send(("w_out", "w_in"), (g_out, g_in))
    dy1, dy1b, gs["ln1_g"], gs["ln1_b"] = _dx_ln(dy2, [(dproj, wt["w_in"])], (sv["xh1"], sv["rs1"], ln["ln1_g"]),
                                                 "mix_dx_ln", deps=(tok,))

    g_down1 = _mm_tn(sv["s1"], dy1b, "g_ffn1_down", scale=0.5)
    tok = send(("ffn1_w_down",), (g_down1,))
    da1, db1 = _ffn_bwd_act(dy1b, wt["ffn1_w_down"], sv["a1"], sv["b1"], 0.5, "ffn1_bwd_act", deps=(tok,))
    g_gate1 = _mm_tn(sv["xb"], da1, "g_ffn1_gate")
    tok = send(("ffn1_w_gate",), (g_gate1,))
    g_up1 = _mm_tn(sv["xb"], db1, "g_ffn1_up", deps=(tok,))
    tok = send(("ffn1_w_up",), (g_up1,))
    grad_x = _dx_ln(dy1, [(da1, wt["ffn1_w_gate"]), (db1, wt["ffn1_w_up"])], None, "ffn1_dx", deps=(tok,))
    return loss, grad_x, gs


_WEIGHT_NAMES = ("ffn1_w_gate", "ffn1_w_up", "ffn1_w_down", "ln1_g", "ln1_b", "w_in", "hg_lb_logits", "hg_norm_g",
                 "sg_ln_g", "sg_ln_b", "sg_w_s", "sg_b_s", "w_out", "ln2_g", "ln2_b", "mem_ln_g", "mem_ln_b",
                 "xa_w_q", "xa_w_k", "xa_w_v", "xa_w_o", "ln3_g", "ln3_b", "ffn2_w_gate", "ffn2_w_up", "ffn2_w_down",
                 "ln4_g", "ln4_b")
_FIRST = ("ffn1_w_gate", "ffn1_w_up")
_SECOND = ("ffn1_w_down", "w_in", "w_out")
_THIRD = ("xa_w_k", "xa_w_v", "xa_w_q", "xa_w_o", "ffn2_w_gate", "ffn2_w_up", "ffn2_w_down")


def kernel(x, mem, ffn1_w_gate, ffn1_w_up, ffn1_w_down, ln1_g, ln1_b, w_in, hg_lb_logits, hg_norm_g, sg_ln_g, sg_ln_b, sg_w_s, sg_b_s, w_out, ln2_g, ln2_b, mem_ln_g, mem_ln_b, xa_w_q, xa_w_k, xa_w_v, xa_w_o, ln3_g, ln3_b, ffn2_w_gate, ffn2_w_up, ffn2_w_down, ln4_g, ln4_b, loss_target, m_ffn1_w_gate, m_ffn1_w_up, m_ffn1_w_down, m_ln1_g, m_ln1_b, m_w_in, m_hg_lb_logits, m_hg_norm_g, m_sg_ln_g, m_sg_ln_b, m_sg_w_s, m_sg_b_s, m_w_out, m_ln2_g, m_ln2_b, m_mem_ln_g, m_mem_ln_b, m_xa_w_q, m_xa_w_k, m_xa_w_v, m_xa_w_o, m_ln3_g, m_ln3_b, m_ffn2_w_gate, m_ffn2_w_up, m_ffn2_w_down, m_ln4_g, m_ln4_b, v_ffn1_w_gate, v_ffn1_w_up, v_ffn1_w_down, v_ln1_g, v_ln1_b, v_w_in, v_hg_lb_logits, v_hg_norm_g, v_sg_ln_g, v_sg_ln_b, v_sg_w_s, v_sg_b_s, v_w_out, v_ln2_g, v_ln2_b, v_mem_ln_g, v_mem_ln_b, v_xa_w_q, v_xa_w_k, v_xa_w_v, v_xa_w_o, v_ln3_g, v_ln3_b, v_ffn2_w_gate, v_ffn2_w_up, v_ffn2_w_down, v_ln4_g, v_ln4_b):
    args = dict(locals())
    w = {k: args[k] for k in _WEIGHT_NAMES}
    m = {k: args["m_" + k] for k in _WEIGHT_NAMES}
    v = {k: args["v_" + k] for k in _WEIGHT_NAMES}
    shards = {k: w[k][0] for k in _BIG_NAMES}
    shard_shapes = {k: shards[k].shape for k in _BIG_NAMES}
    small = {k: (w[k][0] if k != "hg_lb_logits" else w[k]) for k in _SMALL_NAMES}

    srcs1, shapes1, pieces1, idx1 = _gather_plan(_FIRST, shards)
    lands1 = _place_own(srcs1, shapes1, pieces1, "gather_first_own")
    send1, rsib1, rici1, srcs1, lands1, _ = _gather2_first(srcs1, lands1, pieces1, "gather_first_start")
    sfwd1, rfwd1, lands1 = _gather2_pass(lands1, pieces1, rici1, (), "gather_first_pass")
    lands1 = _gather2_last(srcs1, lands1, pieces1, send1, rsib1, sfwd1, rfwd1, "gather_first_wait")
    wt = dict(zip(_FIRST, lands1))

    rest = _SECOND + _THIRD
    srcs2, shapes2, pieces2, idx2 = _gather_plan(rest, shards)
    lands2 = _place_own(srcs2, shapes2, pieces2, "gather_rest_own")
    groups2 = [list(idx2[k]) for k in rest]
    sems2, srcs2, lands2, tok2 = _comm_start(srcs2, lands2, pieces2, groups2, "gather_rest_start", after=tuple(lands1))
    pending = {k: gi for gi, k in enumerate(rest)}

    def get_w(name, after):
        if name in pending:
            gi = pending.pop(name)
            si = [pieces2[p][1] for p in groups2[gi]]
            sub = [(pieces2[p][0], row, 0) + pieces2[p][3:] for row, p in enumerate(groups2[gi])]
            wt[name] = _comm_wait([srcs2[s] for s in si], [lands2[gi]], sub, list(range(len(sub))), sems2[gi],
                                  after, "gather_wait_" + name)[0]
        return wt[name]

    sv = _forward(x[0], mem[0], loss_target[0], get_w, small, first_deps=(tok2,))

    sent = []

    def send(names, grads):
        srcs, shapes, pieces, idx = _scatter_plan(names, dict(zip(names, grads)), shard_shapes)
        lands = _place_own(srcs, shapes, pieces, "grads_own_%d" % len(sent))
        sems, srcs, lands, tok = _comm_start(srcs, lands, pieces, [list(range(len(pieces)))],
                                             "grads_start_%d" % len(sent))
        sent.append((names, srcs, lands, pieces, idx, sems[0]))
        return tok

    loss, grad_x, gs = _backward(sv, wt, small, send)

    ssrc = list(_pack_small_grads(gs, {k: w[k].shape for k in _SMALL_NAMES}))
    sp = [("scatter", i, i, 0, 0, 0, a.shape) for i, a in enumerate(ssrc)]
    sshape = [jax.ShapeDtypeStruct((N_DEV,) + a.shape, F32) for a in ssrc]
    sl = _place_own(ssrc, sshape, sp, "small_own")
    ssem, ssrc, sl, _ = _comm_start(ssrc, sl, sp, [[0, 1]], "small_start")

    out_g, out_d, out_m, out_v = {}, {}, {}, {}
    after = (grad_x,)
    for n_sent, (names, srcs, lands, pieces, idx, sems) in enumerate(sent):
        lands = _comm_wait(srcs, lands, pieces, list(range(len(pieces))), sems, after, "grads_wait_%d" % n_sent)
        for k in names:
            axis = 1 if (k in _COL_FFN or k == "w_in") else 0
            if k in _COL_FFN:
                res = _adam_sharded([lands[i] for i in idx[k]], w[k][0].T, m[k][0].T, v[k][0].T, axis, "adam_" + k)
                res = [r.T for r in res]
            else:
                res = _adam_sharded([lands[i] for i in idx[k]], w[k][0], m[k][0], v[k][0], axis, "adam_" + k)
            out_g[k], out_d[k], out_m[k], out_v[k] = [r[None] for r in res]
        after = (out_v[names[-1]],)
    sl = _comm_wait(ssrc, sl, sp, [0, 1], ssem[0], after, "small_wait")
    for dst, res in zip((out_g, out_d, out_m, out_v), _adam_small(sl[0], sl[1], w, m, v)):
        dst.update(res)

    loss_all = lax.psum(loss[0, 0], ("x", "y", "c"))
    return (loss_all, grad_x[None], *[out_g[k] for k in _WEIGHT_NAMES], *[out_d[k] for k in _WEIGHT_NAMES],
            *[out_m[k] for k in _WEIGHT_NAMES], *[out_v[k] for k in _WEIGHT_NAMES])
```

```python
import itertools

import jax
import jax.numpy as jnp
import numpy as np
from jax import lax
from jax.experimental import pallas as pl
from jax.experimental.pallas import tpu as pltpu

F32 = jnp.float32
BF16 = jnp.bfloat16

N_DEV = 8
ALPHA = 2.0 ** 0.25
LN_EPS = 1e-5
HG_HEADS = 4
HG_DIM = 128
SG_GROUPS = 4
SG_DIM = 128
SG_CHUNK = 128
X_HEADS = 4
HG_BLOCK = 16
HG_UNROLL = 8
ADAM_LR = 0.001
ADAM_B1 = 0.9
ADAM_B2 = 0.999
ADAM_EPS = 1e-08
ADAM_WD = 0.01
ADAM_STEP = 10
VMEM_LIMIT_V7X = 48 * 1024 * 1024
MXU_WIDTH_V7X = 256
LANES = 128
MESH_ID = pl.DeviceIdType.MESH
ANY = pl.BlockSpec(memory_space=pl.ANY)
HBM = pl.BlockSpec(memory_space=pltpu.HBM)
SEM = pl.BlockSpec(memory_space=pltpu.SEMAPHORE)
DATAFLOW = pltpu.SideEffectType.DATAFLOW_SIDE_EFFECTING


def _params(n_axes):
    return pltpu.CompilerParams(dimension_semantics=("arbitrary",) * n_axes, vmem_limit_bytes=VMEM_LIMIT_V7X)


def _dot(a, b):
    return jnp.dot(a, b, preferred_element_type=F32)


def _dot_nt(a, b):
    return lax.dot_general(a, b, (((1,), (1,)), ((), ())), preferred_element_type=F32)


def _dot_tn(a, b):
    return lax.dot_general(a, b, (((0,), (0,)), ((), ())), preferred_element_type=F32)


def _sigmoid(x):
    return 1.0 / (1.0 + jnp.exp(-x))


def _silu_and_grad(a):
    sig = _sigmoid(a)
    return a * sig, sig * (1.0 + a * (1.0 - sig))


_GELU_C = 0.7978845608028654


def _gelu_and_grad(x):
    inner = _GELU_C * (x + 0.044715 * x * x * x)
    t = jnp.tanh(inner)
    val = 0.5 * x * (1.0 + t)
    grad = 0.5 * (1.0 + t) + 0.5 * x * (1.0 - t * t) * _GELU_C * (1.0 + 3.0 * 0.044715 * x * x)
    return val, grad


def _ln_fwd(y, g, b):
    mu = jnp.mean(y, axis=-1, keepdims=True)
    yc = y - mu
    var = jnp.mean(yc * yc, axis=-1, keepdims=True)
    rstd = lax.rsqrt(var + LN_EPS)
    xhat = yc * rstd
    return xhat * g + b, xhat, rstd


def _ln_bwd(dh, xhat, rstd, g):
    dxh = dh * g
    m1 = jnp.mean(dxh, axis=-1, keepdims=True)
    m2 = jnp.mean(dxh * xhat, axis=-1, keepdims=True)
    dy = rstd * (dxh - m1 - xhat * m2)
    dg = jnp.sum(dh * xhat, axis=0, keepdims=True)
    db = jnp.sum(dh, axis=0, keepdims=True)
    return dy, dg, db


def _mask_dot(mask, x):
    hi = x.astype(BF16)
    lo = (x - hi.astype(F32)).astype(BF16)
    return _dot(mask, hi) + _dot(mask, lo)


def _block_masks(n):
    r = np.arange(n)[:, None]
    c = np.arange(n)[None, :]
    same = (r // HG_BLOCK) == (c // HG_BLOCK)
    return jnp.asarray(np.stack([same & (c <= r), same & (c >= r), same]), BF16)


def _row_tile(t):
    return min(t, 512)


def _col_tile(n):
    for cand in (512, 256, 128):
        if n % cand == 0:
            return cand
    return n


def _resident(w):
    return pl.BlockSpec(w.shape, lambda *_: (0, 0), pipeline_mode=pl.Buffered(1))


def _drop_deps(body, n_in, n_deps):
    if n_deps == 0:
        return body
    return lambda *refs: body(*refs[:n_in], *refs[n_in + n_deps:])


def _ffn_up(hb, wg, wu, name, deps=()):
    t, d = hb.shape
    f = wg.shape[1]
    tm = min(t, 256)
    tn = _col_tile(f)

    def body(h_ref, wg_ref, wu_ref, a_ref, b_ref, s_ref):
        h = h_ref[...]
        for c in range(f // tn):
            cols = slice(c * tn, (c + 1) * tn)
            a = _dot(h, wg_ref[:, cols])
            b = _dot(h, wu_ref[:, cols])
            a_ref[:, cols] = a.astype(BF16)
            b_ref[:, cols] = b.astype(BF16)
            s_ref[:, cols] = (a * _sigmoid(a) * b).astype(BF16)

    act = pl.BlockSpec((tm, f), lambda i: (i, 0))
    return pl.pallas_call(
        _drop_deps(body, 3, len(deps)),
        grid=(t // tm,),
        in_specs=[pl.BlockSpec((tm, d), lambda i: (i, 0)), _resident(wg), _resident(wu)] + [ANY] * len(deps),
        out_specs=[act, act, act],
        out_shape=[jax.ShapeDtypeStruct((t, f), BF16)] * 3,
        compiler_params=_params(1),
        name=name,
    )(hb, wg, wu, *deps)


def _mm_res_ln(lhs, w, res, g, b, coef, name, target=None):
    t, kd = lhs.shape
    d = w.shape[1]
    tm = _row_tile(t)
    nt = t // tm
    from_norm = isinstance(res, tuple)
    n_res = 3 if from_norm else 1

    def body(*refs):
        l_ref, w_ref = refs[:2]
        r_refs = refs[2:2 + n_res]
        g_ref, b_ref = refs[2 + n_res:4 + n_res]
        rest = refs[4 + n_res:]
        prev = r_refs[0][...] * r_refs[1][...] + r_refs[2][...] if from_norm else r_refs[0][...]
        y = ALPHA * prev + coef * _dot(l_ref[...], w_ref[...])
        h, xhat, rstd = _ln_fwd(y, g_ref[...], b_ref[...])
        if target is None:
            hb_ref, xh_ref, rs_ref = rest
            hb_ref[...] = h.astype(BF16)
            xh_ref[...] = xhat
            rs_ref[...] = rstd
            return
        t_ref, loss_ref, dy_ref, dyb_ref, dg_ref, db_ref, lacc = rest
        i = pl.program_id(0)

        @pl.when(i == 0)
        def _():
            lacc[...] = jnp.zeros_like(lacc)
            dg_ref[...] = jnp.zeros_like(dg_ref)
            db_ref[...] = jnp.zeros_like(db_ref)

        err = h - t_ref[...]
        lacc[...] += jnp.sum(err * err, axis=0, keepdims=True)
        dy, dg, db = _ln_bwd(err * (1.0 / d), xhat, rstd, g_ref[...])
        dy_ref[...] = dy
        dyb_ref[...] = dy.astype(BF16)
        dg_ref[...] += dg
        db_ref[...] += db

        @pl.when(i == nt - 1)
        def _():
            loss_ref[...] = jnp.zeros_like(loss_ref) + jnp.sum(lacc[...], axis=1, keepdims=True) * (0.5 / d)

    row = pl.BlockSpec((tm, d), lambda i: (i, 0))
    vec = pl.BlockSpec((1, d), lambda i: (0, 0))
    res_specs = [row, vec, vec] if from_norm else [row]
    res_args = list(res) if from_norm else [res]
    in_specs = [pl.BlockSpec((tm, kd), lambda i: (i, 0)), _resident(w)] + res_specs + [vec, vec]
    args = [lhs, w] + res_args + [g, b]
    if target is None:
        out_specs = [row, row, pl.BlockSpec((tm, 1), lambda i: (i, 0))]
        out_shape = [jax.ShapeDtypeStruct((t, d), BF16), jax.ShapeDtypeStruct((t, d), F32),
                     jax.ShapeDtypeStruct((t, 1), F32)]
        scratch = []
    else:
        in_specs.append(row)
        args.append(target)
        out_specs = [pl.BlockSpec((1, LANES), lambda i: (0, 0)), row, row, vec, vec]
        out_shape = [jax.ShapeDtypeStruct((1, LANES), F32), jax.ShapeDtypeStruct((t, d), F32),
                     jax.ShapeDtypeStruct((t, d), BF16), jax.ShapeDtypeStruct((1, d), F32),
                     jax.ShapeDtypeStruct((1, d), F32)]
        scratch = [pltpu.VMEM((1, d), F32)]
    return pl.pallas_call(
        body,
        grid=(nt,),
        in_specs=in_specs,
        out_specs=out_specs,
        out_shape=out_shape,
        scratch_shapes=scratch,
        compiler_params=_params(1),
        name=name,
    )(*args)


def _mm_nn(lhs, w, name):
    t, kd = lhs.shape
    n = w.shape[1]
    tm = _row_tile(t)
    tn = _col_tile(n)

    def body(l_ref, w_ref, o_ref):
        lhs_v = l_ref[...]
        for c in range(n // tn):
            cols = slice(c * tn, (c + 1) * tn)
            o_ref[:, cols] = _dot(lhs_v, w_ref[:, cols])

    return pl.pallas_call(
        body,
        grid=(t // tm,),
        in_specs=[pl.BlockSpec((tm, kd), lambda i: (i, 0)), _resident(w)],
        out_specs=pl.BlockSpec((tm, n), lambda i: (i, 0)),
        out_shape=jax.ShapeDtypeStruct((t, n), F32),
        compiler_params=_params(1),
        name=name,
    )(lhs, w)


def _lower_bound(lg):
    m = jnp.max(lg, axis=0, keepdims=True)
    e = jnp.exp(lg - m)
    return e[0:1, :] / jnp.sum(e, axis=0, keepdims=True)


def _forget_terms(fz, lb):
    e = jnp.exp(-jnp.abs(fz))
    r = 1.0 / (1.0 + e)
    pos = fz >= 0.0
    sig = jnp.where(pos, r, e * r)
    nsig = jnp.where(pos, e * r, r)
    f = lb + (1.0 - lb) * sig
    k = (1.0 - lb) * nsig
    return sig, nsig, f, k


def _hg_tile(t):
    return min(t, 256)


HG_HALF = HG_BLOCK // 2
NEG_BIG = -1e30


def _halves(a):
    return a[:HG_HALF, :], a[HG_HALF:, :]


def _causal_halves(s):
    return (0, 1) if s < HG_HALF else (1,)


def _decay_from(b_half, b_s, s, h, tidx):
    first = s - h * HG_HALF
    diff = b_half - b_s
    if first > 0:
        diff = jnp.where(tidx >= first, diff, NEG_BIG)
    return jnp.exp(diff)


def _hgrn_fwd(proj, logits, gn):
    t = proj.shape[0]
    ct = _hg_tile(t)
    nct = t // ct
    nblk = ct // HG_BLOCK
    nh = HG_HEADS

    def body(q_ref, fz_ref, iv_ref, gg_ref, lg_ref, gn_ref, mask_ref, oraw_ref, oa_ref, st_ref,
             state, qt_s, kt_s, k_s, b_s, dec_s):
        c = pl.program_id(1)

        @pl.when(c == 0)
        def _():
            state[...] = jnp.zeros_like(state)

        lb = _lower_bound(lg_ref[...])
        q = q_ref[...]
        _, _, f, k = _forget_terms(fz_ref[...], lb)
        logf = jnp.log(f)
        b = _mask_dot(mask_ref[0], logf)
        bend = _mask_dot(mask_ref[2], logf)
        qt_s[...] = (q * jnp.exp(b)).astype(BF16)
        kt_s[...] = (k * jnp.exp(bend - b)).astype(BF16)
        k_s[...] = k
        b_s[...] = b
        dec_s[...] = jnp.exp(bend)
        tidx = lax.broadcasted_iota(jnp.int32, (HG_HALF, HG_DIM), 0)

        def blk(i, carry):
            r0 = pl.multiple_of(i * HG_BLOCK, HG_BLOCK)
            rows = pl.ds(r0, HG_BLOCK)
            st = state[...]
            st_ref[i] = st
            v = iv_ref[rows, :]
            qq = q_ref[rows, :]
            kk = k_s[rows, :]
            bb = b_s[rows, :]
            o = list(_halves(_dot_nt(qt_s[rows, :], st.astype(BF16))))
            qh, bh = _halves(qq), _halves(bb)
            for s in range(HG_BLOCK):
                ks, vs = kk[s:s + 1, :], v[s:s + 1, :]
                for h in _causal_halves(s):
                    e = _decay_from(bh[h], bb[s:s + 1, :], s, h, tidx)
                    acol = jnp.sum(qh[h] * (ks * e), axis=1, keepdims=True)
                    o[h] = o[h] + acol * vs
            oraw_ref[rows, :] = jnp.concatenate(o, axis=0)
            state[...] = st * dec_s[pl.ds(r0, 1), :] + _dot_tn(v.astype(BF16), kt_s[rows, :])
            return carry

        lax.fori_loop(0, nblk, blk, 0, unroll=2 * HG_UNROLL)
        oraw = oraw_ref[...]
        r = lax.rsqrt(jnp.mean(oraw * oraw, axis=-1, keepdims=True) + LN_EPS)
        gg = gg_ref[...]
        oa_ref[...] = (oraw * r * gn_ref[...] * gg * _sigmoid(gg)).astype(BF16)

    def slab(off):
        return pl.BlockSpec((ct, HG_DIM), lambda h, c: (c, off + h))

    out_slab = pl.BlockSpec((ct, HG_DIM), lambda h, c: (c, h))
    return pl.pallas_call(
        body,
        grid=(nh, nct),
        in_specs=[slab(0), slab(nh), slab(2 * nh), slab(3 * nh),
                  pl.BlockSpec((None, 2, HG_DIM), lambda h, c: (h, 0, 0)),
                  pl.BlockSpec((1, HG_DIM), lambda h, c: (0, 0)),
                  pl.BlockSpec((3, ct, ct), lambda h, c: (0, 0, 0))],
        out_specs=[out_slab, out_slab, pl.BlockSpec((None, nblk, HG_DIM, HG_DIM), lambda h, c: (h, c, 0, 0))],
        out_shape=[jax.ShapeDtypeStruct((t, nh * HG_DIM), F32),
                   jax.ShapeDtypeStruct((t, (nh + SG_GROUPS) * HG_DIM), BF16),
                   jax.ShapeDtypeStruct((nh, t // HG_BLOCK, HG_DIM, HG_DIM), F32)],
        scratch_shapes=[pltpu.VMEM((HG_DIM, HG_DIM), F32), pltpu.VMEM((ct, HG_DIM), BF16),
                        pltpu.VMEM((ct, HG_DIM), BF16), pltpu.VMEM((ct, HG_DIM), F32),
                        pltpu.VMEM((ct, HG_DIM), F32), pltpu.VMEM((ct, HG_DIM), F32)],
        compiler_params=_params(2),
        name="hgrn_fwd",
    )(proj, proj, proj, proj, logits, gn, _block_masks(ct))


def _sg_tile(t):
    return min(t, 512)


def _sgu_chunk_fwd(u, v, ln_g, ln_b, wm, bs):
    ua, dua = _gelu_and_grad(u)
    va, dva = _gelu_and_grad(v)
    vn, xhat, rstd = _ln_fwd(va, ln_g, ln_b)
    s = _dot(wm, vn.astype(BF16)) + bs
    return ua, dua, dva, vn, xhat, rstd, s


def _tril_weight(w_ref):
    n = SG_CHUNK
    r = lax.broadcasted_iota(jnp.int32, (n, n), 0)
    c = lax.broadcasted_iota(jnp.int32, (n, n), 1)
    return jnp.where(c <= r, w_ref[...], 0.0)


def _sgu_fwd(proj, mix, ln_g, ln_b, w_s, b_col):
    t = proj.shape[0]
    ct = _sg_tile(t)
    ng = SG_GROUPS
    off_u = 4 * HG_HEADS
    off_v = off_u + ng

    def body(u_ref, v_ref, g_ref, b_ref, w_ref, bs_ref, mix_ref, o_ref):
        del mix_ref
        wm = _tril_weight(w_ref).astype(BF16)
        for n in range(ct // SG_CHUNK):
            rows = slice(n * SG_CHUNK, (n + 1) * SG_CHUNK)
            ua, _, _, _, _, _, s = _sgu_chunk_fwd(u_ref[rows, :], v_ref[rows, :], g_ref[...], b_ref[...], wm, bs_ref[...])
            o_ref[rows, :] = (ua * s).astype(BF16)

    vec = pl.BlockSpec((None, 1, SG_DIM), lambda g, c: (g, 0, 0))
    return pl.pallas_call(
        body,
        grid=(ng, t // ct),
        in_specs=[pl.BlockSpec((ct, SG_DIM), lambda g, c: (c, off_u + g)),
                  pl.BlockSpec((ct, SG_DIM), lambda g, c: (c, off_v + g)), vec, vec,
                  pl.BlockSpec((None, SG_CHUNK, SG_CHUNK), lambda g, c: (g, 0, 0)),
                  pl.BlockSpec((None, SG_CHUNK, 1), lambda g, c: (g, 0, 0)), ANY],
        out_specs=pl.BlockSpec((ct, SG_DIM), lambda g, c: (c, HG_HEADS + g)),
        out_shape=jax.ShapeDtypeStruct(mix.shape, mix.dtype),
        input_output_aliases={6: 0},
        compiler_params=_params(2),
        name="sgu_fwd",
    )(proj, proj, ln_g, ln_b, w_s, b_col, mix)


def _mem_kv(mem, g, b, wk, wv):
    m_len, d = mem.shape

    def body(m_ref, g_ref, b_ref, wk_ref, wv_ref, mb_ref, xh_ref, rs_ref, k_ref, v_ref):
        m, xhat, rstd = _ln_fwd(m_ref[...], g_ref[...], b_ref[...])
        mb = m.astype(BF16)
        mb_ref[...] = mb
        xh_ref[...] = xhat
        rs_ref[...] = rstd
        k_ref[...] = _dot(mb, wk_ref[...]).astype(BF16)
        v_ref[...] = _dot(mb, wv_ref[...]).astype(BF16)

    return pl.pallas_call(
        body,
        out_shape=[jax.ShapeDtypeStruct((m_len, d), BF16), jax.ShapeDtypeStruct((m_len, d), F32),
                   jax.ShapeDtypeStruct((m_len, 1), F32), jax.ShapeDtypeStruct((m_len, d), BF16),
                   jax.ShapeDtypeStruct((m_len, d), BF16)],
        compiler_params=pltpu.CompilerParams(vmem_limit_bytes=VMEM_LIMIT_V7X),
        name="mem_kv",
    )(mem, g, b, wk, wv)


def _softmax_rows(s):
    m = jnp.max(s, axis=-1, keepdims=True)
    p = jnp.exp(s - m)
    return p / jnp.sum(p, axis=-1, keepdims=True)


def _attn_fwd(hb, wq, kb, vb):
    t, d = hb.shape
    tm = _row_tile(t)
    dh = d // X_HEADS
    scale = dh ** -0.5

    def body(h_ref, wq_ref, k_ref, v_ref, q_ref, o_ref):
        q = _dot(h_ref[...], wq_ref[...]).astype(BF16)
        q_ref[...] = q
        for hd in range(X_HEADS):
            sl = slice(hd * dh, (hd + 1) * dh)
            p = _softmax_rows(_dot_nt(q[:, sl], k_ref[:, sl]) * scale)
            o_ref[:, sl] = _dot(p.astype(BF16), v_ref[:, sl]).astype(BF16)

    row = pl.BlockSpec((tm, d), lambda i: (i, 0))
    full = lambda a: pl.BlockSpec(a.shape, lambda i: (0, 0))
    return pl.pallas_call(
        body,
        grid=(t // tm,),
        in_specs=[row, full(wq), full(kb), full(vb)],
        out_specs=[row, row],
        out_shape=[jax.ShapeDtypeStruct((t, d), BF16), jax.ShapeDtypeStruct((t, d), BF16)],
        compiler_params=_params(1),
        name="attn_fwd",
    )(hb, wq, kb, vb)


def _ffn_bwd_act(dyb, wd, a, b, coef, name, deps=()):
    t, d = dyb.shape
    f = wd.shape[0]
    tm = _row_tile(t)
    tn = _col_tile(f)

    def body(dy_ref, wd_ref, a_ref, b_ref, da_ref, db_ref):
        dy = dy_ref[...]
        for c in range(f // tn):
            cols = slice(c * tn, (c + 1) * tn)
            ds = _dot_nt(dy, wd_ref[cols, :]) * coef
            silu, dsilu = _silu_and_grad(a_ref[:, cols].astype(F32))
            da_ref[:, cols] = (ds * b_ref[:, cols].astype(F32) * dsilu).astype(BF16)
            db_ref[:, cols] = (ds * silu).astype(BF16)

    act = pl.BlockSpec((tm, f), lambda i: (i, 0))
    return pl.pallas_call(
        _drop_deps(body, 4, len(deps)),
        grid=(t // tm,),
        in_specs=[pl.BlockSpec((tm, d), lambda i: (i, 0)), _resident(wd), act, act] + [ANY] * len(deps),
        out_specs=[act, act],
        out_shape=[jax.ShapeDtypeStruct((t, f), BF16), jax.ShapeDtypeStruct((t, f), BF16)],
        compiler_params=_params(1),
        name=name,
    )(dyb, wd, a, b, *deps)


def _mm_tn(a, b, name, scale=1.0, deps=()):
    t, m = a.shape
    n = b.shape[1]
    tt = _row_tile(t)
    nt = t // tt
    tm_o = m // 2 if (m > n and m * n > 2 ** 21) else m
    tn_o = n // 2 if (n > m and m * n > 2 ** 21) else n

    def body(a_ref, b_ref, o_ref, acc):
        k = pl.program_id(2)

        @pl.when(k == 0)
        def _():
            acc[...] = jnp.zeros_like(acc)

        acc[...] += _dot_tn(a_ref[...], b_ref[...])

        @pl.when(k == nt - 1)
        def _():
            o_ref[...] = (acc[...] * scale).astype(BF16)

    return pl.pallas_call(
        _drop_deps(body, 2, len(deps)),
        grid=(m // tm_o, n // tn_o, nt),
        in_specs=[pl.BlockSpec((tt, tm_o), lambda i, j, k: (k, i)), pl.BlockSpec((tt, tn_o), lambda i, j, k: (k, j))]
        + [ANY] * len(deps),
        out_specs=pl.BlockSpec((tm_o, tn_o), lambda i, j, k: (i, j)),
        out_shape=jax.ShapeDtypeStruct((m, n), BF16),
        scratch_shapes=[pltpu.VMEM((tm_o, tn_o), F32)],
        compiler_params=_params(3),
        name=name,
    )(a, b, *deps)


def _mm_nt(lhs, w, name):
    t, d = lhs.shape
    kd = w.shape[0]
    tm = _row_tile(t)

    def body(l_ref, w_ref, o_ref):
        o_ref[...] = _dot_nt(l_ref[...], w_ref[...])

    return pl.pallas_call(
        body,
        grid=(t // tm,),
        in_specs=[pl.BlockSpec((tm, d), lambda i: (i, 0)), _resident(w)],
        out_specs=pl.BlockSpec((tm, kd), lambda i: (i, 0)),
        out_shape=jax.ShapeDtypeStruct((t, kd), F32),
        compiler_params=_params(1),
        name=name,
    )(lhs, w)


def _dx_ln(dy, pairs, ln, name, deps=()):
    t, d = dy.shape
    npair = len(pairs)
    tm = min(t, 512 // npair)
    nt = t // tm
    n_in = 1 + 2 * npair + (3 if ln is not None else 0)

    def body(*refs):
        dy_ref = refs[0]
        pr = refs[1:1 + 2 * npair]
        pos = 1 + 2 * npair
        dh = ALPHA * dy_ref[...]
        for p in range(npair):
            dh = dh + _dot_nt(pr[2 * p][...], pr[2 * p + 1][...])
        if ln is not None:
            xh_ref, rs_ref, g_ref = refs[pos:pos + 3]
            dyo_ref, dyb_ref, dg_ref, db_ref = refs[pos + 3:pos + 7]

            @pl.when(pl.program_id(0) == 0)
            def _():
                dg_ref[...] = jnp.zeros_like(dg_ref)
                db_ref[...] = jnp.zeros_like(db_ref)

            dyp, dg, db = _ln_bwd(dh, xh_ref[...], rs_ref[...], g_ref[...])
            dyo_ref[...] = dyp
            dyb_ref[...] = dyp.astype(BF16)
            dg_ref[...] += dg
            db_ref[...] += db
        else:
            refs[pos][...] = dh

    row = pl.BlockSpec((tm, d), lambda i: (i, 0))
    vec = pl.BlockSpec((1, d), lambda i: (0, 0))
    in_specs = [row]
    args = [dy]
    for lhs, w in pairs:
        in_specs += [pl.BlockSpec((tm, lhs.shape[1]), lambda i: (i, 0)), _resident(w)]
        args += [lhs, w]
    if ln is not None:
        in_specs += [row, pl.BlockSpec((tm, 1), lambda i: (i, 0)), vec]
        args += list(ln)
        out_specs = [row, row, vec, vec]
        out_shape = [jax.ShapeDtypeStruct((t, d), F32), jax.ShapeDtypeStruct((t, d), BF16),
                     jax.ShapeDtypeStruct((1, d), F32), jax.ShapeDtypeStruct((1, d), F32)]
    else:
        out_specs = row
        out_shape = jax.ShapeDtypeStruct((t, d), F32)
    return pl.pallas_call(
        _drop_deps(body, n_in, len(deps)),
        grid=(nt,),
        in_specs=in_specs + [ANY] * len(deps),
        out_specs=out_specs,
        out_shape=out_shape,
        compiler_params=_params(1),
        name=name,
    )(*args, *deps)


def _hgrn_bwd(proj, oraw, dmix, states, logits, gn):
    t = proj.shape[0]
    ct = _hg_tile(t)
    nct = t // ct
    nblk = ct // HG_BLOCK
    nh = HG_HEADS

    def body(q_ref, fz_ref, iv_ref, gg_ref, or_ref, do_ref, st_ref, lg_ref, gn_ref, mask_ref,
             dq_ref, dfz_ref, div_ref, dgg_ref, dlg_ref, dgn_ref,
             dstate, qt_s, kt_s, k_s, b_s, eb_s, ekb_s, dec_s, dor_s, dbl_s, gr_s, dk_s, dlb_acc):
        c = pl.program_id(1)

        @pl.when(c == 0)
        def _():
            dstate[...] = jnp.zeros_like(dstate)
            dlb_acc[...] = jnp.zeros_like(dlb_acc)
            dgn_ref[...] = jnp.zeros_like(dgn_ref)

        lb = _lower_bound(lg_ref[...])
        q = q_ref[...]
        sig, nsig, f, k = _forget_terms(fz_ref[...], lb)
        logf = jnp.log(f)
        b = _mask_dot(mask_ref[0], logf)
        bend = _mask_dot(mask_ref[2], logf)
        eb = jnp.exp(b)
        ekb = jnp.exp(bend - b)
        qt_s[...] = (q * eb).astype(BF16)
        kt_s[...] = (k * ekb).astype(BF16)
        k_s[...] = k
        b_s[...] = b
        eb_s[...] = eb
        ekb_s[...] = ekb
        dec_s[...] = jnp.exp(bend)
        oraw = or_ref[...]
        r = lax.rsqrt(jnp.mean(oraw * oraw, axis=-1, keepdims=True) + LN_EPS)
        on = oraw * r
        gg = gg_ref[...]
        silu, dsilu = _silu_and_grad(gg)
        doa = do_ref[...]
        gnv = gn_ref[...]
        dgg_ref[...] = (doa * on * gnv * dsilu).astype(BF16)
        dyn = doa * silu
        dgn_ref[...] += jnp.sum(dyn * on, axis=0, keepdims=True)
        don = dyn * gnv
        dor_s[...] = r * (don - on * jnp.mean(don * on, axis=-1, keepdims=True))
        tidx = lax.broadcasted_iota(jnp.int32, (HG_HALF, HG_DIM), 0)

        def blk(ii, carry):
            i = nblk - 1 - ii
            r0 = pl.multiple_of(i * HG_BLOCK, HG_BLOCK)
            rows = pl.ds(r0, HG_BLOCK)
            st = st_ref[i]
            dst = dstate[...]
            dstb = dst.astype(BF16)
            do = dor_s[rows, :]
            dob = do.astype(BF16)
            v = iv_ref[rows, :]
            vb = v.astype(BF16)
            qq = q_ref[rows, :]
            kk = k_s[rows, :]
            bb = b_s[rows, :]
            qt = qt_s[rows, :]
            kt = kt_s[rows, :]
            dec = dec_s[pl.ds(r0, 1), :]
            dkt = _dot(vb, dstb)
            dq = _dot(dob, st.astype(BF16)) * eb_s[rows, :]
            dk = dkt * ekb_s[rows, :]
            dv = _dot_nt(kt, dstb)
            gend = jnp.sum(kk * dk, axis=0, keepdims=True) + dec * jnp.sum(dst * st, axis=0, keepdims=True)
            qh, bh, doh = _halves(qq), _halves(bb), _halves(do)
            dqh, dkh, dvh = list(_halves(dq)), list(_halves(dk)), list(_halves(dv))
            for s in range(HG_BLOCK):
                ks, vs = kk[s:s + 1, :], v[s:s + 1, :]
                dk_part = dv_part = None
                for h in _causal_halves(s):
                    e = _decay_from(bh[h], bb[s:s + 1, :], s, h, tidx)
                    ke = ks * e
                    acol = jnp.sum(qh[h] * ke, axis=1, keepdims=True)
                    dacol = jnp.sum(doh[h] * vs, axis=1, keepdims=True)
                    dqh[h] = dqh[h] + dacol * ke
                    pk = dacol * (qh[h] * e)
                    pv = acol * doh[h]
                    dk_part = pk if dk_part is None else dk_part + pk
                    dv_part = pv if dv_part is None else dv_part + pv
                hs, row = divmod(s, HG_HALF)
                dkh[hs] = dkh[hs] + jnp.where(tidx == row, jnp.sum(dk_part, axis=0, keepdims=True), 0.0)
                dvh[hs] = dvh[hs] + jnp.where(tidx == row, jnp.sum(dv_part, axis=0, keepdims=True), 0.0)
            dq = jnp.concatenate(dqh, axis=0)
            dk = jnp.concatenate(dkh, axis=0)
            dv = jnp.concatenate(dvh, axis=0)
            dq_ref[rows, :] = dq.astype(BF16)
            div_ref[rows, :] = dv.astype(BF16)
            dk_s[rows, :] = dk
            dbl_s[rows, :] = qq * dq - kk * dk
            gr_s[rows, :] = jnp.zeros((HG_BLOCK, HG_DIM), F32) + gend
            dstate[...] = dst * dec + _dot_tn(dob, qt)
            return carry

        lax.fori_loop(0, nblk, blk, 0, unroll=2 * HG_UNROLL)
        dlogf = _mask_dot(mask_ref[1], dbl_s[...]) + gr_s[...]
        dk = dk_s[...]
        dfz_ref[...] = ((dlogf / f - dk) * ((1.0 - lb) * sig * nsig)).astype(BF16)
        dlb_acc[...] += jnp.sum((dlogf / f - dk) * nsig, axis=0, keepdims=True)

        @pl.when(c == nct - 1)
        def _():
            dl0 = dlb_acc[...] * lb * (1.0 - lb)
            layer = lax.broadcasted_iota(jnp.int32, (2, HG_DIM), 0)
            dlg_ref[...] = jnp.where(layer == 0, dl0, -dl0)

    def slab(off):
        return pl.BlockSpec((ct, HG_DIM), lambda h, c: (nct - 1 - c, off + h))

    out_slab = pl.BlockSpec((ct, HG_DIM), lambda h, c: (nct - 1 - c, h))
    tile_f32 = pltpu.VMEM((ct, HG_DIM), F32)
    tile_b16 = pltpu.VMEM((ct, HG_DIM), BF16)
    slab_shape = jax.ShapeDtypeStruct((t, nh * HG_DIM), BF16)
    return pl.pallas_call(
        body,
        grid=(nh, nct),
        in_specs=[slab(0), slab(nh), slab(2 * nh), slab(3 * nh), slab(0), slab(0),
                  pl.BlockSpec((None, nblk, HG_DIM, HG_DIM), lambda h, c: (h, nct - 1 - c, 0, 0)),
                  pl.BlockSpec((None, 2, HG_DIM), lambda h, c: (h, 0, 0)),
                  pl.BlockSpec((1, HG_DIM), lambda h, c: (0, 0)),
                  pl.BlockSpec((3, ct, ct), lambda h, c: (0, 0, 0))],
        out_specs=[out_slab, out_slab, out_slab, out_slab,
                   pl.BlockSpec((None, 2, HG_DIM), lambda h, c: (h, 0, 0)),
                   pl.BlockSpec((None, 1, HG_DIM), lambda h, c: (h, 0, 0))],
        out_shape=[slab_shape, slab_shape, slab_shape, slab_shape,
                   jax.ShapeDtypeStruct((nh, 2, HG_DIM), F32), jax.ShapeDtypeStruct((nh, 1, HG_DIM), F32)],
        scratch_shapes=[pltpu.VMEM((HG_DIM, HG_DIM), F32), tile_b16, tile_b16, tile_f32, tile_f32, tile_f32, tile_f32,
                        tile_f32, tile_f32, tile_f32, tile_f32, tile_f32, pltpu.VMEM((1, HG_DIM), F32)],
        compiler_params=_params(2),
        name="hgrn_bwd",
    )(proj, proj, proj, proj, oraw, dmix, states, logits, gn, _block_masks(ct))


def _sgu_bwd(proj, dmix, ln_g, ln_b, w_s, w_t, b_col):
    t = proj.shape[0]
    ct = _sg_tile(t)
    nct = t // ct
    ng = SG_GROUPS
    off_u = 4 * HG_HEADS
    off_v = off_u + ng
    n = SG_CHUNK

    def body(u_ref, v_ref, do_ref, g_ref, b_ref, w_ref, wt_ref, bs_ref, du_ref, dv_ref, dg_ref, db_ref, dw_ref, dbs_ref):
        c = pl.program_id(1)

        @pl.when(c == 0)
        def _():
            dg_ref[...] = jnp.zeros_like(dg_ref)
            db_ref[...] = jnp.zeros_like(db_ref)
            dw_ref[...] = jnp.zeros_like(dw_ref)
            dbs_ref[...] = jnp.zeros_like(dbs_ref)

        r = lax.broadcasted_iota(jnp.int32, (n, n), 0)
        cc = lax.broadcasted_iota(jnp.int32, (n, n), 1)
        wm = jnp.where(cc <= r, w_ref[...], 0.0).astype(BF16)
        wmt = jnp.where(r <= cc, wt_ref[...], 0.0).astype(BF16)
        for ci in range(ct // n):
            rows = slice(ci * n, (ci + 1) * n)
            ua, dua, dva, vn, xhat, rstd, s = _sgu_chunk_fwd(u_ref[rows, :], v_ref[rows, :], g_ref[...], b_ref[...],
                                                             wm, bs_ref[...])
            do = do_ref[rows, :]
            du_ref[rows, :] = (do * s * dua).astype(BF16)
            ds = do * ua
            dsb = ds.astype(BF16)
            dbs_ref[...] += jnp.sum(ds, axis=1, keepdims=True)
            dw_ref[...] += _dot_nt(dsb, vn.astype(BF16))
            dvn = _dot(wmt, dsb)
            dva_in, dg, db = _ln_bwd(dvn, xhat, rstd, g_ref[...])
            dg_ref[...] += dg
            db_ref[...] += db
            dv_ref[rows, :] = (dva_in * dva).astype(BF16)

        @pl.when(c == nct - 1)
        def _():
            dw_ref[...] = jnp.where(cc <= r, dw_ref[...], 0.0)

    vec = pl.BlockSpec((None, 1, SG_DIM), lambda g, c: (g, 0, 0))
    mat = pl.BlockSpec((None, n, n), lambda g, c: (g, 0, 0))
    col = pl.BlockSpec((None, n, 1), lambda g, c: (g, 0, 0))
    out_slab = pl.BlockSpec((ct, SG_DIM), lambda g, c: (c, g))
    return pl.pallas_call(
        body,
        grid=(ng, nct),
        in_specs=[pl.BlockSpec((ct, SG_DIM), lambda g, c: (c, off_u + g)),
                  pl.BlockSpec((ct, SG_DIM), lambda g, c: (c, off_v + g)),
                  pl.BlockSpec((ct, SG_DIM), lambda g, c: (c, ng + g)), vec, vec, mat, mat, col],
        out_specs=[out_slab, out_slab, vec, vec, mat, col],
        out_shape=[jax.ShapeDtypeStruct((t, ng * SG_DIM), BF16), jax.ShapeDtypeStruct((t, ng * SG_DIM), BF16),
                   jax.ShapeDtypeStruct((ng, 1, SG_DIM), F32), jax.ShapeDtypeStruct((ng, 1, SG_DIM), F32),
                   jax.ShapeDtypeStruct((ng, n, n), F32), jax.ShapeDtypeStruct((ng, n, 1), F32)],
        compiler_params=_params(2),
        name="sgu_bwd",
    )(proj, proj, dmix, ln_g, ln_b, w_s, w_t, b_col)


def _attn_bwd(dyb, wo, qb, kb, vb):
    t, d = dyb.shape
    m_len = kb.shape[0]
    tm = _row_tile(t)
    dh = d // X_HEADS
    scale = dh ** -0.5

    def body(dy_ref, wo_ref, q_ref, k_ref, v_ref, dq_ref, dk_ref, dv_ref):
        i = pl.program_id(0)

        @pl.when(i == 0)
        def _():
            dk_ref[...] = jnp.zeros_like(dk_ref)
            dv_ref[...] = jnp.zeros_like(dv_ref)

        do = _dot_nt(dy_ref[...], wo_ref[...]).astype(BF16)
        for hd in range(X_HEADS):
            sl = slice(hd * dh, (hd + 1) * dh)
            qh = q_ref[:, sl]
            p = _softmax_rows(_dot_nt(qh, k_ref[:, sl]) * scale)
            doh = do[:, sl]
            dp = _dot_nt(doh, v_ref[:, sl])
            ds = (p * (dp - jnp.sum(dp * p, axis=-1, keepdims=True)) * scale).astype(BF16)
            dq_ref[:, sl] = _dot(ds, k_ref[:, sl]).astype(BF16)
            dk_ref[:, sl] += _dot_tn(ds, qh)
            dv_ref[:, sl] += _dot_tn(p.astype(BF16), doh)

    row = pl.BlockSpec((tm, d), lambda i: (i, 0))
    full = lambda a: pl.BlockSpec(a.shape, lambda i: (0, 0))
    kv = pl.BlockSpec((m_len, d), lambda i: (0, 0))
    return pl.pallas_call(
        body,
        grid=(t // tm,),
        in_specs=[row, full(wo), row, full(kb), full(vb)],
        out_specs=[row, kv, kv],
        out_shape=[jax.ShapeDtypeStruct((t, d), BF16), jax.ShapeDtypeStruct((m_len, d), F32),
                   jax.ShapeDtypeStruct((m_len, d), F32)],
        compiler_params=_params(1),
        name="attn_bwd",
    )(dyb, wo, qb, kb, vb)


def _mem_bwd(dk, dv, mb, xhat, rstd, g, wk, wv):
    m_len, d = dk.shape

    def body(dk_ref, dv_ref, mb_ref, xh_ref, rs_ref, g_ref, wk_ref, wv_ref, gwk_ref, gwv_ref, dg_ref, db_ref):
        dkb = dk_ref[...].astype(BF16)
        dvb = dv_ref[...].astype(BF16)
        mb_v = mb_ref[...]
        gwk_ref[...] = _dot_tn(mb_v, dkb).astype(BF16)
        gwv_ref[...] = _dot_tn(mb_v, dvb).astype(BF16)
        dm = _dot_nt(dkb, wk_ref[...]) + _dot_nt(dvb, wv_ref[...])
        _, dg, db = _ln_bwd(dm, xh_ref[...], rs_ref[...], g_ref[...])
        dg_ref[...] = dg
        db_ref[...] = db

    return pl.pallas_call(
        body,
        out_shape=[jax.ShapeDtypeStruct((d, d), BF16), jax.ShapeDtypeStruct((d, d), BF16),
                   jax.ShapeDtypeStruct((1, d), F32), jax.ShapeDtypeStruct((1, d), F32)],
        compiler_params=pltpu.CompilerParams(vmem_limit_bytes=VMEM_LIMIT_V7X),
        name="mem_bwd",
    )(dk, dv, mb, xhat, rstd, g, wk, wv)


def _adamw(w, g, m, v):
    m = ADAM_B1 * m + (1.0 - ADAM_B1) * g
    v = ADAM_B2 * v + (1.0 - ADAM_B2) * (g * g)
    m_hat = m / (1.0 - ADAM_B1 ** ADAM_STEP)
    v_hat = v / (1.0 - ADAM_B2 ** ADAM_STEP)
    delta = -ADAM_LR * (m_hat / (jnp.sqrt(v_hat) + ADAM_EPS) + ADAM_WD * w)
    return delta, m, v


def _slot_sum(ref):
    g = ref[0].astype(F32)
    for s in range(1, N_DEV):
        g = g + ref[s].astype(F32)
    return g


def _adam_sharded(lands, w, m, v, axis, name):
    rows, cols = w.shape
    nl = len(lands)
    transposed = axis == 1 and nl == 2
    if transposed:
        rows, cols = cols, rows
        tr = 256
        grid = (rows // tr,)
        wblk = pl.BlockSpec((cols, tr), lambda i: (0, i))
        lblk = [pl.BlockSpec((N_DEV, tr, a.shape[2]), lambda i: (0, i, 0)) for a in lands]
    elif axis == 1:
        tr = 256 if rows % 256 == 0 else rows
        grid = (rows // tr,)
        wblk = pl.BlockSpec((tr, cols), lambda i: (i, 0))
        lblk = [pl.BlockSpec((N_DEV, tr, a.shape[2]), lambda i: (0, i, 0)) for a in lands]
    else:
        tc = _col_tile(cols)
        grid = (cols // tc,)
        wblk = pl.BlockSpec((rows, tc), lambda i: (0, i))
        lblk = [pl.BlockSpec((N_DEV, a.shape[1], tc), lambda i: (0, 0, i)) for a in lands]

    def body(*refs):
        w_ref, m_ref, v_ref = refs[nl:nl + 3]
        g_ref, d_ref, nm_ref, nv_ref = refs[nl + 3:]
        g = _slot_sum(refs[0])
        if nl == 2:
            tail = _slot_sum(refs[1])
            if transposed:
                g = jnp.concatenate([g.T, tail.T[:cols - g.shape[1], :]], axis=0)
            elif axis == 1:
                g = jnp.concatenate([g, tail[:, :cols - g.shape[1]]], axis=1)
            else:
                g = jnp.concatenate([g, tail[:rows - g.shape[0], :]], axis=0)
        delta, nm, nv = _adamw(w_ref[...], g, m_ref[...], v_ref[...])
        g_ref[...] = g
        d_ref[...] = delta
        nm_ref[...] = nm
        nv_ref[...] = nv

    shp = jax.ShapeDtypeStruct(w.shape, F32)
    return pl.pallas_call(
        body,
        grid=grid,
        in_specs=lblk + [wblk, wblk, wblk],
        out_specs=[wblk, wblk, wblk, wblk],
        out_shape=[shp, shp, shp, shp],
        compiler_params=_params(1),
        name=name,
    )(*lands, w, m, v)


def _mesh_pos():
    return lax.axis_index("x"), lax.axis_index("y"), lax.axis_index("c")


def _peer(k):
    x, y, c = _mesh_pos()
    pos = (x ^ (k >> 2), y ^ ((k >> 1) & 1), c ^ (k & 1))
    return pos, 4 * pos[0] + 2 * pos[1] + pos[2]


def _sem_index(row, k):
    return row * (N_DEV - 1) + k - 1


def _window(ref, axis, start, size):
    align = 16 if axis == 0 else LANES
    start = pl.multiple_of(start, align)
    return ref.at[pl.ds(start, size), :] if axis == 0 else ref.at[:, pl.ds(start, size)]


def _piece_refs(piece, srcs, lands, me, peer):
    kind, si, li, axis, base, stride, shape = piece
    if kind == "gather":
        return srcs[si], _window(lands[li], axis, base + stride * me, shape[axis])
    return _window(srcs[si], axis, base + stride * peer, shape[axis]), lands[li].at[me]


def _place_own(srcs, land_shapes, pieces, name):
    ns, nl, npc = len(srcs), len(land_shapes), len(pieces)

    def body(*refs):
        s_refs = refs[:ns]
        l_refs = refs[ns:ns + nl]
        bufs = refs[ns + nl:ns + nl + npc]
        sems = refs[ns + nl + npc]
        x, y, c = _mesh_pos()
        me = 4 * x + 2 * y + c
        loads = []
        for p, piece in enumerate(pieces):
            src, dst = _piece_refs(piece, s_refs, l_refs, me, me)
            cp = pltpu.make_async_copy(src, bufs[p], sems.at[0, p])
            cp.start()
            loads.append((cp, dst))
        stores = []
        for p, (cp, dst) in enumerate(loads):
            cp.wait()
            out = pltpu.make_async_copy(bufs[p], dst, sems.at[1, p])
            out.start()
            stores.append(out)
        for out in stores:
            out.wait()

    out = pl.pallas_call(
        body,
        in_specs=[ANY] * ns,
        out_specs=[ANY] * nl,
        out_shape=list(land_shapes),
        scratch_shapes=[pltpu.VMEM(pc[6], srcs[pc[1]].dtype) for pc in pieces] + [pltpu.SemaphoreType.DMA((2, npc))],
        compiler_params=pltpu.CompilerParams(vmem_limit_bytes=VMEM_LIMIT_V7X),
        name=name,
    )(*srcs)
    return list(out)


def _comm_start(srcs, lands, pieces, groups, name, after=()):
    ns, nl, na, ng = len(srcs), len(lands), len(after), len(groups)

    def body(*refs):
        s_refs = refs[:ns]
        l_refs = refs[ns:ns + nl]
        outs = refs[ns + nl + na:]
        sems = outs[:2 * ng]
        token = outs[-1]
        x, y, c = _mesh_pos()
        me = 4 * x + 2 * y + c
        for g, members in enumerate(groups):
            for row, p in enumerate(members):
                for k in range(1, N_DEV):
                    pos, peer = _peer(k)
                    src, dst = _piece_refs(pieces[p], s_refs, l_refs, me, peer)
                    pltpu.make_async_remote_copy(src_ref=src, dst_ref=dst, send_sem=sems[2 * g].at[_sem_index(row, k)],
                                                 recv_sem=sems[2 * g + 1].at[_sem_index(row, k)], device_id=pos,
                                                 device_id_type=MESH_ID).start()
        token[...] = jnp.zeros_like(token)

    sem_shapes = []
    for members in groups:
        sem_shapes += [pltpu.SemaphoreType.DMA((len(members) * (N_DEV - 1),))] * 2
    hbm_of = lambda a: pltpu.HBM(a.shape, a.dtype)
    out = pl.pallas_call(
        body,
        in_specs=[HBM] * (ns + nl) + [ANY] * na,
        out_specs=[SEM] * (2 * ng) + [HBM] * (ns + nl) + [pl.BlockSpec(memory_space=pltpu.VMEM)],
        out_shape=sem_shapes + [hbm_of(a) for a in srcs] + [hbm_of(a) for a in lands]
        + [jax.ShapeDtypeStruct((8, LANES), F32)],
        input_output_aliases={i: 2 * ng + i for i in range(ns + nl)},
        compiler_params=pltpu.CompilerParams(has_side_effects=DATAFLOW),
        name=name,
    )(*[pltpu.with_memory_space_constraint(a, pltpu.HBM) for a in list(srcs) + list(lands)], *after)
    sems = [(out[2 * g], out[2 * g + 1]) for g in range(ng)]
    return sems, list(out[2 * ng:2 * ng + ns]), list(out[2 * ng + ns:2 * ng + ns + nl]), out[-1]


def _comm_wait(srcs, lands, pieces, members, sems, after, name):
    ns, nl, na = len(srcs), len(lands), len(after)

    def body(*refs):
        s_refs = refs[:ns]
        l_refs = refs[ns:ns + nl]
        send_sems, recv_sems = refs[ns + nl:ns + nl + 2]
        x, y, c = _mesh_pos()
        me = 4 * x + 2 * y + c
        for row, p in enumerate(members):
            for k in range(1, N_DEV):
                pos, peer = _peer(k)
                src, dst = _piece_refs(pieces[p], s_refs, l_refs, me, peer)
                cp = pltpu.make_async_remote_copy(src_ref=src, dst_ref=dst, send_sem=send_sems.at[_sem_index(row, k)],
                                                  recv_sem=recv_sems.at[_sem_index(row, k)], device_id=pos,
                                                  device_id_type=MESH_ID)
                cp.wait_send()
                cp.wait_recv()

    hbm_of = lambda a: pltpu.HBM(a.shape, a.dtype)
    out = pl.pallas_call(
        body,
        in_specs=[HBM] * (ns + nl) + [SEM, SEM] + [ANY] * na,
        out_specs=[HBM] * (ns + nl),
        out_shape=[hbm_of(a) for a in srcs] + [hbm_of(a) for a in lands],
        input_output_aliases={i: i for i in range(ns + nl)},
        compiler_params=pltpu.CompilerParams(has_side_effects=DATAFLOW),
        name=name,
    )(*srcs, *lands, sems[0], sems[1], *after)
    return list(out[ns:])


_CHIP_FLIPS = (2, 4, 6)
_SIBLING = 1


def _gather2_first(srcs, lands, pieces, name):
    ns, nl, npc = len(srcs), len(lands), len(pieces)

    def body(*refs):
        s_refs, l_refs = refs[:ns], refs[ns:ns + nl]
        send_sems, recv_sib, recv_ici = refs[ns + nl:ns + nl + 3]
        token = refs[-1]
        x, y, c = _mesh_pos()
        me = 4 * x + 2 * y + c
        for p, piece in enumerate(pieces):
            for j, k in enumerate((_SIBLING,) + _CHIP_FLIPS):
                pos, peer = _peer(k)
                src, dst = _piece_refs(piece, s_refs, l_refs, me, peer)
                recv = recv_sib.at[p] if j == 0 else recv_ici.at[3 * p + j - 1]
                pltpu.make_async_remote_copy(src_ref=src, dst_ref=dst, send_sem=send_sems.at[4 * p + j], recv_sem=recv,
                                             device_id=pos, device_id_type=MESH_ID).start()
        token[...] = jnp.zeros_like(token)

    hbm_of = lambda a: pltpu.HBM(a.shape, a.dtype)
    dma = lambda n: pltpu.SemaphoreType.DMA((n,))
    out = pl.pallas_call(
        body,
        in_specs=[HBM] * (ns + nl),
        out_specs=[SEM] * 3 + [HBM] * (ns + nl) + [pl.BlockSpec(memory_space=pltpu.VMEM)],
        out_shape=[dma(4 * npc), dma(npc), dma(3 * npc)] + [hbm_of(a) for a in srcs] + [hbm_of(a) for a in lands]
        + [jax.ShapeDtypeStruct((8, LANES), F32)],
        input_output_aliases={i: 3 + i for i in range(ns + nl)},
        compiler_params=pltpu.CompilerParams(has_side_effects=DATAFLOW),
        name=name,
    )(*[pltpu.with_memory_space_constraint(a, pltpu.HBM) for a in list(srcs) + list(lands)])
    return out[0], out[1], out[2], list(out[3:3 + ns]), list(out[3 + ns:3 + ns + nl]), out[-1]


def _landed_block(piece, lands, owner):
    _, _, li, axis, base, stride, shape = piece
    return _window(lands[li], axis, base + stride * owner, shape[axis])


def _gather2_pass(lands, pieces, recv_ici, after, name):
    nl, npc, na = len(lands), len(pieces), len(after)

    def body(*refs):
        l_refs = refs[:nl]
        recv_sems = refs[nl]
        send_fwd, recv_fwd = refs[nl + 1 + na:nl + 3 + na]
        sib, _ = _peer(_SIBLING)
        for p, piece in enumerate(pieces):
            for j, k in enumerate(_CHIP_FLIPS):
                pos, owner = _peer(k)
                block = _landed_block(piece, l_refs, owner)
                pltpu.make_async_remote_copy(src_ref=block, dst_ref=block, send_sem=send_fwd.at[3 * p + j],
                                             recv_sem=recv_sems.at[3 * p + j], device_id=pos,
                                             device_id_type=MESH_ID).wait_recv()
                pltpu.make_async_remote_copy(src_ref=block, dst_ref=block, send_sem=send_fwd.at[3 * p + j],
                                             recv_sem=recv_fwd.at[3 * p + j], device_id=sib,
                                             device_id_type=MESH_ID).start()

    hbm_of = lambda a: pltpu.HBM(a.shape, a.dtype)
    dma = lambda n: pltpu.SemaphoreType.DMA((n,))
    out = pl.pallas_call(
        body,
        in_specs=[HBM] * nl + [SEM] + [ANY] * na,
        out_specs=[SEM, SEM] + [HBM] * nl,
        out_shape=[dma(3 * npc), dma(3 * npc)] + [hbm_of(a) for a in lands],
        input_output_aliases={i: 2 + i for i in range(nl)},
        compiler_params=pltpu.CompilerParams(has_side_effects=DATAFLOW),
        name=name,
    )(*lands, recv_ici, *after)
    return out[0], out[1], list(out[2:])


def _gather2_last(srcs, lands, pieces, send_sems, recv_sib, send_fwd, recv_fwd, name):
    ns, nl = len(srcs), len(lands)

    def body(*refs):
        s_refs, l_refs = refs[:ns], refs[ns:ns + nl]
        send_a, recv_s, send_f, recv_f = refs[ns + nl:ns + nl + 4]
        x, y, c = _mesh_pos()
        me = 4 * x + 2 * y + c
        sib, sib_index = _peer(_SIBLING)
        for p, piece in enumerate(pieces):
            src, dst = _piece_refs(piece, s_refs, l_refs, me, sib_index)
            own = lambda s_sem, r_sem: pltpu.make_async_remote_copy(
                src_ref=src, dst_ref=dst, send_sem=s_sem, recv_sem=r_sem, device_id=sib, device_id_type=MESH_ID)
            own(send_a.at[4 * p], recv_s.at[p]).wait_recv()
            for j in range(4):
                own(send_a.at[4 * p + j], recv_s.at[p]).wait_send()
            for j in range(3):
                fwd = own(send_f.at[3 * p + j], recv_f.at[3 * p + j])
                fwd.wait_recv()
                fwd.wait_send()

    hbm_of = lambda a: pltpu.HBM(a.shape, a.dtype)
    out = pl.pallas_call(
        body,
        in_specs=[HBM] * (ns + nl) + [SEM] * 4,
        out_specs=[HBM] * (ns + nl),
        out_shape=[hbm_of(a) for a in srcs] + [hbm_of(a) for a in lands],
        input_output_aliases={i: i for i in range(ns + nl)},
        compiler_params=pltpu.CompilerParams(has_side_effects=DATAFLOW),
        name=name,
    )(*srcs, *lands, send_sems, recv_sib, send_fwd, recv_fwd)
    return list(out[ns:])


_SMALL_NAMES = ("ln1_g", "ln1_b", "hg_lb_logits", "hg_norm_g", "sg_ln_g", "sg_ln_b", "sg_w_s", "sg_b_s",
                "ln2_g", "ln2_b", "mem_ln_g", "mem_ln_b", "ln3_g", "ln3_b", "ln4_g", "ln4_b")


_VEC_NAMES = ("ln1_g", "ln1_b", "ln2_g", "ln2_b", "mem_ln_g", "mem_ln_b", "ln3_g", "ln3_b", "ln4_g", "ln4_b")
_ROW_NAMES = ("hg_lb_logits", "hg_norm_g", "sg_ln_g", "sg_ln_b", "sg_b_s", "sg_w_s")
VEC_ROWS = 16


def _row_plan(shapes):
    plan, pos = {}, 0
    for k in _ROW_NAMES:
        shp = shapes[k]
        slabs, off = [], pos
        for idx in itertools.product(*[range(dim) for dim in shp[:-2]]):
            slabs.append((idx, off, shp[-2]))
            off += shp[-2]
        plan[k] = (pos, slabs)
        pos = -(-off // 8) * 8
    return plan, -(-pos // 16) * 16


def _pack_small_grads(gs, shapes):
    vec = jnp.concatenate([gs[k].reshape(1, -1) for k in _VEC_NAMES], axis=0)
    vec = jnp.pad(vec, ((0, VEC_ROWS - vec.shape[0]), (0, 0)))
    plan, total = _row_plan(shapes)
    parts, pos = [], 0
    for k in _ROW_NAMES:
        first, slabs = plan[k]
        rows = gs[k].reshape(-1, LANES)
        end = slabs[-1][1] + slabs[-1][2]
        nxt = -(-end // 8) * 8
        parts.append(jnp.pad(rows, ((0, nxt - first - rows.shape[0]), (0, 0))))
        pos = nxt
    parts.append(jnp.zeros((total - pos, LANES), F32))
    return vec, jnp.concatenate(parts, axis=0)


def _adam_small(land_vec, land_rows, w, m, v):
    names = _VEC_NAMES + _ROW_NAMES
    n = len(names)
    shapes = {k: w[k].shape for k in names}
    plan, _ = _row_plan(shapes)

    def body(*refs):
        lv_ref, lr_ref = refs[:2]
        w_refs, m_refs, v_refs = refs[2:2 + n], refs[2 + n:2 + 2 * n], refs[2 + 2 * n:2 + 3 * n]
        outs = refs[2 + 3 * n:2 + 7 * n]
        gv_s, gr_s = refs[2 + 7 * n:]
        gv_s[...] = _slot_sum(lv_ref)
        gr_s[...] = _slot_sum(lr_ref)
        for p, k in enumerate(names):
            if k in _VEC_NAMES:
                row = _VEC_NAMES.index(k)
                slabs = [((), None, None)]
            else:
                slabs = plan[k][1]
            for idx, off, rows in slabs:
                g = gv_s[row:row + 1, :] if off is None else gr_s[off:off + rows, :]
                sel = idx + (slice(None), slice(None))
                delta, nm, nv = _adamw(w_refs[p][sel], g, m_refs[p][sel], v_refs[p][sel])
                for o, val in zip(range(4), (g, delta, nm, nv)):
                    outs[o * n + p][sel] = val

    flat = lambda tree: [tree[k] for k in names]
    shp = [jax.ShapeDtypeStruct(shapes[k], F32) for k in names]
    out = pl.pallas_call(
        body,
        out_shape=shp * 4,
        scratch_shapes=[pltpu.VMEM(land_vec.shape[1:], F32), pltpu.VMEM(land_rows.shape[1:], F32)],
        name="adam_small",
    )(land_vec, land_rows, *flat(w), *flat(m), *flat(v))
    return [dict(zip(names, out[o * n:(o + 1) * n])) for o in range(4)]


_COL_FFN = ("ffn1_w_gate", "ffn1_w_up", "ffn2_w_gate", "ffn2_w_up")
_ROW_FFN = ("ffn1_w_down", "ffn2_w_down")
_ROW_SQ = ("w_out", "xa_w_q", "xa_w_k", "xa_w_v", "xa_w_o")
_BIG_NAMES = ("ffn1_w_gate", "ffn1_w_up", "ffn1_w_down", "w_in", "w_out", "xa_w_q", "xa_w_k", "xa_w_v", "xa_w_o",
              "ffn2_w_gate", "ffn2_w_up", "ffn2_w_down")


def _ffn_split(fs):
    main = (fs // MXU_WIDTH_V7X) * MXU_WIDTH_V7X
    tail = fs - main
    tail_pad = -(-tail // LANES) * LANES
    assert main > 0 and tail > 0
    return main, tail, tail_pad


def _layout(name, shard_shape):
    r, c = shard_shape
    if name in _COL_FFN:
        main, tail, pad = _ffn_split(c)
        return (r, N_DEV * (main + pad)), [(1, 0, main, (r, main), (0, main)),
                                           (1, N_DEV * main, pad, (r, pad), (main, c))]
    if name in _ROW_FFN:
        main, tail, pad = _ffn_split(r)
        return (N_DEV * (main + pad), c), [(0, 0, main, (main, c), (0, main)),
                                           (0, N_DEV * main, pad, (pad, c), (main, r))]
    if name == "w_in":
        return (r, N_DEV * c), [(1, 0, c, (r, c), (0, c))]
    return (N_DEV * r, c), [(0, 0, r, (r, c), (0, r))]


def _shard_pieces(name, shard):
    out = []
    for axis, _, _, shape, (lo, hi) in _layout(name, shard.shape)[1]:
        part = shard[lo:hi, :] if axis == 0 else shard[:, lo:hi]
        pad = [(0, shape[0] - part.shape[0]), (0, shape[1] - part.shape[1])]
        out.append(jnp.pad(part, pad).astype(BF16))
    return out


def _gather_plan(names, shards):
    srcs, land_shapes, pieces, index = [], [], [], {}
    for li, name in enumerate(names):
        shape2d, parts = _layout(name, shards[name].shape)
        land_shapes.append(jax.ShapeDtypeStruct(shape2d, BF16))
        index[name] = []
        for (axis, base, stride, shape, _), src in zip(parts, _shard_pieces(name, shards[name])):
            index[name].append(len(pieces))
            pieces.append(("gather", len(srcs), li, axis, base, stride, shape))
            srcs.append(src)
    return srcs, land_shapes, pieces, index


def _scatter_plan(names, grads, shard_shapes):
    srcs, land_shapes, pieces, index = [], [], [], {}
    for si, name in enumerate(names):
        _, parts = _layout(name, shard_shapes[name])
        srcs.append(grads[name])
        index[name] = []
        for axis, base, stride, shape, _ in parts:
            index[name].append(len(land_shapes))
            pieces.append(("scatter", si, len(land_shapes), axis, base, stride, shape))
            land_shapes.append(jax.ShapeDtypeStruct((N_DEV,) + shape, grads[name].dtype))
    return srcs, land_shapes, pieces, index


def _small_views(small):
    row = lambda a: a.reshape(1, -1)
    ln = {k: row(small[k]) for k in ("ln1_g", "ln1_b", "ln2_g", "ln2_b", "ln3_g", "ln3_b", "ln4_g", "ln4_b",
                                      "mem_ln_g", "mem_ln_b", "hg_norm_g")}
    sg_w = small["sg_w_s"].reshape(SG_GROUPS, SG_CHUNK, SG_CHUNK)
    sg = dict(logits=jnp.swapaxes(small["hg_lb_logits"], 0, 1),
              g=small["sg_ln_g"].reshape(SG_GROUPS, 1, SG_DIM), b=small["sg_ln_b"].reshape(SG_GROUPS, 1, SG_DIM),
              w=sg_w, wt=jnp.swapaxes(sg_w, 1, 2), bs=small["sg_b_s"].reshape(SG_GROUPS, SG_CHUNK, 1))
    return ln, sg


def _forward(x, mem, target, get_w, small, first_deps=()):
    ln, sg = _small_views(small)
    xb = x.astype(BF16)
    a1, b1, s1 = _ffn_up(xb, get_w("ffn1_w_gate", ()), get_w("ffn1_w_up", ()), "ffn1_up", deps=first_deps)
    h1b, xh1, rs1 = _mm_res_ln(s1, get_w("ffn1_w_down", (s1,)), x, ln["ln1_g"], ln["ln1_b"], 0.5, "ffn1_down_ln")
    proj = _mm_nn(h1b, get_w("w_in", (h1b,)), "mix_in")
    oraw, mix, states = _hgrn_fwd(proj, sg["logits"], ln["hg_norm_g"])
    mix = _sgu_fwd(proj, mix, sg["g"], sg["b"], sg["w"], sg["bs"])
    h2b, xh2, rs2 = _mm_res_ln(mix, get_w("w_out", (mix,)), (xh1, ln["ln1_g"], ln["ln1_b"]), ln["ln2_g"], ln["ln2_b"],
                               1.0, "mix_out_ln")
    mb, mxh, mrs, kb, vb = _mem_kv(mem, ln["mem_ln_g"], ln["mem_ln_b"], get_w("xa_w_k", (h2b,)), get_w("xa_w_v", (h2b,)))
    qb, att = _attn_fwd(h2b, get_w("xa_w_q", (kb,)), kb, vb)
    h3b, xh3, rs3 = _mm_res_ln(att, get_w("xa_w_o", (att,)), (xh2, ln["ln2_g"], ln["ln2_b"]), ln["ln3_g"], ln["ln3_b"],
                               1.0, "attn_out_ln")
    a2, b2, s2 = _ffn_up(h3b, get_w("ffn2_w_gate", (h3b,)), get_w("ffn2_w_up", (h3b,)), "ffn2_up")
    loss, dy4, dy4b, dg4, db4 = _mm_res_ln(s2, get_w("ffn2_w_down", (s2,)), (xh3, ln["ln3_g"], ln["ln3_b"]),
                                           ln["ln4_g"], ln["ln4_b"], 0.5, "ffn2_down_ln_loss", target=target)
    return dict(xb=xb, a1=a1, b1=b1, s1=s1, h1b=h1b, xh1=xh1, rs1=rs1, proj=proj, oraw=oraw, mix=mix, states=states,
                h2b=h2b, xh2=xh2, rs2=rs2, mb=mb, mxh=mxh, mrs=mrs, kb=kb, vb=vb, qb=qb, att=att, h3b=h3b, xh3=xh3,
                rs3=rs3, a2=a2, b2=b2, s2=s2, loss=loss, dy4=dy4, dy4b=dy4b, dg4=dg4, db4=db4)


def _backward(sv, wt, small, send):
    ln, sg = _small_views(small)
    gs = {"ln4_g": sv["dg4"], "ln4_b": sv["db4"]}
    loss, dy4, dy4b = sv["loss"], sv["dy4"], sv["dy4b"]
    g_down2 = _mm_tn(sv["s2"], dy4b, "g_ffn2_down", scale=0.5)
    da2, db2 = _ffn_bwd_act(dy4b, wt["ffn2_w_down"], sv["a2"], sv["b2"], 0.5, "ffn2_bwd_act")
    g_gate2 = _mm_tn(sv["h3b"], da2, "g_ffn2_gate")
    g_up2 = _mm_tn(sv["h3b"], db2, "g_ffn2_up")
    tok = send(("ffn2_w_down", "ffn2_w_gate", "ffn2_w_up"), (g_down2, g_gate2, g_up2))
    dy3, dy3b, gs["ln3_g"], gs["ln3_b"] = _dx_ln(dy4, [(da2, wt["ffn2_w_gate"]), (db2, wt["ffn2_w_up"])],
                                                 (sv["xh3"], sv["rs3"], ln["ln3_g"]), "ffn2_dx_ln", deps=(tok,))

    g_o = _mm_tn(sv["att"], dy3b, "g_xa_o")
    dqb, dk, dv = _attn_bwd(dy3b, wt["xa_w_o"], sv["qb"], sv["kb"], sv["vb"])
    g_q = _mm_tn(sv["h2b"], dqb, "g_xa_q")
    g_k, g_v, gs["mem_ln_g"], gs["mem_ln_b"] = _mem_bwd(dk, dv, sv["mb"], sv["mxh"], sv["mrs"], ln["mem_ln_g"],
                                                        wt["xa_w_k"], wt["xa_w_v"])
    tok = send(("xa_w_o", "xa_w_q", "xa_w_k", "xa_w_v"), (g_o, g_q, g_k, g_v))
    dy2, dy2b, gs["ln2_g"], gs["ln2_b"] = _dx_ln(dy3, [(dqb, wt["xa_w_q"])], (sv["xh2"], sv["rs2"], ln["ln2_g"]),
                                                 "attn_dx_ln", deps=(tok,))

    g_out = _mm_tn(sv["mix"], dy2b, "g_w_out")
    dmix = _mm_nt(dy2b, wt["w_out"], "mix_out_bwd")
    dq, dfz, div, dgg, dlg, dgn = _hgrn_bwd(sv["proj"], sv["oraw"], dmix, sv["states"], sg["logits"], ln["hg_norm_g"])
    du, dvv, gs["sg_ln_g"], gs["sg_ln_b"], gs["sg_w_s"], gs["sg_b_s"] = _sgu_bwd(
        sv["proj"], dmix, sg["g"], sg["b"], sg["w"], sg["wt"], sg["bs"])
    gs["hg_lb_logits"] = jnp.swapaxes(dlg, 0, 1)
    gs["hg_norm_g"] = jnp.sum(dgn, axis=0)
    dproj = jnp.concatenate([dq, dfz, div, dgg, du, dvv], axis=1)
    g_in = _mm_tn(sv["h1b"], dproj, "g_w_in")
    tok = send(("w_out", "w_in"), (g_out, g_in))
    dy1, dy1b, gs["ln1_g"], gs["ln1_b"] = _dx_ln(dy2, [(dproj, wt["w_in"])], (sv["xh1"], sv["rs1"], ln["ln1_g"]),
                                                 "mix_dx_ln", deps=(tok,))

    g_down1 = _mm_tn(sv["s1"], dy1b, "g_ffn1_down", scale=0.5)
    tok = send(("ffn1_w_down",), (g_down1,))
    da1, db1 = _ffn_bwd_act(dy1b, wt["ffn1_w_down"], sv["a1"], sv["b1"], 0.5, "ffn1_bwd_act", deps=(tok,))
    g_gate1 = _mm_tn(sv["xb"], da1, "g_ffn1_gate")
    tok = send(("ffn1_w_gate",), (g_gate1,))
    g_up1 = _mm_tn(sv["xb"], db1, "g_ffn1_up", deps=(tok,))
    tok = send(("ffn1_w_up",), (g_up1,))
    grad_x = _dx_ln(dy1, [(da1, wt["ffn1_w_gate"]), (db1, wt["ffn1_w_up"])], None, "ffn1_dx", deps=(tok,))
    return loss, grad_x, gs


_WEIGHT_NAMES = ("ffn1_w_gate", "ffn1_w_up", "ffn1_w_down", "ln1_g", "ln1_b", "w_in", "hg_lb_logits", "hg_norm_g",
                 "sg_ln_g", "sg_ln_b", "sg_w_s", "sg_b_s", "w_out", "ln2_g", "ln2_b", "mem_ln_g", "mem_ln_b",
                 "xa_w_q", "xa_w_k", "xa_w_v", "xa_w_o", "ln3_g", "ln3_b", "ffn2_w_gate", "ffn2_w_up", "ffn2_w_down",
                 "ln4_g", "ln4_b")
_FIRST = ("ffn1_w_gate", "ffn1_w_up")
_SECOND = ("ffn1_w_down", "w_in", "w_out")
_THIRD = ("xa_w_k", "xa_w_v", "xa_w_q", "xa_w_o", "ffn2_w_gate", "ffn2_w_up", "ffn2_w_down")


def kernel(x, mem, ffn1_w_gate, ffn1_w_up, ffn1_w_down, ln1_g, ln1_b, w_in, hg_lb_logits, hg_norm_g, sg_ln_g, sg_ln_b, sg_w_s, sg_b_s, w_out, ln2_g, ln2_b, mem_ln_g, mem_ln_b, xa_w_q, xa_w_k, xa_w_v, xa_w_o, ln3_g, ln3_b, ffn2_w_gate, ffn2_w_up, ffn2_w_down, ln4_g, ln4_b, loss_target, m_ffn1_w_gate, m_ffn1_w_up, m_ffn1_w_down, m_ln1_g, m_ln1_b, m_w_in, m_hg_lb_logits, m_hg_norm_g, m_sg_ln_g, m_sg_ln_b, m_sg_w_s, m_sg_b_s, m_w_out, m_ln2_g, m_ln2_b, m_mem_ln_g, m_mem_ln_b, m_xa_w_q, m_xa_w_k, m_xa_w_v, m_xa_w_o, m_ln3_g, m_ln3_b, m_ffn2_w_gate, m_ffn2_w_up, m_ffn2_w_down, m_ln4_g, m_ln4_b, v_ffn1_w_gate, v_ffn1_w_up, v_ffn1_w_down, v_ln1_g, v_ln1_b, v_w_in, v_hg_lb_logits, v_hg_norm_g, v_sg_ln_g, v_sg_ln_b, v_sg_w_s, v_sg_b_s, v_w_out, v_ln2_g, v_ln2_b, v_mem_ln_g, v_mem_ln_b, v_xa_w_q, v_xa_w_k, v_xa_w_v, v_xa_w_o, v_ln3_g, v_ln3_b, v_ffn2_w_gate, v_ffn2_w_up, v_ffn2_w_down, v_ln4_g, v_ln4_b):
    args = dict(locals())
    w = {k: args[k] for k in _WEIGHT_NAMES}
    m = {k: args["m_" + k] for k in _WEIGHT_NAMES}
    v = {k: args["v_" + k] for k in _WEIGHT_NAMES}
    shards = {k: w[k][0] for k in _BIG_NAMES}
    shard_shapes = {k: shards[k].shape for k in _BIG_NAMES}
    small = {k: (w[k][0] if k != "hg_lb_logits" else w[k]) for k in _SMALL_NAMES}

    srcs1, shapes1, pieces1, idx1 = _gather_plan(_FIRST, shards)
    lands1 = _place_own(srcs1, shapes1, pieces1, "gather_first_own")
    send1, rsib1, rici1, srcs1, lands1, _ = _gather2_first(srcs1, lands1, pieces1, "gather_first_start")
    sfwd1, rfwd1, lands1 = _gather2_pass(lands1, pieces1, rici1, (), "gather_first_pass")
    lands1 = _gather2_last(srcs1, lands1, pieces1, send1, rsib1, sfwd1, rfwd1, "gather_first_wait")
    wt = dict(zip(_FIRST, lands1))

    rest = _SECOND + _THIRD
    srcs2, shapes2, pieces2, idx2 = _gather_plan(rest, shards)
    lands2 = _place_own(srcs2, shapes2, pieces2, "gather_rest_own")
    groups2 = [list(idx2[k]) for k in rest]
    sems2, srcs2, lands2, tok2 = _comm_start(srcs2, lands2, pieces2, groups2, "gather_rest_start", after=tuple(lands1))
    pending = {k: gi for gi, k in enumerate(rest)}

    def get_w(name, after):
        if name in pending:
            gi = pending.pop(name)
            si = [pieces2[p][1] for p in groups2[gi]]
            sub = [(pieces2[p][0], row, 0) + pieces2[p][3:] for row, p in enumerate(groups2[gi])]
            wt[name] = _comm_wait([srcs2[s] for s in si], [lands2[gi]], sub, list(range(len(sub))), sems2[gi],
                                  after, "gather_wait_" + name)[0]
        return wt[name]

    sv = _forward(x[0], mem[0], loss_target[0], get_w, small, first_deps=(tok2,))

    sent = []

    def send(names, grads):
        srcs, shapes, pieces, idx = _scatter_plan(names, dict(zip(names, grads)), shard_shapes)
        lands = _place_own(srcs, shapes, pieces, "grads_own_%d" % len(sent))
        sems, srcs, lands, tok = _comm_start(srcs, lands, pieces, [list(range(len(pieces)))],
                                             "grads_start_%d" % len(sent))
        sent.append((names, srcs, lands, pieces, idx, sems[0]))
        return tok

    loss, grad_x, gs = _backward(sv, wt, small, send)

    ssrc = list(_pack_small_grads(gs, {k: w[k].shape for k in _SMALL_NAMES}))
    sp = [("scatter", i, i, 0, 0, 0, a.shape) for i, a in enumerate(ssrc)]
    sshape = [jax.ShapeDtypeStruct((N_DEV,) + a.shape, F32) for a in ssrc]
    sl = _place_own(ssrc, sshape, sp, "small_own")
    ssem, ssrc, sl, _ = _comm_start(ssrc, sl, sp, [[0, 1]], "small_start")

    out_g, out_d, out_m, out_v = {}, {}, {}, {}
    after = (grad_x,)
    for n_sent, (names, srcs, lands, pieces, idx, sems) in enumerate(sent):
        lands = _comm_wait(srcs, lands, pieces, list(range(len(pieces))), sems, after, "grads_wait_%d" % n_sent)
        for k in names:
            axis = 1 if (k in _COL_FFN or k == "w_in") else 0
            if k in _COL_FFN:
                res = _adam_sharded([lands[i] for i in idx[k]], w[k][0].T, m[k][0].T, v[k][0].T, axis, "adam_" + k)
                res = [r.T for r in res]
            else:
                res = _adam_sharded([lands[i] for i in idx[k]], w[k][0], m[k][0], v[k][0], axis, "adam_" + k)
            out_g[k], out_d[k], out_m[k], out_v[k] = [r[None] for r in res]
        after = (out_v[names[-1]],)
    sl = _comm_wait(ssrc, sl, sp, [0, 1], ssem[0], after, "small_wait")
    for dst, res in zip((out_g, out_d, out_m, out_v), _adam_small(sl[0], sl[1], w, m, v)):
        dst.update(res)

    loss_all = lax.psum(loss[0, 0], ("x", "y", "c"))
    return (loss_all, grad_x[None], *[out_g[k] for k in _WEIGHT_NAMES], *[out_d[k] for k in _WEIGHT_NAMES],
            *[out_m[k] for k in _WEIGHT_NAMES], *[out_v[k] for k in _WEIGHT_NAMES])
```

```python
import itertools

import jax
import jax.numpy as jnp
import numpy as np
from jax import lax
from jax.experimental import pallas as pl
from jax.experimental.pallas import tpu as pltpu

F32 = jnp.float32
BF16 = jnp.bfloat16

N_DEV = 8
ALPHA = 2.0 ** 0.25
LN_EPS = 1e-5
HG_HEADS = 4
HG_DIM = 128
SG_GROUPS = 4
SG_DIM = 128
SG_CHUNK = 128
X_HEADS = 4
HG_BLOCK = 16
HG_UNROLL = 8
ADAM_LR = 0.001
ADAM_B1 = 0.9
ADAM_B2 = 0.999
ADAM_EPS = 1e-08
ADAM_WD = 0.01
ADAM_STEP = 10
VMEM_LIMIT_V7X = 48 * 1024 * 1024
MXU_WIDTH_V7X = 256
LANES = 128
MESH_ID = pl.DeviceIdType.MESH
ANY = pl.BlockSpec(memory_space=pl.ANY)
HBM = pl.BlockSpec(memory_space=pltpu.HBM)
SEM = pl.BlockSpec(memory_space=pltpu.SEMAPHORE)
DATAFLOW = pltpu.SideEffectType.DATAFLOW_SIDE_EFFECTING


def _params(n_axes):
    return pltpu.CompilerParams(dimension_semantics=("arbitrary",) * n_axes, vmem_limit_bytes=VMEM_LIMIT_V7X)


def _dot(a, b):
    return jnp.dot(a, b, preferred_element_type=F32)


def _dot_nt(a, b):
    return lax.dot_general(a, b, (((1,), (1,)), ((), ())), preferred_element_type=F32)


def _dot_tn(a, b):
    return lax.dot_general(a, b, (((0,), (0,)), ((), ())), preferred_element_type=F32)


def _sigmoid(x):
    return 1.0 / (1.0 + jnp.exp(-x))


def _silu_and_grad(a):
    sig = _sigmoid(a)
    return a * sig, sig * (1.0 + a * (1.0 - sig))


_GELU_C = 0.7978845608028654


def _gelu_and_grad(x):
    inner = _GELU_C * (x + 0.044715 * x * x * x)
    t = jnp.tanh(inner)
    val = 0.5 * x * (1.0 + t)
    grad = 0.5 * (1.0 + t) + 0.5 * x * (1.0 - t * t) * _GELU_C * (1.0 + 3.0 * 0.044715 * x * x)
    return val, grad


def _ln_fwd(y, g, b):
    mu = jnp.mean(y, axis=-1, keepdims=True)
    yc = y - mu
    var = jnp.mean(yc * yc, axis=-1, keepdims=True)
    rstd = lax.rsqrt(var + LN_EPS)
    xhat = yc * rstd
    return xhat * g + b, xhat, rstd


def _ln_bwd(dh, xhat, rstd, g):
    dxh = dh * g
    m1 = jnp.mean(dxh, axis=-1, keepdims=True)
    m2 = jnp.mean(dxh * xhat, axis=-1, keepdims=True)
    dy = rstd * (dxh - m1 - xhat * m2)
    dg = jnp.sum(dh * xhat, axis=0, keepdims=True)
    db = jnp.sum(dh, axis=0, keepdims=True)
    return dy, dg, db


def _mask_dot(mask, x):
    hi = x.astype(BF16)
    lo = (x - hi.astype(F32)).astype(BF16)
    return _dot(mask, hi) + _dot(mask, lo)


def _block_masks(n):
    r = np.arange(n)[:, None]
    c = np.arange(n)[None, :]
    same = (r // HG_BLOCK) == (c // HG_BLOCK)
    return jnp.asarray(np.stack([same & (c <= r), same & (c >= r), same]), BF16)


def _row_tile(t):
    return min(t, 512)


def _col_tile(n):
    for cand in (512, 256, 128):
        if n % cand == 0:
            return cand
    return n


def _resident(w):
    return pl.BlockSpec(w.shape, lambda *_: (0, 0), pipeline_mode=pl.Buffered(1))


def _drop_deps(body, n_in, n_deps):
    if n_deps == 0:
        return body
    return lambda *refs: body(*refs[:n_in], *refs[n_in + n_deps:])


def _ffn_up(hb, wg, wu, name, deps=()):
    t, d = hb.shape
    f = wg.shape[1]
    tm = min(t, 256)
    tn = _col_tile(f)

    def body(h_ref, wg_ref, wu_ref, a_ref, b_ref, s_ref):
        h = h_ref[...]
        for c in range(f // tn):
            cols = slice(c * tn, (c + 1) * tn)
            a = _dot(h, wg_ref[:, cols])
            b = _dot(h, wu_ref[:, cols])
            a_ref[:, cols] = a.astype(BF16)
            b_ref[:, cols] = b.astype(BF16)
            s_ref[:, cols] = (a * _sigmoid(a) * b).astype(BF16)

    act = pl.BlockSpec((tm, f), lambda i: (i, 0))
    return pl.pallas_call(
        _drop_deps(body, 3, len(deps)),
        grid=(t // tm,),
        in_specs=[pl.BlockSpec((tm, d), lambda i: (i, 0)), _resident(wg), _resident(wu)] + [ANY] * len(deps),
        out_specs=[act, act, act],
        out_shape=[jax.ShapeDtypeStruct((t, f), BF16)] * 3,
        compiler_params=_params(1),
        name=name,
    )(hb, wg, wu, *deps)


def _mm_res_ln(lhs, w, res, g, b, coef, name, target=None):
    t, kd = lhs.shape
    d = w.shape[1]
    tm = _row_tile(t)
    nt = t // tm
    from_norm = isinstance(res, tuple)
    n_res = 3 if from_norm else 1

    def body(*refs):
        l_ref, w_ref = refs[:2]
        r_refs = refs[2:2 + n_res]
        g_ref, b_ref = refs[2 + n_res:4 + n_res]
        rest = refs[4 + n_res:]
        prev = r_refs[0][...] * r_refs[1][...] + r_refs[2][...] if from_norm else r_refs[0][...]
        y = ALPHA * prev + coef * _dot(l_ref[...], w_ref[...])
        h, xhat, rstd = _ln_fwd(y, g_ref[...], b_ref[...])
        if target is None:
            hb_ref, xh_ref, rs_ref = rest
            hb_ref[...] = h.astype(BF16)
            xh_ref[...] = xhat
            rs_ref[...] = rstd
            return
        t_ref, loss_ref, dy_ref, dyb_ref, dg_ref, db_ref, lacc = rest
        i = pl.program_id(0)

        @pl.when(i == 0)
        def _():
            lacc[...] = jnp.zeros_like(lacc)
            dg_ref[...] = jnp.zeros_like(dg_ref)
            db_ref[...] = jnp.zeros_like(db_ref)

        err = h - t_ref[...]
        lacc[...] += jnp.sum(err * err, axis=0, keepdims=True)
        dy, dg, db = _ln_bwd(err * (1.0 / d), xhat, rstd, g_ref[...])
        dy_ref[...] = dy
        dyb_ref[...] = dy.astype(BF16)
        dg_ref[...] += dg
        db_ref[...] += db

        @pl.when(i == nt - 1)
        def _():
            loss_ref[...] = jnp.zeros_like(loss_ref) + jnp.sum(lacc[...], axis=1, keepdims=True) * (0.5 / d)

    row = pl.BlockSpec((tm, d), lambda i: (i, 0))
    vec = pl.BlockSpec((1, d), lambda i: (0, 0))
    res_specs = [row, vec, vec] if from_norm else [row]
    res_args = list(res) if from_norm else [res]
    in_specs = [pl.BlockSpec((tm, kd), lambda i: (i, 0)), _resident(w)] + res_specs + [vec, vec]
    args = [lhs, w] + res_args + [g, b]
    if target is None:
        out_specs = [row, row, pl.BlockSpec((tm, 1), lambda i: (i, 0))]
        out_shape = [jax.ShapeDtypeStruct((t, d), BF16), jax.ShapeDtypeStruct((t, d), F32),
                     jax.ShapeDtypeStruct((t, 1), F32)]
        scratch = []
    else:
        in_specs.append(row)
        args.append(target)
        out_specs = [pl.BlockSpec((1, LANES), lambda i: (0, 0)), row, row, vec, vec]
        out_shape = [jax.ShapeDtypeStruct((1, LANES), F32), jax.ShapeDtypeStruct((t, d), F32),
                     jax.ShapeDtypeStruct((t, d), BF16), jax.ShapeDtypeStruct((1, d), F32),
                     jax.ShapeDtypeStruct((1, d), F32)]
        scratch = [pltpu.VMEM((1, d), F32)]
    return pl.pallas_call(
        body,
        grid=(nt,),
        in_specs=in_specs,
        out_specs=out_specs,
        out_shape=out_shape,
        scratch_shapes=scratch,
        compiler_params=_params(1),
        name=name,
    )(*args)


def _mm_nn(lhs, w, name):
    t, kd = lhs.shape
    n = w.shape[1]
    tm = _row_tile(t)
    tn = _col_tile(n)

    def body(l_ref, w_ref, o_ref):
        lhs_v = l_ref[...]
        for c in range(n // tn):
            cols = slice(c * tn, (c + 1) * tn)
            o_ref[:, cols] = _dot(lhs_v, w_ref[:, cols])

    return pl.pallas_call(
        body,
        grid=(t // tm,),
        in_specs=[pl.BlockSpec((tm, kd), lambda i: (i, 0)), _resident(w)],
        out_specs=pl.BlockSpec((tm, n), lambda i: (i, 0)),
        out_shape=jax.ShapeDtypeStruct((t, n), F32),
        compiler_params=_params(1),
        name=name,
    )(lhs, w)


def _lower_bound(lg):
    m = jnp.max(lg, axis=0, keepdims=True)
    e = jnp.exp(lg - m)
    return e[0:1, :] / jnp.sum(e, axis=0, keepdims=True)


def _forget_terms(fz, lb):
    e = jnp.exp(-jnp.abs(fz))
    r = 1.0 / (1.0 + e)
    pos = fz >= 0.0
    sig = jnp.where(pos, r, e * r)
    nsig = jnp.where(pos, e * r, r)
    f = lb + (1.0 - lb) * sig
    k = (1.0 - lb) * nsig
    return sig, nsig, f, k


def _hg_tile(t):
    return min(t, 256)


HG_HALF = HG_BLOCK // 2
NEG_BIG = -1e30


def _halves(a):
    return a[:HG_HALF, :], a[HG_HALF:, :]


def _causal_halves(s):
    return (0, 1) if s < HG_HALF else (1,)


def _decay_from(b_half, b_s, s, h, tidx):
    first = s - h * HG_HALF
    diff = b_half - b_s
    if first > 0:
        diff = jnp.where(tidx >= first, diff, NEG_BIG)
    return jnp.exp(diff)


def _hgrn_fwd(proj, logits, gn):
    t = proj.shape[0]
    ct = _hg_tile(t)
    nct = t // ct
    nblk = ct // HG_BLOCK
    nh = HG_HEADS

    def body(q_ref, fz_ref, iv_ref, gg_ref, lg_ref, gn_ref, mask_ref, oraw_ref, oa_ref, st_ref,
             state, qt_s, kt_s, k_s, b_s, dec_s):
        c = pl.program_id(1)

        @pl.when(c == 0)
        def _():
            state[...] = jnp.zeros_like(state)

        lb = _lower_bound(lg_ref[...])
        q = q_ref[...]
        _, _, f, k = _forget_terms(fz_ref[...], lb)
        logf = jnp.log(f)
        b = _mask_dot(mask_ref[0], logf)
        bend = _mask_dot(mask_ref[2], logf)
        qt_s[...] = (q * jnp.exp(b)).astype(BF16)
        kt_s[...] = (k * jnp.exp(bend - b)).astype(BF16)
        k_s[...] = k
        b_s[...] = b
        dec_s[...] = jnp.exp(bend)
        tidx = lax.broadcasted_iota(jnp.int32, (HG_HALF, HG_DIM), 0)

        def blk(i, carry):
            r0 = pl.multiple_of(i * HG_BLOCK, HG_BLOCK)
            rows = pl.ds(r0, HG_BLOCK)
            st = state[...]
            st_ref[i] = st
            v = iv_ref[rows, :]
            qq = q_ref[rows, :]
            kk = k_s[rows, :]
            bb = b_s[rows, :]
            o = list(_halves(_dot_nt(qt_s[rows, :], st.astype(BF16))))
            qh, bh = _halves(qq), _halves(bb)
            for s in range(HG_BLOCK):
                ks, vs = kk[s:s + 1, :], v[s:s + 1, :]
                for h in _causal_halves(s):
                    e = _decay_from(bh[h], bb[s:s + 1, :], s, h, tidx)
                    acol = jnp.sum(qh[h] * (ks * e), axis=1, keepdims=True)
                    o[h] = o[h] + acol * vs
            oraw_ref[rows, :] = jnp.concatenate(o, axis=0)
            state[...] = st * dec_s[pl.ds(r0, 1), :] + _dot_tn(v.astype(BF16), kt_s[rows, :])
            return carry

        lax.fori_loop(0, nblk, blk, 0, unroll=2 * HG_UNROLL)
        oraw = oraw_ref[...]
        r = lax.rsqrt(jnp.mean(oraw * oraw, axis=-1, keepdims=True) + LN_EPS)
        gg = gg_ref[...]
        oa_ref[...] = (oraw * r * gn_ref[...] * gg * _sigmoid(gg)).astype(BF16)

    def slab(off):
        return pl.BlockSpec((ct, HG_DIM), lambda h, c: (c, off + h))

    out_slab = pl.BlockSpec((ct, HG_DIM), lambda h, c: (c, h))
    return pl.pallas_call(
        body,
        grid=(nh, nct),
        in_specs=[slab(0), slab(nh), slab(2 * nh), slab(3 * nh),
                  pl.BlockSpec((None, 2, HG_DIM), lambda h, c: (h, 0, 0)),
                  pl.BlockSpec((1, HG_DIM), lambda h, c: (0, 0)),
                  pl.BlockSpec((3, ct, ct), lambda h, c: (0, 0, 0))],
        out_specs=[out_slab, out_slab, pl.BlockSpec((None, nblk, HG_DIM, HG_DIM), lambda h, c: (h, c, 0, 0))],
        out_shape=[jax.ShapeDtypeStruct((t, nh * HG_DIM), F32),
                   jax.ShapeDtypeStruct((t, (nh + SG_GROUPS) * HG_DIM), BF16),
                   jax.ShapeDtypeStruct((nh, t // HG_BLOCK, HG_DIM, HG_DIM), F32)],
        scratch_shapes=[pltpu.VMEM((HG_DIM, HG_DIM), F32), pltpu.VMEM((ct, HG_DIM), BF16),
                        pltpu.VMEM((ct, HG_DIM), BF16), pltpu.VMEM((ct, HG_DIM), F32),
                        pltpu.VMEM((ct, HG_DIM), F32), pltpu.VMEM((ct, HG_DIM), F32)],
        compiler_params=_params(2),
        name="hgrn_fwd",
    )(proj, proj, proj, proj, logits, gn, _block_masks(ct))


def _sg_tile(t):
    return min(t, 512)


def _sgu_chunk_fwd(u, v, ln_g, ln_b, wm, bs):
    ua, dua = _gelu_and_grad(u)
    va, dva = _gelu_and_grad(v)
    vn, xhat, rstd = _ln_fwd(va, ln_g, ln_b)
    s = _dot(wm, vn.astype(BF16)) + bs
    return ua, dua, dva, vn, xhat, rstd, s


def _tril_weight(w_ref):
    n = SG_CHUNK
    r = lax.broadcasted_iota(jnp.int32, (n, n), 0)
    c = lax.broadcasted_iota(jnp.int32, (n, n), 1)
    return jnp.where(c <= r, w_ref[...], 0.0)


def _sgu_fwd(proj, mix, ln_g, ln_b, w_s, b_col):
    t = proj.shape[0]
    ct = _sg_tile(t)
    ng = SG_GROUPS
    off_u = 4 * HG_HEADS
    off_v = off_u + ng

    def body(u_ref, v_ref, g_ref, b_ref, w_ref, bs_ref, mix_ref, o_ref):
        del mix_ref
        wm = _tril_weight(w_ref).astype(BF16)
        for n in range(ct // SG_CHUNK):
            rows = slice(n * SG_CHUNK, (n + 1) * SG_CHUNK)
            ua, _, _, _, _, _, s = _sgu_chunk_fwd(u_ref[rows, :], v_ref[rows, :], g_ref[...], b_ref[...], wm, bs_ref[...])
            o_ref[rows, :] = (ua * s).astype(BF16)

    vec = pl.BlockSpec((None, 1, SG_DIM), lambda g, c: (g, 0, 0))
    return pl.pallas_call(
        body,
        grid=(ng, t // ct),
        in_specs=[pl.BlockSpec((ct, SG_DIM), lambda g, c: (c, off_u + g)),
                  pl.BlockSpec((ct, SG_DIM), lambda g, c: (c, off_v + g)), vec, vec,
                  pl.BlockSpec((None, SG_CHUNK, SG_CHUNK), lambda g, c: (g, 0, 0)),
                  pl.BlockSpec((None, SG_CHUNK, 1), lambda g, c: (g, 0, 0)), ANY],
        out_specs=pl.BlockSpec((ct, SG_DIM), lambda g, c: (c, HG_HEADS + g)),
        out_shape=jax.ShapeDtypeStruct(mix.shape, mix.dtype),
        input_output_aliases={6: 0},
        compiler_params=_params(2),
        name="sgu_fwd",
    )(proj, proj, ln_g, ln_b, w_s, b_col, mix)


def _mem_kv(mem, g, b, wk, wv):
    m_len, d = mem.shape

    def body(m_ref, g_ref, b_ref, wk_ref, wv_ref, mb_ref, xh_ref, rs_ref, k_ref, v_ref):
        m, xhat, rstd = _ln_fwd(m_ref[...], g_ref[...], b_ref[...])
        mb = m.astype(BF16)
        mb_ref[...] = mb
        xh_ref[...] = xhat
        rs_ref[...] = rstd
        k_ref[...] = _dot(mb, wk_ref[...]).astype(BF16)
        v_ref[...] = _dot(mb, wv_ref[...]).astype(BF16)

    return pl.pallas_call(
        body,
        out_shape=[jax.ShapeDtypeStruct((m_len, d), BF16), jax.ShapeDtypeStruct((m_len, d), F32),
                   jax.ShapeDtypeStruct((m_len, 1), F32), jax.ShapeDtypeStruct((m_len, d), BF16),
                   jax.ShapeDtypeStruct((m_len, d), BF16)],
        compiler_params=pltpu.CompilerParams(vmem_limit_bytes=VMEM_LIMIT_V7X),
        name="mem_kv",
    )(mem, g, b, wk, wv)


def _softmax_rows(s):
    m = jnp.max(s, axis=-1, keepdims=True)
    p = jnp.exp(s - m)
    return p / jnp.sum(p, axis=-1, keepdims=True)


def _attn_fwd(hb, wq, kb, vb):
    t, d = hb.shape
    tm = _row_tile(t)
    dh = d // X_HEADS
    scale = dh ** -0.5

    def body(h_ref, wq_ref, k_ref, v_ref, q_ref, o_ref):
        q = _dot(h_ref[...], wq_ref[...]).astype(BF16)
        q_ref[...] = q
        for hd in range(X_HEADS):
            sl = slice(hd * dh, (hd + 1) * dh)
            p = _softmax_rows(_dot_nt(q[:, sl], k_ref[:, sl]) * scale)
            o_ref[:, sl] = _dot(p.astype(BF16), v_ref[:, sl]).astype(BF16)

    row = pl.BlockSpec((tm, d), lambda i: (i, 0))
    full = lambda a: pl.BlockSpec(a.shape, lambda i: (0, 0))
    return pl.pallas_call(
        body,
        grid=(t // tm,),
        in_specs=[row, full(wq), full(kb), full(vb)],
        out_specs=[row, row],
        out_shape=[jax.ShapeDtypeStruct((t, d), BF16), jax.ShapeDtypeStruct((t, d), BF16)],
        compiler_params=_params(1),
        name="attn_fwd",
    )(hb, wq, kb, vb)


def _ffn_bwd_act(dyb, wd, a, b, coef, name, deps=()):
    t, d = dyb.shape
    f = wd.shape[0]
    tm = _row_tile(t)
    tn = _col_tile(f)

    def body(dy_ref, wd_ref, a_ref, b_ref, da_ref, db_ref):
        dy = dy_ref[...]
        for c in range(f // tn):
            cols = slice(c * tn, (c + 1) * tn)
            ds = _dot_nt(dy, wd_ref[cols, :]) * coef
            silu, dsilu = _silu_and_grad(a_ref[:, cols].astype(F32))
            da_ref[:, cols] = (ds * b_ref[:, cols].astype(F32) * dsilu).astype(BF16)
            db_ref[:, cols] = (ds * silu).astype(BF16)

    act = pl.BlockSpec((tm, f), lambda i: (i, 0))
    return pl.pallas_call(
        _drop_deps(body, 4, len(deps)),
        grid=(t // tm,),
        in_specs=[pl.BlockSpec((tm, d), lambda i: (i, 0)), _resident(wd), act, act] + [ANY] * len(deps),
        out_specs=[act, act],
        out_shape=[jax.ShapeDtypeStruct((t, f), BF16), jax.ShapeDtypeStruct((t, f), BF16)],
        compiler_params=_params(1),
        name=name,
    )(dyb, wd, a, b, *deps)


def _ffn_bwd_fused(dy, dyb, wd, wg, wu, a, b, coef, ln, name):
    t, d = dy.shape
    f = wd.shape[0]
    tm = min(t, 256)
    tn = _col_tile(f)

    def body(dy_ref, dyb_ref, wd_ref, wg_ref, wu_ref, a_ref, b_ref, xh_ref, rs_ref, g_ref,
             da_ref, db_ref, dyo_ref, dyob_ref, dg_ref, dbl_ref):
        dyb_v = dyb_ref[...]
        dh = ALPHA * dy_ref[...]
        for c in range(f // tn):
            cols = slice(c * tn, (c + 1) * tn)
            ds = _dot_nt(dyb_v, wd_ref[cols, :]) * coef
            silu, dsilu = _silu_and_grad(a_ref[:, cols].astype(F32))
            da = (ds * b_ref[:, cols].astype(F32) * dsilu).astype(BF16)
            db = (ds * silu).astype(BF16)
            da_ref[:, cols] = da
            db_ref[:, cols] = db
            dh = dh + _dot_nt(da, wg_ref[:, cols]) + _dot_nt(db, wu_ref[:, cols])

        @pl.when(pl.program_id(0) == 0)
        def _():
            dg_ref[...] = jnp.zeros_like(dg_ref)
            dbl_ref[...] = jnp.zeros_like(dbl_ref)

        dyp, dg, dbl = _ln_bwd(dh, xh_ref[...], rs_ref[...], g_ref[...])
        dyo_ref[...] = dyp
        dyob_ref[...] = dyp.astype(BF16)
        dg_ref[...] += dg
        dbl_ref[...] += dbl

    row = pl.BlockSpec((tm, d), lambda i: (i, 0))
    act = pl.BlockSpec((tm, f), lambda i: (i, 0))
    vec = pl.BlockSpec((1, d), lambda i: (0, 0))
    return pl.pallas_call(
        body,
        grid=(t // tm,),
        in_specs=[row, row, _resident(wd), _resident(wg), _resident(wu), act, act, row,
                  pl.BlockSpec((tm, 1), lambda i: (i, 0)), vec],
        out_specs=[act, act, row, row, vec, vec],
        out_shape=[jax.ShapeDtypeStruct((t, f), BF16), jax.ShapeDtypeStruct((t, f), BF16),
                   jax.ShapeDtypeStruct((t, d), F32), jax.ShapeDtypeStruct((t, d), BF16),
                   jax.ShapeDtypeStruct((1, d), F32), jax.ShapeDtypeStruct((1, d), F32)],
        compiler_params=_params(1),
        name=name,
    )(dy, dyb, wd, wg, wu, a, b, *ln)


def _mm_tn(a, b, name, scale=1.0, deps=()):
    t, m = a.shape
    n = b.shape[1]
    tt = _row_tile(t)
    nt = t // tt
    tm_o = m // 2 if (m > n and m * n > 2 ** 21) else m
    tn_o = n // 2 if (n > m and m * n > 2 ** 21) else n

    def body(a_ref, b_ref, o_ref, acc):
        k = pl.program_id(2)

        @pl.when(k == 0)
        def _():
            acc[...] = jnp.zeros_like(acc)

        acc[...] += _dot_tn(a_ref[...], b_ref[...])

        @pl.when(k == nt - 1)
        def _():
            o_ref[...] = (acc[...] * scale).astype(BF16)

    return pl.pallas_call(
        _drop_deps(body, 2, len(deps)),
        grid=(m // tm_o, n // tn_o, nt),
        in_specs=[pl.BlockSpec((tt, tm_o), lambda i, j, k: (k, i)), pl.BlockSpec((tt, tn_o), lambda i, j, k: (k, j))]
        + [ANY] * len(deps),
        out_specs=pl.BlockSpec((tm_o, tn_o), lambda i, j, k: (i, j)),
        out_shape=jax.ShapeDtypeStruct((m, n), BF16),
        scratch_shapes=[pltpu.VMEM((tm_o, tn_o), F32)],
        compiler_params=_params(3),
        name=name,
    )(a, b, *deps)


def _mm_nt(lhs, w, name):
    t, d = lhs.shape
    kd = w.shape[0]
    tm = _row_tile(t)

    def body(l_ref, w_ref, o_ref):
        o_ref[...] = _dot_nt(l_ref[...], w_ref[...])

    return pl.pallas_call(
        body,
        grid=(t // tm,),
        in_specs=[pl.BlockSpec((tm, d), lambda i: (i, 0)), _resident(w)],
        out_specs=pl.BlockSpec((tm, kd), lambda i: (i, 0)),
        out_shape=jax.ShapeDtypeStruct((t, kd), F32),
        compiler_params=_params(1),
        name=name,
    )(lhs, w)


def _dx_ln(dy, pairs, ln, name, deps=()):
    t, d = dy.shape
    npair = len(pairs)
    tm = min(t, 512 // npair)
    nt = t // tm
    n_in = 1 + 2 * npair + (3 if ln is not None else 0)

    def body(*refs):
        dy_ref = refs[0]
        pr = refs[1:1 + 2 * npair]
        pos = 1 + 2 * npair
        dh = ALPHA * dy_ref[...]
        for p in range(npair):
            dh = dh + _dot_nt(pr[2 * p][...], pr[2 * p + 1][...])
        if ln is not None:
            xh_ref, rs_ref, g_ref = refs[pos:pos + 3]
            dyo_ref, dyb_ref, dg_ref, db_ref = refs[pos + 3:pos + 7]

            @pl.when(pl.program_id(0) == 0)
            def _():
                dg_ref[...] = jnp.zeros_like(dg_ref)
                db_ref[...] = jnp.zeros_like(db_ref)

            dyp, dg, db = _ln_bwd(dh, xh_ref[...], rs_ref[...], g_ref[...])
            dyo_ref[...] = dyp
            dyb_ref[...] = dyp.astype(BF16)
            dg_ref[...] += dg
            db_ref[...] += db
        else:
            refs[pos][...] = dh

    row = pl.BlockSpec((tm, d), lambda i: (i, 0))
    vec = pl.BlockSpec((1, d), lambda i: (0, 0))
    in_specs = [row]
    args = [dy]
    for lhs, w in pairs:
        in_specs += [pl.BlockSpec((tm, lhs.shape[1]), lambda i: (i, 0)), _resident(w)]
        args += [lhs, w]
    if ln is not None:
        in_specs += [row, pl.BlockSpec((tm, 1), lambda i: (i, 0)), vec]
        args += list(ln)
        out_specs = [row, row, vec, vec]
        out_shape = [jax.ShapeDtypeStruct((t, d), F32), jax.ShapeDtypeStruct((t, d), BF16),
                     jax.ShapeDtypeStruct((1, d), F32), jax.ShapeDtypeStruct((1, d), F32)]
    else:
        out_specs = row
        out_shape = jax.ShapeDtypeStruct((t, d), F32)
    return pl.pallas_call(
        _drop_deps(body, n_in, len(deps)),
        grid=(nt,),
        in_specs=in_specs + [ANY] * len(deps),
        out_specs=out_specs,
        out_shape=out_shape,
        compiler_params=_params(1),
        name=name,
    )(*args, *deps)


def _hgrn_bwd(proj, oraw, dmix, states, logits, gn):
    t = proj.shape[0]
    ct = _hg_tile(t)
    nct = t // ct
    nblk = ct // HG_BLOCK
    nh = HG_HEADS

    def body(q_ref, fz_ref, iv_ref, gg_ref, or_ref, do_ref, st_ref, lg_ref, gn_ref, mask_ref,
             dq_ref, dfz_ref, div_ref, dgg_ref, dlg_ref, dgn_ref,
             dstate, qt_s, kt_s, k_s, b_s, eb_s, ekb_s, dec_s, dor_s, dbl_s, gr_s, dk_s, dlb_acc):
        c = pl.program_id(1)

        @pl.when(c == 0)
        def _():
            dstate[...] = jnp.zeros_like(dstate)
            dlb_acc[...] = jnp.zeros_like(dlb_acc)
            dgn_ref[...] = jnp.zeros_like(dgn_ref)

        lb = _lower_bound(lg_ref[...])
        q = q_ref[...]
        sig, nsig, f, k = _forget_terms(fz_ref[...], lb)
        logf = jnp.log(f)
        b = _mask_dot(mask_ref[0], logf)
        bend = _mask_dot(mask_ref[2], logf)
        eb = jnp.exp(b)
        ekb = jnp.exp(bend - b)
        qt_s[...] = (q * eb).astype(BF16)
        kt_s[...] = (k * ekb).astype(BF16)
        k_s[...] = k
        b_s[...] = b
        eb_s[...] = eb
        ekb_s[...] = ekb
        dec_s[...] = jnp.exp(bend)
        oraw = or_ref[...]
        r = lax.rsqrt(jnp.mean(oraw * oraw, axis=-1, keepdims=True) + LN_EPS)
        on = oraw * r
        gg = gg_ref[...]
        silu, dsilu = _silu_and_grad(gg)
        doa = do_ref[...]
        gnv = gn_ref[...]
        dgg_ref[...] = (doa * on * gnv * dsilu).astype(BF16)
        dyn = doa * silu
        dgn_ref[...] += jnp.sum(dyn * on, axis=0, keepdims=True)
        don = dyn * gnv
        dor_s[...] = r * (don - on * jnp.mean(don * on, axis=-1, keepdims=True))
        tidx = lax.broadcasted_iota(jnp.int32, (HG_HALF, HG_DIM), 0)

        def blk(ii, carry):
            i = nblk - 1 - ii
            r0 = pl.multiple_of(i * HG_BLOCK, HG_BLOCK)
            rows = pl.ds(r0, HG_BLOCK)
            st = st_ref[i]
            dst = dstate[...]
            dstb = dst.astype(BF16)
            do = dor_s[rows, :]
            dob = do.astype(BF16)
            v = iv_ref[rows, :]
            vb = v.astype(BF16)
            qq = q_ref[rows, :]
            kk = k_s[rows, :]
            bb = b_s[rows, :]
            qt = qt_s[rows, :]
            kt = kt_s[rows, :]
            dec = dec_s[pl.ds(r0, 1), :]
            dkt = _dot(vb, dstb)
            dq = _dot(dob, st.astype(BF16)) * eb_s[rows, :]
            dk = dkt * ekb_s[rows, :]
            dv = _dot_nt(kt, dstb)
            gend = jnp.sum(kk * dk, axis=0, keepdims=True) + dec * jnp.sum(dst * st, axis=0, keepdims=True)
            qh, bh, doh = _halves(qq), _halves(bb), _halves(do)
            dqh, dkh, dvh = list(_halves(dq)), list(_halves(dk)), list(_halves(dv))
            for s in range(HG_BLOCK):
                ks, vs = kk[s:s + 1, :], v[s:s + 1, :]
                dk_part = dv_part = None
                for h in _causal_halves(s):
                    e = _decay_from(bh[h], bb[s:s + 1, :], s, h, tidx)
                    ke = ks * e
                    acol = jnp.sum(qh[h] * ke, axis=1, keepdims=True)
                    dacol = jnp.sum(doh[h] * vs, axis=1, keepdims=True)
                    dqh[h] = dqh[h] + dacol * ke
                    pk = dacol * (qh[h] * e)
                    pv = acol * doh[h]
                    dk_part = pk if dk_part is None else dk_part + pk
                    dv_part = pv if dv_part is None else dv_part + pv
                hs, row = divmod(s, HG_HALF)
                dkh[hs] = dkh[hs] + jnp.where(tidx == row, jnp.sum(dk_part, axis=0, keepdims=True), 0.0)
                dvh[hs] = dvh[hs] + jnp.where(tidx == row, jnp.sum(dv_part, axis=0, keepdims=True), 0.0)
            dq = jnp.concatenate(dqh, axis=0)
            dk = jnp.concatenate(dkh, axis=0)
            dv = jnp.concatenate(dvh, axis=0)
            dq_ref[rows, :] = dq.astype(BF16)
            div_ref[rows, :] = dv.astype(BF16)
            dk_s[rows, :] = dk
            dbl_s[rows, :] = qq * dq - kk * dk
            gr_s[rows, :] = jnp.zeros((HG_BLOCK, HG_DIM), F32) + gend
            dstate[...] = dst * dec + _dot_tn(dob, qt)
            return carry

        lax.fori_loop(0, nblk, blk, 0, unroll=2 * HG_UNROLL)
        dlogf = _mask_dot(mask_ref[1], dbl_s[...]) + gr_s[...]
        dk = dk_s[...]
        dfz_ref[...] = ((dlogf / f - dk) * ((1.0 - lb) * sig * nsig)).astype(BF16)
        dlb_acc[...] += jnp.sum((dlogf / f - dk) * nsig, axis=0, keepdims=True)

        @pl.when(c == nct - 1)
        def _():
            dl0 = dlb_acc[...] * lb * (1.0 - lb)
            layer = lax.broadcasted_iota(jnp.int32, (2, HG_DIM), 0)
            dlg_ref[...] = jnp.where(layer == 0, dl0, -dl0)

    def slab(off):
        return pl.BlockSpec((ct, HG_DIM), lambda h, c: (nct - 1 - c, off + h))

    out_slab = pl.BlockSpec((ct, HG_DIM), lambda h, c: (nct - 1 - c, h))
    tile_f32 = pltpu.VMEM((ct, HG_DIM), F32)
    tile_b16 = pltpu.VMEM((ct, HG_DIM), BF16)
    slab_shape = jax.ShapeDtypeStruct((t, nh * HG_DIM), BF16)
    return pl.pallas_call(
        body,
        grid=(nh, nct),
        in_specs=[slab(0), slab(nh), slab(2 * nh), slab(3 * nh), slab(0), slab(0),
                  pl.BlockSpec((None, nblk, HG_DIM, HG_DIM), lambda h, c: (h, nct - 1 - c, 0, 0)),
                  pl.BlockSpec((None, 2, HG_DIM), lambda h, c: (h, 0, 0)),
                  pl.BlockSpec((1, HG_DIM), lambda h, c: (0, 0)),
                  pl.BlockSpec((3, ct, ct), lambda h, c: (0, 0, 0))],
        out_specs=[out_slab, out_slab, out_slab, out_slab,
                   pl.BlockSpec((None, 2, HG_DIM), lambda h, c: (h, 0, 0)),
                   pl.BlockSpec((None, 1, HG_DIM), lambda h, c: (h, 0, 0))],
        out_shape=[slab_shape, slab_shape, slab_shape, slab_shape,
                   jax.ShapeDtypeStruct((nh, 2, HG_DIM), F32), jax.ShapeDtypeStruct((nh, 1, HG_DIM), F32)],
        scratch_shapes=[pltpu.VMEM((HG_DIM, HG_DIM), F32), tile_b16, tile_b16, tile_f32, tile_f32, tile_f32, tile_f32,
                        tile_f32, tile_f32, tile_f32, tile_f32, tile_f32, pltpu.VMEM((1, HG_DIM), F32)],
        compiler_params=_params(2),
        name="hgrn_bwd",
    )(proj, proj, proj, proj, oraw, dmix, states, logits, gn, _block_masks(ct))


def _sgu_bwd(proj, dmix, ln_g, ln_b, w_s, w_t, b_col):
    t = proj.shape[0]
    ct = _sg_tile(t)
    nct = t // ct
    ng = SG_GROUPS
    off_u = 4 * HG_HEADS
    off_v = off_u + ng
    n = SG_CHUNK

    def body(u_ref, v_ref, do_ref, g_ref, b_ref, w_ref, wt_ref, bs_ref, du_ref, dv_ref, dg_ref, db_ref, dw_ref, dbs_ref):
        c = pl.program_id(1)

        @pl.when(c == 0)
        def _():
            dg_ref[...] = jnp.zeros_like(dg_ref)
            db_ref[...] = jnp.zeros_like(db_ref)
            dw_ref[...] = jnp.zeros_like(dw_ref)
            dbs_ref[...] = jnp.zeros_like(dbs_ref)

        r = lax.broadcasted_iota(jnp.int32, (n, n), 0)
        cc = lax.broadcasted_iota(jnp.int32, (n, n), 1)
        wm = jnp.where(cc <= r, w_ref[...], 0.0).astype(BF16)
        wmt = jnp.where(r <= cc, wt_ref[...], 0.0).astype(BF16)
        for ci in range(ct // n):
            rows = slice(ci * n, (ci + 1) * n)
            ua, dua, dva, vn, xhat, rstd, s = _sgu_chunk_fwd(u_ref[rows, :], v_ref[rows, :], g_ref[...], b_ref[...],
                                                             wm, bs_ref[...])
            do = do_ref[rows, :]
            du_ref[rows, :] = (do * s * dua).astype(BF16)
            ds = do * ua
            dsb = ds.astype(BF16)
            dbs_ref[...] += jnp.sum(ds, axis=1, keepdims=True)
            dw_ref[...] += _dot_nt(dsb, vn.astype(BF16))
            dvn = _dot(wmt, dsb)
            dva_in, dg, db = _ln_bwd(dvn, xhat, rstd, g_ref[...])
            dg_ref[...] += dg
            db_ref[...] += db
            dv_ref[rows, :] = (dva_in * dva).astype(BF16)

        @pl.when(c == nct - 1)
        def _():
            dw_ref[...] = jnp.where(cc <= r, dw_ref[...], 0.0)

    vec = pl.BlockSpec((None, 1, SG_DIM), lambda g, c: (g, 0, 0))
    mat = pl.BlockSpec((None, n, n), lambda g, c: (g, 0, 0))
    col = pl.BlockSpec((None, n, 1), lambda g, c: (g, 0, 0))
    out_slab = pl.BlockSpec((ct, SG_DIM), lambda g, c: (c, g))
    return pl.pallas_call(
        body,
        grid=(ng, nct),
        in_specs=[pl.BlockSpec((ct, SG_DIM), lambda g, c: (c, off_u + g)),
                  pl.BlockSpec((ct, SG_DIM), lambda g, c: (c, off_v + g)),
                  pl.BlockSpec((ct, SG_DIM), lambda g, c: (c, ng + g)), vec, vec, mat, mat, col],
        out_specs=[out_slab, out_slab, vec, vec, mat, col],
        out_shape=[jax.ShapeDtypeStruct((t, ng * SG_DIM), BF16), jax.ShapeDtypeStruct((t, ng * SG_DIM), BF16),
                   jax.ShapeDtypeStruct((ng, 1, SG_DIM), F32), jax.ShapeDtypeStruct((ng, 1, SG_DIM), F32),
                   jax.ShapeDtypeStruct((ng, n, n), F32), jax.ShapeDtypeStruct((ng, n, 1), F32)],
        compiler_params=_params(2),
        name="sgu_bwd",
    )(proj, proj, dmix, ln_g, ln_b, w_s, w_t, b_col)


def _attn_bwd(dyb, wo, qb, kb, vb):
    t, d = dyb.shape
    m_len = kb.shape[0]
    tm = _row_tile(t)
    dh = d // X_HEADS
    scale = dh ** -0.5

    def body(dy_ref, wo_ref, q_ref, k_ref, v_ref, dq_ref, dk_ref, dv_ref):
        i = pl.program_id(0)

        @pl.when(i == 0)
        def _():
            dk_ref[...] = jnp.zeros_like(dk_ref)
            dv_ref[...] = jnp.zeros_like(dv_ref)

        do = _dot_nt(dy_ref[...], wo_ref[...]).astype(BF16)
        for hd in range(X_HEADS):
            sl = slice(hd * dh, (hd + 1) * dh)
            qh = q_ref[:, sl]
            p = _softmax_rows(_dot_nt(qh, k_ref[:, sl]) * scale)
            doh = do[:, sl]
            dp = _dot_nt(doh, v_ref[:, sl])
            ds = (p * (dp - jnp.sum(dp * p, axis=-1, keepdims=True)) * scale).astype(BF16)
            dq_ref[:, sl] = _dot(ds, k_ref[:, sl]).astype(BF16)
            dk_ref[:, sl] += _dot_tn(ds, qh)
            dv_ref[:, sl] += _dot_tn(p.astype(BF16), doh)

    row = pl.BlockSpec((tm, d), lambda i: (i, 0))
    full = lambda a: pl.BlockSpec(a.shape, lambda i: (0, 0))
    kv = pl.BlockSpec((m_len, d), lambda i: (0, 0))
    return pl.pallas_call(
        body,
        grid=(t // tm,),
        in_specs=[row, full(wo), row, full(kb), full(vb)],
        out_specs=[row, kv, kv],
        out_shape=[jax.ShapeDtypeStruct((t, d), BF16), jax.ShapeDtypeStruct((m_len, d), F32),
                   jax.ShapeDtypeStruct((m_len, d), F32)],
        compiler_params=_params(1),
        name="attn_bwd",
    )(dyb, wo, qb, kb, vb)


def _mem_bwd(dk, dv, mb, xhat, rstd, g, wk, wv):
    m_len, d = dk.shape

    def body(dk_ref, dv_ref, mb_ref, xh_ref, rs_ref, g_ref, wk_ref, wv_ref, gwk_ref, gwv_ref, dg_ref, db_ref):
        dkb = dk_ref[...].astype(BF16)
        dvb = dv_ref[...].astype(BF16)
        mb_v = mb_ref[...]
        gwk_ref[...] = _dot_tn(mb_v, dkb).astype(BF16)
        gwv_ref[...] = _dot_tn(mb_v, dvb).astype(BF16)
        dm = _dot_nt(dkb, wk_ref[...]) + _dot_nt(dvb, wv_ref[...])
        _, dg, db = _ln_bwd(dm, xh_ref[...], rs_ref[...], g_ref[...])
        dg_ref[...] = dg
        db_ref[...] = db

    return pl.pallas_call(
        body,
        out_shape=[jax.ShapeDtypeStruct((d, d), BF16), jax.ShapeDtypeStruct((d, d), BF16),
                   jax.ShapeDtypeStruct((1, d), F32), jax.ShapeDtypeStruct((1, d), F32)],
        compiler_params=pltpu.CompilerParams(vmem_limit_bytes=VMEM_LIMIT_V7X),
        name="mem_bwd",
    )(dk, dv, mb, xhat, rstd, g, wk, wv)


def _adamw(w, g, m, v):
    m = ADAM_B1 * m + (1.0 - ADAM_B1) * g
    v = ADAM_B2 * v + (1.0 - ADAM_B2) * (g * g)
    m_hat = m / (1.0 - ADAM_B1 ** ADAM_STEP)
    v_hat = v / (1.0 - ADAM_B2 ** ADAM_STEP)
    delta = -ADAM_LR * (m_hat / (jnp.sqrt(v_hat) + ADAM_EPS) + ADAM_WD * w)
    return delta, m, v


def _slot_sum(ref):
    g = ref[0].astype(F32)
    for s in range(1, N_DEV):
        g = g + ref[s].astype(F32)
    return g


def _adam_sharded(lands, w, m, v, axis, name):
    rows, cols = w.shape
    nl = len(lands)
    transposed = axis == 1 and nl == 2
    if transposed:
        rows, cols = cols, rows
        tr = 256
        grid = (rows // tr,)
        wblk = pl.BlockSpec((cols, tr), lambda i: (0, i))
        lblk = [pl.BlockSpec((N_DEV, tr, a.shape[2]), lambda i: (0, i, 0)) for a in lands]
    elif axis == 1:
        tr = 256 if rows % 256 == 0 else rows
        grid = (rows // tr,)
        wblk = pl.BlockSpec((tr, cols), lambda i: (i, 0))
        lblk = [pl.BlockSpec((N_DEV, tr, a.shape[2]), lambda i: (0, i, 0)) for a in lands]
    else:
        tc = _col_tile(cols)
        grid = (cols // tc,)
        wblk = pl.BlockSpec((rows, tc), lambda i: (0, i))
        lblk = [pl.BlockSpec((N_DEV, a.shape[1], tc), lambda i: (0, 0, i)) for a in lands]

    def body(*refs):
        w_ref, m_ref, v_ref = refs[nl:nl + 3]
        g_ref, d_ref, nm_ref, nv_ref = refs[nl + 3:]
        g = _slot_sum(refs[0])
        if nl == 2:
            tail = _slot_sum(refs[1])
            if transposed:
                g = jnp.concatenate([g.T, tail.T[:cols - g.shape[1], :]], axis=0)
            elif axis == 1:
                g = jnp.concatenate([g, tail[:, :cols - g.shape[1]]], axis=1)
            else:
                g = jnp.concatenate([g, tail[:rows - g.shape[0], :]], axis=0)
        delta, nm, nv = _adamw(w_ref[...], g, m_ref[...], v_ref[...])
        g_ref[...] = g
        d_ref[...] = delta
        nm_ref[...] = nm
        nv_ref[...] = nv

    shp = jax.ShapeDtypeStruct(w.shape, F32)
    return pl.pallas_call(
        body,
        grid=grid,
        in_specs=lblk + [wblk, wblk, wblk],
        out_specs=[wblk, wblk, wblk, wblk],
        out_shape=[shp, shp, shp, shp],
        compiler_params=_params(1),
        name=name,
    )(*lands, w, m, v)


def _mesh_pos():
    return lax.axis_index("x"), lax.axis_index("y"), lax.axis_index("c")


def _peer(k):
    x, y, c = _mesh_pos()
    pos = (x ^ (k >> 2), y ^ ((k >> 1) & 1), c ^ (k & 1))
    return pos, 4 * pos[0] + 2 * pos[1] + pos[2]


def _sem_index(row, k):
    return row * (N_DEV - 1) + k - 1


def _window(ref, axis, start, size):
    align = 16 if axis == 0 else LANES
    start = pl.multiple_of(start, align)
    return ref.at[pl.ds(start, size), :] if axis == 0 else ref.at[:, pl.ds(start, size)]


def _piece_refs(piece, srcs, lands, me, peer):
    kind, si, li, axis, base, stride, shape = piece
    if kind == "gather":
        return srcs[si], _window(lands[li], axis, base + stride * me, shape[axis])
    return _window(srcs[si], axis, base + stride * peer, shape[axis]), lands[li].at[me]


def _place_own(srcs, land_shapes, pieces, name):
    ns, nl, npc = len(srcs), len(land_shapes), len(pieces)

    def body(*refs):
        s_refs = refs[:ns]
        l_refs = refs[ns:ns + nl]
        bufs = refs[ns + nl:ns + nl + npc]
        sems = refs[ns + nl + npc]
        x, y, c = _mesh_pos()
        me = 4 * x + 2 * y + c
        loads = []
        for p, piece in enumerate(pieces):
            src, dst = _piece_refs(piece, s_refs, l_refs, me, me)
            cp = pltpu.make_async_copy(src, bufs[p], sems.at[0, p])
            cp.start()
            loads.append((cp, dst))
        stores = []
        for p, (cp, dst) in enumerate(loads):
            cp.wait()
            out = pltpu.make_async_copy(bufs[p], dst, sems.at[1, p])
            out.start()
            stores.append(out)
        for out in stores:
            out.wait()

    out = pl.pallas_call(
        body,
        in_specs=[ANY] * ns,
        out_specs=[ANY] * nl,
        out_shape=list(land_shapes),
        scratch_shapes=[pltpu.VMEM(pc[6], srcs[pc[1]].dtype) for pc in pieces] + [pltpu.SemaphoreType.DMA((2, npc))],
        compiler_params=pltpu.CompilerParams(vmem_limit_bytes=VMEM_LIMIT_V7X),
        name=name,
    )(*srcs)
    return list(out)


def _comm_start(srcs, lands, pieces, groups, name, after=()):
    ns, nl, na, ng = len(srcs), len(lands), len(after), len(groups)

    def body(*refs):
        s_refs = refs[:ns]
        l_refs = refs[ns:ns + nl]
        outs = refs[ns + nl + na:]
        sems = outs[:2 * ng]
        token = outs[-1]
        x, y, c = _mesh_pos()
        me = 4 * x + 2 * y + c
        for g, members in enumerate(groups):
            for row, p in enumerate(members):
                for k in range(1, N_DEV):
                    pos, peer = _peer(k)
                    src, dst = _piece_refs(pieces[p], s_refs, l_refs, me, peer)
                    pltpu.make_async_remote_copy(src_ref=src, dst_ref=dst, send_sem=sems[2 * g].at[_sem_index(row, k)],
                                                 recv_sem=sems[2 * g + 1].at[_sem_index(row, k)], device_id=pos,
                                                 device_id_type=MESH_ID).start()
        token[...] = jnp.zeros_like(token)

    sem_shapes = []
    for members in groups:
        sem_shapes += [pltpu.SemaphoreType.DMA((len(members) * (N_DEV - 1),))] * 2
    hbm_of = lambda a: pltpu.HBM(a.shape, a.dtype)
    out = pl.pallas_call(
        body,
        in_specs=[HBM] * (ns + nl) + [ANY] * na,
        out_specs=[SEM] * (2 * ng) + [HBM] * (ns + nl) + [pl.BlockSpec(memory_space=pltpu.VMEM)],
        out_shape=sem_shapes + [hbm_of(a) for a in srcs] + [hbm_of(a) for a in lands]
        + [jax.ShapeDtypeStruct((8, LANES), F32)],
        input_output_aliases={i: 2 * ng + i for i in range(ns + nl)},
        compiler_params=pltpu.CompilerParams(has_side_effects=DATAFLOW),
        name=name,
    )(*[pltpu.with_memory_space_constraint(a, pltpu.HBM) for a in list(srcs) + list(lands)], *after)
    sems = [(out[2 * g], out[2 * g + 1]) for g in range(ng)]
    return sems, list(out[2 * ng:2 * ng + ns]), list(out[2 * ng + ns:2 * ng + ns + nl]), out[-1]


def _comm_wait(srcs, lands, pieces, members, sems, after, name):
    ns, nl, na = len(srcs), len(lands), len(after)

    def body(*refs):
        s_refs = refs[:ns]
        l_refs = refs[ns:ns + nl]
        send_sems, recv_sems = refs[ns + nl:ns + nl + 2]
        x, y, c = _mesh_pos()
        me = 4 * x + 2 * y + c
        for row, p in enumerate(members):
            for k in range(1, N_DEV):
                pos, peer = _peer(k)
                src, dst = _piece_refs(pieces[p], s_refs, l_refs, me, peer)
                cp = pltpu.make_async_remote_copy(src_ref=src, dst_ref=dst, send_sem=send_sems.at[_sem_index(row, k)],
                                                  recv_sem=recv_sems.at[_sem_index(row, k)], device_id=pos,
                                                  device_id_type=MESH_ID)
                cp.wait_send()
                cp.wait_recv()

    hbm_of = lambda a: pltpu.HBM(a.shape, a.dtype)
    out = pl.pallas_call(
        body,
        in_specs=[HBM] * (ns + nl) + [SEM, SEM] + [ANY] * na,
        out_specs=[HBM] * (ns + nl),
        out_shape=[hbm_of(a) for a in srcs] + [hbm_of(a) for a in lands],
        input_output_aliases={i: i for i in range(ns + nl)},
        compiler_params=pltpu.CompilerParams(has_side_effects=DATAFLOW),
        name=name,
    )(*srcs, *lands, sems[0], sems[1], *after)
    return list(out[ns:])


_CHIP_FLIPS = (2, 4, 6)
_SIBLING = 1


def _gather2_first(srcs, lands, pieces, name):
    ns, nl, npc = len(srcs), len(lands), len(pieces)

    def body(*refs):
        s_refs, l_refs = refs[:ns], refs[ns:ns + nl]
        send_sems, recv_sib, recv_ici = refs[ns + nl:ns + nl + 3]
        token = refs[-1]
        x, y, c = _mesh_pos()
        me = 4 * x + 2 * y + c
        for p, piece in enumerate(pieces):
            for j, k in enumerate((_SIBLING,) + _CHIP_FLIPS):
                pos, peer = _peer(k)
                src, dst = _piece_refs(piece, s_refs, l_refs, me, peer)
                recv = recv_sib.at[p] if j == 0 else recv_ici.at[3 * p + j - 1]
                pltpu.make_async_remote_copy(src_ref=src, dst_ref=dst, send_sem=send_sems.at[4 * p + j], recv_sem=recv,
                                             device_id=pos, device_id_type=MESH_ID).start()
        token[...] = jnp.zeros_like(token)

    hbm_of = lambda a: pltpu.HBM(a.shape, a.dtype)
    dma = lambda n: pltpu.SemaphoreType.DMA((n,))
    out = pl.pallas_call(
        body,
        in_specs=[HBM] * (ns + nl),
        out_specs=[SEM] * 3 + [HBM] * (ns + nl) + [pl.BlockSpec(memory_space=pltpu.VMEM)],
        out_shape=[dma(4 * npc), dma(npc), dma(3 * npc)] + [hbm_of(a) for a in srcs] + [hbm_of(a) for a in lands]
        + [jax.ShapeDtypeStruct((8, LANES), F32)],
        input_output_aliases={i: 3 + i for i in range(ns + nl)},
        compiler_params=pltpu.CompilerParams(has_side_effects=DATAFLOW),
        name=name,
    )(*[pltpu.with_memory_space_constraint(a, pltpu.HBM) for a in list(srcs) + list(lands)])
    return out[0], out[1], out[2], list(out[3:3 + ns]), list(out[3 + ns:3 + ns + nl]), out[-1]


def _landed_block(piece, lands, owner):
    _, _, li, axis, base, stride, shape = piece
    return _window(lands[li], axis, base + stride * owner, shape[axis])


def _gather2_pass(lands, pieces, recv_ici, after, name):
    nl, npc, na = len(lands), len(pieces), len(after)

    def body(*refs):
        l_refs = refs[:nl]
        recv_sems = refs[nl]
        send_fwd, recv_fwd = refs[nl + 1 + na:nl + 3 + na]
        sib, _ = _peer(_SIBLING)
        for p, piece in enumerate(pieces):
            for j, k in enumerate(_CHIP_FLIPS):
                pos, owner = _peer(k)
                block = _landed_block(piece, l_refs, owner)
                pltpu.make_async_remote_copy(src_ref=block, dst_ref=block, send_sem=send_fwd.at[3 * p + j],
                                             recv_sem=recv_sems.at[3 * p + j], device_id=pos,
                                             device_id_type=MESH_ID).wait_recv()
                pltpu.make_async_remote_copy(src_ref=block, dst_ref=block, send_sem=send_fwd.at[3 * p + j],
                                             recv_sem=recv_fwd.at[3 * p + j], device_id=sib,
                                             device_id_type=MESH_ID).start()

    hbm_of = lambda a: pltpu.HBM(a.shape, a.dtype)
    dma = lambda n: pltpu.SemaphoreType.DMA((n,))
    out = pl.pallas_call(
        body,
        in_specs=[HBM] * nl + [SEM] + [ANY] * na,
        out_specs=[SEM, SEM] + [HBM] * nl,
        out_shape=[dma(3 * npc), dma(3 * npc)] + [hbm_of(a) for a in lands],
        input_output_aliases={i: 2 + i for i in range(nl)},
        compiler_params=pltpu.CompilerParams(has_side_effects=DATAFLOW),
        name=name,
    )(*lands, recv_ici, *after)
    return out[0], out[1], list(out[2:])


def _gather2_last(srcs, lands, pieces, send_sems, recv_sib, send_fwd, recv_fwd, name):
    ns, nl = len(srcs), len(lands)

    def body(*refs):
        s_refs, l_refs = refs[:ns], refs[ns:ns + nl]
        send_a, recv_s, send_f, recv_f = refs[ns + nl:ns + nl + 4]
        x, y, c = _mesh_pos()
        me = 4 * x + 2 * y + c
        sib, sib_index = _peer(_SIBLING)
        for p, piece in enumerate(pieces):
            src, dst = _piece_refs(piece, s_refs, l_refs, me, sib_index)
            own = lambda s_sem, r_sem: pltpu.make_async_remote_copy(
                src_ref=src, dst_ref=dst, send_sem=s_sem, recv_sem=r_sem, device_id=sib, device_id_type=MESH_ID)
            own(send_a.at[4 * p], recv_s.at[p]).wait_recv()
            for j in range(4):
                own(send_a.at[4 * p + j], recv_s.at[p]).wait_send()
            for j in range(3):
                fwd = own(send_f.at[3 * p + j], recv_f.at[3 * p + j])
                fwd.wait_recv()
                fwd.wait_send()

    hbm_of = lambda a: pltpu.HBM(a.shape, a.dtype)
    out = pl.pallas_call(
        body,
        in_specs=[HBM] * (ns + nl) + [SEM] * 4,
        out_specs=[HBM] * (ns + nl),
        out_shape=[hbm_of(a) for a in srcs] + [hbm_of(a) for a in lands],
        input_output_aliases={i: i for i in range(ns + nl)},
        compiler_params=pltpu.CompilerParams(has_side_effects=DATAFLOW),
        name=name,
    )(*srcs, *lands, send_sems, recv_sib, send_fwd, recv_fwd)
    return list(out[ns:])


_SMALL_NAMES = ("ln1_g", "ln1_b", "hg_lb_logits", "hg_norm_g", "sg_ln_g", "sg_ln_b", "sg_w_s", "sg_b_s",
                "ln2_g", "ln2_b", "mem_ln_g", "mem_ln_b", "ln3_g", "ln3_b", "ln4_g", "ln4_b")


_VEC_NAMES = ("ln1_g", "ln1_b", "ln2_g", "ln2_b", "mem_ln_g", "mem_ln_b", "ln3_g", "ln3_b", "ln4_g", "ln4_b")
_ROW_NAMES = ("hg_lb_logits", "hg_norm_g", "sg_ln_g", "sg_ln_b", "sg_b_s", "sg_w_s")
VEC_ROWS = 16


def _row_plan(shapes):
    plan, pos = {}, 0
    for k in _ROW_NAMES:
        shp = shapes[k]
        slabs, off = [], pos
        for idx in itertools.product(*[range(dim) for dim in shp[:-2]]):
            slabs.append((idx, off, shp[-2]))
            off += shp[-2]
        plan[k] = (pos, slabs)
        pos = -(-off // 8) * 8
    return plan, -(-pos // 16) * 16


def _pack_small_grads(gs, shapes, loss):
    d = gs[_VEC_NAMES[0]].size
    vec = jnp.concatenate([gs[k].reshape(1, -1) for k in _VEC_NAMES] + [jnp.tile(loss, (1, d // LANES))], axis=0)
    vec = jnp.pad(vec, ((0, VEC_ROWS - vec.shape[0]), (0, 0)))
    plan, total = _row_plan(shapes)
    parts, pos = [], 0
    for k in _ROW_NAMES:
        first, slabs = plan[k]
        rows = gs[k].reshape(-1, LANES)
        end = slabs[-1][1] + slabs[-1][2]
        nxt = -(-end // 8) * 8
        parts.append(jnp.pad(rows, ((0, nxt - first - rows.shape[0]), (0, 0))))
        pos = nxt
    parts.append(jnp.zeros((total - pos, LANES), F32))
    return vec, jnp.concatenate(parts, axis=0)


def _adam_small(land_vec, land_rows, w, m, v):
    names = _VEC_NAMES + _ROW_NAMES
    n = len(names)
    shapes = {k: w[k].shape for k in names}
    plan, _ = _row_plan(shapes)

    def body(*refs):
        lv_ref, lr_ref = refs[:2]
        w_refs, m_refs, v_refs = refs[2:2 + n], refs[2 + n:2 + 2 * n], refs[2 + 2 * n:2 + 3 * n]
        outs = refs[2 + 3 * n:2 + 7 * n]
        loss_ref = refs[2 + 7 * n]
        gv_s, gr_s = refs[3 + 7 * n:]
        gv_s[...] = _slot_sum(lv_ref)
        gr_s[...] = _slot_sum(lr_ref)
        loss_ref[...] = gv_s[len(_VEC_NAMES):len(_VEC_NAMES) + 1, :LANES]
        for p, k in enumerate(names):
            if k in _VEC_NAMES:
                row = _VEC_NAMES.index(k)
                slabs = [((), None, None)]
            else:
                slabs = plan[k][1]
            for idx, off, rows in slabs:
                g = gv_s[row:row + 1, :] if off is None else gr_s[off:off + rows, :]
                sel = idx + (slice(None), slice(None))
                delta, nm, nv = _adamw(w_refs[p][sel], g, m_refs[p][sel], v_refs[p][sel])
                for o, val in zip(range(4), (g, delta, nm, nv)):
                    outs[o * n + p][sel] = val

    flat = lambda tree: [tree[k] for k in names]
    shp = [jax.ShapeDtypeStruct(shapes[k], F32) for k in names]
    out = pl.pallas_call(
        body,
        out_shape=shp * 4 + [jax.ShapeDtypeStruct((1, LANES), F32)],
        scratch_shapes=[pltpu.VMEM(land_vec.shape[1:], F32), pltpu.VMEM(land_rows.shape[1:], F32)],
        name="adam_small",
    )(land_vec, land_rows, *flat(w), *flat(m), *flat(v))
    return [dict(zip(names, out[o * n:(o + 1) * n])) for o in range(4)], out[4 * n]


_COL_FFN = ("ffn1_w_gate", "ffn1_w_up", "ffn2_w_gate", "ffn2_w_up")
_ROW_FFN = ("ffn1_w_down", "ffn2_w_down")
_ROW_SQ = ("w_out", "xa_w_q", "xa_w_k", "xa_w_v", "xa_w_o")
_BIG_NAMES = ("ffn1_w_gate", "ffn1_w_up", "ffn1_w_down", "w_in", "w_out", "xa_w_q", "xa_w_k", "xa_w_v", "xa_w_o",
              "ffn2_w_gate", "ffn2_w_up", "ffn2_w_down")


def _ffn_split(fs):
    main = (fs // MXU_WIDTH_V7X) * MXU_WIDTH_V7X
    tail = fs - main
    tail_pad = -(-tail // LANES) * LANES
    assert main > 0 and tail > 0
    return main, tail, tail_pad


def _layout(name, shard_shape):
    r, c = shard_shape
    if name in _COL_FFN:
        main, tail, pad = _ffn_split(c)
        return (r, N_DEV * (main + pad)), [(1, 0, main, (r, main), (0, main)),
                                           (1, N_DEV * main, pad, (r, pad), (main, c))]
    if name in _ROW_FFN:
        main, tail, pad = _ffn_split(r)
        return (N_DEV * (main + pad), c), [(0, 0, main, (main, c), (0, main)),
                                           (0, N_DEV * main, pad, (pad, c), (main, r))]
    if name == "w_in":
        return (r, N_DEV * c), [(1, 0, c, (r, c), (0, c))]
    return (N_DEV * r, c), [(0, 0, r, (r, c), (0, r))]


def _shard_pieces(name, shard):
    out = []
    for axis, _, _, shape, (lo, hi) in _layout(name, shard.shape)[1]:
        part = shard[lo:hi, :] if axis == 0 else shard[:, lo:hi]
        pad = [(0, shape[0] - part.shape[0]), (0, shape[1] - part.shape[1])]
        out.append(jnp.pad(part, pad).astype(BF16))
    return out


def _gather_plan(names, shards):
    srcs, land_shapes, pieces, index = [], [], [], {}
    for li, name in enumerate(names):
        shape2d, parts = _layout(name, shards[name].shape)
        land_shapes.append(jax.ShapeDtypeStruct(shape2d, BF16))
        index[name] = []
        for (axis, base, stride, shape, _), src in zip(parts, _shard_pieces(name, shards[name])):
            index[name].append(len(pieces))
            pieces.append(("gather", len(srcs), li, axis, base, stride, shape))
            srcs.append(src)
    return srcs, land_shapes, pieces, index


def _scatter_plan(names, grads, shard_shapes):
    srcs, land_shapes, pieces, index = [], [], [], {}
    for si, name in enumerate(names):
        _, parts = _layout(name, shard_shapes[name])
        srcs.append(grads[name])
        index[name] = []
        for axis, base, stride, shape, _ in parts:
            index[name].append(len(land_shapes))
            pieces.append(("scatter", si, len(land_shapes), axis, base, stride, shape))
            land_shapes.append(jax.ShapeDtypeStruct((N_DEV,) + shape, grads[name].dtype))
    return srcs, land_shapes, pieces, index


def _small_views(small):
    row = lambda a: a.reshape(1, -1)
    ln = {k: row(small[k]) for k in ("ln1_g", "ln1_b", "ln2_g", "ln2_b", "ln3_g", "ln3_b", "ln4_g", "ln4_b",
                                      "mem_ln_g", "mem_ln_b", "hg_norm_g")}
    sg_w = small["sg_w_s"].reshape(SG_GROUPS, SG_CHUNK, SG_CHUNK)
    sg = dict(logits=jnp.swapaxes(small["hg_lb_logits"], 0, 1),
              g=small["sg_ln_g"].reshape(SG_GROUPS, 1, SG_DIM), b=small["sg_ln_b"].reshape(SG_GROUPS, 1, SG_DIM),
              w=sg_w, wt=jnp.swapaxes(sg_w, 1, 2), bs=small["sg_b_s"].reshape(SG_GROUPS, SG_CHUNK, 1))
    return ln, sg


def _forward(x, mem, target, get_w, small, first_deps=()):
    ln, sg = _small_views(small)
    xb = x.astype(BF16)
    a1, b1, s1 = _ffn_up(xb, get_w("ffn1_w_gate", ()), get_w("ffn1_w_up", ()), "ffn1_up", deps=first_deps)
    h1b, xh1, rs1 = _mm_res_ln(s1, get_w("ffn1_w_down", (s1,)), x, ln["ln1_g"], ln["ln1_b"], 0.5, "ffn1_down_ln")
    proj = _mm_nn(h1b, get_w("w_in", (h1b,)), "mix_in")
    oraw, mix, states = _hgrn_fwd(proj, sg["logits"], ln["hg_norm_g"])
    mix = _sgu_fwd(proj, mix, sg["g"], sg["b"], sg["w"], sg["bs"])
    h2b, xh2, rs2 = _mm_res_ln(mix, get_w("w_out", (mix,)), (xh1, ln["ln1_g"], ln["ln1_b"]), ln["ln2_g"], ln["ln2_b"],
                               1.0, "mix_out_ln")
    mb, mxh, mrs, kb, vb = _mem_kv(mem, ln["mem_ln_g"], ln["mem_ln_b"], get_w("xa_w_k", (h2b,)), get_w("xa_w_v", (h2b,)))
    qb, att = _attn_fwd(h2b, get_w("xa_w_q", (kb,)), kb, vb)
    h3b, xh3, rs3 = _mm_res_ln(att, get_w("xa_w_o", (att,)), (xh2, ln["ln2_g"], ln["ln2_b"]), ln["ln3_g"], ln["ln3_b"],
                               1.0, "attn_out_ln")
    a2, b2, s2 = _ffn_up(h3b, get_w("ffn2_w_gate", (h3b,)), get_w("ffn2_w_up", (h3b,)), "ffn2_up")
    loss, dy4, dy4b, dg4, db4 = _mm_res_ln(s2, get_w("ffn2_w_down", (s2,)), (xh3, ln["ln3_g"], ln["ln3_b"]),
                                           ln["ln4_g"], ln["ln4_b"], 0.5, "ffn2_down_ln_loss", target=target)
    return dict(xb=xb, a1=a1, b1=b1, s1=s1, h1b=h1b, xh1=xh1, rs1=rs1, proj=proj, oraw=oraw, mix=mix, states=states,
                h2b=h2b, xh2=xh2, rs2=rs2, mb=mb, mxh=mxh, mrs=mrs, kb=kb, vb=vb, qb=qb, att=att, h3b=h3b, xh3=xh3,
                rs3=rs3, a2=a2, b2=b2, s2=s2, loss=loss, dy4=dy4, dy4b=dy4b, dg4=dg4, db4=db4)


def _backward(sv, wt, small, send):
    ln, sg = _small_views(small)
    gs = {"ln4_g": sv["dg4"], "ln4_b": sv["db4"]}
    loss, dy4, dy4b = sv["loss"], sv["dy4"], sv["dy4b"]
    g_down2 = _mm_tn(sv["s2"], dy4b, "g_ffn2_down", scale=0.5)
    da2, db2, dy3, dy3b, gs["ln3_g"], gs["ln3_b"] = _ffn_bwd_fused(
        dy4, dy4b, wt["ffn2_w_down"], wt["ffn2_w_gate"], wt["ffn2_w_up"], sv["a2"], sv["b2"], 0.5,
        (sv["xh3"], sv["rs3"], ln["ln3_g"]), "ffn2_bwd")
    g_gate2 = _mm_tn(sv["h3b"], da2, "g_ffn2_gate")
    g_up2 = _mm_tn(sv["h3b"], db2, "g_ffn2_up")
    tok = send(("ffn2_w_down", "ffn2_w_gate", "ffn2_w_up"), (g_down2, g_gate2, g_up2))

    g_o = _mm_tn(sv["att"], dy3b, "g_xa_o", deps=(tok,))
    dqb, dk, dv = _attn_bwd(dy3b, wt["xa_w_o"], sv["qb"], sv["kb"], sv["vb"])
    g_q = _mm_tn(sv["h2b"], dqb, "g_xa_q")
    g_k, g_v, gs["mem_ln_g"], gs["mem_ln_b"] = _mem_bwd(dk, dv, sv["mb"], sv["mxh"], sv["mrs"], ln["mem_ln_g"],
                                                        wt["xa_w_k"], wt["xa_w_v"])
    tok = send(("xa_w_o", "xa_w_q", "xa_w_k", "xa_w_v"), (g_o, g_q, g_k, g_v))
    dy2, dy2b, gs["ln2_g"], gs["ln2_b"] = _dx_ln(dy3, [(dqb, wt["xa_w_q"])], (sv["xh2"], sv["rs2"], ln["ln2_g"]),
                                                 "attn_dx_ln", deps=(tok,))

    g_out = _mm_tn(sv["mix"], dy2b, "g_w_out")
    dmix = _mm_nt(dy2b, wt["w_out"], "mix_out_bwd")
    dq, dfz, div, dgg, dlg, dgn = _hgrn_bwd(sv["proj"], sv["oraw"], dmix, sv["states"], sg["logits"], ln["hg_norm_g"])
    du, dvv, gs["sg_ln_g"], gs["sg_ln_b"], gs["sg_w_s"], gs["sg_b_s"] = _sgu_bwd(
        sv["proj"], dmix, sg["g"], sg["b"], sg["w"], sg["wt"], sg["bs"])
    gs["hg_lb_logits"] = jnp.swapaxes(dlg, 0, 1)
    gs["hg_norm_g"] = jnp.sum(dgn, axis=0)
    dproj = jnp.concatenate([dq, dfz, div, dgg, du, dvv], axis=1)
    g_in = _mm_tn(sv["h1b"], dproj, "g_w_in")
    tok = send(("w_out", "w_in"), (g_out, g_in))
    dy1, dy1b, gs["ln1_g"], gs["ln1_b"] = _dx_ln(dy2, [(dproj, wt["w_in"])], (sv["xh1"], sv["rs1"], ln["ln1_g"]),
                                                 "mix_dx_ln", deps=(tok,))

    g_down1 = _mm_tn(sv["s1"], dy1b, "g_ffn1_down", scale=0.5)
    tok = send(("ffn1_w_down",), (g_down1,))
    da1, db1 = _ffn_bwd_act(dy1b, wt["ffn1_w_down"], sv["a1"], sv["b1"], 0.5, "ffn1_bwd_act", deps=(tok,))
    g_gate1 = _mm_tn(sv["xb"], da1, "g_ffn1_gate")
    tok = send(("ffn1_w_gate",), (g_gate1,))
    g_up1 = _mm_tn(sv["xb"], db1, "g_ffn1_up", deps=(tok,))
    tok = send(("ffn1_w_up",), (g_up1,))
    grad_x = _dx_ln(dy1, [(da1, wt["ffn1_w_gate"]), (db1, wt["ffn1_w_up"])], None, "ffn1_dx", deps=(tok,))
    return loss, grad_x, gs


_WEIGHT_NAMES = ("ffn1_w_gate", "ffn1_w_up", "ffn1_w_down", "ln1_g", "ln1_b", "w_in", "hg_lb_logits", "hg_norm_g",
                 "sg_ln_g", "sg_ln_b", "sg_w_s", "sg_b_s", "w_out", "ln2_g", "ln2_b", "mem_ln_g", "mem_ln_b",
                 "xa_w_q", "xa_w_k", "xa_w_v", "xa_w_o", "ln3_g", "ln3_b", "ffn2_w_gate", "ffn2_w_up", "ffn2_w_down",
                 "ln4_g", "ln4_b")
_FIRST = ("ffn1_w_gate", "ffn1_w_up")
_SECOND = ("ffn1_w_down", "w_in", "w_out")
_THIRD = ("xa_w_k", "xa_w_v", "xa_w_q", "xa_w_o", "ffn2_w_gate", "ffn2_w_up", "ffn2_w_down")


def kernel(x, mem, ffn1_w_gate, ffn1_w_up, ffn1_w_down, ln1_g, ln1_b, w_in, hg_lb_logits, hg_norm_g, sg_ln_g, sg_ln_b, sg_w_s, sg_b_s, w_out, ln2_g, ln2_b, mem_ln_g, mem_ln_b, xa_w_q, xa_w_k, xa_w_v, xa_w_o, ln3_g, ln3_b, ffn2_w_gate, ffn2_w_up, ffn2_w_down, ln4_g, ln4_b, loss_target, m_ffn1_w_gate, m_ffn1_w_up, m_ffn1_w_down, m_ln1_g, m_ln1_b, m_w_in, m_hg_lb_logits, m_hg_norm_g, m_sg_ln_g, m_sg_ln_b, m_sg_w_s, m_sg_b_s, m_w_out, m_ln2_g, m_ln2_b, m_mem_ln_g, m_mem_ln_b, m_xa_w_q, m_xa_w_k, m_xa_w_v, m_xa_w_o, m_ln3_g, m_ln3_b, m_ffn2_w_gate, m_ffn2_w_up, m_ffn2_w_down, m_ln4_g, m_ln4_b, v_ffn1_w_gate, v_ffn1_w_up, v_ffn1_w_down, v_ln1_g, v_ln1_b, v_w_in, v_hg_lb_logits, v_hg_norm_g, v_sg_ln_g, v_sg_ln_b, v_sg_w_s, v_sg_b_s, v_w_out, v_ln2_g, v_ln2_b, v_mem_ln_g, v_mem_ln_b, v_xa_w_q, v_xa_w_k, v_xa_w_v, v_xa_w_o, v_ln3_g, v_ln3_b, v_ffn2_w_gate, v_ffn2_w_up, v_ffn2_w_down, v_ln4_g, v_ln4_b):
    args = dict(locals())
    w = {k: args[k] for k in _WEIGHT_NAMES}
    m = {k: args["m_" + k] for k in _WEIGHT_NAMES}
    v = {k: args["v_" + k] for k in _WEIGHT_NAMES}
    shards = {k: w[k][0] for k in _BIG_NAMES}
    shard_shapes = {k: shards[k].shape for k in _BIG_NAMES}
    small = {k: (w[k][0] if k != "hg_lb_logits" else w[k]) for k in _SMALL_NAMES}

    srcs1, shapes1, pieces1, idx1 = _gather_plan(_FIRST, shards)
    lands1 = _place_own(srcs1, shapes1, pieces1, "gather_first_own")
    rest = _SECOND + _THIRD
    srcs2, shapes2, pieces2, idx2 = _gather_plan(rest, shards)
    lands2 = _place_own(srcs2, shapes2, pieces2, "gather_rest_own")
    groups2 = [list(idx2[k]) for k in rest]
    send1, rsib1, rici1, srcs1, lands1, tok1 = _gather2_first(srcs1, lands1, pieces1, "gather_first_start")
    sems2, srcs2, lands2, tok2 = _comm_start(srcs2, lands2, pieces2, groups2, "gather_rest_start", after=(tok1,))
    sfwd1, rfwd1, lands1 = _gather2_pass(lands1, pieces1, rici1, (tok2,), "gather_first_pass")
    lands1 = _gather2_last(srcs1, lands1, pieces1, send1, rsib1, sfwd1, rfwd1, "gather_first_wait")
    wt = dict(zip(_FIRST, lands1))
    pending = {k: gi for gi, k in enumerate(rest)}

    def get_w(name, after):
        if name in pending:
            gi = pending.pop(name)
            si = [pieces2[p][1] for p in groups2[gi]]
            sub = [(pieces2[p][0], row, 0) + pieces2[p][3:] for row, p in enumerate(groups2[gi])]
            wt[name] = _comm_wait([srcs2[s] for s in si], [lands2[gi]], sub, list(range(len(sub))), sems2[gi],
                                  after, "gather_wait_" + name)[0]
        return wt[name]

    sv = _forward(x[0], mem[0], loss_target[0], get_w, small, first_deps=(tok2,))

    sent = []

    def send(names, grads):
        srcs, shapes, pieces, idx = _scatter_plan(names, dict(zip(names, grads)), shard_shapes)
        lands = _place_own(srcs, shapes, pieces, "grads_own_%d" % len(sent))
        sems, srcs, lands, tok = _comm_start(srcs, lands, pieces, [list(range(len(pieces)))],
                                             "grads_start_%d" % len(sent))
        sent.append((names, srcs, lands, pieces, idx, sems[0]))
        return tok

    loss, grad_x, gs = _backward(sv, wt, small, send)

    ssrc = list(_pack_small_grads(gs, {k: w[k].shape for k in _SMALL_NAMES}, loss))
    sp = [("scatter", i, i, 0, 0, 0, a.shape) for i, a in enumerate(ssrc)]
    sshape = [jax.ShapeDtypeStruct((N_DEV,) + a.shape, F32) for a in ssrc]
    sl = _place_own(ssrc, sshape, sp, "small_own")
    ssem, ssrc, sl, _ = _comm_start(ssrc, sl, sp, [[0, 1]], "small_start")

    out_g, out_d, out_m, out_v = {}, {}, {}, {}
    after = (grad_x,)
    for n_sent, (names, srcs, lands, pieces, idx, sems) in enumerate(sent):
        lands = _comm_wait(srcs, lands, pieces, list(range(len(pieces))), sems, after, "grads_wait_%d" % n_sent)
        for k in names:
            axis = 1 if (k in _COL_FFN or k == "w_in") else 0
            if k in _COL_FFN:
                res = _adam_sharded([lands[i] for i in idx[k]], w[k][0].T, m[k][0].T, v[k][0].T, axis, "adam_" + k)
                res = [r.T for r in res]
            else:
                res = _adam_sharded([lands[i] for i in idx[k]], w[k][0], m[k][0], v[k][0], axis, "adam_" + k)
            out_g[k], out_d[k], out_m[k], out_v[k] = [r[None] for r in res]
        after = (out_v[names[-1]],)
    sl = _comm_wait(ssrc, sl, sp, [0, 1], ssem[0], after, "small_wait")
    small_out, loss_sum = _adam_small(sl[0], sl[1], w, m, v)
    for dst, res in zip((out_g, out_d, out_m, out_v), small_out):
        dst.update(res)
    loss_all = loss_sum[0, 0]
    return (loss_all, grad_x[None], *[out_g[k] for k in _WEIGHT_NAMES], *[out_d[k] for k in _WEIGHT_NAMES],
            *[out_m[k] for k in _WEIGHT_NAMES], *[out_v[k] for k in _WEIGHT_NAMES])
```

```python
import itertools

import jax
import jax.numpy as jnp
import numpy as np
from jax import lax
from jax.experimental import pallas as pl
from jax.experimental.pallas import tpu as pltpu

F32 = jnp.float32
BF16 = jnp.bfloat16

N_DEV = 8
ALPHA = 2.0 ** 0.25
LN_EPS = 1e-5
HG_HEADS = 4
HG_DIM = 128
SG_GROUPS = 4
SG_DIM = 128
SG_CHUNK = 128
X_HEADS = 4
HG_BLOCK = 16
HG_UNROLL = 8
ADAM_LR = 0.001
ADAM_B1 = 0.9
ADAM_B2 = 0.999
ADAM_EPS = 1e-08
ADAM_WD = 0.01
ADAM_STEP = 10
VMEM_LIMIT_V7X = 48 * 1024 * 1024
MXU_WIDTH_V7X = 256
LANES = 128
MESH_ID = pl.DeviceIdType.MESH
ANY = pl.BlockSpec(memory_space=pl.ANY)
HBM = pl.BlockSpec(memory_space=pltpu.HBM)
SEM = pl.BlockSpec(memory_space=pltpu.SEMAPHORE)
DATAFLOW = pltpu.SideEffectType.DATAFLOW_SIDE_EFFECTING


def _params(n_axes):
    return pltpu.CompilerParams(dimension_semantics=("arbitrary",) * n_axes, vmem_limit_bytes=VMEM_LIMIT_V7X)


def _dot(a, b):
    return jnp.dot(a, b, preferred_element_type=F32)


def _dot_nt(a, b):
    return lax.dot_general(a, b, (((1,), (1,)), ((), ())), preferred_element_type=F32)


def _dot_tn(a, b):
    return lax.dot_general(a, b, (((0,), (0,)), ((), ())), preferred_element_type=F32)


def _sigmoid(x):
    return 1.0 / (1.0 + jnp.exp(-x))


def _silu_and_grad(a):
    sig = _sigmoid(a)
    return a * sig, sig * (1.0 + a * (1.0 - sig))


_GELU_C = 0.7978845608028654


def _gelu_and_grad(x):
    inner = _GELU_C * (x + 0.044715 * x * x * x)
    t = jnp.tanh(inner)
    val = 0.5 * x * (1.0 + t)
    grad = 0.5 * (1.0 + t) + 0.5 * x * (1.0 - t * t) * _GELU_C * (1.0 + 3.0 * 0.044715 * x * x)
    return val, grad


def _ln_fwd(y, g, b):
    mu = jnp.mean(y, axis=-1, keepdims=True)
    yc = y - mu
    var = jnp.mean(yc * yc, axis=-1, keepdims=True)
    rstd = lax.rsqrt(var + LN_EPS)
    xhat = yc * rstd
    return xhat * g + b, xhat, rstd


def _ln_bwd(dh, xhat, rstd, g):
    dxh = dh * g
    m1 = jnp.mean(dxh, axis=-1, keepdims=True)
    m2 = jnp.mean(dxh * xhat, axis=-1, keepdims=True)
    dy = rstd * (dxh - m1 - xhat * m2)
    dg = jnp.sum(dh * xhat, axis=0, keepdims=True)
    db = jnp.sum(dh, axis=0, keepdims=True)
    return dy, dg, db


def _mask_dot(mask, x):
    hi = x.astype(BF16)
    lo = (x - hi.astype(F32)).astype(BF16)
    return _dot(mask, hi) + _dot(mask, lo)


def _block_masks(n):
    r = np.arange(n)[:, None]
    c = np.arange(n)[None, :]
    same = (r // HG_BLOCK) == (c // HG_BLOCK)
    return jnp.asarray(np.stack([same & (c <= r), same & (c >= r), same]), BF16)


def _row_tile(t):
    return min(t, 512)


def _col_tile(n):
    for cand in (512, 256, 128):
        if n % cand == 0:
            return cand
    return n


def _resident(w):
    return pl.BlockSpec(w.shape, lambda *_: (0, 0), pipeline_mode=pl.Buffered(1))


def _drop_deps(body, n_in, n_deps):
    if n_deps == 0:
        return body
    return lambda *refs: body(*refs[:n_in], *refs[n_in + n_deps:])


def _ffn_up(hb, wg, wu, name, deps=()):
    t, d = hb.shape
    f = wg.shape[1]
    tm = min(t, 256)
    tn = _col_tile(f)

    def body(h_ref, wg_ref, wu_ref, a_ref, b_ref, s_ref):
        h = h_ref[...]
        for c in range(f // tn):
            cols = slice(c * tn, (c + 1) * tn)
            a = _dot(h, wg_ref[:, cols])
            b = _dot(h, wu_ref[:, cols])
            a_ref[:, cols] = a.astype(BF16)
            b_ref[:, cols] = b.astype(BF16)
            s_ref[:, cols] = (a * _sigmoid(a) * b).astype(BF16)

    act = pl.BlockSpec((tm, f), lambda i: (i, 0))
    return pl.pallas_call(
        _drop_deps(body, 3, len(deps)),
        grid=(t // tm,),
        in_specs=[pl.BlockSpec((tm, d), lambda i: (i, 0)), _resident(wg), _resident(wu)] + [ANY] * len(deps),
        out_specs=[act, act, act],
        out_shape=[jax.ShapeDtypeStruct((t, f), BF16)] * 3,
        compiler_params=_params(1),
        name=name,
    )(hb, wg, wu, *deps)


def _mm_res_ln(lhs, w, res, g, b, coef, name, target=None):
    t, kd = lhs.shape
    d = w.shape[1]
    tm = _row_tile(t)
    nt = t // tm
    from_norm = isinstance(res, tuple)
    n_res = 3 if from_norm else 1

    def body(*refs):
        l_ref, w_ref = refs[:2]
        r_refs = refs[2:2 + n_res]
        g_ref, b_ref = refs[2 + n_res:4 + n_res]
        rest = refs[4 + n_res:]
        prev = r_refs[0][...] * r_refs[1][...] + r_refs[2][...] if from_norm else r_refs[0][...]
        y = ALPHA * prev + coef * _dot(l_ref[...], w_ref[...])
        h, xhat, rstd = _ln_fwd(y, g_ref[...], b_ref[...])
        if target is None:
            hb_ref, xh_ref, rs_ref = rest
            hb_ref[...] = h.astype(BF16)
            xh_ref[...] = xhat
            rs_ref[...] = rstd
            return
        t_ref, loss_ref, dy_ref, dyb_ref, dg_ref, db_ref, lacc = rest
        i = pl.program_id(0)

        @pl.when(i == 0)
        def _():
            lacc[...] = jnp.zeros_like(lacc)
            dg_ref[...] = jnp.zeros_like(dg_ref)
            db_ref[...] = jnp.zeros_like(db_ref)

        err = h - t_ref[...]
        lacc[...] += jnp.sum(err * err, axis=0, keepdims=True)
        dy, dg, db = _ln_bwd(err * (1.0 / d), xhat, rstd, g_ref[...])
        dy_ref[...] = dy
        dyb_ref[...] = dy.astype(BF16)
        dg_ref[...] += dg
        db_ref[...] += db

        @pl.when(i == nt - 1)
        def _():
            loss_ref[...] = jnp.zeros_like(loss_ref) + jnp.sum(lacc[...], axis=1, keepdims=True) * (0.5 / d)

    row = pl.BlockSpec((tm, d), lambda i: (i, 0))
    vec = pl.BlockSpec((1, d), lambda i: (0, 0))
    res_specs = [row, vec, vec] if from_norm else [row]
    res_args = list(res) if from_norm else [res]
    in_specs = [pl.BlockSpec((tm, kd), lambda i: (i, 0)), _resident(w)] + res_specs + [vec, vec]
    args = [lhs, w] + res_args + [g, b]
    if target is None:
        out_specs = [row, row, pl.BlockSpec((tm, 1), lambda i: (i, 0))]
        out_shape = [jax.ShapeDtypeStruct((t, d), BF16), jax.ShapeDtypeStruct((t, d), F32),
                     jax.ShapeDtypeStruct((t, 1), F32)]
        scratch = []
    else:
        in_specs.append(row)
        args.append(target)
        out_specs = [pl.BlockSpec((1, LANES), lambda i: (0, 0)), row, row, vec, vec]
        out_shape = [jax.ShapeDtypeStruct((1, LANES), F32), jax.ShapeDtypeStruct((t, d), F32),
                     jax.ShapeDtypeStruct((t, d), BF16), jax.ShapeDtypeStruct((1, d), F32),
                     jax.ShapeDtypeStruct((1, d), F32)]
        scratch = [pltpu.VMEM((1, d), F32)]
    return pl.pallas_call(
        body,
        grid=(nt,),
        in_specs=in_specs,
        out_specs=out_specs,
        out_shape=out_shape,
        scratch_shapes=scratch,
        compiler_params=_params(1),
        name=name,
    )(*args)


def _mm_nn(lhs, w, name):
    t, kd = lhs.shape
    n = w.shape[1]
    tm = _row_tile(t)
    tn = _col_tile(n)

    def body(l_ref, w_ref, o_ref):
        lhs_v = l_ref[...]
        for c in range(n // tn):
            cols = slice(c * tn, (c + 1) * tn)
            o_ref[:, cols] = _dot(lhs_v, w_ref[:, cols])

    return pl.pallas_call(
        body,
        grid=(t // tm,),
        in_specs=[pl.BlockSpec((tm, kd), lambda i: (i, 0)), _resident(w)],
        out_specs=pl.BlockSpec((tm, n), lambda i: (i, 0)),
        out_shape=jax.ShapeDtypeStruct((t, n), F32),
        compiler_params=_params(1),
        name=name,
    )(lhs, w)


def _lower_bound(lg):
    m = jnp.max(lg, axis=0, keepdims=True)
    e = jnp.exp(lg - m)
    return e[0:1, :] / jnp.sum(e, axis=0, keepdims=True)


def _forget_terms(fz, lb):
    e = jnp.exp(-jnp.abs(fz))
    r = 1.0 / (1.0 + e)
    pos = fz >= 0.0
    sig = jnp.where(pos, r, e * r)
    nsig = jnp.where(pos, e * r, r)
    f = lb + (1.0 - lb) * sig
    k = (1.0 - lb) * nsig
    return sig, nsig, f, k


def _hg_tile(t):
    return min(t, 256)


HG_HALF = HG_BLOCK // 2
NEG_BIG = -1e30


def _halves(a):
    return a[:HG_HALF, :], a[HG_HALF:, :]


def _causal_halves(s):
    return (0, 1) if s < HG_HALF else (1,)


def _decay_from(b_half, b_s, s, h, tidx):
    first = s - h * HG_HALF
    diff = b_half - b_s
    if first > 0:
        diff = jnp.where(tidx >= first, diff, NEG_BIG)
    return jnp.exp(diff)


def _hgrn_fwd(proj, logits, gn):
    t = proj.shape[0]
    ct = _hg_tile(t)
    nct = t // ct
    nblk = ct // HG_BLOCK
    nh = HG_HEADS

    def body(q_ref, fz_ref, iv_ref, gg_ref, lg_ref, gn_ref, mask_ref, oraw_ref, oa_ref, st_ref,
             state, qt_s, kt_s, k_s, b_s, dec_s):
        c = pl.program_id(1)

        @pl.when(c == 0)
        def _():
            state[...] = jnp.zeros_like(state)

        lb = _lower_bound(lg_ref[...])
        q = q_ref[...]
        _, _, f, k = _forget_terms(fz_ref[...], lb)
        logf = jnp.log(f)
        b = _mask_dot(mask_ref[0], logf)
        bend = _mask_dot(mask_ref[2], logf)
        qt_s[...] = (q * jnp.exp(b)).astype(BF16)
        kt_s[...] = (k * jnp.exp(bend - b)).astype(BF16)
        k_s[...] = k
        b_s[...] = b
        dec_s[...] = jnp.exp(bend)
        tidx = lax.broadcasted_iota(jnp.int32, (HG_HALF, HG_DIM), 0)

        def blk(i, carry):
            r0 = pl.multiple_of(i * HG_BLOCK, HG_BLOCK)
            rows = pl.ds(r0, HG_BLOCK)
            st = state[...]
            st_ref[i] = st
            v = iv_ref[rows, :]
            qq = q_ref[rows, :]
            kk = k_s[rows, :]
            bb = b_s[rows, :]
            o = list(_halves(_dot_nt(qt_s[rows, :], st.astype(BF16))))
            qh, bh = _halves(qq), _halves(bb)
            for s in range(HG_BLOCK):
                ks, vs = kk[s:s + 1, :], v[s:s + 1, :]
                for h in _causal_halves(s):
                    e = _decay_from(bh[h], bb[s:s + 1, :], s, h, tidx)
                    acol = jnp.sum(qh[h] * (ks * e), axis=1, keepdims=True)
                    o[h] = o[h] + acol * vs
            oraw_ref[rows, :] = jnp.concatenate(o, axis=0)
            state[...] = st * dec_s[pl.ds(r0, 1), :] + _dot_tn(v.astype(BF16), kt_s[rows, :])
            return carry

        lax.fori_loop(0, nblk, blk, 0, unroll=2 * HG_UNROLL)
        oraw = oraw_ref[...]
        r = lax.rsqrt(jnp.mean(oraw * oraw, axis=-1, keepdims=True) + LN_EPS)
        gg = gg_ref[...]
        oa_ref[...] = (oraw * r * gn_ref[...] * gg * _sigmoid(gg)).astype(BF16)

    def slab(off):
        return pl.BlockSpec((ct, HG_DIM), lambda h, c: (c, off + h))

    out_slab = pl.BlockSpec((ct, HG_DIM), lambda h, c: (c, h))
    return pl.pallas_call(
        body,
        grid=(nh, nct),
        in_specs=[slab(0), slab(nh), slab(2 * nh), slab(3 * nh),
                  pl.BlockSpec((None, 2, HG_DIM), lambda h, c: (h, 0, 0)),
                  pl.BlockSpec((1, HG_DIM), lambda h, c: (0, 0)),
                  pl.BlockSpec((3, ct, ct), lambda h, c: (0, 0, 0))],
        out_specs=[out_slab, out_slab, pl.BlockSpec((None, nblk, HG_DIM, HG_DIM), lambda h, c: (h, c, 0, 0))],
        out_shape=[jax.ShapeDtypeStruct((t, nh * HG_DIM), F32),
                   jax.ShapeDtypeStruct((t, (nh + SG_GROUPS) * HG_DIM), BF16),
                   jax.ShapeDtypeStruct((nh, t // HG_BLOCK, HG_DIM, HG_DIM), F32)],
        scratch_shapes=[pltpu.VMEM((HG_DIM, HG_DIM), F32), pltpu.VMEM((ct, HG_DIM), BF16),
                        pltpu.VMEM((ct, HG_DIM), BF16), pltpu.VMEM((ct, HG_DIM), F32),
                        pltpu.VMEM((ct, HG_DIM), F32), pltpu.VMEM((ct, HG_DIM), F32)],
        compiler_params=_params(2),
        name="hgrn_fwd",
    )(proj, proj, proj, proj, logits, gn, _block_masks(ct))


def _sg_tile(t):
    return min(t, 512)


def _sgu_chunk_fwd(u, v, ln_g, ln_b, wm, bs):
    ua, dua = _gelu_and_grad(u)
    va, dva = _gelu_and_grad(v)
    vn, xhat, rstd = _ln_fwd(va, ln_g, ln_b)
    s = _dot(wm, vn.astype(BF16)) + bs
    return ua, dua, dva, vn, xhat, rstd, s


def _tril_weight(w_ref):
    n = SG_CHUNK
    r = lax.broadcasted_iota(jnp.int32, (n, n), 0)
    c = lax.broadcasted_iota(jnp.int32, (n, n), 1)
    return jnp.where(c <= r, w_ref[...], 0.0)


def _sgu_fwd(proj, mix, ln_g, ln_b, w_s, b_col):
    t = proj.shape[0]
    ct = _sg_tile(t)
    ng = SG_GROUPS
    off_u = 4 * HG_HEADS
    off_v = off_u + ng

    def body(u_ref, v_ref, g_ref, b_ref, w_ref, bs_ref, mix_ref, o_ref):
        del mix_ref
        wm = _tril_weight(w_ref).astype(BF16)
        for n in range(ct // SG_CHUNK):
            rows = slice(n * SG_CHUNK, (n + 1) * SG_CHUNK)
            ua, _, _, _, _, _, s = _sgu_chunk_fwd(u_ref[rows, :], v_ref[rows, :], g_ref[...], b_ref[...], wm, bs_ref[...])
            o_ref[rows, :] = (ua * s).astype(BF16)

    vec = pl.BlockSpec((None, 1, SG_DIM), lambda g, c: (g, 0, 0))
    return pl.pallas_call(
        body,
        grid=(ng, t // ct),
        in_specs=[pl.BlockSpec((ct, SG_DIM), lambda g, c: (c, off_u + g)),
                  pl.BlockSpec((ct, SG_DIM), lambda g, c: (c, off_v + g)), vec, vec,
                  pl.BlockSpec((None, SG_CHUNK, SG_CHUNK), lambda g, c: (g, 0, 0)),
                  pl.BlockSpec((None, SG_CHUNK, 1), lambda g, c: (g, 0, 0)), ANY],
        out_specs=pl.BlockSpec((ct, SG_DIM), lambda g, c: (c, HG_HEADS + g)),
        out_shape=jax.ShapeDtypeStruct(mix.shape, mix.dtype),
        input_output_aliases={6: 0},
        compiler_params=_params(2),
        name="sgu_fwd",
    )(proj, proj, ln_g, ln_b, w_s, b_col, mix)


def _mem_kv(mem, g, b, wk, wv):
    m_len, d = mem.shape

    def body(m_ref, g_ref, b_ref, wk_ref, wv_ref, mb_ref, xh_ref, rs_ref, k_ref, v_ref):
        m, xhat, rstd = _ln_fwd(m_ref[...], g_ref[...], b_ref[...])
        mb = m.astype(BF16)
        mb_ref[...] = mb
        xh_ref[...] = xhat
        rs_ref[...] = rstd
        k_ref[...] = _dot(mb, wk_ref[...]).astype(BF16)
        v_ref[...] = _dot(mb, wv_ref[...]).astype(BF16)

    return pl.pallas_call(
        body,
        out_shape=[jax.ShapeDtypeStruct((m_len, d), BF16), jax.ShapeDtypeStruct((m_len, d), F32),
                   jax.ShapeDtypeStruct((m_len, 1), F32), jax.ShapeDtypeStruct((m_len, d), BF16),
                   jax.ShapeDtypeStruct((m_len, d), BF16)],
        compiler_params=pltpu.CompilerParams(vmem_limit_bytes=VMEM_LIMIT_V7X),
        name="mem_kv",
    )(mem, g, b, wk, wv)


def _softmax_rows(s):
    m = jnp.max(s, axis=-1, keepdims=True)
    p = jnp.exp(s - m)
    return p / jnp.sum(p, axis=-1, keepdims=True)


def _attn_fwd(hb, wq, kb, vb):
    t, d = hb.shape
    tm = _row_tile(t)
    dh = d // X_HEADS
    scale = dh ** -0.5

    def body(h_ref, wq_ref, k_ref, v_ref, q_ref, o_ref):
        q = _dot(h_ref[...], wq_ref[...]).astype(BF16)
        q_ref[...] = q
        for hd in range(X_HEADS):
            sl = slice(hd * dh, (hd + 1) * dh)
            p = _softmax_rows(_dot_nt(q[:, sl], k_ref[:, sl]) * scale)
            o_ref[:, sl] = _dot(p.astype(BF16), v_ref[:, sl]).astype(BF16)

    row = pl.BlockSpec((tm, d), lambda i: (i, 0))
    full = lambda a: pl.BlockSpec(a.shape, lambda i: (0, 0))
    return pl.pallas_call(
        body,
        grid=(t // tm,),
        in_specs=[row, full(wq), full(kb), full(vb)],
        out_specs=[row, row],
        out_shape=[jax.ShapeDtypeStruct((t, d), BF16), jax.ShapeDtypeStruct((t, d), BF16)],
        compiler_params=_params(1),
        name="attn_fwd",
    )(hb, wq, kb, vb)


def _ffn_bwd_act(dyb, wd, a, b, coef, name, deps=()):
    t, d = dyb.shape
    f = wd.shape[0]
    tm = _row_tile(t)
    tn = _col_tile(f)

    def body(dy_ref, wd_ref, a_ref, b_ref, da_ref, db_ref):
        dy = dy_ref[...]
        for c in range(f // tn):
            cols = slice(c * tn, (c + 1) * tn)
            ds = _dot_nt(dy, wd_ref[cols, :]) * coef
            silu, dsilu = _silu_and_grad(a_ref[:, cols].astype(F32))
            da_ref[:, cols] = (ds * b_ref[:, cols].astype(F32) * dsilu).astype(BF16)
            db_ref[:, cols] = (ds * silu).astype(BF16)

    act = pl.BlockSpec((tm, f), lambda i: (i, 0))
    return pl.pallas_call(
        _drop_deps(body, 4, len(deps)),
        grid=(t // tm,),
        in_specs=[pl.BlockSpec((tm, d), lambda i: (i, 0)), _resident(wd), act, act] + [ANY] * len(deps),
        out_specs=[act, act],
        out_shape=[jax.ShapeDtypeStruct((t, f), BF16), jax.ShapeDtypeStruct((t, f), BF16)],
        compiler_params=_params(1),
        name=name,
    )(dyb, wd, a, b, *deps)


def _ffn_bwd_fused(dy, dyb, wd, wg, wu, a, b, coef, ln, name):
    t, d = dy.shape
    f = wd.shape[0]
    tm = min(t, 256)
    tn = _col_tile(f)

    def body(dy_ref, dyb_ref, wd_ref, wg_ref, wu_ref, a_ref, b_ref, xh_ref, rs_ref, g_ref,
             da_ref, db_ref, dyo_ref, dyob_ref, dg_ref, dbl_ref):
        dyb_v = dyb_ref[...]
        dh = ALPHA * dy_ref[...]
        for c in range(f // tn):
            cols = slice(c * tn, (c + 1) * tn)
            ds = _dot_nt(dyb_v, wd_ref[cols, :]) * coef
            silu, dsilu = _silu_and_grad(a_ref[:, cols].astype(F32))
            da = (ds * b_ref[:, cols].astype(F32) * dsilu).astype(BF16)
            db = (ds * silu).astype(BF16)
            da_ref[:, cols] = da
            db_ref[:, cols] = db
            dh = dh + _dot_nt(da, wg_ref[:, cols]) + _dot_nt(db, wu_ref[:, cols])

        @pl.when(pl.program_id(0) == 0)
        def _():
            dg_ref[...] = jnp.zeros_like(dg_ref)
            dbl_ref[...] = jnp.zeros_like(dbl_ref)

        dyp, dg, dbl = _ln_bwd(dh, xh_ref[...], rs_ref[...], g_ref[...])
        dyo_ref[...] = dyp
        dyob_ref[...] = dyp.astype(BF16)
        dg_ref[...] += dg
        dbl_ref[...] += dbl

    row = pl.BlockSpec((tm, d), lambda i: (i, 0))
    act = pl.BlockSpec((tm, f), lambda i: (i, 0))
    vec = pl.BlockSpec((1, d), lambda i: (0, 0))
    return pl.pallas_call(
        body,
        grid=(t // tm,),
        in_specs=[row, row, _resident(wd), _resident(wg), _resident(wu), act, act, row,
                  pl.BlockSpec((tm, 1), lambda i: (i, 0)), vec],
        out_specs=[act, act, row, row, vec, vec],
        out_shape=[jax.ShapeDtypeStruct((t, f), BF16), jax.ShapeDtypeStruct((t, f), BF16),
                   jax.ShapeDtypeStruct((t, d), F32), jax.ShapeDtypeStruct((t, d), BF16),
                   jax.ShapeDtypeStruct((1, d), F32), jax.ShapeDtypeStruct((1, d), F32)],
        compiler_params=_params(1),
        name=name,
    )(dy, dyb, wd, wg, wu, a, b, *ln)


def _mm_tn(a, b, name, scale=1.0, deps=()):
    t, m = a.shape
    n = b.shape[1]
    tt = _row_tile(t)
    nt = t // tt
    tm_o = m // 2 if (m > n and m * n > 2 ** 21) else m
    tn_o = n // 2 if (n > m and m * n > 2 ** 21) else n

    def body(a_ref, b_ref, o_ref, acc):
        k = pl.program_id(2)

        @pl.when(k == 0)
        def _():
            acc[...] = jnp.zeros_like(acc)

        acc[...] += _dot_tn(a_ref[...], b_ref[...])

        @pl.when(k == nt - 1)
        def _():
            o_ref[...] = (acc[...] * scale).astype(BF16)

    return pl.pallas_call(
        _drop_deps(body, 2, len(deps)),
        grid=(m // tm_o, n // tn_o, nt),
        in_specs=[pl.BlockSpec((tt, tm_o), lambda i, j, k: (k, i)), pl.BlockSpec((tt, tn_o), lambda i, j, k: (k, j))]
        + [ANY] * len(deps),
        out_specs=pl.BlockSpec((tm_o, tn_o), lambda i, j, k: (i, j)),
        out_shape=jax.ShapeDtypeStruct((m, n), BF16),
        scratch_shapes=[pltpu.VMEM((tm_o, tn_o), F32)],
        compiler_params=_params(3),
        name=name,
    )(a, b, *deps)


def _mm_nt(lhs, w, name):
    t, d = lhs.shape
    kd = w.shape[0]
    tm = _row_tile(t)

    def body(l_ref, w_ref, o_ref):
        o_ref[...] = _dot_nt(l_ref[...], w_ref[...])

    return pl.pallas_call(
        body,
        grid=(t // tm,),
        in_specs=[pl.BlockSpec((tm, d), lambda i: (i, 0)), _resident(w)],
        out_specs=pl.BlockSpec((tm, kd), lambda i: (i, 0)),
        out_shape=jax.ShapeDtypeStruct((t, kd), F32),
        compiler_params=_params(1),
        name=name,
    )(lhs, w)


def _dx_ln(dy, pairs, ln, name, deps=()):
    t, d = dy.shape
    npair = len(pairs)
    tm = min(t, 512 // npair)
    nt = t // tm
    n_in = 1 + 2 * npair + (3 if ln is not None else 0)

    def body(*refs):
        dy_ref = refs[0]
        pr = refs[1:1 + 2 * npair]
        pos = 1 + 2 * npair
        dh = ALPHA * dy_ref[...]
        for p in range(npair):
            dh = dh + _dot_nt(pr[2 * p][...], pr[2 * p + 1][...])
        if ln is not None:
            xh_ref, rs_ref, g_ref = refs[pos:pos + 3]
            dyo_ref, dyb_ref, dg_ref, db_ref = refs[pos + 3:pos + 7]

            @pl.when(pl.program_id(0) == 0)
            def _():
                dg_ref[...] = jnp.zeros_like(dg_ref)
                db_ref[...] = jnp.zeros_like(db_ref)

            dyp, dg, db = _ln_bwd(dh, xh_ref[...], rs_ref[...], g_ref[...])
            dyo_ref[...] = dyp
            dyb_ref[...] = dyp.astype(BF16)
            dg_ref[...] += dg
            db_ref[...] += db
        else:
            refs[pos][...] = dh

    row = pl.BlockSpec((tm, d), lambda i: (i, 0))
    vec = pl.BlockSpec((1, d), lambda i: (0, 0))
    in_specs = [row]
    args = [dy]
    for lhs, w in pairs:
        in_specs += [pl.BlockSpec((tm, lhs.shape[1]), lambda i: (i, 0)), _resident(w)]
        args += [lhs, w]
    if ln is not None:
        in_specs += [row, pl.BlockSpec((tm, 1), lambda i: (i, 0)), vec]
        args += list(ln)
        out_specs = [row, row, vec, vec]
        out_shape = [jax.ShapeDtypeStruct((t, d), F32), jax.ShapeDtypeStruct((t, d), BF16),
                     jax.ShapeDtypeStruct((1, d), F32), jax.ShapeDtypeStruct((1, d), F32)]
    else:
        out_specs = row
        out_shape = jax.ShapeDtypeStruct((t, d), F32)
    return pl.pallas_call(
        _drop_deps(body, n_in, len(deps)),
        grid=(nt,),
        in_specs=in_specs + [ANY] * len(deps),
        out_specs=out_specs,
        out_shape=out_shape,
        compiler_params=_params(1),
        name=name,
    )(*args, *deps)


def _hgrn_bwd(proj, oraw, dmix, states, logits, gn):
    t = proj.shape[0]
    ct = _hg_tile(t)
    nct = t // ct
    nblk = ct // HG_BLOCK
    nh = HG_HEADS

    def body(q_ref, fz_ref, iv_ref, gg_ref, or_ref, do_ref, st_ref, lg_ref, gn_ref, mask_ref,
             dq_ref, dfz_ref, div_ref, dgg_ref, dlg_ref, dgn_ref,
             dstate, qt_s, kt_s, k_s, b_s, eb_s, ekb_s, dec_s, dor_s, dbl_s, gr_s, dk_s, dlb_acc):
        c = pl.program_id(1)

        @pl.when(c == 0)
        def _():
            dstate[...] = jnp.zeros_like(dstate)
            dlb_acc[...] = jnp.zeros_like(dlb_acc)
            dgn_ref[...] = jnp.zeros_like(dgn_ref)

        lb = _lower_bound(lg_ref[...])
        q = q_ref[...]
        sig, nsig, f, k = _forget_terms(fz_ref[...], lb)
        logf = jnp.log(f)
        b = _mask_dot(mask_ref[0], logf)
        bend = _mask_dot(mask_ref[2], logf)
        eb = jnp.exp(b)
        ekb = jnp.exp(bend - b)
        qt_s[...] = (q * eb).astype(BF16)
        kt_s[...] = (k * ekb).astype(BF16)
        k_s[...] = k
        b_s[...] = b
        eb_s[...] = eb
        ekb_s[...] = ekb
        dec_s[...] = jnp.exp(bend)
        oraw = or_ref[...]
        r = lax.rsqrt(jnp.mean(oraw * oraw, axis=-1, keepdims=True) + LN_EPS)
        on = oraw * r
        gg = gg_ref[...]
        silu, dsilu = _silu_and_grad(gg)
        doa = do_ref[...]
        gnv = gn_ref[...]
        dgg_ref[...] = (doa * on * gnv * dsilu).astype(BF16)
        dyn = doa * silu
        dgn_ref[...] += jnp.sum(dyn * on, axis=0, keepdims=True)
        don = dyn * gnv
        dor_s[...] = r * (don - on * jnp.mean(don * on, axis=-1, keepdims=True))
        tidx = lax.broadcasted_iota(jnp.int32, (HG_HALF, HG_DIM), 0)

        def blk(ii, carry):
            i = nblk - 1 - ii
            r0 = pl.multiple_of(i * HG_BLOCK, HG_BLOCK)
            rows = pl.ds(r0, HG_BLOCK)
            st = st_ref[i]
            dst = dstate[...]
            dstb = dst.astype(BF16)
            do = dor_s[rows, :]
            dob = do.astype(BF16)
            v = iv_ref[rows, :]
            vb = v.astype(BF16)
            qq = q_ref[rows, :]
            kk = k_s[rows, :]
            bb = b_s[rows, :]
            qt = qt_s[rows, :]
            kt = kt_s[rows, :]
            dec = dec_s[pl.ds(r0, 1), :]
            dkt = _dot(vb, dstb)
            dq = _dot(dob, st.astype(BF16)) * eb_s[rows, :]
            dk = dkt * ekb_s[rows, :]
            dv = _dot_nt(kt, dstb)
            gend = jnp.sum(kk * dk, axis=0, keepdims=True) + dec * jnp.sum(dst * st, axis=0, keepdims=True)
            qh, bh, doh = _halves(qq), _halves(bb), _halves(do)
            dqh, dkh, dvh = list(_halves(dq)), list(_halves(dk)), list(_halves(dv))
            for s in range(HG_BLOCK):
                ks, vs = kk[s:s + 1, :], v[s:s + 1, :]
                dk_part = dv_part = None
                for h in _causal_halves(s):
                    e = _decay_from(bh[h], bb[s:s + 1, :], s, h, tidx)
                    ke = ks * e
                    acol = jnp.sum(qh[h] * ke, axis=1, keepdims=True)
                    dacol = jnp.sum(doh[h] * vs, axis=1, keepdims=True)
                    dqh[h] = dqh[h] + dacol * ke
                    pk = dacol * (qh[h] * e)
                    pv = acol * doh[h]
                    dk_part = pk if dk_part is None else dk_part + pk
                    dv_part = pv if dv_part is None else dv_part + pv
                hs, row = divmod(s, HG_HALF)
                dkh[hs] = dkh[hs] + jnp.where(tidx == row, jnp.sum(dk_part, axis=0, keepdims=True), 0.0)
                dvh[hs] = dvh[hs] + jnp.where(tidx == row, jnp.sum(dv_part, axis=0, keepdims=True), 0.0)
            dq = jnp.concatenate(dqh, axis=0)
            dk = jnp.concatenate(dkh, axis=0)
            dv = jnp.concatenate(dvh, axis=0)
            dq_ref[rows, :] = dq.astype(BF16)
            div_ref[rows, :] = dv.astype(BF16)
            dk_s[rows, :] = dk
            dbl_s[rows, :] = qq * dq - kk * dk
            gr_s[rows, :] = jnp.zeros((HG_BLOCK, HG_DIM), F32) + gend
            dstate[...] = dst * dec + _dot_tn(dob, qt)
            return carry

        lax.fori_loop(0, nblk, blk, 0, unroll=2 * HG_UNROLL)
        dlogf = _mask_dot(mask_ref[1], dbl_s[...]) + gr_s[...]
        dk = dk_s[...]
        dfz_ref[...] = ((dlogf / f - dk) * ((1.0 - lb) * sig * nsig)).astype(BF16)
        dlb_acc[...] += jnp.sum((dlogf / f - dk) * nsig, axis=0, keepdims=True)

        @pl.when(c == nct - 1)
        def _():
            dl0 = dlb_acc[...] * lb * (1.0 - lb)
            layer = lax.broadcasted_iota(jnp.int32, (2, HG_DIM), 0)
            dlg_ref[...] = jnp.where(layer == 0, dl0, -dl0)

    def slab(off):
        return pl.BlockSpec((ct, HG_DIM), lambda h, c: (nct - 1 - c, off + h))

    out_slab = pl.BlockSpec((ct, HG_DIM), lambda h, c: (nct - 1 - c, h))
    tile_f32 = pltpu.VMEM((ct, HG_DIM), F32)
    tile_b16 = pltpu.VMEM((ct, HG_DIM), BF16)
    slab_shape = jax.ShapeDtypeStruct((t, nh * HG_DIM), BF16)
    return pl.pallas_call(
        body,
        grid=(nh, nct),
        in_specs=[slab(0), slab(nh), slab(2 * nh), slab(3 * nh), slab(0), slab(0),
                  pl.BlockSpec((None, nblk, HG_DIM, HG_DIM), lambda h, c: (h, nct - 1 - c, 0, 0)),
                  pl.BlockSpec((None, 2, HG_DIM), lambda h, c: (h, 0, 0)),
                  pl.BlockSpec((1, HG_DIM), lambda h, c: (0, 0)),
                  pl.BlockSpec((3, ct, ct), lambda h, c: (0, 0, 0))],
        out_specs=[out_slab, out_slab, out_slab, out_slab,
                   pl.BlockSpec((None, 2, HG_DIM), lambda h, c: (h, 0, 0)),
                   pl.BlockSpec((None, 1, HG_DIM), lambda h, c: (h, 0, 0))],
        out_shape=[slab_shape, slab_shape, slab_shape, slab_shape,
                   jax.ShapeDtypeStruct((nh, 2, HG_DIM), F32), jax.ShapeDtypeStruct((nh, 1, HG_DIM), F32)],
        scratch_shapes=[pltpu.VMEM((HG_DIM, HG_DIM), F32), tile_b16, tile_b16, tile_f32, tile_f32, tile_f32, tile_f32,
                        tile_f32, tile_f32, tile_f32, tile_f32, tile_f32, pltpu.VMEM((1, HG_DIM), F32)],
        compiler_params=_params(2),
        name="hgrn_bwd",
    )(proj, proj, proj, proj, oraw, dmix, states, logits, gn, _block_masks(ct))


def _sgu_bwd(proj, dmix, ln_g, ln_b, w_s, w_t, b_col):
    t = proj.shape[0]
    ct = _sg_tile(t)
    nct = t // ct
    ng = SG_GROUPS
    off_u = 4 * HG_HEADS
    off_v = off_u + ng
    n = SG_CHUNK

    def body(u_ref, v_ref, do_ref, g_ref, b_ref, w_ref, wt_ref, bs_ref, du_ref, dv_ref, dg_ref, db_ref, dw_ref, dbs_ref):
        c = pl.program_id(1)

        @pl.when(c == 0)
        def _():
            dg_ref[...] = jnp.zeros_like(dg_ref)
            db_ref[...] = jnp.zeros_like(db_ref)
            dw_ref[...] = jnp.zeros_like(dw_ref)
            dbs_ref[...] = jnp.zeros_like(dbs_ref)

        r = lax.broadcasted_iota(jnp.int32, (n, n), 0)
        cc = lax.broadcasted_iota(jnp.int32, (n, n), 1)
        wm = jnp.where(cc <= r, w_ref[...], 0.0).astype(BF16)
        wmt = jnp.where(r <= cc, wt_ref[...], 0.0).astype(BF16)
        for ci in range(ct // n):
            rows = slice(ci * n, (ci + 1) * n)
            ua, dua, dva, vn, xhat, rstd, s = _sgu_chunk_fwd(u_ref[rows, :], v_ref[rows, :], g_ref[...], b_ref[...],
                                                             wm, bs_ref[...])
            do = do_ref[rows, :]
            du_ref[rows, :] = (do * s * dua).astype(BF16)
            ds = do * ua
            dsb = ds.astype(BF16)
            dbs_ref[...] += jnp.sum(ds, axis=1, keepdims=True)
            dw_ref[...] += _dot_nt(dsb, vn.astype(BF16))
            dvn = _dot(wmt, dsb)
            dva_in, dg, db = _ln_bwd(dvn, xhat, rstd, g_ref[...])
            dg_ref[...] += dg
            db_ref[...] += db
            dv_ref[rows, :] = (dva_in * dva).astype(BF16)

        @pl.when(c == nct - 1)
        def _():
            dw_ref[...] = jnp.where(cc <= r, dw_ref[...], 0.0)

    vec = pl.BlockSpec((None, 1, SG_DIM), lambda g, c: (g, 0, 0))
    mat = pl.BlockSpec((None, n, n), lambda g, c: (g, 0, 0))
    col = pl.BlockSpec((None, n, 1), lambda g, c: (g, 0, 0))
    out_slab = pl.BlockSpec((ct, SG_DIM), lambda g, c: (c, g))
    return pl.pallas_call(
        body,
        grid=(ng, nct),
        in_specs=[pl.BlockSpec((ct, SG_DIM), lambda g, c: (c, off_u + g)),
                  pl.BlockSpec((ct, SG_DIM), lambda g, c: (c, off_v + g)),
                  pl.BlockSpec((ct, SG_DIM), lambda g, c: (c, ng + g)), vec, vec, mat, mat, col],
        out_specs=[out_slab, out_slab, vec, vec, mat, col],
        out_shape=[jax.ShapeDtypeStruct((t, ng * SG_DIM), BF16), jax.ShapeDtypeStruct((t, ng * SG_DIM), BF16),
                   jax.ShapeDtypeStruct((ng, 1, SG_DIM), F32), jax.ShapeDtypeStruct((ng, 1, SG_DIM), F32),
                   jax.ShapeDtypeStruct((ng, n, n), F32), jax.ShapeDtypeStruct((ng, n, 1), F32)],
        compiler_params=_params(2),
        name="sgu_bwd",
    )(proj, proj, dmix, ln_g, ln_b, w_s, w_t, b_col)


def _attn_bwd(dyb, wo, qb, kb, vb):
    t, d = dyb.shape
    m_len = kb.shape[0]
    tm = _row_tile(t)
    dh = d // X_HEADS
    scale = dh ** -0.5

    def body(dy_ref, wo_ref, q_ref, k_ref, v_ref, dq_ref, dk_ref, dv_ref):
        i = pl.program_id(0)

        @pl.when(i == 0)
        def _():
            dk_ref[...] = jnp.zeros_like(dk_ref)
            dv_ref[...] = jnp.zeros_like(dv_ref)

        do = _dot_nt(dy_ref[...], wo_ref[...]).astype(BF16)
        for hd in range(X_HEADS):
            sl = slice(hd * dh, (hd + 1) * dh)
            qh = q_ref[:, sl]
            p = _softmax_rows(_dot_nt(qh, k_ref[:, sl]) * scale)
            doh = do[:, sl]
            dp = _dot_nt(doh, v_ref[:, sl])
            ds = (p * (dp - jnp.sum(dp * p, axis=-1, keepdims=True)) * scale).astype(BF16)
            dq_ref[:, sl] = _dot(ds, k_ref[:, sl]).astype(BF16)
            dk_ref[:, sl] += _dot_tn(ds, qh)
            dv_ref[:, sl] += _dot_tn(p.astype(BF16), doh)

    row = pl.BlockSpec((tm, d), lambda i: (i, 0))
    full = lambda a: pl.BlockSpec(a.shape, lambda i: (0, 0))
    kv = pl.BlockSpec((m_len, d), lambda i: (0, 0))
    return pl.pallas_call(
        body,
        grid=(t // tm,),
        in_specs=[row, full(wo), row, full(kb), full(vb)],
        out_specs=[row, kv, kv],
        out_shape=[jax.ShapeDtypeStruct((t, d), BF16), jax.ShapeDtypeStruct((m_len, d), F32),
                   jax.ShapeDtypeStruct((m_len, d), F32)],
        compiler_params=_params(1),
        name="attn_bwd",
    )(dyb, wo, qb, kb, vb)


def _mem_bwd(dk, dv, mb, xhat, rstd, g, wk, wv):
    m_len, d = dk.shape

    def body(dk_ref, dv_ref, mb_ref, xh_ref, rs_ref, g_ref, wk_ref, wv_ref, gwk_ref, gwv_ref, dg_ref, db_ref):
        dkb = dk_ref[...].astype(BF16)
        dvb = dv_ref[...].astype(BF16)
        mb_v = mb_ref[...]
        gwk_ref[...] = _dot_tn(mb_v, dkb).astype(BF16)
        gwv_ref[...] = _dot_tn(mb_v, dvb).astype(BF16)
        dm = _dot_nt(dkb, wk_ref[...]) + _dot_nt(dvb, wv_ref[...])
        _, dg, db = _ln_bwd(dm, xh_ref[...], rs_ref[...], g_ref[...])
        dg_ref[...] = dg
        db_ref[...] = db

    return pl.pallas_call(
        body,
        out_shape=[jax.ShapeDtypeStruct((d, d), BF16), jax.ShapeDtypeStruct((d, d), BF16),
                   jax.ShapeDtypeStruct((1, d), F32), jax.ShapeDtypeStruct((1, d), F32)],
        compiler_params=pltpu.CompilerParams(vmem_limit_bytes=VMEM_LIMIT_V7X),
        name="mem_bwd",
    )(dk, dv, mb, xhat, rstd, g, wk, wv)


def _adamw(w, g, m, v):
    m = ADAM_B1 * m + (1.0 - ADAM_B1) * g
    v = ADAM_B2 * v + (1.0 - ADAM_B2) * (g * g)
    m_hat = m / (1.0 - ADAM_B1 ** ADAM_STEP)
    v_hat = v / (1.0 - ADAM_B2 ** ADAM_STEP)
    delta = -ADAM_LR * (m_hat / (jnp.sqrt(v_hat) + ADAM_EPS) + ADAM_WD * w)
    return delta, m, v


def _slot_sum(ref):
    g = ref[0].astype(F32)
    for s in range(1, N_DEV):
        g = g + ref[s].astype(F32)
    return g


def _adam_sharded(lands, w, m, v, axis, name):
    rows, cols = w.shape
    nl = len(lands)
    transposed = axis == 1 and nl == 2
    if transposed:
        rows, cols = cols, rows
        tr = 256
        grid = (rows // tr,)
        wblk = pl.BlockSpec((cols, tr), lambda i: (0, i))
        lblk = [pl.BlockSpec((N_DEV, tr, a.shape[2]), lambda i: (0, i, 0)) for a in lands]
    elif axis == 1:
        tr = 256 if rows % 256 == 0 else rows
        grid = (rows // tr,)
        wblk = pl.BlockSpec((tr, cols), lambda i: (i, 0))
        lblk = [pl.BlockSpec((N_DEV, tr, a.shape[2]), lambda i: (0, i, 0)) for a in lands]
    else:
        tc = _col_tile(cols)
        grid = (cols // tc,)
        wblk = pl.BlockSpec((rows, tc), lambda i: (0, i))
        lblk = [pl.BlockSpec((N_DEV, a.shape[1], tc), lambda i: (0, 0, i)) for a in lands]

    def body(*refs):
        w_ref, m_ref, v_ref = refs[nl:nl + 3]
        g_ref, d_ref, nm_ref, nv_ref = refs[nl + 3:]
        g = _slot_sum(refs[0])
        if nl == 2:
            tail = _slot_sum(refs[1])
            if transposed:
                g = jnp.concatenate([g.T, tail.T[:cols - g.shape[1], :]], axis=0)
            elif axis == 1:
                g = jnp.concatenate([g, tail[:, :cols - g.shape[1]]], axis=1)
            else:
                g = jnp.concatenate([g, tail[:rows - g.shape[0], :]], axis=0)
        delta, nm, nv = _adamw(w_ref[...], g, m_ref[...], v_ref[...])
        g_ref[...] = g
        d_ref[...] = delta
        nm_ref[...] = nm
        nv_ref[...] = nv

    shp = jax.ShapeDtypeStruct(w.shape, F32)
    return pl.pallas_call(
        body,
        grid=grid,
        in_specs=lblk + [wblk, wblk, wblk],
        out_specs=[wblk, wblk, wblk, wblk],
        out_shape=[shp, shp, shp, shp],
        compiler_params=_params(1),
        name=name,
    )(*lands, w, m, v)


def _mesh_pos():
    return lax.axis_index("x"), lax.axis_index("y"), lax.axis_index("c")


def _peer(k):
    x, y, c = _mesh_pos()
    pos = (x ^ (k >> 2), y ^ ((k >> 1) & 1), c ^ (k & 1))
    return pos, 4 * pos[0] + 2 * pos[1] + pos[2]


def _sem_index(row, k):
    return row * (N_DEV - 1) + k - 1


def _window(ref, axis, start, size):
    align = 16 if axis == 0 else LANES
    start = pl.multiple_of(start, align)
    return ref.at[pl.ds(start, size), :] if axis == 0 else ref.at[:, pl.ds(start, size)]


def _piece_refs(piece, srcs, lands, me, peer):
    kind, si, li, axis, base, stride, shape = piece
    if kind == "gather":
        return srcs[si], _window(lands[li], axis, base + stride * me, shape[axis])
    return _window(srcs[si], axis, base + stride * peer, shape[axis]), lands[li].at[me]


def _place_own(srcs, land_shapes, pieces, name):
    ns, nl, npc = len(srcs), len(land_shapes), len(pieces)

    def body(*refs):
        s_refs = refs[:ns]
        l_refs = refs[ns:ns + nl]
        bufs = refs[ns + nl:ns + nl + npc]
        sems = refs[ns + nl + npc]
        x, y, c = _mesh_pos()
        me = 4 * x + 2 * y + c
        loads = []
        for p, piece in enumerate(pieces):
            src, dst = _piece_refs(piece, s_refs, l_refs, me, me)
            cp = pltpu.make_async_copy(src, bufs[p], sems.at[0, p])
            cp.start()
            loads.append((cp, dst))
        stores = []
        for p, (cp, dst) in enumerate(loads):
            cp.wait()
            out = pltpu.make_async_copy(bufs[p], dst, sems.at[1, p])
            out.start()
            stores.append(out)
        for out in stores:
            out.wait()

    out = pl.pallas_call(
        body,
        in_specs=[ANY] * ns,
        out_specs=[ANY] * nl,
        out_shape=list(land_shapes),
        scratch_shapes=[pltpu.VMEM(pc[6], srcs[pc[1]].dtype) for pc in pieces] + [pltpu.SemaphoreType.DMA((2, npc))],
        compiler_params=pltpu.CompilerParams(vmem_limit_bytes=VMEM_LIMIT_V7X),
        name=name,
    )(*srcs)
    return list(out)


def _comm_start(srcs, lands, pieces, groups, name, after=()):
    ns, nl, na, ng = len(srcs), len(lands), len(after), len(groups)

    def body(*refs):
        s_refs = refs[:ns]
        l_refs = refs[ns:ns + nl]
        outs = refs[ns + nl + na:]
        sems = outs[:2 * ng]
        token = outs[-1]
        x, y, c = _mesh_pos()
        me = 4 * x + 2 * y + c
        for g, members in enumerate(groups):
            for row, p in enumerate(members):
                for k in range(1, N_DEV):
                    pos, peer = _peer(k)
                    src, dst = _piece_refs(pieces[p], s_refs, l_refs, me, peer)
                    pltpu.make_async_remote_copy(src_ref=src, dst_ref=dst, send_sem=sems[2 * g].at[_sem_index(row, k)],
                                                 recv_sem=sems[2 * g + 1].at[_sem_index(row, k)], device_id=pos,
                                                 device_id_type=MESH_ID).start()
        token[...] = jnp.zeros_like(token)

    sem_shapes = []
    for members in groups:
        sem_shapes += [pltpu.SemaphoreType.DMA((len(members) * (N_DEV - 1),))] * 2
    hbm_of = lambda a: pltpu.HBM(a.shape, a.dtype)
    out = pl.pallas_call(
        body,
        in_specs=[HBM] * (ns + nl) + [ANY] * na,
        out_specs=[SEM] * (2 * ng) + [HBM] * (ns + nl) + [pl.BlockSpec(memory_space=pltpu.VMEM)],
        out_shape=sem_shapes + [hbm_of(a) for a in srcs] + [hbm_of(a) for a in lands]
        + [jax.ShapeDtypeStruct((8, LANES), F32)],
        input_output_aliases={i: 2 * ng + i for i in range(ns + nl)},
        compiler_params=pltpu.CompilerParams(has_side_effects=DATAFLOW),
        name=name,
    )(*[pltpu.with_memory_space_constraint(a, pltpu.HBM) for a in list(srcs) + list(lands)], *after)
    sems = [(out[2 * g], out[2 * g + 1]) for g in range(ng)]
    return sems, list(out[2 * ng:2 * ng + ns]), list(out[2 * ng + ns:2 * ng + ns + nl]), out[-1]


def _comm_wait(srcs, lands, pieces, members, sems, after, name):
    ns, nl, na = len(srcs), len(lands), len(after)

    def body(*refs):
        s_refs = refs[:ns]
        l_refs = refs[ns:ns + nl]
        send_sems, recv_sems = refs[ns + nl:ns + nl + 2]
        x, y, c = _mesh_pos()
        me = 4 * x + 2 * y + c
        for row, p in enumerate(members):
            for k in range(1, N_DEV):
                pos, peer = _peer(k)
                src, dst = _piece_refs(pieces[p], s_refs, l_refs, me, peer)
                cp = pltpu.make_async_remote_copy(src_ref=src, dst_ref=dst, send_sem=send_sems.at[_sem_index(row, k)],
                                                  recv_sem=recv_sems.at[_sem_index(row, k)], device_id=pos,
                                                  device_id_type=MESH_ID)
                cp.wait_send()
                cp.wait_recv()

    hbm_of = lambda a: pltpu.HBM(a.shape, a.dtype)
    out = pl.pallas_call(
        body,
        in_specs=[HBM] * (ns + nl) + [SEM, SEM] + [ANY] * na,
        out_specs=[HBM] * (ns + nl),
        out_shape=[hbm_of(a) for a in srcs] + [hbm_of(a) for a in lands],
        input_output_aliases={i: i for i in range(ns + nl)},
        compiler_params=pltpu.CompilerParams(has_side_effects=DATAFLOW),
        name=name,
    )(*srcs, *lands, sems[0], sems[1], *after)
    return list(out[ns:])


_CHIP_FLIPS = (2, 4, 6)
_SIBLING = 1


def _gather2_first(srcs, lands, pieces, name):
    ns, nl, npc = len(srcs), len(lands), len(pieces)

    def body(*refs):
        s_refs, l_refs = refs[:ns], refs[ns:ns + nl]
        send_sems, recv_sib, recv_ici = refs[ns + nl:ns + nl + 3]
        token = refs[-1]
        x, y, c = _mesh_pos()
        me = 4 * x + 2 * y + c
        for p, piece in enumerate(pieces):
            for j, k in enumerate((_SIBLING,) + _CHIP_FLIPS):
                pos, peer = _peer(k)
                src, dst = _piece_refs(piece, s_refs, l_refs, me, peer)
                recv = recv_sib.at[p] if j == 0 else recv_ici.at[3 * p + j - 1]
                pltpu.make_async_remote_copy(src_ref=src, dst_ref=dst, send_sem=send_sems.at[4 * p + j], recv_sem=recv,
                                             device_id=pos, device_id_type=MESH_ID).start()
        token[...] = jnp.zeros_like(token)

    hbm_of = lambda a: pltpu.HBM(a.shape, a.dtype)
    dma = lambda n: pltpu.SemaphoreType.DMA((n,))
    out = pl.pallas_call(
        body,
        in_specs=[HBM] * (ns + nl),
        out_specs=[SEM] * 3 + [HBM] * (ns + nl) + [pl.BlockSpec(memory_space=pltpu.VMEM)],
        out_shape=[dma(4 * npc), dma(npc), dma(3 * npc)] + [hbm_of(a) for a in srcs] + [hbm_of(a) for a in lands]
        + [jax.ShapeDtypeStruct((8, LANES), F32)],
        input_output_aliases={i: 3 + i for i in range(ns + nl)},
        compiler_params=pltpu.CompilerParams(has_side_effects=DATAFLOW),
        name=name,
    )(*[pltpu.with_memory_space_constraint(a, pltpu.HBM) for a in list(srcs) + list(lands)])
    return out[0], out[1], out[2], list(out[3:3 + ns]), list(out[3 + ns:3 + ns + nl]), out[-1]


def _landed_block(piece, lands, owner):
    _, _, li, axis, base, stride, shape = piece
    return _window(lands[li], axis, base + stride * owner, shape[axis])


def _gather2_pass(lands, pieces, recv_ici, after, name):
    nl, npc, na = len(lands), len(pieces), len(after)

    def body(*refs):
        l_refs = refs[:nl]
        recv_sems = refs[nl]
        send_fwd, recv_fwd = refs[nl + 1 + na:nl + 3 + na]
        token = refs[-1]
        token[...] = jnp.zeros_like(token)
        sib, _ = _peer(_SIBLING)
        for p, piece in enumerate(pieces):
            for j, k in enumerate(_CHIP_FLIPS):
                pos, owner = _peer(k)
                block = _landed_block(piece, l_refs, owner)
                pltpu.make_async_remote_copy(src_ref=block, dst_ref=block, send_sem=send_fwd.at[3 * p + j],
                                             recv_sem=recv_sems.at[3 * p + j], device_id=pos,
                                             device_id_type=MESH_ID).wait_recv()
                pltpu.make_async_remote_copy(src_ref=block, dst_ref=block, send_sem=send_fwd.at[3 * p + j],
                                             recv_sem=recv_fwd.at[3 * p + j], device_id=sib,
                                             device_id_type=MESH_ID).start()

    hbm_of = lambda a: pltpu.HBM(a.shape, a.dtype)
    dma = lambda n: pltpu.SemaphoreType.DMA((n,))
    out = pl.pallas_call(
        body,
        in_specs=[HBM] * nl + [SEM] + [ANY] * na,
        out_specs=[SEM, SEM] + [HBM] * nl + [pl.BlockSpec(memory_space=pltpu.VMEM)],
        out_shape=[dma(3 * npc), dma(3 * npc)] + [hbm_of(a) for a in lands] + [jax.ShapeDtypeStruct((8, LANES), F32)],
        input_output_aliases={i: 2 + i for i in range(nl)},
        compiler_params=pltpu.CompilerParams(has_side_effects=DATAFLOW),
        name=name,
    )(*lands, recv_ici, *after)
    return out[0], out[1], list(out[2:2 + nl]), out[-1]


def _gather2_last(srcs, lands, pieces, send_sems, recv_sib, send_fwd, recv_fwd, name):
    ns, nl = len(srcs), len(lands)

    def body(*refs):
        s_refs, l_refs = refs[:ns], refs[ns:ns + nl]
        send_a, recv_s, send_f, recv_f = refs[ns + nl:ns + nl + 4]
        x, y, c = _mesh_pos()
        me = 4 * x + 2 * y + c
        sib, sib_index = _peer(_SIBLING)
        for p, piece in enumerate(pieces):
            src, dst = _piece_refs(piece, s_refs, l_refs, me, sib_index)
            own = lambda s_sem, r_sem: pltpu.make_async_remote_copy(
                src_ref=src, dst_ref=dst, send_sem=s_sem, recv_sem=r_sem, device_id=sib, device_id_type=MESH_ID)
            own(send_a.at[4 * p], recv_s.at[p]).wait_recv()
            for j in range(4):
                own(send_a.at[4 * p + j], recv_s.at[p]).wait_send()
            for j in range(3):
                fwd = own(send_f.at[3 * p + j], recv_f.at[3 * p + j])
                fwd.wait_recv()
                fwd.wait_send()

    hbm_of = lambda a: pltpu.HBM(a.shape, a.dtype)
    out = pl.pallas_call(
        body,
        in_specs=[HBM] * (ns + nl) + [SEM] * 4,
        out_specs=[HBM] * (ns + nl),
        out_shape=[hbm_of(a) for a in srcs] + [hbm_of(a) for a in lands],
        input_output_aliases={i: i for i in range(ns + nl)},
        compiler_params=pltpu.CompilerParams(has_side_effects=DATAFLOW),
        name=name,
    )(*srcs, *lands, send_sems, recv_sib, send_fwd, recv_fwd)
    return list(out[ns:])


_SMALL_NAMES = ("ln1_g", "ln1_b", "hg_lb_logits", "hg_norm_g", "sg_ln_g", "sg_ln_b", "sg_w_s", "sg_b_s",
                "ln2_g", "ln2_b", "mem_ln_g", "mem_ln_b", "ln3_g", "ln3_b", "ln4_g", "ln4_b")


_VEC_NAMES = ("ln1_g", "ln1_b", "ln2_g", "ln2_b", "mem_ln_g", "mem_ln_b", "ln3_g", "ln3_b", "ln4_g", "ln4_b")
_ROW_NAMES = ("hg_lb_logits", "hg_norm_g", "sg_ln_g", "sg_ln_b", "sg_b_s", "sg_w_s")
VEC_ROWS = 16


def _row_plan(shapes):
    plan, pos = {}, 0
    for k in _ROW_NAMES:
        shp = shapes[k]
        slabs, off = [], pos
        for idx in itertools.product(*[range(dim) for dim in shp[:-2]]):
            slabs.append((idx, off, shp[-2]))
            off += shp[-2]
        plan[k] = (pos, slabs)
        pos = -(-off // 8) * 8
    return plan, -(-pos // 16) * 16


def _pack_small_grads(gs, shapes, loss):
    d = gs[_VEC_NAMES[0]].size
    vec = jnp.concatenate([gs[k].reshape(1, -1) for k in _VEC_NAMES] + [jnp.tile(loss, (1, d // LANES))], axis=0)
    vec = jnp.pad(vec, ((0, VEC_ROWS - vec.shape[0]), (0, 0)))
    plan, total = _row_plan(shapes)
    parts, pos = [], 0
    for k in _ROW_NAMES:
        first, slabs = plan[k]
        rows = gs[k].reshape(-1, LANES)
        end = slabs[-1][1] + slabs[-1][2]
        nxt = -(-end // 8) * 8
        parts.append(jnp.pad(rows, ((0, nxt - first - rows.shape[0]), (0, 0))))
        pos = nxt
    parts.append(jnp.zeros((total - pos, LANES), F32))
    return vec, jnp.concatenate(parts, axis=0)


def _adam_small(land_vec, land_rows, w, m, v):
    names = _VEC_NAMES + _ROW_NAMES
    n = len(names)
    shapes = {k: w[k].shape for k in names}
    plan, _ = _row_plan(shapes)

    def body(*refs):
        lv_ref, lr_ref = refs[:2]
        w_refs, m_refs, v_refs = refs[2:2 + n], refs[2 + n:2 + 2 * n], refs[2 + 2 * n:2 + 3 * n]
        outs = refs[2 + 3 * n:2 + 7 * n]
        loss_ref = refs[2 + 7 * n]
        gv_s, gr_s = refs[3 + 7 * n:]
        gv_s[...] = _slot_sum(lv_ref)
        gr_s[...] = _slot_sum(lr_ref)
        loss_ref[...] = gv_s[len(_VEC_NAMES):len(_VEC_NAMES) + 1, :LANES]
        for p, k in enumerate(names):
            if k in _VEC_NAMES:
                row = _VEC_NAMES.index(k)
                slabs = [((), None, None)]
            else:
                slabs = plan[k][1]
            for idx, off, rows in slabs:
                g = gv_s[row:row + 1, :] if off is None else gr_s[off:off + rows, :]
                sel = idx + (slice(None), slice(None))
                delta, nm, nv = _adamw(w_refs[p][sel], g, m_refs[p][sel], v_refs[p][sel])
                for o, val in zip(range(4), (g, delta, nm, nv)):
                    outs[o * n + p][sel] = val

    flat = lambda tree: [tree[k] for k in names]
    shp = [jax.ShapeDtypeStruct(shapes[k], F32) for k in names]
    out = pl.pallas_call(
        body,
        out_shape=shp * 4 + [jax.ShapeDtypeStruct((1, LANES), F32)],
        scratch_shapes=[pltpu.VMEM(land_vec.shape[1:], F32), pltpu.VMEM(land_rows.shape[1:], F32)],
        name="adam_small",
    )(land_vec, land_rows, *flat(w), *flat(m), *flat(v))
    return [dict(zip(names, out[o * n:(o + 1) * n])) for o in range(4)], out[4 * n]


_COL_FFN = ("ffn1_w_gate", "ffn1_w_up", "ffn2_w_gate", "ffn2_w_up")
_ROW_FFN = ("ffn1_w_down", "ffn2_w_down")
_ROW_SQ = ("w_out", "xa_w_q", "xa_w_k", "xa_w_v", "xa_w_o")
_BIG_NAMES = ("ffn1_w_gate", "ffn1_w_up", "ffn1_w_down", "w_in", "w_out", "xa_w_q", "xa_w_k", "xa_w_v", "xa_w_o",
              "ffn2_w_gate", "ffn2_w_up", "ffn2_w_down")


def _ffn_split(fs):
    main = (fs // MXU_WIDTH_V7X) * MXU_WIDTH_V7X
    tail = fs - main
    tail_pad = -(-tail // LANES) * LANES
    assert main > 0 and tail > 0
    return main, tail, tail_pad


def _layout(name, shard_shape):
    r, c = shard_shape
    if name in _COL_FFN:
        main, tail, pad = _ffn_split(c)
        return (r, N_DEV * (main + pad)), [(1, 0, main, (r, main), (0, main)),
                                           (1, N_DEV * main, pad, (r, pad), (main, c))]
    if name in _ROW_FFN:
        main, tail, pad = _ffn_split(r)
        return (N_DEV * (main + pad), c), [(0, 0, main, (main, c), (0, main)),
                                           (0, N_DEV * main, pad, (pad, c), (main, r))]
    if name == "w_in":
        return (r, N_DEV * c), [(1, 0, c, (r, c), (0, c))]
    return (N_DEV * r, c), [(0, 0, r, (r, c), (0, r))]


def _shard_pieces(name, shard):
    out = []
    for axis, _, _, shape, (lo, hi) in _layout(name, shard.shape)[1]:
        part = shard[lo:hi, :] if axis == 0 else shard[:, lo:hi]
        pad = [(0, shape[0] - part.shape[0]), (0, shape[1] - part.shape[1])]
        out.append(jnp.pad(part, pad).astype(BF16))
    return out


def _gather_plan(names, shards):
    srcs, land_shapes, pieces, index = [], [], [], {}
    for li, name in enumerate(names):
        shape2d, parts = _layout(name, shards[name].shape)
        land_shapes.append(jax.ShapeDtypeStruct(shape2d, BF16))
        index[name] = []
        for (axis, base, stride, shape, _), src in zip(parts, _shard_pieces(name, shards[name])):
            index[name].append(len(pieces))
            pieces.append(("gather", len(srcs), li, axis, base, stride, shape))
            srcs.append(src)
    return srcs, land_shapes, pieces, index


def _scatter_plan(names, grads, shard_shapes):
    srcs, land_shapes, pieces, index = [], [], [], {}
    for si, name in enumerate(names):
        _, parts = _layout(name, shard_shapes[name])
        srcs.append(grads[name])
        index[name] = []
        for axis, base, stride, shape, _ in parts:
            index[name].append(len(land_shapes))
            pieces.append(("scatter", si, len(land_shapes), axis, base, stride, shape))
            land_shapes.append(jax.ShapeDtypeStruct((N_DEV,) + shape, grads[name].dtype))
    return srcs, land_shapes, pieces, index


def _small_views(small):
    row = lambda a: a.reshape(1, -1)
    ln = {k: row(small[k]) for k in ("ln1_g", "ln1_b", "ln2_g", "ln2_b", "ln3_g", "ln3_b", "ln4_g", "ln4_b",
                                      "mem_ln_g", "mem_ln_b", "hg_norm_g")}
    sg_w = small["sg_w_s"].reshape(SG_GROUPS, SG_CHUNK, SG_CHUNK)
    sg = dict(logits=jnp.swapaxes(small["hg_lb_logits"], 0, 1),
              g=small["sg_ln_g"].reshape(SG_GROUPS, 1, SG_DIM), b=small["sg_ln_b"].reshape(SG_GROUPS, 1, SG_DIM),
              w=sg_w, wt=jnp.swapaxes(sg_w, 1, 2), bs=small["sg_b_s"].reshape(SG_GROUPS, SG_CHUNK, 1))
    return ln, sg


def _forward(x, mem, target, get_w, small, first_deps=()):
    ln, sg = _small_views(small)
    xb = x.astype(BF16)
    a1, b1, s1 = _ffn_up(xb, get_w("ffn1_w_gate", ()), get_w("ffn1_w_up", ()), "ffn1_up", deps=first_deps)
    h1b, xh1, rs1 = _mm_res_ln(s1, get_w("ffn1_w_down", (s1,)), x, ln["ln1_g"], ln["ln1_b"], 0.5, "ffn1_down_ln")
    proj = _mm_nn(h1b, get_w("w_in", (h1b,)), "mix_in")
    oraw, mix, states = _hgrn_fwd(proj, sg["logits"], ln["hg_norm_g"])
    mix = _sgu_fwd(proj, mix, sg["g"], sg["b"], sg["w"], sg["bs"])
    h2b, xh2, rs2 = _mm_res_ln(mix, get_w("w_out", (mix,)), (xh1, ln["ln1_g"], ln["ln1_b"]), ln["ln2_g"], ln["ln2_b"],
                               1.0, "mix_out_ln")
    mb, mxh, mrs, kb, vb = _mem_kv(mem, ln["mem_ln_g"], ln["mem_ln_b"], get_w("xa_w_k", (h2b,)), get_w("xa_w_v", (h2b,)))
    qb, att = _attn_fwd(h2b, get_w("xa_w_q", (kb,)), kb, vb)
    h3b, xh3, rs3 = _mm_res_ln(att, get_w("xa_w_o", (att,)), (xh2, ln["ln2_g"], ln["ln2_b"]), ln["ln3_g"], ln["ln3_b"],
                               1.0, "attn_out_ln")
    a2, b2, s2 = _ffn_up(h3b, get_w("ffn2_w_gate", (h3b,)), get_w("ffn2_w_up", (h3b,)), "ffn2_up")
    loss, dy4, dy4b, dg4, db4 = _mm_res_ln(s2, get_w("ffn2_w_down", (s2,)), (xh3, ln["ln3_g"], ln["ln3_b"]),
                                           ln["ln4_g"], ln["ln4_b"], 0.5, "ffn2_down_ln_loss", target=target)
    return dict(xb=xb, a1=a1, b1=b1, s1=s1, h1b=h1b, xh1=xh1, rs1=rs1, proj=proj, oraw=oraw, mix=mix, states=states,
                h2b=h2b, xh2=xh2, rs2=rs2, mb=mb, mxh=mxh, mrs=mrs, kb=kb, vb=vb, qb=qb, att=att, h3b=h3b, xh3=xh3,
                rs3=rs3, a2=a2, b2=b2, s2=s2, loss=loss, dy4=dy4, dy4b=dy4b, dg4=dg4, db4=db4)


def _backward(sv, wt, small, send):
    ln, sg = _small_views(small)
    gs = {"ln4_g": sv["dg4"], "ln4_b": sv["db4"]}
    loss, dy4, dy4b = sv["loss"], sv["dy4"], sv["dy4b"]
    g_down2 = _mm_tn(sv["s2"], dy4b, "g_ffn2_down", scale=0.5)
    da2, db2, dy3, dy3b, gs["ln3_g"], gs["ln3_b"] = _ffn_bwd_fused(
        dy4, dy4b, wt["ffn2_w_down"], wt["ffn2_w_gate"], wt["ffn2_w_up"], sv["a2"], sv["b2"], 0.5,
        (sv["xh3"], sv["rs3"], ln["ln3_g"]), "ffn2_bwd")
    g_gate2 = _mm_tn(sv["h3b"], da2, "g_ffn2_gate")
    g_up2 = _mm_tn(sv["h3b"], db2, "g_ffn2_up")
    tok = send(("ffn2_w_down", "ffn2_w_gate", "ffn2_w_up"), (g_down2, g_gate2, g_up2))

    g_o = _mm_tn(sv["att"], dy3b, "g_xa_o", deps=(tok,))
    dqb, dk, dv = _attn_bwd(dy3b, wt["xa_w_o"], sv["qb"], sv["kb"], sv["vb"])
    g_q = _mm_tn(sv["h2b"], dqb, "g_xa_q")
    g_k, g_v, gs["mem_ln_g"], gs["mem_ln_b"] = _mem_bwd(dk, dv, sv["mb"], sv["mxh"], sv["mrs"], ln["mem_ln_g"],
                                                        wt["xa_w_k"], wt["xa_w_v"])
    tok = send(("xa_w_o", "xa_w_q", "xa_w_k", "xa_w_v"), (g_o, g_q, g_k, g_v))
    dy2, dy2b, gs["ln2_g"], gs["ln2_b"] = _dx_ln(dy3, [(dqb, wt["xa_w_q"])], (sv["xh2"], sv["rs2"], ln["ln2_g"]),
                                                 "attn_dx_ln", deps=(tok,))

    g_out = _mm_tn(sv["mix"], dy2b, "g_w_out")
    dmix = _mm_nt(dy2b, wt["w_out"], "mix_out_bwd")
    dq, dfz, div, dgg, dlg, dgn = _hgrn_bwd(sv["proj"], sv["oraw"], dmix, sv["states"], sg["logits"], ln["hg_norm_g"])
    du, dvv, gs["sg_ln_g"], gs["sg_ln_b"], gs["sg_w_s"], gs["sg_b_s"] = _sgu_bwd(
        sv["proj"], dmix, sg["g"], sg["b"], sg["w"], sg["wt"], sg["bs"])
    gs["hg_lb_logits"] = jnp.swapaxes(dlg, 0, 1)
    gs["hg_norm_g"] = jnp.sum(dgn, axis=0)
    dproj = jnp.concatenate([dq, dfz, div, dgg, du, dvv], axis=1)
    g_in = _mm_tn(sv["h1b"], dproj, "g_w_in")
    tok = send(("w_out", "w_in"), (g_out, g_in))
    dy1, dy1b, gs["ln1_g"], gs["ln1_b"] = _dx_ln(dy2, [(dproj, wt["w_in"])], (sv["xh1"], sv["rs1"], ln["ln1_g"]),
                                                 "mix_dx_ln", deps=(tok,))

    g_down1 = _mm_tn(sv["s1"], dy1b, "g_ffn1_down", scale=0.5)
    tok = send(("ffn1_w_down",), (g_down1,))
    da1, db1 = _ffn_bwd_act(dy1b, wt["ffn1_w_down"], sv["a1"], sv["b1"], 0.5, "ffn1_bwd_act", deps=(tok,))
    g_gate1 = _mm_tn(sv["xb"], da1, "g_ffn1_gate")
    tok = send(("ffn1_w_gate",), (g_gate1,))
    g_up1 = _mm_tn(sv["xb"], db1, "g_ffn1_up", deps=(tok,))
    tok = send(("ffn1_w_up",), (g_up1,))
    grad_x = _dx_ln(dy1, [(da1, wt["ffn1_w_gate"]), (db1, wt["ffn1_w_up"])], None, "ffn1_dx", deps=(tok,))
    return loss, grad_x, gs


_WEIGHT_NAMES = ("ffn1_w_gate", "ffn1_w_up", "ffn1_w_down", "ln1_g", "ln1_b", "w_in", "hg_lb_logits", "hg_norm_g",
                 "sg_ln_g", "sg_ln_b", "sg_w_s", "sg_b_s", "w_out", "ln2_g", "ln2_b", "mem_ln_g", "mem_ln_b",
                 "xa_w_q", "xa_w_k", "xa_w_v", "xa_w_o", "ln3_g", "ln3_b", "ffn2_w_gate", "ffn2_w_up", "ffn2_w_down",
                 "ln4_g", "ln4_b")
_FIRST = ("ffn1_w_gate", "ffn1_w_up")
_SECOND = ("ffn1_w_down", "w_in", "w_out")
_THIRD = ("xa_w_k", "xa_w_v", "xa_w_q", "xa_w_o", "ffn2_w_gate", "ffn2_w_up", "ffn2_w_down")


def kernel(x, mem, ffn1_w_gate, ffn1_w_up, ffn1_w_down, ln1_g, ln1_b, w_in, hg_lb_logits, hg_norm_g, sg_ln_g, sg_ln_b, sg_w_s, sg_b_s, w_out, ln2_g, ln2_b, mem_ln_g, mem_ln_b, xa_w_q, xa_w_k, xa_w_v, xa_w_o, ln3_g, ln3_b, ffn2_w_gate, ffn2_w_up, ffn2_w_down, ln4_g, ln4_b, loss_target, m_ffn1_w_gate, m_ffn1_w_up, m_ffn1_w_down, m_ln1_g, m_ln1_b, m_w_in, m_hg_lb_logits, m_hg_norm_g, m_sg_ln_g, m_sg_ln_b, m_sg_w_s, m_sg_b_s, m_w_out, m_ln2_g, m_ln2_b, m_mem_ln_g, m_mem_ln_b, m_xa_w_q, m_xa_w_k, m_xa_w_v, m_xa_w_o, m_ln3_g, m_ln3_b, m_ffn2_w_gate, m_ffn2_w_up, m_ffn2_w_down, m_ln4_g, m_ln4_b, v_ffn1_w_gate, v_ffn1_w_up, v_ffn1_w_down, v_ln1_g, v_ln1_b, v_w_in, v_hg_lb_logits, v_hg_norm_g, v_sg_ln_g, v_sg_ln_b, v_sg_w_s, v_sg_b_s, v_w_out, v_ln2_g, v_ln2_b, v_mem_ln_g, v_mem_ln_b, v_xa_w_q, v_xa_w_k, v_xa_w_v, v_xa_w_o, v_ln3_g, v_ln3_b, v_ffn2_w_gate, v_ffn2_w_up, v_ffn2_w_down, v_ln4_g, v_ln4_b):
    args = dict(locals())
    w = {k: args[k] for k in _WEIGHT_NAMES}
    m = {k: args["m_" + k] for k in _WEIGHT_NAMES}
    v = {k: args["v_" + k] for k in _WEIGHT_NAMES}
    shards = {k: w[k][0] for k in _BIG_NAMES}
    shard_shapes = {k: shards[k].shape for k in _BIG_NAMES}
    small = {k: (w[k][0] if k != "hg_lb_logits" else w[k]) for k in _SMALL_NAMES}

    srcs1, shapes1, pieces1, idx1 = _gather_plan(_FIRST, shards)
    lands1 = _place_own(srcs1, shapes1, pieces1, "gather_first_own")
    rest = _SECOND + _THIRD
    srcs2, shapes2, pieces2, idx2 = _gather_plan(rest, shards)
    lands2 = _place_own(srcs2, shapes2, pieces2, "gather_rest_own")
    groups2 = [list(idx2[k]) for k in rest]
    send1, rsib1, rici1, srcs1, lands1, _ = _gather2_first(srcs1, lands1, pieces1, "gather_first_start")
    sfwd1, rfwd1, lands1, tok1 = _gather2_pass(lands1, pieces1, rici1, tuple(lands2), "gather_first_pass")
    sems2, srcs2, lands2, tok2 = _comm_start(srcs2, lands2, pieces2, groups2, "gather_rest_start", after=(tok1,))
    lands1 = _gather2_last(srcs1, lands1, pieces1, send1, rsib1, sfwd1, rfwd1, "gather_first_wait")
    wt = dict(zip(_FIRST, lands1))
    pending = {k: gi for gi, k in enumerate(rest)}

    def get_w(name, after):
        if name in pending:
            gi = pending.pop(name)
            si = [pieces2[p][1] for p in groups2[gi]]
            sub = [(pieces2[p][0], row, 0) + pieces2[p][3:] for row, p in enumerate(groups2[gi])]
            wt[name] = _comm_wait([srcs2[s] for s in si], [lands2[gi]], sub, list(range(len(sub))), sems2[gi],
                                  after, "gather_wait_" + name)[0]
        return wt[name]

    sv = _forward(x[0], mem[0], loss_target[0], get_w, small, first_deps=(tok2,))

    sent = []

    def send(names, grads):
        srcs, shapes, pieces, idx = _scatter_plan(names, dict(zip(names, grads)), shard_shapes)
        lands = _place_own(srcs, shapes, pieces, "grads_own_%d" % len(sent))
        sems, srcs, lands, tok = _comm_start(srcs, lands, pieces, [list(range(len(pieces)))],
                                             "grads_start_%d" % len(sent))
        sent.append((names, srcs, lands, pieces, idx, sems[0]))
        return tok

    loss, grad_x, gs = _backward(sv, wt, small, send)

    ssrc = list(_pack_small_grads(gs, {k: w[k].shape for k in _SMALL_NAMES}, loss))
    sp = [("scatter", i, i, 0, 0, 0, a.shape) for i, a in enumerate(ssrc)]
    sshape = [jax.ShapeDtypeStruct((N_DEV,) + a.shape, F32) for a in ssrc]
    sl = _place_own(ssrc, sshape, sp, "small_own")
    ssem, ssrc, sl, _ = _comm_start(ssrc, sl, sp, [[0, 1]], "small_start")

    out_g, out_d, out_m, out_v = {}, {}, {}, {}
    after = (grad_x,)
    for n_sent, (names, srcs, lands, pieces, idx, sems) in enumerate(sent):
        lands = _comm_wait(srcs, lands, pieces, list(range(len(pieces))), sems, after, "grads_wait_%d" % n_sent)
        for k in names:
            axis = 1 if (k in _COL_FFN or k == "w_in") else 0
            if k in _COL_FFN:
                res = _adam_sharded([lands[i] for i in idx[k]], w[k][0].T, m[k][0].T, v[k][0].T, axis, "adam_" + k)
                res = [r.T for r in res]
            else:
                res = _adam_sharded([lands[i] for i in idx[k]], w[k][0], m[k][0], v[k][0], axis, "adam_" + k)
            out_g[k], out_d[k], out_m[k], out_v[k] = [r[None] for r in res]
        after = (out_v[names[-1]],)
    sl = _comm_wait(ssrc, sl, sp, [0, 1], ssem[0], after, "small_wait")
    small_out, loss_sum = _adam_small(sl[0], sl[1], w, m, v)
    for dst, res in zip((out_g, out_d, out_m, out_v), small_out):
        dst.update(res)
    loss_all = loss_sum[0, 0]
    return (loss_all, grad_x[None], *[out_g[k] for k in _WEIGHT_NAMES], *[out_d[k] for k in _WEIGHT_NAMES],
            *[out_m[k] for k in _WEIGHT_NAMES], *[out_v[k] for k in _WEIGHT_NAMES])
```

```python
import itertools

import jax
import jax.numpy as jnp
import numpy as np
from jax import lax
from jax.experimental import pallas as pl
from jax.experimental.pallas import tpu as pltpu

F32 = jnp.float32
BF16 = jnp.bfloat16

N_DEV = 8
ALPHA = 2.0 ** 0.25
LN_EPS = 1e-5
HG_HEADS = 4
HG_DIM = 128
SG_GROUPS = 4
SG_DIM = 128
SG_CHUNK = 128
X_HEADS = 4
HG_BLOCK = 16
HG_UNROLL = 8
ADAM_LR = 0.001
ADAM_B1 = 0.9
ADAM_B2 = 0.999
ADAM_EPS = 1e-08
ADAM_WD = 0.01
ADAM_STEP = 10
VMEM_LIMIT_V7X = 48 * 1024 * 1024
MXU_WIDTH_V7X = 256
LANES = 128
MESH_ID = pl.DeviceIdType.MESH
ANY = pl.BlockSpec(memory_space=pl.ANY)
HBM = pl.BlockSpec(memory_space=pltpu.HBM)
SEM = pl.BlockSpec(memory_space=pltpu.SEMAPHORE)
DATAFLOW = pltpu.SideEffectType.DATAFLOW_SIDE_EFFECTING


def _params(n_axes):
    return pltpu.CompilerParams(dimension_semantics=("arbitrary",) * n_axes, vmem_limit_bytes=VMEM_LIMIT_V7X)


def _dot(a, b):
    return jnp.dot(a, b, preferred_element_type=F32)


def _dot_nt(a, b):
    return lax.dot_general(a, b, (((1,), (1,)), ((), ())), preferred_element_type=F32)


def _dot_tn(a, b):
    return lax.dot_general(a, b, (((0,), (0,)), ((), ())), preferred_element_type=F32)


def _sigmoid(x):
    return 1.0 / (1.0 + jnp.exp(-x))


def _silu_and_grad(a):
    sig = _sigmoid(a)
    return a * sig, sig * (1.0 + a * (1.0 - sig))


_GELU_C = 0.7978845608028654


def _gelu_and_grad(x):
    inner = _GELU_C * (x + 0.044715 * x * x * x)
    t = jnp.tanh(inner)
    val = 0.5 * x * (1.0 + t)
    grad = 0.5 * (1.0 + t) + 0.5 * x * (1.0 - t * t) * _GELU_C * (1.0 + 3.0 * 0.044715 * x * x)
    return val, grad


def _ln_fwd(y, g, b):
    mu = jnp.mean(y, axis=-1, keepdims=True)
    yc = y - mu
    var = jnp.mean(yc * yc, axis=-1, keepdims=True)
    rstd = lax.rsqrt(var + LN_EPS)
    xhat = yc * rstd
    return xhat * g + b, xhat, rstd


def _ln_bwd(dh, xhat, rstd, g):
    dxh = dh * g
    m1 = jnp.mean(dxh, axis=-1, keepdims=True)
    m2 = jnp.mean(dxh * xhat, axis=-1, keepdims=True)
    dy = rstd * (dxh - m1 - xhat * m2)
    dg = jnp.sum(dh * xhat, axis=0, keepdims=True)
    db = jnp.sum(dh, axis=0, keepdims=True)
    return dy, dg, db


def _mask_dot(mask, x):
    hi = x.astype(BF16)
    lo = (x - hi.astype(F32)).astype(BF16)
    return _dot(mask, hi) + _dot(mask, lo)


def _block_masks(n):
    r = np.arange(n)[:, None]
    c = np.arange(n)[None, :]
    same = (r // HG_BLOCK) == (c // HG_BLOCK)
    return jnp.asarray(np.stack([same & (c <= r), same & (c >= r), same]), BF16)


def _row_tile(t):
    return min(t, 512)


def _col_tile(n):
    for cand in (512, 256, 128):
        if n % cand == 0:
            return cand
    return n


def _resident(w):
    return pl.BlockSpec(w.shape, lambda *_: (0, 0), pipeline_mode=pl.Buffered(1))


def _drop_deps(body, n_in, n_deps):
    if n_deps == 0:
        return body
    return lambda *refs: body(*refs[:n_in], *refs[n_in + n_deps:])


def _ffn_up(hb, wg, wu, name, deps=()):
    t, d = hb.shape
    f = wg.shape[1]
    tm = min(t, 256)
    tn = _col_tile(f)

    def body(h_ref, wg_ref, wu_ref, a_ref, b_ref, s_ref):
        h = h_ref[...]
        for c in range(f // tn):
            cols = slice(c * tn, (c + 1) * tn)
            a = _dot(h, wg_ref[:, cols])
            b = _dot(h, wu_ref[:, cols])
            a_ref[:, cols] = a.astype(BF16)
            b_ref[:, cols] = b.astype(BF16)
            s_ref[:, cols] = (a * _sigmoid(a) * b).astype(BF16)

    act = pl.BlockSpec((tm, f), lambda i: (i, 0))
    return pl.pallas_call(
        _drop_deps(body, 3, len(deps)),
        grid=(t // tm,),
        in_specs=[pl.BlockSpec((tm, d), lambda i: (i, 0)), _resident(wg), _resident(wu)] + [ANY] * len(deps),
        out_specs=[act, act, act],
        out_shape=[jax.ShapeDtypeStruct((t, f), BF16)] * 3,
        compiler_params=_params(1),
        name=name,
    )(hb, wg, wu, *deps)


def _mm_res_ln(lhs, w, res, g, b, coef, name, target=None):
    t, kd = lhs.shape
    d = w.shape[1]
    tm = _row_tile(t)
    nt = t // tm
    from_norm = isinstance(res, tuple)
    n_res = 3 if from_norm else 1

    def body(*refs):
        l_ref, w_ref = refs[:2]
        r_refs = refs[2:2 + n_res]
        g_ref, b_ref = refs[2 + n_res:4 + n_res]
        rest = refs[4 + n_res:]
        prev = r_refs[0][...] * r_refs[1][...] + r_refs[2][...] if from_norm else r_refs[0][...]
        y = ALPHA * prev + coef * _dot(l_ref[...], w_ref[...])
        h, xhat, rstd = _ln_fwd(y, g_ref[...], b_ref[...])
        if target is None:
            hb_ref, xh_ref, rs_ref = rest
            hb_ref[...] = h.astype(BF16)
            xh_ref[...] = xhat
            rs_ref[...] = rstd
            return
        t_ref, loss_ref, dy_ref, dyb_ref, dg_ref, db_ref, lacc = rest
        i = pl.program_id(0)

        @pl.when(i == 0)
        def _():
            lacc[...] = jnp.zeros_like(lacc)
            dg_ref[...] = jnp.zeros_like(dg_ref)
            db_ref[...] = jnp.zeros_like(db_ref)

        err = h - t_ref[...]
        lacc[...] += jnp.sum(err * err, axis=0, keepdims=True)
        dy, dg, db = _ln_bwd(err * (1.0 / d), xhat, rstd, g_ref[...])
        dy_ref[...] = dy
        dyb_ref[...] = dy.astype(BF16)
        dg_ref[...] += dg
        db_ref[...] += db

        @pl.when(i == nt - 1)
        def _():
            loss_ref[...] = jnp.zeros_like(loss_ref) + jnp.sum(lacc[...], axis=1, keepdims=True) * (0.5 / d)

    row = pl.BlockSpec((tm, d), lambda i: (i, 0))
    vec = pl.BlockSpec((1, d), lambda i: (0, 0))
    res_specs = [row, vec, vec] if from_norm else [row]
    res_args = list(res) if from_norm else [res]
    in_specs = [pl.BlockSpec((tm, kd), lambda i: (i, 0)), _resident(w)] + res_specs + [vec, vec]
    args = [lhs, w] + res_args + [g, b]
    if target is None:
        out_specs = [row, row, pl.BlockSpec((tm, 1), lambda i: (i, 0))]
        out_shape = [jax.ShapeDtypeStruct((t, d), BF16), jax.ShapeDtypeStruct((t, d), F32),
                     jax.ShapeDtypeStruct((t, 1), F32)]
        scratch = []
    else:
        in_specs.append(row)
        args.append(target)
        out_specs = [pl.BlockSpec((1, LANES), lambda i: (0, 0)), row, row, vec, vec]
        out_shape = [jax.ShapeDtypeStruct((1, LANES), F32), jax.ShapeDtypeStruct((t, d), F32),
                     jax.ShapeDtypeStruct((t, d), BF16), jax.ShapeDtypeStruct((1, d), F32),
                     jax.ShapeDtypeStruct((1, d), F32)]
        scratch = [pltpu.VMEM((1, d), F32)]
    return pl.pallas_call(
        body,
        grid=(nt,),
        in_specs=in_specs,
        out_specs=out_specs,
        out_shape=out_shape,
        scratch_shapes=scratch,
        compiler_params=_params(1),
        name=name,
    )(*args)


def _mm_nn(lhs, w, name):
    t, kd = lhs.shape
    n = w.shape[1]
    tm = _row_tile(t)
    tn = _col_tile(n)

    def body(l_ref, w_ref, o_ref):
        lhs_v = l_ref[...]
        for c in range(n // tn):
            cols = slice(c * tn, (c + 1) * tn)
            o_ref[:, cols] = _dot(lhs_v, w_ref[:, cols])

    return pl.pallas_call(
        body,
        grid=(t // tm,),
        in_specs=[pl.BlockSpec((tm, kd), lambda i: (i, 0)), _resident(w)],
        out_specs=pl.BlockSpec((tm, n), lambda i: (i, 0)),
        out_shape=jax.ShapeDtypeStruct((t, n), F32),
        compiler_params=_params(1),
        name=name,
    )(lhs, w)


def _lower_bound(lg):
    m = jnp.max(lg, axis=0, keepdims=True)
    e = jnp.exp(lg - m)
    return e[0:1, :] / jnp.sum(e, axis=0, keepdims=True)


def _forget_terms(fz, lb):
    e = jnp.exp(-jnp.abs(fz))
    r = 1.0 / (1.0 + e)
    pos = fz >= 0.0
    sig = jnp.where(pos, r, e * r)
    nsig = jnp.where(pos, e * r, r)
    f = lb + (1.0 - lb) * sig
    k = (1.0 - lb) * nsig
    return sig, nsig, f, k


def _hg_tile(t):
    return min(t, 256)


HG_HALF = HG_BLOCK // 2
NEG_BIG = -1e30


def _halves(a):
    return a[:HG_HALF, :], a[HG_HALF:, :]


def _causal_halves(s):
    return (0, 1) if s < HG_HALF else (1,)


def _decay_from(b_half, b_s, s, h, tidx):
    first = s - h * HG_HALF
    diff = b_half - b_s
    if first > 0:
        diff = jnp.where(tidx >= first, diff, NEG_BIG)
    return jnp.exp(diff)


def _hgrn_fwd(proj, logits, gn):
    t = proj.shape[0]
    ct = _hg_tile(t)
    nct = t // ct
    nblk = ct // HG_BLOCK
    nh = HG_HEADS

    def body(q_ref, fz_ref, iv_ref, gg_ref, lg_ref, gn_ref, mask_ref, oraw_ref, oa_ref, st_ref,
             state, qt_s, kt_s, k_s, b_s, dec_s):
        c = pl.program_id(1)

        @pl.when(c == 0)
        def _():
            state[...] = jnp.zeros_like(state)

        lb = _lower_bound(lg_ref[...])
        q = q_ref[...]
        _, _, f, k = _forget_terms(fz_ref[...], lb)
        logf = jnp.log(f)
        b = _mask_dot(mask_ref[0], logf)
        bend = _mask_dot(mask_ref[2], logf)
        qt_s[...] = (q * jnp.exp(b)).astype(BF16)
        kt_s[...] = (k * jnp.exp(bend - b)).astype(BF16)
        k_s[...] = k
        b_s[...] = b
        dec_s[...] = jnp.exp(bend)
        tidx = lax.broadcasted_iota(jnp.int32, (HG_HALF, HG_DIM), 0)

        def blk(i, carry):
            r0 = pl.multiple_of(i * HG_BLOCK, HG_BLOCK)
            rows = pl.ds(r0, HG_BLOCK)
            st = state[...]
            st_ref[i] = st
            v = iv_ref[rows, :]
            qq = q_ref[rows, :]
            kk = k_s[rows, :]
            bb = b_s[rows, :]
            o = list(_halves(_dot_nt(qt_s[rows, :], st.astype(BF16))))
            qh, bh = _halves(qq), _halves(bb)
            for s in range(HG_BLOCK):
                ks, vs = kk[s:s + 1, :], v[s:s + 1, :]
                for h in _causal_halves(s):
                    e = _decay_from(bh[h], bb[s:s + 1, :], s, h, tidx)
                    acol = jnp.sum(qh[h] * (ks * e), axis=1, keepdims=True)
                    o[h] = o[h] + acol * vs
            oraw_ref[rows, :] = jnp.concatenate(o, axis=0)
            state[...] = st * dec_s[pl.ds(r0, 1), :] + _dot_tn(v.astype(BF16), kt_s[rows, :])
            return carry

        lax.fori_loop(0, nblk, blk, 0, unroll=2 * HG_UNROLL)
        oraw = oraw_ref[...]
        r = lax.rsqrt(jnp.mean(oraw * oraw, axis=-1, keepdims=True) + LN_EPS)
        gg = gg_ref[...]
        oa_ref[...] = (oraw * r * gn_ref[...] * gg * _sigmoid(gg)).astype(BF16)

    def slab(off):
        return pl.BlockSpec((ct, HG_DIM), lambda h, c: (c, off + h))

    out_slab = pl.BlockSpec((ct, HG_DIM), lambda h, c: (c, h))
    return pl.pallas_call(
        body,
        grid=(nh, nct),
        in_specs=[slab(0), slab(nh), slab(2 * nh), slab(3 * nh),
                  pl.BlockSpec((None, 2, HG_DIM), lambda h, c: (h, 0, 0)),
                  pl.BlockSpec((1, HG_DIM), lambda h, c: (0, 0)),
                  pl.BlockSpec((3, ct, ct), lambda h, c: (0, 0, 0))],
        out_specs=[out_slab, out_slab, pl.BlockSpec((None, nblk, HG_DIM, HG_DIM), lambda h, c: (h, c, 0, 0))],
        out_shape=[jax.ShapeDtypeStruct((t, nh * HG_DIM), F32),
                   jax.ShapeDtypeStruct((t, (nh + SG_GROUPS) * HG_DIM), BF16),
                   jax.ShapeDtypeStruct((nh, t // HG_BLOCK, HG_DIM, HG_DIM), F32)],
        scratch_shapes=[pltpu.VMEM((HG_DIM, HG_DIM), F32), pltpu.VMEM((ct, HG_DIM), BF16),
                        pltpu.VMEM((ct, HG_DIM), BF16), pltpu.VMEM((ct, HG_DIM), F32),
                        pltpu.VMEM((ct, HG_DIM), F32), pltpu.VMEM((ct, HG_DIM), F32)],
        compiler_params=_params(2),
        name="hgrn_fwd",
    )(proj, proj, proj, proj, logits, gn, _block_masks(ct))


def _sg_tile(t):
    return min(t, 512)


def _sgu_chunk_fwd(u, v, ln_g, ln_b, wm, bs):
    ua, dua = _gelu_and_grad(u)
    va, dva = _gelu_and_grad(v)
    vn, xhat, rstd = _ln_fwd(va, ln_g, ln_b)
    s = _dot(wm, vn.astype(BF16)) + bs
    return ua, dua, dva, vn, xhat, rstd, s


def _tril_weight(w_ref):
    n = SG_CHUNK
    r = lax.broadcasted_iota(jnp.int32, (n, n), 0)
    c = lax.broadcasted_iota(jnp.int32, (n, n), 1)
    return jnp.where(c <= r, w_ref[...], 0.0)


def _sgu_fwd(proj, mix, ln_g, ln_b, w_s, b_col):
    t = proj.shape[0]
    ct = _sg_tile(t)
    ng = SG_GROUPS
    off_u = 4 * HG_HEADS
    off_v = off_u + ng

    def body(u_ref, v_ref, g_ref, b_ref, w_ref, bs_ref, mix_ref, o_ref):
        del mix_ref
        wm = _tril_weight(w_ref).astype(BF16)
        for n in range(ct // SG_CHUNK):
            rows = slice(n * SG_CHUNK, (n + 1) * SG_CHUNK)
            ua, _, _, _, _, _, s = _sgu_chunk_fwd(u_ref[rows, :], v_ref[rows, :], g_ref[...], b_ref[...], wm, bs_ref[...])
            o_ref[rows, :] = (ua * s).astype(BF16)

    vec = pl.BlockSpec((None, 1, SG_DIM), lambda g, c: (g, 0, 0))
    return pl.pallas_call(
        body,
        grid=(ng, t // ct),
        in_specs=[pl.BlockSpec((ct, SG_DIM), lambda g, c: (c, off_u + g)),
                  pl.BlockSpec((ct, SG_DIM), lambda g, c: (c, off_v + g)), vec, vec,
                  pl.BlockSpec((None, SG_CHUNK, SG_CHUNK), lambda g, c: (g, 0, 0)),
                  pl.BlockSpec((None, SG_CHUNK, 1), lambda g, c: (g, 0, 0)), ANY],
        out_specs=pl.BlockSpec((ct, SG_DIM), lambda g, c: (c, HG_HEADS + g)),
        out_shape=jax.ShapeDtypeStruct(mix.shape, mix.dtype),
        input_output_aliases={6: 0},
        compiler_params=_params(2),
        name="sgu_fwd",
    )(proj, proj, ln_g, ln_b, w_s, b_col, mix)


def _mem_kv(mem, g, b, wk, wv):
    m_len, d = mem.shape

    def body(m_ref, g_ref, b_ref, wk_ref, wv_ref, mb_ref, xh_ref, rs_ref, k_ref, v_ref):
        m, xhat, rstd = _ln_fwd(m_ref[...], g_ref[...], b_ref[...])
        mb = m.astype(BF16)
        mb_ref[...] = mb
        xh_ref[...] = xhat
        rs_ref[...] = rstd
        k_ref[...] = _dot(mb, wk_ref[...]).astype(BF16)
        v_ref[...] = _dot(mb, wv_ref[...]).astype(BF16)

    return pl.pallas_call(
        body,
        out_shape=[jax.ShapeDtypeStruct((m_len, d), BF16), jax.ShapeDtypeStruct((m_len, d), F32),
                   jax.ShapeDtypeStruct((m_len, 1), F32), jax.ShapeDtypeStruct((m_len, d), BF16),
                   jax.ShapeDtypeStruct((m_len, d), BF16)],
        compiler_params=pltpu.CompilerParams(vmem_limit_bytes=VMEM_LIMIT_V7X),
        name="mem_kv",
    )(mem, g, b, wk, wv)


def _softmax_rows(s):
    m = jnp.max(s, axis=-1, keepdims=True)
    p = jnp.exp(s - m)
    return p / jnp.sum(p, axis=-1, keepdims=True)


def _attn_fwd(hb, wq, kb, vb):
    t, d = hb.shape
    tm = _row_tile(t)
    dh = d // X_HEADS
    scale = dh ** -0.5

    def body(h_ref, wq_ref, k_ref, v_ref, q_ref, o_ref):
        q = _dot(h_ref[...], wq_ref[...]).astype(BF16)
        q_ref[...] = q
        for hd in range(X_HEADS):
            sl = slice(hd * dh, (hd + 1) * dh)
            p = _softmax_rows(_dot_nt(q[:, sl], k_ref[:, sl]) * scale)
            o_ref[:, sl] = _dot(p.astype(BF16), v_ref[:, sl]).astype(BF16)

    row = pl.BlockSpec((tm, d), lambda i: (i, 0))
    full = lambda a: pl.BlockSpec(a.shape, lambda i: (0, 0))
    return pl.pallas_call(
        body,
        grid=(t // tm,),
        in_specs=[row, full(wq), full(kb), full(vb)],
        out_specs=[row, row],
        out_shape=[jax.ShapeDtypeStruct((t, d), BF16), jax.ShapeDtypeStruct((t, d), BF16)],
        compiler_params=_params(1),
        name="attn_fwd",
    )(hb, wq, kb, vb)


def _ffn_bwd_act(dyb, wd, a, b, coef, name, deps=()):
    t, d = dyb.shape
    f = wd.shape[0]
    tm = _row_tile(t)
    tn = _col_tile(f)

    def body(dy_ref, wd_ref, a_ref, b_ref, da_ref, db_ref):
        dy = dy_ref[...]
        for c in range(f // tn):
            cols = slice(c * tn, (c + 1) * tn)
            ds = _dot_nt(dy, wd_ref[cols, :]) * coef
            silu, dsilu = _silu_and_grad(a_ref[:, cols].astype(F32))
            da_ref[:, cols] = (ds * b_ref[:, cols].astype(F32) * dsilu).astype(BF16)
            db_ref[:, cols] = (ds * silu).astype(BF16)

    act = pl.BlockSpec((tm, f), lambda i: (i, 0))
    return pl.pallas_call(
        _drop_deps(body, 4, len(deps)),
        grid=(t // tm,),
        in_specs=[pl.BlockSpec((tm, d), lambda i: (i, 0)), _resident(wd), act, act] + [ANY] * len(deps),
        out_specs=[act, act],
        out_shape=[jax.ShapeDtypeStruct((t, f), BF16), jax.ShapeDtypeStruct((t, f), BF16)],
        compiler_params=_params(1),
        name=name,
    )(dyb, wd, a, b, *deps)


def _ffn_bwd_fused(dy, dyb, wd, wg, wu, a, b, coef, ln, name):
    t, d = dy.shape
    f = wd.shape[0]
    tm = min(t, 256)
    tn = _col_tile(f)

    def body(dy_ref, dyb_ref, wd_ref, wg_ref, wu_ref, a_ref, b_ref, xh_ref, rs_ref, g_ref,
             da_ref, db_ref, dyo_ref, dyob_ref, dg_ref, dbl_ref):
        dyb_v = dyb_ref[...]
        dh = ALPHA * dy_ref[...]
        for c in range(f // tn):
            cols = slice(c * tn, (c + 1) * tn)
            ds = _dot_nt(dyb_v, wd_ref[cols, :]) * coef
            silu, dsilu = _silu_and_grad(a_ref[:, cols].astype(F32))
            da = (ds * b_ref[:, cols].astype(F32) * dsilu).astype(BF16)
            db = (ds * silu).astype(BF16)
            da_ref[:, cols] = da
            db_ref[:, cols] = db
            dh = dh + _dot_nt(da, wg_ref[:, cols]) + _dot_nt(db, wu_ref[:, cols])

        @pl.when(pl.program_id(0) == 0)
        def _():
            dg_ref[...] = jnp.zeros_like(dg_ref)
            dbl_ref[...] = jnp.zeros_like(dbl_ref)

        dyp, dg, dbl = _ln_bwd(dh, xh_ref[...], rs_ref[...], g_ref[...])
        dyo_ref[...] = dyp
        dyob_ref[...] = dyp.astype(BF16)
        dg_ref[...] += dg
        dbl_ref[...] += dbl

    row = pl.BlockSpec((tm, d), lambda i: (i, 0))
    act = pl.BlockSpec((tm, f), lambda i: (i, 0))
    vec = pl.BlockSpec((1, d), lambda i: (0, 0))
    return pl.pallas_call(
        body,
        grid=(t // tm,),
        in_specs=[row, row, _resident(wd), _resident(wg), _resident(wu), act, act, row,
                  pl.BlockSpec((tm, 1), lambda i: (i, 0)), vec],
        out_specs=[act, act, row, row, vec, vec],
        out_shape=[jax.ShapeDtypeStruct((t, f), BF16), jax.ShapeDtypeStruct((t, f), BF16),
                   jax.ShapeDtypeStruct((t, d), F32), jax.ShapeDtypeStruct((t, d), BF16),
                   jax.ShapeDtypeStruct((1, d), F32), jax.ShapeDtypeStruct((1, d), F32)],
        compiler_params=_params(1),
        name=name,
    )(dy, dyb, wd, wg, wu, a, b, *ln)


def _mm_tn(a, b, name, scale=1.0, deps=()):
    t, m = a.shape
    n = b.shape[1]
    tt = _row_tile(t)
    nt = t // tt
    tm_o = m // 2 if (m > n and m * n > 2 ** 21) else m
    tn_o = n // 2 if (n > m and m * n > 2 ** 21) else n

    def body(a_ref, b_ref, o_ref, acc):
        k = pl.program_id(2)

        @pl.when(k == 0)
        def _():
            acc[...] = jnp.zeros_like(acc)

        acc[...] += _dot_tn(a_ref[...], b_ref[...])

        @pl.when(k == nt - 1)
        def _():
            o_ref[...] = (acc[...] * scale).astype(BF16)

    return pl.pallas_call(
        _drop_deps(body, 2, len(deps)),
        grid=(m // tm_o, n // tn_o, nt),
        in_specs=[pl.BlockSpec((tt, tm_o), lambda i, j, k: (k, i)), pl.BlockSpec((tt, tn_o), lambda i, j, k: (k, j))]
        + [ANY] * len(deps),
        out_specs=pl.BlockSpec((tm_o, tn_o), lambda i, j, k: (i, j)),
        out_shape=jax.ShapeDtypeStruct((m, n), BF16),
        scratch_shapes=[pltpu.VMEM((tm_o, tn_o), F32)],
        compiler_params=_params(3),
        name=name,
    )(a, b, *deps)


def _mm_nt(lhs, w, name):
    t, d = lhs.shape
    kd = w.shape[0]
    tm = _row_tile(t)

    def body(l_ref, w_ref, o_ref):
        o_ref[...] = _dot_nt(l_ref[...], w_ref[...])

    return pl.pallas_call(
        body,
        grid=(t // tm,),
        in_specs=[pl.BlockSpec((tm, d), lambda i: (i, 0)), _resident(w)],
        out_specs=pl.BlockSpec((tm, kd), lambda i: (i, 0)),
        out_shape=jax.ShapeDtypeStruct((t, kd), F32),
        compiler_params=_params(1),
        name=name,
    )(lhs, w)


def _dx_ln(dy, pairs, ln, name, deps=()):
    t, d = dy.shape
    npair = len(pairs)
    tm = min(t, 512 // npair)
    nt = t // tm
    n_in = 1 + 2 * npair + (3 if ln is not None else 0)

    def body(*refs):
        dy_ref = refs[0]
        pr = refs[1:1 + 2 * npair]
        pos = 1 + 2 * npair
        dh = ALPHA * dy_ref[...]
        for p in range(npair):
            dh = dh + _dot_nt(pr[2 * p][...], pr[2 * p + 1][...])
        if ln is not None:
            xh_ref, rs_ref, g_ref = refs[pos:pos + 3]
            dyo_ref, dyb_ref, dg_ref, db_ref = refs[pos + 3:pos + 7]

            @pl.when(pl.program_id(0) == 0)
            def _():
                dg_ref[...] = jnp.zeros_like(dg_ref)
                db_ref[...] = jnp.zeros_like(db_ref)

            dyp, dg, db = _ln_bwd(dh, xh_ref[...], rs_ref[...], g_ref[...])
            dyo_ref[...] = dyp
            dyb_ref[...] = dyp.astype(BF16)
            dg_ref[...] += dg
            db_ref[...] += db
        else:
            refs[pos][...] = dh

    row = pl.BlockSpec((tm, d), lambda i: (i, 0))
    vec = pl.BlockSpec((1, d), lambda i: (0, 0))
    in_specs = [row]
    args = [dy]
    for lhs, w in pairs:
        in_specs += [pl.BlockSpec((tm, lhs.shape[1]), lambda i: (i, 0)), _resident(w)]
        args += [lhs, w]
    if ln is not None:
        in_specs += [row, pl.BlockSpec((tm, 1), lambda i: (i, 0)), vec]
        args += list(ln)
        out_specs = [row, row, vec, vec]
        out_shape = [jax.ShapeDtypeStruct((t, d), F32), jax.ShapeDtypeStruct((t, d), BF16),
                     jax.ShapeDtypeStruct((1, d), F32), jax.ShapeDtypeStruct((1, d), F32)]
    else:
        out_specs = row
        out_shape = jax.ShapeDtypeStruct((t, d), F32)
    return pl.pallas_call(
        _drop_deps(body, n_in, len(deps)),
        grid=(nt,),
        in_specs=in_specs + [ANY] * len(deps),
        out_specs=out_specs,
        out_shape=out_shape,
        compiler_params=_params(1),
        name=name,
    )(*args, *deps)


def _hgrn_bwd(proj, oraw, dmix, states, logits, gn):
    t = proj.shape[0]
    ct = _hg_tile(t)
    nct = t // ct
    nblk = ct // HG_BLOCK
    nh = HG_HEADS

    def body(q_ref, fz_ref, iv_ref, gg_ref, or_ref, do_ref, st_ref, lg_ref, gn_ref, mask_ref,
             dq_ref, dfz_ref, div_ref, dgg_ref, dlg_ref, dgn_ref,
             dstate, qt_s, kt_s, k_s, b_s, eb_s, ekb_s, dec_s, dor_s, dbl_s, gr_s, dk_s, dlb_acc):
        c = pl.program_id(1)

        @pl.when(c == 0)
        def _():
            dstate[...] = jnp.zeros_like(dstate)
            dlb_acc[...] = jnp.zeros_like(dlb_acc)
            dgn_ref[...] = jnp.zeros_like(dgn_ref)

        lb = _lower_bound(lg_ref[...])
        q = q_ref[...]
        sig, nsig, f, k = _forget_terms(fz_ref[...], lb)
        logf = jnp.log(f)
        b = _mask_dot(mask_ref[0], logf)
        bend = _mask_dot(mask_ref[2], logf)
        eb = jnp.exp(b)
        ekb = jnp.exp(bend - b)
        qt_s[...] = (q * eb).astype(BF16)
        kt_s[...] = (k * ekb).astype(BF16)
        k_s[...] = k
        b_s[...] = b
        eb_s[...] = eb
        ekb_s[...] = ekb
        dec_s[...] = jnp.exp(bend)
        oraw = or_ref[...]
        r = lax.rsqrt(jnp.mean(oraw * oraw, axis=-1, keepdims=True) + LN_EPS)
        on = oraw * r
        gg = gg_ref[...]
        silu, dsilu = _silu_and_grad(gg)
        doa = do_ref[...]
        gnv = gn_ref[...]
        dgg_ref[...] = (doa * on * gnv * dsilu).astype(BF16)
        dyn = doa * silu
        dgn_ref[...] += jnp.sum(dyn * on, axis=0, keepdims=True)
        don = dyn * gnv
        dor_s[...] = r * (don - on * jnp.mean(don * on, axis=-1, keepdims=True))
        tidx = lax.broadcasted_iota(jnp.int32, (HG_HALF, HG_DIM), 0)

        def blk(ii, carry):
            i = nblk - 1 - ii
            r0 = pl.multiple_of(i * HG_BLOCK, HG_BLOCK)
            rows = pl.ds(r0, HG_BLOCK)
            st = st_ref[i]
            dst = dstate[...]
            dstb = dst.astype(BF16)
            do = dor_s[rows, :]
            dob = do.astype(BF16)
            v = iv_ref[rows, :]
            vb = v.astype(BF16)
            qq = q_ref[rows, :]
            kk = k_s[rows, :]
            bb = b_s[rows, :]
            qt = qt_s[rows, :]
            kt = kt_s[rows, :]
            dec = dec_s[pl.ds(r0, 1), :]
            dkt = _dot(vb, dstb)
            dq = _dot(dob, st.astype(BF16)) * eb_s[rows, :]
            dk = dkt * ekb_s[rows, :]
            dv = _dot_nt(kt, dstb)
            gend = jnp.sum(kk * dk, axis=0, keepdims=True) + dec * jnp.sum(dst * st, axis=0, keepdims=True)
            qh, bh, doh = _halves(qq), _halves(bb), _halves(do)
            dqh, dkh, dvh = list(_halves(dq)), list(_halves(dk)), list(_halves(dv))
            for s in range(HG_BLOCK):
                ks, vs = kk[s:s + 1, :], v[s:s + 1, :]
                dk_part = dv_part = None
                for h in _causal_halves(s):
                    e = _decay_from(bh[h], bb[s:s + 1, :], s, h, tidx)
                    ke = ks * e
                    acol = jnp.sum(qh[h] * ke, axis=1, keepdims=True)
                    dacol = jnp.sum(doh[h] * vs, axis=1, keepdims=True)
                    dqh[h] = dqh[h] + dacol * ke
                    pk = dacol * (qh[h] * e)
                    pv = acol * doh[h]
                    dk_part = pk if dk_part is None else dk_part + pk
                    dv_part = pv if dv_part is None else dv_part + pv
                hs, row = divmod(s, HG_HALF)
                dkh[hs] = dkh[hs] + jnp.where(tidx == row, jnp.sum(dk_part, axis=0, keepdims=True), 0.0)
                dvh[hs] = dvh[hs] + jnp.where(tidx == row, jnp.sum(dv_part, axis=0, keepdims=True), 0.0)
            dq = jnp.concatenate(dqh, axis=0)
            dk = jnp.concatenate(dkh, axis=0)
            dv = jnp.concatenate(dvh, axis=0)
            dq_ref[rows, :] = dq.astype(BF16)
            div_ref[rows, :] = dv.astype(BF16)
            dk_s[rows, :] = dk
            dbl_s[rows, :] = qq * dq - kk * dk
            gr_s[rows, :] = jnp.zeros((HG_BLOCK, HG_DIM), F32) + gend
            dstate[...] = dst * dec + _dot_tn(dob, qt)
            return carry

        lax.fori_loop(0, nblk, blk, 0, unroll=2 * HG_UNROLL)
        dlogf = _mask_dot(mask_ref[1], dbl_s[...]) + gr_s[...]
        dk = dk_s[...]
        dfz_ref[...] = ((dlogf / f - dk) * ((1.0 - lb) * sig * nsig)).astype(BF16)
        dlb_acc[...] += jnp.sum((dlogf / f - dk) * nsig, axis=0, keepdims=True)

        @pl.when(c == nct - 1)
        def _():
            dl0 = dlb_acc[...] * lb * (1.0 - lb)
            layer = lax.broadcasted_iota(jnp.int32, (2, HG_DIM), 0)
            dlg_ref[...] = jnp.where(layer == 0, dl0, -dl0)

    def slab(off):
        return pl.BlockSpec((ct, HG_DIM), lambda h, c: (nct - 1 - c, off + h))

    out_slab = pl.BlockSpec((ct, HG_DIM), lambda h, c: (nct - 1 - c, h))
    tile_f32 = pltpu.VMEM((ct, HG_DIM), F32)
    tile_b16 = pltpu.VMEM((ct, HG_DIM), BF16)
    slab_shape = jax.ShapeDtypeStruct((t, nh * HG_DIM), BF16)
    return pl.pallas_call(
        body,
        grid=(nh, nct),
        in_specs=[slab(0), slab(nh), slab(2 * nh), slab(3 * nh), slab(0), slab(0),
                  pl.BlockSpec((None, nblk, HG_DIM, HG_DIM), lambda h, c: (h, nct - 1 - c, 0, 0)),
                  pl.BlockSpec((None, 2, HG_DIM), lambda h, c: (h, 0, 0)),
                  pl.BlockSpec((1, HG_DIM), lambda h, c: (0, 0)),
                  pl.BlockSpec((3, ct, ct), lambda h, c: (0, 0, 0))],
        out_specs=[out_slab, out_slab, out_slab, out_slab,
                   pl.BlockSpec((None, 2, HG_DIM), lambda h, c: (h, 0, 0)),
                   pl.BlockSpec((None, 1, HG_DIM), lambda h, c: (h, 0, 0))],
        out_shape=[slab_shape, slab_shape, slab_shape, slab_shape,
                   jax.ShapeDtypeStruct((nh, 2, HG_DIM), F32), jax.ShapeDtypeStruct((nh, 1, HG_DIM), F32)],
        scratch_shapes=[pltpu.VMEM((HG_DIM, HG_DIM), F32), tile_b16, tile_b16, tile_f32, tile_f32, tile_f32, tile_f32,
                        tile_f32, tile_f32, tile_f32, tile_f32, tile_f32, pltpu.VMEM((1, HG_DIM), F32)],
        compiler_params=_params(2),
        name="hgrn_bwd",
    )(proj, proj, proj, proj, oraw, dmix, states, logits, gn, _block_masks(ct))


def _sgu_bwd(proj, dmix, ln_g, ln_b, w_s, w_t, b_col):
    t = proj.shape[0]
    ct = _sg_tile(t)
    nct = t // ct
    ng = SG_GROUPS
    off_u = 4 * HG_HEADS
    off_v = off_u + ng
    n = SG_CHUNK

    def body(u_ref, v_ref, do_ref, g_ref, b_ref, w_ref, wt_ref, bs_ref, du_ref, dv_ref, dg_ref, db_ref, dw_ref, dbs_ref):
        c = pl.program_id(1)

        @pl.when(c == 0)
        def _():
            dg_ref[...] = jnp.zeros_like(dg_ref)
            db_ref[...] = jnp.zeros_like(db_ref)
            dw_ref[...] = jnp.zeros_like(dw_ref)
            dbs_ref[...] = jnp.zeros_like(dbs_ref)

        r = lax.broadcasted_iota(jnp.int32, (n, n), 0)
        cc = lax.broadcasted_iota(jnp.int32, (n, n), 1)
        wm = jnp.where(cc <= r, w_ref[...], 0.0).astype(BF16)
        wmt = jnp.where(r <= cc, wt_ref[...], 0.0).astype(BF16)
        for ci in range(ct // n):
            rows = slice(ci * n, (ci + 1) * n)
            ua, dua, dva, vn, xhat, rstd, s = _sgu_chunk_fwd(u_ref[rows, :], v_ref[rows, :], g_ref[...], b_ref[...],
                                                             wm, bs_ref[...])
            do = do_ref[rows, :]
            du_ref[rows, :] = (do * s * dua).astype(BF16)
            ds = do * ua
            dsb = ds.astype(BF16)
            dbs_ref[...] += jnp.sum(ds, axis=1, keepdims=True)
            dw_ref[...] += _dot_nt(dsb, vn.astype(BF16))
            dvn = _dot(wmt, dsb)
            dva_in, dg, db = _ln_bwd(dvn, xhat, rstd, g_ref[...])
            dg_ref[...] += dg
            db_ref[...] += db
            dv_ref[rows, :] = (dva_in * dva).astype(BF16)

        @pl.when(c == nct - 1)
        def _():
            dw_ref[...] = jnp.where(cc <= r, dw_ref[...], 0.0)

    vec = pl.BlockSpec((None, 1, SG_DIM), lambda g, c: (g, 0, 0))
    mat = pl.BlockSpec((None, n, n), lambda g, c: (g, 0, 0))
    col = pl.BlockSpec((None, n, 1), lambda g, c: (g, 0, 0))
    out_slab = pl.BlockSpec((ct, SG_DIM), lambda g, c: (c, g))
    return pl.pallas_call(
        body,
        grid=(ng, nct),
        in_specs=[pl.BlockSpec((ct, SG_DIM), lambda g, c: (c, off_u + g)),
                  pl.BlockSpec((ct, SG_DIM), lambda g, c: (c, off_v + g)),
                  pl.BlockSpec((ct, SG_DIM), lambda g, c: (c, ng + g)), vec, vec, mat, mat, col],
        out_specs=[out_slab, out_slab, vec, vec, mat, col],
        out_shape=[jax.ShapeDtypeStruct((t, ng * SG_DIM), BF16), jax.ShapeDtypeStruct((t, ng * SG_DIM), BF16),
                   jax.ShapeDtypeStruct((ng, 1, SG_DIM), F32), jax.ShapeDtypeStruct((ng, 1, SG_DIM), F32),
                   jax.ShapeDtypeStruct((ng, n, n), F32), jax.ShapeDtypeStruct((ng, n, 1), F32)],
        compiler_params=_params(2),
        name="sgu_bwd",
    )(proj, proj, dmix, ln_g, ln_b, w_s, w_t, b_col)


def _attn_bwd(dyb, wo, qb, kb, vb):
    t, d = dyb.shape
    m_len = kb.shape[0]
    tm = _row_tile(t)
    dh = d // X_HEADS
    scale = dh ** -0.5

    def body(dy_ref, wo_ref, q_ref, k_ref, v_ref, dq_ref, dk_ref, dv_ref):
        i = pl.program_id(0)

        @pl.when(i == 0)
        def _():
            dk_ref[...] = jnp.zeros_like(dk_ref)
            dv_ref[...] = jnp.zeros_like(dv_ref)

        do = _dot_nt(dy_ref[...], wo_ref[...]).astype(BF16)
        for hd in range(X_HEADS):
            sl = slice(hd * dh, (hd + 1) * dh)
            qh = q_ref[:, sl]
            p = _softmax_rows(_dot_nt(qh, k_ref[:, sl]) * scale)
            doh = do[:, sl]
            dp = _dot_nt(doh, v_ref[:, sl])
            ds = (p * (dp - jnp.sum(dp * p, axis=-1, keepdims=True)) * scale).astype(BF16)
            dq_ref[:, sl] = _dot(ds, k_ref[:, sl]).astype(BF16)
            dk_ref[:, sl] += _dot_tn(ds, qh)
            dv_ref[:, sl] += _dot_tn(p.astype(BF16), doh)

    row = pl.BlockSpec((tm, d), lambda i: (i, 0))
    full = lambda a: pl.BlockSpec(a.shape, lambda i: (0, 0))
    kv = pl.BlockSpec((m_len, d), lambda i: (0, 0))
    return pl.pallas_call(
        body,
        grid=(t // tm,),
        in_specs=[row, full(wo), row, full(kb), full(vb)],
        out_specs=[row, kv, kv],
        out_shape=[jax.ShapeDtypeStruct((t, d), BF16), jax.ShapeDtypeStruct((m_len, d), F32),
                   jax.ShapeDtypeStruct((m_len, d), F32)],
        compiler_params=_params(1),
        name="attn_bwd",
    )(dyb, wo, qb, kb, vb)


def _mem_bwd(dk, dv, mb, xhat, rstd, g, wk, wv):
    m_len, d = dk.shape

    def body(dk_ref, dv_ref, mb_ref, xh_ref, rs_ref, g_ref, wk_ref, wv_ref, gwk_ref, gwv_ref, dg_ref, db_ref):
        dkb = dk_ref[...].astype(BF16)
        dvb = dv_ref[...].astype(BF16)
        mb_v = mb_ref[...]
        gwk_ref[...] = _dot_tn(mb_v, dkb).astype(BF16)
        gwv_ref[...] = _dot_tn(mb_v, dvb).astype(BF16)
        dm = _dot_nt(dkb, wk_ref[...]) + _dot_nt(dvb, wv_ref[...])
        _, dg, db = _ln_bwd(dm, xh_ref[...], rs_ref[...], g_ref[...])
        dg_ref[...] = dg
        db_ref[...] = db

    return pl.pallas_call(
        body,
        out_shape=[jax.ShapeDtypeStruct((d, d), BF16), jax.ShapeDtypeStruct((d, d), BF16),
                   jax.ShapeDtypeStruct((1, d), F32), jax.ShapeDtypeStruct((1, d), F32)],
        compiler_params=pltpu.CompilerParams(vmem_limit_bytes=VMEM_LIMIT_V7X),
        name="mem_bwd",
    )(dk, dv, mb, xhat, rstd, g, wk, wv)


def _adamw(w, g, m, v):
    m = ADAM_B1 * m + (1.0 - ADAM_B1) * g
    v = ADAM_B2 * v + (1.0 - ADAM_B2) * (g * g)
    m_hat = m / (1.0 - ADAM_B1 ** ADAM_STEP)
    v_hat = v / (1.0 - ADAM_B2 ** ADAM_STEP)
    delta = -ADAM_LR * (m_hat / (jnp.sqrt(v_hat) + ADAM_EPS) + ADAM_WD * w)
    return delta, m, v


def _slot_sum(ref):
    g = ref[0].astype(F32)
    for s in range(1, N_DEV):
        g = g + ref[s].astype(F32)
    return g


def _adam_sharded(lands, w, m, v, axis, name):
    rows, cols = w.shape
    nl = len(lands)
    transposed = axis == 1 and nl == 2
    if transposed:
        rows, cols = cols, rows
        tr = 256
        grid = (rows // tr,)
        wblk = pl.BlockSpec((cols, tr), lambda i: (0, i))
        lblk = [pl.BlockSpec((N_DEV, tr, a.shape[2]), lambda i: (0, i, 0)) for a in lands]
    elif axis == 1:
        tr = 256 if rows % 256 == 0 else rows
        grid = (rows // tr,)
        wblk = pl.BlockSpec((tr, cols), lambda i: (i, 0))
        lblk = [pl.BlockSpec((N_DEV, tr, a.shape[2]), lambda i: (0, i, 0)) for a in lands]
    else:
        tc = _col_tile(cols)
        grid = (cols // tc,)
        wblk = pl.BlockSpec((rows, tc), lambda i: (0, i))
        lblk = [pl.BlockSpec((N_DEV, a.shape[1], tc), lambda i: (0, 0, i)) for a in lands]

    def body(*refs):
        w_ref, m_ref, v_ref = refs[nl:nl + 3]
        g_ref, d_ref, nm_ref, nv_ref = refs[nl + 3:]
        g = _slot_sum(refs[0])
        if nl == 2:
            tail = _slot_sum(refs[1])
            if transposed:
                g = jnp.concatenate([g.T, tail.T[:cols - g.shape[1], :]], axis=0)
            elif axis == 1:
                g = jnp.concatenate([g, tail[:, :cols - g.shape[1]]], axis=1)
            else:
                g = jnp.concatenate([g, tail[:rows - g.shape[0], :]], axis=0)
        delta, nm, nv = _adamw(w_ref[...], g, m_ref[...], v_ref[...])
        g_ref[...] = g
        d_ref[...] = delta
        nm_ref[...] = nm
        nv_ref[...] = nv

    shp = jax.ShapeDtypeStruct(w.shape, F32)
    return pl.pallas_call(
        body,
        grid=grid,
        in_specs=lblk + [wblk, wblk, wblk],
        out_specs=[wblk, wblk, wblk, wblk],
        out_shape=[shp, shp, shp, shp],
        compiler_params=_params(1),
        name=name,
    )(*lands, w, m, v)


def _mesh_pos():
    return lax.axis_index("x"), lax.axis_index("y"), lax.axis_index("c")


def _peer(k):
    x, y, c = _mesh_pos()
    pos = (x ^ (k >> 2), y ^ ((k >> 1) & 1), c ^ (k & 1))
    return pos, 4 * pos[0] + 2 * pos[1] + pos[2]


def _sem_index(row, k):
    return row * (N_DEV - 1) + k - 1


def _window(ref, axis, start, size):
    align = 16 if axis == 0 else LANES
    start = pl.multiple_of(start, align)
    return ref.at[pl.ds(start, size), :] if axis == 0 else ref.at[:, pl.ds(start, size)]


def _piece_refs(piece, srcs, lands, me, peer):
    kind, si, li, axis, base, stride, shape = piece
    if kind == "gather":
        return srcs[si], _window(lands[li], axis, base + stride * me, shape[axis])
    return _window(srcs[si], axis, base + stride * peer, shape[axis]), lands[li].at[me]


def _place_own(srcs, land_shapes, pieces, name):
    ns, nl, npc = len(srcs), len(land_shapes), len(pieces)

    def body(*refs):
        s_refs = refs[:ns]
        l_refs = refs[ns:ns + nl]
        bufs = refs[ns + nl:ns + nl + npc]
        sems = refs[ns + nl + npc]
        x, y, c = _mesh_pos()
        me = 4 * x + 2 * y + c
        loads = []
        for p, piece in enumerate(pieces):
            src, dst = _piece_refs(piece, s_refs, l_refs, me, me)
            cp = pltpu.make_async_copy(src, bufs[p], sems.at[0, p])
            cp.start()
            loads.append((cp, dst))
        stores = []
        for p, (cp, dst) in enumerate(loads):
            cp.wait()
            out = pltpu.make_async_copy(bufs[p], dst, sems.at[1, p])
            out.start()
            stores.append(out)
        for out in stores:
            out.wait()

    out = pl.pallas_call(
        body,
        in_specs=[ANY] * ns,
        out_specs=[ANY] * nl,
        out_shape=list(land_shapes),
        scratch_shapes=[pltpu.VMEM(pc[6], srcs[pc[1]].dtype) for pc in pieces] + [pltpu.SemaphoreType.DMA((2, npc))],
        compiler_params=pltpu.CompilerParams(vmem_limit_bytes=VMEM_LIMIT_V7X),
        name=name,
    )(*srcs)
    return list(out)


def _comm_start(srcs, lands, pieces, groups, name, after=()):
    ns, nl, na, ng = len(srcs), len(lands), len(after), len(groups)

    def body(*refs):
        s_refs = refs[:ns]
        l_refs = refs[ns:ns + nl]
        outs = refs[ns + nl + na:]
        sems = outs[:2 * ng]
        token = outs[-1]
        x, y, c = _mesh_pos()
        me = 4 * x + 2 * y + c
        for g, members in enumerate(groups):
            for row, p in enumerate(members):
                for k in range(1, N_DEV):
                    pos, peer = _peer(k)
                    src, dst = _piece_refs(pieces[p], s_refs, l_refs, me, peer)
                    pltpu.make_async_remote_copy(src_ref=src, dst_ref=dst, send_sem=sems[2 * g].at[_sem_index(row, k)],
                                                 recv_sem=sems[2 * g + 1].at[_sem_index(row, k)], device_id=pos,
                                                 device_id_type=MESH_ID).start()
        token[...] = jnp.zeros_like(token)

    sem_shapes = []
    for members in groups:
        sem_shapes += [pltpu.SemaphoreType.DMA((len(members) * (N_DEV - 1),))] * 2
    hbm_of = lambda a: pltpu.HBM(a.shape, a.dtype)
    out = pl.pallas_call(
        body,
        in_specs=[HBM] * (ns + nl) + [ANY] * na,
        out_specs=[SEM] * (2 * ng) + [HBM] * (ns + nl) + [pl.BlockSpec(memory_space=pltpu.VMEM)],
        out_shape=sem_shapes + [hbm_of(a) for a in srcs] + [hbm_of(a) for a in lands]
        + [jax.ShapeDtypeStruct((8, LANES), F32)],
        input_output_aliases={i: 2 * ng + i for i in range(ns + nl)},
        compiler_params=pltpu.CompilerParams(has_side_effects=DATAFLOW),
        name=name,
    )(*[pltpu.with_memory_space_constraint(a, pltpu.HBM) for a in list(srcs) + list(lands)], *after)
    sems = [(out[2 * g], out[2 * g + 1]) for g in range(ng)]
    return sems, list(out[2 * ng:2 * ng + ns]), list(out[2 * ng + ns:2 * ng + ns + nl]), out[-1]


def _comm_wait(srcs, lands, pieces, members, sems, after, name):
    ns, nl, na = len(srcs), len(lands), len(after)

    def body(*refs):
        s_refs = refs[:ns]
        l_refs = refs[ns:ns + nl]
        send_sems, recv_sems = refs[ns + nl:ns + nl + 2]
        x, y, c = _mesh_pos()
        me = 4 * x + 2 * y + c
        for row, p in enumerate(members):
            for k in range(1, N_DEV):
                pos, peer = _peer(k)
                src, dst = _piece_refs(pieces[p], s_refs, l_refs, me, peer)
                cp = pltpu.make_async_remote_copy(src_ref=src, dst_ref=dst, send_sem=send_sems.at[_sem_index(row, k)],
                                                  recv_sem=recv_sems.at[_sem_index(row, k)], device_id=pos,
                                                  device_id_type=MESH_ID)
                cp.wait_send()
                cp.wait_recv()

    hbm_of = lambda a: pltpu.HBM(a.shape, a.dtype)
    out = pl.pallas_call(
        body,
        in_specs=[HBM] * (ns + nl) + [SEM, SEM] + [ANY] * na,
        out_specs=[HBM] * (ns + nl),
        out_shape=[hbm_of(a) for a in srcs] + [hbm_of(a) for a in lands],
        input_output_aliases={i: i for i in range(ns + nl)},
        compiler_params=pltpu.CompilerParams(has_side_effects=DATAFLOW),
        name=name,
    )(*srcs, *lands, sems[0], sems[1], *after)
    return list(out[ns:])


_CHIP_FLIPS = (2, 4, 6)
_SIBLING = 1


def _gather2_first(srcs, lands, pieces, name):
    ns, nl, npc = len(srcs), len(lands), len(pieces)

    def body(*refs):
        s_refs, l_refs = refs[:ns], refs[ns:ns + nl]
        send_sems, recv_sib, recv_ici = refs[ns + nl:ns + nl + 3]
        token = refs[-1]
        x, y, c = _mesh_pos()
        me = 4 * x + 2 * y + c
        for p, piece in enumerate(pieces):
            for j, k in enumerate((_SIBLING,) + _CHIP_FLIPS):
                pos, peer = _peer(k)
                src, dst = _piece_refs(piece, s_refs, l_refs, me, peer)
                recv = recv_sib.at[p] if j == 0 else recv_ici.at[3 * p + j - 1]
                pltpu.make_async_remote_copy(src_ref=src, dst_ref=dst, send_sem=send_sems.at[4 * p + j], recv_sem=recv,
                                             device_id=pos, device_id_type=MESH_ID).start()
        token[...] = jnp.zeros_like(token)

    hbm_of = lambda a: pltpu.HBM(a.shape, a.dtype)
    dma = lambda n: pltpu.SemaphoreType.DMA((n,))
    out = pl.pallas_call(
        body,
        in_specs=[HBM] * (ns + nl),
        out_specs=[SEM] * 3 + [HBM] * (ns + nl) + [pl.BlockSpec(memory_space=pltpu.VMEM)],
        out_shape=[dma(4 * npc), dma(npc), dma(3 * npc)] + [hbm_of(a) for a in srcs] + [hbm_of(a) for a in lands]
        + [jax.ShapeDtypeStruct((8, LANES), F32)],
        input_output_aliases={i: 3 + i for i in range(ns + nl)},
        compiler_params=pltpu.CompilerParams(has_side_effects=DATAFLOW),
        name=name,
    )(*[pltpu.with_memory_space_constraint(a, pltpu.HBM) for a in list(srcs) + list(lands)])
    return out[0], out[1], out[2], list(out[3:3 + ns]), list(out[3 + ns:3 + ns + nl]), out[-1]


def _landed_block(piece, lands, owner):
    _, _, li, axis, base, stride, shape = piece
    return _window(lands[li], axis, base + stride * owner, shape[axis])


def _gather2_pass(lands, pieces, recv_ici, after, name):
    nl, npc, na = len(lands), len(pieces), len(after)

    def body(*refs):
        l_refs = refs[:nl]
        recv_sems = refs[nl]
        send_fwd, recv_fwd = refs[nl + 1 + na:nl + 3 + na]
        token = refs[-1]
        token[...] = jnp.zeros_like(token)
        sib, _ = _peer(_SIBLING)
        for p, piece in enumerate(pieces):
            for j, k in enumerate(_CHIP_FLIPS):
                pos, owner = _peer(k)
                block = _landed_block(piece, l_refs, owner)
                pltpu.make_async_remote_copy(src_ref=block, dst_ref=block, send_sem=send_fwd.at[3 * p + j],
                                             recv_sem=recv_sems.at[3 * p + j], device_id=pos,
                                             device_id_type=MESH_ID).wait_recv()
                pltpu.make_async_remote_copy(src_ref=block, dst_ref=block, send_sem=send_fwd.at[3 * p + j],
                                             recv_sem=recv_fwd.at[3 * p + j], device_id=sib,
                                             device_id_type=MESH_ID).start()

    hbm_of = lambda a: pltpu.HBM(a.shape, a.dtype)
    dma = lambda n: pltpu.SemaphoreType.DMA((n,))
    out = pl.pallas_call(
        body,
        in_specs=[HBM] * nl + [SEM] + [ANY] * na,
        out_specs=[SEM, SEM] + [HBM] * nl + [pl.BlockSpec(memory_space=pltpu.VMEM)],
        out_shape=[dma(3 * npc), dma(3 * npc)] + [hbm_of(a) for a in lands] + [jax.ShapeDtypeStruct((8, LANES), F32)],
        input_output_aliases={i: 2 + i for i in range(nl)},
        compiler_params=pltpu.CompilerParams(has_side_effects=DATAFLOW),
        name=name,
    )(*lands, recv_ici, *after)
    return out[0], out[1], list(out[2:2 + nl]), out[-1]


def _gather2_last(srcs, lands, pieces, send_sems, recv_sib, send_fwd, recv_fwd, name):
    ns, nl = len(srcs), len(lands)

    def body(*refs):
        s_refs, l_refs = refs[:ns], refs[ns:ns + nl]
        send_a, recv_s, send_f, recv_f = refs[ns + nl:ns + nl + 4]
        x, y, c = _mesh_pos()
        me = 4 * x + 2 * y + c
        sib, sib_index = _peer(_SIBLING)
        for p, piece in enumerate(pieces):
            src, dst = _piece_refs(piece, s_refs, l_refs, me, sib_index)
            own = lambda s_sem, r_sem: pltpu.make_async_remote_copy(
                src_ref=src, dst_ref=dst, send_sem=s_sem, recv_sem=r_sem, device_id=sib, device_id_type=MESH_ID)
            own(send_a.at[4 * p], recv_s.at[p]).wait_recv()
            for j in range(4):
                own(send_a.at[4 * p + j], recv_s.at[p]).wait_send()
            for j in range(3):
                fwd = own(send_f.at[3 * p + j], recv_f.at[3 * p + j])
                fwd.wait_recv()
                fwd.wait_send()

    hbm_of = lambda a: pltpu.HBM(a.shape, a.dtype)
    out = pl.pallas_call(
        body,
        in_specs=[HBM] * (ns + nl) + [SEM] * 4,
        out_specs=[HBM] * (ns + nl),
        out_shape=[hbm_of(a) for a in srcs] + [hbm_of(a) for a in lands],
        input_output_aliases={i: i for i in range(ns + nl)},
        compiler_params=pltpu.CompilerParams(has_side_effects=DATAFLOW),
        name=name,
    )(*srcs, *lands, send_sems, recv_sib, send_fwd, recv_fwd)
    return list(out[ns:])


def _copy_stage(name, bufs, in_sems, out_sem_sizes, emit, after=()):
    nb, ni, no, na = len(bufs), len(in_sems), len(out_sem_sizes), len(after)

    def body(*refs):
        b_refs = refs[:nb]
        i_refs = refs[nb:nb + ni]
        o_refs = refs[nb + ni + na:nb + ni + na + no]
        emit(b_refs, i_refs, o_refs)
        refs[-1][...] = jnp.zeros_like(refs[-1])

    hbm_of = lambda a: pltpu.HBM(a.shape, a.dtype)
    out = pl.pallas_call(
        body,
        in_specs=[HBM] * nb + [SEM] * ni + [ANY] * na,
        out_specs=[SEM] * no + [HBM] * nb + [pl.BlockSpec(memory_space=pltpu.VMEM)],
        out_shape=[pltpu.SemaphoreType.DMA((n,)) for n in out_sem_sizes] + [hbm_of(a) for a in bufs]
        + [jax.ShapeDtypeStruct((8, LANES), F32)],
        input_output_aliases={i: no + i for i in range(nb)},
        compiler_params=pltpu.CompilerParams(has_side_effects=DATAFLOW),
        name=name,
    )(*[pltpu.with_memory_space_constraint(a, pltpu.HBM) for a in bufs], *in_sems, *after)
    return list(out[:no]), list(out[no:no + nb]), out[-1]


def _remote(src, dst, send, recv, to):
    return pltpu.make_async_remote_copy(src_ref=src, dst_ref=dst, send_sem=send, recv_sem=recv, device_id=to,
                                        device_id_type=MESH_ID)


def _routed_gather(srcs, lands, pieces, after, name):
    ns, npc = len(srcs), len(pieces)

    def places():
        x, y, c = _mesh_pos()
        index = lambda p: 4 * p[0] + 2 * p[1] + p[2]
        me, sib = (x, y, c), (x, y, 1 - c)
        xnb, ynb = (1 - x, y, c), (x, 1 - y, c)
        got_first = (x ^ (1 - c), y ^ c, c)
        pass_to = (x ^ c, y ^ (1 - c), c)
        diag = (1 - x, 1 - y, c)
        return index, me, sib, xnb, ynb, got_first, pass_to, diag

    def start(b, _, o):
        index, me, sib, xnb, ynb, *_rest = places()
        send_a, recv_sib, recv_nb = o
        for p, piece in enumerate(pieces):
            src, dst = _piece_refs(piece, b[:ns], b[ns:], index(me), 0)
            _remote(src, dst, send_a.at[3 * p], recv_sib.at[p], sib).start()
            _remote(src, dst, send_a.at[3 * p + 1], recv_nb.at[2 * p], xnb).start()
            _remote(src, dst, send_a.at[3 * p + 2], recv_nb.at[2 * p + 1], ynb).start()

    def pass_a(b, i, o):
        index, me, sib, xnb, ynb, got_first, pass_to, _diag = places()
        (recv_nb,) = i
        send_f, recv_f, send_d, recv_d = o
        for p, piece in enumerate(pieces):
            for j, nb in enumerate((xnb, ynb)):
                blk = _landed_block(piece, b, index(nb))
                _remote(blk, blk, send_f.at[2 * p + j], recv_nb.at[2 * p + j], sib).wait_recv()
                _remote(blk, blk, send_f.at[2 * p + j], recv_f.at[2 * p + j], sib).start()
            blk = _landed_block(piece, b, index(got_first))
            _remote(blk, blk, send_d.at[p], recv_d.at[p], pass_to).start()

    def pass_b(b, i, o):
        index, me, sib, *_mid, diag = places()
        (recv_d,) = i
        send_g, recv_g = o
        for p, piece in enumerate(pieces):
            blk = _landed_block(piece, b, index(diag))
            _remote(blk, blk, send_g.at[p], recv_d.at[p], sib).wait_recv()
            _remote(blk, blk, send_g.at[p], recv_g.at[p], sib).start()

    def last(b, i, _):
        index, me, sib, *_others = places()
        send_a, recv_sib, send_f, recv_f, send_d, send_g, recv_g = i
        for p, piece in enumerate(pieces):
            src, dst = _piece_refs(piece, b[:ns], b[ns:], index(me), 0)
            cp = lambda s_sem, r_sem: _remote(src, dst, s_sem, r_sem, sib)
            cp(send_a.at[3 * p], recv_sib.at[p]).wait_recv()
            cp(send_a.at[3 * p], recv_g.at[p]).wait_recv()
            for j in range(3):
                cp(send_a.at[3 * p + j], recv_sib.at[p]).wait_send()
            for j in range(2):
                cp(send_f.at[2 * p + j], recv_f.at[2 * p + j]).wait_recv()
                cp(send_f.at[2 * p + j], recv_f.at[2 * p + j]).wait_send()
            cp(send_d.at[p], recv_sib.at[p]).wait_send()
            cp(send_g.at[p], recv_sib.at[p]).wait_send()

    (send_a, recv_sib, recv_nb), bufs, _ = _copy_stage(name + "_start", list(srcs) + list(lands), [],
                                                       [3 * npc, npc, 2 * npc], start)
    srcs, lands = bufs[:ns], bufs[ns:]
    (send_f, recv_f, send_d, recv_d), lands, _ = _copy_stage(name + "_pass_a", lands, [recv_nb],
                                                             [2 * npc, 2 * npc, npc, npc],
                                                             lambda b, i, o: pass_a(b, i, o), after=after)
    (send_g, recv_g), lands, tok = _copy_stage(name + "_pass_b", lands, [recv_d], [npc, npc], pass_b)
    _, bufs, _ = _copy_stage(name + "_last", list(srcs) + list(lands),
                             [send_a, recv_sib, send_f, recv_f, send_d, send_g, recv_g], [], last)
    return bufs[ns:], tok


_SMALL_NAMES = ("ln1_g", "ln1_b", "hg_lb_logits", "hg_norm_g", "sg_ln_g", "sg_ln_b", "sg_w_s", "sg_b_s",
                "ln2_g", "ln2_b", "mem_ln_g", "mem_ln_b", "ln3_g", "ln3_b", "ln4_g", "ln4_b")


_VEC_NAMES = ("ln1_g", "ln1_b", "ln2_g", "ln2_b", "mem_ln_g", "mem_ln_b", "ln3_g", "ln3_b", "ln4_g", "ln4_b")
_ROW_NAMES = ("hg_lb_logits", "hg_norm_g", "sg_ln_g", "sg_ln_b", "sg_b_s", "sg_w_s")
VEC_ROWS = 16


def _row_plan(shapes):
    plan, pos = {}, 0
    for k in _ROW_NAMES:
        shp = shapes[k]
        slabs, off = [], pos
        for idx in itertools.product(*[range(dim) for dim in shp[:-2]]):
            slabs.append((idx, off, shp[-2]))
            off += shp[-2]
        plan[k] = (pos, slabs)
        pos = -(-off // 8) * 8
    return plan, -(-pos // 16) * 16


def _pack_small_grads(gs, shapes, loss):
    d = gs[_VEC_NAMES[0]].size
    vec = jnp.concatenate([gs[k].reshape(1, -1) for k in _VEC_NAMES] + [jnp.tile(loss, (1, d // LANES))], axis=0)
    vec = jnp.pad(vec, ((0, VEC_ROWS - vec.shape[0]), (0, 0)))
    plan, total = _row_plan(shapes)
    parts, pos = [], 0
    for k in _ROW_NAMES:
        first, slabs = plan[k]
        rows = gs[k].reshape(-1, LANES)
        end = slabs[-1][1] + slabs[-1][2]
        nxt = -(-end // 8) * 8
        parts.append(jnp.pad(rows, ((0, nxt - first - rows.shape[0]), (0, 0))))
        pos = nxt
    parts.append(jnp.zeros((total - pos, LANES), F32))
    return vec, jnp.concatenate(parts, axis=0)


def _adam_small(land_vec, land_rows, w, m, v):
    names = _VEC_NAMES + _ROW_NAMES
    n = len(names)
    shapes = {k: w[k].shape for k in names}
    plan, _ = _row_plan(shapes)

    def body(*refs):
        lv_ref, lr_ref = refs[:2]
        w_refs, m_refs, v_refs = refs[2:2 + n], refs[2 + n:2 + 2 * n], refs[2 + 2 * n:2 + 3 * n]
        outs = refs[2 + 3 * n:2 + 7 * n]
        loss_ref = refs[2 + 7 * n]
        gv_s, gr_s = refs[3 + 7 * n:]
        gv_s[...] = _slot_sum(lv_ref)
        gr_s[...] = _slot_sum(lr_ref)
        loss_ref[...] = gv_s[len(_VEC_NAMES):len(_VEC_NAMES) + 1, :LANES]
        for p, k in enumerate(names):
            if k in _VEC_NAMES:
                row = _VEC_NAMES.index(k)
                slabs = [((), None, None)]
            else:
                slabs = plan[k][1]
            for idx, off, rows in slabs:
                g = gv_s[row:row + 1, :] if off is None else gr_s[off:off + rows, :]
                sel = idx + (slice(None), slice(None))
                delta, nm, nv = _adamw(w_refs[p][sel], g, m_refs[p][sel], v_refs[p][sel])
                for o, val in zip(range(4), (g, delta, nm, nv)):
                    outs[o * n + p][sel] = val

    flat = lambda tree: [tree[k] for k in names]
    shp = [jax.ShapeDtypeStruct(shapes[k], F32) for k in names]
    out = pl.pallas_call(
        body,
        out_shape=shp * 4 + [jax.ShapeDtypeStruct((1, LANES), F32)],
        scratch_shapes=[pltpu.VMEM(land_vec.shape[1:], F32), pltpu.VMEM(land_rows.shape[1:], F32)],
        name="adam_small",
    )(land_vec, land_rows, *flat(w), *flat(m), *flat(v))
    return [dict(zip(names, out[o * n:(o + 1) * n])) for o in range(4)], out[4 * n]


_COL_FFN = ("ffn1_w_gate", "ffn1_w_up", "ffn2_w_gate", "ffn2_w_up")
_ROW_FFN = ("ffn1_w_down", "ffn2_w_down")
_ROW_SQ = ("w_out", "xa_w_q", "xa_w_k", "xa_w_v", "xa_w_o")
_BIG_NAMES = ("ffn1_w_gate", "ffn1_w_up", "ffn1_w_down", "w_in", "w_out", "xa_w_q", "xa_w_k", "xa_w_v", "xa_w_o",
              "ffn2_w_gate", "ffn2_w_up", "ffn2_w_down")


def _ffn_split(fs):
    main = (fs // MXU_WIDTH_V7X) * MXU_WIDTH_V7X
    tail = fs - main
    tail_pad = -(-tail // LANES) * LANES
    assert main > 0 and tail > 0
    return main, tail, tail_pad


def _layout(name, shard_shape):
    r, c = shard_shape
    if name in _COL_FFN:
        main, tail, pad = _ffn_split(c)
        return (r, N_DEV * (main + pad)), [(1, 0, main, (r, main), (0, main)),
                                           (1, N_DEV * main, pad, (r, pad), (main, c))]
    if name in _ROW_FFN:
        main, tail, pad = _ffn_split(r)
        return (N_DEV * (main + pad), c), [(0, 0, main, (main, c), (0, main)),
                                           (0, N_DEV * main, pad, (pad, c), (main, r))]
    if name == "w_in":
        return (r, N_DEV * c), [(1, 0, c, (r, c), (0, c))]
    return (N_DEV * r, c), [(0, 0, r, (r, c), (0, r))]


def _shard_pieces(name, shard):
    out = []
    for axis, _, _, shape, (lo, hi) in _layout(name, shard.shape)[1]:
        part = shard[lo:hi, :] if axis == 0 else shard[:, lo:hi]
        pad = [(0, shape[0] - part.shape[0]), (0, shape[1] - part.shape[1])]
        out.append(jnp.pad(part, pad).astype(BF16))
    return out


def _gather_plan(names, shards):
    srcs, land_shapes, pieces, index = [], [], [], {}
    for li, name in enumerate(names):
        shape2d, parts = _layout(name, shards[name].shape)
        land_shapes.append(jax.ShapeDtypeStruct(shape2d, BF16))
        index[name] = []
        for (axis, base, stride, shape, _), src in zip(parts, _shard_pieces(name, shards[name])):
            index[name].append(len(pieces))
            pieces.append(("gather", len(srcs), li, axis, base, stride, shape))
            srcs.append(src)
    return srcs, land_shapes, pieces, index


def _scatter_plan(names, grads, shard_shapes):
    srcs, land_shapes, pieces, index = [], [], [], {}
    for si, name in enumerate(names):
        _, parts = _layout(name, shard_shapes[name])
        srcs.append(grads[name])
        index[name] = []
        for axis, base, stride, shape, _ in parts:
            index[name].append(len(land_shapes))
            pieces.append(("scatter", si, len(land_shapes), axis, base, stride, shape))
            land_shapes.append(jax.ShapeDtypeStruct((N_DEV,) + shape, grads[name].dtype))
    return srcs, land_shapes, pieces, index


def _small_views(small):
    row = lambda a: a.reshape(1, -1)
    ln = {k: row(small[k]) for k in ("ln1_g", "ln1_b", "ln2_g", "ln2_b", "ln3_g", "ln3_b", "ln4_g", "ln4_b",
                                      "mem_ln_g", "mem_ln_b", "hg_norm_g")}
    sg_w = small["sg_w_s"].reshape(SG_GROUPS, SG_CHUNK, SG_CHUNK)
    sg = dict(logits=jnp.swapaxes(small["hg_lb_logits"], 0, 1),
              g=small["sg_ln_g"].reshape(SG_GROUPS, 1, SG_DIM), b=small["sg_ln_b"].reshape(SG_GROUPS, 1, SG_DIM),
              w=sg_w, wt=jnp.swapaxes(sg_w, 1, 2), bs=small["sg_b_s"].reshape(SG_GROUPS, SG_CHUNK, 1))
    return ln, sg


def _forward(x, mem, target, get_w, small, first_deps=()):
    ln, sg = _small_views(small)
    xb = x.astype(BF16)
    a1, b1, s1 = _ffn_up(xb, get_w("ffn1_w_gate", ()), get_w("ffn1_w_up", ()), "ffn1_up", deps=first_deps)
    h1b, xh1, rs1 = _mm_res_ln(s1, get_w("ffn1_w_down", (s1,)), x, ln["ln1_g"], ln["ln1_b"], 0.5, "ffn1_down_ln")
    proj = _mm_nn(h1b, get_w("w_in", (h1b,)), "mix_in")
    oraw, mix, states = _hgrn_fwd(proj, sg["logits"], ln["hg_norm_g"])
    mix = _sgu_fwd(proj, mix, sg["g"], sg["b"], sg["w"], sg["bs"])
    h2b, xh2, rs2 = _mm_res_ln(mix, get_w("w_out", (mix,)), (xh1, ln["ln1_g"], ln["ln1_b"]), ln["ln2_g"], ln["ln2_b"],
                               1.0, "mix_out_ln")
    mb, mxh, mrs, kb, vb = _mem_kv(mem, ln["mem_ln_g"], ln["mem_ln_b"], get_w("xa_w_k", (h2b,)), get_w("xa_w_v", (h2b,)))
    qb, att = _attn_fwd(h2b, get_w("xa_w_q", (kb,)), kb, vb)
    h3b, xh3, rs3 = _mm_res_ln(att, get_w("xa_w_o", (att,)), (xh2, ln["ln2_g"], ln["ln2_b"]), ln["ln3_g"], ln["ln3_b"],
                               1.0, "attn_out_ln")
    a2, b2, s2 = _ffn_up(h3b, get_w("ffn2_w_gate", (h3b,)), get_w("ffn2_w_up", (h3b,)), "ffn2_up")
    loss, dy4, dy4b, dg4, db4 = _mm_res_ln(s2, get_w("ffn2_w_down", (s2,)), (xh3, ln["ln3_g"], ln["ln3_b"]),
                                           ln["ln4_g"], ln["ln4_b"], 0.5, "ffn2_down_ln_loss", target=target)
    return dict(xb=xb, a1=a1, b1=b1, s1=s1, h1b=h1b, xh1=xh1, rs1=rs1, proj=proj, oraw=oraw, mix=mix, states=states,
                h2b=h2b, xh2=xh2, rs2=rs2, mb=mb, mxh=mxh, mrs=mrs, kb=kb, vb=vb, qb=qb, att=att, h3b=h3b, xh3=xh3,
                rs3=rs3, a2=a2, b2=b2, s2=s2, loss=loss, dy4=dy4, dy4b=dy4b, dg4=dg4, db4=db4)


def _backward(sv, wt, small, send):
    ln, sg = _small_views(small)
    gs = {"ln4_g": sv["dg4"], "ln4_b": sv["db4"]}
    loss, dy4, dy4b = sv["loss"], sv["dy4"], sv["dy4b"]
    g_down2 = _mm_tn(sv["s2"], dy4b, "g_ffn2_down", scale=0.5)
    da2, db2, dy3, dy3b, gs["ln3_g"], gs["ln3_b"] = _ffn_bwd_fused(
        dy4, dy4b, wt["ffn2_w_down"], wt["ffn2_w_gate"], wt["ffn2_w_up"], sv["a2"], sv["b2"], 0.5,
        (sv["xh3"], sv["rs3"], ln["ln3_g"]), "ffn2_bwd")
    g_gate2 = _mm_tn(sv["h3b"], da2, "g_ffn2_gate")
    g_up2 = _mm_tn(sv["h3b"], db2, "g_ffn2_up")
    tok = send(("ffn2_w_down", "ffn2_w_gate", "ffn2_w_up"), (g_down2, g_gate2, g_up2))

    g_o = _mm_tn(sv["att"], dy3b, "g_xa_o", deps=(tok,))
    dqb, dk, dv = _attn_bwd(dy3b, wt["xa_w_o"], sv["qb"], sv["kb"], sv["vb"])
    g_q = _mm_tn(sv["h2b"], dqb, "g_xa_q")
    g_k, g_v, gs["mem_ln_g"], gs["mem_ln_b"] = _mem_bwd(dk, dv, sv["mb"], sv["mxh"], sv["mrs"], ln["mem_ln_g"],
                                                        wt["xa_w_k"], wt["xa_w_v"])
    tok = send(("xa_w_o", "xa_w_q", "xa_w_k", "xa_w_v"), (g_o, g_q, g_k, g_v))
    dy2, dy2b, gs["ln2_g"], gs["ln2_b"] = _dx_ln(dy3, [(dqb, wt["xa_w_q"])], (sv["xh2"], sv["rs2"], ln["ln2_g"]),
                                                 "attn_dx_ln", deps=(tok,))

    g_out = _mm_tn(sv["mix"], dy2b, "g_w_out")
    dmix = _mm_nt(dy2b, wt["w_out"], "mix_out_bwd")
    dq, dfz, div, dgg, dlg, dgn = _hgrn_bwd(sv["proj"], sv["oraw"], dmix, sv["states"], sg["logits"], ln["hg_norm_g"])
    du, dvv, gs["sg_ln_g"], gs["sg_ln_b"], gs["sg_w_s"], gs["sg_b_s"] = _sgu_bwd(
        sv["proj"], dmix, sg["g"], sg["b"], sg["w"], sg["wt"], sg["bs"])
    gs["hg_lb_logits"] = jnp.swapaxes(dlg, 0, 1)
    gs["hg_norm_g"] = jnp.sum(dgn, axis=0)
    dproj = jnp.concatenate([dq, dfz, div, dgg, du, dvv], axis=1)
    g_in = _mm_tn(sv["h1b"], dproj, "g_w_in")
    tok = send(("w_out", "w_in"), (g_out, g_in))
    dy1, dy1b, gs["ln1_g"], gs["ln1_b"] = _dx_ln(dy2, [(dproj, wt["w_in"])], (sv["xh1"], sv["rs1"], ln["ln1_g"]),
                                                 "mix_dx_ln", deps=(tok,))

    g_down1 = _mm_tn(sv["s1"], dy1b, "g_ffn1_down", scale=0.5)
    tok = send(("ffn1_w_down",), (g_down1,))
    da1, db1 = _ffn_bwd_act(dy1b, wt["ffn1_w_down"], sv["a1"], sv["b1"], 0.5, "ffn1_bwd_act", deps=(tok,))
    g_gate1 = _mm_tn(sv["xb"], da1, "g_ffn1_gate")
    tok = send(("ffn1_w_gate",), (g_gate1,))
    g_up1 = _mm_tn(sv["xb"], db1, "g_ffn1_up", deps=(tok,))
    tok = send(("ffn1_w_up",), (g_up1,))
    grad_x = _dx_ln(dy1, [(da1, wt["ffn1_w_gate"]), (db1, wt["ffn1_w_up"])], None, "ffn1_dx", deps=(tok,))
    return loss, grad_x, gs


_WEIGHT_NAMES = ("ffn1_w_gate", "ffn1_w_up", "ffn1_w_down", "ln1_g", "ln1_b", "w_in", "hg_lb_logits", "hg_norm_g",
                 "sg_ln_g", "sg_ln_b", "sg_w_s", "sg_b_s", "w_out", "ln2_g", "ln2_b", "mem_ln_g", "mem_ln_b",
                 "xa_w_q", "xa_w_k", "xa_w_v", "xa_w_o", "ln3_g", "ln3_b", "ffn2_w_gate", "ffn2_w_up", "ffn2_w_down",
                 "ln4_g", "ln4_b")
_FIRST = ("ffn1_w_gate", "ffn1_w_up")
_SECOND = ("ffn1_w_down", "w_in", "w_out")
_THIRD = ("xa_w_k", "xa_w_v", "xa_w_q", "xa_w_o", "ffn2_w_gate", "ffn2_w_up", "ffn2_w_down")


def kernel(x, mem, ffn1_w_gate, ffn1_w_up, ffn1_w_down, ln1_g, ln1_b, w_in, hg_lb_logits, hg_norm_g, sg_ln_g, sg_ln_b, sg_w_s, sg_b_s, w_out, ln2_g, ln2_b, mem_ln_g, mem_ln_b, xa_w_q, xa_w_k, xa_w_v, xa_w_o, ln3_g, ln3_b, ffn2_w_gate, ffn2_w_up, ffn2_w_down, ln4_g, ln4_b, loss_target, m_ffn1_w_gate, m_ffn1_w_up, m_ffn1_w_down, m_ln1_g, m_ln1_b, m_w_in, m_hg_lb_logits, m_hg_norm_g, m_sg_ln_g, m_sg_ln_b, m_sg_w_s, m_sg_b_s, m_w_out, m_ln2_g, m_ln2_b, m_mem_ln_g, m_mem_ln_b, m_xa_w_q, m_xa_w_k, m_xa_w_v, m_xa_w_o, m_ln3_g, m_ln3_b, m_ffn2_w_gate, m_ffn2_w_up, m_ffn2_w_down, m_ln4_g, m_ln4_b, v_ffn1_w_gate, v_ffn1_w_up, v_ffn1_w_down, v_ln1_g, v_ln1_b, v_w_in, v_hg_lb_logits, v_hg_norm_g, v_sg_ln_g, v_sg_ln_b, v_sg_w_s, v_sg_b_s, v_w_out, v_ln2_g, v_ln2_b, v_mem_ln_g, v_mem_ln_b, v_xa_w_q, v_xa_w_k, v_xa_w_v, v_xa_w_o, v_ln3_g, v_ln3_b, v_ffn2_w_gate, v_ffn2_w_up, v_ffn2_w_down, v_ln4_g, v_ln4_b):
    args = dict(locals())
    w = {k: args[k] for k in _WEIGHT_NAMES}
    m = {k: args["m_" + k] for k in _WEIGHT_NAMES}
    v = {k: args["v_" + k] for k in _WEIGHT_NAMES}
    shards = {k: w[k][0] for k in _BIG_NAMES}
    shard_shapes = {k: shards[k].shape for k in _BIG_NAMES}
    small = {k: (w[k][0] if k != "hg_lb_logits" else w[k]) for k in _SMALL_NAMES}

    srcs1, shapes1, pieces1, idx1 = _gather_plan(_FIRST, shards)
    lands1 = _place_own(srcs1, shapes1, pieces1, "gather_first_own")
    rest = _SECOND + _THIRD
    srcs2, shapes2, pieces2, idx2 = _gather_plan(rest, shards)
    lands2 = _place_own(srcs2, shapes2, pieces2, "gather_rest_own")
    groups2 = [list(idx2[k]) for k in rest]
    lands1, tok1 = _routed_gather(srcs1, lands1, pieces1, tuple(lands2), "gather_first")
    sems2, srcs2, lands2, tok2 = _comm_start(srcs2, lands2, pieces2, groups2, "gather_rest_start", after=(tok1,))
    wt = dict(zip(_FIRST, lands1))
    pending = {k: gi for gi, k in enumerate(rest)}

    def get_w(name, after):
        if name in pending:
            gi = pending.pop(name)
            si = [pieces2[p][1] for p in groups2[gi]]
            sub = [(pieces2[p][0], row, 0) + pieces2[p][3:] for row, p in enumerate(groups2[gi])]
            wt[name] = _comm_wait([srcs2[s] for s in si], [lands2[gi]], sub, list(range(len(sub))), sems2[gi],
                                  after, "gather_wait_" + name)[0]
        return wt[name]

    sv = _forward(x[0], mem[0], loss_target[0], get_w, small, first_deps=(tok2,))

    sent = []

    def send(names, grads):
        srcs, shapes, pieces, idx = _scatter_plan(names, dict(zip(names, grads)), shard_shapes)
        lands = _place_own(srcs, shapes, pieces, "grads_own_%d" % len(sent))
        sems, srcs, lands, tok = _comm_start(srcs, lands, pieces, [list(range(len(pieces)))],
                                             "grads_start_%d" % len(sent))
        sent.append((names, srcs, lands, pieces, idx, sems[0]))
        return tok

    loss, grad_x, gs = _backward(sv, wt, small, send)

    ssrc = list(_pack_small_grads(gs, {k: w[k].shape for k in _SMALL_NAMES}, loss))
    sp = [("scatter", i, i, 0, 0, 0, a.shape) for i, a in enumerate(ssrc)]
    sshape = [jax.ShapeDtypeStruct((N_DEV,) + a.shape, F32) for a in ssrc]
    sl = _place_own(ssrc, sshape, sp, "small_own")
    ssem, ssrc, sl, _ = _comm_start(ssrc, sl, sp, [[0, 1]], "small_start")

    out_g, out_d, out_m, out_v = {}, {}, {}, {}
    after = (grad_x,)
    for n_sent, (names, srcs, lands, pieces, idx, sems) in enumerate(sent):
        lands = _comm_wait(srcs, lands, pieces, list(range(len(pieces))), sems, after, "grads_wait_%d" % n_sent)
        for k in names:
            axis = 1 if (k in _COL_FFN or k == "w_in") else 0
            if k in _COL_FFN:
                res = _adam_sharded([lands[i] for i in idx[k]], w[k][0].T, m[k][0].T, v[k][0].T, axis, "adam_" + k)
                res = [r.T for r in res]
            else:
                res = _adam_sharded([lands[i] for i in idx[k]], w[k][0], m[k][0], v[k][0], axis, "adam_" + k)
            out_g[k], out_d[k], out_m[k], out_v[k] = [r[None] for r in res]
        after = (out_v[names[-1]],)
    sl = _comm_wait(ssrc, sl, sp, [0, 1], ssem[0], after, "small_wait")
    small_out, loss_sum = _adam_small(sl[0], sl[1], w, m, v)
    for dst, res in zip((out_g, out_d, out_m, out_v), small_out):
        dst.update(res)
    loss_all = loss_sum[0, 0]
    return (loss_all, grad_x[None], *[out_g[k] for k in _WEIGHT_NAMES], *[out_d[k] for k in _WEIGHT_NAMES],
            *[out_m[k] for k in _WEIGHT_NAMES], *[out_v[k] for k in _WEIGHT_NAMES])
```

```python
import itertools

import jax
import jax.numpy as jnp
import numpy as np
from jax import lax
from jax.experimental import pallas as pl
from jax.experimental.pallas import tpu as pltpu

F32 = jnp.float32
BF16 = jnp.bfloat16

N_DEV = 8
ALPHA = 2.0 ** 0.25
LN_EPS = 1e-5
HG_HEADS = 4
HG_DIM = 128
SG_GROUPS = 4
SG_DIM = 128
SG_CHUNK = 128
X_HEADS = 4
HG_BLOCK = 16
HG_UNROLL = 8
ADAM_LR = 0.001
ADAM_B1 = 0.9
ADAM_B2 = 0.999
ADAM_EPS = 1e-08
ADAM_WD = 0.01
ADAM_STEP = 10
VMEM_LIMIT_V7X = 48 * 1024 * 1024
MXU_WIDTH_V7X = 256
LANES = 128
MESH_ID = pl.DeviceIdType.MESH
ANY = pl.BlockSpec(memory_space=pl.ANY)
HBM = pl.BlockSpec(memory_space=pltpu.HBM)
SEM = pl.BlockSpec(memory_space=pltpu.SEMAPHORE)
DATAFLOW = pltpu.SideEffectType.DATAFLOW_SIDE_EFFECTING


def _params(n_axes):
    return pltpu.CompilerParams(dimension_semantics=("arbitrary",) * n_axes, vmem_limit_bytes=VMEM_LIMIT_V7X)


def _dot(a, b):
    return jnp.dot(a, b, preferred_element_type=F32)


def _dot_nt(a, b):
    return lax.dot_general(a, b, (((1,), (1,)), ((), ())), preferred_element_type=F32)


def _dot_tn(a, b):
    return lax.dot_general(a, b, (((0,), (0,)), ((), ())), preferred_element_type=F32)


def _sigmoid(x):
    return 1.0 / (1.0 + jnp.exp(-x))


def _silu_and_grad(a):
    sig = _sigmoid(a)
    return a * sig, sig * (1.0 + a * (1.0 - sig))


_GELU_C = 0.7978845608028654


def _gelu_and_grad(x):
    inner = _GELU_C * (x + 0.044715 * x * x * x)
    t = jnp.tanh(inner)
    val = 0.5 * x * (1.0 + t)
    grad = 0.5 * (1.0 + t) + 0.5 * x * (1.0 - t * t) * _GELU_C * (1.0 + 3.0 * 0.044715 * x * x)
    return val, grad


def _ln_fwd(y, g, b):
    mu = jnp.mean(y, axis=-1, keepdims=True)
    yc = y - mu
    var = jnp.mean(yc * yc, axis=-1, keepdims=True)
    rstd = lax.rsqrt(var + LN_EPS)
    xhat = yc * rstd
    return xhat * g + b, xhat, rstd


def _ln_bwd(dh, xhat, rstd, g):
    dxh = dh * g
    m1 = jnp.mean(dxh, axis=-1, keepdims=True)
    m2 = jnp.mean(dxh * xhat, axis=-1, keepdims=True)
    dy = rstd * (dxh - m1 - xhat * m2)
    dg = jnp.sum(dh * xhat, axis=0, keepdims=True)
    db = jnp.sum(dh, axis=0, keepdims=True)
    return dy, dg, db


def _mask_dot(mask, x):
    hi = x.astype(BF16)
    lo = (x - hi.astype(F32)).astype(BF16)
    n = mask.shape[0]
    parts = [_dot(mask, hi[r:r + n, :]) + _dot(mask, lo[r:r + n, :]) for r in range(0, x.shape[0], n)]
    return parts[0] if len(parts) == 1 else jnp.concatenate(parts, axis=0)


def _block_masks(n):
    r = np.arange(n)[:, None]
    c = np.arange(n)[None, :]
    same = (r // HG_BLOCK) == (c // HG_BLOCK)
    return jnp.asarray(np.stack([same & (c <= r), same & (c >= r), same]), BF16)


def _row_tile(t):
    return min(t, 512)


def _col_tile(n):
    for cand in (512, 256, 128):
        if n % cand == 0:
            return cand
    return n


def _resident(w):
    return pl.BlockSpec(w.shape, lambda *_: (0, 0), pipeline_mode=pl.Buffered(1))


def _drop_deps(body, n_in, n_deps):
    if n_deps == 0:
        return body
    return lambda *refs: body(*refs[:n_in], *refs[n_in + n_deps:])


def _ffn_up(hb, wg, wu, name, deps=()):
    t, d = hb.shape
    f = wg.shape[1]
    tm = min(t, 256)
    tn = _col_tile(f)

    def body(h_ref, wg_ref, wu_ref, a_ref, b_ref, s_ref):
        h = h_ref[...]
        for c in range(f // tn):
            cols = slice(c * tn, (c + 1) * tn)
            a = _dot(h, wg_ref[:, cols])
            b = _dot(h, wu_ref[:, cols])
            a_ref[:, cols] = a.astype(BF16)
            b_ref[:, cols] = b.astype(BF16)
            s_ref[:, cols] = (a * _sigmoid(a) * b).astype(BF16)

    act = pl.BlockSpec((tm, f), lambda i: (i, 0))
    return pl.pallas_call(
        _drop_deps(body, 3, len(deps)),
        grid=(t // tm,),
        in_specs=[pl.BlockSpec((tm, d), lambda i: (i, 0)), _resident(wg), _resident(wu)] + [ANY] * len(deps),
        out_specs=[act, act, act],
        out_shape=[jax.ShapeDtypeStruct((t, f), BF16)] * 3,
        compiler_params=_params(1),
        name=name,
    )(hb, wg, wu, *deps)


def _mm_res_ln(lhs, w, res, g, b, coef, name, target=None):
    t, kd = lhs.shape
    d = w.shape[1]
    tm = _row_tile(t)
    nt = t // tm
    from_norm = isinstance(res, tuple)
    n_res = 3 if from_norm else 1

    def body(*refs):
        l_ref, w_ref = refs[:2]
        r_refs = refs[2:2 + n_res]
        g_ref, b_ref = refs[2 + n_res:4 + n_res]
        rest = refs[4 + n_res:]
        prev = r_refs[0][...] * r_refs[1][...] + r_refs[2][...] if from_norm else r_refs[0][...]
        y = ALPHA * prev + coef * _dot(l_ref[...], w_ref[...])
        h, xhat, rstd = _ln_fwd(y, g_ref[...], b_ref[...])
        if target is None:
            hb_ref, xh_ref, rs_ref = rest
            hb_ref[...] = h.astype(BF16)
            xh_ref[...] = xhat
            rs_ref[...] = rstd
            return
        t_ref, loss_ref, dy_ref, dyb_ref, dg_ref, db_ref, lacc = rest
        i = pl.program_id(0)

        @pl.when(i == 0)
        def _():
            lacc[...] = jnp.zeros_like(lacc)
            dg_ref[...] = jnp.zeros_like(dg_ref)
            db_ref[...] = jnp.zeros_like(db_ref)

        err = h - t_ref[...]
        lacc[...] += jnp.sum(err * err, axis=0, keepdims=True)
        dy, dg, db = _ln_bwd(err * (1.0 / d), xhat, rstd, g_ref[...])
        dy_ref[...] = dy
        dyb_ref[...] = dy.astype(BF16)
        dg_ref[...] += dg
        db_ref[...] += db

        @pl.when(i == nt - 1)
        def _():
            loss_ref[...] = jnp.zeros_like(loss_ref) + jnp.sum(lacc[...], axis=1, keepdims=True) * (0.5 / d)

    row = pl.BlockSpec((tm, d), lambda i: (i, 0))
    vec = pl.BlockSpec((1, d), lambda i: (0, 0))
    res_specs = [row, vec, vec] if from_norm else [row]
    res_args = list(res) if from_norm else [res]
    in_specs = [pl.BlockSpec((tm, kd), lambda i: (i, 0)), _resident(w)] + res_specs + [vec, vec]
    args = [lhs, w] + res_args + [g, b]
    if target is None:
        out_specs = [row, row, pl.BlockSpec((tm, 1), lambda i: (i, 0))]
        out_shape = [jax.ShapeDtypeStruct((t, d), BF16), jax.ShapeDtypeStruct((t, d), F32),
                     jax.ShapeDtypeStruct((t, 1), F32)]
        scratch = []
    else:
        in_specs.append(row)
        args.append(target)
        out_specs = [pl.BlockSpec((1, LANES), lambda i: (0, 0)), row, row, vec, vec]
        out_shape = [jax.ShapeDtypeStruct((1, LANES), F32), jax.ShapeDtypeStruct((t, d), F32),
                     jax.ShapeDtypeStruct((t, d), BF16), jax.ShapeDtypeStruct((1, d), F32),
                     jax.ShapeDtypeStruct((1, d), F32)]
        scratch = [pltpu.VMEM((1, d), F32)]
    return pl.pallas_call(
        body,
        grid=(nt,),
        in_specs=in_specs,
        out_specs=out_specs,
        out_shape=out_shape,
        scratch_shapes=scratch,
        compiler_params=_params(1),
        name=name,
    )(*args)


def _mm_nn(lhs, w, name):
    t, kd = lhs.shape
    n = w.shape[1]
    tm = _row_tile(t)
    tn = _col_tile(n)

    def body(l_ref, w_ref, o_ref):
        lhs_v = l_ref[...]
        for c in range(n // tn):
            cols = slice(c * tn, (c + 1) * tn)
            o_ref[:, cols] = _dot(lhs_v, w_ref[:, cols])

    return pl.pallas_call(
        body,
        grid=(t // tm,),
        in_specs=[pl.BlockSpec((tm, kd), lambda i: (i, 0)), _resident(w)],
        out_specs=pl.BlockSpec((tm, n), lambda i: (i, 0)),
        out_shape=jax.ShapeDtypeStruct((t, n), F32),
        compiler_params=_params(1),
        name=name,
    )(lhs, w)


def _lower_bound(lg):
    m = jnp.max(lg, axis=0, keepdims=True)
    e = jnp.exp(lg - m)
    return e[0:1, :] / jnp.sum(e, axis=0, keepdims=True)


def _forget_terms(fz, lb):
    e = jnp.exp(-jnp.abs(fz))
    r = 1.0 / (1.0 + e)
    pos = fz >= 0.0
    sig = jnp.where(pos, r, e * r)
    nsig = jnp.where(pos, e * r, r)
    f = lb + (1.0 - lb) * sig
    k = (1.0 - lb) * nsig
    return sig, nsig, f, k


def _hg_tile(t):
    return min(t, 512)


HG_HALF = HG_BLOCK // 2
NEG_BIG = -1e30


def _halves(a):
    return a[:HG_HALF, :], a[HG_HALF:, :]


def _causal_halves(s):
    return (0, 1) if s < HG_HALF else (1,)


def _decay_from(b_half, b_s, s, h, tidx):
    first = s - h * HG_HALF
    diff = b_half - b_s
    if first > 0:
        diff = jnp.where(tidx >= first, diff, NEG_BIG)
    return jnp.exp(diff)


def _hgrn_fwd(proj, logits, gn):
    t = proj.shape[0]
    ct = _hg_tile(t)
    nct = t // ct
    nblk = ct // HG_BLOCK
    nh = HG_HEADS
    mrows = min(ct, 256)

    def body(q_ref, fz_ref, iv_ref, gg_ref, lg_ref, gn_ref, mask_ref, oraw_ref, oa_ref, st_ref,
             state, qt_s, kt_s, k_s, b_s, dec_s):
        c = pl.program_id(1)

        @pl.when(c == 0)
        def _():
            state[...] = jnp.zeros_like(state)

        lb = _lower_bound(lg_ref[...])
        q = q_ref[...]
        _, _, f, k = _forget_terms(fz_ref[...], lb)
        logf = jnp.log(f)
        b = _mask_dot(mask_ref[0], logf)
        bend = _mask_dot(mask_ref[2], logf)
        qt_s[...] = (q * jnp.exp(b)).astype(BF16)
        kt_s[...] = (k * jnp.exp(bend - b)).astype(BF16)
        k_s[...] = k
        b_s[...] = b
        dec_s[...] = jnp.exp(bend)
        tidx = lax.broadcasted_iota(jnp.int32, (HG_HALF, HG_DIM), 0)

        def blk(i, carry):
            r0 = pl.multiple_of(i * HG_BLOCK, HG_BLOCK)
            rows = pl.ds(r0, HG_BLOCK)
            st = state[...]
            st_ref[i] = st
            v = iv_ref[rows, :]
            qq = q_ref[rows, :]
            kk = k_s[rows, :]
            bb = b_s[rows, :]
            o = list(_halves(_dot_nt(qt_s[rows, :], st.astype(BF16))))
            qh, bh = _halves(qq), _halves(bb)
            for s in range(HG_BLOCK):
                ks, vs = kk[s:s + 1, :], v[s:s + 1, :]
                for h in _causal_halves(s):
                    e = _decay_from(bh[h], bb[s:s + 1, :], s, h, tidx)
                    acol = jnp.sum(qh[h] * (ks * e), axis=1, keepdims=True)
                    o[h] = o[h] + acol * vs
            oraw_ref[rows, :] = jnp.concatenate(o, axis=0)
            state[...] = st * dec_s[pl.ds(r0, 1), :] + _dot_tn(v.astype(BF16), kt_s[rows, :])
            return carry

        lax.fori_loop(0, nblk, blk, 0, unroll=2 * HG_UNROLL)
        oraw = oraw_ref[...]
        r = lax.rsqrt(jnp.mean(oraw * oraw, axis=-1, keepdims=True) + LN_EPS)
        gg = gg_ref[...]
        oa_ref[...] = (oraw * r * gn_ref[...] * gg * _sigmoid(gg)).astype(BF16)

    def slab(off):
        return pl.BlockSpec((ct, HG_DIM), lambda h, c: (c, off + h))

    out_slab = pl.BlockSpec((ct, HG_DIM), lambda h, c: (c, h))
    return pl.pallas_call(
        body,
        grid=(nh, nct),
        in_specs=[slab(0), slab(nh), slab(2 * nh), slab(3 * nh),
                  pl.BlockSpec((None, 2, HG_DIM), lambda h, c: (h, 0, 0)),
                  pl.BlockSpec((1, HG_DIM), lambda h, c: (0, 0)),
                  pl.BlockSpec((3, mrows, mrows), lambda h, c: (0, 0, 0))],
        out_specs=[out_slab, out_slab, pl.BlockSpec((None, nblk, HG_DIM, HG_DIM), lambda h, c: (h, c, 0, 0))],
        out_shape=[jax.ShapeDtypeStruct((t, nh * HG_DIM), F32),
                   jax.ShapeDtypeStruct((t, (nh + SG_GROUPS) * HG_DIM), BF16),
                   jax.ShapeDtypeStruct((nh, t // HG_BLOCK, HG_DIM, HG_DIM), F32)],
        scratch_shapes=[pltpu.VMEM((HG_DIM, HG_DIM), F32), pltpu.VMEM((ct, HG_DIM), BF16),
                        pltpu.VMEM((ct, HG_DIM), BF16), pltpu.VMEM((ct, HG_DIM), F32),
                        pltpu.VMEM((ct, HG_DIM), F32), pltpu.VMEM((ct, HG_DIM), F32)],
        compiler_params=_params(2),
        name="hgrn_fwd",
    )(proj, proj, proj, proj, logits, gn, _block_masks(mrows))


def _sg_tile(t):
    return min(t, 512)


def _sgu_chunk_fwd(u, v, ln_g, ln_b, wm, bs):
    ua, dua = _gelu_and_grad(u)
    va, dva = _gelu_and_grad(v)
    vn, xhat, rstd = _ln_fwd(va, ln_g, ln_b)
    s = _dot(wm, vn.astype(BF16)) + bs
    return ua, dua, dva, vn, xhat, rstd, s


def _tril_weight(w_ref):
    n = SG_CHUNK
    r = lax.broadcasted_iota(jnp.int32, (n, n), 0)
    c = lax.broadcasted_iota(jnp.int32, (n, n), 1)
    return jnp.where(c <= r, w_ref[...], 0.0)


def _sgu_fwd(proj, mix, ln_g, ln_b, w_s, b_col):
    t = proj.shape[0]
    ct = _sg_tile(t)
    ng = SG_GROUPS
    off_u = 4 * HG_HEADS
    off_v = off_u + ng

    def body(u_ref, v_ref, g_ref, b_ref, w_ref, bs_ref, mix_ref, o_ref):
        del mix_ref
        wm = _tril_weight(w_ref).astype(BF16)
        for n in range(ct // SG_CHUNK):
            rows = slice(n * SG_CHUNK, (n + 1) * SG_CHUNK)
            ua, _, _, _, _, _, s = _sgu_chunk_fwd(u_ref[rows, :], v_ref[rows, :], g_ref[...], b_ref[...], wm, bs_ref[...])
            o_ref[rows, :] = (ua * s).astype(BF16)

    vec = pl.BlockSpec((None, 1, SG_DIM), lambda g, c: (g, 0, 0))
    return pl.pallas_call(
        body,
        grid=(ng, t // ct),
        in_specs=[pl.BlockSpec((ct, SG_DIM), lambda g, c: (c, off_u + g)),
                  pl.BlockSpec((ct, SG_DIM), lambda g, c: (c, off_v + g)), vec, vec,
                  pl.BlockSpec((None, SG_CHUNK, SG_CHUNK), lambda g, c: (g, 0, 0)),
                  pl.BlockSpec((None, SG_CHUNK, 1), lambda g, c: (g, 0, 0)), ANY],
        out_specs=pl.BlockSpec((ct, SG_DIM), lambda g, c: (c, HG_HEADS + g)),
        out_shape=jax.ShapeDtypeStruct(mix.shape, mix.dtype),
        input_output_aliases={6: 0},
        compiler_params=_params(2),
        name="sgu_fwd",
    )(proj, proj, ln_g, ln_b, w_s, b_col, mix)


def _mem_kv(mem, g, b, wk, wv):
    m_len, d = mem.shape

    def body(m_ref, g_ref, b_ref, wk_ref, wv_ref, mb_ref, xh_ref, rs_ref, k_ref, v_ref):
        m, xhat, rstd = _ln_fwd(m_ref[...], g_ref[...], b_ref[...])
        mb = m.astype(BF16)
        mb_ref[...] = mb
        xh_ref[...] = xhat
        rs_ref[...] = rstd
        k_ref[...] = _dot(mb, wk_ref[...]).astype(BF16)
        v_ref[...] = _dot(mb, wv_ref[...]).astype(BF16)

    return pl.pallas_call(
        body,
        out_shape=[jax.ShapeDtypeStruct((m_len, d), BF16), jax.ShapeDtypeStruct((m_len, d), F32),
                   jax.ShapeDtypeStruct((m_len, 1), F32), jax.ShapeDtypeStruct((m_len, d), BF16),
                   jax.ShapeDtypeStruct((m_len, d), BF16)],
        compiler_params=pltpu.CompilerParams(vmem_limit_bytes=VMEM_LIMIT_V7X),
        name="mem_kv",
    )(mem, g, b, wk, wv)


def _softmax_rows(s):
    m = jnp.max(s, axis=-1, keepdims=True)
    p = jnp.exp(s - m)
    return p / jnp.sum(p, axis=-1, keepdims=True)


def _attn_fwd(hb, wq, kb, vb):
    t, d = hb.shape
    tm = _row_tile(t)
    dh = d // X_HEADS
    scale = dh ** -0.5

    def body(h_ref, wq_ref, k_ref, v_ref, q_ref, o_ref):
        q = _dot(h_ref[...], wq_ref[...]).astype(BF16)
        q_ref[...] = q
        for hd in range(X_HEADS):
            sl = slice(hd * dh, (hd + 1) * dh)
            p = _softmax_rows(_dot_nt(q[:, sl], k_ref[:, sl]) * scale)
            o_ref[:, sl] = _dot(p.astype(BF16), v_ref[:, sl]).astype(BF16)

    row = pl.BlockSpec((tm, d), lambda i: (i, 0))
    full = lambda a: pl.BlockSpec(a.shape, lambda i: (0, 0))
    return pl.pallas_call(
        body,
        grid=(t // tm,),
        in_specs=[row, full(wq), full(kb), full(vb)],
        out_specs=[row, row],
        out_shape=[jax.ShapeDtypeStruct((t, d), BF16), jax.ShapeDtypeStruct((t, d), BF16)],
        compiler_params=_params(1),
        name="attn_fwd",
    )(hb, wq, kb, vb)


def _ffn_bwd_act(dyb, wd, a, b, coef, name, deps=()):
    t, d = dyb.shape
    f = wd.shape[0]
    tm = _row_tile(t)
    tn = _col_tile(f)

    def body(dy_ref, wd_ref, a_ref, b_ref, da_ref, db_ref):
        dy = dy_ref[...]
        for c in range(f // tn):
            cols = slice(c * tn, (c + 1) * tn)
            ds = _dot_nt(dy, wd_ref[cols, :]) * coef
            silu, dsilu = _silu_and_grad(a_ref[:, cols].astype(F32))
            da_ref[:, cols] = (ds * b_ref[:, cols].astype(F32) * dsilu).astype(BF16)
            db_ref[:, cols] = (ds * silu).astype(BF16)

    act = pl.BlockSpec((tm, f), lambda i: (i, 0))
    return pl.pallas_call(
        _drop_deps(body, 4, len(deps)),
        grid=(t // tm,),
        in_specs=[pl.BlockSpec((tm, d), lambda i: (i, 0)), _resident(wd), act, act] + [ANY] * len(deps),
        out_specs=[act, act],
        out_shape=[jax.ShapeDtypeStruct((t, f), BF16), jax.ShapeDtypeStruct((t, f), BF16)],
        compiler_params=_params(1),
        name=name,
    )(dyb, wd, a, b, *deps)


def _ffn_bwd_fused(dy, dyb, wd, wg, wu, a, b, coef, ln, name):
    t, d = dy.shape
    f = wd.shape[0]
    tm = min(t, 256)
    tn = _col_tile(f)

    def body(dy_ref, dyb_ref, wd_ref, wg_ref, wu_ref, a_ref, b_ref, xh_ref, rs_ref, g_ref,
             da_ref, db_ref, dyo_ref, dyob_ref, dg_ref, dbl_ref):
        dyb_v = dyb_ref[...]
        dh = ALPHA * dy_ref[...]
        for c in range(f // tn):
            cols = slice(c * tn, (c + 1) * tn)
            ds = _dot_nt(dyb_v, wd_ref[cols, :]) * coef
            silu, dsilu = _silu_and_grad(a_ref[:, cols].astype(F32))
            da = (ds * b_ref[:, cols].astype(F32) * dsilu).astype(BF16)
            db = (ds * silu).astype(BF16)
            da_ref[:, cols] = da
            db_ref[:, cols] = db
            dh = dh + _dot_nt(da, wg_ref[:, cols]) + _dot_nt(db, wu_ref[:, cols])

        @pl.when(pl.program_id(0) == 0)
        def _():
            dg_ref[...] = jnp.zeros_like(dg_ref)
            dbl_ref[...] = jnp.zeros_like(dbl_ref)

        dyp, dg, dbl = _ln_bwd(dh, xh_ref[...], rs_ref[...], g_ref[...])
        dyo_ref[...] = dyp
        dyob_ref[...] = dyp.astype(BF16)
        dg_ref[...] += dg
        dbl_ref[...] += dbl

    row = pl.BlockSpec((tm, d), lambda i: (i, 0))
    act = pl.BlockSpec((tm, f), lambda i: (i, 0))
    vec = pl.BlockSpec((1, d), lambda i: (0, 0))
    return pl.pallas_call(
        body,
        grid=(t // tm,),
        in_specs=[row, row, _resident(wd), _resident(wg), _resident(wu), act, act, row,
                  pl.BlockSpec((tm, 1), lambda i: (i, 0)), vec],
        out_specs=[act, act, row, row, vec, vec],
        out_shape=[jax.ShapeDtypeStruct((t, f), BF16), jax.ShapeDtypeStruct((t, f), BF16),
                   jax.ShapeDtypeStruct((t, d), F32), jax.ShapeDtypeStruct((t, d), BF16),
                   jax.ShapeDtypeStruct((1, d), F32), jax.ShapeDtypeStruct((1, d), F32)],
        compiler_params=_params(1),
        name=name,
    )(dy, dyb, wd, wg, wu, a, b, *ln)


def _mm_tn(a, b, name, scale=1.0, deps=()):
    t, m = a.shape
    n = b.shape[1]
    tt = _row_tile(t)
    nt = t // tt
    tm_o = m // 2 if (m > n and m * n > 2 ** 21) else m
    tn_o = n // 2 if (n > m and m * n > 2 ** 21) else n

    def body(a_ref, b_ref, o_ref, acc):
        k = pl.program_id(2)

        @pl.when(k == 0)
        def _():
            acc[...] = jnp.zeros_like(acc)

        acc[...] += _dot_tn(a_ref[...], b_ref[...])

        @pl.when(k == nt - 1)
        def _():
            o_ref[...] = (acc[...] * scale).astype(BF16)

    return pl.pallas_call(
        _drop_deps(body, 2, len(deps)),
        grid=(m // tm_o, n // tn_o, nt),
        in_specs=[pl.BlockSpec((tt, tm_o), lambda i, j, k: (k, i)), pl.BlockSpec((tt, tn_o), lambda i, j, k: (k, j))]
        + [ANY] * len(deps),
        out_specs=pl.BlockSpec((tm_o, tn_o), lambda i, j, k: (i, j)),
        out_shape=jax.ShapeDtypeStruct((m, n), BF16),
        scratch_shapes=[pltpu.VMEM((tm_o, tn_o), F32)],
        compiler_params=_params(3),
        name=name,
    )(a, b, *deps)


def _mm_nt(lhs, w, name):
    t, d = lhs.shape
    kd = w.shape[0]
    tm = _row_tile(t)

    def body(l_ref, w_ref, o_ref):
        o_ref[...] = _dot_nt(l_ref[...], w_ref[...])

    return pl.pallas_call(
        body,
        grid=(t // tm,),
        in_specs=[pl.BlockSpec((tm, d), lambda i: (i, 0)), _resident(w)],
        out_specs=pl.BlockSpec((tm, kd), lambda i: (i, 0)),
        out_shape=jax.ShapeDtypeStruct((t, kd), F32),
        compiler_params=_params(1),
        name=name,
    )(lhs, w)


def _dx_ln(dy, pairs, ln, name, deps=()):
    t, d = dy.shape
    npair = len(pairs)
    tm = min(t, 512 // npair)
    nt = t // tm
    n_in = 1 + 2 * npair + (3 if ln is not None else 0)

    def body(*refs):
        dy_ref = refs[0]
        pr = refs[1:1 + 2 * npair]
        pos = 1 + 2 * npair
        dh = ALPHA * dy_ref[...]
        for p in range(npair):
            dh = dh + _dot_nt(pr[2 * p][...], pr[2 * p + 1][...])
        if ln is not None:
            xh_ref, rs_ref, g_ref = refs[pos:pos + 3]
            dyo_ref, dyb_ref, dg_ref, db_ref = refs[pos + 3:pos + 7]

            @pl.when(pl.program_id(0) == 0)
            def _():
                dg_ref[...] = jnp.zeros_like(dg_ref)
                db_ref[...] = jnp.zeros_like(db_ref)

            dyp, dg, db = _ln_bwd(dh, xh_ref[...], rs_ref[...], g_ref[...])
            dyo_ref[...] = dyp
            dyb_ref[...] = dyp.astype(BF16)
            dg_ref[...] += dg
            db_ref[...] += db
        else:
            refs[pos][...] = dh

    row = pl.BlockSpec((tm, d), lambda i: (i, 0))
    vec = pl.BlockSpec((1, d), lambda i: (0, 0))
    in_specs = [row]
    args = [dy]
    for lhs, w in pairs:
        in_specs += [pl.BlockSpec((tm, lhs.shape[1]), lambda i: (i, 0)), _resident(w)]
        args += [lhs, w]
    if ln is not None:
        in_specs += [row, pl.BlockSpec((tm, 1), lambda i: (i, 0)), vec]
        args += list(ln)
        out_specs = [row, row, vec, vec]
        out_shape = [jax.ShapeDtypeStruct((t, d), F32), jax.ShapeDtypeStruct((t, d), BF16),
                     jax.ShapeDtypeStruct((1, d), F32), jax.ShapeDtypeStruct((1, d), F32)]
    else:
        out_specs = row
        out_shape = jax.ShapeDtypeStruct((t, d), F32)
    return pl.pallas_call(
        _drop_deps(body, n_in, len(deps)),
        grid=(nt,),
        in_specs=in_specs + [ANY] * len(deps),
        out_specs=out_specs,
        out_shape=out_shape,
        compiler_params=_params(1),
        name=name,
    )(*args, *deps)


def _hgrn_bwd(proj, oraw, dmix, states, logits, gn):
    t = proj.shape[0]
    ct = _hg_tile(t)
    nct = t // ct
    nblk = ct // HG_BLOCK
    nh = HG_HEADS
    mrows = min(ct, 256)

    def body(q_ref, fz_ref, iv_ref, gg_ref, or_ref, do_ref, st_ref, lg_ref, gn_ref, mask_ref,
             dq_ref, dfz_ref, div_ref, dgg_ref, dlg_ref, dgn_ref,
             dstate, qt_s, kt_s, k_s, b_s, eb_s, ekb_s, dec_s, dor_s, dbl_s, gr_s, dk_s, dlb_acc):
        c = pl.program_id(1)

        @pl.when(c == 0)
        def _():
            dstate[...] = jnp.zeros_like(dstate)
            dlb_acc[...] = jnp.zeros_like(dlb_acc)
            dgn_ref[...] = jnp.zeros_like(dgn_ref)

        lb = _lower_bound(lg_ref[...])
        q = q_ref[...]
        sig, nsig, f, k = _forget_terms(fz_ref[...], lb)
        logf = jnp.log(f)
        b = _mask_dot(mask_ref[0], logf)
        bend = _mask_dot(mask_ref[2], logf)
        eb = jnp.exp(b)
        ekb = jnp.exp(bend - b)
        qt_s[...] = (q * eb).astype(BF16)
        kt_s[...] = (k * ekb).astype(BF16)
        k_s[...] = k
        b_s[...] = b
        eb_s[...] = eb
        ekb_s[...] = ekb
        dec_s[...] = jnp.exp(bend)
        oraw = or_ref[...]
        r = lax.rsqrt(jnp.mean(oraw * oraw, axis=-1, keepdims=True) + LN_EPS)
        on = oraw * r
        gg = gg_ref[...]
        silu, dsilu = _silu_and_grad(gg)
        doa = do_ref[...]
        gnv = gn_ref[...]
        dgg_ref[...] = (doa * on * gnv * dsilu).astype(BF16)
        dyn = doa * silu
        dgn_ref[...] += jnp.sum(dyn * on, axis=0, keepdims=True)
        don = dyn * gnv
        dor_s[...] = r * (don - on * jnp.mean(don * on, axis=-1, keepdims=True))
        tidx = lax.broadcasted_iota(jnp.int32, (HG_HALF, HG_DIM), 0)

        def blk(ii, carry):
            i = nblk - 1 - ii
            r0 = pl.multiple_of(i * HG_BLOCK, HG_BLOCK)
            rows = pl.ds(r0, HG_BLOCK)
            st = st_ref[i]
            dst = dstate[...]
            dstb = dst.astype(BF16)
            do = dor_s[rows, :]
            dob = do.astype(BF16)
            v = iv_ref[rows, :]
            vb = v.astype(BF16)
            qq = q_ref[rows, :]
            kk = k_s[rows, :]
            bb = b_s[rows, :]
            qt = qt_s[rows, :]
            kt = kt_s[rows, :]
            dec = dec_s[pl.ds(r0, 1), :]
            dkt = _dot(vb, dstb)
            dq = _dot(dob, st.astype(BF16)) * eb_s[rows, :]
            dk = dkt * ekb_s[rows, :]
            dv = _dot_nt(kt, dstb)
            gend = jnp.sum(kk * dk, axis=0, keepdims=True) + dec * jnp.sum(dst * st, axis=0, keepdims=True)
            qh, bh, doh = _halves(qq), _halves(bb), _halves(do)
            dqh, dkh, dvh = list(_halves(dq)), list(_halves(dk)), list(_halves(dv))
            for s in range(HG_BLOCK):
                ks, vs = kk[s:s + 1, :], v[s:s + 1, :]
                dk_part = dv_part = None
                for h in _causal_halves(s):
                    e = _decay_from(bh[h], bb[s:s + 1, :], s, h, tidx)
                    ke = ks * e
                    acol = jnp.sum(qh[h] * ke, axis=1, keepdims=True)
                    dacol = jnp.sum(doh[h] * vs, axis=1, keepdims=True)
                    dqh[h] = dqh[h] + dacol * ke
                    pk = dacol * (qh[h] * e)
                    pv = acol * doh[h]
                    dk_part = pk if dk_part is None else dk_part + pk
                    dv_part = pv if dv_part is None else dv_part + pv
                hs, row = divmod(s, HG_HALF)
                dkh[hs] = dkh[hs] + jnp.where(tidx == row, jnp.sum(dk_part, axis=0, keepdims=True), 0.0)
                dvh[hs] = dvh[hs] + jnp.where(tidx == row, jnp.sum(dv_part, axis=0, keepdims=True), 0.0)
            dq = jnp.concatenate(dqh, axis=0)
            dk = jnp.concatenate(dkh, axis=0)
            dv = jnp.concatenate(dvh, axis=0)
            dq_ref[rows, :] = dq.astype(BF16)
            div_ref[rows, :] = dv.astype(BF16)
            dk_s[rows, :] = dk
            dbl_s[rows, :] = qq * dq - kk * dk
            gr_s[rows, :] = jnp.zeros((HG_BLOCK, HG_DIM), F32) + gend
            dstate[...] = dst * dec + _dot_tn(dob, qt)
            return carry

        lax.fori_loop(0, nblk, blk, 0, unroll=2 * HG_UNROLL)
        dlogf = _mask_dot(mask_ref[1], dbl_s[...]) + gr_s[...]
        dk = dk_s[...]
        dfz_ref[...] = ((dlogf / f - dk) * ((1.0 - lb) * sig * nsig)).astype(BF16)
        dlb_acc[...] += jnp.sum((dlogf / f - dk) * nsig, axis=0, keepdims=True)

        @pl.when(c == nct - 1)
        def _():
            dl0 = dlb_acc[...] * lb * (1.0 - lb)
            layer = lax.broadcasted_iota(jnp.int32, (2, HG_DIM), 0)
            dlg_ref[...] = jnp.where(layer == 0, dl0, -dl0)

    def slab(off):
        return pl.BlockSpec((ct, HG_DIM), lambda h, c: (nct - 1 - c, off + h))

    out_slab = pl.BlockSpec((ct, HG_DIM), lambda h, c: (nct - 1 - c, h))
    tile_f32 = pltpu.VMEM((ct, HG_DIM), F32)
    tile_b16 = pltpu.VMEM((ct, HG_DIM), BF16)
    slab_shape = jax.ShapeDtypeStruct((t, nh * HG_DIM), BF16)
    return pl.pallas_call(
        body,
        grid=(nh, nct),
        in_specs=[slab(0), slab(nh), slab(2 * nh), slab(3 * nh), slab(0), slab(0),
                  pl.BlockSpec((None, nblk, HG_DIM, HG_DIM), lambda h, c: (h, nct - 1 - c, 0, 0)),
                  pl.BlockSpec((None, 2, HG_DIM), lambda h, c: (h, 0, 0)),
                  pl.BlockSpec((1, HG_DIM), lambda h, c: (0, 0)),
                  pl.BlockSpec((3, mrows, mrows), lambda h, c: (0, 0, 0))],
        out_specs=[out_slab, out_slab, out_slab, out_slab,
                   pl.BlockSpec((None, 2, HG_DIM), lambda h, c: (h, 0, 0)),
                   pl.BlockSpec((None, 1, HG_DIM), lambda h, c: (h, 0, 0))],
        out_shape=[slab_shape, slab_shape, slab_shape, slab_shape,
                   jax.ShapeDtypeStruct((nh, 2, HG_DIM), F32), jax.ShapeDtypeStruct((nh, 1, HG_DIM), F32)],
        scratch_shapes=[pltpu.VMEM((HG_DIM, HG_DIM), F32), tile_b16, tile_b16, tile_f32, tile_f32, tile_f32, tile_f32,
                        tile_f32, tile_f32, tile_f32, tile_f32, tile_f32, pltpu.VMEM((1, HG_DIM), F32)],
        compiler_params=_params(2),
        name="hgrn_bwd",
    )(proj, proj, proj, proj, oraw, dmix, states, logits, gn, _block_masks(mrows))


def _sgu_bwd(proj, dmix, ln_g, ln_b, w_s, w_t, b_col):
    t = proj.shape[0]
    ct = _sg_tile(t)
    nct = t // ct
    ng = SG_GROUPS
    off_u = 4 * HG_HEADS
    off_v = off_u + ng
    n = SG_CHUNK

    def body(u_ref, v_ref, do_ref, g_ref, b_ref, w_ref, wt_ref, bs_ref, du_ref, dv_ref, dg_ref, db_ref, dw_ref, dbs_ref):
        c = pl.program_id(1)

        @pl.when(c == 0)
        def _():
            dg_ref[...] = jnp.zeros_like(dg_ref)
            db_ref[...] = jnp.zeros_like(db_ref)
            dw_ref[...] = jnp.zeros_like(dw_ref)
            dbs_ref[...] = jnp.zeros_like(dbs_ref)

        r = lax.broadcasted_iota(jnp.int32, (n, n), 0)
        cc = lax.broadcasted_iota(jnp.int32, (n, n), 1)
        wm = jnp.where(cc <= r, w_ref[...], 0.0).astype(BF16)
        wmt = jnp.where(r <= cc, wt_ref[...], 0.0).astype(BF16)
        for ci in range(ct // n):
            rows = slice(ci * n, (ci + 1) * n)
            ua, dua, dva, vn, xhat, rstd, s = _sgu_chunk_fwd(u_ref[rows, :], v_ref[rows, :], g_ref[...], b_ref[...],
                                                             wm, bs_ref[...])
            do = do_ref[rows, :]
            du_ref[rows, :] = (do * s * dua).astype(BF16)
            ds = do * ua
            dsb = ds.astype(BF16)
            dbs_ref[...] += jnp.sum(ds, axis=1, keepdims=True)
            dw_ref[...] += _dot_nt(dsb, vn.astype(BF16))
            dvn = _dot(wmt, dsb)
            dva_in, dg, db = _ln_bwd(dvn, xhat, rstd, g_ref[...])
            dg_ref[...] += dg
            db_ref[...] += db
            dv_ref[rows, :] = (dva_in * dva).astype(BF16)

        @pl.when(c == nct - 1)
        def _():
            dw_ref[...] = jnp.where(cc <= r, dw_ref[...], 0.0)

    vec = pl.BlockSpec((None, 1, SG_DIM), lambda g, c: (g, 0, 0))
    mat = pl.BlockSpec((None, n, n), lambda g, c: (g, 0, 0))
    col = pl.BlockSpec((None, n, 1), lambda g, c: (g, 0, 0))
    out_slab = pl.BlockSpec((ct, SG_DIM), lambda g, c: (c, g))
    return pl.pallas_call(
        body,
        grid=(ng, nct),
        in_specs=[pl.BlockSpec((ct, SG_DIM), lambda g, c: (c, off_u + g)),
                  pl.BlockSpec((ct, SG_DIM), lambda g, c: (c, off_v + g)),
                  pl.BlockSpec((ct, SG_DIM), lambda g, c: (c, ng + g)), vec, vec, mat, mat, col],
        out_specs=[out_slab, out_slab, vec, vec, mat, col],
        out_shape=[jax.ShapeDtypeStruct((t, ng * SG_DIM), BF16), jax.ShapeDtypeStruct((t, ng * SG_DIM), BF16),
                   jax.ShapeDtypeStruct((ng, 1, SG_DIM), F32), jax.ShapeDtypeStruct((ng, 1, SG_DIM), F32),
                   jax.ShapeDtypeStruct((ng, n, n), F32), jax.ShapeDtypeStruct((ng, n, 1), F32)],
        compiler_params=_params(2),
        name="sgu_bwd",
    )(proj, proj, dmix, ln_g, ln_b, w_s, w_t, b_col)


def _attn_bwd(dyb, wo, qb, kb, vb):
    t, d = dyb.shape
    m_len = kb.shape[0]
    tm = _row_tile(t)
    dh = d // X_HEADS
    scale = dh ** -0.5

    def body(dy_ref, wo_ref, q_ref, k_ref, v_ref, dq_ref, dk_ref, dv_ref):
        i = pl.program_id(0)

        @pl.when(i == 0)
        def _():
            dk_ref[...] = jnp.zeros_like(dk_ref)
            dv_ref[...] = jnp.zeros_like(dv_ref)

        do = _dot_nt(dy_ref[...], wo_ref[...]).astype(BF16)
        for hd in range(X_HEADS):
            sl = slice(hd * dh, (hd + 1) * dh)
            qh = q_ref[:, sl]
            p = _softmax_rows(_dot_nt(qh, k_ref[:, sl]) * scale)
            doh = do[:, sl]
            dp = _dot_nt(doh, v_ref[:, sl])
            ds = (p * (dp - jnp.sum(dp * p, axis=-1, keepdims=True)) * scale).astype(BF16)
            dq_ref[:, sl] = _dot(ds, k_ref[:, sl]).astype(BF16)
            dk_ref[:, sl] += _dot_tn(ds, qh)
            dv_ref[:, sl] += _dot_tn(p.astype(BF16), doh)

    row = pl.BlockSpec((tm, d), lambda i: (i, 0))
    full = lambda a: pl.BlockSpec(a.shape, lambda i: (0, 0))
    kv = pl.BlockSpec((m_len, d), lambda i: (0, 0))
    return pl.pallas_call(
        body,
        grid=(t // tm,),
        in_specs=[row, full(wo), row, full(kb), full(vb)],
        out_specs=[row, kv, kv],
        out_shape=[jax.ShapeDtypeStruct((t, d), BF16), jax.ShapeDtypeStruct((m_len, d), F32),
                   jax.ShapeDtypeStruct((m_len, d), F32)],
        compiler_params=_params(1),
        name="attn_bwd",
    )(dyb, wo, qb, kb, vb)


def _mem_bwd(dk, dv, mb, xhat, rstd, g, wk, wv):
    m_len, d = dk.shape

    def body(dk_ref, dv_ref, mb_ref, xh_ref, rs_ref, g_ref, wk_ref, wv_ref, gwk_ref, gwv_ref, dg_ref, db_ref):
        dkb = dk_ref[...].astype(BF16)
        dvb = dv_ref[...].astype(BF16)
        mb_v = mb_ref[...]
        gwk_ref[...] = _dot_tn(mb_v, dkb).astype(BF16)
        gwv_ref[...] = _dot_tn(mb_v, dvb).astype(BF16)
        dm = _dot_nt(dkb, wk_ref[...]) + _dot_nt(dvb, wv_ref[...])
        _, dg, db = _ln_bwd(dm, xh_ref[...], rs_ref[...], g_ref[...])
        dg_ref[...] = dg
        db_ref[...] = db

    return pl.pallas_call(
        body,
        out_shape=[jax.ShapeDtypeStruct((d, d), BF16), jax.ShapeDtypeStruct((d, d), BF16),
                   jax.ShapeDtypeStruct((1, d), F32), jax.ShapeDtypeStruct((1, d), F32)],
        compiler_params=pltpu.CompilerParams(vmem_limit_bytes=VMEM_LIMIT_V7X),
        name="mem_bwd",
    )(dk, dv, mb, xhat, rstd, g, wk, wv)


def _adamw(w, g, m, v):
    m = ADAM_B1 * m + (1.0 - ADAM_B1) * g
    v = ADAM_B2 * v + (1.0 - ADAM_B2) * (g * g)
    m_hat = m / (1.0 - ADAM_B1 ** ADAM_STEP)
    v_hat = v / (1.0 - ADAM_B2 ** ADAM_STEP)
    delta = -ADAM_LR * (m_hat / (jnp.sqrt(v_hat) + ADAM_EPS) + ADAM_WD * w)
    return delta, m, v


def _slot_sum(ref):
    g = ref[0].astype(F32)
    for s in range(1, N_DEV):
        g = g + ref[s].astype(F32)
    return g


def _adam_sharded(lands, w, m, v, axis, name):
    rows, cols = w.shape
    nl = len(lands)
    transposed = axis == 1 and nl == 2
    if transposed:
        rows, cols = cols, rows
        tr = 256
        grid = (rows // tr,)
        wblk = pl.BlockSpec((cols, tr), lambda i: (0, i))
        lblk = [pl.BlockSpec((N_DEV, tr, a.shape[2]), lambda i: (0, i, 0)) for a in lands]
    elif axis == 1:
        tr = 256 if rows % 256 == 0 else rows
        grid = (rows // tr,)
        wblk = pl.BlockSpec((tr, cols), lambda i: (i, 0))
        lblk = [pl.BlockSpec((N_DEV, tr, a.shape[2]), lambda i: (0, i, 0)) for a in lands]
    else:
        tc = _col_tile(cols)
        grid = (cols // tc,)
        wblk = pl.BlockSpec((rows, tc), lambda i: (0, i))
        lblk = [pl.BlockSpec((N_DEV, a.shape[1], tc), lambda i: (0, 0, i)) for a in lands]

    def body(*refs):
        w_ref, m_ref, v_ref = refs[nl:nl + 3]
        g_ref, d_ref, nm_ref, nv_ref = refs[nl + 3:]
        g = _slot_sum(refs[0])
        if nl == 2:
            tail = _slot_sum(refs[1])
            if transposed:
                g = jnp.concatenate([g.T, tail.T[:cols - g.shape[1], :]], axis=0)
            elif axis == 1:
                g = jnp.concatenate([g, tail[:, :cols - g.shape[1]]], axis=1)
            else:
                g = jnp.concatenate([g, tail[:rows - g.shape[0], :]], axis=0)
        delta, nm, nv = _adamw(w_ref[...], g, m_ref[...], v_ref[...])
        g_ref[...] = g
        d_ref[...] = delta
        nm_ref[...] = nm
        nv_ref[...] = nv

    shp = jax.ShapeDtypeStruct(w.shape, F32)
    return pl.pallas_call(
        body,
        grid=grid,
        in_specs=lblk + [wblk, wblk, wblk],
        out_specs=[wblk, wblk, wblk, wblk],
        out_shape=[shp, shp, shp, shp],
        compiler_params=_params(1),
        name=name,
    )(*lands, w, m, v)


def _mesh_pos():
    return lax.axis_index("x"), lax.axis_index("y"), lax.axis_index("c")


def _peer(k):
    x, y, c = _mesh_pos()
    pos = (x ^ (k >> 2), y ^ ((k >> 1) & 1), c ^ (k & 1))
    return pos, 4 * pos[0] + 2 * pos[1] + pos[2]


def _sem_index(row, k):
    return row * (N_DEV - 1) + k - 1


def _window(ref, axis, start, size):
    align = 16 if axis == 0 else LANES
    start = pl.multiple_of(start, align)
    return ref.at[pl.ds(start, size), :] if axis == 0 else ref.at[:, pl.ds(start, size)]


def _piece_refs(piece, srcs, lands, me, peer):
    kind, si, li, axis, base, stride, shape = piece
    if kind == "gather":
        return srcs[si], _window(lands[li], axis, base + stride * me, shape[axis])
    return _window(srcs[si], axis, base + stride * peer, shape[axis]), lands[li].at[me]


def _place_own(srcs, land_shapes, pieces, name):
    ns, nl, npc = len(srcs), len(land_shapes), len(pieces)

    def body(*refs):
        s_refs = refs[:ns]
        l_refs = refs[ns:ns + nl]
        bufs = refs[ns + nl:ns + nl + npc]
        sems = refs[ns + nl + npc]
        x, y, c = _mesh_pos()
        me = 4 * x + 2 * y + c
        loads = []
        for p, piece in enumerate(pieces):
            src, dst = _piece_refs(piece, s_refs, l_refs, me, me)
            cp = pltpu.make_async_copy(src, bufs[p], sems.at[0, p])
            cp.start()
            loads.append((cp, dst))
        stores = []
        for p, (cp, dst) in enumerate(loads):
            cp.wait()
            out = pltpu.make_async_copy(bufs[p], dst, sems.at[1, p])
            out.start()
            stores.append(out)
        for out in stores:
            out.wait()

    out = pl.pallas_call(
        body,
        in_specs=[ANY] * ns,
        out_specs=[ANY] * nl,
        out_shape=list(land_shapes),
        scratch_shapes=[pltpu.VMEM(pc[6], srcs[pc[1]].dtype) for pc in pieces] + [pltpu.SemaphoreType.DMA((2, npc))],
        compiler_params=pltpu.CompilerParams(vmem_limit_bytes=VMEM_LIMIT_V7X),
        name=name,
    )(*srcs)
    return list(out)


def _comm_start(srcs, lands, pieces, groups, name, after=()):
    ns, nl, na, ng = len(srcs), len(lands), len(after), len(groups)

    def body(*refs):
        s_refs = refs[:ns]
        l_refs = refs[ns:ns + nl]
        outs = refs[ns + nl + na:]
        sems = outs[:2 * ng]
        token = outs[-1]
        x, y, c = _mesh_pos()
        me = 4 * x + 2 * y + c
        for g, members in enumerate(groups):
            for row, p in enumerate(members):
                for k in range(1, N_DEV):
                    pos, peer = _peer(k)
                    src, dst = _piece_refs(pieces[p], s_refs, l_refs, me, peer)
                    pltpu.make_async_remote_copy(src_ref=src, dst_ref=dst, send_sem=sems[2 * g].at[_sem_index(row, k)],
                                                 recv_sem=sems[2 * g + 1].at[_sem_index(row, k)], device_id=pos,
                                                 device_id_type=MESH_ID).start()
        token[...] = jnp.zeros_like(token)

    sem_shapes = []
    for members in groups:
        sem_shapes += [pltpu.SemaphoreType.DMA((len(members) * (N_DEV - 1),))] * 2
    hbm_of = lambda a: pltpu.HBM(a.shape, a.dtype)
    out = pl.pallas_call(
        body,
        in_specs=[HBM] * (ns + nl) + [ANY] * na,
        out_specs=[SEM] * (2 * ng) + [HBM] * (ns + nl) + [pl.BlockSpec(memory_space=pltpu.VMEM)],
        out_shape=sem_shapes + [hbm_of(a) for a in srcs] + [hbm_of(a) for a in lands]
        + [jax.ShapeDtypeStruct((8, LANES), F32)],
        input_output_aliases={i: 2 * ng + i for i in range(ns + nl)},
        compiler_params=pltpu.CompilerParams(has_side_effects=DATAFLOW),
        name=name,
    )(*[pltpu.with_memory_space_constraint(a, pltpu.HBM) for a in list(srcs) + list(lands)], *after)
    sems = [(out[2 * g], out[2 * g + 1]) for g in range(ng)]
    return sems, list(out[2 * ng:2 * ng + ns]), list(out[2 * ng + ns:2 * ng + ns + nl]), out[-1]


def _comm_wait(srcs, lands, pieces, members, sems, after, name):
    ns, nl, na = len(srcs), len(lands), len(after)

    def body(*refs):
        s_refs = refs[:ns]
        l_refs = refs[ns:ns + nl]
        send_sems, recv_sems = refs[ns + nl:ns + nl + 2]
        x, y, c = _mesh_pos()
        me = 4 * x + 2 * y + c
        for row, p in enumerate(members):
            for k in range(1, N_DEV):
                pos, peer = _peer(k)
                src, dst = _piece_refs(pieces[p], s_refs, l_refs, me, peer)
                cp = pltpu.make_async_remote_copy(src_ref=src, dst_ref=dst, send_sem=send_sems.at[_sem_index(row, k)],
                                                  recv_sem=recv_sems.at[_sem_index(row, k)], device_id=pos,
                                                  device_id_type=MESH_ID)
                cp.wait_send()
                cp.wait_recv()

    hbm_of = lambda a: pltpu.HBM(a.shape, a.dtype)
    out = pl.pallas_call(
        body,
        in_specs=[HBM] * (ns + nl) + [SEM, SEM] + [ANY] * na,
        out_specs=[HBM] * (ns + nl),
        out_shape=[hbm_of(a) for a in srcs] + [hbm_of(a) for a in lands],
        input_output_aliases={i: i for i in range(ns + nl)},
        compiler_params=pltpu.CompilerParams(has_side_effects=DATAFLOW),
        name=name,
    )(*srcs, *lands, sems[0], sems[1], *after)
    return list(out[ns:])


def _landed_block(piece, lands, owner):
    _, _, li, axis, base, stride, shape = piece
    return _window(lands[li], axis, base + stride * owner, shape[axis])


def _copy_stage(name, bufs, in_sems, out_sem_sizes, emit, after=()):
    nb, ni, no, na = len(bufs), len(in_sems), len(out_sem_sizes), len(after)

    def body(*refs):
        b_refs = refs[:nb]
        i_refs = refs[nb:nb + ni]
        o_refs = refs[nb + ni + na:nb + ni + na + no]
        emit(b_refs, i_refs, o_refs)
        refs[-1][...] = jnp.zeros_like(refs[-1])

    hbm_of = lambda a: pltpu.HBM(a.shape, a.dtype)
    out = pl.pallas_call(
        body,
        in_specs=[HBM] * nb + [SEM] * ni + [ANY] * na,
        out_specs=[SEM] * no + [HBM] * nb + [pl.BlockSpec(memory_space=pltpu.VMEM)],
        out_shape=[pltpu.SemaphoreType.DMA((n,)) for n in out_sem_sizes] + [hbm_of(a) for a in bufs]
        + [jax.ShapeDtypeStruct((8, LANES), F32)],
        input_output_aliases={i: no + i for i in range(nb)},
        compiler_params=pltpu.CompilerParams(has_side_effects=DATAFLOW),
        name=name,
    )(*[pltpu.with_memory_space_constraint(a, pltpu.HBM) for a in bufs], *in_sems, *after)
    return list(out[:no]), list(out[no:no + nb]), out[-1]


def _remote(src, dst, send, recv, to):
    return pltpu.make_async_remote_copy(src_ref=src, dst_ref=dst, send_sem=send, recv_sem=recv, device_id=to,
                                        device_id_type=MESH_ID)


def _routed_gather(srcs, lands, pieces, after, name):
    ns, npc = len(srcs), len(pieces)

    def places():
        x, y, c = _mesh_pos()
        index = lambda p: 4 * p[0] + 2 * p[1] + p[2]
        me, sib = (x, y, c), (x, y, 1 - c)
        xnb, ynb = (1 - x, y, c), (x, 1 - y, c)
        got_first = (x ^ (1 - c), y ^ c, c)
        pass_to = (x ^ c, y ^ (1 - c), c)
        diag = (1 - x, 1 - y, c)
        return index, me, sib, xnb, ynb, got_first, pass_to, diag

    def start(b, _, o):
        index, me, sib, xnb, ynb, *_rest = places()
        send_a, recv_sib, recv_nb = o
        for p, piece in enumerate(pieces):
            src, dst = _piece_refs(piece, b[:ns], b[ns:], index(me), 0)
            _remote(src, dst, send_a.at[3 * p], recv_sib.at[p], sib).start()
            _remote(src, dst, send_a.at[3 * p + 1], recv_nb.at[2 * p], xnb).start()
            _remote(src, dst, send_a.at[3 * p + 2], recv_nb.at[2 * p + 1], ynb).start()

    def pass_a(b, i, o):
        index, me, sib, xnb, ynb, got_first, pass_to, _diag = places()
        (recv_nb,) = i
        send_f, recv_f, send_d, recv_d = o
        for p, piece in enumerate(pieces):
            for j, nb in enumerate((xnb, ynb)):
                blk = _landed_block(piece, b, index(nb))
                _remote(blk, blk, send_f.at[2 * p + j], recv_nb.at[2 * p + j], sib).wait_recv()
                _remote(blk, blk, send_f.at[2 * p + j], recv_f.at[2 * p + j], sib).start()
            blk = _landed_block(piece, b, index(got_first))
            _remote(blk, blk, send_d.at[p], recv_d.at[p], pass_to).start()

    def pass_b(b, i, o):
        index, me, sib, *_mid, diag = places()
        (recv_d,) = i
        send_g, recv_g = o
        for p, piece in enumerate(pieces):
            blk = _landed_block(piece, b, index(diag))
            _remote(blk, blk, send_g.at[p], recv_d.at[p], sib).wait_recv()
            _remote(blk, blk, send_g.at[p], recv_g.at[p], sib).start()

    def last(b, i, _):
        index, me, sib, *_others = places()
        send_a, recv_sib, send_f, recv_f, send_d, send_g, recv_g = i
        for p, piece in enumerate(pieces):
            src, dst = _piece_refs(piece, b[:ns], b[ns:], index(me), 0)
            cp = lambda s_sem, r_sem: _remote(src, dst, s_sem, r_sem, sib)
            cp(send_a.at[3 * p], recv_sib.at[p]).wait_recv()
            cp(send_a.at[3 * p], recv_g.at[p]).wait_recv()
            for j in range(3):
                cp(send_a.at[3 * p + j], recv_sib.at[p]).wait_send()
            for j in range(2):
                cp(send_f.at[2 * p + j], recv_f.at[2 * p + j]).wait_recv()
                cp(send_f.at[2 * p + j], recv_f.at[2 * p + j]).wait_send()
            cp(send_d.at[p], recv_sib.at[p]).wait_send()
            cp(send_g.at[p], recv_sib.at[p]).wait_send()

    (send_a, recv_sib, recv_nb), bufs, _ = _copy_stage(name + "_start", list(srcs) + list(lands), [],
                                                       [3 * npc, npc, 2 * npc], start)
    srcs, lands = bufs[:ns], bufs[ns:]
    (send_f, recv_f, send_d, recv_d), lands, _ = _copy_stage(name + "_pass_a", lands, [recv_nb],
                                                             [2 * npc, 2 * npc, npc, npc],
                                                             lambda b, i, o: pass_a(b, i, o), after=after)
    (send_g, recv_g), lands, tok = _copy_stage(name + "_pass_b", lands, [recv_d], [npc, npc], pass_b)
    _, bufs, _ = _copy_stage(name + "_last", list(srcs) + list(lands),
                             [send_a, recv_sib, send_f, recv_f, send_d, send_g, recv_g], [], last)
    return bufs[ns:], tok


_SMALL_NAMES = ("ln1_g", "ln1_b", "hg_lb_logits", "hg_norm_g", "sg_ln_g", "sg_ln_b", "sg_w_s", "sg_b_s",
                "ln2_g", "ln2_b", "mem_ln_g", "mem_ln_b", "ln3_g", "ln3_b", "ln4_g", "ln4_b")


_VEC_NAMES = ("ln1_g", "ln1_b", "ln2_g", "ln2_b", "mem_ln_g", "mem_ln_b", "ln3_g", "ln3_b", "ln4_g", "ln4_b")
_ROW_NAMES = ("hg_lb_logits", "hg_norm_g", "sg_ln_g", "sg_ln_b", "sg_b_s", "sg_w_s")
VEC_ROWS = 16


def _row_plan(shapes):
    plan, pos = {}, 0
    for k in _ROW_NAMES:
        shp = shapes[k]
        slabs, off = [], pos
        for idx in itertools.product(*[range(dim) for dim in shp[:-2]]):
            slabs.append((idx, off, shp[-2]))
            off += shp[-2]
        plan[k] = (pos, slabs)
        pos = -(-off // 8) * 8
    return plan, -(-pos // 16) * 16


def _pack_small_grads(gs, shapes, loss):
    d = gs[_VEC_NAMES[0]].size
    vec = jnp.concatenate([gs[k].reshape(1, -1) for k in _VEC_NAMES] + [jnp.tile(loss, (1, d // LANES))], axis=0)
    vec = jnp.pad(vec, ((0, VEC_ROWS - vec.shape[0]), (0, 0)))
    plan, total = _row_plan(shapes)
    parts, pos = [], 0
    for k in _ROW_NAMES:
        first, slabs = plan[k]
        rows = gs[k].reshape(-1, LANES)
        end = slabs[-1][1] + slabs[-1][2]
        nxt = -(-end // 8) * 8
        parts.append(jnp.pad(rows, ((0, nxt - first - rows.shape[0]), (0, 0))))
        pos = nxt
    parts.append(jnp.zeros((total - pos, LANES), F32))
    return vec, jnp.concatenate(parts, axis=0)


def _adam_small(land_vec, land_rows, w, m, v):
    names = _VEC_NAMES + _ROW_NAMES
    n = len(names)
    shapes = {k: w[k].shape for k in names}
    plan, _ = _row_plan(shapes)

    def body(*refs):
        lv_ref, lr_ref = refs[:2]
        w_refs, m_refs, v_refs = refs[2:2 + n], refs[2 + n:2 + 2 * n], refs[2 + 2 * n:2 + 3 * n]
        outs = refs[2 + 3 * n:2 + 7 * n]
        loss_ref = refs[2 + 7 * n]
        gv_s, gr_s = refs[3 + 7 * n:]
        gv_s[...] = _slot_sum(lv_ref)
        gr_s[...] = _slot_sum(lr_ref)
        loss_ref[...] = gv_s[len(_VEC_NAMES):len(_VEC_NAMES) + 1, :LANES]
        for p, k in enumerate(names):
            if k in _VEC_NAMES:
                row = _VEC_NAMES.index(k)
                slabs = [((), None, None)]
            else:
                slabs = plan[k][1]
            for idx, off, rows in slabs:
                g = gv_s[row:row + 1, :] if off is None else gr_s[off:off + rows, :]
                sel = idx + (slice(None), slice(None))
                delta, nm, nv = _adamw(w_refs[p][sel], g, m_refs[p][sel], v_refs[p][sel])
                for o, val in zip(range(4), (g, delta, nm, nv)):
                    outs[o * n + p][sel] = val

    flat = lambda tree: [tree[k] for k in names]
    shp = [jax.ShapeDtypeStruct(shapes[k], F32) for k in names]
    out = pl.pallas_call(
        body,
        out_shape=shp * 4 + [jax.ShapeDtypeStruct((1, LANES), F32)],
        scratch_shapes=[pltpu.VMEM(land_vec.shape[1:], F32), pltpu.VMEM(land_rows.shape[1:], F32)],
        name="adam_small",
    )(land_vec, land_rows, *flat(w), *flat(m), *flat(v))
    return [dict(zip(names, out[o * n:(o + 1) * n])) for o in range(4)], out[4 * n]


_COL_FFN = ("ffn1_w_gate", "ffn1_w_up", "ffn2_w_gate", "ffn2_w_up")
_ROW_FFN = ("ffn1_w_down", "ffn2_w_down")
_ROW_SQ = ("w_out", "xa_w_q", "xa_w_k", "xa_w_v", "xa_w_o")
_BIG_NAMES = ("ffn1_w_gate", "ffn1_w_up", "ffn1_w_down", "w_in", "w_out", "xa_w_q", "xa_w_k", "xa_w_v", "xa_w_o",
              "ffn2_w_gate", "ffn2_w_up", "ffn2_w_down")


def _ffn_split(fs):
    main = (fs // MXU_WIDTH_V7X) * MXU_WIDTH_V7X
    tail = fs - main
    tail_pad = -(-tail // LANES) * LANES
    assert main > 0 and tail > 0
    return main, tail, tail_pad


def _layout(name, shard_shape):
    r, c = shard_shape
    if name in _COL_FFN:
        main, tail, pad = _ffn_split(c)
        return (r, N_DEV * (main + pad)), [(1, 0, main, (r, main), (0, main)),
                                           (1, N_DEV * main, pad, (r, pad), (main, c))]
    if name in _ROW_FFN:
        main, tail, pad = _ffn_split(r)
        return (N_DEV * (main + pad), c), [(0, 0, main, (main, c), (0, main)),
                                           (0, N_DEV * main, pad, (pad, c), (main, r))]
    if name == "w_in":
        return (r, N_DEV * c), [(1, 0, c, (r, c), (0, c))]
    return (N_DEV * r, c), [(0, 0, r, (r, c), (0, r))]


def _shard_pieces(name, shard):
    out = []
    for axis, _, _, shape, (lo, hi) in _layout(name, shard.shape)[1]:
        part = shard[lo:hi, :] if axis == 0 else shard[:, lo:hi]
        pad = [(0, shape[0] - part.shape[0]), (0, shape[1] - part.shape[1])]
        out.append(jnp.pad(part, pad).astype(BF16))
    return out


def _gather_plan(names, shards):
    srcs, land_shapes, pieces, index = [], [], [], {}
    for li, name in enumerate(names):
        shape2d, parts = _layout(name, shards[name].shape)
        land_shapes.append(jax.ShapeDtypeStruct(shape2d, BF16))
        index[name] = []
        for (axis, base, stride, shape, _), src in zip(parts, _shard_pieces(name, shards[name])):
            index[name].append(len(pieces))
            pieces.append(("gather", len(srcs), li, axis, base, stride, shape))
            srcs.append(src)
    return srcs, land_shapes, pieces, index


def _scatter_plan(names, grads, shard_shapes):
    srcs, land_shapes, pieces, index = [], [], [], {}
    for si, name in enumerate(names):
        _, parts = _layout(name, shard_shapes[name])
        srcs.append(grads[name])
        index[name] = []
        for axis, base, stride, shape, _ in parts:
            index[name].append(len(land_shapes))
            pieces.append(("scatter", si, len(land_shapes), axis, base, stride, shape))
            land_shapes.append(jax.ShapeDtypeStruct((N_DEV,) + shape, grads[name].dtype))
    return srcs, land_shapes, pieces, index


def _small_views(small):
    row = lambda a: a.reshape(1, -1)
    ln = {k: row(small[k]) for k in ("ln1_g", "ln1_b", "ln2_g", "ln2_b", "ln3_g", "ln3_b", "ln4_g", "ln4_b",
                                      "mem_ln_g", "mem_ln_b", "hg_norm_g")}
    sg_w = small["sg_w_s"].reshape(SG_GROUPS, SG_CHUNK, SG_CHUNK)
    sg = dict(logits=jnp.swapaxes(small["hg_lb_logits"], 0, 1),
              g=small["sg_ln_g"].reshape(SG_GROUPS, 1, SG_DIM), b=small["sg_ln_b"].reshape(SG_GROUPS, 1, SG_DIM),
              w=sg_w, wt=jnp.swapaxes(sg_w, 1, 2), bs=small["sg_b_s"].reshape(SG_GROUPS, SG_CHUNK, 1))
    return ln, sg


def _forward(x, mem, target, get_w, small, first_deps=()):
    ln, sg = _small_views(small)
    xb = x.astype(BF16)
    a1, b1, s1 = _ffn_up(xb, get_w("ffn1_w_gate", ()), get_w("ffn1_w_up", ()), "ffn1_up", deps=first_deps)
    h1b, xh1, rs1 = _mm_res_ln(s1, get_w("ffn1_w_down", (s1,)), x, ln["ln1_g"], ln["ln1_b"], 0.5, "ffn1_down_ln")
    proj = _mm_nn(h1b, get_w("w_in", (h1b,)), "mix_in")
    oraw, mix, states = _hgrn_fwd(proj, sg["logits"], ln["hg_norm_g"])
    mix = _sgu_fwd(proj, mix, sg["g"], sg["b"], sg["w"], sg["bs"])
    h2b, xh2, rs2 = _mm_res_ln(mix, get_w("w_out", (mix,)), (xh1, ln["ln1_g"], ln["ln1_b"]), ln["ln2_g"], ln["ln2_b"],
                               1.0, "mix_out_ln")
    mb, mxh, mrs, kb, vb = _mem_kv(mem, ln["mem_ln_g"], ln["mem_ln_b"], get_w("xa_w_k", (h2b,)), get_w("xa_w_v", (h2b,)))
    qb, att = _attn_fwd(h2b, get_w("xa_w_q", (kb,)), kb, vb)
    h3b, xh3, rs3 = _mm_res_ln(att, get_w("xa_w_o", (att,)), (xh2, ln["ln2_g"], ln["ln2_b"]), ln["ln3_g"], ln["ln3_b"],
                               1.0, "attn_out_ln")
    a2, b2, s2 = _ffn_up(h3b, get_w("ffn2_w_gate", (h3b,)), get_w("ffn2_w_up", (h3b,)), "ffn2_up")
    loss, dy4, dy4b, dg4, db4 = _mm_res_ln(s2, get_w("ffn2_w_down", (s2,)), (xh3, ln["ln3_g"], ln["ln3_b"]),
                                           ln["ln4_g"], ln["ln4_b"], 0.5, "ffn2_down_ln_loss", target=target)
    return dict(xb=xb, a1=a1, b1=b1, s1=s1, h1b=h1b, xh1=xh1, rs1=rs1, proj=proj, oraw=oraw, mix=mix, states=states,
                h2b=h2b, xh2=xh2, rs2=rs2, mb=mb, mxh=mxh, mrs=mrs, kb=kb, vb=vb, qb=qb, att=att, h3b=h3b, xh3=xh3,
                rs3=rs3, a2=a2, b2=b2, s2=s2, loss=loss, dy4=dy4, dy4b=dy4b, dg4=dg4, db4=db4)


def _backward(sv, wt, small, send):
    ln, sg = _small_views(small)
    gs = {"ln4_g": sv["dg4"], "ln4_b": sv["db4"]}
    loss, dy4, dy4b = sv["loss"], sv["dy4"], sv["dy4b"]
    g_down2 = _mm_tn(sv["s2"], dy4b, "g_ffn2_down", scale=0.5)
    da2, db2, dy3, dy3b, gs["ln3_g"], gs["ln3_b"] = _ffn_bwd_fused(
        dy4, dy4b, wt["ffn2_w_down"], wt["ffn2_w_gate"], wt["ffn2_w_up"], sv["a2"], sv["b2"], 0.5,
        (sv["xh3"], sv["rs3"], ln["ln3_g"]), "ffn2_bwd")
    g_gate2 = _mm_tn(sv["h3b"], da2, "g_ffn2_gate")
    g_up2 = _mm_tn(sv["h3b"], db2, "g_ffn2_up")
    tok = send(("ffn2_w_down", "ffn2_w_gate", "ffn2_w_up"), (g_down2, g_gate2, g_up2))

    g_o = _mm_tn(sv["att"], dy3b, "g_xa_o", deps=(tok,))
    dqb, dk, dv = _attn_bwd(dy3b, wt["xa_w_o"], sv["qb"], sv["kb"], sv["vb"])
    g_q = _mm_tn(sv["h2b"], dqb, "g_xa_q")
    g_k, g_v, gs["mem_ln_g"], gs["mem_ln_b"] = _mem_bwd(dk, dv, sv["mb"], sv["mxh"], sv["mrs"], ln["mem_ln_g"],
                                                        wt["xa_w_k"], wt["xa_w_v"])
    tok = send(("xa_w_o", "xa_w_q", "xa_w_k", "xa_w_v"), (g_o, g_q, g_k, g_v))
    dy2, dy2b, gs["ln2_g"], gs["ln2_b"] = _dx_ln(dy3, [(dqb, wt["xa_w_q"])], (sv["xh2"], sv["rs2"], ln["ln2_g"]),
                                                 "attn_dx_ln", deps=(tok,))

    g_out = _mm_tn(sv["mix"], dy2b, "g_w_out")
    dmix = _mm_nt(dy2b, wt["w_out"], "mix_out_bwd")
    dq, dfz, div, dgg, dlg, dgn = _hgrn_bwd(sv["proj"], sv["oraw"], dmix, sv["states"], sg["logits"], ln["hg_norm_g"])
    du, dvv, gs["sg_ln_g"], gs["sg_ln_b"], gs["sg_w_s"], gs["sg_b_s"] = _sgu_bwd(
        sv["proj"], dmix, sg["g"], sg["b"], sg["w"], sg["wt"], sg["bs"])
    gs["hg_lb_logits"] = jnp.swapaxes(dlg, 0, 1)
    gs["hg_norm_g"] = jnp.sum(dgn, axis=0)
    dproj = jnp.concatenate([dq, dfz, div, dgg, du, dvv], axis=1)
    g_in = _mm_tn(sv["h1b"], dproj, "g_w_in")
    tok = send(("w_out", "w_in"), (g_out, g_in))
    dy1, dy1b, gs["ln1_g"], gs["ln1_b"] = _dx_ln(dy2, [(dproj, wt["w_in"])], (sv["xh1"], sv["rs1"], ln["ln1_g"]),
                                                 "mix_dx_ln", deps=(tok,))

    g_down1 = _mm_tn(sv["s1"], dy1b, "g_ffn1_down", scale=0.5)
    tok = send(("ffn1_w_down",), (g_down1,))
    da1, db1 = _ffn_bwd_act(dy1b, wt["ffn1_w_down"], sv["a1"], sv["b1"], 0.5, "ffn1_bwd_act", deps=(tok,))
    g_gate1 = _mm_tn(sv["xb"], da1, "g_ffn1_gate")
    tok = send(("ffn1_w_gate",), (g_gate1,))
    g_up1 = _mm_tn(sv["xb"], db1, "g_ffn1_up", deps=(tok,))
    tok = send(("ffn1_w_up",), (g_up1,))
    grad_x = _dx_ln(dy1, [(da1, wt["ffn1_w_gate"]), (db1, wt["ffn1_w_up"])], None, "ffn1_dx", deps=(tok,))
    return loss, grad_x, gs


_WEIGHT_NAMES = ("ffn1_w_gate", "ffn1_w_up", "ffn1_w_down", "ln1_g", "ln1_b", "w_in", "hg_lb_logits", "hg_norm_g",
                 "sg_ln_g", "sg_ln_b", "sg_w_s", "sg_b_s", "w_out", "ln2_g", "ln2_b", "mem_ln_g", "mem_ln_b",
                 "xa_w_q", "xa_w_k", "xa_w_v", "xa_w_o", "ln3_g", "ln3_b", "ffn2_w_gate", "ffn2_w_up", "ffn2_w_down",
                 "ln4_g", "ln4_b")
_FIRST = ("ffn1_w_gate", "ffn1_w_up")
_SECOND = ("ffn1_w_down", "w_in", "w_out")
_THIRD = ("xa_w_k", "xa_w_v", "xa_w_q", "xa_w_o", "ffn2_w_gate", "ffn2_w_up", "ffn2_w_down")


def kernel(x, mem, ffn1_w_gate, ffn1_w_up, ffn1_w_down, ln1_g, ln1_b, w_in, hg_lb_logits, hg_norm_g, sg_ln_g, sg_ln_b, sg_w_s, sg_b_s, w_out, ln2_g, ln2_b, mem_ln_g, mem_ln_b, xa_w_q, xa_w_k, xa_w_v, xa_w_o, ln3_g, ln3_b, ffn2_w_gate, ffn2_w_up, ffn2_w_down, ln4_g, ln4_b, loss_target, m_ffn1_w_gate, m_ffn1_w_up, m_ffn1_w_down, m_ln1_g, m_ln1_b, m_w_in, m_hg_lb_logits, m_hg_norm_g, m_sg_ln_g, m_sg_ln_b, m_sg_w_s, m_sg_b_s, m_w_out, m_ln2_g, m_ln2_b, m_mem_ln_g, m_mem_ln_b, m_xa_w_q, m_xa_w_k, m_xa_w_v, m_xa_w_o, m_ln3_g, m_ln3_b, m_ffn2_w_gate, m_ffn2_w_up, m_ffn2_w_down, m_ln4_g, m_ln4_b, v_ffn1_w_gate, v_ffn1_w_up, v_ffn1_w_down, v_ln1_g, v_ln1_b, v_w_in, v_hg_lb_logits, v_hg_norm_g, v_sg_ln_g, v_sg_ln_b, v_sg_w_s, v_sg_b_s, v_w_out, v_ln2_g, v_ln2_b, v_mem_ln_g, v_mem_ln_b, v_xa_w_q, v_xa_w_k, v_xa_w_v, v_xa_w_o, v_ln3_g, v_ln3_b, v_ffn2_w_gate, v_ffn2_w_up, v_ffn2_w_down, v_ln4_g, v_ln4_b):
    args = dict(locals())
    w = {k: args[k] for k in _WEIGHT_NAMES}
    m = {k: args["m_" + k] for k in _WEIGHT_NAMES}
    v = {k: args["v_" + k] for k in _WEIGHT_NAMES}
    shards = {k: w[k][0] for k in _BIG_NAMES}
    shard_shapes = {k: shards[k].shape for k in _BIG_NAMES}
    small = {k: (w[k][0] if k != "hg_lb_logits" else w[k]) for k in _SMALL_NAMES}

    srcs1, shapes1, pieces1, idx1 = _gather_plan(_FIRST, shards)
    lands1 = _place_own(srcs1, shapes1, pieces1, "gather_first_own")
    rest = _SECOND + _THIRD
    srcs2, shapes2, pieces2, idx2 = _gather_plan(rest, shards)
    lands2 = _place_own(srcs2, shapes2, pieces2, "gather_rest_own")
    groups2 = [list(idx2[k]) for k in rest]
    lands1, tok1 = _routed_gather(srcs1, lands1, pieces1, tuple(lands2), "gather_first")
    sems2, srcs2, lands2, tok2 = _comm_start(srcs2, lands2, pieces2, groups2, "gather_rest_start", after=(tok1,))
    wt = dict(zip(_FIRST, lands1))
    pending = {k: gi for gi, k in enumerate(rest)}

    def get_w(name, after):
        if name in pending:
            gi = pending.pop(name)
            si = [pieces2[p][1] for p in groups2[gi]]
            sub = [(pieces2[p][0], row, 0) + pieces2[p][3:] for row, p in enumerate(groups2[gi])]
            wt[name] = _comm_wait([srcs2[s] for s in si], [lands2[gi]], sub, list(range(len(sub))), sems2[gi],
                                  after, "gather_wait_" + name)[0]
        return wt[name]

    sv = _forward(x[0], mem[0], loss_target[0], get_w, small, first_deps=(tok2,))

    sent = []

    def send(names, grads):
        srcs, shapes, pieces, idx = _scatter_plan(names, dict(zip(names, grads)), shard_shapes)
        lands = _place_own(srcs, shapes, pieces, "grads_own_%d" % len(sent))
        sems, srcs, lands, tok = _comm_start(srcs, lands, pieces, [list(range(len(pieces)))],
                                             "grads_start_%d" % len(sent))
        sent.append((names, srcs, lands, pieces, idx, sems[0]))
        return tok

    loss, grad_x, gs = _backward(sv, wt, small, send)

    ssrc = list(_pack_small_grads(gs, {k: w[k].shape for k in _SMALL_NAMES}, loss))
    sp = [("scatter", i, i, 0, 0, 0, a.shape) for i, a in enumerate(ssrc)]
    sshape = [jax.ShapeDtypeStruct((N_DEV,) + a.shape, F32) for a in ssrc]
    sl = _place_own(ssrc, sshape, sp, "small_own")
    ssem, ssrc, sl, _ = _comm_start(ssrc, sl, sp, [[0, 1]], "small_start")

    out_g, out_d, out_m, out_v = {}, {}, {}, {}
    after = (grad_x,)
    for n_sent, (names, srcs, lands, pieces, idx, sems) in enumerate(sent):
        lands = _comm_wait(srcs, lands, pieces, list(range(len(pieces))), sems, after, "grads_wait_%d" % n_sent)
        for k in names:
            axis = 1 if (k in _COL_FFN or k == "w_in") else 0
            if k in _COL_FFN:
                res = _adam_sharded([lands[i] for i in idx[k]], w[k][0].T, m[k][0].T, v[k][0].T, axis, "adam_" + k)
                res = [r.T for r in res]
            else:
                res = _adam_sharded([lands[i] for i in idx[k]], w[k][0], m[k][0], v[k][0], axis, "adam_" + k)
            out_g[k], out_d[k], out_m[k], out_v[k] = [r[None] for r in res]
        after = (out_v[names[-1]],)
    sl = _comm_wait(ssrc, sl, sp, [0, 1], ssem[0], after, "small_wait")
    small_out, loss_sum = _adam_small(sl[0], sl[1], w, m, v)
    for dst, res in zip((out_g, out_d, out_m, out_v), small_out):
        dst.update(res)
    loss_all = loss_sum[0, 0]
    return (loss_all, grad_x[None], *[out_g[k] for k in _WEIGHT_NAMES], *[out_d[k] for k in _WEIGHT_NAMES],
            *[out_m[k] for k in _WEIGHT_NAMES], *[out_v[k] for k in _WEIGHT_NAMES])
```

```python
import itertools

import jax
import jax.numpy as jnp
import numpy as np
from jax import lax
from jax.experimental import pallas as pl
from jax.experimental.pallas import tpu as pltpu

F32 = jnp.float32
BF16 = jnp.bfloat16

N_DEV = 8
ALPHA = 2.0 ** 0.25
LN_EPS = 1e-5
HG_HEADS = 4
HG_DIM = 128
SG_GROUPS = 4
SG_DIM = 128
SG_CHUNK = 128
X_HEADS = 4
HG_BLOCK = 16
HG_UNROLL = 8
ADAM_LR = 0.001
ADAM_B1 = 0.9
ADAM_B2 = 0.999
ADAM_EPS = 1e-08
ADAM_WD = 0.01
ADAM_STEP = 10
VMEM_LIMIT_V7X = 48 * 1024 * 1024
MXU_WIDTH_V7X = 256
LANES = 128
MESH_ID = pl.DeviceIdType.MESH
ANY = pl.BlockSpec(memory_space=pl.ANY)
HBM = pl.BlockSpec(memory_space=pltpu.HBM)
SEM = pl.BlockSpec(memory_space=pltpu.SEMAPHORE)
DATAFLOW = pltpu.SideEffectType.DATAFLOW_SIDE_EFFECTING


def _params(n_axes):
    return pltpu.CompilerParams(dimension_semantics=("arbitrary",) * n_axes, vmem_limit_bytes=VMEM_LIMIT_V7X)


def _dot(a, b):
    return jnp.dot(a, b, preferred_element_type=F32)


def _dot_nt(a, b):
    return lax.dot_general(a, b, (((1,), (1,)), ((), ())), preferred_element_type=F32)


def _dot_tn(a, b):
    return lax.dot_general(a, b, (((0,), (0,)), ((), ())), preferred_element_type=F32)


def _sigmoid(x):
    return 1.0 / (1.0 + jnp.exp(-x))


def _silu_and_grad(a):
    sig = _sigmoid(a)
    return a * sig, sig * (1.0 + a * (1.0 - sig))


_GELU_C = 0.7978845608028654


def _gelu_and_grad(x):
    inner = _GELU_C * (x + 0.044715 * x * x * x)
    t = jnp.tanh(inner)
    val = 0.5 * x * (1.0 + t)
    grad = 0.5 * (1.0 + t) + 0.5 * x * (1.0 - t * t) * _GELU_C * (1.0 + 3.0 * 0.044715 * x * x)
    return val, grad


def _ln_fwd(y, g, b):
    mu = jnp.mean(y, axis=-1, keepdims=True)
    yc = y - mu
    var = jnp.mean(yc * yc, axis=-1, keepdims=True)
    rstd = lax.rsqrt(var + LN_EPS)
    xhat = yc * rstd
    return xhat * g + b, xhat, rstd


def _ln_bwd(dh, xhat, rstd, g):
    dxh = dh * g
    m1 = jnp.mean(dxh, axis=-1, keepdims=True)
    m2 = jnp.mean(dxh * xhat, axis=-1, keepdims=True)
    dy = rstd * (dxh - m1 - xhat * m2)
    dg = jnp.sum(dh * xhat, axis=0, keepdims=True)
    db = jnp.sum(dh, axis=0, keepdims=True)
    return dy, dg, db


def _mask_dot(mask, x):
    hi = x.astype(BF16)
    lo = (x - hi.astype(F32)).astype(BF16)
    n = mask.shape[0]
    parts = [_dot(mask, hi[r:r + n, :]) + _dot(mask, lo[r:r + n, :]) for r in range(0, x.shape[0], n)]
    return parts[0] if len(parts) == 1 else jnp.concatenate(parts, axis=0)


def _block_masks(n):
    r = np.arange(n)[:, None]
    c = np.arange(n)[None, :]
    same = (r // HG_BLOCK) == (c // HG_BLOCK)
    return jnp.asarray(np.stack([same & (c <= r), same & (c >= r), same]), BF16)


def _row_tile(t):
    return min(t, 512)


def _col_tile(n):
    for cand in (512, 256, 128):
        if n % cand == 0:
            return cand
    return n


def _resident(w):
    return pl.BlockSpec(w.shape, lambda *_: (0, 0), pipeline_mode=pl.Buffered(1))


def _drop_deps(body, n_in, n_deps):
    if n_deps == 0:
        return body
    return lambda *refs: body(*refs[:n_in], *refs[n_in + n_deps:])


def _ffn_up(hb, wg, wu, name, deps=()):
    t, d = hb.shape
    f = wg.shape[1]
    tm = min(t, 256)
    tn = _col_tile(f)

    def body(h_ref, wg_ref, wu_ref, a_ref, b_ref, s_ref):
        h = h_ref[...]
        for c in range(f // tn):
            cols = slice(c * tn, (c + 1) * tn)
            a = _dot(h, wg_ref[:, cols])
            b = _dot(h, wu_ref[:, cols])
            a_ref[:, cols] = a.astype(BF16)
            b_ref[:, cols] = b.astype(BF16)
            s_ref[:, cols] = (a * _sigmoid(a) * b).astype(BF16)

    act = pl.BlockSpec((tm, f), lambda i: (i, 0))
    return pl.pallas_call(
        _drop_deps(body, 3, len(deps)),
        grid=(t // tm,),
        in_specs=[pl.BlockSpec((tm, d), lambda i: (i, 0)), _resident(wg), _resident(wu)] + [ANY] * len(deps),
        out_specs=[act, act, act],
        out_shape=[jax.ShapeDtypeStruct((t, f), BF16)] * 3,
        compiler_params=_params(1),
        name=name,
    )(hb, wg, wu, *deps)


def _mm_res_ln(lhs, w, res, g, b, coef, name, target=None):
    t, kd = lhs.shape
    d = w.shape[1]
    tm = _row_tile(t)
    nt = t // tm
    from_norm = isinstance(res, tuple)
    n_res = 3 if from_norm else 1

    def body(*refs):
        l_ref, w_ref = refs[:2]
        r_refs = refs[2:2 + n_res]
        g_ref, b_ref = refs[2 + n_res:4 + n_res]
        rest = refs[4 + n_res:]
        prev = r_refs[0][...] * r_refs[1][...] + r_refs[2][...] if from_norm else r_refs[0][...]
        y = ALPHA * prev + coef * _dot(l_ref[...], w_ref[...])
        h, xhat, rstd = _ln_fwd(y, g_ref[...], b_ref[...])
        if target is None:
            hb_ref, xh_ref, rs_ref = rest
            hb_ref[...] = h.astype(BF16)
            xh_ref[...] = xhat
            rs_ref[...] = rstd
            return
        t_ref, loss_ref, dy_ref, dyb_ref, dg_ref, db_ref, lacc = rest
        i = pl.program_id(0)

        @pl.when(i == 0)
        def _():
            lacc[...] = jnp.zeros_like(lacc)
            dg_ref[...] = jnp.zeros_like(dg_ref)
            db_ref[...] = jnp.zeros_like(db_ref)

        err = h - t_ref[...]
        lacc[...] += jnp.sum(err * err, axis=0, keepdims=True)
        dy, dg, db = _ln_bwd(err * (1.0 / d), xhat, rstd, g_ref[...])
        dy_ref[...] = dy
        dyb_ref[...] = dy.astype(BF16)
        dg_ref[...] += dg
        db_ref[...] += db

        @pl.when(i == nt - 1)
        def _():
            loss_ref[...] = jnp.zeros_like(loss_ref) + jnp.sum(lacc[...], axis=1, keepdims=True) * (0.5 / d)

    row = pl.BlockSpec((tm, d), lambda i: (i, 0))
    vec = pl.BlockSpec((1, d), lambda i: (0, 0))
    res_specs = [row, vec, vec] if from_norm else [row]
    res_args = list(res) if from_norm else [res]
    in_specs = [pl.BlockSpec((tm, kd), lambda i: (i, 0)), _resident(w)] + res_specs + [vec, vec]
    args = [lhs, w] + res_args + [g, b]
    if target is None:
        out_specs = [row, row, pl.BlockSpec((tm, 1), lambda i: (i, 0))]
        out_shape = [jax.ShapeDtypeStruct((t, d), BF16), jax.ShapeDtypeStruct((t, d), F32),
                     jax.ShapeDtypeStruct((t, 1), F32)]
        scratch = []
    else:
        in_specs.append(row)
        args.append(target)
        out_specs = [pl.BlockSpec((1, LANES), lambda i: (0, 0)), row, row, vec, vec]
        out_shape = [jax.ShapeDtypeStruct((1, LANES), F32), jax.ShapeDtypeStruct((t, d), F32),
                     jax.ShapeDtypeStruct((t, d), BF16), jax.ShapeDtypeStruct((1, d), F32),
                     jax.ShapeDtypeStruct((1, d), F32)]
        scratch = [pltpu.VMEM((1, d), F32)]
    return pl.pallas_call(
        body,
        grid=(nt,),
        in_specs=in_specs,
        out_specs=out_specs,
        out_shape=out_shape,
        scratch_shapes=scratch,
        compiler_params=_params(1),
        name=name,
    )(*args)


def _mm_nn(lhs, w, name):
    t, kd = lhs.shape
    n = w.shape[1]
    tm = _row_tile(t)
    tn = _col_tile(n)

    def body(l_ref, w_ref, o_ref):
        lhs_v = l_ref[...]
        for c in range(n // tn):
            cols = slice(c * tn, (c + 1) * tn)
            o_ref[:, cols] = _dot(lhs_v, w_ref[:, cols])

    return pl.pallas_call(
        body,
        grid=(t // tm,),
        in_specs=[pl.BlockSpec((tm, kd), lambda i: (i, 0)), _resident(w)],
        out_specs=pl.BlockSpec((tm, n), lambda i: (i, 0)),
        out_shape=jax.ShapeDtypeStruct((t, n), F32),
        compiler_params=_params(1),
        name=name,
    )(lhs, w)


def _lower_bound(lg):
    m = jnp.max(lg, axis=0, keepdims=True)
    e = jnp.exp(lg - m)
    return e[0:1, :] / jnp.sum(e, axis=0, keepdims=True)


def _forget_terms(fz, lb):
    e = jnp.exp(-jnp.abs(fz))
    r = 1.0 / (1.0 + e)
    pos = fz >= 0.0
    sig = jnp.where(pos, r, e * r)
    nsig = jnp.where(pos, e * r, r)
    f = lb + (1.0 - lb) * sig
    k = (1.0 - lb) * nsig
    return sig, nsig, f, k


def _hg_tile(t):
    return min(t, 1024)


HG_HALF = HG_BLOCK // 2
NEG_BIG = -1e30


def _halves(a):
    return a[:HG_HALF, :], a[HG_HALF:, :]


def _causal_halves(s):
    return (0, 1) if s < HG_HALF else (1,)


def _decay_from(b_half, b_s, s, h, tidx):
    first = s - h * HG_HALF
    diff = b_half - b_s
    if first > 0:
        diff = jnp.where(tidx >= first, diff, NEG_BIG)
    return jnp.exp(diff)


def _hgrn_fwd(proj, logits, gn):
    t = proj.shape[0]
    ct = _hg_tile(t)
    nct = t // ct
    nblk = ct // HG_BLOCK
    nh = HG_HEADS
    mrows = min(ct, 256)

    def body(q_ref, fz_ref, iv_ref, gg_ref, lg_ref, gn_ref, mask_ref, oraw_ref, oa_ref, st_ref,
             state, qt_s, kt_s, k_s, b_s, dec_s):
        c = pl.program_id(1)

        @pl.when(c == 0)
        def _():
            state[...] = jnp.zeros_like(state)

        lb = _lower_bound(lg_ref[...])
        q = q_ref[...]
        _, _, f, k = _forget_terms(fz_ref[...], lb)
        logf = jnp.log(f)
        b = _mask_dot(mask_ref[0], logf)
        bend = _mask_dot(mask_ref[2], logf)
        qt_s[...] = (q * jnp.exp(b)).astype(BF16)
        kt_s[...] = (k * jnp.exp(bend - b)).astype(BF16)
        k_s[...] = k
        b_s[...] = b
        dec_s[...] = jnp.exp(bend)
        tidx = lax.broadcasted_iota(jnp.int32, (HG_HALF, HG_DIM), 0)

        def blk(i, carry):
            r0 = pl.multiple_of(i * HG_BLOCK, HG_BLOCK)
            rows = pl.ds(r0, HG_BLOCK)
            st = state[...]
            st_ref[i] = st
            v = iv_ref[rows, :]
            qq = q_ref[rows, :]
            kk = k_s[rows, :]
            bb = b_s[rows, :]
            o = list(_halves(_dot_nt(qt_s[rows, :], st.astype(BF16))))
            qh, bh = _halves(qq), _halves(bb)
            for s in range(HG_BLOCK):
                ks, vs = kk[s:s + 1, :], v[s:s + 1, :]
                for h in _causal_halves(s):
                    e = _decay_from(bh[h], bb[s:s + 1, :], s, h, tidx)
                    acol = jnp.sum(qh[h] * (ks * e), axis=1, keepdims=True)
                    o[h] = o[h] + acol * vs
            oraw_ref[rows, :] = jnp.concatenate(o, axis=0)
            state[...] = st * dec_s[pl.ds(r0, 1), :] + _dot_tn(v.astype(BF16), kt_s[rows, :])
            return carry

        lax.fori_loop(0, nblk, blk, 0, unroll=2 * HG_UNROLL)
        oraw = oraw_ref[...]
        r = lax.rsqrt(jnp.mean(oraw * oraw, axis=-1, keepdims=True) + LN_EPS)
        gg = gg_ref[...]
        oa_ref[...] = (oraw * r * gn_ref[...] * gg * _sigmoid(gg)).astype(BF16)

    def slab(off):
        return pl.BlockSpec((ct, HG_DIM), lambda h, c: (c, off + h))

    out_slab = pl.BlockSpec((ct, HG_DIM), lambda h, c: (c, h))
    return pl.pallas_call(
        body,
        grid=(nh, nct),
        in_specs=[slab(0), slab(nh), slab(2 * nh), slab(3 * nh),
                  pl.BlockSpec((None, 2, HG_DIM), lambda h, c: (h, 0, 0)),
                  pl.BlockSpec((1, HG_DIM), lambda h, c: (0, 0)),
                  pl.BlockSpec((3, mrows, mrows), lambda h, c: (0, 0, 0))],
        out_specs=[out_slab, out_slab, pl.BlockSpec((None, nblk, HG_DIM, HG_DIM), lambda h, c: (h, c, 0, 0))],
        out_shape=[jax.ShapeDtypeStruct((t, nh * HG_DIM), F32),
                   jax.ShapeDtypeStruct((t, (nh + SG_GROUPS) * HG_DIM), BF16),
                   jax.ShapeDtypeStruct((nh, t // HG_BLOCK, HG_DIM, HG_DIM), F32)],
        scratch_shapes=[pltpu.VMEM((HG_DIM, HG_DIM), F32), pltpu.VMEM((ct, HG_DIM), BF16),
                        pltpu.VMEM((ct, HG_DIM), BF16), pltpu.VMEM((ct, HG_DIM), F32),
                        pltpu.VMEM((ct, HG_DIM), F32), pltpu.VMEM((ct, HG_DIM), F32)],
        compiler_params=_params(2),
        name="hgrn_fwd",
    )(proj, proj, proj, proj, logits, gn, _block_masks(mrows))


def _sg_tile(t):
    return min(t, 512)


def _sgu_chunk_fwd(u, v, ln_g, ln_b, wm, bs):
    ua, dua = _gelu_and_grad(u)
    va, dva = _gelu_and_grad(v)
    vn, xhat, rstd = _ln_fwd(va, ln_g, ln_b)
    s = _dot(wm, vn.astype(BF16)) + bs
    return ua, dua, dva, vn, xhat, rstd, s


def _tril_weight(w_ref):
    n = SG_CHUNK
    r = lax.broadcasted_iota(jnp.int32, (n, n), 0)
    c = lax.broadcasted_iota(jnp.int32, (n, n), 1)
    return jnp.where(c <= r, w_ref[...], 0.0)


def _sgu_fwd(proj, mix, ln_g, ln_b, w_s, b_col):
    t = proj.shape[0]
    ct = _sg_tile(t)
    ng = SG_GROUPS
    off_u = 4 * HG_HEADS
    off_v = off_u + ng

    def body(u_ref, v_ref, g_ref, b_ref, w_ref, bs_ref, mix_ref, o_ref):
        del mix_ref
        wm = _tril_weight(w_ref).astype(BF16)
        for n in range(ct // SG_CHUNK):
            rows = slice(n * SG_CHUNK, (n + 1) * SG_CHUNK)
            ua, _, _, _, _, _, s = _sgu_chunk_fwd(u_ref[rows, :], v_ref[rows, :], g_ref[...], b_ref[...], wm, bs_ref[...])
            o_ref[rows, :] = (ua * s).astype(BF16)

    vec = pl.BlockSpec((None, 1, SG_DIM), lambda g, c: (g, 0, 0))
    return pl.pallas_call(
        body,
        grid=(ng, t // ct),
        in_specs=[pl.BlockSpec((ct, SG_DIM), lambda g, c: (c, off_u + g)),
                  pl.BlockSpec((ct, SG_DIM), lambda g, c: (c, off_v + g)), vec, vec,
                  pl.BlockSpec((None, SG_CHUNK, SG_CHUNK), lambda g, c: (g, 0, 0)),
                  pl.BlockSpec((None, SG_CHUNK, 1), lambda g, c: (g, 0, 0)), ANY],
        out_specs=pl.BlockSpec((ct, SG_DIM), lambda g, c: (c, HG_HEADS + g)),
        out_shape=jax.ShapeDtypeStruct(mix.shape, mix.dtype),
        input_output_aliases={6: 0},
        compiler_params=_params(2),
        name="sgu_fwd",
    )(proj, proj, ln_g, ln_b, w_s, b_col, mix)


def _mem_kv(mem, g, b, wk, wv):
    m_len, d = mem.shape

    def body(m_ref, g_ref, b_ref, wk_ref, wv_ref, mb_ref, xh_ref, rs_ref, k_ref, v_ref):
        m, xhat, rstd = _ln_fwd(m_ref[...], g_ref[...], b_ref[...])
        mb = m.astype(BF16)
        mb_ref[...] = mb
        xh_ref[...] = xhat
        rs_ref[...] = rstd
        k_ref[...] = _dot(mb, wk_ref[...]).astype(BF16)
        v_ref[...] = _dot(mb, wv_ref[...]).astype(BF16)

    return pl.pallas_call(
        body,
        out_shape=[jax.ShapeDtypeStruct((m_len, d), BF16), jax.ShapeDtypeStruct((m_len, d), F32),
                   jax.ShapeDtypeStruct((m_len, 1), F32), jax.ShapeDtypeStruct((m_len, d), BF16),
                   jax.ShapeDtypeStruct((m_len, d), BF16)],
        compiler_params=pltpu.CompilerParams(vmem_limit_bytes=VMEM_LIMIT_V7X),
        name="mem_kv",
    )(mem, g, b, wk, wv)


def _softmax_rows(s):
    m = jnp.max(s, axis=-1, keepdims=True)
    p = jnp.exp(s - m)
    return p / jnp.sum(p, axis=-1, keepdims=True)


def _attn_fwd(hb, wq, kb, vb):
    t, d = hb.shape
    tm = _row_tile(t)
    dh = d // X_HEADS
    scale = dh ** -0.5

    def body(h_ref, wq_ref, k_ref, v_ref, q_ref, o_ref):
        q = _dot(h_ref[...], wq_ref[...]).astype(BF16)
        q_ref[...] = q
        for hd in range(X_HEADS):
            sl = slice(hd * dh, (hd + 1) * dh)
            p = _softmax_rows(_dot_nt(q[:, sl], k_ref[:, sl]) * scale)
            o_ref[:, sl] = _dot(p.astype(BF16), v_ref[:, sl]).astype(BF16)

    row = pl.BlockSpec((tm, d), lambda i: (i, 0))
    full = lambda a: pl.BlockSpec(a.shape, lambda i: (0, 0))
    return pl.pallas_call(
        body,
        grid=(t // tm,),
        in_specs=[row, full(wq), full(kb), full(vb)],
        out_specs=[row, row],
        out_shape=[jax.ShapeDtypeStruct((t, d), BF16), jax.ShapeDtypeStruct((t, d), BF16)],
        compiler_params=_params(1),
        name="attn_fwd",
    )(hb, wq, kb, vb)


def _ffn_bwd_act(dyb, wd, a, b, coef, name, deps=()):
    t, d = dyb.shape
    f = wd.shape[0]
    tm = _row_tile(t)
    tn = _col_tile(f)

    def body(dy_ref, wd_ref, a_ref, b_ref, da_ref, db_ref):
        dy = dy_ref[...]
        for c in range(f // tn):
            cols = slice(c * tn, (c + 1) * tn)
            ds = _dot_nt(dy, wd_ref[cols, :]) * coef
            silu, dsilu = _silu_and_grad(a_ref[:, cols].astype(F32))
            da_ref[:, cols] = (ds * b_ref[:, cols].astype(F32) * dsilu).astype(BF16)
            db_ref[:, cols] = (ds * silu).astype(BF16)

    act = pl.BlockSpec((tm, f), lambda i: (i, 0))
    return pl.pallas_call(
        _drop_deps(body, 4, len(deps)),
        grid=(t // tm,),
        in_specs=[pl.BlockSpec((tm, d), lambda i: (i, 0)), _resident(wd), act, act] + [ANY] * len(deps),
        out_specs=[act, act],
        out_shape=[jax.ShapeDtypeStruct((t, f), BF16), jax.ShapeDtypeStruct((t, f), BF16)],
        compiler_params=_params(1),
        name=name,
    )(dyb, wd, a, b, *deps)


def _ffn_bwd_fused(dy, dyb, wd, wg, wu, a, b, coef, ln, name):
    t, d = dy.shape
    f = wd.shape[0]
    tm = min(t, 256)
    tn = _col_tile(f)

    def body(dy_ref, dyb_ref, wd_ref, wg_ref, wu_ref, a_ref, b_ref, xh_ref, rs_ref, g_ref,
             da_ref, db_ref, dyo_ref, dyob_ref, dg_ref, dbl_ref):
        dyb_v = dyb_ref[...]
        dh = ALPHA * dy_ref[...]
        for c in range(f // tn):
            cols = slice(c * tn, (c + 1) * tn)
            ds = _dot_nt(dyb_v, wd_ref[cols, :]) * coef
            silu, dsilu = _silu_and_grad(a_ref[:, cols].astype(F32))
            da = (ds * b_ref[:, cols].astype(F32) * dsilu).astype(BF16)
            db = (ds * silu).astype(BF16)
            da_ref[:, cols] = da
            db_ref[:, cols] = db
            dh = dh + _dot_nt(da, wg_ref[:, cols]) + _dot_nt(db, wu_ref[:, cols])

        @pl.when(pl.program_id(0) == 0)
        def _():
            dg_ref[...] = jnp.zeros_like(dg_ref)
            dbl_ref[...] = jnp.zeros_like(dbl_ref)

        dyp, dg, dbl = _ln_bwd(dh, xh_ref[...], rs_ref[...], g_ref[...])
        dyo_ref[...] = dyp
        dyob_ref[...] = dyp.astype(BF16)
        dg_ref[...] += dg
        dbl_ref[...] += dbl

    row = pl.BlockSpec((tm, d), lambda i: (i, 0))
    act = pl.BlockSpec((tm, f), lambda i: (i, 0))
    vec = pl.BlockSpec((1, d), lambda i: (0, 0))
    return pl.pallas_call(
        body,
        grid=(t // tm,),
        in_specs=[row, row, _resident(wd), _resident(wg), _resident(wu), act, act, row,
                  pl.BlockSpec((tm, 1), lambda i: (i, 0)), vec],
        out_specs=[act, act, row, row, vec, vec],
        out_shape=[jax.ShapeDtypeStruct((t, f), BF16), jax.ShapeDtypeStruct((t, f), BF16),
                   jax.ShapeDtypeStruct((t, d), F32), jax.ShapeDtypeStruct((t, d), BF16),
                   jax.ShapeDtypeStruct((1, d), F32), jax.ShapeDtypeStruct((1, d), F32)],
        compiler_params=_params(1),
        name=name,
    )(dy, dyb, wd, wg, wu, a, b, *ln)


def _mm_tn(a, b, name, scale=1.0, deps=()):
    t, m = a.shape
    n = b.shape[1]
    tt = _row_tile(t)
    nt = t // tt
    tm_o, tn_o = m, n

    def body(a_ref, b_ref, o_ref, acc):
        k = pl.program_id(2)

        @pl.when(k == 0)
        def _():
            acc[...] = jnp.zeros_like(acc)

        acc[...] += _dot_tn(a_ref[...], b_ref[...])

        @pl.when(k == nt - 1)
        def _():
            o_ref[...] = (acc[...] * scale).astype(BF16)

    return pl.pallas_call(
        _drop_deps(body, 2, len(deps)),
        grid=(m // tm_o, n // tn_o, nt),
        in_specs=[pl.BlockSpec((tt, tm_o), lambda i, j, k: (k, i)), pl.BlockSpec((tt, tn_o), lambda i, j, k: (k, j))]
        + [ANY] * len(deps),
        out_specs=pl.BlockSpec((tm_o, tn_o), lambda i, j, k: (i, j)),
        out_shape=jax.ShapeDtypeStruct((m, n), BF16),
        scratch_shapes=[pltpu.VMEM((tm_o, tn_o), F32)],
        compiler_params=_params(3),
        name=name,
    )(a, b, *deps)


def _mm_nt(lhs, w, name):
    t, d = lhs.shape
    kd = w.shape[0]
    tm = _row_tile(t)

    def body(l_ref, w_ref, o_ref):
        o_ref[...] = _dot_nt(l_ref[...], w_ref[...])

    return pl.pallas_call(
        body,
        grid=(t // tm,),
        in_specs=[pl.BlockSpec((tm, d), lambda i: (i, 0)), _resident(w)],
        out_specs=pl.BlockSpec((tm, kd), lambda i: (i, 0)),
        out_shape=jax.ShapeDtypeStruct((t, kd), F32),
        compiler_params=_params(1),
        name=name,
    )(lhs, w)


def _dx_ln(dy, pairs, ln, name, deps=()):
    t, d = dy.shape
    npair = len(pairs)
    tm = min(t, 512 // npair)
    nt = t // tm
    n_in = 1 + 2 * npair + (3 if ln is not None else 0)

    def body(*refs):
        dy_ref = refs[0]
        pr = refs[1:1 + 2 * npair]
        pos = 1 + 2 * npair
        dh = ALPHA * dy_ref[...]
        for p in range(npair):
            dh = dh + _dot_nt(pr[2 * p][...], pr[2 * p + 1][...])
        if ln is not None:
            xh_ref, rs_ref, g_ref = refs[pos:pos + 3]
            dyo_ref, dyb_ref, dg_ref, db_ref = refs[pos + 3:pos + 7]

            @pl.when(pl.program_id(0) == 0)
            def _():
                dg_ref[...] = jnp.zeros_like(dg_ref)
                db_ref[...] = jnp.zeros_like(db_ref)

            dyp, dg, db = _ln_bwd(dh, xh_ref[...], rs_ref[...], g_ref[...])
            dyo_ref[...] = dyp
            dyb_ref[...] = dyp.astype(BF16)
            dg_ref[...] += dg
            db_ref[...] += db
        else:
            refs[pos][...] = dh

    row = pl.BlockSpec((tm, d), lambda i: (i, 0))
    vec = pl.BlockSpec((1, d), lambda i: (0, 0))
    in_specs = [row]
    args = [dy]
    for lhs, w in pairs:
        in_specs += [pl.BlockSpec((tm, lhs.shape[1]), lambda i: (i, 0)), _resident(w)]
        args += [lhs, w]
    if ln is not None:
        in_specs += [row, pl.BlockSpec((tm, 1), lambda i: (i, 0)), vec]
        args += list(ln)
        out_specs = [row, row, vec, vec]
        out_shape = [jax.ShapeDtypeStruct((t, d), F32), jax.ShapeDtypeStruct((t, d), BF16),
                     jax.ShapeDtypeStruct((1, d), F32), jax.ShapeDtypeStruct((1, d), F32)]
    else:
        out_specs = row
        out_shape = jax.ShapeDtypeStruct((t, d), F32)
    return pl.pallas_call(
        _drop_deps(body, n_in, len(deps)),
        grid=(nt,),
        in_specs=in_specs + [ANY] * len(deps),
        out_specs=out_specs,
        out_shape=out_shape,
        compiler_params=_params(1),
        name=name,
    )(*args, *deps)


def _hgrn_bwd(proj, oraw, dmix, states, logits, gn):
    t = proj.shape[0]
    ct = _hg_tile(t)
    nct = t // ct
    nblk = ct // HG_BLOCK
    nh = HG_HEADS
    mrows = min(ct, 256)

    def body(q_ref, fz_ref, iv_ref, gg_ref, or_ref, do_ref, st_ref, lg_ref, gn_ref, mask_ref,
             dq_ref, dfz_ref, div_ref, dgg_ref, dlg_ref, dgn_ref,
             dstate, qt_s, kt_s, k_s, b_s, eb_s, ekb_s, dec_s, dor_s, dbl_s, gr_s, dk_s, dlb_acc):
        c = pl.program_id(1)

        @pl.when(c == 0)
        def _():
            dstate[...] = jnp.zeros_like(dstate)
            dlb_acc[...] = jnp.zeros_like(dlb_acc)
            dgn_ref[...] = jnp.zeros_like(dgn_ref)

        lb = _lower_bound(lg_ref[...])
        q = q_ref[...]
        sig, nsig, f, k = _forget_terms(fz_ref[...], lb)
        logf = jnp.log(f)
        b = _mask_dot(mask_ref[0], logf)
        bend = _mask_dot(mask_ref[2], logf)
        eb = jnp.exp(b)
        ekb = jnp.exp(bend - b)
        qt_s[...] = (q * eb).astype(BF16)
        kt_s[...] = (k * ekb).astype(BF16)
        k_s[...] = k
        b_s[...] = b
        eb_s[...] = eb
        ekb_s[...] = ekb
        dec_s[...] = jnp.exp(bend)
        oraw = or_ref[...]
        r = lax.rsqrt(jnp.mean(oraw * oraw, axis=-1, keepdims=True) + LN_EPS)
        on = oraw * r
        gg = gg_ref[...]
        silu, dsilu = _silu_and_grad(gg)
        doa = do_ref[...]
        gnv = gn_ref[...]
        dgg_ref[...] = (doa * on * gnv * dsilu).astype(BF16)
        dyn = doa * silu
        dgn_ref[...] += jnp.sum(dyn * on, axis=0, keepdims=True)
        don = dyn * gnv
        dor_s[...] = r * (don - on * jnp.mean(don * on, axis=-1, keepdims=True))
        tidx = lax.broadcasted_iota(jnp.int32, (HG_HALF, HG_DIM), 0)

        def blk(ii, carry):
            i = nblk - 1 - ii
            r0 = pl.multiple_of(i * HG_BLOCK, HG_BLOCK)
            rows = pl.ds(r0, HG_BLOCK)
            st = st_ref[i]
            dst = dstate[...]
            dstb = dst.astype(BF16)
            do = dor_s[rows, :]
            dob = do.astype(BF16)
            v = iv_ref[rows, :]
            vb = v.astype(BF16)
            qq = q_ref[rows, :]
            kk = k_s[rows, :]
            bb = b_s[rows, :]
            qt = qt_s[rows, :]
            kt = kt_s[rows, :]
            dec = dec_s[pl.ds(r0, 1), :]
            dkt = _dot(vb, dstb)
            dq = _dot(dob, st.astype(BF16)) * eb_s[rows, :]
            dk = dkt * ekb_s[rows, :]
            dv = _dot_nt(kt, dstb)
            gend = jnp.sum(kk * dk, axis=0, keepdims=True) + dec * jnp.sum(dst * st, axis=0, keepdims=True)
            qh, bh, doh = _halves(qq), _halves(bb), _halves(do)
            dqh, dkh, dvh = list(_halves(dq)), list(_halves(dk)), list(_halves(dv))
            for s in range(HG_BLOCK):
                ks, vs = kk[s:s + 1, :], v[s:s + 1, :]
                dk_part = dv_part = None
                for h in _causal_halves(s):
                    e = _decay_from(bh[h], bb[s:s + 1, :], s, h, tidx)
                    ke = ks * e
                    acol = jnp.sum(qh[h] * ke, axis=1, keepdims=True)
                    dacol = jnp.sum(doh[h] * vs, axis=1, keepdims=True)
                    dqh[h] = dqh[h] + dacol * ke
                    pk = dacol * (qh[h] * e)
                    pv = acol * doh[h]
                    dk_part = pk if dk_part is None else dk_part + pk
                    dv_part = pv if dv_part is None else dv_part + pv
                hs, row = divmod(s, HG_HALF)
                dkh[hs] = dkh[hs] + jnp.where(tidx == row, jnp.sum(dk_part, axis=0, keepdims=True), 0.0)
                dvh[hs] = dvh[hs] + jnp.where(tidx == row, jnp.sum(dv_part, axis=0, keepdims=True), 0.0)
            dq = jnp.concatenate(dqh, axis=0)
            dk = jnp.concatenate(dkh, axis=0)
            dv = jnp.concatenate(dvh, axis=0)
            dq_ref[rows, :] = dq.astype(BF16)
            div_ref[rows, :] = dv.astype(BF16)
            dk_s[rows, :] = dk
            dbl_s[rows, :] = qq * dq - kk * dk
            gr_s[rows, :] = jnp.zeros((HG_BLOCK, HG_DIM), F32) + gend
            dstate[...] = dst * dec + _dot_tn(dob, qt)
            return carry

        lax.fori_loop(0, nblk, blk, 0, unroll=2 * HG_UNROLL)
        dlogf = _mask_dot(mask_ref[1], dbl_s[...]) + gr_s[...]
        dk = dk_s[...]
        dfz_ref[...] = ((dlogf / f - dk) * ((1.0 - lb) * sig * nsig)).astype(BF16)
        dlb_acc[...] += jnp.sum((dlogf / f - dk) * nsig, axis=0, keepdims=True)

        @pl.when(c == nct - 1)
        def _():
            dl0 = dlb_acc[...] * lb * (1.0 - lb)
            layer = lax.broadcasted_iota(jnp.int32, (2, HG_DIM), 0)
            dlg_ref[...] = jnp.where(layer == 0, dl0, -dl0)

    def slab(off):
        return pl.BlockSpec((ct, HG_DIM), lambda h, c: (nct - 1 - c, off + h))

    out_slab = pl.BlockSpec((ct, HG_DIM), lambda h, c: (nct - 1 - c, h))
    tile_f32 = pltpu.VMEM((ct, HG_DIM), F32)
    tile_b16 = pltpu.VMEM((ct, HG_DIM), BF16)
    slab_shape = jax.ShapeDtypeStruct((t, nh * HG_DIM), BF16)
    return pl.pallas_call(
        body,
        grid=(nh, nct),
        in_specs=[slab(0), slab(nh), slab(2 * nh), slab(3 * nh), slab(0), slab(0),
                  pl.BlockSpec((None, nblk, HG_DIM, HG_DIM), lambda h, c: (h, nct - 1 - c, 0, 0)),
                  pl.BlockSpec((None, 2, HG_DIM), lambda h, c: (h, 0, 0)),
                  pl.BlockSpec((1, HG_DIM), lambda h, c: (0, 0)),
                  pl.BlockSpec((3, mrows, mrows), lambda h, c: (0, 0, 0))],
        out_specs=[out_slab, out_slab, out_slab, out_slab,
                   pl.BlockSpec((None, 2, HG_DIM), lambda h, c: (h, 0, 0)),
                   pl.BlockSpec((None, 1, HG_DIM), lambda h, c: (h, 0, 0))],
        out_shape=[slab_shape, slab_shape, slab_shape, slab_shape,
                   jax.ShapeDtypeStruct((nh, 2, HG_DIM), F32), jax.ShapeDtypeStruct((nh, 1, HG_DIM), F32)],
        scratch_shapes=[pltpu.VMEM((HG_DIM, HG_DIM), F32), tile_b16, tile_b16, tile_f32, tile_f32, tile_f32, tile_f32,
                        tile_f32, tile_f32, tile_f32, tile_f32, tile_f32, pltpu.VMEM((1, HG_DIM), F32)],
        compiler_params=_params(2),
        name="hgrn_bwd",
    )(proj, proj, proj, proj, oraw, dmix, states, logits, gn, _block_masks(mrows))


def _sgu_bwd(proj, dmix, ln_g, ln_b, w_s, w_t, b_col):
    t = proj.shape[0]
    ct = _sg_tile(t)
    nct = t // ct
    ng = SG_GROUPS
    off_u = 4 * HG_HEADS
    off_v = off_u + ng
    n = SG_CHUNK

    def body(u_ref, v_ref, do_ref, g_ref, b_ref, w_ref, wt_ref, bs_ref, du_ref, dv_ref, dg_ref, db_ref, dw_ref, dbs_ref):
        c = pl.program_id(1)

        @pl.when(c == 0)
        def _():
            dg_ref[...] = jnp.zeros_like(dg_ref)
            db_ref[...] = jnp.zeros_like(db_ref)
            dw_ref[...] = jnp.zeros_like(dw_ref)
            dbs_ref[...] = jnp.zeros_like(dbs_ref)

        r = lax.broadcasted_iota(jnp.int32, (n, n), 0)
        cc = lax.broadcasted_iota(jnp.int32, (n, n), 1)
        wm = jnp.where(cc <= r, w_ref[...], 0.0).astype(BF16)
        wmt = jnp.where(r <= cc, wt_ref[...], 0.0).astype(BF16)
        for ci in range(ct // n):
            rows = slice(ci * n, (ci + 1) * n)
            ua, dua, dva, vn, xhat, rstd, s = _sgu_chunk_fwd(u_ref[rows, :], v_ref[rows, :], g_ref[...], b_ref[...],
                                                             wm, bs_ref[...])
            do = do_ref[rows, :]
            du_ref[rows, :] = (do * s * dua).astype(BF16)
            ds = do * ua
            dsb = ds.astype(BF16)
            dbs_ref[...] += jnp.sum(ds, axis=1, keepdims=True)
            dw_ref[...] += _dot_nt(dsb, vn.astype(BF16))
            dvn = _dot(wmt, dsb)
            dva_in, dg, db = _ln_bwd(dvn, xhat, rstd, g_ref[...])
            dg_ref[...] += dg
            db_ref[...] += db
            dv_ref[rows, :] = (dva_in * dva).astype(BF16)

        @pl.when(c == nct - 1)
        def _():
            dw_ref[...] = jnp.where(cc <= r, dw_ref[...], 0.0)

    vec = pl.BlockSpec((None, 1, SG_DIM), lambda g, c: (g, 0, 0))
    mat = pl.BlockSpec((None, n, n), lambda g, c: (g, 0, 0))
    col = pl.BlockSpec((None, n, 1), lambda g, c: (g, 0, 0))
    out_slab = pl.BlockSpec((ct, SG_DIM), lambda g, c: (c, g))
    return pl.pallas_call(
        body,
        grid=(ng, nct),
        in_specs=[pl.BlockSpec((ct, SG_DIM), lambda g, c: (c, off_u + g)),
                  pl.BlockSpec((ct, SG_DIM), lambda g, c: (c, off_v + g)),
                  pl.BlockSpec((ct, SG_DIM), lambda g, c: (c, ng + g)), vec, vec, mat, mat, col],
        out_specs=[out_slab, out_slab, vec, vec, mat, col],
        out_shape=[jax.ShapeDtypeStruct((t, ng * SG_DIM), BF16), jax.ShapeDtypeStruct((t, ng * SG_DIM), BF16),
                   jax.ShapeDtypeStruct((ng, 1, SG_DIM), F32), jax.ShapeDtypeStruct((ng, 1, SG_DIM), F32),
                   jax.ShapeDtypeStruct((ng, n, n), F32), jax.ShapeDtypeStruct((ng, n, 1), F32)],
        compiler_params=_params(2),
        name="sgu_bwd",
    )(proj, proj, dmix, ln_g, ln_b, w_s, w_t, b_col)


def _attn_bwd(dyb, wo, qb, kb, vb):
    t, d = dyb.shape
    m_len = kb.shape[0]
    tm = _row_tile(t)
    dh = d // X_HEADS
    scale = dh ** -0.5

    def body(dy_ref, wo_ref, q_ref, k_ref, v_ref, dq_ref, dk_ref, dv_ref):
        i = pl.program_id(0)

        @pl.when(i == 0)
        def _():
            dk_ref[...] = jnp.zeros_like(dk_ref)
            dv_ref[...] = jnp.zeros_like(dv_ref)

        do = _dot_nt(dy_ref[...], wo_ref[...]).astype(BF16)
        for hd in range(X_HEADS):
            sl = slice(hd * dh, (hd + 1) * dh)
            qh = q_ref[:, sl]
            p = _softmax_rows(_dot_nt(qh, k_ref[:, sl]) * scale)
            doh = do[:, sl]
            dp = _dot_nt(doh, v_ref[:, sl])
            ds = (p * (dp - jnp.sum(dp * p, axis=-1, keepdims=True)) * scale).astype(BF16)
            dq_ref[:, sl] = _dot(ds, k_ref[:, sl]).astype(BF16)
            dk_ref[:, sl] += _dot_tn(ds, qh)
            dv_ref[:, sl] += _dot_tn(p.astype(BF16), doh)

    row = pl.BlockSpec((tm, d), lambda i: (i, 0))
    full = lambda a: pl.BlockSpec(a.shape, lambda i: (0, 0))
    kv = pl.BlockSpec((m_len, d), lambda i: (0, 0))
    return pl.pallas_call(
        body,
        grid=(t // tm,),
        in_specs=[row, full(wo), row, full(kb), full(vb)],
        out_specs=[row, kv, kv],
        out_shape=[jax.ShapeDtypeStruct((t, d), BF16), jax.ShapeDtypeStruct((m_len, d), F32),
                   jax.ShapeDtypeStruct((m_len, d), F32)],
        compiler_params=_params(1),
        name="attn_bwd",
    )(dyb, wo, qb, kb, vb)


def _mem_bwd(dk, dv, mb, xhat, rstd, g, wk, wv):
    m_len, d = dk.shape

    def body(dk_ref, dv_ref, mb_ref, xh_ref, rs_ref, g_ref, wk_ref, wv_ref, gwk_ref, gwv_ref, dg_ref, db_ref):
        dkb = dk_ref[...].astype(BF16)
        dvb = dv_ref[...].astype(BF16)
        mb_v = mb_ref[...]
        gwk_ref[...] = _dot_tn(mb_v, dkb).astype(BF16)
        gwv_ref[...] = _dot_tn(mb_v, dvb).astype(BF16)
        dm = _dot_nt(dkb, wk_ref[...]) + _dot_nt(dvb, wv_ref[...])
        _, dg, db = _ln_bwd(dm, xh_ref[...], rs_ref[...], g_ref[...])
        dg_ref[...] = dg
        db_ref[...] = db

    return pl.pallas_call(
        body,
        out_shape=[jax.ShapeDtypeStruct((d, d), BF16), jax.ShapeDtypeStruct((d, d), BF16),
                   jax.ShapeDtypeStruct((1, d), F32), jax.ShapeDtypeStruct((1, d), F32)],
        compiler_params=pltpu.CompilerParams(vmem_limit_bytes=VMEM_LIMIT_V7X),
        name="mem_bwd",
    )(dk, dv, mb, xhat, rstd, g, wk, wv)


def _adamw(w, g, m, v):
    m = ADAM_B1 * m + (1.0 - ADAM_B1) * g
    v = ADAM_B2 * v + (1.0 - ADAM_B2) * (g * g)
    m_hat = m / (1.0 - ADAM_B1 ** ADAM_STEP)
    v_hat = v / (1.0 - ADAM_B2 ** ADAM_STEP)
    delta = -ADAM_LR * (m_hat / (jnp.sqrt(v_hat) + ADAM_EPS) + ADAM_WD * w)
    return delta, m, v


def _slot_sum(ref):
    g = ref[0].astype(F32)
    for s in range(1, N_DEV):
        g = g + ref[s].astype(F32)
    return g


def _adam_sharded(lands, w, m, v, axis, name):
    rows, cols = w.shape
    nl = len(lands)
    transposed = axis == 1 and nl == 2
    if transposed:
        rows, cols = cols, rows
        tr = 256
        grid = (rows // tr,)
        wblk = pl.BlockSpec((cols, tr), lambda i: (0, i))
        lblk = [pl.BlockSpec((N_DEV, tr, a.shape[2]), lambda i: (0, i, 0)) for a in lands]
    elif axis == 1:
        tr = 256 if rows % 256 == 0 else rows
        grid = (rows // tr,)
        wblk = pl.BlockSpec((tr, cols), lambda i: (i, 0))
        lblk = [pl.BlockSpec((N_DEV, tr, a.shape[2]), lambda i: (0, i, 0)) for a in lands]
    else:
        tc = _col_tile(cols)
        grid = (cols // tc,)
        wblk = pl.BlockSpec((rows, tc), lambda i: (0, i))
        lblk = [pl.BlockSpec((N_DEV, a.shape[1], tc), lambda i: (0, 0, i)) for a in lands]

    def body(*refs):
        w_ref, m_ref, v_ref = refs[nl:nl + 3]
        g_ref, d_ref, nm_ref, nv_ref = refs[nl + 3:]
        g = _slot_sum(refs[0])
        if nl == 2:
            tail = _slot_sum(refs[1])
            if transposed:
                g = jnp.concatenate([g.T, tail.T[:cols - g.shape[1], :]], axis=0)
            elif axis == 1:
                g = jnp.concatenate([g, tail[:, :cols - g.shape[1]]], axis=1)
            else:
                g = jnp.concatenate([g, tail[:rows - g.shape[0], :]], axis=0)
        delta, nm, nv = _adamw(w_ref[...], g, m_ref[...], v_ref[...])
        g_ref[...] = g
        d_ref[...] = delta
        nm_ref[...] = nm
        nv_ref[...] = nv

    shp = jax.ShapeDtypeStruct(w.shape, F32)
    return pl.pallas_call(
        body,
        grid=grid,
        in_specs=lblk + [wblk, wblk, wblk],
        out_specs=[wblk, wblk, wblk, wblk],
        out_shape=[shp, shp, shp, shp],
        compiler_params=_params(1),
        name=name,
    )(*lands, w, m, v)


def _mesh_pos():
    return lax.axis_index("x"), lax.axis_index("y"), lax.axis_index("c")


def _peer(k):
    x, y, c = _mesh_pos()
    pos = (x ^ (k >> 2), y ^ ((k >> 1) & 1), c ^ (k & 1))
    return pos, 4 * pos[0] + 2 * pos[1] + pos[2]


def _sem_index(row, k):
    return row * (N_DEV - 1) + k - 1


def _window(ref, axis, start, size):
    align = 16 if axis == 0 else LANES
    start = pl.multiple_of(start, align)
    return ref.at[pl.ds(start, size), :] if axis == 0 else ref.at[:, pl.ds(start, size)]


def _piece_refs(piece, srcs, lands, me, peer):
    kind, si, li, axis, base, stride, shape = piece
    if kind == "gather":
        return srcs[si], _window(lands[li], axis, base + stride * me, shape[axis])
    return _window(srcs[si], axis, base + stride * peer, shape[axis]), lands[li].at[me]


def _place_own(srcs, land_shapes, pieces, name):
    ns, nl, npc = len(srcs), len(land_shapes), len(pieces)

    def body(*refs):
        s_refs = refs[:ns]
        l_refs = refs[ns:ns + nl]
        bufs = refs[ns + nl:ns + nl + npc]
        sems = refs[ns + nl + npc]
        x, y, c = _mesh_pos()
        me = 4 * x + 2 * y + c
        loads = []
        for p, piece in enumerate(pieces):
            src, dst = _piece_refs(piece, s_refs, l_refs, me, me)
            cp = pltpu.make_async_copy(src, bufs[p], sems.at[0, p])
            cp.start()
            loads.append((cp, dst))
        stores = []
        for p, (cp, dst) in enumerate(loads):
            cp.wait()
            out = pltpu.make_async_copy(bufs[p], dst, sems.at[1, p])
            out.start()
            stores.append(out)
        for out in stores:
            out.wait()

    out = pl.pallas_call(
        body,
        in_specs=[ANY] * ns,
        out_specs=[ANY] * nl,
        out_shape=list(land_shapes),
        scratch_shapes=[pltpu.VMEM(pc[6], srcs[pc[1]].dtype) for pc in pieces] + [pltpu.SemaphoreType.DMA((2, npc))],
        compiler_params=pltpu.CompilerParams(vmem_limit_bytes=VMEM_LIMIT_V7X),
        name=name,
    )(*srcs)
    return list(out)


def _comm_start(srcs, lands, pieces, groups, name, after=()):
    ns, nl, na, ng = len(srcs), len(lands), len(after), len(groups)

    def body(*refs):
        s_refs = refs[:ns]
        l_refs = refs[ns:ns + nl]
        outs = refs[ns + nl + na:]
        sems = outs[:2 * ng]
        token = outs[-1]
        x, y, c = _mesh_pos()
        me = 4 * x + 2 * y + c
        for g, members in enumerate(groups):
            for row, p in enumerate(members):
                for k in range(1, N_DEV):
                    pos, peer = _peer(k)
                    src, dst = _piece_refs(pieces[p], s_refs, l_refs, me, peer)
                    pltpu.make_async_remote_copy(src_ref=src, dst_ref=dst, send_sem=sems[2 * g].at[_sem_index(row, k)],
                                                 recv_sem=sems[2 * g + 1].at[_sem_index(row, k)], device_id=pos,
                                                 device_id_type=MESH_ID).start()
        token[...] = jnp.zeros_like(token)

    sem_shapes = []
    for members in groups:
        sem_shapes += [pltpu.SemaphoreType.DMA((len(members) * (N_DEV - 1),))] * 2
    hbm_of = lambda a: pltpu.HBM(a.shape, a.dtype)
    out = pl.pallas_call(
        body,
        in_specs=[HBM] * (ns + nl) + [ANY] * na,
        out_specs=[SEM] * (2 * ng) + [HBM] * (ns + nl) + [pl.BlockSpec(memory_space=pltpu.VMEM)],
        out_shape=sem_shapes + [hbm_of(a) for a in srcs] + [hbm_of(a) for a in lands]
        + [jax.ShapeDtypeStruct((8, LANES), F32)],
        input_output_aliases={i: 2 * ng + i for i in range(ns + nl)},
        compiler_params=pltpu.CompilerParams(has_side_effects=DATAFLOW),
        name=name,
    )(*[pltpu.with_memory_space_constraint(a, pltpu.HBM) for a in list(srcs) + list(lands)], *after)
    sems = [(out[2 * g], out[2 * g + 1]) for g in range(ng)]
    return sems, list(out[2 * ng:2 * ng + ns]), list(out[2 * ng + ns:2 * ng + ns + nl]), out[-1]


def _comm_wait(srcs, lands, pieces, members, sems, after, name):
    ns, nl, na = len(srcs), len(lands), len(after)

    def body(*refs):
        s_refs = refs[:ns]
        l_refs = refs[ns:ns + nl]
        send_sems, recv_sems = refs[ns + nl:ns + nl + 2]
        x, y, c = _mesh_pos()
        me = 4 * x + 2 * y + c
        for row, p in enumerate(members):
            for k in range(1, N_DEV):
                pos, peer = _peer(k)
                src, dst = _piece_refs(pieces[p], s_refs, l_refs, me, peer)
                cp = pltpu.make_async_remote_copy(src_ref=src, dst_ref=dst, send_sem=send_sems.at[_sem_index(row, k)],
                                                  recv_sem=recv_sems.at[_sem_index(row, k)], device_id=pos,
                                                  device_id_type=MESH_ID)
                cp.wait_send()
                cp.wait_recv()

    hbm_of = lambda a: pltpu.HBM(a.shape, a.dtype)
    out = pl.pallas_call(
        body,
        in_specs=[HBM] * (ns + nl) + [SEM, SEM] + [ANY] * na,
        out_specs=[HBM] * (ns + nl),
        out_shape=[hbm_of(a) for a in srcs] + [hbm_of(a) for a in lands],
        input_output_aliases={i: i for i in range(ns + nl)},
        compiler_params=pltpu.CompilerParams(has_side_effects=DATAFLOW),
        name=name,
    )(*srcs, *lands, sems[0], sems[1], *after)
    return list(out[ns:])


def _landed_block(piece, lands, owner):
    _, _, li, axis, base, stride, shape = piece
    return _window(lands[li], axis, base + stride * owner, shape[axis])


def _copy_stage(name, bufs, in_sems, out_sem_sizes, emit, after=()):
    nb, ni, no, na = len(bufs), len(in_sems), len(out_sem_sizes), len(after)

    def body(*refs):
        b_refs = refs[:nb]
        i_refs = refs[nb:nb + ni]
        o_refs = refs[nb + ni + na:nb + ni + na + no]
        emit(b_refs, i_refs, o_refs)
        refs[-1][...] = jnp.zeros_like(refs[-1])

    hbm_of = lambda a: pltpu.HBM(a.shape, a.dtype)
    out = pl.pallas_call(
        body,
        in_specs=[HBM] * nb + [SEM] * ni + [ANY] * na,
        out_specs=[SEM] * no + [HBM] * nb + [pl.BlockSpec(memory_space=pltpu.VMEM)],
        out_shape=[pltpu.SemaphoreType.DMA((n,)) for n in out_sem_sizes] + [hbm_of(a) for a in bufs]
        + [jax.ShapeDtypeStruct((8, LANES), F32)],
        input_output_aliases={i: no + i for i in range(nb)},
        compiler_params=pltpu.CompilerParams(has_side_effects=DATAFLOW),
        name=name,
    )(*[pltpu.with_memory_space_constraint(a, pltpu.HBM) for a in bufs], *in_sems, *after)
    return list(out[:no]), list(out[no:no + nb]), out[-1]


def _remote(src, dst, send, recv, to):
    return pltpu.make_async_remote_copy(src_ref=src, dst_ref=dst, send_sem=send, recv_sem=recv, device_id=to,
                                        device_id_type=MESH_ID)


def _routed_gather(srcs, lands, pieces, after, name):
    ns, npc = len(srcs), len(pieces)

    def places():
        x, y, c = _mesh_pos()
        index = lambda p: 4 * p[0] + 2 * p[1] + p[2]
        me, sib = (x, y, c), (x, y, 1 - c)
        xnb, ynb = (1 - x, y, c), (x, 1 - y, c)
        got_first = (x ^ (1 - c), y ^ c, c)
        pass_to = (x ^ c, y ^ (1 - c), c)
        diag = (1 - x, 1 - y, c)
        return index, me, sib, xnb, ynb, got_first, pass_to, diag

    def start(b, _, o):
        index, me, sib, xnb, ynb, *_rest = places()
        send_a, recv_sib, recv_nb = o
        for p, piece in enumerate(pieces):
            src, dst = _piece_refs(piece, b[:ns], b[ns:], index(me), 0)
            _remote(src, dst, send_a.at[3 * p], recv_sib.at[p], sib).start()
            _remote(src, dst, send_a.at[3 * p + 1], recv_nb.at[2 * p], xnb).start()
            _remote(src, dst, send_a.at[3 * p + 2], recv_nb.at[2 * p + 1], ynb).start()

    def pass_a(b, i, o):
        index, me, sib, xnb, ynb, got_first, pass_to, _diag = places()
        (recv_nb,) = i
        send_f, recv_f, send_d, recv_d = o
        for p, piece in enumerate(pieces):
            for j, nb in enumerate((xnb, ynb)):
                blk = _landed_block(piece, b, index(nb))
                _remote(blk, blk, send_f.at[2 * p + j], recv_nb.at[2 * p + j], sib).wait_recv()
                _remote(blk, blk, send_f.at[2 * p + j], recv_f.at[2 * p + j], sib).start()
            blk = _landed_block(piece, b, index(got_first))
            _remote(blk, blk, send_d.at[p], recv_d.at[p], pass_to).start()

    def pass_b(b, i, o):
        index, me, sib, *_mid, diag = places()
        (recv_d,) = i
        send_g, recv_g = o
        for p, piece in enumerate(pieces):
            blk = _landed_block(piece, b, index(diag))
            _remote(blk, blk, send_g.at[p], recv_d.at[p], sib).wait_recv()
            _remote(blk, blk, send_g.at[p], recv_g.at[p], sib).start()

    def last(b, i, _):
        index, me, sib, *_others = places()
        send_a, recv_sib, send_f, recv_f, send_d, send_g, recv_g = i
        for p, piece in enumerate(pieces):
            src, dst = _piece_refs(piece, b[:ns], b[ns:], index(me), 0)
            cp = lambda s_sem, r_sem: _remote(src, dst, s_sem, r_sem, sib)
            cp(send_a.at[3 * p], recv_sib.at[p]).wait_recv()
            cp(send_a.at[3 * p], recv_g.at[p]).wait_recv()
            for j in range(3):
                cp(send_a.at[3 * p + j], recv_sib.at[p]).wait_send()
            for j in range(2):
                cp(send_f.at[2 * p + j], recv_f.at[2 * p + j]).wait_recv()
                cp(send_f.at[2 * p + j], recv_f.at[2 * p + j]).wait_send()
            cp(send_d.at[p], recv_sib.at[p]).wait_send()
            cp(send_g.at[p], recv_sib.at[p]).wait_send()

    (send_a, recv_sib, recv_nb), bufs, _ = _copy_stage(name + "_start", list(srcs) + list(lands), [],
                                                       [3 * npc, npc, 2 * npc], start)
    srcs, lands = bufs[:ns], bufs[ns:]
    (send_f, recv_f, send_d, recv_d), lands, _ = _copy_stage(name + "_pass_a", lands, [recv_nb],
                                                             [2 * npc, 2 * npc, npc, npc],
                                                             lambda b, i, o: pass_a(b, i, o), after=after)
    (send_g, recv_g), lands, tok = _copy_stage(name + "_pass_b", lands, [recv_d], [npc, npc], pass_b)
    _, bufs, _ = _copy_stage(name + "_last", list(srcs) + list(lands),
                             [send_a, recv_sib, send_f, recv_f, send_d, send_g, recv_g], [], last)
    return bufs[ns:], tok


_SMALL_NAMES = ("ln1_g", "ln1_b", "hg_lb_logits", "hg_norm_g", "sg_ln_g", "sg_ln_b", "sg_w_s", "sg_b_s",
                "ln2_g", "ln2_b", "mem_ln_g", "mem_ln_b", "ln3_g", "ln3_b", "ln4_g", "ln4_b")


_VEC_NAMES = ("ln1_g", "ln1_b", "ln2_g", "ln2_b", "mem_ln_g", "mem_ln_b", "ln3_g", "ln3_b", "ln4_g", "ln4_b")
_ROW_NAMES = ("hg_lb_logits", "hg_norm_g", "sg_ln_g", "sg_ln_b", "sg_b_s", "sg_w_s")
VEC_ROWS = 16


def _row_plan(shapes):
    plan, pos = {}, 0
    for k in _ROW_NAMES:
        shp = shapes[k]
        slabs, off = [], pos
        for idx in itertools.product(*[range(dim) for dim in shp[:-2]]):
            slabs.append((idx, off, shp[-2]))
            off += shp[-2]
        plan[k] = (pos, slabs)
        pos = -(-off // 8) * 8
    return plan, -(-pos // 16) * 16


def _pack_small_grads(gs, shapes, loss):
    d = gs[_VEC_NAMES[0]].size
    vec = jnp.concatenate([gs[k].reshape(1, -1) for k in _VEC_NAMES] + [jnp.tile(loss, (1, d // LANES))], axis=0)
    vec = jnp.pad(vec, ((0, VEC_ROWS - vec.shape[0]), (0, 0)))
    plan, total = _row_plan(shapes)
    parts, pos = [], 0
    for k in _ROW_NAMES:
        first, slabs = plan[k]
        rows = gs[k].reshape(-1, LANES)
        end = slabs[-1][1] + slabs[-1][2]
        nxt = -(-end // 8) * 8
        parts.append(jnp.pad(rows, ((0, nxt - first - rows.shape[0]), (0, 0))))
        pos = nxt
    parts.append(jnp.zeros((total - pos, LANES), F32))
    return vec, jnp.concatenate(parts, axis=0)


def _adam_small(land_vec, land_rows, w, m, v):
    names = _VEC_NAMES + _ROW_NAMES
    n = len(names)
    shapes = {k: w[k].shape for k in names}
    plan, _ = _row_plan(shapes)

    def body(*refs):
        lv_ref, lr_ref = refs[:2]
        w_refs, m_refs, v_refs = refs[2:2 + n], refs[2 + n:2 + 2 * n], refs[2 + 2 * n:2 + 3 * n]
        outs = refs[2 + 3 * n:2 + 7 * n]
        loss_ref = refs[2 + 7 * n]
        gv_s, gr_s = refs[3 + 7 * n:]
        gv_s[...] = _slot_sum(lv_ref)
        gr_s[...] = _slot_sum(lr_ref)
        loss_ref[...] = gv_s[len(_VEC_NAMES):len(_VEC_NAMES) + 1, :LANES]
        for p, k in enumerate(names):
            if k in _VEC_NAMES:
                row = _VEC_NAMES.index(k)
                slabs = [((), None, None)]
            else:
                slabs = plan[k][1]
            for idx, off, rows in slabs:
                g = gv_s[row:row + 1, :] if off is None else gr_s[off:off + rows, :]
                sel = idx + (slice(None), slice(None))
                delta, nm, nv = _adamw(w_refs[p][sel], g, m_refs[p][sel], v_refs[p][sel])
                for o, val in zip(range(4), (g, delta, nm, nv)):
                    outs[o * n + p][sel] = val

    flat = lambda tree: [tree[k] for k in names]
    shp = [jax.ShapeDtypeStruct(shapes[k], F32) for k in names]
    out = pl.pallas_call(
        body,
        out_shape=shp * 4 + [jax.ShapeDtypeStruct((1, LANES), F32)],
        scratch_shapes=[pltpu.VMEM(land_vec.shape[1:], F32), pltpu.VMEM(land_rows.shape[1:], F32)],
        name="adam_small",
    )(land_vec, land_rows, *flat(w), *flat(m), *flat(v))
    return [dict(zip(names, out[o * n:(o + 1) * n])) for o in range(4)], out[4 * n]


_COL_FFN = ("ffn1_w_gate", "ffn1_w_up", "ffn2_w_gate", "ffn2_w_up")
_ROW_FFN = ("ffn1_w_down", "ffn2_w_down")
_ROW_SQ = ("w_out", "xa_w_q", "xa_w_k", "xa_w_v", "xa_w_o")
_BIG_NAMES = ("ffn1_w_gate", "ffn1_w_up", "ffn1_w_down", "w_in", "w_out", "xa_w_q", "xa_w_k", "xa_w_v", "xa_w_o",
              "ffn2_w_gate", "ffn2_w_up", "ffn2_w_down")


def _ffn_split(fs):
    main = (fs // MXU_WIDTH_V7X) * MXU_WIDTH_V7X
    tail = fs - main
    tail_pad = -(-tail // LANES) * LANES
    assert main > 0 and tail > 0
    return main, tail, tail_pad


def _layout(name, shard_shape):
    r, c = shard_shape
    if name in _COL_FFN:
        main, tail, pad = _ffn_split(c)
        return (r, N_DEV * (main + pad)), [(1, 0, main, (r, main), (0, main)),
                                           (1, N_DEV * main, pad, (r, pad), (main, c))]
    if name in _ROW_FFN:
        main, tail, pad = _ffn_split(r)
        return (N_DEV * (main + pad), c), [(0, 0, main, (main, c), (0, main)),
                                           (0, N_DEV * main, pad, (pad, c), (main, r))]
    if name == "w_in":
        return (r, N_DEV * c), [(1, 0, c, (r, c), (0, c))]
    return (N_DEV * r, c), [(0, 0, r, (r, c), (0, r))]


def _shard_pieces(name, shard):
    out = []
    for axis, _, _, shape, (lo, hi) in _layout(name, shard.shape)[1]:
        part = shard[lo:hi, :] if axis == 0 else shard[:, lo:hi]
        pad = [(0, shape[0] - part.shape[0]), (0, shape[1] - part.shape[1])]
        out.append(jnp.pad(part, pad).astype(BF16))
    return out


def _gather_plan(names, shards):
    srcs, land_shapes, pieces, index = [], [], [], {}
    for li, name in enumerate(names):
        shape2d, parts = _layout(name, shards[name].shape)
        land_shapes.append(jax.ShapeDtypeStruct(shape2d, BF16))
        index[name] = []
        for (axis, base, stride, shape, _), src in zip(parts, _shard_pieces(name, shards[name])):
            index[name].append(len(pieces))
            pieces.append(("gather", len(srcs), li, axis, base, stride, shape))
            srcs.append(src)
    return srcs, land_shapes, pieces, index


def _scatter_plan(names, grads, shard_shapes):
    srcs, land_shapes, pieces, index = [], [], [], {}
    for si, name in enumerate(names):
        _, parts = _layout(name, shard_shapes[name])
        srcs.append(grads[name])
        index[name] = []
        for axis, base, stride, shape, _ in parts:
            index[name].append(len(land_shapes))
            pieces.append(("scatter", si, len(land_shapes), axis, base, stride, shape))
            land_shapes.append(jax.ShapeDtypeStruct((N_DEV,) + shape, grads[name].dtype))
    return srcs, land_shapes, pieces, index


def _small_views(small):
    row = lambda a: a.reshape(1, -1)
    ln = {k: row(small[k]) for k in ("ln1_g", "ln1_b", "ln2_g", "ln2_b", "ln3_g", "ln3_b", "ln4_g", "ln4_b",
                                      "mem_ln_g", "mem_ln_b", "hg_norm_g")}
    sg_w = small["sg_w_s"].reshape(SG_GROUPS, SG_CHUNK, SG_CHUNK)
    sg = dict(logits=jnp.swapaxes(small["hg_lb_logits"], 0, 1),
              g=small["sg_ln_g"].reshape(SG_GROUPS, 1, SG_DIM), b=small["sg_ln_b"].reshape(SG_GROUPS, 1, SG_DIM),
              w=sg_w, wt=jnp.swapaxes(sg_w, 1, 2), bs=small["sg_b_s"].reshape(SG_GROUPS, SG_CHUNK, 1))
    return ln, sg


def _forward(x, mem, target, get_w, small, first_deps=()):
    ln, sg = _small_views(small)
    xb = x.astype(BF16)
    a1, b1, s1 = _ffn_up(xb, get_w("ffn1_w_gate", ()), get_w("ffn1_w_up", ()), "ffn1_up", deps=first_deps)
    h1b, xh1, rs1 = _mm_res_ln(s1, get_w("ffn1_w_down", (s1,)), x, ln["ln1_g"], ln["ln1_b"], 0.5, "ffn1_down_ln")
    proj = _mm_nn(h1b, get_w("w_in", (h1b,)), "mix_in")
    oraw, mix, states = _hgrn_fwd(proj, sg["logits"], ln["hg_norm_g"])
    mix = _sgu_fwd(proj, mix, sg["g"], sg["b"], sg["w"], sg["bs"])
    h2b, xh2, rs2 = _mm_res_ln(mix, get_w("w_out", (mix,)), (xh1, ln["ln1_g"], ln["ln1_b"]), ln["ln2_g"], ln["ln2_b"],
                               1.0, "mix_out_ln")
    mb, mxh, mrs, kb, vb = _mem_kv(mem, ln["mem_ln_g"], ln["mem_ln_b"], get_w("xa_w_k", (h2b,)), get_w("xa_w_v", (h2b,)))
    qb, att = _attn_fwd(h2b, get_w("xa_w_q", (kb,)), kb, vb)
    h3b, xh3, rs3 = _mm_res_ln(att, get_w("xa_w_o", (att,)), (xh2, ln["ln2_g"], ln["ln2_b"]), ln["ln3_g"], ln["ln3_b"],
                               1.0, "attn_out_ln")
    a2, b2, s2 = _ffn_up(h3b, get_w("ffn2_w_gate", (h3b,)), get_w("ffn2_w_up", (h3b,)), "ffn2_up")
    loss, dy4, dy4b, dg4, db4 = _mm_res_ln(s2, get_w("ffn2_w_down", (s2,)), (xh3, ln["ln3_g"], ln["ln3_b"]),
                                           ln["ln4_g"], ln["ln4_b"], 0.5, "ffn2_down_ln_loss", target=target)
    return dict(xb=xb, a1=a1, b1=b1, s1=s1, h1b=h1b, xh1=xh1, rs1=rs1, proj=proj, oraw=oraw, mix=mix, states=states,
                h2b=h2b, xh2=xh2, rs2=rs2, mb=mb, mxh=mxh, mrs=mrs, kb=kb, vb=vb, qb=qb, att=att, h3b=h3b, xh3=xh3,
                rs3=rs3, a2=a2, b2=b2, s2=s2, loss=loss, dy4=dy4, dy4b=dy4b, dg4=dg4, db4=db4)


def _backward(sv, wt, small, send):
    ln, sg = _small_views(small)
    gs = {"ln4_g": sv["dg4"], "ln4_b": sv["db4"]}
    loss, dy4, dy4b = sv["loss"], sv["dy4"], sv["dy4b"]
    g_down2 = _mm_tn(sv["s2"], dy4b, "g_ffn2_down", scale=0.5)
    da2, db2, dy3, dy3b, gs["ln3_g"], gs["ln3_b"] = _ffn_bwd_fused(
        dy4, dy4b, wt["ffn2_w_down"], wt["ffn2_w_gate"], wt["ffn2_w_up"], sv["a2"], sv["b2"], 0.5,
        (sv["xh3"], sv["rs3"], ln["ln3_g"]), "ffn2_bwd")
    g_gate2 = _mm_tn(sv["h3b"], da2, "g_ffn2_gate")
    g_up2 = _mm_tn(sv["h3b"], db2, "g_ffn2_up")
    tok = send(("ffn2_w_down", "ffn2_w_gate", "ffn2_w_up"), (g_down2, g_gate2, g_up2))

    g_o = _mm_tn(sv["att"], dy3b, "g_xa_o", deps=(tok,))
    dqb, dk, dv = _attn_bwd(dy3b, wt["xa_w_o"], sv["qb"], sv["kb"], sv["vb"])
    g_q = _mm_tn(sv["h2b"], dqb, "g_xa_q")
    g_k, g_v, gs["mem_ln_g"], gs["mem_ln_b"] = _mem_bwd(dk, dv, sv["mb"], sv["mxh"], sv["mrs"], ln["mem_ln_g"],
                                                        wt["xa_w_k"], wt["xa_w_v"])
    tok = send(("xa_w_o", "xa_w_q", "xa_w_k", "xa_w_v"), (g_o, g_q, g_k, g_v))
    dy2, dy2b, gs["ln2_g"], gs["ln2_b"] = _dx_ln(dy3, [(dqb, wt["xa_w_q"])], (sv["xh2"], sv["rs2"], ln["ln2_g"]),
                                                 "attn_dx_ln", deps=(tok,))

    g_out = _mm_tn(sv["mix"], dy2b, "g_w_out")
    dmix = _mm_nt(dy2b, wt["w_out"], "mix_out_bwd")
    dq, dfz, div, dgg, dlg, dgn = _hgrn_bwd(sv["proj"], sv["oraw"], dmix, sv["states"], sg["logits"], ln["hg_norm_g"])
    du, dvv, gs["sg_ln_g"], gs["sg_ln_b"], gs["sg_w_s"], gs["sg_b_s"] = _sgu_bwd(
        sv["proj"], dmix, sg["g"], sg["b"], sg["w"], sg["wt"], sg["bs"])
    gs["hg_lb_logits"] = jnp.swapaxes(dlg, 0, 1)
    gs["hg_norm_g"] = jnp.sum(dgn, axis=0)
    dproj = jnp.concatenate([dq, dfz, div, dgg, du, dvv], axis=1)
    g_in = _mm_tn(sv["h1b"], dproj, "g_w_in")
    tok = send(("w_out", "w_in"), (g_out, g_in))
    dy1, dy1b, gs["ln1_g"], gs["ln1_b"] = _dx_ln(dy2, [(dproj, wt["w_in"])], (sv["xh1"], sv["rs1"], ln["ln1_g"]),
                                                 "mix_dx_ln", deps=(tok,))

    g_down1 = _mm_tn(sv["s1"], dy1b, "g_ffn1_down", scale=0.5)
    tok = send(("ffn1_w_down",), (g_down1,))
    da1, db1 = _ffn_bwd_act(dy1b, wt["ffn1_w_down"], sv["a1"], sv["b1"], 0.5, "ffn1_bwd_act", deps=(tok,))
    g_gate1 = _mm_tn(sv["xb"], da1, "g_ffn1_gate")
    tok = send(("ffn1_w_gate",), (g_gate1,))
    g_up1 = _mm_tn(sv["xb"], db1, "g_ffn1_up", deps=(tok,))
    tok = send(("ffn1_w_up",), (g_up1,))
    grad_x = _dx_ln(dy1, [(da1, wt["ffn1_w_gate"]), (db1, wt["ffn1_w_up"])], None, "ffn1_dx", deps=(tok,))
    return loss, grad_x, gs


_WEIGHT_NAMES = ("ffn1_w_gate", "ffn1_w_up", "ffn1_w_down", "ln1_g", "ln1_b", "w_in", "hg_lb_logits", "hg_norm_g",
                 "sg_ln_g", "sg_ln_b", "sg_w_s", "sg_b_s", "w_out", "ln2_g", "ln2_b", "mem_ln_g", "mem_ln_b",
                 "xa_w_q", "xa_w_k", "xa_w_v", "xa_w_o", "ln3_g", "ln3_b", "ffn2_w_gate", "ffn2_w_up", "ffn2_w_down",
                 "ln4_g", "ln4_b")
_FIRST = ("ffn1_w_gate", "ffn1_w_up")
_SECOND = ("ffn1_w_down", "w_in", "w_out")
_THIRD = ("xa_w_k", "xa_w_v", "xa_w_q", "xa_w_o", "ffn2_w_gate", "ffn2_w_up", "ffn2_w_down")


def kernel(x, mem, ffn1_w_gate, ffn1_w_up, ffn1_w_down, ln1_g, ln1_b, w_in, hg_lb_logits, hg_norm_g, sg_ln_g, sg_ln_b, sg_w_s, sg_b_s, w_out, ln2_g, ln2_b, mem_ln_g, mem_ln_b, xa_w_q, xa_w_k, xa_w_v, xa_w_o, ln3_g, ln3_b, ffn2_w_gate, ffn2_w_up, ffn2_w_down, ln4_g, ln4_b, loss_target, m_ffn1_w_gate, m_ffn1_w_up, m_ffn1_w_down, m_ln1_g, m_ln1_b, m_w_in, m_hg_lb_logits, m_hg_norm_g, m_sg_ln_g, m_sg_ln_b, m_sg_w_s, m_sg_b_s, m_w_out, m_ln2_g, m_ln2_b, m_mem_ln_g, m_mem_ln_b, m_xa_w_q, m_xa_w_k, m_xa_w_v, m_xa_w_o, m_ln3_g, m_ln3_b, m_ffn2_w_gate, m_ffn2_w_up, m_ffn2_w_down, m_ln4_g, m_ln4_b, v_ffn1_w_gate, v_ffn1_w_up, v_ffn1_w_down, v_ln1_g, v_ln1_b, v_w_in, v_hg_lb_logits, v_hg_norm_g, v_sg_ln_g, v_sg_ln_b, v_sg_w_s, v_sg_b_s, v_w_out, v_ln2_g, v_ln2_b, v_mem_ln_g, v_mem_ln_b, v_xa_w_q, v_xa_w_k, v_xa_w_v, v_xa_w_o, v_ln3_g, v_ln3_b, v_ffn2_w_gate, v_ffn2_w_up, v_ffn2_w_down, v_ln4_g, v_ln4_b):
    args = dict(locals())
    w = {k: args[k] for k in _WEIGHT_NAMES}
    m = {k: args["m_" + k] for k in _WEIGHT_NAMES}
    v = {k: args["v_" + k] for k in _WEIGHT_NAMES}
    shards = {k: w[k][0] for k in _BIG_NAMES}
    shard_shapes = {k: shards[k].shape for k in _BIG_NAMES}
    small = {k: (w[k][0] if k != "hg_lb_logits" else w[k]) for k in _SMALL_NAMES}

    srcs1, shapes1, pieces1, idx1 = _gather_plan(_FIRST, shards)
    lands1 = _place_own(srcs1, shapes1, pieces1, "gather_first_own")
    rest = _SECOND + _THIRD
    srcs2, shapes2, pieces2, idx2 = _gather_plan(rest, shards)
    lands2 = _place_own(srcs2, shapes2, pieces2, "gather_rest_own")
    groups2 = [list(idx2[k]) for k in rest]
    lands1, tok1 = _routed_gather(srcs1, lands1, pieces1, tuple(lands2), "gather_first")
    sems2, srcs2, lands2, tok2 = _comm_start(srcs2, lands2, pieces2, groups2, "gather_rest_start", after=(tok1,))
    wt = dict(zip(_FIRST, lands1))
    pending = {k: gi for gi, k in enumerate(rest)}

    def get_w(name, after):
        if name in pending:
            gi = pending.pop(name)
            si = [pieces2[p][1] for p in groups2[gi]]
            sub = [(pieces2[p][0], row, 0) + pieces2[p][3:] for row, p in enumerate(groups2[gi])]
            wt[name] = _comm_wait([srcs2[s] for s in si], [lands2[gi]], sub, list(range(len(sub))), sems2[gi],
                                  after, "gather_wait_" + name)[0]
        return wt[name]

    sv = _forward(x[0], mem[0], loss_target[0], get_w, small, first_deps=(tok2,))

    sent = []

    def send(names, grads):
        srcs, shapes, pieces, idx = _scatter_plan(names, dict(zip(names, grads)), shard_shapes)
        lands = _place_own(srcs, shapes, pieces, "grads_own_%d" % len(sent))
        sems, srcs, lands, tok = _comm_start(srcs, lands, pieces, [list(range(len(pieces)))],
                                             "grads_start_%d" % len(sent))
        sent.append((names, srcs, lands, pieces, idx, sems[0]))
        return tok

    loss, grad_x, gs = _backward(sv, wt, small, send)

    ssrc = list(_pack_small_grads(gs, {k: w[k].shape for k in _SMALL_NAMES}, loss))
    sp = [("scatter", i, i, 0, 0, 0, a.shape) for i, a in enumerate(ssrc)]
    sshape = [jax.ShapeDtypeStruct((N_DEV,) + a.shape, F32) for a in ssrc]
    sl = _place_own(ssrc, sshape, sp, "small_own")
    ssem, ssrc, sl, _ = _comm_start(ssrc, sl, sp, [[0, 1]], "small_start")

    out_g, out_d, out_m, out_v = {}, {}, {}, {}
    after = (grad_x,)
    for n_sent, (names, srcs, lands, pieces, idx, sems) in enumerate(sent):
        lands = _comm_wait(srcs, lands, pieces, list(range(len(pieces))), sems, after, "grads_wait_%d" % n_sent)
        for k in names:
            axis = 1 if (k in _COL_FFN or k == "w_in") else 0
            if k in _COL_FFN:
                res = _adam_sharded([lands[i] for i in idx[k]], w[k][0].T, m[k][0].T, v[k][0].T, axis, "adam_" + k)
                res = [r.T for r in res]
            else:
                res = _adam_sharded([lands[i] for i in idx[k]], w[k][0], m[k][0], v[k][0], axis, "adam_" + k)
            out_g[k], out_d[k], out_m[k], out_v[k] = [r[None] for r in res]
        after = (out_v[names[-1]],)
    sl = _comm_wait(ssrc, sl, sp, [0, 1], ssem[0], after, "small_wait")
    small_out, loss_sum = _adam_small(sl[0], sl[1], w, m, v)
    for dst, res in zip((out_g, out_d, out_m, out_v), small_out):
        dst.update(res)
    loss_all = loss_sum[0, 0]
    return (loss_all, grad_x[None], *[out_g[k] for k in _WEIGHT_NAMES], *[out_d[k] for k in _WEIGHT_NAMES],
            *[out_m[k] for k in _WEIGHT_NAMES], *[out_v[k] for k in _WEIGHT_NAMES])
```

```python
import itertools

import jax
import jax.numpy as jnp
import numpy as np
from jax import lax
from jax.experimental import pallas as pl
from jax.experimental.pallas import tpu as pltpu

F32 = jnp.float32
BF16 = jnp.bfloat16

N_DEV = 8
ALPHA = 2.0 ** 0.25
LN_EPS = 1e-5
HG_HEADS = 4
HG_DIM = 128
SG_GROUPS = 4
SG_DIM = 128
SG_CHUNK = 128
X_HEADS = 4
HG_BLOCK = 16
HG_UNROLL = 8
ADAM_LR = 0.001
ADAM_B1 = 0.9
ADAM_B2 = 0.999
ADAM_EPS = 1e-08
ADAM_WD = 0.01
ADAM_STEP = 10
VMEM_LIMIT_V7X = 48 * 1024 * 1024
MXU_WIDTH_V7X = 256
LANES = 128
MESH_ID = pl.DeviceIdType.MESH
ANY = pl.BlockSpec(memory_space=pl.ANY)
HBM = pl.BlockSpec(memory_space=pltpu.HBM)
SEM = pl.BlockSpec(memory_space=pltpu.SEMAPHORE)
DATAFLOW = pltpu.SideEffectType.DATAFLOW_SIDE_EFFECTING


def _params(n_axes):
    return pltpu.CompilerParams(dimension_semantics=("arbitrary",) * n_axes, vmem_limit_bytes=VMEM_LIMIT_V7X)


def _dot(a, b):
    return jnp.dot(a, b, preferred_element_type=F32)


def _dot_nt(a, b):
    return lax.dot_general(a, b, (((1,), (1,)), ((), ())), preferred_element_type=F32)


def _dot_tn(a, b):
    return lax.dot_general(a, b, (((0,), (0,)), ((), ())), preferred_element_type=F32)


def _sigmoid(x):
    return 1.0 / (1.0 + jnp.exp(-x))


def _silu_and_grad(a):
    sig = _sigmoid(a)
    return a * sig, sig * (1.0 + a * (1.0 - sig))


_GELU_C = 0.7978845608028654


def _gelu_and_grad(x):
    inner = _GELU_C * (x + 0.044715 * x * x * x)
    t = jnp.tanh(inner)
    val = 0.5 * x * (1.0 + t)
    grad = 0.5 * (1.0 + t) + 0.5 * x * (1.0 - t * t) * _GELU_C * (1.0 + 3.0 * 0.044715 * x * x)
    return val, grad


def _ln_fwd(y, g, b):
    mu = jnp.mean(y, axis=-1, keepdims=True)
    yc = y - mu
    var = jnp.mean(yc * yc, axis=-1, keepdims=True)
    rstd = lax.rsqrt(var + LN_EPS)
    xhat = yc * rstd
    return xhat * g + b, xhat, rstd


def _ln_bwd(dh, xhat, rstd, g):
    dxh = dh * g
    m1 = jnp.mean(dxh, axis=-1, keepdims=True)
    m2 = jnp.mean(dxh * xhat, axis=-1, keepdims=True)
    dy = rstd * (dxh - m1 - xhat * m2)
    dg = jnp.sum(dh * xhat, axis=0, keepdims=True)
    db = jnp.sum(dh, axis=0, keepdims=True)
    return dy, dg, db


def _mask_dot(mask, x):
    hi = x.astype(BF16)
    lo = (x - hi.astype(F32)).astype(BF16)
    n = mask.shape[0]
    parts = [_dot(mask, hi[r:r + n, :]) + _dot(mask, lo[r:r + n, :]) for r in range(0, x.shape[0], n)]
    return parts[0] if len(parts) == 1 else jnp.concatenate(parts, axis=0)


def _block_masks(n):
    r = np.arange(n)[:, None]
    c = np.arange(n)[None, :]
    same = (r // HG_BLOCK) == (c // HG_BLOCK)
    return jnp.asarray(np.stack([same & (c <= r), same & (c >= r), same]), BF16)


def _row_tile(t):
    return min(t, 512)


def _col_tile(n):
    for cand in (512, 256, 128):
        if n % cand == 0:
            return cand
    return n


def _resident(w):
    return pl.BlockSpec(w.shape, lambda *_: (0, 0), pipeline_mode=pl.Buffered(1))


def _drop_deps(body, n_in, n_deps):
    if n_deps == 0:
        return body
    return lambda *refs: body(*refs[:n_in], *refs[n_in + n_deps:])


def _ffn_up(hb, wg, wu, name, deps=()):
    t, d = hb.shape
    f = wg.shape[1]
    tm = _row_tile(t)
    tn = _col_tile(f)

    def body(h_ref, wg_ref, wu_ref, a_ref, b_ref, s_ref):
        h = h_ref[...]
        for c in range(f // tn):
            cols = slice(c * tn, (c + 1) * tn)
            a = _dot(h, wg_ref[:, cols])
            b = _dot(h, wu_ref[:, cols])
            a_ref[:, cols] = a.astype(BF16)
            b_ref[:, cols] = b.astype(BF16)
            s_ref[:, cols] = (a * _sigmoid(a) * b).astype(BF16)

    act = pl.BlockSpec((tm, f), lambda i: (i, 0))
    return pl.pallas_call(
        _drop_deps(body, 3, len(deps)),
        grid=(t // tm,),
        in_specs=[pl.BlockSpec((tm, d), lambda i: (i, 0)), _resident(wg), _resident(wu)] + [ANY] * len(deps),
        out_specs=[act, act, act],
        out_shape=[jax.ShapeDtypeStruct((t, f), BF16)] * 3,
        compiler_params=_params(1),
        name=name,
    )(hb, wg, wu, *deps)


def _mm_res_ln(lhs, w, res, g, b, coef, name, target=None):
    t, kd = lhs.shape
    d = w.shape[1]
    tm = _row_tile(t)
    nt = t // tm
    from_norm = isinstance(res, tuple)
    n_res = 3 if from_norm else 1

    def body(*refs):
        l_ref, w_ref = refs[:2]
        r_refs = refs[2:2 + n_res]
        g_ref, b_ref = refs[2 + n_res:4 + n_res]
        rest = refs[4 + n_res:]
        prev = r_refs[0][...] * r_refs[1][...] + r_refs[2][...] if from_norm else r_refs[0][...]
        y = ALPHA * prev + coef * _dot(l_ref[...], w_ref[...])
        h, xhat, rstd = _ln_fwd(y, g_ref[...], b_ref[...])
        if target is None:
            hb_ref, xh_ref, rs_ref = rest
            hb_ref[...] = h.astype(BF16)
            xh_ref[...] = xhat
            rs_ref[...] = rstd
            return
        t_ref, loss_ref, dy_ref, dyb_ref, dg_ref, db_ref, lacc = rest
        i = pl.program_id(0)

        @pl.when(i == 0)
        def _():
            lacc[...] = jnp.zeros_like(lacc)
            dg_ref[...] = jnp.zeros_like(dg_ref)
            db_ref[...] = jnp.zeros_like(db_ref)

        err = h - t_ref[...]
        lacc[...] += jnp.sum(err * err, axis=0, keepdims=True)
        dy, dg, db = _ln_bwd(err * (1.0 / d), xhat, rstd, g_ref[...])
        dy_ref[...] = dy
        dyb_ref[...] = dy.astype(BF16)
        dg_ref[...] += dg
        db_ref[...] += db

        @pl.when(i == nt - 1)
        def _():
            loss_ref[...] = jnp.zeros_like(loss_ref) + jnp.sum(lacc[...], axis=1, keepdims=True) * (0.5 / d)

    row = pl.BlockSpec((tm, d), lambda i: (i, 0))
    vec = pl.BlockSpec((1, d), lambda i: (0, 0))
    res_specs = [row, vec, vec] if from_norm else [row]
    res_args = list(res) if from_norm else [res]
    in_specs = [pl.BlockSpec((tm, kd), lambda i: (i, 0)), _resident(w)] + res_specs + [vec, vec]
    args = [lhs, w] + res_args + [g, b]
    if target is None:
        out_specs = [row, row, pl.BlockSpec((tm, 1), lambda i: (i, 0))]
        out_shape = [jax.ShapeDtypeStruct((t, d), BF16), jax.ShapeDtypeStruct((t, d), F32),
                     jax.ShapeDtypeStruct((t, 1), F32)]
        scratch = []
    else:
        in_specs.append(row)
        args.append(target)
        out_specs = [pl.BlockSpec((1, LANES), lambda i: (0, 0)), row, row, vec, vec]
        out_shape = [jax.ShapeDtypeStruct((1, LANES), F32), jax.ShapeDtypeStruct((t, d), F32),
                     jax.ShapeDtypeStruct((t, d), BF16), jax.ShapeDtypeStruct((1, d), F32),
                     jax.ShapeDtypeStruct((1, d), F32)]
        scratch = [pltpu.VMEM((1, d), F32)]
    return pl.pallas_call(
        body,
        grid=(nt,),
        in_specs=in_specs,
        out_specs=out_specs,
        out_shape=out_shape,
        scratch_shapes=scratch,
        compiler_params=_params(1),
        name=name,
    )(*args)


def _mm_nn(lhs, w, name):
    t, kd = lhs.shape
    n = w.shape[1]
    tm = _row_tile(t)
    tn = _col_tile(n)

    def body(l_ref, w_ref, o_ref):
        lhs_v = l_ref[...]
        for c in range(n // tn):
            cols = slice(c * tn, (c + 1) * tn)
            o_ref[:, cols] = _dot(lhs_v, w_ref[:, cols])

    return pl.pallas_call(
        body,
        grid=(t // tm,),
        in_specs=[pl.BlockSpec((tm, kd), lambda i: (i, 0)), _resident(w)],
        out_specs=pl.BlockSpec((tm, n), lambda i: (i, 0)),
        out_shape=jax.ShapeDtypeStruct((t, n), F32),
        compiler_params=_params(1),
        name=name,
    )(lhs, w)


def _lower_bound(lg):
    m = jnp.max(lg, axis=0, keepdims=True)
    e = jnp.exp(lg - m)
    return e[0:1, :] / jnp.sum(e, axis=0, keepdims=True)


def _forget_terms(fz, lb):
    e = jnp.exp(-jnp.abs(fz))
    r = 1.0 / (1.0 + e)
    pos = fz >= 0.0
    sig = jnp.where(pos, r, e * r)
    nsig = jnp.where(pos, e * r, r)
    f = lb + (1.0 - lb) * sig
    k = (1.0 - lb) * nsig
    return sig, nsig, f, k


def _hg_tile(t):
    return min(t, 1024)


HG_HALF = HG_BLOCK // 2
NEG_BIG = -1e30


def _halves(a):
    return a[:HG_HALF, :], a[HG_HALF:, :]


def _causal_halves(s):
    return (0, 1) if s < HG_HALF else (1,)


def _decay_from(b_half, b_s, s, h, tidx):
    first = s - h * HG_HALF
    diff = b_half - b_s
    if first > 0:
        diff = jnp.where(tidx >= first, diff, NEG_BIG)
    return jnp.exp(diff)


def _hgrn_fwd(proj, logits, gn):
    t = proj.shape[0]
    ct = _hg_tile(t)
    nct = t // ct
    nblk = ct // HG_BLOCK
    nh = HG_HEADS
    mrows = min(ct, 256)

    def body(q_ref, fz_ref, iv_ref, gg_ref, lg_ref, gn_ref, mask_ref, oraw_ref, oa_ref, st_ref,
             state, qt_s, kt_s, k_s, b_s, dec_s):
        c = pl.program_id(1)

        @pl.when(c == 0)
        def _():
            state[...] = jnp.zeros_like(state)

        lb = _lower_bound(lg_ref[...])
        q = q_ref[...]
        _, _, f, k = _forget_terms(fz_ref[...], lb)
        logf = jnp.log(f)
        b = _mask_dot(mask_ref[0], logf)
        bend = _mask_dot(mask_ref[2], logf)
        qt_s[...] = (q * jnp.exp(b)).astype(BF16)
        kt_s[...] = (k * jnp.exp(bend - b)).astype(BF16)
        k_s[...] = k
        b_s[...] = b
        dec_s[...] = jnp.exp(bend)
        tidx = lax.broadcasted_iota(jnp.int32, (HG_HALF, HG_DIM), 0)

        def blk(i, carry):
            r0 = pl.multiple_of(i * HG_BLOCK, HG_BLOCK)
            rows = pl.ds(r0, HG_BLOCK)
            st = state[...]
            st_ref[i] = st
            v = iv_ref[rows, :]
            qq = q_ref[rows, :]
            kk = k_s[rows, :]
            bb = b_s[rows, :]
            o = list(_halves(_dot_nt(qt_s[rows, :], st.astype(BF16))))
            qh, bh = _halves(qq), _halves(bb)
            for s in range(HG_BLOCK):
                ks, vs = kk[s:s + 1, :], v[s:s + 1, :]
                for h in _causal_halves(s):
                    e = _decay_from(bh[h], bb[s:s + 1, :], s, h, tidx)
                    acol = jnp.sum(qh[h] * (ks * e), axis=1, keepdims=True)
                    o[h] = o[h] + acol * vs
            oraw_ref[rows, :] = jnp.concatenate(o, axis=0)
            state[...] = st * dec_s[pl.ds(r0, 1), :] + _dot_tn(v.astype(BF16), kt_s[rows, :])
            return carry

        lax.fori_loop(0, nblk, blk, 0, unroll=2 * HG_UNROLL)
        oraw = oraw_ref[...]
        r = lax.rsqrt(jnp.mean(oraw * oraw, axis=-1, keepdims=True) + LN_EPS)
        gg = gg_ref[...]
        oa_ref[...] = (oraw * r * gn_ref[...] * gg * _sigmoid(gg)).astype(BF16)

    def slab(off):
        return pl.BlockSpec((ct, HG_DIM), lambda h, c: (c, off + h))

    out_slab = pl.BlockSpec((ct, HG_DIM), lambda h, c: (c, h))
    return pl.pallas_call(
        body,
        grid=(nh, nct),
        in_specs=[slab(0), slab(nh), slab(2 * nh), slab(3 * nh),
                  pl.BlockSpec((None, 2, HG_DIM), lambda h, c: (h, 0, 0)),
                  pl.BlockSpec((1, HG_DIM), lambda h, c: (0, 0)),
                  pl.BlockSpec((3, mrows, mrows), lambda h, c: (0, 0, 0))],
        out_specs=[out_slab, out_slab, pl.BlockSpec((None, nblk, HG_DIM, HG_DIM), lambda h, c: (h, c, 0, 0))],
        out_shape=[jax.ShapeDtypeStruct((t, nh * HG_DIM), F32),
                   jax.ShapeDtypeStruct((t, (nh + SG_GROUPS) * HG_DIM), BF16),
                   jax.ShapeDtypeStruct((nh, t // HG_BLOCK, HG_DIM, HG_DIM), F32)],
        scratch_shapes=[pltpu.VMEM((HG_DIM, HG_DIM), F32), pltpu.VMEM((ct, HG_DIM), BF16),
                        pltpu.VMEM((ct, HG_DIM), BF16), pltpu.VMEM((ct, HG_DIM), F32),
                        pltpu.VMEM((ct, HG_DIM), F32), pltpu.VMEM((ct, HG_DIM), F32)],
        compiler_params=_params(2),
        name="hgrn_fwd",
    )(proj, proj, proj, proj, logits, gn, _block_masks(mrows))


def _sg_tile(t):
    return min(t, 512)


def _sgu_chunk_fwd(u, v, ln_g, ln_b, wm, bs):
    ua, dua = _gelu_and_grad(u)
    va, dva = _gelu_and_grad(v)
    vn, xhat, rstd = _ln_fwd(va, ln_g, ln_b)
    s = _dot(wm, vn.astype(BF16)) + bs
    return ua, dua, dva, vn, xhat, rstd, s


def _tril_weight(w_ref):
    n = SG_CHUNK
    r = lax.broadcasted_iota(jnp.int32, (n, n), 0)
    c = lax.broadcasted_iota(jnp.int32, (n, n), 1)
    return jnp.where(c <= r, w_ref[...], 0.0)


def _sgu_fwd(proj, mix, ln_g, ln_b, w_s, b_col):
    t = proj.shape[0]
    ct = _sg_tile(t)
    ng = SG_GROUPS
    off_u = 4 * HG_HEADS
    off_v = off_u + ng

    def body(u_ref, v_ref, g_ref, b_ref, w_ref, bs_ref, mix_ref, o_ref):
        del mix_ref
        wm = _tril_weight(w_ref).astype(BF16)
        for n in range(ct // SG_CHUNK):
            rows = slice(n * SG_CHUNK, (n + 1) * SG_CHUNK)
            ua, _, _, _, _, _, s = _sgu_chunk_fwd(u_ref[rows, :], v_ref[rows, :], g_ref[...], b_ref[...], wm, bs_ref[...])
            o_ref[rows, :] = (ua * s).astype(BF16)

    vec = pl.BlockSpec((None, 1, SG_DIM), lambda g, c: (g, 0, 0))
    return pl.pallas_call(
        body,
        grid=(ng, t // ct),
        in_specs=[pl.BlockSpec((ct, SG_DIM), lambda g, c: (c, off_u + g)),
                  pl.BlockSpec((ct, SG_DIM), lambda g, c: (c, off_v + g)), vec, vec,
                  pl.BlockSpec((None, SG_CHUNK, SG_CHUNK), lambda g, c: (g, 0, 0)),
                  pl.BlockSpec((None, SG_CHUNK, 1), lambda g, c: (g, 0, 0)), ANY],
        out_specs=pl.BlockSpec((ct, SG_DIM), lambda g, c: (c, HG_HEADS + g)),
        out_shape=jax.ShapeDtypeStruct(mix.shape, mix.dtype),
        input_output_aliases={6: 0},
        compiler_params=_params(2),
        name="sgu_fwd",
    )(proj, proj, ln_g, ln_b, w_s, b_col, mix)


def _mem_kv(mem, g, b, wk, wv):
    m_len, d = mem.shape

    def body(m_ref, g_ref, b_ref, wk_ref, wv_ref, mb_ref, xh_ref, rs_ref, k_ref, v_ref):
        m, xhat, rstd = _ln_fwd(m_ref[...], g_ref[...], b_ref[...])
        mb = m.astype(BF16)
        mb_ref[...] = mb
        xh_ref[...] = xhat
        rs_ref[...] = rstd
        k_ref[...] = _dot(mb, wk_ref[...]).astype(BF16)
        v_ref[...] = _dot(mb, wv_ref[...]).astype(BF16)

    return pl.pallas_call(
        body,
        out_shape=[jax.ShapeDtypeStruct((m_len, d), BF16), jax.ShapeDtypeStruct((m_len, d), F32),
                   jax.ShapeDtypeStruct((m_len, 1), F32), jax.ShapeDtypeStruct((m_len, d), BF16),
                   jax.ShapeDtypeStruct((m_len, d), BF16)],
        compiler_params=pltpu.CompilerParams(vmem_limit_bytes=VMEM_LIMIT_V7X),
        name="mem_kv",
    )(mem, g, b, wk, wv)


def _softmax_rows(s):
    m = jnp.max(s, axis=-1, keepdims=True)
    p = jnp.exp(s - m)
    return p / jnp.sum(p, axis=-1, keepdims=True)


def _attn_fwd(hb, wq, kb, vb):
    t, d = hb.shape
    tm = _row_tile(t)
    dh = d // X_HEADS
    scale = dh ** -0.5

    def body(h_ref, wq_ref, k_ref, v_ref, q_ref, o_ref):
        q = _dot(h_ref[...], wq_ref[...]).astype(BF16)
        q_ref[...] = q
        for hd in range(X_HEADS):
            sl = slice(hd * dh, (hd + 1) * dh)
            p = _softmax_rows(_dot_nt(q[:, sl], k_ref[:, sl]) * scale)
            o_ref[:, sl] = _dot(p.astype(BF16), v_ref[:, sl]).astype(BF16)

    row = pl.BlockSpec((tm, d), lambda i: (i, 0))
    full = lambda a: pl.BlockSpec(a.shape, lambda i: (0, 0))
    return pl.pallas_call(
        body,
        grid=(t // tm,),
        in_specs=[row, full(wq), full(kb), full(vb)],
        out_specs=[row, row],
        out_shape=[jax.ShapeDtypeStruct((t, d), BF16), jax.ShapeDtypeStruct((t, d), BF16)],
        compiler_params=_params(1),
        name="attn_fwd",
    )(hb, wq, kb, vb)


def _ffn_bwd_act(dyb, wd, a, b, coef, name, deps=()):
    t, d = dyb.shape
    f = wd.shape[0]
    tm = _row_tile(t)
    tn = _col_tile(f)

    def body(dy_ref, wd_ref, a_ref, b_ref, da_ref, db_ref):
        dy = dy_ref[...]
        for c in range(f // tn):
            cols = slice(c * tn, (c + 1) * tn)
            ds = _dot_nt(dy, wd_ref[cols, :]) * coef
            silu, dsilu = _silu_and_grad(a_ref[:, cols].astype(F32))
            da_ref[:, cols] = (ds * b_ref[:, cols].astype(F32) * dsilu).astype(BF16)
            db_ref[:, cols] = (ds * silu).astype(BF16)

    act = pl.BlockSpec((tm, f), lambda i: (i, 0))
    return pl.pallas_call(
        _drop_deps(body, 4, len(deps)),
        grid=(t // tm,),
        in_specs=[pl.BlockSpec((tm, d), lambda i: (i, 0)), _resident(wd), act, act] + [ANY] * len(deps),
        out_specs=[act, act],
        out_shape=[jax.ShapeDtypeStruct((t, f), BF16), jax.ShapeDtypeStruct((t, f), BF16)],
        compiler_params=_params(1),
        name=name,
    )(dyb, wd, a, b, *deps)


def _ffn_bwd_fused(dy, dyb, wd, wg, wu, a, b, coef, ln, name):
    t, d = dy.shape
    f = wd.shape[0]
    tm = min(t, 256)
    tn = _col_tile(f)

    def body(dy_ref, dyb_ref, wd_ref, wg_ref, wu_ref, a_ref, b_ref, xh_ref, rs_ref, g_ref,
             da_ref, db_ref, dyo_ref, dyob_ref, dg_ref, dbl_ref):
        dyb_v = dyb_ref[...]
        dh = ALPHA * dy_ref[...]
        for c in range(f // tn):
            cols = slice(c * tn, (c + 1) * tn)
            ds = _dot_nt(dyb_v, wd_ref[cols, :]) * coef
            silu, dsilu = _silu_and_grad(a_ref[:, cols].astype(F32))
            da = (ds * b_ref[:, cols].astype(F32) * dsilu).astype(BF16)
            db = (ds * silu).astype(BF16)
            da_ref[:, cols] = da
            db_ref[:, cols] = db
            dh = dh + _dot_nt(da, wg_ref[:, cols]) + _dot_nt(db, wu_ref[:, cols])

        @pl.when(pl.program_id(0) == 0)
        def _():
            dg_ref[...] = jnp.zeros_like(dg_ref)
            dbl_ref[...] = jnp.zeros_like(dbl_ref)

        dyp, dg, dbl = _ln_bwd(dh, xh_ref[...], rs_ref[...], g_ref[...])
        dyo_ref[...] = dyp
        dyob_ref[...] = dyp.astype(BF16)
        dg_ref[...] += dg
        dbl_ref[...] += dbl

    row = pl.BlockSpec((tm, d), lambda i: (i, 0))
    act = pl.BlockSpec((tm, f), lambda i: (i, 0))
    vec = pl.BlockSpec((1, d), lambda i: (0, 0))
    return pl.pallas_call(
        body,
        grid=(t // tm,),
        in_specs=[row, row, _resident(wd), _resident(wg), _resident(wu), act, act, row,
                  pl.BlockSpec((tm, 1), lambda i: (i, 0)), vec],
        out_specs=[act, act, row, row, vec, vec],
        out_shape=[jax.ShapeDtypeStruct((t, f), BF16), jax.ShapeDtypeStruct((t, f), BF16),
                   jax.ShapeDtypeStruct((t, d), F32), jax.ShapeDtypeStruct((t, d), BF16),
                   jax.ShapeDtypeStruct((1, d), F32), jax.ShapeDtypeStruct((1, d), F32)],
        compiler_params=_params(1),
        name=name,
    )(dy, dyb, wd, wg, wu, a, b, *ln)


def _mm_tn(a, b, name, scale=1.0, deps=()):
    t, m = a.shape
    n = b.shape[1]
    tt = _row_tile(t)
    nt = t // tt
    tm_o, tn_o = m, n

    def body(a_ref, b_ref, o_ref, acc):
        k = pl.program_id(2)

        @pl.when(k == 0)
        def _():
            acc[...] = jnp.zeros_like(acc)

        acc[...] += _dot_tn(a_ref[...], b_ref[...])

        @pl.when(k == nt - 1)
        def _():
            o_ref[...] = (acc[...] * scale).astype(BF16)

    return pl.pallas_call(
        _drop_deps(body, 2, len(deps)),
        grid=(m // tm_o, n // tn_o, nt),
        in_specs=[pl.BlockSpec((tt, tm_o), lambda i, j, k: (k, i)), pl.BlockSpec((tt, tn_o), lambda i, j, k: (k, j))]
        + [ANY] * len(deps),
        out_specs=pl.BlockSpec((tm_o, tn_o), lambda i, j, k: (i, j)),
        out_shape=jax.ShapeDtypeStruct((m, n), BF16),
        scratch_shapes=[pltpu.VMEM((tm_o, tn_o), F32)],
        compiler_params=_params(3),
        name=name,
    )(a, b, *deps)


def _mm_nt(lhs, w, name):
    t, d = lhs.shape
    kd = w.shape[0]
    tm = _row_tile(t)

    def body(l_ref, w_ref, o_ref):
        o_ref[...] = _dot_nt(l_ref[...], w_ref[...])

    return pl.pallas_call(
        body,
        grid=(t // tm,),
        in_specs=[pl.BlockSpec((tm, d), lambda i: (i, 0)), _resident(w)],
        out_specs=pl.BlockSpec((tm, kd), lambda i: (i, 0)),
        out_shape=jax.ShapeDtypeStruct((t, kd), F32),
        compiler_params=_params(1),
        name=name,
    )(lhs, w)


def _dx_ln(dy, pairs, ln, name, deps=()):
    t, d = dy.shape
    npair = len(pairs)
    tm = min(t, 512 // npair)
    nt = t // tm
    n_in = 1 + 2 * npair + (3 if ln is not None else 0)

    def body(*refs):
        dy_ref = refs[0]
        pr = refs[1:1 + 2 * npair]
        pos = 1 + 2 * npair
        dh = ALPHA * dy_ref[...]
        for p in range(npair):
            dh = dh + _dot_nt(pr[2 * p][...], pr[2 * p + 1][...])
        if ln is not None:
            xh_ref, rs_ref, g_ref = refs[pos:pos + 3]
            dyo_ref, dyb_ref, dg_ref, db_ref = refs[pos + 3:pos + 7]

            @pl.when(pl.program_id(0) == 0)
            def _():
                dg_ref[...] = jnp.zeros_like(dg_ref)
                db_ref[...] = jnp.zeros_like(db_ref)

            dyp, dg, db = _ln_bwd(dh, xh_ref[...], rs_ref[...], g_ref[...])
            dyo_ref[...] = dyp
            dyb_ref[...] = dyp.astype(BF16)
            dg_ref[...] += dg
            db_ref[...] += db
        else:
            refs[pos][...] = dh

    row = pl.BlockSpec((tm, d), lambda i: (i, 0))
    vec = pl.BlockSpec((1, d), lambda i: (0, 0))
    in_specs = [row]
    args = [dy]
    for lhs, w in pairs:
        in_specs += [pl.BlockSpec((tm, lhs.shape[1]), lambda i: (i, 0)), _resident(w)]
        args += [lhs, w]
    if ln is not None:
        in_specs += [row, pl.BlockSpec((tm, 1), lambda i: (i, 0)), vec]
        args += list(ln)
        out_specs = [row, row, vec, vec]
        out_shape = [jax.ShapeDtypeStruct((t, d), F32), jax.ShapeDtypeStruct((t, d), BF16),
                     jax.ShapeDtypeStruct((1, d), F32), jax.ShapeDtypeStruct((1, d), F32)]
    else:
        out_specs = row
        out_shape = jax.ShapeDtypeStruct((t, d), F32)
    return pl.pallas_call(
        _drop_deps(body, n_in, len(deps)),
        grid=(nt,),
        in_specs=in_specs + [ANY] * len(deps),
        out_specs=out_specs,
        out_shape=out_shape,
        compiler_params=_params(1),
        name=name,
    )(*args, *deps)


def _hgrn_bwd(proj, oraw, dmix, states, logits, gn):
    t = proj.shape[0]
    ct = _hg_tile(t)
    nct = t // ct
    nblk = ct // HG_BLOCK
    nh = HG_HEADS
    mrows = min(ct, 256)

    def body(q_ref, fz_ref, iv_ref, gg_ref, or_ref, do_ref, st_ref, lg_ref, gn_ref, mask_ref,
             dq_ref, dfz_ref, div_ref, dgg_ref, dlg_ref, dgn_ref,
             dstate, qt_s, kt_s, k_s, b_s, eb_s, ekb_s, dec_s, dor_s, dbl_s, gr_s, dk_s, dlb_acc):
        c = pl.program_id(1)

        @pl.when(c == 0)
        def _():
            dstate[...] = jnp.zeros_like(dstate)
            dlb_acc[...] = jnp.zeros_like(dlb_acc)
            dgn_ref[...] = jnp.zeros_like(dgn_ref)

        lb = _lower_bound(lg_ref[...])
        q = q_ref[...]
        sig, nsig, f, k = _forget_terms(fz_ref[...], lb)
        logf = jnp.log(f)
        b = _mask_dot(mask_ref[0], logf)
        bend = _mask_dot(mask_ref[2], logf)
        eb = jnp.exp(b)
        ekb = jnp.exp(bend - b)
        qt_s[...] = (q * eb).astype(BF16)
        kt_s[...] = (k * ekb).astype(BF16)
        k_s[...] = k
        b_s[...] = b
        eb_s[...] = eb
        ekb_s[...] = ekb
        dec_s[...] = jnp.exp(bend)
        oraw = or_ref[...]
        r = lax.rsqrt(jnp.mean(oraw * oraw, axis=-1, keepdims=True) + LN_EPS)
        on = oraw * r
        gg = gg_ref[...]
        silu, dsilu = _silu_and_grad(gg)
        doa = do_ref[...]
        gnv = gn_ref[...]
        dgg_ref[...] = (doa * on * gnv * dsilu).astype(BF16)
        dyn = doa * silu
        dgn_ref[...] += jnp.sum(dyn * on, axis=0, keepdims=True)
        don = dyn * gnv
        dor_s[...] = r * (don - on * jnp.mean(don * on, axis=-1, keepdims=True))
        tidx = lax.broadcasted_iota(jnp.int32, (HG_HALF, HG_DIM), 0)

        def blk(ii, carry):
            i = nblk - 1 - ii
            r0 = pl.multiple_of(i * HG_BLOCK, HG_BLOCK)
            rows = pl.ds(r0, HG_BLOCK)
            st = st_ref[i]
            dst = dstate[...]
            dstb = dst.astype(BF16)
            do = dor_s[rows, :]
            dob = do.astype(BF16)
            v = iv_ref[rows, :]
            vb = v.astype(BF16)
            qq = q_ref[rows, :]
            kk = k_s[rows, :]
            bb = b_s[rows, :]
            qt = qt_s[rows, :]
            kt = kt_s[rows, :]
            dec = dec_s[pl.ds(r0, 1), :]
            dkt = _dot(vb, dstb)
            dq = _dot(dob, st.astype(BF16)) * eb_s[rows, :]
            dk = dkt * ekb_s[rows, :]
            dv = _dot_nt(kt, dstb)
            gend = jnp.sum(kk * dk, axis=0, keepdims=True) + dec * jnp.sum(dst * st, axis=0, keepdims=True)
            qh, bh, doh = _halves(qq), _halves(bb), _halves(do)
            dqh, dkh, dvh = list(_halves(dq)), list(_halves(dk)), list(_halves(dv))
            for s in range(HG_BLOCK):
                ks, vs = kk[s:s + 1, :], v[s:s + 1, :]
                dk_part = dv_part = None
                for h in _causal_halves(s):
                    e = _decay_from(bh[h], bb[s:s + 1, :], s, h, tidx)
                    ke = ks * e
                    acol = jnp.sum(qh[h] * ke, axis=1, keepdims=True)
                    dacol = jnp.sum(doh[h] * vs, axis=1, keepdims=True)
                    dqh[h] = dqh[h] + dacol * ke
                    pk = dacol * (qh[h] * e)
                    pv = acol * doh[h]
                    dk_part = pk if dk_part is None else dk_part + pk
                    dv_part = pv if dv_part is None else dv_part + pv
                hs, row = divmod(s, HG_HALF)
                dkh[hs] = dkh[hs] + jnp.where(tidx == row, jnp.sum(dk_part, axis=0, keepdims=True), 0.0)
                dvh[hs] = dvh[hs] + jnp.where(tidx == row, jnp.sum(dv_part, axis=0, keepdims=True), 0.0)
            dq = jnp.concatenate(dqh, axis=0)
            dk = jnp.concatenate(dkh, axis=0)
            dv = jnp.concatenate(dvh, axis=0)
            dq_ref[rows, :] = dq.astype(BF16)
            div_ref[rows, :] = dv.astype(BF16)
            dk_s[rows, :] = dk
            dbl_s[rows, :] = qq * dq - kk * dk
            gr_s[rows, :] = jnp.zeros((HG_BLOCK, HG_DIM), F32) + gend
            dstate[...] = dst * dec + _dot_tn(dob, qt)
            return carry

        lax.fori_loop(0, nblk, blk, 0, unroll=2 * HG_UNROLL)
        dlogf = _mask_dot(mask_ref[1], dbl_s[...]) + gr_s[...]
        dk = dk_s[...]
        dfz_ref[...] = ((dlogf / f - dk) * ((1.0 - lb) * sig * nsig)).astype(BF16)
        dlb_acc[...] += jnp.sum((dlogf / f - dk) * nsig, axis=0, keepdims=True)

        @pl.when(c == nct - 1)
        def _():
            dl0 = dlb_acc[...] * lb * (1.0 - lb)
            layer = lax.broadcasted_iota(jnp.int32, (2, HG_DIM), 0)
            dlg_ref[...] = jnp.where(layer == 0, dl0, -dl0)

    def slab(off):
        return pl.BlockSpec((ct, HG_DIM), lambda h, c: (nct - 1 - c, off + h))

    out_slab = pl.BlockSpec((ct, HG_DIM), lambda h, c: (nct - 1 - c, h))
    tile_f32 = pltpu.VMEM((ct, HG_DIM), F32)
    tile_b16 = pltpu.VMEM((ct, HG_DIM), BF16)
    slab_shape = jax.ShapeDtypeStruct((t, nh * HG_DIM), BF16)
    return pl.pallas_call(
        body,
        grid=(nh, nct),
        in_specs=[slab(0), slab(nh), slab(2 * nh), slab(3 * nh), slab(0), slab(0),
                  pl.BlockSpec((None, nblk, HG_DIM, HG_DIM), lambda h, c: (h, nct - 1 - c, 0, 0)),
                  pl.BlockSpec((None, 2, HG_DIM), lambda h, c: (h, 0, 0)),
                  pl.BlockSpec((1, HG_DIM), lambda h, c: (0, 0)),
                  pl.BlockSpec((3, mrows, mrows), lambda h, c: (0, 0, 0))],
        out_specs=[out_slab, out_slab, out_slab, out_slab,
                   pl.BlockSpec((None, 2, HG_DIM), lambda h, c: (h, 0, 0)),
                   pl.BlockSpec((None, 1, HG_DIM), lambda h, c: (h, 0, 0))],
        out_shape=[slab_shape, slab_shape, slab_shape, slab_shape,
                   jax.ShapeDtypeStruct((nh, 2, HG_DIM), F32), jax.ShapeDtypeStruct((nh, 1, HG_DIM), F32)],
        scratch_shapes=[pltpu.VMEM((HG_DIM, HG_DIM), F32), tile_b16, tile_b16, tile_f32, tile_f32, tile_f32, tile_f32,
                        tile_f32, tile_f32, tile_f32, tile_f32, tile_f32, pltpu.VMEM((1, HG_DIM), F32)],
        compiler_params=_params(2),
        name="hgrn_bwd",
    )(proj, proj, proj, proj, oraw, dmix, states, logits, gn, _block_masks(mrows))


def _sgu_bwd(proj, dmix, ln_g, ln_b, w_s, w_t, b_col):
    t = proj.shape[0]
    ct = _sg_tile(t)
    nct = t // ct
    ng = SG_GROUPS
    off_u = 4 * HG_HEADS
    off_v = off_u + ng
    n = SG_CHUNK

    def body(u_ref, v_ref, do_ref, g_ref, b_ref, w_ref, wt_ref, bs_ref, du_ref, dv_ref, dg_ref, db_ref, dw_ref, dbs_ref):
        c = pl.program_id(1)

        @pl.when(c == 0)
        def _():
            dg_ref[...] = jnp.zeros_like(dg_ref)
            db_ref[...] = jnp.zeros_like(db_ref)
            dw_ref[...] = jnp.zeros_like(dw_ref)
            dbs_ref[...] = jnp.zeros_like(dbs_ref)

        r = lax.broadcasted_iota(jnp.int32, (n, n), 0)
        cc = lax.broadcasted_iota(jnp.int32, (n, n), 1)
        wm = jnp.where(cc <= r, w_ref[...], 0.0).astype(BF16)
        wmt = jnp.where(r <= cc, wt_ref[...], 0.0).astype(BF16)
        for ci in range(ct // n):
            rows = slice(ci * n, (ci + 1) * n)
            ua, dua, dva, vn, xhat, rstd, s = _sgu_chunk_fwd(u_ref[rows, :], v_ref[rows, :], g_ref[...], b_ref[...],
                                                             wm, bs_ref[...])
            do = do_ref[rows, :]
            du_ref[rows, :] = (do * s * dua).astype(BF16)
            ds = do * ua
            dsb = ds.astype(BF16)
            dbs_ref[...] += jnp.sum(ds, axis=1, keepdims=True)
            dw_ref[...] += _dot_nt(dsb, vn.astype(BF16))
            dvn = _dot(wmt, dsb)
            dva_in, dg, db = _ln_bwd(dvn, xhat, rstd, g_ref[...])
            dg_ref[...] += dg
            db_ref[...] += db
            dv_ref[rows, :] = (dva_in * dva).astype(BF16)

        @pl.when(c == nct - 1)
        def _():
            dw_ref[...] = jnp.where(cc <= r, dw_ref[...], 0.0)

    vec = pl.BlockSpec((None, 1, SG_DIM), lambda g, c: (g, 0, 0))
    mat = pl.BlockSpec((None, n, n), lambda g, c: (g, 0, 0))
    col = pl.BlockSpec((None, n, 1), lambda g, c: (g, 0, 0))
    out_slab = pl.BlockSpec((ct, SG_DIM), lambda g, c: (c, g))
    return pl.pallas_call(
        body,
        grid=(ng, nct),
        in_specs=[pl.BlockSpec((ct, SG_DIM), lambda g, c: (c, off_u + g)),
                  pl.BlockSpec((ct, SG_DIM), lambda g, c: (c, off_v + g)),
                  pl.BlockSpec((ct, SG_DIM), lambda g, c: (c, ng + g)), vec, vec, mat, mat, col],
        out_specs=[out_slab, out_slab, vec, vec, mat, col],
        out_shape=[jax.ShapeDtypeStruct((t, ng * SG_DIM), BF16), jax.ShapeDtypeStruct((t, ng * SG_DIM), BF16),
                   jax.ShapeDtypeStruct((ng, 1, SG_DIM), F32), jax.ShapeDtypeStruct((ng, 1, SG_DIM), F32),
                   jax.ShapeDtypeStruct((ng, n, n), F32), jax.ShapeDtypeStruct((ng, n, 1), F32)],
        compiler_params=_params(2),
        name="sgu_bwd",
    )(proj, proj, dmix, ln_g, ln_b, w_s, w_t, b_col)


def _attn_bwd(dyb, wo, qb, kb, vb):
    t, d = dyb.shape
    m_len = kb.shape[0]
    tm = _row_tile(t)
    dh = d // X_HEADS
    scale = dh ** -0.5

    def body(dy_ref, wo_ref, q_ref, k_ref, v_ref, dq_ref, dk_ref, dv_ref):
        i = pl.program_id(0)

        @pl.when(i == 0)
        def _():
            dk_ref[...] = jnp.zeros_like(dk_ref)
            dv_ref[...] = jnp.zeros_like(dv_ref)

        do = _dot_nt(dy_ref[...], wo_ref[...]).astype(BF16)
        for hd in range(X_HEADS):
            sl = slice(hd * dh, (hd + 1) * dh)
            qh = q_ref[:, sl]
            p = _softmax_rows(_dot_nt(qh, k_ref[:, sl]) * scale)
            doh = do[:, sl]
            dp = _dot_nt(doh, v_ref[:, sl])
            ds = (p * (dp - jnp.sum(dp * p, axis=-1, keepdims=True)) * scale).astype(BF16)
            dq_ref[:, sl] = _dot(ds, k_ref[:, sl]).astype(BF16)
            dk_ref[:, sl] += _dot_tn(ds, qh)
            dv_ref[:, sl] += _dot_tn(p.astype(BF16), doh)

    row = pl.BlockSpec((tm, d), lambda i: (i, 0))
    full = lambda a: pl.BlockSpec(a.shape, lambda i: (0, 0))
    kv = pl.BlockSpec((m_len, d), lambda i: (0, 0))
    return pl.pallas_call(
        body,
        grid=(t // tm,),
        in_specs=[row, full(wo), row, full(kb), full(vb)],
        out_specs=[row, kv, kv],
        out_shape=[jax.ShapeDtypeStruct((t, d), BF16), jax.ShapeDtypeStruct((m_len, d), F32),
                   jax.ShapeDtypeStruct((m_len, d), F32)],
        compiler_params=_params(1),
        name="attn_bwd",
    )(dyb, wo, qb, kb, vb)


def _mem_bwd(dk, dv, mb, xhat, rstd, g, wk, wv):
    m_len, d = dk.shape

    def body(dk_ref, dv_ref, mb_ref, xh_ref, rs_ref, g_ref, wk_ref, wv_ref, gwk_ref, gwv_ref, dg_ref, db_ref):
        dkb = dk_ref[...].astype(BF16)
        dvb = dv_ref[...].astype(BF16)
        mb_v = mb_ref[...]
        gwk_ref[...] = _dot_tn(mb_v, dkb).astype(BF16)
        gwv_ref[...] = _dot_tn(mb_v, dvb).astype(BF16)
        dm = _dot_nt(dkb, wk_ref[...]) + _dot_nt(dvb, wv_ref[...])
        _, dg, db = _ln_bwd(dm, xh_ref[...], rs_ref[...], g_ref[...])
        dg_ref[...] = dg
        db_ref[...] = db

    return pl.pallas_call(
        body,
        out_shape=[jax.ShapeDtypeStruct((d, d), BF16), jax.ShapeDtypeStruct((d, d), BF16),
                   jax.ShapeDtypeStruct((1, d), F32), jax.ShapeDtypeStruct((1, d), F32)],
        compiler_params=pltpu.CompilerParams(vmem_limit_bytes=VMEM_LIMIT_V7X),
        name="mem_bwd",
    )(dk, dv, mb, xhat, rstd, g, wk, wv)


def _adamw(w, g, m, v):
    m = ADAM_B1 * m + (1.0 - ADAM_B1) * g
    v = ADAM_B2 * v + (1.0 - ADAM_B2) * (g * g)
    m_hat = m / (1.0 - ADAM_B1 ** ADAM_STEP)
    v_hat = v / (1.0 - ADAM_B2 ** ADAM_STEP)
    delta = -ADAM_LR * (m_hat / (jnp.sqrt(v_hat) + ADAM_EPS) + ADAM_WD * w)
    return delta, m, v


def _slot_sum(ref):
    g = ref[0].astype(F32)
    for s in range(1, N_DEV):
        g = g + ref[s].astype(F32)
    return g


def _adam_sharded(lands, w, m, v, axis, name):
    rows, cols = w.shape
    nl = len(lands)
    transposed = axis == 1 and nl == 2
    if transposed:
        rows, cols = cols, rows
        tr = 256
        grid = (rows // tr,)
        wblk = pl.BlockSpec((cols, tr), lambda i: (0, i))
        lblk = [pl.BlockSpec((N_DEV, tr, a.shape[2]), lambda i: (0, i, 0)) for a in lands]
    elif axis == 1:
        tr = 256 if rows % 256 == 0 else rows
        grid = (rows // tr,)
        wblk = pl.BlockSpec((tr, cols), lambda i: (i, 0))
        lblk = [pl.BlockSpec((N_DEV, tr, a.shape[2]), lambda i: (0, i, 0)) for a in lands]
    else:
        tc = _col_tile(cols)
        grid = (cols // tc,)
        wblk = pl.BlockSpec((rows, tc), lambda i: (0, i))
        lblk = [pl.BlockSpec((N_DEV, a.shape[1], tc), lambda i: (0, 0, i)) for a in lands]

    def body(*refs):
        w_ref, m_ref, v_ref = refs[nl:nl + 3]
        g_ref, d_ref, nm_ref, nv_ref = refs[nl + 3:]
        g = _slot_sum(refs[0])
        if nl == 2:
            tail = _slot_sum(refs[1])
            if transposed:
                g = jnp.concatenate([g.T, tail.T[:cols - g.shape[1], :]], axis=0)
            elif axis == 1:
                g = jnp.concatenate([g, tail[:, :cols - g.shape[1]]], axis=1)
            else:
                g = jnp.concatenate([g, tail[:rows - g.shape[0], :]], axis=0)
        delta, nm, nv = _adamw(w_ref[...], g, m_ref[...], v_ref[...])
        g_ref[...] = g
        d_ref[...] = delta
        nm_ref[...] = nm
        nv_ref[...] = nv

    shp = jax.ShapeDtypeStruct(w.shape, F32)
    return pl.pallas_call(
        body,
        grid=grid,
        in_specs=lblk + [wblk, wblk, wblk],
        out_specs=[wblk, wblk, wblk, wblk],
        out_shape=[shp, shp, shp, shp],
        compiler_params=_params(1),
        name=name,
    )(*[pltpu.with_memory_space_constraint(a, pltpu.HBM) for a in (*lands, w, m, v)])


def _mesh_pos():
    return lax.axis_index("x"), lax.axis_index("y"), lax.axis_index("c")


def _peer(k):
    x, y, c = _mesh_pos()
    pos = (x ^ (k >> 2), y ^ ((k >> 1) & 1), c ^ (k & 1))
    return pos, 4 * pos[0] + 2 * pos[1] + pos[2]


def _sem_index(row, k):
    return row * (N_DEV - 1) + k - 1


def _window(ref, axis, start, size):
    align = 16 if axis == 0 else LANES
    start = pl.multiple_of(start, align)
    return ref.at[pl.ds(start, size), :] if axis == 0 else ref.at[:, pl.ds(start, size)]


def _piece_refs(piece, srcs, lands, me, peer):
    kind, si, li, axis, base, stride, shape = piece
    if kind == "gather":
        return srcs[si], _window(lands[li], axis, base + stride * me, shape[axis])
    return _window(srcs[si], axis, base + stride * peer, shape[axis]), lands[li].at[me]


def _place_own(srcs, land_shapes, pieces, name):
    ns, nl, npc = len(srcs), len(land_shapes), len(pieces)

    def body(*refs):
        s_refs = refs[:ns]
        l_refs = refs[ns:ns + nl]
        bufs = refs[ns + nl:ns + nl + npc]
        sems = refs[ns + nl + npc]
        x, y, c = _mesh_pos()
        me = 4 * x + 2 * y + c
        loads = []
        for p, piece in enumerate(pieces):
            src, dst = _piece_refs(piece, s_refs, l_refs, me, me)
            cp = pltpu.make_async_copy(src, bufs[p], sems.at[0, p])
            cp.start()
            loads.append((cp, dst))
        stores = []
        for p, (cp, dst) in enumerate(loads):
            cp.wait()
            out = pltpu.make_async_copy(bufs[p], dst, sems.at[1, p])
            out.start()
            stores.append(out)
        for out in stores:
            out.wait()

    out = pl.pallas_call(
        body,
        in_specs=[ANY] * ns,
        out_specs=[ANY] * nl,
        out_shape=list(land_shapes),
        scratch_shapes=[pltpu.VMEM(pc[6], srcs[pc[1]].dtype) for pc in pieces] + [pltpu.SemaphoreType.DMA((2, npc))],
        compiler_params=pltpu.CompilerParams(vmem_limit_bytes=VMEM_LIMIT_V7X),
        name=name,
    )(*srcs)
    return list(out)


def _comm_start(srcs, lands, pieces, groups, name, after=()):
    ns, nl, na, ng = len(srcs), len(lands), len(after), len(groups)

    def body(*refs):
        s_refs = refs[:ns]
        l_refs = refs[ns:ns + nl]
        outs = refs[ns + nl + na:]
        sems = outs[:2 * ng]
        token = outs[-1]
        x, y, c = _mesh_pos()
        me = 4 * x + 2 * y + c
        for g, members in enumerate(groups):
            for row, p in enumerate(members):
                for k in range(1, N_DEV):
                    pos, peer = _peer(k)
                    src, dst = _piece_refs(pieces[p], s_refs, l_refs, me, peer)
                    pltpu.make_async_remote_copy(src_ref=src, dst_ref=dst, send_sem=sems[2 * g].at[_sem_index(row, k)],
                                                 recv_sem=sems[2 * g + 1].at[_sem_index(row, k)], device_id=pos,
                                                 device_id_type=MESH_ID).start()
        token[...] = jnp.zeros_like(token)

    sem_shapes = []
    for members in groups:
        sem_shapes += [pltpu.SemaphoreType.DMA((len(members) * (N_DEV - 1),))] * 2
    hbm_of = lambda a: pltpu.HBM(a.shape, a.dtype)
    out = pl.pallas_call(
        body,
        in_specs=[HBM] * (ns + nl) + [ANY] * na,
        out_specs=[SEM] * (2 * ng) + [HBM] * (ns + nl) + [pl.BlockSpec(memory_space=pltpu.VMEM)],
        out_shape=sem_shapes + [hbm_of(a) for a in srcs] + [hbm_of(a) for a in lands]
        + [jax.ShapeDtypeStruct((8, LANES), F32)],
        input_output_aliases={i: 2 * ng + i for i in range(ns + nl)},
        compiler_params=pltpu.CompilerParams(has_side_effects=DATAFLOW),
        name=name,
    )(*[pltpu.with_memory_space_constraint(a, pltpu.HBM) for a in list(srcs) + list(lands)], *after)
    sems = [(out[2 * g], out[2 * g + 1]) for g in range(ng)]
    return sems, list(out[2 * ng:2 * ng + ns]), list(out[2 * ng + ns:2 * ng + ns + nl]), out[-1]


def _comm_wait(srcs, lands, pieces, members, sems, after, name):
    ns, nl, na = len(srcs), len(lands), len(after)

    def body(*refs):
        s_refs = refs[:ns]
        l_refs = refs[ns:ns + nl]
        send_sems, recv_sems = refs[ns + nl:ns + nl + 2]
        x, y, c = _mesh_pos()
        me = 4 * x + 2 * y + c
        for row, p in enumerate(members):
            for k in range(1, N_DEV):
                pos, peer = _peer(k)
                src, dst = _piece_refs(pieces[p], s_refs, l_refs, me, peer)
                cp = pltpu.make_async_remote_copy(src_ref=src, dst_ref=dst, send_sem=send_sems.at[_sem_index(row, k)],
                                                  recv_sem=recv_sems.at[_sem_index(row, k)], device_id=pos,
                                                  device_id_type=MESH_ID)
                cp.wait_send()
                cp.wait_recv()

    hbm_of = lambda a: pltpu.HBM(a.shape, a.dtype)
    out = pl.pallas_call(
        body,
        in_specs=[HBM] * (ns + nl) + [SEM, SEM] + [ANY] * na,
        out_specs=[HBM] * (ns + nl),
        out_shape=[hbm_of(a) for a in srcs] + [hbm_of(a) for a in lands],
        input_output_aliases={i: i for i in range(ns + nl)},
        compiler_params=pltpu.CompilerParams(has_side_effects=DATAFLOW),
        name=name,
    )(*srcs, *lands, sems[0], sems[1], *after)
    return list(out[ns:])


def _landed_block(piece, lands, owner):
    _, _, li, axis, base, stride, shape = piece
    return _window(lands[li], axis, base + stride * owner, shape[axis])


def _copy_stage(name, bufs, in_sems, out_sem_sizes, emit, after=()):
    nb, ni, no, na = len(bufs), len(in_sems), len(out_sem_sizes), len(after)

    def body(*refs):
        b_refs = refs[:nb]
        i_refs = refs[nb:nb + ni]
        o_refs = refs[nb + ni + na:nb + ni + na + no]
        emit(b_refs, i_refs, o_refs)
        refs[-1][...] = jnp.zeros_like(refs[-1])

    hbm_of = lambda a: pltpu.HBM(a.shape, a.dtype)
    out = pl.pallas_call(
        body,
        in_specs=[HBM] * nb + [SEM] * ni + [ANY] * na,
        out_specs=[SEM] * no + [HBM] * nb + [pl.BlockSpec(memory_space=pltpu.VMEM)],
        out_shape=[pltpu.SemaphoreType.DMA((n,)) for n in out_sem_sizes] + [hbm_of(a) for a in bufs]
        + [jax.ShapeDtypeStruct((8, LANES), F32)],
        input_output_aliases={i: no + i for i in range(nb)},
        compiler_params=pltpu.CompilerParams(has_side_effects=DATAFLOW),
        name=name,
    )(*[pltpu.with_memory_space_constraint(a, pltpu.HBM) for a in bufs], *in_sems, *after)
    return list(out[:no]), list(out[no:no + nb]), out[-1]


def _remote(src, dst, send, recv, to):
    return pltpu.make_async_remote_copy(src_ref=src, dst_ref=dst, send_sem=send, recv_sem=recv, device_id=to,
                                        device_id_type=MESH_ID)


def _routed_gather(srcs, lands, pieces, after, name):
    ns, npc = len(srcs), len(pieces)

    def places():
        x, y, c = _mesh_pos()
        index = lambda p: 4 * p[0] + 2 * p[1] + p[2]
        me, sib = (x, y, c), (x, y, 1 - c)
        xnb, ynb = (1 - x, y, c), (x, 1 - y, c)
        got_first = (x ^ (1 - c), y ^ c, c)
        pass_to = (x ^ c, y ^ (1 - c), c)
        diag = (1 - x, 1 - y, c)
        return index, me, sib, xnb, ynb, got_first, pass_to, diag

    def start(b, _, o):
        index, me, sib, xnb, ynb, *_rest = places()
        send_a, recv_sib, recv_nb = o
        for p, piece in enumerate(pieces):
            src, dst = _piece_refs(piece, b[:ns], b[ns:], index(me), 0)
            _remote(src, dst, send_a.at[3 * p], recv_sib.at[p], sib).start()
            _remote(src, dst, send_a.at[3 * p + 1], recv_nb.at[2 * p], xnb).start()
            _remote(src, dst, send_a.at[3 * p + 2], recv_nb.at[2 * p + 1], ynb).start()

    def pass_a(b, i, o):
        index, me, sib, xnb, ynb, got_first, pass_to, _diag = places()
        (recv_nb,) = i
        send_f, recv_f, send_d, recv_d = o
        for p, piece in enumerate(pieces):
            for j, nb in enumerate((xnb, ynb)):
                blk = _landed_block(piece, b, index(nb))
                _remote(blk, blk, send_f.at[2 * p + j], recv_nb.at[2 * p + j], sib).wait_recv()
                _remote(blk, blk, send_f.at[2 * p + j], recv_f.at[2 * p + j], sib).start()
            blk = _landed_block(piece, b, index(got_first))
            _remote(blk, blk, send_d.at[p], recv_d.at[p], pass_to).start()

    def pass_b(b, i, o):
        index, me, sib, *_mid, diag = places()
        (recv_d,) = i
        send_g, recv_g = o
        for p, piece in enumerate(pieces):
            blk = _landed_block(piece, b, index(diag))
            _remote(blk, blk, send_g.at[p], recv_d.at[p], sib).wait_recv()
            _remote(blk, blk, send_g.at[p], recv_g.at[p], sib).start()

    def last(b, i, _):
        index, me, sib, *_others = places()
        send_a, recv_sib, send_f, recv_f, send_d, send_g, recv_g = i
        for p, piece in enumerate(pieces):
            src, dst = _piece_refs(piece, b[:ns], b[ns:], index(me), 0)
            cp = lambda s_sem, r_sem: _remote(src, dst, s_sem, r_sem, sib)
            cp(send_a.at[3 * p], recv_sib.at[p]).wait_recv()
            cp(send_a.at[3 * p], recv_g.at[p]).wait_recv()
            for j in range(3):
                cp(send_a.at[3 * p + j], recv_sib.at[p]).wait_send()
            for j in range(2):
                cp(send_f.at[2 * p + j], recv_f.at[2 * p + j]).wait_recv()
                cp(send_f.at[2 * p + j], recv_f.at[2 * p + j]).wait_send()
            cp(send_d.at[p], recv_sib.at[p]).wait_send()
            cp(send_g.at[p], recv_sib.at[p]).wait_send()

    (send_a, recv_sib, recv_nb), bufs, _ = _copy_stage(name + "_start", list(srcs) + list(lands), [],
                                                       [3 * npc, npc, 2 * npc], start)
    srcs, lands = bufs[:ns], bufs[ns:]
    (send_f, recv_f, send_d, recv_d), lands, _ = _copy_stage(name + "_pass_a", lands, [recv_nb],
                                                             [2 * npc, 2 * npc, npc, npc],
                                                             lambda b, i, o: pass_a(b, i, o), after=after)
    (send_g, recv_g), lands, tok = _copy_stage(name + "_pass_b", lands, [recv_d], [npc, npc], pass_b)
    _, bufs, _ = _copy_stage(name + "_last", list(srcs) + list(lands),
                             [send_a, recv_sib, send_f, recv_f, send_d, send_g, recv_g], [], last)
    return bufs[ns:], tok


_SMALL_NAMES = ("ln1_g", "ln1_b", "hg_lb_logits", "hg_norm_g", "sg_ln_g", "sg_ln_b", "sg_w_s", "sg_b_s",
                "ln2_g", "ln2_b", "mem_ln_g", "mem_ln_b", "ln3_g", "ln3_b", "ln4_g", "ln4_b")


_VEC_NAMES = ("ln1_g", "ln1_b", "ln2_g", "ln2_b", "mem_ln_g", "mem_ln_b", "ln3_g", "ln3_b", "ln4_g", "ln4_b")
_ROW_NAMES = ("hg_lb_logits", "hg_norm_g", "sg_ln_g", "sg_ln_b", "sg_b_s", "sg_w_s")
VEC_ROWS = 16


def _row_plan(shapes):
    plan, pos = {}, 0
    for k in _ROW_NAMES:
        shp = shapes[k]
        slabs, off = [], pos
        for idx in itertools.product(*[range(dim) for dim in shp[:-2]]):
            slabs.append((idx, off, shp[-2]))
            off += shp[-2]
        plan[k] = (pos, slabs)
        pos = -(-off // 8) * 8
    return plan, -(-pos // 16) * 16


def _pack_small_grads(gs, shapes, loss):
    d = gs[_VEC_NAMES[0]].size
    vec = jnp.concatenate([gs[k].reshape(1, -1) for k in _VEC_NAMES] + [jnp.tile(loss, (1, d // LANES))], axis=0)
    vec = jnp.pad(vec, ((0, VEC_ROWS - vec.shape[0]), (0, 0)))
    plan, total = _row_plan(shapes)
    parts, pos = [], 0
    for k in _ROW_NAMES:
        first, slabs = plan[k]
        rows = gs[k].reshape(-1, LANES)
        end = slabs[-1][1] + slabs[-1][2]
        nxt = -(-end // 8) * 8
        parts.append(jnp.pad(rows, ((0, nxt - first - rows.shape[0]), (0, 0))))
        pos = nxt
    parts.append(jnp.zeros((total - pos, LANES), F32))
    return vec, jnp.concatenate(parts, axis=0)


def _adam_small(land_vec, land_rows, w, m, v):
    names = _VEC_NAMES + _ROW_NAMES
    n = len(names)
    shapes = {k: w[k].shape for k in names}
    plan, _ = _row_plan(shapes)

    def body(*refs):
        lv_ref, lr_ref = refs[:2]
        w_refs, m_refs, v_refs = refs[2:2 + n], refs[2 + n:2 + 2 * n], refs[2 + 2 * n:2 + 3 * n]
        outs = refs[2 + 3 * n:2 + 7 * n]
        loss_ref = refs[2 + 7 * n]
        gv_s, gr_s = refs[3 + 7 * n:]
        gv_s[...] = _slot_sum(lv_ref)
        gr_s[...] = _slot_sum(lr_ref)
        loss_ref[...] = gv_s[len(_VEC_NAMES):len(_VEC_NAMES) + 1, :LANES]
        for p, k in enumerate(names):
            if k in _VEC_NAMES:
                row = _VEC_NAMES.index(k)
                slabs = [((), None, None)]
            else:
                slabs = plan[k][1]
            for idx, off, rows in slabs:
                g = gv_s[row:row + 1, :] if off is None else gr_s[off:off + rows, :]
                sel = idx + (slice(None), slice(None))
                delta, nm, nv = _adamw(w_refs[p][sel], g, m_refs[p][sel], v_refs[p][sel])
                for o, val in zip(range(4), (g, delta, nm, nv)):
                    outs[o * n + p][sel] = val

    flat = lambda tree: [tree[k] for k in names]
    shp = [jax.ShapeDtypeStruct(shapes[k], F32) for k in names]
    out = pl.pallas_call(
        body,
        out_shape=shp * 4 + [jax.ShapeDtypeStruct((1, LANES), F32)],
        scratch_shapes=[pltpu.VMEM(land_vec.shape[1:], F32), pltpu.VMEM(land_rows.shape[1:], F32)],
        name="adam_small",
    )(land_vec, land_rows, *flat(w), *flat(m), *flat(v))
    return [dict(zip(names, out[o * n:(o + 1) * n])) for o in range(4)], out[4 * n]


_COL_FFN = ("ffn1_w_gate", "ffn1_w_up", "ffn2_w_gate", "ffn2_w_up")
_ROW_FFN = ("ffn1_w_down", "ffn2_w_down")
_ROW_SQ = ("w_out", "xa_w_q", "xa_w_k", "xa_w_v", "xa_w_o")
_BIG_NAMES = ("ffn1_w_gate", "ffn1_w_up", "ffn1_w_down", "w_in", "w_out", "xa_w_q", "xa_w_k", "xa_w_v", "xa_w_o",
              "ffn2_w_gate", "ffn2_w_up", "ffn2_w_down")


def _ffn_split(fs):
    main = (fs // MXU_WIDTH_V7X) * MXU_WIDTH_V7X
    tail = fs - main
    tail_pad = -(-tail // LANES) * LANES
    assert main > 0 and tail > 0
    return main, tail, tail_pad


def _layout(name, shard_shape):
    r, c = shard_shape
    if name in _COL_FFN:
        main, tail, pad = _ffn_split(c)
        return (r, N_DEV * (main + pad)), [(1, 0, main, (r, main), (0, main)),
                                           (1, N_DEV * main, pad, (r, pad), (main, c))]
    if name in _ROW_FFN:
        main, tail, pad = _ffn_split(r)
        return (N_DEV * (main + pad), c), [(0, 0, main, (main, c), (0, main)),
                                           (0, N_DEV * main, pad, (pad, c), (main, r))]
    if name == "w_in":
        return (r, N_DEV * c), [(1, 0, c, (r, c), (0, c))]
    return (N_DEV * r, c), [(0, 0, r, (r, c), (0, r))]


def _shard_pieces(name, shard):
    out = []
    for axis, _, _, shape, (lo, hi) in _layout(name, shard.shape)[1]:
        part = shard[lo:hi, :] if axis == 0 else shard[:, lo:hi]
        pad = [(0, shape[0] - part.shape[0]), (0, shape[1] - part.shape[1])]
        out.append(jnp.pad(part, pad).astype(BF16))
    return out


def _gather_plan(names, shards):
    srcs, land_shapes, pieces, index = [], [], [], {}
    for li, name in enumerate(names):
        shape2d, parts = _layout(name, shards[name].shape)
        land_shapes.append(jax.ShapeDtypeStruct(shape2d, BF16))
        index[name] = []
        for (axis, base, stride, shape, _), src in zip(parts, _shard_pieces(name, shards[name])):
            index[name].append(len(pieces))
            pieces.append(("gather", len(srcs), li, axis, base, stride, shape))
            srcs.append(src)
    return srcs, land_shapes, pieces, index


def _scatter_plan(names, grads, shard_shapes):
    srcs, land_shapes, pieces, index = [], [], [], {}
    for si, name in enumerate(names):
        _, parts = _layout(name, shard_shapes[name])
        srcs.append(grads[name])
        index[name] = []
        for axis, base, stride, shape, _ in parts:
            index[name].append(len(land_shapes))
            pieces.append(("scatter", si, len(land_shapes), axis, base, stride, shape))
            land_shapes.append(jax.ShapeDtypeStruct((N_DEV,) + shape, grads[name].dtype))
    return srcs, land_shapes, pieces, index


def _small_views(small):
    row = lambda a: a.reshape(1, -1)
    ln = {k: row(small[k]) for k in ("ln1_g", "ln1_b", "ln2_g", "ln2_b", "ln3_g", "ln3_b", "ln4_g", "ln4_b",
                                      "mem_ln_g", "mem_ln_b", "hg_norm_g")}
    sg_w = small["sg_w_s"].reshape(SG_GROUPS, SG_CHUNK, SG_CHUNK)
    sg = dict(logits=jnp.swapaxes(small["hg_lb_logits"], 0, 1),
              g=small["sg_ln_g"].reshape(SG_GROUPS, 1, SG_DIM), b=small["sg_ln_b"].reshape(SG_GROUPS, 1, SG_DIM),
              w=sg_w, wt=jnp.swapaxes(sg_w, 1, 2), bs=small["sg_b_s"].reshape(SG_GROUPS, SG_CHUNK, 1))
    return ln, sg


def _forward(x, mem, target, get_w, small, first_deps=()):
    ln, sg = _small_views(small)
    xb = x.astype(BF16)
    a1, b1, s1 = _ffn_up(xb, get_w("ffn1_w_gate", ()), get_w("ffn1_w_up", ()), "ffn1_up", deps=first_deps)
    h1b, xh1, rs1 = _mm_res_ln(s1, get_w("ffn1_w_down", (s1,)), x, ln["ln1_g"], ln["ln1_b"], 0.5, "ffn1_down_ln")
    proj = _mm_nn(h1b, get_w("w_in", (h1b,)), "mix_in")
    oraw, mix, states = _hgrn_fwd(proj, sg["logits"], ln["hg_norm_g"])
    mix = _sgu_fwd(proj, mix, sg["g"], sg["b"], sg["w"], sg["bs"])
    h2b, xh2, rs2 = _mm_res_ln(mix, get_w("w_out", (mix,)), (xh1, ln["ln1_g"], ln["ln1_b"]), ln["ln2_g"], ln["ln2_b"],
                               1.0, "mix_out_ln")
    mb, mxh, mrs, kb, vb = _mem_kv(mem, ln["mem_ln_g"], ln["mem_ln_b"], get_w("xa_w_k", (h2b,)), get_w("xa_w_v", (h2b,)))
    qb, att = _attn_fwd(h2b, get_w("xa_w_q", (kb,)), kb, vb)
    h3b, xh3, rs3 = _mm_res_ln(att, get_w("xa_w_o", (att,)), (xh2, ln["ln2_g"], ln["ln2_b"]), ln["ln3_g"], ln["ln3_b"],
                               1.0, "attn_out_ln")
    a2, b2, s2 = _ffn_up(h3b, get_w("ffn2_w_gate", (h3b,)), get_w("ffn2_w_up", (h3b,)), "ffn2_up")
    loss, dy4, dy4b, dg4, db4 = _mm_res_ln(s2, get_w("ffn2_w_down", (s2,)), (xh3, ln["ln3_g"], ln["ln3_b"]),
                                           ln["ln4_g"], ln["ln4_b"], 0.5, "ffn2_down_ln_loss", target=target)
    return dict(xb=xb, a1=a1, b1=b1, s1=s1, h1b=h1b, xh1=xh1, rs1=rs1, proj=proj, oraw=oraw, mix=mix, states=states,
                h2b=h2b, xh2=xh2, rs2=rs2, mb=mb, mxh=mxh, mrs=mrs, kb=kb, vb=vb, qb=qb, att=att, h3b=h3b, xh3=xh3,
                rs3=rs3, a2=a2, b2=b2, s2=s2, loss=loss, dy4=dy4, dy4b=dy4b, dg4=dg4, db4=db4)


def _backward(sv, wt, small, send):
    ln, sg = _small_views(small)
    gs = {"ln4_g": sv["dg4"], "ln4_b": sv["db4"]}
    loss, dy4, dy4b = sv["loss"], sv["dy4"], sv["dy4b"]
    g_down2 = _mm_tn(sv["s2"], dy4b, "g_ffn2_down", scale=0.5)
    da2, db2, dy3, dy3b, gs["ln3_g"], gs["ln3_b"] = _ffn_bwd_fused(
        dy4, dy4b, wt["ffn2_w_down"], wt["ffn2_w_gate"], wt["ffn2_w_up"], sv["a2"], sv["b2"], 0.5,
        (sv["xh3"], sv["rs3"], ln["ln3_g"]), "ffn2_bwd")
    g_gate2 = _mm_tn(sv["h3b"], da2, "g_ffn2_gate")
    g_up2 = _mm_tn(sv["h3b"], db2, "g_ffn2_up")
    tok = send(("ffn2_w_down", "ffn2_w_gate", "ffn2_w_up"), (g_down2, g_gate2, g_up2))

    g_o = _mm_tn(sv["att"], dy3b, "g_xa_o", deps=(tok,))
    dqb, dk, dv = _attn_bwd(dy3b, wt["xa_w_o"], sv["qb"], sv["kb"], sv["vb"])
    g_q = _mm_tn(sv["h2b"], dqb, "g_xa_q")
    g_k, g_v, gs["mem_ln_g"], gs["mem_ln_b"] = _mem_bwd(dk, dv, sv["mb"], sv["mxh"], sv["mrs"], ln["mem_ln_g"],
                                                        wt["xa_w_k"], wt["xa_w_v"])
    tok = send(("xa_w_o", "xa_w_q", "xa_w_k", "xa_w_v"), (g_o, g_q, g_k, g_v))
    dy2, dy2b, gs["ln2_g"], gs["ln2_b"] = _dx_ln(dy3, [(dqb, wt["xa_w_q"])], (sv["xh2"], sv["rs2"], ln["ln2_g"]),
                                                 "attn_dx_ln", deps=(tok,))

    g_out = _mm_tn(sv["mix"], dy2b, "g_w_out")
    dmix = _mm_nt(dy2b, wt["w_out"], "mix_out_bwd")
    dq, dfz, div, dgg, dlg, dgn = _hgrn_bwd(sv["proj"], sv["oraw"], dmix, sv["states"], sg["logits"], ln["hg_norm_g"])
    du, dvv, gs["sg_ln_g"], gs["sg_ln_b"], gs["sg_w_s"], gs["sg_b_s"] = _sgu_bwd(
        sv["proj"], dmix, sg["g"], sg["b"], sg["w"], sg["wt"], sg["bs"])
    gs["hg_lb_logits"] = jnp.swapaxes(dlg, 0, 1)
    gs["hg_norm_g"] = jnp.sum(dgn, axis=0)
    dproj = jnp.concatenate([dq, dfz, div, dgg, du, dvv], axis=1)
    g_in = _mm_tn(sv["h1b"], dproj, "g_w_in")
    tok = send(("w_out", "w_in"), (g_out, g_in))
    dy1, dy1b, gs["ln1_g"], gs["ln1_b"] = _dx_ln(dy2, [(dproj, wt["w_in"])], (sv["xh1"], sv["rs1"], ln["ln1_g"]),
                                                 "mix_dx_ln", deps=(tok,))

    g_down1 = _mm_tn(sv["s1"], dy1b, "g_ffn1_down", scale=0.5)
    tok = send(("ffn1_w_down",), (g_down1,))
    da1, db1 = _ffn_bwd_act(dy1b, wt["ffn1_w_down"], sv["a1"], sv["b1"], 0.5, "ffn1_bwd_act", deps=(tok,))
    g_gate1 = _mm_tn(sv["xb"], da1, "g_ffn1_gate")
    tok = send(("ffn1_w_gate",), (g_gate1,))
    g_up1 = _mm_tn(sv["xb"], db1, "g_ffn1_up", deps=(tok,))
    tok = send(("ffn1_w_up",), (g_up1,))
    grad_x = _dx_ln(dy1, [(da1, wt["ffn1_w_gate"]), (db1, wt["ffn1_w_up"])], None, "ffn1_dx", deps=(tok,))
    return loss, grad_x, gs


_WEIGHT_NAMES = ("ffn1_w_gate", "ffn1_w_up", "ffn1_w_down", "ln1_g", "ln1_b", "w_in", "hg_lb_logits", "hg_norm_g",
                 "sg_ln_g", "sg_ln_b", "sg_w_s", "sg_b_s", "w_out", "ln2_g", "ln2_b", "mem_ln_g", "mem_ln_b",
                 "xa_w_q", "xa_w_k", "xa_w_v", "xa_w_o", "ln3_g", "ln3_b", "ffn2_w_gate", "ffn2_w_up", "ffn2_w_down",
                 "ln4_g", "ln4_b")
_FIRST = ("ffn1_w_gate", "ffn1_w_up")
_SECOND = ("ffn1_w_down", "w_in", "w_out")
_THIRD = ("xa_w_k", "xa_w_v", "xa_w_q", "xa_w_o", "ffn2_w_gate", "ffn2_w_up", "ffn2_w_down")


def kernel(x, mem, ffn1_w_gate, ffn1_w_up, ffn1_w_down, ln1_g, ln1_b, w_in, hg_lb_logits, hg_norm_g, sg_ln_g, sg_ln_b, sg_w_s, sg_b_s, w_out, ln2_g, ln2_b, mem_ln_g, mem_ln_b, xa_w_q, xa_w_k, xa_w_v, xa_w_o, ln3_g, ln3_b, ffn2_w_gate, ffn2_w_up, ffn2_w_down, ln4_g, ln4_b, loss_target, m_ffn1_w_gate, m_ffn1_w_up, m_ffn1_w_down, m_ln1_g, m_ln1_b, m_w_in, m_hg_lb_logits, m_hg_norm_g, m_sg_ln_g, m_sg_ln_b, m_sg_w_s, m_sg_b_s, m_w_out, m_ln2_g, m_ln2_b, m_mem_ln_g, m_mem_ln_b, m_xa_w_q, m_xa_w_k, m_xa_w_v, m_xa_w_o, m_ln3_g, m_ln3_b, m_ffn2_w_gate, m_ffn2_w_up, m_ffn2_w_down, m_ln4_g, m_ln4_b, v_ffn1_w_gate, v_ffn1_w_up, v_ffn1_w_down, v_ln1_g, v_ln1_b, v_w_in, v_hg_lb_logits, v_hg_norm_g, v_sg_ln_g, v_sg_ln_b, v_sg_w_s, v_sg_b_s, v_w_out, v_ln2_g, v_ln2_b, v_mem_ln_g, v_mem_ln_b, v_xa_w_q, v_xa_w_k, v_xa_w_v, v_xa_w_o, v_ln3_g, v_ln3_b, v_ffn2_w_gate, v_ffn2_w_up, v_ffn2_w_down, v_ln4_g, v_ln4_b):
    args = dict(locals())
    w = {k: args[k] for k in _WEIGHT_NAMES}
    m = {k: args["m_" + k] for k in _WEIGHT_NAMES}
    v = {k: args["v_" + k] for k in _WEIGHT_NAMES}
    shards = {k: w[k][0] for k in _BIG_NAMES}
    shard_shapes = {k: shards[k].shape for k in _BIG_NAMES}
    small = {k: (w[k][0] if k != "hg_lb_logits" else w[k]) for k in _SMALL_NAMES}

    srcs1, shapes1, pieces1, idx1 = _gather_plan(_FIRST, shards)
    lands1 = _place_own(srcs1, shapes1, pieces1, "gather_first_own")
    rest = _SECOND + _THIRD
    srcs2, shapes2, pieces2, idx2 = _gather_plan(rest, shards)
    lands2 = _place_own(srcs2, shapes2, pieces2, "gather_rest_own")
    groups2 = [list(idx2[k]) for k in rest]
    lands1, tok1 = _routed_gather(srcs1, lands1, pieces1, tuple(lands2), "gather_first")
    sems2, srcs2, lands2, tok2 = _comm_start(srcs2, lands2, pieces2, groups2, "gather_rest_start", after=(tok1,))
    wt = dict(zip(_FIRST, lands1))
    pending = {k: gi for gi, k in enumerate(rest)}

    def get_w(name, after):
        if name in pending:
            gi = pending.pop(name)
            si = [pieces2[p][1] for p in groups2[gi]]
            sub = [(pieces2[p][0], row, 0) + pieces2[p][3:] for row, p in enumerate(groups2[gi])]
            wt[name] = _comm_wait([srcs2[s] for s in si], [lands2[gi]], sub, list(range(len(sub))), sems2[gi],
                                  after, "gather_wait_" + name)[0]
        return wt[name]

    sv = _forward(x[0], mem[0], loss_target[0], get_w, small, first_deps=(tok2,))

    sent = []

    def send(names, grads):
        srcs, shapes, pieces, idx = _scatter_plan(names, dict(zip(names, grads)), shard_shapes)
        lands = _place_own(srcs, shapes, pieces, "grads_own_%d" % len(sent))
        sems, srcs, lands, tok = _comm_start(srcs, lands, pieces, [list(range(len(pieces)))],
                                             "grads_start_%d" % len(sent))
        sent.append((names, srcs, lands, pieces, idx, sems[0]))
        return tok

    loss, grad_x, gs = _backward(sv, wt, small, send)

    ssrc = list(_pack_small_grads(gs, {k: w[k].shape for k in _SMALL_NAMES}, loss))
    sp = [("scatter", i, i, 0, 0, 0, a.shape) for i, a in enumerate(ssrc)]
    sshape = [jax.ShapeDtypeStruct((N_DEV,) + a.shape, F32) for a in ssrc]
    sl = _place_own(ssrc, sshape, sp, "small_own")
    ssem, ssrc, sl, _ = _comm_start(ssrc, sl, sp, [[0, 1]], "small_start")

    out_g, out_d, out_m, out_v = {}, {}, {}, {}
    after = (grad_x,)
    for n_sent, (names, srcs, lands, pieces, idx, sems) in enumerate(sent):
        lands = _comm_wait(srcs, lands, pieces, list(range(len(pieces))), sems, after, "grads_wait_%d" % n_sent)
        for k in names:
            axis = 1 if (k in _COL_FFN or k == "w_in") else 0
            if k in _COL_FFN:
                res = _adam_sharded([lands[i] for i in idx[k]], w[k][0].T, m[k][0].T, v[k][0].T, axis, "adam_" + k)
                res = [r.T for r in res]
            else:
                res = _adam_sharded([lands[i] for i in idx[k]], w[k][0], m[k][0], v[k][0], axis, "adam_" + k)
            out_g[k], out_d[k], out_m[k], out_v[k] = [r[None] for r in res]
        after = (out_v[names[-1]],)
    sl = _comm_wait(ssrc, sl, sp, [0, 1], ssem[0], after, "small_wait")
    small_out, loss_sum = _adam_small(sl[0], sl[1], w, m, v)
    for dst, res in zip((out_g, out_d, out_m, out_v), small_out):
        dst.update(res)
    loss_all = loss_sum[0, 0]
    return (loss_all, grad_x[None], *[out_g[k] for k in _WEIGHT_NAMES], *[out_d[k] for k in _WEIGHT_NAMES],
            *[out_m[k] for k in _WEIGHT_NAMES], *[out_v[k] for k in _WEIGHT_NAMES])
```

```python
import itertools

import jax
import jax.numpy as jnp
import numpy as np
from jax import lax
from jax.experimental import pallas as pl
from jax.experimental.pallas import tpu as pltpu

F32 = jnp.float32
BF16 = jnp.bfloat16

N_DEV = 8
ALPHA = 2.0 ** 0.25
LN_EPS = 1e-5
HG_HEADS = 4
HG_DIM = 128
SG_GROUPS = 4
SG_DIM = 128
SG_CHUNK = 128
X_HEADS = 4
HG_BLOCK = 16
HG_UNROLL = 8
ADAM_LR = 0.001
ADAM_B1 = 0.9
ADAM_B2 = 0.999
ADAM_EPS = 1e-08
ADAM_WD = 0.01
ADAM_STEP = 10
VMEM_LIMIT_V7X = 48 * 1024 * 1024
MXU_WIDTH_V7X = 256
LANES = 128
MESH_ID = pl.DeviceIdType.MESH
ANY = pl.BlockSpec(memory_space=pl.ANY)
HBM = pl.BlockSpec(memory_space=pltpu.HBM)
SEM = pl.BlockSpec(memory_space=pltpu.SEMAPHORE)
DATAFLOW = pltpu.SideEffectType.DATAFLOW_SIDE_EFFECTING


def _params(n_axes):
    return pltpu.CompilerParams(dimension_semantics=("arbitrary",) * n_axes, vmem_limit_bytes=VMEM_LIMIT_V7X)


def _dot(a, b):
    return jnp.dot(a, b, preferred_element_type=F32)


def _dot_nt(a, b):
    return lax.dot_general(a, b, (((1,), (1,)), ((), ())), preferred_element_type=F32)


def _dot_tn(a, b):
    return lax.dot_general(a, b, (((0,), (0,)), ((), ())), preferred_element_type=F32)


def _sigmoid(x):
    return 1.0 / (1.0 + jnp.exp(-x))


def _silu_and_grad(a):
    sig = _sigmoid(a)
    return a * sig, sig * (1.0 + a * (1.0 - sig))


_GELU_C = 0.7978845608028654


def _gelu_and_grad(x):
    inner = _GELU_C * (x + 0.044715 * x * x * x)
    t = jnp.tanh(inner)
    val = 0.5 * x * (1.0 + t)
    grad = 0.5 * (1.0 + t) + 0.5 * x * (1.0 - t * t) * _GELU_C * (1.0 + 3.0 * 0.044715 * x * x)
    return val, grad


def _ln_fwd(y, g, b):
    mu = jnp.mean(y, axis=-1, keepdims=True)
    yc = y - mu
    var = jnp.mean(yc * yc, axis=-1, keepdims=True)
    rstd = lax.rsqrt(var + LN_EPS)
    xhat = yc * rstd
    return xhat * g + b, xhat, rstd


def _ln_bwd(dh, xhat, rstd, g):
    dxh = dh * g
    m1 = jnp.mean(dxh, axis=-1, keepdims=True)
    m2 = jnp.mean(dxh * xhat, axis=-1, keepdims=True)
    dy = rstd * (dxh - m1 - xhat * m2)
    dg = jnp.sum(dh * xhat, axis=0, keepdims=True)
    db = jnp.sum(dh, axis=0, keepdims=True)
    return dy, dg, db


def _mask_dot(mask, x):
    hi = x.astype(BF16)
    lo = (x - hi.astype(F32)).astype(BF16)
    n = mask.shape[0]
    parts = [_dot(mask, hi[r:r + n, :]) + _dot(mask, lo[r:r + n, :]) for r in range(0, x.shape[0], n)]
    return parts[0] if len(parts) == 1 else jnp.concatenate(parts, axis=0)


def _block_masks(n):
    r = np.arange(n)[:, None]
    c = np.arange(n)[None, :]
    same = (r // HG_BLOCK) == (c // HG_BLOCK)
    return jnp.asarray(np.stack([same & (c <= r), same & (c >= r), same]), BF16)


def _row_tile(t):
    return min(t, 512)


def _col_tile(n):
    for cand in (512, 256, 128):
        if n % cand == 0:
            return cand
    return n


def _resident(w):
    return pl.BlockSpec(w.shape, lambda *_: (0, 0), pipeline_mode=pl.Buffered(1))


def _drop_deps(body, n_in, n_deps):
    if n_deps == 0:
        return body
    return lambda *refs: body(*refs[:n_in], *refs[n_in + n_deps:])


def _ffn_up(hb, wg, wu, name, deps=()):
    t, d = hb.shape
    f = wg.shape[1]
    tm = _row_tile(t)
    tn = _col_tile(f)

    def body(h_ref, wg_ref, wu_ref, a_ref, b_ref, s_ref):
        h = h_ref[...]
        for c in range(f // tn):
            cols = slice(c * tn, (c + 1) * tn)
            a = _dot(h, wg_ref[:, cols])
            b = _dot(h, wu_ref[:, cols])
            a_ref[:, cols] = a.astype(BF16)
            b_ref[:, cols] = b.astype(BF16)
            s_ref[:, cols] = (a * _sigmoid(a) * b).astype(BF16)

    act = pl.BlockSpec((tm, f), lambda i: (i, 0))
    return pl.pallas_call(
        _drop_deps(body, 3, len(deps)),
        grid=(t // tm,),
        in_specs=[pl.BlockSpec((tm, d), lambda i: (i, 0)), _resident(wg), _resident(wu)] + [ANY] * len(deps),
        out_specs=[act, act, act],
        out_shape=[jax.ShapeDtypeStruct((t, f), BF16)] * 3,
        compiler_params=_params(1),
        name=name,
    )(hb, wg, wu, *deps)


def _mm_res_ln(lhs, w, res, g, b, coef, name, target=None):
    t, kd = lhs.shape
    d = w.shape[1]
    tm = _row_tile(t)
    nt = t // tm
    from_norm = isinstance(res, tuple)
    n_res = 3 if from_norm else 1

    def body(*refs):
        l_ref, w_ref = refs[:2]
        r_refs = refs[2:2 + n_res]
        g_ref, b_ref = refs[2 + n_res:4 + n_res]
        rest = refs[4 + n_res:]
        prev = r_refs[0][...] * r_refs[1][...] + r_refs[2][...] if from_norm else r_refs[0][...]
        y = ALPHA * prev + coef * _dot(l_ref[...], w_ref[...])
        h, xhat, rstd = _ln_fwd(y, g_ref[...], b_ref[...])
        if target is None:
            hb_ref, xh_ref, rs_ref = rest
            hb_ref[...] = h.astype(BF16)
            xh_ref[...] = xhat
            rs_ref[...] = rstd
            return
        t_ref, loss_ref, dy_ref, dyb_ref, dg_ref, db_ref, lacc = rest
        i = pl.program_id(0)

        @pl.when(i == 0)
        def _():
            lacc[...] = jnp.zeros_like(lacc)
            dg_ref[...] = jnp.zeros_like(dg_ref)
            db_ref[...] = jnp.zeros_like(db_ref)

        err = h - t_ref[...]
        lacc[...] += jnp.sum(err * err, axis=0, keepdims=True)
        dy, dg, db = _ln_bwd(err * (1.0 / d), xhat, rstd, g_ref[...])
        dy_ref[...] = dy
        dyb_ref[...] = dy.astype(BF16)
        dg_ref[...] += dg
        db_ref[...] += db

        @pl.when(i == nt - 1)
        def _():
            loss_ref[...] = jnp.zeros_like(loss_ref) + jnp.sum(lacc[...], axis=1, keepdims=True) * (0.5 / d)

    row = pl.BlockSpec((tm, d), lambda i: (i, 0))
    vec = pl.BlockSpec((1, d), lambda i: (0, 0))
    res_specs = [row, vec, vec] if from_norm else [row]
    res_args = list(res) if from_norm else [res]
    in_specs = [pl.BlockSpec((tm, kd), lambda i: (i, 0)), _resident(w)] + res_specs + [vec, vec]
    args = [lhs, w] + res_args + [g, b]
    if target is None:
        out_specs = [row, row, pl.BlockSpec((tm, 1), lambda i: (i, 0))]
        out_shape = [jax.ShapeDtypeStruct((t, d), BF16), jax.ShapeDtypeStruct((t, d), F32),
                     jax.ShapeDtypeStruct((t, 1), F32)]
        scratch = []
    else:
        in_specs.append(row)
        args.append(target)
        out_specs = [pl.BlockSpec((1, LANES), lambda i: (0, 0)), row, row, vec, vec]
        out_shape = [jax.ShapeDtypeStruct((1, LANES), F32), jax.ShapeDtypeStruct((t, d), F32),
                     jax.ShapeDtypeStruct((t, d), BF16), jax.ShapeDtypeStruct((1, d), F32),
                     jax.ShapeDtypeStruct((1, d), F32)]
        scratch = [pltpu.VMEM((1, d), F32)]
    return pl.pallas_call(
        body,
        grid=(nt,),
        in_specs=in_specs,
        out_specs=out_specs,
        out_shape=out_shape,
        scratch_shapes=scratch,
        compiler_params=_params(1),
        name=name,
    )(*args)


def _mm_nn(lhs, w, name):
    t, kd = lhs.shape
    n = w.shape[1]
    tm = _row_tile(t)
    tn = _col_tile(n)

    def body(l_ref, w_ref, o_ref):
        lhs_v = l_ref[...]
        for c in range(n // tn):
            cols = slice(c * tn, (c + 1) * tn)
            o_ref[:, cols] = _dot(lhs_v, w_ref[:, cols])

    return pl.pallas_call(
        body,
        grid=(t // tm,),
        in_specs=[pl.BlockSpec((tm, kd), lambda i: (i, 0)), _resident(w)],
        out_specs=pl.BlockSpec((tm, n), lambda i: (i, 0)),
        out_shape=jax.ShapeDtypeStruct((t, n), F32),
        compiler_params=_params(1),
        name=name,
    )(lhs, w)


def _lower_bound(lg):
    m = jnp.max(lg, axis=0, keepdims=True)
    e = jnp.exp(lg - m)
    return e[0:1, :] / jnp.sum(e, axis=0, keepdims=True)


def _forget_terms(fz, lb):
    e = jnp.exp(-jnp.abs(fz))
    r = 1.0 / (1.0 + e)
    pos = fz >= 0.0
    sig = jnp.where(pos, r, e * r)
    nsig = jnp.where(pos, e * r, r)
    f = lb + (1.0 - lb) * sig
    k = (1.0 - lb) * nsig
    return sig, nsig, f, k


def _hg_tile(t):
    return min(t, 1024)


HG_HALF = HG_BLOCK // 2
NEG_BIG = -1e30


def _halves(a):
    return a[:HG_HALF, :], a[HG_HALF:, :]


def _causal_halves(s):
    return (0, 1) if s < HG_HALF else (1,)


def _decay_from(b_half, b_s, s, h, tidx):
    first = s - h * HG_HALF
    diff = b_half - b_s
    if first > 0:
        diff = jnp.where(tidx >= first, diff, NEG_BIG)
    return jnp.exp(diff)


def _hgrn_fwd(proj, logits, gn):
    t = proj.shape[0]
    ct = _hg_tile(t)
    nct = t // ct
    nblk = ct // HG_BLOCK
    nh = HG_HEADS
    mrows = min(ct, 256)

    def body(q_ref, fz_ref, iv_ref, gg_ref, lg_ref, gn_ref, mask_ref, oraw_ref, oa_ref, st_ref,
             state, qt_s, kt_s, k_s, b_s, dec_s):
        c = pl.program_id(1)

        @pl.when(c == 0)
        def _():
            state[...] = jnp.zeros_like(state)

        lb = _lower_bound(lg_ref[...])
        q = q_ref[...]
        _, _, f, k = _forget_terms(fz_ref[...], lb)
        logf = jnp.log(f)
        b = _mask_dot(mask_ref[0], logf)
        bend = _mask_dot(mask_ref[2], logf)
        qt_s[...] = (q * jnp.exp(b)).astype(BF16)
        kt_s[...] = (k * jnp.exp(bend - b)).astype(BF16)
        k_s[...] = k
        b_s[...] = b
        dec_s[...] = jnp.exp(bend)
        tidx = lax.broadcasted_iota(jnp.int32, (HG_HALF, HG_DIM), 0)

        def blk(i, carry):
            r0 = pl.multiple_of(i * HG_BLOCK, HG_BLOCK)
            rows = pl.ds(r0, HG_BLOCK)
            st = state[...]
            stb = st.astype(BF16)
            st_ref[i] = stb
            v = iv_ref[rows, :]
            qq = q_ref[rows, :]
            kk = k_s[rows, :]
            bb = b_s[rows, :]
            o = list(_halves(_dot_nt(qt_s[rows, :], stb)))
            qh, bh = _halves(qq), _halves(bb)
            for s in range(HG_BLOCK):
                ks, vs = kk[s:s + 1, :], v[s:s + 1, :]
                for h in _causal_halves(s):
                    e = _decay_from(bh[h], bb[s:s + 1, :], s, h, tidx)
                    acol = jnp.sum(qh[h] * (ks * e), axis=1, keepdims=True)
                    o[h] = o[h] + acol * vs
            oraw_ref[rows, :] = jnp.concatenate(o, axis=0)
            state[...] = st * dec_s[pl.ds(r0, 1), :] + _dot_tn(v.astype(BF16), kt_s[rows, :])
            return carry

        lax.fori_loop(0, nblk, blk, 0, unroll=2 * HG_UNROLL)
        oraw = oraw_ref[...]
        r = lax.rsqrt(jnp.mean(oraw * oraw, axis=-1, keepdims=True) + LN_EPS)
        gg = gg_ref[...]
        oa_ref[...] = (oraw * r * gn_ref[...] * gg * _sigmoid(gg)).astype(BF16)

    def slab(off):
        return pl.BlockSpec((ct, HG_DIM), lambda h, c: (c, off + h))

    out_slab = pl.BlockSpec((ct, HG_DIM), lambda h, c: (c, h))
    return pl.pallas_call(
        body,
        grid=(nh, nct),
        in_specs=[slab(0), slab(nh), slab(2 * nh), slab(3 * nh),
                  pl.BlockSpec((None, 2, HG_DIM), lambda h, c: (h, 0, 0)),
                  pl.BlockSpec((1, HG_DIM), lambda h, c: (0, 0)),
                  pl.BlockSpec((3, mrows, mrows), lambda h, c: (0, 0, 0))],
        out_specs=[out_slab, out_slab, pl.BlockSpec((None, nblk, HG_DIM, HG_DIM), lambda h, c: (h, c, 0, 0))],
        out_shape=[jax.ShapeDtypeStruct((t, nh * HG_DIM), F32),
                   jax.ShapeDtypeStruct((t, (nh + SG_GROUPS) * HG_DIM), BF16),
                   jax.ShapeDtypeStruct((nh, t // HG_BLOCK, HG_DIM, HG_DIM), BF16)],
        scratch_shapes=[pltpu.VMEM((HG_DIM, HG_DIM), F32), pltpu.VMEM((ct, HG_DIM), BF16),
                        pltpu.VMEM((ct, HG_DIM), BF16), pltpu.VMEM((ct, HG_DIM), F32),
                        pltpu.VMEM((ct, HG_DIM), F32), pltpu.VMEM((ct, HG_DIM), F32)],
        compiler_params=_params(2),
        name="hgrn_fwd",
    )(proj, proj, proj, proj, logits, gn, _block_masks(mrows))


def _sg_tile(t):
    return min(t, 512)


def _sgu_chunk_fwd(u, v, ln_g, ln_b, wm, bs):
    ua, dua = _gelu_and_grad(u)
    va, dva = _gelu_and_grad(v)
    vn, xhat, rstd = _ln_fwd(va, ln_g, ln_b)
    s = _dot(wm, vn.astype(BF16)) + bs
    return ua, dua, dva, vn, xhat, rstd, s


def _tril_weight(w_ref):
    n = SG_CHUNK
    r = lax.broadcasted_iota(jnp.int32, (n, n), 0)
    c = lax.broadcasted_iota(jnp.int32, (n, n), 1)
    return jnp.where(c <= r, w_ref[...], 0.0)


def _sgu_fwd(proj, mix, ln_g, ln_b, w_s, b_col):
    t = proj.shape[0]
    ct = _sg_tile(t)
    ng = SG_GROUPS
    off_u = 4 * HG_HEADS
    off_v = off_u + ng

    def body(u_ref, v_ref, g_ref, b_ref, w_ref, bs_ref, mix_ref, o_ref):
        del mix_ref
        wm = _tril_weight(w_ref).astype(BF16)
        for n in range(ct // SG_CHUNK):
            rows = slice(n * SG_CHUNK, (n + 1) * SG_CHUNK)
            ua, _, _, _, _, _, s = _sgu_chunk_fwd(u_ref[rows, :], v_ref[rows, :], g_ref[...], b_ref[...], wm, bs_ref[...])
            o_ref[rows, :] = (ua * s).astype(BF16)

    vec = pl.BlockSpec((None, 1, SG_DIM), lambda g, c: (g, 0, 0))
    return pl.pallas_call(
        body,
        grid=(ng, t // ct),
        in_specs=[pl.BlockSpec((ct, SG_DIM), lambda g, c: (c, off_u + g)),
                  pl.BlockSpec((ct, SG_DIM), lambda g, c: (c, off_v + g)), vec, vec,
                  pl.BlockSpec((None, SG_CHUNK, SG_CHUNK), lambda g, c: (g, 0, 0)),
                  pl.BlockSpec((None, SG_CHUNK, 1), lambda g, c: (g, 0, 0)), ANY],
        out_specs=pl.BlockSpec((ct, SG_DIM), lambda g, c: (c, HG_HEADS + g)),
        out_shape=jax.ShapeDtypeStruct(mix.shape, mix.dtype),
        input_output_aliases={6: 0},
        compiler_params=_params(2),
        name="sgu_fwd",
    )(proj, proj, ln_g, ln_b, w_s, b_col, mix)


def _mem_kv(mem, g, b, wk, wv):
    m_len, d = mem.shape

    def body(m_ref, g_ref, b_ref, wk_ref, wv_ref, mb_ref, xh_ref, rs_ref, k_ref, v_ref):
        m, xhat, rstd = _ln_fwd(m_ref[...], g_ref[...], b_ref[...])
        mb = m.astype(BF16)
        mb_ref[...] = mb
        xh_ref[...] = xhat
        rs_ref[...] = rstd
        k_ref[...] = _dot(mb, wk_ref[...]).astype(BF16)
        v_ref[...] = _dot(mb, wv_ref[...]).astype(BF16)

    return pl.pallas_call(
        body,
        out_shape=[jax.ShapeDtypeStruct((m_len, d), BF16), jax.ShapeDtypeStruct((m_len, d), F32),
                   jax.ShapeDtypeStruct((m_len, 1), F32), jax.ShapeDtypeStruct((m_len, d), BF16),
                   jax.ShapeDtypeStruct((m_len, d), BF16)],
        compiler_params=pltpu.CompilerParams(vmem_limit_bytes=VMEM_LIMIT_V7X),
        name="mem_kv",
    )(mem, g, b, wk, wv)


def _softmax_rows(s):
    m = jnp.max(s, axis=-1, keepdims=True)
    p = jnp.exp(s - m)
    return p / jnp.sum(p, axis=-1, keepdims=True)


def _attn_fwd(hb, wq, kb, vb):
    t, d = hb.shape
    tm = _row_tile(t)
    dh = d // X_HEADS
    scale = dh ** -0.5

    def body(h_ref, wq_ref, k_ref, v_ref, q_ref, o_ref):
        q = _dot(h_ref[...], wq_ref[...]).astype(BF16)
        q_ref[...] = q
        for hd in range(X_HEADS):
            sl = slice(hd * dh, (hd + 1) * dh)
            p = _softmax_rows(_dot_nt(q[:, sl], k_ref[:, sl]) * scale)
            o_ref[:, sl] = _dot(p.astype(BF16), v_ref[:, sl]).astype(BF16)

    row = pl.BlockSpec((tm, d), lambda i: (i, 0))
    full = lambda a: pl.BlockSpec(a.shape, lambda i: (0, 0))
    return pl.pallas_call(
        body,
        grid=(t // tm,),
        in_specs=[row, full(wq), full(kb), full(vb)],
        out_specs=[row, row],
        out_shape=[jax.ShapeDtypeStruct((t, d), BF16), jax.ShapeDtypeStruct((t, d), BF16)],
        compiler_params=_params(1),
        name="attn_fwd",
    )(hb, wq, kb, vb)


def _ffn_bwd_act(dyb, wd, a, b, coef, name, deps=()):
    t, d = dyb.shape
    f = wd.shape[0]
    tm = _row_tile(t)
    tn = _col_tile(f)

    def body(dy_ref, wd_ref, a_ref, b_ref, da_ref, db_ref):
        dy = dy_ref[...]
        for c in range(f // tn):
            cols = slice(c * tn, (c + 1) * tn)
            ds = _dot_nt(dy, wd_ref[cols, :]) * coef
            silu, dsilu = _silu_and_grad(a_ref[:, cols].astype(F32))
            da_ref[:, cols] = (ds * b_ref[:, cols].astype(F32) * dsilu).astype(BF16)
            db_ref[:, cols] = (ds * silu).astype(BF16)

    act = pl.BlockSpec((tm, f), lambda i: (i, 0))
    return pl.pallas_call(
        _drop_deps(body, 4, len(deps)),
        grid=(t // tm,),
        in_specs=[pl.BlockSpec((tm, d), lambda i: (i, 0)), _resident(wd), act, act] + [ANY] * len(deps),
        out_specs=[act, act],
        out_shape=[jax.ShapeDtypeStruct((t, f), BF16), jax.ShapeDtypeStruct((t, f), BF16)],
        compiler_params=_params(1),
        name=name,
    )(dyb, wd, a, b, *deps)


def _ffn_bwd_fused(dy, dyb, wd, wg, wu, a, b, coef, ln, name):
    t, d = dy.shape
    f = wd.shape[0]
    tm = min(t, 256)
    tn = _col_tile(f)

    def body(dy_ref, dyb_ref, wd_ref, wg_ref, wu_ref, a_ref, b_ref, xh_ref, rs_ref, g_ref,
             da_ref, db_ref, dyo_ref, dyob_ref, dg_ref, dbl_ref):
        dyb_v = dyb_ref[...]
        dh = ALPHA * dy_ref[...]
        for c in range(f // tn):
            cols = slice(c * tn, (c + 1) * tn)
            ds = _dot_nt(dyb_v, wd_ref[cols, :]) * coef
            silu, dsilu = _silu_and_grad(a_ref[:, cols].astype(F32))
            da = (ds * b_ref[:, cols].astype(F32) * dsilu).astype(BF16)
            db = (ds * silu).astype(BF16)
            da_ref[:, cols] = da
            db_ref[:, cols] = db
            dh = dh + _dot_nt(da, wg_ref[:, cols]) + _dot_nt(db, wu_ref[:, cols])

        @pl.when(pl.program_id(0) == 0)
        def _():
            dg_ref[...] = jnp.zeros_like(dg_ref)
            dbl_ref[...] = jnp.zeros_like(dbl_ref)

        dyp, dg, dbl = _ln_bwd(dh, xh_ref[...], rs_ref[...], g_ref[...])
        dyo_ref[...] = dyp
        dyob_ref[...] = dyp.astype(BF16)
        dg_ref[...] += dg
        dbl_ref[...] += dbl

    row = pl.BlockSpec((tm, d), lambda i: (i, 0))
    act = pl.BlockSpec((tm, f), lambda i: (i, 0))
    vec = pl.BlockSpec((1, d), lambda i: (0, 0))
    return pl.pallas_call(
        body,
        grid=(t // tm,),
        in_specs=[row, row, _resident(wd), _resident(wg), _resident(wu), act, act, row,
                  pl.BlockSpec((tm, 1), lambda i: (i, 0)), vec],
        out_specs=[act, act, row, row, vec, vec],
        out_shape=[jax.ShapeDtypeStruct((t, f), BF16), jax.ShapeDtypeStruct((t, f), BF16),
                   jax.ShapeDtypeStruct((t, d), F32), jax.ShapeDtypeStruct((t, d), BF16),
                   jax.ShapeDtypeStruct((1, d), F32), jax.ShapeDtypeStruct((1, d), F32)],
        compiler_params=_params(1),
        name=name,
    )(dy, dyb, wd, wg, wu, a, b, *ln)


def _mm_tn(a, b, name, scale=1.0, deps=()):
    t, m = a.shape
    n = b.shape[1]
    tt = _row_tile(t)
    nt = t // tt
    tm_o, tn_o = m, n

    def body(a_ref, b_ref, o_ref, acc):
        k = pl.program_id(2)

        @pl.when(k == 0)
        def _():
            acc[...] = jnp.zeros_like(acc)

        acc[...] += _dot_tn(a_ref[...], b_ref[...])

        @pl.when(k == nt - 1)
        def _():
            o_ref[...] = (acc[...] * scale).astype(BF16)

    return pl.pallas_call(
        _drop_deps(body, 2, len(deps)),
        grid=(m // tm_o, n // tn_o, nt),
        in_specs=[pl.BlockSpec((tt, tm_o), lambda i, j, k: (k, i)), pl.BlockSpec((tt, tn_o), lambda i, j, k: (k, j))]
        + [ANY] * len(deps),
        out_specs=pl.BlockSpec((tm_o, tn_o), lambda i, j, k: (i, j)),
        out_shape=jax.ShapeDtypeStruct((m, n), BF16),
        scratch_shapes=[pltpu.VMEM((tm_o, tn_o), F32)],
        compiler_params=_params(3),
        name=name,
    )(a, b, *deps)


def _mm_nt(lhs, w, name):
    t, d = lhs.shape
    kd = w.shape[0]
    tm = _row_tile(t)

    def body(l_ref, w_ref, o_ref):
        o_ref[...] = _dot_nt(l_ref[...], w_ref[...])

    return pl.pallas_call(
        body,
        grid=(t // tm,),
        in_specs=[pl.BlockSpec((tm, d), lambda i: (i, 0)), _resident(w)],
        out_specs=pl.BlockSpec((tm, kd), lambda i: (i, 0)),
        out_shape=jax.ShapeDtypeStruct((t, kd), F32),
        compiler_params=_params(1),
        name=name,
    )(lhs, w)


def _dx_ln(dy, pairs, ln, name, deps=()):
    t, d = dy.shape
    npair = len(pairs)
    tm = min(t, 512 // npair)
    nt = t // tm
    n_in = 1 + 2 * npair + (3 if ln is not None else 0)

    def body(*refs):
        dy_ref = refs[0]
        pr = refs[1:1 + 2 * npair]
        pos = 1 + 2 * npair
        dh = ALPHA * dy_ref[...]
        for p in range(npair):
            dh = dh + _dot_nt(pr[2 * p][...], pr[2 * p + 1][...])
        if ln is not None:
            xh_ref, rs_ref, g_ref = refs[pos:pos + 3]
            dyo_ref, dyb_ref, dg_ref, db_ref = refs[pos + 3:pos + 7]

            @pl.when(pl.program_id(0) == 0)
            def _():
                dg_ref[...] = jnp.zeros_like(dg_ref)
                db_ref[...] = jnp.zeros_like(db_ref)

            dyp, dg, db = _ln_bwd(dh, xh_ref[...], rs_ref[...], g_ref[...])
            dyo_ref[...] = dyp
            dyb_ref[...] = dyp.astype(BF16)
            dg_ref[...] += dg
            db_ref[...] += db
        else:
            refs[pos][...] = dh

    row = pl.BlockSpec((tm, d), lambda i: (i, 0))
    vec = pl.BlockSpec((1, d), lambda i: (0, 0))
    in_specs = [row]
    args = [dy]
    for lhs, w in pairs:
        in_specs += [pl.BlockSpec((tm, lhs.shape[1]), lambda i: (i, 0)), _resident(w)]
        args += [lhs, w]
    if ln is not None:
        in_specs += [row, pl.BlockSpec((tm, 1), lambda i: (i, 0)), vec]
        args += list(ln)
        out_specs = [row, row, vec, vec]
        out_shape = [jax.ShapeDtypeStruct((t, d), F32), jax.ShapeDtypeStruct((t, d), BF16),
                     jax.ShapeDtypeStruct((1, d), F32), jax.ShapeDtypeStruct((1, d), F32)]
    else:
        out_specs = row
        out_shape = jax.ShapeDtypeStruct((t, d), F32)
    return pl.pallas_call(
        _drop_deps(body, n_in, len(deps)),
        grid=(nt,),
        in_specs=in_specs + [ANY] * len(deps),
        out_specs=out_specs,
        out_shape=out_shape,
        compiler_params=_params(1),
        name=name,
    )(*args, *deps)


def _hgrn_bwd(proj, oraw, dmix, states, logits, gn):
    t = proj.shape[0]
    ct = _hg_tile(t)
    nct = t // ct
    nblk = ct // HG_BLOCK
    nh = HG_HEADS
    mrows = min(ct, 256)

    def body(q_ref, fz_ref, iv_ref, gg_ref, or_ref, do_ref, st_ref, lg_ref, gn_ref, mask_ref,
             dq_ref, dfz_ref, div_ref, dgg_ref, dlg_ref, dgn_ref,
             dstate, qt_s, kt_s, k_s, b_s, eb_s, ekb_s, dec_s, dor_s, dbl_s, gr_s, dk_s, dlb_acc):
        c = pl.program_id(1)

        @pl.when(c == 0)
        def _():
            dstate[...] = jnp.zeros_like(dstate)
            dlb_acc[...] = jnp.zeros_like(dlb_acc)
            dgn_ref[...] = jnp.zeros_like(dgn_ref)

        lb = _lower_bound(lg_ref[...])
        q = q_ref[...]
        sig, nsig, f, k = _forget_terms(fz_ref[...], lb)
        logf = jnp.log(f)
        b = _mask_dot(mask_ref[0], logf)
        bend = _mask_dot(mask_ref[2], logf)
        eb = jnp.exp(b)
        ekb = jnp.exp(bend - b)
        qt_s[...] = (q * eb).astype(BF16)
        kt_s[...] = (k * ekb).astype(BF16)
        k_s[...] = k
        b_s[...] = b
        eb_s[...] = eb
        ekb_s[...] = ekb
        dec_s[...] = jnp.exp(bend)
        oraw = or_ref[...]
        r = lax.rsqrt(jnp.mean(oraw * oraw, axis=-1, keepdims=True) + LN_EPS)
        on = oraw * r
        gg = gg_ref[...]
        silu, dsilu = _silu_and_grad(gg)
        doa = do_ref[...]
        gnv = gn_ref[...]
        dgg_ref[...] = (doa * on * gnv * dsilu).astype(BF16)
        dyn = doa * silu
        dgn_ref[...] += jnp.sum(dyn * on, axis=0, keepdims=True)
        don = dyn * gnv
        dor_s[...] = r * (don - on * jnp.mean(don * on, axis=-1, keepdims=True))
        tidx = lax.broadcasted_iota(jnp.int32, (HG_HALF, HG_DIM), 0)

        def blk(ii, carry):
            i = nblk - 1 - ii
            r0 = pl.multiple_of(i * HG_BLOCK, HG_BLOCK)
            rows = pl.ds(r0, HG_BLOCK)
            st = st_ref[i]
            dst = dstate[...]
            dstb = dst.astype(BF16)
            do = dor_s[rows, :]
            dob = do.astype(BF16)
            v = iv_ref[rows, :]
            vb = v.astype(BF16)
            qq = q_ref[rows, :]
            kk = k_s[rows, :]
            bb = b_s[rows, :]
            qt = qt_s[rows, :]
            kt = kt_s[rows, :]
            dec = dec_s[pl.ds(r0, 1), :]
            dkt = _dot(vb, dstb)
            dq = _dot(dob, st) * eb_s[rows, :]
            dk = dkt * ekb_s[rows, :]
            dv = _dot_nt(kt, dstb)
            gend = (jnp.sum(kk * dk, axis=0, keepdims=True)
                    + dec * jnp.sum(dst * st.astype(F32), axis=0, keepdims=True))
            qh, bh, doh = _halves(qq), _halves(bb), _halves(do)
            dqh, dkh, dvh = list(_halves(dq)), list(_halves(dk)), list(_halves(dv))
            for s in range(HG_BLOCK):
                ks, vs = kk[s:s + 1, :], v[s:s + 1, :]
                dk_part = dv_part = None
                for h in _causal_halves(s):
                    e = _decay_from(bh[h], bb[s:s + 1, :], s, h, tidx)
                    ke = ks * e
                    acol = jnp.sum(qh[h] * ke, axis=1, keepdims=True)
                    dacol = jnp.sum(doh[h] * vs, axis=1, keepdims=True)
                    dqh[h] = dqh[h] + dacol * ke
                    pk = dacol * (qh[h] * e)
                    pv = acol * doh[h]
                    dk_part = pk if dk_part is None else dk_part + pk
                    dv_part = pv if dv_part is None else dv_part + pv
                hs, row = divmod(s, HG_HALF)
                dkh[hs] = dkh[hs] + jnp.where(tidx == row, jnp.sum(dk_part, axis=0, keepdims=True), 0.0)
                dvh[hs] = dvh[hs] + jnp.where(tidx == row, jnp.sum(dv_part, axis=0, keepdims=True), 0.0)
            dq = jnp.concatenate(dqh, axis=0)
            dk = jnp.concatenate(dkh, axis=0)
            dv = jnp.concatenate(dvh, axis=0)
            dq_ref[rows, :] = dq.astype(BF16)
            div_ref[rows, :] = dv.astype(BF16)
            dk_s[rows, :] = dk
            dbl_s[rows, :] = qq * dq - kk * dk
            gr_s[rows, :] = jnp.zeros((HG_BLOCK, HG_DIM), F32) + gend
            dstate[...] = dst * dec + _dot_tn(dob, qt)
            return carry

        lax.fori_loop(0, nblk, blk, 0, unroll=2 * HG_UNROLL)
        dlogf = _mask_dot(mask_ref[1], dbl_s[...]) + gr_s[...]
        dk = dk_s[...]
        dfz_ref[...] = ((dlogf / f - dk) * ((1.0 - lb) * sig * nsig)).astype(BF16)
        dlb_acc[...] += jnp.sum((dlogf / f - dk) * nsig, axis=0, keepdims=True)

        @pl.when(c == nct - 1)
        def _():
            dl0 = dlb_acc[...] * lb * (1.0 - lb)
            layer = lax.broadcasted_iota(jnp.int32, (2, HG_DIM), 0)
            dlg_ref[...] = jnp.where(layer == 0, dl0, -dl0)

    def slab(off):
        return pl.BlockSpec((ct, HG_DIM), lambda h, c: (nct - 1 - c, off + h))

    out_slab = pl.BlockSpec((ct, HG_DIM), lambda h, c: (nct - 1 - c, h))
    tile_f32 = pltpu.VMEM((ct, HG_DIM), F32)
    tile_b16 = pltpu.VMEM((ct, HG_DIM), BF16)
    slab_shape = jax.ShapeDtypeStruct((t, nh * HG_DIM), BF16)
    return pl.pallas_call(
        body,
        grid=(nh, nct),
        in_specs=[slab(0), slab(nh), slab(2 * nh), slab(3 * nh), slab(0), slab(0),
                  pl.BlockSpec((None, nblk, HG_DIM, HG_DIM), lambda h, c: (h, nct - 1 - c, 0, 0)),
                  pl.BlockSpec((None, 2, HG_DIM), lambda h, c: (h, 0, 0)),
                  pl.BlockSpec((1, HG_DIM), lambda h, c: (0, 0)),
                  pl.BlockSpec((3, mrows, mrows), lambda h, c: (0, 0, 0))],
        out_specs=[out_slab, out_slab, out_slab, out_slab,
                   pl.BlockSpec((None, 2, HG_DIM), lambda h, c: (h, 0, 0)),
                   pl.BlockSpec((None, 1, HG_DIM), lambda h, c: (h, 0, 0))],
        out_shape=[slab_shape, slab_shape, slab_shape, slab_shape,
                   jax.ShapeDtypeStruct((nh, 2, HG_DIM), F32), jax.ShapeDtypeStruct((nh, 1, HG_DIM), F32)],
        scratch_shapes=[pltpu.VMEM((HG_DIM, HG_DIM), F32), tile_b16, tile_b16, tile_f32, tile_f32, tile_f32, tile_f32,
                        tile_f32, tile_f32, tile_f32, tile_f32, tile_f32, pltpu.VMEM((1, HG_DIM), F32)],
        compiler_params=_params(2),
        name="hgrn_bwd",
    )(proj, proj, proj, proj, oraw, dmix, states, logits, gn, _block_masks(mrows))


def _sgu_bwd(proj, dmix, ln_g, ln_b, w_s, w_t, b_col):
    t = proj.shape[0]
    ct = _sg_tile(t)
    nct = t // ct
    ng = SG_GROUPS
    off_u = 4 * HG_HEADS
    off_v = off_u + ng
    n = SG_CHUNK

    def body(u_ref, v_ref, do_ref, g_ref, b_ref, w_ref, wt_ref, bs_ref, du_ref, dv_ref, dg_ref, db_ref, dw_ref, dbs_ref):
        c = pl.program_id(1)

        @pl.when(c == 0)
        def _():
            dg_ref[...] = jnp.zeros_like(dg_ref)
            db_ref[...] = jnp.zeros_like(db_ref)
            dw_ref[...] = jnp.zeros_like(dw_ref)
            dbs_ref[...] = jnp.zeros_like(dbs_ref)

        r = lax.broadcasted_iota(jnp.int32, (n, n), 0)
        cc = lax.broadcasted_iota(jnp.int32, (n, n), 1)
        wm = jnp.where(cc <= r, w_ref[...], 0.0).astype(BF16)
        wmt = jnp.where(r <= cc, wt_ref[...], 0.0).astype(BF16)
        for ci in range(ct // n):
            rows = slice(ci * n, (ci + 1) * n)
            ua, dua, dva, vn, xhat, rstd, s = _sgu_chunk_fwd(u_ref[rows, :], v_ref[rows, :], g_ref[...], b_ref[...],
                                                             wm, bs_ref[...])
            do = do_ref[rows, :]
            du_ref[rows, :] = (do * s * dua).astype(BF16)
            ds = do * ua
            dsb = ds.astype(BF16)
            dbs_ref[...] += jnp.sum(ds, axis=1, keepdims=True)
            dw_ref[...] += _dot_nt(dsb, vn.astype(BF16))
            dvn = _dot(wmt, dsb)
            dva_in, dg, db = _ln_bwd(dvn, xhat, rstd, g_ref[...])
            dg_ref[...] += dg
            db_ref[...] += db
            dv_ref[rows, :] = (dva_in * dva).astype(BF16)

        @pl.when(c == nct - 1)
        def _():
            dw_ref[...] = jnp.where(cc <= r, dw_ref[...], 0.0)

    vec = pl.BlockSpec((None, 1, SG_DIM), lambda g, c: (g, 0, 0))
    mat = pl.BlockSpec((None, n, n), lambda g, c: (g, 0, 0))
    col = pl.BlockSpec((None, n, 1), lambda g, c: (g, 0, 0))
    out_slab = pl.BlockSpec((ct, SG_DIM), lambda g, c: (c, g))
    return pl.pallas_call(
        body,
        grid=(ng, nct),
        in_specs=[pl.BlockSpec((ct, SG_DIM), lambda g, c: (c, off_u + g)),
                  pl.BlockSpec((ct, SG_DIM), lambda g, c: (c, off_v + g)),
                  pl.BlockSpec((ct, SG_DIM), lambda g, c: (c, ng + g)), vec, vec, mat, mat, col],
        out_specs=[out_slab, out_slab, vec, vec, mat, col],
        out_shape=[jax.ShapeDtypeStruct((t, ng * SG_DIM), BF16), jax.ShapeDtypeStruct((t, ng * SG_DIM), BF16),
                   jax.ShapeDtypeStruct((ng, 1, SG_DIM), F32), jax.ShapeDtypeStruct((ng, 1, SG_DIM), F32),
                   jax.ShapeDtypeStruct((ng, n, n), F32), jax.ShapeDtypeStruct((ng, n, 1), F32)],
        compiler_params=_params(2),
        name="sgu_bwd",
    )(proj, proj, dmix, ln_g, ln_b, w_s, w_t, b_col)


def _attn_bwd(dyb, wo, qb, kb, vb):
    t, d = dyb.shape
    m_len = kb.shape[0]
    tm = _row_tile(t)
    dh = d // X_HEADS
    scale = dh ** -0.5

    def body(dy_ref, wo_ref, q_ref, k_ref, v_ref, dq_ref, dk_ref, dv_ref):
        i = pl.program_id(0)

        @pl.when(i == 0)
        def _():
            dk_ref[...] = jnp.zeros_like(dk_ref)
            dv_ref[...] = jnp.zeros_like(dv_ref)

        do = _dot_nt(dy_ref[...], wo_ref[...]).astype(BF16)
        for hd in range(X_HEADS):
            sl = slice(hd * dh, (hd + 1) * dh)
            qh = q_ref[:, sl]
            p = _softmax_rows(_dot_nt(qh, k_ref[:, sl]) * scale)
            doh = do[:, sl]
            dp = _dot_nt(doh, v_ref[:, sl])
            ds = (p * (dp - jnp.sum(dp * p, axis=-1, keepdims=True)) * scale).astype(BF16)
            dq_ref[:, sl] = _dot(ds, k_ref[:, sl]).astype(BF16)
            dk_ref[:, sl] += _dot_tn(ds, qh)
            dv_ref[:, sl] += _dot_tn(p.astype(BF16), doh)

    row = pl.BlockSpec((tm, d), lambda i: (i, 0))
    full = lambda a: pl.BlockSpec(a.shape, lambda i: (0, 0))
    kv = pl.BlockSpec((m_len, d), lambda i: (0, 0))
    return pl.pallas_call(
        body,
        grid=(t // tm,),
        in_specs=[row, full(wo), row, full(kb), full(vb)],
        out_specs=[row, kv, kv],
        out_shape=[jax.ShapeDtypeStruct((t, d), BF16), jax.ShapeDtypeStruct((m_len, d), F32),
                   jax.ShapeDtypeStruct((m_len, d), F32)],
        compiler_params=_params(1),
        name="attn_bwd",
    )(dyb, wo, qb, kb, vb)


def _mem_bwd(dk, dv, mb, xhat, rstd, g, wk, wv):
    m_len, d = dk.shape

    def body(dk_ref, dv_ref, mb_ref, xh_ref, rs_ref, g_ref, wk_ref, wv_ref, gwk_ref, gwv_ref, dg_ref, db_ref):
        dkb = dk_ref[...].astype(BF16)
        dvb = dv_ref[...].astype(BF16)
        mb_v = mb_ref[...]
        gwk_ref[...] = _dot_tn(mb_v, dkb).astype(BF16)
        gwv_ref[...] = _dot_tn(mb_v, dvb).astype(BF16)
        dm = _dot_nt(dkb, wk_ref[...]) + _dot_nt(dvb, wv_ref[...])
        _, dg, db = _ln_bwd(dm, xh_ref[...], rs_ref[...], g_ref[...])
        dg_ref[...] = dg
        db_ref[...] = db

    return pl.pallas_call(
        body,
        out_shape=[jax.ShapeDtypeStruct((d, d), BF16), jax.ShapeDtypeStruct((d, d), BF16),
                   jax.ShapeDtypeStruct((1, d), F32), jax.ShapeDtypeStruct((1, d), F32)],
        compiler_params=pltpu.CompilerParams(vmem_limit_bytes=VMEM_LIMIT_V7X),
        name="mem_bwd",
    )(dk, dv, mb, xhat, rstd, g, wk, wv)


def _adamw(w, g, m, v):
    m = ADAM_B1 * m + (1.0 - ADAM_B1) * g
    v = ADAM_B2 * v + (1.0 - ADAM_B2) * (g * g)
    m_hat = m / (1.0 - ADAM_B1 ** ADAM_STEP)
    v_hat = v / (1.0 - ADAM_B2 ** ADAM_STEP)
    delta = -ADAM_LR * (m_hat / (jnp.sqrt(v_hat) + ADAM_EPS) + ADAM_WD * w)
    return delta, m, v


def _slot_sum(ref):
    g = ref[0].astype(F32)
    for s in range(1, N_DEV):
        g = g + ref[s].astype(F32)
    return g


def _adam_sharded(lands, w, m, v, axis, name):
    rows, cols = w.shape
    nl = len(lands)
    transposed = axis == 1 and nl == 2
    if transposed:
        rows, cols = cols, rows
        tr = 256
        grid = (rows // tr,)
        wblk = pl.BlockSpec((cols, tr), lambda i: (0, i))
        lblk = [pl.BlockSpec((N_DEV, tr, a.shape[2]), lambda i: (0, i, 0)) for a in lands]
    elif axis == 1:
        tr = 256 if rows % 256 == 0 else rows
        grid = (rows // tr,)
        wblk = pl.BlockSpec((tr, cols), lambda i: (i, 0))
        lblk = [pl.BlockSpec((N_DEV, tr, a.shape[2]), lambda i: (0, i, 0)) for a in lands]
    else:
        tc = _col_tile(cols)
        grid = (cols // tc,)
        wblk = pl.BlockSpec((rows, tc), lambda i: (0, i))
        lblk = [pl.BlockSpec((N_DEV, a.shape[1], tc), lambda i: (0, 0, i)) for a in lands]

    def body(*refs):
        w_ref, m_ref, v_ref = refs[nl:nl + 3]
        g_ref, d_ref, nm_ref, nv_ref = refs[nl + 3:]
        g = _slot_sum(refs[0])
        if nl == 2:
            tail = _slot_sum(refs[1])
            if transposed:
                g = jnp.concatenate([g.T, tail.T[:cols - g.shape[1], :]], axis=0)
            elif axis == 1:
                g = jnp.concatenate([g, tail[:, :cols - g.shape[1]]], axis=1)
            else:
                g = jnp.concatenate([g, tail[:rows - g.shape[0], :]], axis=0)
        delta, nm, nv = _adamw(w_ref[...], g, m_ref[...], v_ref[...])
        g_ref[...] = g
        d_ref[...] = delta
        nm_ref[...] = nm
        nv_ref[...] = nv

    shp = jax.ShapeDtypeStruct(w.shape, F32)
    return pl.pallas_call(
        body,
        grid=grid,
        in_specs=lblk + [wblk, wblk, wblk],
        out_specs=[wblk, wblk, wblk, wblk],
        out_shape=[shp, shp, shp, shp],
        compiler_params=_params(1),
        name=name,
    )(*[pltpu.with_memory_space_constraint(a, pltpu.HBM) for a in (*lands, w, m, v)])


def _mesh_pos():
    return lax.axis_index("x"), lax.axis_index("y"), lax.axis_index("c")


def _peer(k):
    x, y, c = _mesh_pos()
    pos = (x ^ (k >> 2), y ^ ((k >> 1) & 1), c ^ (k & 1))
    return pos, 4 * pos[0] + 2 * pos[1] + pos[2]


def _sem_index(row, k):
    return row * (N_DEV - 1) + k - 1


def _window(ref, axis, start, size):
    align = 16 if axis == 0 else LANES
    start = pl.multiple_of(start, align)
    return ref.at[pl.ds(start, size), :] if axis == 0 else ref.at[:, pl.ds(start, size)]


def _piece_refs(piece, srcs, lands, me, peer):
    kind, si, li, axis, base, stride, shape = piece
    if kind == "gather":
        return srcs[si], _window(lands[li], axis, base + stride * me, shape[axis])
    return _window(srcs[si], axis, base + stride * peer, shape[axis]), lands[li].at[me]


def _place_own(srcs, land_shapes, pieces, name):
    ns, nl, npc = len(srcs), len(land_shapes), len(pieces)

    def body(*refs):
        s_refs = refs[:ns]
        l_refs = refs[ns:ns + nl]
        bufs = refs[ns + nl:ns + nl + npc]
        sems = refs[ns + nl + npc]
        x, y, c = _mesh_pos()
        me = 4 * x + 2 * y + c
        loads = []
        for p, piece in enumerate(pieces):
            src, dst = _piece_refs(piece, s_refs, l_refs, me, me)
            cp = pltpu.make_async_copy(src, bufs[p], sems.at[0, p])
            cp.start()
            loads.append((cp, dst))
        stores = []
        for p, (cp, dst) in enumerate(loads):
            cp.wait()
            out = pltpu.make_async_copy(bufs[p], dst, sems.at[1, p])
            out.start()
            stores.append(out)
        for out in stores:
            out.wait()

    out = pl.pallas_call(
        body,
        in_specs=[ANY] * ns,
        out_specs=[ANY] * nl,
        out_shape=list(land_shapes),
        scratch_shapes=[pltpu.VMEM(pc[6], srcs[pc[1]].dtype) for pc in pieces] + [pltpu.SemaphoreType.DMA((2, npc))],
        compiler_params=pltpu.CompilerParams(vmem_limit_bytes=VMEM_LIMIT_V7X),
        name=name,
    )(*srcs)
    return list(out)


def _comm_start(srcs, lands, pieces, groups, name, after=()):
    ns, nl, na, ng = len(srcs), len(lands), len(after), len(groups)

    def body(*refs):
        s_refs = refs[:ns]
        l_refs = refs[ns:ns + nl]
        outs = refs[ns + nl + na:]
        sems = outs[:2 * ng]
        token = outs[-1]
        x, y, c = _mesh_pos()
        me = 4 * x + 2 * y + c
        for g, members in enumerate(groups):
            for row, p in enumerate(members):
                for k in range(1, N_DEV):
                    pos, peer = _peer(k)
                    src, dst = _piece_refs(pieces[p], s_refs, l_refs, me, peer)
                    pltpu.make_async_remote_copy(src_ref=src, dst_ref=dst, send_sem=sems[2 * g].at[_sem_index(row, k)],
                                                 recv_sem=sems[2 * g + 1].at[_sem_index(row, k)], device_id=pos,
                                                 device_id_type=MESH_ID).start()
        token[...] = jnp.zeros_like(token)

    sem_shapes = []
    for members in groups:
        sem_shapes += [pltpu.SemaphoreType.DMA((len(members) * (N_DEV - 1),))] * 2
    hbm_of = lambda a: pltpu.HBM(a.shape, a.dtype)
    out = pl.pallas_call(
        body,
        in_specs=[HBM] * (ns + nl) + [ANY] * na,
        out_specs=[SEM] * (2 * ng) + [HBM] * (ns + nl) + [pl.BlockSpec(memory_space=pltpu.VMEM)],
        out_shape=sem_shapes + [hbm_of(a) for a in srcs] + [hbm_of(a) for a in lands]
        + [jax.ShapeDtypeStruct((8, LANES), F32)],
        input_output_aliases={i: 2 * ng + i for i in range(ns + nl)},
        compiler_params=pltpu.CompilerParams(has_side_effects=DATAFLOW),
        name=name,
    )(*[pltpu.with_memory_space_constraint(a, pltpu.HBM) for a in list(srcs) + list(lands)], *after)
    sems = [(out[2 * g], out[2 * g + 1]) for g in range(ng)]
    return sems, list(out[2 * ng:2 * ng + ns]), list(out[2 * ng + ns:2 * ng + ns + nl]), out[-1]


def _comm_wait(srcs, lands, pieces, members, sems, after, name):
    ns, nl, na = len(srcs), len(lands), len(after)

    def body(*refs):
        s_refs = refs[:ns]
        l_refs = refs[ns:ns + nl]
        send_sems, recv_sems = refs[ns + nl:ns + nl + 2]
        x, y, c = _mesh_pos()
        me = 4 * x + 2 * y + c
        for row, p in enumerate(members):
            for k in range(1, N_DEV):
                pos, peer = _peer(k)
                src, dst = _piece_refs(pieces[p], s_refs, l_refs, me, peer)
                cp = pltpu.make_async_remote_copy(src_ref=src, dst_ref=dst, send_sem=send_sems.at[_sem_index(row, k)],
                                                  recv_sem=recv_sems.at[_sem_index(row, k)], device_id=pos,
                                                  device_id_type=MESH_ID)
                cp.wait_send()
                cp.wait_recv()

    hbm_of = lambda a: pltpu.HBM(a.shape, a.dtype)
    out = pl.pallas_call(
        body,
        in_specs=[HBM] * (ns + nl) + [SEM, SEM] + [ANY] * na,
        out_specs=[HBM] * (ns + nl),
        out_shape=[hbm_of(a) for a in srcs] + [hbm_of(a) for a in lands],
        input_output_aliases={i: i for i in range(ns + nl)},
        compiler_params=pltpu.CompilerParams(has_side_effects=DATAFLOW),
        name=name,
    )(*srcs, *lands, sems[0], sems[1], *after)
    return list(out[ns:])


def _landed_block(piece, lands, owner):
    _, _, li, axis, base, stride, shape = piece
    return _window(lands[li], axis, base + stride * owner, shape[axis])


def _copy_stage(name, bufs, in_sems, out_sem_sizes, emit, after=()):
    nb, ni, no, na = len(bufs), len(in_sems), len(out_sem_sizes), len(after)

    def body(*refs):
        b_refs = refs[:nb]
        i_refs = refs[nb:nb + ni]
        o_refs = refs[nb + ni + na:nb + ni + na + no]
        emit(b_refs, i_refs, o_refs)
        refs[-1][...] = jnp.zeros_like(refs[-1])

    hbm_of = lambda a: pltpu.HBM(a.shape, a.dtype)
    out = pl.pallas_call(
        body,
        in_specs=[HBM] * nb + [SEM] * ni + [ANY] * na,
        out_specs=[SEM] * no + [HBM] * nb + [pl.BlockSpec(memory_space=pltpu.VMEM)],
        out_shape=[pltpu.SemaphoreType.DMA((n,)) for n in out_sem_sizes] + [hbm_of(a) for a in bufs]
        + [jax.ShapeDtypeStruct((8, LANES), F32)],
        input_output_aliases={i: no + i for i in range(nb)},
        compiler_params=pltpu.CompilerParams(has_side_effects=DATAFLOW),
        name=name,
    )(*[pltpu.with_memory_space_constraint(a, pltpu.HBM) for a in bufs], *in_sems, *after)
    return list(out[:no]), list(out[no:no + nb]), out[-1]


def _remote(src, dst, send, recv, to):
    return pltpu.make_async_remote_copy(src_ref=src, dst_ref=dst, send_sem=send, recv_sem=recv, device_id=to,
                                        device_id_type=MESH_ID)


def _routed_gather(srcs, lands, pieces, after, name):
    ns, npc = len(srcs), len(pieces)

    def places():
        x, y, c = _mesh_pos()
        index = lambda p: 4 * p[0] + 2 * p[1] + p[2]
        me, sib = (x, y, c), (x, y, 1 - c)
        xnb, ynb = (1 - x, y, c), (x, 1 - y, c)
        got_first = (x ^ (1 - c), y ^ c, c)
        pass_to = (x ^ c, y ^ (1 - c), c)
        diag = (1 - x, 1 - y, c)
        return index, me, sib, xnb, ynb, got_first, pass_to, diag

    def start(b, _, o):
        index, me, sib, xnb, ynb, *_rest = places()
        send_a, recv_sib, recv_nb = o
        for p, piece in enumerate(pieces):
            src, dst = _piece_refs(piece, b[:ns], b[ns:], index(me), 0)
            _remote(src, dst, send_a.at[3 * p], recv_sib.at[p], sib).start()
            _remote(src, dst, send_a.at[3 * p + 1], recv_nb.at[2 * p], xnb).start()
            _remote(src, dst, send_a.at[3 * p + 2], recv_nb.at[2 * p + 1], ynb).start()

    def pass_a(b, i, o):
        index, me, sib, xnb, ynb, got_first, pass_to, _diag = places()
        (recv_nb,) = i
        send_f, recv_f, send_d, recv_d = o
        for p, piece in enumerate(pieces):
            for j, nb in enumerate((xnb, ynb)):
                blk = _landed_block(piece, b, index(nb))
                _remote(blk, blk, send_f.at[2 * p + j], recv_nb.at[2 * p + j], sib).wait_recv()
                _remote(blk, blk, send_f.at[2 * p + j], recv_f.at[2 * p + j], sib).start()
            blk = _landed_block(piece, b, index(got_first))
            _remote(blk, blk, send_d.at[p], recv_d.at[p], pass_to).start()

    def pass_b(b, i, o):
        index, me, sib, *_mid, diag = places()
        (recv_d,) = i
        send_g, recv_g = o
        for p, piece in enumerate(pieces):
            blk = _landed_block(piece, b, index(diag))
            _remote(blk, blk, send_g.at[p], recv_d.at[p], sib).wait_recv()
            _remote(blk, blk, send_g.at[p], recv_g.at[p], sib).start()

    def last(b, i, _):
        index, me, sib, *_others = places()
        send_a, recv_sib, send_f, recv_f, send_d, send_g, recv_g = i
        for p, piece in enumerate(pieces):
            src, dst = _piece_refs(piece, b[:ns], b[ns:], index(me), 0)
            cp = lambda s_sem, r_sem: _remote(src, dst, s_sem, r_sem, sib)
            cp(send_a.at[3 * p], recv_sib.at[p]).wait_recv()
            cp(send_a.at[3 * p], recv_g.at[p]).wait_recv()
            for j in range(3):
                cp(send_a.at[3 * p + j], recv_sib.at[p]).wait_send()
            for j in range(2):
                cp(send_f.at[2 * p + j], recv_f.at[2 * p + j]).wait_recv()
                cp(send_f.at[2 * p + j], recv_f.at[2 * p + j]).wait_send()
            cp(send_d.at[p], recv_sib.at[p]).wait_send()
            cp(send_g.at[p], recv_sib.at[p]).wait_send()

    (send_a, recv_sib, recv_nb), bufs, _ = _copy_stage(name + "_start", list(srcs) + list(lands), [],
                                                       [3 * npc, npc, 2 * npc], start)
    srcs, lands = bufs[:ns], bufs[ns:]
    (send_f, recv_f, send_d, recv_d), lands, _ = _copy_stage(name + "_pass_a", lands, [recv_nb],
                                                             [2 * npc, 2 * npc, npc, npc],
                                                             lambda b, i, o: pass_a(b, i, o), after=after)
    (send_g, recv_g), lands, tok = _copy_stage(name + "_pass_b", lands, [recv_d], [npc, npc], pass_b)
    _, bufs, _ = _copy_stage(name + "_last", list(srcs) + list(lands),
                             [send_a, recv_sib, send_f, recv_f, send_d, send_g, recv_g], [], last)
    return bufs[ns:], tok


_SMALL_NAMES = ("ln1_g", "ln1_b", "hg_lb_logits", "hg_norm_g", "sg_ln_g", "sg_ln_b", "sg_w_s", "sg_b_s",
                "ln2_g", "ln2_b", "mem_ln_g", "mem_ln_b", "ln3_g", "ln3_b", "ln4_g", "ln4_b")


_VEC_NAMES = ("ln1_g", "ln1_b", "ln2_g", "ln2_b", "mem_ln_g", "mem_ln_b", "ln3_g", "ln3_b", "ln4_g", "ln4_b")
_ROW_NAMES = ("hg_lb_logits", "hg_norm_g", "sg_ln_g", "sg_ln_b", "sg_b_s", "sg_w_s")
VEC_ROWS = 16


def _row_plan(shapes):
    plan, pos = {}, 0
    for k in _ROW_NAMES:
        shp = shapes[k]
        slabs, off = [], pos
        for idx in itertools.product(*[range(dim) for dim in shp[:-2]]):
            slabs.append((idx, off, shp[-2]))
            off += shp[-2]
        plan[k] = (pos, slabs)
        pos = -(-off // 8) * 8
    return plan, -(-pos // 16) * 16


def _pack_small_grads(gs, shapes, loss):
    d = gs[_VEC_NAMES[0]].size
    vec = jnp.concatenate([gs[k].reshape(1, -1) for k in _VEC_NAMES] + [jnp.tile(loss, (1, d // LANES))], axis=0)
    vec = jnp.pad(vec, ((0, VEC_ROWS - vec.shape[0]), (0, 0)))
    plan, total = _row_plan(shapes)
    parts, pos = [], 0
    for k in _ROW_NAMES:
        first, slabs = plan[k]
        rows = gs[k].reshape(-1, LANES)
        end = slabs[-1][1] + slabs[-1][2]
        nxt = -(-end // 8) * 8
        parts.append(jnp.pad(rows, ((0, nxt - first - rows.shape[0]), (0, 0))))
        pos = nxt
    parts.append(jnp.zeros((total - pos, LANES), F32))
    return vec, jnp.concatenate(parts, axis=0)


def _adam_small(land_vec, land_rows, w, m, v):
    names = _VEC_NAMES + _ROW_NAMES
    n = len(names)
    shapes = {k: w[k].shape for k in names}
    plan, _ = _row_plan(shapes)

    def body(*refs):
        lv_ref, lr_ref = refs[:2]
        w_refs, m_refs, v_refs = refs[2:2 + n], refs[2 + n:2 + 2 * n], refs[2 + 2 * n:2 + 3 * n]
        outs = refs[2 + 3 * n:2 + 7 * n]
        loss_ref = refs[2 + 7 * n]
        gv_s, gr_s = refs[3 + 7 * n:]
        gv_s[...] = _slot_sum(lv_ref)
        gr_s[...] = _slot_sum(lr_ref)
        loss_ref[...] = gv_s[len(_VEC_NAMES):len(_VEC_NAMES) + 1, :LANES]
        for p, k in enumerate(names):
            if k in _VEC_NAMES:
                row = _VEC_NAMES.index(k)
                slabs = [((), None, None)]
            else:
                slabs = plan[k][1]
            for idx, off, rows in slabs:
                g = gv_s[row:row + 1, :] if off is None else gr_s[off:off + rows, :]
                sel = idx + (slice(None), slice(None))
                delta, nm, nv = _adamw(w_refs[p][sel], g, m_refs[p][sel], v_refs[p][sel])
                for o, val in zip(range(4), (g, delta, nm, nv)):
                    outs[o * n + p][sel] = val

    flat = lambda tree: [tree[k] for k in names]
    shp = [jax.ShapeDtypeStruct(shapes[k], F32) for k in names]
    out = pl.pallas_call(
        body,
        out_shape=shp * 4 + [jax.ShapeDtypeStruct((1, LANES), F32)],
        scratch_shapes=[pltpu.VMEM(land_vec.shape[1:], F32), pltpu.VMEM(land_rows.shape[1:], F32)],
        name="adam_small",
    )(land_vec, land_rows, *flat(w), *flat(m), *flat(v))
    return [dict(zip(names, out[o * n:(o + 1) * n])) for o in range(4)], out[4 * n]


_COL_FFN = ("ffn1_w_gate", "ffn1_w_up", "ffn2_w_gate", "ffn2_w_up")
_ROW_FFN = ("ffn1_w_down", "ffn2_w_down")
_ROW_SQ = ("w_out", "xa_w_q", "xa_w_k", "xa_w_v", "xa_w_o")
_BIG_NAMES = ("ffn1_w_gate", "ffn1_w_up", "ffn1_w_down", "w_in", "w_out", "xa_w_q", "xa_w_k", "xa_w_v", "xa_w_o",
              "ffn2_w_gate", "ffn2_w_up", "ffn2_w_down")


def _ffn_split(fs):
    main = (fs // MXU_WIDTH_V7X) * MXU_WIDTH_V7X
    tail = fs - main
    tail_pad = -(-tail // LANES) * LANES
    assert main > 0 and tail > 0
    return main, tail, tail_pad


def _layout(name, shard_shape):
    r, c = shard_shape
    if name in _COL_FFN:
        main, tail, pad = _ffn_split(c)
        return (r, N_DEV * (main + pad)), [(1, 0, main, (r, main), (0, main)),
                                           (1, N_DEV * main, pad, (r, pad), (main, c))]
    if name in _ROW_FFN:
        main, tail, pad = _ffn_split(r)
        return (N_DEV * (main + pad), c), [(0, 0, main, (main, c), (0, main)),
                                           (0, N_DEV * main, pad, (pad, c), (main, r))]
    if name == "w_in":
        return (r, N_DEV * c), [(1, 0, c, (r, c), (0, c))]
    return (N_DEV * r, c), [(0, 0, r, (r, c), (0, r))]


def _shard_pieces(name, shard):
    out = []
    for axis, _, _, shape, (lo, hi) in _layout(name, shard.shape)[1]:
        part = shard[lo:hi, :] if axis == 0 else shard[:, lo:hi]
        pad = [(0, shape[0] - part.shape[0]), (0, shape[1] - part.shape[1])]
        out.append(jnp.pad(part, pad).astype(BF16))
    return out


def _gather_plan(names, shards):
    srcs, land_shapes, pieces, index = [], [], [], {}
    for li, name in enumerate(names):
        shape2d, parts = _layout(name, shards[name].shape)
        land_shapes.append(jax.ShapeDtypeStruct(shape2d, BF16))
        index[name] = []
        for (axis, base, stride, shape, _), src in zip(parts, _shard_pieces(name, shards[name])):
            index[name].append(len(pieces))
            pieces.append(("gather", len(srcs), li, axis, base, stride, shape))
            srcs.append(src)
    return srcs, land_shapes, pieces, index


def _scatter_plan(names, grads, shard_shapes):
    srcs, land_shapes, pieces, index = [], [], [], {}
    for si, name in enumerate(names):
        _, parts = _layout(name, shard_shapes[name])
        srcs.append(grads[name])
        index[name] = []
        for axis, base, stride, shape, _ in parts:
            index[name].append(len(land_shapes))
            pieces.append(("scatter", si, len(land_shapes), axis, base, stride, shape))
            land_shapes.append(jax.ShapeDtypeStruct((N_DEV,) + shape, grads[name].dtype))
    return srcs, land_shapes, pieces, index


def _small_views(small):
    row = lambda a: a.reshape(1, -1)
    ln = {k: row(small[k]) for k in ("ln1_g", "ln1_b", "ln2_g", "ln2_b", "ln3_g", "ln3_b", "ln4_g", "ln4_b",
                                      "mem_ln_g", "mem_ln_b", "hg_norm_g")}
    sg_w = small["sg_w_s"].reshape(SG_GROUPS, SG_CHUNK, SG_CHUNK)
    sg = dict(logits=jnp.swapaxes(small["hg_lb_logits"], 0, 1),
              g=small["sg_ln_g"].reshape(SG_GROUPS, 1, SG_DIM), b=small["sg_ln_b"].reshape(SG_GROUPS, 1, SG_DIM),
              w=sg_w, wt=jnp.swapaxes(sg_w, 1, 2), bs=small["sg_b_s"].reshape(SG_GROUPS, SG_CHUNK, 1))
    return ln, sg


def _forward(x, mem, target, get_w, small, first_deps=()):
    ln, sg = _small_views(small)
    xb = x.astype(BF16)
    a1, b1, s1 = _ffn_up(xb, get_w("ffn1_w_gate", ()), get_w("ffn1_w_up", ()), "ffn1_up", deps=first_deps)
    h1b, xh1, rs1 = _mm_res_ln(s1, get_w("ffn1_w_down", (s1,)), x, ln["ln1_g"], ln["ln1_b"], 0.5, "ffn1_down_ln")
    proj = _mm_nn(h1b, get_w("w_in", (h1b,)), "mix_in")
    oraw, mix, states = _hgrn_fwd(proj, sg["logits"], ln["hg_norm_g"])
    mix = _sgu_fwd(proj, mix, sg["g"], sg["b"], sg["w"], sg["bs"])
    h2b, xh2, rs2 = _mm_res_ln(mix, get_w("w_out", (mix,)), (xh1, ln["ln1_g"], ln["ln1_b"]), ln["ln2_g"], ln["ln2_b"],
                               1.0, "mix_out_ln")
    mb, mxh, mrs, kb, vb = _mem_kv(mem, ln["mem_ln_g"], ln["mem_ln_b"], get_w("xa_w_k", (h2b,)), get_w("xa_w_v", (h2b,)))
    qb, att = _attn_fwd(h2b, get_w("xa_w_q", (kb,)), kb, vb)
    h3b, xh3, rs3 = _mm_res_ln(att, get_w("xa_w_o", (att,)), (xh2, ln["ln2_g"], ln["ln2_b"]), ln["ln3_g"], ln["ln3_b"],
                               1.0, "attn_out_ln")
    a2, b2, s2 = _ffn_up(h3b, get_w("ffn2_w_gate", (h3b,)), get_w("ffn2_w_up", (h3b,)), "ffn2_up")
    loss, dy4, dy4b, dg4, db4 = _mm_res_ln(s2, get_w("ffn2_w_down", (s2,)), (xh3, ln["ln3_g"], ln["ln3_b"]),
                                           ln["ln4_g"], ln["ln4_b"], 0.5, "ffn2_down_ln_loss", target=target)
    return dict(xb=xb, a1=a1, b1=b1, s1=s1, h1b=h1b, xh1=xh1, rs1=rs1, proj=proj, oraw=oraw, mix=mix, states=states,
                h2b=h2b, xh2=xh2, rs2=rs2, mb=mb, mxh=mxh, mrs=mrs, kb=kb, vb=vb, qb=qb, att=att, h3b=h3b, xh3=xh3,
                rs3=rs3, a2=a2, b2=b2, s2=s2, loss=loss, dy4=dy4, dy4b=dy4b, dg4=dg4, db4=db4)


def _backward(sv, wt, small, send):
    ln, sg = _small_views(small)
    gs = {"ln4_g": sv["dg4"], "ln4_b": sv["db4"]}
    loss, dy4, dy4b = sv["loss"], sv["dy4"], sv["dy4b"]
    g_down2 = _mm_tn(sv["s2"], dy4b, "g_ffn2_down", scale=0.5)
    da2, db2, dy3, dy3b, gs["ln3_g"], gs["ln3_b"] = _ffn_bwd_fused(
        dy4, dy4b, wt["ffn2_w_down"], wt["ffn2_w_gate"], wt["ffn2_w_up"], sv["a2"], sv["b2"], 0.5,
        (sv["xh3"], sv["rs3"], ln["ln3_g"]), "ffn2_bwd")
    g_gate2 = _mm_tn(sv["h3b"], da2, "g_ffn2_gate")
    g_up2 = _mm_tn(sv["h3b"], db2, "g_ffn2_up")
    tok = send(("ffn2_w_down", "ffn2_w_gate", "ffn2_w_up"), (g_down2, g_gate2, g_up2))

    g_o = _mm_tn(sv["att"], dy3b, "g_xa_o", deps=(tok,))
    dqb, dk, dv = _attn_bwd(dy3b, wt["xa_w_o"], sv["qb"], sv["kb"], sv["vb"])
    g_q = _mm_tn(sv["h2b"], dqb, "g_xa_q")
    g_k, g_v, gs["mem_ln_g"], gs["mem_ln_b"] = _mem_bwd(dk, dv, sv["mb"], sv["mxh"], sv["mrs"], ln["mem_ln_g"],
                                                        wt["xa_w_k"], wt["xa_w_v"])
    tok = send(("xa_w_o", "xa_w_q", "xa_w_k", "xa_w_v"), (g_o, g_q, g_k, g_v))
    dy2, dy2b, gs["ln2_g"], gs["ln2_b"] = _dx_ln(dy3, [(dqb, wt["xa_w_q"])], (sv["xh2"], sv["rs2"], ln["ln2_g"]),
                                                 "attn_dx_ln", deps=(tok,))

    g_out = _mm_tn(sv["mix"], dy2b, "g_w_out")
    dmix = _mm_nt(dy2b, wt["w_out"], "mix_out_bwd")
    dq, dfz, div, dgg, dlg, dgn = _hgrn_bwd(sv["proj"], sv["oraw"], dmix, sv["states"], sg["logits"], ln["hg_norm_g"])
    du, dvv, gs["sg_ln_g"], gs["sg_ln_b"], gs["sg_w_s"], gs["sg_b_s"] = _sgu_bwd(
        sv["proj"], dmix, sg["g"], sg["b"], sg["w"], sg["wt"], sg["bs"])
    gs["hg_lb_logits"] = jnp.swapaxes(dlg, 0, 1)
    gs["hg_norm_g"] = jnp.sum(dgn, axis=0)
    dproj = jnp.concatenate([dq, dfz, div, dgg, du, dvv], axis=1)
    g_in = _mm_tn(sv["h1b"], dproj, "g_w_in")
    tok = send(("w_out", "w_in"), (g_out, g_in))
    dy1, dy1b, gs["ln1_g"], gs["ln1_b"] = _dx_ln(dy2, [(dproj, wt["w_in"])], (sv["xh1"], sv["rs1"], ln["ln1_g"]),
                                                 "mix_dx_ln", deps=(tok,))

    g_down1 = _mm_tn(sv["s1"], dy1b, "g_ffn1_down", scale=0.5)
    tok = send(("ffn1_w_down",), (g_down1,))
    da1, db1 = _ffn_bwd_act(dy1b, wt["ffn1_w_down"], sv["a1"], sv["b1"], 0.5, "ffn1_bwd_act", deps=(tok,))
    g_gate1 = _mm_tn(sv["xb"], da1, "g_ffn1_gate")
    tok = send(("ffn1_w_gate",), (g_gate1,))
    g_up1 = _mm_tn(sv["xb"], db1, "g_ffn1_up", deps=(tok,))
    tok = send(("ffn1_w_up",), (g_up1,))
    grad_x = _dx_ln(dy1, [(da1, wt["ffn1_w_gate"]), (db1, wt["ffn1_w_up"])], None, "ffn1_dx", deps=(tok,))
    return loss, grad_x, gs


_WEIGHT_NAMES = ("ffn1_w_gate", "ffn1_w_up", "ffn1_w_down", "ln1_g", "ln1_b", "w_in", "hg_lb_logits", "hg_norm_g",
                 "sg_ln_g", "sg_ln_b", "sg_w_s", "sg_b_s", "w_out", "ln2_g", "ln2_b", "mem_ln_g", "mem_ln_b",
                 "xa_w_q", "xa_w_k", "xa_w_v", "xa_w_o", "ln3_g", "ln3_b", "ffn2_w_gate", "ffn2_w_up", "ffn2_w_down",
                 "ln4_g", "ln4_b")
_FIRST = ("ffn1_w_gate", "ffn1_w_up")
_SECOND = ("ffn1_w_down", "w_in", "w_out")
_THIRD = ("xa_w_k", "xa_w_v", "xa_w_q", "xa_w_o", "ffn2_w_gate", "ffn2_w_up", "ffn2_w_down")


def kernel(x, mem, ffn1_w_gate, ffn1_w_up, ffn1_w_down, ln1_g, ln1_b, w_in, hg_lb_logits, hg_norm_g, sg_ln_g, sg_ln_b, sg_w_s, sg_b_s, w_out, ln2_g, ln2_b, mem_ln_g, mem_ln_b, xa_w_q, xa_w_k, xa_w_v, xa_w_o, ln3_g, ln3_b, ffn2_w_gate, ffn2_w_up, ffn2_w_down, ln4_g, ln4_b, loss_target, m_ffn1_w_gate, m_ffn1_w_up, m_ffn1_w_down, m_ln1_g, m_ln1_b, m_w_in, m_hg_lb_logits, m_hg_norm_g, m_sg_ln_g, m_sg_ln_b, m_sg_w_s, m_sg_b_s, m_w_out, m_ln2_g, m_ln2_b, m_mem_ln_g, m_mem_ln_b, m_xa_w_q, m_xa_w_k, m_xa_w_v, m_xa_w_o, m_ln3_g, m_ln3_b, m_ffn2_w_gate, m_ffn2_w_up, m_ffn2_w_down, m_ln4_g, m_ln4_b, v_ffn1_w_gate, v_ffn1_w_up, v_ffn1_w_down, v_ln1_g, v_ln1_b, v_w_in, v_hg_lb_logits, v_hg_norm_g, v_sg_ln_g, v_sg_ln_b, v_sg_w_s, v_sg_b_s, v_w_out, v_ln2_g, v_ln2_b, v_mem_ln_g, v_mem_ln_b, v_xa_w_q, v_xa_w_k, v_xa_w_v, v_xa_w_o, v_ln3_g, v_ln3_b, v_ffn2_w_gate, v_ffn2_w_up, v_ffn2_w_down, v_ln4_g, v_ln4_b):
    args = dict(locals())
    w = {k: args[k] for k in _WEIGHT_NAMES}
    m = {k: args["m_" + k] for k in _WEIGHT_NAMES}
    v = {k: args["v_" + k] for k in _WEIGHT_NAMES}
    shards = {k: w[k][0] for k in _BIG_NAMES}
    shard_shapes = {k: shards[k].shape for k in _BIG_NAMES}
    small = {k: (w[k][0] if k != "hg_lb_logits" else w[k]) for k in _SMALL_NAMES}

    srcs1, shapes1, pieces1, idx1 = _gather_plan(_FIRST, shards)
    lands1 = _place_own(srcs1, shapes1, pieces1, "gather_first_own")
    rest = _SECOND + _THIRD
    srcs2, shapes2, pieces2, idx2 = _gather_plan(rest, shards)
    lands2 = _place_own(srcs2, shapes2, pieces2, "gather_rest_own")
    groups2 = [list(idx2[k]) for k in rest]
    lands1, tok1 = _routed_gather(srcs1, lands1, pieces1, tuple(lands2), "gather_first")
    sems2, srcs2, lands2, tok2 = _comm_start(srcs2, lands2, pieces2, groups2, "gather_rest_start", after=(tok1,))
    wt = dict(zip(_FIRST, lands1))
    pending = {k: gi for gi, k in enumerate(rest)}

    def get_w(name, after):
        if name in pending:
            gi = pending.pop(name)
            si = [pieces2[p][1] for p in groups2[gi]]
            sub = [(pieces2[p][0], row, 0) + pieces2[p][3:] for row, p in enumerate(groups2[gi])]
            wt[name] = _comm_wait([srcs2[s] for s in si], [lands2[gi]], sub, list(range(len(sub))), sems2[gi],
                                  after, "gather_wait_" + name)[0]
        return wt[name]

    sv = _forward(x[0], mem[0], loss_target[0], get_w, small, first_deps=(tok2,))

    sent = []

    def send(names, grads):
        srcs, shapes, pieces, idx = _scatter_plan(names, dict(zip(names, grads)), shard_shapes)
        lands = _place_own(srcs, shapes, pieces, "grads_own_%d" % len(sent))
        sems, srcs, lands, tok = _comm_start(srcs, lands, pieces, [list(range(len(pieces)))],
                                             "grads_start_%d" % len(sent))
        sent.append((names, srcs, lands, pieces, idx, sems[0]))
        return tok

    loss, grad_x, gs = _backward(sv, wt, small, send)

    ssrc = list(_pack_small_grads(gs, {k: w[k].shape for k in _SMALL_NAMES}, loss))
    sp = [("scatter", i, i, 0, 0, 0, a.shape) for i, a in enumerate(ssrc)]
    sshape = [jax.ShapeDtypeStruct((N_DEV,) + a.shape, F32) for a in ssrc]
    sl = _place_own(ssrc, sshape, sp, "small_own")
    ssem, ssrc, sl, _ = _comm_start(ssrc, sl, sp, [[0, 1]], "small_start")

    out_g, out_d, out_m, out_v = {}, {}, {}, {}
    after = (grad_x,)
    for n_sent, (names, srcs, lands, pieces, idx, sems) in enumerate(sent):
        lands = _comm_wait(srcs, lands, pieces, list(range(len(pieces))), sems, after, "grads_wait_%d" % n_sent)
        for k in names:
            axis = 1 if (k in _COL_FFN or k == "w_in") else 0
            if k in _COL_FFN:
                res = _adam_sharded([lands[i] for i in idx[k]], w[k][0].T, m[k][0].T, v[k][0].T, axis, "adam_" + k)
                res = [r.T for r in res]
            else:
                res = _adam_sharded([lands[i] for i in idx[k]], w[k][0], m[k][0], v[k][0], axis, "adam_" + k)
            out_g[k], out_d[k], out_m[k], out_v[k] = [r[None] for r in res]
        after = (out_v[names[-1]],)
    sl = _comm_wait(ssrc, sl, sp, [0, 1], ssem[0], after, "small_wait")
    small_out, loss_sum = _adam_small(sl[0], sl[1], w, m, v)
    for dst, res in zip((out_g, out_d, out_m, out_v), small_out):
        dst.update(res)
    loss_all = loss_sum[0, 0]
    return (loss_all, grad_x[None], *[out_g[k] for k in _WEIGHT_NAMES], *[out_d[k] for k in _WEIGHT_NAMES],
            *[out_m[k] for k in _WEIGHT_NAMES], *[out_v[k] for k in _WEIGHT_NAMES])
```

```python
import itertools

import jax
import jax.numpy as jnp
import numpy as np
from jax import lax
from jax.experimental import pallas as pl
from jax.experimental.pallas import tpu as pltpu

F32 = jnp.float32
BF16 = jnp.bfloat16

N_DEV = 8
ALPHA = 2.0 ** 0.25
LN_EPS = 1e-5
HG_HEADS = 4
HG_DIM = 128
SG_GROUPS = 4
SG_DIM = 128
SG_CHUNK = 128
X_HEADS = 4
HG_BLOCK = 16
HG_UNROLL = 8
ADAM_LR = 0.001
ADAM_B1 = 0.9
ADAM_B2 = 0.999
ADAM_EPS = 1e-08
ADAM_WD = 0.01
ADAM_STEP = 10
VMEM_LIMIT_V7X = 48 * 1024 * 1024
MXU_WIDTH_V7X = 256
LANES = 128
MESH_ID = pl.DeviceIdType.MESH
ANY = pl.BlockSpec(memory_space=pl.ANY)
HBM = pl.BlockSpec(memory_space=pltpu.HBM)
SEM = pl.BlockSpec(memory_space=pltpu.SEMAPHORE)
DATAFLOW = pltpu.SideEffectType.DATAFLOW_SIDE_EFFECTING


def _params(n_axes):
    return pltpu.CompilerParams(dimension_semantics=("arbitrary",) * n_axes, vmem_limit_bytes=VMEM_LIMIT_V7X)


def _dot(a, b):
    return jnp.dot(a, b, preferred_element_type=F32)


def _dot_nt(a, b):
    return lax.dot_general(a, b, (((1,), (1,)), ((), ())), preferred_element_type=F32)


def _dot_tn(a, b):
    return lax.dot_general(a, b, (((0,), (0,)), ((), ())), preferred_element_type=F32)


def _sigmoid(x):
    return 1.0 / (1.0 + jnp.exp(-x))


def _silu_and_grad(a):
    sig = _sigmoid(a)
    return a * sig, sig * (1.0 + a * (1.0 - sig))


_GELU_C = 0.7978845608028654


def _gelu_and_grad(x):
    inner = _GELU_C * (x + 0.044715 * x * x * x)
    t = jnp.tanh(inner)
    val = 0.5 * x * (1.0 + t)
    grad = 0.5 * (1.0 + t) + 0.5 * x * (1.0 - t * t) * _GELU_C * (1.0 + 3.0 * 0.044715 * x * x)
    return val, grad


def _ln_fwd(y, g, b):
    mu = jnp.mean(y, axis=-1, keepdims=True)
    yc = y - mu
    var = jnp.mean(yc * yc, axis=-1, keepdims=True)
    rstd = lax.rsqrt(var + LN_EPS)
    xhat = yc * rstd
    return xhat * g + b, xhat, rstd


def _ln_bwd(dh, xhat, rstd, g):
    dxh = dh * g
    m1 = jnp.mean(dxh, axis=-1, keepdims=True)
    m2 = jnp.mean(dxh * xhat, axis=-1, keepdims=True)
    dy = rstd * (dxh - m1 - xhat * m2)
    dg = jnp.sum(dh * xhat, axis=0, keepdims=True)
    db = jnp.sum(dh, axis=0, keepdims=True)
    return dy, dg, db


def _mask_dot(mask, x):
    hi = x.astype(BF16)
    lo = (x - hi.astype(F32)).astype(BF16)
    n = mask.shape[0]
    parts = [_dot(mask, hi[r:r + n, :]) + _dot(mask, lo[r:r + n, :]) for r in range(0, x.shape[0], n)]
    return parts[0] if len(parts) == 1 else jnp.concatenate(parts, axis=0)


def _block_masks(n):
    r = np.arange(n)[:, None]
    c = np.arange(n)[None, :]
    same = (r // HG_BLOCK) == (c // HG_BLOCK)
    return jnp.asarray(np.stack([same & (c <= r), same & (c >= r), same]), BF16)


def _row_tile(t):
    return min(t, 512)


def _col_tile(n):
    for cand in (512, 256, 128):
        if n % cand == 0:
            return cand
    return n


def _resident(w):
    return pl.BlockSpec(w.shape, lambda *_: (0, 0), pipeline_mode=pl.Buffered(1))


def _drop_deps(body, n_in, n_deps):
    if n_deps == 0:
        return body
    return lambda *refs: body(*refs[:n_in], *refs[n_in + n_deps:])


def _ffn_up(hb, wg, wu, name, deps=()):
    t, d = hb.shape
    f = wg.shape[1]
    tm = _row_tile(t)
    tn = _col_tile(f)

    def body(h_ref, wg_ref, wu_ref, a_ref, b_ref, s_ref):
        h = h_ref[...]
        for c in range(f // tn):
            cols = slice(c * tn, (c + 1) * tn)
            a = _dot(h, wg_ref[:, cols])
            b = _dot(h, wu_ref[:, cols])
            a_ref[:, cols] = a.astype(BF16)
            b_ref[:, cols] = b.astype(BF16)
            s_ref[:, cols] = (a * _sigmoid(a) * b).astype(BF16)

    act = pl.BlockSpec((tm, f), lambda i: (i, 0))
    return pl.pallas_call(
        _drop_deps(body, 3, len(deps)),
        grid=(t // tm,),
        in_specs=[pl.BlockSpec((tm, d), lambda i: (i, 0)), _resident(wg), _resident(wu)] + [ANY] * len(deps),
        out_specs=[act, act, act],
        out_shape=[jax.ShapeDtypeStruct((t, f), BF16)] * 3,
        compiler_params=_params(1),
        name=name,
    )(hb, wg, wu, *deps)


def _mm_res_ln(lhs, w, res, g, b, coef, name, target=None, ffn=None):
    t, kd = lhs.shape
    d = w.shape[1]
    tm = _row_tile(t) if ffn is None else min(t, 256)
    nt = t // tm
    from_norm = isinstance(res, tuple)
    n_res = 3 if from_norm else 1
    n_ffn = 0 if ffn is None else 2
    f = w.shape[0]
    tn = _col_tile(f)

    def body(*refs):
        l_ref, w_ref = refs[:2]
        r_refs = refs[2 + n_ffn:2 + n_ffn + n_res]
        g_ref, b_ref = refs[2 + n_ffn + n_res:4 + n_ffn + n_res]
        rest = refs[4 + n_ffn + n_res:]
        if target is not None:
            t_ref, rest = rest[0], rest[1:]
        if ffn is None:
            prod = _dot(l_ref[...], w_ref[...])
        else:
            wg_ref, wu_ref = refs[2:4]
            a_ref, bb_ref = rest[:2]
            rest = rest[2:]
            hv = l_ref[...]
            prod = None
            for c in range(f // tn):
                cols = slice(c * tn, (c + 1) * tn)
                a = _dot(hv, wg_ref[:, cols])
                bu = _dot(hv, wu_ref[:, cols])
                a_ref[:, cols] = a.astype(BF16)
                bb_ref[:, cols] = bu.astype(BF16)
                part = _dot((a * _sigmoid(a) * bu).astype(BF16), w_ref[cols, :])
                prod = part if prod is None else prod + part
        prev = r_refs[0][...] * r_refs[1][...] + r_refs[2][...] if from_norm else r_refs[0][...]
        y = ALPHA * prev + coef * prod
        h, xhat, rstd = _ln_fwd(y, g_ref[...], b_ref[...])
        if target is None:
            hb_ref, xh_ref, rs_ref = rest
            hb_ref[...] = h.astype(BF16)
            xh_ref[...] = xhat
            rs_ref[...] = rstd
            return
        loss_ref, dy_ref, dyb_ref, dg_ref, db_ref, lacc = rest
        i = pl.program_id(0)

        @pl.when(i == 0)
        def _():
            lacc[...] = jnp.zeros_like(lacc)
            dg_ref[...] = jnp.zeros_like(dg_ref)
            db_ref[...] = jnp.zeros_like(db_ref)

        err = h - t_ref[...]
        lacc[...] += jnp.sum(err * err, axis=0, keepdims=True)
        dy, dg, db = _ln_bwd(err * (1.0 / d), xhat, rstd, g_ref[...])
        dy_ref[...] = dy
        dyb_ref[...] = dy.astype(BF16)
        dg_ref[...] += dg
        db_ref[...] += db

        @pl.when(i == nt - 1)
        def _():
            loss_ref[...] = jnp.zeros_like(loss_ref) + jnp.sum(lacc[...], axis=1, keepdims=True) * (0.5 / d)

    row = pl.BlockSpec((tm, d), lambda i: (i, 0))
    vec = pl.BlockSpec((1, d), lambda i: (0, 0))
    res_specs = [row, vec, vec] if from_norm else [row]
    res_args = list(res) if from_norm else [res]
    ffn_specs = [] if ffn is None else [_resident(ffn[0]), _resident(ffn[1])]
    in_specs = [pl.BlockSpec((tm, kd), lambda i: (i, 0)), _resident(w)] + ffn_specs + res_specs + [vec, vec]
    args = [lhs, w] + list(ffn or ()) + res_args + [g, b]
    if target is None:
        out_specs = [row, row, pl.BlockSpec((tm, 1), lambda i: (i, 0))]
        out_shape = [jax.ShapeDtypeStruct((t, d), BF16), jax.ShapeDtypeStruct((t, d), F32),
                     jax.ShapeDtypeStruct((t, 1), F32)]
        scratch = []
    else:
        in_specs.append(row)
        args.append(target)
        out_specs = [pl.BlockSpec((1, LANES), lambda i: (0, 0)), row, row, vec, vec]
        out_shape = [jax.ShapeDtypeStruct((1, LANES), F32), jax.ShapeDtypeStruct((t, d), F32),
                     jax.ShapeDtypeStruct((t, d), BF16), jax.ShapeDtypeStruct((1, d), F32),
                     jax.ShapeDtypeStruct((1, d), F32)]
        scratch = [pltpu.VMEM((1, d), F32)]
    if ffn is not None:
        act = pl.BlockSpec((tm, f), lambda i: (i, 0))
        out_specs = [act, act] + out_specs
        out_shape = [jax.ShapeDtypeStruct((t, f), BF16)] * 2 + out_shape
    return pl.pallas_call(
        body,
        grid=(nt,),
        in_specs=in_specs,
        out_specs=out_specs,
        out_shape=out_shape,
        scratch_shapes=scratch,
        compiler_params=_params(1),
        name=name,
    )(*args)


def _mm_nn(lhs, w, name):
    t, kd = lhs.shape
    n = w.shape[1]
    tm = _row_tile(t)
    tn = _col_tile(n)

    def body(l_ref, w_ref, o_ref):
        lhs_v = l_ref[...]
        for c in range(n // tn):
            cols = slice(c * tn, (c + 1) * tn)
            o_ref[:, cols] = _dot(lhs_v, w_ref[:, cols])

    return pl.pallas_call(
        body,
        grid=(t // tm,),
        in_specs=[pl.BlockSpec((tm, kd), lambda i: (i, 0)), _resident(w)],
        out_specs=pl.BlockSpec((tm, n), lambda i: (i, 0)),
        out_shape=jax.ShapeDtypeStruct((t, n), F32),
        compiler_params=_params(1),
        name=name,
    )(lhs, w)


def _lower_bound(lg):
    m = jnp.max(lg, axis=0, keepdims=True)
    e = jnp.exp(lg - m)
    return e[0:1, :] / jnp.sum(e, axis=0, keepdims=True)


def _forget_terms(fz, lb):
    e = jnp.exp(-jnp.abs(fz))
    r = 1.0 / (1.0 + e)
    pos = fz >= 0.0
    sig = jnp.where(pos, r, e * r)
    nsig = jnp.where(pos, e * r, r)
    f = lb + (1.0 - lb) * sig
    k = (1.0 - lb) * nsig
    return sig, nsig, f, k


def _hg_tile(t):
    return min(t, 1024)


HG_HALF = HG_BLOCK // 2
NEG_BIG = -1e30


def _halves(a):
    return a[:HG_HALF, :], a[HG_HALF:, :]


def _causal_halves(s):
    return (0, 1) if s < HG_HALF else (1,)


def _decay_from(b_half, b_s, s, h, tidx):
    first = s - h * HG_HALF
    diff = b_half - b_s
    if first > 0:
        diff = jnp.where(tidx >= first, diff, NEG_BIG)
    return jnp.exp(diff)


def _hgrn_fwd(proj, logits, gn):
    t = proj.shape[0]
    ct = _hg_tile(t)
    nct = t // ct
    nblk = ct // HG_BLOCK
    nh = HG_HEADS
    mrows = min(ct, 256)

    def body(q_ref, fz_ref, iv_ref, gg_ref, lg_ref, gn_ref, mask_ref, oraw_ref, oa_ref, st_ref,
             state, qt_s, kt_s, k_s, b_s, dec_s):
        c = pl.program_id(1)

        @pl.when(c == 0)
        def _():
            state[...] = jnp.zeros_like(state)

        lb = _lower_bound(lg_ref[...])
        q = q_ref[...]
        _, _, f, k = _forget_terms(fz_ref[...], lb)
        logf = jnp.log(f)
        b = _mask_dot(mask_ref[0], logf)
        bend = _mask_dot(mask_ref[2], logf)
        qt_s[...] = (q * jnp.exp(b)).astype(BF16)
        kt_s[...] = (k * jnp.exp(bend - b)).astype(BF16)
        k_s[...] = k
        b_s[...] = b
        dec_s[...] = jnp.exp(bend)
        tidx = lax.broadcasted_iota(jnp.int32, (HG_HALF, HG_DIM), 0)

        def blk(i, carry):
            r0 = pl.multiple_of(i * HG_BLOCK, HG_BLOCK)
            rows = pl.ds(r0, HG_BLOCK)
            st = state[...]
            stb = st.astype(BF16)
            st_ref[i] = stb
            v = iv_ref[rows, :]
            qq = q_ref[rows, :]
            kk = k_s[rows, :]
            bb = b_s[rows, :]
            o = list(_halves(_dot_nt(qt_s[rows, :], stb)))
            qh, bh = _halves(qq), _halves(bb)
            for s in range(HG_BLOCK):
                ks, vs = kk[s:s + 1, :], v[s:s + 1, :]
                for h in _causal_halves(s):
                    e = _decay_from(bh[h], bb[s:s + 1, :], s, h, tidx)
                    acol = jnp.sum(qh[h] * (ks * e), axis=1, keepdims=True)
                    o[h] = o[h] + acol * vs
            oraw_ref[rows, :] = jnp.concatenate(o, axis=0)
            state[...] = st * dec_s[pl.ds(r0, 1), :] + _dot_tn(v.astype(BF16), kt_s[rows, :])
            return carry

        lax.fori_loop(0, nblk, blk, 0, unroll=2 * HG_UNROLL)
        oraw = oraw_ref[...]
        r = lax.rsqrt(jnp.mean(oraw * oraw, axis=-1, keepdims=True) + LN_EPS)
        gg = gg_ref[...]
        oa_ref[...] = (oraw * r * gn_ref[...] * gg * _sigmoid(gg)).astype(BF16)

    def slab(off):
        return pl.BlockSpec((ct, HG_DIM), lambda h, c: (c, off + h))

    out_slab = pl.BlockSpec((ct, HG_DIM), lambda h, c: (c, h))
    return pl.pallas_call(
        body,
        grid=(nh, nct),
        in_specs=[slab(0), slab(nh), slab(2 * nh), slab(3 * nh),
                  pl.BlockSpec((None, 2, HG_DIM), lambda h, c: (h, 0, 0)),
                  pl.BlockSpec((1, HG_DIM), lambda h, c: (0, 0)),
                  pl.BlockSpec((3, mrows, mrows), lambda h, c: (0, 0, 0))],
        out_specs=[out_slab, out_slab, pl.BlockSpec((None, nblk, HG_DIM, HG_DIM), lambda h, c: (h, c, 0, 0))],
        out_shape=[jax.ShapeDtypeStruct((t, nh * HG_DIM), F32),
                   jax.ShapeDtypeStruct((t, (nh + SG_GROUPS) * HG_DIM), BF16),
                   jax.ShapeDtypeStruct((nh, t // HG_BLOCK, HG_DIM, HG_DIM), BF16)],
        scratch_shapes=[pltpu.VMEM((HG_DIM, HG_DIM), F32), pltpu.VMEM((ct, HG_DIM), BF16),
                        pltpu.VMEM((ct, HG_DIM), BF16), pltpu.VMEM((ct, HG_DIM), F32),
                        pltpu.VMEM((ct, HG_DIM), F32), pltpu.VMEM((ct, HG_DIM), F32)],
        compiler_params=_params(2),
        name="hgrn_fwd",
    )(proj, proj, proj, proj, logits, gn, _block_masks(mrows))


def _sg_tile(t):
    return min(t, 512)


def _sgu_chunk_fwd(u, v, ln_g, ln_b, wm, bs):
    ua, dua = _gelu_and_grad(u)
    va, dva = _gelu_and_grad(v)
    vn, xhat, rstd = _ln_fwd(va, ln_g, ln_b)
    s = _dot(wm, vn.astype(BF16)) + bs
    return ua, dua, dva, vn, xhat, rstd, s


def _tril_weight(w_ref):
    n = SG_CHUNK
    r = lax.broadcasted_iota(jnp.int32, (n, n), 0)
    c = lax.broadcasted_iota(jnp.int32, (n, n), 1)
    return jnp.where(c <= r, w_ref[...], 0.0)


def _sgu_fwd(proj, mix, ln_g, ln_b, w_s, b_col):
    t = proj.shape[0]
    ct = _sg_tile(t)
    ng = SG_GROUPS
    off_u = 4 * HG_HEADS
    off_v = off_u + ng

    def body(u_ref, v_ref, g_ref, b_ref, w_ref, bs_ref, mix_ref, o_ref):
        del mix_ref
        wm = _tril_weight(w_ref).astype(BF16)
        for n in range(ct // SG_CHUNK):
            rows = slice(n * SG_CHUNK, (n + 1) * SG_CHUNK)
            ua, _, _, _, _, _, s = _sgu_chunk_fwd(u_ref[rows, :], v_ref[rows, :], g_ref[...], b_ref[...], wm, bs_ref[...])
            o_ref[rows, :] = (ua * s).astype(BF16)

    vec = pl.BlockSpec((None, 1, SG_DIM), lambda g, c: (g, 0, 0))
    return pl.pallas_call(
        body,
        grid=(ng, t // ct),
        in_specs=[pl.BlockSpec((ct, SG_DIM), lambda g, c: (c, off_u + g)),
                  pl.BlockSpec((ct, SG_DIM), lambda g, c: (c, off_v + g)), vec, vec,
                  pl.BlockSpec((None, SG_CHUNK, SG_CHUNK), lambda g, c: (g, 0, 0)),
                  pl.BlockSpec((None, SG_CHUNK, 1), lambda g, c: (g, 0, 0)), ANY],
        out_specs=pl.BlockSpec((ct, SG_DIM), lambda g, c: (c, HG_HEADS + g)),
        out_shape=jax.ShapeDtypeStruct(mix.shape, mix.dtype),
        input_output_aliases={6: 0},
        compiler_params=_params(2),
        name="sgu_fwd",
    )(proj, proj, ln_g, ln_b, w_s, b_col, mix)


def _mem_kv(mem, g, b, wk, wv):
    m_len, d = mem.shape

    def body(m_ref, g_ref, b_ref, wk_ref, wv_ref, mb_ref, xh_ref, rs_ref, k_ref, v_ref):
        m, xhat, rstd = _ln_fwd(m_ref[...], g_ref[...], b_ref[...])
        mb = m.astype(BF16)
        mb_ref[...] = mb
        xh_ref[...] = xhat
        rs_ref[...] = rstd
        k_ref[...] = _dot(mb, wk_ref[...]).astype(BF16)
        v_ref[...] = _dot(mb, wv_ref[...]).astype(BF16)

    return pl.pallas_call(
        body,
        out_shape=[jax.ShapeDtypeStruct((m_len, d), BF16), jax.ShapeDtypeStruct((m_len, d), F32),
                   jax.ShapeDtypeStruct((m_len, 1), F32), jax.ShapeDtypeStruct((m_len, d), BF16),
                   jax.ShapeDtypeStruct((m_len, d), BF16)],
        compiler_params=pltpu.CompilerParams(vmem_limit_bytes=VMEM_LIMIT_V7X),
        name="mem_kv",
    )(mem, g, b, wk, wv)


def _softmax_rows(s):
    m = jnp.max(s, axis=-1, keepdims=True)
    p = jnp.exp(s - m)
    return p / jnp.sum(p, axis=-1, keepdims=True)


def _attn_fwd(hb, wq, kb, vb):
    t, d = hb.shape
    tm = _row_tile(t)
    dh = d // X_HEADS
    scale = dh ** -0.5

    def body(h_ref, wq_ref, k_ref, v_ref, q_ref, o_ref):
        q = _dot(h_ref[...], wq_ref[...]).astype(BF16)
        q_ref[...] = q
        for hd in range(X_HEADS):
            sl = slice(hd * dh, (hd + 1) * dh)
            p = _softmax_rows(_dot_nt(q[:, sl], k_ref[:, sl]) * scale)
            o_ref[:, sl] = _dot(p.astype(BF16), v_ref[:, sl]).astype(BF16)

    row = pl.BlockSpec((tm, d), lambda i: (i, 0))
    full = lambda a: pl.BlockSpec(a.shape, lambda i: (0, 0))
    return pl.pallas_call(
        body,
        grid=(t // tm,),
        in_specs=[row, full(wq), full(kb), full(vb)],
        out_specs=[row, row],
        out_shape=[jax.ShapeDtypeStruct((t, d), BF16), jax.ShapeDtypeStruct((t, d), BF16)],
        compiler_params=_params(1),
        name="attn_fwd",
    )(hb, wq, kb, vb)


def _ffn_bwd_act(dyb, wd, a, b, coef, name, deps=()):
    t, d = dyb.shape
    f = wd.shape[0]
    tm = _row_tile(t)
    tn = _col_tile(f)

    def body(dy_ref, wd_ref, a_ref, b_ref, da_ref, db_ref):
        dy = dy_ref[...]
        for c in range(f // tn):
            cols = slice(c * tn, (c + 1) * tn)
            ds = _dot_nt(dy, wd_ref[cols, :]) * coef
            silu, dsilu = _silu_and_grad(a_ref[:, cols].astype(F32))
            da_ref[:, cols] = (ds * b_ref[:, cols].astype(F32) * dsilu).astype(BF16)
            db_ref[:, cols] = (ds * silu).astype(BF16)

    act = pl.BlockSpec((tm, f), lambda i: (i, 0))
    return pl.pallas_call(
        _drop_deps(body, 4, len(deps)),
        grid=(t // tm,),
        in_specs=[pl.BlockSpec((tm, d), lambda i: (i, 0)), _resident(wd), act, act] + [ANY] * len(deps),
        out_specs=[act, act],
        out_shape=[jax.ShapeDtypeStruct((t, f), BF16), jax.ShapeDtypeStruct((t, f), BF16)],
        compiler_params=_params(1),
        name=name,
    )(dyb, wd, a, b, *deps)


def _ffn_bwd_fused(dy, dyb, wd, wg, wu, a, b, coef, ln, name):
    t, d = dy.shape
    f = wd.shape[0]
    tm = min(t, 256)
    tn = _col_tile(f)

    def body(dy_ref, dyb_ref, wd_ref, wg_ref, wu_ref, a_ref, b_ref, xh_ref, rs_ref, g_ref,
             da_ref, db_ref, dyo_ref, dyob_ref, dg_ref, dbl_ref):
        dyb_v = dyb_ref[...]
        dh = ALPHA * dy_ref[...]
        for c in range(f // tn):
            cols = slice(c * tn, (c + 1) * tn)
            ds = _dot_nt(dyb_v, wd_ref[cols, :]) * coef
            silu, dsilu = _silu_and_grad(a_ref[:, cols].astype(F32))
            da = (ds * b_ref[:, cols].astype(F32) * dsilu).astype(BF16)
            db = (ds * silu).astype(BF16)
            da_ref[:, cols] = da
            db_ref[:, cols] = db
            dh = dh + _dot_nt(da, wg_ref[:, cols]) + _dot_nt(db, wu_ref[:, cols])

        @pl.when(pl.program_id(0) == 0)
        def _():
            dg_ref[...] = jnp.zeros_like(dg_ref)
            dbl_ref[...] = jnp.zeros_like(dbl_ref)

        dyp, dg, dbl = _ln_bwd(dh, xh_ref[...], rs_ref[...], g_ref[...])
        dyo_ref[...] = dyp
        dyob_ref[...] = dyp.astype(BF16)
        dg_ref[...] += dg
        dbl_ref[...] += dbl

    row = pl.BlockSpec((tm, d), lambda i: (i, 0))
    act = pl.BlockSpec((tm, f), lambda i: (i, 0))
    vec = pl.BlockSpec((1, d), lambda i: (0, 0))
    return pl.pallas_call(
        body,
        grid=(t // tm,),
        in_specs=[row, row, _resident(wd), _resident(wg), _resident(wu), act, act, row,
                  pl.BlockSpec((tm, 1), lambda i: (i, 0)), vec],
        out_specs=[act, act, row, row, vec, vec],
        out_shape=[jax.ShapeDtypeStruct((t, f), BF16), jax.ShapeDtypeStruct((t, f), BF16),
                   jax.ShapeDtypeStruct((t, d), F32), jax.ShapeDtypeStruct((t, d), BF16),
                   jax.ShapeDtypeStruct((1, d), F32), jax.ShapeDtypeStruct((1, d), F32)],
        compiler_params=_params(1),
        name=name,
    )(dy, dyb, wd, wg, wu, a, b, *ln)


def _mm_tn(a, b, name, scale=1.0, deps=(), gate=None):
    t, m = a.shape
    n = b.shape[1]
    tt = _row_tile(t)
    nt = t // tt
    n_in = 2 if gate is None else 3

    def body(*refs):
        a_ref, b_ref = refs[0], refs[n_in - 1]
        o_ref, acc = refs[n_in:]
        k = pl.program_id(0)

        @pl.when(k == 0)
        def _():
            acc[...] = jnp.zeros_like(acc)

        lhs = a_ref[...]
        if gate is not None:
            af = lhs.astype(F32)
            lhs = (af * _sigmoid(af) * refs[1][...].astype(F32)).astype(BF16)
        acc[...] += _dot_tn(lhs, b_ref[...])

        @pl.when(k == nt - 1)
        def _():
            o_ref[...] = (acc[...] * scale).astype(BF16)

    lhs_spec = pl.BlockSpec((tt, m), lambda k: (k, 0))
    return pl.pallas_call(
        _drop_deps(body, n_in, len(deps)),
        grid=(nt,),
        in_specs=[lhs_spec] * (n_in - 1) + [pl.BlockSpec((tt, n), lambda k: (k, 0))] + [ANY] * len(deps),
        out_specs=pl.BlockSpec((m, n), lambda k: (0, 0)),
        out_shape=jax.ShapeDtypeStruct((m, n), BF16),
        scratch_shapes=[pltpu.VMEM((m, n), F32)],
        compiler_params=_params(1),
        name=name,
    )(*([a] if gate is None else [a, gate]), b, *deps)


def _mm_nt(lhs, w, name):
    t, d = lhs.shape
    kd = w.shape[0]
    tm = _row_tile(t)

    def body(l_ref, w_ref, o_ref):
        o_ref[...] = _dot_nt(l_ref[...], w_ref[...])

    return pl.pallas_call(
        body,
        grid=(t // tm,),
        in_specs=[pl.BlockSpec((tm, d), lambda i: (i, 0)), _resident(w)],
        out_specs=pl.BlockSpec((tm, kd), lambda i: (i, 0)),
        out_shape=jax.ShapeDtypeStruct((t, kd), F32),
        compiler_params=_params(1),
        name=name,
    )(lhs, w)


def _dx_ln(dy, pairs, ln, name, deps=()):
    t, d = dy.shape
    npair = len(pairs)
    tm = min(t, 512 // npair)
    nt = t // tm
    n_in = 1 + 2 * npair + (3 if ln is not None else 0)

    def body(*refs):
        dy_ref = refs[0]
        pr = refs[1:1 + 2 * npair]
        pos = 1 + 2 * npair
        dh = ALPHA * dy_ref[...]
        for p in range(npair):
            dh = dh + _dot_nt(pr[2 * p][...], pr[2 * p + 1][...])
        if ln is not None:
            xh_ref, rs_ref, g_ref = refs[pos:pos + 3]
            dyo_ref, dyb_ref, dg_ref, db_ref = refs[pos + 3:pos + 7]

            @pl.when(pl.program_id(0) == 0)
            def _():
                dg_ref[...] = jnp.zeros_like(dg_ref)
                db_ref[...] = jnp.zeros_like(db_ref)

            dyp, dg, db = _ln_bwd(dh, xh_ref[...], rs_ref[...], g_ref[...])
            dyo_ref[...] = dyp
            dyb_ref[...] = dyp.astype(BF16)
            dg_ref[...] += dg
            db_ref[...] += db
        else:
            refs[pos][...] = dh

    row = pl.BlockSpec((tm, d), lambda i: (i, 0))
    vec = pl.BlockSpec((1, d), lambda i: (0, 0))
    in_specs = [row]
    args = [dy]
    for lhs, w in pairs:
        in_specs += [pl.BlockSpec((tm, lhs.shape[1]), lambda i: (i, 0)), _resident(w)]
        args += [lhs, w]
    if ln is not None:
        in_specs += [row, pl.BlockSpec((tm, 1), lambda i: (i, 0)), vec]
        args += list(ln)
        out_specs = [row, row, vec, vec]
        out_shape = [jax.ShapeDtypeStruct((t, d), F32), jax.ShapeDtypeStruct((t, d), BF16),
                     jax.ShapeDtypeStruct((1, d), F32), jax.ShapeDtypeStruct((1, d), F32)]
    else:
        out_specs = row
        out_shape = jax.ShapeDtypeStruct((t, d), F32)
    return pl.pallas_call(
        _drop_deps(body, n_in, len(deps)),
        grid=(nt,),
        in_specs=in_specs + [ANY] * len(deps),
        out_specs=out_specs,
        out_shape=out_shape,
        compiler_params=_params(1),
        name=name,
    )(*args, *deps)


def _hgrn_bwd(proj, oraw, dmix, states, logits, gn):
    t = proj.shape[0]
    ct = _hg_tile(t)
    nct = t // ct
    nblk = ct // HG_BLOCK
    nh = HG_HEADS
    mrows = min(ct, 256)

    def body(q_ref, fz_ref, iv_ref, gg_ref, or_ref, do_ref, st_ref, lg_ref, gn_ref, mask_ref,
             dq_ref, dfz_ref, div_ref, dgg_ref, dlg_ref, dgn_ref,
             dstate, qt_s, kt_s, k_s, b_s, eb_s, ekb_s, dec_s, dor_s, dbl_s, gr_s, dk_s, dlb_acc):
        c = pl.program_id(1)

        @pl.when(c == 0)
        def _():
            dstate[...] = jnp.zeros_like(dstate)
            dlb_acc[...] = jnp.zeros_like(dlb_acc)
            dgn_ref[...] = jnp.zeros_like(dgn_ref)

        lb = _lower_bound(lg_ref[...])
        q = q_ref[...]
        sig, nsig, f, k = _forget_terms(fz_ref[...], lb)
        logf = jnp.log(f)
        b = _mask_dot(mask_ref[0], logf)
        bend = _mask_dot(mask_ref[2], logf)
        eb = jnp.exp(b)
        ekb = jnp.exp(bend - b)
        qt_s[...] = (q * eb).astype(BF16)
        kt_s[...] = (k * ekb).astype(BF16)
        k_s[...] = k
        b_s[...] = b
        eb_s[...] = eb
        ekb_s[...] = ekb
        dec_s[...] = jnp.exp(bend)
        oraw = or_ref[...]
        r = lax.rsqrt(jnp.mean(oraw * oraw, axis=-1, keepdims=True) + LN_EPS)
        on = oraw * r
        gg = gg_ref[...]
        silu, dsilu = _silu_and_grad(gg)
        doa = do_ref[...]
        gnv = gn_ref[...]
        dgg_ref[...] = (doa * on * gnv * dsilu).astype(BF16)
        dyn = doa * silu
        dgn_ref[...] += jnp.sum(dyn * on, axis=0, keepdims=True)
        don = dyn * gnv
        dor_s[...] = r * (don - on * jnp.mean(don * on, axis=-1, keepdims=True))
        tidx = lax.broadcasted_iota(jnp.int32, (HG_HALF, HG_DIM), 0)

        def blk(ii, carry):
            i = nblk - 1 - ii
            r0 = pl.multiple_of(i * HG_BLOCK, HG_BLOCK)
            rows = pl.ds(r0, HG_BLOCK)
            st = st_ref[i]
            dst = dstate[...]
            dstb = dst.astype(BF16)
            do = dor_s[rows, :]
            dob = do.astype(BF16)
            v = iv_ref[rows, :]
            vb = v.astype(BF16)
            qq = q_ref[rows, :]
            kk = k_s[rows, :]
            bb = b_s[rows, :]
            qt = qt_s[rows, :]
            kt = kt_s[rows, :]
            dec = dec_s[pl.ds(r0, 1), :]
            dkt = _dot(vb, dstb)
            dq = _dot(dob, st) * eb_s[rows, :]
            dk = dkt * ekb_s[rows, :]
            dv = _dot_nt(kt, dstb)
            gend = (jnp.sum(kk * dk, axis=0, keepdims=True)
                    + dec * jnp.sum(dst * st.astype(F32), axis=0, keepdims=True))
            qh, bh, doh = _halves(qq), _halves(bb), _halves(do)
            dqh, dkh, dvh = list(_halves(dq)), list(_halves(dk)), list(_halves(dv))
            for s in range(HG_BLOCK):
                ks, vs = kk[s:s + 1, :], v[s:s + 1, :]
                dk_part = dv_part = None
                for h in _causal_halves(s):
                    e = _decay_from(bh[h], bb[s:s + 1, :], s, h, tidx)
                    ke = ks * e
                    acol = jnp.sum(qh[h] * ke, axis=1, keepdims=True)
                    dacol = jnp.sum(doh[h] * vs, axis=1, keepdims=True)
                    dqh[h] = dqh[h] + dacol * ke
                    pk = dacol * (qh[h] * e)
                    pv = acol * doh[h]
                    dk_part = pk if dk_part is None else dk_part + pk
                    dv_part = pv if dv_part is None else dv_part + pv
                hs, row = divmod(s, HG_HALF)
                dkh[hs] = dkh[hs] + jnp.where(tidx == row, jnp.sum(dk_part, axis=0, keepdims=True), 0.0)
                dvh[hs] = dvh[hs] + jnp.where(tidx == row, jnp.sum(dv_part, axis=0, keepdims=True), 0.0)
            dq = jnp.concatenate(dqh, axis=0)
            dk = jnp.concatenate(dkh, axis=0)
            dv = jnp.concatenate(dvh, axis=0)
            dq_ref[rows, :] = dq.astype(BF16)
            div_ref[rows, :] = dv.astype(BF16)
            dk_s[rows, :] = dk
            dbl_s[rows, :] = qq * dq - kk * dk
            gr_s[rows, :] = jnp.zeros((HG_BLOCK, HG_DIM), F32) + gend
            dstate[...] = dst * dec + _dot_tn(dob, qt)
            return carry

        lax.fori_loop(0, nblk, blk, 0, unroll=2 * HG_UNROLL)
        dlogf = _mask_dot(mask_ref[1], dbl_s[...]) + gr_s[...]
        dk = dk_s[...]
        dfz_ref[...] = ((dlogf / f - dk) * ((1.0 - lb) * sig * nsig)).astype(BF16)
        dlb_acc[...] += jnp.sum((dlogf / f - dk) * nsig, axis=0, keepdims=True)

        @pl.when(c == nct - 1)
        def _():
            dl0 = dlb_acc[...] * lb * (1.0 - lb)
            layer = lax.broadcasted_iota(jnp.int32, (2, HG_DIM), 0)
            dlg_ref[...] = jnp.where(layer == 0, dl0, -dl0)

    def slab(off):
        return pl.BlockSpec((ct, HG_DIM), lambda h, c: (nct - 1 - c, off + h))

    out_slab = pl.BlockSpec((ct, HG_DIM), lambda h, c: (nct - 1 - c, h))
    tile_f32 = pltpu.VMEM((ct, HG_DIM), F32)
    tile_b16 = pltpu.VMEM((ct, HG_DIM), BF16)
    slab_shape = jax.ShapeDtypeStruct((t, nh * HG_DIM), BF16)
    return pl.pallas_call(
        body,
        grid=(nh, nct),
        in_specs=[slab(0), slab(nh), slab(2 * nh), slab(3 * nh), slab(0), slab(0),
                  pl.BlockSpec((None, nblk, HG_DIM, HG_DIM), lambda h, c: (h, nct - 1 - c, 0, 0)),
                  pl.BlockSpec((None, 2, HG_DIM), lambda h, c: (h, 0, 0)),
                  pl.BlockSpec((1, HG_DIM), lambda h, c: (0, 0)),
                  pl.BlockSpec((3, mrows, mrows), lambda h, c: (0, 0, 0))],
        out_specs=[out_slab, out_slab, out_slab, out_slab,
                   pl.BlockSpec((None, 2, HG_DIM), lambda h, c: (h, 0, 0)),
                   pl.BlockSpec((None, 1, HG_DIM), lambda h, c: (h, 0, 0))],
        out_shape=[slab_shape, slab_shape, slab_shape, slab_shape,
                   jax.ShapeDtypeStruct((nh, 2, HG_DIM), F32), jax.ShapeDtypeStruct((nh, 1, HG_DIM), F32)],
        scratch_shapes=[pltpu.VMEM((HG_DIM, HG_DIM), F32), tile_b16, tile_b16, tile_f32, tile_f32, tile_f32, tile_f32,
                        tile_f32, tile_f32, tile_f32, tile_f32, tile_f32, pltpu.VMEM((1, HG_DIM), F32)],
        compiler_params=_params(2),
        name="hgrn_bwd",
    )(proj, proj, proj, proj, oraw, dmix, states, logits, gn, _block_masks(mrows))


def _sgu_bwd(proj, dmix, ln_g, ln_b, w_s, w_t, b_col):
    t = proj.shape[0]
    ct = _sg_tile(t)
    nct = t // ct
    ng = SG_GROUPS
    off_u = 4 * HG_HEADS
    off_v = off_u + ng
    n = SG_CHUNK

    def body(u_ref, v_ref, do_ref, g_ref, b_ref, w_ref, wt_ref, bs_ref, du_ref, dv_ref, dg_ref, db_ref, dw_ref, dbs_ref):
        c = pl.program_id(1)

        @pl.when(c == 0)
        def _():
            dg_ref[...] = jnp.zeros_like(dg_ref)
            db_ref[...] = jnp.zeros_like(db_ref)
            dw_ref[...] = jnp.zeros_like(dw_ref)
            dbs_ref[...] = jnp.zeros_like(dbs_ref)

        r = lax.broadcasted_iota(jnp.int32, (n, n), 0)
        cc = lax.broadcasted_iota(jnp.int32, (n, n), 1)
        wm = jnp.where(cc <= r, w_ref[...], 0.0).astype(BF16)
        wmt = jnp.where(r <= cc, wt_ref[...], 0.0).astype(BF16)
        for ci in range(ct // n):
            rows = slice(ci * n, (ci + 1) * n)
            ua, dua, dva, vn, xhat, rstd, s = _sgu_chunk_fwd(u_ref[rows, :], v_ref[rows, :], g_ref[...], b_ref[...],
                                                             wm, bs_ref[...])
            do = do_ref[rows, :]
            du_ref[rows, :] = (do * s * dua).astype(BF16)
            ds = do * ua
            dsb = ds.astype(BF16)
            dbs_ref[...] += jnp.sum(ds, axis=1, keepdims=True)
            dw_ref[...] += _dot_nt(dsb, vn.astype(BF16))
            dvn = _dot(wmt, dsb)
            dva_in, dg, db = _ln_bwd(dvn, xhat, rstd, g_ref[...])
            dg_ref[...] += dg
            db_ref[...] += db
            dv_ref[rows, :] = (dva_in * dva).astype(BF16)

        @pl.when(c == nct - 1)
        def _():
            dw_ref[...] = jnp.where(cc <= r, dw_ref[...], 0.0)

    vec = pl.BlockSpec((None, 1, SG_DIM), lambda g, c: (g, 0, 0))
    mat = pl.BlockSpec((None, n, n), lambda g, c: (g, 0, 0))
    col = pl.BlockSpec((None, n, 1), lambda g, c: (g, 0, 0))
    out_slab = pl.BlockSpec((ct, SG_DIM), lambda g, c: (c, g))
    return pl.pallas_call(
        body,
        grid=(ng, nct),
        in_specs=[pl.BlockSpec((ct, SG_DIM), lambda g, c: (c, off_u + g)),
                  pl.BlockSpec((ct, SG_DIM), lambda g, c: (c, off_v + g)),
                  pl.BlockSpec((ct, SG_DIM), lambda g, c: (c, ng + g)), vec, vec, mat, mat, col],
        out_specs=[out_slab, out_slab, vec, vec, mat, col],
        out_shape=[jax.ShapeDtypeStruct((t, ng * SG_DIM), BF16), jax.ShapeDtypeStruct((t, ng * SG_DIM), BF16),
                   jax.ShapeDtypeStruct((ng, 1, SG_DIM), F32), jax.ShapeDtypeStruct((ng, 1, SG_DIM), F32),
                   jax.ShapeDtypeStruct((ng, n, n), F32), jax.ShapeDtypeStruct((ng, n, 1), F32)],
        compiler_params=_params(2),
        name="sgu_bwd",
    )(proj, proj, dmix, ln_g, ln_b, w_s, w_t, b_col)


def _attn_bwd(dyb, wo, qb, kb, vb):
    t, d = dyb.shape
    m_len = kb.shape[0]
    tm = _row_tile(t)
    dh = d // X_HEADS
    scale = dh ** -0.5

    def body(dy_ref, wo_ref, q_ref, k_ref, v_ref, dq_ref, dk_ref, dv_ref):
        i = pl.program_id(0)

        @pl.when(i == 0)
        def _():
            dk_ref[...] = jnp.zeros_like(dk_ref)
            dv_ref[...] = jnp.zeros_like(dv_ref)

        do = _dot_nt(dy_ref[...], wo_ref[...]).astype(BF16)
        for hd in range(X_HEADS):
            sl = slice(hd * dh, (hd + 1) * dh)
            qh = q_ref[:, sl]
            p = _softmax_rows(_dot_nt(qh, k_ref[:, sl]) * scale)
            doh = do[:, sl]
            dp = _dot_nt(doh, v_ref[:, sl])
            ds = (p * (dp - jnp.sum(dp * p, axis=-1, keepdims=True)) * scale).astype(BF16)
            dq_ref[:, sl] = _dot(ds, k_ref[:, sl]).astype(BF16)
            dk_ref[:, sl] += _dot_tn(ds, qh)
            dv_ref[:, sl] += _dot_tn(p.astype(BF16), doh)

    row = pl.BlockSpec((tm, d), lambda i: (i, 0))
    full = lambda a: pl.BlockSpec(a.shape, lambda i: (0, 0))
    kv = pl.BlockSpec((m_len, d), lambda i: (0, 0))
    return pl.pallas_call(
        body,
        grid=(t // tm,),
        in_specs=[row, full(wo), row, full(kb), full(vb)],
        out_specs=[row, kv, kv],
        out_shape=[jax.ShapeDtypeStruct((t, d), BF16), jax.ShapeDtypeStruct((m_len, d), F32),
                   jax.ShapeDtypeStruct((m_len, d), F32)],
        compiler_params=_params(1),
        name="attn_bwd",
    )(dyb, wo, qb, kb, vb)


def _mem_bwd(dk, dv, mb, xhat, rstd, g, wk, wv):
    m_len, d = dk.shape

    def body(dk_ref, dv_ref, mb_ref, xh_ref, rs_ref, g_ref, wk_ref, wv_ref, gwk_ref, gwv_ref, dg_ref, db_ref):
        dkb = dk_ref[...].astype(BF16)
        dvb = dv_ref[...].astype(BF16)
        mb_v = mb_ref[...]
        gwk_ref[...] = _dot_tn(mb_v, dkb).astype(BF16)
        gwv_ref[...] = _dot_tn(mb_v, dvb).astype(BF16)
        dm = _dot_nt(dkb, wk_ref[...]) + _dot_nt(dvb, wv_ref[...])
        _, dg, db = _ln_bwd(dm, xh_ref[...], rs_ref[...], g_ref[...])
        dg_ref[...] = dg
        db_ref[...] = db

    return pl.pallas_call(
        body,
        out_shape=[jax.ShapeDtypeStruct((d, d), BF16), jax.ShapeDtypeStruct((d, d), BF16),
                   jax.ShapeDtypeStruct((1, d), F32), jax.ShapeDtypeStruct((1, d), F32)],
        compiler_params=pltpu.CompilerParams(vmem_limit_bytes=VMEM_LIMIT_V7X),
        name="mem_bwd",
    )(dk, dv, mb, xhat, rstd, g, wk, wv)


def _adamw(w, g, m, v):
    m = ADAM_B1 * m + (1.0 - ADAM_B1) * g
    v = ADAM_B2 * v + (1.0 - ADAM_B2) * (g * g)
    m_hat = m / (1.0 - ADAM_B1 ** ADAM_STEP)
    v_hat = v / (1.0 - ADAM_B2 ** ADAM_STEP)
    delta = -ADAM_LR * (m_hat / (jnp.sqrt(v_hat) + ADAM_EPS) + ADAM_WD * w)
    return delta, m, v


def _slot_sum(ref):
    g = ref[0].astype(F32)
    for s in range(1, N_DEV):
        g = g + ref[s].astype(F32)
    return g


def _adam_sharded(lands, w, m, v, axis, name):
    rows, cols = w.shape
    nl = len(lands)
    transposed = axis == 1 and nl == 2
    if transposed:
        rows, cols = cols, rows
        tr = 256
        grid = (rows // tr,)
        wblk = pl.BlockSpec((cols, tr), lambda i: (0, i))
        lblk = [pl.BlockSpec((N_DEV, tr, a.shape[2]), lambda i: (0, i, 0)) for a in lands]
    elif axis == 1:
        tr = 256 if rows % 256 == 0 else rows
        grid = (rows // tr,)
        wblk = pl.BlockSpec((tr, cols), lambda i: (i, 0))
        lblk = [pl.BlockSpec((N_DEV, tr, a.shape[2]), lambda i: (0, i, 0)) for a in lands]
    else:
        tc = _col_tile(cols)
        grid = (cols // tc,)
        wblk = pl.BlockSpec((rows, tc), lambda i: (0, i))
        lblk = [pl.BlockSpec((N_DEV, a.shape[1], tc), lambda i: (0, 0, i)) for a in lands]

    def body(*refs):
        w_ref, m_ref, v_ref = refs[nl:nl + 3]
        g_ref, d_ref, nm_ref, nv_ref = refs[nl + 3:]
        g = _slot_sum(refs[0])
        if nl == 2:
            tail = _slot_sum(refs[1])
            if transposed:
                g = jnp.concatenate([g.T, tail.T[:cols - g.shape[1], :]], axis=0)
            elif axis == 1:
                g = jnp.concatenate([g, tail[:, :cols - g.shape[1]]], axis=1)
            else:
                g = jnp.concatenate([g, tail[:rows - g.shape[0], :]], axis=0)
        delta, nm, nv = _adamw(w_ref[...], g, m_ref[...], v_ref[...])
        g_ref[...] = g
        d_ref[...] = delta
        nm_ref[...] = nm
        nv_ref[...] = nv

    shp = jax.ShapeDtypeStruct(w.shape, F32)
    return pl.pallas_call(
        body,
        grid=grid,
        in_specs=lblk + [wblk, wblk, wblk],
        out_specs=[wblk, wblk, wblk, wblk],
        out_shape=[shp, shp, shp, shp],
        compiler_params=_params(1),
        name=name,
    )(*[pltpu.with_memory_space_constraint(a, pltpu.HBM) for a in (*lands, w, m, v)])


def _mesh_pos():
    return lax.axis_index("x"), lax.axis_index("y"), lax.axis_index("c")


def _peer(k):
    x, y, c = _mesh_pos()
    pos = (x ^ (k >> 2), y ^ ((k >> 1) & 1), c ^ (k & 1))
    return pos, 4 * pos[0] + 2 * pos[1] + pos[2]


def _sem_index(row, k):
    return row * (N_DEV - 1) + k - 1


def _window(ref, axis, start, size):
    align = 16 if axis == 0 else LANES
    start = pl.multiple_of(start, align)
    return ref.at[pl.ds(start, size), :] if axis == 0 else ref.at[:, pl.ds(start, size)]


def _piece_refs(piece, srcs, lands, me, peer):
    kind, si, li, axis, base, stride, shape = piece
    if kind == "gather":
        return srcs[si], _window(lands[li], axis, base + stride * me, shape[axis])
    return _window(srcs[si], axis, base + stride * peer, shape[axis]), lands[li].at[me]


def _place_own(srcs, land_shapes, pieces, name):
    ns, nl, npc = len(srcs), len(land_shapes), len(pieces)

    def body(*refs):
        s_refs = refs[:ns]
        l_refs = refs[ns:ns + nl]
        bufs = refs[ns + nl:ns + nl + npc]
        sems = refs[ns + nl + npc]
        x, y, c = _mesh_pos()
        me = 4 * x + 2 * y + c
        loads = []
        for p, piece in enumerate(pieces):
            src, dst = _piece_refs(piece, s_refs, l_refs, me, me)
            cp = pltpu.make_async_copy(src, bufs[p], sems.at[0, p])
            cp.start()
            loads.append((cp, dst))
        stores = []
        for p, (cp, dst) in enumerate(loads):
            cp.wait()
            out = pltpu.make_async_copy(bufs[p], dst, sems.at[1, p])
            out.start()
            stores.append(out)
        for out in stores:
            out.wait()

    out = pl.pallas_call(
        body,
        in_specs=[ANY] * ns,
        out_specs=[ANY] * nl,
        out_shape=list(land_shapes),
        scratch_shapes=[pltpu.VMEM(pc[6], srcs[pc[1]].dtype) for pc in pieces] + [pltpu.SemaphoreType.DMA((2, npc))],
        compiler_params=pltpu.CompilerParams(vmem_limit_bytes=VMEM_LIMIT_V7X),
        name=name,
    )(*srcs)
    return list(out)


def _comm_start(srcs, lands, pieces, groups, name, after=()):
    ns, nl, na, ng = len(srcs), len(lands), len(after), len(groups)

    def body(*refs):
        s_refs = refs[:ns]
        l_refs = refs[ns:ns + nl]
        outs = refs[ns + nl + na:]
        sems = outs[:2 * ng]
        token = outs[-1]
        x, y, c = _mesh_pos()
        me = 4 * x + 2 * y + c
        for g, members in enumerate(groups):
            for row, p in enumerate(members):
                for k in range(1, N_DEV):
                    pos, peer = _peer(k)
                    src, dst = _piece_refs(pieces[p], s_refs, l_refs, me, peer)
                    pltpu.make_async_remote_copy(src_ref=src, dst_ref=dst, send_sem=sems[2 * g].at[_sem_index(row, k)],
                                                 recv_sem=sems[2 * g + 1].at[_sem_index(row, k)], device_id=pos,
                                                 device_id_type=MESH_ID).start()
        token[...] = jnp.zeros_like(token)

    sem_shapes = []
    for members in groups:
        sem_shapes += [pltpu.SemaphoreType.DMA((len(members) * (N_DEV - 1),))] * 2
    hbm_of = lambda a: pltpu.HBM(a.shape, a.dtype)
    out = pl.pallas_call(
        body,
        in_specs=[HBM] * (ns + nl) + [ANY] * na,
        out_specs=[SEM] * (2 * ng) + [HBM] * (ns + nl) + [pl.BlockSpec(memory_space=pltpu.VMEM)],
        out_shape=sem_shapes + [hbm_of(a) for a in srcs] + [hbm_of(a) for a in lands]
        + [jax.ShapeDtypeStruct((8, LANES), F32)],
        input_output_aliases={i: 2 * ng + i for i in range(ns + nl)},
        compiler_params=pltpu.CompilerParams(has_side_effects=DATAFLOW),
        name=name,
    )(*[pltpu.with_memory_space_constraint(a, pltpu.HBM) for a in list(srcs) + list(lands)], *after)
    sems = [(out[2 * g], out[2 * g + 1]) for g in range(ng)]
    return sems, list(out[2 * ng:2 * ng + ns]), list(out[2 * ng + ns:2 * ng + ns + nl]), out[-1]


def _comm_wait(srcs, lands, pieces, members, sems, after, name):
    ns, nl, na = len(srcs), len(lands), len(after)

    def body(*refs):
        s_refs = refs[:ns]
        l_refs = refs[ns:ns + nl]
        send_sems, recv_sems = refs[ns + nl:ns + nl + 2]
        x, y, c = _mesh_pos()
        me = 4 * x + 2 * y + c
        for row, p in enumerate(members):
            for k in range(1, N_DEV):
                pos, peer = _peer(k)
                src, dst = _piece_refs(pieces[p], s_refs, l_refs, me, peer)
                cp = pltpu.make_async_remote_copy(src_ref=src, dst_ref=dst, send_sem=send_sems.at[_sem_index(row, k)],
                                                  recv_sem=recv_sems.at[_sem_index(row, k)], device_id=pos,
                                                  device_id_type=MESH_ID)
                cp.wait_send()
                cp.wait_recv()

    hbm_of = lambda a: pltpu.HBM(a.shape, a.dtype)
    out = pl.pallas_call(
        body,
        in_specs=[HBM] * (ns + nl) + [SEM, SEM] + [ANY] * na,
        out_specs=[HBM] * (ns + nl),
        out_shape=[hbm_of(a) for a in srcs] + [hbm_of(a) for a in lands],
        input_output_aliases={i: i for i in range(ns + nl)},
        compiler_params=pltpu.CompilerParams(has_side_effects=DATAFLOW),
        name=name,
    )(*srcs, *lands, sems[0], sems[1], *after)
    return list(out[ns:])


def _landed_block(piece, lands, owner):
    _, _, li, axis, base, stride, shape = piece
    return _window(lands[li], axis, base + stride * owner, shape[axis])


def _copy_stage(name, bufs, in_sems, out_sem_sizes, emit, after=()):
    nb, ni, no, na = len(bufs), len(in_sems), len(out_sem_sizes), len(after)

    def body(*refs):
        b_refs = refs[:nb]
        i_refs = refs[nb:nb + ni]
        o_refs = refs[nb + ni + na:nb + ni + na + no]
        emit(b_refs, i_refs, o_refs)
        refs[-1][...] = jnp.zeros_like(refs[-1])

    hbm_of = lambda a: pltpu.HBM(a.shape, a.dtype)
    out = pl.pallas_call(
        body,
        in_specs=[HBM] * nb + [SEM] * ni + [ANY] * na,
        out_specs=[SEM] * no + [HBM] * nb + [pl.BlockSpec(memory_space=pltpu.VMEM)],
        out_shape=[pltpu.SemaphoreType.DMA((n,)) for n in out_sem_sizes] + [hbm_of(a) for a in bufs]
        + [jax.ShapeDtypeStruct((8, LANES), F32)],
        input_output_aliases={i: no + i for i in range(nb)},
        compiler_params=pltpu.CompilerParams(has_side_effects=DATAFLOW),
        name=name,
    )(*[pltpu.with_memory_space_constraint(a, pltpu.HBM) for a in bufs], *in_sems, *after)
    return list(out[:no]), list(out[no:no + nb]), out[-1]


def _remote(src, dst, send, recv, to):
    return pltpu.make_async_remote_copy(src_ref=src, dst_ref=dst, send_sem=send, recv_sem=recv, device_id=to,
                                        device_id_type=MESH_ID)


def _routed_gather(srcs, lands, pieces, after, name):
    ns, npc = len(srcs), len(pieces)

    def places():
        x, y, c = _mesh_pos()
        index = lambda p: 4 * p[0] + 2 * p[1] + p[2]
        me, sib = (x, y, c), (x, y, 1 - c)
        xnb, ynb = (1 - x, y, c), (x, 1 - y, c)
        got_first = (x ^ (1 - c), y ^ c, c)
        pass_to = (x ^ c, y ^ (1 - c), c)
        diag = (1 - x, 1 - y, c)
        return index, me, sib, xnb, ynb, got_first, pass_to, diag

    def start(b, _, o):
        index, me, sib, xnb, ynb, *_rest = places()
        send_a, recv_sib, recv_nb = o
        for p, piece in enumerate(pieces):
            src, dst = _piece_refs(piece, b[:ns], b[ns:], index(me), 0)
            _remote(src, dst, send_a.at[3 * p], recv_sib.at[p], sib).start()
            _remote(src, dst, send_a.at[3 * p + 1], recv_nb.at[2 * p], xnb).start()
            _remote(src, dst, send_a.at[3 * p + 2], recv_nb.at[2 * p + 1], ynb).start()

    def pass_a(b, i, o):
        index, me, sib, xnb, ynb, got_first, pass_to, _diag = places()
        (recv_nb,) = i
        send_f, recv_f, send_d, recv_d = o
        for p, piece in enumerate(pieces):
            for j, nb in enumerate((xnb, ynb)):
                blk = _landed_block(piece, b, index(nb))
                _remote(blk, blk, send_f.at[2 * p + j], recv_nb.at[2 * p + j], sib).wait_recv()
                _remote(blk, blk, send_f.at[2 * p + j], recv_f.at[2 * p + j], sib).start()
            blk = _landed_block(piece, b, index(got_first))
            _remote(blk, blk, send_d.at[p], recv_d.at[p], pass_to).start()

    def pass_b(b, i, o):
        index, me, sib, *_mid, diag = places()
        (recv_d,) = i
        send_g, recv_g = o
        for p, piece in enumerate(pieces):
            blk = _landed_block(piece, b, index(diag))
            _remote(blk, blk, send_g.at[p], recv_d.at[p], sib).wait_recv()
            _remote(blk, blk, send_g.at[p], recv_g.at[p], sib).start()

    def last(b, i, _):
        index, me, sib, *_others = places()
        send_a, recv_sib, send_f, recv_f, send_d, send_g, recv_g = i
        for p, piece in enumerate(pieces):
            src, dst = _piece_refs(piece, b[:ns], b[ns:], index(me), 0)
            cp = lambda s_sem, r_sem: _remote(src, dst, s_sem, r_sem, sib)
            cp(send_a.at[3 * p], recv_sib.at[p]).wait_recv()
            cp(send_a.at[3 * p], recv_g.at[p]).wait_recv()
            for j in range(3):
                cp(send_a.at[3 * p + j], recv_sib.at[p]).wait_send()
            for j in range(2):
                cp(send_f.at[2 * p + j], recv_f.at[2 * p + j]).wait_recv()
                cp(send_f.at[2 * p + j], recv_f.at[2 * p + j]).wait_send()
            cp(send_d.at[p], recv_sib.at[p]).wait_send()
            cp(send_g.at[p], recv_sib.at[p]).wait_send()

    (send_a, recv_sib, recv_nb), bufs, _ = _copy_stage(name + "_start", list(srcs) + list(lands), [],
                                                       [3 * npc, npc, 2 * npc], start)
    srcs, lands = bufs[:ns], bufs[ns:]
    (send_f, recv_f, send_d, recv_d), lands, _ = _copy_stage(name + "_pass_a", lands, [recv_nb],
                                                             [2 * npc, 2 * npc, npc, npc],
                                                             lambda b, i, o: pass_a(b, i, o), after=after)
    (send_g, recv_g), lands, tok = _copy_stage(name + "_pass_b", lands, [recv_d], [npc, npc], pass_b)
    _, bufs, _ = _copy_stage(name + "_last", list(srcs) + list(lands),
                             [send_a, recv_sib, send_f, recv_f, send_d, send_g, recv_g], [], last)
    return bufs[ns:], tok


_SMALL_NAMES = ("ln1_g", "ln1_b", "hg_lb_logits", "hg_norm_g", "sg_ln_g", "sg_ln_b", "sg_w_s", "sg_b_s",
                "ln2_g", "ln2_b", "mem_ln_g", "mem_ln_b", "ln3_g", "ln3_b", "ln4_g", "ln4_b")


_VEC_NAMES = ("ln1_g", "ln1_b", "ln2_g", "ln2_b", "mem_ln_g", "mem_ln_b", "ln3_g", "ln3_b", "ln4_g", "ln4_b")
_ROW_NAMES = ("hg_lb_logits", "hg_norm_g", "sg_ln_g", "sg_ln_b", "sg_b_s", "sg_w_s")
VEC_ROWS = 16


def _row_plan(shapes):
    plan, pos = {}, 0
    for k in _ROW_NAMES:
        shp = shapes[k]
        slabs, off = [], pos
        for idx in itertools.product(*[range(dim) for dim in shp[:-2]]):
            slabs.append((idx, off, shp[-2]))
            off += shp[-2]
        plan[k] = (pos, slabs)
        pos = -(-off // 8) * 8
    return plan, -(-pos // 16) * 16


def _pack_small_grads(gs, shapes, loss):
    d = gs[_VEC_NAMES[0]].size
    vec = jnp.concatenate([gs[k].reshape(1, -1) for k in _VEC_NAMES] + [jnp.tile(loss, (1, d // LANES))], axis=0)
    vec = jnp.pad(vec, ((0, VEC_ROWS - vec.shape[0]), (0, 0)))
    plan, total = _row_plan(shapes)
    parts, pos = [], 0
    for k in _ROW_NAMES:
        first, slabs = plan[k]
        rows = gs[k].reshape(-1, LANES)
        end = slabs[-1][1] + slabs[-1][2]
        nxt = -(-end // 8) * 8
        parts.append(jnp.pad(rows, ((0, nxt - first - rows.shape[0]), (0, 0))))
        pos = nxt
    parts.append(jnp.zeros((total - pos, LANES), F32))
    return vec, jnp.concatenate(parts, axis=0)


def _adam_small(land_vec, land_rows, w, m, v):
    names = _VEC_NAMES + _ROW_NAMES
    n = len(names)
    shapes = {k: w[k].shape for k in names}
    plan, _ = _row_plan(shapes)

    def body(*refs):
        lv_ref, lr_ref = refs[:2]
        w_refs, m_refs, v_refs = refs[2:2 + n], refs[2 + n:2 + 2 * n], refs[2 + 2 * n:2 + 3 * n]
        outs = refs[2 + 3 * n:2 + 7 * n]
        loss_ref = refs[2 + 7 * n]
        gv_s, gr_s = refs[3 + 7 * n:]
        gv_s[...] = _slot_sum(lv_ref)
        gr_s[...] = _slot_sum(lr_ref)
        loss_ref[...] = gv_s[len(_VEC_NAMES):len(_VEC_NAMES) + 1, :LANES]
        for p, k in enumerate(names):
            if k in _VEC_NAMES:
                row = _VEC_NAMES.index(k)
                slabs = [((), None, None)]
            else:
                slabs = plan[k][1]
            for idx, off, rows in slabs:
                g = gv_s[row:row + 1, :] if off is None else gr_s[off:off + rows, :]
                sel = idx + (slice(None), slice(None))
                delta, nm, nv = _adamw(w_refs[p][sel], g, m_refs[p][sel], v_refs[p][sel])
                for o, val in zip(range(4), (g, delta, nm, nv)):
                    outs[o * n + p][sel] = val

    flat = lambda tree: [tree[k] for k in names]
    shp = [jax.ShapeDtypeStruct(shapes[k], F32) for k in names]
    out = pl.pallas_call(
        body,
        out_shape=shp * 4 + [jax.ShapeDtypeStruct((1, LANES), F32)],
        scratch_shapes=[pltpu.VMEM(land_vec.shape[1:], F32), pltpu.VMEM(land_rows.shape[1:], F32)],
        name="adam_small",
    )(land_vec, land_rows, *flat(w), *flat(m), *flat(v))
    return [dict(zip(names, out[o * n:(o + 1) * n])) for o in range(4)], out[4 * n]


_COL_FFN = ("ffn1_w_gate", "ffn1_w_up", "ffn2_w_gate", "ffn2_w_up")
_ROW_FFN = ("ffn1_w_down", "ffn2_w_down")
_ROW_SQ = ("w_out", "xa_w_q", "xa_w_k", "xa_w_v", "xa_w_o")
_BIG_NAMES = ("ffn1_w_gate", "ffn1_w_up", "ffn1_w_down", "w_in", "w_out", "xa_w_q", "xa_w_k", "xa_w_v", "xa_w_o",
              "ffn2_w_gate", "ffn2_w_up", "ffn2_w_down")


def _ffn_split(fs):
    main = (fs // MXU_WIDTH_V7X) * MXU_WIDTH_V7X
    tail = fs - main
    tail_pad = -(-tail // LANES) * LANES
    assert main > 0 and tail > 0
    return main, tail, tail_pad


def _layout(name, shard_shape):
    r, c = shard_shape
    if name in _COL_FFN:
        main, tail, pad = _ffn_split(c)
        return (r, N_DEV * (main + pad)), [(1, 0, main, (r, main), (0, main)),
                                           (1, N_DEV * main, pad, (r, pad), (main, c))]
    if name in _ROW_FFN:
        main, tail, pad = _ffn_split(r)
        return (N_DEV * (main + pad), c), [(0, 0, main, (main, c), (0, main)),
                                           (0, N_DEV * main, pad, (pad, c), (main, r))]
    if name == "w_in":
        return (r, N_DEV * c), [(1, 0, c, (r, c), (0, c))]
    return (N_DEV * r, c), [(0, 0, r, (r, c), (0, r))]


def _shard_pieces(name, shard):
    out = []
    for axis, _, _, shape, (lo, hi) in _layout(name, shard.shape)[1]:
        part = shard[lo:hi, :] if axis == 0 else shard[:, lo:hi]
        pad = [(0, shape[0] - part.shape[0]), (0, shape[1] - part.shape[1])]
        out.append(jnp.pad(part, pad).astype(BF16))
    return out


def _gather_plan(names, shards):
    srcs, land_shapes, pieces, index = [], [], [], {}
    for li, name in enumerate(names):
        shape2d, parts = _layout(name, shards[name].shape)
        land_shapes.append(jax.ShapeDtypeStruct(shape2d, BF16))
        index[name] = []
        for (axis, base, stride, shape, _), src in zip(parts, _shard_pieces(name, shards[name])):
            index[name].append(len(pieces))
            pieces.append(("gather", len(srcs), li, axis, base, stride, shape))
            srcs.append(src)
    return srcs, land_shapes, pieces, index


def _scatter_plan(names, grads, shard_shapes):
    srcs, land_shapes, pieces, index = [], [], [], {}
    for si, name in enumerate(names):
        _, parts = _layout(name, shard_shapes[name])
        srcs.append(grads[name])
        index[name] = []
        for axis, base, stride, shape, _ in parts:
            index[name].append(len(land_shapes))
            pieces.append(("scatter", si, len(land_shapes), axis, base, stride, shape))
            land_shapes.append(jax.ShapeDtypeStruct((N_DEV,) + shape, grads[name].dtype))
    return srcs, land_shapes, pieces, index


def _small_views(small):
    row = lambda a: a.reshape(1, -1)
    ln = {k: row(small[k]) for k in ("ln1_g", "ln1_b", "ln2_g", "ln2_b", "ln3_g", "ln3_b", "ln4_g", "ln4_b",
                                      "mem_ln_g", "mem_ln_b", "hg_norm_g")}
    sg_w = small["sg_w_s"].reshape(SG_GROUPS, SG_CHUNK, SG_CHUNK)
    sg = dict(logits=jnp.swapaxes(small["hg_lb_logits"], 0, 1),
              g=small["sg_ln_g"].reshape(SG_GROUPS, 1, SG_DIM), b=small["sg_ln_b"].reshape(SG_GROUPS, 1, SG_DIM),
              w=sg_w, wt=jnp.swapaxes(sg_w, 1, 2), bs=small["sg_b_s"].reshape(SG_GROUPS, SG_CHUNK, 1))
    return ln, sg


def _forward(x, mem, target, get_w, small, first_deps=()):
    ln, sg = _small_views(small)
    xb = x.astype(BF16)
    a1, b1, s1 = _ffn_up(xb, get_w("ffn1_w_gate", ()), get_w("ffn1_w_up", ()), "ffn1_up", deps=first_deps)
    h1b, xh1, rs1 = _mm_res_ln(s1, get_w("ffn1_w_down", (s1,)), x, ln["ln1_g"], ln["ln1_b"], 0.5, "ffn1_down_ln")
    proj = _mm_nn(h1b, get_w("w_in", (h1b,)), "mix_in")
    oraw, mix, states = _hgrn_fwd(proj, sg["logits"], ln["hg_norm_g"])
    mix = _sgu_fwd(proj, mix, sg["g"], sg["b"], sg["w"], sg["bs"])
    h2b, xh2, rs2 = _mm_res_ln(mix, get_w("w_out", (mix,)), (xh1, ln["ln1_g"], ln["ln1_b"]), ln["ln2_g"], ln["ln2_b"],
                               1.0, "mix_out_ln")
    mb, mxh, mrs, kb, vb = _mem_kv(mem, ln["mem_ln_g"], ln["mem_ln_b"], get_w("xa_w_k", (h2b,)), get_w("xa_w_v", (h2b,)))
    qb, att = _attn_fwd(h2b, get_w("xa_w_q", (kb,)), kb, vb)
    h3b, xh3, rs3 = _mm_res_ln(att, get_w("xa_w_o", (att,)), (xh2, ln["ln2_g"], ln["ln2_b"]), ln["ln3_g"], ln["ln3_b"],
                               1.0, "attn_out_ln")
    a2, b2, loss, dy4, dy4b, dg4, db4 = _mm_res_ln(
        h3b, get_w("ffn2_w_down", (h3b,)), (xh3, ln["ln3_g"], ln["ln3_b"]), ln["ln4_g"], ln["ln4_b"], 0.5,
        "ffn2_fwd_loss", target=target, ffn=(get_w("ffn2_w_gate", (h3b,)), get_w("ffn2_w_up", (h3b,))))
    return dict(xb=xb, a1=a1, b1=b1, s1=s1, h1b=h1b, xh1=xh1, rs1=rs1, proj=proj, oraw=oraw, mix=mix, states=states,
                h2b=h2b, xh2=xh2, rs2=rs2, mb=mb, mxh=mxh, mrs=mrs, kb=kb, vb=vb, qb=qb, att=att, h3b=h3b, xh3=xh3,
                rs3=rs3, a2=a2, b2=b2, loss=loss, dy4=dy4, dy4b=dy4b, dg4=dg4, db4=db4)


def _backward(sv, wt, small, send):
    ln, sg = _small_views(small)
    gs = {"ln4_g": sv["dg4"], "ln4_b": sv["db4"]}
    loss, dy4, dy4b = sv["loss"], sv["dy4"], sv["dy4b"]
    g_down2 = _mm_tn(sv["a2"], dy4b, "g_ffn2_down", scale=0.5, gate=sv["b2"])
    da2, db2, dy3, dy3b, gs["ln3_g"], gs["ln3_b"] = _ffn_bwd_fused(
        dy4, dy4b, wt["ffn2_w_down"], wt["ffn2_w_gate"], wt["ffn2_w_up"], sv["a2"], sv["b2"], 0.5,
        (sv["xh3"], sv["rs3"], ln["ln3_g"]), "ffn2_bwd")
    g_gate2 = _mm_tn(sv["h3b"], da2, "g_ffn2_gate")
    g_up2 = _mm_tn(sv["h3b"], db2, "g_ffn2_up")
    tok = send(("ffn2_w_down", "ffn2_w_gate", "ffn2_w_up"), (g_down2, g_gate2, g_up2))

    g_o = _mm_tn(sv["att"], dy3b, "g_xa_o", deps=(tok,))
    dqb, dk, dv = _attn_bwd(dy3b, wt["xa_w_o"], sv["qb"], sv["kb"], sv["vb"])
    g_q = _mm_tn(sv["h2b"], dqb, "g_xa_q")
    g_k, g_v, gs["mem_ln_g"], gs["mem_ln_b"] = _mem_bwd(dk, dv, sv["mb"], sv["mxh"], sv["mrs"], ln["mem_ln_g"],
                                                        wt["xa_w_k"], wt["xa_w_v"])
    tok = send(("xa_w_o", "xa_w_q", "xa_w_k", "xa_w_v"), (g_o, g_q, g_k, g_v))
    dy2, dy2b, gs["ln2_g"], gs["ln2_b"] = _dx_ln(dy3, [(dqb, wt["xa_w_q"])], (sv["xh2"], sv["rs2"], ln["ln2_g"]),
                                                 "attn_dx_ln", deps=(tok,))

    g_out = _mm_tn(sv["mix"], dy2b, "g_w_out")
    dmix = _mm_nt(dy2b, wt["w_out"], "mix_out_bwd")
    dq, dfz, div, dgg, dlg, dgn = _hgrn_bwd(sv["proj"], sv["oraw"], dmix, sv["states"], sg["logits"], ln["hg_norm_g"])
    du, dvv, gs["sg_ln_g"], gs["sg_ln_b"], gs["sg_w_s"], gs["sg_b_s"] = _sgu_bwd(
        sv["proj"], dmix, sg["g"], sg["b"], sg["w"], sg["wt"], sg["bs"])
    gs["hg_lb_logits"] = jnp.swapaxes(dlg, 0, 1)
    gs["hg_norm_g"] = jnp.sum(dgn, axis=0)
    dproj = jnp.concatenate([dq, dfz, div, dgg, du, dvv], axis=1)
    g_in = _mm_tn(sv["h1b"], dproj, "g_w_in")
    tok = send(("w_out", "w_in"), (g_out, g_in))
    dy1, dy1b, gs["ln1_g"], gs["ln1_b"] = _dx_ln(dy2, [(dproj, wt["w_in"])], (sv["xh1"], sv["rs1"], ln["ln1_g"]),
                                                 "mix_dx_ln", deps=(tok,))

    g_down1 = _mm_tn(sv["s1"], dy1b, "g_ffn1_down", scale=0.5)
    tok = send(("ffn1_w_down",), (g_down1,))
    da1, db1 = _ffn_bwd_act(dy1b, wt["ffn1_w_down"], sv["a1"], sv["b1"], 0.5, "ffn1_bwd_act", deps=(tok,))
    g_gate1 = _mm_tn(sv["xb"], da1, "g_ffn1_gate")
    tok = send(("ffn1_w_gate",), (g_gate1,))
    g_up1 = _mm_tn(sv["xb"], db1, "g_ffn1_up", deps=(tok,))
    tok = send(("ffn1_w_up",), (g_up1,))
    grad_x = _dx_ln(dy1, [(da1, wt["ffn1_w_gate"]), (db1, wt["ffn1_w_up"])], None, "ffn1_dx", deps=(tok,))
    return loss, grad_x, gs


_WEIGHT_NAMES = ("ffn1_w_gate", "ffn1_w_up", "ffn1_w_down", "ln1_g", "ln1_b", "w_in", "hg_lb_logits", "hg_norm_g",
                 "sg_ln_g", "sg_ln_b", "sg_w_s", "sg_b_s", "w_out", "ln2_g", "ln2_b", "mem_ln_g", "mem_ln_b",
                 "xa_w_q", "xa_w_k", "xa_w_v", "xa_w_o", "ln3_g", "ln3_b", "ffn2_w_gate", "ffn2_w_up", "ffn2_w_down",
                 "ln4_g", "ln4_b")
_FIRST = ("ffn1_w_gate", "ffn1_w_up")
_SECOND = ("ffn1_w_down", "w_in", "w_out")
_THIRD = ("xa_w_k", "xa_w_v", "xa_w_q", "xa_w_o", "ffn2_w_gate", "ffn2_w_up", "ffn2_w_down")


def kernel(x, mem, ffn1_w_gate, ffn1_w_up, ffn1_w_down, ln1_g, ln1_b, w_in, hg_lb_logits, hg_norm_g, sg_ln_g, sg_ln_b, sg_w_s, sg_b_s, w_out, ln2_g, ln2_b, mem_ln_g, mem_ln_b, xa_w_q, xa_w_k, xa_w_v, xa_w_o, ln3_g, ln3_b, ffn2_w_gate, ffn2_w_up, ffn2_w_down, ln4_g, ln4_b, loss_target, m_ffn1_w_gate, m_ffn1_w_up, m_ffn1_w_down, m_ln1_g, m_ln1_b, m_w_in, m_hg_lb_logits, m_hg_norm_g, m_sg_ln_g, m_sg_ln_b, m_sg_w_s, m_sg_b_s, m_w_out, m_ln2_g, m_ln2_b, m_mem_ln_g, m_mem_ln_b, m_xa_w_q, m_xa_w_k, m_xa_w_v, m_xa_w_o, m_ln3_g, m_ln3_b, m_ffn2_w_gate, m_ffn2_w_up, m_ffn2_w_down, m_ln4_g, m_ln4_b, v_ffn1_w_gate, v_ffn1_w_up, v_ffn1_w_down, v_ln1_g, v_ln1_b, v_w_in, v_hg_lb_logits, v_hg_norm_g, v_sg_ln_g, v_sg_ln_b, v_sg_w_s, v_sg_b_s, v_w_out, v_ln2_g, v_ln2_b, v_mem_ln_g, v_mem_ln_b, v_xa_w_q, v_xa_w_k, v_xa_w_v, v_xa_w_o, v_ln3_g, v_ln3_b, v_ffn2_w_gate, v_ffn2_w_up, v_ffn2_w_down, v_ln4_g, v_ln4_b):
    args = dict(locals())
    w = {k: args[k] for k in _WEIGHT_NAMES}
    m = {k: args["m_" + k] for k in _WEIGHT_NAMES}
    v = {k: args["v_" + k] for k in _WEIGHT_NAMES}
    shards = {k: w[k][0] for k in _BIG_NAMES}
    shard_shapes = {k: shards[k].shape for k in _BIG_NAMES}
    small = {k: (w[k][0] if k != "hg_lb_logits" else w[k]) for k in _SMALL_NAMES}

    srcs1, shapes1, pieces1, idx1 = _gather_plan(_FIRST, shards)
    lands1 = _place_own(srcs1, shapes1, pieces1, "gather_first_own")
    rest = _SECOND + _THIRD
    srcs2, shapes2, pieces2, idx2 = _gather_plan(rest, shards)
    lands2 = _place_own(srcs2, shapes2, pieces2, "gather_rest_own")
    groups2 = [list(idx2[k]) for k in rest]
    lands1, tok1 = _routed_gather(srcs1, lands1, pieces1, tuple(lands2), "gather_first")
    sems2, srcs2, lands2, tok2 = _comm_start(srcs2, lands2, pieces2, groups2, "gather_rest_start", after=(tok1,))
    wt = dict(zip(_FIRST, lands1))
    pending = {k: gi for gi, k in enumerate(rest)}

    def get_w(name, after):
        if name in pending:
            gi = pending.pop(name)
            si = [pieces2[p][1] for p in groups2[gi]]
            sub = [(pieces2[p][0], row, 0) + pieces2[p][3:] for row, p in enumerate(groups2[gi])]
            wt[name] = _comm_wait([srcs2[s] for s in si], [lands2[gi]], sub, list(range(len(sub))), sems2[gi],
                                  after, "gather_wait_" + name)[0]
        return wt[name]

    sv = _forward(x[0], mem[0], loss_target[0], get_w, small, first_deps=(tok2,))

    sent = []

    def send(names, grads):
        srcs, shapes, pieces, idx = _scatter_plan(names, dict(zip(names, grads)), shard_shapes)
        lands = _place_own(srcs, shapes, pieces, "grads_own_%d" % len(sent))
        sems, srcs, lands, tok = _comm_start(srcs, lands, pieces, [list(range(len(pieces)))],
                                             "grads_start_%d" % len(sent))
        sent.append((names, srcs, lands, pieces, idx, sems[0]))
        return tok

    loss, grad_x, gs = _backward(sv, wt, small, send)

    ssrc = list(_pack_small_grads(gs, {k: w[k].shape for k in _SMALL_NAMES}, loss))
    sp = [("scatter", i, i, 0, 0, 0, a.shape) for i, a in enumerate(ssrc)]
    sshape = [jax.ShapeDtypeStruct((N_DEV,) + a.shape, F32) for a in ssrc]
    sl = _place_own(ssrc, sshape, sp, "small_own")
    ssem, ssrc, sl, _ = _comm_start(ssrc, sl, sp, [[0, 1]], "small_start")

    out_g, out_d, out_m, out_v = {}, {}, {}, {}
    after = (grad_x,)
    for n_sent, (names, srcs, lands, pieces, idx, sems) in enumerate(sent):
        lands = _comm_wait(srcs, lands, pieces, list(range(len(pieces))), sems, after, "grads_wait_%d" % n_sent)
        for k in names:
            axis = 1 if (k in _COL_FFN or k == "w_in") else 0
            if k in _COL_FFN:
                res = _adam_sharded([lands[i] for i in idx[k]], w[k][0].T, m[k][0].T, v[k][0].T, axis, "adam_" + k)
                res = [r.T for r in res]
            else:
                res = _adam_sharded([lands[i] for i in idx[k]], w[k][0], m[k][0], v[k][0], axis, "adam_" + k)
            out_g[k], out_d[k], out_m[k], out_v[k] = [r[None] for r in res]
        after = (out_v[names[-1]],)
    sl = _comm_wait(ssrc, sl, sp, [0, 1], ssem[0], after, "small_wait")
    small_out, loss_sum = _adam_small(sl[0], sl[1], w, m, v)
    for dst, res in zip((out_g, out_d, out_m, out_v), small_out):
        dst.update(res)
    loss_all = loss_sum[0, 0]
    return (loss_all, grad_x[None], *[out_g[k] for k in _WEIGHT_NAMES], *[out_d[k] for k in _WEIGHT_NAMES],
            *[out_m[k] for k in _WEIGHT_NAMES], *[out_v[k] for k in _WEIGHT_NAMES])
```

```python
import itertools

import jax
import jax.numpy as jnp
import numpy as np
from jax import lax
from jax.experimental import pallas as pl
from jax.experimental.pallas import tpu as pltpu

F32 = jnp.float32
BF16 = jnp.bfloat16

N_DEV = 8
ALPHA = 2.0 ** 0.25
LN_EPS = 1e-5
HG_HEADS = 4
HG_DIM = 128
SG_GROUPS = 4
SG_DIM = 128
SG_CHUNK = 128
X_HEADS = 4
HG_BLOCK = 16
HG_UNROLL = 16
ADAM_LR = 0.001
ADAM_B1 = 0.9
ADAM_B2 = 0.999
ADAM_EPS = 1e-08
ADAM_WD = 0.01
ADAM_STEP = 10
VMEM_LIMIT_V7X = 48 * 1024 * 1024
MXU_WIDTH_V7X = 256
LANES = 128
MESH_ID = pl.DeviceIdType.MESH
ANY = pl.BlockSpec(memory_space=pl.ANY)
HBM = pl.BlockSpec(memory_space=pltpu.HBM)
SEM = pl.BlockSpec(memory_space=pltpu.SEMAPHORE)
DATAFLOW = pltpu.SideEffectType.DATAFLOW_SIDE_EFFECTING


def _params(n_axes):
    return pltpu.CompilerParams(dimension_semantics=("arbitrary",) * n_axes, vmem_limit_bytes=VMEM_LIMIT_V7X)


def _dot(a, b):
    return jnp.dot(a, b, preferred_element_type=F32)


def _dot_nt(a, b):
    return lax.dot_general(a, b, (((1,), (1,)), ((), ())), preferred_element_type=F32)


def _dot_tn(a, b):
    return lax.dot_general(a, b, (((0,), (0,)), ((), ())), preferred_element_type=F32)


def _sigmoid(x):
    return 1.0 / (1.0 + jnp.exp(-x))


def _silu_and_grad(a):
    sig = _sigmoid(a)
    return a * sig, sig * (1.0 + a * (1.0 - sig))


_GELU_C = 0.7978845608028654


def _gelu_and_grad(x):
    inner = _GELU_C * (x + 0.044715 * x * x * x)
    t = jnp.tanh(inner)
    val = 0.5 * x * (1.0 + t)
    grad = 0.5 * (1.0 + t) + 0.5 * x * (1.0 - t * t) * _GELU_C * (1.0 + 3.0 * 0.044715 * x * x)
    return val, grad


def _ln_fwd(y, g, b):
    mu = jnp.mean(y, axis=-1, keepdims=True)
    yc = y - mu
    var = jnp.mean(yc * yc, axis=-1, keepdims=True)
    rstd = lax.rsqrt(var + LN_EPS)
    xhat = yc * rstd
    return xhat * g + b, xhat, rstd


def _ln_bwd(dh, xhat, rstd, g):
    dxh = dh * g
    m1 = jnp.mean(dxh, axis=-1, keepdims=True)
    m2 = jnp.mean(dxh * xhat, axis=-1, keepdims=True)
    dy = rstd * (dxh - m1 - xhat * m2)
    dg = jnp.sum(dh * xhat, axis=0, keepdims=True)
    db = jnp.sum(dh, axis=0, keepdims=True)
    return dy, dg, db


def _mask_dot(mask, x):
    hi = x.astype(BF16)
    lo = (x - hi.astype(F32)).astype(BF16)
    n = mask.shape[0]
    parts = [_dot(mask, hi[r:r + n, :]) + _dot(mask, lo[r:r + n, :]) for r in range(0, x.shape[0], n)]
    return parts[0] if len(parts) == 1 else jnp.concatenate(parts, axis=0)


def _block_masks(n):
    r = np.arange(n)[:, None]
    c = np.arange(n)[None, :]
    same = (r // HG_BLOCK) == (c // HG_BLOCK)
    return jnp.asarray(np.stack([same & (c <= r), same & (c >= r), same]), BF16)


def _row_tile(t):
    return min(t, 512)


def _col_tile(n):
    for cand in (512, 256, 128):
        if n % cand == 0:
            return cand
    return n


def _resident(w):
    return pl.BlockSpec(w.shape, lambda *_: (0, 0), pipeline_mode=pl.Buffered(1))


def _drop_deps(body, n_in, n_deps):
    if n_deps == 0:
        return body
    return lambda *refs: body(*refs[:n_in], *refs[n_in + n_deps:])


def _ffn_up(hb, wg, wu, name, deps=()):
    t, d = hb.shape
    f = wg.shape[1]
    tm = _row_tile(t)
    tn = _col_tile(f)

    def body(h_ref, wg_ref, wu_ref, a_ref, b_ref, s_ref):
        h = h_ref[...]
        for c in range(f // tn):
            cols = slice(c * tn, (c + 1) * tn)
            a = _dot(h, wg_ref[:, cols])
            b = _dot(h, wu_ref[:, cols])
            a_ref[:, cols] = a.astype(BF16)
            b_ref[:, cols] = b.astype(BF16)
            s_ref[:, cols] = (a * _sigmoid(a) * b).astype(BF16)

    act = pl.BlockSpec((tm, f), lambda i: (i, 0))
    return pl.pallas_call(
        _drop_deps(body, 3, len(deps)),
        grid=(t // tm,),
        in_specs=[pl.BlockSpec((tm, d), lambda i: (i, 0)), _resident(wg), _resident(wu)] + [ANY] * len(deps),
        out_specs=[act, act, act],
        out_shape=[jax.ShapeDtypeStruct((t, f), BF16)] * 3,
        compiler_params=_params(1),
        name=name,
    )(hb, wg, wu, *deps)


def _mm_res_ln(lhs, w, res, g, b, coef, name, target=None):
    t, kd = lhs.shape
    d = w.shape[1]
    tm = _row_tile(t)
    nt = t // tm
    from_norm = isinstance(res, tuple)
    n_res = 3 if from_norm else 1

    def body(*refs):
        l_ref, w_ref = refs[:2]
        r_refs = refs[2:2 + n_res]
        g_ref, b_ref = refs[2 + n_res:4 + n_res]
        rest = refs[4 + n_res:]
        prev = r_refs[0][...] * r_refs[1][...] + r_refs[2][...] if from_norm else r_refs[0][...]
        y = ALPHA * prev + coef * _dot(l_ref[...], w_ref[...])
        h, xhat, rstd = _ln_fwd(y, g_ref[...], b_ref[...])
        if target is None:
            hb_ref, xh_ref, rs_ref = rest
            hb_ref[...] = h.astype(BF16)
            xh_ref[...] = xhat
            rs_ref[...] = rstd
            return
        t_ref, loss_ref, dy_ref, dyb_ref, dg_ref, db_ref, lacc = rest
        i = pl.program_id(0)

        @pl.when(i == 0)
        def _():
            lacc[...] = jnp.zeros_like(lacc)
            dg_ref[...] = jnp.zeros_like(dg_ref)
            db_ref[...] = jnp.zeros_like(db_ref)

        err = h - t_ref[...]
        lacc[...] += jnp.sum(err * err, axis=0, keepdims=True)
        dy, dg, db = _ln_bwd(err * (1.0 / d), xhat, rstd, g_ref[...])
        dy_ref[...] = dy
        dyb_ref[...] = dy.astype(BF16)
        dg_ref[...] += dg
        db_ref[...] += db

        @pl.when(i == nt - 1)
        def _():
            loss_ref[...] = jnp.zeros_like(loss_ref) + jnp.sum(lacc[...], axis=1, keepdims=True) * (0.5 / d)

    row = pl.BlockSpec((tm, d), lambda i: (i, 0))
    vec = pl.BlockSpec((1, d), lambda i: (0, 0))
    res_specs = [row, vec, vec] if from_norm else [row]
    res_args = list(res) if from_norm else [res]
    in_specs = [pl.BlockSpec((tm, kd), lambda i: (i, 0)), _resident(w)] + res_specs + [vec, vec]
    args = [lhs, w] + res_args + [g, b]
    if target is None:
        out_specs = [row, row, pl.BlockSpec((tm, 1), lambda i: (i, 0))]
        out_shape = [jax.ShapeDtypeStruct((t, d), BF16), jax.ShapeDtypeStruct((t, d), F32),
                     jax.ShapeDtypeStruct((t, 1), F32)]
        scratch = []
    else:
        in_specs.append(row)
        args.append(target)
        out_specs = [pl.BlockSpec((1, LANES), lambda i: (0, 0)), row, row, vec, vec]
        out_shape = [jax.ShapeDtypeStruct((1, LANES), F32), jax.ShapeDtypeStruct((t, d), F32),
                     jax.ShapeDtypeStruct((t, d), BF16), jax.ShapeDtypeStruct((1, d), F32),
                     jax.ShapeDtypeStruct((1, d), F32)]
        scratch = [pltpu.VMEM((1, d), F32)]
    return pl.pallas_call(
        body,
        grid=(nt,),
        in_specs=in_specs,
        out_specs=out_specs,
        out_shape=out_shape,
        scratch_shapes=scratch,
        compiler_params=_params(1),
        name=name,
    )(*args)


def _mm_nn(lhs, w, name):
    t, kd = lhs.shape
    n = w.shape[1]
    tm = _row_tile(t)
    tn = _col_tile(n)

    def body(l_ref, w_ref, o_ref):
        lhs_v = l_ref[...]
        for c in range(n // tn):
            cols = slice(c * tn, (c + 1) * tn)
            o_ref[:, cols] = _dot(lhs_v, w_ref[:, cols])

    return pl.pallas_call(
        body,
        grid=(t // tm,),
        in_specs=[pl.BlockSpec((tm, kd), lambda i: (i, 0)), _resident(w)],
        out_specs=pl.BlockSpec((tm, n), lambda i: (i, 0)),
        out_shape=jax.ShapeDtypeStruct((t, n), F32),
        compiler_params=_params(1),
        name=name,
    )(lhs, w)


def _lower_bound(lg):
    m = jnp.max(lg, axis=0, keepdims=True)
    e = jnp.exp(lg - m)
    return e[0:1, :] / jnp.sum(e, axis=0, keepdims=True)


def _forget_terms(fz, lb):
    e = jnp.exp(-jnp.abs(fz))
    r = 1.0 / (1.0 + e)
    pos = fz >= 0.0
    sig = jnp.where(pos, r, e * r)
    nsig = jnp.where(pos, e * r, r)
    f = lb + (1.0 - lb) * sig
    k = (1.0 - lb) * nsig
    return sig, nsig, f, k


def _hg_tile(t):
    return min(t, 1024)


HG_HALF = HG_BLOCK // 2
NEG_BIG = -1e30


def _halves(a):
    return a[:HG_HALF, :], a[HG_HALF:, :]


def _causal_halves(s):
    return (0, 1) if s < HG_HALF else (1,)


def _decay_from(b_half, b_s, s, h, tidx):
    first = s - h * HG_HALF
    diff = b_half - b_s
    if first > 0:
        diff = jnp.where(tidx >= first, diff, NEG_BIG)
    return jnp.exp(diff)


def _hgrn_fwd(proj, logits, gn):
    t = proj.shape[0]
    ct = _hg_tile(t)
    nct = t // ct
    nblk = ct // HG_BLOCK
    nh = HG_HEADS
    mrows = min(ct, 256)

    def body(q_ref, fz_ref, iv_ref, gg_ref, lg_ref, gn_ref, mask_ref, oraw_ref, oa_ref, st_ref,
             state, qt_s, kt_s, k_s, b_s, dec_s):
        c = pl.program_id(1)

        @pl.when(c == 0)
        def _():
            state[...] = jnp.zeros_like(state)

        lb = _lower_bound(lg_ref[...])
        q = q_ref[...]
        _, _, f, k = _forget_terms(fz_ref[...], lb)
        logf = jnp.log(f)
        b = _mask_dot(mask_ref[0], logf)
        bend = _mask_dot(mask_ref[2], logf)
        qt_s[...] = (q * jnp.exp(b)).astype(BF16)
        kt_s[...] = (k * jnp.exp(bend - b)).astype(BF16)
        k_s[...] = k
        b_s[...] = b
        dec_s[...] = jnp.exp(bend)
        tidx = lax.broadcasted_iota(jnp.int32, (HG_HALF, HG_DIM), 0)

        def blk(i, carry):
            r0 = pl.multiple_of(i * HG_BLOCK, HG_BLOCK)
            rows = pl.ds(r0, HG_BLOCK)
            st = state[...]
            stb = st.astype(BF16)
            st_ref[i] = stb
            v = iv_ref[rows, :]
            qq = q_ref[rows, :]
            kk = k_s[rows, :]
            bb = b_s[rows, :]
            o = list(_halves(_dot_nt(qt_s[rows, :], stb)))
            qh, bh = _halves(qq), _halves(bb)
            for s in range(HG_BLOCK):
                ks, vs = kk[s:s + 1, :], v[s:s + 1, :]
                for h in _causal_halves(s):
                    e = _decay_from(bh[h], bb[s:s + 1, :], s, h, tidx)
                    acol = jnp.sum(qh[h] * (ks * e), axis=1, keepdims=True)
                    o[h] = o[h] + acol * vs
            oraw_ref[rows, :] = jnp.concatenate(o, axis=0)
            state[...] = st * dec_s[pl.ds(r0, 1), :] + _dot_tn(v.astype(BF16), kt_s[rows, :])
            return carry

        lax.fori_loop(0, nblk, blk, 0, unroll=HG_UNROLL)
        oraw = oraw_ref[...]
        r = lax.rsqrt(jnp.mean(oraw * oraw, axis=-1, keepdims=True) + LN_EPS)
        gg = gg_ref[...]
        oa_ref[...] = (oraw * r * gn_ref[...] * gg * _sigmoid(gg)).astype(BF16)

    def slab(off):
        return pl.BlockSpec((ct, HG_DIM), lambda h, c: (c, off + h))

    out_slab = pl.BlockSpec((ct, HG_DIM), lambda h, c: (c, h))
    return pl.pallas_call(
        body,
        grid=(nh, nct),
        in_specs=[slab(0), slab(nh), slab(2 * nh), slab(3 * nh),
                  pl.BlockSpec((None, 2, HG_DIM), lambda h, c: (h, 0, 0)),
                  pl.BlockSpec((1, HG_DIM), lambda h, c: (0, 0)),
                  pl.BlockSpec((3, mrows, mrows), lambda h, c: (0, 0, 0))],
        out_specs=[out_slab, out_slab, pl.BlockSpec((None, nblk, HG_DIM, HG_DIM), lambda h, c: (h, c, 0, 0))],
        out_shape=[jax.ShapeDtypeStruct((t, nh * HG_DIM), F32),
                   jax.ShapeDtypeStruct((t, (nh + SG_GROUPS) * HG_DIM), BF16),
                   jax.ShapeDtypeStruct((nh, t // HG_BLOCK, HG_DIM, HG_DIM), BF16)],
        scratch_shapes=[pltpu.VMEM((HG_DIM, HG_DIM), F32), pltpu.VMEM((ct, HG_DIM), BF16),
                        pltpu.VMEM((ct, HG_DIM), BF16), pltpu.VMEM((ct, HG_DIM), F32),
                        pltpu.VMEM((ct, HG_DIM), F32), pltpu.VMEM((ct, HG_DIM), F32)],
        compiler_params=_params(2),
        name="hgrn_fwd",
    )(proj, proj, proj, proj, logits, gn, _block_masks(mrows))


def _sg_tile(t):
    return min(t, 512)


def _sgu_chunk_fwd(u, v, ln_g, ln_b, wm, bs):
    ua, dua = _gelu_and_grad(u)
    va, dva = _gelu_and_grad(v)
    vn, xhat, rstd = _ln_fwd(va, ln_g, ln_b)
    s = _dot(wm, vn.astype(BF16)) + bs
    return ua, dua, dva, vn, xhat, rstd, s


def _tril_weight(w):
    n = SG_CHUNK
    r = lax.broadcasted_iota(jnp.int32, (n, n), 0)
    c = lax.broadcasted_iota(jnp.int32, (n, n), 1)
    return jnp.where(c <= r, w, 0.0)


def _sgu_fwd(proj, mix, ln_g, ln_b, w_s, b_col):
    t = proj.shape[0]
    ct = _sg_tile(t)
    ng = SG_GROUPS
    wide = ng * SG_DIM
    blk_u = 4 * HG_HEADS * HG_DIM // wide

    def body(u_ref, v_ref, g_ref, b_ref, w_ref, bs_ref, mix_ref, o_ref):
        del mix_ref
        for g in range(ng):
            lanes = slice(g * SG_DIM, (g + 1) * SG_DIM)
            wm = _tril_weight(w_ref[g]).astype(BF16)
            for n in range(ct // SG_CHUNK):
                rows = slice(n * SG_CHUNK, (n + 1) * SG_CHUNK)
                ua, _, _, _, _, _, s = _sgu_chunk_fwd(u_ref[rows, lanes], v_ref[rows, lanes], g_ref[g], b_ref[g], wm,
                                                      bs_ref[g])
                o_ref[rows, lanes] = (ua * s).astype(BF16)

    full = lambda a: pl.BlockSpec(a.shape, lambda c: (0,) * a.ndim)
    return pl.pallas_call(
        body,
        grid=(t // ct,),
        in_specs=[pl.BlockSpec((ct, wide), lambda c: (c, blk_u)), pl.BlockSpec((ct, wide), lambda c: (c, blk_u + 1)),
                  full(ln_g), full(ln_b), full(w_s), full(b_col), ANY],
        out_specs=pl.BlockSpec((ct, wide), lambda c: (c, 1)),
        out_shape=jax.ShapeDtypeStruct(mix.shape, mix.dtype),
        input_output_aliases={6: 0},
        compiler_params=_params(1),
        name="sgu_fwd",
    )(proj, proj, ln_g, ln_b, w_s, b_col, mix)


def _mem_kv(mem, g, b, wk, wv):
    m_len, d = mem.shape

    def body(m_ref, g_ref, b_ref, wk_ref, wv_ref, mb_ref, xh_ref, rs_ref, k_ref, v_ref):
        m, xhat, rstd = _ln_fwd(m_ref[...], g_ref[...], b_ref[...])
        mb = m.astype(BF16)
        mb_ref[...] = mb
        xh_ref[...] = xhat
        rs_ref[...] = rstd
        k_ref[...] = _dot(mb, wk_ref[...]).astype(BF16)
        v_ref[...] = _dot(mb, wv_ref[...]).astype(BF16)

    return pl.pallas_call(
        body,
        out_shape=[jax.ShapeDtypeStruct((m_len, d), BF16), jax.ShapeDtypeStruct((m_len, d), F32),
                   jax.ShapeDtypeStruct((m_len, 1), F32), jax.ShapeDtypeStruct((m_len, d), BF16),
                   jax.ShapeDtypeStruct((m_len, d), BF16)],
        compiler_params=pltpu.CompilerParams(vmem_limit_bytes=VMEM_LIMIT_V7X),
        name="mem_kv",
    )(mem, g, b, wk, wv)


def _softmax_rows(s):
    m = jnp.max(s, axis=-1, keepdims=True)
    p = jnp.exp(s - m)
    return p / jnp.sum(p, axis=-1, keepdims=True)


def _attn_fwd(hb, wq, kb, vb):
    t, d = hb.shape
    tm = _row_tile(t)
    dh = d // X_HEADS
    scale = dh ** -0.5

    def body(h_ref, wq_ref, k_ref, v_ref, q_ref, o_ref):
        q = _dot(h_ref[...], wq_ref[...]).astype(BF16)
        q_ref[...] = q
        for hd in range(X_HEADS):
            sl = slice(hd * dh, (hd + 1) * dh)
            p = _softmax_rows(_dot_nt(q[:, sl], k_ref[:, sl]) * scale)
            o_ref[:, sl] = _dot(p.astype(BF16), v_ref[:, sl]).astype(BF16)

    row = pl.BlockSpec((tm, d), lambda i: (i, 0))
    full = lambda a: pl.BlockSpec(a.shape, lambda i: (0, 0))
    return pl.pallas_call(
        body,
        grid=(t // tm,),
        in_specs=[row, full(wq), full(kb), full(vb)],
        out_specs=[row, row],
        out_shape=[jax.ShapeDtypeStruct((t, d), BF16), jax.ShapeDtypeStruct((t, d), BF16)],
        compiler_params=_params(1),
        name="attn_fwd",
    )(hb, wq, kb, vb)


def _ffn_bwd_act(dyb, wd, a, b, coef, name, deps=()):
    t, d = dyb.shape
    f = wd.shape[0]
    tm = _row_tile(t)
    tn = _col_tile(f)

    def body(dy_ref, wd_ref, a_ref, b_ref, da_ref, db_ref):
        dy = dy_ref[...]
        for c in range(f // tn):
            cols = slice(c * tn, (c + 1) * tn)
            ds = _dot_nt(dy, wd_ref[cols, :]) * coef
            silu, dsilu = _silu_and_grad(a_ref[:, cols].astype(F32))
            da_ref[:, cols] = (ds * b_ref[:, cols].astype(F32) * dsilu).astype(BF16)
            db_ref[:, cols] = (ds * silu).astype(BF16)

    act = pl.BlockSpec((tm, f), lambda i: (i, 0))
    return pl.pallas_call(
        _drop_deps(body, 4, len(deps)),
        grid=(t // tm,),
        in_specs=[pl.BlockSpec((tm, d), lambda i: (i, 0)), _resident(wd), act, act] + [ANY] * len(deps),
        out_specs=[act, act],
        out_shape=[jax.ShapeDtypeStruct((t, f), BF16), jax.ShapeDtypeStruct((t, f), BF16)],
        compiler_params=_params(1),
        name=name,
    )(dyb, wd, a, b, *deps)


def _ffn_bwd_fused(dy, dyb, wd, wg, wu, a, b, coef, ln, name):
    t, d = dy.shape
    f = wd.shape[0]
    tm = min(t, 256)
    tn = _col_tile(f)

    def body(dy_ref, dyb_ref, wd_ref, wg_ref, wu_ref, a_ref, b_ref, xh_ref, rs_ref, g_ref,
             da_ref, db_ref, dyo_ref, dyob_ref, dg_ref, dbl_ref):
        dyb_v = dyb_ref[...]
        dh = ALPHA * dy_ref[...]
        for c in range(f // tn):
            cols = slice(c * tn, (c + 1) * tn)
            ds = _dot_nt(dyb_v, wd_ref[cols, :]) * coef
            silu, dsilu = _silu_and_grad(a_ref[:, cols].astype(F32))
            da = (ds * b_ref[:, cols].astype(F32) * dsilu).astype(BF16)
            db = (ds * silu).astype(BF16)
            da_ref[:, cols] = da
            db_ref[:, cols] = db
            dh = dh + _dot_nt(da, wg_ref[:, cols]) + _dot_nt(db, wu_ref[:, cols])

        @pl.when(pl.program_id(0) == 0)
        def _():
            dg_ref[...] = jnp.zeros_like(dg_ref)
            dbl_ref[...] = jnp.zeros_like(dbl_ref)

        dyp, dg, dbl = _ln_bwd(dh, xh_ref[...], rs_ref[...], g_ref[...])
        dyo_ref[...] = dyp
        dyob_ref[...] = dyp.astype(BF16)
        dg_ref[...] += dg
        dbl_ref[...] += dbl

    row = pl.BlockSpec((tm, d), lambda i: (i, 0))
    act = pl.BlockSpec((tm, f), lambda i: (i, 0))
    vec = pl.BlockSpec((1, d), lambda i: (0, 0))
    return pl.pallas_call(
        body,
        grid=(t // tm,),
        in_specs=[row, row, _resident(wd), _resident(wg), _resident(wu), act, act, row,
                  pl.BlockSpec((tm, 1), lambda i: (i, 0)), vec],
        out_specs=[act, act, row, row, vec, vec],
        out_shape=[jax.ShapeDtypeStruct((t, f), BF16), jax.ShapeDtypeStruct((t, f), BF16),
                   jax.ShapeDtypeStruct((t, d), F32), jax.ShapeDtypeStruct((t, d), BF16),
                   jax.ShapeDtypeStruct((1, d), F32), jax.ShapeDtypeStruct((1, d), F32)],
        compiler_params=_params(1),
        name=name,
    )(dy, dyb, wd, wg, wu, a, b, *ln)


def _mm_tn(a, b, name, scale=1.0, deps=()):
    t, m = a.shape
    n = b.shape[1]
    tt = _row_tile(t)
    nt = t // tt
    tm_o, tn_o = m, n

    def body(a_ref, b_ref, o_ref, acc):
        k = pl.program_id(2)

        @pl.when(k == 0)
        def _():
            acc[...] = jnp.zeros_like(acc)

        acc[...] += _dot_tn(a_ref[...], b_ref[...])

        @pl.when(k == nt - 1)
        def _():
            o_ref[...] = (acc[...] * scale).astype(BF16)

    return pl.pallas_call(
        _drop_deps(body, 2, len(deps)),
        grid=(m // tm_o, n // tn_o, nt),
        in_specs=[pl.BlockSpec((tt, tm_o), lambda i, j, k: (k, i)), pl.BlockSpec((tt, tn_o), lambda i, j, k: (k, j))]
        + [ANY] * len(deps),
        out_specs=pl.BlockSpec((tm_o, tn_o), lambda i, j, k: (i, j)),
        out_shape=jax.ShapeDtypeStruct((m, n), BF16),
        scratch_shapes=[pltpu.VMEM((tm_o, tn_o), F32)],
        compiler_params=_params(3),
        name=name,
    )(a, b, *deps)


def _mm_nt(lhs, w, name):
    t, d = lhs.shape
    kd = w.shape[0]
    tm = _row_tile(t)

    def body(l_ref, w_ref, o_ref):
        o_ref[...] = _dot_nt(l_ref[...], w_ref[...])

    return pl.pallas_call(
        body,
        grid=(t // tm,),
        in_specs=[pl.BlockSpec((tm, d), lambda i: (i, 0)), _resident(w)],
        out_specs=pl.BlockSpec((tm, kd), lambda i: (i, 0)),
        out_shape=jax.ShapeDtypeStruct((t, kd), F32),
        compiler_params=_params(1),
        name=name,
    )(lhs, w)


def _dx_ln(dy, pairs, ln, name, deps=()):
    t, d = dy.shape
    npair = len(pairs)
    tm = min(t, 512 // npair)
    nt = t // tm
    n_in = 1 + 2 * npair + (3 if ln is not None else 0)

    def body(*refs):
        dy_ref = refs[0]
        pr = refs[1:1 + 2 * npair]
        pos = 1 + 2 * npair
        dh = ALPHA * dy_ref[...]
        for p in range(npair):
            dh = dh + _dot_nt(pr[2 * p][...], pr[2 * p + 1][...])
        if ln is not None:
            xh_ref, rs_ref, g_ref = refs[pos:pos + 3]
            dyo_ref, dyb_ref, dg_ref, db_ref = refs[pos + 3:pos + 7]

            @pl.when(pl.program_id(0) == 0)
            def _():
                dg_ref[...] = jnp.zeros_like(dg_ref)
                db_ref[...] = jnp.zeros_like(db_ref)

            dyp, dg, db = _ln_bwd(dh, xh_ref[...], rs_ref[...], g_ref[...])
            dyo_ref[...] = dyp
            dyb_ref[...] = dyp.astype(BF16)
            dg_ref[...] += dg
            db_ref[...] += db
        else:
            refs[pos][...] = dh

    row = pl.BlockSpec((tm, d), lambda i: (i, 0))
    vec = pl.BlockSpec((1, d), lambda i: (0, 0))
    in_specs = [row]
    args = [dy]
    for lhs, w in pairs:
        in_specs += [pl.BlockSpec((tm, lhs.shape[1]), lambda i: (i, 0)), _resident(w)]
        args += [lhs, w]
    if ln is not None:
        in_specs += [row, pl.BlockSpec((tm, 1), lambda i: (i, 0)), vec]
        args += list(ln)
        out_specs = [row, row, vec, vec]
        out_shape = [jax.ShapeDtypeStruct((t, d), F32), jax.ShapeDtypeStruct((t, d), BF16),
                     jax.ShapeDtypeStruct((1, d), F32), jax.ShapeDtypeStruct((1, d), F32)]
    else:
        out_specs = row
        out_shape = jax.ShapeDtypeStruct((t, d), F32)
    return pl.pallas_call(
        _drop_deps(body, n_in, len(deps)),
        grid=(nt,),
        in_specs=in_specs + [ANY] * len(deps),
        out_specs=out_specs,
        out_shape=out_shape,
        compiler_params=_params(1),
        name=name,
    )(*args, *deps)


def _hgrn_bwd(proj, oraw, dmix, states, logits, gn):
    t = proj.shape[0]
    ct = _hg_tile(t)
    nct = t // ct
    nblk = ct // HG_BLOCK
    nh = HG_HEADS
    mrows = min(ct, 256)

    def body(q_ref, fz_ref, iv_ref, gg_ref, or_ref, do_ref, st_ref, lg_ref, gn_ref, mask_ref,
             dq_ref, dfz_ref, div_ref, dgg_ref, dlg_ref, dgn_ref,
             dstate, qt_s, kt_s, k_s, b_s, eb_s, ekb_s, dec_s, dor_s, dbl_s, gr_s, dk_s, dlb_acc):
        c = pl.program_id(1)

        @pl.when(c == 0)
        def _():
            dstate[...] = jnp.zeros_like(dstate)
            dlb_acc[...] = jnp.zeros_like(dlb_acc)
            dgn_ref[...] = jnp.zeros_like(dgn_ref)

        lb = _lower_bound(lg_ref[...])
        q = q_ref[...]
        sig, nsig, f, k = _forget_terms(fz_ref[...], lb)
        logf = jnp.log(f)
        b = _mask_dot(mask_ref[0], logf)
        bend = _mask_dot(mask_ref[2], logf)
        eb = jnp.exp(b)
        ekb = jnp.exp(bend - b)
        qt_s[...] = (q * eb).astype(BF16)
        kt_s[...] = (k * ekb).astype(BF16)
        k_s[...] = k
        b_s[...] = b
        eb_s[...] = eb
        ekb_s[...] = ekb
        dec_s[...] = jnp.exp(bend)
        oraw = or_ref[...]
        r = lax.rsqrt(jnp.mean(oraw * oraw, axis=-1, keepdims=True) + LN_EPS)
        on = oraw * r
        gg = gg_ref[...]
        silu, dsilu = _silu_and_grad(gg)
        doa = do_ref[...]
        gnv = gn_ref[...]
        dgg_ref[...] = (doa * on * gnv * dsilu).astype(BF16)
        dyn = doa * silu
        dgn_ref[...] += jnp.sum(dyn * on, axis=0, keepdims=True)
        don = dyn * gnv
        dor_s[...] = r * (don - on * jnp.mean(don * on, axis=-1, keepdims=True))
        tidx = lax.broadcasted_iota(jnp.int32, (HG_HALF, HG_DIM), 0)

        def blk(ii, carry):
            i = nblk - 1 - ii
            r0 = pl.multiple_of(i * HG_BLOCK, HG_BLOCK)
            rows = pl.ds(r0, HG_BLOCK)
            st = st_ref[i]
            dst = dstate[...]
            dstb = dst.astype(BF16)
            do = dor_s[rows, :]
            dob = do.astype(BF16)
            v = iv_ref[rows, :]
            vb = v.astype(BF16)
            qq = q_ref[rows, :]
            kk = k_s[rows, :]
            bb = b_s[rows, :]
            qt = qt_s[rows, :]
            kt = kt_s[rows, :]
            dec = dec_s[pl.ds(r0, 1), :]
            dkt = _dot(vb, dstb)
            dq = _dot(dob, st) * eb_s[rows, :]
            dk = dkt * ekb_s[rows, :]
            dv = _dot_nt(kt, dstb)
            gend = (jnp.sum(kk * dk, axis=0, keepdims=True)
                    + dec * jnp.sum(dst * st.astype(F32), axis=0, keepdims=True))
            qh, bh, doh = _halves(qq), _halves(bb), _halves(do)
            dqh, dkh, dvh = list(_halves(dq)), list(_halves(dk)), list(_halves(dv))
            for s in range(HG_BLOCK):
                ks, vs = kk[s:s + 1, :], v[s:s + 1, :]
                dk_part = dv_part = None
                for h in _causal_halves(s):
                    e = _decay_from(bh[h], bb[s:s + 1, :], s, h, tidx)
                    ke = ks * e
                    acol = jnp.sum(qh[h] * ke, axis=1, keepdims=True)
                    dacol = jnp.sum(doh[h] * vs, axis=1, keepdims=True)
                    dqh[h] = dqh[h] + dacol * ke
                    pk = dacol * (qh[h] * e)
                    pv = acol * doh[h]
                    dk_part = pk if dk_part is None else dk_part + pk
                    dv_part = pv if dv_part is None else dv_part + pv
                hs, row = divmod(s, HG_HALF)
                dkh[hs] = dkh[hs] + jnp.where(tidx == row, jnp.sum(dk_part, axis=0, keepdims=True), 0.0)
                dvh[hs] = dvh[hs] + jnp.where(tidx == row, jnp.sum(dv_part, axis=0, keepdims=True), 0.0)
            dq = jnp.concatenate(dqh, axis=0)
            dk = jnp.concatenate(dkh, axis=0)
            dv = jnp.concatenate(dvh, axis=0)
            dq_ref[rows, :] = dq.astype(BF16)
            div_ref[rows, :] = dv.astype(BF16)
            dk_s[rows, :] = dk
            dbl_s[rows, :] = qq * dq - kk * dk
            gr_s[rows, :] = jnp.zeros((HG_BLOCK, HG_DIM), F32) + gend
            dstate[...] = dst * dec + _dot_tn(dob, qt)
            return carry

        lax.fori_loop(0, nblk, blk, 0, unroll=HG_UNROLL)
        dlogf = _mask_dot(mask_ref[1], dbl_s[...]) + gr_s[...]
        dk = dk_s[...]
        dfz_ref[...] = ((dlogf / f - dk) * ((1.0 - lb) * sig * nsig)).astype(BF16)
        dlb_acc[...] += jnp.sum((dlogf / f - dk) * nsig, axis=0, keepdims=True)

        @pl.when(c == nct - 1)
        def _():
            dl0 = dlb_acc[...] * lb * (1.0 - lb)
            layer = lax.broadcasted_iota(jnp.int32, (2, HG_DIM), 0)
            dlg_ref[...] = jnp.where(layer == 0, dl0, -dl0)

    def slab(off):
        return pl.BlockSpec((ct, HG_DIM), lambda h, c: (nct - 1 - c, off + h))

    out_slab = pl.BlockSpec((ct, HG_DIM), lambda h, c: (nct - 1 - c, h))
    tile_f32 = pltpu.VMEM((ct, HG_DIM), F32)
    tile_b16 = pltpu.VMEM((ct, HG_DIM), BF16)
    slab_shape = jax.ShapeDtypeStruct((t, nh * HG_DIM), BF16)
    return pl.pallas_call(
        body,
        grid=(nh, nct),
        in_specs=[slab(0), slab(nh), slab(2 * nh), slab(3 * nh), slab(0), slab(0),
                  pl.BlockSpec((None, nblk, HG_DIM, HG_DIM), lambda h, c: (h, nct - 1 - c, 0, 0)),
                  pl.BlockSpec((None, 2, HG_DIM), lambda h, c: (h, 0, 0)),
                  pl.BlockSpec((1, HG_DIM), lambda h, c: (0, 0)),
                  pl.BlockSpec((3, mrows, mrows), lambda h, c: (0, 0, 0))],
        out_specs=[out_slab, out_slab, out_slab, out_slab,
                   pl.BlockSpec((None, 2, HG_DIM), lambda h, c: (h, 0, 0)),
                   pl.BlockSpec((None, 1, HG_DIM), lambda h, c: (h, 0, 0))],
        out_shape=[slab_shape, slab_shape, slab_shape, slab_shape,
                   jax.ShapeDtypeStruct((nh, 2, HG_DIM), F32), jax.ShapeDtypeStruct((nh, 1, HG_DIM), F32)],
        scratch_shapes=[pltpu.VMEM((HG_DIM, HG_DIM), F32), tile_b16, tile_b16, tile_f32, tile_f32, tile_f32, tile_f32,
                        tile_f32, tile_f32, tile_f32, tile_f32, tile_f32, pltpu.VMEM((1, HG_DIM), F32)],
        compiler_params=_params(2),
        name="hgrn_bwd",
    )(proj, proj, proj, proj, oraw, dmix, states, logits, gn, _block_masks(mrows))


def _sgu_bwd(proj, dmix, ln_g, ln_b, w_s, w_t, b_col):
    t = proj.shape[0]
    ct = _sg_tile(t)
    nct = t // ct
    ng = SG_GROUPS
    off_u = 4 * HG_HEADS
    off_v = off_u + ng
    n = SG_CHUNK

    def body(u_ref, v_ref, do_ref, g_ref, b_ref, w_ref, wt_ref, bs_ref, du_ref, dv_ref, dg_ref, db_ref, dw_ref, dbs_ref):
        c = pl.program_id(1)

        @pl.when(c == 0)
        def _():
            dg_ref[...] = jnp.zeros_like(dg_ref)
            db_ref[...] = jnp.zeros_like(db_ref)
            dw_ref[...] = jnp.zeros_like(dw_ref)
            dbs_ref[...] = jnp.zeros_like(dbs_ref)

        r = lax.broadcasted_iota(jnp.int32, (n, n), 0)
        cc = lax.broadcasted_iota(jnp.int32, (n, n), 1)
        wm = jnp.where(cc <= r, w_ref[...], 0.0).astype(BF16)
        wmt = jnp.where(r <= cc, wt_ref[...], 0.0).astype(BF16)
        for ci in range(ct // n):
            rows = slice(ci * n, (ci + 1) * n)
            ua, dua, dva, vn, xhat, rstd, s = _sgu_chunk_fwd(u_ref[rows, :], v_ref[rows, :], g_ref[...], b_ref[...],
                                                             wm, bs_ref[...])
            do = do_ref[rows, :]
            du_ref[rows, :] = (do * s * dua).astype(BF16)
            ds = do * ua
            dsb = ds.astype(BF16)
            dbs_ref[...] += jnp.sum(ds, axis=1, keepdims=True)
            dw_ref[...] += _dot_nt(dsb, vn.astype(BF16))
            dvn = _dot(wmt, dsb)
            dva_in, dg, db = _ln_bwd(dvn, xhat, rstd, g_ref[...])
            dg_ref[...] += dg
            db_ref[...] += db
            dv_ref[rows, :] = (dva_in * dva).astype(BF16)

        @pl.when(c == nct - 1)
        def _():
            dw_ref[...] = jnp.where(cc <= r, dw_ref[...], 0.0)

    vec = pl.BlockSpec((None, 1, SG_DIM), lambda g, c: (g, 0, 0))
    mat = pl.BlockSpec((None, n, n), lambda g, c: (g, 0, 0))
    col = pl.BlockSpec((None, n, 1), lambda g, c: (g, 0, 0))
    out_slab = pl.BlockSpec((ct, SG_DIM), lambda g, c: (c, g))
    return pl.pallas_call(
        body,
        grid=(ng, nct),
        in_specs=[pl.BlockSpec((ct, SG_DIM), lambda g, c: (c, off_u + g)),
                  pl.BlockSpec((ct, SG_DIM), lambda g, c: (c, off_v + g)),
                  pl.BlockSpec((ct, SG_DIM), lambda g, c: (c, ng + g)), vec, vec, mat, mat, col],
        out_specs=[out_slab, out_slab, vec, vec, mat, col],
        out_shape=[jax.ShapeDtypeStruct((t, ng * SG_DIM), BF16), jax.ShapeDtypeStruct((t, ng * SG_DIM), BF16),
                   jax.ShapeDtypeStruct((ng, 1, SG_DIM), F32), jax.ShapeDtypeStruct((ng, 1, SG_DIM), F32),
                   jax.ShapeDtypeStruct((ng, n, n), F32), jax.ShapeDtypeStruct((ng, n, 1), F32)],
        compiler_params=_params(2),
        name="sgu_bwd",
    )(proj, proj, dmix, ln_g, ln_b, w_s, w_t, b_col)


def _attn_bwd(dyb, wo, qb, kb, vb):
    t, d = dyb.shape
    m_len = kb.shape[0]
    tm = _row_tile(t)
    dh = d // X_HEADS
    scale = dh ** -0.5

    def body(dy_ref, wo_ref, q_ref, k_ref, v_ref, dq_ref, dk_ref, dv_ref):
        i = pl.program_id(0)

        @pl.when(i == 0)
        def _():
            dk_ref[...] = jnp.zeros_like(dk_ref)
            dv_ref[...] = jnp.zeros_like(dv_ref)

        do = _dot_nt(dy_ref[...], wo_ref[...]).astype(BF16)
        for hd in range(X_HEADS):
            sl = slice(hd * dh, (hd + 1) * dh)
            qh = q_ref[:, sl]
            p = _softmax_rows(_dot_nt(qh, k_ref[:, sl]) * scale)
            doh = do[:, sl]
            dp = _dot_nt(doh, v_ref[:, sl])
            ds = (p * (dp - jnp.sum(dp * p, axis=-1, keepdims=True)) * scale).astype(BF16)
            dq_ref[:, sl] = _dot(ds, k_ref[:, sl]).astype(BF16)
            dk_ref[:, sl] += _dot_tn(ds, qh)
            dv_ref[:, sl] += _dot_tn(p.astype(BF16), doh)

    row = pl.BlockSpec((tm, d), lambda i: (i, 0))
    full = lambda a: pl.BlockSpec(a.shape, lambda i: (0, 0))
    kv = pl.BlockSpec((m_len, d), lambda i: (0, 0))
    return pl.pallas_call(
        body,
        grid=(t // tm,),
        in_specs=[row, full(wo), row, full(kb), full(vb)],
        out_specs=[row, kv, kv],
        out_shape=[jax.ShapeDtypeStruct((t, d), BF16), jax.ShapeDtypeStruct((m_len, d), F32),
                   jax.ShapeDtypeStruct((m_len, d), F32)],
        compiler_params=_params(1),
        name="attn_bwd",
    )(dyb, wo, qb, kb, vb)


def _mem_bwd(dk, dv, mb, xhat, rstd, g, wk, wv):
    m_len, d = dk.shape

    def body(dk_ref, dv_ref, mb_ref, xh_ref, rs_ref, g_ref, wk_ref, wv_ref, gwk_ref, gwv_ref, dg_ref, db_ref):
        dkb = dk_ref[...].astype(BF16)
        dvb = dv_ref[...].astype(BF16)
        mb_v = mb_ref[...]
        gwk_ref[...] = _dot_tn(mb_v, dkb).astype(BF16)
        gwv_ref[...] = _dot_tn(mb_v, dvb).astype(BF16)
        dm = _dot_nt(dkb, wk_ref[...]) + _dot_nt(dvb, wv_ref[...])
        _, dg, db = _ln_bwd(dm, xh_ref[...], rs_ref[...], g_ref[...])
        dg_ref[...] = dg
        db_ref[...] = db

    return pl.pallas_call(
        body,
        out_shape=[jax.ShapeDtypeStruct((d, d), BF16), jax.ShapeDtypeStruct((d, d), BF16),
                   jax.ShapeDtypeStruct((1, d), F32), jax.ShapeDtypeStruct((1, d), F32)],
        compiler_params=pltpu.CompilerParams(vmem_limit_bytes=VMEM_LIMIT_V7X),
        name="mem_bwd",
    )(dk, dv, mb, xhat, rstd, g, wk, wv)


def _adamw(w, g, m, v):
    m = ADAM_B1 * m + (1.0 - ADAM_B1) * g
    v = ADAM_B2 * v + (1.0 - ADAM_B2) * (g * g)
    m_hat = m / (1.0 - ADAM_B1 ** ADAM_STEP)
    v_hat = v / (1.0 - ADAM_B2 ** ADAM_STEP)
    delta = -ADAM_LR * (m_hat / (jnp.sqrt(v_hat) + ADAM_EPS) + ADAM_WD * w)
    return delta, m, v


def _slot_sum(ref):
    g = ref[0].astype(F32)
    for s in range(1, N_DEV):
        g = g + ref[s].astype(F32)
    return g


def _adam_sharded(lands, w, m, v, axis, name):
    rows, cols = w.shape
    nl = len(lands)
    transposed = axis == 1 and nl == 2
    if transposed:
        rows, cols = cols, rows
        tr = 256
        grid = (rows // tr,)
        wblk = pl.BlockSpec((cols, tr), lambda i: (0, i))
        lblk = [pl.BlockSpec((N_DEV, tr, a.shape[2]), lambda i: (0, i, 0)) for a in lands]
    elif axis == 1:
        tr = 256 if rows % 256 == 0 else rows
        grid = (rows // tr,)
        wblk = pl.BlockSpec((tr, cols), lambda i: (i, 0))
        lblk = [pl.BlockSpec((N_DEV, tr, a.shape[2]), lambda i: (0, i, 0)) for a in lands]
    else:
        tc = _col_tile(cols)
        grid = (cols // tc,)
        wblk = pl.BlockSpec((rows, tc), lambda i: (0, i))
        lblk = [pl.BlockSpec((N_DEV, a.shape[1], tc), lambda i: (0, 0, i)) for a in lands]

    def body(*refs):
        w_ref, m_ref, v_ref = refs[nl:nl + 3]
        g_ref, d_ref, nm_ref, nv_ref = refs[nl + 3:]
        g = _slot_sum(refs[0])
        if nl == 2:
            tail = _slot_sum(refs[1])
            if transposed:
                g = jnp.concatenate([g.T, tail.T[:cols - g.shape[1], :]], axis=0)
            elif axis == 1:
                g = jnp.concatenate([g, tail[:, :cols - g.shape[1]]], axis=1)
            else:
                g = jnp.concatenate([g, tail[:rows - g.shape[0], :]], axis=0)
        delta, nm, nv = _adamw(w_ref[...], g, m_ref[...], v_ref[...])
        g_ref[...] = g
        d_ref[...] = delta
        nm_ref[...] = nm
        nv_ref[...] = nv

    shp = jax.ShapeDtypeStruct(w.shape, F32)
    return pl.pallas_call(
        body,
        grid=grid,
        in_specs=lblk + [wblk, wblk, wblk],
        out_specs=[wblk, wblk, wblk, wblk],
        out_shape=[shp, shp, shp, shp],
        compiler_params=_params(1),
        name=name,
    )(*[pltpu.with_memory_space_constraint(a, pltpu.HBM) for a in (*lands, w, m, v)])


def _mesh_pos():
    return lax.axis_index("x"), lax.axis_index("y"), lax.axis_index("c")


def _peer(k):
    x, y, c = _mesh_pos()
    pos = (x ^ (k >> 2), y ^ ((k >> 1) & 1), c ^ (k & 1))
    return pos, 4 * pos[0] + 2 * pos[1] + pos[2]


def _sem_index(row, k):
    return row * (N_DEV - 1) + k - 1


def _window(ref, axis, start, size):
    align = 16 if axis == 0 else LANES
    start = pl.multiple_of(start, align)
    return ref.at[pl.ds(start, size), :] if axis == 0 else ref.at[:, pl.ds(start, size)]


def _piece_refs(piece, srcs, lands, me, peer):
    kind, si, li, axis, base, stride, shape = piece
    if kind == "gather":
        return srcs[si], _window(lands[li], axis, base + stride * me, shape[axis])
    return _window(srcs[si], axis, base + stride * peer, shape[axis]), lands[li].at[me]


def _place_own(srcs, land_shapes, pieces, name):
    ns, nl, npc = len(srcs), len(land_shapes), len(pieces)

    def body(*refs):
        s_refs = refs[:ns]
        l_refs = refs[ns:ns + nl]
        bufs = refs[ns + nl:ns + nl + npc]
        sems = refs[ns + nl + npc]
        x, y, c = _mesh_pos()
        me = 4 * x + 2 * y + c
        loads = []
        for p, piece in enumerate(pieces):
            src, dst = _piece_refs(piece, s_refs, l_refs, me, me)
            cp = pltpu.make_async_copy(src, bufs[p], sems.at[0, p])
            cp.start()
            loads.append((cp, dst))
        stores = []
        for p, (cp, dst) in enumerate(loads):
            cp.wait()
            out = pltpu.make_async_copy(bufs[p], dst, sems.at[1, p])
            out.start()
            stores.append(out)
        for out in stores:
            out.wait()

    out = pl.pallas_call(
        body,
        in_specs=[ANY] * ns,
        out_specs=[ANY] * nl,
        out_shape=list(land_shapes),
        scratch_shapes=[pltpu.VMEM(pc[6], srcs[pc[1]].dtype) for pc in pieces] + [pltpu.SemaphoreType.DMA((2, npc))],
        compiler_params=pltpu.CompilerParams(vmem_limit_bytes=VMEM_LIMIT_V7X),
        name=name,
    )(*srcs)
    return list(out)


def _comm_start(srcs, lands, pieces, groups, name, after=()):
    ns, nl, na, ng = len(srcs), len(lands), len(after), len(groups)

    def body(*refs):
        s_refs = refs[:ns]
        l_refs = refs[ns:ns + nl]
        outs = refs[ns + nl + na:]
        sems = outs[:2 * ng]
        token = outs[-1]
        x, y, c = _mesh_pos()
        me = 4 * x + 2 * y + c
        for g, members in enumerate(groups):
            for row, p in enumerate(members):
                for k in range(1, N_DEV):
                    pos, peer = _peer(k)
                    src, dst = _piece_refs(pieces[p], s_refs, l_refs, me, peer)
                    pltpu.make_async_remote_copy(src_ref=src, dst_ref=dst, send_sem=sems[2 * g].at[_sem_index(row, k)],
                                                 recv_sem=sems[2 * g + 1].at[_sem_index(row, k)], device_id=pos,
                                                 device_id_type=MESH_ID).start()
        token[...] = jnp.zeros_like(token)

    sem_shapes = []
    for members in groups:
        sem_shapes += [pltpu.SemaphoreType.DMA((len(members) * (N_DEV - 1),))] * 2
    hbm_of = lambda a: pltpu.HBM(a.shape, a.dtype)
    out = pl.pallas_call(
        body,
        in_specs=[HBM] * (ns + nl) + [ANY] * na,
        out_specs=[SEM] * (2 * ng) + [HBM] * (ns + nl) + [pl.BlockSpec(memory_space=pltpu.VMEM)],
        out_shape=sem_shapes + [hbm_of(a) for a in srcs] + [hbm_of(a) for a in lands]
        + [jax.ShapeDtypeStruct((8, LANES), F32)],
        input_output_aliases={i: 2 * ng + i for i in range(ns + nl)},
        compiler_params=pltpu.CompilerParams(has_side_effects=DATAFLOW),
        name=name,
    )(*[pltpu.with_memory_space_constraint(a, pltpu.HBM) for a in list(srcs) + list(lands)], *after)
    sems = [(out[2 * g], out[2 * g + 1]) for g in range(ng)]
    return sems, list(out[2 * ng:2 * ng + ns]), list(out[2 * ng + ns:2 * ng + ns + nl]), out[-1]


def _comm_wait(srcs, lands, pieces, members, sems, after, name):
    ns, nl, na = len(srcs), len(lands), len(after)

    def body(*refs):
        s_refs = refs[:ns]
        l_refs = refs[ns:ns + nl]
        send_sems, recv_sems = refs[ns + nl:ns + nl + 2]
        x, y, c = _mesh_pos()
        me = 4 * x + 2 * y + c
        for row, p in enumerate(members):
            for k in range(1, N_DEV):
                pos, peer = _peer(k)
                src, dst = _piece_refs(pieces[p], s_refs, l_refs, me, peer)
                cp = pltpu.make_async_remote_copy(src_ref=src, dst_ref=dst, send_sem=send_sems.at[_sem_index(row, k)],
                                                  recv_sem=recv_sems.at[_sem_index(row, k)], device_id=pos,
                                                  device_id_type=MESH_ID)
                cp.wait_send()
                cp.wait_recv()

    hbm_of = lambda a: pltpu.HBM(a.shape, a.dtype)
    out = pl.pallas_call(
        body,
        in_specs=[HBM] * (ns + nl) + [SEM, SEM] + [ANY] * na,
        out_specs=[HBM] * (ns + nl),
        out_shape=[hbm_of(a) for a in srcs] + [hbm_of(a) for a in lands],
        input_output_aliases={i: i for i in range(ns + nl)},
        compiler_params=pltpu.CompilerParams(has_side_effects=DATAFLOW),
        name=name,
    )(*srcs, *lands, sems[0], sems[1], *after)
    return list(out[ns:])


def _landed_block(piece, lands, owner):
    _, _, li, axis, base, stride, shape = piece
    return _window(lands[li], axis, base + stride * owner, shape[axis])


def _copy_stage(name, bufs, in_sems, out_sem_sizes, emit, after=()):
    nb, ni, no, na = len(bufs), len(in_sems), len(out_sem_sizes), len(after)

    def body(*refs):
        b_refs = refs[:nb]
        i_refs = refs[nb:nb + ni]
        o_refs = refs[nb + ni + na:nb + ni + na + no]
        emit(b_refs, i_refs, o_refs)
        refs[-1][...] = jnp.zeros_like(refs[-1])

    hbm_of = lambda a: pltpu.HBM(a.shape, a.dtype)
    out = pl.pallas_call(
        body,
        in_specs=[HBM] * nb + [SEM] * ni + [ANY] * na,
        out_specs=[SEM] * no + [HBM] * nb + [pl.BlockSpec(memory_space=pltpu.VMEM)],
        out_shape=[pltpu.SemaphoreType.DMA((n,)) for n in out_sem_sizes] + [hbm_of(a) for a in bufs]
        + [jax.ShapeDtypeStruct((8, LANES), F32)],
        input_output_aliases={i: no + i for i in range(nb)},
        compiler_params=pltpu.CompilerParams(has_side_effects=DATAFLOW),
        name=name,
    )(*[pltpu.with_memory_space_constraint(a, pltpu.HBM) for a in bufs], *in_sems, *after)
    return list(out[:no]), list(out[no:no + nb]), out[-1]


def _remote(src, dst, send, recv, to):
    return pltpu.make_async_remote_copy(src_ref=src, dst_ref=dst, send_sem=send, recv_sem=recv, device_id=to,
                                        device_id_type=MESH_ID)


def _routed_gather(srcs, lands, pieces, after, name):
    ns, npc = len(srcs), len(pieces)

    def places():
        x, y, c = _mesh_pos()
        index = lambda p: 4 * p[0] + 2 * p[1] + p[2]
        me, sib = (x, y, c), (x, y, 1 - c)
        xnb, ynb = (1 - x, y, c), (x, 1 - y, c)
        got_first = (x ^ (1 - c), y ^ c, c)
        pass_to = (x ^ c, y ^ (1 - c), c)
        diag = (1 - x, 1 - y, c)
        return index, me, sib, xnb, ynb, got_first, pass_to, diag

    def start(b, _, o):
        index, me, sib, xnb, ynb, *_rest = places()
        send_a, recv_sib, recv_nb = o
        for p, piece in enumerate(pieces):
            src, dst = _piece_refs(piece, b[:ns], b[ns:], index(me), 0)
            _remote(src, dst, send_a.at[3 * p], recv_sib.at[p], sib).start()
            _remote(src, dst, send_a.at[3 * p + 1], recv_nb.at[2 * p], xnb).start()
            _remote(src, dst, send_a.at[3 * p + 2], recv_nb.at[2 * p + 1], ynb).start()

    def pass_a(b, i, o):
        index, me, sib, xnb, ynb, got_first, pass_to, _diag = places()
        (recv_nb,) = i
        send_f, recv_f, send_d, recv_d = o
        for p, piece in enumerate(pieces):
            for j, nb in enumerate((xnb, ynb)):
                blk = _landed_block(piece, b, index(nb))
                _remote(blk, blk, send_f.at[2 * p + j], recv_nb.at[2 * p + j], sib).wait_recv()
                _remote(blk, blk, send_f.at[2 * p + j], recv_f.at[2 * p + j], sib).start()
            blk = _landed_block(piece, b, index(got_first))
            _remote(blk, blk, send_d.at[p], recv_d.at[p], pass_to).start()

    def pass_b(b, i, o):
        index, me, sib, *_mid, diag = places()
        (recv_d,) = i
        send_g, recv_g = o
        for p, piece in enumerate(pieces):
            blk = _landed_block(piece, b, index(diag))
            _remote(blk, blk, send_g.at[p], recv_d.at[p], sib).wait_recv()
            _remote(blk, blk, send_g.at[p], recv_g.at[p], sib).start()

    def last(b, i, _):
        index, me, sib, *_others = places()
        send_a, recv_sib, send_f, recv_f, send_d, send_g, recv_g = i
        for p, piece in enumerate(pieces):
            src, dst = _piece_refs(piece, b[:ns], b[ns:], index(me), 0)
            cp = lambda s_sem, r_sem: _remote(src, dst, s_sem, r_sem, sib)
            cp(send_a.at[3 * p], recv_sib.at[p]).wait_recv()
            cp(send_a.at[3 * p], recv_g.at[p]).wait_recv()
            for j in range(3):
                cp(send_a.at[3 * p + j], recv_sib.at[p]).wait_send()
            for j in range(2):
                cp(send_f.at[2 * p + j], recv_f.at[2 * p + j]).wait_recv()
                cp(send_f.at[2 * p + j], recv_f.at[2 * p + j]).wait_send()
            cp(send_d.at[p], recv_sib.at[p]).wait_send()
            cp(send_g.at[p], recv_sib.at[p]).wait_send()

    (send_a, recv_sib, recv_nb), bufs, _ = _copy_stage(name + "_start", list(srcs) + list(lands), [],
                                                       [3 * npc, npc, 2 * npc], start)
    srcs, lands = bufs[:ns], bufs[ns:]
    (send_f, recv_f, send_d, recv_d), lands, _ = _copy_stage(name + "_pass_a", lands, [recv_nb],
                                                             [2 * npc, 2 * npc, npc, npc],
                                                             lambda b, i, o: pass_a(b, i, o), after=after)
    (send_g, recv_g), lands, tok = _copy_stage(name + "_pass_b", lands, [recv_d], [npc, npc], pass_b)
    _, bufs, _ = _copy_stage(name + "_last", list(srcs) + list(lands),
                             [send_a, recv_sib, send_f, recv_f, send_d, send_g, recv_g], [], last)
    return bufs[ns:], tok


_SMALL_NAMES = ("ln1_g", "ln1_b", "hg_lb_logits", "hg_norm_g", "sg_ln_g", "sg_ln_b", "sg_w_s", "sg_b_s",
                "ln2_g", "ln2_b", "mem_ln_g", "mem_ln_b", "ln3_g", "ln3_b", "ln4_g", "ln4_b")


_VEC_NAMES = ("ln1_g", "ln1_b", "ln2_g", "ln2_b", "mem_ln_g", "mem_ln_b", "ln3_g", "ln3_b", "ln4_g", "ln4_b")
_ROW_NAMES = ("hg_lb_logits", "hg_norm_g", "sg_ln_g", "sg_ln_b", "sg_b_s", "sg_w_s")
VEC_ROWS = 16


def _row_plan(shapes):
    plan, pos = {}, 0
    for k in _ROW_NAMES:
        shp = shapes[k]
        slabs, off = [], pos
        for idx in itertools.product(*[range(dim) for dim in shp[:-2]]):
            slabs.append((idx, off, shp[-2]))
            off += shp[-2]
        plan[k] = (pos, slabs)
        pos = -(-off // 8) * 8
    return plan, -(-pos // 16) * 16


def _pack_small_grads(gs, shapes, loss):
    d = gs[_VEC_NAMES[0]].size
    vec = jnp.concatenate([gs[k].reshape(1, -1) for k in _VEC_NAMES] + [jnp.tile(loss, (1, d // LANES))], axis=0)
    vec = jnp.pad(vec, ((0, VEC_ROWS - vec.shape[0]), (0, 0)))
    plan, total = _row_plan(shapes)
    parts, pos = [], 0
    for k in _ROW_NAMES:
        first, slabs = plan[k]
        rows = gs[k].reshape(-1, LANES)
        end = slabs[-1][1] + slabs[-1][2]
        nxt = -(-end // 8) * 8
        parts.append(jnp.pad(rows, ((0, nxt - first - rows.shape[0]), (0, 0))))
        pos = nxt
    parts.append(jnp.zeros((total - pos, LANES), F32))
    return vec, jnp.concatenate(parts, axis=0)


def _adam_small(land_vec, land_rows, w, m, v):
    names = _VEC_NAMES + _ROW_NAMES
    n = len(names)
    shapes = {k: w[k].shape for k in names}
    plan, _ = _row_plan(shapes)

    def body(*refs):
        lv_ref, lr_ref = refs[:2]
        w_refs, m_refs, v_refs = refs[2:2 + n], refs[2 + n:2 + 2 * n], refs[2 + 2 * n:2 + 3 * n]
        outs = refs[2 + 3 * n:2 + 7 * n]
        loss_ref = refs[2 + 7 * n]
        gv_s, gr_s = refs[3 + 7 * n:]
        gv_s[...] = _slot_sum(lv_ref)
        gr_s[...] = _slot_sum(lr_ref)
        loss_ref[...] = gv_s[len(_VEC_NAMES):len(_VEC_NAMES) + 1, :LANES]
        for p, k in enumerate(names):
            if k in _VEC_NAMES:
                row = _VEC_NAMES.index(k)
                slabs = [((), None, None)]
            else:
                slabs = plan[k][1]
            for idx, off, rows in slabs:
                g = gv_s[row:row + 1, :] if off is None else gr_s[off:off + rows, :]
                sel = idx + (slice(None), slice(None))
                delta, nm, nv = _adamw(w_refs[p][sel], g, m_refs[p][sel], v_refs[p][sel])
                for o, val in zip(range(4), (g, delta, nm, nv)):
                    outs[o * n + p][sel] = val

    flat = lambda tree: [tree[k] for k in names]
    shp = [jax.ShapeDtypeStruct(shapes[k], F32) for k in names]
    out = pl.pallas_call(
        body,
        out_shape=shp * 4 + [jax.ShapeDtypeStruct((1, LANES), F32)],
        scratch_shapes=[pltpu.VMEM(land_vec.shape[1:], F32), pltpu.VMEM(land_rows.shape[1:], F32)],
        name="adam_small",
    )(land_vec, land_rows, *flat(w), *flat(m), *flat(v))
    return [dict(zip(names, out[o * n:(o + 1) * n])) for o in range(4)], out[4 * n]


_COL_FFN = ("ffn1_w_gate", "ffn1_w_up", "ffn2_w_gate", "ffn2_w_up")
_ROW_FFN = ("ffn1_w_down", "ffn2_w_down")
_ROW_SQ = ("w_out", "xa_w_q", "xa_w_k", "xa_w_v", "xa_w_o")
_BIG_NAMES = ("ffn1_w_gate", "ffn1_w_up", "ffn1_w_down", "w_in", "w_out", "xa_w_q", "xa_w_k", "xa_w_v", "xa_w_o",
              "ffn2_w_gate", "ffn2_w_up", "ffn2_w_down")


def _ffn_split(fs):
    main = (fs // MXU_WIDTH_V7X) * MXU_WIDTH_V7X
    tail = fs - main
    tail_pad = -(-tail // LANES) * LANES
    assert main > 0 and tail > 0
    return main, tail, tail_pad


def _layout(name, shard_shape):
    r, c = shard_shape
    if name in _COL_FFN:
        main, tail, pad = _ffn_split(c)
        return (r, N_DEV * (main + pad)), [(1, 0, main, (r, main), (0, main)),
                                           (1, N_DEV * main, pad, (r, pad), (main, c))]
    if name in _ROW_FFN:
        main, tail, pad = _ffn_split(r)
        return (N_DEV * (main + pad), c), [(0, 0, main, (main, c), (0, main)),
                                           (0, N_DEV * main, pad, (pad, c), (main, r))]
    if name == "w_in":
        return (r, N_DEV * c), [(1, 0, c, (r, c), (0, c))]
    return (N_DEV * r, c), [(0, 0, r, (r, c), (0, r))]


def _shard_pieces(name, shard):
    out = []
    for axis, _, _, shape, (lo, hi) in _layout(name, shard.shape)[1]:
        part = shard[lo:hi, :] if axis == 0 else shard[:, lo:hi]
        pad = [(0, shape[0] - part.shape[0]), (0, shape[1] - part.shape[1])]
        out.append(jnp.pad(part, pad).astype(BF16))
    return out


def _gather_plan(names, shards):
    srcs, land_shapes, pieces, index = [], [], [], {}
    for li, name in enumerate(names):
        shape2d, parts = _layout(name, shards[name].shape)
        land_shapes.append(jax.ShapeDtypeStruct(shape2d, BF16))
        index[name] = []
        for (axis, base, stride, shape, _), src in zip(parts, _shard_pieces(name, shards[name])):
            index[name].append(len(pieces))
            pieces.append(("gather", len(srcs), li, axis, base, stride, shape))
            srcs.append(src)
    return srcs, land_shapes, pieces, index


def _scatter_plan(names, grads, shard_shapes):
    srcs, land_shapes, pieces, index = [], [], [], {}
    for si, name in enumerate(names):
        _, parts = _layout(name, shard_shapes[name])
        srcs.append(grads[name])
        index[name] = []
        for axis, base, stride, shape, _ in parts:
            index[name].append(len(land_shapes))
            pieces.append(("scatter", si, len(land_shapes), axis, base, stride, shape))
            land_shapes.append(jax.ShapeDtypeStruct((N_DEV,) + shape, grads[name].dtype))
    return srcs, land_shapes, pieces, index


def _small_views(small):
    row = lambda a: a.reshape(1, -1)
    ln = {k: row(small[k]) for k in ("ln1_g", "ln1_b", "ln2_g", "ln2_b", "ln3_g", "ln3_b", "ln4_g", "ln4_b",
                                      "mem_ln_g", "mem_ln_b", "hg_norm_g")}
    sg_w = small["sg_w_s"].reshape(SG_GROUPS, SG_CHUNK, SG_CHUNK)
    sg = dict(logits=jnp.swapaxes(small["hg_lb_logits"], 0, 1),
              g=small["sg_ln_g"].reshape(SG_GROUPS, 1, SG_DIM), b=small["sg_ln_b"].reshape(SG_GROUPS, 1, SG_DIM),
              w=sg_w, wt=jnp.swapaxes(sg_w, 1, 2), bs=small["sg_b_s"].reshape(SG_GROUPS, SG_CHUNK, 1))
    return ln, sg


def _forward(x, mem, target, get_w, small, first_deps=()):
    ln, sg = _small_views(small)
    xb = x.astype(BF16)
    a1, b1, s1 = _ffn_up(xb, get_w("ffn1_w_gate", ()), get_w("ffn1_w_up", ()), "ffn1_up", deps=first_deps)
    h1b, xh1, rs1 = _mm_res_ln(s1, get_w("ffn1_w_down", (s1,)), x, ln["ln1_g"], ln["ln1_b"], 0.5, "ffn1_down_ln")
    proj = _mm_nn(h1b, get_w("w_in", (h1b,)), "mix_in")
    oraw, mix, states = _hgrn_fwd(proj, sg["logits"], ln["hg_norm_g"])
    mix = _sgu_fwd(proj, mix, sg["g"], sg["b"], sg["w"], sg["bs"])
    h2b, xh2, rs2 = _mm_res_ln(mix, get_w("w_out", (mix,)), (xh1, ln["ln1_g"], ln["ln1_b"]), ln["ln2_g"], ln["ln2_b"],
                               1.0, "mix_out_ln")
    mb, mxh, mrs, kb, vb = _mem_kv(mem, ln["mem_ln_g"], ln["mem_ln_b"], get_w("xa_w_k", (h2b,)), get_w("xa_w_v", (h2b,)))
    qb, att = _attn_fwd(h2b, get_w("xa_w_q", (kb,)), kb, vb)
    h3b, xh3, rs3 = _mm_res_ln(att, get_w("xa_w_o", (att,)), (xh2, ln["ln2_g"], ln["ln2_b"]), ln["ln3_g"], ln["ln3_b"],
                               1.0, "attn_out_ln")
    a2, b2, s2 = _ffn_up(h3b, get_w("ffn2_w_gate", (h3b,)), get_w("ffn2_w_up", (h3b,)), "ffn2_up")
    loss, dy4, dy4b, dg4, db4 = _mm_res_ln(s2, get_w("ffn2_w_down", (s2,)), (xh3, ln["ln3_g"], ln["ln3_b"]),
                                           ln["ln4_g"], ln["ln4_b"], 0.5, "ffn2_down_ln_loss", target=target)
    return dict(xb=xb, a1=a1, b1=b1, s1=s1, h1b=h1b, xh1=xh1, rs1=rs1, proj=proj, oraw=oraw, mix=mix, states=states,
                h2b=h2b, xh2=xh2, rs2=rs2, mb=mb, mxh=mxh, mrs=mrs, kb=kb, vb=vb, qb=qb, att=att, h3b=h3b, xh3=xh3,
                rs3=rs3, a2=a2, b2=b2, s2=s2, loss=loss, dy4=dy4, dy4b=dy4b, dg4=dg4, db4=db4)


def _backward(sv, wt, small, send):
    ln, sg = _small_views(small)
    gs = {"ln4_g": sv["dg4"], "ln4_b": sv["db4"]}
    loss, dy4, dy4b = sv["loss"], sv["dy4"], sv["dy4b"]
    g_down2 = _mm_tn(sv["s2"], dy4b, "g_ffn2_down", scale=0.5)
    da2, db2, dy3, dy3b, gs["ln3_g"], gs["ln3_b"] = _ffn_bwd_fused(
        dy4, dy4b, wt["ffn2_w_down"], wt["ffn2_w_gate"], wt["ffn2_w_up"], sv["a2"], sv["b2"], 0.5,
        (sv["xh3"], sv["rs3"], ln["ln3_g"]), "ffn2_bwd")
    g_gate2 = _mm_tn(sv["h3b"], da2, "g_ffn2_gate")
    g_up2 = _mm_tn(sv["h3b"], db2, "g_ffn2_up")
    tok = send(("ffn2_w_down", "ffn2_w_gate", "ffn2_w_up"), (g_down2, g_gate2, g_up2))

    g_o = _mm_tn(sv["att"], dy3b, "g_xa_o", deps=(tok,))
    dqb, dk, dv = _attn_bwd(dy3b, wt["xa_w_o"], sv["qb"], sv["kb"], sv["vb"])
    g_q = _mm_tn(sv["h2b"], dqb, "g_xa_q")
    g_k, g_v, gs["mem_ln_g"], gs["mem_ln_b"] = _mem_bwd(dk, dv, sv["mb"], sv["mxh"], sv["mrs"], ln["mem_ln_g"],
                                                        wt["xa_w_k"], wt["xa_w_v"])
    tok = send(("xa_w_o", "xa_w_q", "xa_w_k", "xa_w_v"), (g_o, g_q, g_k, g_v))
    dy2, dy2b, gs["ln2_g"], gs["ln2_b"] = _dx_ln(dy3, [(dqb, wt["xa_w_q"])], (sv["xh2"], sv["rs2"], ln["ln2_g"]),
                                                 "attn_dx_ln", deps=(tok,))

    g_out = _mm_tn(sv["mix"], dy2b, "g_w_out")
    dmix = _mm_nt(dy2b, wt["w_out"], "mix_out_bwd")
    dq, dfz, div, dgg, dlg, dgn = _hgrn_bwd(sv["proj"], sv["oraw"], dmix, sv["states"], sg["logits"], ln["hg_norm_g"])
    du, dvv, gs["sg_ln_g"], gs["sg_ln_b"], gs["sg_w_s"], gs["sg_b_s"] = _sgu_bwd(
        sv["proj"], dmix, sg["g"], sg["b"], sg["w"], sg["wt"], sg["bs"])
    gs["hg_lb_logits"] = jnp.swapaxes(dlg, 0, 1)
    gs["hg_norm_g"] = jnp.sum(dgn, axis=0)
    dproj = jnp.concatenate([dq, dfz, div, dgg, du, dvv], axis=1)
    g_in = _mm_tn(sv["h1b"], dproj, "g_w_in")
    tok = send(("w_out", "w_in"), (g_out, g_in))
    dy1, dy1b, gs["ln1_g"], gs["ln1_b"] = _dx_ln(dy2, [(dproj, wt["w_in"])], (sv["xh1"], sv["rs1"], ln["ln1_g"]),
                                                 "mix_dx_ln", deps=(tok,))

    g_down1 = _mm_tn(sv["s1"], dy1b, "g_ffn1_down", scale=0.5)
    tok = send(("ffn1_w_down",), (g_down1,))
    da1, db1 = _ffn_bwd_act(dy1b, wt["ffn1_w_down"], sv["a1"], sv["b1"], 0.5, "ffn1_bwd_act", deps=(tok,))
    g_gate1 = _mm_tn(sv["xb"], da1, "g_ffn1_gate")
    tok = send(("ffn1_w_gate",), (g_gate1,))
    g_up1 = _mm_tn(sv["xb"], db1, "g_ffn1_up", deps=(tok,))
    tok = send(("ffn1_w_up",), (g_up1,))
    grad_x = _dx_ln(dy1, [(da1, wt["ffn1_w_gate"]), (db1, wt["ffn1_w_up"])], None, "ffn1_dx", deps=(tok,))
    return loss, grad_x, gs


_WEIGHT_NAMES = ("ffn1_w_gate", "ffn1_w_up", "ffn1_w_down", "ln1_g", "ln1_b", "w_in", "hg_lb_logits", "hg_norm_g",
                 "sg_ln_g", "sg_ln_b", "sg_w_s", "sg_b_s", "w_out", "ln2_g", "ln2_b", "mem_ln_g", "mem_ln_b",
                 "xa_w_q", "xa_w_k", "xa_w_v", "xa_w_o", "ln3_g", "ln3_b", "ffn2_w_gate", "ffn2_w_up", "ffn2_w_down",
                 "ln4_g", "ln4_b")
_FIRST = ("ffn1_w_gate", "ffn1_w_up")
_SECOND = ("ffn1_w_down", "w_in", "w_out")
_THIRD = ("xa_w_k", "xa_w_v", "xa_w_q", "xa_w_o", "ffn2_w_gate", "ffn2_w_up", "ffn2_w_down")


def kernel(x, mem, ffn1_w_gate, ffn1_w_up, ffn1_w_down, ln1_g, ln1_b, w_in, hg_lb_logits, hg_norm_g, sg_ln_g, sg_ln_b, sg_w_s, sg_b_s, w_out, ln2_g, ln2_b, mem_ln_g, mem_ln_b, xa_w_q, xa_w_k, xa_w_v, xa_w_o, ln3_g, ln3_b, ffn2_w_gate, ffn2_w_up, ffn2_w_down, ln4_g, ln4_b, loss_target, m_ffn1_w_gate, m_ffn1_w_up, m_ffn1_w_down, m_ln1_g, m_ln1_b, m_w_in, m_hg_lb_logits, m_hg_norm_g, m_sg_ln_g, m_sg_ln_b, m_sg_w_s, m_sg_b_s, m_w_out, m_ln2_g, m_ln2_b, m_mem_ln_g, m_mem_ln_b, m_xa_w_q, m_xa_w_k, m_xa_w_v, m_xa_w_o, m_ln3_g, m_ln3_b, m_ffn2_w_gate, m_ffn2_w_up, m_ffn2_w_down, m_ln4_g, m_ln4_b, v_ffn1_w_gate, v_ffn1_w_up, v_ffn1_w_down, v_ln1_g, v_ln1_b, v_w_in, v_hg_lb_logits, v_hg_norm_g, v_sg_ln_g, v_sg_ln_b, v_sg_w_s, v_sg_b_s, v_w_out, v_ln2_g, v_ln2_b, v_mem_ln_g, v_mem_ln_b, v_xa_w_q, v_xa_w_k, v_xa_w_v, v_xa_w_o, v_ln3_g, v_ln3_b, v_ffn2_w_gate, v_ffn2_w_up, v_ffn2_w_down, v_ln4_g, v_ln4_b):
    args = dict(locals())
    w = {k: args[k] for k in _WEIGHT_NAMES}
    m = {k: args["m_" + k] for k in _WEIGHT_NAMES}
    v = {k: args["v_" + k] for k in _WEIGHT_NAMES}
    shards = {k: w[k][0] for k in _BIG_NAMES}
    shard_shapes = {k: shards[k].shape for k in _BIG_NAMES}
    small = {k: (w[k][0] if k != "hg_lb_logits" else w[k]) for k in _SMALL_NAMES}

    srcs1, shapes1, pieces1, idx1 = _gather_plan(_FIRST, shards)
    lands1 = _place_own(srcs1, shapes1, pieces1, "gather_first_own")
    rest = _SECOND + _THIRD
    srcs2, shapes2, pieces2, idx2 = _gather_plan(rest, shards)
    lands2 = _place_own(srcs2, shapes2, pieces2, "gather_rest_own")
    groups2 = [list(idx2[k]) for k in rest]
    lands1, tok1 = _routed_gather(srcs1, lands1, pieces1, tuple(lands2), "gather_first")
    sems2, srcs2, lands2, tok2 = _comm_start(srcs2, lands2, pieces2, groups2, "gather_rest_start", after=(tok1,))
    wt = dict(zip(_FIRST, lands1))
    pending = {k: gi for gi, k in enumerate(rest)}

    def get_w(name, after):
        if name in pending:
            gi = pending.pop(name)
            si = [pieces2[p][1] for p in groups2[gi]]
            sub = [(pieces2[p][0], row, 0) + pieces2[p][3:] for row, p in enumerate(groups2[gi])]
            wt[name] = _comm_wait([srcs2[s] for s in si], [lands2[gi]], sub, list(range(len(sub))), sems2[gi],
                                  after, "gather_wait_" + name)[0]
        return wt[name]

    sv = _forward(x[0], mem[0], loss_target[0], get_w, small, first_deps=(tok2,))

    sent = []

    def send(names, grads):
        srcs, shapes, pieces, idx = _scatter_plan(names, dict(zip(names, grads)), shard_shapes)
        lands = _place_own(srcs, shapes, pieces, "grads_own_%d" % len(sent))
        sems, srcs, lands, tok = _comm_start(srcs, lands, pieces, [list(range(len(pieces)))],
                                             "grads_start_%d" % len(sent))
        sent.append((names, srcs, lands, pieces, idx, sems[0]))
        return tok

    loss, grad_x, gs = _backward(sv, wt, small, send)

    ssrc = list(_pack_small_grads(gs, {k: w[k].shape for k in _SMALL_NAMES}, loss))
    sp = [("scatter", i, i, 0, 0, 0, a.shape) for i, a in enumerate(ssrc)]
    sshape = [jax.ShapeDtypeStruct((N_DEV,) + a.shape, F32) for a in ssrc]
    sl = _place_own(ssrc, sshape, sp, "small_own")
    ssem, ssrc, sl, _ = _comm_start(ssrc, sl, sp, [[0, 1]], "small_start")

    out_g, out_d, out_m, out_v = {}, {}, {}, {}
    after = (grad_x,)
    for n_sent, (names, srcs, lands, pieces, idx, sems) in enumerate(sent):
        lands = _comm_wait(srcs, lands, pieces, list(range(len(pieces))), sems, after, "grads_wait_%d" % n_sent)
        for k in names:
            axis = 1 if (k in _COL_FFN or k == "w_in") else 0
            if k in _COL_FFN:
                res = _adam_sharded([lands[i] for i in idx[k]], w[k][0].T, m[k][0].T, v[k][0].T, axis, "adam_" + k)
                res = [r.T for r in res]
            else:
                res = _adam_sharded([lands[i] for i in idx[k]], w[k][0], m[k][0], v[k][0], axis, "adam_" + k)
            out_g[k], out_d[k], out_m[k], out_v[k] = [r[None] for r in res]
        after = (out_v[names[-1]],)
    sl = _comm_wait(ssrc, sl, sp, [0, 1], ssem[0], after, "small_wait")
    small_out, loss_sum = _adam_small(sl[0], sl[1], w, m, v)
    for dst, res in zip((out_g, out_d, out_m, out_v), small_out):
        dst.update(res)
    loss_all = loss_sum[0, 0]
    return (loss_all, grad_x[None], *[out_g[k] for k in _WEIGHT_NAMES], *[out_d[k] for k in _WEIGHT_NAMES],
            *[out_m[k] for k in _WEIGHT_NAMES], *[out_v[k] for k in _WEIGHT_NAMES])
```

```python
import itertools

import jax
import jax.numpy as jnp
import numpy as np
from jax import lax
from jax.experimental import pallas as pl
from jax.experimental.pallas import tpu as pltpu

F32 = jnp.float32
BF16 = jnp.bfloat16

N_DEV = 8
ALPHA = 2.0 ** 0.25
LN_EPS = 1e-5
HG_HEADS = 4
HG_DIM = 128
SG_GROUPS = 4
SG_DIM = 128
SG_CHUNK = 128
X_HEADS = 4
HG_BLOCK = 16
HG_UNROLL = 16
ADAM_LR = 0.001
ADAM_B1 = 0.9
ADAM_B2 = 0.999
ADAM_EPS = 1e-08
ADAM_WD = 0.01
ADAM_STEP = 10
VMEM_LIMIT_V7X = 48 * 1024 * 1024
MXU_WIDTH_V7X = 256
LANES = 128
MESH_ID = pl.DeviceIdType.MESH
ANY = pl.BlockSpec(memory_space=pl.ANY)
HBM = pl.BlockSpec(memory_space=pltpu.HBM)
SEM = pl.BlockSpec(memory_space=pltpu.SEMAPHORE)
DATAFLOW = pltpu.SideEffectType.DATAFLOW_SIDE_EFFECTING


def _params(n_axes):
    return pltpu.CompilerParams(dimension_semantics=("arbitrary",) * n_axes, vmem_limit_bytes=VMEM_LIMIT_V7X)


def _dot(a, b):
    return jnp.dot(a, b, preferred_element_type=F32)


def _dot_nt(a, b):
    return lax.dot_general(a, b, (((1,), (1,)), ((), ())), preferred_element_type=F32)


def _dot_tn(a, b):
    return lax.dot_general(a, b, (((0,), (0,)), ((), ())), preferred_element_type=F32)


def _sigmoid(x):
    return 1.0 / (1.0 + jnp.exp(-x))


def _silu_and_grad(a):
    sig = _sigmoid(a)
    return a * sig, sig * (1.0 + a * (1.0 - sig))


_GELU_C = 0.7978845608028654


def _gelu_and_grad(x):
    inner = _GELU_C * (x + 0.044715 * x * x * x)
    t = jnp.tanh(inner)
    val = 0.5 * x * (1.0 + t)
    grad = 0.5 * (1.0 + t) + 0.5 * x * (1.0 - t * t) * _GELU_C * (1.0 + 3.0 * 0.044715 * x * x)
    return val, grad


def _ln_fwd(y, g, b):
    mu = jnp.mean(y, axis=-1, keepdims=True)
    yc = y - mu
    var = jnp.mean(yc * yc, axis=-1, keepdims=True)
    rstd = lax.rsqrt(var + LN_EPS)
    xhat = yc * rstd
    return xhat * g + b, xhat, rstd


def _ln_bwd(dh, xhat, rstd, g):
    dxh = dh * g
    m1 = jnp.mean(dxh, axis=-1, keepdims=True)
    m2 = jnp.mean(dxh * xhat, axis=-1, keepdims=True)
    dy = rstd * (dxh - m1 - xhat * m2)
    dg = jnp.sum(dh * xhat, axis=0, keepdims=True)
    db = jnp.sum(dh, axis=0, keepdims=True)
    return dy, dg, db


def _mask_dot(mask, x):
    hi = x.astype(BF16)
    lo = (x - hi.astype(F32)).astype(BF16)
    n = mask.shape[0]
    parts = [_dot(mask, hi[r:r + n, :]) + _dot(mask, lo[r:r + n, :]) for r in range(0, x.shape[0], n)]
    return parts[0] if len(parts) == 1 else jnp.concatenate(parts, axis=0)


def _block_masks(n):
    r = np.arange(n)[:, None]
    c = np.arange(n)[None, :]
    same = (r // HG_BLOCK) == (c // HG_BLOCK)
    return jnp.asarray(np.stack([same & (c <= r), same & (c >= r), same]), BF16)


def _row_tile(t):
    return min(t, 512)


def _col_tile(n):
    for cand in (512, 256, 128):
        if n % cand == 0:
            return cand
    return n


def _resident(w):
    return pl.BlockSpec(w.shape, lambda *_: (0, 0), pipeline_mode=pl.Buffered(1))


def _drop_deps(body, n_in, n_deps):
    if n_deps == 0:
        return body
    return lambda *refs: body(*refs[:n_in], *refs[n_in + n_deps:])


def _ffn_up(hb, wg, wu, name, deps=()):
    t, d = hb.shape
    f = wg.shape[1]
    tm = _row_tile(t)
    tn = _col_tile(f)

    def body(h_ref, wg_ref, wu_ref, a_ref, b_ref, s_ref):
        h = h_ref[...]
        for c in range(f // tn):
            cols = slice(c * tn, (c + 1) * tn)
            a = _dot(h, wg_ref[:, cols])
            b = _dot(h, wu_ref[:, cols])
            a_ref[:, cols] = a.astype(BF16)
            b_ref[:, cols] = b.astype(BF16)
            s_ref[:, cols] = (a * _sigmoid(a) * b).astype(BF16)

    act = pl.BlockSpec((tm, f), lambda i: (i, 0))
    return pl.pallas_call(
        _drop_deps(body, 3, len(deps)),
        grid=(t // tm,),
        in_specs=[pl.BlockSpec((tm, d), lambda i: (i, 0)), _resident(wg), _resident(wu)] + [ANY] * len(deps),
        out_specs=[act, act, act],
        out_shape=[jax.ShapeDtypeStruct((t, f), BF16)] * 3,
        compiler_params=_params(1),
        name=name,
    )(hb, wg, wu, *deps)


def _mm_res_ln(lhs, w, res, g, b, coef, name, target=None):
    t, kd = lhs.shape
    d = w.shape[1]
    tm = _row_tile(t)
    nt = t // tm
    from_norm = isinstance(res, tuple)
    n_res = 3 if from_norm else 1

    def body(*refs):
        l_ref, w_ref = refs[:2]
        r_refs = refs[2:2 + n_res]
        g_ref, b_ref = refs[2 + n_res:4 + n_res]
        rest = refs[4 + n_res:]
        prev = r_refs[0][...] * r_refs[1][...] + r_refs[2][...] if from_norm else r_refs[0][...]
        y = ALPHA * prev + coef * _dot(l_ref[...], w_ref[...])
        h, xhat, rstd = _ln_fwd(y, g_ref[...], b_ref[...])
        if target is None:
            hb_ref, xh_ref, rs_ref = rest
            hb_ref[...] = h.astype(BF16)
            xh_ref[...] = xhat
            rs_ref[...] = rstd
            return
        t_ref, loss_ref, dy_ref, dyb_ref, dg_ref, db_ref, lacc = rest
        i = pl.program_id(0)

        @pl.when(i == 0)
        def _():
            lacc[...] = jnp.zeros_like(lacc)
            dg_ref[...] = jnp.zeros_like(dg_ref)
            db_ref[...] = jnp.zeros_like(db_ref)

        err = h - t_ref[...]
        lacc[...] += jnp.sum(err * err, axis=0, keepdims=True)
        dy, dg, db = _ln_bwd(err * (1.0 / d), xhat, rstd, g_ref[...])
        dy_ref[...] = dy
        dyb_ref[...] = dy.astype(BF16)
        dg_ref[...] += dg
        db_ref[...] += db

        @pl.when(i == nt - 1)
        def _():
            loss_ref[...] = jnp.zeros_like(loss_ref) + jnp.sum(lacc[...], axis=1, keepdims=True) * (0.5 / d)

    row = pl.BlockSpec((tm, d), lambda i: (i, 0))
    vec = pl.BlockSpec((1, d), lambda i: (0, 0))
    res_specs = [row, vec, vec] if from_norm else [row]
    res_args = list(res) if from_norm else [res]
    in_specs = [pl.BlockSpec((tm, kd), lambda i: (i, 0)), _resident(w)] + res_specs + [vec, vec]
    args = [lhs, w] + res_args + [g, b]
    if target is None:
        out_specs = [row, row, pl.BlockSpec((tm, 1), lambda i: (i, 0))]
        out_shape = [jax.ShapeDtypeStruct((t, d), BF16), jax.ShapeDtypeStruct((t, d), F32),
                     jax.ShapeDtypeStruct((t, 1), F32)]
        scratch = []
    else:
        in_specs.append(row)
        args.append(target)
        out_specs = [pl.BlockSpec((1, LANES), lambda i: (0, 0)), row, row, vec, vec]
        out_shape = [jax.ShapeDtypeStruct((1, LANES), F32), jax.ShapeDtypeStruct((t, d), F32),
                     jax.ShapeDtypeStruct((t, d), BF16), jax.ShapeDtypeStruct((1, d), F32),
                     jax.ShapeDtypeStruct((1, d), F32)]
        scratch = [pltpu.VMEM((1, d), F32)]
    return pl.pallas_call(
        body,
        grid=(nt,),
        in_specs=in_specs,
        out_specs=out_specs,
        out_shape=out_shape,
        scratch_shapes=scratch,
        compiler_params=_params(1),
        name=name,
    )(*args)


def _store_slabs(o_ref, first, tile):
    for s in range(tile.shape[1] // LANES):
        o_ref[first + s] = tile[:, s * LANES:(s + 1) * LANES]


def _mm_nn(lhs, w, name):
    t, kd = lhs.shape
    n = w.shape[1]
    tm = _row_tile(t)
    tn = _col_tile(n)

    def body(l_ref, w_ref, o_ref):
        lhs_v = l_ref[...]
        for c in range(n // tn):
            _store_slabs(o_ref, c * (tn // LANES), _dot(lhs_v, w_ref[:, c * tn:(c + 1) * tn]))

    return pl.pallas_call(
        body,
        grid=(t // tm,),
        in_specs=[pl.BlockSpec((tm, kd), lambda i: (i, 0)), _resident(w)],
        out_specs=pl.BlockSpec((n // LANES, tm, LANES), lambda i: (0, i, 0)),
        out_shape=jax.ShapeDtypeStruct((n // LANES, t, LANES), F32),
        compiler_params=_params(1),
        name=name,
    )(lhs, w)


def _lower_bound(lg):
    m = jnp.max(lg, axis=0, keepdims=True)
    e = jnp.exp(lg - m)
    return e[0:1, :] / jnp.sum(e, axis=0, keepdims=True)


def _forget_terms(fz, lb):
    e = jnp.exp(-jnp.abs(fz))
    r = 1.0 / (1.0 + e)
    pos = fz >= 0.0
    sig = jnp.where(pos, r, e * r)
    nsig = jnp.where(pos, e * r, r)
    f = lb + (1.0 - lb) * sig
    k = (1.0 - lb) * nsig
    return sig, nsig, f, k


def _hg_tile(t):
    return min(t, 1024)


HG_HALF = HG_BLOCK // 2
NEG_BIG = -1e30


def _halves(a):
    return a[:HG_HALF, :], a[HG_HALF:, :]


def _causal_halves(s):
    return (0, 1) if s < HG_HALF else (1,)


def _decay_from(b_half, b_s, s, h, tidx):
    first = s - h * HG_HALF
    diff = b_half - b_s
    if first > 0:
        diff = jnp.where(tidx >= first, diff, NEG_BIG)
    return jnp.exp(diff)


def _hgrn_fwd(proj, logits, gn):
    t = proj.shape[1]
    ct = _hg_tile(t)
    nct = t // ct
    nblk = ct // HG_BLOCK
    nh = HG_HEADS
    mrows = min(ct, 256)

    def body(q_ref, fz_ref, iv_ref, gg_ref, lg_ref, gn_ref, mask_ref, oraw_ref, oa_ref, st_ref,
             state, qt_s, kt_s, k_s, b_s, dec_s):
        c = pl.program_id(1)

        @pl.when(c == 0)
        def _():
            state[...] = jnp.zeros_like(state)

        lb = _lower_bound(lg_ref[...])
        q = q_ref[...]
        _, _, f, k = _forget_terms(fz_ref[...], lb)
        logf = jnp.log(f)
        b = _mask_dot(mask_ref[0], logf)
        bend = _mask_dot(mask_ref[2], logf)
        qt_s[...] = (q * jnp.exp(b)).astype(BF16)
        kt_s[...] = (k * jnp.exp(bend - b)).astype(BF16)
        k_s[...] = k
        b_s[...] = b
        dec_s[...] = jnp.exp(bend)
        tidx = lax.broadcasted_iota(jnp.int32, (HG_HALF, HG_DIM), 0)

        def blk(i, carry):
            r0 = pl.multiple_of(i * HG_BLOCK, HG_BLOCK)
            rows = pl.ds(r0, HG_BLOCK)
            st = state[...]
            stb = st.astype(BF16)
            st_ref[i] = stb
            v = iv_ref[rows, :]
            qq = q_ref[rows, :]
            kk = k_s[rows, :]
            bb = b_s[rows, :]
            o = list(_halves(_dot_nt(qt_s[rows, :], stb)))
            qh, bh = _halves(qq), _halves(bb)
            for s in range(HG_BLOCK):
                ks, vs = kk[s:s + 1, :], v[s:s + 1, :]
                for h in _causal_halves(s):
                    e = _decay_from(bh[h], bb[s:s + 1, :], s, h, tidx)
                    acol = jnp.sum(qh[h] * (ks * e), axis=1, keepdims=True)
                    o[h] = o[h] + acol * vs
            oraw_ref[rows, :] = jnp.concatenate(o, axis=0)
            state[...] = st * dec_s[pl.ds(r0, 1), :] + _dot_tn(v.astype(BF16), kt_s[rows, :])
            return carry

        lax.fori_loop(0, nblk, blk, 0, unroll=HG_UNROLL)
        oraw = oraw_ref[...]
        r = lax.rsqrt(jnp.mean(oraw * oraw, axis=-1, keepdims=True) + LN_EPS)
        gg = gg_ref[...]
        oa_ref[...] = (oraw * r * gn_ref[...] * gg * _sigmoid(gg)).astype(BF16)

    def slab(off):
        return pl.BlockSpec((None, ct, HG_DIM), lambda h, c: (off + h, c, 0))

    return pl.pallas_call(
        body,
        grid=(nh, nct),
        in_specs=[slab(0), slab(nh), slab(2 * nh), slab(3 * nh),
                  pl.BlockSpec((None, 2, HG_DIM), lambda h, c: (h, 0, 0)),
                  pl.BlockSpec((1, HG_DIM), lambda h, c: (0, 0)),
                  pl.BlockSpec((3, mrows, mrows), lambda h, c: (0, 0, 0))],
        out_specs=[slab(0), pl.BlockSpec((ct, HG_DIM), lambda h, c: (c, h)),
                   pl.BlockSpec((None, nblk, HG_DIM, HG_DIM), lambda h, c: (h, c, 0, 0))],
        out_shape=[jax.ShapeDtypeStruct((nh, t, HG_DIM), F32),
                   jax.ShapeDtypeStruct((t, (nh + SG_GROUPS) * HG_DIM), BF16),
                   jax.ShapeDtypeStruct((nh, t // HG_BLOCK, HG_DIM, HG_DIM), BF16)],
        scratch_shapes=[pltpu.VMEM((HG_DIM, HG_DIM), F32), pltpu.VMEM((ct, HG_DIM), BF16),
                        pltpu.VMEM((ct, HG_DIM), BF16), pltpu.VMEM((ct, HG_DIM), F32),
                        pltpu.VMEM((ct, HG_DIM), F32), pltpu.VMEM((ct, HG_DIM), F32)],
        compiler_params=_params(2),
        name="hgrn_fwd",
    )(proj, proj, proj, proj, logits, gn, _block_masks(mrows))


def _sg_tile(t):
    return min(t, 512)


def _sgu_chunk_fwd(u, v, ln_g, ln_b, wm, bs):
    ua, dua = _gelu_and_grad(u)
    va, dva = _gelu_and_grad(v)
    vn, xhat, rstd = _ln_fwd(va, ln_g, ln_b)
    s = _dot(wm, vn.astype(BF16)) + bs
    return ua, dua, dva, vn, xhat, rstd, s


def _tril_weight(w):
    n = SG_CHUNK
    r = lax.broadcasted_iota(jnp.int32, (n, n), 0)
    c = lax.broadcasted_iota(jnp.int32, (n, n), 1)
    return jnp.where(c <= r, w, 0.0)


def _sgu_fwd(proj, mix, ln_g, ln_b, w_s, b_col):
    t = proj.shape[1]
    ct = _sg_tile(t)
    ng = SG_GROUPS
    wide = ng * SG_DIM
    blk_u = 4 * HG_HEADS // ng

    def body(u_ref, v_ref, g_ref, b_ref, w_ref, bs_ref, mix_ref, o_ref):
        del mix_ref
        for g in range(ng):
            lanes = slice(g * SG_DIM, (g + 1) * SG_DIM)
            wm = _tril_weight(w_ref[g]).astype(BF16)
            for n in range(ct // SG_CHUNK):
                rows = slice(n * SG_CHUNK, (n + 1) * SG_CHUNK)
                ua, _, _, _, _, _, s = _sgu_chunk_fwd(u_ref[g, rows, :], v_ref[g, rows, :], g_ref[g], b_ref[g], wm,
                                                      bs_ref[g])
                o_ref[rows, lanes] = (ua * s).astype(BF16)

    full = lambda a: pl.BlockSpec(a.shape, lambda c: (0,) * a.ndim)
    return pl.pallas_call(
        body,
        grid=(t // ct,),
        in_specs=[pl.BlockSpec((ng, ct, SG_DIM), lambda c: (blk_u, c, 0)),
                  pl.BlockSpec((ng, ct, SG_DIM), lambda c: (blk_u + 1, c, 0)),
                  full(ln_g), full(ln_b), full(w_s), full(b_col), ANY],
        out_specs=pl.BlockSpec((ct, wide), lambda c: (c, 1)),
        out_shape=jax.ShapeDtypeStruct(mix.shape, mix.dtype),
        input_output_aliases={6: 0},
        compiler_params=_params(1),
        name="sgu_fwd",
    )(proj, proj, ln_g, ln_b, w_s, b_col, mix)


def _mem_kv(mem, g, b, wk, wv):
    m_len, d = mem.shape

    def body(m_ref, g_ref, b_ref, wk_ref, wv_ref, mb_ref, xh_ref, rs_ref, k_ref, v_ref):
        m, xhat, rstd = _ln_fwd(m_ref[...], g_ref[...], b_ref[...])
        mb = m.astype(BF16)
        mb_ref[...] = mb
        xh_ref[...] = xhat
        rs_ref[...] = rstd
        k_ref[...] = _dot(mb, wk_ref[...]).astype(BF16)
        v_ref[...] = _dot(mb, wv_ref[...]).astype(BF16)

    return pl.pallas_call(
        body,
        out_shape=[jax.ShapeDtypeStruct((m_len, d), BF16), jax.ShapeDtypeStruct((m_len, d), F32),
                   jax.ShapeDtypeStruct((m_len, 1), F32), jax.ShapeDtypeStruct((m_len, d), BF16),
                   jax.ShapeDtypeStruct((m_len, d), BF16)],
        compiler_params=pltpu.CompilerParams(vmem_limit_bytes=VMEM_LIMIT_V7X),
        name="mem_kv",
    )(mem, g, b, wk, wv)


def _softmax_rows(s):
    m = jnp.max(s, axis=-1, keepdims=True)
    p = jnp.exp(s - m)
    return p / jnp.sum(p, axis=-1, keepdims=True)


def _attn_fwd(hb, wq, kb, vb):
    t, d = hb.shape
    tm = _row_tile(t)
    dh = d // X_HEADS
    scale = dh ** -0.5

    def body(h_ref, wq_ref, k_ref, v_ref, q_ref, o_ref):
        q = _dot(h_ref[...], wq_ref[...]).astype(BF16)
        q_ref[...] = q
        for hd in range(X_HEADS):
            sl = slice(hd * dh, (hd + 1) * dh)
            p = _softmax_rows(_dot_nt(q[:, sl], k_ref[:, sl]) * scale)
            o_ref[:, sl] = _dot(p.astype(BF16), v_ref[:, sl]).astype(BF16)

    row = pl.BlockSpec((tm, d), lambda i: (i, 0))
    full = lambda a: pl.BlockSpec(a.shape, lambda i: (0, 0))
    return pl.pallas_call(
        body,
        grid=(t // tm,),
        in_specs=[row, full(wq), full(kb), full(vb)],
        out_specs=[row, row],
        out_shape=[jax.ShapeDtypeStruct((t, d), BF16), jax.ShapeDtypeStruct((t, d), BF16)],
        compiler_params=_params(1),
        name="attn_fwd",
    )(hb, wq, kb, vb)


def _ffn_bwd_act(dyb, wd, a, b, coef, name, deps=()):
    t, d = dyb.shape
    f = wd.shape[0]
    tm = _row_tile(t)
    tn = _col_tile(f)

    def body(dy_ref, wd_ref, a_ref, b_ref, da_ref, db_ref):
        dy = dy_ref[...]
        for c in range(f // tn):
            cols = slice(c * tn, (c + 1) * tn)
            ds = _dot_nt(dy, wd_ref[cols, :]) * coef
            silu, dsilu = _silu_and_grad(a_ref[:, cols].astype(F32))
            da_ref[:, cols] = (ds * b_ref[:, cols].astype(F32) * dsilu).astype(BF16)
            db_ref[:, cols] = (ds * silu).astype(BF16)

    act = pl.BlockSpec((tm, f), lambda i: (i, 0))
    return pl.pallas_call(
        _drop_deps(body, 4, len(deps)),
        grid=(t // tm,),
        in_specs=[pl.BlockSpec((tm, d), lambda i: (i, 0)), _resident(wd), act, act] + [ANY] * len(deps),
        out_specs=[act, act],
        out_shape=[jax.ShapeDtypeStruct((t, f), BF16), jax.ShapeDtypeStruct((t, f), BF16)],
        compiler_params=_params(1),
        name=name,
    )(dyb, wd, a, b, *deps)


def _ffn_bwd_fused(dy, dyb, wd, wg, wu, a, b, coef, ln, name):
    t, d = dy.shape
    f = wd.shape[0]
    tm = min(t, 256)
    tn = _col_tile(f)

    def body(dy_ref, dyb_ref, wd_ref, wg_ref, wu_ref, a_ref, b_ref, xh_ref, rs_ref, g_ref,
             da_ref, db_ref, dyo_ref, dyob_ref, dg_ref, dbl_ref):
        dyb_v = dyb_ref[...]
        dh = ALPHA * dy_ref[...]
        for c in range(f // tn):
            cols = slice(c * tn, (c + 1) * tn)
            ds = _dot_nt(dyb_v, wd_ref[cols, :]) * coef
            silu, dsilu = _silu_and_grad(a_ref[:, cols].astype(F32))
            da = (ds * b_ref[:, cols].astype(F32) * dsilu).astype(BF16)
            db = (ds * silu).astype(BF16)
            da_ref[:, cols] = da
            db_ref[:, cols] = db
            dh = dh + _dot_nt(da, wg_ref[:, cols]) + _dot_nt(db, wu_ref[:, cols])

        @pl.when(pl.program_id(0) == 0)
        def _():
            dg_ref[...] = jnp.zeros_like(dg_ref)
            dbl_ref[...] = jnp.zeros_like(dbl_ref)

        dyp, dg, dbl = _ln_bwd(dh, xh_ref[...], rs_ref[...], g_ref[...])
        dyo_ref[...] = dyp
        dyob_ref[...] = dyp.astype(BF16)
        dg_ref[...] += dg
        dbl_ref[...] += dbl

    row = pl.BlockSpec((tm, d), lambda i: (i, 0))
    act = pl.BlockSpec((tm, f), lambda i: (i, 0))
    vec = pl.BlockSpec((1, d), lambda i: (0, 0))
    return pl.pallas_call(
        body,
        grid=(t // tm,),
        in_specs=[row, row, _resident(wd), _resident(wg), _resident(wu), act, act, row,
                  pl.BlockSpec((tm, 1), lambda i: (i, 0)), vec],
        out_specs=[act, act, row, row, vec, vec],
        out_shape=[jax.ShapeDtypeStruct((t, f), BF16), jax.ShapeDtypeStruct((t, f), BF16),
                   jax.ShapeDtypeStruct((t, d), F32), jax.ShapeDtypeStruct((t, d), BF16),
                   jax.ShapeDtypeStruct((1, d), F32), jax.ShapeDtypeStruct((1, d), F32)],
        compiler_params=_params(1),
        name=name,
    )(dy, dyb, wd, wg, wu, a, b, *ln)


def _mm_tn(a, b, name, scale=1.0, deps=()):
    t, m = a.shape
    n = b.shape[1]
    tt = _row_tile(t)
    nt = t // tt
    tm_o, tn_o = m, n

    def body(a_ref, b_ref, o_ref, acc):
        k = pl.program_id(2)

        @pl.when(k == 0)
        def _():
            acc[...] = jnp.zeros_like(acc)

        acc[...] += _dot_tn(a_ref[...], b_ref[...])

        @pl.when(k == nt - 1)
        def _():
            o_ref[...] = (acc[...] * scale).astype(BF16)

    return pl.pallas_call(
        _drop_deps(body, 2, len(deps)),
        grid=(m // tm_o, n // tn_o, nt),
        in_specs=[pl.BlockSpec((tt, tm_o), lambda i, j, k: (k, i)), pl.BlockSpec((tt, tn_o), lambda i, j, k: (k, j))]
        + [ANY] * len(deps),
        out_specs=pl.BlockSpec((tm_o, tn_o), lambda i, j, k: (i, j)),
        out_shape=jax.ShapeDtypeStruct((m, n), BF16),
        scratch_shapes=[pltpu.VMEM((tm_o, tn_o), F32)],
        compiler_params=_params(3),
        name=name,
    )(a, b, *deps)


def _mm_nt(lhs, w, name):
    t, d = lhs.shape
    kd = w.shape[0]
    tm = _row_tile(t)

    def body(l_ref, w_ref, o_ref):
        _store_slabs(o_ref, 0, _dot_nt(l_ref[...], w_ref[...]))

    return pl.pallas_call(
        body,
        grid=(t // tm,),
        in_specs=[pl.BlockSpec((tm, d), lambda i: (i, 0)), _resident(w)],
        out_specs=pl.BlockSpec((kd // LANES, tm, LANES), lambda i: (0, i, 0)),
        out_shape=jax.ShapeDtypeStruct((kd // LANES, t, LANES), F32),
        compiler_params=_params(1),
        name=name,
    )(lhs, w)


def _dx_ln(dy, pairs, ln, name, deps=()):
    t, d = dy.shape
    npair = len(pairs)
    tm = min(t, 512 // npair)
    nt = t // tm
    n_in = 1 + 2 * npair + (3 if ln is not None else 0)

    def body(*refs):
        dy_ref = refs[0]
        pr = refs[1:1 + 2 * npair]
        pos = 1 + 2 * npair
        dh = ALPHA * dy_ref[...]
        for p in range(npair):
            dh = dh + _dot_nt(pr[2 * p][...], pr[2 * p + 1][...])
        if ln is not None:
            xh_ref, rs_ref, g_ref = refs[pos:pos + 3]
            dyo_ref, dyb_ref, dg_ref, db_ref = refs[pos + 3:pos + 7]

            @pl.when(pl.program_id(0) == 0)
            def _():
                dg_ref[...] = jnp.zeros_like(dg_ref)
                db_ref[...] = jnp.zeros_like(db_ref)

            dyp, dg, db = _ln_bwd(dh, xh_ref[...], rs_ref[...], g_ref[...])
            dyo_ref[...] = dyp
            dyb_ref[...] = dyp.astype(BF16)
            dg_ref[...] += dg
            db_ref[...] += db
        else:
            refs[pos][...] = dh

    row = pl.BlockSpec((tm, d), lambda i: (i, 0))
    vec = pl.BlockSpec((1, d), lambda i: (0, 0))
    in_specs = [row]
    args = [dy]
    for lhs, w in pairs:
        in_specs += [pl.BlockSpec((tm, lhs.shape[1]), lambda i: (i, 0)), _resident(w)]
        args += [lhs, w]
    if ln is not None:
        in_specs += [row, pl.BlockSpec((tm, 1), lambda i: (i, 0)), vec]
        args += list(ln)
        out_specs = [row, row, vec, vec]
        out_shape = [jax.ShapeDtypeStruct((t, d), F32), jax.ShapeDtypeStruct((t, d), BF16),
                     jax.ShapeDtypeStruct((1, d), F32), jax.ShapeDtypeStruct((1, d), F32)]
    else:
        out_specs = row
        out_shape = jax.ShapeDtypeStruct((t, d), F32)
    return pl.pallas_call(
        _drop_deps(body, n_in, len(deps)),
        grid=(nt,),
        in_specs=in_specs + [ANY] * len(deps),
        out_specs=out_specs,
        out_shape=out_shape,
        compiler_params=_params(1),
        name=name,
    )(*args, *deps)


def _hgrn_bwd(proj, oraw, dmix, states, logits, gn):
    t = proj.shape[1]
    ct = _hg_tile(t)
    nct = t // ct
    nblk = ct // HG_BLOCK
    nh = HG_HEADS
    mrows = min(ct, 256)

    def body(q_ref, fz_ref, iv_ref, gg_ref, or_ref, do_ref, st_ref, lg_ref, gn_ref, mask_ref,
             dq_ref, dfz_ref, div_ref, dgg_ref, dlg_ref, dgn_ref,
             dstate, qt_s, kt_s, k_s, b_s, eb_s, ekb_s, dec_s, dor_s, dbl_s, gr_s, dk_s, dlb_acc):
        c = pl.program_id(1)

        @pl.when(c == 0)
        def _():
            dstate[...] = jnp.zeros_like(dstate)
            dlb_acc[...] = jnp.zeros_like(dlb_acc)
            dgn_ref[...] = jnp.zeros_like(dgn_ref)

        lb = _lower_bound(lg_ref[...])
        q = q_ref[...]
        sig, nsig, f, k = _forget_terms(fz_ref[...], lb)
        logf = jnp.log(f)
        b = _mask_dot(mask_ref[0], logf)
        bend = _mask_dot(mask_ref[2], logf)
        eb = jnp.exp(b)
        ekb = jnp.exp(bend - b)
        qt_s[...] = (q * eb).astype(BF16)
        kt_s[...] = (k * ekb).astype(BF16)
        k_s[...] = k
        b_s[...] = b
        eb_s[...] = eb
        ekb_s[...] = ekb
        dec_s[...] = jnp.exp(bend)
        oraw = or_ref[...]
        r = lax.rsqrt(jnp.mean(oraw * oraw, axis=-1, keepdims=True) + LN_EPS)
        on = oraw * r
        gg = gg_ref[...]
        silu, dsilu = _silu_and_grad(gg)
        doa = do_ref[...]
        gnv = gn_ref[...]
        dgg_ref[...] = (doa * on * gnv * dsilu).astype(BF16)
        dyn = doa * silu
        dgn_ref[...] += jnp.sum(dyn * on, axis=0, keepdims=True)
        don = dyn * gnv
        dor_s[...] = r * (don - on * jnp.mean(don * on, axis=-1, keepdims=True))
        tidx = lax.broadcasted_iota(jnp.int32, (HG_HALF, HG_DIM), 0)

        def blk(ii, carry):
            i = nblk - 1 - ii
            r0 = pl.multiple_of(i * HG_BLOCK, HG_BLOCK)
            rows = pl.ds(r0, HG_BLOCK)
            st = st_ref[i]
            dst = dstate[...]
            dstb = dst.astype(BF16)
            do = dor_s[rows, :]
            dob = do.astype(BF16)
            v = iv_ref[rows, :]
            vb = v.astype(BF16)
            qq = q_ref[rows, :]
            kk = k_s[rows, :]
            bb = b_s[rows, :]
            qt = qt_s[rows, :]
            kt = kt_s[rows, :]
            dec = dec_s[pl.ds(r0, 1), :]
            dkt = _dot(vb, dstb)
            dq = _dot(dob, st) * eb_s[rows, :]
            dk = dkt * ekb_s[rows, :]
            dv = _dot_nt(kt, dstb)
            gend = (jnp.sum(kk * dk, axis=0, keepdims=True)
                    + dec * jnp.sum(dst * st.astype(F32), axis=0, keepdims=True))
            qh, bh, doh = _halves(qq), _halves(bb), _halves(do)
            dqh, dkh, dvh = list(_halves(dq)), list(_halves(dk)), list(_halves(dv))
            for s in range(HG_BLOCK):
                ks, vs = kk[s:s + 1, :], v[s:s + 1, :]
                dk_part = dv_part = None
                for h in _causal_halves(s):
                    e = _decay_from(bh[h], bb[s:s + 1, :], s, h, tidx)
                    ke = ks * e
                    acol = jnp.sum(qh[h] * ke, axis=1, keepdims=True)
                    dacol = jnp.sum(doh[h] * vs, axis=1, keepdims=True)
                    dqh[h] = dqh[h] + dacol * ke
                    pk = dacol * (qh[h] * e)
                    pv = acol * doh[h]
                    dk_part = pk if dk_part is None else dk_part + pk
                    dv_part = pv if dv_part is None else dv_part + pv
                hs, row = divmod(s, HG_HALF)
                dkh[hs] = dkh[hs] + jnp.where(tidx == row, jnp.sum(dk_part, axis=0, keepdims=True), 0.0)
                dvh[hs] = dvh[hs] + jnp.where(tidx == row, jnp.sum(dv_part, axis=0, keepdims=True), 0.0)
            dq = jnp.concatenate(dqh, axis=0)
            dk = jnp.concatenate(dkh, axis=0)
            dv = jnp.concatenate(dvh, axis=0)
            dq_ref[rows, :] = dq.astype(BF16)
            div_ref[rows, :] = dv.astype(BF16)
            dk_s[rows, :] = dk
            dbl_s[rows, :] = qq * dq - kk * dk
            gr_s[rows, :] = jnp.zeros((HG_BLOCK, HG_DIM), F32) + gend
            dstate[...] = dst * dec + _dot_tn(dob, qt)
            return carry

        lax.fori_loop(0, nblk, blk, 0, unroll=HG_UNROLL)
        dlogf = _mask_dot(mask_ref[1], dbl_s[...]) + gr_s[...]
        dk = dk_s[...]
        dfz_ref[...] = ((dlogf / f - dk) * ((1.0 - lb) * sig * nsig)).astype(BF16)
        dlb_acc[...] += jnp.sum((dlogf / f - dk) * nsig, axis=0, keepdims=True)

        @pl.when(c == nct - 1)
        def _():
            dl0 = dlb_acc[...] * lb * (1.0 - lb)
            layer = lax.broadcasted_iota(jnp.int32, (2, HG_DIM), 0)
            dlg_ref[...] = jnp.where(layer == 0, dl0, -dl0)

    def slab(off):
        return pl.BlockSpec((None, ct, HG_DIM), lambda h, c: (off + h, nct - 1 - c, 0))

    out_slab = pl.BlockSpec((ct, HG_DIM), lambda h, c: (nct - 1 - c, h))
    tile_f32 = pltpu.VMEM((ct, HG_DIM), F32)
    tile_b16 = pltpu.VMEM((ct, HG_DIM), BF16)
    slab_shape = jax.ShapeDtypeStruct((t, nh * HG_DIM), BF16)
    return pl.pallas_call(
        body,
        grid=(nh, nct),
        in_specs=[slab(0), slab(nh), slab(2 * nh), slab(3 * nh), slab(0), slab(0),
                  pl.BlockSpec((None, nblk, HG_DIM, HG_DIM), lambda h, c: (h, nct - 1 - c, 0, 0)),
                  pl.BlockSpec((None, 2, HG_DIM), lambda h, c: (h, 0, 0)),
                  pl.BlockSpec((1, HG_DIM), lambda h, c: (0, 0)),
                  pl.BlockSpec((3, mrows, mrows), lambda h, c: (0, 0, 0))],
        out_specs=[out_slab, out_slab, out_slab, out_slab,
                   pl.BlockSpec((None, 2, HG_DIM), lambda h, c: (h, 0, 0)),
                   pl.BlockSpec((None, 1, HG_DIM), lambda h, c: (h, 0, 0))],
        out_shape=[slab_shape, slab_shape, slab_shape, slab_shape,
                   jax.ShapeDtypeStruct((nh, 2, HG_DIM), F32), jax.ShapeDtypeStruct((nh, 1, HG_DIM), F32)],
        scratch_shapes=[pltpu.VMEM((HG_DIM, HG_DIM), F32), tile_b16, tile_b16, tile_f32, tile_f32, tile_f32, tile_f32,
                        tile_f32, tile_f32, tile_f32, tile_f32, tile_f32, pltpu.VMEM((1, HG_DIM), F32)],
        compiler_params=_params(2),
        name="hgrn_bwd",
    )(proj, proj, proj, proj, oraw, dmix, states, logits, gn, _block_masks(mrows))


def _sgu_bwd(proj, dmix, ln_g, ln_b, w_s, w_t, b_col):
    t = proj.shape[1]
    ct = _sg_tile(t)
    nct = t // ct
    ng = SG_GROUPS
    off_u = 4 * HG_HEADS
    off_v = off_u + ng
    n = SG_CHUNK

    def body(u_ref, v_ref, do_ref, g_ref, b_ref, w_ref, wt_ref, bs_ref, du_ref, dv_ref, dg_ref, db_ref, dw_ref, dbs_ref):
        c = pl.program_id(1)

        @pl.when(c == 0)
        def _():
            dg_ref[...] = jnp.zeros_like(dg_ref)
            db_ref[...] = jnp.zeros_like(db_ref)
            dw_ref[...] = jnp.zeros_like(dw_ref)
            dbs_ref[...] = jnp.zeros_like(dbs_ref)

        r = lax.broadcasted_iota(jnp.int32, (n, n), 0)
        cc = lax.broadcasted_iota(jnp.int32, (n, n), 1)
        wm = jnp.where(cc <= r, w_ref[...], 0.0).astype(BF16)
        wmt = jnp.where(r <= cc, wt_ref[...], 0.0).astype(BF16)
        for ci in range(ct // n):
            rows = slice(ci * n, (ci + 1) * n)
            ua, dua, dva, vn, xhat, rstd, s = _sgu_chunk_fwd(u_ref[rows, :], v_ref[rows, :], g_ref[...], b_ref[...],
                                                             wm, bs_ref[...])
            do = do_ref[rows, :]
            du_ref[rows, :] = (do * s * dua).astype(BF16)
            ds = do * ua
            dsb = ds.astype(BF16)
            dbs_ref[...] += jnp.sum(ds, axis=1, keepdims=True)
            dw_ref[...] += _dot_nt(dsb, vn.astype(BF16))
            dvn = _dot(wmt, dsb)
            dva_in, dg, db = _ln_bwd(dvn, xhat, rstd, g_ref[...])
            dg_ref[...] += dg
            db_ref[...] += db
            dv_ref[rows, :] = (dva_in * dva).astype(BF16)

        @pl.when(c == nct - 1)
        def _():
            dw_ref[...] = jnp.where(cc <= r, dw_ref[...], 0.0)

    vec = pl.BlockSpec((None, 1, SG_DIM), lambda g, c: (g, 0, 0))
    mat = pl.BlockSpec((None, n, n), lambda g, c: (g, 0, 0))
    col = pl.BlockSpec((None, n, 1), lambda g, c: (g, 0, 0))
    out_slab = pl.BlockSpec((ct, SG_DIM), lambda g, c: (c, g))
    return pl.pallas_call(
        body,
        grid=(ng, nct),
        in_specs=[pl.BlockSpec((None, ct, SG_DIM), lambda g, c: (off_u + g, c, 0)),
                  pl.BlockSpec((None, ct, SG_DIM), lambda g, c: (off_v + g, c, 0)),
                  pl.BlockSpec((None, ct, SG_DIM), lambda g, c: (ng + g, c, 0)), vec, vec, mat, mat, col],
        out_specs=[out_slab, out_slab, vec, vec, mat, col],
        out_shape=[jax.ShapeDtypeStruct((t, ng * SG_DIM), BF16), jax.ShapeDtypeStruct((t, ng * SG_DIM), BF16),
                   jax.ShapeDtypeStruct((ng, 1, SG_DIM), F32), jax.ShapeDtypeStruct((ng, 1, SG_DIM), F32),
                   jax.ShapeDtypeStruct((ng, n, n), F32), jax.ShapeDtypeStruct((ng, n, 1), F32)],
        compiler_params=_params(2),
        name="sgu_bwd",
    )(proj, proj, dmix, ln_g, ln_b, w_s, w_t, b_col)


def _attn_bwd(dyb, wo, qb, kb, vb):
    t, d = dyb.shape
    m_len = kb.shape[0]
    tm = _row_tile(t)
    dh = d // X_HEADS
    scale = dh ** -0.5

    def body(dy_ref, wo_ref, q_ref, k_ref, v_ref, dq_ref, dk_ref, dv_ref):
        i = pl.program_id(0)

        @pl.when(i == 0)
        def _():
            dk_ref[...] = jnp.zeros_like(dk_ref)
            dv_ref[...] = jnp.zeros_like(dv_ref)

        do = _dot_nt(dy_ref[...], wo_ref[...]).astype(BF16)
        for hd in range(X_HEADS):
            sl = slice(hd * dh, (hd + 1) * dh)
            qh = q_ref[:, sl]
            p = _softmax_rows(_dot_nt(qh, k_ref[:, sl]) * scale)
            doh = do[:, sl]
            dp = _dot_nt(doh, v_ref[:, sl])
            ds = (p * (dp - jnp.sum(dp * p, axis=-1, keepdims=True)) * scale).astype(BF16)
            dq_ref[:, sl] = _dot(ds, k_ref[:, sl]).astype(BF16)
            dk_ref[:, sl] += _dot_tn(ds, qh)
            dv_ref[:, sl] += _dot_tn(p.astype(BF16), doh)

    row = pl.BlockSpec((tm, d), lambda i: (i, 0))
    full = lambda a: pl.BlockSpec(a.shape, lambda i: (0, 0))
    kv = pl.BlockSpec((m_len, d), lambda i: (0, 0))
    return pl.pallas_call(
        body,
        grid=(t // tm,),
        in_specs=[row, full(wo), row, full(kb), full(vb)],
        out_specs=[row, kv, kv],
        out_shape=[jax.ShapeDtypeStruct((t, d), BF16), jax.ShapeDtypeStruct((m_len, d), F32),
                   jax.ShapeDtypeStruct((m_len, d), F32)],
        compiler_params=_params(1),
        name="attn_bwd",
    )(dyb, wo, qb, kb, vb)


def _mem_bwd(dk, dv, mb, xhat, rstd, g, wk, wv):
    m_len, d = dk.shape

    def body(dk_ref, dv_ref, mb_ref, xh_ref, rs_ref, g_ref, wk_ref, wv_ref, gwk_ref, gwv_ref, dg_ref, db_ref):
        dkb = dk_ref[...].astype(BF16)
        dvb = dv_ref[...].astype(BF16)
        mb_v = mb_ref[...]
        gwk_ref[...] = _dot_tn(mb_v, dkb).astype(BF16)
        gwv_ref[...] = _dot_tn(mb_v, dvb).astype(BF16)
        dm = _dot_nt(dkb, wk_ref[...]) + _dot_nt(dvb, wv_ref[...])
        _, dg, db = _ln_bwd(dm, xh_ref[...], rs_ref[...], g_ref[...])
        dg_ref[...] = dg
        db_ref[...] = db

    return pl.pallas_call(
        body,
        out_shape=[jax.ShapeDtypeStruct((d, d), BF16), jax.ShapeDtypeStruct((d, d), BF16),
                   jax.ShapeDtypeStruct((1, d), F32), jax.ShapeDtypeStruct((1, d), F32)],
        compiler_params=pltpu.CompilerParams(vmem_limit_bytes=VMEM_LIMIT_V7X),
        name="mem_bwd",
    )(dk, dv, mb, xhat, rstd, g, wk, wv)


def _adamw(w, g, m, v):
    m = ADAM_B1 * m + (1.0 - ADAM_B1) * g
    v = ADAM_B2 * v + (1.0 - ADAM_B2) * (g * g)
    m_hat = m / (1.0 - ADAM_B1 ** ADAM_STEP)
    v_hat = v / (1.0 - ADAM_B2 ** ADAM_STEP)
    delta = -ADAM_LR * (m_hat / (jnp.sqrt(v_hat) + ADAM_EPS) + ADAM_WD * w)
    return delta, m, v


def _slot_sum(ref):
    g = ref[0].astype(F32)
    for s in range(1, N_DEV):
        g = g + ref[s].astype(F32)
    return g


def _adam_sharded(lands, w, m, v, axis, name):
    rows, cols = w.shape
    nl = len(lands)
    transposed = axis == 1 and nl == 2
    if transposed:
        rows, cols = cols, rows
        tr = 256
        grid = (rows // tr,)
        wblk = pl.BlockSpec((cols, tr), lambda i: (0, i))
        lblk = [pl.BlockSpec((N_DEV, tr, a.shape[2]), lambda i: (0, i, 0)) for a in lands]
    elif axis == 1:
        tr = 256 if rows % 256 == 0 else rows
        grid = (rows // tr,)
        wblk = pl.BlockSpec((tr, cols), lambda i: (i, 0))
        lblk = [pl.BlockSpec((N_DEV, tr, a.shape[2]), lambda i: (0, i, 0)) for a in lands]
    else:
        tc = _col_tile(cols)
        grid = (cols // tc,)
        wblk = pl.BlockSpec((rows, tc), lambda i: (0, i))
        lblk = [pl.BlockSpec((N_DEV, a.shape[1], tc), lambda i: (0, 0, i)) for a in lands]

    def body(*refs):
        w_ref, m_ref, v_ref = refs[nl:nl + 3]
        g_ref, d_ref, nm_ref, nv_ref = refs[nl + 3:]
        g = _slot_sum(refs[0])
        if nl == 2:
            tail = _slot_sum(refs[1])
            if transposed:
                g = jnp.concatenate([g.T, tail.T[:cols - g.shape[1], :]], axis=0)
            elif axis == 1:
                g = jnp.concatenate([g, tail[:, :cols - g.shape[1]]], axis=1)
            else:
                g = jnp.concatenate([g, tail[:rows - g.shape[0], :]], axis=0)
        delta, nm, nv = _adamw(w_ref[...], g, m_ref[...], v_ref[...])
        g_ref[...] = g
        d_ref[...] = delta
        nm_ref[...] = nm
        nv_ref[...] = nv

    shp = jax.ShapeDtypeStruct(w.shape, F32)
    return pl.pallas_call(
        body,
        grid=grid,
        in_specs=lblk + [wblk, wblk, wblk],
        out_specs=[wblk, wblk, wblk, wblk],
        out_shape=[shp, shp, shp, shp],
        compiler_params=_params(1),
        name=name,
    )(*[pltpu.with_memory_space_constraint(a, pltpu.HBM) for a in (*lands, w, m, v)])


def _mesh_pos():
    return lax.axis_index("x"), lax.axis_index("y"), lax.axis_index("c")


def _peer(k):
    x, y, c = _mesh_pos()
    pos = (x ^ (k >> 2), y ^ ((k >> 1) & 1), c ^ (k & 1))
    return pos, 4 * pos[0] + 2 * pos[1] + pos[2]


def _sem_index(row, k):
    return row * (N_DEV - 1) + k - 1


def _window(ref, axis, start, size):
    align = 16 if axis == 0 else LANES
    start = pl.multiple_of(start, align)
    return ref.at[pl.ds(start, size), :] if axis == 0 else ref.at[:, pl.ds(start, size)]


def _piece_refs(piece, srcs, lands, me, peer):
    kind, si, li, axis, base, stride, shape = piece
    if kind == "gather":
        return srcs[si], _window(lands[li], axis, base + stride * me, shape[axis])
    return _window(srcs[si], axis, base + stride * peer, shape[axis]), lands[li].at[me]


def _place_own(srcs, land_shapes, pieces, name):
    ns, nl, npc = len(srcs), len(land_shapes), len(pieces)

    def body(*refs):
        s_refs = refs[:ns]
        l_refs = refs[ns:ns + nl]
        bufs = refs[ns + nl:ns + nl + npc]
        sems = refs[ns + nl + npc]
        x, y, c = _mesh_pos()
        me = 4 * x + 2 * y + c
        loads = []
        for p, piece in enumerate(pieces):
            src, dst = _piece_refs(piece, s_refs, l_refs, me, me)
            cp = pltpu.make_async_copy(src, bufs[p], sems.at[0, p])
            cp.start()
            loads.append((cp, dst))
        stores = []
        for p, (cp, dst) in enumerate(loads):
            cp.wait()
            out = pltpu.make_async_copy(bufs[p], dst, sems.at[1, p])
            out.start()
            stores.append(out)
        for out in stores:
            out.wait()

    out = pl.pallas_call(
        body,
        in_specs=[ANY] * ns,
        out_specs=[ANY] * nl,
        out_shape=list(land_shapes),
        scratch_shapes=[pltpu.VMEM(pc[6], srcs[pc[1]].dtype) for pc in pieces] + [pltpu.SemaphoreType.DMA((2, npc))],
        compiler_params=pltpu.CompilerParams(vmem_limit_bytes=VMEM_LIMIT_V7X),
        name=name,
    )(*srcs)
    return list(out)


def _comm_start(srcs, lands, pieces, groups, name, after=()):
    ns, nl, na, ng = len(srcs), len(lands), len(after), len(groups)

    def body(*refs):
        s_refs = refs[:ns]
        l_refs = refs[ns:ns + nl]
        outs = refs[ns + nl + na:]
        sems = outs[:2 * ng]
        token = outs[-1]
        x, y, c = _mesh_pos()
        me = 4 * x + 2 * y + c
        for g, members in enumerate(groups):
            for row, p in enumerate(members):
                for k in range(1, N_DEV):
                    pos, peer = _peer(k)
                    src, dst = _piece_refs(pieces[p], s_refs, l_refs, me, peer)
                    pltpu.make_async_remote_copy(src_ref=src, dst_ref=dst, send_sem=sems[2 * g].at[_sem_index(row, k)],
                                                 recv_sem=sems[2 * g + 1].at[_sem_index(row, k)], device_id=pos,
                                                 device_id_type=MESH_ID).start()
        token[...] = jnp.zeros_like(token)

    sem_shapes = []
    for members in groups:
        sem_shapes += [pltpu.SemaphoreType.DMA((len(members) * (N_DEV - 1),))] * 2
    hbm_of = lambda a: pltpu.HBM(a.shape, a.dtype)
    out = pl.pallas_call(
        body,
        in_specs=[HBM] * (ns + nl) + [ANY] * na,
        out_specs=[SEM] * (2 * ng) + [HBM] * (ns + nl) + [pl.BlockSpec(memory_space=pltpu.VMEM)],
        out_shape=sem_shapes + [hbm_of(a) for a in srcs] + [hbm_of(a) for a in lands]
        + [jax.ShapeDtypeStruct((8, LANES), F32)],
        input_output_aliases={i: 2 * ng + i for i in range(ns + nl)},
        compiler_params=pltpu.CompilerParams(has_side_effects=DATAFLOW),
        name=name,
    )(*[pltpu.with_memory_space_constraint(a, pltpu.HBM) for a in list(srcs) + list(lands)], *after)
    sems = [(out[2 * g], out[2 * g + 1]) for g in range(ng)]
    return sems, list(out[2 * ng:2 * ng + ns]), list(out[2 * ng + ns:2 * ng + ns + nl]), out[-1]


def _comm_wait(srcs, lands, pieces, members, sems, after, name):
    ns, nl, na = len(srcs), len(lands), len(after)

    def body(*refs):
        s_refs = refs[:ns]
        l_refs = refs[ns:ns + nl]
        send_sems, recv_sems = refs[ns + nl:ns + nl + 2]
        x, y, c = _mesh_pos()
        me = 4 * x + 2 * y + c
        for row, p in enumerate(members):
            for k in range(1, N_DEV):
                pos, peer = _peer(k)
                src, dst = _piece_refs(pieces[p], s_refs, l_refs, me, peer)
                cp = pltpu.make_async_remote_copy(src_ref=src, dst_ref=dst, send_sem=send_sems.at[_sem_index(row, k)],
                                                  recv_sem=recv_sems.at[_sem_index(row, k)], device_id=pos,
                                                  device_id_type=MESH_ID)
                cp.wait_send()
                cp.wait_recv()

    hbm_of = lambda a: pltpu.HBM(a.shape, a.dtype)
    out = pl.pallas_call(
        body,
        in_specs=[HBM] * (ns + nl) + [SEM, SEM] + [ANY] * na,
        out_specs=[HBM] * (ns + nl),
        out_shape=[hbm_of(a) for a in srcs] + [hbm_of(a) for a in lands],
        input_output_aliases={i: i for i in range(ns + nl)},
        compiler_params=pltpu.CompilerParams(has_side_effects=DATAFLOW),
        name=name,
    )(*srcs, *lands, sems[0], sems[1], *after)
    return list(out[ns:])


def _landed_block(piece, lands, owner):
    _, _, li, axis, base, stride, shape = piece
    return _window(lands[li], axis, base + stride * owner, shape[axis])


def _copy_stage(name, bufs, in_sems, out_sem_sizes, emit, after=()):
    nb, ni, no, na = len(bufs), len(in_sems), len(out_sem_sizes), len(after)

    def body(*refs):
        b_refs = refs[:nb]
        i_refs = refs[nb:nb + ni]
        o_refs = refs[nb + ni + na:nb + ni + na + no]
        emit(b_refs, i_refs, o_refs)
        refs[-1][...] = jnp.zeros_like(refs[-1])

    hbm_of = lambda a: pltpu.HBM(a.shape, a.dtype)
    out = pl.pallas_call(
        body,
        in_specs=[HBM] * nb + [SEM] * ni + [ANY] * na,
        out_specs=[SEM] * no + [HBM] * nb + [pl.BlockSpec(memory_space=pltpu.VMEM)],
        out_shape=[pltpu.SemaphoreType.DMA((n,)) for n in out_sem_sizes] + [hbm_of(a) for a in bufs]
        + [jax.ShapeDtypeStruct((8, LANES), F32)],
        input_output_aliases={i: no + i for i in range(nb)},
        compiler_params=pltpu.CompilerParams(has_side_effects=DATAFLOW),
        name=name,
    )(*[pltpu.with_memory_space_constraint(a, pltpu.HBM) for a in bufs], *in_sems, *after)
    return list(out[:no]), list(out[no:no + nb]), out[-1]


def _remote(src, dst, send, recv, to):
    return pltpu.make_async_remote_copy(src_ref=src, dst_ref=dst, send_sem=send, recv_sem=recv, device_id=to,
                                        device_id_type=MESH_ID)


def _routed_gather(srcs, lands, pieces, after, name):
    ns, npc = len(srcs), len(pieces)

    def places():
        x, y, c = _mesh_pos()
        index = lambda p: 4 * p[0] + 2 * p[1] + p[2]
        me, sib = (x, y, c), (x, y, 1 - c)
        xnb, ynb = (1 - x, y, c), (x, 1 - y, c)
        got_first = (x ^ (1 - c), y ^ c, c)
        pass_to = (x ^ c, y ^ (1 - c), c)
        diag = (1 - x, 1 - y, c)
        return index, me, sib, xnb, ynb, got_first, pass_to, diag

    def start(b, _, o):
        index, me, sib, xnb, ynb, *_rest = places()
        send_a, recv_sib, recv_nb = o
        for p, piece in enumerate(pieces):
            src, dst = _piece_refs(piece, b[:ns], b[ns:], index(me), 0)
            _remote(src, dst, send_a.at[3 * p], recv_sib.at[p], sib).start()
            _remote(src, dst, send_a.at[3 * p + 1], recv_nb.at[2 * p], xnb).start()
            _remote(src, dst, send_a.at[3 * p + 2], recv_nb.at[2 * p + 1], ynb).start()

    def pass_a(b, i, o):
        index, me, sib, xnb, ynb, got_first, pass_to, _diag = places()
        (recv_nb,) = i
        send_f, recv_f, send_d, recv_d = o
        for p, piece in enumerate(pieces):
            for j, nb in enumerate((xnb, ynb)):
                blk = _landed_block(piece, b, index(nb))
                _remote(blk, blk, send_f.at[2 * p + j], recv_nb.at[2 * p + j], sib).wait_recv()
                _remote(blk, blk, send_f.at[2 * p + j], recv_f.at[2 * p + j], sib).start()
            blk = _landed_block(piece, b, index(got_first))
            _remote(blk, blk, send_d.at[p], recv_d.at[p], pass_to).start()

    def pass_b(b, i, o):
        index, me, sib, *_mid, diag = places()
        (recv_d,) = i
        send_g, recv_g = o
        for p, piece in enumerate(pieces):
            blk = _landed_block(piece, b, index(diag))
            _remote(blk, blk, send_g.at[p], recv_d.at[p], sib).wait_recv()
            _remote(blk, blk, send_g.at[p], recv_g.at[p], sib).start()

    def last(b, i, _):
        index, me, sib, *_others = places()
        send_a, recv_sib, send_f, recv_f, send_d, send_g, recv_g = i
        for p, piece in enumerate(pieces):
            src, dst = _piece_refs(piece, b[:ns], b[ns:], index(me), 0)
            cp = lambda s_sem, r_sem: _remote(src, dst, s_sem, r_sem, sib)
            cp(send_a.at[3 * p], recv_sib.at[p]).wait_recv()
            cp(send_a.at[3 * p], recv_g.at[p]).wait_recv()
            for j in range(3):
                cp(send_a.at[3 * p + j], recv_sib.at[p]).wait_send()
            for j in range(2):
                cp(send_f.at[2 * p + j], recv_f.at[2 * p + j]).wait_recv()
                cp(send_f.at[2 * p + j], recv_f.at[2 * p + j]).wait_send()
            cp(send_d.at[p], recv_sib.at[p]).wait_send()
            cp(send_g.at[p], recv_sib.at[p]).wait_send()

    (send_a, recv_sib, recv_nb), bufs, _ = _copy_stage(name + "_start", list(srcs) + list(lands), [],
                                                       [3 * npc, npc, 2 * npc], start)
    srcs, lands = bufs[:ns], bufs[ns:]
    (send_f, recv_f, send_d, recv_d), lands, _ = _copy_stage(name + "_pass_a", lands, [recv_nb],
                                                             [2 * npc, 2 * npc, npc, npc],
                                                             lambda b, i, o: pass_a(b, i, o), after=after)
    (send_g, recv_g), lands, tok = _copy_stage(name + "_pass_b", lands, [recv_d], [npc, npc], pass_b)
    _, bufs, _ = _copy_stage(name + "_last", list(srcs) + list(lands),
                             [send_a, recv_sib, send_f, recv_f, send_d, send_g, recv_g], [], last)
    return bufs[ns:], tok


_SMALL_NAMES = ("ln1_g", "ln1_b", "hg_lb_logits", "hg_norm_g", "sg_ln_g", "sg_ln_b", "sg_w_s", "sg_b_s",
                "ln2_g", "ln2_b", "mem_ln_g", "mem_ln_b", "ln3_g", "ln3_b", "ln4_g", "ln4_b")


_VEC_NAMES = ("ln1_g", "ln1_b", "ln2_g", "ln2_b", "mem_ln_g", "mem_ln_b", "ln3_g", "ln3_b", "ln4_g", "ln4_b")
_ROW_NAMES = ("hg_lb_logits", "hg_norm_g", "sg_ln_g", "sg_ln_b", "sg_b_s", "sg_w_s")
VEC_ROWS = 16


def _row_plan(shapes):
    plan, pos = {}, 0
    for k in _ROW_NAMES:
        shp = shapes[k]
        slabs, off = [], pos
        for idx in itertools.product(*[range(dim) for dim in shp[:-2]]):
            slabs.append((idx, off, shp[-2]))
            off += shp[-2]
        plan[k] = (pos, slabs)
        pos = -(-off // 8) * 8
    return plan, -(-pos // 16) * 16


def _pack_small_grads(gs, shapes, loss):
    d = gs[_VEC_NAMES[0]].size
    vec = jnp.concatenate([gs[k].reshape(1, -1) for k in _VEC_NAMES] + [jnp.tile(loss, (1, d // LANES))], axis=0)
    vec = jnp.pad(vec, ((0, VEC_ROWS - vec.shape[0]), (0, 0)))
    plan, total = _row_plan(shapes)
    parts, pos = [], 0
    for k in _ROW_NAMES:
        first, slabs = plan[k]
        rows = gs[k].reshape(-1, LANES)
        end = slabs[-1][1] + slabs[-1][2]
        nxt = -(-end // 8) * 8
        parts.append(jnp.pad(rows, ((0, nxt - first - rows.shape[0]), (0, 0))))
        pos = nxt
    parts.append(jnp.zeros((total - pos, LANES), F32))
    return vec, jnp.concatenate(parts, axis=0)


def _adam_small(land_vec, land_rows, w, m, v):
    names = _VEC_NAMES + _ROW_NAMES
    n = len(names)
    shapes = {k: w[k].shape for k in names}
    plan, _ = _row_plan(shapes)

    def body(*refs):
        lv_ref, lr_ref = refs[:2]
        w_refs, m_refs, v_refs = refs[2:2 + n], refs[2 + n:2 + 2 * n], refs[2 + 2 * n:2 + 3 * n]
        outs = refs[2 + 3 * n:2 + 7 * n]
        loss_ref = refs[2 + 7 * n]
        gv_s, gr_s = refs[3 + 7 * n:]
        gv_s[...] = _slot_sum(lv_ref)
        gr_s[...] = _slot_sum(lr_ref)
        loss_ref[...] = gv_s[len(_VEC_NAMES):len(_VEC_NAMES) + 1, :LANES]
        for p, k in enumerate(names):
            if k in _VEC_NAMES:
                row = _VEC_NAMES.index(k)
                slabs = [((), None, None)]
            else:
                slabs = plan[k][1]
            for idx, off, rows in slabs:
                g = gv_s[row:row + 1, :] if off is None else gr_s[off:off + rows, :]
                sel = idx + (slice(None), slice(None))
                delta, nm, nv = _adamw(w_refs[p][sel], g, m_refs[p][sel], v_refs[p][sel])
                for o, val in zip(range(4), (g, delta, nm, nv)):
                    outs[o * n + p][sel] = val

    flat = lambda tree: [tree[k] for k in names]
    shp = [jax.ShapeDtypeStruct(shapes[k], F32) for k in names]
    out = pl.pallas_call(
        body,
        out_shape=shp * 4 + [jax.ShapeDtypeStruct((1, LANES), F32)],
        scratch_shapes=[pltpu.VMEM(land_vec.shape[1:], F32), pltpu.VMEM(land_rows.shape[1:], F32)],
        name="adam_small",
    )(land_vec, land_rows, *flat(w), *flat(m), *flat(v))
    return [dict(zip(names, out[o * n:(o + 1) * n])) for o in range(4)], out[4 * n]


_COL_FFN = ("ffn1_w_gate", "ffn1_w_up", "ffn2_w_gate", "ffn2_w_up")
_ROW_FFN = ("ffn1_w_down", "ffn2_w_down")
_ROW_SQ = ("w_out", "xa_w_q", "xa_w_k", "xa_w_v", "xa_w_o")
_BIG_NAMES = ("ffn1_w_gate", "ffn1_w_up", "ffn1_w_down", "w_in", "w_out", "xa_w_q", "xa_w_k", "xa_w_v", "xa_w_o",
              "ffn2_w_gate", "ffn2_w_up", "ffn2_w_down")


def _ffn_split(fs):
    main = (fs // MXU_WIDTH_V7X) * MXU_WIDTH_V7X
    tail = fs - main
    tail_pad = -(-tail // LANES) * LANES
    assert main > 0 and tail > 0
    return main, tail, tail_pad


def _layout(name, shard_shape):
    r, c = shard_shape
    if name in _COL_FFN:
        main, tail, pad = _ffn_split(c)
        return (r, N_DEV * (main + pad)), [(1, 0, main, (r, main), (0, main)),
                                           (1, N_DEV * main, pad, (r, pad), (main, c))]
    if name in _ROW_FFN:
        main, tail, pad = _ffn_split(r)
        return (N_DEV * (main + pad), c), [(0, 0, main, (main, c), (0, main)),
                                           (0, N_DEV * main, pad, (pad, c), (main, r))]
    if name == "w_in":
        return (r, N_DEV * c), [(1, 0, c, (r, c), (0, c))]
    return (N_DEV * r, c), [(0, 0, r, (r, c), (0, r))]


def _shard_pieces(name, shard):
    out = []
    for axis, _, _, shape, (lo, hi) in _layout(name, shard.shape)[1]:
        part = shard[lo:hi, :] if axis == 0 else shard[:, lo:hi]
        pad = [(0, shape[0] - part.shape[0]), (0, shape[1] - part.shape[1])]
        out.append(jnp.pad(part, pad).astype(BF16))
    return out


def _gather_plan(names, shards):
    srcs, land_shapes, pieces, index = [], [], [], {}
    for li, name in enumerate(names):
        shape2d, parts = _layout(name, shards[name].shape)
        land_shapes.append(jax.ShapeDtypeStruct(shape2d, BF16))
        index[name] = []
        for (axis, base, stride, shape, _), src in zip(parts, _shard_pieces(name, shards[name])):
            index[name].append(len(pieces))
            pieces.append(("gather", len(srcs), li, axis, base, stride, shape))
            srcs.append(src)
    return srcs, land_shapes, pieces, index


def _scatter_plan(names, grads, shard_shapes):
    srcs, land_shapes, pieces, index = [], [], [], {}
    for si, name in enumerate(names):
        _, parts = _layout(name, shard_shapes[name])
        srcs.append(grads[name])
        index[name] = []
        for axis, base, stride, shape, _ in parts:
            index[name].append(len(land_shapes))
            pieces.append(("scatter", si, len(land_shapes), axis, base, stride, shape))
            land_shapes.append(jax.ShapeDtypeStruct((N_DEV,) + shape, grads[name].dtype))
    return srcs, land_shapes, pieces, index


def _small_views(small):
    row = lambda a: a.reshape(1, -1)
    ln = {k: row(small[k]) for k in ("ln1_g", "ln1_b", "ln2_g", "ln2_b", "ln3_g", "ln3_b", "ln4_g", "ln4_b",
                                      "mem_ln_g", "mem_ln_b", "hg_norm_g")}
    sg_w = small["sg_w_s"].reshape(SG_GROUPS, SG_CHUNK, SG_CHUNK)
    sg = dict(logits=jnp.swapaxes(small["hg_lb_logits"], 0, 1),
              g=small["sg_ln_g"].reshape(SG_GROUPS, 1, SG_DIM), b=small["sg_ln_b"].reshape(SG_GROUPS, 1, SG_DIM),
              w=sg_w, wt=jnp.swapaxes(sg_w, 1, 2), bs=small["sg_b_s"].reshape(SG_GROUPS, SG_CHUNK, 1))
    return ln, sg


def _forward(x, mem, target, get_w, small, first_deps=()):
    ln, sg = _small_views(small)
    xb = x.astype(BF16)
    a1, b1, s1 = _ffn_up(xb, get_w("ffn1_w_gate", ()), get_w("ffn1_w_up", ()), "ffn1_up", deps=first_deps)
    h1b, xh1, rs1 = _mm_res_ln(s1, get_w("ffn1_w_down", (s1,)), x, ln["ln1_g"], ln["ln1_b"], 0.5, "ffn1_down_ln")
    proj = _mm_nn(h1b, get_w("w_in", (h1b,)), "mix_in")
    oraw, mix, states = _hgrn_fwd(proj, sg["logits"], ln["hg_norm_g"])
    mix = _sgu_fwd(proj, mix, sg["g"], sg["b"], sg["w"], sg["bs"])
    h2b, xh2, rs2 = _mm_res_ln(mix, get_w("w_out", (mix,)), (xh1, ln["ln1_g"], ln["ln1_b"]), ln["ln2_g"], ln["ln2_b"],
                               1.0, "mix_out_ln")
    mb, mxh, mrs, kb, vb = _mem_kv(mem, ln["mem_ln_g"], ln["mem_ln_b"], get_w("xa_w_k", (h2b,)), get_w("xa_w_v", (h2b,)))
    qb, att = _attn_fwd(h2b, get_w("xa_w_q", (kb,)), kb, vb)
    h3b, xh3, rs3 = _mm_res_ln(att, get_w("xa_w_o", (att,)), (xh2, ln["ln2_g"], ln["ln2_b"]), ln["ln3_g"], ln["ln3_b"],
                               1.0, "attn_out_ln")
    a2, b2, s2 = _ffn_up(h3b, get_w("ffn2_w_gate", (h3b,)), get_w("ffn2_w_up", (h3b,)), "ffn2_up")
    loss, dy4, dy4b, dg4, db4 = _mm_res_ln(s2, get_w("ffn2_w_down", (s2,)), (xh3, ln["ln3_g"], ln["ln3_b"]),
                                           ln["ln4_g"], ln["ln4_b"], 0.5, "ffn2_down_ln_loss", target=target)
    return dict(xb=xb, a1=a1, b1=b1, s1=s1, h1b=h1b, xh1=xh1, rs1=rs1, proj=proj, oraw=oraw, mix=mix, states=states,
                h2b=h2b, xh2=xh2, rs2=rs2, mb=mb, mxh=mxh, mrs=mrs, kb=kb, vb=vb, qb=qb, att=att, h3b=h3b, xh3=xh3,
                rs3=rs3, a2=a2, b2=b2, s2=s2, loss=loss, dy4=dy4, dy4b=dy4b, dg4=dg4, db4=db4)


def _backward(sv, wt, small, send):
    ln, sg = _small_views(small)
    gs = {"ln4_g": sv["dg4"], "ln4_b": sv["db4"]}
    loss, dy4, dy4b = sv["loss"], sv["dy4"], sv["dy4b"]
    g_down2 = _mm_tn(sv["s2"], dy4b, "g_ffn2_down", scale=0.5)
    da2, db2, dy3, dy3b, gs["ln3_g"], gs["ln3_b"] = _ffn_bwd_fused(
        dy4, dy4b, wt["ffn2_w_down"], wt["ffn2_w_gate"], wt["ffn2_w_up"], sv["a2"], sv["b2"], 0.5,
        (sv["xh3"], sv["rs3"], ln["ln3_g"]), "ffn2_bwd")
    g_gate2 = _mm_tn(sv["h3b"], da2, "g_ffn2_gate")
    g_up2 = _mm_tn(sv["h3b"], db2, "g_ffn2_up")
    tok = send(("ffn2_w_down", "ffn2_w_gate", "ffn2_w_up"), (g_down2, g_gate2, g_up2))

    g_o = _mm_tn(sv["att"], dy3b, "g_xa_o", deps=(tok,))
    dqb, dk, dv = _attn_bwd(dy3b, wt["xa_w_o"], sv["qb"], sv["kb"], sv["vb"])
    g_q = _mm_tn(sv["h2b"], dqb, "g_xa_q")
    g_k, g_v, gs["mem_ln_g"], gs["mem_ln_b"] = _mem_bwd(dk, dv, sv["mb"], sv["mxh"], sv["mrs"], ln["mem_ln_g"],
                                                        wt["xa_w_k"], wt["xa_w_v"])
    tok = send(("xa_w_o", "xa_w_q", "xa_w_k", "xa_w_v"), (g_o, g_q, g_k, g_v))
    dy2, dy2b, gs["ln2_g"], gs["ln2_b"] = _dx_ln(dy3, [(dqb, wt["xa_w_q"])], (sv["xh2"], sv["rs2"], ln["ln2_g"]),
                                                 "attn_dx_ln", deps=(tok,))

    g_out = _mm_tn(sv["mix"], dy2b, "g_w_out")
    dmix = _mm_nt(dy2b, wt["w_out"], "mix_out_bwd")
    dq, dfz, div, dgg, dlg, dgn = _hgrn_bwd(sv["proj"], sv["oraw"], dmix, sv["states"], sg["logits"], ln["hg_norm_g"])
    du, dvv, gs["sg_ln_g"], gs["sg_ln_b"], gs["sg_w_s"], gs["sg_b_s"] = _sgu_bwd(
        sv["proj"], dmix, sg["g"], sg["b"], sg["w"], sg["wt"], sg["bs"])
    gs["hg_lb_logits"] = jnp.swapaxes(dlg, 0, 1)
    gs["hg_norm_g"] = jnp.sum(dgn, axis=0)
    dproj = jnp.concatenate([dq, dfz, div, dgg, du, dvv], axis=1)
    g_in = _mm_tn(sv["h1b"], dproj, "g_w_in")
    tok = send(("w_out", "w_in"), (g_out, g_in))
    dy1, dy1b, gs["ln1_g"], gs["ln1_b"] = _dx_ln(dy2, [(dproj, wt["w_in"])], (sv["xh1"], sv["rs1"], ln["ln1_g"]),
                                                 "mix_dx_ln", deps=(tok,))

    g_down1 = _mm_tn(sv["s1"], dy1b, "g_ffn1_down", scale=0.5)
    tok = send(("ffn1_w_down",), (g_down1,))
    da1, db1 = _ffn_bwd_act(dy1b, wt["ffn1_w_down"], sv["a1"], sv["b1"], 0.5, "ffn1_bwd_act", deps=(tok,))
    g_gate1 = _mm_tn(sv["xb"], da1, "g_ffn1_gate")
    tok = send(("ffn1_w_gate",), (g_gate1,))
    g_up1 = _mm_tn(sv["xb"], db1, "g_ffn1_up", deps=(tok,))
    tok = send(("ffn1_w_up",), (g_up1,))
    grad_x = _dx_ln(dy1, [(da1, wt["ffn1_w_gate"]), (db1, wt["ffn1_w_up"])], None, "ffn1_dx", deps=(tok,))
    return loss, grad_x, gs


_WEIGHT_NAMES = ("ffn1_w_gate", "ffn1_w_up", "ffn1_w_down", "ln1_g", "ln1_b", "w_in", "hg_lb_logits", "hg_norm_g",
                 "sg_ln_g", "sg_ln_b", "sg_w_s", "sg_b_s", "w_out", "ln2_g", "ln2_b", "mem_ln_g", "mem_ln_b",
                 "xa_w_q", "xa_w_k", "xa_w_v", "xa_w_o", "ln3_g", "ln3_b", "ffn2_w_gate", "ffn2_w_up", "ffn2_w_down",
                 "ln4_g", "ln4_b")
_FIRST = ("ffn1_w_gate", "ffn1_w_up")
_SECOND = ("ffn1_w_down", "w_in", "w_out")
_THIRD = ("xa_w_k", "xa_w_v", "xa_w_q", "xa_w_o", "ffn2_w_gate", "ffn2_w_up", "ffn2_w_down")


def kernel(x, mem, ffn1_w_gate, ffn1_w_up, ffn1_w_down, ln1_g, ln1_b, w_in, hg_lb_logits, hg_norm_g, sg_ln_g, sg_ln_b, sg_w_s, sg_b_s, w_out, ln2_g, ln2_b, mem_ln_g, mem_ln_b, xa_w_q, xa_w_k, xa_w_v, xa_w_o, ln3_g, ln3_b, ffn2_w_gate, ffn2_w_up, ffn2_w_down, ln4_g, ln4_b, loss_target, m_ffn1_w_gate, m_ffn1_w_up, m_ffn1_w_down, m_ln1_g, m_ln1_b, m_w_in, m_hg_lb_logits, m_hg_norm_g, m_sg_ln_g, m_sg_ln_b, m_sg_w_s, m_sg_b_s, m_w_out, m_ln2_g, m_ln2_b, m_mem_ln_g, m_mem_ln_b, m_xa_w_q, m_xa_w_k, m_xa_w_v, m_xa_w_o, m_ln3_g, m_ln3_b, m_ffn2_w_gate, m_ffn2_w_up, m_ffn2_w_down, m_ln4_g, m_ln4_b, v_ffn1_w_gate, v_ffn1_w_up, v_ffn1_w_down, v_ln1_g, v_ln1_b, v_w_in, v_hg_lb_logits, v_hg_norm_g, v_sg_ln_g, v_sg_ln_b, v_sg_w_s, v_sg_b_s, v_w_out, v_ln2_g, v_ln2_b, v_mem_ln_g, v_mem_ln_b, v_xa_w_q, v_xa_w_k, v_xa_w_v, v_xa_w_o, v_ln3_g, v_ln3_b, v_ffn2_w_gate, v_ffn2_w_up, v_ffn2_w_down, v_ln4_g, v_ln4_b):
    args = dict(locals())
    w = {k: args[k] for k in _WEIGHT_NAMES}
    m = {k: args["m_" + k] for k in _WEIGHT_NAMES}
    v = {k: args["v_" + k] for k in _WEIGHT_NAMES}
    shards = {k: w[k][0] for k in _BIG_NAMES}
    shard_shapes = {k: shards[k].shape for k in _BIG_NAMES}
    small = {k: (w[k][0] if k != "hg_lb_logits" else w[k]) for k in _SMALL_NAMES}

    srcs1, shapes1, pieces1, idx1 = _gather_plan(_FIRST, shards)
    lands1 = _place_own(srcs1, shapes1, pieces1, "gather_first_own")
    rest = _SECOND + _THIRD
    srcs2, shapes2, pieces2, idx2 = _gather_plan(rest, shards)
    lands2 = _place_own(srcs2, shapes2, pieces2, "gather_rest_own")
    groups2 = [list(idx2[k]) for k in rest]
    lands1, tok1 = _routed_gather(srcs1, lands1, pieces1, tuple(lands2), "gather_first")
    sems2, srcs2, lands2, tok2 = _comm_start(srcs2, lands2, pieces2, groups2, "gather_rest_start", after=(tok1,))
    wt = dict(zip(_FIRST, lands1))
    pending = {k: gi for gi, k in enumerate(rest)}

    def get_w(name, after):
        if name in pending:
            gi = pending.pop(name)
            si = [pieces2[p][1] for p in groups2[gi]]
            sub = [(pieces2[p][0], row, 0) + pieces2[p][3:] for row, p in enumerate(groups2[gi])]
            wt[name] = _comm_wait([srcs2[s] for s in si], [lands2[gi]], sub, list(range(len(sub))), sems2[gi],
                                  after, "gather_wait_" + name)[0]
        return wt[name]

    sv = _forward(x[0], mem[0], loss_target[0], get_w, small, first_deps=(tok2,))

    sent = []

    def send(names, grads):
        srcs, shapes, pieces, idx = _scatter_plan(names, dict(zip(names, grads)), shard_shapes)
        lands = _place_own(srcs, shapes, pieces, "grads_own_%d" % len(sent))
        sems, srcs, lands, tok = _comm_start(srcs, lands, pieces, [list(range(len(pieces)))],
                                             "grads_start_%d" % len(sent))
        sent.append((names, srcs, lands, pieces, idx, sems[0]))
        return tok

    loss, grad_x, gs = _backward(sv, wt, small, send)

    ssrc = list(_pack_small_grads(gs, {k: w[k].shape for k in _SMALL_NAMES}, loss))
    sp = [("scatter", i, i, 0, 0, 0, a.shape) for i, a in enumerate(ssrc)]
    sshape = [jax.ShapeDtypeStruct((N_DEV,) + a.shape, F32) for a in ssrc]
    sl = _place_own(ssrc, sshape, sp, "small_own")
    ssem, ssrc, sl, _ = _comm_start(ssrc, sl, sp, [[0, 1]], "small_start")

    out_g, out_d, out_m, out_v = {}, {}, {}, {}
    after = (grad_x,)
    for n_sent, (names, srcs, lands, pieces, idx, sems) in enumerate(sent):
        lands = _comm_wait(srcs, lands, pieces, list(range(len(pieces))), sems, after, "grads_wait_%d" % n_sent)
        for k in names:
            axis = 1 if (k in _COL_FFN or k == "w_in") else 0
            if k in _COL_FFN:
                res = _adam_sharded([lands[i] for i in idx[k]], w[k][0].T, m[k][0].T, v[k][0].T, axis, "adam_" + k)
                res = [r.T for r in res]
            else:
                res = _adam_sharded([lands[i] for i in idx[k]], w[k][0], m[k][0], v[k][0], axis, "adam_" + k)
            out_g[k], out_d[k], out_m[k], out_v[k] = [r[None] for r in res]
        after = (out_v[names[-1]],)
    sl = _comm_wait(ssrc, sl, sp, [0, 1], ssem[0], after, "small_wait")
    small_out, loss_sum = _adam_small(sl[0], sl[1], w, m, v)
    for dst, res in zip((out_g, out_d, out_m, out_v), small_out):
        dst.update(res)
    loss_all = loss_sum[0, 0]
    return (loss_all, grad_x[None], *[out_g[k] for k in _WEIGHT_NAMES], *[out_d[k] for k in _WEIGHT_NAMES],
            *[out_m[k] for k in _WEIGHT_NAMES], *[out_v[k] for k in _WEIGHT_NAMES])
```

```python
import itertools

import jax
import jax.numpy as jnp
import numpy as np
from jax import lax
from jax.experimental import pallas as pl
from jax.experimental.pallas import tpu as pltpu

F32 = jnp.float32
BF16 = jnp.bfloat16

N_DEV = 8
ALPHA = 2.0 ** 0.25
LN_EPS = 1e-5
HG_HEADS = 4
HG_DIM = 128
SG_GROUPS = 4
SG_DIM = 128
SG_CHUNK = 128
X_HEADS = 4
HG_BLOCK = 16
HG_UNROLL = 16
ADAM_LR = 0.001
ADAM_B1 = 0.9
ADAM_B2 = 0.999
ADAM_EPS = 1e-08
ADAM_WD = 0.01
ADAM_STEP = 10
VMEM_LIMIT_V7X = 48 * 1024 * 1024
MXU_WIDTH_V7X = 256
LANES = 128
MESH_ID = pl.DeviceIdType.MESH
ANY = pl.BlockSpec(memory_space=pl.ANY)
HBM = pl.BlockSpec(memory_space=pltpu.HBM)
SEM = pl.BlockSpec(memory_space=pltpu.SEMAPHORE)
DATAFLOW = pltpu.SideEffectType.DATAFLOW_SIDE_EFFECTING


def _params(n_axes):
    return pltpu.CompilerParams(dimension_semantics=("arbitrary",) * n_axes, vmem_limit_bytes=VMEM_LIMIT_V7X)


def _dot(a, b):
    return jnp.dot(a, b, preferred_element_type=F32)


def _dot_nt(a, b):
    return lax.dot_general(a, b, (((1,), (1,)), ((), ())), preferred_element_type=F32)


def _dot_tn(a, b):
    return lax.dot_general(a, b, (((0,), (0,)), ((), ())), preferred_element_type=F32)


def _sigmoid(x):
    return 1.0 / (1.0 + jnp.exp(-x))


def _silu_and_grad(a):
    sig = _sigmoid(a)
    return a * sig, sig * (1.0 + a * (1.0 - sig))


_GELU_C = 0.7978845608028654


def _gelu_and_grad(x):
    inner = _GELU_C * (x + 0.044715 * x * x * x)
    t = jnp.tanh(inner)
    val = 0.5 * x * (1.0 + t)
    grad = 0.5 * (1.0 + t) + 0.5 * x * (1.0 - t * t) * _GELU_C * (1.0 + 3.0 * 0.044715 * x * x)
    return val, grad


def _ln_fwd(y, g, b):
    mu = jnp.mean(y, axis=-1, keepdims=True)
    yc = y - mu
    var = jnp.mean(yc * yc, axis=-1, keepdims=True)
    rstd = lax.rsqrt(var + LN_EPS)
    xhat = yc * rstd
    return xhat * g + b, xhat, rstd


def _ln_bwd(dh, xhat, rstd, g):
    dxh = dh * g
    m1 = jnp.mean(dxh, axis=-1, keepdims=True)
    m2 = jnp.mean(dxh * xhat, axis=-1, keepdims=True)
    dy = rstd * (dxh - m1 - xhat * m2)
    dg = jnp.sum(dh * xhat, axis=0, keepdims=True)
    db = jnp.sum(dh, axis=0, keepdims=True)
    return dy, dg, db


def _mask_dot(mask, x):
    hi = x.astype(BF16)
    lo = (x - hi.astype(F32)).astype(BF16)
    n = mask.shape[0]
    parts = [_dot(mask, hi[r:r + n, :]) + _dot(mask, lo[r:r + n, :]) for r in range(0, x.shape[0], n)]
    return parts[0] if len(parts) == 1 else jnp.concatenate(parts, axis=0)


def _block_masks(n):
    r = np.arange(n)[:, None]
    c = np.arange(n)[None, :]
    same = (r // HG_BLOCK) == (c // HG_BLOCK)
    return jnp.asarray(np.stack([same & (c <= r), same & (c >= r), same]), BF16)


def _row_tile(t):
    return min(t, 512)


def _col_tile(n):
    for cand in (512, 256, 128):
        if n % cand == 0:
            return cand
    return n


def _resident(w):
    return pl.BlockSpec(w.shape, lambda *_: (0, 0), pipeline_mode=pl.Buffered(1))


def _drop_deps(body, n_in, n_deps):
    if n_deps == 0:
        return body
    return lambda *refs: body(*refs[:n_in], *refs[n_in + n_deps:])


def _ffn_up(hb, wg, wu, name, deps=()):
    t, d = hb.shape
    f = wg.shape[1]
    tm = _row_tile(t)
    tn = _col_tile(f)

    def body(h_ref, wg_ref, wu_ref, a_ref, b_ref, s_ref):
        h = h_ref[...]
        for c in range(f // tn):
            cols = slice(c * tn, (c + 1) * tn)
            a = _dot(h, wg_ref[:, cols])
            b = _dot(h, wu_ref[:, cols])
            a_ref[:, cols] = a.astype(BF16)
            b_ref[:, cols] = b.astype(BF16)
            s_ref[:, cols] = (a * _sigmoid(a) * b).astype(BF16)

    act = pl.BlockSpec((tm, f), lambda i: (i, 0))
    return pl.pallas_call(
        _drop_deps(body, 3, len(deps)),
        grid=(t // tm,),
        in_specs=[pl.BlockSpec((tm, d), lambda i: (i, 0)), _resident(wg), _resident(wu)] + [ANY] * len(deps),
        out_specs=[act, act, act],
        out_shape=[jax.ShapeDtypeStruct((t, f), BF16)] * 3,
        compiler_params=_params(1),
        name=name,
    )(hb, wg, wu, *deps)


def _mm_res_ln(lhs, w, res, g, b, coef, name, target=None):
    t, kd = lhs.shape
    d = w.shape[1]
    tm = _row_tile(t)
    nt = t // tm
    from_norm = isinstance(res, tuple)
    n_res = 3 if from_norm else 1

    def body(*refs):
        l_ref, w_ref = refs[:2]
        r_refs = refs[2:2 + n_res]
        g_ref, b_ref = refs[2 + n_res:4 + n_res]
        rest = refs[4 + n_res:]
        prev = r_refs[0][...] * r_refs[1][...] + r_refs[2][...] if from_norm else r_refs[0][...]
        y = ALPHA * prev + coef * _dot(l_ref[...], w_ref[...])
        h, xhat, rstd = _ln_fwd(y, g_ref[...], b_ref[...])
        if target is None:
            hb_ref, xh_ref, rs_ref = rest
            hb_ref[...] = h.astype(BF16)
            xh_ref[...] = xhat
            rs_ref[...] = rstd
            return
        t_ref, loss_ref, dy_ref, dyb_ref, dg_ref, db_ref, lacc = rest
        i = pl.program_id(0)

        @pl.when(i == 0)
        def _():
            lacc[...] = jnp.zeros_like(lacc)
            dg_ref[...] = jnp.zeros_like(dg_ref)
            db_ref[...] = jnp.zeros_like(db_ref)

        err = h - t_ref[...]
        lacc[...] += jnp.sum(err * err, axis=0, keepdims=True)
        dy, dg, db = _ln_bwd(err * (1.0 / d), xhat, rstd, g_ref[...])
        dy_ref[...] = dy
        dyb_ref[...] = dy.astype(BF16)
        dg_ref[...] += dg
        db_ref[...] += db

        @pl.when(i == nt - 1)
        def _():
            loss_ref[...] = jnp.zeros_like(loss_ref) + jnp.sum(lacc[...], axis=1, keepdims=True) * (0.5 / d)

    row = pl.BlockSpec((tm, d), lambda i: (i, 0))
    vec = pl.BlockSpec((1, d), lambda i: (0, 0))
    res_specs = [row, vec, vec] if from_norm else [row]
    res_args = list(res) if from_norm else [res]
    in_specs = [pl.BlockSpec((tm, kd), lambda i: (i, 0)), _resident(w)] + res_specs + [vec, vec]
    args = [lhs, w] + res_args + [g, b]
    if target is None:
        out_specs = [row, row, pl.BlockSpec((tm, 1), lambda i: (i, 0))]
        out_shape = [jax.ShapeDtypeStruct((t, d), BF16), jax.ShapeDtypeStruct((t, d), F32),
                     jax.ShapeDtypeStruct((t, 1), F32)]
        scratch = []
    else:
        in_specs.append(row)
        args.append(target)
        out_specs = [pl.BlockSpec((1, LANES), lambda i: (0, 0)), row, row, vec, vec]
        out_shape = [jax.ShapeDtypeStruct((1, LANES), F32), jax.ShapeDtypeStruct((t, d), F32),
                     jax.ShapeDtypeStruct((t, d), BF16), jax.ShapeDtypeStruct((1, d), F32),
                     jax.ShapeDtypeStruct((1, d), F32)]
        scratch = [pltpu.VMEM((1, d), F32)]
    return pl.pallas_call(
        body,
        grid=(nt,),
        in_specs=in_specs,
        out_specs=out_specs,
        out_shape=out_shape,
        scratch_shapes=scratch,
        compiler_params=_params(1),
        name=name,
    )(*args)


def _store_slabs(o_ref, first, tile):
    for s in range(tile.shape[1] // LANES):
        o_ref[first + s] = tile[:, s * LANES:(s + 1) * LANES]


def _mm_nn(lhs, w, name):
    t, kd = lhs.shape
    n = w.shape[1]
    tm = _row_tile(t)
    tn = _col_tile(n)

    def body(l_ref, w_ref, o_ref):
        lhs_v = l_ref[...]
        for c in range(n // tn):
            _store_slabs(o_ref, c * (tn // LANES), _dot(lhs_v, w_ref[:, c * tn:(c + 1) * tn]))

    return pl.pallas_call(
        body,
        grid=(t // tm,),
        in_specs=[pl.BlockSpec((tm, kd), lambda i: (i, 0)), _resident(w)],
        out_specs=pl.BlockSpec((n // LANES, tm, LANES), lambda i: (0, i, 0)),
        out_shape=jax.ShapeDtypeStruct((n // LANES, t, LANES), F32),
        compiler_params=_params(1),
        name=name,
    )(lhs, w)


def _lower_bound(lg):
    m = jnp.max(lg, axis=0, keepdims=True)
    e = jnp.exp(lg - m)
    return e[0:1, :] / jnp.sum(e, axis=0, keepdims=True)


def _forget_terms(fz, lb):
    e = jnp.exp(-jnp.abs(fz))
    r = 1.0 / (1.0 + e)
    pos = fz >= 0.0
    sig = jnp.where(pos, r, e * r)
    nsig = jnp.where(pos, e * r, r)
    f = lb + (1.0 - lb) * sig
    k = (1.0 - lb) * nsig
    return sig, nsig, f, k


def _hg_tile(t):
    return min(t, 1024)


HG_HALF = HG_BLOCK // 2
NEG_BIG = -1e30


def _halves(a):
    return a[:HG_HALF, :], a[HG_HALF:, :]


def _causal_halves(s):
    return (0, 1) if s < HG_HALF else (1,)


def _decay_from(b_half, b_s, s, h, tidx):
    first = s - h * HG_HALF
    diff = b_half - b_s
    if first > 0:
        diff = jnp.where(tidx >= first, diff, NEG_BIG)
    return jnp.exp(diff)


def _hgrn_fwd(proj, logits, gn):
    t = proj.shape[1]
    ct = _hg_tile(t)
    nct = t // ct
    nblk = ct // HG_BLOCK
    nh = HG_HEADS
    mrows = min(ct, 256)

    def body(q_ref, fz_ref, iv_ref, gg_ref, lg_ref, gn_ref, mask_ref, oraw_ref, oa_ref, st_ref,
             state, qt_s, kt_s, k_s, b_s, dec_s):
        c = pl.program_id(1)

        @pl.when(c == 0)
        def _():
            state[...] = jnp.zeros_like(state)

        lb = _lower_bound(lg_ref[...])
        q = q_ref[...]
        _, _, f, k = _forget_terms(fz_ref[...], lb)
        logf = jnp.log(f)
        b = _mask_dot(mask_ref[0], logf)
        bend = _mask_dot(mask_ref[2], logf)
        qt_s[...] = (q * jnp.exp(b)).astype(BF16)
        kt_s[...] = (k * jnp.exp(bend - b)).astype(BF16)
        k_s[...] = k
        b_s[...] = b
        dec_s[...] = jnp.exp(bend)
        tidx = lax.broadcasted_iota(jnp.int32, (HG_HALF, HG_DIM), 0)

        def blk(i, carry):
            r0 = pl.multiple_of(i * HG_BLOCK, HG_BLOCK)
            rows = pl.ds(r0, HG_BLOCK)
            st = state[...]
            stb = st.astype(BF16)
            st_ref[i] = stb
            v = iv_ref[rows, :]
            qq = q_ref[rows, :]
            kk = k_s[rows, :]
            bb = b_s[rows, :]
            o = list(_halves(_dot_nt(qt_s[rows, :], stb)))
            qh, bh = _halves(qq), _halves(bb)
            for s in range(HG_BLOCK):
                ks, vs = kk[s:s + 1, :], v[s:s + 1, :]
                for h in _causal_halves(s):
                    e = _decay_from(bh[h], bb[s:s + 1, :], s, h, tidx)
                    acol = jnp.sum(qh[h] * (ks * e), axis=1, keepdims=True)
                    o[h] = o[h] + acol * vs
            oraw_ref[rows, :] = jnp.concatenate(o, axis=0)
            state[...] = st * dec_s[pl.ds(r0, 1), :] + _dot_tn(v.astype(BF16), kt_s[rows, :])
            return carry

        lax.fori_loop(0, nblk, blk, 0, unroll=HG_UNROLL)
        oraw = oraw_ref[...]
        r = lax.rsqrt(jnp.mean(oraw * oraw, axis=-1, keepdims=True) + LN_EPS)
        gg = gg_ref[...]
        oa_ref[...] = (oraw * r * gn_ref[...] * gg * _sigmoid(gg)).astype(BF16)

    def slab(off):
        return pl.BlockSpec((None, ct, HG_DIM), lambda h, c: (off + h, c, 0))

    return pl.pallas_call(
        body,
        grid=(nh, nct),
        in_specs=[slab(0), slab(nh), slab(2 * nh), slab(3 * nh),
                  pl.BlockSpec((None, 2, HG_DIM), lambda h, c: (h, 0, 0)),
                  pl.BlockSpec((1, HG_DIM), lambda h, c: (0, 0)),
                  pl.BlockSpec((3, mrows, mrows), lambda h, c: (0, 0, 0))],
        out_specs=[slab(0), pl.BlockSpec((ct, HG_DIM), lambda h, c: (c, h)),
                   pl.BlockSpec((None, nblk, HG_DIM, HG_DIM), lambda h, c: (h, c, 0, 0))],
        out_shape=[jax.ShapeDtypeStruct((nh, t, HG_DIM), F32),
                   jax.ShapeDtypeStruct((t, (nh + SG_GROUPS) * HG_DIM), BF16),
                   jax.ShapeDtypeStruct((nh, t // HG_BLOCK, HG_DIM, HG_DIM), BF16)],
        scratch_shapes=[pltpu.VMEM((HG_DIM, HG_DIM), F32), pltpu.VMEM((ct, HG_DIM), BF16),
                        pltpu.VMEM((ct, HG_DIM), BF16), pltpu.VMEM((ct, HG_DIM), F32),
                        pltpu.VMEM((ct, HG_DIM), F32), pltpu.VMEM((ct, HG_DIM), F32)],
        compiler_params=_params(2),
        name="hgrn_fwd",
    )(proj, proj, proj, proj, logits, gn, _block_masks(mrows))


def _sg_tile(t):
    return min(t, 512)


def _sgu_chunk_fwd(u, v, ln_g, ln_b, wm, bs):
    ua, dua = _gelu_and_grad(u)
    va, dva = _gelu_and_grad(v)
    vn, xhat, rstd = _ln_fwd(va, ln_g, ln_b)
    s = _dot(wm, vn.astype(BF16)) + bs
    return ua, dua, dva, vn, xhat, rstd, s


def _tril_weight(w):
    n = SG_CHUNK
    r = lax.broadcasted_iota(jnp.int32, (n, n), 0)
    c = lax.broadcasted_iota(jnp.int32, (n, n), 1)
    return jnp.where(c <= r, w, 0.0)


def _sgu_fwd(proj, mix, ln_g, ln_b, w_s, b_col):
    t = proj.shape[1]
    ct = _sg_tile(t)
    ng = SG_GROUPS
    wide = ng * SG_DIM
    blk_u = 4 * HG_HEADS // ng

    def body(u_ref, v_ref, g_ref, b_ref, w_ref, bs_ref, mix_ref, o_ref):
        del mix_ref
        for g in range(ng):
            lanes = slice(g * SG_DIM, (g + 1) * SG_DIM)
            wm = _tril_weight(w_ref[g]).astype(BF16)
            for n in range(ct // SG_CHUNK):
                rows = slice(n * SG_CHUNK, (n + 1) * SG_CHUNK)
                ua, _, _, _, _, _, s = _sgu_chunk_fwd(u_ref[g, rows, :], v_ref[g, rows, :], g_ref[g], b_ref[g], wm,
                                                      bs_ref[g])
                o_ref[rows, lanes] = (ua * s).astype(BF16)

    full = lambda a: pl.BlockSpec(a.shape, lambda c: (0,) * a.ndim)
    return pl.pallas_call(
        body,
        grid=(t // ct,),
        in_specs=[pl.BlockSpec((ng, ct, SG_DIM), lambda c: (blk_u, c, 0)),
                  pl.BlockSpec((ng, ct, SG_DIM), lambda c: (blk_u + 1, c, 0)),
                  full(ln_g), full(ln_b), full(w_s), full(b_col), ANY],
        out_specs=pl.BlockSpec((ct, wide), lambda c: (c, 1)),
        out_shape=jax.ShapeDtypeStruct(mix.shape, mix.dtype),
        input_output_aliases={6: 0},
        compiler_params=_params(1),
        name="sgu_fwd",
    )(proj, proj, ln_g, ln_b, w_s, b_col, mix)


def _mem_kv(mem, g, b, wk, wv):
    m_len, d = mem.shape

    def body(m_ref, g_ref, b_ref, wk_ref, wv_ref, mb_ref, xh_ref, rs_ref, k_ref, v_ref):
        m, xhat, rstd = _ln_fwd(m_ref[...], g_ref[...], b_ref[...])
        mb = m.astype(BF16)
        mb_ref[...] = mb
        xh_ref[...] = xhat
        rs_ref[...] = rstd
        k_ref[...] = _dot(mb, wk_ref[...]).astype(BF16)
        v_ref[...] = _dot(mb, wv_ref[...]).astype(BF16)

    return pl.pallas_call(
        body,
        out_shape=[jax.ShapeDtypeStruct((m_len, d), BF16), jax.ShapeDtypeStruct((m_len, d), F32),
                   jax.ShapeDtypeStruct((m_len, 1), F32), jax.ShapeDtypeStruct((m_len, d), BF16),
                   jax.ShapeDtypeStruct((m_len, d), BF16)],
        compiler_params=pltpu.CompilerParams(vmem_limit_bytes=VMEM_LIMIT_V7X),
        name="mem_kv",
    )(mem, g, b, wk, wv)


def _softmax_rows(s):
    m = jnp.max(s, axis=-1, keepdims=True)
    p = jnp.exp(s - m)
    return p / jnp.sum(p, axis=-1, keepdims=True)


def _attn_fwd(hb, wq, kb, vb):
    t, d = hb.shape
    tm = _row_tile(t)
    dh = d // X_HEADS
    scale = dh ** -0.5

    def body(h_ref, wq_ref, k_ref, v_ref, q_ref, o_ref):
        q = _dot(h_ref[...], wq_ref[...]).astype(BF16)
        q_ref[...] = q
        for hd in range(X_HEADS):
            sl = slice(hd * dh, (hd + 1) * dh)
            p = _softmax_rows(_dot_nt(q[:, sl], k_ref[:, sl]) * scale)
            o_ref[:, sl] = _dot(p.astype(BF16), v_ref[:, sl]).astype(BF16)

    row = pl.BlockSpec((tm, d), lambda i: (i, 0))
    full = lambda a: pl.BlockSpec(a.shape, lambda i: (0, 0))
    return pl.pallas_call(
        body,
        grid=(t // tm,),
        in_specs=[row, full(wq), full(kb), full(vb)],
        out_specs=[row, row],
        out_shape=[jax.ShapeDtypeStruct((t, d), BF16), jax.ShapeDtypeStruct((t, d), BF16)],
        compiler_params=_params(1),
        name="attn_fwd",
    )(hb, wq, kb, vb)


def _ffn_bwd_act(dyb, wd, a, b, coef, name, deps=()):
    t, d = dyb.shape
    f = wd.shape[0]
    tm = _row_tile(t)
    tn = _col_tile(f)

    def body(dy_ref, wd_ref, a_ref, b_ref, da_ref, db_ref):
        dy = dy_ref[...]
        for c in range(f // tn):
            cols = slice(c * tn, (c + 1) * tn)
            ds = _dot_nt(dy, wd_ref[cols, :]) * coef
            silu, dsilu = _silu_and_grad(a_ref[:, cols].astype(F32))
            da_ref[:, cols] = (ds * b_ref[:, cols].astype(F32) * dsilu).astype(BF16)
            db_ref[:, cols] = (ds * silu).astype(BF16)

    act = pl.BlockSpec((tm, f), lambda i: (i, 0))
    return pl.pallas_call(
        _drop_deps(body, 4, len(deps)),
        grid=(t // tm,),
        in_specs=[pl.BlockSpec((tm, d), lambda i: (i, 0)), _resident(wd), act, act] + [ANY] * len(deps),
        out_specs=[act, act],
        out_shape=[jax.ShapeDtypeStruct((t, f), BF16), jax.ShapeDtypeStruct((t, f), BF16)],
        compiler_params=_params(1),
        name=name,
    )(dyb, wd, a, b, *deps)


def _ffn_bwd_fused(dy, dyb, wd, wg, wu, a, b, coef, ln, name):
    t, d = dy.shape
    f = wd.shape[0]
    tm = min(t, 256)
    tn = _col_tile(f)

    def body(dy_ref, dyb_ref, wd_ref, wg_ref, wu_ref, a_ref, b_ref, xh_ref, rs_ref, g_ref,
             da_ref, db_ref, dyo_ref, dyob_ref, dg_ref, dbl_ref):
        dyb_v = dyb_ref[...]
        dh = ALPHA * dy_ref[...]
        for c in range(f // tn):
            cols = slice(c * tn, (c + 1) * tn)
            ds = _dot_nt(dyb_v, wd_ref[cols, :]) * coef
            silu, dsilu = _silu_and_grad(a_ref[:, cols].astype(F32))
            da = (ds * b_ref[:, cols].astype(F32) * dsilu).astype(BF16)
            db = (ds * silu).astype(BF16)
            da_ref[:, cols] = da
            db_ref[:, cols] = db
            dh = dh + _dot_nt(da, wg_ref[:, cols]) + _dot_nt(db, wu_ref[:, cols])

        @pl.when(pl.program_id(0) == 0)
        def _():
            dg_ref[...] = jnp.zeros_like(dg_ref)
            dbl_ref[...] = jnp.zeros_like(dbl_ref)

        dyp, dg, dbl = _ln_bwd(dh, xh_ref[...], rs_ref[...], g_ref[...])
        dyo_ref[...] = dyp
        dyob_ref[...] = dyp.astype(BF16)
        dg_ref[...] += dg
        dbl_ref[...] += dbl

    row = pl.BlockSpec((tm, d), lambda i: (i, 0))
    act = pl.BlockSpec((tm, f), lambda i: (i, 0))
    vec = pl.BlockSpec((1, d), lambda i: (0, 0))
    return pl.pallas_call(
        body,
        grid=(t // tm,),
        in_specs=[row, row, _resident(wd), _resident(wg), _resident(wu), act, act, row,
                  pl.BlockSpec((tm, 1), lambda i: (i, 0)), vec],
        out_specs=[act, act, row, row, vec, vec],
        out_shape=[jax.ShapeDtypeStruct((t, f), BF16), jax.ShapeDtypeStruct((t, f), BF16),
                   jax.ShapeDtypeStruct((t, d), F32), jax.ShapeDtypeStruct((t, d), BF16),
                   jax.ShapeDtypeStruct((1, d), F32), jax.ShapeDtypeStruct((1, d), F32)],
        compiler_params=_params(1),
        name=name,
    )(dy, dyb, wd, wg, wu, a, b, *ln)


def _mm_tn(a, b, name, scale=1.0, deps=()):
    t, m = a.shape
    n = b.shape[1]
    tt = _row_tile(t)
    nt = t // tt
    tm_o, tn_o = m, n

    def body(a_ref, b_ref, o_ref, acc):
        k = pl.program_id(2)

        @pl.when(k == 0)
        def _():
            acc[...] = jnp.zeros_like(acc)

        acc[...] += _dot_tn(a_ref[...], b_ref[...])

        @pl.when(k == nt - 1)
        def _():
            o_ref[...] = (acc[...] * scale).astype(BF16)

    return pl.pallas_call(
        _drop_deps(body, 2, len(deps)),
        grid=(m // tm_o, n // tn_o, nt),
        in_specs=[pl.BlockSpec((tt, tm_o), lambda i, j, k: (k, i)), pl.BlockSpec((tt, tn_o), lambda i, j, k: (k, j))]
        + [ANY] * len(deps),
        out_specs=pl.BlockSpec((tm_o, tn_o), lambda i, j, k: (i, j)),
        out_shape=jax.ShapeDtypeStruct((m, n), BF16),
        scratch_shapes=[pltpu.VMEM((tm_o, tn_o), F32)],
        compiler_params=_params(3),
        name=name,
    )(a, b, *deps)


def _mm_nt(lhs, w, name):
    t, d = lhs.shape
    kd = w.shape[0]
    tm = _row_tile(t)

    def body(l_ref, w_ref, o_ref):
        _store_slabs(o_ref, 0, _dot_nt(l_ref[...], w_ref[...]))

    return pl.pallas_call(
        body,
        grid=(t // tm,),
        in_specs=[pl.BlockSpec((tm, d), lambda i: (i, 0)), _resident(w)],
        out_specs=pl.BlockSpec((kd // LANES, tm, LANES), lambda i: (0, i, 0)),
        out_shape=jax.ShapeDtypeStruct((kd // LANES, t, LANES), F32),
        compiler_params=_params(1),
        name=name,
    )(lhs, w)


def _dx_ln(dy, pairs, ln, name, deps=()):
    t, d = dy.shape
    npair = len(pairs)
    tm = min(t, 512 // npair)
    nt = t // tm
    n_in = 1 + 2 * npair + (3 if ln is not None else 0)

    def body(*refs):
        dy_ref = refs[0]
        pr = refs[1:1 + 2 * npair]
        pos = 1 + 2 * npair
        dh = ALPHA * dy_ref[...]
        for p in range(npair):
            dh = dh + _dot_nt(pr[2 * p][...], pr[2 * p + 1][...])
        if ln is not None:
            xh_ref, rs_ref, g_ref = refs[pos:pos + 3]
            dyo_ref, dyb_ref, dg_ref, db_ref = refs[pos + 3:pos + 7]

            @pl.when(pl.program_id(0) == 0)
            def _():
                dg_ref[...] = jnp.zeros_like(dg_ref)
                db_ref[...] = jnp.zeros_like(db_ref)

            dyp, dg, db = _ln_bwd(dh, xh_ref[...], rs_ref[...], g_ref[...])
            dyo_ref[...] = dyp
            dyb_ref[...] = dyp.astype(BF16)
            dg_ref[...] += dg
            db_ref[...] += db
        else:
            refs[pos][...] = dh

    row = pl.BlockSpec((tm, d), lambda i: (i, 0))
    vec = pl.BlockSpec((1, d), lambda i: (0, 0))
    in_specs = [row]
    args = [dy]
    for lhs, w in pairs:
        in_specs += [pl.BlockSpec((tm, lhs.shape[1]), lambda i: (i, 0)), _resident(w)]
        args += [lhs, w]
    if ln is not None:
        in_specs += [row, pl.BlockSpec((tm, 1), lambda i: (i, 0)), vec]
        args += list(ln)
        out_specs = [row, row, vec, vec]
        out_shape = [jax.ShapeDtypeStruct((t, d), F32), jax.ShapeDtypeStruct((t, d), BF16),
                     jax.ShapeDtypeStruct((1, d), F32), jax.ShapeDtypeStruct((1, d), F32)]
    else:
        out_specs = row
        out_shape = jax.ShapeDtypeStruct((t, d), F32)
    return pl.pallas_call(
        _drop_deps(body, n_in, len(deps)),
        grid=(nt,),
        in_specs=in_specs + [ANY] * len(deps),
        out_specs=out_specs,
        out_shape=out_shape,
        compiler_params=_params(1),
        name=name,
    )(*args, *deps)


def _hgrn_bwd(proj, oraw, dmix, states, logits, gn):
    t = proj.shape[1]
    ct = _hg_tile(t)
    nct = t // ct
    nblk = ct // HG_BLOCK
    nh = HG_HEADS
    mrows = min(ct, 256)

    def body(q_ref, fz_ref, iv_ref, gg_ref, or_ref, do_ref, st_ref, lg_ref, gn_ref, mask_ref,
             dq_ref, dfz_ref, div_ref, dgg_ref, dlg_ref, dgn_ref,
             dstate, qt_s, kt_s, k_s, b_s, eb_s, ekb_s, dec_s, dor_s, dbl_s, gr_s, dk_s, dlb_acc):
        c = pl.program_id(1)

        @pl.when(c == 0)
        def _():
            dstate[...] = jnp.zeros_like(dstate)
            dlb_acc[...] = jnp.zeros_like(dlb_acc)
            dgn_ref[...] = jnp.zeros_like(dgn_ref)

        lb = _lower_bound(lg_ref[...])
        q = q_ref[...]
        sig, nsig, f, k = _forget_terms(fz_ref[...], lb)
        logf = jnp.log(f)
        b = _mask_dot(mask_ref[0], logf)
        bend = _mask_dot(mask_ref[2], logf)
        eb = jnp.exp(b)
        ekb = jnp.exp(bend - b)
        qt_s[...] = (q * eb).astype(BF16)
        kt_s[...] = (k * ekb).astype(BF16)
        k_s[...] = k
        b_s[...] = b
        eb_s[...] = eb
        ekb_s[...] = ekb
        dec_s[...] = jnp.exp(bend)
        oraw = or_ref[...]
        r = lax.rsqrt(jnp.mean(oraw * oraw, axis=-1, keepdims=True) + LN_EPS)
        on = oraw * r
        gg = gg_ref[...]
        silu, dsilu = _silu_and_grad(gg)
        doa = do_ref[...]
        gnv = gn_ref[...]
        dgg_ref[...] = (doa * on * gnv * dsilu).astype(BF16)
        dyn = doa * silu
        dgn_ref[...] += jnp.sum(dyn * on, axis=0, keepdims=True)
        don = dyn * gnv
        dor_s[...] = r * (don - on * jnp.mean(don * on, axis=-1, keepdims=True))
        tidx = lax.broadcasted_iota(jnp.int32, (HG_HALF, HG_DIM), 0)

        def blk(ii, carry):
            i = nblk - 1 - ii
            r0 = pl.multiple_of(i * HG_BLOCK, HG_BLOCK)
            rows = pl.ds(r0, HG_BLOCK)
            st = st_ref[i]
            dst = dstate[...]
            dstb = dst.astype(BF16)
            do = dor_s[rows, :]
            dob = do.astype(BF16)
            v = iv_ref[rows, :]
            vb = v.astype(BF16)
            qq = q_ref[rows, :]
            kk = k_s[rows, :]
            bb = b_s[rows, :]
            qt = qt_s[rows, :]
            kt = kt_s[rows, :]
            dec = dec_s[pl.ds(r0, 1), :]
            dkt = _dot(vb, dstb)
            dq = _dot(dob, st) * eb_s[rows, :]
            dk = dkt * ekb_s[rows, :]
            dv = _dot_nt(kt, dstb)
            gend = (jnp.sum(kk * dk, axis=0, keepdims=True)
                    + dec * jnp.sum(dst * st.astype(F32), axis=0, keepdims=True))
            qh, bh, doh = _halves(qq), _halves(bb), _halves(do)
            dqh, dkh, dvh = list(_halves(dq)), list(_halves(dk)), list(_halves(dv))
            for s in range(HG_BLOCK):
                ks, vs = kk[s:s + 1, :], v[s:s + 1, :]
                dk_part = dv_part = None
                for h in _causal_halves(s):
                    e = _decay_from(bh[h], bb[s:s + 1, :], s, h, tidx)
                    ke = ks * e
                    acol = jnp.sum(qh[h] * ke, axis=1, keepdims=True)
                    dacol = jnp.sum(doh[h] * vs, axis=1, keepdims=True)
                    dqh[h] = dqh[h] + dacol * ke
                    pk = dacol * (qh[h] * e)
                    pv = acol * doh[h]
                    dk_part = pk if dk_part is None else dk_part + pk
                    dv_part = pv if dv_part is None else dv_part + pv
                hs, row = divmod(s, HG_HALF)
                dkh[hs] = dkh[hs] + jnp.where(tidx == row, jnp.sum(dk_part, axis=0, keepdims=True), 0.0)
                dvh[hs] = dvh[hs] + jnp.where(tidx == row, jnp.sum(dv_part, axis=0, keepdims=True), 0.0)
            dq = jnp.concatenate(dqh, axis=0)
            dk = jnp.concatenate(dkh, axis=0)
            dv = jnp.concatenate(dvh, axis=0)
            dq_ref[rows, :] = dq.astype(BF16)
            div_ref[rows, :] = dv.astype(BF16)
            dk_s[rows, :] = dk
            dbl_s[rows, :] = qq * dq - kk * dk
            gr_s[rows, :] = jnp.zeros((HG_BLOCK, HG_DIM), F32) + gend
            dstate[...] = dst * dec + _dot_tn(dob, qt)
            return carry

        lax.fori_loop(0, nblk, blk, 0, unroll=HG_UNROLL)
        dlogf = _mask_dot(mask_ref[1], dbl_s[...]) + gr_s[...]
        dk = dk_s[...]
        dfz_ref[...] = ((dlogf / f - dk) * ((1.0 - lb) * sig * nsig)).astype(BF16)
        dlb_acc[...] += jnp.sum((dlogf / f - dk) * nsig, axis=0, keepdims=True)

        @pl.when(c == nct - 1)
        def _():
            dl0 = dlb_acc[...] * lb * (1.0 - lb)
            layer = lax.broadcasted_iota(jnp.int32, (2, HG_DIM), 0)
            dlg_ref[...] = jnp.where(layer == 0, dl0, -dl0)

    def slab(off):
        return pl.BlockSpec((None, ct, HG_DIM), lambda h, c: (off + h, nct - 1 - c, 0))

    out_slab = pl.BlockSpec((ct, HG_DIM), lambda h, c: (nct - 1 - c, h))
    tile_f32 = pltpu.VMEM((ct, HG_DIM), F32)
    tile_b16 = pltpu.VMEM((ct, HG_DIM), BF16)
    slab_shape = jax.ShapeDtypeStruct((t, nh * HG_DIM), BF16)
    return pl.pallas_call(
        body,
        grid=(nh, nct),
        in_specs=[slab(0), slab(nh), slab(2 * nh), slab(3 * nh), slab(0), slab(0),
                  pl.BlockSpec((None, nblk, HG_DIM, HG_DIM), lambda h, c: (h, nct - 1 - c, 0, 0)),
                  pl.BlockSpec((None, 2, HG_DIM), lambda h, c: (h, 0, 0)),
                  pl.BlockSpec((1, HG_DIM), lambda h, c: (0, 0)),
                  pl.BlockSpec((3, mrows, mrows), lambda h, c: (0, 0, 0))],
        out_specs=[out_slab, out_slab, out_slab, out_slab,
                   pl.BlockSpec((None, 2, HG_DIM), lambda h, c: (h, 0, 0)),
                   pl.BlockSpec((None, 1, HG_DIM), lambda h, c: (h, 0, 0))],
        out_shape=[slab_shape, slab_shape, slab_shape, slab_shape,
                   jax.ShapeDtypeStruct((nh, 2, HG_DIM), F32), jax.ShapeDtypeStruct((nh, 1, HG_DIM), F32)],
        scratch_shapes=[pltpu.VMEM((HG_DIM, HG_DIM), F32), tile_b16, tile_b16, tile_f32, tile_f32, tile_f32, tile_f32,
                        tile_f32, tile_f32, tile_f32, tile_f32, tile_f32, pltpu.VMEM((1, HG_DIM), F32)],
        compiler_params=_params(2),
        name="hgrn_bwd",
    )(proj, proj, proj, proj, oraw, dmix, states, logits, gn, _block_masks(mrows))


def _sgu_bwd(proj, dmix, ln_g, ln_b, w_s, w_t, b_col):
    t = proj.shape[1]
    ct = _sg_tile(t)
    nct = t // ct
    ng = SG_GROUPS
    off_u = 4 * HG_HEADS
    off_v = off_u + ng
    n = SG_CHUNK

    def body(u_ref, v_ref, do_ref, g_ref, b_ref, w_ref, wt_ref, bs_ref, du_ref, dv_ref, dg_ref, db_ref, dw_ref, dbs_ref):
        c = pl.program_id(1)

        @pl.when(c == 0)
        def _():
            dg_ref[...] = jnp.zeros_like(dg_ref)
            db_ref[...] = jnp.zeros_like(db_ref)
            dw_ref[...] = jnp.zeros_like(dw_ref)
            dbs_ref[...] = jnp.zeros_like(dbs_ref)

        r = lax.broadcasted_iota(jnp.int32, (n, n), 0)
        cc = lax.broadcasted_iota(jnp.int32, (n, n), 1)
        wm = jnp.where(cc <= r, w_ref[...], 0.0).astype(BF16)
        wmt = jnp.where(r <= cc, wt_ref[...], 0.0).astype(BF16)
        for ci in range(ct // n):
            rows = slice(ci * n, (ci + 1) * n)
            ua, dua, dva, vn, xhat, rstd, s = _sgu_chunk_fwd(u_ref[rows, :], v_ref[rows, :], g_ref[...], b_ref[...],
                                                             wm, bs_ref[...])
            do = do_ref[rows, :]
            du_ref[rows, :] = (do * s * dua).astype(BF16)
            ds = do * ua
            dsb = ds.astype(BF16)
            dbs_ref[...] += jnp.sum(ds, axis=1, keepdims=True)
            dw_ref[...] += _dot_nt(dsb, vn.astype(BF16))
            dvn = _dot(wmt, dsb)
            dva_in, dg, db = _ln_bwd(dvn, xhat, rstd, g_ref[...])
            dg_ref[...] += dg
            db_ref[...] += db
            dv_ref[rows, :] = (dva_in * dva).astype(BF16)

        @pl.when(c == nct - 1)
        def _():
            dw_ref[...] = jnp.where(cc <= r, dw_ref[...], 0.0)

    vec = pl.BlockSpec((None, 1, SG_DIM), lambda g, c: (g, 0, 0))
    mat = pl.BlockSpec((None, n, n), lambda g, c: (g, 0, 0))
    col = pl.BlockSpec((None, n, 1), lambda g, c: (g, 0, 0))
    out_slab = pl.BlockSpec((ct, SG_DIM), lambda g, c: (c, g))
    return pl.pallas_call(
        body,
        grid=(ng, nct),
        in_specs=[pl.BlockSpec((None, ct, SG_DIM), lambda g, c: (off_u + g, c, 0)),
                  pl.BlockSpec((None, ct, SG_DIM), lambda g, c: (off_v + g, c, 0)),
                  pl.BlockSpec((None, ct, SG_DIM), lambda g, c: (ng + g, c, 0)), vec, vec, mat, mat, col],
        out_specs=[out_slab, out_slab, vec, vec, mat, col],
        out_shape=[jax.ShapeDtypeStruct((t, ng * SG_DIM), BF16), jax.ShapeDtypeStruct((t, ng * SG_DIM), BF16),
                   jax.ShapeDtypeStruct((ng, 1, SG_DIM), F32), jax.ShapeDtypeStruct((ng, 1, SG_DIM), F32),
                   jax.ShapeDtypeStruct((ng, n, n), F32), jax.ShapeDtypeStruct((ng, n, 1), F32)],
        compiler_params=_params(2),
        name="sgu_bwd",
    )(proj, proj, dmix, ln_g, ln_b, w_s, w_t, b_col)


def _attn_bwd(dyb, wo, qb, kb, vb):
    t, d = dyb.shape
    m_len = kb.shape[0]
    tm = _row_tile(t)
    dh = d // X_HEADS
    scale = dh ** -0.5

    def body(dy_ref, wo_ref, q_ref, k_ref, v_ref, dq_ref, dk_ref, dv_ref):
        i = pl.program_id(0)

        @pl.when(i == 0)
        def _():
            dk_ref[...] = jnp.zeros_like(dk_ref)
            dv_ref[...] = jnp.zeros_like(dv_ref)

        do = _dot_nt(dy_ref[...], wo_ref[...]).astype(BF16)
        for hd in range(X_HEADS):
            sl = slice(hd * dh, (hd + 1) * dh)
            qh = q_ref[:, sl]
            p = _softmax_rows(_dot_nt(qh, k_ref[:, sl]) * scale)
            doh = do[:, sl]
            dp = _dot_nt(doh, v_ref[:, sl])
            ds = (p * (dp - jnp.sum(dp * p, axis=-1, keepdims=True)) * scale).astype(BF16)
            dq_ref[:, sl] = _dot(ds, k_ref[:, sl]).astype(BF16)
            dk_ref[:, sl] += _dot_tn(ds, qh)
            dv_ref[:, sl] += _dot_tn(p.astype(BF16), doh)

    row = pl.BlockSpec((tm, d), lambda i: (i, 0))
    full = lambda a: pl.BlockSpec(a.shape, lambda i: (0, 0))
    kv = pl.BlockSpec((m_len, d), lambda i: (0, 0))
    return pl.pallas_call(
        body,
        grid=(t // tm,),
        in_specs=[row, full(wo), row, full(kb), full(vb)],
        out_specs=[row, kv, kv],
        out_shape=[jax.ShapeDtypeStruct((t, d), BF16), jax.ShapeDtypeStruct((m_len, d), F32),
                   jax.ShapeDtypeStruct((m_len, d), F32)],
        compiler_params=_params(1),
        name="attn_bwd",
    )(dyb, wo, qb, kb, vb)


def _mem_bwd(dk, dv, mb, xhat, rstd, g, wk, wv):
    m_len, d = dk.shape

    def body(dk_ref, dv_ref, mb_ref, xh_ref, rs_ref, g_ref, wk_ref, wv_ref, gwk_ref, gwv_ref, dg_ref, db_ref):
        dkb = dk_ref[...].astype(BF16)
        dvb = dv_ref[...].astype(BF16)
        mb_v = mb_ref[...]
        gwk_ref[...] = _dot_tn(mb_v, dkb).astype(BF16)
        gwv_ref[...] = _dot_tn(mb_v, dvb).astype(BF16)
        dm = _dot_nt(dkb, wk_ref[...]) + _dot_nt(dvb, wv_ref[...])
        _, dg, db = _ln_bwd(dm, xh_ref[...], rs_ref[...], g_ref[...])
        dg_ref[...] = dg
        db_ref[...] = db

    return pl.pallas_call(
        body,
        out_shape=[jax.ShapeDtypeStruct((d, d), BF16), jax.ShapeDtypeStruct((d, d), BF16),
                   jax.ShapeDtypeStruct((1, d), F32), jax.ShapeDtypeStruct((1, d), F32)],
        compiler_params=pltpu.CompilerParams(vmem_limit_bytes=VMEM_LIMIT_V7X),
        name="mem_bwd",
    )(dk, dv, mb, xhat, rstd, g, wk, wv)


def _adamw(w, g, m, v):
    m = ADAM_B1 * m + (1.0 - ADAM_B1) * g
    v = ADAM_B2 * v + (1.0 - ADAM_B2) * (g * g)
    m_hat = m / (1.0 - ADAM_B1 ** ADAM_STEP)
    v_hat = v / (1.0 - ADAM_B2 ** ADAM_STEP)
    delta = -ADAM_LR * (m_hat / (jnp.sqrt(v_hat) + ADAM_EPS) + ADAM_WD * w)
    return delta, m, v


def _slot_sum(ref):
    g = ref[0].astype(F32)
    for s in range(1, N_DEV):
        g = g + ref[s].astype(F32)
    return g


def _adam_sharded(lands, w, m, v, axis, name):
    rows, cols = w.shape
    nl = len(lands)
    transposed = axis == 1 and nl == 2
    if transposed:
        rows, cols = cols, rows
        tr = 256
        grid = (rows // tr,)
        wblk = pl.BlockSpec((cols, tr), lambda i: (0, i))
        lblk = [pl.BlockSpec((N_DEV, tr, a.shape[2]), lambda i: (0, i, 0)) for a in lands]
    elif axis == 1:
        tr = 256 if rows % 256 == 0 else rows
        grid = (rows // tr,)
        wblk = pl.BlockSpec((tr, cols), lambda i: (i, 0))
        lblk = [pl.BlockSpec((N_DEV, tr, a.shape[2]), lambda i: (0, i, 0)) for a in lands]
    else:
        tc = _col_tile(cols)
        grid = (cols // tc,)
        wblk = pl.BlockSpec((rows, tc), lambda i: (0, i))
        lblk = [pl.BlockSpec((N_DEV, a.shape[1], tc), lambda i: (0, 0, i)) for a in lands]

    def body(*refs):
        w_ref, m_ref, v_ref = refs[nl:nl + 3]
        g_ref, d_ref, nm_ref, nv_ref = refs[nl + 3:]
        g = _slot_sum(refs[0])
        if nl == 2:
            tail = _slot_sum(refs[1])
            if transposed:
                g = jnp.concatenate([g.T, tail.T[:cols - g.shape[1], :]], axis=0)
            elif axis == 1:
                g = jnp.concatenate([g, tail[:, :cols - g.shape[1]]], axis=1)
            else:
                g = jnp.concatenate([g, tail[:rows - g.shape[0], :]], axis=0)
        delta, nm, nv = _adamw(w_ref[...], g, m_ref[...], v_ref[...])
        g_ref[...] = g
        d_ref[...] = delta
        nm_ref[...] = nm
        nv_ref[...] = nv

    shp = jax.ShapeDtypeStruct(w.shape, F32)
    return pl.pallas_call(
        body,
        grid=grid,
        in_specs=lblk + [wblk, wblk, wblk],
        out_specs=[wblk, wblk, wblk, wblk],
        out_shape=[shp, shp, shp, shp],
        compiler_params=_params(1),
        name=name,
    )(*[pltpu.with_memory_space_constraint(a, pltpu.HBM) for a in (*lands, w, m, v)])


def _mesh_pos():
    return lax.axis_index("x"), lax.axis_index("y"), lax.axis_index("c")


def _peer(k):
    x, y, c = _mesh_pos()
    pos = (x ^ (k >> 2), y ^ ((k >> 1) & 1), c ^ (k & 1))
    return pos, 4 * pos[0] + 2 * pos[1] + pos[2]


def _sem_index(row, k):
    return row * (N_DEV - 1) + k - 1


def _window(ref, axis, start, size):
    align = 16 if axis == 0 else LANES
    start = pl.multiple_of(start, align)
    return ref.at[pl.ds(start, size), :] if axis == 0 else ref.at[:, pl.ds(start, size)]


def _piece_refs(piece, srcs, lands, me, peer):
    kind, si, li, axis, base, stride, shape = piece
    if kind == "gather":
        return srcs[si], _window(lands[li], axis, base + stride * me, shape[axis])
    return _window(srcs[si], axis, base + stride * peer, shape[axis]), lands[li].at[me]


def _place_own(srcs, land_shapes, pieces, name):
    ns, nl, npc = len(srcs), len(land_shapes), len(pieces)

    def body(*refs):
        s_refs = refs[:ns]
        l_refs = refs[ns:ns + nl]
        bufs = refs[ns + nl:ns + nl + npc]
        sems = refs[ns + nl + npc]
        x, y, c = _mesh_pos()
        me = 4 * x + 2 * y + c
        loads = []
        for p, piece in enumerate(pieces):
            src, dst = _piece_refs(piece, s_refs, l_refs, me, me)
            cp = pltpu.make_async_copy(src, bufs[p], sems.at[0, p])
            cp.start()
            loads.append((cp, dst))
        stores = []
        for p, (cp, dst) in enumerate(loads):
            cp.wait()
            out = pltpu.make_async_copy(bufs[p], dst, sems.at[1, p])
            out.start()
            stores.append(out)
        for out in stores:
            out.wait()

    out = pl.pallas_call(
        body,
        in_specs=[ANY] * ns,
        out_specs=[ANY] * nl,
        out_shape=list(land_shapes),
        scratch_shapes=[pltpu.VMEM(pc[6], srcs[pc[1]].dtype) for pc in pieces] + [pltpu.SemaphoreType.DMA((2, npc))],
        compiler_params=pltpu.CompilerParams(vmem_limit_bytes=VMEM_LIMIT_V7X),
        name=name,
    )(*srcs)
    return list(out)


def _comm_start(srcs, lands, pieces, groups, name, after=()):
    ns, nl, na, ng = len(srcs), len(lands), len(after), len(groups)

    def body(*refs):
        s_refs = refs[:ns]
        l_refs = refs[ns:ns + nl]
        outs = refs[ns + nl + na:]
        sems = outs[:2 * ng]
        token = outs[-1]
        x, y, c = _mesh_pos()
        me = 4 * x + 2 * y + c
        for g, members in enumerate(groups):
            for row, p in enumerate(members):
                for k in range(1, N_DEV):
                    pos, peer = _peer(k)
                    src, dst = _piece_refs(pieces[p], s_refs, l_refs, me, peer)
                    pltpu.make_async_remote_copy(src_ref=src, dst_ref=dst, send_sem=sems[2 * g].at[_sem_index(row, k)],
                                                 recv_sem=sems[2 * g + 1].at[_sem_index(row, k)], device_id=pos,
                                                 device_id_type=MESH_ID).start()
        token[...] = jnp.zeros_like(token)

    sem_shapes = []
    for members in groups:
        sem_shapes += [pltpu.SemaphoreType.DMA((len(members) * (N_DEV - 1),))] * 2
    hbm_of = lambda a: pltpu.HBM(a.shape, a.dtype)
    out = pl.pallas_call(
        body,
        in_specs=[HBM] * (ns + nl) + [ANY] * na,
        out_specs=[SEM] * (2 * ng) + [HBM] * (ns + nl) + [pl.BlockSpec(memory_space=pltpu.VMEM)],
        out_shape=sem_shapes + [hbm_of(a) for a in srcs] + [hbm_of(a) for a in lands]
        + [jax.ShapeDtypeStruct((8, LANES), F32)],
        input_output_aliases={i: 2 * ng + i for i in range(ns + nl)},
        compiler_params=pltpu.CompilerParams(has_side_effects=DATAFLOW),
        name=name,
    )(*[pltpu.with_memory_space_constraint(a, pltpu.HBM) for a in list(srcs) + list(lands)], *after)
    sems = [(out[2 * g], out[2 * g + 1]) for g in range(ng)]
    return sems, list(out[2 * ng:2 * ng + ns]), list(out[2 * ng + ns:2 * ng + ns + nl]), out[-1]


def _comm_wait(srcs, lands, pieces, members, sems, after, name):
    ns, nl, na = len(srcs), len(lands), len(after)

    def body(*refs):
        s_refs = refs[:ns]
        l_refs = refs[ns:ns + nl]
        send_sems, recv_sems = refs[ns + nl:ns + nl + 2]
        x, y, c = _mesh_pos()
        me = 4 * x + 2 * y + c
        for row, p in enumerate(members):
            for k in range(1, N_DEV):
                pos, peer = _peer(k)
                src, dst = _piece_refs(pieces[p], s_refs, l_refs, me, peer)
                cp = pltpu.make_async_remote_copy(src_ref=src, dst_ref=dst, send_sem=send_sems.at[_sem_index(row, k)],
                                                  recv_sem=recv_sems.at[_sem_index(row, k)], device_id=pos,
                                                  device_id_type=MESH_ID)
                cp.wait_send()
                cp.wait_recv()

    hbm_of = lambda a: pltpu.HBM(a.shape, a.dtype)
    out = pl.pallas_call(
        body,
        in_specs=[HBM] * (ns + nl) + [SEM, SEM] + [ANY] * na,
        out_specs=[HBM] * (ns + nl),
        out_shape=[hbm_of(a) for a in srcs] + [hbm_of(a) for a in lands],
        input_output_aliases={i: i for i in range(ns + nl)},
        compiler_params=pltpu.CompilerParams(has_side_effects=DATAFLOW),
        name=name,
    )(*srcs, *lands, sems[0], sems[1], *after)
    return list(out[ns:])


def _landed_block(piece, lands, owner):
    _, _, li, axis, base, stride, shape = piece
    return _window(lands[li], axis, base + stride * owner, shape[axis])


def _copy_stage(name, bufs, in_sems, out_sem_sizes, emit, after=()):
    nb, ni, no, na = len(bufs), len(in_sems), len(out_sem_sizes), len(after)

    def body(*refs):
        b_refs = refs[:nb]
        i_refs = refs[nb:nb + ni]
        o_refs = refs[nb + ni + na:nb + ni + na + no]
        emit(b_refs, i_refs, o_refs)
        refs[-1][...] = jnp.zeros_like(refs[-1])

    hbm_of = lambda a: pltpu.HBM(a.shape, a.dtype)
    out = pl.pallas_call(
        body,
        in_specs=[HBM] * nb + [SEM] * ni + [ANY] * na,
        out_specs=[SEM] * no + [HBM] * nb + [pl.BlockSpec(memory_space=pltpu.VMEM)],
        out_shape=[pltpu.SemaphoreType.DMA((n,)) for n in out_sem_sizes] + [hbm_of(a) for a in bufs]
        + [jax.ShapeDtypeStruct((8, LANES), F32)],
        input_output_aliases={i: no + i for i in range(nb)},
        compiler_params=pltpu.CompilerParams(has_side_effects=DATAFLOW),
        name=name,
    )(*[pltpu.with_memory_space_constraint(a, pltpu.HBM) for a in bufs], *in_sems, *after)
    return list(out[:no]), list(out[no:no + nb]), out[-1]


def _remote(src, dst, send, recv, to):
    return pltpu.make_async_remote_copy(src_ref=src, dst_ref=dst, send_sem=send, recv_sem=recv, device_id=to,
                                        device_id_type=MESH_ID)


def _routed_gather(srcs, lands, pieces, after, name):
    ns, npc = len(srcs), len(pieces)

    def places():
        x, y, c = _mesh_pos()
        index = lambda p: 4 * p[0] + 2 * p[1] + p[2]
        me, sib = (x, y, c), (x, y, 1 - c)
        xnb, ynb = (1 - x, y, c), (x, 1 - y, c)
        got_first = (x ^ (1 - c), y ^ c, c)
        pass_to = (x ^ c, y ^ (1 - c), c)
        diag = (1 - x, 1 - y, c)
        return index, me, sib, xnb, ynb, got_first, pass_to, diag

    def start(b, _, o):
        index, me, sib, xnb, ynb, *_rest = places()
        send_a, recv_sib, recv_nb = o
        for p, piece in enumerate(pieces):
            src, dst = _piece_refs(piece, b[:ns], b[ns:], index(me), 0)
            _remote(src, dst, send_a.at[3 * p], recv_sib.at[p], sib).start()
            _remote(src, dst, send_a.at[3 * p + 1], recv_nb.at[2 * p], xnb).start()
            _remote(src, dst, send_a.at[3 * p + 2], recv_nb.at[2 * p + 1], ynb).start()

    def pass_a(b, i, o):
        index, me, sib, xnb, ynb, got_first, pass_to, _diag = places()
        (recv_nb,) = i
        send_f, recv_f, send_d, recv_d = o
        for p, piece in enumerate(pieces):
            for j, nb in enumerate((xnb, ynb)):
                blk = _landed_block(piece, b, index(nb))
                _remote(blk, blk, send_f.at[2 * p + j], recv_nb.at[2 * p + j], sib).wait_recv()
                _remote(blk, blk, send_f.at[2 * p + j], recv_f.at[2 * p + j], sib).start()
            blk = _landed_block(piece, b, index(got_first))
            _remote(blk, blk, send_d.at[p], recv_d.at[p], pass_to).start()

    def pass_b(b, i, o):
        index, me, sib, *_mid, diag = places()
        (recv_d,) = i
        send_g, recv_g = o
        for p, piece in enumerate(pieces):
            blk = _landed_block(piece, b, index(diag))
            _remote(blk, blk, send_g.at[p], recv_d.at[p], sib).wait_recv()
            _remote(blk, blk, send_g.at[p], recv_g.at[p], sib).start()

    def last(b, i, _):
        index, me, sib, *_others = places()
        send_a, recv_sib, send_f, recv_f, send_d, send_g, recv_g = i
        for p, piece in enumerate(pieces):
            src, dst = _piece_refs(piece, b[:ns], b[ns:], index(me), 0)
            cp = lambda s_sem, r_sem: _remote(src, dst, s_sem, r_sem, sib)
            cp(send_a.at[3 * p], recv_sib.at[p]).wait_recv()
            cp(send_a.at[3 * p], recv_g.at[p]).wait_recv()
            for j in range(3):
                cp(send_a.at[3 * p + j], recv_sib.at[p]).wait_send()
            for j in range(2):
                cp(send_f.at[2 * p + j], recv_f.at[2 * p + j]).wait_recv()
                cp(send_f.at[2 * p + j], recv_f.at[2 * p + j]).wait_send()
            cp(send_d.at[p], recv_sib.at[p]).wait_send()
            cp(send_g.at[p], recv_sib.at[p]).wait_send()

    (send_a, recv_sib, recv_nb), bufs, _ = _copy_stage(name + "_start", list(srcs) + list(lands), [],
                                                       [3 * npc, npc, 2 * npc], start)
    srcs, lands = bufs[:ns], bufs[ns:]
    (send_f, recv_f, send_d, recv_d), lands, _ = _copy_stage(name + "_pass_a", lands, [recv_nb],
                                                             [2 * npc, 2 * npc, npc, npc],
                                                             lambda b, i, o: pass_a(b, i, o), after=after)
    (send_g, recv_g), lands, tok = _copy_stage(name + "_pass_b", lands, [recv_d], [npc, npc], pass_b)
    _, bufs, _ = _copy_stage(name + "_last", list(srcs) + list(lands),
                             [send_a, recv_sib, send_f, recv_f, send_d, send_g, recv_g], [], last)
    return bufs[ns:], tok


_SMALL_NAMES = ("ln1_g", "ln1_b", "hg_lb_logits", "hg_norm_g", "sg_ln_g", "sg_ln_b", "sg_w_s", "sg_b_s",
                "ln2_g", "ln2_b", "mem_ln_g", "mem_ln_b", "ln3_g", "ln3_b", "ln4_g", "ln4_b")


_VEC_NAMES = ("ln1_g", "ln1_b", "ln2_g", "ln2_b", "mem_ln_g", "mem_ln_b", "ln3_g", "ln3_b", "ln4_g", "ln4_b")
_ROW_NAMES = ("hg_lb_logits", "hg_norm_g", "sg_ln_g", "sg_ln_b", "sg_b_s", "sg_w_s")
VEC_ROWS = 16


def _row_plan(shapes):
    plan, pos = {}, 0
    for k in _ROW_NAMES:
        shp = shapes[k]
        slabs, off = [], pos
        for idx in itertools.product(*[range(dim) for dim in shp[:-2]]):
            slabs.append((idx, off, shp[-2]))
            off += shp[-2]
        plan[k] = (pos, slabs)
        pos = -(-off // 8) * 8
    return plan, -(-pos // 16) * 16


def _pack_small_grads(gs, shapes, loss):
    d = gs[_VEC_NAMES[0]].size
    vec = jnp.concatenate([gs[k].reshape(1, -1) for k in _VEC_NAMES] + [jnp.tile(loss, (1, d // LANES))], axis=0)
    vec = jnp.pad(vec, ((0, VEC_ROWS - vec.shape[0]), (0, 0)))
    plan, total = _row_plan(shapes)
    parts, pos = [], 0
    for k in _ROW_NAMES:
        first, slabs = plan[k]
        rows = gs[k].reshape(-1, LANES)
        end = slabs[-1][1] + slabs[-1][2]
        nxt = -(-end // 8) * 8
        parts.append(jnp.pad(rows, ((0, nxt - first - rows.shape[0]), (0, 0))))
        pos = nxt
    parts.append(jnp.zeros((total - pos, LANES), F32))
    return vec, jnp.concatenate(parts, axis=0)


def _adam_small(land_vec, land_rows, w, m, v):
    names = _VEC_NAMES + _ROW_NAMES
    n = len(names)
    shapes = {k: w[k].shape for k in names}
    plan, _ = _row_plan(shapes)

    def body(*refs):
        lv_ref, lr_ref = refs[:2]
        w_refs, m_refs, v_refs = refs[2:2 + n], refs[2 + n:2 + 2 * n], refs[2 + 2 * n:2 + 3 * n]
        outs = refs[2 + 3 * n:2 + 7 * n]
        loss_ref = refs[2 + 7 * n]
        gv_s, gr_s = refs[3 + 7 * n:]
        gv_s[...] = _slot_sum(lv_ref)
        gr_s[...] = _slot_sum(lr_ref)
        loss_ref[...] = gv_s[len(_VEC_NAMES):len(_VEC_NAMES) + 1, :LANES]
        for p, k in enumerate(names):
            if k in _VEC_NAMES:
                row = _VEC_NAMES.index(k)
                slabs = [((), None, None)]
            else:
                slabs = plan[k][1]
            for idx, off, rows in slabs:
                g = gv_s[row:row + 1, :] if off is None else gr_s[off:off + rows, :]
                sel = idx + (slice(None), slice(None))
                delta, nm, nv = _adamw(w_refs[p][sel], g, m_refs[p][sel], v_refs[p][sel])
                for o, val in zip(range(4), (g, delta, nm, nv)):
                    outs[o * n + p][sel] = val

    flat = lambda tree: [tree[k] for k in names]
    shp = [jax.ShapeDtypeStruct(shapes[k], F32) for k in names]
    out = pl.pallas_call(
        body,
        out_shape=shp * 4 + [jax.ShapeDtypeStruct((1, LANES), F32)],
        scratch_shapes=[pltpu.VMEM(land_vec.shape[1:], F32), pltpu.VMEM(land_rows.shape[1:], F32)],
        name="adam_small",
    )(land_vec, land_rows, *flat(w), *flat(m), *flat(v))
    return [dict(zip(names, out[o * n:(o + 1) * n])) for o in range(4)], out[4 * n]


_COL_FFN = ("ffn1_w_gate", "ffn1_w_up", "ffn2_w_gate", "ffn2_w_up")
_ROW_FFN = ("ffn1_w_down", "ffn2_w_down")
_ROW_SQ = ("w_out", "xa_w_q", "xa_w_k", "xa_w_v", "xa_w_o")
_BIG_NAMES = ("ffn1_w_gate", "ffn1_w_up", "ffn1_w_down", "w_in", "w_out", "xa_w_q", "xa_w_k", "xa_w_v", "xa_w_o",
              "ffn2_w_gate", "ffn2_w_up", "ffn2_w_down")


def _ffn_split(fs):
    main = (fs // MXU_WIDTH_V7X) * MXU_WIDTH_V7X
    tail = fs - main
    tail_pad = -(-tail // LANES) * LANES
    assert main > 0 and tail > 0
    return main, tail, tail_pad


def _layout(name, shard_shape):
    r, c = shard_shape
    if name in _COL_FFN:
        main, tail, pad = _ffn_split(c)
        return (r, N_DEV * (main + pad)), [(1, 0, main, (r, main), (0, main)),
                                           (1, N_DEV * main, pad, (r, pad), (main, c))]
    if name in _ROW_FFN:
        main, tail, pad = _ffn_split(r)
        return (N_DEV * (main + pad), c), [(0, 0, main, (main, c), (0, main)),
                                           (0, N_DEV * main, pad, (pad, c), (main, r))]
    if name == "w_in":
        return (r, N_DEV * c), [(1, 0, c, (r, c), (0, c))]
    return (N_DEV * r, c), [(0, 0, r, (r, c), (0, r))]


def _shard_pieces(name, shard):
    out = []
    for axis, _, _, shape, (lo, hi) in _layout(name, shard.shape)[1]:
        part = shard[lo:hi, :] if axis == 0 else shard[:, lo:hi]
        pad = [(0, shape[0] - part.shape[0]), (0, shape[1] - part.shape[1])]
        out.append(jnp.pad(part, pad).astype(BF16))
    return out


def _gather_plan(names, shards):
    srcs, land_shapes, pieces, index = [], [], [], {}
    for li, name in enumerate(names):
        shape2d, parts = _layout(name, shards[name].shape)
        land_shapes.append(jax.ShapeDtypeStruct(shape2d, BF16))
        index[name] = []
        for (axis, base, stride, shape, _), src in zip(parts, _shard_pieces(name, shards[name])):
            index[name].append(len(pieces))
            pieces.append(("gather", len(srcs), li, axis, base, stride, shape))
            srcs.append(src)
    return srcs, land_shapes, pieces, index


def _scatter_plan(names, grads, shard_shapes):
    srcs, land_shapes, pieces, index = [], [], [], {}
    for si, name in enumerate(names):
        _, parts = _layout(name, shard_shapes[name])
        srcs.append(grads[name])
        index[name] = []
        for axis, base, stride, shape, _ in parts:
            index[name].append(len(land_shapes))
            pieces.append(("scatter", si, len(land_shapes), axis, base, stride, shape))
            land_shapes.append(jax.ShapeDtypeStruct((N_DEV,) + shape, grads[name].dtype))
    return srcs, land_shapes, pieces, index


def _small_views(small):
    row = lambda a: pltpu.with_memory_space_constraint(a.reshape(1, -1), pltpu.HBM)
    ln = {k: row(small[k]) for k in ("ln1_g", "ln1_b", "ln2_g", "ln2_b", "ln3_g", "ln3_b", "ln4_g", "ln4_b",
                                      "mem_ln_g", "mem_ln_b", "hg_norm_g")}
    sg_w = small["sg_w_s"].reshape(SG_GROUPS, SG_CHUNK, SG_CHUNK)
    sg = dict(logits=jnp.swapaxes(small["hg_lb_logits"], 0, 1),
              g=small["sg_ln_g"].reshape(SG_GROUPS, 1, SG_DIM), b=small["sg_ln_b"].reshape(SG_GROUPS, 1, SG_DIM),
              w=sg_w, wt=jnp.swapaxes(sg_w, 1, 2), bs=small["sg_b_s"].reshape(SG_GROUPS, SG_CHUNK, 1))
    return ln, sg


def _forward(x, mem, target, get_w, small, first_deps=()):
    ln, sg = _small_views(small)
    xb = x.astype(BF16)
    a1, b1, s1 = _ffn_up(xb, get_w("ffn1_w_gate", ()), get_w("ffn1_w_up", ()), "ffn1_up", deps=first_deps)
    h1b, xh1, rs1 = _mm_res_ln(s1, get_w("ffn1_w_down", (s1,)), x, ln["ln1_g"], ln["ln1_b"], 0.5, "ffn1_down_ln")
    proj = _mm_nn(h1b, get_w("w_in", (h1b,)), "mix_in")
    oraw, mix, states = _hgrn_fwd(proj, sg["logits"], ln["hg_norm_g"])
    mix = _sgu_fwd(proj, mix, sg["g"], sg["b"], sg["w"], sg["bs"])
    h2b, xh2, rs2 = _mm_res_ln(mix, get_w("w_out", (mix,)), (xh1, ln["ln1_g"], ln["ln1_b"]), ln["ln2_g"], ln["ln2_b"],
                               1.0, "mix_out_ln")
    mb, mxh, mrs, kb, vb = _mem_kv(mem, ln["mem_ln_g"], ln["mem_ln_b"], get_w("xa_w_k", (h2b,)), get_w("xa_w_v", (h2b,)))
    qb, att = _attn_fwd(h2b, get_w("xa_w_q", (kb,)), kb, vb)
    h3b, xh3, rs3 = _mm_res_ln(att, get_w("xa_w_o", (att,)), (xh2, ln["ln2_g"], ln["ln2_b"]), ln["ln3_g"], ln["ln3_b"],
                               1.0, "attn_out_ln")
    a2, b2, s2 = _ffn_up(h3b, get_w("ffn2_w_gate", (h3b,)), get_w("ffn2_w_up", (h3b,)), "ffn2_up")
    loss, dy4, dy4b, dg4, db4 = _mm_res_ln(s2, get_w("ffn2_w_down", (s2,)), (xh3, ln["ln3_g"], ln["ln3_b"]),
                                           ln["ln4_g"], ln["ln4_b"], 0.5, "ffn2_down_ln_loss", target=target)
    return dict(xb=xb, a1=a1, b1=b1, s1=s1, h1b=h1b, xh1=xh1, rs1=rs1, proj=proj, oraw=oraw, mix=mix, states=states,
                h2b=h2b, xh2=xh2, rs2=rs2, mb=mb, mxh=mxh, mrs=mrs, kb=kb, vb=vb, qb=qb, att=att, h3b=h3b, xh3=xh3,
                rs3=rs3, a2=a2, b2=b2, s2=s2, loss=loss, dy4=dy4, dy4b=dy4b, dg4=dg4, db4=db4)


def _backward(sv, wt, small, send):
    ln, sg = _small_views(small)
    gs = {"ln4_g": sv["dg4"], "ln4_b": sv["db4"]}
    loss, dy4, dy4b = sv["loss"], sv["dy4"], sv["dy4b"]
    g_down2 = _mm_tn(sv["s2"], dy4b, "g_ffn2_down", scale=0.5)
    da2, db2, dy3, dy3b, gs["ln3_g"], gs["ln3_b"] = _ffn_bwd_fused(
        dy4, dy4b, wt["ffn2_w_down"], wt["ffn2_w_gate"], wt["ffn2_w_up"], sv["a2"], sv["b2"], 0.5,
        (sv["xh3"], sv["rs3"], ln["ln3_g"]), "ffn2_bwd")
    g_gate2 = _mm_tn(sv["h3b"], da2, "g_ffn2_gate")
    g_up2 = _mm_tn(sv["h3b"], db2, "g_ffn2_up")
    tok = send(("ffn2_w_down", "ffn2_w_gate", "ffn2_w_up"), (g_down2, g_gate2, g_up2))

    g_o = _mm_tn(sv["att"], dy3b, "g_xa_o", deps=(tok,))
    dqb, dk, dv = _attn_bwd(dy3b, wt["xa_w_o"], sv["qb"], sv["kb"], sv["vb"])
    g_q = _mm_tn(sv["h2b"], dqb, "g_xa_q")
    g_k, g_v, gs["mem_ln_g"], gs["mem_ln_b"] = _mem_bwd(dk, dv, sv["mb"], sv["mxh"], sv["mrs"], ln["mem_ln_g"],
                                                        wt["xa_w_k"], wt["xa_w_v"])
    tok = send(("xa_w_o", "xa_w_q", "xa_w_k", "xa_w_v"), (g_o, g_q, g_k, g_v))
    dy2, dy2b, gs["ln2_g"], gs["ln2_b"] = _dx_ln(dy3, [(dqb, wt["xa_w_q"])], (sv["xh2"], sv["rs2"], ln["ln2_g"]),
                                                 "attn_dx_ln", deps=(tok,))

    g_out = _mm_tn(sv["mix"], dy2b, "g_w_out")
    dmix = _mm_nt(dy2b, wt["w_out"], "mix_out_bwd")
    dq, dfz, div, dgg, dlg, dgn = _hgrn_bwd(sv["proj"], sv["oraw"], dmix, sv["states"], sg["logits"], ln["hg_norm_g"])
    du, dvv, gs["sg_ln_g"], gs["sg_ln_b"], gs["sg_w_s"], gs["sg_b_s"] = _sgu_bwd(
        sv["proj"], dmix, sg["g"], sg["b"], sg["w"], sg["wt"], sg["bs"])
    gs["hg_lb_logits"] = jnp.swapaxes(dlg, 0, 1)
    gs["hg_norm_g"] = jnp.sum(dgn, axis=0)
    dproj = jnp.concatenate([dq, dfz, div, dgg, du, dvv], axis=1)
    g_in = _mm_tn(sv["h1b"], dproj, "g_w_in")
    tok = send(("w_out", "w_in"), (g_out, g_in))
    dy1, dy1b, gs["ln1_g"], gs["ln1_b"] = _dx_ln(dy2, [(dproj, wt["w_in"])], (sv["xh1"], sv["rs1"], ln["ln1_g"]),
                                                 "mix_dx_ln", deps=(tok,))

    g_down1 = _mm_tn(sv["s1"], dy1b, "g_ffn1_down", scale=0.5)
    tok = send(("ffn1_w_down",), (g_down1,))
    da1, db1 = _ffn_bwd_act(dy1b, wt["ffn1_w_down"], sv["a1"], sv["b1"], 0.5, "ffn1_bwd_act", deps=(tok,))
    g_gate1 = _mm_tn(sv["xb"], da1, "g_ffn1_gate")
    tok = send(("ffn1_w_gate",), (g_gate1,))
    g_up1 = _mm_tn(sv["xb"], db1, "g_ffn1_up", deps=(tok,))
    tok = send(("ffn1_w_up",), (g_up1,))
    grad_x = _dx_ln(dy1, [(da1, wt["ffn1_w_gate"]), (db1, wt["ffn1_w_up"])], None, "ffn1_dx", deps=(tok,))
    return loss, grad_x, gs


_WEIGHT_NAMES = ("ffn1_w_gate", "ffn1_w_up", "ffn1_w_down", "ln1_g", "ln1_b", "w_in", "hg_lb_logits", "hg_norm_g",
                 "sg_ln_g", "sg_ln_b", "sg_w_s", "sg_b_s", "w_out", "ln2_g", "ln2_b", "mem_ln_g", "mem_ln_b",
                 "xa_w_q", "xa_w_k", "xa_w_v", "xa_w_o", "ln3_g", "ln3_b", "ffn2_w_gate", "ffn2_w_up", "ffn2_w_down",
                 "ln4_g", "ln4_b")
_FIRST = ("ffn1_w_gate", "ffn1_w_up")
_SECOND = ("ffn1_w_down", "w_in", "w_out")
_THIRD = ("xa_w_k", "xa_w_v", "xa_w_q", "xa_w_o", "ffn2_w_gate", "ffn2_w_up", "ffn2_w_down")


def kernel(x, mem, ffn1_w_gate, ffn1_w_up, ffn1_w_down, ln1_g, ln1_b, w_in, hg_lb_logits, hg_norm_g, sg_ln_g, sg_ln_b, sg_w_s, sg_b_s, w_out, ln2_g, ln2_b, mem_ln_g, mem_ln_b, xa_w_q, xa_w_k, xa_w_v, xa_w_o, ln3_g, ln3_b, ffn2_w_gate, ffn2_w_up, ffn2_w_down, ln4_g, ln4_b, loss_target, m_ffn1_w_gate, m_ffn1_w_up, m_ffn1_w_down, m_ln1_g, m_ln1_b, m_w_in, m_hg_lb_logits, m_hg_norm_g, m_sg_ln_g, m_sg_ln_b, m_sg_w_s, m_sg_b_s, m_w_out, m_ln2_g, m_ln2_b, m_mem_ln_g, m_mem_ln_b, m_xa_w_q, m_xa_w_k, m_xa_w_v, m_xa_w_o, m_ln3_g, m_ln3_b, m_ffn2_w_gate, m_ffn2_w_up, m_ffn2_w_down, m_ln4_g, m_ln4_b, v_ffn1_w_gate, v_ffn1_w_up, v_ffn1_w_down, v_ln1_g, v_ln1_b, v_w_in, v_hg_lb_logits, v_hg_norm_g, v_sg_ln_g, v_sg_ln_b, v_sg_w_s, v_sg_b_s, v_w_out, v_ln2_g, v_ln2_b, v_mem_ln_g, v_mem_ln_b, v_xa_w_q, v_xa_w_k, v_xa_w_v, v_xa_w_o, v_ln3_g, v_ln3_b, v_ffn2_w_gate, v_ffn2_w_up, v_ffn2_w_down, v_ln4_g, v_ln4_b):
    args = dict(locals())
    w = {k: args[k] for k in _WEIGHT_NAMES}
    m = {k: args["m_" + k] for k in _WEIGHT_NAMES}
    v = {k: args["v_" + k] for k in _WEIGHT_NAMES}
    shards = {k: w[k][0] for k in _BIG_NAMES}
    shard_shapes = {k: shards[k].shape for k in _BIG_NAMES}
    small = {k: (w[k][0] if k != "hg_lb_logits" else w[k]) for k in _SMALL_NAMES}

    srcs1, shapes1, pieces1, idx1 = _gather_plan(_FIRST, shards)
    lands1 = _place_own(srcs1, shapes1, pieces1, "gather_first_own")
    rest = _SECOND + _THIRD
    srcs2, shapes2, pieces2, idx2 = _gather_plan(rest, shards)
    lands2 = _place_own(srcs2, shapes2, pieces2, "gather_rest_own")
    groups2 = [list(idx2[k]) for k in rest]
    lands1, tok1 = _routed_gather(srcs1, lands1, pieces1, tuple(lands2), "gather_first")
    sems2, srcs2, lands2, tok2 = _comm_start(srcs2, lands2, pieces2, groups2, "gather_rest_start", after=(tok1,))
    wt = dict(zip(_FIRST, lands1))
    pending = {k: gi for gi, k in enumerate(rest)}

    def get_w(name, after):
        if name in pending:
            gi = pending.pop(name)
            si = [pieces2[p][1] for p in groups2[gi]]
            sub = [(pieces2[p][0], row, 0) + pieces2[p][3:] for row, p in enumerate(groups2[gi])]
            wt[name] = _comm_wait([srcs2[s] for s in si], [lands2[gi]], sub, list(range(len(sub))), sems2[gi],
                                  after, "gather_wait_" + name)[0]
        return wt[name]

    sv = _forward(x[0], mem[0], loss_target[0], get_w, small, first_deps=(tok2,))

    sent = []

    def send(names, grads):
        srcs, shapes, pieces, idx = _scatter_plan(names, dict(zip(names, grads)), shard_shapes)
        lands = _place_own(srcs, shapes, pieces, "grads_own_%d" % len(sent))
        sems, srcs, lands, tok = _comm_start(srcs, lands, pieces, [list(range(len(pieces)))],
                                             "grads_start_%d" % len(sent))
        sent.append((names, srcs, lands, pieces, idx, sems[0]))
        return tok

    loss, grad_x, gs = _backward(sv, wt, small, send)

    ssrc = list(_pack_small_grads(gs, {k: w[k].shape for k in _SMALL_NAMES}, loss))
    sp = [("scatter", i, i, 0, 0, 0, a.shape) for i, a in enumerate(ssrc)]
    sshape = [jax.ShapeDtypeStruct((N_DEV,) + a.shape, F32) for a in ssrc]
    sl = _place_own(ssrc, sshape, sp, "small_own")
    ssem, ssrc, sl, _ = _comm_start(ssrc, sl, sp, [[0, 1]], "small_start")

    out_g, out_d, out_m, out_v = {}, {}, {}, {}
    after = (grad_x,)
    for n_sent, (names, srcs, lands, pieces, idx, sems) in enumerate(sent):
        lands = _comm_wait(srcs, lands, pieces, list(range(len(pieces))), sems, after, "grads_wait_%d" % n_sent)
        for k in names:
            axis = 1 if (k in _COL_FFN or k == "w_in") else 0
            if k in _COL_FFN:
                res = _adam_sharded([lands[i] for i in idx[k]], w[k][0].T, m[k][0].T, v[k][0].T, axis, "adam_" + k)
                res = [r.T for r in res]
            else:
                res = _adam_sharded([lands[i] for i in idx[k]], w[k][0], m[k][0], v[k][0], axis, "adam_" + k)
            out_g[k], out_d[k], out_m[k], out_v[k] = [r[None] for r in res]
        after = (out_v[names[-1]],)
    sl = _comm_wait(ssrc, sl, sp, [0, 1], ssem[0], after, "small_wait")
    small_out, loss_sum = _adam_small(sl[0], sl[1], w, m, v)
    for dst, res in zip((out_g, out_d, out_m, out_v), small_out):
        dst.update(res)
    loss_all = loss_sum[0, 0]
    return (loss_all, grad_x[None], *[out_g[k] for k in _WEIGHT_NAMES], *[out_d[k] for k in _WEIGHT_NAMES],
            *[out_m[k] for k in _WEIGHT_NAMES], *[out_v[k] for k in _WEIGHT_NAMES])
```

```python
import itertools

import jax
import jax.numpy as jnp
import numpy as np
from jax import lax
from jax.experimental import pallas as pl
from jax.experimental.pallas import tpu as pltpu

F32 = jnp.float32
BF16 = jnp.bfloat16

N_DEV = 8
ALPHA = 2.0 ** 0.25
LN_EPS = 1e-5
HG_HEADS = 4
HG_DIM = 128
SG_GROUPS = 4
SG_DIM = 128
SG_CHUNK = 128
X_HEADS = 4
HG_BLOCK = 16
HG_UNROLL = 16
ADAM_LR = 0.001
ADAM_B1 = 0.9
ADAM_B2 = 0.999
ADAM_EPS = 1e-08
ADAM_WD = 0.01
ADAM_STEP = 10
VMEM_LIMIT_V7X = 48 * 1024 * 1024
MXU_WIDTH_V7X = 256
LANES = 128
MESH_ID = pl.DeviceIdType.MESH
ANY = pl.BlockSpec(memory_space=pl.ANY)
HBM = pl.BlockSpec(memory_space=pltpu.HBM)
SEM = pl.BlockSpec(memory_space=pltpu.SEMAPHORE)
DATAFLOW = pltpu.SideEffectType.DATAFLOW_SIDE_EFFECTING


def _params(n_axes):
    return pltpu.CompilerParams(dimension_semantics=("arbitrary",) * n_axes, vmem_limit_bytes=VMEM_LIMIT_V7X)


def _dot(a, b):
    return jnp.dot(a, b, preferred_element_type=F32)


def _dot_nt(a, b):
    return lax.dot_general(a, b, (((1,), (1,)), ((), ())), preferred_element_type=F32)


def _dot_tn(a, b):
    return lax.dot_general(a, b, (((0,), (0,)), ((), ())), preferred_element_type=F32)


def _sigmoid(x):
    return 1.0 / (1.0 + jnp.exp(-x))


def _silu_and_grad(a):
    sig = _sigmoid(a)
    return a * sig, sig * (1.0 + a * (1.0 - sig))


_GELU_C = 0.7978845608028654


def _gelu_and_grad(x):
    inner = _GELU_C * (x + 0.044715 * x * x * x)
    t = jnp.tanh(inner)
    val = 0.5 * x * (1.0 + t)
    grad = 0.5 * (1.0 + t) + 0.5 * x * (1.0 - t * t) * _GELU_C * (1.0 + 3.0 * 0.044715 * x * x)
    return val, grad


def _ln_fwd(y, g, b):
    mu = jnp.mean(y, axis=-1, keepdims=True)
    yc = y - mu
    var = jnp.mean(yc * yc, axis=-1, keepdims=True)
    rstd = lax.rsqrt(var + LN_EPS)
    xhat = yc * rstd
    return xhat * g + b, xhat, rstd


def _ln_bwd(dh, xhat, rstd, g):
    dxh = dh * g
    m1 = jnp.mean(dxh, axis=-1, keepdims=True)
    m2 = jnp.mean(dxh * xhat, axis=-1, keepdims=True)
    dy = rstd * (dxh - m1 - xhat * m2)
    dg = jnp.sum(dh * xhat, axis=0, keepdims=True)
    db = jnp.sum(dh, axis=0, keepdims=True)
    return dy, dg, db


def _mask_dot(mask, x):
    hi = x.astype(BF16)
    lo = (x - hi.astype(F32)).astype(BF16)
    n = mask.shape[0]
    parts = [_dot(mask, hi[r:r + n, :]) + _dot(mask, lo[r:r + n, :]) for r in range(0, x.shape[0], n)]
    return parts[0] if len(parts) == 1 else jnp.concatenate(parts, axis=0)


def _block_masks(n):
    r = np.arange(n)[:, None]
    c = np.arange(n)[None, :]
    same = (r // HG_BLOCK) == (c // HG_BLOCK)
    return jnp.asarray(np.stack([same & (c <= r), same & (c >= r), same]), BF16)


def _row_tile(t):
    return min(t, 512)


def _col_tile(n):
    for cand in (512, 256, 128):
        if n % cand == 0:
            return cand
    return n


def _resident(w):
    return pl.BlockSpec(w.shape, lambda *_: (0, 0), pipeline_mode=pl.Buffered(1))


def _window_loads(w_hbm, w_vmem, sems, first_sem, size, axis):
    def window(ref, c):
        span = pl.ds(c * size, size)
        return ref.at[span, :] if axis == 0 else ref.at[:, span]
    return [pltpu.make_async_copy(window(w_hbm, c), window(w_vmem, c), sems.at[first_sem + c])
            for c in range(w_hbm.shape[axis] // size)]


def _first_step_loads(per_weight):
    first = pl.program_id(0) == 0

    @pl.when(first)
    def _():
        for loads in per_weight:
            for load in loads:
                load.start()

    def wait(c):
        @pl.when(first)
        def _():
            for loads in per_weight:
                loads[c].wait()

    return wait


def _drop_deps(body, n_in, n_deps):
    if n_deps == 0:
        return body
    return lambda *refs: body(*refs[:n_in], *refs[n_in + n_deps:])


def _ffn_up(hb, wg, wu, name, deps=()):
    t, d = hb.shape
    f = wg.shape[1]
    tm = _row_tile(t)
    tn = _col_tile(f)

    nc = f // tn

    def body(h_ref, wg_hbm, wu_hbm, a_ref, b_ref, s_ref, wg_ref, wu_ref, sems):
        wait = _first_step_loads([_window_loads(wg_hbm, wg_ref, sems, 0, tn, 1),
                                  _window_loads(wu_hbm, wu_ref, sems, nc, tn, 1)])
        h = h_ref[...]
        for c in range(nc):
            cols = slice(c * tn, (c + 1) * tn)
            wait(c)
            a = _dot(h, wg_ref[:, cols])
            b = _dot(h, wu_ref[:, cols])
            a_ref[:, cols] = a.astype(BF16)
            b_ref[:, cols] = b.astype(BF16)
            s_ref[:, cols] = (a * _sigmoid(a) * b).astype(BF16)

    act = pl.BlockSpec((tm, f), lambda i: (i, 0))
    return pl.pallas_call(
        _drop_deps(body, 3, len(deps)),
        grid=(t // tm,),
        in_specs=[pl.BlockSpec((tm, d), lambda i: (i, 0)), ANY, ANY] + [ANY] * len(deps),
        out_specs=[act, act, act],
        out_shape=[jax.ShapeDtypeStruct((t, f), BF16)] * 3,
        scratch_shapes=[pltpu.VMEM(wg.shape, BF16), pltpu.VMEM(wu.shape, BF16), pltpu.SemaphoreType.DMA((2 * nc,))],
        compiler_params=_params(1),
        name=name,
    )(hb, wg, wu, *deps)


def _mm_res_ln(lhs, w, res, g, b, coef, name, target=None):
    t, kd = lhs.shape
    d = w.shape[1]
    tm = _row_tile(t)
    nt = t // tm
    from_norm = isinstance(res, tuple)
    n_res = 3 if from_norm else 1

    def body(*refs):
        l_ref, w_ref = refs[:2]
        r_refs = refs[2:2 + n_res]
        g_ref, b_ref = refs[2 + n_res:4 + n_res]
        rest = refs[4 + n_res:]
        prev = r_refs[0][...] * r_refs[1][...] + r_refs[2][...] if from_norm else r_refs[0][...]
        y = ALPHA * prev + coef * _dot(l_ref[...], w_ref[...])
        h, xhat, rstd = _ln_fwd(y, g_ref[...], b_ref[...])
        if target is None:
            hb_ref, xh_ref, rs_ref = rest
            hb_ref[...] = h.astype(BF16)
            xh_ref[...] = xhat
            rs_ref[...] = rstd
            return
        t_ref, loss_ref, dy_ref, dyb_ref, dg_ref, db_ref, lacc = rest
        i = pl.program_id(0)

        @pl.when(i == 0)
        def _():
            lacc[...] = jnp.zeros_like(lacc)
            dg_ref[...] = jnp.zeros_like(dg_ref)
            db_ref[...] = jnp.zeros_like(db_ref)

        err = h - t_ref[...]
        lacc[...] += jnp.sum(err * err, axis=0, keepdims=True)
        dy, dg, db = _ln_bwd(err * (1.0 / d), xhat, rstd, g_ref[...])
        dy_ref[...] = dy
        dyb_ref[...] = dy.astype(BF16)
        dg_ref[...] += dg
        db_ref[...] += db

        @pl.when(i == nt - 1)
        def _():
            loss_ref[...] = jnp.zeros_like(loss_ref) + jnp.sum(lacc[...], axis=1, keepdims=True) * (0.5 / d)

    row = pl.BlockSpec((tm, d), lambda i: (i, 0))
    vec = pl.BlockSpec((1, d), lambda i: (0, 0))
    res_specs = [row, vec, vec] if from_norm else [row]
    res_args = list(res) if from_norm else [res]
    in_specs = [pl.BlockSpec((tm, kd), lambda i: (i, 0)), _resident(w)] + res_specs + [vec, vec]
    args = [lhs, w] + res_args + [g, b]
    if target is None:
        out_specs = [row, row, pl.BlockSpec((tm, 1), lambda i: (i, 0))]
        out_shape = [jax.ShapeDtypeStruct((t, d), BF16), jax.ShapeDtypeStruct((t, d), F32),
                     jax.ShapeDtypeStruct((t, 1), F32)]
        scratch = []
    else:
        in_specs.append(row)
        args.append(target)
        out_specs = [pl.BlockSpec((1, LANES), lambda i: (0, 0)), row, row, vec, vec]
        out_shape = [jax.ShapeDtypeStruct((1, LANES), F32), jax.ShapeDtypeStruct((t, d), F32),
                     jax.ShapeDtypeStruct((t, d), BF16), jax.ShapeDtypeStruct((1, d), F32),
                     jax.ShapeDtypeStruct((1, d), F32)]
        scratch = [pltpu.VMEM((1, d), F32)]
    return pl.pallas_call(
        body,
        grid=(nt,),
        in_specs=in_specs,
        out_specs=out_specs,
        out_shape=out_shape,
        scratch_shapes=scratch,
        compiler_params=_params(1),
        name=name,
    )(*args)


def _store_slabs(o_ref, first, tile):
    for s in range(tile.shape[1] // LANES):
        o_ref[first + s] = tile[:, s * LANES:(s + 1) * LANES]


def _mm_nn(lhs, w, name):
    t, kd = lhs.shape
    n = w.shape[1]
    tm = _row_tile(t)
    tn = _col_tile(n)

    def body(l_ref, w_ref, o_ref):
        lhs_v = l_ref[...]
        for c in range(n // tn):
            _store_slabs(o_ref, c * (tn // LANES), _dot(lhs_v, w_ref[:, c * tn:(c + 1) * tn]))

    return pl.pallas_call(
        body,
        grid=(t // tm,),
        in_specs=[pl.BlockSpec((tm, kd), lambda i: (i, 0)), _resident(w)],
        out_specs=pl.BlockSpec((n // LANES, tm, LANES), lambda i: (0, i, 0)),
        out_shape=jax.ShapeDtypeStruct((n // LANES, t, LANES), F32),
        compiler_params=_params(1),
        name=name,
    )(lhs, w)


def _lower_bound(lg):
    m = jnp.max(lg, axis=0, keepdims=True)
    e = jnp.exp(lg - m)
    return e[0:1, :] / jnp.sum(e, axis=0, keepdims=True)


def _forget_terms(fz, lb):
    e = jnp.exp(-jnp.abs(fz))
    r = 1.0 / (1.0 + e)
    pos = fz >= 0.0
    sig = jnp.where(pos, r, e * r)
    nsig = jnp.where(pos, e * r, r)
    f = lb + (1.0 - lb) * sig
    k = (1.0 - lb) * nsig
    return sig, nsig, f, k


def _hg_tile(t):
    return min(t, 1024)


HG_HALF = HG_BLOCK // 2
NEG_BIG = -1e30


def _halves(a):
    return a[:HG_HALF, :], a[HG_HALF:, :]


def _causal_halves(s):
    return (0, 1) if s < HG_HALF else (1,)


def _decay_from(b_half, b_s, s, h, tidx):
    first = s - h * HG_HALF
    diff = b_half - b_s
    if first > 0:
        diff = jnp.where(tidx >= first, diff, NEG_BIG)
    return jnp.exp(diff)


def _hgrn_fwd(proj, logits, gn):
    t = proj.shape[1]
    ct = _hg_tile(t)
    nct = t // ct
    nblk = ct // HG_BLOCK
    nh = HG_HEADS
    mrows = min(ct, 256)

    def body(q_ref, fz_ref, iv_ref, gg_ref, lg_ref, gn_ref, mask_ref, oraw_ref, oa_ref, st_ref,
             state, qt_s, kt_s, k_s, b_s, dec_s):
        c = pl.program_id(1)

        @pl.when(c == 0)
        def _():
            state[...] = jnp.zeros_like(state)

        lb = _lower_bound(lg_ref[...])
        q = q_ref[...]
        _, _, f, k = _forget_terms(fz_ref[...], lb)
        logf = jnp.log(f)
        b = _mask_dot(mask_ref[0], logf)
        bend = _mask_dot(mask_ref[2], logf)
        qt_s[...] = (q * jnp.exp(b)).astype(BF16)
        kt_s[...] = (k * jnp.exp(bend - b)).astype(BF16)
        k_s[...] = k
        b_s[...] = b
        dec_s[...] = jnp.exp(bend)
        tidx = lax.broadcasted_iota(jnp.int32, (HG_HALF, HG_DIM), 0)

        def blk(i, carry):
            r0 = pl.multiple_of(i * HG_BLOCK, HG_BLOCK)
            rows = pl.ds(r0, HG_BLOCK)
            st = state[...]
            stb = st.astype(BF16)
            st_ref[i] = stb
            v = iv_ref[rows, :]
            qq = q_ref[rows, :]
            kk = k_s[rows, :]
            bb = b_s[rows, :]
            o = list(_halves(_dot_nt(qt_s[rows, :], stb)))
            qh, bh = _halves(qq), _halves(bb)
            for s in range(HG_BLOCK):
                ks, vs = kk[s:s + 1, :], v[s:s + 1, :]
                for h in _causal_halves(s):
                    e = _decay_from(bh[h], bb[s:s + 1, :], s, h, tidx)
                    acol = jnp.sum(qh[h] * (ks * e), axis=1, keepdims=True)
                    o[h] = o[h] + acol * vs
            oraw_ref[rows, :] = jnp.concatenate(o, axis=0)
            state[...] = st * dec_s[pl.ds(r0, 1), :] + _dot_tn(v.astype(BF16), kt_s[rows, :])
            return carry

        lax.fori_loop(0, nblk, blk, 0, unroll=HG_UNROLL)
        oraw = oraw_ref[...]
        r = lax.rsqrt(jnp.mean(oraw * oraw, axis=-1, keepdims=True) + LN_EPS)
        gg = gg_ref[...]
        oa_ref[...] = (oraw * r * gn_ref[...] * gg * _sigmoid(gg)).astype(BF16)

    def slab(off):
        return pl.BlockSpec((None, ct, HG_DIM), lambda h, c: (off + h, c, 0))

    return pl.pallas_call(
        body,
        grid=(nh, nct),
        in_specs=[slab(0), slab(nh), slab(2 * nh), slab(3 * nh),
                  pl.BlockSpec((None, 2, HG_DIM), lambda h, c: (h, 0, 0)),
                  pl.BlockSpec((1, HG_DIM), lambda h, c: (0, 0)),
                  pl.BlockSpec((3, mrows, mrows), lambda h, c: (0, 0, 0))],
        out_specs=[slab(0), pl.BlockSpec((ct, HG_DIM), lambda h, c: (c, h)),
                   pl.BlockSpec((None, nblk, HG_DIM, HG_DIM), lambda h, c: (h, c, 0, 0))],
        out_shape=[jax.ShapeDtypeStruct((nh, t, HG_DIM), F32),
                   jax.ShapeDtypeStruct((t, (nh + SG_GROUPS) * HG_DIM), BF16),
                   jax.ShapeDtypeStruct((nh, t // HG_BLOCK, HG_DIM, HG_DIM), BF16)],
        scratch_shapes=[pltpu.VMEM((HG_DIM, HG_DIM), F32), pltpu.VMEM((ct, HG_DIM), BF16),
                        pltpu.VMEM((ct, HG_DIM), BF16), pltpu.VMEM((ct, HG_DIM), F32),
                        pltpu.VMEM((ct, HG_DIM), F32), pltpu.VMEM((ct, HG_DIM), F32)],
        compiler_params=_params(2),
        name="hgrn_fwd",
    )(proj, proj, proj, proj, logits, gn, _block_masks(mrows))


def _sg_tile(t):
    return min(t, 512)


def _sgu_chunk_fwd(u, v, ln_g, ln_b, wm, bs):
    ua, dua = _gelu_and_grad(u)
    va, dva = _gelu_and_grad(v)
    vn, xhat, rstd = _ln_fwd(va, ln_g, ln_b)
    s = _dot(wm, vn.astype(BF16)) + bs
    return ua, dua, dva, vn, xhat, rstd, s


def _tril_weight(w):
    n = SG_CHUNK
    r = lax.broadcasted_iota(jnp.int32, (n, n), 0)
    c = lax.broadcasted_iota(jnp.int32, (n, n), 1)
    return jnp.where(c <= r, w, 0.0)


def _sgu_fwd(proj, mix, ln_g, ln_b, w_s, b_col):
    t = proj.shape[1]
    ct = _sg_tile(t)
    ng = SG_GROUPS
    wide = ng * SG_DIM
    blk_u = 4 * HG_HEADS // ng

    def body(u_ref, v_ref, g_ref, b_ref, w_ref, bs_ref, mix_ref, o_ref):
        del mix_ref
        for g in range(ng):
            lanes = slice(g * SG_DIM, (g + 1) * SG_DIM)
            wm = _tril_weight(w_ref[g]).astype(BF16)
            for n in range(ct // SG_CHUNK):
                rows = slice(n * SG_CHUNK, (n + 1) * SG_CHUNK)
                ua, _, _, _, _, _, s = _sgu_chunk_fwd(u_ref[g, rows, :], v_ref[g, rows, :], g_ref[g], b_ref[g], wm,
                                                      bs_ref[g])
                o_ref[rows, lanes] = (ua * s).astype(BF16)

    full = lambda a: pl.BlockSpec(a.shape, lambda c: (0,) * a.ndim)
    return pl.pallas_call(
        body,
        grid=(t // ct,),
        in_specs=[pl.BlockSpec((ng, ct, SG_DIM), lambda c: (blk_u, c, 0)),
                  pl.BlockSpec((ng, ct, SG_DIM), lambda c: (blk_u + 1, c, 0)),
                  full(ln_g), full(ln_b), full(w_s), full(b_col), ANY],
        out_specs=pl.BlockSpec((ct, wide), lambda c: (c, 1)),
        out_shape=jax.ShapeDtypeStruct(mix.shape, mix.dtype),
        input_output_aliases={6: 0},
        compiler_params=_params(1),
        name="sgu_fwd",
    )(proj, proj, ln_g, ln_b, w_s, b_col, mix)


def _mem_kv(mem, g, b, wk, wv):
    m_len, d = mem.shape

    def body(m_ref, g_ref, b_ref, wk_ref, wv_ref, mb_ref, xh_ref, rs_ref, k_ref, v_ref):
        m, xhat, rstd = _ln_fwd(m_ref[...], g_ref[...], b_ref[...])
        mb = m.astype(BF16)
        mb_ref[...] = mb
        xh_ref[...] = xhat
        rs_ref[...] = rstd
        k_ref[...] = _dot(mb, wk_ref[...]).astype(BF16)
        v_ref[...] = _dot(mb, wv_ref[...]).astype(BF16)

    return pl.pallas_call(
        body,
        out_shape=[jax.ShapeDtypeStruct((m_len, d), BF16), jax.ShapeDtypeStruct((m_len, d), F32),
                   jax.ShapeDtypeStruct((m_len, 1), F32), jax.ShapeDtypeStruct((m_len, d), BF16),
                   jax.ShapeDtypeStruct((m_len, d), BF16)],
        compiler_params=pltpu.CompilerParams(vmem_limit_bytes=VMEM_LIMIT_V7X),
        name="mem_kv",
    )(mem, g, b, wk, wv)


def _softmax_rows(s):
    m = jnp.max(s, axis=-1, keepdims=True)
    p = jnp.exp(s - m)
    return p / jnp.sum(p, axis=-1, keepdims=True)


def _attn_fwd(hb, wq, kb, vb):
    t, d = hb.shape
    tm = _row_tile(t)
    dh = d // X_HEADS
    scale = dh ** -0.5

    def body(h_ref, wq_ref, k_ref, v_ref, q_ref, o_ref):
        q = _dot(h_ref[...], wq_ref[...]).astype(BF16)
        q_ref[...] = q
        for hd in range(X_HEADS):
            sl = slice(hd * dh, (hd + 1) * dh)
            p = _softmax_rows(_dot_nt(q[:, sl], k_ref[:, sl]) * scale)
            o_ref[:, sl] = _dot(p.astype(BF16), v_ref[:, sl]).astype(BF16)

    row = pl.BlockSpec((tm, d), lambda i: (i, 0))
    full = lambda a: pl.BlockSpec(a.shape, lambda i: (0, 0))
    return pl.pallas_call(
        body,
        grid=(t // tm,),
        in_specs=[row, full(wq), full(kb), full(vb)],
        out_specs=[row, row],
        out_shape=[jax.ShapeDtypeStruct((t, d), BF16), jax.ShapeDtypeStruct((t, d), BF16)],
        compiler_params=_params(1),
        name="attn_fwd",
    )(hb, wq, kb, vb)


def _ffn_bwd_act(dyb, wd, a, b, coef, name, deps=()):
    t, d = dyb.shape
    f = wd.shape[0]
    tm = _row_tile(t)
    tn = _col_tile(f)

    def body(dy_ref, wd_hbm, a_ref, b_ref, da_ref, db_ref, wd_ref, sems):
        wait = _first_step_loads([_window_loads(wd_hbm, wd_ref, sems, 0, tn, 0)])
        dy = dy_ref[...]
        for c in range(f // tn):
            cols = slice(c * tn, (c + 1) * tn)
            wait(c)
            ds = _dot_nt(dy, wd_ref[cols, :]) * coef
            silu, dsilu = _silu_and_grad(a_ref[:, cols].astype(F32))
            da_ref[:, cols] = (ds * b_ref[:, cols].astype(F32) * dsilu).astype(BF16)
            db_ref[:, cols] = (ds * silu).astype(BF16)

    act = pl.BlockSpec((tm, f), lambda i: (i, 0))
    return pl.pallas_call(
        _drop_deps(body, 4, len(deps)),
        grid=(t // tm,),
        in_specs=[pl.BlockSpec((tm, d), lambda i: (i, 0)), ANY, act, act] + [ANY] * len(deps),
        out_specs=[act, act],
        out_shape=[jax.ShapeDtypeStruct((t, f), BF16), jax.ShapeDtypeStruct((t, f), BF16)],
        scratch_shapes=[pltpu.VMEM(wd.shape, BF16), pltpu.SemaphoreType.DMA((f // tn,))],
        compiler_params=_params(1),
        name=name,
    )(dyb, wd, a, b, *deps)


def _ffn_bwd_fused(dy, dyb, wd, wg, wu, a, b, coef, ln, name):
    t, d = dy.shape
    f = wd.shape[0]
    tm = min(t, 256)
    tn = _col_tile(f)

    nc = f // tn

    def body(dy_ref, dyb_ref, wd_hbm, wg_hbm, wu_hbm, a_ref, b_ref, xh_ref, rs_ref, g_ref,
             da_ref, db_ref, dyo_ref, dyob_ref, dg_ref, dbl_ref, wd_ref, wg_ref, wu_ref, sems):
        wait = _first_step_loads([_window_loads(wd_hbm, wd_ref, sems, 0, tn, 0),
                                  _window_loads(wg_hbm, wg_ref, sems, nc, tn, 1),
                                  _window_loads(wu_hbm, wu_ref, sems, 2 * nc, tn, 1)])
        dyb_v = dyb_ref[...]
        dh = ALPHA * dy_ref[...]
        for c in range(nc):
            cols = slice(c * tn, (c + 1) * tn)
            wait(c)
            ds = _dot_nt(dyb_v, wd_ref[cols, :]) * coef
            silu, dsilu = _silu_and_grad(a_ref[:, cols].astype(F32))
            da = (ds * b_ref[:, cols].astype(F32) * dsilu).astype(BF16)
            db = (ds * silu).astype(BF16)
            da_ref[:, cols] = da
            db_ref[:, cols] = db
            dh = dh + _dot_nt(da, wg_ref[:, cols]) + _dot_nt(db, wu_ref[:, cols])

        @pl.when(pl.program_id(0) == 0)
        def _():
            dg_ref[...] = jnp.zeros_like(dg_ref)
            dbl_ref[...] = jnp.zeros_like(dbl_ref)

        dyp, dg, dbl = _ln_bwd(dh, xh_ref[...], rs_ref[...], g_ref[...])
        dyo_ref[...] = dyp
        dyob_ref[...] = dyp.astype(BF16)
        dg_ref[...] += dg
        dbl_ref[...] += dbl

    row = pl.BlockSpec((tm, d), lambda i: (i, 0))
    act = pl.BlockSpec((tm, f), lambda i: (i, 0))
    vec = pl.BlockSpec((1, d), lambda i: (0, 0))
    return pl.pallas_call(
        body,
        grid=(t // tm,),
        in_specs=[row, row, ANY, ANY, ANY, act, act, row, pl.BlockSpec((tm, 1), lambda i: (i, 0)), vec],
        out_specs=[act, act, row, row, vec, vec],
        out_shape=[jax.ShapeDtypeStruct((t, f), BF16), jax.ShapeDtypeStruct((t, f), BF16),
                   jax.ShapeDtypeStruct((t, d), F32), jax.ShapeDtypeStruct((t, d), BF16),
                   jax.ShapeDtypeStruct((1, d), F32), jax.ShapeDtypeStruct((1, d), F32)],
        scratch_shapes=[pltpu.VMEM(wd.shape, BF16), pltpu.VMEM(wg.shape, BF16), pltpu.VMEM(wu.shape, BF16),
                        pltpu.SemaphoreType.DMA((3 * nc,))],
        compiler_params=_params(1),
        name=name,
    )(dy, dyb, wd, wg, wu, a, b, *ln)


def _mm_tn(a, b, name, scale=1.0, deps=()):
    t, m = a.shape
    n = b.shape[1]
    tt = _row_tile(t)
    nt = t // tt
    tm_o, tn_o = m, n

    def body(a_ref, b_ref, o_ref, acc):
        k = pl.program_id(2)

        @pl.when(k == 0)
        def _():
            acc[...] = jnp.zeros_like(acc)

        acc[...] += _dot_tn(a_ref[...], b_ref[...])

        @pl.when(k == nt - 1)
        def _():
            o_ref[...] = (acc[...] * scale).astype(BF16)

    return pl.pallas_call(
        _drop_deps(body, 2, len(deps)),
        grid=(m // tm_o, n // tn_o, nt),
        in_specs=[pl.BlockSpec((tt, tm_o), lambda i, j, k: (k, i)), pl.BlockSpec((tt, tn_o), lambda i, j, k: (k, j))]
        + [ANY] * len(deps),
        out_specs=pl.BlockSpec((tm_o, tn_o), lambda i, j, k: (i, j)),
        out_shape=jax.ShapeDtypeStruct((m, n), BF16),
        scratch_shapes=[pltpu.VMEM((tm_o, tn_o), F32)],
        compiler_params=_params(3),
        name=name,
    )(a, b, *deps)


def _mm_nt(lhs, w, name):
    t, d = lhs.shape
    kd = w.shape[0]
    tm = _row_tile(t)

    def body(l_ref, w_ref, o_ref):
        _store_slabs(o_ref, 0, _dot_nt(l_ref[...], w_ref[...]))

    return pl.pallas_call(
        body,
        grid=(t // tm,),
        in_specs=[pl.BlockSpec((tm, d), lambda i: (i, 0)), _resident(w)],
        out_specs=pl.BlockSpec((kd // LANES, tm, LANES), lambda i: (0, i, 0)),
        out_shape=jax.ShapeDtypeStruct((kd // LANES, t, LANES), F32),
        compiler_params=_params(1),
        name=name,
    )(lhs, w)


def _dx_ln(dy, pairs, ln, name, deps=()):
    t, d = dy.shape
    npair = len(pairs)
    tm = min(t, 512 // npair)
    nt = t // tm
    n_in = 1 + 2 * npair + (3 if ln is not None else 0)

    def body(*refs):
        dy_ref = refs[0]
        pr = refs[1:1 + 2 * npair]
        pos = 1 + 2 * npair
        dh = ALPHA * dy_ref[...]
        for p in range(npair):
            dh = dh + _dot_nt(pr[2 * p][...], pr[2 * p + 1][...])
        if ln is not None:
            xh_ref, rs_ref, g_ref = refs[pos:pos + 3]
            dyo_ref, dyb_ref, dg_ref, db_ref = refs[pos + 3:pos + 7]

            @pl.when(pl.program_id(0) == 0)
            def _():
                dg_ref[...] = jnp.zeros_like(dg_ref)
                db_ref[...] = jnp.zeros_like(db_ref)

            dyp, dg, db = _ln_bwd(dh, xh_ref[...], rs_ref[...], g_ref[...])
            dyo_ref[...] = dyp
            dyb_ref[...] = dyp.astype(BF16)
            dg_ref[...] += dg
            db_ref[...] += db
        else:
            refs[pos][...] = dh

    row = pl.BlockSpec((tm, d), lambda i: (i, 0))
    vec = pl.BlockSpec((1, d), lambda i: (0, 0))
    in_specs = [row]
    args = [dy]
    for lhs, w in pairs:
        in_specs += [pl.BlockSpec((tm, lhs.shape[1]), lambda i: (i, 0)), _resident(w)]
        args += [lhs, w]
    if ln is not None:
        in_specs += [row, pl.BlockSpec((tm, 1), lambda i: (i, 0)), vec]
        args += list(ln)
        out_specs = [row, row, vec, vec]
        out_shape = [jax.ShapeDtypeStruct((t, d), F32), jax.ShapeDtypeStruct((t, d), BF16),
                     jax.ShapeDtypeStruct((1, d), F32), jax.ShapeDtypeStruct((1, d), F32)]
    else:
        out_specs = row
        out_shape = jax.ShapeDtypeStruct((t, d), F32)
    return pl.pallas_call(
        _drop_deps(body, n_in, len(deps)),
        grid=(nt,),
        in_specs=in_specs + [ANY] * len(deps),
        out_specs=out_specs,
        out_shape=out_shape,
        compiler_params=_params(1),
        name=name,
    )(*args, *deps)


def _hgrn_bwd(proj, oraw, dmix, states, logits, gn):
    t = proj.shape[1]
    ct = _hg_tile(t)
    nct = t // ct
    nblk = ct // HG_BLOCK
    nh = HG_HEADS
    mrows = min(ct, 256)

    def body(q_ref, fz_ref, iv_ref, gg_ref, or_ref, do_ref, st_ref, lg_ref, gn_ref, mask_ref,
             dq_ref, dfz_ref, div_ref, dgg_ref, dlg_ref, dgn_ref,
             dstate, qt_s, kt_s, k_s, b_s, eb_s, ekb_s, dec_s, dor_s, dbl_s, gr_s, dk_s, dlb_acc):
        c = pl.program_id(1)

        @pl.when(c == 0)
        def _():
            dstate[...] = jnp.zeros_like(dstate)
            dlb_acc[...] = jnp.zeros_like(dlb_acc)
            dgn_ref[...] = jnp.zeros_like(dgn_ref)

        lb = _lower_bound(lg_ref[...])
        q = q_ref[...]
        sig, nsig, f, k = _forget_terms(fz_ref[...], lb)
        logf = jnp.log(f)
        b = _mask_dot(mask_ref[0], logf)
        bend = _mask_dot(mask_ref[2], logf)
        eb = jnp.exp(b)
        ekb = jnp.exp(bend - b)
        qt_s[...] = (q * eb).astype(BF16)
        kt_s[...] = (k * ekb).astype(BF16)
        k_s[...] = k
        b_s[...] = b
        eb_s[...] = eb
        ekb_s[...] = ekb
        dec_s[...] = jnp.exp(bend)
        oraw = or_ref[...]
        r = lax.rsqrt(jnp.mean(oraw * oraw, axis=-1, keepdims=True) + LN_EPS)
        on = oraw * r
        gg = gg_ref[...]
        silu, dsilu = _silu_and_grad(gg)
        doa = do_ref[...]
        gnv = gn_ref[...]
        dgg_ref[...] = (doa * on * gnv * dsilu).astype(BF16)
        dyn = doa * silu
        dgn_ref[...] += jnp.sum(dyn * on, axis=0, keepdims=True)
        don = dyn * gnv
        dor_s[...] = r * (don - on * jnp.mean(don * on, axis=-1, keepdims=True))
        tidx = lax.broadcasted_iota(jnp.int32, (HG_HALF, HG_DIM), 0)

        def blk(ii, carry):
            i = nblk - 1 - ii
            r0 = pl.multiple_of(i * HG_BLOCK, HG_BLOCK)
            rows = pl.ds(r0, HG_BLOCK)
            st = st_ref[i]
            dst = dstate[...]
            dstb = dst.astype(BF16)
            do = dor_s[rows, :]
            dob = do.astype(BF16)
            v = iv_ref[rows, :]
            vb = v.astype(BF16)
            qq = q_ref[rows, :]
            kk = k_s[rows, :]
            bb = b_s[rows, :]
            qt = qt_s[rows, :]
            kt = kt_s[rows, :]
            dec = dec_s[pl.ds(r0, 1), :]
            dkt = _dot(vb, dstb)
            dq = _dot(dob, st) * eb_s[rows, :]
            dk = dkt * ekb_s[rows, :]
            dv = _dot_nt(kt, dstb)
            gend = (jnp.sum(kk * dk, axis=0, keepdims=True)
                    + dec * jnp.sum(dst * st.astype(F32), axis=0, keepdims=True))
            qh, bh, doh = _halves(qq), _halves(bb), _halves(do)
            dqh, dkh, dvh = list(_halves(dq)), list(_halves(dk)), list(_halves(dv))
            for s in range(HG_BLOCK):
                ks, vs = kk[s:s + 1, :], v[s:s + 1, :]
                dk_part = dv_part = None
                for h in _causal_halves(s):
                    e = _decay_from(bh[h], bb[s:s + 1, :], s, h, tidx)
                    ke = ks * e
                    acol = jnp.sum(qh[h] * ke, axis=1, keepdims=True)
                    dacol = jnp.sum(doh[h] * vs, axis=1, keepdims=True)
                    dqh[h] = dqh[h] + dacol * ke
                    pk = dacol * (qh[h] * e)
                    pv = acol * doh[h]
                    dk_part = pk if dk_part is None else dk_part + pk
                    dv_part = pv if dv_part is None else dv_part + pv
                hs, row = divmod(s, HG_HALF)
                dkh[hs] = dkh[hs] + jnp.where(tidx == row, jnp.sum(dk_part, axis=0, keepdims=True), 0.0)
                dvh[hs] = dvh[hs] + jnp.where(tidx == row, jnp.sum(dv_part, axis=0, keepdims=True), 0.0)
            dq = jnp.concatenate(dqh, axis=0)
            dk = jnp.concatenate(dkh, axis=0)
            dv = jnp.concatenate(dvh, axis=0)
            dq_ref[rows, :] = dq.astype(BF16)
            div_ref[rows, :] = dv.astype(BF16)
            dk_s[rows, :] = dk
            dbl_s[rows, :] = qq * dq - kk * dk
            gr_s[rows, :] = jnp.zeros((HG_BLOCK, HG_DIM), F32) + gend
            dstate[...] = dst * dec + _dot_tn(dob, qt)
            return carry

        lax.fori_loop(0, nblk, blk, 0, unroll=HG_UNROLL)
        dlogf = _mask_dot(mask_ref[1], dbl_s[...]) + gr_s[...]
        dk = dk_s[...]
        dfz_ref[...] = ((dlogf / f - dk) * ((1.0 - lb) * sig * nsig)).astype(BF16)
        dlb_acc[...] += jnp.sum((dlogf / f - dk) * nsig, axis=0, keepdims=True)

        @pl.when(c == nct - 1)
        def _():
            dl0 = dlb_acc[...] * lb * (1.0 - lb)
            layer = lax.broadcasted_iota(jnp.int32, (2, HG_DIM), 0)
            dlg_ref[...] = jnp.where(layer == 0, dl0, -dl0)

    def slab(off):
        return pl.BlockSpec((None, ct, HG_DIM), lambda h, c: (off + h, nct - 1 - c, 0))

    out_slab = pl.BlockSpec((ct, HG_DIM), lambda h, c: (nct - 1 - c, h))
    tile_f32 = pltpu.VMEM((ct, HG_DIM), F32)
    tile_b16 = pltpu.VMEM((ct, HG_DIM), BF16)
    slab_shape = jax.ShapeDtypeStruct((t, nh * HG_DIM), BF16)
    return pl.pallas_call(
        body,
        grid=(nh, nct),
        in_specs=[slab(0), slab(nh), slab(2 * nh), slab(3 * nh), slab(0), slab(0),
                  pl.BlockSpec((None, nblk, HG_DIM, HG_DIM), lambda h, c: (h, nct - 1 - c, 0, 0)),
                  pl.BlockSpec((None, 2, HG_DIM), lambda h, c: (h, 0, 0)),
                  pl.BlockSpec((1, HG_DIM), lambda h, c: (0, 0)),
                  pl.BlockSpec((3, mrows, mrows), lambda h, c: (0, 0, 0))],
        out_specs=[out_slab, out_slab, out_slab, out_slab,
                   pl.BlockSpec((None, 2, HG_DIM), lambda h, c: (h, 0, 0)),
                   pl.BlockSpec((None, 1, HG_DIM), lambda h, c: (h, 0, 0))],
        out_shape=[slab_shape, slab_shape, slab_shape, slab_shape,
                   jax.ShapeDtypeStruct((nh, 2, HG_DIM), F32), jax.ShapeDtypeStruct((nh, 1, HG_DIM), F32)],
        scratch_shapes=[pltpu.VMEM((HG_DIM, HG_DIM), F32), tile_b16, tile_b16, tile_f32, tile_f32, tile_f32, tile_f32,
                        tile_f32, tile_f32, tile_f32, tile_f32, tile_f32, pltpu.VMEM((1, HG_DIM), F32)],
        compiler_params=_params(2),
        name="hgrn_bwd",
    )(proj, proj, proj, proj, oraw, dmix, states, logits, gn, _block_masks(mrows))


def _sgu_bwd(proj, dmix, ln_g, ln_b, w_s, w_t, b_col):
    t = proj.shape[1]
    ct = _sg_tile(t)
    nct = t // ct
    ng = SG_GROUPS
    off_u = 4 * HG_HEADS
    off_v = off_u + ng
    n = SG_CHUNK

    def body(u_ref, v_ref, do_ref, g_ref, b_ref, w_ref, wt_ref, bs_ref, du_ref, dv_ref, dg_ref, db_ref, dw_ref, dbs_ref):
        c = pl.program_id(1)

        @pl.when(c == 0)
        def _():
            dg_ref[...] = jnp.zeros_like(dg_ref)
            db_ref[...] = jnp.zeros_like(db_ref)
            dw_ref[...] = jnp.zeros_like(dw_ref)
            dbs_ref[...] = jnp.zeros_like(dbs_ref)

        r = lax.broadcasted_iota(jnp.int32, (n, n), 0)
        cc = lax.broadcasted_iota(jnp.int32, (n, n), 1)
        wm = jnp.where(cc <= r, w_ref[...], 0.0).astype(BF16)
        wmt = jnp.where(r <= cc, wt_ref[...], 0.0).astype(BF16)
        for ci in range(ct // n):
            rows = slice(ci * n, (ci + 1) * n)
            ua, dua, dva, vn, xhat, rstd, s = _sgu_chunk_fwd(u_ref[rows, :], v_ref[rows, :], g_ref[...], b_ref[...],
                                                             wm, bs_ref[...])
            do = do_ref[rows, :]
            du_ref[rows, :] = (do * s * dua).astype(BF16)
            ds = do * ua
            dsb = ds.astype(BF16)
            dbs_ref[...] += jnp.sum(ds, axis=1, keepdims=True)
            dw_ref[...] += _dot_nt(dsb, vn.astype(BF16))
            dvn = _dot(wmt, dsb)
            dva_in, dg, db = _ln_bwd(dvn, xhat, rstd, g_ref[...])
            dg_ref[...] += dg
            db_ref[...] += db
            dv_ref[rows, :] = (dva_in * dva).astype(BF16)

        @pl.when(c == nct - 1)
        def _():
            dw_ref[...] = jnp.where(cc <= r, dw_ref[...], 0.0)

    vec = pl.BlockSpec((None, 1, SG_DIM), lambda g, c: (g, 0, 0))
    mat = pl.BlockSpec((None, n, n), lambda g, c: (g, 0, 0))
    col = pl.BlockSpec((None, n, 1), lambda g, c: (g, 0, 0))
    out_slab = pl.BlockSpec((ct, SG_DIM), lambda g, c: (c, g))
    return pl.pallas_call(
        body,
        grid=(ng, nct),
        in_specs=[pl.BlockSpec((None, ct, SG_DIM), lambda g, c: (off_u + g, c, 0)),
                  pl.BlockSpec((None, ct, SG_DIM), lambda g, c: (off_v + g, c, 0)),
                  pl.BlockSpec((None, ct, SG_DIM), lambda g, c: (ng + g, c, 0)), vec, vec, mat, mat, col],
        out_specs=[out_slab, out_slab, vec, vec, mat, col],
        out_shape=[jax.ShapeDtypeStruct((t, ng * SG_DIM), BF16), jax.ShapeDtypeStruct((t, ng * SG_DIM), BF16),
                   jax.ShapeDtypeStruct((ng, 1, SG_DIM), F32), jax.ShapeDtypeStruct((ng, 1, SG_DIM), F32),
                   jax.ShapeDtypeStruct((ng, n, n), F32), jax.ShapeDtypeStruct((ng, n, 1), F32)],
        compiler_params=_params(2),
        name="sgu_bwd",
    )(proj, proj, dmix, ln_g, ln_b, w_s, w_t, b_col)


def _attn_bwd(dyb, wo, qb, kb, vb):
    t, d = dyb.shape
    m_len = kb.shape[0]
    tm = _row_tile(t)
    dh = d // X_HEADS
    scale = dh ** -0.5

    def body(dy_ref, wo_ref, q_ref, k_ref, v_ref, dq_ref, dk_ref, dv_ref):
        i = pl.program_id(0)

        @pl.when(i == 0)
        def _():
            dk_ref[...] = jnp.zeros_like(dk_ref)
            dv_ref[...] = jnp.zeros_like(dv_ref)

        do = _dot_nt(dy_ref[...], wo_ref[...]).astype(BF16)
        for hd in range(X_HEADS):
            sl = slice(hd * dh, (hd + 1) * dh)
            qh = q_ref[:, sl]
            p = _softmax_rows(_dot_nt(qh, k_ref[:, sl]) * scale)
            doh = do[:, sl]
            dp = _dot_nt(doh, v_ref[:, sl])
            ds = (p * (dp - jnp.sum(dp * p, axis=-1, keepdims=True)) * scale).astype(BF16)
            dq_ref[:, sl] = _dot(ds, k_ref[:, sl]).astype(BF16)
            dk_ref[:, sl] += _dot_tn(ds, qh)
            dv_ref[:, sl] += _dot_tn(p.astype(BF16), doh)

    row = pl.BlockSpec((tm, d), lambda i: (i, 0))
    full = lambda a: pl.BlockSpec(a.shape, lambda i: (0, 0))
    kv = pl.BlockSpec((m_len, d), lambda i: (0, 0))
    return pl.pallas_call(
        body,
        grid=(t // tm,),
        in_specs=[row, full(wo), row, full(kb), full(vb)],
        out_specs=[row, kv, kv],
        out_shape=[jax.ShapeDtypeStruct((t, d), BF16), jax.ShapeDtypeStruct((m_len, d), F32),
                   jax.ShapeDtypeStruct((m_len, d), F32)],
        compiler_params=_params(1),
        name="attn_bwd",
    )(dyb, wo, qb, kb, vb)


def _mem_bwd(dk, dv, mb, xhat, rstd, g, wk, wv):
    m_len, d = dk.shape

    def body(dk_ref, dv_ref, mb_ref, xh_ref, rs_ref, g_ref, wk_ref, wv_ref, gwk_ref, gwv_ref, dg_ref, db_ref):
        dkb = dk_ref[...].astype(BF16)
        dvb = dv_ref[...].astype(BF16)
        mb_v = mb_ref[...]
        gwk_ref[...] = _dot_tn(mb_v, dkb).astype(BF16)
        gwv_ref[...] = _dot_tn(mb_v, dvb).astype(BF16)
        dm = _dot_nt(dkb, wk_ref[...]) + _dot_nt(dvb, wv_ref[...])
        _, dg, db = _ln_bwd(dm, xh_ref[...], rs_ref[...], g_ref[...])
        dg_ref[...] = dg
        db_ref[...] = db

    return pl.pallas_call(
        body,
        out_shape=[jax.ShapeDtypeStruct((d, d), BF16), jax.ShapeDtypeStruct((d, d), BF16),
                   jax.ShapeDtypeStruct((1, d), F32), jax.ShapeDtypeStruct((1, d), F32)],
        compiler_params=pltpu.CompilerParams(vmem_limit_bytes=VMEM_LIMIT_V7X),
        name="mem_bwd",
    )(dk, dv, mb, xhat, rstd, g, wk, wv)


def _adamw(w, g, m, v):
    m = ADAM_B1 * m + (1.0 - ADAM_B1) * g
    v = ADAM_B2 * v + (1.0 - ADAM_B2) * (g * g)
    m_hat = m / (1.0 - ADAM_B1 ** ADAM_STEP)
    v_hat = v / (1.0 - ADAM_B2 ** ADAM_STEP)
    delta = -ADAM_LR * (m_hat / (jnp.sqrt(v_hat) + ADAM_EPS) + ADAM_WD * w)
    return delta, m, v


def _slot_sum(ref):
    g = ref[0].astype(F32)
    for s in range(1, N_DEV):
        g = g + ref[s].astype(F32)
    return g


def _adam_sharded(lands, w, m, v, axis, name):
    rows, cols = w.shape
    nl = len(lands)
    transposed = axis == 1 and nl == 2
    if transposed:
        rows, cols = cols, rows
        tr = 256
        grid = (rows // tr,)
        wblk = pl.BlockSpec((cols, tr), lambda i: (0, i))
        lblk = [pl.BlockSpec((N_DEV, tr, a.shape[2]), lambda i: (0, i, 0)) for a in lands]
    elif axis == 1:
        tr = 256 if rows % 256 == 0 else rows
        grid = (rows // tr,)
        wblk = pl.BlockSpec((tr, cols), lambda i: (i, 0))
        lblk = [pl.BlockSpec((N_DEV, tr, a.shape[2]), lambda i: (0, i, 0)) for a in lands]
    else:
        tc = _col_tile(cols)
        grid = (cols // tc,)
        wblk = pl.BlockSpec((rows, tc), lambda i: (0, i))
        lblk = [pl.BlockSpec((N_DEV, a.shape[1], tc), lambda i: (0, 0, i)) for a in lands]

    def body(*refs):
        w_ref, m_ref, v_ref = refs[nl:nl + 3]
        g_ref, d_ref, nm_ref, nv_ref = refs[nl + 3:]
        g = _slot_sum(refs[0])
        if nl == 2:
            tail = _slot_sum(refs[1])
            if transposed:
                g = jnp.concatenate([g.T, tail.T[:cols - g.shape[1], :]], axis=0)
            elif axis == 1:
                g = jnp.concatenate([g, tail[:, :cols - g.shape[1]]], axis=1)
            else:
                g = jnp.concatenate([g, tail[:rows - g.shape[0], :]], axis=0)
        delta, nm, nv = _adamw(w_ref[...], g, m_ref[...], v_ref[...])
        g_ref[...] = g
        d_ref[...] = delta
        nm_ref[...] = nm
        nv_ref[...] = nv

    shp = jax.ShapeDtypeStruct(w.shape, F32)
    return pl.pallas_call(
        body,
        grid=grid,
        in_specs=lblk + [wblk, wblk, wblk],
        out_specs=[wblk, wblk, wblk, wblk],
        out_shape=[shp, shp, shp, shp],
        compiler_params=_params(1),
        name=name,
    )(*[pltpu.with_memory_space_constraint(a, pltpu.HBM) for a in (*lands, w, m, v)])


def _mesh_pos():
    return lax.axis_index("x"), lax.axis_index("y"), lax.axis_index("c")


def _peer(k):
    x, y, c = _mesh_pos()
    pos = (x ^ (k >> 2), y ^ ((k >> 1) & 1), c ^ (k & 1))
    return pos, 4 * pos[0] + 2 * pos[1] + pos[2]


def _sem_index(row, k):
    return row * (N_DEV - 1) + k - 1


def _window(ref, axis, start, size):
    align = 16 if axis == 0 else LANES
    start = pl.multiple_of(start, align)
    return ref.at[pl.ds(start, size), :] if axis == 0 else ref.at[:, pl.ds(start, size)]


def _piece_refs(piece, srcs, lands, me, peer):
    kind, si, li, axis, base, stride, shape = piece
    if kind == "gather":
        return srcs[si], _window(lands[li], axis, base + stride * me, shape[axis])
    return _window(srcs[si], axis, base + stride * peer, shape[axis]), lands[li].at[me]


def _place_own(srcs, land_shapes, pieces, name):
    ns, nl, npc = len(srcs), len(land_shapes), len(pieces)

    def body(*refs):
        s_refs = refs[:ns]
        l_refs = refs[ns:ns + nl]
        bufs = refs[ns + nl:ns + nl + npc]
        sems = refs[ns + nl + npc]
        x, y, c = _mesh_pos()
        me = 4 * x + 2 * y + c
        loads = []
        for p, piece in enumerate(pieces):
            src, dst = _piece_refs(piece, s_refs, l_refs, me, me)
            cp = pltpu.make_async_copy(src, bufs[p], sems.at[0, p])
            cp.start()
            loads.append((cp, dst))
        stores = []
        for p, (cp, dst) in enumerate(loads):
            cp.wait()
            out = pltpu.make_async_copy(bufs[p], dst, sems.at[1, p])
            out.start()
            stores.append(out)
        for out in stores:
            out.wait()

    out = pl.pallas_call(
        body,
        in_specs=[ANY] * ns,
        out_specs=[ANY] * nl,
        out_shape=list(land_shapes),
        scratch_shapes=[pltpu.VMEM(pc[6], srcs[pc[1]].dtype) for pc in pieces] + [pltpu.SemaphoreType.DMA((2, npc))],
        compiler_params=pltpu.CompilerParams(vmem_limit_bytes=VMEM_LIMIT_V7X),
        name=name,
    )(*srcs)
    return list(out)


def _comm_start(srcs, lands, pieces, groups, name, after=()):
    ns, nl, na, ng = len(srcs), len(lands), len(after), len(groups)

    def body(*refs):
        s_refs = refs[:ns]
        l_refs = refs[ns:ns + nl]
        outs = refs[ns + nl + na:]
        sems = outs[:2 * ng]
        token = outs[-1]
        x, y, c = _mesh_pos()
        me = 4 * x + 2 * y + c
        for g, members in enumerate(groups):
            for row, p in enumerate(members):
                for k in range(1, N_DEV):
                    pos, peer = _peer(k)
                    src, dst = _piece_refs(pieces[p], s_refs, l_refs, me, peer)
                    pltpu.make_async_remote_copy(src_ref=src, dst_ref=dst, send_sem=sems[2 * g].at[_sem_index(row, k)],
                                                 recv_sem=sems[2 * g + 1].at[_sem_index(row, k)], device_id=pos,
                                                 device_id_type=MESH_ID).start()
        token[...] = jnp.zeros_like(token)

    sem_shapes = []
    for members in groups:
        sem_shapes += [pltpu.SemaphoreType.DMA((len(members) * (N_DEV - 1),))] * 2
    hbm_of = lambda a: pltpu.HBM(a.shape, a.dtype)
    out = pl.pallas_call(
        body,
        in_specs=[HBM] * (ns + nl) + [ANY] * na,
        out_specs=[SEM] * (2 * ng) + [HBM] * (ns + nl) + [pl.BlockSpec(memory_space=pltpu.VMEM)],
        out_shape=sem_shapes + [hbm_of(a) for a in srcs] + [hbm_of(a) for a in lands]
        + [jax.ShapeDtypeStruct((8, LANES), F32)],
        input_output_aliases={i: 2 * ng + i for i in range(ns + nl)},
        compiler_params=pltpu.CompilerParams(has_side_effects=DATAFLOW),
        name=name,
    )(*[pltpu.with_memory_space_constraint(a, pltpu.HBM) for a in list(srcs) + list(lands)], *after)
    sems = [(out[2 * g], out[2 * g + 1]) for g in range(ng)]
    return sems, list(out[2 * ng:2 * ng + ns]), list(out[2 * ng + ns:2 * ng + ns + nl]), out[-1]


def _comm_wait(srcs, lands, pieces, members, sems, after, name):
    ns, nl, na = len(srcs), len(lands), len(after)

    def body(*refs):
        s_refs = refs[:ns]
        l_refs = refs[ns:ns + nl]
        send_sems, recv_sems = refs[ns + nl:ns + nl + 2]
        x, y, c = _mesh_pos()
        me = 4 * x + 2 * y + c
        for row, p in enumerate(members):
            for k in range(1, N_DEV):
                pos, peer = _peer(k)
                src, dst = _piece_refs(pieces[p], s_refs, l_refs, me, peer)
                cp = pltpu.make_async_remote_copy(src_ref=src, dst_ref=dst, send_sem=send_sems.at[_sem_index(row, k)],
                                                  recv_sem=recv_sems.at[_sem_index(row, k)], device_id=pos,
                                                  device_id_type=MESH_ID)
                cp.wait_send()
                cp.wait_recv()

    hbm_of = lambda a: pltpu.HBM(a.shape, a.dtype)
    out = pl.pallas_call(
        body,
        in_specs=[HBM] * (ns + nl) + [SEM, SEM] + [ANY] * na,
        out_specs=[HBM] * (ns + nl),
        out_shape=[hbm_of(a) for a in srcs] + [hbm_of(a) for a in lands],
        input_output_aliases={i: i for i in range(ns + nl)},
        compiler_params=pltpu.CompilerParams(has_side_effects=DATAFLOW),
        name=name,
    )(*srcs, *lands, sems[0], sems[1], *after)
    return list(out[ns:])


def _landed_block(piece, lands, owner):
    _, _, li, axis, base, stride, shape = piece
    return _window(lands[li], axis, base + stride * owner, shape[axis])


def _copy_stage(name, bufs, in_sems, out_sem_sizes, emit, after=()):
    nb, ni, no, na = len(bufs), len(in_sems), len(out_sem_sizes), len(after)

    def body(*refs):
        b_refs = refs[:nb]
        i_refs = refs[nb:nb + ni]
        o_refs = refs[nb + ni + na:nb + ni + na + no]
        emit(b_refs, i_refs, o_refs)
        refs[-1][...] = jnp.zeros_like(refs[-1])

    hbm_of = lambda a: pltpu.HBM(a.shape, a.dtype)
    out = pl.pallas_call(
        body,
        in_specs=[HBM] * nb + [SEM] * ni + [ANY] * na,
        out_specs=[SEM] * no + [HBM] * nb + [pl.BlockSpec(memory_space=pltpu.VMEM)],
        out_shape=[pltpu.SemaphoreType.DMA((n,)) for n in out_sem_sizes] + [hbm_of(a) for a in bufs]
        + [jax.ShapeDtypeStruct((8, LANES), F32)],
        input_output_aliases={i: no + i for i in range(nb)},
        compiler_params=pltpu.CompilerParams(has_side_effects=DATAFLOW),
        name=name,
    )(*[pltpu.with_memory_space_constraint(a, pltpu.HBM) for a in bufs], *in_sems, *after)
    return list(out[:no]), list(out[no:no + nb]), out[-1]


def _remote(src, dst, send, recv, to):
    return pltpu.make_async_remote_copy(src_ref=src, dst_ref=dst, send_sem=send, recv_sem=recv, device_id=to,
                                        device_id_type=MESH_ID)


def _routed_gather(srcs, lands, pieces, after, name):
    ns, npc = len(srcs), len(pieces)

    def places():
        x, y, c = _mesh_pos()
        index = lambda p: 4 * p[0] + 2 * p[1] + p[2]
        me, sib = (x, y, c), (x, y, 1 - c)
        xnb, ynb = (1 - x, y, c), (x, 1 - y, c)
        got_first = (x ^ (1 - c), y ^ c, c)
        pass_to = (x ^ c, y ^ (1 - c), c)
        diag = (1 - x, 1 - y, c)
        return index, me, sib, xnb, ynb, got_first, pass_to, diag

    def start(b, _, o):
        index, me, sib, xnb, ynb, *_rest = places()
        send_a, recv_sib, recv_nb = o
        for p, piece in enumerate(pieces):
            src, dst = _piece_refs(piece, b[:ns], b[ns:], index(me), 0)
            _remote(src, dst, send_a.at[3 * p], recv_sib.at[p], sib).start()
            _remote(src, dst, send_a.at[3 * p + 1], recv_nb.at[2 * p], xnb).start()
            _remote(src, dst, send_a.at[3 * p + 2], recv_nb.at[2 * p + 1], ynb).start()

    def pass_a(b, i, o):
        index, me, sib, xnb, ynb, got_first, pass_to, _diag = places()
        (recv_nb,) = i
        send_f, recv_f, send_d, recv_d = o
        for p, piece in enumerate(pieces):
            for j, nb in enumerate((xnb, ynb)):
                blk = _landed_block(piece, b, index(nb))
                _remote(blk, blk, send_f.at[2 * p + j], recv_nb.at[2 * p + j], sib).wait_recv()
                _remote(blk, blk, send_f.at[2 * p + j], recv_f.at[2 * p + j], sib).start()
            blk = _landed_block(piece, b, index(got_first))
            _remote(blk, blk, send_d.at[p], recv_d.at[p], pass_to).start()

    def pass_b(b, i, o):
        index, me, sib, *_mid, diag = places()
        (recv_d,) = i
        send_g, recv_g = o
        for p, piece in enumerate(pieces):
            blk = _landed_block(piece, b, index(diag))
            _remote(blk, blk, send_g.at[p], recv_d.at[p], sib).wait_recv()
            _remote(blk, blk, send_g.at[p], recv_g.at[p], sib).start()

    def last(b, i, _):
        index, me, sib, *_others = places()
        send_a, recv_sib, send_f, recv_f, send_d, send_g, recv_g = i
        for p, piece in enumerate(pieces):
            src, dst = _piece_refs(piece, b[:ns], b[ns:], index(me), 0)
            cp = lambda s_sem, r_sem: _remote(src, dst, s_sem, r_sem, sib)
            cp(send_a.at[3 * p], recv_sib.at[p]).wait_recv()
            cp(send_a.at[3 * p], recv_g.at[p]).wait_recv()
            for j in range(3):
                cp(send_a.at[3 * p + j], recv_sib.at[p]).wait_send()
            for j in range(2):
                cp(send_f.at[2 * p + j], recv_f.at[2 * p + j]).wait_recv()
                cp(send_f.at[2 * p + j], recv_f.at[2 * p + j]).wait_send()
            cp(send_d.at[p], recv_sib.at[p]).wait_send()
            cp(send_g.at[p], recv_sib.at[p]).wait_send()

    (send_a, recv_sib, recv_nb), bufs, _ = _copy_stage(name + "_start", list(srcs) + list(lands), [],
                                                       [3 * npc, npc, 2 * npc], start)
    srcs, lands = bufs[:ns], bufs[ns:]
    (send_f, recv_f, send_d, recv_d), lands, _ = _copy_stage(name + "_pass_a", lands, [recv_nb],
                                                             [2 * npc, 2 * npc, npc, npc],
                                                             lambda b, i, o: pass_a(b, i, o), after=after)
    (send_g, recv_g), lands, tok = _copy_stage(name + "_pass_b", lands, [recv_d], [npc, npc], pass_b)
    _, bufs, _ = _copy_stage(name + "_last", list(srcs) + list(lands),
                             [send_a, recv_sib, send_f, recv_f, send_d, send_g, recv_g], [], last)
    return bufs[ns:], tok


_SMALL_NAMES = ("ln1_g", "ln1_b", "hg_lb_logits", "hg_norm_g", "sg_ln_g", "sg_ln_b", "sg_w_s", "sg_b_s",
                "ln2_g", "ln2_b", "mem_ln_g", "mem_ln_b", "ln3_g", "ln3_b", "ln4_g", "ln4_b")


_VEC_NAMES = ("ln1_g", "ln1_b", "ln2_g", "ln2_b", "mem_ln_g", "mem_ln_b", "ln3_g", "ln3_b", "ln4_g", "ln4_b")
_ROW_NAMES = ("hg_lb_logits", "hg_norm_g", "sg_ln_g", "sg_ln_b", "sg_b_s", "sg_w_s")
VEC_ROWS = 16


def _row_plan(shapes):
    plan, pos = {}, 0
    for k in _ROW_NAMES:
        shp = shapes[k]
        slabs, off = [], pos
        for idx in itertools.product(*[range(dim) for dim in shp[:-2]]):
            slabs.append((idx, off, shp[-2]))
            off += shp[-2]
        plan[k] = (pos, slabs)
        pos = -(-off // 8) * 8
    return plan, -(-pos // 16) * 16


def _pack_small_grads(gs, shapes, loss):
    d = gs[_VEC_NAMES[0]].size
    vec = jnp.concatenate([gs[k].reshape(1, -1) for k in _VEC_NAMES] + [jnp.tile(loss, (1, d // LANES))], axis=0)
    vec = jnp.pad(vec, ((0, VEC_ROWS - vec.shape[0]), (0, 0)))
    plan, total = _row_plan(shapes)
    parts, pos = [], 0
    for k in _ROW_NAMES:
        first, slabs = plan[k]
        rows = gs[k].reshape(-1, LANES)
        end = slabs[-1][1] + slabs[-1][2]
        nxt = -(-end // 8) * 8
        parts.append(jnp.pad(rows, ((0, nxt - first - rows.shape[0]), (0, 0))))
        pos = nxt
    parts.append(jnp.zeros((total - pos, LANES), F32))
    return vec, jnp.concatenate(parts, axis=0)


def _adam_small(land_vec, land_rows, w, m, v):
    names = _VEC_NAMES + _ROW_NAMES
    n = len(names)
    shapes = {k: w[k].shape for k in names}
    plan, _ = _row_plan(shapes)

    def body(*refs):
        lv_ref, lr_ref = refs[:2]
        w_refs, m_refs, v_refs = refs[2:2 + n], refs[2 + n:2 + 2 * n], refs[2 + 2 * n:2 + 3 * n]
        outs = refs[2 + 3 * n:2 + 7 * n]
        loss_ref = refs[2 + 7 * n]
        gv_s, gr_s = refs[3 + 7 * n:]
        gv_s[...] = _slot_sum(lv_ref)
        gr_s[...] = _slot_sum(lr_ref)
        loss_ref[...] = gv_s[len(_VEC_NAMES):len(_VEC_NAMES) + 1, :LANES]
        for p, k in enumerate(names):
            if k in _VEC_NAMES:
                row = _VEC_NAMES.index(k)
                slabs = [((), None, None)]
            else:
                slabs = plan[k][1]
            for idx, off, rows in slabs:
                g = gv_s[row:row + 1, :] if off is None else gr_s[off:off + rows, :]
                sel = idx + (slice(None), slice(None))
                delta, nm, nv = _adamw(w_refs[p][sel], g, m_refs[p][sel], v_refs[p][sel])
                for o, val in zip(range(4), (g, delta, nm, nv)):
                    outs[o * n + p][sel] = val

    flat = lambda tree: [tree[k] for k in names]
    shp = [jax.ShapeDtypeStruct(shapes[k], F32) for k in names]
    out = pl.pallas_call(
        body,
        out_shape=shp * 4 + [jax.ShapeDtypeStruct((1, LANES), F32)],
        scratch_shapes=[pltpu.VMEM(land_vec.shape[1:], F32), pltpu.VMEM(land_rows.shape[1:], F32)],
        name="adam_small",
    )(land_vec, land_rows, *flat(w), *flat(m), *flat(v))
    return [dict(zip(names, out[o * n:(o + 1) * n])) for o in range(4)], out[4 * n]


_COL_FFN = ("ffn1_w_gate", "ffn1_w_up", "ffn2_w_gate", "ffn2_w_up")
_ROW_FFN = ("ffn1_w_down", "ffn2_w_down")
_ROW_SQ = ("w_out", "xa_w_q", "xa_w_k", "xa_w_v", "xa_w_o")
_BIG_NAMES = ("ffn1_w_gate", "ffn1_w_up", "ffn1_w_down", "w_in", "w_out", "xa_w_q", "xa_w_k", "xa_w_v", "xa_w_o",
              "ffn2_w_gate", "ffn2_w_up", "ffn2_w_down")


def _ffn_split(fs):
    main = (fs // MXU_WIDTH_V7X) * MXU_WIDTH_V7X
    tail = fs - main
    tail_pad = -(-tail // LANES) * LANES
    assert main > 0 and tail > 0
    return main, tail, tail_pad


def _layout(name, shard_shape):
    r, c = shard_shape
    if name in _COL_FFN:
        main, tail, pad = _ffn_split(c)
        return (r, N_DEV * (main + pad)), [(1, 0, main, (r, main), (0, main)),
                                           (1, N_DEV * main, pad, (r, pad), (main, c))]
    if name in _ROW_FFN:
        main, tail, pad = _ffn_split(r)
        return (N_DEV * (main + pad), c), [(0, 0, main, (main, c), (0, main)),
                                           (0, N_DEV * main, pad, (pad, c), (main, r))]
    if name == "w_in":
        return (r, N_DEV * c), [(1, 0, c, (r, c), (0, c))]
    return (N_DEV * r, c), [(0, 0, r, (r, c), (0, r))]


def _shard_pieces(name, shard):
    out = []
    for axis, _, _, shape, (lo, hi) in _layout(name, shard.shape)[1]:
        part = shard[lo:hi, :] if axis == 0 else shard[:, lo:hi]
        pad = [(0, shape[0] - part.shape[0]), (0, shape[1] - part.shape[1])]
        out.append(jnp.pad(part, pad).astype(BF16))
    return out


def _gather_plan(names, shards):
    srcs, land_shapes, pieces, index = [], [], [], {}
    for li, name in enumerate(names):
        shape2d, parts = _layout(name, shards[name].shape)
        land_shapes.append(jax.ShapeDtypeStruct(shape2d, BF16))
        index[name] = []
        for (axis, base, stride, shape, _), src in zip(parts, _shard_pieces(name, shards[name])):
            index[name].append(len(pieces))
            pieces.append(("gather", len(srcs), li, axis, base, stride, shape))
            srcs.append(src)
    return srcs, land_shapes, pieces, index


def _scatter_plan(names, grads, shard_shapes):
    srcs, land_shapes, pieces, index = [], [], [], {}
    for si, name in enumerate(names):
        _, parts = _layout(name, shard_shapes[name])
        srcs.append(grads[name])
        index[name] = []
        for axis, base, stride, shape, _ in parts:
            index[name].append(len(land_shapes))
            pieces.append(("scatter", si, len(land_shapes), axis, base, stride, shape))
            land_shapes.append(jax.ShapeDtypeStruct((N_DEV,) + shape, grads[name].dtype))
    return srcs, land_shapes, pieces, index


def _small_views(small):
    row = lambda a: a.reshape(1, -1)
    ln = {k: row(small[k]) for k in ("ln1_g", "ln1_b", "ln2_g", "ln2_b", "ln3_g", "ln3_b", "ln4_g", "ln4_b",
                                      "mem_ln_g", "mem_ln_b", "hg_norm_g")}
    sg_w = small["sg_w_s"].reshape(SG_GROUPS, SG_CHUNK, SG_CHUNK)
    sg = dict(logits=jnp.swapaxes(small["hg_lb_logits"], 0, 1),
              g=small["sg_ln_g"].reshape(SG_GROUPS, 1, SG_DIM), b=small["sg_ln_b"].reshape(SG_GROUPS, 1, SG_DIM),
              w=sg_w, wt=jnp.swapaxes(sg_w, 1, 2), bs=small["sg_b_s"].reshape(SG_GROUPS, SG_CHUNK, 1))
    return ln, sg


def _forward(x, mem, target, get_w, small, first_deps=()):
    ln, sg = _small_views(small)
    xb = x.astype(BF16)
    a1, b1, s1 = _ffn_up(xb, get_w("ffn1_w_gate", ()), get_w("ffn1_w_up", ()), "ffn1_up", deps=first_deps)
    h1b, xh1, rs1 = _mm_res_ln(s1, get_w("ffn1_w_down", (s1,)), x, ln["ln1_g"], ln["ln1_b"], 0.5, "ffn1_down_ln")
    proj = _mm_nn(h1b, get_w("w_in", (h1b,)), "mix_in")
    oraw, mix, states = _hgrn_fwd(proj, sg["logits"], ln["hg_norm_g"])
    mix = _sgu_fwd(proj, mix, sg["g"], sg["b"], sg["w"], sg["bs"])
    h2b, xh2, rs2 = _mm_res_ln(mix, get_w("w_out", (mix,)), (xh1, ln["ln1_g"], ln["ln1_b"]), ln["ln2_g"], ln["ln2_b"],
                               1.0, "mix_out_ln")
    mb, mxh, mrs, kb, vb = _mem_kv(mem, ln["mem_ln_g"], ln["mem_ln_b"], get_w("xa_w_k", (h2b,)), get_w("xa_w_v", (h2b,)))
    qb, att = _attn_fwd(h2b, get_w("xa_w_q", (kb,)), kb, vb)
    h3b, xh3, rs3 = _mm_res_ln(att, get_w("xa_w_o", (att,)), (xh2, ln["ln2_g"], ln["ln2_b"]), ln["ln3_g"], ln["ln3_b"],
                               1.0, "attn_out_ln")
    a2, b2, s2 = _ffn_up(h3b, get_w("ffn2_w_gate", (h3b,)), get_w("ffn2_w_up", (h3b,)), "ffn2_up")
    loss, dy4, dy4b, dg4, db4 = _mm_res_ln(s2, get_w("ffn2_w_down", (s2,)), (xh3, ln["ln3_g"], ln["ln3_b"]),
                                           ln["ln4_g"], ln["ln4_b"], 0.5, "ffn2_down_ln_loss", target=target)
    return dict(xb=xb, a1=a1, b1=b1, s1=s1, h1b=h1b, xh1=xh1, rs1=rs1, proj=proj, oraw=oraw, mix=mix, states=states,
                h2b=h2b, xh2=xh2, rs2=rs2, mb=mb, mxh=mxh, mrs=mrs, kb=kb, vb=vb, qb=qb, att=att, h3b=h3b, xh3=xh3,
                rs3=rs3, a2=a2, b2=b2, s2=s2, loss=loss, dy4=dy4, dy4b=dy4b, dg4=dg4, db4=db4)


def _backward(sv, wt, small, send):
    ln, sg = _small_views(small)
    gs = {"ln4_g": sv["dg4"], "ln4_b": sv["db4"]}
    loss, dy4, dy4b = sv["loss"], sv["dy4"], sv["dy4b"]
    g_down2 = _mm_tn(sv["s2"], dy4b, "g_ffn2_down", scale=0.5)
    da2, db2, dy3, dy3b, gs["ln3_g"], gs["ln3_b"] = _ffn_bwd_fused(
        dy4, dy4b, wt["ffn2_w_down"], wt["ffn2_w_gate"], wt["ffn2_w_up"], sv["a2"], sv["b2"], 0.5,
        (sv["xh3"], sv["rs3"], ln["ln3_g"]), "ffn2_bwd")
    g_gate2 = _mm_tn(sv["h3b"], da2, "g_ffn2_gate")
    g_up2 = _mm_tn(sv["h3b"], db2, "g_ffn2_up")
    tok = send(("ffn2_w_down", "ffn2_w_gate", "ffn2_w_up"), (g_down2, g_gate2, g_up2))

    g_o = _mm_tn(sv["att"], dy3b, "g_xa_o", deps=(tok,))
    dqb, dk, dv = _attn_bwd(dy3b, wt["xa_w_o"], sv["qb"], sv["kb"], sv["vb"])
    g_q = _mm_tn(sv["h2b"], dqb, "g_xa_q")
    g_k, g_v, gs["mem_ln_g"], gs["mem_ln_b"] = _mem_bwd(dk, dv, sv["mb"], sv["mxh"], sv["mrs"], ln["mem_ln_g"],
                                                        wt["xa_w_k"], wt["xa_w_v"])
    tok = send(("xa_w_o", "xa_w_q", "xa_w_k", "xa_w_v"), (g_o, g_q, g_k, g_v))
    dy2, dy2b, gs["ln2_g"], gs["ln2_b"] = _dx_ln(dy3, [(dqb, wt["xa_w_q"])], (sv["xh2"], sv["rs2"], ln["ln2_g"]),
                                                 "attn_dx_ln", deps=(tok,))

    g_out = _mm_tn(sv["mix"], dy2b, "g_w_out")
    dmix = _mm_nt(dy2b, wt["w_out"], "mix_out_bwd")
    dq, dfz, div, dgg, dlg, dgn = _hgrn_bwd(sv["proj"], sv["oraw"], dmix, sv["states"], sg["logits"], ln["hg_norm_g"])
    du, dvv, gs["sg_ln_g"], gs["sg_ln_b"], gs["sg_w_s"], gs["sg_b_s"] = _sgu_bwd(
        sv["proj"], dmix, sg["g"], sg["b"], sg["w"], sg["wt"], sg["bs"])
    gs["hg_lb_logits"] = jnp.swapaxes(dlg, 0, 1)
    gs["hg_norm_g"] = jnp.sum(dgn, axis=0)
    dproj = jnp.concatenate([dq, dfz, div, dgg, du, dvv], axis=1)
    g_in = _mm_tn(sv["h1b"], dproj, "g_w_in")
    tok = send(("w_out", "w_in"), (g_out, g_in))
    dy1, dy1b, gs["ln1_g"], gs["ln1_b"] = _dx_ln(dy2, [(dproj, wt["w_in"])], (sv["xh1"], sv["rs1"], ln["ln1_g"]),
                                                 "mix_dx_ln", deps=(tok,))

    g_down1 = _mm_tn(sv["s1"], dy1b, "g_ffn1_down", scale=0.5)
    tok = send(("ffn1_w_down",), (g_down1,))
    da1, db1 = _ffn_bwd_act(dy1b, wt["ffn1_w_down"], sv["a1"], sv["b1"], 0.5, "ffn1_bwd_act", deps=(tok,))
    g_gate1 = _mm_tn(sv["xb"], da1, "g_ffn1_gate")
    tok = send(("ffn1_w_gate",), (g_gate1,))
    g_up1 = _mm_tn(sv["xb"], db1, "g_ffn1_up", deps=(tok,))
    tok = send(("ffn1_w_up",), (g_up1,))
    grad_x = _dx_ln(dy1, [(da1, wt["ffn1_w_gate"]), (db1, wt["ffn1_w_up"])], None, "ffn1_dx", deps=(tok,))
    return loss, grad_x, gs


_WEIGHT_NAMES = ("ffn1_w_gate", "ffn1_w_up", "ffn1_w_down", "ln1_g", "ln1_b", "w_in", "hg_lb_logits", "hg_norm_g",
                 "sg_ln_g", "sg_ln_b", "sg_w_s", "sg_b_s", "w_out", "ln2_g", "ln2_b", "mem_ln_g", "mem_ln_b",
                 "xa_w_q", "xa_w_k", "xa_w_v", "xa_w_o", "ln3_g", "ln3_b", "ffn2_w_gate", "ffn2_w_up", "ffn2_w_down",
                 "ln4_g", "ln4_b")
_FIRST = ("ffn1_w_gate", "ffn1_w_up")
_SECOND = ("ffn1_w_down", "w_in", "w_out")
_THIRD = ("xa_w_k", "xa_w_v", "xa_w_q", "xa_w_o", "ffn2_w_gate", "ffn2_w_up", "ffn2_w_down")


def kernel(x, mem, ffn1_w_gate, ffn1_w_up, ffn1_w_down, ln1_g, ln1_b, w_in, hg_lb_logits, hg_norm_g, sg_ln_g, sg_ln_b, sg_w_s, sg_b_s, w_out, ln2_g, ln2_b, mem_ln_g, mem_ln_b, xa_w_q, xa_w_k, xa_w_v, xa_w_o, ln3_g, ln3_b, ffn2_w_gate, ffn2_w_up, ffn2_w_down, ln4_g, ln4_b, loss_target, m_ffn1_w_gate, m_ffn1_w_up, m_ffn1_w_down, m_ln1_g, m_ln1_b, m_w_in, m_hg_lb_logits, m_hg_norm_g, m_sg_ln_g, m_sg_ln_b, m_sg_w_s, m_sg_b_s, m_w_out, m_ln2_g, m_ln2_b, m_mem_ln_g, m_mem_ln_b, m_xa_w_q, m_xa_w_k, m_xa_w_v, m_xa_w_o, m_ln3_g, m_ln3_b, m_ffn2_w_gate, m_ffn2_w_up, m_ffn2_w_down, m_ln4_g, m_ln4_b, v_ffn1_w_gate, v_ffn1_w_up, v_ffn1_w_down, v_ln1_g, v_ln1_b, v_w_in, v_hg_lb_logits, v_hg_norm_g, v_sg_ln_g, v_sg_ln_b, v_sg_w_s, v_sg_b_s, v_w_out, v_ln2_g, v_ln2_b, v_mem_ln_g, v_mem_ln_b, v_xa_w_q, v_xa_w_k, v_xa_w_v, v_xa_w_o, v_ln3_g, v_ln3_b, v_ffn2_w_gate, v_ffn2_w_up, v_ffn2_w_down, v_ln4_g, v_ln4_b):
    args = dict(locals())
    w = {k: args[k] for k in _WEIGHT_NAMES}
    m = {k: args["m_" + k] for k in _WEIGHT_NAMES}
    v = {k: args["v_" + k] for k in _WEIGHT_NAMES}
    shards = {k: w[k][0] for k in _BIG_NAMES}
    shard_shapes = {k: shards[k].shape for k in _BIG_NAMES}
    small = {k: (w[k][0] if k != "hg_lb_logits" else w[k]) for k in _SMALL_NAMES}

    srcs1, shapes1, pieces1, idx1 = _gather_plan(_FIRST, shards)
    lands1 = _place_own(srcs1, shapes1, pieces1, "gather_first_own")
    rest = _SECOND + _THIRD
    srcs2, shapes2, pieces2, idx2 = _gather_plan(rest, shards)
    lands2 = _place_own(srcs2, shapes2, pieces2, "gather_rest_own")
    groups2 = [list(idx2[k]) for k in rest]
    lands1, tok1 = _routed_gather(srcs1, lands1, pieces1, tuple(lands2), "gather_first")
    sems2, srcs2, lands2, tok2 = _comm_start(srcs2, lands2, pieces2, groups2, "gather_rest_start", after=(tok1,))
    wt = dict(zip(_FIRST, lands1))
    pending = {k: gi for gi, k in enumerate(rest)}

    def get_w(name, after):
        if name in pending:
            gi = pending.pop(name)
            si = [pieces2[p][1] for p in groups2[gi]]
            sub = [(pieces2[p][0], row, 0) + pieces2[p][3:] for row, p in enumerate(groups2[gi])]
            wt[name] = _comm_wait([srcs2[s] for s in si], [lands2[gi]], sub, list(range(len(sub))), sems2[gi],
                                  after, "gather_wait_" + name)[0]
        return wt[name]

    sv = _forward(x[0], mem[0], loss_target[0], get_w, small, first_deps=(tok2,))

    sent = []

    def send(names, grads):
        srcs, shapes, pieces, idx = _scatter_plan(names, dict(zip(names, grads)), shard_shapes)
        lands = _place_own(srcs, shapes, pieces, "grads_own_%d" % len(sent))
        sems, srcs, lands, tok = _comm_start(srcs, lands, pieces, [list(range(len(pieces)))],
                                             "grads_start_%d" % len(sent))
        sent.append((names, srcs, lands, pieces, idx, sems[0]))
        return tok

    loss, grad_x, gs = _backward(sv, wt, small, send)

    ssrc = list(_pack_small_grads(gs, {k: w[k].shape for k in _SMALL_NAMES}, loss))
    sp = [("scatter", i, i, 0, 0, 0, a.shape) for i, a in enumerate(ssrc)]
    sshape = [jax.ShapeDtypeStruct((N_DEV,) + a.shape, F32) for a in ssrc]
    sl = _place_own(ssrc, sshape, sp, "small_own")
    ssem, ssrc, sl, _ = _comm_start(ssrc, sl, sp, [[0, 1]], "small_start")

    out_g, out_d, out_m, out_v = {}, {}, {}, {}
    after = (grad_x,)
    for n_sent, (names, srcs, lands, pieces, idx, sems) in enumerate(sent):
        lands = _comm_wait(srcs, lands, pieces, list(range(len(pieces))), sems, after, "grads_wait_%d" % n_sent)
        for k in names:
            axis = 1 if (k in _COL_FFN or k == "w_in") else 0
            if k in _COL_FFN:
                res = _adam_sharded([lands[i] for i in idx[k]], w[k][0].T, m[k][0].T, v[k][0].T, axis, "adam_" + k)
                res = [r.T for r in res]
            else:
                res = _adam_sharded([lands[i] for i in idx[k]], w[k][0], m[k][0], v[k][0], axis, "adam_" + k)
            out_g[k], out_d[k], out_m[k], out_v[k] = [r[None] for r in res]
        after = (out_v[names[-1]],)
    sl = _comm_wait(ssrc, sl, sp, [0, 1], ssem[0], after, "small_wait")
    small_out, loss_sum = _adam_small(sl[0], sl[1], w, m, v)
    for dst, res in zip((out_g, out_d, out_m, out_v), small_out):
        dst.update(res)
    loss_all = loss_sum[0, 0]
    return (loss_all, grad_x[None], *[out_g[k] for k in _WEIGHT_NAMES], *[out_d[k] for k in _WEIGHT_NAMES],
            *[out_m[k] for k in _WEIGHT_NAMES], *[out_v[k] for k in _WEIGHT_NAMES])
```

```python
import itertools

import jax
import jax.numpy as jnp
import numpy as np
from jax import lax
from jax.experimental import pallas as pl
from jax.experimental.pallas import tpu as pltpu

F32 = jnp.float32
BF16 = jnp.bfloat16

N_DEV = 8
ALPHA = 2.0 ** 0.25
LN_EPS = 1e-5
HG_HEADS = 4
HG_DIM = 128
SG_GROUPS = 4
SG_DIM = 128
SG_CHUNK = 128
X_HEADS = 4
HG_BLOCK = 16
HG_UNROLL = 16
ADAM_LR = 0.001
ADAM_B1 = 0.9
ADAM_B2 = 0.999
ADAM_EPS = 1e-08
ADAM_WD = 0.01
ADAM_STEP = 10
VMEM_LIMIT_V7X = 48 * 1024 * 1024
MXU_WIDTH_V7X = 256
LANES = 128
MESH_ID = pl.DeviceIdType.MESH
ANY = pl.BlockSpec(memory_space=pl.ANY)
HBM = pl.BlockSpec(memory_space=pltpu.HBM)
SEM = pl.BlockSpec(memory_space=pltpu.SEMAPHORE)
DATAFLOW = pltpu.SideEffectType.DATAFLOW_SIDE_EFFECTING


def _params(n_axes):
    return pltpu.CompilerParams(dimension_semantics=("arbitrary",) * n_axes, vmem_limit_bytes=VMEM_LIMIT_V7X)


def _dot(a, b):
    return jnp.dot(a, b, preferred_element_type=F32)


def _dot_nt(a, b):
    return lax.dot_general(a, b, (((1,), (1,)), ((), ())), preferred_element_type=F32)


def _dot_tn(a, b):
    return lax.dot_general(a, b, (((0,), (0,)), ((), ())), preferred_element_type=F32)


def _sigmoid(x):
    return 1.0 / (1.0 + jnp.exp(-x))


def _silu_and_grad(a):
    sig = _sigmoid(a)
    return a * sig, sig * (1.0 + a * (1.0 - sig))


_GELU_C = 0.7978845608028654


def _gelu_and_grad(x):
    inner = _GELU_C * (x + 0.044715 * x * x * x)
    t = jnp.tanh(inner)
    val = 0.5 * x * (1.0 + t)
    grad = 0.5 * (1.0 + t) + 0.5 * x * (1.0 - t * t) * _GELU_C * (1.0 + 3.0 * 0.044715 * x * x)
    return val, grad


def _ln_fwd(y, g, b):
    mu = jnp.mean(y, axis=-1, keepdims=True)
    yc = y - mu
    var = jnp.mean(yc * yc, axis=-1, keepdims=True)
    rstd = lax.rsqrt(var + LN_EPS)
    xhat = yc * rstd
    return xhat * g + b, xhat, rstd


def _ln_bwd(dh, xhat, rstd, g):
    dxh = dh * g
    m1 = jnp.mean(dxh, axis=-1, keepdims=True)
    m2 = jnp.mean(dxh * xhat, axis=-1, keepdims=True)
    dy = rstd * (dxh - m1 - xhat * m2)
    dg = jnp.sum(dh * xhat, axis=0, keepdims=True)
    db = jnp.sum(dh, axis=0, keepdims=True)
    return dy, dg, db


def _mask_dot(mask, x):
    hi = x.astype(BF16)
    lo = (x - hi.astype(F32)).astype(BF16)
    n = mask.shape[0]
    parts = [_dot(mask, hi[r:r + n, :]) + _dot(mask, lo[r:r + n, :]) for r in range(0, x.shape[0], n)]
    return parts[0] if len(parts) == 1 else jnp.concatenate(parts, axis=0)


def _block_masks(n):
    r = np.arange(n)[:, None]
    c = np.arange(n)[None, :]
    same = (r // HG_BLOCK) == (c // HG_BLOCK)
    return jnp.asarray(np.stack([same & (c <= r), same & (c >= r), same]), BF16)


def _row_tile(t):
    return min(t, 512)


def _col_tile(n):
    for cand in (512, 256, 128):
        if n % cand == 0:
            return cand
    return n


def _resident(w):
    return pl.BlockSpec(w.shape, lambda *_: (0, 0), pipeline_mode=pl.Buffered(1))


def _window_loads(w_hbm, w_vmem, sems, first_sem, size, axis):
    def window(ref, c):
        span = pl.ds(c * size, size)
        return ref.at[span, :] if axis == 0 else ref.at[:, span]
    return [pltpu.make_async_copy(window(w_hbm, c), window(w_vmem, c), sems.at[first_sem + c])
            for c in range(w_hbm.shape[axis] // size)]


def _with_first_step_loads(per_weight, compute):
    first = pl.program_id(0) == 0

    @pl.when(first)
    def _():
        for loads in per_weight:
            for load in loads:
                load.start()

        def wait(c):
            for loads in per_weight:
                loads[c].wait()

        compute(wait)

    @pl.when(jnp.logical_not(first))
    def _():
        compute(lambda c: None)


def _drop_deps(body, n_in, n_deps):
    if n_deps == 0:
        return body
    return lambda *refs: body(*refs[:n_in], *refs[n_in + n_deps:])


def _ffn_up(hb, wg, wu, name, deps=()):
    t, d = hb.shape
    f = wg.shape[1]
    tm = _row_tile(t)
    tn = _col_tile(f)
    nc = f // tn

    def body(h_ref, wg_hbm, wu_hbm, a_ref, b_ref, s_ref, wg_ref, wu_ref, sems):
        def compute(wait):
            h = h_ref[...]
            for c in range(nc):
                cols = slice(c * tn, (c + 1) * tn)
                wait(c)
                a = _dot(h, wg_ref[:, cols])
                b = _dot(h, wu_ref[:, cols])
                a_ref[:, cols] = a.astype(BF16)
                b_ref[:, cols] = b.astype(BF16)
                s_ref[:, cols] = (a * _sigmoid(a) * b).astype(BF16)

        _with_first_step_loads([_window_loads(wg_hbm, wg_ref, sems, 0, tn, 1),
                                _window_loads(wu_hbm, wu_ref, sems, nc, tn, 1)], compute)

    act = pl.BlockSpec((tm, f), lambda i: (i, 0))
    return pl.pallas_call(
        _drop_deps(body, 3, len(deps)),
        grid=(t // tm,),
        in_specs=[pl.BlockSpec((tm, d), lambda i: (i, 0)), ANY, ANY] + [ANY] * len(deps),
        out_specs=[act, act, act],
        out_shape=[jax.ShapeDtypeStruct((t, f), BF16)] * 3,
        scratch_shapes=[pltpu.VMEM(wg.shape, BF16), pltpu.VMEM(wu.shape, BF16), pltpu.SemaphoreType.DMA((2 * nc,))],
        compiler_params=_params(1),
        name=name,
    )(hb, wg, wu, *deps)


def _mm_res_ln(lhs, w, res, g, b, coef, name, target=None):
    t, kd = lhs.shape
    d = w.shape[1]
    tm = _row_tile(t)
    nt = t // tm
    from_norm = isinstance(res, tuple)
    n_res = 3 if from_norm else 1

    def body(*refs):
        l_ref, w_ref = refs[:2]
        r_refs = refs[2:2 + n_res]
        g_ref, b_ref = refs[2 + n_res:4 + n_res]
        rest = refs[4 + n_res:]
        prev = r_refs[0][...] * r_refs[1][...] + r_refs[2][...] if from_norm else r_refs[0][...]
        y = ALPHA * prev + coef * _dot(l_ref[...], w_ref[...])
        h, xhat, rstd = _ln_fwd(y, g_ref[...], b_ref[...])
        if target is None:
            hb_ref, xh_ref, rs_ref = rest
            hb_ref[...] = h.astype(BF16)
            xh_ref[...] = xhat
            rs_ref[...] = rstd
            return
        t_ref, loss_ref, dy_ref, dyb_ref, dg_ref, db_ref, lacc = rest
        i = pl.program_id(0)

        @pl.when(i == 0)
        def _():
            lacc[...] = jnp.zeros_like(lacc)
            dg_ref[...] = jnp.zeros_like(dg_ref)
            db_ref[...] = jnp.zeros_like(db_ref)

        err = h - t_ref[...]
        lacc[...] += jnp.sum(err * err, axis=0, keepdims=True)
        dy, dg, db = _ln_bwd(err * (1.0 / d), xhat, rstd, g_ref[...])
        dy_ref[...] = dy
        dyb_ref[...] = dy.astype(BF16)
        dg_ref[...] += dg
        db_ref[...] += db

        @pl.when(i == nt - 1)
        def _():
            loss_ref[...] = jnp.zeros_like(loss_ref) + jnp.sum(lacc[...], axis=1, keepdims=True) * (0.5 / d)

    row = pl.BlockSpec((tm, d), lambda i: (i, 0))
    vec = pl.BlockSpec((1, d), lambda i: (0, 0))
    res_specs = [row, vec, vec] if from_norm else [row]
    res_args = list(res) if from_norm else [res]
    in_specs = [pl.BlockSpec((tm, kd), lambda i: (i, 0)), _resident(w)] + res_specs + [vec, vec]
    args = [lhs, w] + res_args + [g, b]
    if target is None:
        out_specs = [row, row, pl.BlockSpec((tm, 1), lambda i: (i, 0))]
        out_shape = [jax.ShapeDtypeStruct((t, d), BF16), jax.ShapeDtypeStruct((t, d), F32),
                     jax.ShapeDtypeStruct((t, 1), F32)]
        scratch = []
    else:
        in_specs.append(row)
        args.append(target)
        out_specs = [pl.BlockSpec((1, LANES), lambda i: (0, 0)), row, row, vec, vec]
        out_shape = [jax.ShapeDtypeStruct((1, LANES), F32), jax.ShapeDtypeStruct((t, d), F32),
                     jax.ShapeDtypeStruct((t, d), BF16), jax.ShapeDtypeStruct((1, d), F32),
                     jax.ShapeDtypeStruct((1, d), F32)]
        scratch = [pltpu.VMEM((1, d), F32)]
    return pl.pallas_call(
        body,
        grid=(nt,),
        in_specs=in_specs,
        out_specs=out_specs,
        out_shape=out_shape,
        scratch_shapes=scratch,
        compiler_params=_params(1),
        name=name,
    )(*args)


def _store_slabs(o_ref, first, tile):
    for s in range(tile.shape[1] // LANES):
        o_ref[first + s] = tile[:, s * LANES:(s + 1) * LANES]


def _mm_nn(lhs, w, name):
    t, kd = lhs.shape
    n = w.shape[1]
    tm = _row_tile(t)
    tn = _col_tile(n)

    def body(l_ref, w_ref, o_ref):
        lhs_v = l_ref[...]
        for c in range(n // tn):
            _store_slabs(o_ref, c * (tn // LANES), _dot(lhs_v, w_ref[:, c * tn:(c + 1) * tn]))

    return pl.pallas_call(
        body,
        grid=(t // tm,),
        in_specs=[pl.BlockSpec((tm, kd), lambda i: (i, 0)), _resident(w)],
        out_specs=pl.BlockSpec((n // LANES, tm, LANES), lambda i: (0, i, 0)),
        out_shape=jax.ShapeDtypeStruct((n // LANES, t, LANES), F32),
        compiler_params=_params(1),
        name=name,
    )(lhs, w)


def _lower_bound(lg):
    m = jnp.max(lg, axis=0, keepdims=True)
    e = jnp.exp(lg - m)
    return e[0:1, :] / jnp.sum(e, axis=0, keepdims=True)


def _forget_terms(fz, lb):
    e = jnp.exp(-jnp.abs(fz))
    r = 1.0 / (1.0 + e)
    pos = fz >= 0.0
    sig = jnp.where(pos, r, e * r)
    nsig = jnp.where(pos, e * r, r)
    f = lb + (1.0 - lb) * sig
    k = (1.0 - lb) * nsig
    return sig, nsig, f, k


def _hg_tile(t):
    return min(t, 1024)


HG_HALF = HG_BLOCK // 2
NEG_BIG = -1e30


def _halves(a):
    return a[:HG_HALF, :], a[HG_HALF:, :]


def _causal_halves(s):
    return (0, 1) if s < HG_HALF else (1,)


def _decay_from(b_half, b_s, s, h, tidx):
    first = s - h * HG_HALF
    diff = b_half - b_s
    if first > 0:
        diff = jnp.where(tidx >= first, diff, NEG_BIG)
    return jnp.exp(diff)


def _hgrn_fwd(proj, logits, gn):
    t = proj.shape[1]
    ct = _hg_tile(t)
    nct = t // ct
    nblk = ct // HG_BLOCK
    nh = HG_HEADS
    mrows = min(ct, 256)

    def body(q_ref, fz_ref, iv_ref, gg_ref, lg_ref, gn_ref, mask_ref, oraw_ref, oa_ref, st_ref,
             state, qt_s, kt_s, k_s, b_s, dec_s):
        c = pl.program_id(1)

        @pl.when(c == 0)
        def _():
            state[...] = jnp.zeros_like(state)

        lb = _lower_bound(lg_ref[...])
        q = q_ref[...]
        _, _, f, k = _forget_terms(fz_ref[...], lb)
        logf = jnp.log(f)
        b = _mask_dot(mask_ref[0], logf)
        bend = _mask_dot(mask_ref[2], logf)
        qt_s[...] = (q * jnp.exp(b)).astype(BF16)
        kt_s[...] = (k * jnp.exp(bend - b)).astype(BF16)
        k_s[...] = k
        b_s[...] = b
        dec_s[...] = jnp.exp(bend)
        tidx = lax.broadcasted_iota(jnp.int32, (HG_HALF, HG_DIM), 0)

        def blk(i, carry):
            r0 = pl.multiple_of(i * HG_BLOCK, HG_BLOCK)
            rows = pl.ds(r0, HG_BLOCK)
            st = state[...]
            stb = st.astype(BF16)
            st_ref[i] = stb
            v = iv_ref[rows, :]
            qq = q_ref[rows, :]
            kk = k_s[rows, :]
            bb = b_s[rows, :]
            o = list(_halves(_dot_nt(qt_s[rows, :], stb)))
            qh, bh = _halves(qq), _halves(bb)
            for s in range(HG_BLOCK):
                ks, vs = kk[s:s + 1, :], v[s:s + 1, :]
                for h in _causal_halves(s):
                    e = _decay_from(bh[h], bb[s:s + 1, :], s, h, tidx)
                    acol = jnp.sum(qh[h] * (ks * e), axis=1, keepdims=True)
                    o[h] = o[h] + acol * vs
            oraw_ref[rows, :] = jnp.concatenate(o, axis=0)
            state[...] = st * dec_s[pl.ds(r0, 1), :] + _dot_tn(v.astype(BF16), kt_s[rows, :])
            return carry

        lax.fori_loop(0, nblk, blk, 0, unroll=HG_UNROLL)
        oraw = oraw_ref[...]
        r = lax.rsqrt(jnp.mean(oraw * oraw, axis=-1, keepdims=True) + LN_EPS)
        gg = gg_ref[...]
        oa_ref[...] = (oraw * r * gn_ref[...] * gg * _sigmoid(gg)).astype(BF16)

    def slab(off):
        return pl.BlockSpec((None, ct, HG_DIM), lambda h, c: (off + h, c, 0))

    return pl.pallas_call(
        body,
        grid=(nh, nct),
        in_specs=[slab(0), slab(nh), slab(2 * nh), slab(3 * nh),
                  pl.BlockSpec((None, 2, HG_DIM), lambda h, c: (h, 0, 0)),
                  pl.BlockSpec((1, HG_DIM), lambda h, c: (0, 0)),
                  pl.BlockSpec((3, mrows, mrows), lambda h, c: (0, 0, 0))],
        out_specs=[slab(0), pl.BlockSpec((ct, HG_DIM), lambda h, c: (c, h)),
                   pl.BlockSpec((None, nblk, HG_DIM, HG_DIM), lambda h, c: (h, c, 0, 0))],
        out_shape=[jax.ShapeDtypeStruct((nh, t, HG_DIM), F32),
                   jax.ShapeDtypeStruct((t, (nh + SG_GROUPS) * HG_DIM), BF16),
                   jax.ShapeDtypeStruct((nh, t // HG_BLOCK, HG_DIM, HG_DIM), BF16)],
        scratch_shapes=[pltpu.VMEM((HG_DIM, HG_DIM), F32), pltpu.VMEM((ct, HG_DIM), BF16),
                        pltpu.VMEM((ct, HG_DIM), BF16), pltpu.VMEM((ct, HG_DIM), F32),
                        pltpu.VMEM((ct, HG_DIM), F32), pltpu.VMEM((ct, HG_DIM), F32)],
        compiler_params=_params(2),
        name="hgrn_fwd",
    )(proj, proj, proj, proj, logits, gn, _block_masks(mrows))


def _sg_tile(t):
    return min(t, 512)


def _sgu_chunk_fwd(u, v, ln_g, ln_b, wm, bs):
    ua, dua = _gelu_and_grad(u)
    va, dva = _gelu_and_grad(v)
    vn, xhat, rstd = _ln_fwd(va, ln_g, ln_b)
    s = _dot(wm, vn.astype(BF16)) + bs
    return ua, dua, dva, vn, xhat, rstd, s


def _tril_weight(w):
    n = SG_CHUNK
    r = lax.broadcasted_iota(jnp.int32, (n, n), 0)
    c = lax.broadcasted_iota(jnp.int32, (n, n), 1)
    return jnp.where(c <= r, w, 0.0)


def _sgu_fwd(proj, mix, ln_g, ln_b, w_s, b_col):
    t = proj.shape[1]
    ct = _sg_tile(t)
    ng = SG_GROUPS
    wide = ng * SG_DIM
    blk_u = 4 * HG_HEADS // ng

    def body(u_ref, v_ref, g_ref, b_ref, w_ref, bs_ref, mix_ref, o_ref):
        del mix_ref
        for g in range(ng):
            lanes = slice(g * SG_DIM, (g + 1) * SG_DIM)
            wm = _tril_weight(w_ref[g]).astype(BF16)
            for n in range(ct // SG_CHUNK):
                rows = slice(n * SG_CHUNK, (n + 1) * SG_CHUNK)
                ua, _, _, _, _, _, s = _sgu_chunk_fwd(u_ref[g, rows, :], v_ref[g, rows, :], g_ref[g], b_ref[g], wm,
                                                      bs_ref[g])
                o_ref[rows, lanes] = (ua * s).astype(BF16)

    full = lambda a: pl.BlockSpec(a.shape, lambda c: (0,) * a.ndim)
    return pl.pallas_call(
        body,
        grid=(t // ct,),
        in_specs=[pl.BlockSpec((ng, ct, SG_DIM), lambda c: (blk_u, c, 0)),
                  pl.BlockSpec((ng, ct, SG_DIM), lambda c: (blk_u + 1, c, 0)),
                  full(ln_g), full(ln_b), full(w_s), full(b_col), ANY],
        out_specs=pl.BlockSpec((ct, wide), lambda c: (c, 1)),
        out_shape=jax.ShapeDtypeStruct(mix.shape, mix.dtype),
        input_output_aliases={6: 0},
        compiler_params=_params(1),
        name="sgu_fwd",
    )(proj, proj, ln_g, ln_b, w_s, b_col, mix)


def _mem_kv(mem, g, b, wk, wv):
    m_len, d = mem.shape

    def body(m_ref, g_ref, b_ref, wk_ref, wv_ref, mb_ref, xh_ref, rs_ref, k_ref, v_ref):
        m, xhat, rstd = _ln_fwd(m_ref[...], g_ref[...], b_ref[...])
        mb = m.astype(BF16)
        mb_ref[...] = mb
        xh_ref[...] = xhat
        rs_ref[...] = rstd
        k_ref[...] = _dot(mb, wk_ref[...]).astype(BF16)
        v_ref[...] = _dot(mb, wv_ref[...]).astype(BF16)

    return pl.pallas_call(
        body,
        out_shape=[jax.ShapeDtypeStruct((m_len, d), BF16), jax.ShapeDtypeStruct((m_len, d), F32),
                   jax.ShapeDtypeStruct((m_len, 1), F32), jax.ShapeDtypeStruct((m_len, d), BF16),
                   jax.ShapeDtypeStruct((m_len, d), BF16)],
        compiler_params=pltpu.CompilerParams(vmem_limit_bytes=VMEM_LIMIT_V7X),
        name="mem_kv",
    )(mem, g, b, wk, wv)


def _softmax_rows(s):
    m = jnp.max(s, axis=-1, keepdims=True)
    p = jnp.exp(s - m)
    return p / jnp.sum(p, axis=-1, keepdims=True)


def _attn_fwd(hb, wq, kb, vb):
    t, d = hb.shape
    tm = _row_tile(t)
    dh = d // X_HEADS
    scale = dh ** -0.5

    def body(h_ref, wq_ref, k_ref, v_ref, q_ref, o_ref):
        q = _dot(h_ref[...], wq_ref[...]).astype(BF16)
        q_ref[...] = q
        for hd in range(X_HEADS):
            sl = slice(hd * dh, (hd + 1) * dh)
            p = _softmax_rows(_dot_nt(q[:, sl], k_ref[:, sl]) * scale)
            o_ref[:, sl] = _dot(p.astype(BF16), v_ref[:, sl]).astype(BF16)

    row = pl.BlockSpec((tm, d), lambda i: (i, 0))
    full = lambda a: pl.BlockSpec(a.shape, lambda i: (0, 0))
    return pl.pallas_call(
        body,
        grid=(t // tm,),
        in_specs=[row, full(wq), full(kb), full(vb)],
        out_specs=[row, row],
        out_shape=[jax.ShapeDtypeStruct((t, d), BF16), jax.ShapeDtypeStruct((t, d), BF16)],
        compiler_params=_params(1),
        name="attn_fwd",
    )(hb, wq, kb, vb)


def _ffn_bwd_act(dyb, wd, a, b, coef, name, deps=()):
    t, d = dyb.shape
    f = wd.shape[0]
    tm = _row_tile(t)
    tn = _col_tile(f)

    def body(dy_ref, wd_hbm, a_ref, b_ref, da_ref, db_ref, wd_ref, sems):
        def compute(wait):
            dy = dy_ref[...]
            for c in range(f // tn):
                cols = slice(c * tn, (c + 1) * tn)
                wait(c)
                ds = _dot_nt(dy, wd_ref[cols, :]) * coef
                silu, dsilu = _silu_and_grad(a_ref[:, cols].astype(F32))
                da_ref[:, cols] = (ds * b_ref[:, cols].astype(F32) * dsilu).astype(BF16)
                db_ref[:, cols] = (ds * silu).astype(BF16)

        _with_first_step_loads([_window_loads(wd_hbm, wd_ref, sems, 0, tn, 0)], compute)

    act = pl.BlockSpec((tm, f), lambda i: (i, 0))
    return pl.pallas_call(
        _drop_deps(body, 4, len(deps)),
        grid=(t // tm,),
        in_specs=[pl.BlockSpec((tm, d), lambda i: (i, 0)), ANY, act, act] + [ANY] * len(deps),
        out_specs=[act, act],
        out_shape=[jax.ShapeDtypeStruct((t, f), BF16), jax.ShapeDtypeStruct((t, f), BF16)],
        scratch_shapes=[pltpu.VMEM(wd.shape, BF16), pltpu.SemaphoreType.DMA((f // tn,))],
        compiler_params=_params(1),
        name=name,
    )(dyb, wd, a, b, *deps)


def _ffn_bwd_fused(dy, dyb, wd, wg, wu, a, b, coef, ln, name):
    t, d = dy.shape
    f = wd.shape[0]
    tm = min(t, 256)
    tn = _col_tile(f)
    nc = f // tn

    def body(dy_ref, dyb_ref, wd_hbm, wg_hbm, wu_hbm, a_ref, b_ref, xh_ref, rs_ref, g_ref,
             da_ref, db_ref, dyo_ref, dyob_ref, dg_ref, dbl_ref, wd_ref, wg_ref, wu_ref, sems):
        @pl.when(pl.program_id(0) == 0)
        def _():
            dg_ref[...] = jnp.zeros_like(dg_ref)
            dbl_ref[...] = jnp.zeros_like(dbl_ref)

        def compute(wait):
            dyb_v = dyb_ref[...]
            dh = ALPHA * dy_ref[...]
            for c in range(nc):
                cols = slice(c * tn, (c + 1) * tn)
                wait(c)
                ds = _dot_nt(dyb_v, wd_ref[cols, :]) * coef
                silu, dsilu = _silu_and_grad(a_ref[:, cols].astype(F32))
                da = (ds * b_ref[:, cols].astype(F32) * dsilu).astype(BF16)
                db = (ds * silu).astype(BF16)
                da_ref[:, cols] = da
                db_ref[:, cols] = db
                dh = dh + _dot_nt(da, wg_ref[:, cols]) + _dot_nt(db, wu_ref[:, cols])
            dyp, dg, dbl = _ln_bwd(dh, xh_ref[...], rs_ref[...], g_ref[...])
            dyo_ref[...] = dyp
            dyob_ref[...] = dyp.astype(BF16)
            dg_ref[...] += dg
            dbl_ref[...] += dbl

        _with_first_step_loads([_window_loads(wd_hbm, wd_ref, sems, 0, tn, 0),
                                _window_loads(wg_hbm, wg_ref, sems, nc, tn, 1),
                                _window_loads(wu_hbm, wu_ref, sems, 2 * nc, tn, 1)], compute)

    row = pl.BlockSpec((tm, d), lambda i: (i, 0))
    act = pl.BlockSpec((tm, f), lambda i: (i, 0))
    vec = pl.BlockSpec((1, d), lambda i: (0, 0))
    return pl.pallas_call(
        body,
        grid=(t // tm,),
        in_specs=[row, row, ANY, ANY, ANY, act, act, row, pl.BlockSpec((tm, 1), lambda i: (i, 0)), vec],
        out_specs=[act, act, row, row, vec, vec],
        out_shape=[jax.ShapeDtypeStruct((t, f), BF16), jax.ShapeDtypeStruct((t, f), BF16),
                   jax.ShapeDtypeStruct((t, d), F32), jax.ShapeDtypeStruct((t, d), BF16),
                   jax.ShapeDtypeStruct((1, d), F32), jax.ShapeDtypeStruct((1, d), F32)],
        scratch_shapes=[pltpu.VMEM(wd.shape, BF16), pltpu.VMEM(wg.shape, BF16), pltpu.VMEM(wu.shape, BF16),
                        pltpu.SemaphoreType.DMA((3 * nc,))],
        compiler_params=_params(1),
        name=name,
    )(dy, dyb, wd, wg, wu, a, b, *ln)


def _mm_tn(a, b, name, scale=1.0, deps=()):
    t, m = a.shape
    n = b.shape[1]
    tt = _row_tile(t)
    nt = t // tt
    tm_o, tn_o = m, n

    def body(a_ref, b_ref, o_ref, acc):
        k = pl.program_id(2)

        @pl.when(k == 0)
        def _():
            acc[...] = jnp.zeros_like(acc)

        acc[...] += _dot_tn(a_ref[...], b_ref[...])

        @pl.when(k == nt - 1)
        def _():
            o_ref[...] = (acc[...] * scale).astype(BF16)

    return pl.pallas_call(
        _drop_deps(body, 2, len(deps)),
        grid=(m // tm_o, n // tn_o, nt),
        in_specs=[pl.BlockSpec((tt, tm_o), lambda i, j, k: (k, i)), pl.BlockSpec((tt, tn_o), lambda i, j, k: (k, j))]
        + [ANY] * len(deps),
        out_specs=pl.BlockSpec((tm_o, tn_o), lambda i, j, k: (i, j)),
        out_shape=jax.ShapeDtypeStruct((m, n), BF16),
        scratch_shapes=[pltpu.VMEM((tm_o, tn_o), F32)],
        compiler_params=_params(3),
        name=name,
    )(a, b, *deps)


def _mm_nt(lhs, w, name):
    t, d = lhs.shape
    kd = w.shape[0]
    tm = _row_tile(t)

    def body(l_ref, w_ref, o_ref):
        _store_slabs(o_ref, 0, _dot_nt(l_ref[...], w_ref[...]))

    return pl.pallas_call(
        body,
        grid=(t // tm,),
        in_specs=[pl.BlockSpec((tm, d), lambda i: (i, 0)), _resident(w)],
        out_specs=pl.BlockSpec((kd // LANES, tm, LANES), lambda i: (0, i, 0)),
        out_shape=jax.ShapeDtypeStruct((kd // LANES, t, LANES), F32),
        compiler_params=_params(1),
        name=name,
    )(lhs, w)


def _dx_ln(dy, pairs, ln, name, deps=()):
    t, d = dy.shape
    npair = len(pairs)
    tm = min(t, 512 // npair)
    nt = t // tm
    n_in = 1 + 2 * npair + (3 if ln is not None else 0)

    def body(*refs):
        dy_ref = refs[0]
        pr = refs[1:1 + 2 * npair]
        pos = 1 + 2 * npair
        dh = ALPHA * dy_ref[...]
        for p in range(npair):
            dh = dh + _dot_nt(pr[2 * p][...], pr[2 * p + 1][...])
        if ln is not None:
            xh_ref, rs_ref, g_ref = refs[pos:pos + 3]
            dyo_ref, dyb_ref, dg_ref, db_ref = refs[pos + 3:pos + 7]

            @pl.when(pl.program_id(0) == 0)
            def _():
                dg_ref[...] = jnp.zeros_like(dg_ref)
                db_ref[...] = jnp.zeros_like(db_ref)

            dyp, dg, db = _ln_bwd(dh, xh_ref[...], rs_ref[...], g_ref[...])
            dyo_ref[...] = dyp
            dyb_ref[...] = dyp.astype(BF16)
            dg_ref[...] += dg
            db_ref[...] += db
        else:
            refs[pos][...] = dh

    row = pl.BlockSpec((tm, d), lambda i: (i, 0))
    vec = pl.BlockSpec((1, d), lambda i: (0, 0))
    in_specs = [row]
    args = [dy]
    for lhs, w in pairs:
        in_specs += [pl.BlockSpec((tm, lhs.shape[1]), lambda i: (i, 0)), _resident(w)]
        args += [lhs, w]
    if ln is not None:
        in_specs += [row, pl.BlockSpec((tm, 1), lambda i: (i, 0)), vec]
        args += list(ln)
        out_specs = [row, row, vec, vec]
        out_shape = [jax.ShapeDtypeStruct((t, d), F32), jax.ShapeDtypeStruct((t, d), BF16),
                     jax.ShapeDtypeStruct((1, d), F32), jax.ShapeDtypeStruct((1, d), F32)]
    else:
        out_specs = row
        out_shape = jax.ShapeDtypeStruct((t, d), F32)
    return pl.pallas_call(
        _drop_deps(body, n_in, len(deps)),
        grid=(nt,),
        in_specs=in_specs + [ANY] * len(deps),
        out_specs=out_specs,
        out_shape=out_shape,
        compiler_params=_params(1),
        name=name,
    )(*args, *deps)


def _hgrn_bwd(proj, oraw, dmix, states, logits, gn):
    t = proj.shape[1]
    ct = _hg_tile(t)
    nct = t // ct
    nblk = ct // HG_BLOCK
    nh = HG_HEADS
    mrows = min(ct, 256)

    def body(q_ref, fz_ref, iv_ref, gg_ref, or_ref, do_ref, st_ref, lg_ref, gn_ref, mask_ref,
             dq_ref, dfz_ref, div_ref, dgg_ref, dlg_ref, dgn_ref,
             dstate, qt_s, kt_s, k_s, b_s, eb_s, ekb_s, dec_s, dor_s, dbl_s, gr_s, dk_s, dlb_acc):
        c = pl.program_id(1)

        @pl.when(c == 0)
        def _():
            dstate[...] = jnp.zeros_like(dstate)
            dlb_acc[...] = jnp.zeros_like(dlb_acc)
            dgn_ref[...] = jnp.zeros_like(dgn_ref)

        lb = _lower_bound(lg_ref[...])
        q = q_ref[...]
        sig, nsig, f, k = _forget_terms(fz_ref[...], lb)
        logf = jnp.log(f)
        b = _mask_dot(mask_ref[0], logf)
        bend = _mask_dot(mask_ref[2], logf)
        eb = jnp.exp(b)
        ekb = jnp.exp(bend - b)
        qt_s[...] = (q * eb).astype(BF16)
        kt_s[...] = (k * ekb).astype(BF16)
        k_s[...] = k
        b_s[...] = b
        eb_s[...] = eb
        ekb_s[...] = ekb
        dec_s[...] = jnp.exp(bend)
        oraw = or_ref[...]
        r = lax.rsqrt(jnp.mean(oraw * oraw, axis=-1, keepdims=True) + LN_EPS)
        on = oraw * r
        gg = gg_ref[...]
        silu, dsilu = _silu_and_grad(gg)
        doa = do_ref[...]
        gnv = gn_ref[...]
        dgg_ref[...] = (doa * on * gnv * dsilu).astype(BF16)
        dyn = doa * silu
        dgn_ref[...] += jnp.sum(dyn * on, axis=0, keepdims=True)
        don = dyn * gnv
        dor_s[...] = r * (don - on * jnp.mean(don * on, axis=-1, keepdims=True))
        tidx = lax.broadcasted_iota(jnp.int32, (HG_HALF, HG_DIM), 0)

        def blk(ii, carry):
            i = nblk - 1 - ii
            r0 = pl.multiple_of(i * HG_BLOCK, HG_BLOCK)
            rows = pl.ds(r0, HG_BLOCK)
            st = st_ref[i]
            dst = dstate[...]
            dstb = dst.astype(BF16)
            do = dor_s[rows, :]
            dob = do.astype(BF16)
            v = iv_ref[rows, :]
            vb = v.astype(BF16)
            qq = q_ref[rows, :]
            kk = k_s[rows, :]
            bb = b_s[rows, :]
            qt = qt_s[rows, :]
            kt = kt_s[rows, :]
            dec = dec_s[pl.ds(r0, 1), :]
            dkt = _dot(vb, dstb)
            dq = _dot(dob, st) * eb_s[rows, :]
            dk = dkt * ekb_s[rows, :]
            dv = _dot_nt(kt, dstb)
            gend = (jnp.sum(kk * dk, axis=0, keepdims=True)
                    + dec * jnp.sum(dst * st.astype(F32), axis=0, keepdims=True))
            qh, bh, doh = _halves(qq), _halves(bb), _halves(do)
            dqh, dkh, dvh = list(_halves(dq)), list(_halves(dk)), list(_halves(dv))
            for s in range(HG_BLOCK):
                ks, vs = kk[s:s + 1, :], v[s:s + 1, :]
                dk_part = dv_part = None
                for h in _causal_halves(s):
                    e = _decay_from(bh[h], bb[s:s + 1, :], s, h, tidx)
                    ke = ks * e
                    acol = jnp.sum(qh[h] * ke, axis=1, keepdims=True)
                    dacol = jnp.sum(doh[h] * vs, axis=1, keepdims=True)
                    dqh[h] = dqh[h] + dacol * ke
                    pk = dacol * (qh[h] * e)
                    pv = acol * doh[h]
                    dk_part = pk if dk_part is None else dk_part + pk
                    dv_part = pv if dv_part is None else dv_part + pv
                hs, row = divmod(s, HG_HALF)
                dkh[hs] = dkh[hs] + jnp.where(tidx == row, jnp.sum(dk_part, axis=0, keepdims=True), 0.0)
                dvh[hs] = dvh[hs] + jnp.where(tidx == row, jnp.sum(dv_part, axis=0, keepdims=True), 0.0)
            dq = jnp.concatenate(dqh, axis=0)
            dk = jnp.concatenate(dkh, axis=0)
            dv = jnp.concatenate(dvh, axis=0)
            dq_ref[rows, :] = dq.astype(BF16)
            div_ref[rows, :] = dv.astype(BF16)
            dk_s[rows, :] = dk
            dbl_s[rows, :] = qq * dq - kk * dk
            gr_s[rows, :] = jnp.zeros((HG_BLOCK, HG_DIM), F32) + gend
            dstate[...] = dst * dec + _dot_tn(dob, qt)
            return carry

        lax.fori_loop(0, nblk, blk, 0, unroll=HG_UNROLL)
        dlogf = _mask_dot(mask_ref[1], dbl_s[...]) + gr_s[...]
        dk = dk_s[...]
        dfz_ref[...] = ((dlogf / f - dk) * ((1.0 - lb) * sig * nsig)).astype(BF16)
        dlb_acc[...] += jnp.sum((dlogf / f - dk) * nsig, axis=0, keepdims=True)

        @pl.when(c == nct - 1)
        def _():
            dl0 = dlb_acc[...] * lb * (1.0 - lb)
            layer = lax.broadcasted_iota(jnp.int32, (2, HG_DIM), 0)
            dlg_ref[...] = jnp.where(layer == 0, dl0, -dl0)

    def slab(off):
        return pl.BlockSpec((None, ct, HG_DIM), lambda h, c: (off + h, nct - 1 - c, 0))

    out_slab = pl.BlockSpec((ct, HG_DIM), lambda h, c: (nct - 1 - c, h))
    tile_f32 = pltpu.VMEM((ct, HG_DIM), F32)
    tile_b16 = pltpu.VMEM((ct, HG_DIM), BF16)
    slab_shape = jax.ShapeDtypeStruct((t, nh * HG_DIM), BF16)
    return pl.pallas_call(
        body,
        grid=(nh, nct),
        in_specs=[slab(0), slab(nh), slab(2 * nh), slab(3 * nh), slab(0), slab(0),
                  pl.BlockSpec((None, nblk, HG_DIM, HG_DIM), lambda h, c: (h, nct - 1 - c, 0, 0)),
                  pl.BlockSpec((None, 2, HG_DIM), lambda h, c: (h, 0, 0)),
                  pl.BlockSpec((1, HG_DIM), lambda h, c: (0, 0)),
                  pl.BlockSpec((3, mrows, mrows), lambda h, c: (0, 0, 0))],
        out_specs=[out_slab, out_slab, out_slab, out_slab,
                   pl.BlockSpec((None, 2, HG_DIM), lambda h, c: (h, 0, 0)),
                   pl.BlockSpec((None, 1, HG_DIM), lambda h, c: (h, 0, 0))],
        out_shape=[slab_shape, slab_shape, slab_shape, slab_shape,
                   jax.ShapeDtypeStruct((nh, 2, HG_DIM), F32), jax.ShapeDtypeStruct((nh, 1, HG_DIM), F32)],
        scratch_shapes=[pltpu.VMEM((HG_DIM, HG_DIM), F32), tile_b16, tile_b16, tile_f32, tile_f32, tile_f32, tile_f32,
                        tile_f32, tile_f32, tile_f32, tile_f32, tile_f32, pltpu.VMEM((1, HG_DIM), F32)],
        compiler_params=_params(2),
        name="hgrn_bwd",
    )(proj, proj, proj, proj, oraw, dmix, states, logits, gn, _block_masks(mrows))


def _sgu_bwd(proj, dmix, ln_g, ln_b, w_s, w_t, b_col):
    t = proj.shape[1]
    ct = _sg_tile(t)
    nct = t // ct
    ng = SG_GROUPS
    off_u = 4 * HG_HEADS
    off_v = off_u + ng
    n = SG_CHUNK

    def body(u_ref, v_ref, do_ref, g_ref, b_ref, w_ref, wt_ref, bs_ref, du_ref, dv_ref, dg_ref, db_ref, dw_ref, dbs_ref):
        c = pl.program_id(1)

        @pl.when(c == 0)
        def _():
            dg_ref[...] = jnp.zeros_like(dg_ref)
            db_ref[...] = jnp.zeros_like(db_ref)
            dw_ref[...] = jnp.zeros_like(dw_ref)
            dbs_ref[...] = jnp.zeros_like(dbs_ref)

        r = lax.broadcasted_iota(jnp.int32, (n, n), 0)
        cc = lax.broadcasted_iota(jnp.int32, (n, n), 1)
        wm = jnp.where(cc <= r, w_ref[...], 0.0).astype(BF16)
        wmt = jnp.where(r <= cc, wt_ref[...], 0.0).astype(BF16)
        for ci in range(ct // n):
            rows = slice(ci * n, (ci + 1) * n)
            ua, dua, dva, vn, xhat, rstd, s = _sgu_chunk_fwd(u_ref[rows, :], v_ref[rows, :], g_ref[...], b_ref[...],
                                                             wm, bs_ref[...])
            do = do_ref[rows, :]
            du_ref[rows, :] = (do * s * dua).astype(BF16)
            ds = do * ua
            dsb = ds.astype(BF16)
            dbs_ref[...] += jnp.sum(ds, axis=1, keepdims=True)
            dw_ref[...] += _dot_nt(dsb, vn.astype(BF16))
            dvn = _dot(wmt, dsb)
            dva_in, dg, db = _ln_bwd(dvn, xhat, rstd, g_ref[...])
            dg_ref[...] += dg
            db_ref[...] += db
            dv_ref[rows, :] = (dva_in * dva).astype(BF16)

        @pl.when(c == nct - 1)
        def _():
            dw_ref[...] = jnp.where(cc <= r, dw_ref[...], 0.0)

    vec = pl.BlockSpec((None, 1, SG_DIM), lambda g, c: (g, 0, 0))
    mat = pl.BlockSpec((None, n, n), lambda g, c: (g, 0, 0))
    col = pl.BlockSpec((None, n, 1), lambda g, c: (g, 0, 0))
    out_slab = pl.BlockSpec((ct, SG_DIM), lambda g, c: (c, g))
    return pl.pallas_call(
        body,
        grid=(ng, nct),
        in_specs=[pl.BlockSpec((None, ct, SG_DIM), lambda g, c: (off_u + g, c, 0)),
                  pl.BlockSpec((None, ct, SG_DIM), lambda g, c: (off_v + g, c, 0)),
                  pl.BlockSpec((None, ct, SG_DIM), lambda g, c: (ng + g, c, 0)), vec, vec, mat, mat, col],
        out_specs=[out_slab, out_slab, vec, vec, mat, col],
        out_shape=[jax.ShapeDtypeStruct((t, ng * SG_DIM), BF16), jax.ShapeDtypeStruct((t, ng * SG_DIM), BF16),
                   jax.ShapeDtypeStruct((ng, 1, SG_DIM), F32), jax.ShapeDtypeStruct((ng, 1, SG_DIM), F32),
                   jax.ShapeDtypeStruct((ng, n, n), F32), jax.ShapeDtypeStruct((ng, n, 1), F32)],
        compiler_params=_params(2),
        name="sgu_bwd",
    )(proj, proj, dmix, ln_g, ln_b, w_s, w_t, b_col)


def _attn_bwd(dyb, wo, qb, kb, vb):
    t, d = dyb.shape
    m_len = kb.shape[0]
    tm = _row_tile(t)
    dh = d // X_HEADS
    scale = dh ** -0.5

    def body(dy_ref, wo_ref, q_ref, k_ref, v_ref, dq_ref, dk_ref, dv_ref):
        i = pl.program_id(0)

        @pl.when(i == 0)
        def _():
            dk_ref[...] = jnp.zeros_like(dk_ref)
            dv_ref[...] = jnp.zeros_like(dv_ref)

        do = _dot_nt(dy_ref[...], wo_ref[...]).astype(BF16)
        for hd in range(X_HEADS):
            sl = slice(hd * dh, (hd + 1) * dh)
            qh = q_ref[:, sl]
            p = _softmax_rows(_dot_nt(qh, k_ref[:, sl]) * scale)
            doh = do[:, sl]
            dp = _dot_nt(doh, v_ref[:, sl])
            ds = (p * (dp - jnp.sum(dp * p, axis=-1, keepdims=True)) * scale).astype(BF16)
            dq_ref[:, sl] = _dot(ds, k_ref[:, sl]).astype(BF16)
            dk_ref[:, sl] += _dot_tn(ds, qh)
            dv_ref[:, sl] += _dot_tn(p.astype(BF16), doh)

    row = pl.BlockSpec((tm, d), lambda i: (i, 0))
    full = lambda a: pl.BlockSpec(a.shape, lambda i: (0, 0))
    kv = pl.BlockSpec((m_len, d), lambda i: (0, 0))
    return pl.pallas_call(
        body,
        grid=(t // tm,),
        in_specs=[row, full(wo), row, full(kb), full(vb)],
        out_specs=[row, kv, kv],
        out_shape=[jax.ShapeDtypeStruct((t, d), BF16), jax.ShapeDtypeStruct((m_len, d), F32),
                   jax.ShapeDtypeStruct((m_len, d), F32)],
        compiler_params=_params(1),
        name="attn_bwd",
    )(dyb, wo, qb, kb, vb)


def _mem_bwd(dk, dv, mb, xhat, rstd, g, wk, wv):
    m_len, d = dk.shape

    def body(dk_ref, dv_ref, mb_ref, xh_ref, rs_ref, g_ref, wk_ref, wv_ref, gwk_ref, gwv_ref, dg_ref, db_ref):
        dkb = dk_ref[...].astype(BF16)
        dvb = dv_ref[...].astype(BF16)
        mb_v = mb_ref[...]
        gwk_ref[...] = _dot_tn(mb_v, dkb).astype(BF16)
        gwv_ref[...] = _dot_tn(mb_v, dvb).astype(BF16)
        dm = _dot_nt(dkb, wk_ref[...]) + _dot_nt(dvb, wv_ref[...])
        _, dg, db = _ln_bwd(dm, xh_ref[...], rs_ref[...], g_ref[...])
        dg_ref[...] = dg
        db_ref[...] = db

    return pl.pallas_call(
        body,
        out_shape=[jax.ShapeDtypeStruct((d, d), BF16), jax.ShapeDtypeStruct((d, d), BF16),
                   jax.ShapeDtypeStruct((1, d), F32), jax.ShapeDtypeStruct((1, d), F32)],
        compiler_params=pltpu.CompilerParams(vmem_limit_bytes=VMEM_LIMIT_V7X),
        name="mem_bwd",
    )(dk, dv, mb, xhat, rstd, g, wk, wv)


def _adamw(w, g, m, v):
    m = ADAM_B1 * m + (1.0 - ADAM_B1) * g
    v = ADAM_B2 * v + (1.0 - ADAM_B2) * (g * g)
    m_hat = m / (1.0 - ADAM_B1 ** ADAM_STEP)
    v_hat = v / (1.0 - ADAM_B2 ** ADAM_STEP)
    delta = -ADAM_LR * (m_hat / (jnp.sqrt(v_hat) + ADAM_EPS) + ADAM_WD * w)
    return delta, m, v


def _slot_sum(ref):
    g = ref[0].astype(F32)
    for s in range(1, N_DEV):
        g = g + ref[s].astype(F32)
    return g


def _adam_sharded(lands, w, m, v, axis, name):
    rows, cols = w.shape
    nl = len(lands)
    transposed = axis == 1 and nl == 2
    if transposed:
        rows, cols = cols, rows
        tr = 256
        grid = (rows // tr,)
        wblk = pl.BlockSpec((cols, tr), lambda i: (0, i))
        lblk = [pl.BlockSpec((N_DEV, tr, a.shape[2]), lambda i: (0, i, 0)) for a in lands]
    elif axis == 1:
        tr = 256 if rows % 256 == 0 else rows
        grid = (rows // tr,)
        wblk = pl.BlockSpec((tr, cols), lambda i: (i, 0))
        lblk = [pl.BlockSpec((N_DEV, tr, a.shape[2]), lambda i: (0, i, 0)) for a in lands]
    else:
        tc = _col_tile(cols)
        grid = (cols // tc,)
        wblk = pl.BlockSpec((rows, tc), lambda i: (0, i))
        lblk = [pl.BlockSpec((N_DEV, a.shape[1], tc), lambda i: (0, 0, i)) for a in lands]

    def body(*refs):
        w_ref, m_ref, v_ref = refs[nl:nl + 3]
        g_ref, d_ref, nm_ref, nv_ref = refs[nl + 3:]
        g = _slot_sum(refs[0])
        if nl == 2:
            tail = _slot_sum(refs[1])
            if transposed:
                g = jnp.concatenate([g.T, tail.T[:cols - g.shape[1], :]], axis=0)
            elif axis == 1:
                g = jnp.concatenate([g, tail[:, :cols - g.shape[1]]], axis=1)
            else:
                g = jnp.concatenate([g, tail[:rows - g.shape[0], :]], axis=0)
        delta, nm, nv = _adamw(w_ref[...], g, m_ref[...], v_ref[...])
        g_ref[...] = g
        d_ref[...] = delta
        nm_ref[...] = nm
        nv_ref[...] = nv

    shp = jax.ShapeDtypeStruct(w.shape, F32)
    return pl.pallas_call(
        body,
        grid=grid,
        in_specs=lblk + [wblk, wblk, wblk],
        out_specs=[wblk, wblk, wblk, wblk],
        out_shape=[shp, shp, shp, shp],
        compiler_params=_params(1),
        name=name,
    )(*[pltpu.with_memory_space_constraint(a, pltpu.HBM) for a in (*lands, w, m, v)])


def _mesh_pos():
    return lax.axis_index("x"), lax.axis_index("y"), lax.axis_index("c")


def _peer(k):
    x, y, c = _mesh_pos()
    pos = (x ^ (k >> 2), y ^ ((k >> 1) & 1), c ^ (k & 1))
    return pos, 4 * pos[0] + 2 * pos[1] + pos[2]


def _sem_index(row, k):
    return row * (N_DEV - 1) + k - 1


def _window(ref, axis, start, size):
    align = 16 if axis == 0 else LANES
    start = pl.multiple_of(start, align)
    return ref.at[pl.ds(start, size), :] if axis == 0 else ref.at[:, pl.ds(start, size)]


def _piece_refs(piece, srcs, lands, me, peer):
    kind, si, li, axis, base, stride, shape = piece
    if kind == "gather":
        return srcs[si], _window(lands[li], axis, base + stride * me, shape[axis])
    return _window(srcs[si], axis, base + stride * peer, shape[axis]), lands[li].at[me]


def _place_own(srcs, land_shapes, pieces, name):
    ns, nl, npc = len(srcs), len(land_shapes), len(pieces)

    def body(*refs):
        s_refs = refs[:ns]
        l_refs = refs[ns:ns + nl]
        bufs = refs[ns + nl:ns + nl + npc]
        sems = refs[ns + nl + npc]
        x, y, c = _mesh_pos()
        me = 4 * x + 2 * y + c
        loads = []
        for p, piece in enumerate(pieces):
            src, dst = _piece_refs(piece, s_refs, l_refs, me, me)
            cp = pltpu.make_async_copy(src, bufs[p], sems.at[0, p])
            cp.start()
            loads.append((cp, dst))
        stores = []
        for p, (cp, dst) in enumerate(loads):
            cp.wait()
            out = pltpu.make_async_copy(bufs[p], dst, sems.at[1, p])
            out.start()
            stores.append(out)
        for out in stores:
            out.wait()

    out = pl.pallas_call(
        body,
        in_specs=[ANY] * ns,
        out_specs=[ANY] * nl,
        out_shape=list(land_shapes),
        scratch_shapes=[pltpu.VMEM(pc[6], srcs[pc[1]].dtype) for pc in pieces] + [pltpu.SemaphoreType.DMA((2, npc))],
        compiler_params=pltpu.CompilerParams(vmem_limit_bytes=VMEM_LIMIT_V7X),
        name=name,
    )(*srcs)
    return list(out)


def _comm_start(srcs, lands, pieces, groups, name, after=()):
    ns, nl, na, ng = len(srcs), len(lands), len(after), len(groups)

    def body(*refs):
        s_refs = refs[:ns]
        l_refs = refs[ns:ns + nl]
        outs = refs[ns + nl + na:]
        sems = outs[:2 * ng]
        token = outs[-1]
        x, y, c = _mesh_pos()
        me = 4 * x + 2 * y + c
        for g, members in enumerate(groups):
            for row, p in enumerate(members):
                for k in range(1, N_DEV):
                    pos, peer = _peer(k)
                    src, dst = _piece_refs(pieces[p], s_refs, l_refs, me, peer)
                    pltpu.make_async_remote_copy(src_ref=src, dst_ref=dst, send_sem=sems[2 * g].at[_sem_index(row, k)],
                                                 recv_sem=sems[2 * g + 1].at[_sem_index(row, k)], device_id=pos,
                                                 device_id_type=MESH_ID).start()
        token[...] = jnp.zeros_like(token)

    sem_shapes = []
    for members in groups:
        sem_shapes += [pltpu.SemaphoreType.DMA((len(members) * (N_DEV - 1),))] * 2
    hbm_of = lambda a: pltpu.HBM(a.shape, a.dtype)
    out = pl.pallas_call(
        body,
        in_specs=[HBM] * (ns + nl) + [ANY] * na,
        out_specs=[SEM] * (2 * ng) + [HBM] * (ns + nl) + [pl.BlockSpec(memory_space=pltpu.VMEM)],
        out_shape=sem_shapes + [hbm_of(a) for a in srcs] + [hbm_of(a) for a in lands]
        + [jax.ShapeDtypeStruct((8, LANES), F32)],
        input_output_aliases={i: 2 * ng + i for i in range(ns + nl)},
        compiler_params=pltpu.CompilerParams(has_side_effects=DATAFLOW),
        name=name,
    )(*[pltpu.with_memory_space_constraint(a, pltpu.HBM) for a in list(srcs) + list(lands)], *after)
    sems = [(out[2 * g], out[2 * g + 1]) for g in range(ng)]
    return sems, list(out[2 * ng:2 * ng + ns]), list(out[2 * ng + ns:2 * ng + ns + nl]), out[-1]


def _comm_wait(srcs, lands, pieces, members, sems, after, name):
    ns, nl, na = len(srcs), len(lands), len(after)

    def body(*refs):
        s_refs = refs[:ns]
        l_refs = refs[ns:ns + nl]
        send_sems, recv_sems = refs[ns + nl:ns + nl + 2]
        x, y, c = _mesh_pos()
        me = 4 * x + 2 * y + c
        for row, p in enumerate(members):
            for k in range(1, N_DEV):
                pos, peer = _peer(k)
                src, dst = _piece_refs(pieces[p], s_refs, l_refs, me, peer)
                cp = pltpu.make_async_remote_copy(src_ref=src, dst_ref=dst, send_sem=send_sems.at[_sem_index(row, k)],
                                                  recv_sem=recv_sems.at[_sem_index(row, k)], device_id=pos,
                                                  device_id_type=MESH_ID)
                cp.wait_send()
                cp.wait_recv()

    hbm_of = lambda a: pltpu.HBM(a.shape, a.dtype)
    out = pl.pallas_call(
        body,
        in_specs=[HBM] * (ns + nl) + [SEM, SEM] + [ANY] * na,
        out_specs=[HBM] * (ns + nl),
        out_shape=[hbm_of(a) for a in srcs] + [hbm_of(a) for a in lands],
        input_output_aliases={i: i for i in range(ns + nl)},
        compiler_params=pltpu.CompilerParams(has_side_effects=DATAFLOW),
        name=name,
    )(*srcs, *lands, sems[0], sems[1], *after)
    return list(out[ns:])


def _landed_block(piece, lands, owner):
    _, _, li, axis, base, stride, shape = piece
    return _window(lands[li], axis, base + stride * owner, shape[axis])


def _copy_stage(name, bufs, in_sems, out_sem_sizes, emit, after=()):
    nb, ni, no, na = len(bufs), len(in_sems), len(out_sem_sizes), len(after)

    def body(*refs):
        b_refs = refs[:nb]
        i_refs = refs[nb:nb + ni]
        o_refs = refs[nb + ni + na:nb + ni + na + no]
        emit(b_refs, i_refs, o_refs)
        refs[-1][...] = jnp.zeros_like(refs[-1])

    hbm_of = lambda a: pltpu.HBM(a.shape, a.dtype)
    out = pl.pallas_call(
        body,
        in_specs=[HBM] * nb + [SEM] * ni + [ANY] * na,
        out_specs=[SEM] * no + [HBM] * nb + [pl.BlockSpec(memory_space=pltpu.VMEM)],
        out_shape=[pltpu.SemaphoreType.DMA((n,)) for n in out_sem_sizes] + [hbm_of(a) for a in bufs]
        + [jax.ShapeDtypeStruct((8, LANES), F32)],
        input_output_aliases={i: no + i for i in range(nb)},
        compiler_params=pltpu.CompilerParams(has_side_effects=DATAFLOW),
        name=name,
    )(*[pltpu.with_memory_space_constraint(a, pltpu.HBM) for a in bufs], *in_sems, *after)
    return list(out[:no]), list(out[no:no + nb]), out[-1]


def _remote(src, dst, send, recv, to):
    return pltpu.make_async_remote_copy(src_ref=src, dst_ref=dst, send_sem=send, recv_sem=recv, device_id=to,
                                        device_id_type=MESH_ID)


def _routed_gather(srcs, lands, pieces, after, name):
    ns, npc = len(srcs), len(pieces)

    def places():
        x, y, c = _mesh_pos()
        index = lambda p: 4 * p[0] + 2 * p[1] + p[2]
        me, sib = (x, y, c), (x, y, 1 - c)
        xnb, ynb = (1 - x, y, c), (x, 1 - y, c)
        got_first = (x ^ (1 - c), y ^ c, c)
        pass_to = (x ^ c, y ^ (1 - c), c)
        diag = (1 - x, 1 - y, c)
        return index, me, sib, xnb, ynb, got_first, pass_to, diag

    def start(b, _, o):
        index, me, sib, xnb, ynb, *_rest = places()
        send_a, recv_sib, recv_nb = o
        for p, piece in enumerate(pieces):
            src, dst = _piece_refs(piece, b[:ns], b[ns:], index(me), 0)
            _remote(src, dst, send_a.at[3 * p], recv_sib.at[p], sib).start()
            _remote(src, dst, send_a.at[3 * p + 1], recv_nb.at[2 * p], xnb).start()
            _remote(src, dst, send_a.at[3 * p + 2], recv_nb.at[2 * p + 1], ynb).start()

    def pass_a(b, i, o):
        index, me, sib, xnb, ynb, got_first, pass_to, _diag = places()
        (recv_nb,) = i
        send_f, recv_f, send_d, recv_d = o
        for p, piece in enumerate(pieces):
            for j, nb in enumerate((xnb, ynb)):
                blk = _landed_block(piece, b, index(nb))
                _remote(blk, blk, send_f.at[2 * p + j], recv_nb.at[2 * p + j], sib).wait_recv()
                _remote(blk, blk, send_f.at[2 * p + j], recv_f.at[2 * p + j], sib).start()
            blk = _landed_block(piece, b, index(got_first))
            _remote(blk, blk, send_d.at[p], recv_d.at[p], pass_to).start()

    def pass_b(b, i, o):
        index, me, sib, *_mid, diag = places()
        (recv_d,) = i
        send_g, recv_g = o
        for p, piece in enumerate(pieces):
            blk = _landed_block(piece, b, index(diag))
            _remote(blk, blk, send_g.at[p], recv_d.at[p], sib).wait_recv()
            _remote(blk, blk, send_g.at[p], recv_g.at[p], sib).start()

    def last(b, i, _):
        index, me, sib, *_others = places()
        send_a, recv_sib, send_f, recv_f, send_d, send_g, recv_g = i
        for p, piece in enumerate(pieces):
            src, dst = _piece_refs(piece, b[:ns], b[ns:], index(me), 0)
            cp = lambda s_sem, r_sem: _remote(src, dst, s_sem, r_sem, sib)
            cp(send_a.at[3 * p], recv_sib.at[p]).wait_recv()
            cp(send_a.at[3 * p], recv_g.at[p]).wait_recv()
            for j in range(3):
                cp(send_a.at[3 * p + j], recv_sib.at[p]).wait_send()
            for j in range(2):
                cp(send_f.at[2 * p + j], recv_f.at[2 * p + j]).wait_recv()
                cp(send_f.at[2 * p + j], recv_f.at[2 * p + j]).wait_send()
            cp(send_d.at[p], recv_sib.at[p]).wait_send()
            cp(send_g.at[p], recv_sib.at[p]).wait_send()

    (send_a, recv_sib, recv_nb), bufs, _ = _copy_stage(name + "_start", list(srcs) + list(lands), [],
                                                       [3 * npc, npc, 2 * npc], start)
    srcs, lands = bufs[:ns], bufs[ns:]
    (send_f, recv_f, send_d, recv_d), lands, _ = _copy_stage(name + "_pass_a", lands, [recv_nb],
                                                             [2 * npc, 2 * npc, npc, npc],
                                                             lambda b, i, o: pass_a(b, i, o), after=after)
    (send_g, recv_g), lands, tok = _copy_stage(name + "_pass_b", lands, [recv_d], [npc, npc], pass_b)
    _, bufs, _ = _copy_stage(name + "_last", list(srcs) + list(lands),
                             [send_a, recv_sib, send_f, recv_f, send_d, send_g, recv_g], [], last)
    return bufs[ns:], tok


_SMALL_NAMES = ("ln1_g", "ln1_b", "hg_lb_logits", "hg_norm_g", "sg_ln_g", "sg_ln_b", "sg_w_s", "sg_b_s",
                "ln2_g", "ln2_b", "mem_ln_g", "mem_ln_b", "ln3_g", "ln3_b", "ln4_g", "ln4_b")


_VEC_NAMES = ("ln1_g", "ln1_b", "ln2_g", "ln2_b", "mem_ln_g", "mem_ln_b", "ln3_g", "ln3_b", "ln4_g", "ln4_b")
_ROW_NAMES = ("hg_lb_logits", "hg_norm_g", "sg_ln_g", "sg_ln_b", "sg_b_s", "sg_w_s")
VEC_ROWS = 16


def _row_plan(shapes):
    plan, pos = {}, 0
    for k in _ROW_NAMES:
        shp = shapes[k]
        slabs, off = [], pos
        for idx in itertools.product(*[range(dim) for dim in shp[:-2]]):
            slabs.append((idx, off, shp[-2]))
            off += shp[-2]
        plan[k] = (pos, slabs)
        pos = -(-off // 8) * 8
    return plan, -(-pos // 16) * 16


def _pack_small_grads(gs, shapes, loss):
    d = gs[_VEC_NAMES[0]].size
    vec = jnp.concatenate([gs[k].reshape(1, -1) for k in _VEC_NAMES] + [jnp.tile(loss, (1, d // LANES))], axis=0)
    vec = jnp.pad(vec, ((0, VEC_ROWS - vec.shape[0]), (0, 0)))
    plan, total = _row_plan(shapes)
    parts, pos = [], 0
    for k in _ROW_NAMES:
        first, slabs = plan[k]
        rows = gs[k].reshape(-1, LANES)
        end = slabs[-1][1] + slabs[-1][2]
        nxt = -(-end // 8) * 8
        parts.append(jnp.pad(rows, ((0, nxt - first - rows.shape[0]), (0, 0))))
        pos = nxt
    parts.append(jnp.zeros((total - pos, LANES), F32))
    return vec, jnp.concatenate(parts, axis=0)


def _adam_small(land_vec, land_rows, w, m, v):
    names = _VEC_NAMES + _ROW_NAMES
    n = len(names)
    shapes = {k: w[k].shape for k in names}
    plan, _ = _row_plan(shapes)

    def body(*refs):
        lv_ref, lr_ref = refs[:2]
        w_refs, m_refs, v_refs = refs[2:2 + n], refs[2 + n:2 + 2 * n], refs[2 + 2 * n:2 + 3 * n]
        outs = refs[2 + 3 * n:2 + 7 * n]
        loss_ref = refs[2 + 7 * n]
        gv_s, gr_s = refs[3 + 7 * n:]
        gv_s[...] = _slot_sum(lv_ref)
        gr_s[...] = _slot_sum(lr_ref)
        loss_ref[...] = gv_s[len(_VEC_NAMES):len(_VEC_NAMES) + 1, :LANES]
        for p, k in enumerate(names):
            if k in _VEC_NAMES:
                row = _VEC_NAMES.index(k)
                slabs = [((), None, None)]
            else:
                slabs = plan[k][1]
            for idx, off, rows in slabs:
                g = gv_s[row:row + 1, :] if off is None else gr_s[off:off + rows, :]
                sel = idx + (slice(None), slice(None))
                delta, nm, nv = _adamw(w_refs[p][sel], g, m_refs[p][sel], v_refs[p][sel])
                for o, val in zip(range(4), (g, delta, nm, nv)):
                    outs[o * n + p][sel] = val

    flat = lambda tree: [tree[k] for k in names]
    shp = [jax.ShapeDtypeStruct(shapes[k], F32) for k in names]
    out = pl.pallas_call(
        body,
        out_shape=shp * 4 + [jax.ShapeDtypeStruct((1, LANES), F32)],
        scratch_shapes=[pltpu.VMEM(land_vec.shape[1:], F32), pltpu.VMEM(land_rows.shape[1:], F32)],
        name="adam_small",
    )(land_vec, land_rows, *flat(w), *flat(m), *flat(v))
    return [dict(zip(names, out[o * n:(o + 1) * n])) for o in range(4)], out[4 * n]


_COL_FFN = ("ffn1_w_gate", "ffn1_w_up", "ffn2_w_gate", "ffn2_w_up")
_ROW_FFN = ("ffn1_w_down", "ffn2_w_down")
_ROW_SQ = ("w_out", "xa_w_q", "xa_w_k", "xa_w_v", "xa_w_o")
_BIG_NAMES = ("ffn1_w_gate", "ffn1_w_up", "ffn1_w_down", "w_in", "w_out", "xa_w_q", "xa_w_k", "xa_w_v", "xa_w_o",
              "ffn2_w_gate", "ffn2_w_up", "ffn2_w_down")


def _ffn_split(fs):
    main = (fs // MXU_WIDTH_V7X) * MXU_WIDTH_V7X
    tail = fs - main
    tail_pad = -(-tail // LANES) * LANES
    assert main > 0 and tail > 0
    return main, tail, tail_pad


def _layout(name, shard_shape):
    r, c = shard_shape
    if name in _COL_FFN:
        main, tail, pad = _ffn_split(c)
        return (r, N_DEV * (main + pad)), [(1, 0, main, (r, main), (0, main)),
                                           (1, N_DEV * main, pad, (r, pad), (main, c))]
    if name in _ROW_FFN:
        main, tail, pad = _ffn_split(r)
        return (N_DEV * (main + pad), c), [(0, 0, main, (main, c), (0, main)),
                                           (0, N_DEV * main, pad, (pad, c), (main, r))]
    if name == "w_in":
        return (r, N_DEV * c), [(1, 0, c, (r, c), (0, c))]
    return (N_DEV * r, c), [(0, 0, r, (r, c), (0, r))]


def _shard_pieces(name, shard):
    out = []
    for axis, _, _, shape, (lo, hi) in _layout(name, shard.shape)[1]:
        part = shard[lo:hi, :] if axis == 0 else shard[:, lo:hi]
        pad = [(0, shape[0] - part.shape[0]), (0, shape[1] - part.shape[1])]
        out.append(jnp.pad(part, pad).astype(BF16))
    return out


def _gather_plan(names, shards):
    srcs, land_shapes, pieces, index = [], [], [], {}
    for li, name in enumerate(names):
        shape2d, parts = _layout(name, shards[name].shape)
        land_shapes.append(jax.ShapeDtypeStruct(shape2d, BF16))
        index[name] = []
        for (axis, base, stride, shape, _), src in zip(parts, _shard_pieces(name, shards[name])):
            index[name].append(len(pieces))
            pieces.append(("gather", len(srcs), li, axis, base, stride, shape))
            srcs.append(src)
    return srcs, land_shapes, pieces, index


def _scatter_plan(names, grads, shard_shapes):
    srcs, land_shapes, pieces, index = [], [], [], {}
    for si, name in enumerate(names):
        _, parts = _layout(name, shard_shapes[name])
        srcs.append(grads[name])
        index[name] = []
        for axis, base, stride, shape, _ in parts:
            index[name].append(len(land_shapes))
            pieces.append(("scatter", si, len(land_shapes), axis, base, stride, shape))
            land_shapes.append(jax.ShapeDtypeStruct((N_DEV,) + shape, grads[name].dtype))
    return srcs, land_shapes, pieces, index


def _small_views(small):
    row = lambda a: a.reshape(1, -1)
    ln = {k: row(small[k]) for k in ("ln1_g", "ln1_b", "ln2_g", "ln2_b", "ln3_g", "ln3_b", "ln4_g", "ln4_b",
                                      "mem_ln_g", "mem_ln_b", "hg_norm_g")}
    sg_w = small["sg_w_s"].reshape(SG_GROUPS, SG_CHUNK, SG_CHUNK)
    sg = dict(logits=jnp.swapaxes(small["hg_lb_logits"], 0, 1),
              g=small["sg_ln_g"].reshape(SG_GROUPS, 1, SG_DIM), b=small["sg_ln_b"].reshape(SG_GROUPS, 1, SG_DIM),
              w=sg_w, wt=jnp.swapaxes(sg_w, 1, 2), bs=small["sg_b_s"].reshape(SG_GROUPS, SG_CHUNK, 1))
    return ln, sg


def _forward(x, mem, target, get_w, small, first_deps=()):
    ln, sg = _small_views(small)
    xb = x.astype(BF16)
    a1, b1, s1 = _ffn_up(xb, get_w("ffn1_w_gate", ()), get_w("ffn1_w_up", ()), "ffn1_up", deps=first_deps)
    h1b, xh1, rs1 = _mm_res_ln(s1, get_w("ffn1_w_down", (s1,)), x, ln["ln1_g"], ln["ln1_b"], 0.5, "ffn1_down_ln")
    proj = _mm_nn(h1b, get_w("w_in", (h1b,)), "mix_in")
    oraw, mix, states = _hgrn_fwd(proj, sg["logits"], ln["hg_norm_g"])
    mix = _sgu_fwd(proj, mix, sg["g"], sg["b"], sg["w"], sg["bs"])
    h2b, xh2, rs2 = _mm_res_ln(mix, get_w("w_out", (mix,)), (xh1, ln["ln1_g"], ln["ln1_b"]), ln["ln2_g"], ln["ln2_b"],
                               1.0, "mix_out_ln")
    mb, mxh, mrs, kb, vb = _mem_kv(mem, ln["mem_ln_g"], ln["mem_ln_b"], get_w("xa_w_k", (h2b,)), get_w("xa_w_v", (h2b,)))
    qb, att = _attn_fwd(h2b, get_w("xa_w_q", (kb,)), kb, vb)
    h3b, xh3, rs3 = _mm_res_ln(att, get_w("xa_w_o", (att,)), (xh2, ln["ln2_g"], ln["ln2_b"]), ln["ln3_g"], ln["ln3_b"],
                               1.0, "attn_out_ln")
    a2, b2, s2 = _ffn_up(h3b, get_w("ffn2_w_gate", (h3b,)), get_w("ffn2_w_up", (h3b,)), "ffn2_up")
    loss, dy4, dy4b, dg4, db4 = _mm_res_ln(s2, get_w("ffn2_w_down", (s2,)), (xh3, ln["ln3_g"], ln["ln3_b"]),
                                           ln["ln4_g"], ln["ln4_b"], 0.5, "ffn2_down_ln_loss", target=target)
    return dict(xb=xb, a1=a1, b1=b1, s1=s1, h1b=h1b, xh1=xh1, rs1=rs1, proj=proj, oraw=oraw, mix=mix, states=states,
                h2b=h2b, xh2=xh2, rs2=rs2, mb=mb, mxh=mxh, mrs=mrs, kb=kb, vb=vb, qb=qb, att=att, h3b=h3b, xh3=xh3,
                rs3=rs3, a2=a2, b2=b2, s2=s2, loss=loss, dy4=dy4, dy4b=dy4b, dg4=dg4, db4=db4)


def _backward(sv, wt, small, send):
    ln, sg = _small_views(small)
    gs = {"ln4_g": sv["dg4"], "ln4_b": sv["db4"]}
    loss, dy4, dy4b = sv["loss"], sv["dy4"], sv["dy4b"]
    g_down2 = _mm_tn(sv["s2"], dy4b, "g_ffn2_down", scale=0.5)
    da2, db2, dy3, dy3b, gs["ln3_g"], gs["ln3_b"] = _ffn_bwd_fused(
        dy4, dy4b, wt["ffn2_w_down"], wt["ffn2_w_gate"], wt["ffn2_w_up"], sv["a2"], sv["b2"], 0.5,
        (sv["xh3"], sv["rs3"], ln["ln3_g"]), "ffn2_bwd")
    g_gate2 = _mm_tn(sv["h3b"], da2, "g_ffn2_gate")
    g_up2 = _mm_tn(sv["h3b"], db2, "g_ffn2_up")
    tok = send(("ffn2_w_down", "ffn2_w_gate", "ffn2_w_up"), (g_down2, g_gate2, g_up2))

    g_o = _mm_tn(sv["att"], dy3b, "g_xa_o", deps=(tok,))
    dqb, dk, dv = _attn_bwd(dy3b, wt["xa_w_o"], sv["qb"], sv["kb"], sv["vb"])
    g_q = _mm_tn(sv["h2b"], dqb, "g_xa_q")
    g_k, g_v, gs["mem_ln_g"], gs["mem_ln_b"] = _mem_bwd(dk, dv, sv["mb"], sv["mxh"], sv["mrs"], ln["mem_ln_g"],
                                                        wt["xa_w_k"], wt["xa_w_v"])
    tok = send(("xa_w_o", "xa_w_q", "xa_w_k", "xa_w_v"), (g_o, g_q, g_k, g_v))
    dy2, dy2b, gs["ln2_g"], gs["ln2_b"] = _dx_ln(dy3, [(dqb, wt["xa_w_q"])], (sv["xh2"], sv["rs2"], ln["ln2_g"]),
                                                 "attn_dx_ln", deps=(tok,))

    g_out = _mm_tn(sv["mix"], dy2b, "g_w_out")
    dmix = _mm_nt(dy2b, wt["w_out"], "mix_out_bwd")
    dq, dfz, div, dgg, dlg, dgn = _hgrn_bwd(sv["proj"], sv["oraw"], dmix, sv["states"], sg["logits"], ln["hg_norm_g"])
    du, dvv, gs["sg_ln_g"], gs["sg_ln_b"], gs["sg_w_s"], gs["sg_b_s"] = _sgu_bwd(
        sv["proj"], dmix, sg["g"], sg["b"], sg["w"], sg["wt"], sg["bs"])
    gs["hg_lb_logits"] = jnp.swapaxes(dlg, 0, 1)
    gs["hg_norm_g"] = jnp.sum(dgn, axis=0)
    dproj = jnp.concatenate([dq, dfz, div, dgg, du, dvv], axis=1)
    g_in = _mm_tn(sv["h1b"], dproj, "g_w_in")
    tok = send(("w_out", "w_in"), (g_out, g_in))
    dy1, dy1b, gs["ln1_g"], gs["ln1_b"] = _dx_ln(dy2, [(dproj, wt["w_in"])], (sv["xh1"], sv["rs1"], ln["ln1_g"]),
                                                 "mix_dx_ln", deps=(tok,))

    g_down1 = _mm_tn(sv["s1"], dy1b, "g_ffn1_down", scale=0.5)
    tok = send(("ffn1_w_down",), (g_down1,))
    da1, db1 = _ffn_bwd_act(dy1b, wt["ffn1_w_down"], sv["a1"], sv["b1"], 0.5, "ffn1_bwd_act", deps=(tok,))
    g_gate1 = _mm_tn(sv["xb"], da1, "g_ffn1_gate")
    tok = send(("ffn1_w_gate",), (g_gate1,))
    g_up1 = _mm_tn(sv["xb"], db1, "g_ffn1_up", deps=(tok,))
    tok = send(("ffn1_w_up",), (g_up1,))
    grad_x = _dx_ln(dy1, [(da1, wt["ffn1_w_gate"]), (db1, wt["ffn1_w_up"])], None, "ffn1_dx", deps=(tok,))
    return loss, grad_x, gs


_WEIGHT_NAMES = ("ffn1_w_gate", "ffn1_w_up", "ffn1_w_down", "ln1_g", "ln1_b", "w_in", "hg_lb_logits", "hg_norm_g",
                 "sg_ln_g", "sg_ln_b", "sg_w_s", "sg_b_s", "w_out", "ln2_g", "ln2_b", "mem_ln_g", "mem_ln_b",
                 "xa_w_q", "xa_w_k", "xa_w_v", "xa_w_o", "ln3_g", "ln3_b", "ffn2_w_gate", "ffn2_w_up", "ffn2_w_down",
                 "ln4_g", "ln4_b")
_FIRST = ("ffn1_w_gate", "ffn1_w_up")
_SECOND = ("ffn1_w_down", "w_in", "w_out")
_THIRD = ("xa_w_k", "xa_w_v", "xa_w_q", "xa_w_o", "ffn2_w_gate", "ffn2_w_up", "ffn2_w_down")


def kernel(x, mem, ffn1_w_gate, ffn1_w_up, ffn1_w_down, ln1_g, ln1_b, w_in, hg_lb_logits, hg_norm_g, sg_ln_g, sg_ln_b, sg_w_s, sg_b_s, w_out, ln2_g, ln2_b, mem_ln_g, mem_ln_b, xa_w_q, xa_w_k, xa_w_v, xa_w_o, ln3_g, ln3_b, ffn2_w_gate, ffn2_w_up, ffn2_w_down, ln4_g, ln4_b, loss_target, m_ffn1_w_gate, m_ffn1_w_up, m_ffn1_w_down, m_ln1_g, m_ln1_b, m_w_in, m_hg_lb_logits, m_hg_norm_g, m_sg_ln_g, m_sg_ln_b, m_sg_w_s, m_sg_b_s, m_w_out, m_ln2_g, m_ln2_b, m_mem_ln_g, m_mem_ln_b, m_xa_w_q, m_xa_w_k, m_xa_w_v, m_xa_w_o, m_ln3_g, m_ln3_b, m_ffn2_w_gate, m_ffn2_w_up, m_ffn2_w_down, m_ln4_g, m_ln4_b, v_ffn1_w_gate, v_ffn1_w_up, v_ffn1_w_down, v_ln1_g, v_ln1_b, v_w_in, v_hg_lb_logits, v_hg_norm_g, v_sg_ln_g, v_sg_ln_b, v_sg_w_s, v_sg_b_s, v_w_out, v_ln2_g, v_ln2_b, v_mem_ln_g, v_mem_ln_b, v_xa_w_q, v_xa_w_k, v_xa_w_v, v_xa_w_o, v_ln3_g, v_ln3_b, v_ffn2_w_gate, v_ffn2_w_up, v_ffn2_w_down, v_ln4_g, v_ln4_b):
    args = dict(locals())
    w = {k: args[k] for k in _WEIGHT_NAMES}
    m = {k: args["m_" + k] for k in _WEIGHT_NAMES}
    v = {k: args["v_" + k] for k in _WEIGHT_NAMES}
    shards = {k: w[k][0] for k in _BIG_NAMES}
    shard_shapes = {k: shards[k].shape for k in _BIG_NAMES}
    small = {k: (w[k][0] if k != "hg_lb_logits" else w[k]) for k in _SMALL_NAMES}

    srcs1, shapes1, pieces1, idx1 = _gather_plan(_FIRST, shards)
    lands1 = _place_own(srcs1, shapes1, pieces1, "gather_first_own")
    rest = _SECOND + _THIRD
    srcs2, shapes2, pieces2, idx2 = _gather_plan(rest, shards)
    lands2 = _place_own(srcs2, shapes2, pieces2, "gather_rest_own")
    groups2 = [list(idx2[k]) for k in rest]
    lands1, tok1 = _routed_gather(srcs1, lands1, pieces1, tuple(lands2), "gather_first")
    sems2, srcs2, lands2, tok2 = _comm_start(srcs2, lands2, pieces2, groups2, "gather_rest_start", after=(tok1,))
    wt = dict(zip(_FIRST, lands1))
    pending = {k: gi for gi, k in enumerate(rest)}

    def get_w(name, after):
        if name in pending:
            gi = pending.pop(name)
            si = [pieces2[p][1] for p in groups2[gi]]
            sub = [(pieces2[p][0], row, 0) + pieces2[p][3:] for row, p in enumerate(groups2[gi])]
            wt[name] = _comm_wait([srcs2[s] for s in si], [lands2[gi]], sub, list(range(len(sub))), sems2[gi],
                                  after, "gather_wait_" + name)[0]
        return wt[name]

    sv = _forward(x[0], mem[0], loss_target[0], get_w, small, first_deps=(tok2,))

    sent = []

    def send(names, grads):
        srcs, shapes, pieces, idx = _scatter_plan(names, dict(zip(names, grads)), shard_shapes)
        lands = _place_own(srcs, shapes, pieces, "grads_own_%d" % len(sent))
        sems, srcs, lands, tok = _comm_start(srcs, lands, pieces, [list(range(len(pieces)))],
                                             "grads_start_%d" % len(sent))
        sent.append((names, srcs, lands, pieces, idx, sems[0]))
        return tok

    loss, grad_x, gs = _backward(sv, wt, small, send)

    ssrc = list(_pack_small_grads(gs, {k: w[k].shape for k in _SMALL_NAMES}, loss))
    sp = [("scatter", i, i, 0, 0, 0, a.shape) for i, a in enumerate(ssrc)]
    sshape = [jax.ShapeDtypeStruct((N_DEV,) + a.shape, F32) for a in ssrc]
    sl = _place_own(ssrc, sshape, sp, "small_own")
    ssem, ssrc, sl, _ = _comm_start(ssrc, sl, sp, [[0, 1]], "small_start")

    out_g, out_d, out_m, out_v = {}, {}, {}, {}
    after = (grad_x,)
    for n_sent, (names, srcs, lands, pieces, idx, sems) in enumerate(sent):
        lands = _comm_wait(srcs, lands, pieces, list(range(len(pieces))), sems, after, "grads_wait_%d" % n_sent)
        for k in names:
            axis = 1 if (k in _COL_FFN or k == "w_in") else 0
            if k in _COL_FFN:
                res = _adam_sharded([lands[i] for i in idx[k]], w[k][0].T, m[k][0].T, v[k][0].T, axis, "adam_" + k)
                res = [r.T for r in res]
            else:
                res = _adam_sharded([lands[i] for i in idx[k]], w[k][0], m[k][0], v[k][0], axis, "adam_" + k)
            out_g[k], out_d[k], out_m[k], out_v[k] = [r[None] for r in res]
        after = (out_v[names[-1]],)
    sl = _comm_wait(ssrc, sl, sp, [0, 1], ssem[0], after, "small_wait")
    small_out, loss_sum = _adam_small(sl[0], sl[1], w, m, v)
    for dst, res in zip((out_g, out_d, out_m, out_v), small_out):
        dst.update(res)
    loss_all = loss_sum[0, 0]
    return (loss_all, grad_x[None], *[out_g[k] for k in _WEIGHT_NAMES], *[out_d[k] for k in _WEIGHT_NAMES],
            *[out_m[k] for k in _WEIGHT_NAMES], *[out_v[k] for k in _WEIGHT_NAMES])
```

```python
import itertools

import jax
import jax.numpy as jnp
import numpy as np
from jax import lax
from jax.experimental import pallas as pl
from jax.experimental.pallas import tpu as pltpu

F32 = jnp.float32
BF16 = jnp.bfloat16

N_DEV = 8
ALPHA = 2.0 ** 0.25
LN_EPS = 1e-5
HG_HEADS = 4
HG_DIM = 128
SG_GROUPS = 4
SG_DIM = 128
SG_CHUNK = 128
X_HEADS = 4
HG_BLOCK = 16
HG_UNROLL = 16
ADAM_LR = 0.001
ADAM_B1 = 0.9
ADAM_B2 = 0.999
ADAM_EPS = 1e-08
ADAM_WD = 0.01
ADAM_STEP = 10
VMEM_LIMIT_V7X = 48 * 1024 * 1024
MXU_WIDTH_V7X = 256
LANES = 128
MESH_ID = pl.DeviceIdType.MESH
ANY = pl.BlockSpec(memory_space=pl.ANY)
HBM = pl.BlockSpec(memory_space=pltpu.HBM)
SEM = pl.BlockSpec(memory_space=pltpu.SEMAPHORE)
DATAFLOW = pltpu.SideEffectType.DATAFLOW_SIDE_EFFECTING


def _params(n_axes):
    return pltpu.CompilerParams(dimension_semantics=("arbitrary",) * n_axes, vmem_limit_bytes=VMEM_LIMIT_V7X)


def _dot(a, b):
    return jnp.dot(a, b, preferred_element_type=F32)


def _dot_nt(a, b):
    return lax.dot_general(a, b, (((1,), (1,)), ((), ())), preferred_element_type=F32)


def _dot_tn(a, b):
    return lax.dot_general(a, b, (((0,), (0,)), ((), ())), preferred_element_type=F32)


def _sigmoid(x):
    return 1.0 / (1.0 + jnp.exp(-x))


def _silu_and_grad(a):
    sig = _sigmoid(a)
    return a * sig, sig * (1.0 + a * (1.0 - sig))


_GELU_C = 0.7978845608028654


def _gelu_and_grad(x):
    inner = _GELU_C * (x + 0.044715 * x * x * x)
    t = jnp.tanh(inner)
    val = 0.5 * x * (1.0 + t)
    grad = 0.5 * (1.0 + t) + 0.5 * x * (1.0 - t * t) * _GELU_C * (1.0 + 3.0 * 0.044715 * x * x)
    return val, grad


def _ln_fwd(y, g, b):
    mu = jnp.mean(y, axis=-1, keepdims=True)
    yc = y - mu
    var = jnp.mean(yc * yc, axis=-1, keepdims=True)
    rstd = lax.rsqrt(var + LN_EPS)
    xhat = yc * rstd
    return xhat * g + b, xhat, rstd


def _ln_bwd(dh, xhat, rstd, g):
    dxh = dh * g
    m1 = jnp.mean(dxh, axis=-1, keepdims=True)
    m2 = jnp.mean(dxh * xhat, axis=-1, keepdims=True)
    dy = rstd * (dxh - m1 - xhat * m2)
    dg = jnp.sum(dh * xhat, axis=0, keepdims=True)
    db = jnp.sum(dh, axis=0, keepdims=True)
    return dy, dg, db


def _mask_dot(mask, x):
    hi = x.astype(BF16)
    lo = (x - hi.astype(F32)).astype(BF16)
    n = mask.shape[0]
    parts = [_dot(mask, hi[r:r + n, :]) + _dot(mask, lo[r:r + n, :]) for r in range(0, x.shape[0], n)]
    return parts[0] if len(parts) == 1 else jnp.concatenate(parts, axis=0)


def _block_masks(n):
    r = np.arange(n)[:, None]
    c = np.arange(n)[None, :]
    same = (r // HG_BLOCK) == (c // HG_BLOCK)
    return jnp.asarray(np.stack([same & (c <= r), same & (c >= r), same]), BF16)


def _row_tile(t):
    return min(t, 512)


def _col_tile(n):
    for cand in (512, 256, 128):
        if n % cand == 0:
            return cand
    return n


def _resident(w):
    return pl.BlockSpec(w.shape, lambda *_: (0, 0), pipeline_mode=pl.Buffered(1))


def _drop_deps(body, n_in, n_deps):
    if n_deps == 0:
        return body
    return lambda *refs: body(*refs[:n_in], *refs[n_in + n_deps:])


def _to_bf16(x, name, deps=()):
    t, d = x.shape
    tm = _row_tile(t)

    def body(x_ref, o_ref):
        o_ref[...] = x_ref[...].astype(BF16)

    row = pl.BlockSpec((tm, d), lambda i: (i, 0))
    return pl.pallas_call(
        _drop_deps(body, 1, len(deps)),
        grid=(t // tm,),
        in_specs=[row] + [ANY] * len(deps),
        out_specs=row,
        out_shape=jax.ShapeDtypeStruct((t, d), BF16),
        compiler_params=_params(1),
        name=name,
    )(x, *deps)


def _ffn_up(hb, wg, wu, name, deps=()):
    t, d = hb.shape
    f = wg.shape[1]
    tm = _row_tile(t)
    tn = _col_tile(f)

    def body(h_ref, wg_ref, wu_ref, a_ref, b_ref, s_ref):
        h = h_ref[...]
        for c in range(f // tn):
            cols = slice(c * tn, (c + 1) * tn)
            a = _dot(h, wg_ref[:, cols])
            b = _dot(h, wu_ref[:, cols])
            a_ref[:, cols] = a.astype(BF16)
            b_ref[:, cols] = b.astype(BF16)
            s_ref[:, cols] = (a * _sigmoid(a) * b).astype(BF16)

    act = pl.BlockSpec((tm, f), lambda i: (i, 0))
    return pl.pallas_call(
        _drop_deps(body, 3, len(deps)),
        grid=(t // tm,),
        in_specs=[pl.BlockSpec((tm, d), lambda i: (i, 0)), _resident(wg), _resident(wu)] + [ANY] * len(deps),
        out_specs=[act, act, act],
        out_shape=[jax.ShapeDtypeStruct((t, f), BF16)] * 3,
        compiler_params=_params(1),
        name=name,
    )(hb, wg, wu, *deps)


def _mm_res_ln(lhs, w, res, g, b, coef, name, target=None):
    t, kd = lhs.shape
    d = w.shape[1]
    tm = _row_tile(t)
    nt = t // tm
    from_norm = isinstance(res, tuple)
    n_res = 3 if from_norm else 1

    def body(*refs):
        l_ref, w_ref = refs[:2]
        r_refs = refs[2:2 + n_res]
        g_ref, b_ref = refs[2 + n_res:4 + n_res]
        rest = refs[4 + n_res:]
        prev = r_refs[0][...] * r_refs[1][...] + r_refs[2][...] if from_norm else r_refs[0][...]
        y = ALPHA * prev + coef * _dot(l_ref[...], w_ref[...])
        h, xhat, rstd = _ln_fwd(y, g_ref[...], b_ref[...])
        if target is None:
            hb_ref, xh_ref, rs_ref = rest
            hb_ref[...] = h.astype(BF16)
            xh_ref[...] = xhat
            rs_ref[...] = rstd
            return
        t_ref, loss_ref, dy_ref, dyb_ref, dg_ref, db_ref, lacc = rest
        i = pl.program_id(0)

        @pl.when(i == 0)
        def _():
            lacc[...] = jnp.zeros_like(lacc)
            dg_ref[...] = jnp.zeros_like(dg_ref)
            db_ref[...] = jnp.zeros_like(db_ref)

        err = h - t_ref[...]
        lacc[...] += jnp.sum(err * err, axis=0, keepdims=True)
        dy, dg, db = _ln_bwd(err * (1.0 / d), xhat, rstd, g_ref[...])
        dy_ref[...] = dy
        dyb_ref[...] = dy.astype(BF16)
        dg_ref[...] += dg
        db_ref[...] += db

        @pl.when(i == nt - 1)
        def _():
            loss_ref[...] = jnp.zeros_like(loss_ref) + jnp.sum(lacc[...], axis=1, keepdims=True) * (0.5 / d)

    row = pl.BlockSpec((tm, d), lambda i: (i, 0))
    vec = pl.BlockSpec((1, d), lambda i: (0, 0))
    res_specs = [row, vec, vec] if from_norm else [row]
    res_args = list(res) if from_norm else [res]
    in_specs = [pl.BlockSpec((tm, kd), lambda i: (i, 0)), _resident(w)] + res_specs + [vec, vec]
    args = [lhs, w] + res_args + [g, b]
    if target is None:
        out_specs = [row, row, pl.BlockSpec((tm, 1), lambda i: (i, 0))]
        out_shape = [jax.ShapeDtypeStruct((t, d), BF16), jax.ShapeDtypeStruct((t, d), F32),
                     jax.ShapeDtypeStruct((t, 1), F32)]
        scratch = []
    else:
        in_specs.append(row)
        args.append(target)
        out_specs = [pl.BlockSpec((1, LANES), lambda i: (0, 0)), row, row, vec, vec]
        out_shape = [jax.ShapeDtypeStruct((1, LANES), F32), jax.ShapeDtypeStruct((t, d), F32),
                     jax.ShapeDtypeStruct((t, d), BF16), jax.ShapeDtypeStruct((1, d), F32),
                     jax.ShapeDtypeStruct((1, d), F32)]
        scratch = [pltpu.VMEM((1, d), F32)]
    return pl.pallas_call(
        body,
        grid=(nt,),
        in_specs=in_specs,
        out_specs=out_specs,
        out_shape=out_shape,
        scratch_shapes=scratch,
        compiler_params=_params(1),
        name=name,
    )(*args)


def _store_slabs(o_ref, first, tile):
    for s in range(tile.shape[1] // LANES):
        o_ref[first + s] = tile[:, s * LANES:(s + 1) * LANES]


def _mm_nn(lhs, w, name):
    t, kd = lhs.shape
    n = w.shape[1]
    tm = _row_tile(t)
    tn = _col_tile(n)

    def body(l_ref, w_ref, o_ref):
        lhs_v = l_ref[...]
        for c in range(n // tn):
            _store_slabs(o_ref, c * (tn // LANES), _dot(lhs_v, w_ref[:, c * tn:(c + 1) * tn]))

    return pl.pallas_call(
        body,
        grid=(t // tm,),
        in_specs=[pl.BlockSpec((tm, kd), lambda i: (i, 0)), _resident(w)],
        out_specs=pl.BlockSpec((n // LANES, tm, LANES), lambda i: (0, i, 0)),
        out_shape=jax.ShapeDtypeStruct((n // LANES, t, LANES), F32),
        compiler_params=_params(1),
        name=name,
    )(lhs, w)


def _lower_bound(lg):
    m = jnp.max(lg, axis=0, keepdims=True)
    e = jnp.exp(lg - m)
    return e[0:1, :] / jnp.sum(e, axis=0, keepdims=True)


def _forget_terms(fz, lb):
    e = jnp.exp(-jnp.abs(fz))
    r = 1.0 / (1.0 + e)
    pos = fz >= 0.0
    sig = jnp.where(pos, r, e * r)
    nsig = jnp.where(pos, e * r, r)
    f = lb + (1.0 - lb) * sig
    k = (1.0 - lb) * nsig
    return sig, nsig, f, k


def _hg_tile(t):
    return min(t, 1024)


HG_HALF = HG_BLOCK // 2
NEG_BIG = -1e30


def _halves(a):
    return a[:HG_HALF, :], a[HG_HALF:, :]


def _causal_halves(s):
    return (0, 1) if s < HG_HALF else (1,)


def _decay_from(b_half, b_s, s, h, tidx):
    first = s - h * HG_HALF
    diff = b_half - b_s
    if first > 0:
        diff = jnp.where(tidx >= first, diff, NEG_BIG)
    return jnp.exp(diff)


def _hgrn_fwd(proj, logits, gn):
    t = proj.shape[1]
    ct = _hg_tile(t)
    nct = t // ct
    nblk = ct // HG_BLOCK
    nh = HG_HEADS
    mrows = min(ct, 256)

    def body(q_ref, fz_ref, iv_ref, gg_ref, lg_ref, gn_ref, mask_ref, oraw_ref, oa_ref, st_ref,
             state, qt_s, kt_s, k_s, b_s, dec_s):
        c = pl.program_id(1)

        @pl.when(c == 0)
        def _():
            state[...] = jnp.zeros_like(state)

        lb = _lower_bound(lg_ref[...])
        q = q_ref[...]
        _, _, f, k = _forget_terms(fz_ref[...], lb)
        logf = jnp.log(f)
        b = _mask_dot(mask_ref[0], logf)
        bend = _mask_dot(mask_ref[2], logf)
        qt_s[...] = (q * jnp.exp(b)).astype(BF16)
        kt_s[...] = (k * jnp.exp(bend - b)).astype(BF16)
        k_s[...] = k
        b_s[...] = b
        dec_s[...] = jnp.exp(bend)
        tidx = lax.broadcasted_iota(jnp.int32, (HG_HALF, HG_DIM), 0)

        def blk(i, carry):
            r0 = pl.multiple_of(i * HG_BLOCK, HG_BLOCK)
            rows = pl.ds(r0, HG_BLOCK)
            st = state[...]
            stb = st.astype(BF16)
            st_ref[i] = stb
            v = iv_ref[rows, :]
            qq = q_ref[rows, :]
            kk = k_s[rows, :]
            bb = b_s[rows, :]
            o = list(_halves(_dot_nt(qt_s[rows, :], stb)))
            qh, bh = _halves(qq), _halves(bb)
            for s in range(HG_BLOCK):
                ks, vs = kk[s:s + 1, :], v[s:s + 1, :]
                for h in _causal_halves(s):
                    e = _decay_from(bh[h], bb[s:s + 1, :], s, h, tidx)
                    acol = jnp.sum(qh[h] * (ks * e), axis=1, keepdims=True)
                    o[h] = o[h] + acol * vs
            oraw_ref[rows, :] = jnp.concatenate(o, axis=0)
            state[...] = st * dec_s[pl.ds(r0, 1), :] + _dot_tn(v.astype(BF16), kt_s[rows, :])
            return carry

        lax.fori_loop(0, nblk, blk, 0, unroll=HG_UNROLL)
        oraw = oraw_ref[...]
        r = lax.rsqrt(jnp.mean(oraw * oraw, axis=-1, keepdims=True) + LN_EPS)
        gg = gg_ref[...]
        oa_ref[...] = (oraw * r * gn_ref[...] * gg * _sigmoid(gg)).astype(BF16)

    def slab(off):
        return pl.BlockSpec((None, ct, HG_DIM), lambda h, c: (off + h, c, 0))

    return pl.pallas_call(
        body,
        grid=(nh, nct),
        in_specs=[slab(0), slab(nh), slab(2 * nh), slab(3 * nh),
                  pl.BlockSpec((None, 2, HG_DIM), lambda h, c: (h, 0, 0)),
                  pl.BlockSpec((1, HG_DIM), lambda h, c: (0, 0)),
                  pl.BlockSpec((3, mrows, mrows), lambda h, c: (0, 0, 0))],
        out_specs=[slab(0), pl.BlockSpec((ct, HG_DIM), lambda h, c: (c, h)),
                   pl.BlockSpec((None, nblk, HG_DIM, HG_DIM), lambda h, c: (h, c, 0, 0))],
        out_shape=[jax.ShapeDtypeStruct((nh, t, HG_DIM), F32),
                   jax.ShapeDtypeStruct((t, (nh + SG_GROUPS) * HG_DIM), BF16),
                   jax.ShapeDtypeStruct((nh, t // HG_BLOCK, HG_DIM, HG_DIM), BF16)],
        scratch_shapes=[pltpu.VMEM((HG_DIM, HG_DIM), F32), pltpu.VMEM((ct, HG_DIM), BF16),
                        pltpu.VMEM((ct, HG_DIM), BF16), pltpu.VMEM((ct, HG_DIM), F32),
                        pltpu.VMEM((ct, HG_DIM), F32), pltpu.VMEM((ct, HG_DIM), F32)],
        compiler_params=_params(2),
        name="hgrn_fwd",
    )(proj, proj, proj, proj, logits, gn, _block_masks(mrows))


def _sg_tile(t):
    return min(t, 512)


def _sgu_chunk_fwd(u, v, ln_g, ln_b, wm, bs):
    ua, dua = _gelu_and_grad(u)
    va, dva = _gelu_and_grad(v)
    vn, xhat, rstd = _ln_fwd(va, ln_g, ln_b)
    s = _dot(wm, vn.astype(BF16)) + bs
    return ua, dua, dva, vn, xhat, rstd, s


def _tril_weight(w):
    n = SG_CHUNK
    r = lax.broadcasted_iota(jnp.int32, (n, n), 0)
    c = lax.broadcasted_iota(jnp.int32, (n, n), 1)
    return jnp.where(c <= r, w, 0.0)


def _sgu_fwd(proj, mix, ln_g, ln_b, w_s, b_col):
    t = proj.shape[1]
    ct = _sg_tile(t)
    ng = SG_GROUPS
    wide = ng * SG_DIM
    blk_u = 4 * HG_HEADS // ng

    def body(u_ref, v_ref, g_ref, b_ref, w_ref, bs_ref, mix_ref, o_ref):
        del mix_ref
        for g in range(ng):
            lanes = slice(g * SG_DIM, (g + 1) * SG_DIM)
            wm = _tril_weight(w_ref[g]).astype(BF16)
            for n in range(ct // SG_CHUNK):
                rows = slice(n * SG_CHUNK, (n + 1) * SG_CHUNK)
                ua, _, _, _, _, _, s = _sgu_chunk_fwd(u_ref[g, rows, :], v_ref[g, rows, :], g_ref[g], b_ref[g], wm,
                                                      bs_ref[g])
                o_ref[rows, lanes] = (ua * s).astype(BF16)

    full = lambda a: pl.BlockSpec(a.shape, lambda c: (0,) * a.ndim)
    return pl.pallas_call(
        body,
        grid=(t // ct,),
        in_specs=[pl.BlockSpec((ng, ct, SG_DIM), lambda c: (blk_u, c, 0)),
                  pl.BlockSpec((ng, ct, SG_DIM), lambda c: (blk_u + 1, c, 0)),
                  full(ln_g), full(ln_b), full(w_s), full(b_col), ANY],
        out_specs=pl.BlockSpec((ct, wide), lambda c: (c, 1)),
        out_shape=jax.ShapeDtypeStruct(mix.shape, mix.dtype),
        input_output_aliases={6: 0},
        compiler_params=_params(1),
        name="sgu_fwd",
    )(proj, proj, ln_g, ln_b, w_s, b_col, mix)


def _mem_kv(mem, g, b, wk, wv):
    m_len, d = mem.shape

    def body(m_ref, g_ref, b_ref, wk_ref, wv_ref, mb_ref, xh_ref, rs_ref, k_ref, v_ref):
        m, xhat, rstd = _ln_fwd(m_ref[...], g_ref[...], b_ref[...])
        mb = m.astype(BF16)
        mb_ref[...] = mb
        xh_ref[...] = xhat
        rs_ref[...] = rstd
        k_ref[...] = _dot(mb, wk_ref[...]).astype(BF16)
        v_ref[...] = _dot(mb, wv_ref[...]).astype(BF16)

    return pl.pallas_call(
        body,
        out_shape=[jax.ShapeDtypeStruct((m_len, d), BF16), jax.ShapeDtypeStruct((m_len, d), F32),
                   jax.ShapeDtypeStruct((m_len, 1), F32), jax.ShapeDtypeStruct((m_len, d), BF16),
                   jax.ShapeDtypeStruct((m_len, d), BF16)],
        compiler_params=pltpu.CompilerParams(vmem_limit_bytes=VMEM_LIMIT_V7X),
        name="mem_kv",
    )(mem, g, b, wk, wv)


def _softmax_rows(s):
    m = jnp.max(s, axis=-1, keepdims=True)
    p = jnp.exp(s - m)
    return p / jnp.sum(p, axis=-1, keepdims=True)


def _attn_fwd(hb, wq, kb, vb):
    t, d = hb.shape
    tm = _row_tile(t)
    dh = d // X_HEADS
    scale = dh ** -0.5

    def body(h_ref, wq_ref, k_ref, v_ref, q_ref, o_ref):
        q = _dot(h_ref[...], wq_ref[...]).astype(BF16)
        q_ref[...] = q
        for hd in range(X_HEADS):
            sl = slice(hd * dh, (hd + 1) * dh)
            p = _softmax_rows(_dot_nt(q[:, sl], k_ref[:, sl]) * scale)
            o_ref[:, sl] = _dot(p.astype(BF16), v_ref[:, sl]).astype(BF16)

    row = pl.BlockSpec((tm, d), lambda i: (i, 0))
    full = lambda a: pl.BlockSpec(a.shape, lambda i: (0, 0))
    return pl.pallas_call(
        body,
        grid=(t // tm,),
        in_specs=[row, full(wq), full(kb), full(vb)],
        out_specs=[row, row],
        out_shape=[jax.ShapeDtypeStruct((t, d), BF16), jax.ShapeDtypeStruct((t, d), BF16)],
        compiler_params=_params(1),
        name="attn_fwd",
    )(hb, wq, kb, vb)


def _ffn_bwd_act(dyb, wd, a, b, coef, name, deps=()):
    t, d = dyb.shape
    f = wd.shape[0]
    tm = _row_tile(t)
    tn = _col_tile(f)

    def body(dy_ref, wd_ref, a_ref, b_ref, da_ref, db_ref):
        dy = dy_ref[...]
        for c in range(f // tn):
            cols = slice(c * tn, (c + 1) * tn)
            ds = _dot_nt(dy, wd_ref[cols, :]) * coef
            silu, dsilu = _silu_and_grad(a_ref[:, cols].astype(F32))
            da_ref[:, cols] = (ds * b_ref[:, cols].astype(F32) * dsilu).astype(BF16)
            db_ref[:, cols] = (ds * silu).astype(BF16)

    act = pl.BlockSpec((tm, f), lambda i: (i, 0))
    return pl.pallas_call(
        _drop_deps(body, 4, len(deps)),
        grid=(t // tm,),
        in_specs=[pl.BlockSpec((tm, d), lambda i: (i, 0)), _resident(wd), act, act] + [ANY] * len(deps),
        out_specs=[act, act],
        out_shape=[jax.ShapeDtypeStruct((t, f), BF16), jax.ShapeDtypeStruct((t, f), BF16)],
        compiler_params=_params(1),
        name=name,
    )(dyb, wd, a, b, *deps)


def _ffn_bwd_fused(dy, dyb, wd, wg, wu, a, b, coef, ln, name):
    t, d = dy.shape
    f = wd.shape[0]
    tm = min(t, 256)
    tn = _col_tile(f)

    def body(dy_ref, dyb_ref, wd_ref, wg_ref, wu_ref, a_ref, b_ref, xh_ref, rs_ref, g_ref,
             da_ref, db_ref, dyo_ref, dyob_ref, dg_ref, dbl_ref):
        dyb_v = dyb_ref[...]
        dh = ALPHA * dy_ref[...]
        for c in range(f // tn):
            cols = slice(c * tn, (c + 1) * tn)
            ds = _dot_nt(dyb_v, wd_ref[cols, :]) * coef
            silu, dsilu = _silu_and_grad(a_ref[:, cols].astype(F32))
            da = (ds * b_ref[:, cols].astype(F32) * dsilu).astype(BF16)
            db = (ds * silu).astype(BF16)
            da_ref[:, cols] = da
            db_ref[:, cols] = db
            dh = dh + _dot_nt(da, wg_ref[:, cols]) + _dot_nt(db, wu_ref[:, cols])

        @pl.when(pl.program_id(0) == 0)
        def _():
            dg_ref[...] = jnp.zeros_like(dg_ref)
            dbl_ref[...] = jnp.zeros_like(dbl_ref)

        dyp, dg, dbl = _ln_bwd(dh, xh_ref[...], rs_ref[...], g_ref[...])
        dyo_ref[...] = dyp
        dyob_ref[...] = dyp.astype(BF16)
        dg_ref[...] += dg
        dbl_ref[...] += dbl

    row = pl.BlockSpec((tm, d), lambda i: (i, 0))
    act = pl.BlockSpec((tm, f), lambda i: (i, 0))
    vec = pl.BlockSpec((1, d), lambda i: (0, 0))
    return pl.pallas_call(
        body,
        grid=(t // tm,),
        in_specs=[row, row, _resident(wd), _resident(wg), _resident(wu), act, act, row,
                  pl.BlockSpec((tm, 1), lambda i: (i, 0)), vec],
        out_specs=[act, act, row, row, vec, vec],
        out_shape=[jax.ShapeDtypeStruct((t, f), BF16), jax.ShapeDtypeStruct((t, f), BF16),
                   jax.ShapeDtypeStruct((t, d), F32), jax.ShapeDtypeStruct((t, d), BF16),
                   jax.ShapeDtypeStruct((1, d), F32), jax.ShapeDtypeStruct((1, d), F32)],
        compiler_params=_params(1),
        name=name,
    )(dy, dyb, wd, wg, wu, a, b, *ln)


def _mm_tn(a, b, name, scale=1.0, deps=()):
    t, m = a.shape
    bs = list(b) if isinstance(b, (list, tuple)) else [b]
    n = sum(piece.shape[1] for piece in bs)
    tt = _row_tile(t)
    nt = t // tt
    tm_o, tn_o = m, n

    def body(a_ref, *refs):
        b_refs, (o_ref, acc) = refs[:len(bs)], refs[len(bs):]
        k = pl.program_id(2)

        @pl.when(k == 0)
        def _():
            acc[...] = jnp.zeros_like(acc)

        first = 0
        for b_ref in b_refs:
            cols = slice(first, first + b_ref.shape[1])
            acc[:, cols] += _dot_tn(a_ref[...], b_ref[...])
            first = cols.stop

        @pl.when(k == nt - 1)
        def _():
            o_ref[...] = (acc[...] * scale).astype(BF16)

    return pl.pallas_call(
        _drop_deps(body, 1 + len(bs), len(deps)),
        grid=(m // tm_o, n // tn_o, nt),
        in_specs=[pl.BlockSpec((tt, tm_o), lambda i, j, k: (k, i))]
        + [pl.BlockSpec((tt, piece.shape[1]), lambda i, j, k: (k, j)) for piece in bs] + [ANY] * len(deps),
        out_specs=pl.BlockSpec((tm_o, tn_o), lambda i, j, k: (i, j)),
        out_shape=jax.ShapeDtypeStruct((m, n), BF16),
        scratch_shapes=[pltpu.VMEM((tm_o, tn_o), F32)],
        compiler_params=_params(3),
        name=name,
    )(a, *bs, *deps)


def _mm_nt(lhs, w, name):
    t, d = lhs.shape
    kd = w.shape[0]
    tm = _row_tile(t)

    def body(l_ref, w_ref, o_ref):
        _store_slabs(o_ref, 0, _dot_nt(l_ref[...], w_ref[...]))

    return pl.pallas_call(
        body,
        grid=(t // tm,),
        in_specs=[pl.BlockSpec((tm, d), lambda i: (i, 0)), _resident(w)],
        out_specs=pl.BlockSpec((kd // LANES, tm, LANES), lambda i: (0, i, 0)),
        out_shape=jax.ShapeDtypeStruct((kd // LANES, t, LANES), F32),
        compiler_params=_params(1),
        name=name,
    )(lhs, w)


def _dx_ln(dy, pairs, ln, name, deps=()):
    t, d = dy.shape
    npair = len(pairs)
    pairs = [(list(lhs) if isinstance(lhs, (list, tuple)) else [lhs], w) for lhs, w in pairs]
    tm = min(t, 512 // npair)
    nt = t // tm
    n_in = 1 + sum(len(pieces) + 1 for pieces, _ in pairs) + (3 if ln is not None else 0)

    def body(*refs):
        dy_ref = refs[0]
        pos = 1
        dh = ALPHA * dy_ref[...]
        for pieces, _ in pairs:
            w_ref = refs[pos + len(pieces)]
            first = 0
            for l_ref in refs[pos:pos + len(pieces)]:
                cols = slice(first, first + l_ref.shape[1])
                dh = dh + _dot_nt(l_ref[...], w_ref[:, cols])
                first = cols.stop
            pos += len(pieces) + 1
        if ln is not None:
            xh_ref, rs_ref, g_ref = refs[pos:pos + 3]
            dyo_ref, dyb_ref, dg_ref, db_ref = refs[pos + 3:pos + 7]

            @pl.when(pl.program_id(0) == 0)
            def _():
                dg_ref[...] = jnp.zeros_like(dg_ref)
                db_ref[...] = jnp.zeros_like(db_ref)

            dyp, dg, db = _ln_bwd(dh, xh_ref[...], rs_ref[...], g_ref[...])
            dyo_ref[...] = dyp
            dyb_ref[...] = dyp.astype(BF16)
            dg_ref[...] += dg
            db_ref[...] += db
        else:
            refs[pos][...] = dh

    row = pl.BlockSpec((tm, d), lambda i: (i, 0))
    vec = pl.BlockSpec((1, d), lambda i: (0, 0))
    in_specs = [row]
    args = [dy]
    for pieces, w in pairs:
        in_specs += [pl.BlockSpec((tm, piece.shape[1]), lambda i: (i, 0)) for piece in pieces] + [_resident(w)]
        args += pieces + [w]
    if ln is not None:
        in_specs += [row, pl.BlockSpec((tm, 1), lambda i: (i, 0)), vec]
        args += list(ln)
        out_specs = [row, row, vec, vec]
        out_shape = [jax.ShapeDtypeStruct((t, d), F32), jax.ShapeDtypeStruct((t, d), BF16),
                     jax.ShapeDtypeStruct((1, d), F32), jax.ShapeDtypeStruct((1, d), F32)]
    else:
        out_specs = row
        out_shape = jax.ShapeDtypeStruct((t, d), F32)
    return pl.pallas_call(
        _drop_deps(body, n_in, len(deps)),
        grid=(nt,),
        in_specs=in_specs + [ANY] * len(deps),
        out_specs=out_specs,
        out_shape=out_shape,
        compiler_params=_params(1),
        name=name,
    )(*args, *deps)


def _hgrn_bwd(proj, oraw, dmix, states, logits, gn):
    t = proj.shape[1]
    ct = _hg_tile(t)
    nct = t // ct
    nblk = ct // HG_BLOCK
    nh = HG_HEADS
    mrows = min(ct, 256)

    def body(q_ref, fz_ref, iv_ref, gg_ref, or_ref, do_ref, st_ref, lg_ref, gn_ref, mask_ref,
             dq_ref, dfz_ref, div_ref, dgg_ref, dlg_ref, dgn_ref,
             dstate, qt_s, kt_s, k_s, b_s, eb_s, ekb_s, dec_s, dor_s, dbl_s, gr_s, dk_s, dlb_acc):
        c = pl.program_id(1)

        @pl.when(c == 0)
        def _():
            dstate[...] = jnp.zeros_like(dstate)
            dlb_acc[...] = jnp.zeros_like(dlb_acc)
            dgn_ref[...] = jnp.zeros_like(dgn_ref)

        lb = _lower_bound(lg_ref[...])
        q = q_ref[...]
        sig, nsig, f, k = _forget_terms(fz_ref[...], lb)
        logf = jnp.log(f)
        b = _mask_dot(mask_ref[0], logf)
        bend = _mask_dot(mask_ref[2], logf)
        eb = jnp.exp(b)
        ekb = jnp.exp(bend - b)
        qt_s[...] = (q * eb).astype(BF16)
        kt_s[...] = (k * ekb).astype(BF16)
        k_s[...] = k
        b_s[...] = b
        eb_s[...] = eb
        ekb_s[...] = ekb
        dec_s[...] = jnp.exp(bend)
        oraw = or_ref[...]
        r = lax.rsqrt(jnp.mean(oraw * oraw, axis=-1, keepdims=True) + LN_EPS)
        on = oraw * r
        gg = gg_ref[...]
        silu, dsilu = _silu_and_grad(gg)
        doa = do_ref[...]
        gnv = gn_ref[...]
        dgg_ref[...] = (doa * on * gnv * dsilu).astype(BF16)
        dyn = doa * silu
        dgn_ref[...] += jnp.sum(dyn * on, axis=0, keepdims=True)
        don = dyn * gnv
        dor_s[...] = r * (don - on * jnp.mean(don * on, axis=-1, keepdims=True))
        tidx = lax.broadcasted_iota(jnp.int32, (HG_HALF, HG_DIM), 0)

        def blk(ii, carry):
            i = nblk - 1 - ii
            r0 = pl.multiple_of(i * HG_BLOCK, HG_BLOCK)
            rows = pl.ds(r0, HG_BLOCK)
            st = st_ref[i]
            dst = dstate[...]
            dstb = dst.astype(BF16)
            do = dor_s[rows, :]
            dob = do.astype(BF16)
            v = iv_ref[rows, :]
            vb = v.astype(BF16)
            qq = q_ref[rows, :]
            kk = k_s[rows, :]
            bb = b_s[rows, :]
            qt = qt_s[rows, :]
            kt = kt_s[rows, :]
            dec = dec_s[pl.ds(r0, 1), :]
            dkt = _dot(vb, dstb)
            dq = _dot(dob, st) * eb_s[rows, :]
            dk = dkt * ekb_s[rows, :]
            dv = _dot_nt(kt, dstb)
            gend = (jnp.sum(kk * dk, axis=0, keepdims=True)
                    + dec * jnp.sum(dst * st.astype(F32), axis=0, keepdims=True))
            qh, bh, doh = _halves(qq), _halves(bb), _halves(do)
            dqh, dkh, dvh = list(_halves(dq)), list(_halves(dk)), list(_halves(dv))
            for s in range(HG_BLOCK):
                ks, vs = kk[s:s + 1, :], v[s:s + 1, :]
                dk_part = dv_part = None
                for h in _causal_halves(s):
                    e = _decay_from(bh[h], bb[s:s + 1, :], s, h, tidx)
                    ke = ks * e
                    acol = jnp.sum(qh[h] * ke, axis=1, keepdims=True)
                    dacol = jnp.sum(doh[h] * vs, axis=1, keepdims=True)
                    dqh[h] = dqh[h] + dacol * ke
                    pk = dacol * (qh[h] * e)
                    pv = acol * doh[h]
                    dk_part = pk if dk_part is None else dk_part + pk
                    dv_part = pv if dv_part is None else dv_part + pv
                hs, row = divmod(s, HG_HALF)
                dkh[hs] = dkh[hs] + jnp.where(tidx == row, jnp.sum(dk_part, axis=0, keepdims=True), 0.0)
                dvh[hs] = dvh[hs] + jnp.where(tidx == row, jnp.sum(dv_part, axis=0, keepdims=True), 0.0)
            dq = jnp.concatenate(dqh, axis=0)
            dk = jnp.concatenate(dkh, axis=0)
            dv = jnp.concatenate(dvh, axis=0)
            dq_ref[rows, :] = dq.astype(BF16)
            div_ref[rows, :] = dv.astype(BF16)
            dk_s[rows, :] = dk
            dbl_s[rows, :] = qq * dq - kk * dk
            gr_s[rows, :] = jnp.zeros((HG_BLOCK, HG_DIM), F32) + gend
            dstate[...] = dst * dec + _dot_tn(dob, qt)
            return carry

        lax.fori_loop(0, nblk, blk, 0, unroll=HG_UNROLL)
        dlogf = _mask_dot(mask_ref[1], dbl_s[...]) + gr_s[...]
        dk = dk_s[...]
        dfz_ref[...] = ((dlogf / f - dk) * ((1.0 - lb) * sig * nsig)).astype(BF16)
        dlb_acc[...] += jnp.sum((dlogf / f - dk) * nsig, axis=0, keepdims=True)

        @pl.when(c == nct - 1)
        def _():
            dl0 = dlb_acc[...] * lb * (1.0 - lb)
            layer = lax.broadcasted_iota(jnp.int32, (2, HG_DIM), 0)
            dlg_ref[...] = jnp.where(layer == 0, dl0, -dl0)

    def slab(off):
        return pl.BlockSpec((None, ct, HG_DIM), lambda h, c: (off + h, nct - 1 - c, 0))

    out_slab = pl.BlockSpec((ct, HG_DIM), lambda h, c: (nct - 1 - c, h))
    tile_f32 = pltpu.VMEM((ct, HG_DIM), F32)
    tile_b16 = pltpu.VMEM((ct, HG_DIM), BF16)
    slab_shape = jax.ShapeDtypeStruct((t, nh * HG_DIM), BF16)
    return pl.pallas_call(
        body,
        grid=(nh, nct),
        in_specs=[slab(0), slab(nh), slab(2 * nh), slab(3 * nh), slab(0), slab(0),
                  pl.BlockSpec((None, nblk, HG_DIM, HG_DIM), lambda h, c: (h, nct - 1 - c, 0, 0)),
                  pl.BlockSpec((None, 2, HG_DIM), lambda h, c: (h, 0, 0)),
                  pl.BlockSpec((1, HG_DIM), lambda h, c: (0, 0)),
                  pl.BlockSpec((3, mrows, mrows), lambda h, c: (0, 0, 0))],
        out_specs=[out_slab, out_slab, out_slab, out_slab,
                   pl.BlockSpec((None, 2, HG_DIM), lambda h, c: (h, 0, 0)),
                   pl.BlockSpec((None, 1, HG_DIM), lambda h, c: (h, 0, 0))],
        out_shape=[slab_shape, slab_shape, slab_shape, slab_shape,
                   jax.ShapeDtypeStruct((nh, 2, HG_DIM), F32), jax.ShapeDtypeStruct((nh, 1, HG_DIM), F32)],
        scratch_shapes=[pltpu.VMEM((HG_DIM, HG_DIM), F32), tile_b16, tile_b16, tile_f32, tile_f32, tile_f32, tile_f32,
                        tile_f32, tile_f32, tile_f32, tile_f32, tile_f32, pltpu.VMEM((1, HG_DIM), F32)],
        compiler_params=_params(2),
        name="hgrn_bwd",
    )(proj, proj, proj, proj, oraw, dmix, states, logits, gn, _block_masks(mrows))


def _sgu_bwd(proj, dmix, ln_g, ln_b, w_s, w_t, b_col):
    t = proj.shape[1]
    ct = _sg_tile(t)
    nct = t // ct
    ng = SG_GROUPS
    off_u = 4 * HG_HEADS
    off_v = off_u + ng
    n = SG_CHUNK

    def body(u_ref, v_ref, do_ref, g_ref, b_ref, w_ref, wt_ref, bs_ref, du_ref, dv_ref, dg_ref, db_ref, dw_ref, dbs_ref):
        c = pl.program_id(1)

        @pl.when(c == 0)
        def _():
            dg_ref[...] = jnp.zeros_like(dg_ref)
            db_ref[...] = jnp.zeros_like(db_ref)
            dw_ref[...] = jnp.zeros_like(dw_ref)
            dbs_ref[...] = jnp.zeros_like(dbs_ref)

        r = lax.broadcasted_iota(jnp.int32, (n, n), 0)
        cc = lax.broadcasted_iota(jnp.int32, (n, n), 1)
        wm = jnp.where(cc <= r, w_ref[...], 0.0).astype(BF16)
        wmt = jnp.where(r <= cc, wt_ref[...], 0.0).astype(BF16)
        for ci in range(ct // n):
            rows = slice(ci * n, (ci + 1) * n)
            ua, dua, dva, vn, xhat, rstd, s = _sgu_chunk_fwd(u_ref[rows, :], v_ref[rows, :], g_ref[...], b_ref[...],
                                                             wm, bs_ref[...])
            do = do_ref[rows, :]
            du_ref[rows, :] = (do * s * dua).astype(BF16)
            ds = do * ua
            dsb = ds.astype(BF16)
            dbs_ref[...] += jnp.sum(ds, axis=1, keepdims=True)
            dw_ref[...] += _dot_nt(dsb, vn.astype(BF16))
            dvn = _dot(wmt, dsb)
            dva_in, dg, db = _ln_bwd(dvn, xhat, rstd, g_ref[...])
            dg_ref[...] += dg
            db_ref[...] += db
            dv_ref[rows, :] = (dva_in * dva).astype(BF16)

        @pl.when(c == nct - 1)
        def _():
            dw_ref[...] = jnp.where(cc <= r, dw_ref[...], 0.0)

    vec = pl.BlockSpec((None, 1, SG_DIM), lambda g, c: (g, 0, 0))
    mat = pl.BlockSpec((None, n, n), lambda g, c: (g, 0, 0))
    col = pl.BlockSpec((None, n, 1), lambda g, c: (g, 0, 0))
    out_slab = pl.BlockSpec((ct, SG_DIM), lambda g, c: (c, g))
    return pl.pallas_call(
        body,
        grid=(ng, nct),
        in_specs=[pl.BlockSpec((None, ct, SG_DIM), lambda g, c: (off_u + g, c, 0)),
                  pl.BlockSpec((None, ct, SG_DIM), lambda g, c: (off_v + g, c, 0)),
                  pl.BlockSpec((None, ct, SG_DIM), lambda g, c: (ng + g, c, 0)), vec, vec, mat, mat, col],
        out_specs=[out_slab, out_slab, vec, vec, mat, col],
        out_shape=[jax.ShapeDtypeStruct((t, ng * SG_DIM), BF16), jax.ShapeDtypeStruct((t, ng * SG_DIM), BF16),
                   jax.ShapeDtypeStruct((ng, 1, SG_DIM), F32), jax.ShapeDtypeStruct((ng, 1, SG_DIM), F32),
                   jax.ShapeDtypeStruct((ng, n, n), F32), jax.ShapeDtypeStruct((ng, n, 1), F32)],
        compiler_params=_params(2),
        name="sgu_bwd",
    )(proj, proj, dmix, ln_g, ln_b, w_s, w_t, b_col)


def _attn_bwd(dyb, wo, qb, kb, vb):
    t, d = dyb.shape
    m_len = kb.shape[0]
    tm = _row_tile(t)
    dh = d // X_HEADS
    scale = dh ** -0.5

    def body(dy_ref, wo_ref, q_ref, k_ref, v_ref, dq_ref, dk_ref, dv_ref):
        i = pl.program_id(0)

        @pl.when(i == 0)
        def _():
            dk_ref[...] = jnp.zeros_like(dk_ref)
            dv_ref[...] = jnp.zeros_like(dv_ref)

        do = _dot_nt(dy_ref[...], wo_ref[...]).astype(BF16)
        for hd in range(X_HEADS):
            sl = slice(hd * dh, (hd + 1) * dh)
            qh = q_ref[:, sl]
            p = _softmax_rows(_dot_nt(qh, k_ref[:, sl]) * scale)
            doh = do[:, sl]
            dp = _dot_nt(doh, v_ref[:, sl])
            ds = (p * (dp - jnp.sum(dp * p, axis=-1, keepdims=True)) * scale).astype(BF16)
            dq_ref[:, sl] = _dot(ds, k_ref[:, sl]).astype(BF16)
            dk_ref[:, sl] += _dot_tn(ds, qh)
            dv_ref[:, sl] += _dot_tn(p.astype(BF16), doh)

    row = pl.BlockSpec((tm, d), lambda i: (i, 0))
    full = lambda a: pl.BlockSpec(a.shape, lambda i: (0, 0))
    kv = pl.BlockSpec((m_len, d), lambda i: (0, 0))
    return pl.pallas_call(
        body,
        grid=(t // tm,),
        in_specs=[row, full(wo), row, full(kb), full(vb)],
        out_specs=[row, kv, kv],
        out_shape=[jax.ShapeDtypeStruct((t, d), BF16), jax.ShapeDtypeStruct((m_len, d), F32),
                   jax.ShapeDtypeStruct((m_len, d), F32)],
        compiler_params=_params(1),
        name="attn_bwd",
    )(dyb, wo, qb, kb, vb)


def _mem_bwd(dk, dv, mb, xhat, rstd, g, wk, wv):
    m_len, d = dk.shape

    def body(dk_ref, dv_ref, mb_ref, xh_ref, rs_ref, g_ref, wk_ref, wv_ref, gwk_ref, gwv_ref, dg_ref, db_ref):
        dkb = dk_ref[...].astype(BF16)
        dvb = dv_ref[...].astype(BF16)
        mb_v = mb_ref[...]
        gwk_ref[...] = _dot_tn(mb_v, dkb).astype(BF16)
        gwv_ref[...] = _dot_tn(mb_v, dvb).astype(BF16)
        dm = _dot_nt(dkb, wk_ref[...]) + _dot_nt(dvb, wv_ref[...])
        _, dg, db = _ln_bwd(dm, xh_ref[...], rs_ref[...], g_ref[...])
        dg_ref[...] = dg
        db_ref[...] = db

    return pl.pallas_call(
        body,
        out_shape=[jax.ShapeDtypeStruct((d, d), BF16), jax.ShapeDtypeStruct((d, d), BF16),
                   jax.ShapeDtypeStruct((1, d), F32), jax.ShapeDtypeStruct((1, d), F32)],
        compiler_params=pltpu.CompilerParams(vmem_limit_bytes=VMEM_LIMIT_V7X),
        name="mem_bwd",
    )(dk, dv, mb, xhat, rstd, g, wk, wv)


def _adamw(w, g, m, v):
    m = ADAM_B1 * m + (1.0 - ADAM_B1) * g
    v = ADAM_B2 * v + (1.0 - ADAM_B2) * (g * g)
    m_hat = m / (1.0 - ADAM_B1 ** ADAM_STEP)
    v_hat = v / (1.0 - ADAM_B2 ** ADAM_STEP)
    delta = -ADAM_LR * (m_hat / (jnp.sqrt(v_hat) + ADAM_EPS) + ADAM_WD * w)
    return delta, m, v


def _slot_sum(ref):
    g = ref[0].astype(F32)
    for s in range(1, N_DEV):
        g = g + ref[s].astype(F32)
    return g


def _adam_sharded(lands, w, m, v, axis, name):
    rows, cols = w.shape
    nl = len(lands)
    transposed = axis == 1 and nl == 2
    if transposed:
        rows, cols = cols, rows
        tr = 256
        grid = (rows // tr,)
        wblk = pl.BlockSpec((cols, tr), lambda i: (0, i))
        lblk = [pl.BlockSpec((N_DEV, tr, a.shape[2]), lambda i: (0, i, 0)) for a in lands]
    elif axis == 1:
        tr = 256 if rows % 256 == 0 else rows
        grid = (rows // tr,)
        wblk = pl.BlockSpec((tr, cols), lambda i: (i, 0))
        lblk = [pl.BlockSpec((N_DEV, tr, a.shape[2]), lambda i: (0, i, 0)) for a in lands]
    else:
        tc = _col_tile(cols)
        grid = (cols // tc,)
        wblk = pl.BlockSpec((rows, tc), lambda i: (0, i))
        lblk = [pl.BlockSpec((N_DEV, a.shape[1], tc), lambda i: (0, 0, i)) for a in lands]

    def body(*refs):
        w_ref, m_ref, v_ref = refs[nl:nl + 3]
        g_ref, d_ref, nm_ref, nv_ref = refs[nl + 3:]
        g = _slot_sum(refs[0])
        if nl == 2:
            tail = _slot_sum(refs[1])
            if transposed:
                g = jnp.concatenate([g.T, tail.T[:cols - g.shape[1], :]], axis=0)
            elif axis == 1:
                g = jnp.concatenate([g, tail[:, :cols - g.shape[1]]], axis=1)
            else:
                g = jnp.concatenate([g, tail[:rows - g.shape[0], :]], axis=0)
        delta, nm, nv = _adamw(w_ref[...], g, m_ref[...], v_ref[...])
        g_ref[...] = g
        d_ref[...] = delta
        nm_ref[...] = nm
        nv_ref[...] = nv

    shp = jax.ShapeDtypeStruct(w.shape, F32)
    return pl.pallas_call(
        body,
        grid=grid,
        in_specs=lblk + [wblk, wblk, wblk],
        out_specs=[wblk, wblk, wblk, wblk],
        out_shape=[shp, shp, shp, shp],
        compiler_params=_params(1),
        name=name,
    )(*[pltpu.with_memory_space_constraint(a, pltpu.HBM) for a in (*lands, w, m, v)])


def _mesh_pos():
    return lax.axis_index("x"), lax.axis_index("y"), lax.axis_index("c")


def _peer(k):
    x, y, c = _mesh_pos()
    pos = (x ^ (k >> 2), y ^ ((k >> 1) & 1), c ^ (k & 1))
    return pos, 4 * pos[0] + 2 * pos[1] + pos[2]


def _sem_index(row, k):
    return row * (N_DEV - 1) + k - 1


def _window(ref, axis, start, size):
    align = 16 if axis == 0 else LANES
    start = pl.multiple_of(start, align)
    return ref.at[pl.ds(start, size), :] if axis == 0 else ref.at[:, pl.ds(start, size)]


def _piece_refs(piece, srcs, lands, me, peer):
    kind, si, li, axis, base, stride, shape = piece
    if kind == "gather":
        return srcs[si], _window(lands[li], axis, base + stride * me, shape[axis])
    return _window(srcs[si], axis, base + stride * peer, shape[axis]), lands[li].at[me]


def _place_own(srcs, land_shapes, pieces, name):
    ns, nl, npc = len(srcs), len(land_shapes), len(pieces)

    def body(*refs):
        s_refs = refs[:ns]
        l_refs = refs[ns:ns + nl]
        bufs = refs[ns + nl:ns + nl + npc]
        sems = refs[ns + nl + npc]
        x, y, c = _mesh_pos()
        me = 4 * x + 2 * y + c
        loads = []
        for p, piece in enumerate(pieces):
            src, dst = _piece_refs(piece, s_refs, l_refs, me, me)
            cp = pltpu.make_async_copy(src, bufs[p], sems.at[0, p])
            cp.start()
            loads.append((cp, dst))
        stores = []
        for p, (cp, dst) in enumerate(loads):
            cp.wait()
            out = pltpu.make_async_copy(bufs[p], dst, sems.at[1, p])
            out.start()
            stores.append(out)
        for out in stores:
            out.wait()

    out = pl.pallas_call(
        body,
        in_specs=[ANY] * ns,
        out_specs=[ANY] * nl,
        out_shape=list(land_shapes),
        scratch_shapes=[pltpu.VMEM(pc[6], srcs[pc[1]].dtype) for pc in pieces] + [pltpu.SemaphoreType.DMA((2, npc))],
        compiler_params=pltpu.CompilerParams(vmem_limit_bytes=VMEM_LIMIT_V7X),
        name=name,
    )(*srcs)
    return list(out)


def _comm_start(srcs, lands, pieces, groups, name, after=()):
    ns, nl, na, ng = len(srcs), len(lands), len(after), len(groups)

    def body(*refs):
        s_refs = refs[:ns]
        l_refs = refs[ns:ns + nl]
        outs = refs[ns + nl + na:]
        sems = outs[:2 * ng]
        token = outs[-1]
        x, y, c = _mesh_pos()
        me = 4 * x + 2 * y + c
        for g, members in enumerate(groups):
            for row, p in enumerate(members):
                for k in range(1, N_DEV):
                    pos, peer = _peer(k)
                    src, dst = _piece_refs(pieces[p], s_refs, l_refs, me, peer)
                    pltpu.make_async_remote_copy(src_ref=src, dst_ref=dst, send_sem=sems[2 * g].at[_sem_index(row, k)],
                                                 recv_sem=sems[2 * g + 1].at[_sem_index(row, k)], device_id=pos,
                                                 device_id_type=MESH_ID).start()
        token[...] = jnp.zeros_like(token)

    sem_shapes = []
    for members in groups:
        sem_shapes += [pltpu.SemaphoreType.DMA((len(members) * (N_DEV - 1),))] * 2
    hbm_of = lambda a: pltpu.HBM(a.shape, a.dtype)
    out = pl.pallas_call(
        body,
        in_specs=[HBM] * (ns + nl) + [ANY] * na,
        out_specs=[SEM] * (2 * ng) + [HBM] * (ns + nl) + [pl.BlockSpec(memory_space=pltpu.VMEM)],
        out_shape=sem_shapes + [hbm_of(a) for a in srcs] + [hbm_of(a) for a in lands]
        + [jax.ShapeDtypeStruct((8, LANES), F32)],
        input_output_aliases={i: 2 * ng + i for i in range(ns + nl)},
        compiler_params=pltpu.CompilerParams(has_side_effects=DATAFLOW),
        name=name,
    )(*[pltpu.with_memory_space_constraint(a, pltpu.HBM) for a in list(srcs) + list(lands)], *after)
    sems = [(out[2 * g], out[2 * g + 1]) for g in range(ng)]
    return sems, list(out[2 * ng:2 * ng + ns]), list(out[2 * ng + ns:2 * ng + ns + nl]), out[-1]


def _comm_wait(srcs, lands, pieces, members, sems, after, name):
    ns, nl, na = len(srcs), len(lands), len(after)

    def body(*refs):
        s_refs = refs[:ns]
        l_refs = refs[ns:ns + nl]
        send_sems, recv_sems = refs[ns + nl:ns + nl + 2]
        x, y, c = _mesh_pos()
        me = 4 * x + 2 * y + c
        for row, p in enumerate(members):
            for k in range(1, N_DEV):
                pos, peer = _peer(k)
                src, dst = _piece_refs(pieces[p], s_refs, l_refs, me, peer)
                cp = pltpu.make_async_remote_copy(src_ref=src, dst_ref=dst, send_sem=send_sems.at[_sem_index(row, k)],
                                                  recv_sem=recv_sems.at[_sem_index(row, k)], device_id=pos,
                                                  device_id_type=MESH_ID)
                cp.wait_send()
                cp.wait_recv()

    hbm_of = lambda a: pltpu.HBM(a.shape, a.dtype)
    out = pl.pallas_call(
        body,
        in_specs=[HBM] * (ns + nl) + [SEM, SEM] + [ANY] * na,
        out_specs=[HBM] * (ns + nl),
        out_shape=[hbm_of(a) for a in srcs] + [hbm_of(a) for a in lands],
        input_output_aliases={i: i for i in range(ns + nl)},
        compiler_params=pltpu.CompilerParams(has_side_effects=DATAFLOW),
        name=name,
    )(*srcs, *lands, sems[0], sems[1], *after)
    return list(out[ns:])


def _landed_block(piece, lands, owner):
    _, _, li, axis, base, stride, shape = piece
    return _window(lands[li], axis, base + stride * owner, shape[axis])


def _copy_stage(name, bufs, in_sems, out_sem_sizes, emit, after=()):
    nb, ni, no, na = len(bufs), len(in_sems), len(out_sem_sizes), len(after)

    def body(*refs):
        b_refs = refs[:nb]
        i_refs = refs[nb:nb + ni]
        o_refs = refs[nb + ni + na:nb + ni + na + no]
        emit(b_refs, i_refs, o_refs)
        refs[-1][...] = jnp.zeros_like(refs[-1])

    hbm_of = lambda a: pltpu.HBM(a.shape, a.dtype)
    out = pl.pallas_call(
        body,
        in_specs=[HBM] * nb + [SEM] * ni + [ANY] * na,
        out_specs=[SEM] * no + [HBM] * nb + [pl.BlockSpec(memory_space=pltpu.VMEM)],
        out_shape=[pltpu.SemaphoreType.DMA((n,)) for n in out_sem_sizes] + [hbm_of(a) for a in bufs]
        + [jax.ShapeDtypeStruct((8, LANES), F32)],
        input_output_aliases={i: no + i for i in range(nb)},
        compiler_params=pltpu.CompilerParams(has_side_effects=DATAFLOW),
        name=name,
    )(*[pltpu.with_memory_space_constraint(a, pltpu.HBM) for a in bufs], *in_sems, *after)
    return list(out[:no]), list(out[no:no + nb]), out[-1]


def _remote(src, dst, send, recv, to):
    return pltpu.make_async_remote_copy(src_ref=src, dst_ref=dst, send_sem=send, recv_sem=recv, device_id=to,
                                        device_id_type=MESH_ID)


def _routed_gather(srcs, lands, pieces, meanwhile, name):
    ns, npc = len(srcs), len(pieces)

    def places():
        x, y, c = _mesh_pos()
        index = lambda p: 4 * p[0] + 2 * p[1] + p[2]
        me, sib = (x, y, c), (x, y, 1 - c)
        xnb, ynb = (1 - x, y, c), (x, 1 - y, c)
        got_first = (x ^ (1 - c), y ^ c, c)
        pass_to = (x ^ c, y ^ (1 - c), c)
        diag = (1 - x, 1 - y, c)
        return index, me, sib, xnb, ynb, got_first, pass_to, diag

    def start(b, _, o):
        index, me, sib, xnb, ynb, *_rest = places()
        send_a, recv_sib, recv_nb = o
        for p, piece in enumerate(pieces):
            src, dst = _piece_refs(piece, b[:ns], b[ns:], index(me), 0)
            _remote(src, dst, send_a.at[3 * p], recv_sib.at[p], sib).start()
            _remote(src, dst, send_a.at[3 * p + 1], recv_nb.at[2 * p], xnb).start()
            _remote(src, dst, send_a.at[3 * p + 2], recv_nb.at[2 * p + 1], ynb).start()

    def pass_a(b, i, o):
        index, me, sib, xnb, ynb, got_first, pass_to, _diag = places()
        (recv_nb,) = i
        send_f, recv_f, send_d, recv_d = o
        for p, piece in enumerate(pieces):
            for j, nb in enumerate((xnb, ynb)):
                blk = _landed_block(piece, b, index(nb))
                _remote(blk, blk, send_f.at[2 * p + j], recv_nb.at[2 * p + j], sib).wait_recv()
                _remote(blk, blk, send_f.at[2 * p + j], recv_f.at[2 * p + j], sib).start()
            blk = _landed_block(piece, b, index(got_first))
            _remote(blk, blk, send_d.at[p], recv_d.at[p], pass_to).start()

    def pass_b(b, i, o):
        index, me, sib, *_mid, diag = places()
        (recv_d,) = i
        send_g, recv_g = o
        for p, piece in enumerate(pieces):
            blk = _landed_block(piece, b, index(diag))
            _remote(blk, blk, send_g.at[p], recv_d.at[p], sib).wait_recv()
            _remote(blk, blk, send_g.at[p], recv_g.at[p], sib).start()

    def last(b, i, _):
        index, me, sib, *_others = places()
        send_a, recv_sib, send_f, recv_f, send_d, send_g, recv_g = i
        for p, piece in enumerate(pieces):
            src, dst = _piece_refs(piece, b[:ns], b[ns:], index(me), 0)
            cp = lambda s_sem, r_sem: _remote(src, dst, s_sem, r_sem, sib)
            cp(send_a.at[3 * p], recv_sib.at[p]).wait_recv()
            cp(send_a.at[3 * p], recv_g.at[p]).wait_recv()
            for j in range(3):
                cp(send_a.at[3 * p + j], recv_sib.at[p]).wait_send()
            for j in range(2):
                cp(send_f.at[2 * p + j], recv_f.at[2 * p + j]).wait_recv()
                cp(send_f.at[2 * p + j], recv_f.at[2 * p + j]).wait_send()
            cp(send_d.at[p], recv_sib.at[p]).wait_send()
            cp(send_g.at[p], recv_sib.at[p]).wait_send()

    (send_a, recv_sib, recv_nb), bufs, started = _copy_stage(name + "_start", list(srcs) + list(lands), [],
                                                             [3 * npc, npc, 2 * npc], start)
    srcs, lands = bufs[:ns], bufs[ns:]
    (send_f, recv_f, send_d, recv_d), lands, _ = _copy_stage(name + "_pass_a", lands, [recv_nb],
                                                             [2 * npc, 2 * npc, npc, npc],
                                                             lambda b, i, o: pass_a(b, i, o), after=meanwhile(started))
    (send_g, recv_g), lands, tok = _copy_stage(name + "_pass_b", lands, [recv_d], [npc, npc], pass_b)
    _, bufs, _ = _copy_stage(name + "_last", list(srcs) + list(lands),
                             [send_a, recv_sib, send_f, recv_f, send_d, send_g, recv_g], [], last)
    return bufs[ns:], tok


_SMALL_NAMES = ("ln1_g", "ln1_b", "hg_lb_logits", "hg_norm_g", "sg_ln_g", "sg_ln_b", "sg_w_s", "sg_b_s",
                "ln2_g", "ln2_b", "mem_ln_g", "mem_ln_b", "ln3_g", "ln3_b", "ln4_g", "ln4_b")


_VEC_NAMES = ("ln1_g", "ln1_b", "ln2_g", "ln2_b", "mem_ln_g", "mem_ln_b", "ln3_g", "ln3_b", "ln4_g", "ln4_b")
_ROW_NAMES = ("hg_lb_logits", "hg_norm_g", "sg_ln_g", "sg_ln_b", "sg_b_s", "sg_w_s")
VEC_ROWS = 16


def _row_plan(shapes):
    plan, pos = {}, 0
    for k in _ROW_NAMES:
        shp = shapes[k]
        slabs, off = [], pos
        for idx in itertools.product(*[range(dim) for dim in shp[:-2]]):
            slabs.append((idx, off, shp[-2]))
            off += shp[-2]
        plan[k] = (pos, slabs)
        pos = -(-off // 8) * 8
    return plan, -(-pos // 16) * 16


def _pack_small_grads(gs, shapes, loss):
    d = gs[_VEC_NAMES[0]].size
    vec = jnp.concatenate([gs[k].reshape(1, -1) for k in _VEC_NAMES] + [jnp.tile(loss, (1, d // LANES))], axis=0)
    vec = jnp.pad(vec, ((0, VEC_ROWS - vec.shape[0]), (0, 0)))
    plan, total = _row_plan(shapes)
    parts, pos = [], 0
    for k in _ROW_NAMES:
        first, slabs = plan[k]
        rows = gs[k].reshape(-1, LANES)
        end = slabs[-1][1] + slabs[-1][2]
        nxt = -(-end // 8) * 8
        parts.append(jnp.pad(rows, ((0, nxt - first - rows.shape[0]), (0, 0))))
        pos = nxt
    parts.append(jnp.zeros((total - pos, LANES), F32))
    return vec, jnp.concatenate(parts, axis=0)


def _adam_small(land_vec, land_rows, w, m, v):
    names = _VEC_NAMES + _ROW_NAMES
    n = len(names)
    shapes = {k: w[k].shape for k in names}
    plan, _ = _row_plan(shapes)

    def body(*refs):
        lv_ref, lr_ref = refs[:2]
        w_refs, m_refs, v_refs = refs[2:2 + n], refs[2 + n:2 + 2 * n], refs[2 + 2 * n:2 + 3 * n]
        outs = refs[2 + 3 * n:2 + 7 * n]
        loss_ref = refs[2 + 7 * n]
        gv_s, gr_s = refs[3 + 7 * n:]
        gv_s[...] = _slot_sum(lv_ref)
        gr_s[...] = _slot_sum(lr_ref)
        loss_ref[...] = gv_s[len(_VEC_NAMES):len(_VEC_NAMES) + 1, :LANES]
        for p, k in enumerate(names):
            if k in _VEC_NAMES:
                row = _VEC_NAMES.index(k)
                slabs = [((), None, None)]
            else:
                slabs = plan[k][1]
            for idx, off, rows in slabs:
                g = gv_s[row:row + 1, :] if off is None else gr_s[off:off + rows, :]
                sel = idx + (slice(None), slice(None))
                delta, nm, nv = _adamw(w_refs[p][sel], g, m_refs[p][sel], v_refs[p][sel])
                for o, val in zip(range(4), (g, delta, nm, nv)):
                    outs[o * n + p][sel] = val

    flat = lambda tree: [tree[k] for k in names]
    shp = [jax.ShapeDtypeStruct(shapes[k], F32) for k in names]
    out = pl.pallas_call(
        body,
        out_shape=shp * 4 + [jax.ShapeDtypeStruct((1, LANES), F32)],
        scratch_shapes=[pltpu.VMEM(land_vec.shape[1:], F32), pltpu.VMEM(land_rows.shape[1:], F32)],
        name="adam_small",
    )(land_vec, land_rows, *flat(w), *flat(m), *flat(v))
    return [dict(zip(names, out[o * n:(o + 1) * n])) for o in range(4)], out[4 * n]


_COL_FFN = ("ffn1_w_gate", "ffn1_w_up", "ffn2_w_gate", "ffn2_w_up")
_ROW_FFN = ("ffn1_w_down", "ffn2_w_down")
_ROW_SQ = ("w_out", "xa_w_q", "xa_w_k", "xa_w_v", "xa_w_o")
_BIG_NAMES = ("ffn1_w_gate", "ffn1_w_up", "ffn1_w_down", "w_in", "w_out", "xa_w_q", "xa_w_k", "xa_w_v", "xa_w_o",
              "ffn2_w_gate", "ffn2_w_up", "ffn2_w_down")


def _ffn_split(fs):
    main = (fs // MXU_WIDTH_V7X) * MXU_WIDTH_V7X
    tail = fs - main
    tail_pad = -(-tail // LANES) * LANES
    assert main > 0 and tail > 0
    return main, tail, tail_pad


def _layout(name, shard_shape):
    r, c = shard_shape
    if name in _COL_FFN:
        main, tail, pad = _ffn_split(c)
        return (r, N_DEV * (main + pad)), [(1, 0, main, (r, main), (0, main)),
                                           (1, N_DEV * main, pad, (r, pad), (main, c))]
    if name in _ROW_FFN:
        main, tail, pad = _ffn_split(r)
        return (N_DEV * (main + pad), c), [(0, 0, main, (main, c), (0, main)),
                                           (0, N_DEV * main, pad, (pad, c), (main, r))]
    if name == "w_in":
        return (r, N_DEV * c), [(1, 0, c, (r, c), (0, c))]
    return (N_DEV * r, c), [(0, 0, r, (r, c), (0, r))]


def _shard_pieces(name, shard):
    out = []
    for axis, _, _, shape, (lo, hi) in _layout(name, shard.shape)[1]:
        part = shard[lo:hi, :] if axis == 0 else shard[:, lo:hi]
        pad = [(0, shape[0] - part.shape[0]), (0, shape[1] - part.shape[1])]
        out.append(jnp.pad(part, pad).astype(BF16))
    return out


def _gather_plan(names, shards):
    srcs, land_shapes, pieces, index = [], [], [], {}
    for li, name in enumerate(names):
        shape2d, parts = _layout(name, shards[name].shape)
        land_shapes.append(jax.ShapeDtypeStruct(shape2d, BF16))
        index[name] = []
        for (axis, base, stride, shape, _), src in zip(parts, _shard_pieces(name, shards[name])):
            index[name].append(len(pieces))
            pieces.append(("gather", len(srcs), li, axis, base, stride, shape))
            srcs.append(src)
    return srcs, land_shapes, pieces, index


def _scatter_plan(names, grads, shard_shapes):
    srcs, land_shapes, pieces, index = [], [], [], {}
    for si, name in enumerate(names):
        _, parts = _layout(name, shard_shapes[name])
        srcs.append(grads[name])
        index[name] = []
        for axis, base, stride, shape, _ in parts:
            index[name].append(len(land_shapes))
            pieces.append(("scatter", si, len(land_shapes), axis, base, stride, shape))
            land_shapes.append(jax.ShapeDtypeStruct((N_DEV,) + shape, grads[name].dtype))
    return srcs, land_shapes, pieces, index


def _small_views(small):
    row = lambda a: a.reshape(1, -1)
    ln = {k: row(small[k]) for k in ("ln1_g", "ln1_b", "ln2_g", "ln2_b", "ln3_g", "ln3_b", "ln4_g", "ln4_b",
                                      "mem_ln_g", "mem_ln_b", "hg_norm_g")}
    sg_w = small["sg_w_s"].reshape(SG_GROUPS, SG_CHUNK, SG_CHUNK)
    sg = dict(logits=jnp.swapaxes(small["hg_lb_logits"], 0, 1),
              g=small["sg_ln_g"].reshape(SG_GROUPS, 1, SG_DIM), b=small["sg_ln_b"].reshape(SG_GROUPS, 1, SG_DIM),
              w=sg_w, wt=jnp.swapaxes(sg_w, 1, 2), bs=small["sg_b_s"].reshape(SG_GROUPS, SG_CHUNK, 1))
    return ln, sg


def _forward(x, xb, mem, target, get_w, small, first_deps=()):
    ln, sg = _small_views(small)
    a1, b1, s1 = _ffn_up(xb, get_w("ffn1_w_gate", ()), get_w("ffn1_w_up", ()), "ffn1_up", deps=first_deps)
    h1b, xh1, rs1 = _mm_res_ln(s1, get_w("ffn1_w_down", (s1,)), x, ln["ln1_g"], ln["ln1_b"], 0.5, "ffn1_down_ln")
    proj = _mm_nn(h1b, get_w("w_in", (h1b,)), "mix_in")
    oraw, mix, states = _hgrn_fwd(proj, sg["logits"], ln["hg_norm_g"])
    mix = _sgu_fwd(proj, mix, sg["g"], sg["b"], sg["w"], sg["bs"])
    h2b, xh2, rs2 = _mm_res_ln(mix, get_w("w_out", (mix,)), (xh1, ln["ln1_g"], ln["ln1_b"]), ln["ln2_g"], ln["ln2_b"],
                               1.0, "mix_out_ln")
    mb, mxh, mrs, kb, vb = _mem_kv(mem, ln["mem_ln_g"], ln["mem_ln_b"], get_w("xa_w_k", (h2b,)), get_w("xa_w_v", (h2b,)))
    qb, att = _attn_fwd(h2b, get_w("xa_w_q", (kb,)), kb, vb)
    h3b, xh3, rs3 = _mm_res_ln(att, get_w("xa_w_o", (att,)), (xh2, ln["ln2_g"], ln["ln2_b"]), ln["ln3_g"], ln["ln3_b"],
                               1.0, "attn_out_ln")
    a2, b2, s2 = _ffn_up(h3b, get_w("ffn2_w_gate", (h3b,)), get_w("ffn2_w_up", (h3b,)), "ffn2_up")
    loss, dy4, dy4b, dg4, db4 = _mm_res_ln(s2, get_w("ffn2_w_down", (s2,)), (xh3, ln["ln3_g"], ln["ln3_b"]),
                                           ln["ln4_g"], ln["ln4_b"], 0.5, "ffn2_down_ln_loss", target=target)
    return dict(xb=xb, a1=a1, b1=b1, s1=s1, h1b=h1b, xh1=xh1, rs1=rs1, proj=proj, oraw=oraw, mix=mix, states=states,
                h2b=h2b, xh2=xh2, rs2=rs2, mb=mb, mxh=mxh, mrs=mrs, kb=kb, vb=vb, qb=qb, att=att, h3b=h3b, xh3=xh3,
                rs3=rs3, a2=a2, b2=b2, s2=s2, loss=loss, dy4=dy4, dy4b=dy4b, dg4=dg4, db4=db4)


def _backward(sv, wt, small, send):
    ln, sg = _small_views(small)
    gs = {"ln4_g": sv["dg4"], "ln4_b": sv["db4"]}
    loss, dy4, dy4b = sv["loss"], sv["dy4"], sv["dy4b"]
    g_down2 = _mm_tn(sv["s2"], dy4b, "g_ffn2_down", scale=0.5)
    da2, db2, dy3, dy3b, gs["ln3_g"], gs["ln3_b"] = _ffn_bwd_fused(
        dy4, dy4b, wt["ffn2_w_down"], wt["ffn2_w_gate"], wt["ffn2_w_up"], sv["a2"], sv["b2"], 0.5,
        (sv["xh3"], sv["rs3"], ln["ln3_g"]), "ffn2_bwd")
    g_gate2 = _mm_tn(sv["h3b"], da2, "g_ffn2_gate")
    g_up2 = _mm_tn(sv["h3b"], db2, "g_ffn2_up")
    tok = send(("ffn2_w_down", "ffn2_w_gate", "ffn2_w_up"), (g_down2, g_gate2, g_up2))

    g_o = _mm_tn(sv["att"], dy3b, "g_xa_o", deps=(tok,))
    dqb, dk, dv = _attn_bwd(dy3b, wt["xa_w_o"], sv["qb"], sv["kb"], sv["vb"])
    g_q = _mm_tn(sv["h2b"], dqb, "g_xa_q")
    g_k, g_v, gs["mem_ln_g"], gs["mem_ln_b"] = _mem_bwd(dk, dv, sv["mb"], sv["mxh"], sv["mrs"], ln["mem_ln_g"],
                                                        wt["xa_w_k"], wt["xa_w_v"])
    tok = send(("xa_w_o", "xa_w_q", "xa_w_k", "xa_w_v"), (g_o, g_q, g_k, g_v))
    dy2, dy2b, gs["ln2_g"], gs["ln2_b"] = _dx_ln(dy3, [(dqb, wt["xa_w_q"])], (sv["xh2"], sv["rs2"], ln["ln2_g"]),
                                                 "attn_dx_ln", deps=(tok,))

    g_out = _mm_tn(sv["mix"], dy2b, "g_w_out")
    dmix = _mm_nt(dy2b, wt["w_out"], "mix_out_bwd")
    dq, dfz, div, dgg, dlg, dgn = _hgrn_bwd(sv["proj"], sv["oraw"], dmix, sv["states"], sg["logits"], ln["hg_norm_g"])
    du, dvv, gs["sg_ln_g"], gs["sg_ln_b"], gs["sg_w_s"], gs["sg_b_s"] = _sgu_bwd(
        sv["proj"], dmix, sg["g"], sg["b"], sg["w"], sg["wt"], sg["bs"])
    gs["hg_lb_logits"] = jnp.swapaxes(dlg, 0, 1)
    gs["hg_norm_g"] = jnp.sum(dgn, axis=0)
    dproj = [dq, dfz, div, dgg, du, dvv]
    g_in = _mm_tn(sv["h1b"], dproj, "g_w_in")
    tok = send(("w_out", "w_in"), (g_out, g_in))
    dy1, dy1b, gs["ln1_g"], gs["ln1_b"] = _dx_ln(dy2, [(dproj, wt["w_in"])], (sv["xh1"], sv["rs1"], ln["ln1_g"]),
                                                 "mix_dx_ln", deps=(tok,))

    g_down1 = _mm_tn(sv["s1"], dy1b, "g_ffn1_down", scale=0.5)
    tok = send(("ffn1_w_down",), (g_down1,))
    da1, db1 = _ffn_bwd_act(dy1b, wt["ffn1_w_down"], sv["a1"], sv["b1"], 0.5, "ffn1_bwd_act", deps=(tok,))
    g_gate1 = _mm_tn(sv["xb"], da1, "g_ffn1_gate")
    tok = send(("ffn1_w_gate",), (g_gate1,))
    g_up1 = _mm_tn(sv["xb"], db1, "g_ffn1_up", deps=(tok,))
    tok = send(("ffn1_w_up",), (g_up1,))
    grad_x = _dx_ln(dy1, [(da1, wt["ffn1_w_gate"]), (db1, wt["ffn1_w_up"])], None, "ffn1_dx", deps=(tok,))
    return loss, grad_x, gs


_WEIGHT_NAMES = ("ffn1_w_gate", "ffn1_w_up", "ffn1_w_down", "ln1_g", "ln1_b", "w_in", "hg_lb_logits", "hg_norm_g",
                 "sg_ln_g", "sg_ln_b", "sg_w_s", "sg_b_s", "w_out", "ln2_g", "ln2_b", "mem_ln_g", "mem_ln_b",
                 "xa_w_q", "xa_w_k", "xa_w_v", "xa_w_o", "ln3_g", "ln3_b", "ffn2_w_gate", "ffn2_w_up", "ffn2_w_down",
                 "ln4_g", "ln4_b")
_FIRST = ("ffn1_w_gate", "ffn1_w_up")
_SECOND = ("ffn1_w_down", "w_in", "w_out")
_THIRD = ("xa_w_k", "xa_w_v", "xa_w_q", "xa_w_o", "ffn2_w_gate", "ffn2_w_up", "ffn2_w_down")


def kernel(x, mem, ffn1_w_gate, ffn1_w_up, ffn1_w_down, ln1_g, ln1_b, w_in, hg_lb_logits, hg_norm_g, sg_ln_g, sg_ln_b, sg_w_s, sg_b_s, w_out, ln2_g, ln2_b, mem_ln_g, mem_ln_b, xa_w_q, xa_w_k, xa_w_v, xa_w_o, ln3_g, ln3_b, ffn2_w_gate, ffn2_w_up, ffn2_w_down, ln4_g, ln4_b, loss_target, m_ffn1_w_gate, m_ffn1_w_up, m_ffn1_w_down, m_ln1_g, m_ln1_b, m_w_in, m_hg_lb_logits, m_hg_norm_g, m_sg_ln_g, m_sg_ln_b, m_sg_w_s, m_sg_b_s, m_w_out, m_ln2_g, m_ln2_b, m_mem_ln_g, m_mem_ln_b, m_xa_w_q, m_xa_w_k, m_xa_w_v, m_xa_w_o, m_ln3_g, m_ln3_b, m_ffn2_w_gate, m_ffn2_w_up, m_ffn2_w_down, m_ln4_g, m_ln4_b, v_ffn1_w_gate, v_ffn1_w_up, v_ffn1_w_down, v_ln1_g, v_ln1_b, v_w_in, v_hg_lb_logits, v_hg_norm_g, v_sg_ln_g, v_sg_ln_b, v_sg_w_s, v_sg_b_s, v_w_out, v_ln2_g, v_ln2_b, v_mem_ln_g, v_mem_ln_b, v_xa_w_q, v_xa_w_k, v_xa_w_v, v_xa_w_o, v_ln3_g, v_ln3_b, v_ffn2_w_gate, v_ffn2_w_up, v_ffn2_w_down, v_ln4_g, v_ln4_b):
    args = dict(locals())
    w = {k: args[k] for k in _WEIGHT_NAMES}
    m = {k: args["m_" + k] for k in _WEIGHT_NAMES}
    v = {k: args["v_" + k] for k in _WEIGHT_NAMES}
    shards = {k: w[k][0] for k in _BIG_NAMES}
    shard_shapes = {k: shards[k].shape for k in _BIG_NAMES}
    small = {k: (w[k][0] if k != "hg_lb_logits" else w[k]) for k in _SMALL_NAMES}

    srcs1, shapes1, pieces1, idx1 = _gather_plan(_FIRST, shards)
    lands1 = _place_own(srcs1, shapes1, pieces1, "gather_first_own")
    rest = _SECOND + _THIRD
    prepared = {}

    def prepare_rest(started):
        later = {k: shards[k] + started[0, 0] for k in rest}
        srcs2, shapes2, pieces2, idx2 = _gather_plan(rest, later)
        lands2 = _place_own(srcs2, shapes2, pieces2, "gather_rest_own")
        xb = _to_bf16(x[0], "x_bf16", deps=(started,))
        prepared.update(srcs=srcs2, pieces=pieces2, idx=idx2, lands=lands2, xb=xb)
        return tuple(lands2) + (xb,)

    lands1, tok1 = _routed_gather(srcs1, lands1, pieces1, prepare_rest, "gather_first")
    srcs2, pieces2, idx2, lands2 = (prepared[k] for k in ("srcs", "pieces", "idx", "lands"))
    groups2 = [list(idx2[k]) for k in rest]
    sems2, srcs2, lands2, tok2 = _comm_start(srcs2, lands2, pieces2, groups2, "gather_rest_start", after=(tok1,))
    wt = dict(zip(_FIRST, lands1))
    pending = {k: gi for gi, k in enumerate(rest)}

    def get_w(name, after):
        if name in pending:
            gi = pending.pop(name)
            si = [pieces2[p][1] for p in groups2[gi]]
            sub = [(pieces2[p][0], row, 0) + pieces2[p][3:] for row, p in enumerate(groups2[gi])]
            wt[name] = _comm_wait([srcs2[s] for s in si], [lands2[gi]], sub, list(range(len(sub))), sems2[gi],
                                  after, "gather_wait_" + name)[0]
        return wt[name]

    sv = _forward(x[0], prepared["xb"], mem[0], loss_target[0], get_w, small, first_deps=(tok2,))

    sent = []

    def send(names, grads):
        srcs, shapes, pieces, idx = _scatter_plan(names, dict(zip(names, grads)), shard_shapes)
        lands = _place_own(srcs, shapes, pieces, "grads_own_%d" % len(sent))
        sems, srcs, lands, tok = _comm_start(srcs, lands, pieces, [list(range(len(pieces)))],
                                             "grads_start_%d" % len(sent))
        sent.append((names, srcs, lands, pieces, idx, sems[0]))
        return tok

    loss, grad_x, gs = _backward(sv, wt, small, send)

    ssrc = list(_pack_small_grads(gs, {k: w[k].shape for k in _SMALL_NAMES}, loss))
    sp = [("scatter", i, i, 0, 0, 0, a.shape) for i, a in enumerate(ssrc)]
    sshape = [jax.ShapeDtypeStruct((N_DEV,) + a.shape, F32) for a in ssrc]
    sl = _place_own(ssrc, sshape, sp, "small_own")
    ssem, ssrc, sl, _ = _comm_start(ssrc, sl, sp, [[0, 1]], "small_start")

    out_g, out_d, out_m, out_v = {}, {}, {}, {}
    after = (grad_x,)
    for n_sent, (names, srcs, lands, pieces, idx, sems) in enumerate(sent):
        lands = _comm_wait(srcs, lands, pieces, list(range(len(pieces))), sems, after, "grads_wait_%d" % n_sent)
        for k in names:
            axis = 1 if (k in _COL_FFN or k == "w_in") else 0
            if k in _COL_FFN:
                res = _adam_sharded([lands[i] for i in idx[k]], w[k][0].T, m[k][0].T, v[k][0].T, axis, "adam_" + k)
                res = [r.T for r in res]
            else:
                res = _adam_sharded([lands[i] for i in idx[k]], w[k][0], m[k][0], v[k][0], axis, "adam_" + k)
            out_g[k], out_d[k], out_m[k], out_v[k] = [r[None] for r in res]
        after = (out_v[names[-1]],)
    sl = _comm_wait(ssrc, sl, sp, [0, 1], ssem[0], after, "small_wait")
    small_out, loss_sum = _adam_small(sl[0], sl[1], w, m, v)
    for dst, res in zip((out_g, out_d, out_m, out_v), small_out):
        dst.update(res)
    loss_all = loss_sum[0, 0]
    return (loss_all, grad_x[None], *[out_g[k] for k in _WEIGHT_NAMES], *[out_d[k] for k in _WEIGHT_NAMES],
            *[out_m[k] for k in _WEIGHT_NAMES], *[out_v[k] for k in _WEIGHT_NAMES])
```

```python
import itertools

import jax
import jax.numpy as jnp
import numpy as np
from jax import lax
from jax.experimental import pallas as pl
from jax.experimental.pallas import tpu as pltpu

F32 = jnp.float32
BF16 = jnp.bfloat16

N_DEV = 8
ALPHA = 2.0 ** 0.25
LN_EPS = 1e-5
HG_HEADS = 4
HG_DIM = 128
SG_GROUPS = 4
SG_DIM = 128
SG_CHUNK = 128
X_HEADS = 4
HG_BLOCK = 16
HG_UNROLL = 16
ADAM_LR = 0.001
ADAM_B1 = 0.9
ADAM_B2 = 0.999
ADAM_EPS = 1e-08
ADAM_WD = 0.01
ADAM_STEP = 10
VMEM_LIMIT_V7X = 48 * 1024 * 1024
MXU_WIDTH_V7X = 256
LANES = 128
MESH_ID = pl.DeviceIdType.MESH
ANY = pl.BlockSpec(memory_space=pl.ANY)
HBM = pl.BlockSpec(memory_space=pltpu.HBM)
SEM = pl.BlockSpec(memory_space=pltpu.SEMAPHORE)
DATAFLOW = pltpu.SideEffectType.DATAFLOW_SIDE_EFFECTING


def _params(n_axes):
    return pltpu.CompilerParams(dimension_semantics=("arbitrary",) * n_axes, vmem_limit_bytes=VMEM_LIMIT_V7X)


def _dot(a, b):
    return jnp.dot(a, b, preferred_element_type=F32)


def _dot_nt(a, b):
    return lax.dot_general(a, b, (((1,), (1,)), ((), ())), preferred_element_type=F32)


def _dot_tn(a, b):
    return lax.dot_general(a, b, (((0,), (0,)), ((), ())), preferred_element_type=F32)


def _sigmoid(x):
    return 1.0 / (1.0 + jnp.exp(-x))


def _silu_and_grad(a):
    sig = _sigmoid(a)
    return a * sig, sig * (1.0 + a * (1.0 - sig))


_GELU_C = 0.7978845608028654


def _gelu_and_grad(x):
    inner = _GELU_C * (x + 0.044715 * x * x * x)
    t = jnp.tanh(inner)
    val = 0.5 * x * (1.0 + t)
    grad = 0.5 * (1.0 + t) + 0.5 * x * (1.0 - t * t) * _GELU_C * (1.0 + 3.0 * 0.044715 * x * x)
    return val, grad


def _ln_fwd(y, g, b):
    mu = jnp.mean(y, axis=-1, keepdims=True)
    yc = y - mu
    var = jnp.mean(yc * yc, axis=-1, keepdims=True)
    rstd = lax.rsqrt(var + LN_EPS)
    xhat = yc * rstd
    return xhat * g + b, xhat, rstd


def _ln_bwd(dh, xhat, rstd, g):
    dxh = dh * g
    m1 = jnp.mean(dxh, axis=-1, keepdims=True)
    m2 = jnp.mean(dxh * xhat, axis=-1, keepdims=True)
    dy = rstd * (dxh - m1 - xhat * m2)
    dg = jnp.sum(dh * xhat, axis=0, keepdims=True)
    db = jnp.sum(dh, axis=0, keepdims=True)
    return dy, dg, db


def _mask_dot(mask, x):
    hi = x.astype(BF16)
    lo = (x - hi.astype(F32)).astype(BF16)
    n = mask.shape[0]
    parts = [_dot(mask, hi[r:r + n, :]) + _dot(mask, lo[r:r + n, :]) for r in range(0, x.shape[0], n)]
    return parts[0] if len(parts) == 1 else jnp.concatenate(parts, axis=0)


def _block_masks(n):
    r = np.arange(n)[:, None]
    c = np.arange(n)[None, :]
    same = (r // HG_BLOCK) == (c // HG_BLOCK)
    return jnp.asarray(np.stack([same & (c <= r), same & (c >= r), same]), BF16)


def _row_tile(t):
    return min(t, 512)


def _col_tile(n):
    for cand in (512, 256, 128):
        if n % cand == 0:
            return cand
    return n


def _resident(w):
    return pl.BlockSpec(w.shape, lambda *_: (0, 0), pipeline_mode=pl.Buffered(1))


def _drop_deps(body, n_in, n_deps):
    if n_deps == 0:
        return body
    return lambda *refs: body(*refs[:n_in], *refs[n_in + n_deps:])


def _to_bf16(x, name, deps=()):
    t, d = x.shape
    tm = _row_tile(t)

    def body(x_ref, o_ref):
        o_ref[...] = x_ref[...].astype(BF16)

    row = pl.BlockSpec((tm, d), lambda i: (i, 0))
    return pl.pallas_call(
        _drop_deps(body, 1, len(deps)),
        grid=(t // tm,),
        in_specs=[row] + [ANY] * len(deps),
        out_specs=row,
        out_shape=jax.ShapeDtypeStruct((t, d), BF16),
        compiler_params=_params(1),
        name=name,
    )(x, *deps)


def _ffn_up(hb, wg, wu, name, deps=()):
    t, d = hb.shape
    f = wg.shape[1]
    tm = _row_tile(t)
    tn = _col_tile(f)

    def body(h_ref, wg_ref, wu_ref, a_ref, b_ref, s_ref):
        h = h_ref[...]
        for c in range(f // tn):
            cols = slice(c * tn, (c + 1) * tn)
            a = _dot(h, wg_ref[:, cols])
            b = _dot(h, wu_ref[:, cols])
            a_ref[:, cols] = a.astype(BF16)
            b_ref[:, cols] = b.astype(BF16)
            s_ref[:, cols] = (a * _sigmoid(a) * b).astype(BF16)

    act = pl.BlockSpec((tm, f), lambda i: (i, 0))
    return pl.pallas_call(
        _drop_deps(body, 3, len(deps)),
        grid=(t // tm,),
        in_specs=[pl.BlockSpec((tm, d), lambda i: (i, 0)), _resident(wg), _resident(wu)] + [ANY] * len(deps),
        out_specs=[act, act, act],
        out_shape=[jax.ShapeDtypeStruct((t, f), BF16)] * 3,
        compiler_params=_params(1),
        name=name,
    )(hb, wg, wu, *deps)


def _mm_res_ln(lhs, w, res, g, b, coef, name, target=None):
    t, kd = lhs.shape
    d = w.shape[1]
    tm = _row_tile(t)
    nt = t // tm
    from_norm = isinstance(res, tuple)
    n_res = 3 if from_norm else 1

    def body(*refs):
        l_ref, w_ref = refs[:2]
        r_refs = refs[2:2 + n_res]
        g_ref, b_ref = refs[2 + n_res:4 + n_res]
        rest = refs[4 + n_res:]
        prev = r_refs[0][...] * r_refs[1][...] + r_refs[2][...] if from_norm else r_refs[0][...]
        y = ALPHA * prev + coef * _dot(l_ref[...], w_ref[...])
        h, xhat, rstd = _ln_fwd(y, g_ref[...], b_ref[...])
        if target is None:
            hb_ref, xh_ref, rs_ref = rest
            hb_ref[...] = h.astype(BF16)
            xh_ref[...] = xhat
            rs_ref[...] = rstd
            return
        t_ref, loss_ref, dy_ref, dyb_ref, dg_ref, db_ref, lacc = rest
        i = pl.program_id(0)

        @pl.when(i == 0)
        def _():
            lacc[...] = jnp.zeros_like(lacc)
            dg_ref[...] = jnp.zeros_like(dg_ref)
            db_ref[...] = jnp.zeros_like(db_ref)

        err = h - t_ref[...]
        lacc[...] += jnp.sum(err * err, axis=0, keepdims=True)
        dy, dg, db = _ln_bwd(err * (1.0 / d), xhat, rstd, g_ref[...])
        dy_ref[...] = dy
        dyb_ref[...] = dy.astype(BF16)
        dg_ref[...] += dg
        db_ref[...] += db

        @pl.when(i == nt - 1)
        def _():
            loss_ref[...] = jnp.zeros_like(loss_ref) + jnp.sum(lacc[...], axis=1, keepdims=True) * (0.5 / d)

    row = pl.BlockSpec((tm, d), lambda i: (i, 0))
    vec = pl.BlockSpec((1, d), lambda i: (0, 0))
    res_specs = [row, vec, vec] if from_norm else [row]
    res_args = list(res) if from_norm else [res]
    in_specs = [pl.BlockSpec((tm, kd), lambda i: (i, 0)), _resident(w)] + res_specs + [vec, vec]
    args = [lhs, w] + res_args + [g, b]
    if target is None:
        out_specs = [row, row, pl.BlockSpec((tm, 1), lambda i: (i, 0))]
        out_shape = [jax.ShapeDtypeStruct((t, d), BF16), jax.ShapeDtypeStruct((t, d), F32),
                     jax.ShapeDtypeStruct((t, 1), F32)]
        scratch = []
    else:
        in_specs.append(row)
        args.append(target)
        out_specs = [pl.BlockSpec((1, LANES), lambda i: (0, 0)), row, row, vec, vec]
        out_shape = [jax.ShapeDtypeStruct((1, LANES), F32), jax.ShapeDtypeStruct((t, d), F32),
                     jax.ShapeDtypeStruct((t, d), BF16), jax.ShapeDtypeStruct((1, d), F32),
                     jax.ShapeDtypeStruct((1, d), F32)]
        scratch = [pltpu.VMEM((1, d), F32)]
    return pl.pallas_call(
        body,
        grid=(nt,),
        in_specs=in_specs,
        out_specs=out_specs,
        out_shape=out_shape,
        scratch_shapes=scratch,
        compiler_params=_params(1),
        name=name,
    )(*args)


def _store_slabs(o_ref, first, tile):
    for s in range(tile.shape[1] // LANES):
        o_ref[first + s] = tile[:, s * LANES:(s + 1) * LANES]


def _mm_nn(lhs, w, name):
    t, kd = lhs.shape
    n = w.shape[1]
    tm = _row_tile(t)
    tn = _col_tile(n)

    def body(l_ref, w_ref, o_ref):
        lhs_v = l_ref[...]
        for c in range(n // tn):
            _store_slabs(o_ref, c * (tn // LANES), _dot(lhs_v, w_ref[:, c * tn:(c + 1) * tn]))

    return pl.pallas_call(
        body,
        grid=(t // tm,),
        in_specs=[pl.BlockSpec((tm, kd), lambda i: (i, 0)), _resident(w)],
        out_specs=pl.BlockSpec((n // LANES, tm, LANES), lambda i: (0, i, 0)),
        out_shape=jax.ShapeDtypeStruct((n // LANES, t, LANES), F32),
        compiler_params=_params(1),
        name=name,
    )(lhs, w)


def _lower_bound(lg):
    m = jnp.max(lg, axis=0, keepdims=True)
    e = jnp.exp(lg - m)
    return e[0:1, :] / jnp.sum(e, axis=0, keepdims=True)


def _forget_terms(fz, lb):
    e = jnp.exp(-jnp.abs(fz))
    r = 1.0 / (1.0 + e)
    pos = fz >= 0.0
    sig = jnp.where(pos, r, e * r)
    nsig = jnp.where(pos, e * r, r)
    f = lb + (1.0 - lb) * sig
    k = (1.0 - lb) * nsig
    return sig, nsig, f, k


def _hg_tile(t):
    return min(t, 1024)


HG_HALF = HG_BLOCK // 2
NEG_BIG = -1e30


def _halves(a):
    return a[:HG_HALF, :], a[HG_HALF:, :]


def _causal_halves(s):
    return (0, 1) if s < HG_HALF else (1,)


def _decay_from(b_half, b_s, s, h, tidx):
    first = s - h * HG_HALF
    diff = b_half - b_s
    if first > 0:
        diff = jnp.where(tidx >= first, diff, NEG_BIG)
    return jnp.exp(diff)


def _hgrn_fwd(proj, logits, gn):
    t = proj.shape[1]
    ct = _hg_tile(t)
    nct = t // ct
    nblk = ct // HG_BLOCK
    nh = HG_HEADS
    mrows = min(ct, 256)

    def body(q_ref, fz_ref, iv_ref, gg_ref, lg_ref, gn_ref, mask_ref, oraw_ref, oa_ref, st_ref,
             state, qt_s, kt_s, k_s, b_s, dec_s):
        c = pl.program_id(1)

        @pl.when(c == 0)
        def _():
            state[...] = jnp.zeros_like(state)

        lb = _lower_bound(lg_ref[...])
        q = q_ref[...]
        _, _, f, k = _forget_terms(fz_ref[...], lb)
        logf = jnp.log(f)
        b = _mask_dot(mask_ref[0], logf)
        bend = _mask_dot(mask_ref[2], logf)
        qt_s[...] = (q * jnp.exp(b)).astype(BF16)
        kt_s[...] = (k * jnp.exp(bend - b)).astype(BF16)
        k_s[...] = k
        b_s[...] = b
        dec_s[...] = jnp.exp(bend)
        tidx = lax.broadcasted_iota(jnp.int32, (HG_HALF, HG_DIM), 0)

        def blk(i, carry):
            r0 = pl.multiple_of(i * HG_BLOCK, HG_BLOCK)
            rows = pl.ds(r0, HG_BLOCK)
            st = state[...]
            stb = st.astype(BF16)
            st_ref[i] = stb
            v = iv_ref[rows, :]
            qq = q_ref[rows, :]
            kk = k_s[rows, :]
            bb = b_s[rows, :]
            o = list(_halves(_dot_nt(qt_s[rows, :], stb)))
            qh, bh = _halves(qq), _halves(bb)
            for s in range(HG_BLOCK):
                ks, vs = kk[s:s + 1, :], v[s:s + 1, :]
                for h in _causal_halves(s):
                    e = _decay_from(bh[h], bb[s:s + 1, :], s, h, tidx)
                    acol = jnp.sum(qh[h] * (ks * e), axis=1, keepdims=True)
                    o[h] = o[h] + acol * vs
            oraw_ref[rows, :] = jnp.concatenate(o, axis=0)
            state[...] = st * dec_s[pl.ds(r0, 1), :] + _dot_tn(v.astype(BF16), kt_s[rows, :])
            return carry

        lax.fori_loop(0, nblk, blk, 0, unroll=HG_UNROLL)
        oraw = oraw_ref[...]
        r = lax.rsqrt(jnp.mean(oraw * oraw, axis=-1, keepdims=True) + LN_EPS)
        gg = gg_ref[...]
        oa_ref[...] = (oraw * r * gn_ref[...] * gg * _sigmoid(gg)).astype(BF16)

    def slab(off):
        return pl.BlockSpec((None, ct, HG_DIM), lambda h, c: (off + h, c, 0))

    return pl.pallas_call(
        body,
        grid=(nh, nct),
        in_specs=[slab(0), slab(nh), slab(2 * nh), slab(3 * nh),
                  pl.BlockSpec((None, 2, HG_DIM), lambda h, c: (h, 0, 0)),
                  pl.BlockSpec((1, HG_DIM), lambda h, c: (0, 0)),
                  pl.BlockSpec((3, mrows, mrows), lambda h, c: (0, 0, 0))],
        out_specs=[slab(0), pl.BlockSpec((ct, HG_DIM), lambda h, c: (c, h)),
                   pl.BlockSpec((None, nblk, HG_DIM, HG_DIM), lambda h, c: (h, c, 0, 0))],
        out_shape=[jax.ShapeDtypeStruct((nh, t, HG_DIM), F32),
                   jax.ShapeDtypeStruct((t, (nh + SG_GROUPS) * HG_DIM), BF16),
                   jax.ShapeDtypeStruct((nh, t // HG_BLOCK, HG_DIM, HG_DIM), BF16)],
        scratch_shapes=[pltpu.VMEM((HG_DIM, HG_DIM), F32), pltpu.VMEM((ct, HG_DIM), BF16),
                        pltpu.VMEM((ct, HG_DIM), BF16), pltpu.VMEM((ct, HG_DIM), F32),
                        pltpu.VMEM((ct, HG_DIM), F32), pltpu.VMEM((ct, HG_DIM), F32)],
        compiler_params=_params(2),
        name="hgrn_fwd",
    )(proj, proj, proj, proj, logits, gn, _block_masks(mrows))


def _sg_tile(t):
    return min(t, 512)


def _sgu_chunk_fwd(u, v, ln_g, ln_b, wm, bs):
    ua, dua = _gelu_and_grad(u)
    va, dva = _gelu_and_grad(v)
    vn, xhat, rstd = _ln_fwd(va, ln_g, ln_b)
    s = _dot(wm, vn.astype(BF16)) + bs
    return ua, dua, dva, vn, xhat, rstd, s


def _tril_weight(w):
    n = SG_CHUNK
    r = lax.broadcasted_iota(jnp.int32, (n, n), 0)
    c = lax.broadcasted_iota(jnp.int32, (n, n), 1)
    return jnp.where(c <= r, w, 0.0)


def _sgu_fwd(proj, mix, ln_g, ln_b, w_s, b_col):
    t = proj.shape[1]
    ct = _sg_tile(t)
    ng = SG_GROUPS
    wide = ng * SG_DIM
    blk_u = 4 * HG_HEADS // ng

    def body(u_ref, v_ref, g_ref, b_ref, w_ref, bs_ref, mix_ref, o_ref):
        del mix_ref
        for g in range(ng):
            lanes = slice(g * SG_DIM, (g + 1) * SG_DIM)
            wm = _tril_weight(w_ref[g]).astype(BF16)
            for n in range(ct // SG_CHUNK):
                rows = slice(n * SG_CHUNK, (n + 1) * SG_CHUNK)
                ua, _, _, _, _, _, s = _sgu_chunk_fwd(u_ref[g, rows, :], v_ref[g, rows, :], g_ref[g], b_ref[g], wm,
                                                      bs_ref[g])
                o_ref[rows, lanes] = (ua * s).astype(BF16)

    full = lambda a: pl.BlockSpec(a.shape, lambda c: (0,) * a.ndim)
    return pl.pallas_call(
        body,
        grid=(t // ct,),
        in_specs=[pl.BlockSpec((ng, ct, SG_DIM), lambda c: (blk_u, c, 0)),
                  pl.BlockSpec((ng, ct, SG_DIM), lambda c: (blk_u + 1, c, 0)),
                  full(ln_g), full(ln_b), full(w_s), full(b_col), ANY],
        out_specs=pl.BlockSpec((ct, wide), lambda c: (c, 1)),
        out_shape=jax.ShapeDtypeStruct(mix.shape, mix.dtype),
        input_output_aliases={6: 0},
        compiler_params=_params(1),
        name="sgu_fwd",
    )(proj, proj, ln_g, ln_b, w_s, b_col, mix)


def _mem_kv(mem, g, b, wk, wv):
    m_len, d = mem.shape

    def body(m_ref, g_ref, b_ref, wk_ref, wv_ref, mb_ref, xh_ref, rs_ref, k_ref, v_ref):
        m, xhat, rstd = _ln_fwd(m_ref[...], g_ref[...], b_ref[...])
        mb = m.astype(BF16)
        mb_ref[...] = mb
        xh_ref[...] = xhat
        rs_ref[...] = rstd
        k_ref[...] = _dot(mb, wk_ref[...]).astype(BF16)
        v_ref[...] = _dot(mb, wv_ref[...]).astype(BF16)

    return pl.pallas_call(
        body,
        out_shape=[jax.ShapeDtypeStruct((m_len, d), BF16), jax.ShapeDtypeStruct((m_len, d), F32),
                   jax.ShapeDtypeStruct((m_len, 1), F32), jax.ShapeDtypeStruct((m_len, d), BF16),
                   jax.ShapeDtypeStruct((m_len, d), BF16)],
        compiler_params=pltpu.CompilerParams(vmem_limit_bytes=VMEM_LIMIT_V7X),
        name="mem_kv",
    )(mem, g, b, wk, wv)


def _softmax_rows(s):
    m = jnp.max(s, axis=-1, keepdims=True)
    p = jnp.exp(s - m)
    return p / jnp.sum(p, axis=-1, keepdims=True)


def _attn_fwd(hb, wq, kb, vb):
    t, d = hb.shape
    tm = _row_tile(t)
    dh = d // X_HEADS
    scale = dh ** -0.5

    def body(h_ref, wq_ref, k_ref, v_ref, q_ref, o_ref):
        q = _dot(h_ref[...], wq_ref[...]).astype(BF16)
        q_ref[...] = q
        for hd in range(X_HEADS):
            sl = slice(hd * dh, (hd + 1) * dh)
            p = _softmax_rows(_dot_nt(q[:, sl], k_ref[:, sl]) * scale)
            o_ref[:, sl] = _dot(p.astype(BF16), v_ref[:, sl]).astype(BF16)

    row = pl.BlockSpec((tm, d), lambda i: (i, 0))
    full = lambda a: pl.BlockSpec(a.shape, lambda i: (0, 0))
    return pl.pallas_call(
        body,
        grid=(t // tm,),
        in_specs=[row, full(wq), full(kb), full(vb)],
        out_specs=[row, row],
        out_shape=[jax.ShapeDtypeStruct((t, d), BF16), jax.ShapeDtypeStruct((t, d), BF16)],
        compiler_params=_params(1),
        name="attn_fwd",
    )(hb, wq, kb, vb)


def _ffn_bwd_act(dyb, wd, a, b, coef, name, deps=()):
    t, d = dyb.shape
    f = wd.shape[0]
    tm = _row_tile(t)
    tn = _col_tile(f)

    def body(dy_ref, wd_ref, a_ref, b_ref, da_ref, db_ref):
        dy = dy_ref[...]
        for c in range(f // tn):
            cols = slice(c * tn, (c + 1) * tn)
            ds = _dot_nt(dy, wd_ref[cols, :]) * coef
            silu, dsilu = _silu_and_grad(a_ref[:, cols].astype(F32))
            da_ref[:, cols] = (ds * b_ref[:, cols].astype(F32) * dsilu).astype(BF16)
            db_ref[:, cols] = (ds * silu).astype(BF16)

    act = pl.BlockSpec((tm, f), lambda i: (i, 0))
    return pl.pallas_call(
        _drop_deps(body, 4, len(deps)),
        grid=(t // tm,),
        in_specs=[pl.BlockSpec((tm, d), lambda i: (i, 0)), _resident(wd), act, act] + [ANY] * len(deps),
        out_specs=[act, act],
        out_shape=[jax.ShapeDtypeStruct((t, f), BF16), jax.ShapeDtypeStruct((t, f), BF16)],
        compiler_params=_params(1),
        name=name,
    )(dyb, wd, a, b, *deps)


def _ffn_bwd_fused(dy, dyb, wd, wg, wu, a, b, coef, ln, name):
    t, d = dy.shape
    f = wd.shape[0]
    tm = min(t, 256)
    tn = _col_tile(f)

    def body(dy_ref, dyb_ref, wd_ref, wg_ref, wu_ref, a_ref, b_ref, xh_ref, rs_ref, g_ref,
             da_ref, db_ref, dyo_ref, dyob_ref, dg_ref, dbl_ref):
        dyb_v = dyb_ref[...]
        dh = ALPHA * dy_ref[...]
        for c in range(f // tn):
            cols = slice(c * tn, (c + 1) * tn)
            ds = _dot_nt(dyb_v, wd_ref[cols, :]) * coef
            silu, dsilu = _silu_and_grad(a_ref[:, cols].astype(F32))
            da = (ds * b_ref[:, cols].astype(F32) * dsilu).astype(BF16)
            db = (ds * silu).astype(BF16)
            da_ref[:, cols] = da
            db_ref[:, cols] = db
            dh = dh + _dot_nt(da, wg_ref[:, cols]) + _dot_nt(db, wu_ref[:, cols])

        @pl.when(pl.program_id(0) == 0)
        def _():
            dg_ref[...] = jnp.zeros_like(dg_ref)
            dbl_ref[...] = jnp.zeros_like(dbl_ref)

        dyp, dg, dbl = _ln_bwd(dh, xh_ref[...], rs_ref[...], g_ref[...])
        dyo_ref[...] = dyp
        dyob_ref[...] = dyp.astype(BF16)
        dg_ref[...] += dg
        dbl_ref[...] += dbl

    row = pl.BlockSpec((tm, d), lambda i: (i, 0))
    act = pl.BlockSpec((tm, f), lambda i: (i, 0))
    vec = pl.BlockSpec((1, d), lambda i: (0, 0))
    return pl.pallas_call(
        body,
        grid=(t // tm,),
        in_specs=[row, row, _resident(wd), _resident(wg), _resident(wu), act, act, row,
                  pl.BlockSpec((tm, 1), lambda i: (i, 0)), vec],
        out_specs=[act, act, row, row, vec, vec],
        out_shape=[jax.ShapeDtypeStruct((t, f), BF16), jax.ShapeDtypeStruct((t, f), BF16),
                   jax.ShapeDtypeStruct((t, d), F32), jax.ShapeDtypeStruct((t, d), BF16),
                   jax.ShapeDtypeStruct((1, d), F32), jax.ShapeDtypeStruct((1, d), F32)],
        compiler_params=_params(1),
        name=name,
    )(dy, dyb, wd, wg, wu, a, b, *ln)


def _mm_tn(a, b, name, scale=1.0, deps=()):
    t, m = a.shape
    bs = list(b) if isinstance(b, (list, tuple)) else [b]
    n = sum(piece.shape[1] for piece in bs)
    tt = _row_tile(t)
    nt = t // tt
    tm_o, tn_o = m, n

    def body(a_ref, *refs):
        b_refs, (o_ref, acc) = refs[:len(bs)], refs[len(bs):]
        k = pl.program_id(2)

        @pl.when(k == 0)
        def _():
            acc[...] = jnp.zeros_like(acc)

        first = 0
        for b_ref in b_refs:
            cols = slice(first, first + b_ref.shape[1])
            acc[:, cols] += _dot_tn(a_ref[...], b_ref[...])
            first = cols.stop

        @pl.when(k == nt - 1)
        def _():
            o_ref[...] = (acc[...] * scale).astype(BF16)

    return pl.pallas_call(
        _drop_deps(body, 1 + len(bs), len(deps)),
        grid=(m // tm_o, n // tn_o, nt),
        in_specs=[pl.BlockSpec((tt, tm_o), lambda i, j, k: (k, i))]
        + [pl.BlockSpec((tt, piece.shape[1]), lambda i, j, k: (k, j)) for piece in bs] + [ANY] * len(deps),
        out_specs=pl.BlockSpec((tm_o, tn_o), lambda i, j, k: (i, j)),
        out_shape=pltpu.HBM((m, n), BF16),
        scratch_shapes=[pltpu.VMEM((tm_o, tn_o), F32)],
        compiler_params=_params(3),
        name=name,
    )(a, *bs, *deps)


def _mm_nt(lhs, w, name):
    t, d = lhs.shape
    kd = w.shape[0]
    tm = _row_tile(t)

    def body(l_ref, w_ref, o_ref):
        _store_slabs(o_ref, 0, _dot_nt(l_ref[...], w_ref[...]))

    return pl.pallas_call(
        body,
        grid=(t // tm,),
        in_specs=[pl.BlockSpec((tm, d), lambda i: (i, 0)), _resident(w)],
        out_specs=pl.BlockSpec((kd // LANES, tm, LANES), lambda i: (0, i, 0)),
        out_shape=jax.ShapeDtypeStruct((kd // LANES, t, LANES), F32),
        compiler_params=_params(1),
        name=name,
    )(lhs, w)


def _dx_ln(dy, pairs, ln, name, deps=()):
    t, d = dy.shape
    npair = len(pairs)
    pairs = [(list(lhs) if isinstance(lhs, (list, tuple)) else [lhs], w) for lhs, w in pairs]
    tm = min(t, 512 // npair)
    nt = t // tm
    n_in = 1 + sum(len(pieces) + 1 for pieces, _ in pairs) + (3 if ln is not None else 0)

    def body(*refs):
        dy_ref = refs[0]
        pos = 1
        dh = ALPHA * dy_ref[...]
        for pieces, _ in pairs:
            w_ref = refs[pos + len(pieces)]
            first = 0
            for l_ref in refs[pos:pos + len(pieces)]:
                cols = slice(first, first + l_ref.shape[1])
                dh = dh + _dot_nt(l_ref[...], w_ref[:, cols])
                first = cols.stop
            pos += len(pieces) + 1
        if ln is not None:
            xh_ref, rs_ref, g_ref = refs[pos:pos + 3]
            dyo_ref, dyb_ref, dg_ref, db_ref = refs[pos + 3:pos + 7]

            @pl.when(pl.program_id(0) == 0)
            def _():
                dg_ref[...] = jnp.zeros_like(dg_ref)
                db_ref[...] = jnp.zeros_like(db_ref)

            dyp, dg, db = _ln_bwd(dh, xh_ref[...], rs_ref[...], g_ref[...])
            dyo_ref[...] = dyp
            dyb_ref[...] = dyp.astype(BF16)
            dg_ref[...] += dg
            db_ref[...] += db
        else:
            refs[pos][...] = dh

    row = pl.BlockSpec((tm, d), lambda i: (i, 0))
    vec = pl.BlockSpec((1, d), lambda i: (0, 0))
    in_specs = [row]
    args = [dy]
    for pieces, w in pairs:
        in_specs += [pl.BlockSpec((tm, piece.shape[1]), lambda i: (i, 0)) for piece in pieces] + [_resident(w)]
        args += pieces + [w]
    if ln is not None:
        in_specs += [row, pl.BlockSpec((tm, 1), lambda i: (i, 0)), vec]
        args += list(ln)
        out_specs = [row, row, vec, vec]
        out_shape = [jax.ShapeDtypeStruct((t, d), F32), jax.ShapeDtypeStruct((t, d), BF16),
                     jax.ShapeDtypeStruct((1, d), F32), jax.ShapeDtypeStruct((1, d), F32)]
    else:
        out_specs = row
        out_shape = jax.ShapeDtypeStruct((t, d), F32)
    return pl.pallas_call(
        _drop_deps(body, n_in, len(deps)),
        grid=(nt,),
        in_specs=in_specs + [ANY] * len(deps),
        out_specs=out_specs,
        out_shape=out_shape,
        compiler_params=_params(1),
        name=name,
    )(*args, *deps)


def _hgrn_bwd(proj, oraw, dmix, states, logits, gn):
    t = proj.shape[1]
    ct = _hg_tile(t)
    nct = t // ct
    nblk = ct // HG_BLOCK
    nh = HG_HEADS
    mrows = min(ct, 256)

    def body(q_ref, fz_ref, iv_ref, gg_ref, or_ref, do_ref, st_ref, lg_ref, gn_ref, mask_ref,
             dq_ref, dfz_ref, div_ref, dgg_ref, dlg_ref, dgn_ref,
             dstate, qt_s, kt_s, k_s, b_s, eb_s, ekb_s, dec_s, dor_s, dbl_s, gr_s, dk_s, dlb_acc):
        c = pl.program_id(1)

        @pl.when(c == 0)
        def _():
            dstate[...] = jnp.zeros_like(dstate)
            dlb_acc[...] = jnp.zeros_like(dlb_acc)
            dgn_ref[...] = jnp.zeros_like(dgn_ref)

        lb = _lower_bound(lg_ref[...])
        q = q_ref[...]
        sig, nsig, f, k = _forget_terms(fz_ref[...], lb)
        logf = jnp.log(f)
        b = _mask_dot(mask_ref[0], logf)
        bend = _mask_dot(mask_ref[2], logf)
        eb = jnp.exp(b)
        ekb = jnp.exp(bend - b)
        qt_s[...] = (q * eb).astype(BF16)
        kt_s[...] = (k * ekb).astype(BF16)
        k_s[...] = k
        b_s[...] = b
        eb_s[...] = eb
        ekb_s[...] = ekb
        dec_s[...] = jnp.exp(bend)
        oraw = or_ref[...]
        r = lax.rsqrt(jnp.mean(oraw * oraw, axis=-1, keepdims=True) + LN_EPS)
        on = oraw * r
        gg = gg_ref[...]
        silu, dsilu = _silu_and_grad(gg)
        doa = do_ref[...]
        gnv = gn_ref[...]
        dgg_ref[...] = (doa * on * gnv * dsilu).astype(BF16)
        dyn = doa * silu
        dgn_ref[...] += jnp.sum(dyn * on, axis=0, keepdims=True)
        don = dyn * gnv
        dor_s[...] = r * (don - on * jnp.mean(don * on, axis=-1, keepdims=True))
        tidx = lax.broadcasted_iota(jnp.int32, (HG_HALF, HG_DIM), 0)

        def blk(ii, carry):
            i = nblk - 1 - ii
            r0 = pl.multiple_of(i * HG_BLOCK, HG_BLOCK)
            rows = pl.ds(r0, HG_BLOCK)
            st = st_ref[i]
            dst = dstate[...]
            dstb = dst.astype(BF16)
            do = dor_s[rows, :]
            dob = do.astype(BF16)
            v = iv_ref[rows, :]
            vb = v.astype(BF16)
            qq = q_ref[rows, :]
            kk = k_s[rows, :]
            bb = b_s[rows, :]
            qt = qt_s[rows, :]
            kt = kt_s[rows, :]
            dec = dec_s[pl.ds(r0, 1), :]
            dkt = _dot(vb, dstb)
            dq = _dot(dob, st) * eb_s[rows, :]
            dk = dkt * ekb_s[rows, :]
            dv = _dot_nt(kt, dstb)
            gend = (jnp.sum(kk * dk, axis=0, keepdims=True)
                    + dec * jnp.sum(dst * st.astype(F32), axis=0, keepdims=True))
            qh, bh, doh = _halves(qq), _halves(bb), _halves(do)
            dqh, dkh, dvh = list(_halves(dq)), list(_halves(dk)), list(_halves(dv))
            for s in range(HG_BLOCK):
                ks, vs = kk[s:s + 1, :], v[s:s + 1, :]
                dk_part = dv_part = None
                for h in _causal_halves(s):
                    e = _decay_from(bh[h], bb[s:s + 1, :], s, h, tidx)
                    ke = ks * e
                    acol = jnp.sum(qh[h] * ke, axis=1, keepdims=True)
                    dacol = jnp.sum(doh[h] * vs, axis=1, keepdims=True)
                    dqh[h] = dqh[h] + dacol * ke
                    pk = dacol * (qh[h] * e)
                    pv = acol * doh[h]
                    dk_part = pk if dk_part is None else dk_part + pk
                    dv_part = pv if dv_part is None else dv_part + pv
                hs, row = divmod(s, HG_HALF)
                dkh[hs] = dkh[hs] + jnp.where(tidx == row, jnp.sum(dk_part, axis=0, keepdims=True), 0.0)
                dvh[hs] = dvh[hs] + jnp.where(tidx == row, jnp.sum(dv_part, axis=0, keepdims=True), 0.0)
            dq = jnp.concatenate(dqh, axis=0)
            dk = jnp.concatenate(dkh, axis=0)
            dv = jnp.concatenate(dvh, axis=0)
            dq_ref[rows, :] = dq.astype(BF16)
            div_ref[rows, :] = dv.astype(BF16)
            dk_s[rows, :] = dk
            dbl_s[rows, :] = qq * dq - kk * dk
            gr_s[rows, :] = jnp.zeros((HG_BLOCK, HG_DIM), F32) + gend
            dstate[...] = dst * dec + _dot_tn(dob, qt)
            return carry

        lax.fori_loop(0, nblk, blk, 0, unroll=HG_UNROLL)
        dlogf = _mask_dot(mask_ref[1], dbl_s[...]) + gr_s[...]
        dk = dk_s[...]
        dfz_ref[...] = ((dlogf / f - dk) * ((1.0 - lb) * sig * nsig)).astype(BF16)
        dlb_acc[...] += jnp.sum((dlogf / f - dk) * nsig, axis=0, keepdims=True)

        @pl.when(c == nct - 1)
        def _():
            dl0 = dlb_acc[...] * lb * (1.0 - lb)
            layer = lax.broadcasted_iota(jnp.int32, (2, HG_DIM), 0)
            dlg_ref[...] = jnp.where(layer == 0, dl0, -dl0)

    def slab(off):
        return pl.BlockSpec((None, ct, HG_DIM), lambda h, c: (off + h, nct - 1 - c, 0))

    out_slab = pl.BlockSpec((ct, HG_DIM), lambda h, c: (nct - 1 - c, h))
    tile_f32 = pltpu.VMEM((ct, HG_DIM), F32)
    tile_b16 = pltpu.VMEM((ct, HG_DIM), BF16)
    slab_shape = jax.ShapeDtypeStruct((t, nh * HG_DIM), BF16)
    return pl.pallas_call(
        body,
        grid=(nh, nct),
        in_specs=[slab(0), slab(nh), slab(2 * nh), slab(3 * nh), slab(0), slab(0),
                  pl.BlockSpec((None, nblk, HG_DIM, HG_DIM), lambda h, c: (h, nct - 1 - c, 0, 0)),
                  pl.BlockSpec((None, 2, HG_DIM), lambda h, c: (h, 0, 0)),
                  pl.BlockSpec((1, HG_DIM), lambda h, c: (0, 0)),
                  pl.BlockSpec((3, mrows, mrows), lambda h, c: (0, 0, 0))],
        out_specs=[out_slab, out_slab, out_slab, out_slab,
                   pl.BlockSpec((None, 2, HG_DIM), lambda h, c: (h, 0, 0)),
                   pl.BlockSpec((None, 1, HG_DIM), lambda h, c: (h, 0, 0))],
        out_shape=[slab_shape, slab_shape, slab_shape, slab_shape,
                   jax.ShapeDtypeStruct((nh, 2, HG_DIM), F32), jax.ShapeDtypeStruct((nh, 1, HG_DIM), F32)],
        scratch_shapes=[pltpu.VMEM((HG_DIM, HG_DIM), F32), tile_b16, tile_b16, tile_f32, tile_f32, tile_f32, tile_f32,
                        tile_f32, tile_f32, tile_f32, tile_f32, tile_f32, pltpu.VMEM((1, HG_DIM), F32)],
        compiler_params=_params(2),
        name="hgrn_bwd",
    )(proj, proj, proj, proj, oraw, dmix, states, logits, gn, _block_masks(mrows))


def _sgu_bwd(proj, dmix, ln_g, ln_b, w_s, w_t, b_col):
    t = proj.shape[1]
    ct = _sg_tile(t)
    nct = t // ct
    ng = SG_GROUPS
    off_u = 4 * HG_HEADS
    off_v = off_u + ng
    n = SG_CHUNK

    def body(u_ref, v_ref, do_ref, g_ref, b_ref, w_ref, wt_ref, bs_ref, du_ref, dv_ref, dg_ref, db_ref, dw_ref, dbs_ref):
        c = pl.program_id(1)

        @pl.when(c == 0)
        def _():
            dg_ref[...] = jnp.zeros_like(dg_ref)
            db_ref[...] = jnp.zeros_like(db_ref)
            dw_ref[...] = jnp.zeros_like(dw_ref)
            dbs_ref[...] = jnp.zeros_like(dbs_ref)

        r = lax.broadcasted_iota(jnp.int32, (n, n), 0)
        cc = lax.broadcasted_iota(jnp.int32, (n, n), 1)
        wm = jnp.where(cc <= r, w_ref[...], 0.0).astype(BF16)
        wmt = jnp.where(r <= cc, wt_ref[...], 0.0).astype(BF16)
        for ci in range(ct // n):
            rows = slice(ci * n, (ci + 1) * n)
            ua, dua, dva, vn, xhat, rstd, s = _sgu_chunk_fwd(u_ref[rows, :], v_ref[rows, :], g_ref[...], b_ref[...],
                                                             wm, bs_ref[...])
            do = do_ref[rows, :]
            du_ref[rows, :] = (do * s * dua).astype(BF16)
            ds = do * ua
            dsb = ds.astype(BF16)
            dbs_ref[...] += jnp.sum(ds, axis=1, keepdims=True)
            dw_ref[...] += _dot_nt(dsb, vn.astype(BF16))
            dvn = _dot(wmt, dsb)
            dva_in, dg, db = _ln_bwd(dvn, xhat, rstd, g_ref[...])
            dg_ref[...] += dg
            db_ref[...] += db
            dv_ref[rows, :] = (dva_in * dva).astype(BF16)

        @pl.when(c == nct - 1)
        def _():
            dw_ref[...] = jnp.where(cc <= r, dw_ref[...], 0.0)

    vec = pl.BlockSpec((None, 1, SG_DIM), lambda g, c: (g, 0, 0))
    mat = pl.BlockSpec((None, n, n), lambda g, c: (g, 0, 0))
    col = pl.BlockSpec((None, n, 1), lambda g, c: (g, 0, 0))
    out_slab = pl.BlockSpec((ct, SG_DIM), lambda g, c: (c, g))
    return pl.pallas_call(
        body,
        grid=(ng, nct),
        in_specs=[pl.BlockSpec((None, ct, SG_DIM), lambda g, c: (off_u + g, c, 0)),
                  pl.BlockSpec((None, ct, SG_DIM), lambda g, c: (off_v + g, c, 0)),
                  pl.BlockSpec((None, ct, SG_DIM), lambda g, c: (ng + g, c, 0)), vec, vec, mat, mat, col],
        out_specs=[out_slab, out_slab, vec, vec, mat, col],
        out_shape=[jax.ShapeDtypeStruct((t, ng * SG_DIM), BF16), jax.ShapeDtypeStruct((t, ng * SG_DIM), BF16),
                   jax.ShapeDtypeStruct((ng, 1, SG_DIM), F32), jax.ShapeDtypeStruct((ng, 1, SG_DIM), F32),
                   jax.ShapeDtypeStruct((ng, n, n), F32), jax.ShapeDtypeStruct((ng, n, 1), F32)],
        compiler_params=_params(2),
        name="sgu_bwd",
    )(proj, proj, dmix, ln_g, ln_b, w_s, w_t, b_col)


def _attn_bwd(dyb, wo, qb, kb, vb):
    t, d = dyb.shape
    m_len = kb.shape[0]
    tm = _row_tile(t)
    dh = d // X_HEADS
    scale = dh ** -0.5

    def body(dy_ref, wo_ref, q_ref, k_ref, v_ref, dq_ref, dk_ref, dv_ref):
        i = pl.program_id(0)

        @pl.when(i == 0)
        def _():
            dk_ref[...] = jnp.zeros_like(dk_ref)
            dv_ref[...] = jnp.zeros_like(dv_ref)

        do = _dot_nt(dy_ref[...], wo_ref[...]).astype(BF16)
        for hd in range(X_HEADS):
            sl = slice(hd * dh, (hd + 1) * dh)
            qh = q_ref[:, sl]
            p = _softmax_rows(_dot_nt(qh, k_ref[:, sl]) * scale)
            doh = do[:, sl]
            dp = _dot_nt(doh, v_ref[:, sl])
            ds = (p * (dp - jnp.sum(dp * p, axis=-1, keepdims=True)) * scale).astype(BF16)
            dq_ref[:, sl] = _dot(ds, k_ref[:, sl]).astype(BF16)
            dk_ref[:, sl] += _dot_tn(ds, qh)
            dv_ref[:, sl] += _dot_tn(p.astype(BF16), doh)

    row = pl.BlockSpec((tm, d), lambda i: (i, 0))
    full = lambda a: pl.BlockSpec(a.shape, lambda i: (0, 0))
    kv = pl.BlockSpec((m_len, d), lambda i: (0, 0))
    return pl.pallas_call(
        body,
        grid=(t // tm,),
        in_specs=[row, full(wo), row, full(kb), full(vb)],
        out_specs=[row, kv, kv],
        out_shape=[jax.ShapeDtypeStruct((t, d), BF16), jax.ShapeDtypeStruct((m_len, d), F32),
                   jax.ShapeDtypeStruct((m_len, d), F32)],
        compiler_params=_params(1),
        name="attn_bwd",
    )(dyb, wo, qb, kb, vb)


def _mem_bwd(dk, dv, mb, xhat, rstd, g, wk, wv):
    m_len, d = dk.shape

    def body(dk_ref, dv_ref, mb_ref, xh_ref, rs_ref, g_ref, wk_ref, wv_ref, gwk_ref, gwv_ref, dg_ref, db_ref):
        dkb = dk_ref[...].astype(BF16)
        dvb = dv_ref[...].astype(BF16)
        mb_v = mb_ref[...]
        gwk_ref[...] = _dot_tn(mb_v, dkb).astype(BF16)
        gwv_ref[...] = _dot_tn(mb_v, dvb).astype(BF16)
        dm = _dot_nt(dkb, wk_ref[...]) + _dot_nt(dvb, wv_ref[...])
        _, dg, db = _ln_bwd(dm, xh_ref[...], rs_ref[...], g_ref[...])
        dg_ref[...] = dg
        db_ref[...] = db

    return pl.pallas_call(
        body,
        out_shape=[jax.ShapeDtypeStruct((d, d), BF16), jax.ShapeDtypeStruct((d, d), BF16),
                   jax.ShapeDtypeStruct((1, d), F32), jax.ShapeDtypeStruct((1, d), F32)],
        compiler_params=pltpu.CompilerParams(vmem_limit_bytes=VMEM_LIMIT_V7X),
        name="mem_bwd",
    )(dk, dv, mb, xhat, rstd, g, wk, wv)


def _adamw(w, g, m, v):
    m = ADAM_B1 * m + (1.0 - ADAM_B1) * g
    v = ADAM_B2 * v + (1.0 - ADAM_B2) * (g * g)
    m_hat = m / (1.0 - ADAM_B1 ** ADAM_STEP)
    v_hat = v / (1.0 - ADAM_B2 ** ADAM_STEP)
    delta = -ADAM_LR * (m_hat / (jnp.sqrt(v_hat) + ADAM_EPS) + ADAM_WD * w)
    return delta, m, v


def _slot_sum(ref):
    g = ref[0].astype(F32)
    for s in range(1, N_DEV):
        g = g + ref[s].astype(F32)
    return g


def _adam_sharded(lands, w, m, v, axis, name):
    rows, cols = w.shape
    nl = len(lands)
    transposed = axis == 1 and nl == 2
    if transposed:
        rows, cols = cols, rows
        tr = 256
        grid = (rows // tr,)
        wblk = pl.BlockSpec((cols, tr), lambda i: (0, i))
        lblk = [pl.BlockSpec((N_DEV, tr, a.shape[2]), lambda i: (0, i, 0)) for a in lands]
    elif axis == 1:
        tr = 256 if rows % 256 == 0 else rows
        grid = (rows // tr,)
        wblk = pl.BlockSpec((tr, cols), lambda i: (i, 0))
        lblk = [pl.BlockSpec((N_DEV, tr, a.shape[2]), lambda i: (0, i, 0)) for a in lands]
    else:
        tc = _col_tile(cols)
        grid = (cols // tc,)
        wblk = pl.BlockSpec((rows, tc), lambda i: (0, i))
        lblk = [pl.BlockSpec((N_DEV, a.shape[1], tc), lambda i: (0, 0, i)) for a in lands]

    def body(*refs):
        w_ref, m_ref, v_ref = refs[nl:nl + 3]
        g_ref, d_ref, nm_ref, nv_ref = refs[nl + 3:]
        g = _slot_sum(refs[0])
        if nl == 2:
            tail = _slot_sum(refs[1])
            if transposed:
                g = jnp.concatenate([g.T, tail.T[:cols - g.shape[1], :]], axis=0)
            elif axis == 1:
                g = jnp.concatenate([g, tail[:, :cols - g.shape[1]]], axis=1)
            else:
                g = jnp.concatenate([g, tail[:rows - g.shape[0], :]], axis=0)
        delta, nm, nv = _adamw(w_ref[...], g, m_ref[...], v_ref[...])
        g_ref[...] = g
        d_ref[...] = delta
        nm_ref[...] = nm
        nv_ref[...] = nv

    shp = jax.ShapeDtypeStruct(w.shape, F32)
    return pl.pallas_call(
        body,
        grid=grid,
        in_specs=lblk + [wblk, wblk, wblk],
        out_specs=[wblk, wblk, wblk, wblk],
        out_shape=[shp, shp, shp, shp],
        compiler_params=_params(1),
        name=name,
    )(*[pltpu.with_memory_space_constraint(a, pltpu.HBM) for a in (*lands, w, m, v)])


def _mesh_pos():
    return lax.axis_index("x"), lax.axis_index("y"), lax.axis_index("c")


def _peer(k):
    x, y, c = _mesh_pos()
    pos = (x ^ (k >> 2), y ^ ((k >> 1) & 1), c ^ (k & 1))
    return pos, 4 * pos[0] + 2 * pos[1] + pos[2]


def _sem_index(row, k):
    return row * (N_DEV - 1) + k - 1


def _window(ref, axis, start, size):
    align = 16 if axis == 0 else LANES
    start = pl.multiple_of(start, align)
    return ref.at[pl.ds(start, size), :] if axis == 0 else ref.at[:, pl.ds(start, size)]


def _piece_refs(piece, srcs, lands, me, peer):
    kind, si, li, axis, base, stride, shape = piece
    if kind == "gather":
        return srcs[si], _window(lands[li], axis, base + stride * me, shape[axis])
    return _window(srcs[si], axis, base + stride * peer, shape[axis]), lands[li].at[me]


def _place_own(srcs, land_shapes, pieces, name):
    ns, nl, npc = len(srcs), len(land_shapes), len(pieces)

    def body(*refs):
        s_refs = refs[:ns]
        l_refs = refs[ns:ns + nl]
        bufs = refs[ns + nl:ns + nl + npc]
        sems = refs[ns + nl + npc]
        x, y, c = _mesh_pos()
        me = 4 * x + 2 * y + c
        loads = []
        for p, piece in enumerate(pieces):
            src, dst = _piece_refs(piece, s_refs, l_refs, me, me)
            cp = pltpu.make_async_copy(src, bufs[p], sems.at[0, p])
            cp.start()
            loads.append((cp, dst))
        stores = []
        for p, (cp, dst) in enumerate(loads):
            cp.wait()
            out = pltpu.make_async_copy(bufs[p], dst, sems.at[1, p])
            out.start()
            stores.append(out)
        for out in stores:
            out.wait()

    out = pl.pallas_call(
        body,
        in_specs=[ANY] * ns,
        out_specs=[HBM] * nl,
        out_shape=[pltpu.HBM(s.shape, s.dtype) for s in land_shapes],
        scratch_shapes=[pltpu.VMEM(pc[6], srcs[pc[1]].dtype) for pc in pieces] + [pltpu.SemaphoreType.DMA((2, npc))],
        compiler_params=pltpu.CompilerParams(vmem_limit_bytes=VMEM_LIMIT_V7X),
        name=name,
    )(*srcs)
    return list(out)


def _comm_start(srcs, lands, pieces, groups, name, after=()):
    ns, nl, na, ng = len(srcs), len(lands), len(after), len(groups)

    def body(*refs):
        s_refs = refs[:ns]
        l_refs = refs[ns:ns + nl]
        outs = refs[ns + nl + na:]
        sems = outs[:2 * ng]
        token = outs[-1]
        x, y, c = _mesh_pos()
        me = 4 * x + 2 * y + c
        for g, members in enumerate(groups):
            for row, p in enumerate(members):
                for k in range(1, N_DEV):
                    pos, peer = _peer(k)
                    src, dst = _piece_refs(pieces[p], s_refs, l_refs, me, peer)
                    pltpu.make_async_remote_copy(src_ref=src, dst_ref=dst, send_sem=sems[2 * g].at[_sem_index(row, k)],
                                                 recv_sem=sems[2 * g + 1].at[_sem_index(row, k)], device_id=pos,
                                                 device_id_type=MESH_ID).start()
        token[...] = jnp.zeros_like(token)

    sem_shapes = []
    for members in groups:
        sem_shapes += [pltpu.SemaphoreType.DMA((len(members) * (N_DEV - 1),))] * 2
    hbm_of = lambda a: pltpu.HBM(a.shape, a.dtype)
    out = pl.pallas_call(
        body,
        in_specs=[HBM] * (ns + nl) + [ANY] * na,
        out_specs=[SEM] * (2 * ng) + [HBM] * (ns + nl) + [pl.BlockSpec(memory_space=pltpu.VMEM)],
        out_shape=sem_shapes + [hbm_of(a) for a in srcs] + [hbm_of(a) for a in lands]
        + [jax.ShapeDtypeStruct((8, LANES), F32)],
        input_output_aliases={i: 2 * ng + i for i in range(ns + nl)},
        compiler_params=pltpu.CompilerParams(has_side_effects=DATAFLOW),
        name=name,
    )(*[pltpu.with_memory_space_constraint(a, pltpu.HBM) for a in list(srcs) + list(lands)], *after)
    sems = [(out[2 * g], out[2 * g + 1]) for g in range(ng)]
    return sems, list(out[2 * ng:2 * ng + ns]), list(out[2 * ng + ns:2 * ng + ns + nl]), out[-1]


def _comm_wait(srcs, lands, pieces, members, sems, after, name):
    ns, nl, na = len(srcs), len(lands), len(after)

    def body(*refs):
        s_refs = refs[:ns]
        l_refs = refs[ns:ns + nl]
        send_sems, recv_sems = refs[ns + nl:ns + nl + 2]
        x, y, c = _mesh_pos()
        me = 4 * x + 2 * y + c
        for row, p in enumerate(members):
            for k in range(1, N_DEV):
                pos, peer = _peer(k)
                src, dst = _piece_refs(pieces[p], s_refs, l_refs, me, peer)
                cp = pltpu.make_async_remote_copy(src_ref=src, dst_ref=dst, send_sem=send_sems.at[_sem_index(row, k)],
                                                  recv_sem=recv_sems.at[_sem_index(row, k)], device_id=pos,
                                                  device_id_type=MESH_ID)
                cp.wait_send()
                cp.wait_recv()

    hbm_of = lambda a: pltpu.HBM(a.shape, a.dtype)
    out = pl.pallas_call(
        body,
        in_specs=[HBM] * (ns + nl) + [SEM, SEM] + [ANY] * na,
        out_specs=[HBM] * (ns + nl),
        out_shape=[hbm_of(a) for a in srcs] + [hbm_of(a) for a in lands],
        input_output_aliases={i: i for i in range(ns + nl)},
        compiler_params=pltpu.CompilerParams(has_side_effects=DATAFLOW),
        name=name,
    )(*srcs, *lands, sems[0], sems[1], *after)
    return list(out[ns:])


def _landed_block(piece, lands, owner):
    _, _, li, axis, base, stride, shape = piece
    return _window(lands[li], axis, base + stride * owner, shape[axis])


def _copy_stage(name, bufs, in_sems, out_sem_sizes, emit, after=()):
    nb, ni, no, na = len(bufs), len(in_sems), len(out_sem_sizes), len(after)

    def body(*refs):
        b_refs = refs[:nb]
        i_refs = refs[nb:nb + ni]
        o_refs = refs[nb + ni + na:nb + ni + na + no]
        emit(b_refs, i_refs, o_refs)
        refs[-1][...] = jnp.zeros_like(refs[-1])

    hbm_of = lambda a: pltpu.HBM(a.shape, a.dtype)
    out = pl.pallas_call(
        body,
        in_specs=[HBM] * nb + [SEM] * ni + [ANY] * na,
        out_specs=[SEM] * no + [HBM] * nb + [pl.BlockSpec(memory_space=pltpu.VMEM)],
        out_shape=[pltpu.SemaphoreType.DMA((n,)) for n in out_sem_sizes] + [hbm_of(a) for a in bufs]
        + [jax.ShapeDtypeStruct((8, LANES), F32)],
        input_output_aliases={i: no + i for i in range(nb)},
        compiler_params=pltpu.CompilerParams(has_side_effects=DATAFLOW),
        name=name,
    )(*[pltpu.with_memory_space_constraint(a, pltpu.HBM) for a in bufs], *in_sems, *after)
    return list(out[:no]), list(out[no:no + nb]), out[-1]


def _remote(src, dst, send, recv, to):
    return pltpu.make_async_remote_copy(src_ref=src, dst_ref=dst, send_sem=send, recv_sem=recv, device_id=to,
                                        device_id_type=MESH_ID)


def _routed_gather(srcs, lands, pieces, meanwhile, name):
    ns, npc = len(srcs), len(pieces)

    def places():
        x, y, c = _mesh_pos()
        index = lambda p: 4 * p[0] + 2 * p[1] + p[2]
        me, sib = (x, y, c), (x, y, 1 - c)
        xnb, ynb = (1 - x, y, c), (x, 1 - y, c)
        got_first = (x ^ (1 - c), y ^ c, c)
        pass_to = (x ^ c, y ^ (1 - c), c)
        diag = (1 - x, 1 - y, c)
        return index, me, sib, xnb, ynb, got_first, pass_to, diag

    def start(b, _, o):
        index, me, sib, xnb, ynb, *_rest = places()
        send_a, recv_sib, recv_nb = o
        for p, piece in enumerate(pieces):
            src, dst = _piece_refs(piece, b[:ns], b[ns:], index(me), 0)
            _remote(src, dst, send_a.at[3 * p], recv_sib.at[p], sib).start()
            _remote(src, dst, send_a.at[3 * p + 1], recv_nb.at[2 * p], xnb).start()
            _remote(src, dst, send_a.at[3 * p + 2], recv_nb.at[2 * p + 1], ynb).start()

    def pass_a(b, i, o):
        index, me, sib, xnb, ynb, got_first, pass_to, _diag = places()
        (recv_nb,) = i
        send_f, recv_f, send_d, recv_d = o
        for p, piece in enumerate(pieces):
            for j, nb in enumerate((xnb, ynb)):
                blk = _landed_block(piece, b, index(nb))
                _remote(blk, blk, send_f.at[2 * p + j], recv_nb.at[2 * p + j], sib).wait_recv()
                _remote(blk, blk, send_f.at[2 * p + j], recv_f.at[2 * p + j], sib).start()
            blk = _landed_block(piece, b, index(got_first))
            _remote(blk, blk, send_d.at[p], recv_d.at[p], pass_to).start()

    def pass_b(b, i, o):
        index, me, sib, *_mid, diag = places()
        (recv_d,) = i
        send_g, recv_g = o
        for p, piece in enumerate(pieces):
            blk = _landed_block(piece, b, index(diag))
            _remote(blk, blk, send_g.at[p], recv_d.at[p], sib).wait_recv()
            _remote(blk, blk, send_g.at[p], recv_g.at[p], sib).start()

    def last(b, i, _):
        index, me, sib, *_others = places()
        send_a, recv_sib, send_f, recv_f, send_d, send_g, recv_g = i
        for p, piece in enumerate(pieces):
            src, dst = _piece_refs(piece, b[:ns], b[ns:], index(me), 0)
            cp = lambda s_sem, r_sem: _remote(src, dst, s_sem, r_sem, sib)
            cp(send_a.at[3 * p], recv_sib.at[p]).wait_recv()
            cp(send_a.at[3 * p], recv_g.at[p]).wait_recv()
            for j in range(3):
                cp(send_a.at[3 * p + j], recv_sib.at[p]).wait_send()
            for j in range(2):
                cp(send_f.at[2 * p + j], recv_f.at[2 * p + j]).wait_recv()
                cp(send_f.at[2 * p + j], recv_f.at[2 * p + j]).wait_send()
            cp(send_d.at[p], recv_sib.at[p]).wait_send()
            cp(send_g.at[p], recv_sib.at[p]).wait_send()

    (send_a, recv_sib, recv_nb), bufs, started = _copy_stage(name + "_start", list(srcs) + list(lands), [],
                                                             [3 * npc, npc, 2 * npc], start)
    srcs, lands = bufs[:ns], bufs[ns:]
    (send_f, recv_f, send_d, recv_d), lands, _ = _copy_stage(name + "_pass_a", lands, [recv_nb],
                                                             [2 * npc, 2 * npc, npc, npc],
                                                             lambda b, i, o: pass_a(b, i, o), after=meanwhile(started))
    (send_g, recv_g), lands, tok = _copy_stage(name + "_pass_b", lands, [recv_d], [npc, npc], pass_b)
    _, bufs, _ = _copy_stage(name + "_last", list(srcs) + list(lands),
                             [send_a, recv_sib, send_f, recv_f, send_d, send_g, recv_g], [], last)
    return bufs[ns:], tok


_SMALL_NAMES = ("ln1_g", "ln1_b", "hg_lb_logits", "hg_norm_g", "sg_ln_g", "sg_ln_b", "sg_w_s", "sg_b_s",
                "ln2_g", "ln2_b", "mem_ln_g", "mem_ln_b", "ln3_g", "ln3_b", "ln4_g", "ln4_b")


_VEC_NAMES = ("ln1_g", "ln1_b", "ln2_g", "ln2_b", "mem_ln_g", "mem_ln_b", "ln3_g", "ln3_b", "ln4_g", "ln4_b")
_ROW_NAMES = ("hg_lb_logits", "hg_norm_g", "sg_ln_g", "sg_ln_b", "sg_b_s", "sg_w_s")
VEC_ROWS = 16


def _row_plan(shapes):
    plan, pos = {}, 0
    for k in _ROW_NAMES:
        shp = shapes[k]
        slabs, off = [], pos
        for idx in itertools.product(*[range(dim) for dim in shp[:-2]]):
            slabs.append((idx, off, shp[-2]))
            off += shp[-2]
        plan[k] = (pos, slabs)
        pos = -(-off // 8) * 8
    return plan, -(-pos // 16) * 16


def _pack_small_grads(gs, shapes, loss):
    d = gs[_VEC_NAMES[0]].size
    vec = jnp.concatenate([gs[k].reshape(1, -1) for k in _VEC_NAMES] + [jnp.tile(loss, (1, d // LANES))], axis=0)
    vec = jnp.pad(vec, ((0, VEC_ROWS - vec.shape[0]), (0, 0)))
    plan, total = _row_plan(shapes)
    parts, pos = [], 0
    for k in _ROW_NAMES:
        first, slabs = plan[k]
        rows = gs[k].reshape(-1, LANES)
        end = slabs[-1][1] + slabs[-1][2]
        nxt = -(-end // 8) * 8
        parts.append(jnp.pad(rows, ((0, nxt - first - rows.shape[0]), (0, 0))))
        pos = nxt
    parts.append(jnp.zeros((total - pos, LANES), F32))
    return vec, jnp.concatenate(parts, axis=0)


def _adam_small(land_vec, land_rows, w, m, v):
    names = _VEC_NAMES + _ROW_NAMES
    n = len(names)
    shapes = {k: w[k].shape for k in names}
    plan, _ = _row_plan(shapes)

    def body(*refs):
        lv_ref, lr_ref = refs[:2]
        w_refs, m_refs, v_refs = refs[2:2 + n], refs[2 + n:2 + 2 * n], refs[2 + 2 * n:2 + 3 * n]
        outs = refs[2 + 3 * n:2 + 7 * n]
        loss_ref = refs[2 + 7 * n]
        gv_s, gr_s = refs[3 + 7 * n:]
        gv_s[...] = _slot_sum(lv_ref)
        gr_s[...] = _slot_sum(lr_ref)
        loss_ref[...] = gv_s[len(_VEC_NAMES):len(_VEC_NAMES) + 1, :LANES]
        for p, k in enumerate(names):
            if k in _VEC_NAMES:
                row = _VEC_NAMES.index(k)
                slabs = [((), None, None)]
            else:
                slabs = plan[k][1]
            for idx, off, rows in slabs:
                g = gv_s[row:row + 1, :] if off is None else gr_s[off:off + rows, :]
                sel = idx + (slice(None), slice(None))
                delta, nm, nv = _adamw(w_refs[p][sel], g, m_refs[p][sel], v_refs[p][sel])
                for o, val in zip(range(4), (g, delta, nm, nv)):
                    outs[o * n + p][sel] = val

    flat = lambda tree: [tree[k] for k in names]
    shp = [jax.ShapeDtypeStruct(shapes[k], F32) for k in names]
    out = pl.pallas_call(
        body,
        out_shape=shp * 4 + [jax.ShapeDtypeStruct((1, LANES), F32)],
        scratch_shapes=[pltpu.VMEM(land_vec.shape[1:], F32), pltpu.VMEM(land_rows.shape[1:], F32)],
        name="adam_small",
    )(land_vec, land_rows, *flat(w), *flat(m), *flat(v))
    return [dict(zip(names, out[o * n:(o + 1) * n])) for o in range(4)], out[4 * n]


_COL_FFN = ("ffn1_w_gate", "ffn1_w_up", "ffn2_w_gate", "ffn2_w_up")
_ROW_FFN = ("ffn1_w_down", "ffn2_w_down")
_ROW_SQ = ("w_out", "xa_w_q", "xa_w_k", "xa_w_v", "xa_w_o")
_BIG_NAMES = ("ffn1_w_gate", "ffn1_w_up", "ffn1_w_down", "w_in", "w_out", "xa_w_q", "xa_w_k", "xa_w_v", "xa_w_o",
              "ffn2_w_gate", "ffn2_w_up", "ffn2_w_down")


def _ffn_split(fs):
    main = (fs // MXU_WIDTH_V7X) * MXU_WIDTH_V7X
    tail = fs - main
    tail_pad = -(-tail // LANES) * LANES
    assert main > 0 and tail > 0
    return main, tail, tail_pad


def _layout(name, shard_shape):
    r, c = shard_shape
    if name in _COL_FFN:
        main, tail, pad = _ffn_split(c)
        return (r, N_DEV * (main + pad)), [(1, 0, main, (r, main), (0, main)),
                                           (1, N_DEV * main, pad, (r, pad), (main, c))]
    if name in _ROW_FFN:
        main, tail, pad = _ffn_split(r)
        return (N_DEV * (main + pad), c), [(0, 0, main, (main, c), (0, main)),
                                           (0, N_DEV * main, pad, (pad, c), (main, r))]
    if name == "w_in":
        return (r, N_DEV * c), [(1, 0, c, (r, c), (0, c))]
    return (N_DEV * r, c), [(0, 0, r, (r, c), (0, r))]


def _shard_pieces(name, shard):
    out = []
    for axis, _, _, shape, (lo, hi) in _layout(name, shard.shape)[1]:
        part = shard[lo:hi, :] if axis == 0 else shard[:, lo:hi]
        pad = [(0, shape[0] - part.shape[0]), (0, shape[1] - part.shape[1])]
        out.append(jnp.pad(part, pad).astype(BF16))
    return out


def _gather_plan(names, shards):
    srcs, land_shapes, pieces, index = [], [], [], {}
    for li, name in enumerate(names):
        shape2d, parts = _layout(name, shards[name].shape)
        land_shapes.append(jax.ShapeDtypeStruct(shape2d, BF16))
        index[name] = []
        for (axis, base, stride, shape, _), src in zip(parts, _shard_pieces(name, shards[name])):
            index[name].append(len(pieces))
            pieces.append(("gather", len(srcs), li, axis, base, stride, shape))
            srcs.append(src)
    return srcs, land_shapes, pieces, index


def _scatter_plan(names, grads, shard_shapes):
    srcs, land_shapes, pieces, index = [], [], [], {}
    for si, name in enumerate(names):
        _, parts = _layout(name, shard_shapes[name])
        srcs.append(grads[name])
        index[name] = []
        for axis, base, stride, shape, _ in parts:
            index[name].append(len(land_shapes))
            pieces.append(("scatter", si, len(land_shapes), axis, base, stride, shape))
            land_shapes.append(jax.ShapeDtypeStruct((N_DEV,) + shape, grads[name].dtype))
    return srcs, land_shapes, pieces, index


def _small_views(small):
    row = lambda a: a.reshape(1, -1)
    ln = {k: row(small[k]) for k in ("ln1_g", "ln1_b", "ln2_g", "ln2_b", "ln3_g", "ln3_b", "ln4_g", "ln4_b",
                                      "mem_ln_g", "mem_ln_b", "hg_norm_g")}
    sg_w = small["sg_w_s"].reshape(SG_GROUPS, SG_CHUNK, SG_CHUNK)
    sg = dict(logits=jnp.swapaxes(small["hg_lb_logits"], 0, 1),
              g=small["sg_ln_g"].reshape(SG_GROUPS, 1, SG_DIM), b=small["sg_ln_b"].reshape(SG_GROUPS, 1, SG_DIM),
              w=sg_w, wt=jnp.swapaxes(sg_w, 1, 2), bs=small["sg_b_s"].reshape(SG_GROUPS, SG_CHUNK, 1))
    return ln, sg


def _forward(x, xb, mem, target, get_w, small, first_deps=()):
    ln, sg = _small_views(small)
    a1, b1, s1 = _ffn_up(xb, get_w("ffn1_w_gate", ()), get_w("ffn1_w_up", ()), "ffn1_up", deps=first_deps)
    h1b, xh1, rs1 = _mm_res_ln(s1, get_w("ffn1_w_down", (s1,)), x, ln["ln1_g"], ln["ln1_b"], 0.5, "ffn1_down_ln")
    proj = _mm_nn(h1b, get_w("w_in", (h1b,)), "mix_in")
    oraw, mix, states = _hgrn_fwd(proj, sg["logits"], ln["hg_norm_g"])
    mix = _sgu_fwd(proj, mix, sg["g"], sg["b"], sg["w"], sg["bs"])
    h2b, xh2, rs2 = _mm_res_ln(mix, get_w("w_out", (mix,)), (xh1, ln["ln1_g"], ln["ln1_b"]), ln["ln2_g"], ln["ln2_b"],
                               1.0, "mix_out_ln")
    mb, mxh, mrs, kb, vb = _mem_kv(mem, ln["mem_ln_g"], ln["mem_ln_b"], get_w("xa_w_k", (h2b,)), get_w("xa_w_v", (h2b,)))
    qb, att = _attn_fwd(h2b, get_w("xa_w_q", (kb,)), kb, vb)
    h3b, xh3, rs3 = _mm_res_ln(att, get_w("xa_w_o", (att,)), (xh2, ln["ln2_g"], ln["ln2_b"]), ln["ln3_g"], ln["ln3_b"],
                               1.0, "attn_out_ln")
    a2, b2, s2 = _ffn_up(h3b, get_w("ffn2_w_gate", (h3b,)), get_w("ffn2_w_up", (h3b,)), "ffn2_up")
    loss, dy4, dy4b, dg4, db4 = _mm_res_ln(s2, get_w("ffn2_w_down", (s2,)), (xh3, ln["ln3_g"], ln["ln3_b"]),
                                           ln["ln4_g"], ln["ln4_b"], 0.5, "ffn2_down_ln_loss", target=target)
    return dict(xb=xb, a1=a1, b1=b1, s1=s1, h1b=h1b, xh1=xh1, rs1=rs1, proj=proj, oraw=oraw, mix=mix, states=states,
                h2b=h2b, xh2=xh2, rs2=rs2, mb=mb, mxh=mxh, mrs=mrs, kb=kb, vb=vb, qb=qb, att=att, h3b=h3b, xh3=xh3,
                rs3=rs3, a2=a2, b2=b2, s2=s2, loss=loss, dy4=dy4, dy4b=dy4b, dg4=dg4, db4=db4)


def _backward(sv, wt, small, send):
    ln, sg = _small_views(small)
    gs = {"ln4_g": sv["dg4"], "ln4_b": sv["db4"]}
    loss, dy4, dy4b = sv["loss"], sv["dy4"], sv["dy4b"]
    g_down2 = _mm_tn(sv["s2"], dy4b, "g_ffn2_down", scale=0.5)
    da2, db2, dy3, dy3b, gs["ln3_g"], gs["ln3_b"] = _ffn_bwd_fused(
        dy4, dy4b, wt["ffn2_w_down"], wt["ffn2_w_gate"], wt["ffn2_w_up"], sv["a2"], sv["b2"], 0.5,
        (sv["xh3"], sv["rs3"], ln["ln3_g"]), "ffn2_bwd")
    g_gate2 = _mm_tn(sv["h3b"], da2, "g_ffn2_gate")
    g_up2 = _mm_tn(sv["h3b"], db2, "g_ffn2_up")
    tok = send(("ffn2_w_down", "ffn2_w_gate", "ffn2_w_up"), (g_down2, g_gate2, g_up2))

    g_o = _mm_tn(sv["att"], dy3b, "g_xa_o", deps=(tok,))
    dqb, dk, dv = _attn_bwd(dy3b, wt["xa_w_o"], sv["qb"], sv["kb"], sv["vb"])
    g_q = _mm_tn(sv["h2b"], dqb, "g_xa_q")
    g_k, g_v, gs["mem_ln_g"], gs["mem_ln_b"] = _mem_bwd(dk, dv, sv["mb"], sv["mxh"], sv["mrs"], ln["mem_ln_g"],
                                                        wt["xa_w_k"], wt["xa_w_v"])
    tok = send(("xa_w_o", "xa_w_q", "xa_w_k", "xa_w_v"), (g_o, g_q, g_k, g_v))
    dy2, dy2b, gs["ln2_g"], gs["ln2_b"] = _dx_ln(dy3, [(dqb, wt["xa_w_q"])], (sv["xh2"], sv["rs2"], ln["ln2_g"]),
                                                 "attn_dx_ln", deps=(tok,))

    g_out = _mm_tn(sv["mix"], dy2b, "g_w_out")
    dmix = _mm_nt(dy2b, wt["w_out"], "mix_out_bwd")
    dq, dfz, div, dgg, dlg, dgn = _hgrn_bwd(sv["proj"], sv["oraw"], dmix, sv["states"], sg["logits"], ln["hg_norm_g"])
    du, dvv, gs["sg_ln_g"], gs["sg_ln_b"], gs["sg_w_s"], gs["sg_b_s"] = _sgu_bwd(
        sv["proj"], dmix, sg["g"], sg["b"], sg["w"], sg["wt"], sg["bs"])
    gs["hg_lb_logits"] = jnp.swapaxes(dlg, 0, 1)
    gs["hg_norm_g"] = jnp.sum(dgn, axis=0)
    dproj = [dq, dfz, div, dgg, du, dvv]
    g_in = _mm_tn(sv["h1b"], dproj, "g_w_in")
    tok = send(("w_out", "w_in"), (g_out, g_in))
    dy1, dy1b, gs["ln1_g"], gs["ln1_b"] = _dx_ln(dy2, [(dproj, wt["w_in"])], (sv["xh1"], sv["rs1"], ln["ln1_g"]),
                                                 "mix_dx_ln", deps=(tok,))

    g_down1 = _mm_tn(sv["s1"], dy1b, "g_ffn1_down", scale=0.5)
    tok = send(("ffn1_w_down",), (g_down1,))
    da1, db1 = _ffn_bwd_act(dy1b, wt["ffn1_w_down"], sv["a1"], sv["b1"], 0.5, "ffn1_bwd_act", deps=(tok,))
    g_gate1 = _mm_tn(sv["xb"], da1, "g_ffn1_gate")
    tok = send(("ffn1_w_gate",), (g_gate1,))
    g_up1 = _mm_tn(sv["xb"], db1, "g_ffn1_up", deps=(tok,))
    tok = send(("ffn1_w_up",), (g_up1,))
    grad_x = _dx_ln(dy1, [(da1, wt["ffn1_w_gate"]), (db1, wt["ffn1_w_up"])], None, "ffn1_dx", deps=(tok,))
    return loss, grad_x, gs


_WEIGHT_NAMES = ("ffn1_w_gate", "ffn1_w_up", "ffn1_w_down", "ln1_g", "ln1_b", "w_in", "hg_lb_logits", "hg_norm_g",
                 "sg_ln_g", "sg_ln_b", "sg_w_s", "sg_b_s", "w_out", "ln2_g", "ln2_b", "mem_ln_g", "mem_ln_b",
                 "xa_w_q", "xa_w_k", "xa_w_v", "xa_w_o", "ln3_g", "ln3_b", "ffn2_w_gate", "ffn2_w_up", "ffn2_w_down",
                 "ln4_g", "ln4_b")
_FIRST = ("ffn1_w_gate", "ffn1_w_up")
_SECOND = ("ffn1_w_down", "w_in", "w_out")
_THIRD = ("xa_w_k", "xa_w_v", "xa_w_q", "xa_w_o", "ffn2_w_gate", "ffn2_w_up", "ffn2_w_down")


def kernel(x, mem, ffn1_w_gate, ffn1_w_up, ffn1_w_down, ln1_g, ln1_b, w_in, hg_lb_logits, hg_norm_g, sg_ln_g, sg_ln_b, sg_w_s, sg_b_s, w_out, ln2_g, ln2_b, mem_ln_g, mem_ln_b, xa_w_q, xa_w_k, xa_w_v, xa_w_o, ln3_g, ln3_b, ffn2_w_gate, ffn2_w_up, ffn2_w_down, ln4_g, ln4_b, loss_target, m_ffn1_w_gate, m_ffn1_w_up, m_ffn1_w_down, m_ln1_g, m_ln1_b, m_w_in, m_hg_lb_logits, m_hg_norm_g, m_sg_ln_g, m_sg_ln_b, m_sg_w_s, m_sg_b_s, m_w_out, m_ln2_g, m_ln2_b, m_mem_ln_g, m_mem_ln_b, m_xa_w_q, m_xa_w_k, m_xa_w_v, m_xa_w_o, m_ln3_g, m_ln3_b, m_ffn2_w_gate, m_ffn2_w_up, m_ffn2_w_down, m_ln4_g, m_ln4_b, v_ffn1_w_gate, v_ffn1_w_up, v_ffn1_w_down, v_ln1_g, v_ln1_b, v_w_in, v_hg_lb_logits, v_hg_norm_g, v_sg_ln_g, v_sg_ln_b, v_sg_w_s, v_sg_b_s, v_w_out, v_ln2_g, v_ln2_b, v_mem_ln_g, v_mem_ln_b, v_xa_w_q, v_xa_w_k, v_xa_w_v, v_xa_w_o, v_ln3_g, v_ln3_b, v_ffn2_w_gate, v_ffn2_w_up, v_ffn2_w_down, v_ln4_g, v_ln4_b):
    args = dict(locals())
    w = {k: args[k] for k in _WEIGHT_NAMES}
    m = {k: args["m_" + k] for k in _WEIGHT_NAMES}
    v = {k: args["v_" + k] for k in _WEIGHT_NAMES}
    shards = {k: w[k][0] for k in _BIG_NAMES}
    shard_shapes = {k: shards[k].shape for k in _BIG_NAMES}
    small = {k: (w[k][0] if k != "hg_lb_logits" else w[k]) for k in _SMALL_NAMES}

    srcs1, shapes1, pieces1, idx1 = _gather_plan(_FIRST, shards)
    lands1 = _place_own(srcs1, shapes1, pieces1, "gather_first_own")
    rest = _SECOND + _THIRD
    prepared = {}

    def prepare_rest(started):
        later = {k: shards[k] + started[0, 0] for k in rest}
        srcs2, shapes2, pieces2, idx2 = _gather_plan(rest, later)
        lands2 = _place_own(srcs2, shapes2, pieces2, "gather_rest_own")
        xb = _to_bf16(x[0], "x_bf16", deps=(started,))
        prepared.update(srcs=srcs2, pieces=pieces2, idx=idx2, lands=lands2, xb=xb)
        return tuple(lands2) + (xb,)

    lands1, tok1 = _routed_gather(srcs1, lands1, pieces1, prepare_rest, "gather_first")
    srcs2, pieces2, idx2, lands2 = (prepared[k] for k in ("srcs", "pieces", "idx", "lands"))
    groups2 = [list(idx2[k]) for k in rest]
    sems2, srcs2, lands2, tok2 = _comm_start(srcs2, lands2, pieces2, groups2, "gather_rest_start", after=(tok1,))
    wt = dict(zip(_FIRST, lands1))
    pending = {k: gi for gi, k in enumerate(rest)}

    def get_w(name, after):
        if name in pending:
            gi = pending.pop(name)
            si = [pieces2[p][1] for p in groups2[gi]]
            sub = [(pieces2[p][0], row, 0) + pieces2[p][3:] for row, p in enumerate(groups2[gi])]
            wt[name] = _comm_wait([srcs2[s] for s in si], [lands2[gi]], sub, list(range(len(sub))), sems2[gi],
                                  after, "gather_wait_" + name)[0]
        return wt[name]

    sv = _forward(x[0], prepared["xb"], mem[0], loss_target[0], get_w, small, first_deps=(tok2,))

    sent = []

    def send(names, grads):
        srcs, shapes, pieces, idx = _scatter_plan(names, dict(zip(names, grads)), shard_shapes)
        lands = _place_own(srcs, shapes, pieces, "grads_own_%d" % len(sent))
        sems, srcs, lands, tok = _comm_start(srcs, lands, pieces, [list(range(len(pieces)))],
                                             "grads_start_%d" % len(sent))
        sent.append((names, srcs, lands, pieces, idx, sems[0]))
        return tok

    loss, grad_x, gs = _backward(sv, wt, small, send)

    ssrc = list(_pack_small_grads(gs, {k: w[k].shape for k in _SMALL_NAMES}, loss))
    sp = [("scatter", i, i, 0, 0, 0, a.shape) for i, a in enumerate(ssrc)]
    sshape = [jax.ShapeDtypeStruct((N_DEV,) + a.shape, F32) for a in ssrc]
    sl = _place_own(ssrc, sshape, sp, "small_own")
    ssem, ssrc, sl, _ = _comm_start(ssrc, sl, sp, [[0, 1]], "small_start")

    out_g, out_d, out_m, out_v = {}, {}, {}, {}
    after = (grad_x,)
    for n_sent, (names, srcs, lands, pieces, idx, sems) in enumerate(sent):
        lands = _comm_wait(srcs, lands, pieces, list(range(len(pieces))), sems, after, "grads_wait_%d" % n_sent)
        for k in names:
            axis = 1 if (k in _COL_FFN or k == "w_in") else 0
            if k in _COL_FFN:
                res = _adam_sharded([lands[i] for i in idx[k]], w[k][0].T, m[k][0].T, v[k][0].T, axis, "adam_" + k)
                res = [r.T for r in res]
            else:
                res = _adam_sharded([lands[i] for i in idx[k]], w[k][0], m[k][0], v[k][0], axis, "adam_" + k)
            out_g[k], out_d[k], out_m[k], out_v[k] = [r[None] for r in res]
        after = (out_v[names[-1]],)
    sl = _comm_wait(ssrc, sl, sp, [0, 1], ssem[0], after, "small_wait")
    small_out, loss_sum = _adam_small(sl[0], sl[1], w, m, v)
    for dst, res in zip((out_g, out_d, out_m, out_v), small_out):
        dst.update(res)
    loss_all = loss_sum[0, 0]
    return (loss_all, grad_x[None], *[out_g[k] for k in _WEIGHT_NAMES], *[out_d[k] for k in _WEIGHT_NAMES],
            *[out_m[k] for k in _WEIGHT_NAMES], *[out_v[k] for k in _WEIGHT_NAMES])
```

```python
import itertools

import jax
import jax.numpy as jnp
import numpy as np
from jax import lax
from jax.experimental import pallas as pl
from jax.experimental.pallas import tpu as pltpu

F32 = jnp.float32
BF16 = jnp.bfloat16

N_DEV = 8
ALPHA = 2.0 ** 0.25
LN_EPS = 1e-5
HG_HEADS = 4
HG_DIM = 128
SG_GROUPS = 4
SG_DIM = 128
SG_CHUNK = 128
X_HEADS = 4
HG_BLOCK = 16
HG_UNROLL = 16
ADAM_LR = 0.001
ADAM_B1 = 0.9
ADAM_B2 = 0.999
ADAM_EPS = 1e-08
ADAM_WD = 0.01
ADAM_STEP = 10
VMEM_LIMIT_V7X = 48 * 1024 * 1024
MXU_WIDTH_V7X = 256
LANES = 128
MESH_ID = pl.DeviceIdType.MESH
ANY = pl.BlockSpec(memory_space=pl.ANY)
HBM = pl.BlockSpec(memory_space=pltpu.HBM)
SEM = pl.BlockSpec(memory_space=pltpu.SEMAPHORE)
DATAFLOW = pltpu.SideEffectType.DATAFLOW_SIDE_EFFECTING


def _params(n_axes):
    return pltpu.CompilerParams(dimension_semantics=("arbitrary",) * n_axes, vmem_limit_bytes=VMEM_LIMIT_V7X)


def _dot(a, b):
    return jnp.dot(a, b, preferred_element_type=F32)


def _dot_nt(a, b):
    return lax.dot_general(a, b, (((1,), (1,)), ((), ())), preferred_element_type=F32)


def _dot_tn(a, b):
    return lax.dot_general(a, b, (((0,), (0,)), ((), ())), preferred_element_type=F32)


def _sigmoid(x):
    return 1.0 / (1.0 + jnp.exp(-x))


def _silu_and_grad(a):
    sig = _sigmoid(a)
    return a * sig, sig * (1.0 + a * (1.0 - sig))


_GELU_C = 0.7978845608028654


def _gelu_and_grad(x):
    inner = _GELU_C * (x + 0.044715 * x * x * x)
    t = jnp.tanh(inner)
    val = 0.5 * x * (1.0 + t)
    grad = 0.5 * (1.0 + t) + 0.5 * x * (1.0 - t * t) * _GELU_C * (1.0 + 3.0 * 0.044715 * x * x)
    return val, grad


def _ln_fwd(y, g, b):
    mu = jnp.mean(y, axis=-1, keepdims=True)
    yc = y - mu
    var = jnp.mean(yc * yc, axis=-1, keepdims=True)
    rstd = lax.rsqrt(var + LN_EPS)
    xhat = yc * rstd
    return xhat * g + b, xhat, rstd


def _ln_bwd(dh, xhat, rstd, g):
    dxh = dh * g
    m1 = jnp.mean(dxh, axis=-1, keepdims=True)
    m2 = jnp.mean(dxh * xhat, axis=-1, keepdims=True)
    dy = rstd * (dxh - m1 - xhat * m2)
    dg = jnp.sum(dh * xhat, axis=0, keepdims=True)
    db = jnp.sum(dh, axis=0, keepdims=True)
    return dy, dg, db


def _mask_dot(mask, x):
    hi = x.astype(BF16)
    lo = (x - hi.astype(F32)).astype(BF16)
    n = mask.shape[0]
    parts = [_dot(mask, hi[r:r + n, :]) + _dot(mask, lo[r:r + n, :]) for r in range(0, x.shape[0], n)]
    return parts[0] if len(parts) == 1 else jnp.concatenate(parts, axis=0)


def _block_masks(n):
    r = np.arange(n)[:, None]
    c = np.arange(n)[None, :]
    same = (r // HG_BLOCK) == (c // HG_BLOCK)
    return jnp.asarray(np.stack([same & (c <= r), same & (c >= r), same]), BF16)


def _row_tile(t):
    return min(t, 512)


def _col_tile(n):
    for cand in (512, 256, 128):
        if n % cand == 0:
            return cand
    return n


def _resident(w):
    return pl.BlockSpec(w.shape, lambda *_: (0, 0), pipeline_mode=pl.Buffered(1))


def _drop_deps(body, n_in, n_deps):
    if n_deps == 0:
        return body
    return lambda *refs: body(*refs[:n_in], *refs[n_in + n_deps:])


def _to_bf16(x, name, deps=()):
    t, d = x.shape
    tm = _row_tile(t)

    def body(x_ref, o_ref):
        o_ref[...] = x_ref[...].astype(BF16)

    row = pl.BlockSpec((tm, d), lambda i: (i, 0))
    return pl.pallas_call(
        _drop_deps(body, 1, len(deps)),
        grid=(t // tm,),
        in_specs=[row] + [ANY] * len(deps),
        out_specs=row,
        out_shape=jax.ShapeDtypeStruct((t, d), BF16),
        compiler_params=_params(1),
        name=name,
    )(x, *deps)


def _ffn_up(hb, wg, wu, name, deps=()):
    t, d = hb.shape
    f = wg.shape[1]
    tm = _row_tile(t)
    tn = _col_tile(f)

    def body(h_ref, wg_ref, wu_ref, a_ref, b_ref, s_ref):
        h = h_ref[...]
        for c in range(f // tn):
            cols = slice(c * tn, (c + 1) * tn)
            a = _dot(h, wg_ref[:, cols])
            b = _dot(h, wu_ref[:, cols])
            a_ref[:, cols] = a.astype(BF16)
            b_ref[:, cols] = b.astype(BF16)
            s_ref[:, cols] = (a * _sigmoid(a) * b).astype(BF16)

    act = pl.BlockSpec((tm, f), lambda i: (i, 0))
    return pl.pallas_call(
        _drop_deps(body, 3, len(deps)),
        grid=(t // tm,),
        in_specs=[pl.BlockSpec((tm, d), lambda i: (i, 0)), _resident(wg), _resident(wu)] + [ANY] * len(deps),
        out_specs=[act, act, act],
        out_shape=[jax.ShapeDtypeStruct((t, f), BF16)] * 3,
        compiler_params=_params(1),
        name=name,
    )(hb, wg, wu, *deps)


def _mm_res_ln(lhs, w, res, g, b, coef, name, target=None):
    t, kd = lhs.shape
    d = w.shape[1]
    tm = _row_tile(t)
    nt = t // tm
    from_norm = isinstance(res, tuple)
    n_res = 3 if from_norm else 1

    def body(*refs):
        l_ref, w_ref = refs[:2]
        r_refs = refs[2:2 + n_res]
        g_ref, b_ref = refs[2 + n_res:4 + n_res]
        rest = refs[4 + n_res:]
        prev = r_refs[0][...] * r_refs[1][...] + r_refs[2][...] if from_norm else r_refs[0][...]
        y = ALPHA * prev + coef * _dot(l_ref[...], w_ref[...])
        h, xhat, rstd = _ln_fwd(y, g_ref[...], b_ref[...])
        if target is None:
            hb_ref, xh_ref, rs_ref = rest
            hb_ref[...] = h.astype(BF16)
            xh_ref[...] = xhat
            rs_ref[...] = rstd
            return
        t_ref, loss_ref, dy_ref, dyb_ref, dg_ref, db_ref, lacc = rest
        i = pl.program_id(0)

        @pl.when(i == 0)
        def _():
            lacc[...] = jnp.zeros_like(lacc)
            dg_ref[...] = jnp.zeros_like(dg_ref)
            db_ref[...] = jnp.zeros_like(db_ref)

        err = h - t_ref[...]
        lacc[...] += jnp.sum(err * err, axis=0, keepdims=True)
        dy, dg, db = _ln_bwd(err * (1.0 / d), xhat, rstd, g_ref[...])
        dy_ref[...] = dy
        dyb_ref[...] = dy.astype(BF16)
        dg_ref[...] += dg
        db_ref[...] += db

        @pl.when(i == nt - 1)
        def _():
            loss_ref[...] = jnp.zeros_like(loss_ref) + jnp.sum(lacc[...], axis=1, keepdims=True) * (0.5 / d)

    row = pl.BlockSpec((tm, d), lambda i: (i, 0))
    vec = pl.BlockSpec((1, d), lambda i: (0, 0))
    res_specs = [row, vec, vec] if from_norm else [row]
    res_args = list(res) if from_norm else [res]
    in_specs = [pl.BlockSpec((tm, kd), lambda i: (i, 0)), _resident(w)] + res_specs + [vec, vec]
    args = [lhs, w] + res_args + [g, b]
    if target is None:
        out_specs = [row, row, pl.BlockSpec((tm, 1), lambda i: (i, 0))]
        out_shape = [jax.ShapeDtypeStruct((t, d), BF16), jax.ShapeDtypeStruct((t, d), F32), pltpu.HBM((t, 1), F32)]
        scratch = []
    else:
        in_specs.append(row)
        args.append(target)
        out_specs = [pl.BlockSpec((1, LANES), lambda i: (0, 0)), row, row, vec, vec]
        out_shape = [jax.ShapeDtypeStruct((1, LANES), F32), jax.ShapeDtypeStruct((t, d), F32),
                     jax.ShapeDtypeStruct((t, d), BF16), jax.ShapeDtypeStruct((1, d), F32),
                     jax.ShapeDtypeStruct((1, d), F32)]
        scratch = [pltpu.VMEM((1, d), F32)]
    return pl.pallas_call(
        body,
        grid=(nt,),
        in_specs=in_specs,
        out_specs=out_specs,
        out_shape=out_shape,
        scratch_shapes=scratch,
        compiler_params=_params(1),
        name=name,
    )(*args)


def _store_slabs(o_ref, first, tile):
    for s in range(tile.shape[1] // LANES):
        o_ref[first + s] = tile[:, s * LANES:(s + 1) * LANES]


def _mm_nn(lhs, w, name):
    t, kd = lhs.shape
    n = w.shape[1]
    tm = _row_tile(t)
    tn = _col_tile(n)

    def body(l_ref, w_ref, o_ref):
        lhs_v = l_ref[...]
        for c in range(n // tn):
            _store_slabs(o_ref, c * (tn // LANES), _dot(lhs_v, w_ref[:, c * tn:(c + 1) * tn]))

    return pl.pallas_call(
        body,
        grid=(t // tm,),
        in_specs=[pl.BlockSpec((tm, kd), lambda i: (i, 0)), _resident(w)],
        out_specs=pl.BlockSpec((n // LANES, tm, LANES), lambda i: (0, i, 0)),
        out_shape=jax.ShapeDtypeStruct((n // LANES, t, LANES), F32),
        compiler_params=_params(1),
        name=name,
    )(lhs, w)


def _lower_bound(lg):
    m = jnp.max(lg, axis=0, keepdims=True)
    e = jnp.exp(lg - m)
    return e[0:1, :] / jnp.sum(e, axis=0, keepdims=True)


def _forget_terms(fz, lb):
    e = jnp.exp(-jnp.abs(fz))
    r = 1.0 / (1.0 + e)
    pos = fz >= 0.0
    sig = jnp.where(pos, r, e * r)
    nsig = jnp.where(pos, e * r, r)
    f = lb + (1.0 - lb) * sig
    k = (1.0 - lb) * nsig
    return sig, nsig, f, k


def _hg_tile(t):
    return min(t, 1024)


HG_HALF = HG_BLOCK // 2
NEG_BIG = -1e30


def _halves(a):
    return a[:HG_HALF, :], a[HG_HALF:, :]


def _causal_halves(s):
    return (0, 1) if s < HG_HALF else (1,)


def _decay_from(b_half, b_s, s, h, tidx):
    first = s - h * HG_HALF
    diff = b_half - b_s
    if first > 0:
        diff = jnp.where(tidx >= first, diff, NEG_BIG)
    return jnp.exp(diff)


def _hgrn_fwd(proj, logits, gn):
    t = proj.shape[1]
    ct = _hg_tile(t)
    nct = t // ct
    nblk = ct // HG_BLOCK
    nh = HG_HEADS
    mrows = min(ct, 256)

    def body(q_ref, fz_ref, iv_ref, gg_ref, lg_ref, gn_ref, mask_ref, oraw_ref, oa_ref, st_ref,
             state, qt_s, kt_s, k_s, b_s, dec_s):
        c = pl.program_id(1)

        @pl.when(c == 0)
        def _():
            state[...] = jnp.zeros_like(state)

        lb = _lower_bound(lg_ref[...])
        q = q_ref[...]
        _, _, f, k = _forget_terms(fz_ref[...], lb)
        logf = jnp.log(f)
        b = _mask_dot(mask_ref[0], logf)
        bend = _mask_dot(mask_ref[2], logf)
        qt_s[...] = (q * jnp.exp(b)).astype(BF16)
        kt_s[...] = (k * jnp.exp(bend - b)).astype(BF16)
        k_s[...] = k
        b_s[...] = b
        dec_s[...] = jnp.exp(bend)
        tidx = lax.broadcasted_iota(jnp.int32, (HG_HALF, HG_DIM), 0)

        def blk(i, carry):
            r0 = pl.multiple_of(i * HG_BLOCK, HG_BLOCK)
            rows = pl.ds(r0, HG_BLOCK)
            st = state[...]
            stb = st.astype(BF16)
            st_ref[i] = stb
            v = iv_ref[rows, :]
            qq = q_ref[rows, :]
            kk = k_s[rows, :]
            bb = b_s[rows, :]
            o = list(_halves(_dot_nt(qt_s[rows, :], stb)))
            qh, bh = _halves(qq), _halves(bb)
            for s in range(HG_BLOCK):
                ks, vs = kk[s:s + 1, :], v[s:s + 1, :]
                for h in _causal_halves(s):
                    e = _decay_from(bh[h], bb[s:s + 1, :], s, h, tidx)
                    acol = jnp.sum(qh[h] * (ks * e), axis=1, keepdims=True)
                    o[h] = o[h] + acol * vs
            oraw_ref[rows, :] = jnp.concatenate(o, axis=0)
            state[...] = st * dec_s[pl.ds(r0, 1), :] + _dot_tn(v.astype(BF16), kt_s[rows, :])
            return carry

        lax.fori_loop(0, nblk, blk, 0, unroll=HG_UNROLL)
        oraw = oraw_ref[...]
        r = lax.rsqrt(jnp.mean(oraw * oraw, axis=-1, keepdims=True) + LN_EPS)
        gg = gg_ref[...]
        oa_ref[...] = (oraw * r * gn_ref[...] * gg * _sigmoid(gg)).astype(BF16)

    def slab(off):
        return pl.BlockSpec((None, ct, HG_DIM), lambda h, c: (off + h, c, 0))

    return pl.pallas_call(
        body,
        grid=(nh, nct),
        in_specs=[slab(0), slab(nh), slab(2 * nh), slab(3 * nh),
                  pl.BlockSpec((None, 2, HG_DIM), lambda h, c: (h, 0, 0)),
                  pl.BlockSpec((1, HG_DIM), lambda h, c: (0, 0)),
                  pl.BlockSpec((3, mrows, mrows), lambda h, c: (0, 0, 0))],
        out_specs=[slab(0), pl.BlockSpec((ct, HG_DIM), lambda h, c: (c, h)),
                   pl.BlockSpec((None, nblk, HG_DIM, HG_DIM), lambda h, c: (h, c, 0, 0))],
        out_shape=[jax.ShapeDtypeStruct((nh, t, HG_DIM), F32),
                   jax.ShapeDtypeStruct((t, (nh + SG_GROUPS) * HG_DIM), BF16),
                   jax.ShapeDtypeStruct((nh, t // HG_BLOCK, HG_DIM, HG_DIM), BF16)],
        scratch_shapes=[pltpu.VMEM((HG_DIM, HG_DIM), F32), pltpu.VMEM((ct, HG_DIM), BF16),
                        pltpu.VMEM((ct, HG_DIM), BF16), pltpu.VMEM((ct, HG_DIM), F32),
                        pltpu.VMEM((ct, HG_DIM), F32), pltpu.VMEM((ct, HG_DIM), F32)],
        compiler_params=_params(2),
        name="hgrn_fwd",
    )(proj, proj, proj, proj, logits, gn, _block_masks(mrows))


def _sg_tile(t):
    return min(t, 512)


def _sgu_chunk_fwd(u, v, ln_g, ln_b, wm, bs):
    ua, dua = _gelu_and_grad(u)
    va, dva = _gelu_and_grad(v)
    vn, xhat, rstd = _ln_fwd(va, ln_g, ln_b)
    s = _dot(wm, vn.astype(BF16)) + bs
    return ua, dua, dva, vn, xhat, rstd, s


def _tril_weight(w):
    n = SG_CHUNK
    r = lax.broadcasted_iota(jnp.int32, (n, n), 0)
    c = lax.broadcasted_iota(jnp.int32, (n, n), 1)
    return jnp.where(c <= r, w, 0.0)


def _sgu_fwd(proj, mix, ln_g, ln_b, w_s, b_col):
    t = proj.shape[1]
    ct = _sg_tile(t)
    ng = SG_GROUPS
    wide = ng * SG_DIM
    blk_u = 4 * HG_HEADS // ng

    def body(u_ref, v_ref, g_ref, b_ref, w_ref, bs_ref, mix_ref, o_ref):
        del mix_ref
        for g in range(ng):
            lanes = slice(g * SG_DIM, (g + 1) * SG_DIM)
            wm = _tril_weight(w_ref[g]).astype(BF16)
            for n in range(ct // SG_CHUNK):
                rows = slice(n * SG_CHUNK, (n + 1) * SG_CHUNK)
                ua, _, _, _, _, _, s = _sgu_chunk_fwd(u_ref[g, rows, :], v_ref[g, rows, :], g_ref[g], b_ref[g], wm,
                                                      bs_ref[g])
                o_ref[rows, lanes] = (ua * s).astype(BF16)

    full = lambda a: pl.BlockSpec(a.shape, lambda c: (0,) * a.ndim)
    return pl.pallas_call(
        body,
        grid=(t // ct,),
        in_specs=[pl.BlockSpec((ng, ct, SG_DIM), lambda c: (blk_u, c, 0)),
                  pl.BlockSpec((ng, ct, SG_DIM), lambda c: (blk_u + 1, c, 0)),
                  full(ln_g), full(ln_b), full(w_s), full(b_col), ANY],
        out_specs=pl.BlockSpec((ct, wide), lambda c: (c, 1)),
        out_shape=jax.ShapeDtypeStruct(mix.shape, mix.dtype),
        input_output_aliases={6: 0},
        compiler_params=_params(1),
        name="sgu_fwd",
    )(proj, proj, ln_g, ln_b, w_s, b_col, mix)


def _mem_kv(mem, g, b, wk, wv):
    m_len, d = mem.shape

    def body(m_ref, g_ref, b_ref, wk_ref, wv_ref, mb_ref, xh_ref, rs_ref, k_ref, v_ref):
        m, xhat, rstd = _ln_fwd(m_ref[...], g_ref[...], b_ref[...])
        mb = m.astype(BF16)
        mb_ref[...] = mb
        xh_ref[...] = xhat
        rs_ref[...] = rstd
        k_ref[...] = _dot(mb, wk_ref[...]).astype(BF16)
        v_ref[...] = _dot(mb, wv_ref[...]).astype(BF16)

    return pl.pallas_call(
        body,
        out_shape=[jax.ShapeDtypeStruct((m_len, d), BF16), jax.ShapeDtypeStruct((m_len, d), F32),
                   jax.ShapeDtypeStruct((m_len, 1), F32), jax.ShapeDtypeStruct((m_len, d), BF16),
                   jax.ShapeDtypeStruct((m_len, d), BF16)],
        compiler_params=pltpu.CompilerParams(vmem_limit_bytes=VMEM_LIMIT_V7X),
        name="mem_kv",
    )(mem, g, b, wk, wv)


def _softmax_rows(s):
    m = jnp.max(s, axis=-1, keepdims=True)
    p = jnp.exp(s - m)
    return p / jnp.sum(p, axis=-1, keepdims=True)


def _attn_fwd(hb, wq, kb, vb):
    t, d = hb.shape
    tm = _row_tile(t)
    dh = d // X_HEADS
    scale = dh ** -0.5

    def body(h_ref, wq_ref, k_ref, v_ref, q_ref, o_ref):
        q = _dot(h_ref[...], wq_ref[...]).astype(BF16)
        q_ref[...] = q
        for hd in range(X_HEADS):
            sl = slice(hd * dh, (hd + 1) * dh)
            p = _softmax_rows(_dot_nt(q[:, sl], k_ref[:, sl]) * scale)
            o_ref[:, sl] = _dot(p.astype(BF16), v_ref[:, sl]).astype(BF16)

    row = pl.BlockSpec((tm, d), lambda i: (i, 0))
    full = lambda a: pl.BlockSpec(a.shape, lambda i: (0, 0))
    return pl.pallas_call(
        body,
        grid=(t // tm,),
        in_specs=[row, full(wq), full(kb), full(vb)],
        out_specs=[row, row],
        out_shape=[jax.ShapeDtypeStruct((t, d), BF16), jax.ShapeDtypeStruct((t, d), BF16)],
        compiler_params=_params(1),
        name="attn_fwd",
    )(hb, wq, kb, vb)


def _ffn_bwd_act(dyb, wd, a, b, coef, name, deps=()):
    t, d = dyb.shape
    f = wd.shape[0]
    tm = _row_tile(t)
    tn = _col_tile(f)

    def body(dy_ref, wd_ref, a_ref, b_ref, da_ref, db_ref):
        dy = dy_ref[...]
        for c in range(f // tn):
            cols = slice(c * tn, (c + 1) * tn)
            ds = _dot_nt(dy, wd_ref[cols, :]) * coef
            silu, dsilu = _silu_and_grad(a_ref[:, cols].astype(F32))
            da_ref[:, cols] = (ds * b_ref[:, cols].astype(F32) * dsilu).astype(BF16)
            db_ref[:, cols] = (ds * silu).astype(BF16)

    act = pl.BlockSpec((tm, f), lambda i: (i, 0))
    return pl.pallas_call(
        _drop_deps(body, 4, len(deps)),
        grid=(t // tm,),
        in_specs=[pl.BlockSpec((tm, d), lambda i: (i, 0)), _resident(wd), act, act] + [ANY] * len(deps),
        out_specs=[act, act],
        out_shape=[jax.ShapeDtypeStruct((t, f), BF16), jax.ShapeDtypeStruct((t, f), BF16)],
        compiler_params=_params(1),
        name=name,
    )(dyb, wd, a, b, *deps)


def _ffn_bwd_fused(dy, dyb, wd, wg, wu, a, b, coef, ln, name):
    t, d = dy.shape
    f = wd.shape[0]
    tm = min(t, 256)
    tn = _col_tile(f)

    def body(dy_ref, dyb_ref, wd_ref, wg_ref, wu_ref, a_ref, b_ref, xh_ref, rs_ref, g_ref,
             da_ref, db_ref, dyo_ref, dyob_ref, dg_ref, dbl_ref):
        dyb_v = dyb_ref[...]
        dh = ALPHA * dy_ref[...]
        for c in range(f // tn):
            cols = slice(c * tn, (c + 1) * tn)
            ds = _dot_nt(dyb_v, wd_ref[cols, :]) * coef
            silu, dsilu = _silu_and_grad(a_ref[:, cols].astype(F32))
            da = (ds * b_ref[:, cols].astype(F32) * dsilu).astype(BF16)
            db = (ds * silu).astype(BF16)
            da_ref[:, cols] = da
            db_ref[:, cols] = db
            dh = dh + _dot_nt(da, wg_ref[:, cols]) + _dot_nt(db, wu_ref[:, cols])

        @pl.when(pl.program_id(0) == 0)
        def _():
            dg_ref[...] = jnp.zeros_like(dg_ref)
            dbl_ref[...] = jnp.zeros_like(dbl_ref)

        dyp, dg, dbl = _ln_bwd(dh, xh_ref[...], rs_ref[...], g_ref[...])
        dyo_ref[...] = dyp
        dyob_ref[...] = dyp.astype(BF16)
        dg_ref[...] += dg
        dbl_ref[...] += dbl

    row = pl.BlockSpec((tm, d), lambda i: (i, 0))
    act = pl.BlockSpec((tm, f), lambda i: (i, 0))
    vec = pl.BlockSpec((1, d), lambda i: (0, 0))
    return pl.pallas_call(
        body,
        grid=(t // tm,),
        in_specs=[row, row, _resident(wd), _resident(wg), _resident(wu), act, act, row,
                  pl.BlockSpec((tm, 1), lambda i: (i, 0)), vec],
        out_specs=[act, act, row, row, vec, vec],
        out_shape=[jax.ShapeDtypeStruct((t, f), BF16), jax.ShapeDtypeStruct((t, f), BF16),
                   jax.ShapeDtypeStruct((t, d), F32), jax.ShapeDtypeStruct((t, d), BF16),
                   jax.ShapeDtypeStruct((1, d), F32), jax.ShapeDtypeStruct((1, d), F32)],
        compiler_params=_params(1),
        name=name,
    )(dy, dyb, wd, wg, wu, a, b, *ln)


def _mm_tn(a, b, name, scale=1.0, deps=()):
    t, m = a.shape
    bs = list(b) if isinstance(b, (list, tuple)) else [b]
    n = sum(piece.shape[1] for piece in bs)
    tt = _row_tile(t)
    nt = t // tt
    tm_o, tn_o = m, n

    def body(a_ref, *refs):
        b_refs, (o_ref, acc) = refs[:len(bs)], refs[len(bs):]
        k = pl.program_id(2)

        @pl.when(k == 0)
        def _():
            acc[...] = jnp.zeros_like(acc)

        first = 0
        for b_ref in b_refs:
            cols = slice(first, first + b_ref.shape[1])
            acc[:, cols] += _dot_tn(a_ref[...], b_ref[...])
            first = cols.stop

        @pl.when(k == nt - 1)
        def _():
            o_ref[...] = (acc[...] * scale).astype(BF16)

    return pl.pallas_call(
        _drop_deps(body, 1 + len(bs), len(deps)),
        grid=(m // tm_o, n // tn_o, nt),
        in_specs=[pl.BlockSpec((tt, tm_o), lambda i, j, k: (k, i))]
        + [pl.BlockSpec((tt, piece.shape[1]), lambda i, j, k: (k, j)) for piece in bs] + [ANY] * len(deps),
        out_specs=pl.BlockSpec((tm_o, tn_o), lambda i, j, k: (i, j)),
        out_shape=pltpu.HBM((m, n), BF16),
        scratch_shapes=[pltpu.VMEM((tm_o, tn_o), F32)],
        compiler_params=_params(3),
        name=name,
    )(a, *bs, *deps)


def _mm_nt(lhs, w, name):
    t, d = lhs.shape
    kd = w.shape[0]
    tm = _row_tile(t)

    def body(l_ref, w_ref, o_ref):
        _store_slabs(o_ref, 0, _dot_nt(l_ref[...], w_ref[...]))

    return pl.pallas_call(
        body,
        grid=(t // tm,),
        in_specs=[pl.BlockSpec((tm, d), lambda i: (i, 0)), _resident(w)],
        out_specs=pl.BlockSpec((kd // LANES, tm, LANES), lambda i: (0, i, 0)),
        out_shape=jax.ShapeDtypeStruct((kd // LANES, t, LANES), F32),
        compiler_params=_params(1),
        name=name,
    )(lhs, w)


def _dx_ln(dy, pairs, ln, name, deps=()):
    t, d = dy.shape
    npair = len(pairs)
    pairs = [(list(lhs) if isinstance(lhs, (list, tuple)) else [lhs], w) for lhs, w in pairs]
    tm = min(t, 512 // npair)
    nt = t // tm
    n_in = 1 + sum(len(pieces) + 1 for pieces, _ in pairs) + (3 if ln is not None else 0)

    def body(*refs):
        dy_ref = refs[0]
        pos = 1
        dh = ALPHA * dy_ref[...]
        for pieces, _ in pairs:
            w_ref = refs[pos + len(pieces)]
            first = 0
            for l_ref in refs[pos:pos + len(pieces)]:
                cols = slice(first, first + l_ref.shape[1])
                dh = dh + _dot_nt(l_ref[...], w_ref[:, cols])
                first = cols.stop
            pos += len(pieces) + 1
        if ln is not None:
            xh_ref, rs_ref, g_ref = refs[pos:pos + 3]
            dyo_ref, dyb_ref, dg_ref, db_ref = refs[pos + 3:pos + 7]

            @pl.when(pl.program_id(0) == 0)
            def _():
                dg_ref[...] = jnp.zeros_like(dg_ref)
                db_ref[...] = jnp.zeros_like(db_ref)

            dyp, dg, db = _ln_bwd(dh, xh_ref[...], rs_ref[...], g_ref[...])
            dyo_ref[...] = dyp
            dyb_ref[...] = dyp.astype(BF16)
            dg_ref[...] += dg
            db_ref[...] += db
        else:
            refs[pos][...] = dh

    row = pl.BlockSpec((tm, d), lambda i: (i, 0))
    vec = pl.BlockSpec((1, d), lambda i: (0, 0))
    in_specs = [row]
    args = [dy]
    for pieces, w in pairs:
        in_specs += [pl.BlockSpec((tm, piece.shape[1]), lambda i: (i, 0)) for piece in pieces] + [_resident(w)]
        args += pieces + [w]
    if ln is not None:
        in_specs += [row, pl.BlockSpec((tm, 1), lambda i: (i, 0)), vec]
        args += list(ln)
        out_specs = [row, row, vec, vec]
        out_shape = [jax.ShapeDtypeStruct((t, d), F32), jax.ShapeDtypeStruct((t, d), BF16),
                     jax.ShapeDtypeStruct((1, d), F32), jax.ShapeDtypeStruct((1, d), F32)]
    else:
        out_specs = row
        out_shape = jax.ShapeDtypeStruct((t, d), F32)
    return pl.pallas_call(
        _drop_deps(body, n_in, len(deps)),
        grid=(nt,),
        in_specs=in_specs + [ANY] * len(deps),
        out_specs=out_specs,
        out_shape=out_shape,
        compiler_params=_params(1),
        name=name,
    )(*args, *deps)


def _hgrn_bwd(proj, oraw, dmix, states, logits, gn):
    t = proj.shape[1]
    ct = _hg_tile(t)
    nct = t // ct
    nblk = ct // HG_BLOCK
    nh = HG_HEADS
    mrows = min(ct, 256)

    def body(q_ref, fz_ref, iv_ref, gg_ref, or_ref, do_ref, st_ref, lg_ref, gn_ref, mask_ref,
             dq_ref, dfz_ref, div_ref, dgg_ref, dlg_ref, dgn_ref,
             dstate, qt_s, kt_s, k_s, b_s, eb_s, ekb_s, dec_s, dor_s, dbl_s, gr_s, dk_s, dlb_acc):
        c = pl.program_id(1)

        @pl.when(c == 0)
        def _():
            dstate[...] = jnp.zeros_like(dstate)
            dlb_acc[...] = jnp.zeros_like(dlb_acc)
            dgn_ref[...] = jnp.zeros_like(dgn_ref)

        lb = _lower_bound(lg_ref[...])
        q = q_ref[...]
        sig, nsig, f, k = _forget_terms(fz_ref[...], lb)
        logf = jnp.log(f)
        b = _mask_dot(mask_ref[0], logf)
        bend = _mask_dot(mask_ref[2], logf)
        eb = jnp.exp(b)
        ekb = jnp.exp(bend - b)
        qt_s[...] = (q * eb).astype(BF16)
        kt_s[...] = (k * ekb).astype(BF16)
        k_s[...] = k
        b_s[...] = b
        eb_s[...] = eb
        ekb_s[...] = ekb
        dec_s[...] = jnp.exp(bend)
        oraw = or_ref[...]
        r = lax.rsqrt(jnp.mean(oraw * oraw, axis=-1, keepdims=True) + LN_EPS)
        on = oraw * r
        gg = gg_ref[...]
        silu, dsilu = _silu_and_grad(gg)
        doa = do_ref[...]
        gnv = gn_ref[...]
        dgg_ref[...] = (doa * on * gnv * dsilu).astype(BF16)
        dyn = doa * silu
        dgn_ref[...] += jnp.sum(dyn * on, axis=0, keepdims=True)
        don = dyn * gnv
        dor_s[...] = r * (don - on * jnp.mean(don * on, axis=-1, keepdims=True))
        tidx = lax.broadcasted_iota(jnp.int32, (HG_HALF, HG_DIM), 0)

        def blk(ii, carry):
            i = nblk - 1 - ii
            r0 = pl.multiple_of(i * HG_BLOCK, HG_BLOCK)
            rows = pl.ds(r0, HG_BLOCK)
            st = st_ref[i]
            dst = dstate[...]
            dstb = dst.astype(BF16)
            do = dor_s[rows, :]
            dob = do.astype(BF16)
            v = iv_ref[rows, :]
            vb = v.astype(BF16)
            qq = q_ref[rows, :]
            kk = k_s[rows, :]
            bb = b_s[rows, :]
            qt = qt_s[rows, :]
            kt = kt_s[rows, :]
            dec = dec_s[pl.ds(r0, 1), :]
            dkt = _dot(vb, dstb)
            dq = _dot(dob, st) * eb_s[rows, :]
            dk = dkt * ekb_s[rows, :]
            dv = _dot_nt(kt, dstb)
            gend = (jnp.sum(kk * dk, axis=0, keepdims=True)
                    + dec * jnp.sum(dst * st.astype(F32), axis=0, keepdims=True))
            qh, bh, doh = _halves(qq), _halves(bb), _halves(do)
            dqh, dkh, dvh = list(_halves(dq)), list(_halves(dk)), list(_halves(dv))
            for s in range(HG_BLOCK):
                ks, vs = kk[s:s + 1, :], v[s:s + 1, :]
                dk_part = dv_part = None
                for h in _causal_halves(s):
                    e = _decay_from(bh[h], bb[s:s + 1, :], s, h, tidx)
                    ke = ks * e
                    acol = jnp.sum(qh[h] * ke, axis=1, keepdims=True)
                    dacol = jnp.sum(doh[h] * vs, axis=1, keepdims=True)
                    dqh[h] = dqh[h] + dacol * ke
                    pk = dacol * (qh[h] * e)
                    pv = acol * doh[h]
                    dk_part = pk if dk_part is None else dk_part + pk
                    dv_part = pv if dv_part is None else dv_part + pv
                hs, row = divmod(s, HG_HALF)
                dkh[hs] = dkh[hs] + jnp.where(tidx == row, jnp.sum(dk_part, axis=0, keepdims=True), 0.0)
                dvh[hs] = dvh[hs] + jnp.where(tidx == row, jnp.sum(dv_part, axis=0, keepdims=True), 0.0)
            dq = jnp.concatenate(dqh, axis=0)
            dk = jnp.concatenate(dkh, axis=0)
            dv = jnp.concatenate(dvh, axis=0)
            dq_ref[rows, :] = dq.astype(BF16)
            div_ref[rows, :] = dv.astype(BF16)
            dk_s[rows, :] = dk
            dbl_s[rows, :] = qq * dq - kk * dk
            gr_s[rows, :] = jnp.zeros((HG_BLOCK, HG_DIM), F32) + gend
            dstate[...] = dst * dec + _dot_tn(dob, qt)
            return carry

        lax.fori_loop(0, nblk, blk, 0, unroll=HG_UNROLL)
        dlogf = _mask_dot(mask_ref[1], dbl_s[...]) + gr_s[...]
        dk = dk_s[...]
        dfz_ref[...] = ((dlogf / f - dk) * ((1.0 - lb) * sig * nsig)).astype(BF16)
        dlb_acc[...] += jnp.sum((dlogf / f - dk) * nsig, axis=0, keepdims=True)

        @pl.when(c == nct - 1)
        def _():
            dl0 = dlb_acc[...] * lb * (1.0 - lb)
            layer = lax.broadcasted_iota(jnp.int32, (2, HG_DIM), 0)
            dlg_ref[...] = jnp.where(layer == 0, dl0, -dl0)

    def slab(off):
        return pl.BlockSpec((None, ct, HG_DIM), lambda h, c: (off + h, nct - 1 - c, 0))

    out_slab = pl.BlockSpec((ct, HG_DIM), lambda h, c: (nct - 1 - c, h))
    tile_f32 = pltpu.VMEM((ct, HG_DIM), F32)
    tile_b16 = pltpu.VMEM((ct, HG_DIM), BF16)
    slab_shape = pltpu.HBM((t, nh * HG_DIM), BF16)
    return pl.pallas_call(
        body,
        grid=(nh, nct),
        in_specs=[slab(0), slab(nh), slab(2 * nh), slab(3 * nh), slab(0), slab(0),
                  pl.BlockSpec((None, nblk, HG_DIM, HG_DIM), lambda h, c: (h, nct - 1 - c, 0, 0)),
                  pl.BlockSpec((None, 2, HG_DIM), lambda h, c: (h, 0, 0)),
                  pl.BlockSpec((1, HG_DIM), lambda h, c: (0, 0)),
                  pl.BlockSpec((3, mrows, mrows), lambda h, c: (0, 0, 0))],
        out_specs=[out_slab, out_slab, out_slab, out_slab,
                   pl.BlockSpec((None, 2, HG_DIM), lambda h, c: (h, 0, 0)),
                   pl.BlockSpec((None, 1, HG_DIM), lambda h, c: (h, 0, 0))],
        out_shape=[slab_shape, slab_shape, slab_shape, slab_shape,
                   jax.ShapeDtypeStruct((nh, 2, HG_DIM), F32), jax.ShapeDtypeStruct((nh, 1, HG_DIM), F32)],
        scratch_shapes=[pltpu.VMEM((HG_DIM, HG_DIM), F32), tile_b16, tile_b16, tile_f32, tile_f32, tile_f32, tile_f32,
                        tile_f32, tile_f32, tile_f32, tile_f32, tile_f32, pltpu.VMEM((1, HG_DIM), F32)],
        compiler_params=_params(2),
        name="hgrn_bwd",
    )(proj, proj, proj, proj, oraw, dmix, states, logits, gn, _block_masks(mrows))


def _sgu_bwd(proj, dmix, ln_g, ln_b, w_s, w_t, b_col):
    t = proj.shape[1]
    ct = _sg_tile(t)
    nct = t // ct
    ng = SG_GROUPS
    off_u = 4 * HG_HEADS
    off_v = off_u + ng
    n = SG_CHUNK

    def body(u_ref, v_ref, do_ref, g_ref, b_ref, w_ref, wt_ref, bs_ref, du_ref, dv_ref, dg_ref, db_ref, dw_ref, dbs_ref):
        c = pl.program_id(1)

        @pl.when(c == 0)
        def _():
            dg_ref[...] = jnp.zeros_like(dg_ref)
            db_ref[...] = jnp.zeros_like(db_ref)
            dw_ref[...] = jnp.zeros_like(dw_ref)
            dbs_ref[...] = jnp.zeros_like(dbs_ref)

        r = lax.broadcasted_iota(jnp.int32, (n, n), 0)
        cc = lax.broadcasted_iota(jnp.int32, (n, n), 1)
        wm = jnp.where(cc <= r, w_ref[...], 0.0).astype(BF16)
        wmt = jnp.where(r <= cc, wt_ref[...], 0.0).astype(BF16)
        for ci in range(ct // n):
            rows = slice(ci * n, (ci + 1) * n)
            ua, dua, dva, vn, xhat, rstd, s = _sgu_chunk_fwd(u_ref[rows, :], v_ref[rows, :], g_ref[...], b_ref[...],
                                                             wm, bs_ref[...])
            do = do_ref[rows, :]
            du_ref[rows, :] = (do * s * dua).astype(BF16)
            ds = do * ua
            dsb = ds.astype(BF16)
            dbs_ref[...] += jnp.sum(ds, axis=1, keepdims=True)
            dw_ref[...] += _dot_nt(dsb, vn.astype(BF16))
            dvn = _dot(wmt, dsb)
            dva_in, dg, db = _ln_bwd(dvn, xhat, rstd, g_ref[...])
            dg_ref[...] += dg
            db_ref[...] += db
            dv_ref[rows, :] = (dva_in * dva).astype(BF16)

        @pl.when(c == nct - 1)
        def _():
            dw_ref[...] = jnp.where(cc <= r, dw_ref[...], 0.0)

    vec = pl.BlockSpec((None, 1, SG_DIM), lambda g, c: (g, 0, 0))
    mat = pl.BlockSpec((None, n, n), lambda g, c: (g, 0, 0))
    col = pl.BlockSpec((None, n, 1), lambda g, c: (g, 0, 0))
    out_slab = pl.BlockSpec((ct, SG_DIM), lambda g, c: (c, g))
    return pl.pallas_call(
        body,
        grid=(ng, nct),
        in_specs=[pl.BlockSpec((None, ct, SG_DIM), lambda g, c: (off_u + g, c, 0)),
                  pl.BlockSpec((None, ct, SG_DIM), lambda g, c: (off_v + g, c, 0)),
                  pl.BlockSpec((None, ct, SG_DIM), lambda g, c: (ng + g, c, 0)), vec, vec, mat, mat, col],
        out_specs=[out_slab, out_slab, vec, vec, mat, col],
        out_shape=[pltpu.HBM((t, ng * SG_DIM), BF16), pltpu.HBM((t, ng * SG_DIM), BF16),
                   jax.ShapeDtypeStruct((ng, 1, SG_DIM), F32), jax.ShapeDtypeStruct((ng, 1, SG_DIM), F32),
                   jax.ShapeDtypeStruct((ng, n, n), F32), jax.ShapeDtypeStruct((ng, n, 1), F32)],
        compiler_params=_params(2),
        name="sgu_bwd",
    )(proj, proj, dmix, ln_g, ln_b, w_s, w_t, b_col)


def _attn_bwd(dyb, wo, qb, kb, vb):
    t, d = dyb.shape
    m_len = kb.shape[0]
    tm = _row_tile(t)
    dh = d // X_HEADS
    scale = dh ** -0.5

    def body(dy_ref, wo_ref, q_ref, k_ref, v_ref, dq_ref, dk_ref, dv_ref):
        i = pl.program_id(0)

        @pl.when(i == 0)
        def _():
            dk_ref[...] = jnp.zeros_like(dk_ref)
            dv_ref[...] = jnp.zeros_like(dv_ref)

        do = _dot_nt(dy_ref[...], wo_ref[...]).astype(BF16)
        for hd in range(X_HEADS):
            sl = slice(hd * dh, (hd + 1) * dh)
            qh = q_ref[:, sl]
            p = _softmax_rows(_dot_nt(qh, k_ref[:, sl]) * scale)
            doh = do[:, sl]
            dp = _dot_nt(doh, v_ref[:, sl])
            ds = (p * (dp - jnp.sum(dp * p, axis=-1, keepdims=True)) * scale).astype(BF16)
            dq_ref[:, sl] = _dot(ds, k_ref[:, sl]).astype(BF16)
            dk_ref[:, sl] += _dot_tn(ds, qh)
            dv_ref[:, sl] += _dot_tn(p.astype(BF16), doh)

    row = pl.BlockSpec((tm, d), lambda i: (i, 0))
    full = lambda a: pl.BlockSpec(a.shape, lambda i: (0, 0))
    kv = pl.BlockSpec((m_len, d), lambda i: (0, 0))
    return pl.pallas_call(
        body,
        grid=(t // tm,),
        in_specs=[row, full(wo), row, full(kb), full(vb)],
        out_specs=[row, kv, kv],
        out_shape=[jax.ShapeDtypeStruct((t, d), BF16), jax.ShapeDtypeStruct((m_len, d), F32),
                   jax.ShapeDtypeStruct((m_len, d), F32)],
        compiler_params=_params(1),
        name="attn_bwd",
    )(dyb, wo, qb, kb, vb)


def _mem_bwd(dk, dv, mb, xhat, rstd, g, wk, wv):
    m_len, d = dk.shape

    def body(dk_ref, dv_ref, mb_ref, xh_ref, rs_ref, g_ref, wk_ref, wv_ref, gwk_ref, gwv_ref, dg_ref, db_ref):
        dkb = dk_ref[...].astype(BF16)
        dvb = dv_ref[...].astype(BF16)
        mb_v = mb_ref[...]
        gwk_ref[...] = _dot_tn(mb_v, dkb).astype(BF16)
        gwv_ref[...] = _dot_tn(mb_v, dvb).astype(BF16)
        dm = _dot_nt(dkb, wk_ref[...]) + _dot_nt(dvb, wv_ref[...])
        _, dg, db = _ln_bwd(dm, xh_ref[...], rs_ref[...], g_ref[...])
        dg_ref[...] = dg
        db_ref[...] = db

    return pl.pallas_call(
        body,
        out_shape=[jax.ShapeDtypeStruct((d, d), BF16), jax.ShapeDtypeStruct((d, d), BF16),
                   jax.ShapeDtypeStruct((1, d), F32), jax.ShapeDtypeStruct((1, d), F32)],
        compiler_params=pltpu.CompilerParams(vmem_limit_bytes=VMEM_LIMIT_V7X),
        name="mem_bwd",
    )(dk, dv, mb, xhat, rstd, g, wk, wv)


def _adamw(w, g, m, v):
    m = ADAM_B1 * m + (1.0 - ADAM_B1) * g
    v = ADAM_B2 * v + (1.0 - ADAM_B2) * (g * g)
    m_hat = m / (1.0 - ADAM_B1 ** ADAM_STEP)
    v_hat = v / (1.0 - ADAM_B2 ** ADAM_STEP)
    delta = -ADAM_LR * (m_hat / (jnp.sqrt(v_hat) + ADAM_EPS) + ADAM_WD * w)
    return delta, m, v


def _slot_sum(ref):
    g = ref[0].astype(F32)
    for s in range(1, N_DEV):
        g = g + ref[s].astype(F32)
    return g


def _adam_sharded(lands, w, m, v, axis, name):
    rows, cols = w.shape
    nl = len(lands)
    transposed = axis == 1 and nl == 2
    if transposed:
        rows, cols = cols, rows
        tr = 256
        grid = (rows // tr,)
        wblk = pl.BlockSpec((cols, tr), lambda i: (0, i))
        lblk = [pl.BlockSpec((N_DEV, tr, a.shape[2]), lambda i: (0, i, 0)) for a in lands]
    elif axis == 1:
        tr = 256 if rows % 256 == 0 else rows
        grid = (rows // tr,)
        wblk = pl.BlockSpec((tr, cols), lambda i: (i, 0))
        lblk = [pl.BlockSpec((N_DEV, tr, a.shape[2]), lambda i: (0, i, 0)) for a in lands]
    else:
        tc = _col_tile(cols)
        grid = (cols // tc,)
        wblk = pl.BlockSpec((rows, tc), lambda i: (0, i))
        lblk = [pl.BlockSpec((N_DEV, a.shape[1], tc), lambda i: (0, 0, i)) for a in lands]

    def body(*refs):
        w_ref, m_ref, v_ref = refs[nl:nl + 3]
        g_ref, d_ref, nm_ref, nv_ref = refs[nl + 3:]
        g = _slot_sum(refs[0])
        if nl == 2:
            tail = _slot_sum(refs[1])
            if transposed:
                g = jnp.concatenate([g.T, tail.T[:cols - g.shape[1], :]], axis=0)
            elif axis == 1:
                g = jnp.concatenate([g, tail[:, :cols - g.shape[1]]], axis=1)
            else:
                g = jnp.concatenate([g, tail[:rows - g.shape[0], :]], axis=0)
        delta, nm, nv = _adamw(w_ref[...], g, m_ref[...], v_ref[...])
        g_ref[...] = g
        d_ref[...] = delta
        nm_ref[...] = nm
        nv_ref[...] = nv

    shp = pltpu.HBM(w.shape, F32)
    return pl.pallas_call(
        body,
        grid=grid,
        in_specs=lblk + [wblk, wblk, wblk],
        out_specs=[wblk, wblk, wblk, wblk],
        out_shape=[shp, shp, shp, shp],
        compiler_params=_params(1),
        name=name,
    )(*[pltpu.with_memory_space_constraint(a, pltpu.HBM) for a in (*lands, w, m, v)])


def _mesh_pos():
    return lax.axis_index("x"), lax.axis_index("y"), lax.axis_index("c")


def _peer(k):
    x, y, c = _mesh_pos()
    pos = (x ^ (k >> 2), y ^ ((k >> 1) & 1), c ^ (k & 1))
    return pos, 4 * pos[0] + 2 * pos[1] + pos[2]


def _sem_index(row, k):
    return row * (N_DEV - 1) + k - 1


def _window(ref, axis, start, size):
    align = 16 if axis == 0 else LANES
    start = pl.multiple_of(start, align)
    return ref.at[pl.ds(start, size), :] if axis == 0 else ref.at[:, pl.ds(start, size)]


def _piece_refs(piece, srcs, lands, me, peer):
    kind, si, li, axis, base, stride, shape = piece
    if kind == "gather":
        return srcs[si], _window(lands[li], axis, base + stride * me, shape[axis])
    return _window(srcs[si], axis, base + stride * peer, shape[axis]), lands[li].at[me]


def _place_own(srcs, land_shapes, pieces, name):
    ns, nl, npc = len(srcs), len(land_shapes), len(pieces)

    def body(*refs):
        s_refs = refs[:ns]
        l_refs = refs[ns:ns + nl]
        bufs = refs[ns + nl:ns + nl + npc]
        sems = refs[ns + nl + npc]
        x, y, c = _mesh_pos()
        me = 4 * x + 2 * y + c
        loads = []
        for p, piece in enumerate(pieces):
            src, dst = _piece_refs(piece, s_refs, l_refs, me, me)
            cp = pltpu.make_async_copy(src, bufs[p], sems.at[0, p])
            cp.start()
            loads.append((cp, dst))
        stores = []
        for p, (cp, dst) in enumerate(loads):
            cp.wait()
            out = pltpu.make_async_copy(bufs[p], dst, sems.at[1, p])
            out.start()
            stores.append(out)
        for out in stores:
            out.wait()

    out = pl.pallas_call(
        body,
        in_specs=[ANY] * ns,
        out_specs=[HBM] * nl,
        out_shape=[pltpu.HBM(s.shape, s.dtype) for s in land_shapes],
        scratch_shapes=[pltpu.VMEM(pc[6], srcs[pc[1]].dtype) for pc in pieces] + [pltpu.SemaphoreType.DMA((2, npc))],
        compiler_params=pltpu.CompilerParams(vmem_limit_bytes=VMEM_LIMIT_V7X),
        name=name,
    )(*srcs)
    return list(out)


def _comm_start(srcs, lands, pieces, groups, name, after=()):
    ns, nl, na, ng = len(srcs), len(lands), len(after), len(groups)

    def body(*refs):
        s_refs = refs[:ns]
        l_refs = refs[ns:ns + nl]
        outs = refs[ns + nl + na:]
        sems = outs[:2 * ng]
        token = outs[-1]
        x, y, c = _mesh_pos()
        me = 4 * x + 2 * y + c
        for g, members in enumerate(groups):
            for row, p in enumerate(members):
                for k in range(1, N_DEV):
                    pos, peer = _peer(k)
                    src, dst = _piece_refs(pieces[p], s_refs, l_refs, me, peer)
                    pltpu.make_async_remote_copy(src_ref=src, dst_ref=dst, send_sem=sems[2 * g].at[_sem_index(row, k)],
                                                 recv_sem=sems[2 * g + 1].at[_sem_index(row, k)], device_id=pos,
                                                 device_id_type=MESH_ID).start()
        token[...] = jnp.zeros_like(token)

    sem_shapes = []
    for members in groups:
        sem_shapes += [pltpu.SemaphoreType.DMA((len(members) * (N_DEV - 1),))] * 2
    hbm_of = lambda a: pltpu.HBM(a.shape, a.dtype)
    out = pl.pallas_call(
        body,
        in_specs=[HBM] * (ns + nl) + [ANY] * na,
        out_specs=[SEM] * (2 * ng) + [HBM] * (ns + nl) + [pl.BlockSpec(memory_space=pltpu.VMEM)],
        out_shape=sem_shapes + [hbm_of(a) for a in srcs] + [hbm_of(a) for a in lands]
        + [jax.ShapeDtypeStruct((8, LANES), F32)],
        input_output_aliases={i: 2 * ng + i for i in range(ns + nl)},
        compiler_params=pltpu.CompilerParams(has_side_effects=DATAFLOW),
        name=name,
    )(*[pltpu.with_memory_space_constraint(a, pltpu.HBM) for a in list(srcs) + list(lands)], *after)
    sems = [(out[2 * g], out[2 * g + 1]) for g in range(ng)]
    return sems, list(out[2 * ng:2 * ng + ns]), list(out[2 * ng + ns:2 * ng + ns + nl]), out[-1]


def _comm_wait(srcs, lands, pieces, members, sems, after, name):
    ns, nl, na = len(srcs), len(lands), len(after)

    def body(*refs):
        s_refs = refs[:ns]
        l_refs = refs[ns:ns + nl]
        send_sems, recv_sems = refs[ns + nl:ns + nl + 2]
        x, y, c = _mesh_pos()
        me = 4 * x + 2 * y + c
        for row, p in enumerate(members):
            for k in range(1, N_DEV):
                pos, peer = _peer(k)
                src, dst = _piece_refs(pieces[p], s_refs, l_refs, me, peer)
                cp = pltpu.make_async_remote_copy(src_ref=src, dst_ref=dst, send_sem=send_sems.at[_sem_index(row, k)],
                                                  recv_sem=recv_sems.at[_sem_index(row, k)], device_id=pos,
                                                  device_id_type=MESH_ID)
                cp.wait_send()
                cp.wait_recv()

    hbm_of = lambda a: pltpu.HBM(a.shape, a.dtype)
    out = pl.pallas_call(
        body,
        in_specs=[HBM] * (ns + nl) + [SEM, SEM] + [ANY] * na,
        out_specs=[HBM] * (ns + nl),
        out_shape=[hbm_of(a) for a in srcs] + [hbm_of(a) for a in lands],
        input_output_aliases={i: i for i in range(ns + nl)},
        compiler_params=pltpu.CompilerParams(has_side_effects=DATAFLOW),
        name=name,
    )(*srcs, *lands, sems[0], sems[1], *after)
    return list(out[ns:])


def _landed_block(piece, lands, owner):
    _, _, li, axis, base, stride, shape = piece
    return _window(lands[li], axis, base + stride * owner, shape[axis])


def _copy_stage(name, bufs, in_sems, out_sem_sizes, emit, after=()):
    nb, ni, no, na = len(bufs), len(in_sems), len(out_sem_sizes), len(after)

    def body(*refs):
        b_refs = refs[:nb]
        i_refs = refs[nb:nb + ni]
        o_refs = refs[nb + ni + na:nb + ni + na + no]
        emit(b_refs, i_refs, o_refs)
        refs[-1][...] = jnp.zeros_like(refs[-1])

    hbm_of = lambda a: pltpu.HBM(a.shape, a.dtype)
    out = pl.pallas_call(
        body,
        in_specs=[HBM] * nb + [SEM] * ni + [ANY] * na,
        out_specs=[SEM] * no + [HBM] * nb + [pl.BlockSpec(memory_space=pltpu.VMEM)],
        out_shape=[pltpu.SemaphoreType.DMA((n,)) for n in out_sem_sizes] + [hbm_of(a) for a in bufs]
        + [jax.ShapeDtypeStruct((8, LANES), F32)],
        input_output_aliases={i: no + i for i in range(nb)},
        compiler_params=pltpu.CompilerParams(has_side_effects=DATAFLOW),
        name=name,
    )(*[pltpu.with_memory_space_constraint(a, pltpu.HBM) for a in bufs], *in_sems, *after)
    return list(out[:no]), list(out[no:no + nb]), out[-1]


def _remote(src, dst, send, recv, to):
    return pltpu.make_async_remote_copy(src_ref=src, dst_ref=dst, send_sem=send, recv_sem=recv, device_id=to,
                                        device_id_type=MESH_ID)


def _routed_gather(srcs, lands, pieces, meanwhile, name):
    ns, npc = len(srcs), len(pieces)

    def places():
        x, y, c = _mesh_pos()
        index = lambda p: 4 * p[0] + 2 * p[1] + p[2]
        me, sib = (x, y, c), (x, y, 1 - c)
        xnb, ynb = (1 - x, y, c), (x, 1 - y, c)
        got_first = (x ^ (1 - c), y ^ c, c)
        pass_to = (x ^ c, y ^ (1 - c), c)
        diag = (1 - x, 1 - y, c)
        return index, me, sib, xnb, ynb, got_first, pass_to, diag

    def start(b, _, o):
        index, me, sib, xnb, ynb, *_rest = places()
        send_a, recv_sib, recv_nb = o
        for p, piece in enumerate(pieces):
            src, dst = _piece_refs(piece, b[:ns], b[ns:], index(me), 0)
            _remote(src, dst, send_a.at[3 * p], recv_sib.at[p], sib).start()
            _remote(src, dst, send_a.at[3 * p + 1], recv_nb.at[2 * p], xnb).start()
            _remote(src, dst, send_a.at[3 * p + 2], recv_nb.at[2 * p + 1], ynb).start()

    def pass_a(b, i, o):
        index, me, sib, xnb, ynb, got_first, pass_to, _diag = places()
        (recv_nb,) = i
        send_f, recv_f, send_d, recv_d = o
        for p, piece in enumerate(pieces):
            for j, nb in enumerate((xnb, ynb)):
                blk = _landed_block(piece, b, index(nb))
                _remote(blk, blk, send_f.at[2 * p + j], recv_nb.at[2 * p + j], sib).wait_recv()
                _remote(blk, blk, send_f.at[2 * p + j], recv_f.at[2 * p + j], sib).start()
            blk = _landed_block(piece, b, index(got_first))
            _remote(blk, blk, send_d.at[p], recv_d.at[p], pass_to).start()

    def pass_b(b, i, o):
        index, me, sib, *_mid, diag = places()
        (recv_d,) = i
        send_g, recv_g = o
        for p, piece in enumerate(pieces):
            blk = _landed_block(piece, b, index(diag))
            _remote(blk, blk, send_g.at[p], recv_d.at[p], sib).wait_recv()
            _remote(blk, blk, send_g.at[p], recv_g.at[p], sib).start()

    def last(b, i, _):
        index, me, sib, *_others = places()
        send_a, recv_sib, send_f, recv_f, send_d, send_g, recv_g = i
        for p, piece in enumerate(pieces):
            src, dst = _piece_refs(piece, b[:ns], b[ns:], index(me), 0)
            cp = lambda s_sem, r_sem: _remote(src, dst, s_sem, r_sem, sib)
            cp(send_a.at[3 * p], recv_sib.at[p]).wait_recv()
            cp(send_a.at[3 * p], recv_g.at[p]).wait_recv()
            for j in range(3):
                cp(send_a.at[3 * p + j], recv_sib.at[p]).wait_send()
            for j in range(2):
                cp(send_f.at[2 * p + j], recv_f.at[2 * p + j]).wait_recv()
                cp(send_f.at[2 * p + j], recv_f.at[2 * p + j]).wait_send()
            cp(send_d.at[p], recv_sib.at[p]).wait_send()
            cp(send_g.at[p], recv_sib.at[p]).wait_send()

    (send_a, recv_sib, recv_nb), bufs, started = _copy_stage(name + "_start", list(srcs) + list(lands), [],
                                                             [3 * npc, npc, 2 * npc], start)
    srcs, lands = bufs[:ns], bufs[ns:]
    (send_f, recv_f, send_d, recv_d), lands, _ = _copy_stage(name + "_pass_a", lands, [recv_nb],
                                                             [2 * npc, 2 * npc, npc, npc],
                                                             lambda b, i, o: pass_a(b, i, o), after=meanwhile(started))
    (send_g, recv_g), lands, tok = _copy_stage(name + "_pass_b", lands, [recv_d], [npc, npc], pass_b)
    _, bufs, _ = _copy_stage(name + "_last", list(srcs) + list(lands),
                             [send_a, recv_sib, send_f, recv_f, send_d, send_g, recv_g], [], last)
    return bufs[ns:], tok


_SMALL_NAMES = ("ln1_g", "ln1_b", "hg_lb_logits", "hg_norm_g", "sg_ln_g", "sg_ln_b", "sg_w_s", "sg_b_s",
                "ln2_g", "ln2_b", "mem_ln_g", "mem_ln_b", "ln3_g", "ln3_b", "ln4_g", "ln4_b")


_VEC_NAMES = ("ln1_g", "ln1_b", "ln2_g", "ln2_b", "mem_ln_g", "mem_ln_b", "ln3_g", "ln3_b", "ln4_g", "ln4_b")
_ROW_NAMES = ("hg_lb_logits", "hg_norm_g", "sg_ln_g", "sg_ln_b", "sg_b_s", "sg_w_s")
VEC_ROWS = 16


def _row_plan(shapes):
    plan, pos = {}, 0
    for k in _ROW_NAMES:
        shp = shapes[k]
        slabs, off = [], pos
        for idx in itertools.product(*[range(dim) for dim in shp[:-2]]):
            slabs.append((idx, off, shp[-2]))
            off += shp[-2]
        plan[k] = (pos, slabs)
        pos = -(-off // 8) * 8
    return plan, -(-pos // 16) * 16


def _pack_small_grads(gs, shapes, loss):
    d = gs[_VEC_NAMES[0]].size
    vec = jnp.concatenate([gs[k].reshape(1, -1) for k in _VEC_NAMES] + [jnp.tile(loss, (1, d // LANES))], axis=0)
    vec = jnp.pad(vec, ((0, VEC_ROWS - vec.shape[0]), (0, 0)))
    plan, total = _row_plan(shapes)
    parts, pos = [], 0
    for k in _ROW_NAMES:
        first, slabs = plan[k]
        rows = gs[k].reshape(-1, LANES)
        end = slabs[-1][1] + slabs[-1][2]
        nxt = -(-end // 8) * 8
        parts.append(jnp.pad(rows, ((0, nxt - first - rows.shape[0]), (0, 0))))
        pos = nxt
    parts.append(jnp.zeros((total - pos, LANES), F32))
    return vec, jnp.concatenate(parts, axis=0)


def _adam_small(land_vec, land_rows, w, m, v):
    names = _VEC_NAMES + _ROW_NAMES
    n = len(names)
    shapes = {k: w[k].shape for k in names}
    plan, _ = _row_plan(shapes)

    def body(*refs):
        lv_ref, lr_ref = refs[:2]
        w_refs, m_refs, v_refs = refs[2:2 + n], refs[2 + n:2 + 2 * n], refs[2 + 2 * n:2 + 3 * n]
        outs = refs[2 + 3 * n:2 + 7 * n]
        loss_ref = refs[2 + 7 * n]
        gv_s, gr_s = refs[3 + 7 * n:]
        gv_s[...] = _slot_sum(lv_ref)
        gr_s[...] = _slot_sum(lr_ref)
        loss_ref[...] = gv_s[len(_VEC_NAMES):len(_VEC_NAMES) + 1, :LANES]
        for p, k in enumerate(names):
            if k in _VEC_NAMES:
                row = _VEC_NAMES.index(k)
                slabs = [((), None, None)]
            else:
                slabs = plan[k][1]
            for idx, off, rows in slabs:
                g = gv_s[row:row + 1, :] if off is None else gr_s[off:off + rows, :]
                sel = idx + (slice(None), slice(None))
                delta, nm, nv = _adamw(w_refs[p][sel], g, m_refs[p][sel], v_refs[p][sel])
                for o, val in zip(range(4), (g, delta, nm, nv)):
                    outs[o * n + p][sel] = val

    flat = lambda tree: [tree[k] for k in names]
    shp = [jax.ShapeDtypeStruct(shapes[k], F32) for k in names]
    out = pl.pallas_call(
        body,
        out_shape=shp * 4 + [jax.ShapeDtypeStruct((1, LANES), F32)],
        scratch_shapes=[pltpu.VMEM(land_vec.shape[1:], F32), pltpu.VMEM(land_rows.shape[1:], F32)],
        name="adam_small",
    )(land_vec, land_rows, *flat(w), *flat(m), *flat(v))
    return [dict(zip(names, out[o * n:(o + 1) * n])) for o in range(4)], out[4 * n]


_COL_FFN = ("ffn1_w_gate", "ffn1_w_up", "ffn2_w_gate", "ffn2_w_up")
_ROW_FFN = ("ffn1_w_down", "ffn2_w_down")
_ROW_SQ = ("w_out", "xa_w_q", "xa_w_k", "xa_w_v", "xa_w_o")
_BIG_NAMES = ("ffn1_w_gate", "ffn1_w_up", "ffn1_w_down", "w_in", "w_out", "xa_w_q", "xa_w_k", "xa_w_v", "xa_w_o",
              "ffn2_w_gate", "ffn2_w_up", "ffn2_w_down")


def _ffn_split(fs):
    main = (fs // MXU_WIDTH_V7X) * MXU_WIDTH_V7X
    tail = fs - main
    tail_pad = -(-tail // LANES) * LANES
    assert main > 0 and tail > 0
    return main, tail, tail_pad


def _layout(name, shard_shape):
    r, c = shard_shape
    if name in _COL_FFN:
        main, tail, pad = _ffn_split(c)
        return (r, N_DEV * (main + pad)), [(1, 0, main, (r, main), (0, main)),
                                           (1, N_DEV * main, pad, (r, pad), (main, c))]
    if name in _ROW_FFN:
        main, tail, pad = _ffn_split(r)
        return (N_DEV * (main + pad), c), [(0, 0, main, (main, c), (0, main)),
                                           (0, N_DEV * main, pad, (pad, c), (main, r))]
    if name == "w_in":
        return (r, N_DEV * c), [(1, 0, c, (r, c), (0, c))]
    return (N_DEV * r, c), [(0, 0, r, (r, c), (0, r))]


def _shard_pieces(name, shard):
    out = []
    for axis, _, _, shape, (lo, hi) in _layout(name, shard.shape)[1]:
        part = shard[lo:hi, :] if axis == 0 else shard[:, lo:hi]
        pad = [(0, shape[0] - part.shape[0]), (0, shape[1] - part.shape[1])]
        out.append(jnp.pad(part, pad).astype(BF16))
    return out


def _gather_plan(names, shards):
    srcs, land_shapes, pieces, index = [], [], [], {}
    for li, name in enumerate(names):
        shape2d, parts = _layout(name, shards[name].shape)
        land_shapes.append(jax.ShapeDtypeStruct(shape2d, BF16))
        index[name] = []
        for (axis, base, stride, shape, _), src in zip(parts, _shard_pieces(name, shards[name])):
            index[name].append(len(pieces))
            pieces.append(("gather", len(srcs), li, axis, base, stride, shape))
            srcs.append(src)
    return srcs, land_shapes, pieces, index


def _scatter_plan(names, grads, shard_shapes):
    srcs, land_shapes, pieces, index = [], [], [], {}
    for si, name in enumerate(names):
        _, parts = _layout(name, shard_shapes[name])
        srcs.append(grads[name])
        index[name] = []
        for axis, base, stride, shape, _ in parts:
            index[name].append(len(land_shapes))
            pieces.append(("scatter", si, len(land_shapes), axis, base, stride, shape))
            land_shapes.append(jax.ShapeDtypeStruct((N_DEV,) + shape, grads[name].dtype))
    return srcs, land_shapes, pieces, index


def _small_views(small):
    row = lambda a: a.reshape(1, -1)
    ln = {k: row(small[k]) for k in ("ln1_g", "ln1_b", "ln2_g", "ln2_b", "ln3_g", "ln3_b", "ln4_g", "ln4_b",
                                      "mem_ln_g", "mem_ln_b", "hg_norm_g")}
    sg_w = small["sg_w_s"].reshape(SG_GROUPS, SG_CHUNK, SG_CHUNK)
    sg = dict(logits=jnp.swapaxes(small["hg_lb_logits"], 0, 1),
              g=small["sg_ln_g"].reshape(SG_GROUPS, 1, SG_DIM), b=small["sg_ln_b"].reshape(SG_GROUPS, 1, SG_DIM),
              w=sg_w, wt=jnp.swapaxes(sg_w, 1, 2), bs=small["sg_b_s"].reshape(SG_GROUPS, SG_CHUNK, 1))
    return ln, sg


def _forward(x, xb, mem, target, get_w, small, first_deps=()):
    ln, sg = _small_views(small)
    a1, b1, s1 = _ffn_up(xb, get_w("ffn1_w_gate", ()), get_w("ffn1_w_up", ()), "ffn1_up", deps=first_deps)
    h1b, xh1, rs1 = _mm_res_ln(s1, get_w("ffn1_w_down", (s1,)), x, ln["ln1_g"], ln["ln1_b"], 0.5, "ffn1_down_ln")
    proj = _mm_nn(h1b, get_w("w_in", (h1b,)), "mix_in")
    oraw, mix, states = _hgrn_fwd(proj, sg["logits"], ln["hg_norm_g"])
    mix = _sgu_fwd(proj, mix, sg["g"], sg["b"], sg["w"], sg["bs"])
    h2b, xh2, rs2 = _mm_res_ln(mix, get_w("w_out", (mix,)), (xh1, ln["ln1_g"], ln["ln1_b"]), ln["ln2_g"], ln["ln2_b"],
                               1.0, "mix_out_ln")
    mb, mxh, mrs, kb, vb = _mem_kv(mem, ln["mem_ln_g"], ln["mem_ln_b"], get_w("xa_w_k", (h2b,)), get_w("xa_w_v", (h2b,)))
    qb, att = _attn_fwd(h2b, get_w("xa_w_q", (mrs,)), kb, vb)
    h3b, xh3, rs3 = _mm_res_ln(att, get_w("xa_w_o", (att,)), (xh2, ln["ln2_g"], ln["ln2_b"]), ln["ln3_g"], ln["ln3_b"],
                               1.0, "attn_out_ln")
    a2, b2, s2 = _ffn_up(h3b, get_w("ffn2_w_gate", (h3b,)), get_w("ffn2_w_up", (h3b,)), "ffn2_up")
    loss, dy4, dy4b, dg4, db4 = _mm_res_ln(s2, get_w("ffn2_w_down", (s2,)), (xh3, ln["ln3_g"], ln["ln3_b"]),
                                           ln["ln4_g"], ln["ln4_b"], 0.5, "ffn2_down_ln_loss", target=target)
    return dict(xb=xb, a1=a1, b1=b1, s1=s1, h1b=h1b, xh1=xh1, rs1=rs1, proj=proj, oraw=oraw, mix=mix, states=states,
                h2b=h2b, xh2=xh2, rs2=rs2, mb=mb, mxh=mxh, mrs=mrs, kb=kb, vb=vb, qb=qb, att=att, h3b=h3b, xh3=xh3,
                rs3=rs3, a2=a2, b2=b2, s2=s2, loss=loss, dy4=dy4, dy4b=dy4b, dg4=dg4, db4=db4)


def _backward(sv, wt, small, send):
    ln, sg = _small_views(small)
    gs = {"ln4_g": sv["dg4"], "ln4_b": sv["db4"]}
    loss, dy4, dy4b = sv["loss"], sv["dy4"], sv["dy4b"]
    g_down2 = _mm_tn(sv["s2"], dy4b, "g_ffn2_down", scale=0.5)
    da2, db2, dy3, dy3b, gs["ln3_g"], gs["ln3_b"] = _ffn_bwd_fused(
        dy4, dy4b, wt["ffn2_w_down"], wt["ffn2_w_gate"], wt["ffn2_w_up"], sv["a2"], sv["b2"], 0.5,
        (sv["xh3"], sv["rs3"], ln["ln3_g"]), "ffn2_bwd")
    g_gate2 = _mm_tn(sv["h3b"], da2, "g_ffn2_gate")
    g_up2 = _mm_tn(sv["h3b"], db2, "g_ffn2_up")
    tok = send(("ffn2_w_down", "ffn2_w_gate", "ffn2_w_up"), (g_down2, g_gate2, g_up2))

    g_o = _mm_tn(sv["att"], dy3b, "g_xa_o", deps=(tok,))
    dqb, dk, dv = _attn_bwd(dy3b, wt["xa_w_o"], sv["qb"], sv["kb"], sv["vb"])
    g_q = _mm_tn(sv["h2b"], dqb, "g_xa_q")
    g_k, g_v, gs["mem_ln_g"], gs["mem_ln_b"] = _mem_bwd(dk, dv, sv["mb"], sv["mxh"], sv["mrs"], ln["mem_ln_g"],
                                                        wt["xa_w_k"], wt["xa_w_v"])
    tok = send(("xa_w_o", "xa_w_q", "xa_w_k", "xa_w_v"), (g_o, g_q, g_k, g_v))
    dy2, dy2b, gs["ln2_g"], gs["ln2_b"] = _dx_ln(dy3, [(dqb, wt["xa_w_q"])], (sv["xh2"], sv["rs2"], ln["ln2_g"]),
                                                 "attn_dx_ln", deps=(tok,))

    g_out = _mm_tn(sv["mix"], dy2b, "g_w_out")
    dmix = _mm_nt(dy2b, wt["w_out"], "mix_out_bwd")
    dq, dfz, div, dgg, dlg, dgn = _hgrn_bwd(sv["proj"], sv["oraw"], dmix, sv["states"], sg["logits"], ln["hg_norm_g"])
    du, dvv, gs["sg_ln_g"], gs["sg_ln_b"], gs["sg_w_s"], gs["sg_b_s"] = _sgu_bwd(
        sv["proj"], dmix, sg["g"], sg["b"], sg["w"], sg["wt"], sg["bs"])
    gs["hg_lb_logits"] = jnp.swapaxes(dlg, 0, 1)
    gs["hg_norm_g"] = jnp.sum(dgn, axis=0)
    dproj = [dq, dfz, div, dgg, du, dvv]
    g_in = _mm_tn(sv["h1b"], dproj, "g_w_in")
    tok = send(("w_out", "w_in"), (g_out, g_in))
    dy1, dy1b, gs["ln1_g"], gs["ln1_b"] = _dx_ln(dy2, [(dproj, wt["w_in"])], (sv["xh1"], sv["rs1"], ln["ln1_g"]),
                                                 "mix_dx_ln", deps=(tok,))

    g_down1 = _mm_tn(sv["s1"], dy1b, "g_ffn1_down", scale=0.5)
    tok = send(("ffn1_w_down",), (g_down1,))
    da1, db1 = _ffn_bwd_act(dy1b, wt["ffn1_w_down"], sv["a1"], sv["b1"], 0.5, "ffn1_bwd_act", deps=(tok,))
    g_gate1 = _mm_tn(sv["xb"], da1, "g_ffn1_gate")
    tok = send(("ffn1_w_gate",), (g_gate1,))
    g_up1 = _mm_tn(sv["xb"], db1, "g_ffn1_up", deps=(tok,))
    tok = send(("ffn1_w_up",), (g_up1,))
    grad_x = _dx_ln(dy1, [(da1, wt["ffn1_w_gate"]), (db1, wt["ffn1_w_up"])], None, "ffn1_dx", deps=(tok,))
    return loss, grad_x, gs


_WEIGHT_NAMES = ("ffn1_w_gate", "ffn1_w_up", "ffn1_w_down", "ln1_g", "ln1_b", "w_in", "hg_lb_logits", "hg_norm_g",
                 "sg_ln_g", "sg_ln_b", "sg_w_s", "sg_b_s", "w_out", "ln2_g", "ln2_b", "mem_ln_g", "mem_ln_b",
                 "xa_w_q", "xa_w_k", "xa_w_v", "xa_w_o", "ln3_g", "ln3_b", "ffn2_w_gate", "ffn2_w_up", "ffn2_w_down",
                 "ln4_g", "ln4_b")
_FIRST = ("ffn1_w_gate", "ffn1_w_up")
_SECOND = ("ffn1_w_down", "w_in", "w_out")
_THIRD = ("xa_w_k", "xa_w_v", "xa_w_q", "xa_w_o", "ffn2_w_gate", "ffn2_w_up", "ffn2_w_down")


def kernel(x, mem, ffn1_w_gate, ffn1_w_up, ffn1_w_down, ln1_g, ln1_b, w_in, hg_lb_logits, hg_norm_g, sg_ln_g, sg_ln_b, sg_w_s, sg_b_s, w_out, ln2_g, ln2_b, mem_ln_g, mem_ln_b, xa_w_q, xa_w_k, xa_w_v, xa_w_o, ln3_g, ln3_b, ffn2_w_gate, ffn2_w_up, ffn2_w_down, ln4_g, ln4_b, loss_target, m_ffn1_w_gate, m_ffn1_w_up, m_ffn1_w_down, m_ln1_g, m_ln1_b, m_w_in, m_hg_lb_logits, m_hg_norm_g, m_sg_ln_g, m_sg_ln_b, m_sg_w_s, m_sg_b_s, m_w_out, m_ln2_g, m_ln2_b, m_mem_ln_g, m_mem_ln_b, m_xa_w_q, m_xa_w_k, m_xa_w_v, m_xa_w_o, m_ln3_g, m_ln3_b, m_ffn2_w_gate, m_ffn2_w_up, m_ffn2_w_down, m_ln4_g, m_ln4_b, v_ffn1_w_gate, v_ffn1_w_up, v_ffn1_w_down, v_ln1_g, v_ln1_b, v_w_in, v_hg_lb_logits, v_hg_norm_g, v_sg_ln_g, v_sg_ln_b, v_sg_w_s, v_sg_b_s, v_w_out, v_ln2_g, v_ln2_b, v_mem_ln_g, v_mem_ln_b, v_xa_w_q, v_xa_w_k, v_xa_w_v, v_xa_w_o, v_ln3_g, v_ln3_b, v_ffn2_w_gate, v_ffn2_w_up, v_ffn2_w_down, v_ln4_g, v_ln4_b):
    args = dict(locals())
    w = {k: args[k] for k in _WEIGHT_NAMES}
    m = {k: args["m_" + k] for k in _WEIGHT_NAMES}
    v = {k: args["v_" + k] for k in _WEIGHT_NAMES}
    shards = {k: w[k][0] for k in _BIG_NAMES}
    shard_shapes = {k: shards[k].shape for k in _BIG_NAMES}
    small = {k: (w[k][0] if k != "hg_lb_logits" else w[k]) for k in _SMALL_NAMES}

    srcs1, shapes1, pieces1, idx1 = _gather_plan(_FIRST, shards)
    lands1 = _place_own(srcs1, shapes1, pieces1, "gather_first_own")
    rest = _SECOND + _THIRD
    prepared = {}

    def prepare_rest(started):
        later = {k: shards[k] + started[0, 0] for k in rest}
        srcs2, shapes2, pieces2, idx2 = _gather_plan(rest, later)
        lands2 = _place_own(srcs2, shapes2, pieces2, "gather_rest_own")
        xb = _to_bf16(x[0], "x_bf16", deps=(started,))
        prepared.update(srcs=srcs2, pieces=pieces2, idx=idx2, lands=lands2, xb=xb)
        return tuple(lands2) + (xb,)

    lands1, tok1 = _routed_gather(srcs1, lands1, pieces1, prepare_rest, "gather_first")
    srcs2, pieces2, idx2, lands2 = (prepared[k] for k in ("srcs", "pieces", "idx", "lands"))
    groups2 = [list(idx2[k]) for k in rest]
    sems2, srcs2, lands2, tok2 = _comm_start(srcs2, lands2, pieces2, groups2, "gather_rest_start", after=(tok1,))
    wt = dict(zip(_FIRST, lands1))
    pending = {k: gi for gi, k in enumerate(rest)}

    def get_w(name, after):
        if name in pending:
            gi = pending.pop(name)
            si = [pieces2[p][1] for p in groups2[gi]]
            sub = [(pieces2[p][0], row, 0) + pieces2[p][3:] for row, p in enumerate(groups2[gi])]
            wt[name] = _comm_wait([srcs2[s] for s in si], [lands2[gi]], sub, list(range(len(sub))), sems2[gi],
                                  after, "gather_wait_" + name)[0]
        return wt[name]

    sv = _forward(x[0], prepared["xb"], mem[0], loss_target[0], get_w, small, first_deps=(tok2,))

    sent = []

    def send(names, grads):
        srcs, shapes, pieces, idx = _scatter_plan(names, dict(zip(names, grads)), shard_shapes)
        lands = _place_own(srcs, shapes, pieces, "grads_own_%d" % len(sent))
        sems, srcs, lands, tok = _comm_start(srcs, lands, pieces, [list(range(len(pieces)))],
                                             "grads_start_%d" % len(sent))
        sent.append((names, srcs, lands, pieces, idx, sems[0]))
        return tok

    loss, grad_x, gs = _backward(sv, wt, small, send)

    ssrc = list(_pack_small_grads(gs, {k: w[k].shape for k in _SMALL_NAMES}, loss))
    sp = [("scatter", i, i, 0, 0, 0, a.shape) for i, a in enumerate(ssrc)]
    sshape = [jax.ShapeDtypeStruct((N_DEV,) + a.shape, F32) for a in ssrc]
    sl = _place_own(ssrc, sshape, sp, "small_own")
    ssem, ssrc, sl, _ = _comm_start(ssrc, sl, sp, [[0, 1]], "small_start")

    out_g, out_d, out_m, out_v = {}, {}, {}, {}
    after = (grad_x,)
    for n_sent, (names, srcs, lands, pieces, idx, sems) in enumerate(sent):
        lands = _comm_wait(srcs, lands, pieces, list(range(len(pieces))), sems, after, "grads_wait_%d" % n_sent)
        for k in names:
            axis = 1 if (k in _COL_FFN or k == "w_in") else 0
            if k in _COL_FFN:
                done = _adam_sharded([lands[i] for i in idx[k]], w[k][0].T, m[k][0].T, v[k][0].T, axis, "adam_" + k)
                res = [r.T for r in done]
            else:
                res = done = _adam_sharded([lands[i] for i in idx[k]], w[k][0], m[k][0], v[k][0], axis, "adam_" + k)
            out_g[k], out_d[k], out_m[k], out_v[k] = [r[None] for r in res]
        after = (done[3],)
    sl = _comm_wait(ssrc, sl, sp, [0, 1], ssem[0], after, "small_wait")
    small_out, loss_sum = _adam_small(sl[0], sl[1], w, m, v)
    for dst, res in zip((out_g, out_d, out_m, out_v), small_out):
        dst.update(res)
    loss_all = loss_sum[0, 0]
    return (loss_all, grad_x[None], *[out_g[k] for k in _WEIGHT_NAMES], *[out_d[k] for k in _WEIGHT_NAMES],
            *[out_m[k] for k in _WEIGHT_NAMES], *[out_v[k] for k in _WEIGHT_NAMES])
```

```python
import itertools

import jax
import jax.numpy as jnp
import numpy as np
from jax import lax
from jax.experimental import pallas as pl
from jax.experimental.pallas import tpu as pltpu

F32 = jnp.float32
BF16 = jnp.bfloat16

N_DEV = 8
ALPHA = 2.0 ** 0.25
LN_EPS = 1e-5
HG_HEADS = 4
HG_DIM = 128
SG_GROUPS = 4
SG_DIM = 128
SG_CHUNK = 128
X_HEADS = 4
HG_BLOCK = 16
HG_UNROLL = 16
ADAM_LR = 0.001
ADAM_B1 = 0.9
ADAM_B2 = 0.999
ADAM_EPS = 1e-08
ADAM_WD = 0.01
ADAM_STEP = 10
VMEM_LIMIT_V7X = 48 * 1024 * 1024
MXU_WIDTH_V7X = 256
LANES = 128
MESH_ID = pl.DeviceIdType.MESH
ANY = pl.BlockSpec(memory_space=pl.ANY)
HBM = pl.BlockSpec(memory_space=pltpu.HBM)
SEM = pl.BlockSpec(memory_space=pltpu.SEMAPHORE)
DATAFLOW = pltpu.SideEffectType.DATAFLOW_SIDE_EFFECTING


def _params(n_axes):
    return pltpu.CompilerParams(dimension_semantics=("arbitrary",) * n_axes, vmem_limit_bytes=VMEM_LIMIT_V7X)


def _dot(a, b):
    return jnp.dot(a, b, preferred_element_type=F32)


def _dot_nt(a, b):
    return lax.dot_general(a, b, (((1,), (1,)), ((), ())), preferred_element_type=F32)


def _dot_tn(a, b):
    return lax.dot_general(a, b, (((0,), (0,)), ((), ())), preferred_element_type=F32)


def _sigmoid(x):
    return 1.0 / (1.0 + jnp.exp(-x))


def _silu_and_grad(a):
    sig = _sigmoid(a)
    return a * sig, sig * (1.0 + a * (1.0 - sig))


_GELU_C = 0.7978845608028654


def _gelu_and_grad(x):
    inner = _GELU_C * (x + 0.044715 * x * x * x)
    t = jnp.tanh(inner)
    val = 0.5 * x * (1.0 + t)
    grad = 0.5 * (1.0 + t) + 0.5 * x * (1.0 - t * t) * _GELU_C * (1.0 + 3.0 * 0.044715 * x * x)
    return val, grad


def _ln_fwd(y, g, b):
    mu = jnp.mean(y, axis=-1, keepdims=True)
    yc = y - mu
    var = jnp.mean(yc * yc, axis=-1, keepdims=True)
    rstd = lax.rsqrt(var + LN_EPS)
    xhat = yc * rstd
    return xhat * g + b, xhat, rstd


def _ln_bwd(dh, xhat, rstd, g):
    dxh = dh * g
    m1 = jnp.mean(dxh, axis=-1, keepdims=True)
    m2 = jnp.mean(dxh * xhat, axis=-1, keepdims=True)
    dy = rstd * (dxh - m1 - xhat * m2)
    dg = jnp.sum(dh * xhat, axis=0, keepdims=True)
    db = jnp.sum(dh, axis=0, keepdims=True)
    return dy, dg, db


def _mask_dot(mask, x):
    hi = x.astype(BF16)
    lo = (x - hi.astype(F32)).astype(BF16)
    n = mask.shape[0]
    parts = [_dot(mask, hi[r:r + n, :]) + _dot(mask, lo[r:r + n, :]) for r in range(0, x.shape[0], n)]
    return parts[0] if len(parts) == 1 else jnp.concatenate(parts, axis=0)


def _block_masks(n):
    r = np.arange(n)[:, None]
    c = np.arange(n)[None, :]
    same = (r // HG_BLOCK) == (c // HG_BLOCK)
    return jnp.asarray(np.stack([same & (c <= r), same & (c >= r), same]), BF16)


def _row_tile(t):
    return min(t, 512)


def _col_tile(n):
    for cand in (512, 256, 128):
        if n % cand == 0:
            return cand
    return n


def _resident(w):
    return pl.BlockSpec(w.shape, lambda *_: (0, 0), pipeline_mode=pl.Buffered(1))


def _drop_deps(body, n_in, n_deps):
    if n_deps == 0:
        return body
    return lambda *refs: body(*refs[:n_in], *refs[n_in + n_deps:])


def _to_bf16(x, name, deps=()):
    t, d = x.shape
    tm = _row_tile(t)

    def body(x_ref, o_ref):
        o_ref[...] = x_ref[...].astype(BF16)

    row = pl.BlockSpec((tm, d), lambda i: (i, 0))
    return pl.pallas_call(
        _drop_deps(body, 1, len(deps)),
        grid=(t // tm,),
        in_specs=[row] + [ANY] * len(deps),
        out_specs=row,
        out_shape=jax.ShapeDtypeStruct((t, d), BF16),
        compiler_params=_params(1),
        name=name,
    )(x, *deps)


def _ffn_up(hb, wg, wu, name, deps=()):
    t, d = hb.shape
    f = wg.shape[1]
    tm = _row_tile(t)
    tn = _col_tile(f)

    def body(h_ref, wg_ref, wu_ref, a_ref, b_ref, s_ref):
        h = h_ref[...]
        for c in range(f // tn):
            cols = slice(c * tn, (c + 1) * tn)
            a = _dot(h, wg_ref[:, cols])
            b = _dot(h, wu_ref[:, cols])
            a_ref[:, cols] = a.astype(BF16)
            b_ref[:, cols] = b.astype(BF16)
            s_ref[:, cols] = (a * _sigmoid(a) * b).astype(BF16)

    act = pl.BlockSpec((tm, f), lambda i: (i, 0))
    return pl.pallas_call(
        _drop_deps(body, 3, len(deps)),
        grid=(t // tm,),
        in_specs=[pl.BlockSpec((tm, d), lambda i: (i, 0)), _resident(wg), _resident(wu)] + [ANY] * len(deps),
        out_specs=[act, act, act],
        out_shape=[jax.ShapeDtypeStruct((t, f), BF16)] * 3,
        compiler_params=_params(1),
        name=name,
    )(hb, wg, wu, *deps)


def _mm_res_ln(lhs, w, res, g, b, coef, name, target=None):
    t, kd = lhs.shape
    d = w.shape[1]
    tm = _row_tile(t)
    nt = t // tm
    from_norm = isinstance(res, tuple)
    n_res = 3 if from_norm else 1

    def body(*refs):
        l_ref, w_ref = refs[:2]
        r_refs = refs[2:2 + n_res]
        g_ref, b_ref = refs[2 + n_res:4 + n_res]
        rest = refs[4 + n_res:]
        prev = r_refs[0][...] * r_refs[1][...] + r_refs[2][...] if from_norm else r_refs[0][...]
        y = ALPHA * prev + coef * _dot(l_ref[...], w_ref[...])
        h, xhat, rstd = _ln_fwd(y, g_ref[...], b_ref[...])
        if target is None:
            hb_ref, xh_ref, rs_ref = rest
            hb_ref[...] = h.astype(BF16)
            xh_ref[...] = xhat
            rs_ref[...] = rstd
            return
        t_ref, loss_ref, dy_ref, dyb_ref, dg_ref, db_ref, lacc = rest
        i = pl.program_id(0)

        @pl.when(i == 0)
        def _():
            lacc[...] = jnp.zeros_like(lacc)
            dg_ref[...] = jnp.zeros_like(dg_ref)
            db_ref[...] = jnp.zeros_like(db_ref)

        err = h - t_ref[...]
        lacc[...] += jnp.sum(err * err, axis=0, keepdims=True)
        dy, dg, db = _ln_bwd(err * (1.0 / d), xhat, rstd, g_ref[...])
        dy_ref[...] = dy
        dyb_ref[...] = dy.astype(BF16)
        dg_ref[...] += dg
        db_ref[...] += db

        @pl.when(i == nt - 1)
        def _():
            loss_ref[...] = jnp.zeros_like(loss_ref) + jnp.sum(lacc[...], axis=1, keepdims=True) * (0.5 / d)

    row = pl.BlockSpec((tm, d), lambda i: (i, 0))
    vec = pl.BlockSpec((1, d), lambda i: (0, 0))
    res_specs = [row, vec, vec] if from_norm else [row]
    res_args = list(res) if from_norm else [res]
    in_specs = [pl.BlockSpec((tm, kd), lambda i: (i, 0)), _resident(w)] + res_specs + [vec, vec]
    args = [lhs, w] + res_args + [g, b]
    if target is None:
        out_specs = [row, row, pl.BlockSpec((tm, 1), lambda i: (i, 0))]
        out_shape = [jax.ShapeDtypeStruct((t, d), BF16), jax.ShapeDtypeStruct((t, d), F32), pltpu.HBM((t, 1), F32)]
        scratch = []
    else:
        in_specs.append(row)
        args.append(target)
        out_specs = [pl.BlockSpec((1, LANES), lambda i: (0, 0)), row, row, vec, vec]
        out_shape = [jax.ShapeDtypeStruct((1, LANES), F32), jax.ShapeDtypeStruct((t, d), F32),
                     jax.ShapeDtypeStruct((t, d), BF16), jax.ShapeDtypeStruct((1, d), F32),
                     jax.ShapeDtypeStruct((1, d), F32)]
        scratch = [pltpu.VMEM((1, d), F32)]
    return pl.pallas_call(
        body,
        grid=(nt,),
        in_specs=in_specs,
        out_specs=out_specs,
        out_shape=out_shape,
        scratch_shapes=scratch,
        compiler_params=_params(1),
        name=name,
    )(*args)


def _store_slabs(o_ref, first, tile):
    for s in range(tile.shape[1] // LANES):
        o_ref[first + s] = tile[:, s * LANES:(s + 1) * LANES]


def _mm_nn(lhs, w, name):
    t, kd = lhs.shape
    n = w.shape[1]
    tm = _row_tile(t)
    tn = _col_tile(n)

    def body(l_ref, w_ref, o_ref):
        lhs_v = l_ref[...]
        for c in range(n // tn):
            _store_slabs(o_ref, c * (tn // LANES), _dot(lhs_v, w_ref[:, c * tn:(c + 1) * tn]))

    return pl.pallas_call(
        body,
        grid=(t // tm,),
        in_specs=[pl.BlockSpec((tm, kd), lambda i: (i, 0)), _resident(w)],
        out_specs=pl.BlockSpec((n // LANES, tm, LANES), lambda i: (0, i, 0)),
        out_shape=jax.ShapeDtypeStruct((n // LANES, t, LANES), F32),
        compiler_params=_params(1),
        name=name,
    )(lhs, w)


def _lower_bound(lg):
    m = jnp.max(lg, axis=0, keepdims=True)
    e = jnp.exp(lg - m)
    return e[0:1, :] / jnp.sum(e, axis=0, keepdims=True)


def _forget_terms(fz, lb):
    e = jnp.exp(-jnp.abs(fz))
    r = 1.0 / (1.0 + e)
    pos = fz >= 0.0
    sig = jnp.where(pos, r, e * r)
    nsig = jnp.where(pos, e * r, r)
    f = lb + (1.0 - lb) * sig
    k = (1.0 - lb) * nsig
    return sig, nsig, f, k


def _hg_tile(t):
    return min(t, 1024)


HG_HALF = HG_BLOCK // 2
NEG_BIG = -1e30


def _halves(a):
    return a[:HG_HALF, :], a[HG_HALF:, :]


def _causal_halves(s):
    return (0, 1) if s < HG_HALF else (1,)


def _decay_from(b_half, b_s, s, h, tidx):
    first = s - h * HG_HALF
    diff = b_half - b_s
    if first > 0:
        diff = jnp.where(tidx >= first, diff, NEG_BIG)
    return jnp.exp(diff)


def _hgrn_fwd(proj, logits, gn):
    t = proj.shape[1]
    ct = _hg_tile(t)
    nct = t // ct
    nblk = ct // HG_BLOCK
    nh = HG_HEADS
    mrows = min(ct, 256)

    def body(q_ref, fz_ref, iv_ref, gg_ref, lg_ref, gn_ref, mask_ref, oraw_ref, oa_ref, st_ref,
             state, qt_s, kt_s, k_s, b_s, dec_s):
        c = pl.program_id(1)

        @pl.when(c == 0)
        def _():
            state[...] = jnp.zeros_like(state)

        lb = _lower_bound(lg_ref[...])
        q = q_ref[...]
        _, _, f, k = _forget_terms(fz_ref[...], lb)
        logf = jnp.log(f)
        b = _mask_dot(mask_ref[0], logf)
        bend = _mask_dot(mask_ref[2], logf)
        qt_s[...] = (q * jnp.exp(b)).astype(BF16)
        kt_s[...] = (k * jnp.exp(bend - b)).astype(BF16)
        k_s[...] = k
        b_s[...] = b
        dec_s[...] = jnp.exp(bend)
        tidx = lax.broadcasted_iota(jnp.int32, (HG_HALF, HG_DIM), 0)

        def blk(i, carry):
            r0 = pl.multiple_of(i * HG_BLOCK, HG_BLOCK)
            rows = pl.ds(r0, HG_BLOCK)
            st = state[...]
            stb = st.astype(BF16)
            st_ref[i] = stb
            v = iv_ref[rows, :]
            qq = q_ref[rows, :]
            kk = k_s[rows, :]
            bb = b_s[rows, :]
            o = list(_halves(_dot_nt(qt_s[rows, :], stb)))
            qh, bh = _halves(qq), _halves(bb)
            for s in range(HG_BLOCK):
                ks, vs = kk[s:s + 1, :], v[s:s + 1, :]
                for h in _causal_halves(s):
                    e = _decay_from(bh[h], bb[s:s + 1, :], s, h, tidx)
                    acol = jnp.sum(qh[h] * (ks * e), axis=1, keepdims=True)
                    o[h] = o[h] + acol * vs
            oraw_ref[rows, :] = jnp.concatenate(o, axis=0)
            state[...] = st * dec_s[pl.ds(r0, 1), :] + _dot_tn(v.astype(BF16), kt_s[rows, :])
            return carry

        lax.fori_loop(0, nblk, blk, 0, unroll=HG_UNROLL)
        oraw = oraw_ref[...]
        r = lax.rsqrt(jnp.mean(oraw * oraw, axis=-1, keepdims=True) + LN_EPS)
        gg = gg_ref[...]
        oa_ref[...] = (oraw * r * gn_ref[...] * gg * _sigmoid(gg)).astype(BF16)

    def slab(off):
        return pl.BlockSpec((None, ct, HG_DIM), lambda h, c: (off + h, c, 0))

    return pl.pallas_call(
        body,
        grid=(nh, nct),
        in_specs=[slab(0), slab(nh), slab(2 * nh), slab(3 * nh),
                  pl.BlockSpec((None, 2, HG_DIM), lambda h, c: (h, 0, 0)),
                  pl.BlockSpec((1, HG_DIM), lambda h, c: (0, 0)),
                  pl.BlockSpec((3, mrows, mrows), lambda h, c: (0, 0, 0))],
        out_specs=[slab(0), pl.BlockSpec((ct, HG_DIM), lambda h, c: (c, h)),
                   pl.BlockSpec((None, nblk, HG_DIM, HG_DIM), lambda h, c: (h, c, 0, 0))],
        out_shape=[jax.ShapeDtypeStruct((nh, t, HG_DIM), F32),
                   jax.ShapeDtypeStruct((t, (nh + SG_GROUPS) * HG_DIM), BF16),
                   jax.ShapeDtypeStruct((nh, t // HG_BLOCK, HG_DIM, HG_DIM), BF16)],
        scratch_shapes=[pltpu.VMEM((HG_DIM, HG_DIM), F32), pltpu.VMEM((ct, HG_DIM), BF16),
                        pltpu.VMEM((ct, HG_DIM), BF16), pltpu.VMEM((ct, HG_DIM), F32),
                        pltpu.VMEM((ct, HG_DIM), F32), pltpu.VMEM((ct, HG_DIM), F32)],
        compiler_params=_params(2),
        name="hgrn_fwd",
    )(proj, proj, proj, proj, logits, gn, _block_masks(mrows))


def _sg_tile(t):
    return min(t, 512)


def _sgu_chunk_fwd(u, v, ln_g, ln_b, wm, bs):
    ua, dua = _gelu_and_grad(u)
    va, dva = _gelu_and_grad(v)
    vn, xhat, rstd = _ln_fwd(va, ln_g, ln_b)
    s = _dot(wm, vn.astype(BF16)) + bs
    return ua, dua, dva, vn, xhat, rstd, s


def _tril_weight(w):
    n = SG_CHUNK
    r = lax.broadcasted_iota(jnp.int32, (n, n), 0)
    c = lax.broadcasted_iota(jnp.int32, (n, n), 1)
    return jnp.where(c <= r, w, 0.0)


def _sgu_fwd(proj, mix, ln_g, ln_b, w_s, b_col):
    t = proj.shape[1]
    ct = _sg_tile(t)
    ng = SG_GROUPS
    wide = ng * SG_DIM
    blk_u = 4 * HG_HEADS // ng

    def body(u_ref, v_ref, g_ref, b_ref, w_ref, bs_ref, mix_ref, o_ref):
        del mix_ref
        for g in range(ng):
            lanes = slice(g * SG_DIM, (g + 1) * SG_DIM)
            wm = _tril_weight(w_ref[g]).astype(BF16)
            for n in range(ct // SG_CHUNK):
                rows = slice(n * SG_CHUNK, (n + 1) * SG_CHUNK)
                ua, _, _, _, _, _, s = _sgu_chunk_fwd(u_ref[g, rows, :], v_ref[g, rows, :], g_ref[g], b_ref[g], wm,
                                                      bs_ref[g])
                o_ref[rows, lanes] = (ua * s).astype(BF16)

    full = lambda a: pl.BlockSpec(a.shape, lambda c: (0,) * a.ndim)
    return pl.pallas_call(
        body,
        grid=(t // ct,),
        in_specs=[pl.BlockSpec((ng, ct, SG_DIM), lambda c: (blk_u, c, 0)),
                  pl.BlockSpec((ng, ct, SG_DIM), lambda c: (blk_u + 1, c, 0)),
                  full(ln_g), full(ln_b), full(w_s), full(b_col), ANY],
        out_specs=pl.BlockSpec((ct, wide), lambda c: (c, 1)),
        out_shape=jax.ShapeDtypeStruct(mix.shape, mix.dtype),
        input_output_aliases={6: 0},
        compiler_params=_params(1),
        name="sgu_fwd",
    )(proj, proj, ln_g, ln_b, w_s, b_col, mix)


def _mem_kv(mem, g, b, wk, wv):
    m_len, d = mem.shape

    def body(m_ref, g_ref, b_ref, wk_ref, wv_ref, mb_ref, xh_ref, rs_ref, k_ref, v_ref):
        m, xhat, rstd = _ln_fwd(m_ref[...], g_ref[...], b_ref[...])
        mb = m.astype(BF16)
        mb_ref[...] = mb
        xh_ref[...] = xhat
        rs_ref[...] = rstd
        k_ref[...] = _dot(mb, wk_ref[...]).astype(BF16)
        v_ref[...] = _dot(mb, wv_ref[...]).astype(BF16)

    return pl.pallas_call(
        body,
        out_shape=[jax.ShapeDtypeStruct((m_len, d), BF16), jax.ShapeDtypeStruct((m_len, d), F32),
                   jax.ShapeDtypeStruct((m_len, 1), F32), jax.ShapeDtypeStruct((m_len, d), BF16),
                   jax.ShapeDtypeStruct((m_len, d), BF16)],
        compiler_params=pltpu.CompilerParams(vmem_limit_bytes=VMEM_LIMIT_V7X),
        name="mem_kv",
    )(mem, g, b, wk, wv)


def _softmax_rows(s):
    m = jnp.max(s, axis=-1, keepdims=True)
    p = jnp.exp(s - m)
    return p / jnp.sum(p, axis=-1, keepdims=True)


def _attn_fwd(hb, wq, kb, vb):
    t, d = hb.shape
    tm = _row_tile(t)
    dh = d // X_HEADS
    scale = dh ** -0.5

    def body(h_ref, wq_ref, k_ref, v_ref, q_ref, o_ref):
        q = _dot(h_ref[...], wq_ref[...]).astype(BF16)
        q_ref[...] = q
        for hd in range(X_HEADS):
            sl = slice(hd * dh, (hd + 1) * dh)
            p = _softmax_rows(_dot_nt(q[:, sl], k_ref[:, sl]) * scale)
            o_ref[:, sl] = _dot(p.astype(BF16), v_ref[:, sl]).astype(BF16)

    row = pl.BlockSpec((tm, d), lambda i: (i, 0))
    full = lambda a: pl.BlockSpec(a.shape, lambda i: (0, 0))
    return pl.pallas_call(
        body,
        grid=(t // tm,),
        in_specs=[row, full(wq), full(kb), full(vb)],
        out_specs=[row, row],
        out_shape=[jax.ShapeDtypeStruct((t, d), BF16), jax.ShapeDtypeStruct((t, d), BF16)],
        compiler_params=_params(1),
        name="attn_fwd",
    )(hb, wq, kb, vb)


def _ffn_bwd_act(dyb, wd, a, b, coef, name, deps=()):
    t, d = dyb.shape
    f = wd.shape[0]
    tm = _row_tile(t)
    tn = _col_tile(f)

    def body(dy_ref, wd_ref, a_ref, b_ref, da_ref, db_ref):
        dy = dy_ref[...]
        for c in range(f // tn):
            cols = slice(c * tn, (c + 1) * tn)
            ds = _dot_nt(dy, wd_ref[cols, :]) * coef
            silu, dsilu = _silu_and_grad(a_ref[:, cols].astype(F32))
            da_ref[:, cols] = (ds * b_ref[:, cols].astype(F32) * dsilu).astype(BF16)
            db_ref[:, cols] = (ds * silu).astype(BF16)

    act = pl.BlockSpec((tm, f), lambda i: (i, 0))
    return pl.pallas_call(
        _drop_deps(body, 4, len(deps)),
        grid=(t // tm,),
        in_specs=[pl.BlockSpec((tm, d), lambda i: (i, 0)), _resident(wd), act, act] + [ANY] * len(deps),
        out_specs=[act, act],
        out_shape=[jax.ShapeDtypeStruct((t, f), BF16), jax.ShapeDtypeStruct((t, f), BF16)],
        compiler_params=_params(1),
        name=name,
    )(dyb, wd, a, b, *deps)


def _ffn_bwd_fused(dy, dyb, wd, wg, wu, a, b, coef, ln, name):
    t, d = dy.shape
    f = wd.shape[0]
    tm = min(t, 256)
    tn = _col_tile(f)

    def body(dy_ref, dyb_ref, wd_ref, wg_ref, wu_ref, a_ref, b_ref, xh_ref, rs_ref, g_ref,
             da_ref, db_ref, dyo_ref, dyob_ref, dg_ref, dbl_ref):
        dyb_v = dyb_ref[...]
        dh = ALPHA * dy_ref[...]
        for c in range(f // tn):
            cols = slice(c * tn, (c + 1) * tn)
            ds = _dot_nt(dyb_v, wd_ref[cols, :]) * coef
            silu, dsilu = _silu_and_grad(a_ref[:, cols].astype(F32))
            da = (ds * b_ref[:, cols].astype(F32) * dsilu).astype(BF16)
            db = (ds * silu).astype(BF16)
            da_ref[:, cols] = da
            db_ref[:, cols] = db
            dh = dh + _dot_nt(da, wg_ref[:, cols]) + _dot_nt(db, wu_ref[:, cols])

        @pl.when(pl.program_id(0) == 0)
        def _():
            dg_ref[...] = jnp.zeros_like(dg_ref)
            dbl_ref[...] = jnp.zeros_like(dbl_ref)

        dyp, dg, dbl = _ln_bwd(dh, xh_ref[...], rs_ref[...], g_ref[...])
        dyo_ref[...] = dyp
        dyob_ref[...] = dyp.astype(BF16)
        dg_ref[...] += dg
        dbl_ref[...] += dbl

    row = pl.BlockSpec((tm, d), lambda i: (i, 0))
    act = pl.BlockSpec((tm, f), lambda i: (i, 0))
    vec = pl.BlockSpec((1, d), lambda i: (0, 0))
    return pl.pallas_call(
        body,
        grid=(t // tm,),
        in_specs=[row, row, _resident(wd), _resident(wg), _resident(wu), act, act, row,
                  pl.BlockSpec((tm, 1), lambda i: (i, 0)), vec],
        out_specs=[act, act, row, row, vec, vec],
        out_shape=[jax.ShapeDtypeStruct((t, f), BF16), jax.ShapeDtypeStruct((t, f), BF16),
                   jax.ShapeDtypeStruct((t, d), F32), jax.ShapeDtypeStruct((t, d), BF16),
                   jax.ShapeDtypeStruct((1, d), F32), jax.ShapeDtypeStruct((1, d), F32)],
        compiler_params=_params(1),
        name=name,
    )(dy, dyb, wd, wg, wu, a, b, *ln)


def _mm_tn(a, b, name, scale=1.0, deps=()):
    t, m = a.shape
    bs = list(b) if isinstance(b, (list, tuple)) else [b]
    n = sum(piece.shape[1] for piece in bs)
    tt = _row_tile(t)
    nt = t // tt
    tm_o, tn_o = m, n

    def body(a_ref, *refs):
        b_refs, (o_ref, acc) = refs[:len(bs)], refs[len(bs):]
        k = pl.program_id(2)

        @pl.when(k == 0)
        def _():
            acc[...] = jnp.zeros_like(acc)

        first = 0
        for b_ref in b_refs:
            cols = slice(first, first + b_ref.shape[1])
            acc[:, cols] += _dot_tn(a_ref[...], b_ref[...])
            first = cols.stop

        @pl.when(k == nt - 1)
        def _():
            o_ref[...] = (acc[...] * scale).astype(BF16)

    return pl.pallas_call(
        _drop_deps(body, 1 + len(bs), len(deps)),
        grid=(m // tm_o, n // tn_o, nt),
        in_specs=[pl.BlockSpec((tt, tm_o), lambda i, j, k: (k, i))]
        + [pl.BlockSpec((tt, piece.shape[1]), lambda i, j, k: (k, j)) for piece in bs] + [ANY] * len(deps),
        out_specs=pl.BlockSpec((tm_o, tn_o), lambda i, j, k: (i, j)),
        out_shape=pltpu.HBM((m, n), BF16),
        scratch_shapes=[pltpu.VMEM((tm_o, tn_o), F32)],
        compiler_params=_params(3),
        name=name,
    )(a, *bs, *deps)


def _mm_nt(lhs, w, name):
    t, d = lhs.shape
    kd = w.shape[0]
    tm = _row_tile(t)

    def body(l_ref, w_ref, o_ref):
        _store_slabs(o_ref, 0, _dot_nt(l_ref[...], w_ref[...]))

    return pl.pallas_call(
        body,
        grid=(t // tm,),
        in_specs=[pl.BlockSpec((tm, d), lambda i: (i, 0)), _resident(w)],
        out_specs=pl.BlockSpec((kd // LANES, tm, LANES), lambda i: (0, i, 0)),
        out_shape=jax.ShapeDtypeStruct((kd // LANES, t, LANES), F32),
        compiler_params=_params(1),
        name=name,
    )(lhs, w)


def _dx_ln(dy, pairs, ln, name, deps=()):
    t, d = dy.shape
    npair = len(pairs)
    pairs = [(list(lhs) if isinstance(lhs, (list, tuple)) else [lhs], w) for lhs, w in pairs]
    tm = min(t, 512 // npair)
    nt = t // tm
    n_in = 1 + sum(len(pieces) + 1 for pieces, _ in pairs) + (3 if ln is not None else 0)

    def body(*refs):
        dy_ref = refs[0]
        pos = 1
        dh = ALPHA * dy_ref[...]
        for pieces, _ in pairs:
            w_ref = refs[pos + len(pieces)]
            first = 0
            for l_ref in refs[pos:pos + len(pieces)]:
                cols = slice(first, first + l_ref.shape[1])
                dh = dh + _dot_nt(l_ref[...], w_ref[:, cols])
                first = cols.stop
            pos += len(pieces) + 1
        if ln is not None:
            xh_ref, rs_ref, g_ref = refs[pos:pos + 3]
            dyo_ref, dyb_ref, dg_ref, db_ref = refs[pos + 3:pos + 7]

            @pl.when(pl.program_id(0) == 0)
            def _():
                dg_ref[...] = jnp.zeros_like(dg_ref)
                db_ref[...] = jnp.zeros_like(db_ref)

            dyp, dg, db = _ln_bwd(dh, xh_ref[...], rs_ref[...], g_ref[...])
            dyo_ref[...] = dyp
            dyb_ref[...] = dyp.astype(BF16)
            dg_ref[...] += dg
            db_ref[...] += db
        else:
            refs[pos][...] = dh

    row = pl.BlockSpec((tm, d), lambda i: (i, 0))
    vec = pl.BlockSpec((1, d), lambda i: (0, 0))
    in_specs = [row]
    args = [dy]
    for pieces, w in pairs:
        in_specs += [pl.BlockSpec((tm, piece.shape[1]), lambda i: (i, 0)) for piece in pieces] + [_resident(w)]
        args += pieces + [w]
    if ln is not None:
        in_specs += [row, pl.BlockSpec((tm, 1), lambda i: (i, 0)), vec]
        args += list(ln)
        out_specs = [row, row, vec, vec]
        out_shape = [jax.ShapeDtypeStruct((t, d), F32), jax.ShapeDtypeStruct((t, d), BF16),
                     jax.ShapeDtypeStruct((1, d), F32), jax.ShapeDtypeStruct((1, d), F32)]
    else:
        out_specs = row
        out_shape = jax.ShapeDtypeStruct((t, d), F32)
    return pl.pallas_call(
        _drop_deps(body, n_in, len(deps)),
        grid=(nt,),
        in_specs=in_specs + [ANY] * len(deps),
        out_specs=out_specs,
        out_shape=out_shape,
        compiler_params=_params(1),
        name=name,
    )(*args, *deps)


def _hgrn_bwd(proj, oraw, dmix, states, logits, gn):
    t = proj.shape[1]
    ct = _hg_tile(t)
    nct = t // ct
    nblk = ct // HG_BLOCK
    nh = HG_HEADS
    mrows = min(ct, 256)

    def body(q_ref, fz_ref, iv_ref, gg_ref, or_ref, do_ref, st_ref, lg_ref, gn_ref, mask_ref,
             dq_ref, dfz_ref, div_ref, dgg_ref, dlg_ref, dgn_ref,
             dstate, qt_s, kt_s, k_s, b_s, eb_s, ekb_s, dec_s, dor_s, dbl_s, gr_s, dk_s, dlb_acc):
        c = pl.program_id(1)

        @pl.when(c == 0)
        def _():
            dstate[...] = jnp.zeros_like(dstate)
            dlb_acc[...] = jnp.zeros_like(dlb_acc)
            dgn_ref[...] = jnp.zeros_like(dgn_ref)

        lb = _lower_bound(lg_ref[...])
        q = q_ref[...]
        sig, nsig, f, k = _forget_terms(fz_ref[...], lb)
        logf = jnp.log(f)
        b = _mask_dot(mask_ref[0], logf)
        bend = _mask_dot(mask_ref[2], logf)
        eb = jnp.exp(b)
        ekb = jnp.exp(bend - b)
        qt_s[...] = (q * eb).astype(BF16)
        kt_s[...] = (k * ekb).astype(BF16)
        k_s[...] = k
        b_s[...] = b
        eb_s[...] = eb
        ekb_s[...] = ekb
        dec_s[...] = jnp.exp(bend)
        oraw = or_ref[...]
        r = lax.rsqrt(jnp.mean(oraw * oraw, axis=-1, keepdims=True) + LN_EPS)
        on = oraw * r
        gg = gg_ref[...]
        silu, dsilu = _silu_and_grad(gg)
        doa = do_ref[...]
        gnv = gn_ref[...]
        dgg_ref[...] = (doa * on * gnv * dsilu).astype(BF16)
        dyn = doa * silu
        dgn_ref[...] += jnp.sum(dyn * on, axis=0, keepdims=True)
        don = dyn * gnv
        dor_s[...] = r * (don - on * jnp.mean(don * on, axis=-1, keepdims=True))
        tidx = lax.broadcasted_iota(jnp.int32, (HG_HALF, HG_DIM), 0)

        def blk(ii, carry):
            i = nblk - 1 - ii
            r0 = pl.multiple_of(i * HG_BLOCK, HG_BLOCK)
            rows = pl.ds(r0, HG_BLOCK)
            st = st_ref[i]
            dst = dstate[...]
            dstb = dst.astype(BF16)
            do = dor_s[rows, :]
            dob = do.astype(BF16)
            v = iv_ref[rows, :]
            vb = v.astype(BF16)
            qq = q_ref[rows, :]
            kk = k_s[rows, :]
            bb = b_s[rows, :]
            qt = qt_s[rows, :]
            kt = kt_s[rows, :]
            dec = dec_s[pl.ds(r0, 1), :]
            dkt = _dot(vb, dstb)
            dq = _dot(dob, st) * eb_s[rows, :]
            dk = dkt * ekb_s[rows, :]
            dv = _dot_nt(kt, dstb)
            gend = (jnp.sum(kk * dk, axis=0, keepdims=True)
                    + dec * jnp.sum(dst * st.astype(F32), axis=0, keepdims=True))
            qh, bh, doh = _halves(qq), _halves(bb), _halves(do)
            dqh, dkh, dvh = list(_halves(dq)), list(_halves(dk)), list(_halves(dv))
            for s in range(HG_BLOCK):
                ks, vs = kk[s:s + 1, :], v[s:s + 1, :]
                dk_part = dv_part = None
                for h in _causal_halves(s):
                    e = _decay_from(bh[h], bb[s:s + 1, :], s, h, tidx)
                    ke = ks * e
                    acol = jnp.sum(qh[h] * ke, axis=1, keepdims=True)
                    dacol = jnp.sum(doh[h] * vs, axis=1, keepdims=True)
                    dqh[h] = dqh[h] + dacol * ke
                    pk = dacol * (qh[h] * e)
                    pv = acol * doh[h]
                    dk_part = pk if dk_part is None else dk_part + pk
                    dv_part = pv if dv_part is None else dv_part + pv
                hs, row = divmod(s, HG_HALF)
                dkh[hs] = dkh[hs] + jnp.where(tidx == row, jnp.sum(dk_part, axis=0, keepdims=True), 0.0)
                dvh[hs] = dvh[hs] + jnp.where(tidx == row, jnp.sum(dv_part, axis=0, keepdims=True), 0.0)
            dq = jnp.concatenate(dqh, axis=0)
            dk = jnp.concatenate(dkh, axis=0)
            dv = jnp.concatenate(dvh, axis=0)
            dq_ref[rows, :] = dq.astype(BF16)
            div_ref[rows, :] = dv.astype(BF16)
            dk_s[rows, :] = dk
            dbl_s[rows, :] = qq * dq - kk * dk
            gr_s[rows, :] = jnp.zeros((HG_BLOCK, HG_DIM), F32) + gend
            dstate[...] = dst * dec + _dot_tn(dob, qt)
            return carry

        lax.fori_loop(0, nblk, blk, 0, unroll=HG_UNROLL)
        dlogf = _mask_dot(mask_ref[1], dbl_s[...]) + gr_s[...]
        dk = dk_s[...]
        dfz_ref[...] = ((dlogf / f - dk) * ((1.0 - lb) * sig * nsig)).astype(BF16)
        dlb_acc[...] += jnp.sum((dlogf / f - dk) * nsig, axis=0, keepdims=True)

        @pl.when(c == nct - 1)
        def _():
            dl0 = dlb_acc[...] * lb * (1.0 - lb)
            layer = lax.broadcasted_iota(jnp.int32, (2, HG_DIM), 0)
            dlg_ref[...] = jnp.where(layer == 0, dl0, -dl0)

    def slab(off):
        return pl.BlockSpec((None, ct, HG_DIM), lambda h, c: (off + h, nct - 1 - c, 0))

    out_slab = pl.BlockSpec((ct, HG_DIM), lambda h, c: (nct - 1 - c, h))
    tile_f32 = pltpu.VMEM((ct, HG_DIM), F32)
    tile_b16 = pltpu.VMEM((ct, HG_DIM), BF16)
    slab_shape = pltpu.HBM((t, nh * HG_DIM), BF16)
    return pl.pallas_call(
        body,
        grid=(nh, nct),
        in_specs=[slab(0), slab(nh), slab(2 * nh), slab(3 * nh), slab(0), slab(0),
                  pl.BlockSpec((None, nblk, HG_DIM, HG_DIM), lambda h, c: (h, nct - 1 - c, 0, 0)),
                  pl.BlockSpec((None, 2, HG_DIM), lambda h, c: (h, 0, 0)),
                  pl.BlockSpec((1, HG_DIM), lambda h, c: (0, 0)),
                  pl.BlockSpec((3, mrows, mrows), lambda h, c: (0, 0, 0))],
        out_specs=[out_slab, out_slab, out_slab, out_slab,
                   pl.BlockSpec((None, 2, HG_DIM), lambda h, c: (h, 0, 0)),
                   pl.BlockSpec((None, 1, HG_DIM), lambda h, c: (h, 0, 0))],
        out_shape=[slab_shape, slab_shape, slab_shape, slab_shape,
                   jax.ShapeDtypeStruct((nh, 2, HG_DIM), F32), jax.ShapeDtypeStruct((nh, 1, HG_DIM), F32)],
        scratch_shapes=[pltpu.VMEM((HG_DIM, HG_DIM), F32), tile_b16, tile_b16, tile_f32, tile_f32, tile_f32, tile_f32,
                        tile_f32, tile_f32, tile_f32, tile_f32, tile_f32, pltpu.VMEM((1, HG_DIM), F32)],
        compiler_params=_params(2),
        name="hgrn_bwd",
    )(proj, proj, proj, proj, oraw, dmix, states, logits, gn, _block_masks(mrows))


def _sgu_bwd(proj, dmix, ln_g, ln_b, w_s, w_t, b_col):
    t = proj.shape[1]
    ct = _sg_tile(t)
    nct = t // ct
    ng = SG_GROUPS
    off_u = 4 * HG_HEADS
    off_v = off_u + ng
    n = SG_CHUNK

    def body(u_ref, v_ref, do_ref, g_ref, b_ref, w_ref, wt_ref, bs_ref, du_ref, dv_ref, dg_ref, db_ref, dw_ref, dbs_ref):
        c = pl.program_id(1)

        @pl.when(c == 0)
        def _():
            dg_ref[...] = jnp.zeros_like(dg_ref)
            db_ref[...] = jnp.zeros_like(db_ref)
            dw_ref[...] = jnp.zeros_like(dw_ref)
            dbs_ref[...] = jnp.zeros_like(dbs_ref)

        r = lax.broadcasted_iota(jnp.int32, (n, n), 0)
        cc = lax.broadcasted_iota(jnp.int32, (n, n), 1)
        wm = jnp.where(cc <= r, w_ref[...], 0.0).astype(BF16)
        wmt = jnp.where(r <= cc, wt_ref[...], 0.0).astype(BF16)
        for ci in range(ct // n):
            rows = slice(ci * n, (ci + 1) * n)
            ua, dua, dva, vn, xhat, rstd, s = _sgu_chunk_fwd(u_ref[rows, :], v_ref[rows, :], g_ref[...], b_ref[...],
                                                             wm, bs_ref[...])
            do = do_ref[rows, :]
            du_ref[rows, :] = (do * s * dua).astype(BF16)
            ds = do * ua
            dsb = ds.astype(BF16)
            dbs_ref[...] += jnp.sum(ds, axis=1, keepdims=True)
            dw_ref[...] += _dot_nt(dsb, vn.astype(BF16))
            dvn = _dot(wmt, dsb)
            dva_in, dg, db = _ln_bwd(dvn, xhat, rstd, g_ref[...])
            dg_ref[...] += dg
            db_ref[...] += db
            dv_ref[rows, :] = (dva_in * dva).astype(BF16)

        @pl.when(c == nct - 1)
        def _():
            dw_ref[...] = jnp.where(cc <= r, dw_ref[...], 0.0)

    vec = pl.BlockSpec((None, 1, SG_DIM), lambda g, c: (g, 0, 0))
    mat = pl.BlockSpec((None, n, n), lambda g, c: (g, 0, 0))
    col = pl.BlockSpec((None, n, 1), lambda g, c: (g, 0, 0))
    out_slab = pl.BlockSpec((ct, SG_DIM), lambda g, c: (c, g))
    return pl.pallas_call(
        body,
        grid=(ng, nct),
        in_specs=[pl.BlockSpec((None, ct, SG_DIM), lambda g, c: (off_u + g, c, 0)),
                  pl.BlockSpec((None, ct, SG_DIM), lambda g, c: (off_v + g, c, 0)),
                  pl.BlockSpec((None, ct, SG_DIM), lambda g, c: (ng + g, c, 0)), vec, vec, mat, mat, col],
        out_specs=[out_slab, out_slab, vec, vec, mat, col],
        out_shape=[pltpu.HBM((t, ng * SG_DIM), BF16), pltpu.HBM((t, ng * SG_DIM), BF16),
                   jax.ShapeDtypeStruct((ng, 1, SG_DIM), F32), jax.ShapeDtypeStruct((ng, 1, SG_DIM), F32),
                   jax.ShapeDtypeStruct((ng, n, n), F32), jax.ShapeDtypeStruct((ng, n, 1), F32)],
        compiler_params=_params(2),
        name="sgu_bwd",
    )(proj, proj, dmix, ln_g, ln_b, w_s, w_t, b_col)


def _attn_bwd(dyb, wo, qb, kb, vb):
    t, d = dyb.shape
    m_len = kb.shape[0]
    tm = _row_tile(t)
    dh = d // X_HEADS
    scale = dh ** -0.5

    def body(dy_ref, wo_ref, q_ref, k_ref, v_ref, dq_ref, dk_ref, dv_ref):
        i = pl.program_id(0)

        @pl.when(i == 0)
        def _():
            dk_ref[...] = jnp.zeros_like(dk_ref)
            dv_ref[...] = jnp.zeros_like(dv_ref)

        do = _dot_nt(dy_ref[...], wo_ref[...]).astype(BF16)
        for hd in range(X_HEADS):
            sl = slice(hd * dh, (hd + 1) * dh)
            qh = q_ref[:, sl]
            p = _softmax_rows(_dot_nt(qh, k_ref[:, sl]) * scale)
            doh = do[:, sl]
            dp = _dot_nt(doh, v_ref[:, sl])
            ds = (p * (dp - jnp.sum(dp * p, axis=-1, keepdims=True)) * scale).astype(BF16)
            dq_ref[:, sl] = _dot(ds, k_ref[:, sl]).astype(BF16)
            dk_ref[:, sl] += _dot_tn(ds, qh)
            dv_ref[:, sl] += _dot_tn(p.astype(BF16), doh)

    row = pl.BlockSpec((tm, d), lambda i: (i, 0))
    full = lambda a: pl.BlockSpec(a.shape, lambda i: (0, 0))
    kv = pl.BlockSpec((m_len, d), lambda i: (0, 0))
    return pl.pallas_call(
        body,
        grid=(t // tm,),
        in_specs=[row, full(wo), row, full(kb), full(vb)],
        out_specs=[row, kv, kv],
        out_shape=[jax.ShapeDtypeStruct((t, d), BF16), jax.ShapeDtypeStruct((m_len, d), F32),
                   jax.ShapeDtypeStruct((m_len, d), F32)],
        compiler_params=_params(1),
        name="attn_bwd",
    )(dyb, wo, qb, kb, vb)


def _mem_bwd(dk, dv, mb, xhat, rstd, g, wk, wv):
    m_len, d = dk.shape

    def body(dk_ref, dv_ref, mb_ref, xh_ref, rs_ref, g_ref, wk_ref, wv_ref, gwk_ref, gwv_ref, dg_ref, db_ref):
        dkb = dk_ref[...].astype(BF16)
        dvb = dv_ref[...].astype(BF16)
        mb_v = mb_ref[...]
        gwk_ref[...] = _dot_tn(mb_v, dkb).astype(BF16)
        gwv_ref[...] = _dot_tn(mb_v, dvb).astype(BF16)
        dm = _dot_nt(dkb, wk_ref[...]) + _dot_nt(dvb, wv_ref[...])
        _, dg, db = _ln_bwd(dm, xh_ref[...], rs_ref[...], g_ref[...])
        dg_ref[...] = dg
        db_ref[...] = db

    return pl.pallas_call(
        body,
        out_shape=[jax.ShapeDtypeStruct((d, d), BF16), jax.ShapeDtypeStruct((d, d), BF16),
                   jax.ShapeDtypeStruct((1, d), F32), jax.ShapeDtypeStruct((1, d), F32)],
        compiler_params=pltpu.CompilerParams(vmem_limit_bytes=VMEM_LIMIT_V7X),
        name="mem_bwd",
    )(dk, dv, mb, xhat, rstd, g, wk, wv)


def _adamw(w, g, m, v):
    m = ADAM_B1 * m + (1.0 - ADAM_B1) * g
    v = ADAM_B2 * v + (1.0 - ADAM_B2) * (g * g)
    m_hat = m / (1.0 - ADAM_B1 ** ADAM_STEP)
    v_hat = v / (1.0 - ADAM_B2 ** ADAM_STEP)
    delta = -ADAM_LR * (m_hat / (jnp.sqrt(v_hat) + ADAM_EPS) + ADAM_WD * w)
    return delta, m, v


def _slot_sum(ref):
    g = ref[0].astype(F32)
    for s in range(1, N_DEV):
        g = g + ref[s].astype(F32)
    return g


def _adam_sharded(lands, w, m, v, axis, name):
    rows, cols = w.shape
    nl = len(lands)
    transposed = axis == 1 and nl == 2
    if transposed:
        rows, cols = cols, rows
        tr = 256
        grid = (rows // tr,)
        wblk = pl.BlockSpec((cols, tr), lambda i: (0, i))
        lblk = [pl.BlockSpec((N_DEV, tr, a.shape[2]), lambda i: (0, i, 0)) for a in lands]
    elif axis == 1:
        tr = 256 if rows % 256 == 0 else rows
        grid = (rows // tr,)
        wblk = pl.BlockSpec((tr, cols), lambda i: (i, 0))
        lblk = [pl.BlockSpec((N_DEV, tr, a.shape[2]), lambda i: (0, i, 0)) for a in lands]
    else:
        tc = _col_tile(cols)
        grid = (cols // tc,)
        wblk = pl.BlockSpec((rows, tc), lambda i: (0, i))
        lblk = [pl.BlockSpec((N_DEV, a.shape[1], tc), lambda i: (0, 0, i)) for a in lands]

    def body(*refs):
        w_ref, m_ref, v_ref = refs[nl:nl + 3]
        g_ref, d_ref, nm_ref, nv_ref = refs[nl + 3:]
        g = _slot_sum(refs[0])
        if nl == 2:
            tail = _slot_sum(refs[1])
            if transposed:
                g = jnp.concatenate([g.T, tail.T[:cols - g.shape[1], :]], axis=0)
            elif axis == 1:
                g = jnp.concatenate([g, tail[:, :cols - g.shape[1]]], axis=1)
            else:
                g = jnp.concatenate([g, tail[:rows - g.shape[0], :]], axis=0)
        delta, nm, nv = _adamw(w_ref[...], g, m_ref[...], v_ref[...])
        g_ref[...] = g
        d_ref[...] = delta
        nm_ref[...] = nm
        nv_ref[...] = nv

    shp = pltpu.HBM(w.shape, F32)
    return pl.pallas_call(
        body,
        grid=grid,
        in_specs=lblk + [wblk, wblk, wblk],
        out_specs=[wblk, wblk, wblk, wblk],
        out_shape=[shp, shp, shp, shp],
        compiler_params=_params(1),
        name=name,
    )(*[pltpu.with_memory_space_constraint(a, pltpu.HBM) for a in (*lands, w, m, v)])


def _mesh_pos():
    return lax.axis_index("x"), lax.axis_index("y"), lax.axis_index("c")


def _peer(k):
    x, y, c = _mesh_pos()
    pos = (x ^ (k >> 2), y ^ ((k >> 1) & 1), c ^ (k & 1))
    return pos, 4 * pos[0] + 2 * pos[1] + pos[2]


def _sem_index(row, k):
    return row * (N_DEV - 1) + k - 1


def _window(ref, axis, start, size):
    align = 16 if axis == 0 else LANES
    start = pl.multiple_of(start, align)
    return ref.at[pl.ds(start, size), :] if axis == 0 else ref.at[:, pl.ds(start, size)]


def _piece_refs(piece, srcs, lands, me, peer):
    kind, si, li, axis, base, stride, shape = piece
    if kind == "gather":
        return srcs[si], _window(lands[li], axis, base + stride * me, shape[axis])
    return _window(srcs[si], axis, base + stride * peer, shape[axis]), lands[li].at[me]


def _place_own(srcs, land_shapes, pieces, name):
    ns, nl, npc = len(srcs), len(land_shapes), len(pieces)

    def body(*refs):
        s_refs = refs[:ns]
        l_refs = refs[ns:ns + nl]
        bufs = refs[ns + nl:ns + nl + npc]
        sems = refs[ns + nl + npc]
        x, y, c = _mesh_pos()
        me = 4 * x + 2 * y + c
        loads = []
        for p, piece in enumerate(pieces):
            src, dst = _piece_refs(piece, s_refs, l_refs, me, me)
            cp = pltpu.make_async_copy(src, bufs[p], sems.at[0, p])
            cp.start()
            loads.append((cp, dst))
        stores = []
        for p, (cp, dst) in enumerate(loads):
            cp.wait()
            out = pltpu.make_async_copy(bufs[p], dst, sems.at[1, p])
            out.start()
            stores.append(out)
        for out in stores:
            out.wait()

    out = pl.pallas_call(
        body,
        in_specs=[ANY] * ns,
        out_specs=[HBM] * nl,
        out_shape=[pltpu.HBM(s.shape, s.dtype) for s in land_shapes],
        scratch_shapes=[pltpu.VMEM(pc[6], srcs[pc[1]].dtype) for pc in pieces] + [pltpu.SemaphoreType.DMA((2, npc))],
        compiler_params=pltpu.CompilerParams(vmem_limit_bytes=VMEM_LIMIT_V7X),
        name=name,
    )(*srcs)
    return list(out)


def _comm_start(srcs, lands, pieces, groups, name, after=()):
    ns, nl, na, ng = len(srcs), len(lands), len(after), len(groups)

    def body(*refs):
        s_refs = refs[:ns]
        l_refs = refs[ns:ns + nl]
        outs = refs[ns + nl + na:]
        sems = outs[:2 * ng]
        token = outs[-1]
        x, y, c = _mesh_pos()
        me = 4 * x + 2 * y + c
        for g, members in enumerate(groups):
            for row, p in enumerate(members):
                for k in range(1, N_DEV):
                    pos, peer = _peer(k)
                    src, dst = _piece_refs(pieces[p], s_refs, l_refs, me, peer)
                    pltpu.make_async_remote_copy(src_ref=src, dst_ref=dst, send_sem=sems[2 * g].at[_sem_index(row, k)],
                                                 recv_sem=sems[2 * g + 1].at[_sem_index(row, k)], device_id=pos,
                                                 device_id_type=MESH_ID).start()
        token[...] = jnp.zeros_like(token)

    sem_shapes = []
    for members in groups:
        sem_shapes += [pltpu.SemaphoreType.DMA((len(members) * (N_DEV - 1),))] * 2
    hbm_of = lambda a: pltpu.HBM(a.shape, a.dtype)
    out = pl.pallas_call(
        body,
        in_specs=[HBM] * (ns + nl) + [ANY] * na,
        out_specs=[SEM] * (2 * ng) + [HBM] * (ns + nl) + [pl.BlockSpec(memory_space=pltpu.VMEM)],
        out_shape=sem_shapes + [hbm_of(a) for a in srcs] + [hbm_of(a) for a in lands]
        + [jax.ShapeDtypeStruct((8, LANES), F32)],
        input_output_aliases={i: 2 * ng + i for i in range(ns + nl)},
        compiler_params=pltpu.CompilerParams(has_side_effects=DATAFLOW),
        name=name,
    )(*[pltpu.with_memory_space_constraint(a, pltpu.HBM) for a in list(srcs) + list(lands)], *after)
    sems = [(out[2 * g], out[2 * g + 1]) for g in range(ng)]
    return sems, list(out[2 * ng:2 * ng + ns]), list(out[2 * ng + ns:2 * ng + ns + nl]), out[-1]


def _comm_wait(srcs, lands, pieces, members, sems, after, name):
    ns, nl, na = len(srcs), len(lands), len(after)

    def body(*refs):
        s_refs = refs[:ns]
        l_refs = refs[ns:ns + nl]
        send_sems, recv_sems = refs[ns + nl:ns + nl + 2]
        x, y, c = _mesh_pos()
        me = 4 * x + 2 * y + c
        for row, p in enumerate(members):
            for k in range(1, N_DEV):
                pos, peer = _peer(k)
                src, dst = _piece_refs(pieces[p], s_refs, l_refs, me, peer)
                cp = pltpu.make_async_remote_copy(src_ref=src, dst_ref=dst, send_sem=send_sems.at[_sem_index(row, k)],
                                                  recv_sem=recv_sems.at[_sem_index(row, k)], device_id=pos,
                                                  device_id_type=MESH_ID)
                cp.wait_send()
                cp.wait_recv()

    hbm_of = lambda a: pltpu.HBM(a.shape, a.dtype)
    out = pl.pallas_call(
        body,
        in_specs=[HBM] * (ns + nl) + [SEM, SEM] + [ANY] * na,
        out_specs=[HBM] * (ns + nl),
        out_shape=[hbm_of(a) for a in srcs] + [hbm_of(a) for a in lands],
        input_output_aliases={i: i for i in range(ns + nl)},
        compiler_params=pltpu.CompilerParams(has_side_effects=DATAFLOW),
        name=name,
    )(*srcs, *lands, sems[0], sems[1], *after)
    return list(out[ns:])


def _landed_block(piece, lands, owner):
    _, _, li, axis, base, stride, shape = piece
    return _window(lands[li], axis, base + stride * owner, shape[axis])


def _copy_stage(name, bufs, in_sems, out_sem_sizes, emit, after=()):
    nb, ni, no, na = len(bufs), len(in_sems), len(out_sem_sizes), len(after)

    def body(*refs):
        b_refs = refs[:nb]
        i_refs = refs[nb:nb + ni]
        o_refs = refs[nb + ni + na:nb + ni + na + no]
        emit(b_refs, i_refs, o_refs)
        refs[-1][...] = jnp.zeros_like(refs[-1])

    hbm_of = lambda a: pltpu.HBM(a.shape, a.dtype)
    out = pl.pallas_call(
        body,
        in_specs=[HBM] * nb + [SEM] * ni + [ANY] * na,
        out_specs=[SEM] * no + [HBM] * nb + [pl.BlockSpec(memory_space=pltpu.VMEM)],
        out_shape=[pltpu.SemaphoreType.DMA((n,)) for n in out_sem_sizes] + [hbm_of(a) for a in bufs]
        + [jax.ShapeDtypeStruct((8, LANES), F32)],
        input_output_aliases={i: no + i for i in range(nb)},
        compiler_params=pltpu.CompilerParams(has_side_effects=DATAFLOW),
        name=name,
    )(*[pltpu.with_memory_space_constraint(a, pltpu.HBM) for a in bufs], *in_sems, *after)
    return list(out[:no]), list(out[no:no + nb]), out[-1]


def _remote(src, dst, send, recv, to):
    return pltpu.make_async_remote_copy(src_ref=src, dst_ref=dst, send_sem=send, recv_sem=recv, device_id=to,
                                        device_id_type=MESH_ID)


def _routed_gather(srcs, lands, pieces, meanwhile, name):
    ns, npc = len(srcs), len(pieces)

    def places():
        x, y, c = _mesh_pos()
        index = lambda p: 4 * p[0] + 2 * p[1] + p[2]
        me, sib = (x, y, c), (x, y, 1 - c)
        xnb, ynb = (1 - x, y, c), (x, 1 - y, c)
        got_first = (x ^ (1 - c), y ^ c, c)
        pass_to = (x ^ c, y ^ (1 - c), c)
        diag = (1 - x, 1 - y, c)
        return index, me, sib, xnb, ynb, got_first, pass_to, diag

    def start(b, _, o):
        index, me, sib, xnb, ynb, *_rest = places()
        send_a, recv_sib, recv_nb = o
        for p, piece in enumerate(pieces):
            src, dst = _piece_refs(piece, b[:ns], b[ns:], index(me), 0)
            _remote(src, dst, send_a.at[3 * p], recv_sib.at[p], sib).start()
            _remote(src, dst, send_a.at[3 * p + 1], recv_nb.at[2 * p], xnb).start()
            _remote(src, dst, send_a.at[3 * p + 2], recv_nb.at[2 * p + 1], ynb).start()

    def pass_a(b, i, o):
        index, me, sib, xnb, ynb, got_first, pass_to, _diag = places()
        (recv_nb,) = i
        send_f, recv_f, send_d, recv_d = o
        for p, piece in enumerate(pieces):
            for j, nb in enumerate((xnb, ynb)):
                blk = _landed_block(piece, b, index(nb))
                _remote(blk, blk, send_f.at[2 * p + j], recv_nb.at[2 * p + j], sib).wait_recv()
                _remote(blk, blk, send_f.at[2 * p + j], recv_f.at[2 * p + j], sib).start()
            blk = _landed_block(piece, b, index(got_first))
            _remote(blk, blk, send_d.at[p], recv_d.at[p], pass_to).start()

    def pass_b(b, i, o):
        index, me, sib, *_mid, diag = places()
        (recv_d,) = i
        send_g, recv_g = o
        for p, piece in enumerate(pieces):
            blk = _landed_block(piece, b, index(diag))
            _remote(blk, blk, send_g.at[p], recv_d.at[p], sib).wait_recv()
            _remote(blk, blk, send_g.at[p], recv_g.at[p], sib).start()

    def last(b, i, _):
        index, me, sib, *_others = places()
        send_a, recv_sib, send_f, recv_f, send_d, send_g, recv_g = i
        for p, piece in enumerate(pieces):
            src, dst = _piece_refs(piece, b[:ns], b[ns:], index(me), 0)
            cp = lambda s_sem, r_sem: _remote(src, dst, s_sem, r_sem, sib)
            cp(send_a.at[3 * p], recv_sib.at[p]).wait_recv()
            cp(send_a.at[3 * p], recv_g.at[p]).wait_recv()
            for j in range(3):
                cp(send_a.at[3 * p + j], recv_sib.at[p]).wait_send()
            for j in range(2):
                cp(send_f.at[2 * p + j], recv_f.at[2 * p + j]).wait_recv()
                cp(send_f.at[2 * p + j], recv_f.at[2 * p + j]).wait_send()
            cp(send_d.at[p], recv_sib.at[p]).wait_send()
            cp(send_g.at[p], recv_sib.at[p]).wait_send()

    (send_a, recv_sib, recv_nb), bufs, started = _copy_stage(name + "_start", list(srcs) + list(lands), [],
                                                             [3 * npc, npc, 2 * npc], start)
    srcs, lands = bufs[:ns], bufs[ns:]
    (send_f, recv_f, send_d, recv_d), lands, _ = _copy_stage(name + "_pass_a", lands, [recv_nb],
                                                             [2 * npc, 2 * npc, npc, npc],
                                                             lambda b, i, o: pass_a(b, i, o), after=meanwhile(started))
    (send_g, recv_g), lands, tok = _copy_stage(name + "_pass_b", lands, [recv_d], [npc, npc], pass_b)
    _, bufs, _ = _copy_stage(name + "_last", list(srcs) + list(lands),
                             [send_a, recv_sib, send_f, recv_f, send_d, send_g, recv_g], [], last)
    return bufs[ns:], tok


def _paired_gather(srcs, lands, pieces, groups, after, name):
    ns, ng = len(srcs), len(groups)
    far = (1, 2, 3)

    def me_sib():
        x, y, c = _mesh_pos()
        return 4 * x + 2 * y + c, (x, y, 1 - c)

    def start(b, _, o):
        me, sib = me_sib()
        for g, members in enumerate(groups):
            send, recv_sib, recv_far = o[3 * g:3 * g + 3]
            for r, p in enumerate(members):
                src, dst = _piece_refs(pieces[p], b[:ns], b[ns:], me, 0)
                _remote(src, dst, send.at[4 * r], recv_sib.at[r], sib).start()
                for j in far:
                    _remote(src, dst, send.at[4 * r + j], recv_far.at[3 * r + j - 1], _peer(2 * j)[0]).start()

    def forward(b, i, o):
        _, sib = me_sib()
        for g, members in enumerate(groups):
            send_f, recv_f = o[2 * g:2 * g + 2]
            for r, p in enumerate(members):
                for j in far:
                    blk = _landed_block(pieces[p], b, _peer(2 * j)[1])
                    _remote(blk, blk, send_f.at[3 * r + j - 1], i[g].at[3 * r + j - 1], sib).wait_recv()
                    _remote(blk, blk, send_f.at[3 * r + j - 1], recv_f.at[3 * r + j - 1], sib).start()

    def last(sub):
        def emit(b, i, _):
            send, recv_sib, send_f, recv_f = i
            me, sib = me_sib()
            for r, piece in enumerate(sub):
                src, dst = _piece_refs(piece, b[:-1], b[-1:], me, 0)
                _remote(src, dst, send.at[4 * r], recv_sib.at[r], sib).wait_recv()
                for j in range(4):
                    _remote(src, dst, send.at[4 * r + j], recv_sib.at[r], sib).wait_send()
                for j in far:
                    blk = _landed_block(piece, b[-1:], _peer(2 * j + 1)[1])
                    _remote(blk, blk, send_f.at[3 * r + j - 1], recv_f.at[3 * r + j - 1], sib).wait_recv()
                    _remote(blk, blk, send_f.at[3 * r + j - 1], recv_f.at[3 * r + j - 1], sib).wait_send()
        return emit

    sizes = []
    for members in groups:
        sizes += [4 * len(members), len(members), 3 * len(members)]
    sems, bufs, started = _copy_stage(name + "_start", list(srcs) + list(lands), [], sizes, start, after=after)
    srcs = bufs[:ns]
    state = dict(lands=bufs[ns:])

    def finish(g, after):
        if "passed" not in state:
            sizes_f = []
            for members in groups:
                sizes_f += [3 * len(members)] * 2
            state["passed"], state["lands"], _ = _copy_stage(name + "_pass", state["lands"],
                                                             [sems[3 * k + 2] for k in range(ng)], sizes_f, forward,
                                                             after=after)
            after = ()
        members = groups[g]
        sub = [(pieces[p][0], r, 0) + pieces[p][3:] for r, p in enumerate(members)]
        _, bufs, _ = _copy_stage("%s_last_%d" % (name, g), [srcs[pieces[p][1]] for p in members] + [state["lands"][g]],
                                 [sems[3 * g], sems[3 * g + 1], state["passed"][2 * g], state["passed"][2 * g + 1]],
                                 [], last(sub), after=after)
        return bufs[-1]

    return finish, started


_SMALL_NAMES = ("ln1_g", "ln1_b", "hg_lb_logits", "hg_norm_g", "sg_ln_g", "sg_ln_b", "sg_w_s", "sg_b_s",
                "ln2_g", "ln2_b", "mem_ln_g", "mem_ln_b", "ln3_g", "ln3_b", "ln4_g", "ln4_b")


_VEC_NAMES = ("ln1_g", "ln1_b", "ln2_g", "ln2_b", "mem_ln_g", "mem_ln_b", "ln3_g", "ln3_b", "ln4_g", "ln4_b")
_ROW_NAMES = ("hg_lb_logits", "hg_norm_g", "sg_ln_g", "sg_ln_b", "sg_b_s", "sg_w_s")
VEC_ROWS = 16


def _row_plan(shapes):
    plan, pos = {}, 0
    for k in _ROW_NAMES:
        shp = shapes[k]
        slabs, off = [], pos
        for idx in itertools.product(*[range(dim) for dim in shp[:-2]]):
            slabs.append((idx, off, shp[-2]))
            off += shp[-2]
        plan[k] = (pos, slabs)
        pos = -(-off // 8) * 8
    return plan, -(-pos // 16) * 16


def _pack_small_grads(gs, shapes, loss):
    d = gs[_VEC_NAMES[0]].size
    vec = jnp.concatenate([gs[k].reshape(1, -1) for k in _VEC_NAMES] + [jnp.tile(loss, (1, d // LANES))], axis=0)
    vec = jnp.pad(vec, ((0, VEC_ROWS - vec.shape[0]), (0, 0)))
    plan, total = _row_plan(shapes)
    parts, pos = [], 0
    for k in _ROW_NAMES:
        first, slabs = plan[k]
        rows = gs[k].reshape(-1, LANES)
        end = slabs[-1][1] + slabs[-1][2]
        nxt = -(-end // 8) * 8
        parts.append(jnp.pad(rows, ((0, nxt - first - rows.shape[0]), (0, 0))))
        pos = nxt
    parts.append(jnp.zeros((total - pos, LANES), F32))
    return vec, jnp.concatenate(parts, axis=0)


def _adam_small(land_vec, land_rows, w, m, v):
    names = _VEC_NAMES + _ROW_NAMES
    n = len(names)
    shapes = {k: w[k].shape for k in names}
    plan, _ = _row_plan(shapes)

    def body(*refs):
        lv_ref, lr_ref = refs[:2]
        w_refs, m_refs, v_refs = refs[2:2 + n], refs[2 + n:2 + 2 * n], refs[2 + 2 * n:2 + 3 * n]
        outs = refs[2 + 3 * n:2 + 7 * n]
        loss_ref = refs[2 + 7 * n]
        gv_s, gr_s = refs[3 + 7 * n:]
        gv_s[...] = _slot_sum(lv_ref)
        gr_s[...] = _slot_sum(lr_ref)
        loss_ref[...] = gv_s[len(_VEC_NAMES):len(_VEC_NAMES) + 1, :LANES]
        for p, k in enumerate(names):
            if k in _VEC_NAMES:
                row = _VEC_NAMES.index(k)
                slabs = [((), None, None)]
            else:
                slabs = plan[k][1]
            for idx, off, rows in slabs:
                g = gv_s[row:row + 1, :] if off is None else gr_s[off:off + rows, :]
                sel = idx + (slice(None), slice(None))
                delta, nm, nv = _adamw(w_refs[p][sel], g, m_refs[p][sel], v_refs[p][sel])
                for o, val in zip(range(4), (g, delta, nm, nv)):
                    outs[o * n + p][sel] = val

    flat = lambda tree: [tree[k] for k in names]
    shp = [jax.ShapeDtypeStruct(shapes[k], F32) for k in names]
    out = pl.pallas_call(
        body,
        out_shape=shp * 4 + [jax.ShapeDtypeStruct((1, LANES), F32)],
        scratch_shapes=[pltpu.VMEM(land_vec.shape[1:], F32), pltpu.VMEM(land_rows.shape[1:], F32)],
        name="adam_small",
    )(land_vec, land_rows, *flat(w), *flat(m), *flat(v))
    return [dict(zip(names, out[o * n:(o + 1) * n])) for o in range(4)], out[4 * n]


_COL_FFN = ("ffn1_w_gate", "ffn1_w_up", "ffn2_w_gate", "ffn2_w_up")
_ROW_FFN = ("ffn1_w_down", "ffn2_w_down")
_ROW_SQ = ("w_out", "xa_w_q", "xa_w_k", "xa_w_v", "xa_w_o")
_BIG_NAMES = ("ffn1_w_gate", "ffn1_w_up", "ffn1_w_down", "w_in", "w_out", "xa_w_q", "xa_w_k", "xa_w_v", "xa_w_o",
              "ffn2_w_gate", "ffn2_w_up", "ffn2_w_down")


def _ffn_split(fs):
    main = (fs // MXU_WIDTH_V7X) * MXU_WIDTH_V7X
    tail = fs - main
    tail_pad = -(-tail // LANES) * LANES
    assert main > 0 and tail > 0
    return main, tail, tail_pad


def _layout(name, shard_shape):
    r, c = shard_shape
    if name in _COL_FFN:
        main, tail, pad = _ffn_split(c)
        return (r, N_DEV * (main + pad)), [(1, 0, main, (r, main), (0, main)),
                                           (1, N_DEV * main, pad, (r, pad), (main, c))]
    if name in _ROW_FFN:
        main, tail, pad = _ffn_split(r)
        return (N_DEV * (main + pad), c), [(0, 0, main, (main, c), (0, main)),
                                           (0, N_DEV * main, pad, (pad, c), (main, r))]
    if name == "w_in":
        return (r, N_DEV * c), [(1, 0, c, (r, c), (0, c))]
    return (N_DEV * r, c), [(0, 0, r, (r, c), (0, r))]


def _shard_pieces(name, shard):
    out = []
    for axis, _, _, shape, (lo, hi) in _layout(name, shard.shape)[1]:
        part = shard[lo:hi, :] if axis == 0 else shard[:, lo:hi]
        pad = [(0, shape[0] - part.shape[0]), (0, shape[1] - part.shape[1])]
        out.append(jnp.pad(part, pad).astype(BF16))
    return out


def _gather_plan(names, shards):
    srcs, land_shapes, pieces, index = [], [], [], {}
    for li, name in enumerate(names):
        shape2d, parts = _layout(name, shards[name].shape)
        land_shapes.append(jax.ShapeDtypeStruct(shape2d, BF16))
        index[name] = []
        for (axis, base, stride, shape, _), src in zip(parts, _shard_pieces(name, shards[name])):
            index[name].append(len(pieces))
            pieces.append(("gather", len(srcs), li, axis, base, stride, shape))
            srcs.append(src)
    return srcs, land_shapes, pieces, index


def _scatter_plan(names, grads, shard_shapes):
    srcs, land_shapes, pieces, index = [], [], [], {}
    for si, name in enumerate(names):
        _, parts = _layout(name, shard_shapes[name])
        srcs.append(grads[name])
        index[name] = []
        for axis, base, stride, shape, _ in parts:
            index[name].append(len(land_shapes))
            pieces.append(("scatter", si, len(land_shapes), axis, base, stride, shape))
            land_shapes.append(jax.ShapeDtypeStruct((N_DEV,) + shape, grads[name].dtype))
    return srcs, land_shapes, pieces, index


def _small_views(small):
    row = lambda a: a.reshape(1, -1)
    ln = {k: row(small[k]) for k in ("ln1_g", "ln1_b", "ln2_g", "ln2_b", "ln3_g", "ln3_b", "ln4_g", "ln4_b",
                                      "mem_ln_g", "mem_ln_b", "hg_norm_g")}
    sg_w = small["sg_w_s"].reshape(SG_GROUPS, SG_CHUNK, SG_CHUNK)
    sg = dict(logits=jnp.swapaxes(small["hg_lb_logits"], 0, 1),
              g=small["sg_ln_g"].reshape(SG_GROUPS, 1, SG_DIM), b=small["sg_ln_b"].reshape(SG_GROUPS, 1, SG_DIM),
              w=sg_w, wt=jnp.swapaxes(sg_w, 1, 2), bs=small["sg_b_s"].reshape(SG_GROUPS, SG_CHUNK, 1))
    return ln, sg


def _forward(x, xb, mem, target, get_w, small, first_deps=()):
    ln, sg = _small_views(small)
    a1, b1, s1 = _ffn_up(xb, get_w("ffn1_w_gate", ()), get_w("ffn1_w_up", ()), "ffn1_up", deps=first_deps)
    h1b, xh1, rs1 = _mm_res_ln(s1, get_w("ffn1_w_down", (s1,)), x, ln["ln1_g"], ln["ln1_b"], 0.5, "ffn1_down_ln")
    proj = _mm_nn(h1b, get_w("w_in", (h1b,)), "mix_in")
    oraw, mix, states = _hgrn_fwd(proj, sg["logits"], ln["hg_norm_g"])
    mix = _sgu_fwd(proj, mix, sg["g"], sg["b"], sg["w"], sg["bs"])
    h2b, xh2, rs2 = _mm_res_ln(mix, get_w("w_out", (mix,)), (xh1, ln["ln1_g"], ln["ln1_b"]), ln["ln2_g"], ln["ln2_b"],
                               1.0, "mix_out_ln")
    mb, mxh, mrs, kb, vb = _mem_kv(mem, ln["mem_ln_g"], ln["mem_ln_b"], get_w("xa_w_k", (h2b,)), get_w("xa_w_v", (h2b,)))
    qb, att = _attn_fwd(h2b, get_w("xa_w_q", (mrs,)), kb, vb)
    h3b, xh3, rs3 = _mm_res_ln(att, get_w("xa_w_o", (att,)), (xh2, ln["ln2_g"], ln["ln2_b"]), ln["ln3_g"], ln["ln3_b"],
                               1.0, "attn_out_ln")
    a2, b2, s2 = _ffn_up(h3b, get_w("ffn2_w_gate", (h3b,)), get_w("ffn2_w_up", (h3b,)), "ffn2_up")
    loss, dy4, dy4b, dg4, db4 = _mm_res_ln(s2, get_w("ffn2_w_down", (s2,)), (xh3, ln["ln3_g"], ln["ln3_b"]),
                                           ln["ln4_g"], ln["ln4_b"], 0.5, "ffn2_down_ln_loss", target=target)
    return dict(xb=xb, a1=a1, b1=b1, s1=s1, h1b=h1b, xh1=xh1, rs1=rs1, proj=proj, oraw=oraw, mix=mix, states=states,
                h2b=h2b, xh2=xh2, rs2=rs2, mb=mb, mxh=mxh, mrs=mrs, kb=kb, vb=vb, qb=qb, att=att, h3b=h3b, xh3=xh3,
                rs3=rs3, a2=a2, b2=b2, s2=s2, loss=loss, dy4=dy4, dy4b=dy4b, dg4=dg4, db4=db4)


def _backward(sv, wt, small, send):
    ln, sg = _small_views(small)
    gs = {"ln4_g": sv["dg4"], "ln4_b": sv["db4"]}
    loss, dy4, dy4b = sv["loss"], sv["dy4"], sv["dy4b"]
    g_down2 = _mm_tn(sv["s2"], dy4b, "g_ffn2_down", scale=0.5)
    da2, db2, dy3, dy3b, gs["ln3_g"], gs["ln3_b"] = _ffn_bwd_fused(
        dy4, dy4b, wt["ffn2_w_down"], wt["ffn2_w_gate"], wt["ffn2_w_up"], sv["a2"], sv["b2"], 0.5,
        (sv["xh3"], sv["rs3"], ln["ln3_g"]), "ffn2_bwd")
    g_gate2 = _mm_tn(sv["h3b"], da2, "g_ffn2_gate")
    g_up2 = _mm_tn(sv["h3b"], db2, "g_ffn2_up")
    tok = send(("ffn2_w_down", "ffn2_w_gate", "ffn2_w_up"), (g_down2, g_gate2, g_up2))

    g_o = _mm_tn(sv["att"], dy3b, "g_xa_o", deps=(tok,))
    dqb, dk, dv = _attn_bwd(dy3b, wt["xa_w_o"], sv["qb"], sv["kb"], sv["vb"])
    g_q = _mm_tn(sv["h2b"], dqb, "g_xa_q")
    g_k, g_v, gs["mem_ln_g"], gs["mem_ln_b"] = _mem_bwd(dk, dv, sv["mb"], sv["mxh"], sv["mrs"], ln["mem_ln_g"],
                                                        wt["xa_w_k"], wt["xa_w_v"])
    tok = send(("xa_w_o", "xa_w_q", "xa_w_k", "xa_w_v"), (g_o, g_q, g_k, g_v))
    dy2, dy2b, gs["ln2_g"], gs["ln2_b"] = _dx_ln(dy3, [(dqb, wt["xa_w_q"])], (sv["xh2"], sv["rs2"], ln["ln2_g"]),
                                                 "attn_dx_ln", deps=(tok,))

    g_out = _mm_tn(sv["mix"], dy2b, "g_w_out")
    dmix = _mm_nt(dy2b, wt["w_out"], "mix_out_bwd")
    dq, dfz, div, dgg, dlg, dgn = _hgrn_bwd(sv["proj"], sv["oraw"], dmix, sv["states"], sg["logits"], ln["hg_norm_g"])
    du, dvv, gs["sg_ln_g"], gs["sg_ln_b"], gs["sg_w_s"], gs["sg_b_s"] = _sgu_bwd(
        sv["proj"], dmix, sg["g"], sg["b"], sg["w"], sg["wt"], sg["bs"])
    gs["hg_lb_logits"] = jnp.swapaxes(dlg, 0, 1)
    gs["hg_norm_g"] = jnp.sum(dgn, axis=0)
    dproj = [dq, dfz, div, dgg, du, dvv]
    g_in = _mm_tn(sv["h1b"], dproj, "g_w_in")
    tok = send(("w_out", "w_in"), (g_out, g_in))
    dy1, dy1b, gs["ln1_g"], gs["ln1_b"] = _dx_ln(dy2, [(dproj, wt["w_in"])], (sv["xh1"], sv["rs1"], ln["ln1_g"]),
                                                 "mix_dx_ln", deps=(tok,))

    g_down1 = _mm_tn(sv["s1"], dy1b, "g_ffn1_down", scale=0.5)
    tok = send(("ffn1_w_down",), (g_down1,))
    da1, db1 = _ffn_bwd_act(dy1b, wt["ffn1_w_down"], sv["a1"], sv["b1"], 0.5, "ffn1_bwd_act", deps=(tok,))
    g_gate1 = _mm_tn(sv["xb"], da1, "g_ffn1_gate")
    tok = send(("ffn1_w_gate",), (g_gate1,))
    g_up1 = _mm_tn(sv["xb"], db1, "g_ffn1_up", deps=(tok,))
    tok = send(("ffn1_w_up",), (g_up1,))
    grad_x = _dx_ln(dy1, [(da1, wt["ffn1_w_gate"]), (db1, wt["ffn1_w_up"])], None, "ffn1_dx", deps=(tok,))
    return loss, grad_x, gs


_WEIGHT_NAMES = ("ffn1_w_gate", "ffn1_w_up", "ffn1_w_down", "ln1_g", "ln1_b", "w_in", "hg_lb_logits", "hg_norm_g",
                 "sg_ln_g", "sg_ln_b", "sg_w_s", "sg_b_s", "w_out", "ln2_g", "ln2_b", "mem_ln_g", "mem_ln_b",
                 "xa_w_q", "xa_w_k", "xa_w_v", "xa_w_o", "ln3_g", "ln3_b", "ffn2_w_gate", "ffn2_w_up", "ffn2_w_down",
                 "ln4_g", "ln4_b")
_FIRST = ("ffn1_w_gate", "ffn1_w_up")
_SECOND = ("ffn1_w_down", "w_in")
_THIRD = ("w_out", "xa_w_k", "xa_w_v", "xa_w_q", "xa_w_o", "ffn2_w_gate", "ffn2_w_up", "ffn2_w_down")


def kernel(x, mem, ffn1_w_gate, ffn1_w_up, ffn1_w_down, ln1_g, ln1_b, w_in, hg_lb_logits, hg_norm_g, sg_ln_g, sg_ln_b, sg_w_s, sg_b_s, w_out, ln2_g, ln2_b, mem_ln_g, mem_ln_b, xa_w_q, xa_w_k, xa_w_v, xa_w_o, ln3_g, ln3_b, ffn2_w_gate, ffn2_w_up, ffn2_w_down, ln4_g, ln4_b, loss_target, m_ffn1_w_gate, m_ffn1_w_up, m_ffn1_w_down, m_ln1_g, m_ln1_b, m_w_in, m_hg_lb_logits, m_hg_norm_g, m_sg_ln_g, m_sg_ln_b, m_sg_w_s, m_sg_b_s, m_w_out, m_ln2_g, m_ln2_b, m_mem_ln_g, m_mem_ln_b, m_xa_w_q, m_xa_w_k, m_xa_w_v, m_xa_w_o, m_ln3_g, m_ln3_b, m_ffn2_w_gate, m_ffn2_w_up, m_ffn2_w_down, m_ln4_g, m_ln4_b, v_ffn1_w_gate, v_ffn1_w_up, v_ffn1_w_down, v_ln1_g, v_ln1_b, v_w_in, v_hg_lb_logits, v_hg_norm_g, v_sg_ln_g, v_sg_ln_b, v_sg_w_s, v_sg_b_s, v_w_out, v_ln2_g, v_ln2_b, v_mem_ln_g, v_mem_ln_b, v_xa_w_q, v_xa_w_k, v_xa_w_v, v_xa_w_o, v_ln3_g, v_ln3_b, v_ffn2_w_gate, v_ffn2_w_up, v_ffn2_w_down, v_ln4_g, v_ln4_b):
    args = dict(locals())
    w = {k: args[k] for k in _WEIGHT_NAMES}
    m = {k: args["m_" + k] for k in _WEIGHT_NAMES}
    v = {k: args["v_" + k] for k in _WEIGHT_NAMES}
    shards = {k: w[k][0] for k in _BIG_NAMES}
    shard_shapes = {k: shards[k].shape for k in _BIG_NAMES}
    small = {k: (w[k][0] if k != "hg_lb_logits" else w[k]) for k in _SMALL_NAMES}

    srcs1, shapes1, pieces1, idx1 = _gather_plan(_FIRST, shards)
    lands1 = _place_own(srcs1, shapes1, pieces1, "gather_first_own")
    prepared = {}

    def prepare_rest(started):
        later = {k: shards[k] + started[0, 0] for k in _SECOND + _THIRD}
        placed = ()
        for key, names in (("second", _SECOND), ("third", _THIRD)):
            srcs, shapes, pieces, idx = _gather_plan(names, later)
            lands = _place_own(srcs, shapes, pieces, "gather_%s_own" % key)
            prepared[key] = (srcs, lands, pieces, idx)
            placed += tuple(lands)
        prepared["xb"] = _to_bf16(x[0], "x_bf16", deps=(started,))
        return placed + (prepared["xb"],)

    lands1, tok1 = _routed_gather(srcs1, lands1, pieces1, prepare_rest, "gather_first")
    srcs_p, lands_p, pieces_p, idx_p = prepared["second"]
    finish_second, tok_p = _paired_gather(srcs_p, lands_p, pieces_p, [list(idx_p[k]) for k in _SECOND], (tok1,),
                                          "gather_second")
    srcs2, lands2, pieces2, idx2 = prepared["third"]
    groups2 = [list(idx2[k]) for k in _THIRD]
    sems2, srcs2, lands2, tok2 = _comm_start(srcs2, lands2, pieces2, groups2, "gather_third_start", after=(tok_p,))
    wt = dict(zip(_FIRST, lands1))
    pending = {k: gi for gi, k in enumerate(_THIRD)}

    def get_w(name, after):
        if name in _SECOND and name not in wt:
            wt[name] = finish_second(_SECOND.index(name), after)
        if name in pending:
            gi = pending.pop(name)
            si = [pieces2[p][1] for p in groups2[gi]]
            sub = [(pieces2[p][0], row, 0) + pieces2[p][3:] for row, p in enumerate(groups2[gi])]
            wt[name] = _comm_wait([srcs2[s] for s in si], [lands2[gi]], sub, list(range(len(sub))), sems2[gi],
                                  after, "gather_wait_" + name)[0]
        return wt[name]

    sv = _forward(x[0], prepared["xb"], mem[0], loss_target[0], get_w, small, first_deps=(tok2,))

    sent = []

    def send(names, grads):
        srcs, shapes, pieces, idx = _scatter_plan(names, dict(zip(names, grads)), shard_shapes)
        lands = _place_own(srcs, shapes, pieces, "grads_own_%d" % len(sent))
        sems, srcs, lands, tok = _comm_start(srcs, lands, pieces, [list(range(len(pieces)))],
                                             "grads_start_%d" % len(sent))
        sent.append((names, srcs, lands, pieces, idx, sems[0]))
        return tok

    loss, grad_x, gs = _backward(sv, wt, small, send)

    ssrc = list(_pack_small_grads(gs, {k: w[k].shape for k in _SMALL_NAMES}, loss))
    sp = [("scatter", i, i, 0, 0, 0, a.shape) for i, a in enumerate(ssrc)]
    sshape = [jax.ShapeDtypeStruct((N_DEV,) + a.shape, F32) for a in ssrc]
    sl = _place_own(ssrc, sshape, sp, "small_own")
    ssem, ssrc, sl, _ = _comm_start(ssrc, sl, sp, [[0, 1]], "small_start")

    out_g, out_d, out_m, out_v = {}, {}, {}, {}
    after = (grad_x,)
    for n_sent, (names, srcs, lands, pieces, idx, sems) in enumerate(sent):
        lands = _comm_wait(srcs, lands, pieces, list(range(len(pieces))), sems, after, "grads_wait_%d" % n_sent)
        for k in names:
            axis = 1 if (k in _COL_FFN or k == "w_in") else 0
            if k in _COL_FFN:
                done = _adam_sharded([lands[i] for i in idx[k]], w[k][0].T, m[k][0].T, v[k][0].T, axis, "adam_" + k)
                res = [r.T for r in done]
            else:
                res = done = _adam_sharded([lands[i] for i in idx[k]], w[k][0], m[k][0], v[k][0], axis, "adam_" + k)
            out_g[k], out_d[k], out_m[k], out_v[k] = [r[None] for r in res]
        after = (done[3],)
    sl = _comm_wait(ssrc, sl, sp, [0, 1], ssem[0], after, "small_wait")
    small_out, loss_sum = _adam_small(sl[0], sl[1], w, m, v)
    for dst, res in zip((out_g, out_d, out_m, out_v), small_out):
        dst.update(res)
    loss_all = loss_sum[0, 0]
    return (loss_all, grad_x[None], *[out_g[k] for k in _WEIGHT_NAMES], *[out_d[k] for k in _WEIGHT_NAMES],
            *[out_m[k] for k in _WEIGHT_NAMES], *[out_v[k] for k in _WEIGHT_NAMES])
```

```python
import itertools

import jax
import jax.numpy as jnp
import numpy as np
from jax import lax
from jax.experimental import pallas as pl
from jax.experimental.pallas import tpu as pltpu

F32 = jnp.float32
BF16 = jnp.bfloat16

N_DEV = 8
ALPHA = 2.0 ** 0.25
LN_EPS = 1e-5
HG_HEADS = 4
HG_DIM = 128
SG_GROUPS = 4
SG_DIM = 128
SG_CHUNK = 128
X_HEADS = 4
HG_BLOCK = 16
HG_UNROLL = 16
ADAM_LR = 0.001
ADAM_B1 = 0.9
ADAM_B2 = 0.999
ADAM_EPS = 1e-08
ADAM_WD = 0.01
ADAM_STEP = 10
VMEM_LIMIT_V7X = 48 * 1024 * 1024
MXU_WIDTH_V7X = 256
LANES = 128
MESH_ID = pl.DeviceIdType.MESH
ANY = pl.BlockSpec(memory_space=pl.ANY)
HBM = pl.BlockSpec(memory_space=pltpu.HBM)
SEM = pl.BlockSpec(memory_space=pltpu.SEMAPHORE)
DATAFLOW = pltpu.SideEffectType.DATAFLOW_SIDE_EFFECTING


def _params(n_axes):
    return pltpu.CompilerParams(dimension_semantics=("arbitrary",) * n_axes, vmem_limit_bytes=VMEM_LIMIT_V7X)


def _dot(a, b):
    return jnp.dot(a, b, preferred_element_type=F32)


def _dot_nt(a, b):
    return lax.dot_general(a, b, (((1,), (1,)), ((), ())), preferred_element_type=F32)


def _dot_tn(a, b):
    return lax.dot_general(a, b, (((0,), (0,)), ((), ())), preferred_element_type=F32)


def _sigmoid(x):
    return 1.0 / (1.0 + jnp.exp(-x))


def _silu_and_grad(a):
    sig = _sigmoid(a)
    return a * sig, sig * (1.0 + a * (1.0 - sig))


_GELU_C = 0.7978845608028654


def _gelu_and_grad(x):
    inner = _GELU_C * (x + 0.044715 * x * x * x)
    t = jnp.tanh(inner)
    val = 0.5 * x * (1.0 + t)
    grad = 0.5 * (1.0 + t) + 0.5 * x * (1.0 - t * t) * _GELU_C * (1.0 + 3.0 * 0.044715 * x * x)
    return val, grad


def _ln_fwd(y, g, b):
    mu = jnp.mean(y, axis=-1, keepdims=True)
    yc = y - mu
    var = jnp.mean(yc * yc, axis=-1, keepdims=True)
    rstd = lax.rsqrt(var + LN_EPS)
    xhat = yc * rstd
    return xhat * g + b, xhat, rstd


def _ln_bwd(dh, xhat, rstd, g):
    dxh = dh * g
    m1 = jnp.mean(dxh, axis=-1, keepdims=True)
    m2 = jnp.mean(dxh * xhat, axis=-1, keepdims=True)
    dy = rstd * (dxh - m1 - xhat * m2)
    dg = jnp.sum(dh * xhat, axis=0, keepdims=True)
    db = jnp.sum(dh, axis=0, keepdims=True)
    return dy, dg, db


def _mask_dot(mask, x):
    hi = x.astype(BF16)
    lo = (x - hi.astype(F32)).astype(BF16)
    n = mask.shape[0]
    parts = [_dot(mask, hi[r:r + n, :]) + _dot(mask, lo[r:r + n, :]) for r in range(0, x.shape[0], n)]
    return parts[0] if len(parts) == 1 else jnp.concatenate(parts, axis=0)


def _block_masks(n):
    r = np.arange(n)[:, None]
    c = np.arange(n)[None, :]
    same = (r // HG_BLOCK) == (c // HG_BLOCK)
    return jnp.asarray(np.stack([same & (c <= r), same & (c >= r), same]), BF16)


def _row_tile(t):
    return min(t, 512)


def _col_tile(n):
    for cand in (512, 256, 128):
        if n % cand == 0:
            return cand
    return n


def _resident(w):
    return pl.BlockSpec(w.shape, lambda *_: (0, 0), pipeline_mode=pl.Buffered(1))


def _drop_deps(body, n_in, n_deps):
    if n_deps == 0:
        return body
    return lambda *refs: body(*refs[:n_in], *refs[n_in + n_deps:])


def _to_bf16(x, name, deps=()):
    t, d = x.shape
    tm = _row_tile(t)

    def body(x_ref, o_ref):
        o_ref[...] = x_ref[...].astype(BF16)

    row = pl.BlockSpec((tm, d), lambda i: (i, 0))
    return pl.pallas_call(
        _drop_deps(body, 1, len(deps)),
        grid=(t // tm,),
        in_specs=[row] + [ANY] * len(deps),
        out_specs=row,
        out_shape=jax.ShapeDtypeStruct((t, d), BF16),
        compiler_params=_params(1),
        name=name,
    )(x, *deps)


def _ffn_up(hb, wg, wu, name, deps=()):
    t, d = hb.shape
    f = wg.shape[1]
    tm = _row_tile(t)
    tn = _col_tile(f)

    def body(h_ref, wg_ref, wu_ref, a_ref, b_ref, s_ref):
        h = h_ref[...]
        for c in range(f // tn):
            cols = slice(c * tn, (c + 1) * tn)
            a = _dot(h, wg_ref[:, cols])
            b = _dot(h, wu_ref[:, cols])
            a_ref[:, cols] = a.astype(BF16)
            b_ref[:, cols] = b.astype(BF16)
            s_ref[:, cols] = (a * _sigmoid(a) * b).astype(BF16)

    act = pl.BlockSpec((tm, f), lambda i: (i, 0))
    return pl.pallas_call(
        _drop_deps(body, 3, len(deps)),
        grid=(t // tm,),
        in_specs=[pl.BlockSpec((tm, d), lambda i: (i, 0)), _resident(wg), _resident(wu)] + [ANY] * len(deps),
        out_specs=[act, act, act],
        out_shape=[jax.ShapeDtypeStruct((t, f), BF16)] * 3,
        compiler_params=_params(1),
        name=name,
    )(hb, wg, wu, *deps)


def _mm_res_ln(lhs, w, res, g, b, coef, name, target=None):
    t, kd = lhs.shape
    d = w.shape[1]
    tm = _row_tile(t)
    nt = t // tm
    from_norm = isinstance(res, tuple)
    n_res = 3 if from_norm else 1

    def body(*refs):
        l_ref, w_ref = refs[:2]
        r_refs = refs[2:2 + n_res]
        g_ref, b_ref = refs[2 + n_res:4 + n_res]
        rest = refs[4 + n_res:]
        prev = r_refs[0][...] * r_refs[1][...] + r_refs[2][...] if from_norm else r_refs[0][...]
        y = ALPHA * prev + coef * _dot(l_ref[...], w_ref[...])
        h, xhat, rstd = _ln_fwd(y, g_ref[...], b_ref[...])
        if target is None:
            hb_ref, xh_ref, rs_ref = rest
            hb_ref[...] = h.astype(BF16)
            xh_ref[...] = xhat
            rs_ref[...] = rstd
            return
        t_ref, loss_ref, dy_ref, dyb_ref, dg_ref, db_ref, lacc = rest
        i = pl.program_id(0)

        @pl.when(i == 0)
        def _():
            lacc[...] = jnp.zeros_like(lacc)
            dg_ref[...] = jnp.zeros_like(dg_ref)
            db_ref[...] = jnp.zeros_like(db_ref)

        err = h - t_ref[...]
        lacc[...] += jnp.sum(err * err, axis=0, keepdims=True)
        dy, dg, db = _ln_bwd(err * (1.0 / d), xhat, rstd, g_ref[...])
        dy_ref[...] = dy
        dyb_ref[...] = dy.astype(BF16)
        dg_ref[...] += dg
        db_ref[...] += db

        @pl.when(i == nt - 1)
        def _():
            loss_ref[...] = jnp.zeros_like(loss_ref) + jnp.sum(lacc[...], axis=1, keepdims=True) * (0.5 / d)

    row = pl.BlockSpec((tm, d), lambda i: (i, 0))
    vec = pl.BlockSpec((1, d), lambda i: (0, 0))
    res_specs = [row, vec, vec] if from_norm else [row]
    res_args = list(res) if from_norm else [res]
    in_specs = [pl.BlockSpec((tm, kd), lambda i: (i, 0)), _resident(w)] + res_specs + [vec, vec]
    args = [lhs, w] + res_args + [g, b]
    if target is None:
        out_specs = [row, row, pl.BlockSpec((tm, 1), lambda i: (i, 0))]
        out_shape = [jax.ShapeDtypeStruct((t, d), BF16), jax.ShapeDtypeStruct((t, d), F32), pltpu.HBM((t, 1), F32)]
        scratch = []
    else:
        in_specs.append(row)
        args.append(target)
        out_specs = [pl.BlockSpec((1, LANES), lambda i: (0, 0)), row, row, vec, vec]
        out_shape = [jax.ShapeDtypeStruct((1, LANES), F32), jax.ShapeDtypeStruct((t, d), F32),
                     jax.ShapeDtypeStruct((t, d), BF16), jax.ShapeDtypeStruct((1, d), F32),
                     jax.ShapeDtypeStruct((1, d), F32)]
        scratch = [pltpu.VMEM((1, d), F32)]
    return pl.pallas_call(
        body,
        grid=(nt,),
        in_specs=in_specs,
        out_specs=out_specs,
        out_shape=out_shape,
        scratch_shapes=scratch,
        compiler_params=_params(1),
        name=name,
    )(*args)


def _store_slabs(o_ref, first, tile):
    for s in range(tile.shape[1] // LANES):
        o_ref[first + s] = tile[:, s * LANES:(s + 1) * LANES]


def _mm_nn(lhs, w, name):
    t, kd = lhs.shape
    n = w.shape[1]
    tm = _row_tile(t)
    tn = _col_tile(n)

    def body(l_ref, w_ref, o_ref):
        lhs_v = l_ref[...]
        for c in range(n // tn):
            _store_slabs(o_ref, c * (tn // LANES), _dot(lhs_v, w_ref[:, c * tn:(c + 1) * tn]))

    return pl.pallas_call(
        body,
        grid=(t // tm,),
        in_specs=[pl.BlockSpec((tm, kd), lambda i: (i, 0)), _resident(w)],
        out_specs=pl.BlockSpec((n // LANES, tm, LANES), lambda i: (0, i, 0)),
        out_shape=jax.ShapeDtypeStruct((n // LANES, t, LANES), F32),
        compiler_params=_params(1),
        name=name,
    )(lhs, w)


def _lower_bound(lg):
    m = jnp.max(lg, axis=0, keepdims=True)
    e = jnp.exp(lg - m)
    return e[0:1, :] / jnp.sum(e, axis=0, keepdims=True)


def _forget_terms(fz, lb):
    e = jnp.exp(-jnp.abs(fz))
    r = 1.0 / (1.0 + e)
    pos = fz >= 0.0
    sig = jnp.where(pos, r, e * r)
    nsig = jnp.where(pos, e * r, r)
    f = lb + (1.0 - lb) * sig
    k = (1.0 - lb) * nsig
    return sig, nsig, f, k


def _hg_tile(t):
    return min(t, 1024)


HG_HALF = HG_BLOCK // 2
NEG_BIG = -1e30


def _halves(a):
    return a[:HG_HALF, :], a[HG_HALF:, :]


def _causal_halves(s):
    return (0, 1) if s < HG_HALF else (1,)


def _decay_from(b_half, b_s, s, h, tidx):
    first = s - h * HG_HALF
    diff = b_half - b_s
    if first > 0:
        diff = jnp.where(tidx >= first, diff, NEG_BIG)
    return jnp.exp(diff)


def _hgrn_fwd(proj, logits, gn):
    t = proj.shape[1]
    ct = _hg_tile(t)
    nct = t // ct
    nblk = ct // HG_BLOCK
    nh = HG_HEADS
    mrows = min(ct, 256)

    def body(q_ref, fz_ref, iv_ref, gg_ref, lg_ref, gn_ref, mask_ref, oraw_ref, oa_ref, st_ref,
             state, qt_s, kt_s, k_s, b_s, dec_s):
        c = pl.program_id(1)

        @pl.when(c == 0)
        def _():
            state[...] = jnp.zeros_like(state)

        lb = _lower_bound(lg_ref[...])
        q = q_ref[...]
        _, _, f, k = _forget_terms(fz_ref[...], lb)
        logf = jnp.log(f)
        b = _mask_dot(mask_ref[0], logf)
        bend = _mask_dot(mask_ref[2], logf)
        qt_s[...] = (q * jnp.exp(b)).astype(BF16)
        kt_s[...] = (k * jnp.exp(bend - b)).astype(BF16)
        k_s[...] = k
        b_s[...] = b
        dec_s[...] = jnp.exp(bend)
        tidx = lax.broadcasted_iota(jnp.int32, (HG_HALF, HG_DIM), 0)

        def blk(i, carry):
            r0 = pl.multiple_of(i * HG_BLOCK, HG_BLOCK)
            rows = pl.ds(r0, HG_BLOCK)
            st = state[...]
            stb = st.astype(BF16)
            st_ref[i] = stb
            v = iv_ref[rows, :]
            qq = q_ref[rows, :]
            kk = k_s[rows, :]
            bb = b_s[rows, :]
            o = list(_halves(_dot_nt(qt_s[rows, :], stb)))
            qh, bh = _halves(qq), _halves(bb)
            for s in range(HG_BLOCK):
                ks, vs = kk[s:s + 1, :], v[s:s + 1, :]
                for h in _causal_halves(s):
                    e = _decay_from(bh[h], bb[s:s + 1, :], s, h, tidx)
                    acol = jnp.sum(qh[h] * (ks * e), axis=1, keepdims=True)
                    o[h] = o[h] + acol * vs
            oraw_ref[rows, :] = jnp.concatenate(o, axis=0)
            state[...] = st * dec_s[pl.ds(r0, 1), :] + _dot_tn(v.astype(BF16), kt_s[rows, :])
            return carry

        lax.fori_loop(0, nblk, blk, 0, unroll=HG_UNROLL)
        oraw = oraw_ref[...]
        r = lax.rsqrt(jnp.mean(oraw * oraw, axis=-1, keepdims=True) + LN_EPS)
        gg = gg_ref[...]
        oa_ref[...] = (oraw * r * gn_ref[...] * gg * _sigmoid(gg)).astype(BF16)

    def slab(off):
        return pl.BlockSpec((None, ct, HG_DIM), lambda h, c: (off + h, c, 0))

    return pl.pallas_call(
        body,
        grid=(nh, nct),
        in_specs=[slab(0), slab(nh), slab(2 * nh), slab(3 * nh),
                  pl.BlockSpec((None, 2, HG_DIM), lambda h, c: (h, 0, 0)),
                  pl.BlockSpec((1, HG_DIM), lambda h, c: (0, 0)),
                  pl.BlockSpec((3, mrows, mrows), lambda h, c: (0, 0, 0))],
        out_specs=[slab(0), pl.BlockSpec((ct, HG_DIM), lambda h, c: (c, h)),
                   pl.BlockSpec((None, nblk, HG_DIM, HG_DIM), lambda h, c: (h, c, 0, 0))],
        out_shape=[jax.ShapeDtypeStruct((nh, t, HG_DIM), F32),
                   jax.ShapeDtypeStruct((t, (nh + SG_GROUPS) * HG_DIM), BF16),
                   jax.ShapeDtypeStruct((nh, t // HG_BLOCK, HG_DIM, HG_DIM), BF16)],
        scratch_shapes=[pltpu.VMEM((HG_DIM, HG_DIM), F32), pltpu.VMEM((ct, HG_DIM), BF16),
                        pltpu.VMEM((ct, HG_DIM), BF16), pltpu.VMEM((ct, HG_DIM), F32),
                        pltpu.VMEM((ct, HG_DIM), F32), pltpu.VMEM((ct, HG_DIM), F32)],
        compiler_params=_params(2),
        name="hgrn_fwd",
    )(proj, proj, proj, proj, logits, gn, _block_masks(mrows))


def _sg_tile(t):
    return min(t, 512)


def _sgu_chunk_fwd(u, v, ln_g, ln_b, wm, bs):
    ua, dua = _gelu_and_grad(u)
    va, dva = _gelu_and_grad(v)
    vn, xhat, rstd = _ln_fwd(va, ln_g, ln_b)
    s = _dot(wm, vn.astype(BF16)) + bs
    return ua, dua, dva, vn, xhat, rstd, s


def _tril_weight(w):
    n = SG_CHUNK
    r = lax.broadcasted_iota(jnp.int32, (n, n), 0)
    c = lax.broadcasted_iota(jnp.int32, (n, n), 1)
    return jnp.where(c <= r, w, 0.0)


def _bias_by_position(b_row):
    return jnp.broadcast_to(b_row, (SG_CHUNK, SG_CHUNK)).T


def _sgu_fwd(proj, mix, ln_g, ln_b, w_s, b_row):
    t = proj.shape[1]
    ct = _sg_tile(t)
    ng = SG_GROUPS
    wide = ng * SG_DIM
    blk_u = 4 * HG_HEADS // ng

    def body(u_ref, v_ref, g_ref, b_ref, w_ref, bs_ref, mix_ref, o_ref):
        del mix_ref
        for g in range(ng):
            lanes = slice(g * SG_DIM, (g + 1) * SG_DIM)
            wm = _tril_weight(w_ref[g]).astype(BF16)
            bs = _bias_by_position(bs_ref[g])
            for n in range(ct // SG_CHUNK):
                rows = slice(n * SG_CHUNK, (n + 1) * SG_CHUNK)
                ua, _, _, _, _, _, s = _sgu_chunk_fwd(u_ref[g, rows, :], v_ref[g, rows, :], g_ref[g], b_ref[g], wm, bs)
                o_ref[rows, lanes] = (ua * s).astype(BF16)

    full = lambda a: pl.BlockSpec(a.shape, lambda c: (0,) * a.ndim)
    return pl.pallas_call(
        body,
        grid=(t // ct,),
        in_specs=[pl.BlockSpec((ng, ct, SG_DIM), lambda c: (blk_u, c, 0)),
                  pl.BlockSpec((ng, ct, SG_DIM), lambda c: (blk_u + 1, c, 0)),
                  full(ln_g), full(ln_b), full(w_s), full(b_row), ANY],
        out_specs=pl.BlockSpec((ct, wide), lambda c: (c, 1)),
        out_shape=jax.ShapeDtypeStruct(mix.shape, mix.dtype),
        input_output_aliases={6: 0},
        compiler_params=_params(1),
        name="sgu_fwd",
    )(proj, proj, ln_g, ln_b, w_s, b_row, mix)


def _mem_kv(mem, g, b, wk, wv):
    m_len, d = mem.shape

    def body(m_ref, g_ref, b_ref, wk_ref, wv_ref, mb_ref, xh_ref, rs_ref, k_ref, v_ref):
        m, xhat, rstd = _ln_fwd(m_ref[...], g_ref[...], b_ref[...])
        mb = m.astype(BF16)
        mb_ref[...] = mb
        xh_ref[...] = xhat
        rs_ref[...] = rstd
        k_ref[...] = _dot(mb, wk_ref[...]).astype(BF16)
        v_ref[...] = _dot(mb, wv_ref[...]).astype(BF16)

    return pl.pallas_call(
        body,
        out_shape=[jax.ShapeDtypeStruct((m_len, d), BF16), jax.ShapeDtypeStruct((m_len, d), F32),
                   jax.ShapeDtypeStruct((m_len, 1), F32), jax.ShapeDtypeStruct((m_len, d), BF16),
                   jax.ShapeDtypeStruct((m_len, d), BF16)],
        compiler_params=pltpu.CompilerParams(vmem_limit_bytes=VMEM_LIMIT_V7X),
        name="mem_kv",
    )(mem, g, b, wk, wv)


def _softmax_rows(s):
    m = jnp.max(s, axis=-1, keepdims=True)
    p = jnp.exp(s - m)
    return p / jnp.sum(p, axis=-1, keepdims=True)


def _attn_fwd(hb, wq, kb, vb):
    t, d = hb.shape
    tm = _row_tile(t)
    dh = d // X_HEADS
    scale = dh ** -0.5

    def body(h_ref, wq_ref, k_ref, v_ref, q_ref, o_ref):
        q = _dot(h_ref[...], wq_ref[...]).astype(BF16)
        q_ref[...] = q
        for hd in range(X_HEADS):
            sl = slice(hd * dh, (hd + 1) * dh)
            p = _softmax_rows(_dot_nt(q[:, sl], k_ref[:, sl]) * scale)
            o_ref[:, sl] = _dot(p.astype(BF16), v_ref[:, sl]).astype(BF16)

    row = pl.BlockSpec((tm, d), lambda i: (i, 0))
    full = lambda a: pl.BlockSpec(a.shape, lambda i: (0, 0))
    return pl.pallas_call(
        body,
        grid=(t // tm,),
        in_specs=[row, full(wq), full(kb), full(vb)],
        out_specs=[row, row],
        out_shape=[jax.ShapeDtypeStruct((t, d), BF16), jax.ShapeDtypeStruct((t, d), BF16)],
        compiler_params=_params(1),
        name="attn_fwd",
    )(hb, wq, kb, vb)


def _ffn_bwd_act(dyb, wd, a, b, coef, name, deps=()):
    t, d = dyb.shape
    f = wd.shape[0]
    tm = _row_tile(t)
    tn = _col_tile(f)

    def body(dy_ref, wd_ref, a_ref, b_ref, da_ref, db_ref):
        dy = dy_ref[...]
        for c in range(f // tn):
            cols = slice(c * tn, (c + 1) * tn)
            ds = _dot_nt(dy, wd_ref[cols, :]) * coef
            silu, dsilu = _silu_and_grad(a_ref[:, cols].astype(F32))
            da_ref[:, cols] = (ds * b_ref[:, cols].astype(F32) * dsilu).astype(BF16)
            db_ref[:, cols] = (ds * silu).astype(BF16)

    act = pl.BlockSpec((tm, f), lambda i: (i, 0))
    return pl.pallas_call(
        _drop_deps(body, 4, len(deps)),
        grid=(t // tm,),
        in_specs=[pl.BlockSpec((tm, d), lambda i: (i, 0)), _resident(wd), act, act] + [ANY] * len(deps),
        out_specs=[act, act],
        out_shape=[jax.ShapeDtypeStruct((t, f), BF16), jax.ShapeDtypeStruct((t, f), BF16)],
        compiler_params=_params(1),
        name=name,
    )(dyb, wd, a, b, *deps)


def _ffn_bwd_fused(dy, dyb, wd, wg, wu, a, b, coef, ln, name):
    t, d = dy.shape
    f = wd.shape[0]
    tm = min(t, 256)
    tn = _col_tile(f)

    def body(dy_ref, dyb_ref, wd_ref, wg_ref, wu_ref, a_ref, b_ref, xh_ref, rs_ref, g_ref,
             da_ref, db_ref, dyo_ref, dyob_ref, dg_ref, dbl_ref):
        dyb_v = dyb_ref[...]
        dh = ALPHA * dy_ref[...]
        for c in range(f // tn):
            cols = slice(c * tn, (c + 1) * tn)
            ds = _dot_nt(dyb_v, wd_ref[cols, :]) * coef
            silu, dsilu = _silu_and_grad(a_ref[:, cols].astype(F32))
            da = (ds * b_ref[:, cols].astype(F32) * dsilu).astype(BF16)
            db = (ds * silu).astype(BF16)
            da_ref[:, cols] = da
            db_ref[:, cols] = db
            dh = dh + _dot_nt(da, wg_ref[:, cols]) + _dot_nt(db, wu_ref[:, cols])

        @pl.when(pl.program_id(0) == 0)
        def _():
            dg_ref[...] = jnp.zeros_like(dg_ref)
            dbl_ref[...] = jnp.zeros_like(dbl_ref)

        dyp, dg, dbl = _ln_bwd(dh, xh_ref[...], rs_ref[...], g_ref[...])
        dyo_ref[...] = dyp
        dyob_ref[...] = dyp.astype(BF16)
        dg_ref[...] += dg
        dbl_ref[...] += dbl

    row = pl.BlockSpec((tm, d), lambda i: (i, 0))
    act = pl.BlockSpec((tm, f), lambda i: (i, 0))
    vec = pl.BlockSpec((1, d), lambda i: (0, 0))
    return pl.pallas_call(
        body,
        grid=(t // tm,),
        in_specs=[row, row, _resident(wd), _resident(wg), _resident(wu), act, act, row,
                  pl.BlockSpec((tm, 1), lambda i: (i, 0)), vec],
        out_specs=[act, act, row, row, vec, vec],
        out_shape=[jax.ShapeDtypeStruct((t, f), BF16), jax.ShapeDtypeStruct((t, f), BF16),
                   jax.ShapeDtypeStruct((t, d), F32), jax.ShapeDtypeStruct((t, d), BF16),
                   jax.ShapeDtypeStruct((1, d), F32), jax.ShapeDtypeStruct((1, d), F32)],
        compiler_params=_params(1),
        name=name,
    )(dy, dyb, wd, wg, wu, a, b, *ln)


def _mm_tn(a, b, name, scale=1.0, deps=()):
    t, m = a.shape
    bs = list(b) if isinstance(b, (list, tuple)) else [b]
    n = sum(piece.shape[1] for piece in bs)
    tt = _row_tile(t)
    nt = t // tt
    tm_o, tn_o = m, n

    def body(a_ref, *refs):
        b_refs, (o_ref, acc) = refs[:len(bs)], refs[len(bs):]
        k = pl.program_id(2)

        @pl.when(k == 0)
        def _():
            acc[...] = jnp.zeros_like(acc)

        first = 0
        for b_ref in b_refs:
            cols = slice(first, first + b_ref.shape[1])
            acc[:, cols] += _dot_tn(a_ref[...], b_ref[...])
            first = cols.stop

        @pl.when(k == nt - 1)
        def _():
            o_ref[...] = (acc[...] * scale).astype(BF16)

    return pl.pallas_call(
        _drop_deps(body, 1 + len(bs), len(deps)),
        grid=(m // tm_o, n // tn_o, nt),
        in_specs=[pl.BlockSpec((tt, tm_o), lambda i, j, k: (k, i))]
        + [pl.BlockSpec((tt, piece.shape[1]), lambda i, j, k: (k, j)) for piece in bs] + [ANY] * len(deps),
        out_specs=pl.BlockSpec((tm_o, tn_o), lambda i, j, k: (i, j)),
        out_shape=pltpu.HBM((m, n), BF16),
        scratch_shapes=[pltpu.VMEM((tm_o, tn_o), F32)],
        compiler_params=_params(3),
        name=name,
    )(a, *bs, *deps)


def _mm_nt(lhs, w, name):
    t, d = lhs.shape
    kd = w.shape[0]
    tm = _row_tile(t)

    def body(l_ref, w_ref, o_ref):
        _store_slabs(o_ref, 0, _dot_nt(l_ref[...], w_ref[...]))

    return pl.pallas_call(
        body,
        grid=(t // tm,),
        in_specs=[pl.BlockSpec((tm, d), lambda i: (i, 0)), _resident(w)],
        out_specs=pl.BlockSpec((kd // LANES, tm, LANES), lambda i: (0, i, 0)),
        out_shape=jax.ShapeDtypeStruct((kd // LANES, t, LANES), F32),
        compiler_params=_params(1),
        name=name,
    )(lhs, w)


def _dx_ln(dy, pairs, ln, name, deps=()):
    t, d = dy.shape
    npair = len(pairs)
    pairs = [(list(lhs) if isinstance(lhs, (list, tuple)) else [lhs], w) for lhs, w in pairs]
    tm = min(t, 512 // npair)
    nt = t // tm
    n_in = 1 + sum(len(pieces) + 1 for pieces, _ in pairs) + (3 if ln is not None else 0)

    def body(*refs):
        dy_ref = refs[0]
        pos = 1
        dh = ALPHA * dy_ref[...]
        for pieces, _ in pairs:
            w_ref = refs[pos + len(pieces)]
            first = 0
            for l_ref in refs[pos:pos + len(pieces)]:
                cols = slice(first, first + l_ref.shape[1])
                dh = dh + _dot_nt(l_ref[...], w_ref[:, cols])
                first = cols.stop
            pos += len(pieces) + 1
        if ln is not None:
            xh_ref, rs_ref, g_ref = refs[pos:pos + 3]
            dyo_ref, dyb_ref, dg_ref, db_ref = refs[pos + 3:pos + 7]

            @pl.when(pl.program_id(0) == 0)
            def _():
                dg_ref[...] = jnp.zeros_like(dg_ref)
                db_ref[...] = jnp.zeros_like(db_ref)

            dyp, dg, db = _ln_bwd(dh, xh_ref[...], rs_ref[...], g_ref[...])
            dyo_ref[...] = dyp
            dyb_ref[...] = dyp.astype(BF16)
            dg_ref[...] += dg
            db_ref[...] += db
        else:
            refs[pos][...] = dh

    row = pl.BlockSpec((tm, d), lambda i: (i, 0))
    vec = pl.BlockSpec((1, d), lambda i: (0, 0))
    in_specs = [row]
    args = [dy]
    for pieces, w in pairs:
        in_specs += [pl.BlockSpec((tm, piece.shape[1]), lambda i: (i, 0)) for piece in pieces] + [_resident(w)]
        args += pieces + [w]
    if ln is not None:
        in_specs += [row, pl.BlockSpec((tm, 1), lambda i: (i, 0)), vec]
        args += list(ln)
        out_specs = [row, row, vec, vec]
        out_shape = [jax.ShapeDtypeStruct((t, d), F32), jax.ShapeDtypeStruct((t, d), BF16),
                     jax.ShapeDtypeStruct((1, d), F32), jax.ShapeDtypeStruct((1, d), F32)]
    else:
        out_specs = row
        out_shape = jax.ShapeDtypeStruct((t, d), F32)
    return pl.pallas_call(
        _drop_deps(body, n_in, len(deps)),
        grid=(nt,),
        in_specs=in_specs + [ANY] * len(deps),
        out_specs=out_specs,
        out_shape=out_shape,
        compiler_params=_params(1),
        name=name,
    )(*args, *deps)


def _hgrn_bwd(proj, oraw, dmix, states, logits, gn):
    t = proj.shape[1]
    ct = _hg_tile(t)
    nct = t // ct
    nblk = ct // HG_BLOCK
    nh = HG_HEADS
    mrows = min(ct, 256)

    def body(q_ref, fz_ref, iv_ref, gg_ref, or_ref, do_ref, st_ref, lg_ref, gn_ref, mask_ref,
             dq_ref, dfz_ref, div_ref, dgg_ref, dlg_ref, dgn_ref,
             dstate, qt_s, kt_s, k_s, b_s, eb_s, ekb_s, dec_s, dor_s, dbl_s, gr_s, dk_s, dlb_acc):
        c = pl.program_id(1)

        @pl.when(c == 0)
        def _():
            dstate[...] = jnp.zeros_like(dstate)
            dlb_acc[...] = jnp.zeros_like(dlb_acc)
            dgn_ref[...] = jnp.zeros_like(dgn_ref)

        lb = _lower_bound(lg_ref[...])
        q = q_ref[...]
        sig, nsig, f, k = _forget_terms(fz_ref[...], lb)
        logf = jnp.log(f)
        b = _mask_dot(mask_ref[0], logf)
        bend = _mask_dot(mask_ref[2], logf)
        eb = jnp.exp(b)
        ekb = jnp.exp(bend - b)
        qt_s[...] = (q * eb).astype(BF16)
        kt_s[...] = (k * ekb).astype(BF16)
        k_s[...] = k
        b_s[...] = b
        eb_s[...] = eb
        ekb_s[...] = ekb
        dec_s[...] = jnp.exp(bend)
        oraw = or_ref[...]
        r = lax.rsqrt(jnp.mean(oraw * oraw, axis=-1, keepdims=True) + LN_EPS)
        on = oraw * r
        gg = gg_ref[...]
        silu, dsilu = _silu_and_grad(gg)
        doa = do_ref[...]
        gnv = gn_ref[...]
        dgg_ref[...] = (doa * on * gnv * dsilu).astype(BF16)
        dyn = doa * silu
        dgn_ref[...] += jnp.sum(dyn * on, axis=0, keepdims=True)
        don = dyn * gnv
        dor_s[...] = r * (don - on * jnp.mean(don * on, axis=-1, keepdims=True))
        tidx = lax.broadcasted_iota(jnp.int32, (HG_HALF, HG_DIM), 0)

        def blk(ii, carry):
            i = nblk - 1 - ii
            r0 = pl.multiple_of(i * HG_BLOCK, HG_BLOCK)
            rows = pl.ds(r0, HG_BLOCK)
            st = st_ref[i]
            dst = dstate[...]
            dstb = dst.astype(BF16)
            do = dor_s[rows, :]
            dob = do.astype(BF16)
            v = iv_ref[rows, :]
            vb = v.astype(BF16)
            qq = q_ref[rows, :]
            kk = k_s[rows, :]
            bb = b_s[rows, :]
            qt = qt_s[rows, :]
            kt = kt_s[rows, :]
            dec = dec_s[pl.ds(r0, 1), :]
            dkt = _dot(vb, dstb)
            dq = _dot(dob, st) * eb_s[rows, :]
            dk = dkt * ekb_s[rows, :]
            dv = _dot_nt(kt, dstb)
            gend = (jnp.sum(kk * dk, axis=0, keepdims=True)
                    + dec * jnp.sum(dst * st.astype(F32), axis=0, keepdims=True))
            qh, bh, doh = _halves(qq), _halves(bb), _halves(do)
            dqh, dkh, dvh = list(_halves(dq)), list(_halves(dk)), list(_halves(dv))
            for s in range(HG_BLOCK):
                ks, vs = kk[s:s + 1, :], v[s:s + 1, :]
                dk_part = dv_part = None
                for h in _causal_halves(s):
                    e = _decay_from(bh[h], bb[s:s + 1, :], s, h, tidx)
                    ke = ks * e
                    acol = jnp.sum(qh[h] * ke, axis=1, keepdims=True)
                    dacol = jnp.sum(doh[h] * vs, axis=1, keepdims=True)
                    dqh[h] = dqh[h] + dacol * ke
                    pk = dacol * (qh[h] * e)
                    pv = acol * doh[h]
                    dk_part = pk if dk_part is None else dk_part + pk
                    dv_part = pv if dv_part is None else dv_part + pv
                hs, row = divmod(s, HG_HALF)
                dkh[hs] = dkh[hs] + jnp.where(tidx == row, jnp.sum(dk_part, axis=0, keepdims=True), 0.0)
                dvh[hs] = dvh[hs] + jnp.where(tidx == row, jnp.sum(dv_part, axis=0, keepdims=True), 0.0)
            dq = jnp.concatenate(dqh, axis=0)
            dk = jnp.concatenate(dkh, axis=0)
            dv = jnp.concatenate(dvh, axis=0)
            dq_ref[rows, :] = dq.astype(BF16)
            div_ref[rows, :] = dv.astype(BF16)
            dk_s[rows, :] = dk
            dbl_s[rows, :] = qq * dq - kk * dk
            gr_s[rows, :] = jnp.zeros((HG_BLOCK, HG_DIM), F32) + gend
            dstate[...] = dst * dec + _dot_tn(dob, qt)
            return carry

        lax.fori_loop(0, nblk, blk, 0, unroll=HG_UNROLL)
        dlogf = _mask_dot(mask_ref[1], dbl_s[...]) + gr_s[...]
        dk = dk_s[...]
        dfz_ref[...] = ((dlogf / f - dk) * ((1.0 - lb) * sig * nsig)).astype(BF16)
        dlb_acc[...] += jnp.sum((dlogf / f - dk) * nsig, axis=0, keepdims=True)

        @pl.when(c == nct - 1)
        def _():
            dl0 = dlb_acc[...] * lb * (1.0 - lb)
            layer = lax.broadcasted_iota(jnp.int32, (2, HG_DIM), 0)
            dlg_ref[...] = jnp.where(layer == 0, dl0, -dl0)

    def slab(off):
        return pl.BlockSpec((None, ct, HG_DIM), lambda h, c: (off + h, nct - 1 - c, 0))

    out_slab = pl.BlockSpec((ct, HG_DIM), lambda h, c: (nct - 1 - c, h))
    tile_f32 = pltpu.VMEM((ct, HG_DIM), F32)
    tile_b16 = pltpu.VMEM((ct, HG_DIM), BF16)
    slab_shape = pltpu.HBM((t, nh * HG_DIM), BF16)
    return pl.pallas_call(
        body,
        grid=(nh, nct),
        in_specs=[slab(0), slab(nh), slab(2 * nh), slab(3 * nh), slab(0), slab(0),
                  pl.BlockSpec((None, nblk, HG_DIM, HG_DIM), lambda h, c: (h, nct - 1 - c, 0, 0)),
                  pl.BlockSpec((None, 2, HG_DIM), lambda h, c: (h, 0, 0)),
                  pl.BlockSpec((1, HG_DIM), lambda h, c: (0, 0)),
                  pl.BlockSpec((3, mrows, mrows), lambda h, c: (0, 0, 0))],
        out_specs=[out_slab, out_slab, out_slab, out_slab,
                   pl.BlockSpec((None, 2, HG_DIM), lambda h, c: (h, 0, 0)),
                   pl.BlockSpec((None, 1, HG_DIM), lambda h, c: (h, 0, 0))],
        out_shape=[slab_shape, slab_shape, slab_shape, slab_shape,
                   jax.ShapeDtypeStruct((nh, 2, HG_DIM), F32), jax.ShapeDtypeStruct((nh, 1, HG_DIM), F32)],
        scratch_shapes=[pltpu.VMEM((HG_DIM, HG_DIM), F32), tile_b16, tile_b16, tile_f32, tile_f32, tile_f32, tile_f32,
                        tile_f32, tile_f32, tile_f32, tile_f32, tile_f32, pltpu.VMEM((1, HG_DIM), F32)],
        compiler_params=_params(2),
        name="hgrn_bwd",
    )(proj, proj, proj, proj, oraw, dmix, states, logits, gn, _block_masks(mrows))


def _sgu_bwd(proj, dmix, ln_g, ln_b, w_s, b_row):
    t = proj.shape[1]
    ct = _sg_tile(t)
    nct = t // ct
    ng = SG_GROUPS
    off_u = 4 * HG_HEADS
    off_v = off_u + ng
    n = SG_CHUNK

    def body(u_ref, v_ref, do_ref, g_ref, b_ref, w_ref, bs_ref, du_ref, dv_ref, dg_ref, db_ref, dw_ref, dbs_ref):
        c = pl.program_id(1)

        @pl.when(c == 0)
        def _():
            dg_ref[...] = jnp.zeros_like(dg_ref)
            db_ref[...] = jnp.zeros_like(db_ref)
            dw_ref[...] = jnp.zeros_like(dw_ref)
            dbs_ref[...] = jnp.zeros_like(dbs_ref)

        r = lax.broadcasted_iota(jnp.int32, (n, n), 0)
        cc = lax.broadcasted_iota(jnp.int32, (n, n), 1)
        wm = jnp.where(cc <= r, w_ref[...], 0.0).astype(BF16)
        wmt = jnp.where(r <= cc, w_ref[...].T, 0.0).astype(BF16)
        bs = _bias_by_position(bs_ref[...])
        for ci in range(ct // n):
            rows = slice(ci * n, (ci + 1) * n)
            ua, dua, dva, vn, xhat, rstd, s = _sgu_chunk_fwd(u_ref[rows, :], v_ref[rows, :], g_ref[...], b_ref[...],
                                                             wm, bs)
            do = do_ref[rows, :]
            du_ref[rows, :] = (do * s * dua).astype(BF16)
            ds = do * ua
            dsb = ds.astype(BF16)
            dbs_ref[...] += jnp.sum(ds, axis=1, keepdims=True)
            dw_ref[...] += _dot_nt(dsb, vn.astype(BF16))
            dvn = _dot(wmt, dsb)
            dva_in, dg, db = _ln_bwd(dvn, xhat, rstd, g_ref[...])
            dg_ref[...] += dg
            db_ref[...] += db
            dv_ref[rows, :] = (dva_in * dva).astype(BF16)

        @pl.when(c == nct - 1)
        def _():
            dw_ref[...] = jnp.where(cc <= r, dw_ref[...], 0.0)

    vec = pl.BlockSpec((None, 1, SG_DIM), lambda g, c: (g, 0, 0))
    mat = pl.BlockSpec((None, n, n), lambda g, c: (g, 0, 0))
    col = pl.BlockSpec((None, n, 1), lambda g, c: (g, 0, 0))
    out_slab = pl.BlockSpec((ct, SG_DIM), lambda g, c: (c, g))
    return pl.pallas_call(
        body,
        grid=(ng, nct),
        in_specs=[pl.BlockSpec((None, ct, SG_DIM), lambda g, c: (off_u + g, c, 0)),
                  pl.BlockSpec((None, ct, SG_DIM), lambda g, c: (off_v + g, c, 0)),
                  pl.BlockSpec((None, ct, SG_DIM), lambda g, c: (ng + g, c, 0)), vec, vec, mat, vec],
        out_specs=[out_slab, out_slab, vec, vec, mat, col],
        out_shape=[pltpu.HBM((t, ng * SG_DIM), BF16), pltpu.HBM((t, ng * SG_DIM), BF16),
                   jax.ShapeDtypeStruct((ng, 1, SG_DIM), F32), jax.ShapeDtypeStruct((ng, 1, SG_DIM), F32),
                   jax.ShapeDtypeStruct((ng, n, n), F32), jax.ShapeDtypeStruct((ng, n, 1), F32)],
        compiler_params=_params(2),
        name="sgu_bwd",
    )(proj, proj, dmix, ln_g, ln_b, w_s, b_row)


def _attn_bwd(dyb, wo, qb, kb, vb):
    t, d = dyb.shape
    m_len = kb.shape[0]
    tm = _row_tile(t)
    dh = d // X_HEADS
    scale = dh ** -0.5

    def body(dy_ref, wo_ref, q_ref, k_ref, v_ref, dq_ref, dk_ref, dv_ref):
        i = pl.program_id(0)

        @pl.when(i == 0)
        def _():
            dk_ref[...] = jnp.zeros_like(dk_ref)
            dv_ref[...] = jnp.zeros_like(dv_ref)

        do = _dot_nt(dy_ref[...], wo_ref[...]).astype(BF16)
        for hd in range(X_HEADS):
            sl = slice(hd * dh, (hd + 1) * dh)
            qh = q_ref[:, sl]
            p = _softmax_rows(_dot_nt(qh, k_ref[:, sl]) * scale)
            doh = do[:, sl]
            dp = _dot_nt(doh, v_ref[:, sl])
            ds = (p * (dp - jnp.sum(dp * p, axis=-1, keepdims=True)) * scale).astype(BF16)
            dq_ref[:, sl] = _dot(ds, k_ref[:, sl]).astype(BF16)
            dk_ref[:, sl] += _dot_tn(ds, qh)
            dv_ref[:, sl] += _dot_tn(p.astype(BF16), doh)

    row = pl.BlockSpec((tm, d), lambda i: (i, 0))
    full = lambda a: pl.BlockSpec(a.shape, lambda i: (0, 0))
    kv = pl.BlockSpec((m_len, d), lambda i: (0, 0))
    return pl.pallas_call(
        body,
        grid=(t // tm,),
        in_specs=[row, full(wo), row, full(kb), full(vb)],
        out_specs=[row, kv, kv],
        out_shape=[jax.ShapeDtypeStruct((t, d), BF16), jax.ShapeDtypeStruct((m_len, d), F32),
                   jax.ShapeDtypeStruct((m_len, d), F32)],
        compiler_params=_params(1),
        name="attn_bwd",
    )(dyb, wo, qb, kb, vb)


def _mem_bwd(dk, dv, mb, xhat, rstd, g, wk, wv):
    m_len, d = dk.shape

    def body(dk_ref, dv_ref, mb_ref, xh_ref, rs_ref, g_ref, wk_ref, wv_ref, gwk_ref, gwv_ref, dg_ref, db_ref):
        dkb = dk_ref[...].astype(BF16)
        dvb = dv_ref[...].astype(BF16)
        mb_v = mb_ref[...]
        gwk_ref[...] = _dot_tn(mb_v, dkb).astype(BF16)
        gwv_ref[...] = _dot_tn(mb_v, dvb).astype(BF16)
        dm = _dot_nt(dkb, wk_ref[...]) + _dot_nt(dvb, wv_ref[...])
        _, dg, db = _ln_bwd(dm, xh_ref[...], rs_ref[...], g_ref[...])
        dg_ref[...] = dg
        db_ref[...] = db

    return pl.pallas_call(
        body,
        out_shape=[jax.ShapeDtypeStruct((d, d), BF16), jax.ShapeDtypeStruct((d, d), BF16),
                   jax.ShapeDtypeStruct((1, d), F32), jax.ShapeDtypeStruct((1, d), F32)],
        compiler_params=pltpu.CompilerParams(vmem_limit_bytes=VMEM_LIMIT_V7X),
        name="mem_bwd",
    )(dk, dv, mb, xhat, rstd, g, wk, wv)


def _adamw(w, g, m, v):
    m = ADAM_B1 * m + (1.0 - ADAM_B1) * g
    v = ADAM_B2 * v + (1.0 - ADAM_B2) * (g * g)
    m_hat = m / (1.0 - ADAM_B1 ** ADAM_STEP)
    v_hat = v / (1.0 - ADAM_B2 ** ADAM_STEP)
    delta = -ADAM_LR * (m_hat / (jnp.sqrt(v_hat) + ADAM_EPS) + ADAM_WD * w)
    return delta, m, v


def _slot_sum(ref):
    g = ref[0].astype(F32)
    for s in range(1, N_DEV):
        g = g + ref[s].astype(F32)
    return g


def _adam_sharded(lands, w, m, v, axis, name):
    rows, cols = w.shape
    nl = len(lands)
    transposed = axis == 1 and nl == 2
    if transposed:
        rows, cols = cols, rows
        tr = 256
        grid = (rows // tr,)
        wblk = pl.BlockSpec((cols, tr), lambda i: (0, i))
        lblk = [pl.BlockSpec((N_DEV, tr, a.shape[2]), lambda i: (0, i, 0)) for a in lands]
    elif axis == 1:
        tr = 256 if rows % 256 == 0 else rows
        grid = (rows // tr,)
        wblk = pl.BlockSpec((tr, cols), lambda i: (i, 0))
        lblk = [pl.BlockSpec((N_DEV, tr, a.shape[2]), lambda i: (0, i, 0)) for a in lands]
    else:
        tc = _col_tile(cols)
        grid = (cols // tc,)
        wblk = pl.BlockSpec((rows, tc), lambda i: (0, i))
        lblk = [pl.BlockSpec((N_DEV, a.shape[1], tc), lambda i: (0, 0, i)) for a in lands]

    def body(*refs):
        w_ref, m_ref, v_ref = refs[nl:nl + 3]
        g_ref, d_ref, nm_ref, nv_ref = refs[nl + 3:]
        g = _slot_sum(refs[0])
        if nl == 2:
            tail = _slot_sum(refs[1])
            if transposed:
                g = jnp.concatenate([g.T, tail.T[:cols - g.shape[1], :]], axis=0)
            elif axis == 1:
                g = jnp.concatenate([g, tail[:, :cols - g.shape[1]]], axis=1)
            else:
                g = jnp.concatenate([g, tail[:rows - g.shape[0], :]], axis=0)
        delta, nm, nv = _adamw(w_ref[...], g, m_ref[...], v_ref[...])
        g_ref[...] = g
        d_ref[...] = delta
        nm_ref[...] = nm
        nv_ref[...] = nv

    shp = pltpu.HBM(w.shape, F32)
    return pl.pallas_call(
        body,
        grid=grid,
        in_specs=lblk + [wblk, wblk, wblk],
        out_specs=[wblk, wblk, wblk, wblk],
        out_shape=[shp, shp, shp, shp],
        compiler_params=_params(1),
        name=name,
    )(*[pltpu.with_memory_space_constraint(a, pltpu.HBM) for a in (*lands, w, m, v)])


def _mesh_pos():
    return lax.axis_index("x"), lax.axis_index("y"), lax.axis_index("c")


def _peer(k):
    x, y, c = _mesh_pos()
    pos = (x ^ (k >> 2), y ^ ((k >> 1) & 1), c ^ (k & 1))
    return pos, 4 * pos[0] + 2 * pos[1] + pos[2]


def _sem_index(row, k):
    return row * (N_DEV - 1) + k - 1


def _window(ref, axis, start, size):
    align = 16 if axis == 0 else LANES
    start = pl.multiple_of(start, align)
    return ref.at[pl.ds(start, size), :] if axis == 0 else ref.at[:, pl.ds(start, size)]


def _piece_refs(piece, srcs, lands, me, peer):
    kind, si, li, axis, base, stride, shape = piece
    if kind == "gather":
        return srcs[si], _window(lands[li], axis, base + stride * me, shape[axis])
    return _window(srcs[si], axis, base + stride * peer, shape[axis]), lands[li].at[me]


def _place_own(srcs, land_shapes, pieces, name):
    ns, nl, npc = len(srcs), len(land_shapes), len(pieces)

    def body(*refs):
        s_refs = refs[:ns]
        l_refs = refs[ns:ns + nl]
        bufs = refs[ns + nl:ns + nl + npc]
        sems = refs[ns + nl + npc]
        x, y, c = _mesh_pos()
        me = 4 * x + 2 * y + c
        loads = []
        for p, piece in enumerate(pieces):
            src, dst = _piece_refs(piece, s_refs, l_refs, me, me)
            cp = pltpu.make_async_copy(src, bufs[p], sems.at[0, p])
            cp.start()
            loads.append((cp, dst))
        stores = []
        for p, (cp, dst) in enumerate(loads):
            cp.wait()
            out = pltpu.make_async_copy(bufs[p], dst, sems.at[1, p])
            out.start()
            stores.append(out)
        for out in stores:
            out.wait()

    out = pl.pallas_call(
        body,
        in_specs=[ANY] * ns,
        out_specs=[HBM] * nl,
        out_shape=[pltpu.HBM(s.shape, s.dtype) for s in land_shapes],
        scratch_shapes=[pltpu.VMEM(pc[6], srcs[pc[1]].dtype) for pc in pieces] + [pltpu.SemaphoreType.DMA((2, npc))],
        compiler_params=pltpu.CompilerParams(vmem_limit_bytes=VMEM_LIMIT_V7X),
        name=name,
    )(*srcs)
    return list(out)


def _comm_start(srcs, lands, pieces, groups, name, after=()):
    ns, nl, na, ng = len(srcs), len(lands), len(after), len(groups)

    def body(*refs):
        s_refs = refs[:ns]
        l_refs = refs[ns:ns + nl]
        outs = refs[ns + nl + na:]
        sems = outs[:2 * ng]
        token = outs[-1]
        x, y, c = _mesh_pos()
        me = 4 * x + 2 * y + c
        for g, members in enumerate(groups):
            for row, p in enumerate(members):
                for k in range(1, N_DEV):
                    pos, peer = _peer(k)
                    src, dst = _piece_refs(pieces[p], s_refs, l_refs, me, peer)
                    pltpu.make_async_remote_copy(src_ref=src, dst_ref=dst, send_sem=sems[2 * g].at[_sem_index(row, k)],
                                                 recv_sem=sems[2 * g + 1].at[_sem_index(row, k)], device_id=pos,
                                                 device_id_type=MESH_ID).start()
        token[...] = jnp.zeros_like(token)

    sem_shapes = []
    for members in groups:
        sem_shapes += [pltpu.SemaphoreType.DMA((len(members) * (N_DEV - 1),))] * 2
    hbm_of = lambda a: pltpu.HBM(a.shape, a.dtype)
    out = pl.pallas_call(
        body,
        in_specs=[HBM] * (ns + nl) + [ANY] * na,
        out_specs=[SEM] * (2 * ng) + [HBM] * (ns + nl) + [pl.BlockSpec(memory_space=pltpu.VMEM)],
        out_shape=sem_shapes + [hbm_of(a) for a in srcs] + [hbm_of(a) for a in lands]
        + [jax.ShapeDtypeStruct((8, LANES), F32)],
        input_output_aliases={i: 2 * ng + i for i in range(ns + nl)},
        compiler_params=pltpu.CompilerParams(has_side_effects=DATAFLOW),
        name=name,
    )(*[pltpu.with_memory_space_constraint(a, pltpu.HBM) for a in list(srcs) + list(lands)], *after)
    sems = [(out[2 * g], out[2 * g + 1]) for g in range(ng)]
    return sems, list(out[2 * ng:2 * ng + ns]), list(out[2 * ng + ns:2 * ng + ns + nl]), out[-1]


def _comm_wait(srcs, lands, pieces, members, sems, after, name):
    ns, nl, na = len(srcs), len(lands), len(after)

    def body(*refs):
        s_refs = refs[:ns]
        l_refs = refs[ns:ns + nl]
        send_sems, recv_sems = refs[ns + nl:ns + nl + 2]
        x, y, c = _mesh_pos()
        me = 4 * x + 2 * y + c
        for row, p in enumerate(members):
            for k in range(1, N_DEV):
                pos, peer = _peer(k)
                src, dst = _piece_refs(pieces[p], s_refs, l_refs, me, peer)
                cp = pltpu.make_async_remote_copy(src_ref=src, dst_ref=dst, send_sem=send_sems.at[_sem_index(row, k)],
                                                  recv_sem=recv_sems.at[_sem_index(row, k)], device_id=pos,
                                                  device_id_type=MESH_ID)
                cp.wait_send()
                cp.wait_recv()

    hbm_of = lambda a: pltpu.HBM(a.shape, a.dtype)
    out = pl.pallas_call(
        body,
        in_specs=[HBM] * (ns + nl) + [SEM, SEM] + [ANY] * na,
        out_specs=[HBM] * (ns + nl),
        out_shape=[hbm_of(a) for a in srcs] + [hbm_of(a) for a in lands],
        input_output_aliases={i: i for i in range(ns + nl)},
        compiler_params=pltpu.CompilerParams(has_side_effects=DATAFLOW),
        name=name,
    )(*srcs, *lands, sems[0], sems[1], *after)
    return list(out[ns:])


def _landed_block(piece, lands, owner):
    _, _, li, axis, base, stride, shape = piece
    return _window(lands[li], axis, base + stride * owner, shape[axis])


def _copy_stage(name, bufs, in_sems, out_sem_sizes, emit, after=()):
    nb, ni, no, na = len(bufs), len(in_sems), len(out_sem_sizes), len(after)

    def body(*refs):
        b_refs = refs[:nb]
        i_refs = refs[nb:nb + ni]
        o_refs = refs[nb + ni + na:nb + ni + na + no]
        emit(b_refs, i_refs, o_refs)
        refs[-1][...] = jnp.zeros_like(refs[-1])

    hbm_of = lambda a: pltpu.HBM(a.shape, a.dtype)
    out = pl.pallas_call(
        body,
        in_specs=[HBM] * nb + [SEM] * ni + [ANY] * na,
        out_specs=[SEM] * no + [HBM] * nb + [pl.BlockSpec(memory_space=pltpu.VMEM)],
        out_shape=[pltpu.SemaphoreType.DMA((n,)) for n in out_sem_sizes] + [hbm_of(a) for a in bufs]
        + [jax.ShapeDtypeStruct((8, LANES), F32)],
        input_output_aliases={i: no + i for i in range(nb)},
        compiler_params=pltpu.CompilerParams(has_side_effects=DATAFLOW),
        name=name,
    )(*[pltpu.with_memory_space_constraint(a, pltpu.HBM) for a in bufs], *in_sems, *after)
    return list(out[:no]), list(out[no:no + nb]), out[-1]


def _remote(src, dst, send, recv, to):
    return pltpu.make_async_remote_copy(src_ref=src, dst_ref=dst, send_sem=send, recv_sem=recv, device_id=to,
                                        device_id_type=MESH_ID)


def _routed_gather(srcs, lands, pieces, meanwhile, then, name):
    ns, npc = len(srcs), len(pieces)

    def places():
        x, y, c = _mesh_pos()
        index = lambda p: 4 * p[0] + 2 * p[1] + p[2]
        me, sib = (x, y, c), (x, y, 1 - c)
        xnb, ynb = (1 - x, y, c), (x, 1 - y, c)
        got_first = (x ^ (1 - c), y ^ c, c)
        pass_to = (x ^ c, y ^ (1 - c), c)
        diag = (1 - x, 1 - y, c)
        return index, me, sib, xnb, ynb, got_first, pass_to, diag

    def start(b, _, o):
        index, me, sib, xnb, ynb, *_rest = places()
        send_a, recv_sib, recv_nb = o
        for p, piece in enumerate(pieces):
            src, dst = _piece_refs(piece, b[:ns], b[ns:], index(me), 0)
            _remote(src, dst, send_a.at[3 * p], recv_sib.at[p], sib).start()
            _remote(src, dst, send_a.at[3 * p + 1], recv_nb.at[2 * p], xnb).start()
            _remote(src, dst, send_a.at[3 * p + 2], recv_nb.at[2 * p + 1], ynb).start()

    def pass_a(b, i, o):
        index, me, sib, xnb, ynb, got_first, pass_to, _diag = places()
        (recv_nb,) = i
        send_f, recv_f, send_d, recv_d = o
        for p, piece in enumerate(pieces):
            for j, nb in enumerate((xnb, ynb)):
                blk = _landed_block(piece, b, index(nb))
                _remote(blk, blk, send_f.at[2 * p + j], recv_nb.at[2 * p + j], sib).wait_recv()
                _remote(blk, blk, send_f.at[2 * p + j], recv_f.at[2 * p + j], sib).start()
            blk = _landed_block(piece, b, index(got_first))
            _remote(blk, blk, send_d.at[p], recv_d.at[p], pass_to).start()

    def pass_b(b, i, o):
        index, me, sib, *_mid, diag = places()
        (recv_d,) = i
        send_g, recv_g = o
        for p, piece in enumerate(pieces):
            blk = _landed_block(piece, b, index(diag))
            _remote(blk, blk, send_g.at[p], recv_d.at[p], sib).wait_recv()
            _remote(blk, blk, send_g.at[p], recv_g.at[p], sib).start()

    def last(b, i, _):
        index, me, sib, *_others = places()
        send_a, recv_sib, send_f, recv_f, send_d, send_g, recv_g = i
        for p, piece in enumerate(pieces):
            src, dst = _piece_refs(piece, b[:ns], b[ns:], index(me), 0)
            cp = lambda s_sem, r_sem: _remote(src, dst, s_sem, r_sem, sib)
            cp(send_a.at[3 * p], recv_sib.at[p]).wait_recv()
            cp(send_a.at[3 * p], recv_g.at[p]).wait_recv()
            for j in range(3):
                cp(send_a.at[3 * p + j], recv_sib.at[p]).wait_send()
            for j in range(2):
                cp(send_f.at[2 * p + j], recv_f.at[2 * p + j]).wait_recv()
                cp(send_f.at[2 * p + j], recv_f.at[2 * p + j]).wait_send()
            cp(send_d.at[p], recv_sib.at[p]).wait_send()
            cp(send_g.at[p], recv_sib.at[p]).wait_send()

    (send_a, recv_sib, recv_nb), bufs, started = _copy_stage(name + "_start", list(srcs) + list(lands), [],
                                                             [3 * npc, npc, 2 * npc], start)
    srcs, lands = bufs[:ns], bufs[ns:]
    (send_f, recv_f, send_d, recv_d), lands, passed = _copy_stage(name + "_pass_a", lands, [recv_nb],
                                                                  [2 * npc, 2 * npc, npc, npc],
                                                                  lambda b, i, o: pass_a(b, i, o),
                                                                  after=meanwhile(started))
    (send_g, recv_g), lands, tok = _copy_stage(name + "_pass_b", lands, [recv_d], [npc, npc], pass_b,
                                               after=then(passed))
    _, bufs, _ = _copy_stage(name + "_last", list(srcs) + list(lands),
                             [send_a, recv_sib, send_f, recv_f, send_d, send_g, recv_g], [], last)
    return bufs[ns:], tok


def _paired_gather(srcs, lands, pieces, groups, after, name):
    ns, ng = len(srcs), len(groups)
    far = (1, 2, 3)

    def me_sib():
        x, y, c = _mesh_pos()
        return 4 * x + 2 * y + c, (x, y, 1 - c)

    def start(b, _, o):
        me, sib = me_sib()
        for g, members in enumerate(groups):
            send, recv_sib, recv_far = o[3 * g:3 * g + 3]
            for r, p in enumerate(members):
                src, dst = _piece_refs(pieces[p], b[:ns], b[ns:], me, 0)
                _remote(src, dst, send.at[4 * r], recv_sib.at[r], sib).start()
                for j in far:
                    _remote(src, dst, send.at[4 * r + j], recv_far.at[3 * r + j - 1], _peer(2 * j)[0]).start()

    def forward(b, i, o):
        _, sib = me_sib()
        for g, members in enumerate(groups):
            send_f, recv_f = o[2 * g:2 * g + 2]
            for r, p in enumerate(members):
                for j in far:
                    blk = _landed_block(pieces[p], b, _peer(2 * j)[1])
                    _remote(blk, blk, send_f.at[3 * r + j - 1], i[g].at[3 * r + j - 1], sib).wait_recv()
                    _remote(blk, blk, send_f.at[3 * r + j - 1], recv_f.at[3 * r + j - 1], sib).start()

    def last(sub):
        def emit(b, i, _):
            send, recv_sib, send_f, recv_f = i
            me, sib = me_sib()
            for r, piece in enumerate(sub):
                src, dst = _piece_refs(piece, b[:-1], b[-1:], me, 0)
                _remote(src, dst, send.at[4 * r], recv_sib.at[r], sib).wait_recv()
                for j in range(4):
                    _remote(src, dst, send.at[4 * r + j], recv_sib.at[r], sib).wait_send()
                for j in far:
                    blk = _landed_block(piece, b[-1:], _peer(2 * j + 1)[1])
                    _remote(blk, blk, send_f.at[3 * r + j - 1], recv_f.at[3 * r + j - 1], sib).wait_recv()
                    _remote(blk, blk, send_f.at[3 * r + j - 1], recv_f.at[3 * r + j - 1], sib).wait_send()
        return emit

    sizes = []
    for members in groups:
        sizes += [4 * len(members), len(members), 3 * len(members)]
    sems, bufs, started = _copy_stage(name + "_start", list(srcs) + list(lands), [], sizes, start, after=after)
    srcs = bufs[:ns]
    state = dict(lands=bufs[ns:])

    def finish(g, after):
        if "passed" not in state:
            sizes_f = []
            for members in groups:
                sizes_f += [3 * len(members)] * 2
            state["passed"], state["lands"], _ = _copy_stage(name + "_pass", state["lands"],
                                                             [sems[3 * k + 2] for k in range(ng)], sizes_f, forward,
                                                             after=after)
            after = ()
        members = groups[g]
        sub = [(pieces[p][0], r, 0) + pieces[p][3:] for r, p in enumerate(members)]
        _, bufs, _ = _copy_stage("%s_last_%d" % (name, g), [srcs[pieces[p][1]] for p in members] + [state["lands"][g]],
                                 [sems[3 * g], sems[3 * g + 1], state["passed"][2 * g], state["passed"][2 * g + 1]],
                                 [], last(sub), after=after)
        return bufs[-1]

    return finish, started


_SMALL_NAMES = ("ln1_g", "ln1_b", "hg_lb_logits", "hg_norm_g", "sg_ln_g", "sg_ln_b", "sg_w_s", "sg_b_s",
                "ln2_g", "ln2_b", "mem_ln_g", "mem_ln_b", "ln3_g", "ln3_b", "ln4_g", "ln4_b")


_VEC_NAMES = ("ln1_g", "ln1_b", "ln2_g", "ln2_b", "mem_ln_g", "mem_ln_b", "ln3_g", "ln3_b", "ln4_g", "ln4_b")
_ROW_NAMES = ("hg_lb_logits", "hg_norm_g", "sg_ln_g", "sg_ln_b", "sg_b_s", "sg_w_s")
VEC_ROWS = 16


def _row_plan(shapes):
    plan, pos = {}, 0
    for k in _ROW_NAMES:
        shp = shapes[k]
        slabs, off = [], pos
        for idx in itertools.product(*[range(dim) for dim in shp[:-2]]):
            slabs.append((idx, off, shp[-2]))
            off += shp[-2]
        plan[k] = (pos, slabs)
        pos = -(-off // 8) * 8
    return plan, -(-pos // 16) * 16


def _pack_small_grads(gs, shapes, loss):
    d = gs[_VEC_NAMES[0]].size
    vec = jnp.concatenate([gs[k].reshape(1, -1) for k in _VEC_NAMES] + [jnp.tile(loss, (1, d // LANES))], axis=0)
    vec = jnp.pad(vec, ((0, VEC_ROWS - vec.shape[0]), (0, 0)))
    plan, total = _row_plan(shapes)
    parts, pos = [], 0
    for k in _ROW_NAMES:
        first, slabs = plan[k]
        rows = gs[k].reshape(-1, LANES)
        end = slabs[-1][1] + slabs[-1][2]
        nxt = -(-end // 8) * 8
        parts.append(jnp.pad(rows, ((0, nxt - first - rows.shape[0]), (0, 0))))
        pos = nxt
    parts.append(jnp.zeros((total - pos, LANES), F32))
    return vec, jnp.concatenate(parts, axis=0)


def _adam_small(land_vec, land_rows, w, m, v):
    names = _VEC_NAMES + _ROW_NAMES
    n = len(names)
    shapes = {k: w[k].shape for k in names}
    plan, _ = _row_plan(shapes)

    def body(*refs):
        lv_ref, lr_ref = refs[:2]
        w_refs, m_refs, v_refs = refs[2:2 + n], refs[2 + n:2 + 2 * n], refs[2 + 2 * n:2 + 3 * n]
        outs = refs[2 + 3 * n:2 + 7 * n]
        loss_ref = refs[2 + 7 * n]
        gv_s, gr_s = refs[3 + 7 * n:]
        gv_s[...] = _slot_sum(lv_ref)
        gr_s[...] = _slot_sum(lr_ref)
        loss_ref[...] = gv_s[len(_VEC_NAMES):len(_VEC_NAMES) + 1, :LANES]
        for p, k in enumerate(names):
            if k in _VEC_NAMES:
                row = _VEC_NAMES.index(k)
                slabs = [((), None, None)]
            else:
                slabs = plan[k][1]
            for idx, off, rows in slabs:
                g = gv_s[row:row + 1, :] if off is None else gr_s[off:off + rows, :]
                sel = idx + (slice(None), slice(None))
                delta, nm, nv = _adamw(w_refs[p][sel], g, m_refs[p][sel], v_refs[p][sel])
                for o, val in zip(range(4), (g, delta, nm, nv)):
                    outs[o * n + p][sel] = val

    flat = lambda tree: [tree[k] for k in names]
    shp = [jax.ShapeDtypeStruct(shapes[k], F32) for k in names]
    out = pl.pallas_call(
        body,
        out_shape=shp * 4 + [jax.ShapeDtypeStruct((1, LANES), F32)],
        scratch_shapes=[pltpu.VMEM(land_vec.shape[1:], F32), pltpu.VMEM(land_rows.shape[1:], F32)],
        name="adam_small",
    )(land_vec, land_rows, *flat(w), *flat(m), *flat(v))
    return [dict(zip(names, out[o * n:(o + 1) * n])) for o in range(4)], out[4 * n]


_COL_FFN = ("ffn1_w_gate", "ffn1_w_up", "ffn2_w_gate", "ffn2_w_up")
_ROW_FFN = ("ffn1_w_down", "ffn2_w_down")
_ROW_SQ = ("w_out", "xa_w_q", "xa_w_k", "xa_w_v", "xa_w_o")
_BIG_NAMES = ("ffn1_w_gate", "ffn1_w_up", "ffn1_w_down", "w_in", "w_out", "xa_w_q", "xa_w_k", "xa_w_v", "xa_w_o",
              "ffn2_w_gate", "ffn2_w_up", "ffn2_w_down")


def _ffn_split(fs):
    main = (fs // MXU_WIDTH_V7X) * MXU_WIDTH_V7X
    tail = fs - main
    tail_pad = -(-tail // LANES) * LANES
    assert main > 0 and tail > 0
    return main, tail, tail_pad


def _layout(name, shard_shape):
    r, c = shard_shape
    if name in _COL_FFN:
        main, tail, pad = _ffn_split(c)
        return (r, N_DEV * (main + pad)), [(1, 0, main, (r, main), (0, main)),
                                           (1, N_DEV * main, pad, (r, pad), (main, c))]
    if name in _ROW_FFN:
        main, tail, pad = _ffn_split(r)
        return (N_DEV * (main + pad), c), [(0, 0, main, (main, c), (0, main)),
                                           (0, N_DEV * main, pad, (pad, c), (main, r))]
    if name == "w_in":
        return (r, N_DEV * c), [(1, 0, c, (r, c), (0, c))]
    return (N_DEV * r, c), [(0, 0, r, (r, c), (0, r))]


def _shard_pieces(name, shard):
    out = []
    for axis, _, _, shape, (lo, hi) in _layout(name, shard.shape)[1]:
        part = shard[lo:hi, :] if axis == 0 else shard[:, lo:hi]
        pad = [(0, shape[0] - part.shape[0]), (0, shape[1] - part.shape[1])]
        out.append(jnp.pad(part, pad).astype(BF16))
    return out


def _gather_plan(names, shards):
    srcs, land_shapes, pieces, index = [], [], [], {}
    for li, name in enumerate(names):
        shape2d, parts = _layout(name, shards[name].shape)
        land_shapes.append(jax.ShapeDtypeStruct(shape2d, BF16))
        index[name] = []
        for (axis, base, stride, shape, _), src in zip(parts, _shard_pieces(name, shards[name])):
            index[name].append(len(pieces))
            pieces.append(("gather", len(srcs), li, axis, base, stride, shape))
            srcs.append(src)
    return srcs, land_shapes, pieces, index


def _scatter_plan(names, grads, shard_shapes):
    srcs, land_shapes, pieces, index = [], [], [], {}
    for si, name in enumerate(names):
        _, parts = _layout(name, shard_shapes[name])
        srcs.append(grads[name])
        index[name] = []
        for axis, base, stride, shape, _ in parts:
            index[name].append(len(land_shapes))
            pieces.append(("scatter", si, len(land_shapes), axis, base, stride, shape))
            land_shapes.append(jax.ShapeDtypeStruct((N_DEV,) + shape, grads[name].dtype))
    return srcs, land_shapes, pieces, index


def _small_views(small):
    row = lambda a: a.reshape(1, -1)
    ln = {k: row(small[k]) for k in ("ln1_g", "ln1_b", "ln2_g", "ln2_b", "ln3_g", "ln3_b", "ln4_g", "ln4_b",
                                      "mem_ln_g", "mem_ln_b", "hg_norm_g")}
    sg_w = small["sg_w_s"].reshape(SG_GROUPS, SG_CHUNK, SG_CHUNK)
    sg = dict(logits=jnp.swapaxes(small["hg_lb_logits"], 0, 1),
              g=small["sg_ln_g"].reshape(SG_GROUPS, 1, SG_DIM), b=small["sg_ln_b"].reshape(SG_GROUPS, 1, SG_DIM),
              w=sg_w, bs=small["sg_b_s"].reshape(SG_GROUPS, 1, SG_CHUNK))
    return ln, sg


def _forward(x, xb, mem, target, get_w, small, first_deps=()):
    ln, sg = _small_views(small)
    a1, b1, s1 = _ffn_up(xb, get_w("ffn1_w_gate", ()), get_w("ffn1_w_up", ()), "ffn1_up", deps=first_deps)
    h1b, xh1, rs1 = _mm_res_ln(s1, get_w("ffn1_w_down", (s1,)), x, ln["ln1_g"], ln["ln1_b"], 0.5, "ffn1_down_ln")
    proj = _mm_nn(h1b, get_w("w_in", (h1b,)), "mix_in")
    oraw, mix, states = _hgrn_fwd(proj, sg["logits"], ln["hg_norm_g"])
    mix = _sgu_fwd(proj, mix, sg["g"], sg["b"], sg["w"], sg["bs"])
    h2b, xh2, rs2 = _mm_res_ln(mix, get_w("w_out", (mix,)), (xh1, ln["ln1_g"], ln["ln1_b"]), ln["ln2_g"], ln["ln2_b"],
                               1.0, "mix_out_ln")
    mb, mxh, mrs, kb, vb = _mem_kv(mem, ln["mem_ln_g"], ln["mem_ln_b"], get_w("xa_w_k", (h2b,)), get_w("xa_w_v", (h2b,)))
    qb, att = _attn_fwd(h2b, get_w("xa_w_q", (mrs,)), kb, vb)
    h3b, xh3, rs3 = _mm_res_ln(att, get_w("xa_w_o", (att,)), (xh2, ln["ln2_g"], ln["ln2_b"]), ln["ln3_g"], ln["ln3_b"],
                               1.0, "attn_out_ln")
    a2, b2, s2 = _ffn_up(h3b, get_w("ffn2_w_gate", (h3b,)), get_w("ffn2_w_up", (h3b,)), "ffn2_up")
    loss, dy4, dy4b, dg4, db4 = _mm_res_ln(s2, get_w("ffn2_w_down", (s2,)), (xh3, ln["ln3_g"], ln["ln3_b"]),
                                           ln["ln4_g"], ln["ln4_b"], 0.5, "ffn2_down_ln_loss", target=target)
    return dict(xb=xb, a1=a1, b1=b1, s1=s1, h1b=h1b, xh1=xh1, rs1=rs1, proj=proj, oraw=oraw, mix=mix, states=states,
                h2b=h2b, xh2=xh2, rs2=rs2, mb=mb, mxh=mxh, mrs=mrs, kb=kb, vb=vb, qb=qb, att=att, h3b=h3b, xh3=xh3,
                rs3=rs3, a2=a2, b2=b2, s2=s2, loss=loss, dy4=dy4, dy4b=dy4b, dg4=dg4, db4=db4)


def _backward(sv, wt, small, send):
    ln, sg = _small_views(small)
    gs = {"ln4_g": sv["dg4"], "ln4_b": sv["db4"]}
    loss, dy4, dy4b = sv["loss"], sv["dy4"], sv["dy4b"]
    g_down2 = _mm_tn(sv["s2"], dy4b, "g_ffn2_down", scale=0.5)
    da2, db2, dy3, dy3b, gs["ln3_g"], gs["ln3_b"] = _ffn_bwd_fused(
        dy4, dy4b, wt["ffn2_w_down"], wt["ffn2_w_gate"], wt["ffn2_w_up"], sv["a2"], sv["b2"], 0.5,
        (sv["xh3"], sv["rs3"], ln["ln3_g"]), "ffn2_bwd")
    g_gate2 = _mm_tn(sv["h3b"], da2, "g_ffn2_gate")
    g_up2 = _mm_tn(sv["h3b"], db2, "g_ffn2_up")
    tok = send(("ffn2_w_down", "ffn2_w_gate", "ffn2_w_up"), (g_down2, g_gate2, g_up2))

    g_o = _mm_tn(sv["att"], dy3b, "g_xa_o", deps=(tok,))
    dqb, dk, dv = _attn_bwd(dy3b, wt["xa_w_o"], sv["qb"], sv["kb"], sv["vb"])
    g_q = _mm_tn(sv["h2b"], dqb, "g_xa_q")
    g_k, g_v, gs["mem_ln_g"], gs["mem_ln_b"] = _mem_bwd(dk, dv, sv["mb"], sv["mxh"], sv["mrs"], ln["mem_ln_g"],
                                                        wt["xa_w_k"], wt["xa_w_v"])
    tok = send(("xa_w_o", "xa_w_q", "xa_w_k", "xa_w_v"), (g_o, g_q, g_k, g_v))
    dy2, dy2b, gs["ln2_g"], gs["ln2_b"] = _dx_ln(dy3, [(dqb, wt["xa_w_q"])], (sv["xh2"], sv["rs2"], ln["ln2_g"]),
                                                 "attn_dx_ln", deps=(tok,))

    g_out = _mm_tn(sv["mix"], dy2b, "g_w_out")
    dmix = _mm_nt(dy2b, wt["w_out"], "mix_out_bwd")
    dq, dfz, div, dgg, dlg, dgn = _hgrn_bwd(sv["proj"], sv["oraw"], dmix, sv["states"], sg["logits"], ln["hg_norm_g"])
    du, dvv, gs["sg_ln_g"], gs["sg_ln_b"], gs["sg_w_s"], gs["sg_b_s"] = _sgu_bwd(
        sv["proj"], dmix, sg["g"], sg["b"], sg["w"], sg["bs"])
    gs["hg_lb_logits"] = jnp.swapaxes(dlg, 0, 1)
    gs["hg_norm_g"] = jnp.sum(dgn, axis=0)
    dproj = [dq, dfz, div, dgg, du, dvv]
    g_in = _mm_tn(sv["h1b"], dproj, "g_w_in")
    tok = send(("w_out", "w_in"), (g_out, g_in))
    dy1, dy1b, gs["ln1_g"], gs["ln1_b"] = _dx_ln(dy2, [(dproj, wt["w_in"])], (sv["xh1"], sv["rs1"], ln["ln1_g"]),
                                                 "mix_dx_ln", deps=(tok,))

    g_down1 = _mm_tn(sv["s1"], dy1b, "g_ffn1_down", scale=0.5)
    tok = send(("ffn1_w_down",), (g_down1,))
    da1, db1 = _ffn_bwd_act(dy1b, wt["ffn1_w_down"], sv["a1"], sv["b1"], 0.5, "ffn1_bwd_act", deps=(tok,))
    g_gate1 = _mm_tn(sv["xb"], da1, "g_ffn1_gate")
    tok = send(("ffn1_w_gate",), (g_gate1,))
    g_up1 = _mm_tn(sv["xb"], db1, "g_ffn1_up", deps=(tok,))
    tok = send(("ffn1_w_up",), (g_up1,))
    grad_x = _dx_ln(dy1, [(da1, wt["ffn1_w_gate"]), (db1, wt["ffn1_w_up"])], None, "ffn1_dx", deps=(tok,))
    return loss, grad_x, gs


_WEIGHT_NAMES = ("ffn1_w_gate", "ffn1_w_up", "ffn1_w_down", "ln1_g", "ln1_b", "w_in", "hg_lb_logits", "hg_norm_g",
                 "sg_ln_g", "sg_ln_b", "sg_w_s", "sg_b_s", "w_out", "ln2_g", "ln2_b", "mem_ln_g", "mem_ln_b",
                 "xa_w_q", "xa_w_k", "xa_w_v", "xa_w_o", "ln3_g", "ln3_b", "ffn2_w_gate", "ffn2_w_up", "ffn2_w_down",
                 "ln4_g", "ln4_b")
_FIRST = ("ffn1_w_gate", "ffn1_w_up")
_SECOND = ("ffn1_w_down", "w_in")
_THIRD = ("w_out", "xa_w_k", "xa_w_v", "xa_w_q", "xa_w_o", "ffn2_w_gate", "ffn2_w_up", "ffn2_w_down")


def kernel(x, mem, ffn1_w_gate, ffn1_w_up, ffn1_w_down, ln1_g, ln1_b, w_in, hg_lb_logits, hg_norm_g, sg_ln_g, sg_ln_b, sg_w_s, sg_b_s, w_out, ln2_g, ln2_b, mem_ln_g, mem_ln_b, xa_w_q, xa_w_k, xa_w_v, xa_w_o, ln3_g, ln3_b, ffn2_w_gate, ffn2_w_up, ffn2_w_down, ln4_g, ln4_b, loss_target, m_ffn1_w_gate, m_ffn1_w_up, m_ffn1_w_down, m_ln1_g, m_ln1_b, m_w_in, m_hg_lb_logits, m_hg_norm_g, m_sg_ln_g, m_sg_ln_b, m_sg_w_s, m_sg_b_s, m_w_out, m_ln2_g, m_ln2_b, m_mem_ln_g, m_mem_ln_b, m_xa_w_q, m_xa_w_k, m_xa_w_v, m_xa_w_o, m_ln3_g, m_ln3_b, m_ffn2_w_gate, m_ffn2_w_up, m_ffn2_w_down, m_ln4_g, m_ln4_b, v_ffn1_w_gate, v_ffn1_w_up, v_ffn1_w_down, v_ln1_g, v_ln1_b, v_w_in, v_hg_lb_logits, v_hg_norm_g, v_sg_ln_g, v_sg_ln_b, v_sg_w_s, v_sg_b_s, v_w_out, v_ln2_g, v_ln2_b, v_mem_ln_g, v_mem_ln_b, v_xa_w_q, v_xa_w_k, v_xa_w_v, v_xa_w_o, v_ln3_g, v_ln3_b, v_ffn2_w_gate, v_ffn2_w_up, v_ffn2_w_down, v_ln4_g, v_ln4_b):
    args = dict(locals())
    w = {k: args[k] for k in _WEIGHT_NAMES}
    m = {k: args["m_" + k] for k in _WEIGHT_NAMES}
    v = {k: args["v_" + k] for k in _WEIGHT_NAMES}
    shards = {k: w[k][0] for k in _BIG_NAMES}
    shard_shapes = {k: shards[k].shape for k in _BIG_NAMES}
    small = {k: (w[k][0] if k != "hg_lb_logits" else w[k]) for k in _SMALL_NAMES}

    srcs1, shapes1, pieces1, idx1 = _gather_plan(_FIRST, shards)
    lands1 = _place_own(srcs1, shapes1, pieces1, "gather_first_own")
    prepared = {}

    def prepare_rest(started):
        later = {k: shards[k] + started[0, 0] for k in _SECOND + _THIRD}
        placed = ()
        for key, names in (("second", _SECOND), ("third", _THIRD)):
            srcs, shapes, pieces, idx = _gather_plan(names, later)
            lands = _place_own(srcs, shapes, pieces, "gather_%s_own" % key)
            prepared[key] = (srcs, lands, pieces, idx)
            placed += tuple(lands)
        prepared["xb"] = _to_bf16(x[0], "x_bf16", deps=(started,))
        return placed + (prepared["xb"],)

    def start_rest(passed):
        srcs_p, lands_p, pieces_p, idx_p = prepared["second"]
        prepared["finish_second"], tok_p = _paired_gather(srcs_p, lands_p, pieces_p, [list(idx_p[k]) for k in _SECOND],
                                                          (passed,), "gather_second")
        srcs2, lands2, pieces2, idx2 = prepared["third"]
        groups2 = [list(idx2[k]) for k in _THIRD]
        prepared["third_started"] = (pieces2, groups2) + _comm_start(srcs2, lands2, pieces2, groups2,
                                                                     "gather_third_start", after=(tok_p,))
        return (prepared["third_started"][-1],)

    lands1, _ = _routed_gather(srcs1, lands1, pieces1, prepare_rest, start_rest, "gather_first")
    finish_second = prepared["finish_second"]
    pieces2, groups2, sems2, srcs2, lands2, tok2 = prepared["third_started"]
    wt = dict(zip(_FIRST, lands1))
    pending = {k: gi for gi, k in enumerate(_THIRD)}

    def get_w(name, after):
        if name in _SECOND and name not in wt:
            wt[name] = finish_second(_SECOND.index(name), after)
        if name in pending:
            gi = pending.pop(name)
            si = [pieces2[p][1] for p in groups2[gi]]
            sub = [(pieces2[p][0], row, 0) + pieces2[p][3:] for row, p in enumerate(groups2[gi])]
            wt[name] = _comm_wait([srcs2[s] for s in si], [lands2[gi]], sub, list(range(len(sub))), sems2[gi],
                                  after, "gather_wait_" + name)[0]
        return wt[name]

    sv = _forward(x[0], prepared["xb"], mem[0], loss_target[0], get_w, small, first_deps=(tok2,))

    sent = []

    def send(names, grads):
        srcs, shapes, pieces, idx = _scatter_plan(names, dict(zip(names, grads)), shard_shapes)
        lands = _place_own(srcs, shapes, pieces, "grads_own_%d" % len(sent))
        sems, srcs, lands, tok = _comm_start(srcs, lands, pieces, [list(range(len(pieces)))],
                                             "grads_start_%d" % len(sent))
        sent.append((names, srcs, lands, pieces, idx, sems[0]))
        return tok

    loss, grad_x, gs = _backward(sv, wt, small, send)

    ssrc = list(_pack_small_grads(gs, {k: w[k].shape for k in _SMALL_NAMES}, loss))
    sp = [("scatter", i, i, 0, 0, 0, a.shape) for i, a in enumerate(ssrc)]
    sshape = [jax.ShapeDtypeStruct((N_DEV,) + a.shape, F32) for a in ssrc]
    sl = _place_own(ssrc, sshape, sp, "small_own")
    ssem, ssrc, sl, _ = _comm_start(ssrc, sl, sp, [[0, 1]], "small_start")

    out_g, out_d, out_m, out_v = {}, {}, {}, {}
    after = (grad_x,)
    for n_sent, (names, srcs, lands, pieces, idx, sems) in enumerate(sent):
        lands = _comm_wait(srcs, lands, pieces, list(range(len(pieces))), sems, after, "grads_wait_%d" % n_sent)
        for k in names:
            axis = 1 if (k in _COL_FFN or k == "w_in") else 0
            if k in _COL_FFN:
                done = _adam_sharded([lands[i] for i in idx[k]], w[k][0].T, m[k][0].T, v[k][0].T, axis, "adam_" + k)
                res = [r.T for r in done]
            else:
                res = done = _adam_sharded([lands[i] for i in idx[k]], w[k][0], m[k][0], v[k][0], axis, "adam_" + k)
            out_g[k], out_d[k], out_m[k], out_v[k] = [r[None] for r in res]
        after = (done[3],)
    sl = _comm_wait(ssrc, sl, sp, [0, 1], ssem[0], after, "small_wait")
    small_out, loss_sum = _adam_small(sl[0], sl[1], w, m, v)
    for dst, res in zip((out_g, out_d, out_m, out_v), small_out):
        dst.update(res)
    loss_all = loss_sum[0, 0]
    return (loss_all, grad_x[None], *[out_g[k] for k in _WEIGHT_NAMES], *[out_d[k] for k in _WEIGHT_NAMES],
            *[out_m[k] for k in _WEIGHT_NAMES], *[out_v[k] for k in _WEIGHT_NAMES])
```

```python
import itertools

import jax
import jax.numpy as jnp
import numpy as np
from jax import lax
from jax.experimental import pallas as pl
from jax.experimental.pallas import tpu as pltpu

F32 = jnp.float32
BF16 = jnp.bfloat16

N_DEV = 8
ALPHA = 2.0 ** 0.25
LN_EPS = 1e-5
HG_HEADS = 4
HG_DIM = 128
SG_GROUPS = 4
SG_DIM = 128
SG_CHUNK = 128
X_HEADS = 4
HG_BLOCK = 16
HG_UNROLL = 16
ADAM_LR = 0.001
ADAM_B1 = 0.9
ADAM_B2 = 0.999
ADAM_EPS = 1e-08
ADAM_WD = 0.01
ADAM_STEP = 10
VMEM_LIMIT_V7X = 48 * 1024 * 1024
MXU_WIDTH_V7X = 256
LANES = 128
MESH_ID = pl.DeviceIdType.MESH
ANY = pl.BlockSpec(memory_space=pl.ANY)
HBM = pl.BlockSpec(memory_space=pltpu.HBM)
SEM = pl.BlockSpec(memory_space=pltpu.SEMAPHORE)
DATAFLOW = pltpu.SideEffectType.DATAFLOW_SIDE_EFFECTING


def _params(n_axes):
    return pltpu.CompilerParams(dimension_semantics=("arbitrary",) * n_axes, vmem_limit_bytes=VMEM_LIMIT_V7X)


def _dot(a, b):
    return jnp.dot(a, b, preferred_element_type=F32)


def _dot_nt(a, b):
    return lax.dot_general(a, b, (((1,), (1,)), ((), ())), preferred_element_type=F32)


def _dot_tn(a, b):
    return lax.dot_general(a, b, (((0,), (0,)), ((), ())), preferred_element_type=F32)


def _sigmoid(x):
    return 1.0 / (1.0 + jnp.exp(-x))


def _silu_and_grad(a):
    sig = _sigmoid(a)
    return a * sig, sig * (1.0 + a * (1.0 - sig))


_GELU_C = 0.7978845608028654


def _gelu_and_grad(x):
    inner = _GELU_C * (x + 0.044715 * x * x * x)
    t = jnp.tanh(inner)
    val = 0.5 * x * (1.0 + t)
    grad = 0.5 * (1.0 + t) + 0.5 * x * (1.0 - t * t) * _GELU_C * (1.0 + 3.0 * 0.044715 * x * x)
    return val, grad


def _ln_fwd(y, g, b):
    mu = jnp.mean(y, axis=-1, keepdims=True)
    yc = y - mu
    var = jnp.mean(yc * yc, axis=-1, keepdims=True)
    rstd = lax.rsqrt(var + LN_EPS)
    xhat = yc * rstd
    return xhat * g + b, xhat, rstd


def _ln_bwd(dh, xhat, rstd, g):
    dxh = dh * g
    m1 = jnp.mean(dxh, axis=-1, keepdims=True)
    m2 = jnp.mean(dxh * xhat, axis=-1, keepdims=True)
    dy = rstd * (dxh - m1 - xhat * m2)
    dg = jnp.sum(dh * xhat, axis=0, keepdims=True)
    db = jnp.sum(dh, axis=0, keepdims=True)
    return dy, dg, db


def _mask_dot(mask, x):
    hi = x.astype(BF16)
    lo = (x - hi.astype(F32)).astype(BF16)
    n = mask.shape[0]
    parts = [_dot(mask, hi[r:r + n, :]) + _dot(mask, lo[r:r + n, :]) for r in range(0, x.shape[0], n)]
    return parts[0] if len(parts) == 1 else jnp.concatenate(parts, axis=0)


def _block_masks(n):
    r = np.arange(n)[:, None]
    c = np.arange(n)[None, :]
    same = (r // HG_BLOCK) == (c // HG_BLOCK)
    return jnp.asarray(np.stack([same & (c <= r), same & (c >= r), same]), BF16)


def _row_tile(t):
    return min(t, 512)


def _col_tile(n):
    for cand in (512, 256, 128):
        if n % cand == 0:
            return cand
    return n


def _resident(w):
    return pl.BlockSpec(w.shape, lambda *_: (0, 0), pipeline_mode=pl.Buffered(1))


def _drop_deps(body, n_in, n_deps):
    if n_deps == 0:
        return body
    return lambda *refs: body(*refs[:n_in], *refs[n_in + n_deps:])


def _to_bf16(x, name, deps=()):
    t, d = x.shape
    tm = _row_tile(t)

    def body(x_ref, o_ref):
        o_ref[...] = x_ref[...].astype(BF16)

    row = pl.BlockSpec((tm, d), lambda i: (i, 0))
    return pl.pallas_call(
        _drop_deps(body, 1, len(deps)),
        grid=(t // tm,),
        in_specs=[row] + [ANY] * len(deps),
        out_specs=row,
        out_shape=jax.ShapeDtypeStruct((t, d), BF16),
        compiler_params=_params(1),
        name=name,
    )(x, *deps)


def _ffn_up(hb, wg, wu, name, deps=()):
    t, d = hb.shape
    f = wg.shape[1]
    tm = _row_tile(t)
    tn = _col_tile(f)

    def body(h_ref, wg_ref, wu_ref, a_ref, b_ref, s_ref):
        h = h_ref[...]
        for c in range(f // tn):
            cols = slice(c * tn, (c + 1) * tn)
            a = _dot(h, wg_ref[:, cols])
            b = _dot(h, wu_ref[:, cols])
            a_ref[:, cols] = a.astype(BF16)
            b_ref[:, cols] = b.astype(BF16)
            s_ref[:, cols] = (a * _sigmoid(a) * b).astype(BF16)

    act = pl.BlockSpec((tm, f), lambda i: (i, 0))
    return pl.pallas_call(
        _drop_deps(body, 3, len(deps)),
        grid=(t // tm,),
        in_specs=[pl.BlockSpec((tm, d), lambda i: (i, 0)), _resident(wg), _resident(wu)] + [ANY] * len(deps),
        out_specs=[act, act, act],
        out_shape=[jax.ShapeDtypeStruct((t, f), BF16)] * 3,
        compiler_params=_params(1),
        name=name,
    )(hb, wg, wu, *deps)


def _mm_res_ln(lhs, w, res, g, b, coef, name, target=None):
    t, kd = lhs.shape
    d = w.shape[1]
    tm = _row_tile(t)
    nt = t // tm
    from_norm = isinstance(res, tuple)
    n_res = 3 if from_norm else 1

    def body(*refs):
        l_ref, w_ref = refs[:2]
        r_refs = refs[2:2 + n_res]
        g_ref, b_ref = refs[2 + n_res:4 + n_res]
        rest = refs[4 + n_res:]
        prev = r_refs[0][...] * r_refs[1][...] + r_refs[2][...] if from_norm else r_refs[0][...]
        y = ALPHA * prev + coef * _dot(l_ref[...], w_ref[...])
        h, xhat, rstd = _ln_fwd(y, g_ref[...], b_ref[...])
        if target is None:
            hb_ref, xh_ref, rs_ref = rest
            hb_ref[...] = h.astype(BF16)
            xh_ref[...] = xhat
            rs_ref[...] = rstd
            return
        t_ref, loss_ref, dy_ref, dyb_ref, dg_ref, db_ref, lacc = rest
        i = pl.program_id(0)

        @pl.when(i == 0)
        def _():
            lacc[...] = jnp.zeros_like(lacc)
            dg_ref[...] = jnp.zeros_like(dg_ref)
            db_ref[...] = jnp.zeros_like(db_ref)

        err = h - t_ref[...]
        lacc[...] += jnp.sum(err * err, axis=0, keepdims=True)
        dy, dg, db = _ln_bwd(err * (1.0 / d), xhat, rstd, g_ref[...])
        dy_ref[...] = dy
        dyb_ref[...] = dy.astype(BF16)
        dg_ref[...] += dg
        db_ref[...] += db

        @pl.when(i == nt - 1)
        def _():
            loss_ref[...] = jnp.zeros_like(loss_ref) + jnp.sum(lacc[...], axis=1, keepdims=True) * (0.5 / d)

    row = pl.BlockSpec((tm, d), lambda i: (i, 0))
    vec = pl.BlockSpec((1, d), lambda i: (0, 0))
    res_specs = [row, vec, vec] if from_norm else [row]
    res_args = list(res) if from_norm else [res]
    in_specs = [pl.BlockSpec((tm, kd), lambda i: (i, 0)), _resident(w)] + res_specs + [vec, vec]
    args = [lhs, w] + res_args + [g, b]
    if target is None:
        out_specs = [row, row, pl.BlockSpec((tm, 1), lambda i: (i, 0))]
        out_shape = [jax.ShapeDtypeStruct((t, d), BF16), jax.ShapeDtypeStruct((t, d), F32), pltpu.HBM((t, 1), F32)]
        scratch = []
    else:
        in_specs.append(row)
        args.append(target)
        out_specs = [pl.BlockSpec((1, LANES), lambda i: (0, 0)), row, row, vec, vec]
        out_shape = [jax.ShapeDtypeStruct((1, LANES), F32), jax.ShapeDtypeStruct((t, d), F32),
                     jax.ShapeDtypeStruct((t, d), BF16), jax.ShapeDtypeStruct((1, d), F32),
                     jax.ShapeDtypeStruct((1, d), F32)]
        scratch = [pltpu.VMEM((1, d), F32)]
    return pl.pallas_call(
        body,
        grid=(nt,),
        in_specs=in_specs,
        out_specs=out_specs,
        out_shape=out_shape,
        scratch_shapes=scratch,
        compiler_params=_params(1),
        name=name,
    )(*args)


def _store_slabs(o_ref, first, tile):
    for s in range(tile.shape[1] // LANES):
        o_ref[first + s] = tile[:, s * LANES:(s + 1) * LANES]


def _mm_nn(lhs, w, name):
    t, kd = lhs.shape
    n = w.shape[1]
    tm = _row_tile(t)
    tn = _col_tile(n)

    def body(l_ref, w_ref, o_ref):
        lhs_v = l_ref[...]
        for c in range(n // tn):
            _store_slabs(o_ref, c * (tn // LANES), _dot(lhs_v, w_ref[:, c * tn:(c + 1) * tn]))

    return pl.pallas_call(
        body,
        grid=(t // tm,),
        in_specs=[pl.BlockSpec((tm, kd), lambda i: (i, 0)), _resident(w)],
        out_specs=pl.BlockSpec((n // LANES, tm, LANES), lambda i: (0, i, 0)),
        out_shape=jax.ShapeDtypeStruct((n // LANES, t, LANES), F32),
        compiler_params=_params(1),
        name=name,
    )(lhs, w)


def _lower_bound(lg):
    m = jnp.max(lg, axis=0, keepdims=True)
    e = jnp.exp(lg - m)
    return e[0:1, :] / jnp.sum(e, axis=0, keepdims=True)


def _forget_terms(fz, lb):
    e = jnp.exp(-jnp.abs(fz))
    r = 1.0 / (1.0 + e)
    pos = fz >= 0.0
    sig = jnp.where(pos, r, e * r)
    nsig = jnp.where(pos, e * r, r)
    f = lb + (1.0 - lb) * sig
    k = (1.0 - lb) * nsig
    return sig, nsig, f, k


def _hg_tile(t):
    return min(t, 1024)


HG_HALF = HG_BLOCK // 2
NEG_BIG = -1e30


def _halves(a):
    return a[:HG_HALF, :], a[HG_HALF:, :]


def _causal_halves(s):
    return (0, 1) if s < HG_HALF else (1,)


def _decay_from(b_half, b_s, s, h, tidx):
    first = s - h * HG_HALF
    diff = b_half - b_s
    if first > 0:
        diff = jnp.where(tidx >= first, diff, NEG_BIG)
    return jnp.exp(diff)


def _hgrn_fwd(proj, logits, gn):
    t = proj.shape[1]
    ct = _hg_tile(t)
    nct = t // ct
    nblk = ct // HG_BLOCK
    nh = HG_HEADS
    mrows = min(ct, 256)

    def body(q_ref, fz_ref, iv_ref, gg_ref, lg_ref, gn_ref, mask_ref, oraw_ref, oa_ref, st_ref,
             state, qt_s, kt_s, k_s, b_s, dec_s):
        c = pl.program_id(1)

        @pl.when(c == 0)
        def _():
            state[...] = jnp.zeros_like(state)

        lb = _lower_bound(lg_ref[...])
        q = q_ref[...]
        _, _, f, k = _forget_terms(fz_ref[...], lb)
        logf = jnp.log(f)
        b = _mask_dot(mask_ref[0], logf)
        bend = _mask_dot(mask_ref[2], logf)
        qt_s[...] = (q * jnp.exp(b)).astype(BF16)
        kt_s[...] = (k * jnp.exp(bend - b)).astype(BF16)
        k_s[...] = k
        b_s[...] = b
        dec_s[...] = jnp.exp(bend)
        tidx = lax.broadcasted_iota(jnp.int32, (HG_HALF, HG_DIM), 0)

        def blk(i, carry):
            r0 = pl.multiple_of(i * HG_BLOCK, HG_BLOCK)
            rows = pl.ds(r0, HG_BLOCK)
            st = state[...]
            stb = st.astype(BF16)
            st_ref[i] = stb
            v = iv_ref[rows, :]
            qq = q_ref[rows, :]
            kk = k_s[rows, :]
            bb = b_s[rows, :]
            o = list(_halves(_dot_nt(qt_s[rows, :], stb)))
            qh, bh = _halves(qq), _halves(bb)
            for s in range(HG_BLOCK):
                ks, vs = kk[s:s + 1, :], v[s:s + 1, :]
                for h in _causal_halves(s):
                    e = _decay_from(bh[h], bb[s:s + 1, :], s, h, tidx)
                    acol = jnp.sum(qh[h] * (ks * e), axis=1, keepdims=True)
                    o[h] = o[h] + acol * vs
            oraw_ref[rows, :] = jnp.concatenate(o, axis=0)
            state[...] = st * dec_s[pl.ds(r0, 1), :] + _dot_tn(v.astype(BF16), kt_s[rows, :])
            return carry

        lax.fori_loop(0, nblk, blk, 0, unroll=HG_UNROLL)
        oraw = oraw_ref[...]
        r = lax.rsqrt(jnp.mean(oraw * oraw, axis=-1, keepdims=True) + LN_EPS)
        gg = gg_ref[...]
        oa_ref[...] = (oraw * r * gn_ref[...] * gg * _sigmoid(gg)).astype(BF16)

    def slab(off):
        return pl.BlockSpec((None, ct, HG_DIM), lambda h, c: (off + h, c, 0))

    return pl.pallas_call(
        body,
        grid=(nh, nct),
        in_specs=[slab(0), slab(nh), slab(2 * nh), slab(3 * nh),
                  pl.BlockSpec((None, 2, HG_DIM), lambda h, c: (h, 0, 0)),
                  pl.BlockSpec((1, HG_DIM), lambda h, c: (0, 0)),
                  pl.BlockSpec((3, mrows, mrows), lambda h, c: (0, 0, 0))],
        out_specs=[slab(0), pl.BlockSpec((ct, HG_DIM), lambda h, c: (c, h)),
                   pl.BlockSpec((None, nblk, HG_DIM, HG_DIM), lambda h, c: (h, c, 0, 0))],
        out_shape=[jax.ShapeDtypeStruct((nh, t, HG_DIM), F32),
                   jax.ShapeDtypeStruct((t, (nh + SG_GROUPS) * HG_DIM), BF16),
                   jax.ShapeDtypeStruct((nh, t // HG_BLOCK, HG_DIM, HG_DIM), BF16)],
        scratch_shapes=[pltpu.VMEM((HG_DIM, HG_DIM), F32), pltpu.VMEM((ct, HG_DIM), BF16),
                        pltpu.VMEM((ct, HG_DIM), BF16), pltpu.VMEM((ct, HG_DIM), F32),
                        pltpu.VMEM((ct, HG_DIM), F32), pltpu.VMEM((ct, HG_DIM), F32)],
        compiler_params=_params(2),
        name="hgrn_fwd",
    )(proj, proj, proj, proj, logits, gn, _block_masks(mrows))


def _sg_tile(t):
    return min(t, 512)


def _sgu_chunk_fwd(u, v, ln_g, ln_b, wm, bs):
    ua, dua = _gelu_and_grad(u)
    va, dva = _gelu_and_grad(v)
    vn, xhat, rstd = _ln_fwd(va, ln_g, ln_b)
    s = _dot(wm, vn.astype(BF16)) + bs
    return ua, dua, dva, vn, xhat, rstd, s


def _tril_weight(w):
    n = SG_CHUNK
    r = lax.broadcasted_iota(jnp.int32, (n, n), 0)
    c = lax.broadcasted_iota(jnp.int32, (n, n), 1)
    return jnp.where(c <= r, w, 0.0)


def _bias_by_position(b_row):
    return jnp.broadcast_to(b_row, (SG_CHUNK, SG_CHUNK)).T


def _sgu_fwd(proj, mix, ln_g, ln_b, w_s, b_row):
    t = proj.shape[1]
    ct = _sg_tile(t)
    ng = SG_GROUPS
    wide = ng * SG_DIM
    blk_u = 4 * HG_HEADS // ng

    def body(u_ref, v_ref, g_ref, b_ref, w_ref, bs_ref, mix_ref, o_ref):
        del mix_ref
        for g in range(ng):
            lanes = slice(g * SG_DIM, (g + 1) * SG_DIM)
            wm = _tril_weight(w_ref[g]).astype(BF16)
            bs = _bias_by_position(bs_ref[g])
            for n in range(ct // SG_CHUNK):
                rows = slice(n * SG_CHUNK, (n + 1) * SG_CHUNK)
                ua, _, _, _, _, _, s = _sgu_chunk_fwd(u_ref[g, rows, :], v_ref[g, rows, :], g_ref[g], b_ref[g], wm, bs)
                o_ref[rows, lanes] = (ua * s).astype(BF16)

    full = lambda a: pl.BlockSpec(a.shape, lambda c: (0,) * a.ndim)
    return pl.pallas_call(
        body,
        grid=(t // ct,),
        in_specs=[pl.BlockSpec((ng, ct, SG_DIM), lambda c: (blk_u, c, 0)),
                  pl.BlockSpec((ng, ct, SG_DIM), lambda c: (blk_u + 1, c, 0)),
                  full(ln_g), full(ln_b), full(w_s), full(b_row), ANY],
        out_specs=pl.BlockSpec((ct, wide), lambda c: (c, 1)),
        out_shape=jax.ShapeDtypeStruct(mix.shape, mix.dtype),
        input_output_aliases={6: 0},
        compiler_params=_params(1),
        name="sgu_fwd",
    )(proj, proj, ln_g, ln_b, w_s, b_row, mix)


def _mem_kv(mem, g, b, wk, wv):
    m_len, d = mem.shape

    def body(m_ref, g_ref, b_ref, wk_ref, wv_ref, mb_ref, xh_ref, rs_ref, k_ref, v_ref):
        m, xhat, rstd = _ln_fwd(m_ref[...], g_ref[...], b_ref[...])
        mb = m.astype(BF16)
        mb_ref[...] = mb
        xh_ref[...] = xhat
        rs_ref[...] = rstd
        k_ref[...] = _dot(mb, wk_ref[...]).astype(BF16)
        v_ref[...] = _dot(mb, wv_ref[...]).astype(BF16)

    return pl.pallas_call(
        body,
        out_shape=[jax.ShapeDtypeStruct((m_len, d), BF16), jax.ShapeDtypeStruct((m_len, d), F32),
                   jax.ShapeDtypeStruct((m_len, 1), F32), jax.ShapeDtypeStruct((m_len, d), BF16),
                   jax.ShapeDtypeStruct((m_len, d), BF16)],
        compiler_params=pltpu.CompilerParams(vmem_limit_bytes=VMEM_LIMIT_V7X),
        name="mem_kv",
    )(mem, g, b, wk, wv)


def _softmax_rows(s):
    m = jnp.max(s, axis=-1, keepdims=True)
    p = jnp.exp(s - m)
    return p / jnp.sum(p, axis=-1, keepdims=True)


def _attn_fwd(hb, wq, kb, vb):
    t, d = hb.shape
    tm = _row_tile(t)
    dh = d // X_HEADS
    scale = dh ** -0.5

    def body(h_ref, wq_ref, k_ref, v_ref, q_ref, o_ref):
        q = _dot(h_ref[...], wq_ref[...]).astype(BF16)
        q_ref[...] = q
        for hd in range(X_HEADS):
            sl = slice(hd * dh, (hd + 1) * dh)
            p = _softmax_rows(_dot_nt(q[:, sl], k_ref[:, sl]) * scale)
            o_ref[:, sl] = _dot(p.astype(BF16), v_ref[:, sl]).astype(BF16)

    row = pl.BlockSpec((tm, d), lambda i: (i, 0))
    full = lambda a: pl.BlockSpec(a.shape, lambda i: (0, 0))
    return pl.pallas_call(
        body,
        grid=(t // tm,),
        in_specs=[row, full(wq), full(kb), full(vb)],
        out_specs=[row, row],
        out_shape=[jax.ShapeDtypeStruct((t, d), BF16), jax.ShapeDtypeStruct((t, d), BF16)],
        compiler_params=_params(1),
        name="attn_fwd",
    )(hb, wq, kb, vb)


def _ffn_bwd_act(dyb, wd, a, b, coef, name, deps=()):
    t, d = dyb.shape
    f = wd.shape[0]
    tm = _row_tile(t)
    tn = _col_tile(f)

    def body(dy_ref, wd_ref, a_ref, b_ref, da_ref, db_ref):
        dy = dy_ref[...]
        for c in range(f // tn):
            cols = slice(c * tn, (c + 1) * tn)
            ds = _dot_nt(dy, wd_ref[cols, :]) * coef
            silu, dsilu = _silu_and_grad(a_ref[:, cols].astype(F32))
            da_ref[:, cols] = (ds * b_ref[:, cols].astype(F32) * dsilu).astype(BF16)
            db_ref[:, cols] = (ds * silu).astype(BF16)

    act = pl.BlockSpec((tm, f), lambda i: (i, 0))
    return pl.pallas_call(
        _drop_deps(body, 4, len(deps)),
        grid=(t // tm,),
        in_specs=[pl.BlockSpec((tm, d), lambda i: (i, 0)), _resident(wd), act, act] + [ANY] * len(deps),
        out_specs=[act, act],
        out_shape=[jax.ShapeDtypeStruct((t, f), BF16), jax.ShapeDtypeStruct((t, f), BF16)],
        compiler_params=_params(1),
        name=name,
    )(dyb, wd, a, b, *deps)


def _ffn_bwd_fused(dy, dyb, wd, wg, wu, a, b, coef, ln, name):
    t, d = dy.shape
    f = wd.shape[0]
    tm = min(t, 256)
    tn = _col_tile(f)

    def body(dy_ref, dyb_ref, wd_ref, wg_ref, wu_ref, a_ref, b_ref, xh_ref, rs_ref, g_ref,
             da_ref, db_ref, dyo_ref, dyob_ref, dg_ref, dbl_ref):
        dyb_v = dyb_ref[...]
        dh = ALPHA * dy_ref[...]
        for c in range(f // tn):
            cols = slice(c * tn, (c + 1) * tn)
            ds = _dot_nt(dyb_v, wd_ref[cols, :]) * coef
            silu, dsilu = _silu_and_grad(a_ref[:, cols].astype(F32))
            da = (ds * b_ref[:, cols].astype(F32) * dsilu).astype(BF16)
            db = (ds * silu).astype(BF16)
            da_ref[:, cols] = da
            db_ref[:, cols] = db
            dh = dh + _dot_nt(da, wg_ref[:, cols]) + _dot_nt(db, wu_ref[:, cols])

        @pl.when(pl.program_id(0) == 0)
        def _():
            dg_ref[...] = jnp.zeros_like(dg_ref)
            dbl_ref[...] = jnp.zeros_like(dbl_ref)

        dyp, dg, dbl = _ln_bwd(dh, xh_ref[...], rs_ref[...], g_ref[...])
        dyo_ref[...] = dyp
        dyob_ref[...] = dyp.astype(BF16)
        dg_ref[...] += dg
        dbl_ref[...] += dbl

    row = pl.BlockSpec((tm, d), lambda i: (i, 0))
    act = pl.BlockSpec((tm, f), lambda i: (i, 0))
    vec = pl.BlockSpec((1, d), lambda i: (0, 0))
    return pl.pallas_call(
        body,
        grid=(t // tm,),
        in_specs=[row, row, _resident(wd), _resident(wg), _resident(wu), act, act, row,
                  pl.BlockSpec((tm, 1), lambda i: (i, 0)), vec],
        out_specs=[act, act, row, row, vec, vec],
        out_shape=[jax.ShapeDtypeStruct((t, f), BF16), jax.ShapeDtypeStruct((t, f), BF16),
                   jax.ShapeDtypeStruct((t, d), F32), jax.ShapeDtypeStruct((t, d), BF16),
                   jax.ShapeDtypeStruct((1, d), F32), jax.ShapeDtypeStruct((1, d), F32)],
        compiler_params=_params(1),
        name=name,
    )(dy, dyb, wd, wg, wu, a, b, *ln)


def _mm_tn(a, b, name, scale=1.0, deps=()):
    t, m = a.shape
    bs = list(b) if isinstance(b, (list, tuple)) else [b]
    n = sum(piece.shape[1] for piece in bs)
    tt = _row_tile(t)
    nt = t // tt
    tm_o, tn_o = m, n

    def body(a_ref, *refs):
        b_refs, (o_ref, acc) = refs[:len(bs)], refs[len(bs):]
        k = pl.program_id(2)

        @pl.when(k == 0)
        def _():
            acc[...] = jnp.zeros_like(acc)

        first = 0
        for b_ref in b_refs:
            cols = slice(first, first + b_ref.shape[1])
            acc[:, cols] += _dot_tn(a_ref[...], b_ref[...])
            first = cols.stop

        @pl.when(k == nt - 1)
        def _():
            o_ref[...] = (acc[...] * scale).astype(BF16)

    return pl.pallas_call(
        _drop_deps(body, 1 + len(bs), len(deps)),
        grid=(m // tm_o, n // tn_o, nt),
        in_specs=[pl.BlockSpec((tt, tm_o), lambda i, j, k: (k, i))]
        + [pl.BlockSpec((tt, piece.shape[1]), lambda i, j, k: (k, j)) for piece in bs] + [ANY] * len(deps),
        out_specs=pl.BlockSpec((tm_o, tn_o), lambda i, j, k: (i, j)),
        out_shape=pltpu.HBM((m, n), BF16),
        scratch_shapes=[pltpu.VMEM((tm_o, tn_o), F32)],
        compiler_params=_params(3),
        name=name,
    )(a, *bs, *deps)


def _mm_nt(lhs, w, name):
    t, d = lhs.shape
    kd = w.shape[0]
    tm = _row_tile(t)

    def body(l_ref, w_ref, o_ref):
        _store_slabs(o_ref, 0, _dot_nt(l_ref[...], w_ref[...]))

    return pl.pallas_call(
        body,
        grid=(t // tm,),
        in_specs=[pl.BlockSpec((tm, d), lambda i: (i, 0)), _resident(w)],
        out_specs=pl.BlockSpec((kd // LANES, tm, LANES), lambda i: (0, i, 0)),
        out_shape=jax.ShapeDtypeStruct((kd // LANES, t, LANES), F32),
        compiler_params=_params(1),
        name=name,
    )(lhs, w)


def _dx_ln(dy, pairs, ln, name, deps=()):
    t, d = dy.shape
    npair = len(pairs)
    pairs = [(list(lhs) if isinstance(lhs, (list, tuple)) else [lhs], w) for lhs, w in pairs]
    tm = min(t, 512 // npair)
    nt = t // tm
    n_in = 1 + sum(len(pieces) + 1 for pieces, _ in pairs) + (3 if ln is not None else 0)

    def body(*refs):
        dy_ref = refs[0]
        pos = 1
        dh = ALPHA * dy_ref[...]
        for pieces, _ in pairs:
            w_ref = refs[pos + len(pieces)]
            first = 0
            for l_ref in refs[pos:pos + len(pieces)]:
                cols = slice(first, first + l_ref.shape[1])
                dh = dh + _dot_nt(l_ref[...], w_ref[:, cols])
                first = cols.stop
            pos += len(pieces) + 1
        if ln is not None:
            xh_ref, rs_ref, g_ref = refs[pos:pos + 3]
            dyo_ref, dyb_ref, dg_ref, db_ref = refs[pos + 3:pos + 7]

            @pl.when(pl.program_id(0) == 0)
            def _():
                dg_ref[...] = jnp.zeros_like(dg_ref)
                db_ref[...] = jnp.zeros_like(db_ref)

            dyp, dg, db = _ln_bwd(dh, xh_ref[...], rs_ref[...], g_ref[...])
            dyo_ref[...] = dyp
            dyb_ref[...] = dyp.astype(BF16)
            dg_ref[...] += dg
            db_ref[...] += db
        else:
            refs[pos][...] = dh

    row = pl.BlockSpec((tm, d), lambda i: (i, 0))
    vec = pl.BlockSpec((1, d), lambda i: (0, 0))
    in_specs = [row]
    args = [dy]
    for pieces, w in pairs:
        in_specs += [pl.BlockSpec((tm, piece.shape[1]), lambda i: (i, 0)) for piece in pieces] + [_resident(w)]
        args += pieces + [w]
    if ln is not None:
        in_specs += [row, pl.BlockSpec((tm, 1), lambda i: (i, 0)), vec]
        args += list(ln)
        out_specs = [row, row, vec, vec]
        out_shape = [jax.ShapeDtypeStruct((t, d), F32), jax.ShapeDtypeStruct((t, d), BF16),
                     jax.ShapeDtypeStruct((1, d), F32), jax.ShapeDtypeStruct((1, d), F32)]
    else:
        out_specs = row
        out_shape = jax.ShapeDtypeStruct((t, d), F32)
    return pl.pallas_call(
        _drop_deps(body, n_in, len(deps)),
        grid=(nt,),
        in_specs=in_specs + [ANY] * len(deps),
        out_specs=out_specs,
        out_shape=out_shape,
        compiler_params=_params(1),
        name=name,
    )(*args, *deps)


def _hgrn_bwd(proj, oraw, dmix, states, logits, gn):
    t = proj.shape[1]
    ct = _hg_tile(t)
    nct = t // ct
    nblk = ct // HG_BLOCK
    nh = HG_HEADS
    mrows = min(ct, 256)

    def body(q_ref, fz_ref, iv_ref, gg_ref, or_ref, do_ref, st_ref, lg_ref, gn_ref, mask_ref,
             dq_ref, dfz_ref, div_ref, dgg_ref, dlg_ref, dgn_ref,
             dstate, qt_s, kt_s, k_s, b_s, eb_s, ekb_s, dec_s, dor_s, dbl_s, gr_s, dk_s, dlb_acc):
        c = pl.program_id(1)

        @pl.when(c == 0)
        def _():
            dstate[...] = jnp.zeros_like(dstate)
            dlb_acc[...] = jnp.zeros_like(dlb_acc)
            dgn_ref[...] = jnp.zeros_like(dgn_ref)

        lb = _lower_bound(lg_ref[...])
        q = q_ref[...]
        sig, nsig, f, k = _forget_terms(fz_ref[...], lb)
        logf = jnp.log(f)
        b = _mask_dot(mask_ref[0], logf)
        bend = _mask_dot(mask_ref[2], logf)
        eb = jnp.exp(b)
        ekb = jnp.exp(bend - b)
        qt_s[...] = (q * eb).astype(BF16)
        kt_s[...] = (k * ekb).astype(BF16)
        k_s[...] = k
        b_s[...] = b
        eb_s[...] = eb
        ekb_s[...] = ekb
        dec_s[...] = jnp.exp(bend)
        oraw = or_ref[...]
        r = lax.rsqrt(jnp.mean(oraw * oraw, axis=-1, keepdims=True) + LN_EPS)
        on = oraw * r
        gg = gg_ref[...]
        silu, dsilu = _silu_and_grad(gg)
        doa = do_ref[...]
        gnv = gn_ref[...]
        dgg_ref[...] = (doa * on * gnv * dsilu).astype(BF16)
        dyn = doa * silu
        dgn_ref[...] += jnp.sum(dyn * on, axis=0, keepdims=True)
        don = dyn * gnv
        dor_s[...] = r * (don - on * jnp.mean(don * on, axis=-1, keepdims=True))
        tidx = lax.broadcasted_iota(jnp.int32, (HG_HALF, HG_DIM), 0)

        def blk(ii, carry):
            i = nblk - 1 - ii
            r0 = pl.multiple_of(i * HG_BLOCK, HG_BLOCK)
            rows = pl.ds(r0, HG_BLOCK)
            st = st_ref[i]
            dst = dstate[...]
            dstb = dst.astype(BF16)
            do = dor_s[rows, :]
            dob = do.astype(BF16)
            v = iv_ref[rows, :]
            vb = v.astype(BF16)
            qq = q_ref[rows, :]
            kk = k_s[rows, :]
            bb = b_s[rows, :]
            qt = qt_s[rows, :]
            kt = kt_s[rows, :]
            dec = dec_s[pl.ds(r0, 1), :]
            dkt = _dot(vb, dstb)
            dq = _dot(dob, st) * eb_s[rows, :]
            dk = dkt * ekb_s[rows, :]
            dv = _dot_nt(kt, dstb)
            gend = (jnp.sum(kk * dk, axis=0, keepdims=True)
                    + dec * jnp.sum(dst * st.astype(F32), axis=0, keepdims=True))
            qh, bh, doh = _halves(qq), _halves(bb), _halves(do)
            dqh, dkh, dvh = list(_halves(dq)), list(_halves(dk)), list(_halves(dv))
            for s in range(HG_BLOCK):
                ks, vs = kk[s:s + 1, :], v[s:s + 1, :]
                dk_part = dv_part = None
                for h in _causal_halves(s):
                    e = _decay_from(bh[h], bb[s:s + 1, :], s, h, tidx)
                    ke = ks * e
                    acol = jnp.sum(qh[h] * ke, axis=1, keepdims=True)
                    dacol = jnp.sum(doh[h] * vs, axis=1, keepdims=True)
                    dqh[h] = dqh[h] + dacol * ke
                    pk = dacol * (qh[h] * e)
                    pv = acol * doh[h]
                    dk_part = pk if dk_part is None else dk_part + pk
                    dv_part = pv if dv_part is None else dv_part + pv
                hs, row = divmod(s, HG_HALF)
                dkh[hs] = dkh[hs] + jnp.where(tidx == row, jnp.sum(dk_part, axis=0, keepdims=True), 0.0)
                dvh[hs] = dvh[hs] + jnp.where(tidx == row, jnp.sum(dv_part, axis=0, keepdims=True), 0.0)
            dq = jnp.concatenate(dqh, axis=0)
            dk = jnp.concatenate(dkh, axis=0)
            dv = jnp.concatenate(dvh, axis=0)
            dq_ref[rows, :] = dq.astype(BF16)
            div_ref[rows, :] = dv.astype(BF16)
            dk_s[rows, :] = dk
            dbl_s[rows, :] = qq * dq - kk * dk
            gr_s[rows, :] = jnp.zeros((HG_BLOCK, HG_DIM), F32) + gend
            dstate[...] = dst * dec + _dot_tn(dob, qt)
            return carry

        lax.fori_loop(0, nblk, blk, 0, unroll=HG_UNROLL)
        dlogf = _mask_dot(mask_ref[1], dbl_s[...]) + gr_s[...]
        dk = dk_s[...]
        dfz_ref[...] = ((dlogf / f - dk) * ((1.0 - lb) * sig * nsig)).astype(BF16)
        dlb_acc[...] += jnp.sum((dlogf / f - dk) * nsig, axis=0, keepdims=True)

        @pl.when(c == nct - 1)
        def _():
            dl0 = dlb_acc[...] * lb * (1.0 - lb)
            layer = lax.broadcasted_iota(jnp.int32, (2, HG_DIM), 0)
            dlg_ref[...] = jnp.where(layer == 0, dl0, -dl0)

    def slab(off):
        return pl.BlockSpec((None, ct, HG_DIM), lambda h, c: (off + h, nct - 1 - c, 0))

    out_slab = pl.BlockSpec((ct, HG_DIM), lambda h, c: (nct - 1 - c, h))
    tile_f32 = pltpu.VMEM((ct, HG_DIM), F32)
    tile_b16 = pltpu.VMEM((ct, HG_DIM), BF16)
    slab_shape = pltpu.HBM((t, nh * HG_DIM), BF16)
    return pl.pallas_call(
        body,
        grid=(nh, nct),
        in_specs=[slab(0), slab(nh), slab(2 * nh), slab(3 * nh), slab(0), slab(0),
                  pl.BlockSpec((None, nblk, HG_DIM, HG_DIM), lambda h, c: (h, nct - 1 - c, 0, 0)),
                  pl.BlockSpec((None, 2, HG_DIM), lambda h, c: (h, 0, 0)),
                  pl.BlockSpec((1, HG_DIM), lambda h, c: (0, 0)),
                  pl.BlockSpec((3, mrows, mrows), lambda h, c: (0, 0, 0))],
        out_specs=[out_slab, out_slab, out_slab, out_slab,
                   pl.BlockSpec((None, 2, HG_DIM), lambda h, c: (h, 0, 0)),
                   pl.BlockSpec((None, 1, HG_DIM), lambda h, c: (h, 0, 0))],
        out_shape=[slab_shape, slab_shape, slab_shape, slab_shape,
                   jax.ShapeDtypeStruct((nh, 2, HG_DIM), F32), jax.ShapeDtypeStruct((nh, 1, HG_DIM), F32)],
        scratch_shapes=[pltpu.VMEM((HG_DIM, HG_DIM), F32), tile_b16, tile_b16, tile_f32, tile_f32, tile_f32, tile_f32,
                        tile_f32, tile_f32, tile_f32, tile_f32, tile_f32, pltpu.VMEM((1, HG_DIM), F32)],
        compiler_params=_params(2),
        name="hgrn_bwd",
    )(proj, proj, proj, proj, oraw, dmix, states, logits, gn, _block_masks(mrows))


def _sgu_bwd(proj, dmix, ln_g, ln_b, w_s, b_row):
    t = proj.shape[1]
    ct = _sg_tile(t)
    nct = t // ct
    ng = SG_GROUPS
    off_u = 4 * HG_HEADS
    off_v = off_u + ng
    n = SG_CHUNK

    def body(u_ref, v_ref, do_ref, g_ref, b_ref, w_ref, bs_ref, du_ref, dv_ref, dg_ref, db_ref, dw_ref, dbs_ref):
        c = pl.program_id(1)

        @pl.when(c == 0)
        def _():
            dg_ref[...] = jnp.zeros_like(dg_ref)
            db_ref[...] = jnp.zeros_like(db_ref)
            dw_ref[...] = jnp.zeros_like(dw_ref)
            dbs_ref[...] = jnp.zeros_like(dbs_ref)

        r = lax.broadcasted_iota(jnp.int32, (n, n), 0)
        cc = lax.broadcasted_iota(jnp.int32, (n, n), 1)
        wm = jnp.where(cc <= r, w_ref[...], 0.0).astype(BF16)
        wmt = jnp.where(r <= cc, w_ref[...].T, 0.0).astype(BF16)
        bs = _bias_by_position(bs_ref[...])
        for ci in range(ct // n):
            rows = slice(ci * n, (ci + 1) * n)
            ua, dua, dva, vn, xhat, rstd, s = _sgu_chunk_fwd(u_ref[rows, :], v_ref[rows, :], g_ref[...], b_ref[...],
                                                             wm, bs)
            do = do_ref[rows, :]
            du_ref[rows, :] = (do * s * dua).astype(BF16)
            ds = do * ua
            dsb = ds.astype(BF16)
            dbs_ref[...] += jnp.sum(ds, axis=1, keepdims=True)
            dw_ref[...] += _dot_nt(dsb, vn.astype(BF16))
            dvn = _dot(wmt, dsb)
            dva_in, dg, db = _ln_bwd(dvn, xhat, rstd, g_ref[...])
            dg_ref[...] += dg
            db_ref[...] += db
            dv_ref[rows, :] = (dva_in * dva).astype(BF16)

        @pl.when(c == nct - 1)
        def _():
            dw_ref[...] = jnp.where(cc <= r, dw_ref[...], 0.0)

    vec = pl.BlockSpec((None, 1, SG_DIM), lambda g, c: (g, 0, 0))
    mat = pl.BlockSpec((None, n, n), lambda g, c: (g, 0, 0))
    col = pl.BlockSpec((None, n, 1), lambda g, c: (g, 0, 0))
    out_slab = pl.BlockSpec((ct, SG_DIM), lambda g, c: (c, g))
    return pl.pallas_call(
        body,
        grid=(ng, nct),
        in_specs=[pl.BlockSpec((None, ct, SG_DIM), lambda g, c: (off_u + g, c, 0)),
                  pl.BlockSpec((None, ct, SG_DIM), lambda g, c: (off_v + g, c, 0)),
                  pl.BlockSpec((None, ct, SG_DIM), lambda g, c: (ng + g, c, 0)), vec, vec, mat, vec],
        out_specs=[out_slab, out_slab, vec, vec, mat, col],
        out_shape=[pltpu.HBM((t, ng * SG_DIM), BF16), pltpu.HBM((t, ng * SG_DIM), BF16),
                   jax.ShapeDtypeStruct((ng, 1, SG_DIM), F32), jax.ShapeDtypeStruct((ng, 1, SG_DIM), F32),
                   jax.ShapeDtypeStruct((ng, n, n), F32), jax.ShapeDtypeStruct((ng, n, 1), F32)],
        compiler_params=_params(2),
        name="sgu_bwd",
    )(proj, proj, dmix, ln_g, ln_b, w_s, b_row)


def _attn_bwd(dyb, wo, qb, kb, vb):
    t, d = dyb.shape
    m_len = kb.shape[0]
    tm = _row_tile(t)
    dh = d // X_HEADS
    scale = dh ** -0.5

    def body(dy_ref, wo_ref, q_ref, k_ref, v_ref, dq_ref, dk_ref, dv_ref):
        i = pl.program_id(0)

        @pl.when(i == 0)
        def _():
            dk_ref[...] = jnp.zeros_like(dk_ref)
            dv_ref[...] = jnp.zeros_like(dv_ref)

        do = _dot_nt(dy_ref[...], wo_ref[...]).astype(BF16)
        for hd in range(X_HEADS):
            sl = slice(hd * dh, (hd + 1) * dh)
            qh = q_ref[:, sl]
            p = _softmax_rows(_dot_nt(qh, k_ref[:, sl]) * scale)
            doh = do[:, sl]
            dp = _dot_nt(doh, v_ref[:, sl])
            ds = (p * (dp - jnp.sum(dp * p, axis=-1, keepdims=True)) * scale).astype(BF16)
            dq_ref[:, sl] = _dot(ds, k_ref[:, sl]).astype(BF16)
            dk_ref[:, sl] += _dot_tn(ds, qh)
            dv_ref[:, sl] += _dot_tn(p.astype(BF16), doh)

    row = pl.BlockSpec((tm, d), lambda i: (i, 0))
    full = lambda a: pl.BlockSpec(a.shape, lambda i: (0, 0))
    kv = pl.BlockSpec((m_len, d), lambda i: (0, 0))
    return pl.pallas_call(
        body,
        grid=(t // tm,),
        in_specs=[row, full(wo), row, full(kb), full(vb)],
        out_specs=[row, kv, kv],
        out_shape=[jax.ShapeDtypeStruct((t, d), BF16), jax.ShapeDtypeStruct((m_len, d), F32),
                   jax.ShapeDtypeStruct((m_len, d), F32)],
        compiler_params=_params(1),
        name="attn_bwd",
    )(dyb, wo, qb, kb, vb)


def _mem_bwd(dk, dv, mb, xhat, rstd, g, wk, wv):
    m_len, d = dk.shape

    def body(dk_ref, dv_ref, mb_ref, xh_ref, rs_ref, g_ref, wk_ref, wv_ref, gwk_ref, gwv_ref, dg_ref, db_ref):
        dkb = dk_ref[...].astype(BF16)
        dvb = dv_ref[...].astype(BF16)
        mb_v = mb_ref[...]
        gwk_ref[...] = _dot_tn(mb_v, dkb).astype(BF16)
        gwv_ref[...] = _dot_tn(mb_v, dvb).astype(BF16)
        dm = _dot_nt(dkb, wk_ref[...]) + _dot_nt(dvb, wv_ref[...])
        _, dg, db = _ln_bwd(dm, xh_ref[...], rs_ref[...], g_ref[...])
        dg_ref[...] = dg
        db_ref[...] = db

    return pl.pallas_call(
        body,
        out_shape=[jax.ShapeDtypeStruct((d, d), BF16), jax.ShapeDtypeStruct((d, d), BF16),
                   jax.ShapeDtypeStruct((1, d), F32), jax.ShapeDtypeStruct((1, d), F32)],
        compiler_params=pltpu.CompilerParams(vmem_limit_bytes=VMEM_LIMIT_V7X),
        name="mem_bwd",
    )(dk, dv, mb, xhat, rstd, g, wk, wv)


def _adamw(w, g, m, v):
    m = ADAM_B1 * m + (1.0 - ADAM_B1) * g
    v = ADAM_B2 * v + (1.0 - ADAM_B2) * (g * g)
    m_hat = m / (1.0 - ADAM_B1 ** ADAM_STEP)
    v_hat = v / (1.0 - ADAM_B2 ** ADAM_STEP)
    delta = -ADAM_LR * (m_hat / (jnp.sqrt(v_hat) + ADAM_EPS) + ADAM_WD * w)
    return delta, m, v


def _slot_sum(ref):
    g = ref[0].astype(F32)
    for s in range(1, N_DEV):
        g = g + ref[s].astype(F32)
    return g


def _adam_sharded(lands, w, m, v, axis, name):
    rows, cols = w.shape
    nl = len(lands)
    transposed = axis == 1 and nl == 2
    if transposed:
        rows, cols = cols, rows
        tr = 256
        grid = (rows // tr,)
        wblk = pl.BlockSpec((cols, tr), lambda i: (0, i))
        lblk = [pl.BlockSpec((N_DEV, tr, a.shape[2]), lambda i: (0, i, 0)) for a in lands]
    elif axis == 1:
        tr = 256 if rows % 256 == 0 else rows
        grid = (rows // tr,)
        wblk = pl.BlockSpec((tr, cols), lambda i: (i, 0))
        lblk = [pl.BlockSpec((N_DEV, tr, a.shape[2]), lambda i: (0, i, 0)) for a in lands]
    else:
        tc = _col_tile(cols)
        grid = (cols // tc,)
        wblk = pl.BlockSpec((rows, tc), lambda i: (0, i))
        lblk = [pl.BlockSpec((N_DEV, a.shape[1], tc), lambda i: (0, 0, i)) for a in lands]

    def body(*refs):
        w_ref, m_ref, v_ref = refs[nl:nl + 3]
        g_ref, d_ref, nm_ref, nv_ref = refs[nl + 3:]
        g = _slot_sum(refs[0])
        if nl == 2:
            tail = _slot_sum(refs[1])
            if transposed:
                g = jnp.concatenate([g.T, tail.T[:cols - g.shape[1], :]], axis=0)
            elif axis == 1:
                g = jnp.concatenate([g, tail[:, :cols - g.shape[1]]], axis=1)
            else:
                g = jnp.concatenate([g, tail[:rows - g.shape[0], :]], axis=0)
        delta, nm, nv = _adamw(w_ref[...], g, m_ref[...], v_ref[...])
        g_ref[...] = g
        d_ref[...] = delta
        nm_ref[...] = nm
        nv_ref[...] = nv

    shp = pltpu.HBM(w.shape, F32)
    return pl.pallas_call(
        body,
        grid=grid,
        in_specs=lblk + [wblk, wblk, wblk],
        out_specs=[wblk, wblk, wblk, wblk],
        out_shape=[shp, shp, shp, shp],
        compiler_params=_params(1),
        name=name,
    )(*[pltpu.with_memory_space_constraint(a, pltpu.HBM) for a in (*lands, w, m, v)])


def _mesh_pos():
    return lax.axis_index("x"), lax.axis_index("y"), lax.axis_index("c")


def _peer(k):
    x, y, c = _mesh_pos()
    pos = (x ^ (k >> 2), y ^ ((k >> 1) & 1), c ^ (k & 1))
    return pos, 4 * pos[0] + 2 * pos[1] + pos[2]


def _sem_index(row, k):
    return row * (N_DEV - 1) + k - 1


def _window(ref, axis, start, size):
    align = 16 if axis == 0 else LANES
    start = pl.multiple_of(start, align)
    return ref.at[pl.ds(start, size), :] if axis == 0 else ref.at[:, pl.ds(start, size)]


def _piece_refs(piece, srcs, lands, me, peer):
    kind, si, li, axis, base, stride, shape = piece
    if kind == "gather":
        return srcs[si], _window(lands[li], axis, base + stride * me, shape[axis])
    return _window(srcs[si], axis, base + stride * peer, shape[axis]), lands[li].at[me]


def _place_own(srcs, land_shapes, pieces, name):
    ns, nl, npc = len(srcs), len(land_shapes), len(pieces)

    def body(*refs):
        s_refs = refs[:ns]
        l_refs = refs[ns:ns + nl]
        bufs = refs[ns + nl:ns + nl + npc]
        sems = refs[ns + nl + npc]
        x, y, c = _mesh_pos()
        me = 4 * x + 2 * y + c
        loads = []
        for p, piece in enumerate(pieces):
            src, dst = _piece_refs(piece, s_refs, l_refs, me, me)
            cp = pltpu.make_async_copy(src, bufs[p], sems.at[0, p])
            cp.start()
            loads.append((cp, dst))
        stores = []
        for p, (cp, dst) in enumerate(loads):
            cp.wait()
            out = pltpu.make_async_copy(bufs[p], dst, sems.at[1, p])
            out.start()
            stores.append(out)
        for out in stores:
            out.wait()

    out = pl.pallas_call(
        body,
        in_specs=[ANY] * ns,
        out_specs=[HBM] * nl,
        out_shape=[pltpu.HBM(s.shape, s.dtype) for s in land_shapes],
        scratch_shapes=[pltpu.VMEM(pc[6], srcs[pc[1]].dtype) for pc in pieces] + [pltpu.SemaphoreType.DMA((2, npc))],
        compiler_params=pltpu.CompilerParams(vmem_limit_bytes=VMEM_LIMIT_V7X),
        name=name,
    )(*srcs)
    return list(out)


def _comm_start(srcs, lands, pieces, groups, name, after=()):
    ns, nl, na, ng = len(srcs), len(lands), len(after), len(groups)

    def body(*refs):
        s_refs = refs[:ns]
        l_refs = refs[ns:ns + nl]
        outs = refs[ns + nl + na:]
        sems = outs[:2 * ng]
        token = outs[-1]
        x, y, c = _mesh_pos()
        me = 4 * x + 2 * y + c
        for g, members in enumerate(groups):
            for row, p in enumerate(members):
                for k in range(1, N_DEV):
                    pos, peer = _peer(k)
                    src, dst = _piece_refs(pieces[p], s_refs, l_refs, me, peer)
                    pltpu.make_async_remote_copy(src_ref=src, dst_ref=dst, send_sem=sems[2 * g].at[_sem_index(row, k)],
                                                 recv_sem=sems[2 * g + 1].at[_sem_index(row, k)], device_id=pos,
                                                 device_id_type=MESH_ID).start()
        token[...] = jnp.zeros_like(token)

    sem_shapes = []
    for members in groups:
        sem_shapes += [pltpu.SemaphoreType.DMA((len(members) * (N_DEV - 1),))] * 2
    hbm_of = lambda a: pltpu.HBM(a.shape, a.dtype)
    out = pl.pallas_call(
        body,
        in_specs=[HBM] * (ns + nl) + [ANY] * na,
        out_specs=[SEM] * (2 * ng) + [HBM] * (ns + nl) + [pl.BlockSpec(memory_space=pltpu.VMEM)],
        out_shape=sem_shapes + [hbm_of(a) for a in srcs] + [hbm_of(a) for a in lands]
        + [jax.ShapeDtypeStruct((8, LANES), F32)],
        input_output_aliases={i: 2 * ng + i for i in range(ns + nl)},
        compiler_params=pltpu.CompilerParams(has_side_effects=DATAFLOW),
        name=name,
    )(*[pltpu.with_memory_space_constraint(a, pltpu.HBM) for a in list(srcs) + list(lands)], *after)
    sems = [(out[2 * g], out[2 * g + 1]) for g in range(ng)]
    return sems, list(out[2 * ng:2 * ng + ns]), list(out[2 * ng + ns:2 * ng + ns + nl]), out[-1]


def _comm_wait(srcs, lands, pieces, members, sems, after, name):
    ns, nl, na = len(srcs), len(lands), len(after)

    def body(*refs):
        s_refs = refs[:ns]
        l_refs = refs[ns:ns + nl]
        send_sems, recv_sems = refs[ns + nl:ns + nl + 2]
        x, y, c = _mesh_pos()
        me = 4 * x + 2 * y + c
        for row, p in enumerate(members):
            for k in range(1, N_DEV):
                pos, peer = _peer(k)
                src, dst = _piece_refs(pieces[p], s_refs, l_refs, me, peer)
                cp = pltpu.make_async_remote_copy(src_ref=src, dst_ref=dst, send_sem=send_sems.at[_sem_index(row, k)],
                                                  recv_sem=recv_sems.at[_sem_index(row, k)], device_id=pos,
                                                  device_id_type=MESH_ID)
                cp.wait_send()
                cp.wait_recv()

    hbm_of = lambda a: pltpu.HBM(a.shape, a.dtype)
    out = pl.pallas_call(
        body,
        in_specs=[HBM] * (ns + nl) + [SEM, SEM] + [ANY] * na,
        out_specs=[HBM] * (ns + nl),
        out_shape=[hbm_of(a) for a in srcs] + [hbm_of(a) for a in lands],
        input_output_aliases={i: i for i in range(ns + nl)},
        compiler_params=pltpu.CompilerParams(has_side_effects=DATAFLOW),
        name=name,
    )(*srcs, *lands, sems[0], sems[1], *after)
    return list(out[ns:])


def _landed_block(piece, lands, owner):
    _, _, li, axis, base, stride, shape = piece
    return _window(lands[li], axis, base + stride * owner, shape[axis])


def _copy_stage(name, bufs, in_sems, out_sem_sizes, emit, after=()):
    nb, ni, no, na = len(bufs), len(in_sems), len(out_sem_sizes), len(after)

    def body(*refs):
        b_refs = refs[:nb]
        i_refs = refs[nb:nb + ni]
        o_refs = refs[nb + ni + na:nb + ni + na + no]
        emit(b_refs, i_refs, o_refs)
        refs[-1][...] = jnp.zeros_like(refs[-1])

    hbm_of = lambda a: pltpu.HBM(a.shape, a.dtype)
    out = pl.pallas_call(
        body,
        in_specs=[HBM] * nb + [SEM] * ni + [ANY] * na,
        out_specs=[SEM] * no + [HBM] * nb + [pl.BlockSpec(memory_space=pltpu.VMEM)],
        out_shape=[pltpu.SemaphoreType.DMA((n,)) for n in out_sem_sizes] + [hbm_of(a) for a in bufs]
        + [jax.ShapeDtypeStruct((8, LANES), F32)],
        input_output_aliases={i: no + i for i in range(nb)},
        compiler_params=pltpu.CompilerParams(has_side_effects=DATAFLOW),
        name=name,
    )(*[pltpu.with_memory_space_constraint(a, pltpu.HBM) for a in bufs], *in_sems, *after)
    return list(out[:no]), list(out[no:no + nb]), out[-1]


def _remote(src, dst, send, recv, to):
    return pltpu.make_async_remote_copy(src_ref=src, dst_ref=dst, send_sem=send, recv_sem=recv, device_id=to,
                                        device_id_type=MESH_ID)


def _routed_gather(srcs, lands, pieces, meanwhile, name):
    ns, npc = len(srcs), len(pieces)

    def places():
        x, y, c = _mesh_pos()
        index = lambda p: 4 * p[0] + 2 * p[1] + p[2]
        me, sib = (x, y, c), (x, y, 1 - c)
        xnb, ynb = (1 - x, y, c), (x, 1 - y, c)
        got_first = (x ^ (1 - c), y ^ c, c)
        pass_to = (x ^ c, y ^ (1 - c), c)
        diag = (1 - x, 1 - y, c)
        return index, me, sib, xnb, ynb, got_first, pass_to, diag

    def start(b, _, o):
        index, me, sib, xnb, ynb, *_rest = places()
        send_a, recv_sib, recv_nb = o
        for p, piece in enumerate(pieces):
            src, dst = _piece_refs(piece, b[:ns], b[ns:], index(me), 0)
            _remote(src, dst, send_a.at[3 * p], recv_sib.at[p], sib).start()
            _remote(src, dst, send_a.at[3 * p + 1], recv_nb.at[2 * p], xnb).start()
            _remote(src, dst, send_a.at[3 * p + 2], recv_nb.at[2 * p + 1], ynb).start()

    def pass_a(b, i, o):
        index, me, sib, xnb, ynb, got_first, pass_to, _diag = places()
        (recv_nb,) = i
        send_f, recv_f, send_d, recv_d = o
        for p, piece in enumerate(pieces):
            for j, nb in enumerate((xnb, ynb)):
                blk = _landed_block(piece, b, index(nb))
                _remote(blk, blk, send_f.at[2 * p + j], recv_nb.at[2 * p + j], sib).wait_recv()
                _remote(blk, blk, send_f.at[2 * p + j], recv_f.at[2 * p + j], sib).start()
            blk = _landed_block(piece, b, index(got_first))
            _remote(blk, blk, send_d.at[p], recv_d.at[p], pass_to).start()

    def pass_b(b, i, o):
        index, me, sib, *_mid, diag = places()
        (recv_d,) = i
        send_g, recv_g = o
        for p, piece in enumerate(pieces):
            blk = _landed_block(piece, b, index(diag))
            _remote(blk, blk, send_g.at[p], recv_d.at[p], sib).wait_recv()
            _remote(blk, blk, send_g.at[p], recv_g.at[p], sib).start()

    def last(b, i, _):
        index, me, sib, *_others = places()
        send_a, recv_sib, send_f, recv_f, send_d, send_g, recv_g = i
        for p, piece in enumerate(pieces):
            src, dst = _piece_refs(piece, b[:ns], b[ns:], index(me), 0)
            cp = lambda s_sem, r_sem: _remote(src, dst, s_sem, r_sem, sib)
            cp(send_a.at[3 * p], recv_sib.at[p]).wait_recv()
            cp(send_a.at[3 * p], recv_g.at[p]).wait_recv()
            for j in range(3):
                cp(send_a.at[3 * p + j], recv_sib.at[p]).wait_send()
            for j in range(2):
                cp(send_f.at[2 * p + j], recv_f.at[2 * p + j]).wait_recv()
                cp(send_f.at[2 * p + j], recv_f.at[2 * p + j]).wait_send()
            cp(send_d.at[p], recv_sib.at[p]).wait_send()
            cp(send_g.at[p], recv_sib.at[p]).wait_send()

    (send_a, recv_sib, recv_nb), bufs, started = _copy_stage(name + "_start", list(srcs) + list(lands), [],
                                                             [3 * npc, npc, 2 * npc], start)
    srcs, lands = bufs[:ns], bufs[ns:]
    (send_f, recv_f, send_d, recv_d), lands, _ = _copy_stage(name + "_pass_a", lands, [recv_nb],
                                                             [2 * npc, 2 * npc, npc, npc],
                                                             lambda b, i, o: pass_a(b, i, o), after=meanwhile(started))
    (send_g, recv_g), lands, tok = _copy_stage(name + "_pass_b", lands, [recv_d], [npc, npc], pass_b)
    _, bufs, _ = _copy_stage(name + "_last", list(srcs) + list(lands),
                             [send_a, recv_sib, send_f, recv_f, send_d, send_g, recv_g], [], last)
    return bufs[ns:], tok


def _paired_gather(srcs, lands, pieces, groups, after, name):
    ns, ng = len(srcs), len(groups)
    far = (1, 2, 3)

    def me_sib():
        x, y, c = _mesh_pos()
        return 4 * x + 2 * y + c, (x, y, 1 - c)

    def start(b, _, o):
        me, sib = me_sib()
        for g, members in enumerate(groups):
            send, recv_sib, recv_far = o[3 * g:3 * g + 3]
            for r, p in enumerate(members):
                src, dst = _piece_refs(pieces[p], b[:ns], b[ns:], me, 0)
                _remote(src, dst, send.at[4 * r], recv_sib.at[r], sib).start()
                for j in far:
                    _remote(src, dst, send.at[4 * r + j], recv_far.at[3 * r + j - 1], _peer(2 * j)[0]).start()

    def forward(b, i, o):
        _, sib = me_sib()
        for g, members in enumerate(groups):
            send_f, recv_f = o[2 * g:2 * g + 2]
            for r, p in enumerate(members):
                for j in far:
                    blk = _landed_block(pieces[p], b, _peer(2 * j)[1])
                    _remote(blk, blk, send_f.at[3 * r + j - 1], i[g].at[3 * r + j - 1], sib).wait_recv()
                    _remote(blk, blk, send_f.at[3 * r + j - 1], recv_f.at[3 * r + j - 1], sib).start()

    def last(sub):
        def emit(b, i, _):
            send, recv_sib, send_f, recv_f = i
            me, sib = me_sib()
            for r, piece in enumerate(sub):
                src, dst = _piece_refs(piece, b[:-1], b[-1:], me, 0)
                _remote(src, dst, send.at[4 * r], recv_sib.at[r], sib).wait_recv()
                for j in range(4):
                    _remote(src, dst, send.at[4 * r + j], recv_sib.at[r], sib).wait_send()
                for j in far:
                    blk = _landed_block(piece, b[-1:], _peer(2 * j + 1)[1])
                    _remote(blk, blk, send_f.at[3 * r + j - 1], recv_f.at[3 * r + j - 1], sib).wait_recv()
                    _remote(blk, blk, send_f.at[3 * r + j - 1], recv_f.at[3 * r + j - 1], sib).wait_send()
        return emit

    sizes = []
    for members in groups:
        sizes += [4 * len(members), len(members), 3 * len(members)]
    sems, bufs, started = _copy_stage(name + "_start", list(srcs) + list(lands), [], sizes, start, after=after)
    srcs = bufs[:ns]
    state = dict(lands=bufs[ns:])

    def finish(g, after):
        if "passed" not in state:
            sizes_f = []
            for members in groups:
                sizes_f += [3 * len(members)] * 2
            state["passed"], state["lands"], _ = _copy_stage(name + "_pass", state["lands"],
                                                             [sems[3 * k + 2] for k in range(ng)], sizes_f, forward,
                                                             after=after)
            after = ()
        members = groups[g]
        sub = [(pieces[p][0], r, 0) + pieces[p][3:] for r, p in enumerate(members)]
        _, bufs, _ = _copy_stage("%s_last_%d" % (name, g), [srcs[pieces[p][1]] for p in members] + [state["lands"][g]],
                                 [sems[3 * g], sems[3 * g + 1], state["passed"][2 * g], state["passed"][2 * g + 1]],
                                 [], last(sub), after=after)
        return bufs[-1]

    return finish, started


_SMALL_NAMES = ("ln1_g", "ln1_b", "hg_lb_logits", "hg_norm_g", "sg_ln_g", "sg_ln_b", "sg_w_s", "sg_b_s",
                "ln2_g", "ln2_b", "mem_ln_g", "mem_ln_b", "ln3_g", "ln3_b", "ln4_g", "ln4_b")


_VEC_NAMES = ("ln1_g", "ln1_b", "ln2_g", "ln2_b", "mem_ln_g", "mem_ln_b", "ln3_g", "ln3_b", "ln4_g", "ln4_b")
_ROW_NAMES = ("hg_lb_logits", "hg_norm_g", "sg_ln_g", "sg_ln_b", "sg_b_s", "sg_w_s")
VEC_ROWS = 16


def _row_plan(shapes):
    plan, pos = {}, 0
    for k in _ROW_NAMES:
        shp = shapes[k]
        slabs, off = [], pos
        for idx in itertools.product(*[range(dim) for dim in shp[:-2]]):
            slabs.append((idx, off, shp[-2]))
            off += shp[-2]
        plan[k] = (pos, slabs)
        pos = -(-off // 8) * 8
    return plan, -(-pos // 16) * 16


def _pack_small_grads(gs, shapes, loss):
    d = gs[_VEC_NAMES[0]].size
    vec = jnp.concatenate([gs[k].reshape(1, -1) for k in _VEC_NAMES] + [jnp.tile(loss, (1, d // LANES))], axis=0)
    vec = jnp.pad(vec, ((0, VEC_ROWS - vec.shape[0]), (0, 0)))
    plan, total = _row_plan(shapes)
    parts, pos = [], 0
    for k in _ROW_NAMES:
        first, slabs = plan[k]
        rows = gs[k].reshape(-1, LANES)
        end = slabs[-1][1] + slabs[-1][2]
        nxt = -(-end // 8) * 8
        parts.append(jnp.pad(rows, ((0, nxt - first - rows.shape[0]), (0, 0))))
        pos = nxt
    parts.append(jnp.zeros((total - pos, LANES), F32))
    return vec, jnp.concatenate(parts, axis=0)


def _adam_small(land_vec, land_rows, w, m, v):
    names = _VEC_NAMES + _ROW_NAMES
    n = len(names)
    shapes = {k: w[k].shape for k in names}
    plan, _ = _row_plan(shapes)

    def body(*refs):
        lv_ref, lr_ref = refs[:2]
        w_refs, m_refs, v_refs = refs[2:2 + n], refs[2 + n:2 + 2 * n], refs[2 + 2 * n:2 + 3 * n]
        outs = refs[2 + 3 * n:2 + 7 * n]
        loss_ref = refs[2 + 7 * n]
        gv_s, gr_s = refs[3 + 7 * n:]
        gv_s[...] = _slot_sum(lv_ref)
        gr_s[...] = _slot_sum(lr_ref)
        loss_ref[...] = gv_s[len(_VEC_NAMES):len(_VEC_NAMES) + 1, :LANES]
        for p, k in enumerate(names):
            if k in _VEC_NAMES:
                row = _VEC_NAMES.index(k)
                slabs = [((), None, None)]
            else:
                slabs = plan[k][1]
            for idx, off, rows in slabs:
                g = gv_s[row:row + 1, :] if off is None else gr_s[off:off + rows, :]
                sel = idx + (slice(None), slice(None))
                delta, nm, nv = _adamw(w_refs[p][sel], g, m_refs[p][sel], v_refs[p][sel])
                for o, val in zip(range(4), (g, delta, nm, nv)):
                    outs[o * n + p][sel] = val

    flat = lambda tree: [tree[k] for k in names]
    shp = [jax.ShapeDtypeStruct(shapes[k], F32) for k in names]
    out = pl.pallas_call(
        body,
        out_shape=shp * 4 + [jax.ShapeDtypeStruct((1, LANES), F32)],
        scratch_shapes=[pltpu.VMEM(land_vec.shape[1:], F32), pltpu.VMEM(land_rows.shape[1:], F32)],
        name="adam_small",
    )(land_vec, land_rows, *flat(w), *flat(m), *flat(v))
    return [dict(zip(names, out[o * n:(o + 1) * n])) for o in range(4)], out[4 * n]


_COL_FFN = ("ffn1_w_gate", "ffn1_w_up", "ffn2_w_gate", "ffn2_w_up")
_ROW_FFN = ("ffn1_w_down", "ffn2_w_down")
_ROW_SQ = ("w_out", "xa_w_q", "xa_w_k", "xa_w_v", "xa_w_o")
_BIG_NAMES = ("ffn1_w_gate", "ffn1_w_up", "ffn1_w_down", "w_in", "w_out", "xa_w_q", "xa_w_k", "xa_w_v", "xa_w_o",
              "ffn2_w_gate", "ffn2_w_up", "ffn2_w_down")


def _ffn_split(fs):
    main = (fs // MXU_WIDTH_V7X) * MXU_WIDTH_V7X
    tail = fs - main
    tail_pad = -(-tail // LANES) * LANES
    assert main > 0 and tail > 0
    return main, tail, tail_pad


def _layout(name, shard_shape):
    r, c = shard_shape
    if name in _COL_FFN:
        main, tail, pad = _ffn_split(c)
        return (r, N_DEV * (main + pad)), [(1, 0, main, (r, main), (0, main)),
                                           (1, N_DEV * main, pad, (r, pad), (main, c))]
    if name in _ROW_FFN:
        main, tail, pad = _ffn_split(r)
        return (N_DEV * (main + pad), c), [(0, 0, main, (main, c), (0, main)),
                                           (0, N_DEV * main, pad, (pad, c), (main, r))]
    if name == "w_in":
        return (r, N_DEV * c), [(1, 0, c, (r, c), (0, c))]
    return (N_DEV * r, c), [(0, 0, r, (r, c), (0, r))]


def _shard_pieces(name, shard):
    out = []
    for axis, _, _, shape, (lo, hi) in _layout(name, shard.shape)[1]:
        part = shard[lo:hi, :] if axis == 0 else shard[:, lo:hi]
        pad = [(0, shape[0] - part.shape[0]), (0, shape[1] - part.shape[1])]
        out.append(jnp.pad(part, pad).astype(BF16))
    return out


def _gather_plan(names, shards):
    srcs, land_shapes, pieces, index = [], [], [], {}
    for li, name in enumerate(names):
        shape2d, parts = _layout(name, shards[name].shape)
        land_shapes.append(jax.ShapeDtypeStruct(shape2d, BF16))
        index[name] = []
        for (axis, base, stride, shape, _), src in zip(parts, _shard_pieces(name, shards[name])):
            index[name].append(len(pieces))
            pieces.append(("gather", len(srcs), li, axis, base, stride, shape))
            srcs.append(src)
    return srcs, land_shapes, pieces, index


def _scatter_plan(names, grads, shard_shapes):
    srcs, land_shapes, pieces, index = [], [], [], {}
    for si, name in enumerate(names):
        _, parts = _layout(name, shard_shapes[name])
        srcs.append(grads[name])
        index[name] = []
        for axis, base, stride, shape, _ in parts:
            index[name].append(len(land_shapes))
            pieces.append(("scatter", si, len(land_shapes), axis, base, stride, shape))
            land_shapes.append(jax.ShapeDtypeStruct((N_DEV,) + shape, grads[name].dtype))
    return srcs, land_shapes, pieces, index


def _small_views(small):
    row = lambda a: a.reshape(1, -1)
    ln = {k: row(small[k]) for k in ("ln1_g", "ln1_b", "ln2_g", "ln2_b", "ln3_g", "ln3_b", "ln4_g", "ln4_b",
                                      "mem_ln_g", "mem_ln_b", "hg_norm_g")}
    sg_w = small["sg_w_s"].reshape(SG_GROUPS, SG_CHUNK, SG_CHUNK)
    sg = dict(logits=jnp.swapaxes(small["hg_lb_logits"], 0, 1),
              g=small["sg_ln_g"].reshape(SG_GROUPS, 1, SG_DIM), b=small["sg_ln_b"].reshape(SG_GROUPS, 1, SG_DIM),
              w=sg_w, bs=small["sg_b_s"].reshape(SG_GROUPS, 1, SG_CHUNK))
    return ln, sg


def _forward(x, xb, mem, target, get_w, small, first_deps=()):
    ln, sg = _small_views(small)
    a1, b1, s1 = _ffn_up(xb, get_w("ffn1_w_gate", ()), get_w("ffn1_w_up", ()), "ffn1_up", deps=first_deps)
    h1b, xh1, rs1 = _mm_res_ln(s1, get_w("ffn1_w_down", (s1,)), x, ln["ln1_g"], ln["ln1_b"], 0.5, "ffn1_down_ln")
    proj = _mm_nn(h1b, get_w("w_in", (h1b,)), "mix_in")
    oraw, mix, states = _hgrn_fwd(proj, sg["logits"], ln["hg_norm_g"])
    mix = _sgu_fwd(proj, mix, sg["g"], sg["b"], sg["w"], sg["bs"])
    h2b, xh2, rs2 = _mm_res_ln(mix, get_w("w_out", (mix,)), (xh1, ln["ln1_g"], ln["ln1_b"]), ln["ln2_g"], ln["ln2_b"],
                               1.0, "mix_out_ln")
    mb, mxh, mrs, kb, vb = _mem_kv(mem, ln["mem_ln_g"], ln["mem_ln_b"], get_w("xa_w_k", (h2b,)), get_w("xa_w_v", (h2b,)))
    qb, att = _attn_fwd(h2b, get_w("xa_w_q", (mrs,)), kb, vb)
    h3b, xh3, rs3 = _mm_res_ln(att, get_w("xa_w_o", (att,)), (xh2, ln["ln2_g"], ln["ln2_b"]), ln["ln3_g"], ln["ln3_b"],
                               1.0, "attn_out_ln")
    a2, b2, s2 = _ffn_up(h3b, get_w("ffn2_w_gate", (h3b,)), get_w("ffn2_w_up", (h3b,)), "ffn2_up")
    loss, dy4, dy4b, dg4, db4 = _mm_res_ln(s2, get_w("ffn2_w_down", (s2,)), (xh3, ln["ln3_g"], ln["ln3_b"]),
                                           ln["ln4_g"], ln["ln4_b"], 0.5, "ffn2_down_ln_loss", target=target)
    return dict(xb=xb, a1=a1, b1=b1, s1=s1, h1b=h1b, xh1=xh1, rs1=rs1, proj=proj, oraw=oraw, mix=mix, states=states,
                h2b=h2b, xh2=xh2, rs2=rs2, mb=mb, mxh=mxh, mrs=mrs, kb=kb, vb=vb, qb=qb, att=att, h3b=h3b, xh3=xh3,
                rs3=rs3, a2=a2, b2=b2, s2=s2, loss=loss, dy4=dy4, dy4b=dy4b, dg4=dg4, db4=db4)


def _backward(sv, wt, small, send):
    ln, sg = _small_views(small)
    gs = {"ln4_g": sv["dg4"], "ln4_b": sv["db4"]}
    loss, dy4, dy4b = sv["loss"], sv["dy4"], sv["dy4b"]
    g_down2 = _mm_tn(sv["s2"], dy4b, "g_ffn2_down", scale=0.5)
    da2, db2, dy3, dy3b, gs["ln3_g"], gs["ln3_b"] = _ffn_bwd_fused(
        dy4, dy4b, wt["ffn2_w_down"], wt["ffn2_w_gate"], wt["ffn2_w_up"], sv["a2"], sv["b2"], 0.5,
        (sv["xh3"], sv["rs3"], ln["ln3_g"]), "ffn2_bwd")
    g_gate2 = _mm_tn(sv["h3b"], da2, "g_ffn2_gate")
    g_up2 = _mm_tn(sv["h3b"], db2, "g_ffn2_up")
    tok = send(("ffn2_w_down", "ffn2_w_gate", "ffn2_w_up"), (g_down2, g_gate2, g_up2))

    g_o = _mm_tn(sv["att"], dy3b, "g_xa_o", deps=(tok,))
    dqb, dk, dv = _attn_bwd(dy3b, wt["xa_w_o"], sv["qb"], sv["kb"], sv["vb"])
    g_q = _mm_tn(sv["h2b"], dqb, "g_xa_q")
    g_k, g_v, gs["mem_ln_g"], gs["mem_ln_b"] = _mem_bwd(dk, dv, sv["mb"], sv["mxh"], sv["mrs"], ln["mem_ln_g"],
                                                        wt["xa_w_k"], wt["xa_w_v"])
    tok = send(("xa_w_o", "xa_w_q", "xa_w_k", "xa_w_v"), (g_o, g_q, g_k, g_v))
    dy2, dy2b, gs["ln2_g"], gs["ln2_b"] = _dx_ln(dy3, [(dqb, wt["xa_w_q"])], (sv["xh2"], sv["rs2"], ln["ln2_g"]),
                                                 "attn_dx_ln", deps=(tok,))

    g_out = _mm_tn(sv["mix"], dy2b, "g_w_out")
    dmix = _mm_nt(dy2b, wt["w_out"], "mix_out_bwd")
    dq, dfz, div, dgg, dlg, dgn = _hgrn_bwd(sv["proj"], sv["oraw"], dmix, sv["states"], sg["logits"], ln["hg_norm_g"])
    du, dvv, gs["sg_ln_g"], gs["sg_ln_b"], gs["sg_w_s"], gs["sg_b_s"] = _sgu_bwd(
        sv["proj"], dmix, sg["g"], sg["b"], sg["w"], sg["bs"])
    gs["hg_lb_logits"] = jnp.swapaxes(dlg, 0, 1)
    gs["hg_norm_g"] = jnp.sum(dgn, axis=0)
    dproj = [dq, dfz, div, dgg, du, dvv]
    g_in = _mm_tn(sv["h1b"], dproj, "g_w_in")
    tok = send(("w_out", "w_in"), (g_out, g_in))
    dy1, dy1b, gs["ln1_g"], gs["ln1_b"] = _dx_ln(dy2, [(dproj, wt["w_in"])], (sv["xh1"], sv["rs1"], ln["ln1_g"]),
                                                 "mix_dx_ln", deps=(tok,))

    g_down1 = _mm_tn(sv["s1"], dy1b, "g_ffn1_down", scale=0.5)
    tok = send(("ffn1_w_down",), (g_down1,))
    da1, db1 = _ffn_bwd_act(dy1b, wt["ffn1_w_down"], sv["a1"], sv["b1"], 0.5, "ffn1_bwd_act", deps=(tok,))
    g_gate1 = _mm_tn(sv["xb"], da1, "g_ffn1_gate")
    tok = send(("ffn1_w_gate",), (g_gate1,))
    g_up1 = _mm_tn(sv["xb"], db1, "g_ffn1_up", deps=(tok,))
    tok = send(("ffn1_w_up",), (g_up1,))
    grad_x = _dx_ln(dy1, [(da1, wt["ffn1_w_gate"]), (db1, wt["ffn1_w_up"])], None, "ffn1_dx", deps=(tok,))
    return loss, grad_x, gs


_WEIGHT_NAMES = ("ffn1_w_gate", "ffn1_w_up", "ffn1_w_down", "ln1_g", "ln1_b", "w_in", "hg_lb_logits", "hg_norm_g",
                 "sg_ln_g", "sg_ln_b", "sg_w_s", "sg_b_s", "w_out", "ln2_g", "ln2_b", "mem_ln_g", "mem_ln_b",
                 "xa_w_q", "xa_w_k", "xa_w_v", "xa_w_o", "ln3_g", "ln3_b", "ffn2_w_gate", "ffn2_w_up", "ffn2_w_down",
                 "ln4_g", "ln4_b")
_FIRST = ("ffn1_w_gate", "ffn1_w_up")
_SECOND = ("ffn1_w_down", "w_in")
_THIRD = ("w_out", "xa_w_k", "xa_w_v", "xa_w_q", "xa_w_o", "ffn2_w_gate", "ffn2_w_up", "ffn2_w_down")


def kernel(x, mem, ffn1_w_gate, ffn1_w_up, ffn1_w_down, ln1_g, ln1_b, w_in, hg_lb_logits, hg_norm_g, sg_ln_g, sg_ln_b, sg_w_s, sg_b_s, w_out, ln2_g, ln2_b, mem_ln_g, mem_ln_b, xa_w_q, xa_w_k, xa_w_v, xa_w_o, ln3_g, ln3_b, ffn2_w_gate, ffn2_w_up, ffn2_w_down, ln4_g, ln4_b, loss_target, m_ffn1_w_gate, m_ffn1_w_up, m_ffn1_w_down, m_ln1_g, m_ln1_b, m_w_in, m_hg_lb_logits, m_hg_norm_g, m_sg_ln_g, m_sg_ln_b, m_sg_w_s, m_sg_b_s, m_w_out, m_ln2_g, m_ln2_b, m_mem_ln_g, m_mem_ln_b, m_xa_w_q, m_xa_w_k, m_xa_w_v, m_xa_w_o, m_ln3_g, m_ln3_b, m_ffn2_w_gate, m_ffn2_w_up, m_ffn2_w_down, m_ln4_g, m_ln4_b, v_ffn1_w_gate, v_ffn1_w_up, v_ffn1_w_down, v_ln1_g, v_ln1_b, v_w_in, v_hg_lb_logits, v_hg_norm_g, v_sg_ln_g, v_sg_ln_b, v_sg_w_s, v_sg_b_s, v_w_out, v_ln2_g, v_ln2_b, v_mem_ln_g, v_mem_ln_b, v_xa_w_q, v_xa_w_k, v_xa_w_v, v_xa_w_o, v_ln3_g, v_ln3_b, v_ffn2_w_gate, v_ffn2_w_up, v_ffn2_w_down, v_ln4_g, v_ln4_b):
    args = dict(locals())
    w = {k: args[k] for k in _WEIGHT_NAMES}
    m = {k: args["m_" + k] for k in _WEIGHT_NAMES}
    v = {k: args["v_" + k] for k in _WEIGHT_NAMES}
    shards = {k: w[k][0] for k in _BIG_NAMES}
    shard_shapes = {k: shards[k].shape for k in _BIG_NAMES}
    small = {k: (w[k][0] if k != "hg_lb_logits" else w[k]) for k in _SMALL_NAMES}

    srcs1, shapes1, pieces1, idx1 = _gather_plan(_FIRST, shards)
    lands1 = _place_own(srcs1, shapes1, pieces1, "gather_first_own")
    prepared = {}

    def prepare_rest(started):
        later = {k: shards[k] + started[0, 0] for k in _SECOND + _THIRD}
        placed = ()
        for key, names in (("second", _SECOND), ("third", _THIRD)):
            srcs, shapes, pieces, idx = _gather_plan(names, later)
            lands = _place_own(srcs, shapes, pieces, "gather_%s_own" % key)
            prepared[key] = (srcs, lands, pieces, idx)
            placed += tuple(lands)
        prepared["xb"] = _to_bf16(x[0], "x_bf16", deps=(started,))
        return placed + (prepared["xb"],)

    lands1, tok1 = _routed_gather(srcs1, lands1, pieces1, prepare_rest, "gather_first")
    srcs_p, lands_p, pieces_p, idx_p = prepared["second"]
    finish_second, tok_p = _paired_gather(srcs_p, lands_p, pieces_p, [list(idx_p[k]) for k in _SECOND], (tok1,),
                                          "gather_second")
    srcs2, lands2, pieces2, idx2 = prepared["third"]
    groups2 = [list(idx2[k]) for k in _THIRD]
    sems2, srcs2, lands2, tok2 = _comm_start(srcs2, lands2, pieces2, groups2, "gather_third_start", after=(tok_p,))
    wt = dict(zip(_FIRST, lands1))
    pending = {k: gi for gi, k in enumerate(_THIRD)}

    def get_w(name, after):
        if name in _SECOND and name not in wt:
            wt[name] = finish_second(_SECOND.index(name), after)
        if name in pending:
            gi = pending.pop(name)
            si = [pieces2[p][1] for p in groups2[gi]]
            sub = [(pieces2[p][0], row, 0) + pieces2[p][3:] for row, p in enumerate(groups2[gi])]
            wt[name] = _comm_wait([srcs2[s] for s in si], [lands2[gi]], sub, list(range(len(sub))), sems2[gi],
                                  after, "gather_wait_" + name)[0]
        return wt[name]

    sv = _forward(x[0], prepared["xb"], mem[0], loss_target[0], get_w, small, first_deps=(tok2,))

    sent = []

    def send(names, grads):
        srcs, shapes, pieces, idx = _scatter_plan(names, dict(zip(names, grads)), shard_shapes)
        lands = _place_own(srcs, shapes, pieces, "grads_own_%d" % len(sent))
        sems, srcs, lands, tok = _comm_start(srcs, lands, pieces, [list(range(len(pieces)))],
                                             "grads_start_%d" % len(sent))
        sent.append((names, srcs, lands, pieces, idx, sems[0]))
        return tok

    loss, grad_x, gs = _backward(sv, wt, small, send)

    ssrc = list(_pack_small_grads(gs, {k: w[k].shape for k in _SMALL_NAMES}, loss))
    sp = [("scatter", i, i, 0, 0, 0, a.shape) for i, a in enumerate(ssrc)]
    sshape = [jax.ShapeDtypeStruct((N_DEV,) + a.shape, F32) for a in ssrc]
    sl = _place_own(ssrc, sshape, sp, "small_own")
    ssem, ssrc, sl, _ = _comm_start(ssrc, sl, sp, [[0, 1]], "small_start")

    out_g, out_d, out_m, out_v = {}, {}, {}, {}
    after = (grad_x,)
    for n_sent, (names, srcs, lands, pieces, idx, sems) in enumerate(sent):
        lands = _comm_wait(srcs, lands, pieces, list(range(len(pieces))), sems, after, "grads_wait_%d" % n_sent)
        for k in names:
            axis = 1 if (k in _COL_FFN or k == "w_in") else 0
            if k in _COL_FFN:
                done = _adam_sharded([lands[i] for i in idx[k]], w[k][0].T, m[k][0].T, v[k][0].T, axis, "adam_" + k)
                res = [r.T for r in done]
            else:
                res = done = _adam_sharded([lands[i] for i in idx[k]], w[k][0], m[k][0], v[k][0], axis, "adam_" + k)
            out_g[k], out_d[k], out_m[k], out_v[k] = [r[None] for r in res]
        after = (done[3],)
    sl = _comm_wait(ssrc, sl, sp, [0, 1], ssem[0], after, "small_wait")
    small_out, loss_sum = _adam_small(sl[0], sl[1], w, m, v)
    for dst, res in zip((out_g, out_d, out_m, out_v), small_out):
        dst.update(res)
    loss_all = loss_sum[0, 0]
    return (loss_all, grad_x[None], *[out_g[k] for k in _WEIGHT_NAMES], *[out_d[k] for k in _WEIGHT_NAMES],
            *[out_m[k] for k in _WEIGHT_NAMES], *[out_v[k] for k in _WEIGHT_NAMES])
```

```python
import itertools

import jax
import jax.numpy as jnp
import numpy as np
from jax import lax
from jax.experimental import pallas as pl
from jax.experimental.pallas import tpu as pltpu

F32 = jnp.float32
BF16 = jnp.bfloat16

N_DEV = 8
ALPHA = 2.0 ** 0.25
LN_EPS = 1e-5
HG_HEADS = 4
HG_DIM = 128
SG_GROUPS = 4
SG_DIM = 128
SG_CHUNK = 128
X_HEADS = 4
HG_BLOCK = 16
HG_UNROLL = 16
ADAM_LR = 0.001
ADAM_B1 = 0.9
ADAM_B2 = 0.999
ADAM_EPS = 1e-08
ADAM_WD = 0.01
ADAM_STEP = 10
VMEM_LIMIT_V7X = 48 * 1024 * 1024
MXU_WIDTH_V7X = 256
LANES = 128
MESH_ID = pl.DeviceIdType.MESH
ANY = pl.BlockSpec(memory_space=pl.ANY)
HBM = pl.BlockSpec(memory_space=pltpu.HBM)
SEM = pl.BlockSpec(memory_space=pltpu.SEMAPHORE)
DATAFLOW = pltpu.SideEffectType.DATAFLOW_SIDE_EFFECTING


def _params(n_axes):
    return pltpu.CompilerParams(dimension_semantics=("arbitrary",) * n_axes, vmem_limit_bytes=VMEM_LIMIT_V7X)


def _dot(a, b):
    return jnp.dot(a, b, preferred_element_type=F32)


def _dot_nt(a, b):
    return lax.dot_general(a, b, (((1,), (1,)), ((), ())), preferred_element_type=F32)


def _dot_tn(a, b):
    return lax.dot_general(a, b, (((0,), (0,)), ((), ())), preferred_element_type=F32)


def _sigmoid(x):
    return 1.0 / (1.0 + jnp.exp(-x))


def _silu_and_grad(a):
    sig = _sigmoid(a)
    return a * sig, sig * (1.0 + a * (1.0 - sig))


_GELU_C = 0.7978845608028654


def _gelu_and_grad(x):
    inner = _GELU_C * (x + 0.044715 * x * x * x)
    t = jnp.tanh(inner)
    val = 0.5 * x * (1.0 + t)
    grad = 0.5 * (1.0 + t) + 0.5 * x * (1.0 - t * t) * _GELU_C * (1.0 + 3.0 * 0.044715 * x * x)
    return val, grad


def _ln_fwd(y, g, b):
    mu = jnp.mean(y, axis=-1, keepdims=True)
    yc = y - mu
    var = jnp.mean(yc * yc, axis=-1, keepdims=True)
    rstd = lax.rsqrt(var + LN_EPS)
    xhat = yc * rstd
    return xhat * g + b, xhat, rstd


def _ln_bwd(dh, xhat, rstd, g):
    dxh = dh * g
    m1 = jnp.mean(dxh, axis=-1, keepdims=True)
    m2 = jnp.mean(dxh * xhat, axis=-1, keepdims=True)
    dy = rstd * (dxh - m1 - xhat * m2)
    dg = jnp.sum(dh * xhat, axis=0, keepdims=True)
    db = jnp.sum(dh, axis=0, keepdims=True)
    return dy, dg, db


def _mask_dot(mask, x):
    hi = x.astype(BF16)
    lo = (x - hi.astype(F32)).astype(BF16)
    n = mask.shape[0]
    parts = [_dot(mask, hi[r:r + n, :]) + _dot(mask, lo[r:r + n, :]) for r in range(0, x.shape[0], n)]
    return parts[0] if len(parts) == 1 else jnp.concatenate(parts, axis=0)


def _block_masks(n):
    r = np.arange(n)[:, None]
    c = np.arange(n)[None, :]
    same = (r // HG_BLOCK) == (c // HG_BLOCK)
    return jnp.asarray(np.stack([same & (c <= r), same & (c >= r), same]), BF16)


def _row_tile(t):
    return min(t, 512)


def _col_tile(n):
    for cand in (512, 256, 128):
        if n % cand == 0:
            return cand
    return n


def _resident(w):
    return pl.BlockSpec(w.shape, lambda *_: (0, 0), pipeline_mode=pl.Buffered(1))


def _drop_deps(body, n_in, n_deps):
    if n_deps == 0:
        return body
    return lambda *refs: body(*refs[:n_in], *refs[n_in + n_deps:])


def _to_bf16(x, name, deps=()):
    t, d = x.shape
    tm = _row_tile(t)

    def body(x_ref, o_ref):
        o_ref[...] = x_ref[...].astype(BF16)

    row = pl.BlockSpec((tm, d), lambda i: (i, 0))
    return pl.pallas_call(
        _drop_deps(body, 1, len(deps)),
        grid=(t // tm,),
        in_specs=[row] + [ANY] * len(deps),
        out_specs=row,
        out_shape=jax.ShapeDtypeStruct((t, d), BF16),
        compiler_params=_params(1),
        name=name,
    )(x, *deps)


def _ffn_up(hb, wg, wu, name, deps=()):
    t, d = hb.shape
    f = wg.shape[1]
    tm = _row_tile(t)
    tn = _col_tile(f)

    def body(h_ref, wg_ref, wu_ref, a_ref, b_ref, s_ref):
        h = h_ref[...]
        for c in range(f // tn):
            cols = slice(c * tn, (c + 1) * tn)
            a = _dot(h, wg_ref[:, cols])
            b = _dot(h, wu_ref[:, cols])
            a_ref[:, cols] = a.astype(BF16)
            b_ref[:, cols] = b.astype(BF16)
            s_ref[:, cols] = (a * _sigmoid(a) * b).astype(BF16)

    act = pl.BlockSpec((tm, f), lambda i: (i, 0))
    return pl.pallas_call(
        _drop_deps(body, 3, len(deps)),
        grid=(t // tm,),
        in_specs=[pl.BlockSpec((tm, d), lambda i: (i, 0)), _resident(wg), _resident(wu)] + [ANY] * len(deps),
        out_specs=[act, act, act],
        out_shape=[jax.ShapeDtypeStruct((t, f), BF16)] * 3,
        compiler_params=_params(1),
        name=name,
    )(hb, wg, wu, *deps)


def _mm_res_ln(lhs, w, res, g, b, coef, name, target=None):
    t, kd = lhs.shape
    d = w.shape[1]
    tm = _row_tile(t)
    nt = t // tm
    from_norm = isinstance(res, tuple)
    n_res = 3 if from_norm else 1

    def body(*refs):
        l_ref, w_ref = refs[:2]
        r_refs = refs[2:2 + n_res]
        g_ref, b_ref = refs[2 + n_res:4 + n_res]
        rest = refs[4 + n_res:]
        prev = r_refs[0][...] * r_refs[1][...] + r_refs[2][...] if from_norm else r_refs[0][...]
        y = ALPHA * prev + coef * _dot(l_ref[...], w_ref[...])
        h, xhat, rstd = _ln_fwd(y, g_ref[...], b_ref[...])
        if target is None:
            hb_ref, xh_ref, rs_ref = rest
            hb_ref[...] = h.astype(BF16)
            xh_ref[...] = xhat
            rs_ref[...] = rstd
            return
        t_ref, loss_ref, dy_ref, dyb_ref, dg_ref, db_ref, lacc = rest
        i = pl.program_id(0)

        @pl.when(i == 0)
        def _():
            lacc[...] = jnp.zeros_like(lacc)
            dg_ref[...] = jnp.zeros_like(dg_ref)
            db_ref[...] = jnp.zeros_like(db_ref)

        err = h - t_ref[...]
        lacc[...] += jnp.sum(err * err, axis=0, keepdims=True)
        dy, dg, db = _ln_bwd(err * (1.0 / d), xhat, rstd, g_ref[...])
        dy_ref[...] = dy
        dyb_ref[...] = dy.astype(BF16)
        dg_ref[...] += dg
        db_ref[...] += db

        @pl.when(i == nt - 1)
        def _():
            loss_ref[...] = jnp.zeros_like(loss_ref) + jnp.sum(lacc[...], axis=1, keepdims=True) * (0.5 / d)

    row = pl.BlockSpec((tm, d), lambda i: (i, 0))
    vec = pl.BlockSpec((1, d), lambda i: (0, 0))
    res_specs = [row, vec, vec] if from_norm else [row]
    res_args = list(res) if from_norm else [res]
    in_specs = [pl.BlockSpec((tm, kd), lambda i: (i, 0)), _resident(w)] + res_specs + [vec, vec]
    args = [lhs, w] + res_args + [g, b]
    if target is None:
        out_specs = [row, row, pl.BlockSpec((tm, 1), lambda i: (i, 0))]
        out_shape = [jax.ShapeDtypeStruct((t, d), BF16), jax.ShapeDtypeStruct((t, d), F32), pltpu.HBM((t, 1), F32)]
        scratch = []
    else:
        in_specs.append(row)
        args.append(target)
        out_specs = [pl.BlockSpec((1, LANES), lambda i: (0, 0)), row, row, vec, vec]
        out_shape = [jax.ShapeDtypeStruct((1, LANES), F32), jax.ShapeDtypeStruct((t, d), F32),
                     jax.ShapeDtypeStruct((t, d), BF16), jax.ShapeDtypeStruct((1, d), F32),
                     jax.ShapeDtypeStruct((1, d), F32)]
        scratch = [pltpu.VMEM((1, d), F32)]
    return pl.pallas_call(
        body,
        grid=(nt,),
        in_specs=in_specs,
        out_specs=out_specs,
        out_shape=out_shape,
        scratch_shapes=scratch,
        compiler_params=_params(1),
        name=name,
    )(*args)


def _store_slabs(o_ref, first, tile):
    for s in range(tile.shape[1] // LANES):
        o_ref[first + s] = tile[:, s * LANES:(s + 1) * LANES]


def _mm_nn(lhs, w, name):
    t, kd = lhs.shape
    n = w.shape[1]
    tm = _row_tile(t)
    tn = _col_tile(n)

    def body(l_ref, w_ref, o_ref):
        lhs_v = l_ref[...]
        for c in range(n // tn):
            _store_slabs(o_ref, c * (tn // LANES), _dot(lhs_v, w_ref[:, c * tn:(c + 1) * tn]))

    return pl.pallas_call(
        body,
        grid=(t // tm,),
        in_specs=[pl.BlockSpec((tm, kd), lambda i: (i, 0)), _resident(w)],
        out_specs=pl.BlockSpec((n // LANES, tm, LANES), lambda i: (0, i, 0)),
        out_shape=jax.ShapeDtypeStruct((n // LANES, t, LANES), F32),
        compiler_params=_params(1),
        name=name,
    )(lhs, w)


def _lower_bound(lg):
    m = jnp.max(lg, axis=0, keepdims=True)
    e = jnp.exp(lg - m)
    return e[0:1, :] / jnp.sum(e, axis=0, keepdims=True)


def _forget_terms(fz, lb):
    e = jnp.exp(-jnp.abs(fz))
    r = 1.0 / (1.0 + e)
    pos = fz >= 0.0
    sig = jnp.where(pos, r, e * r)
    nsig = jnp.where(pos, e * r, r)
    f = lb + (1.0 - lb) * sig
    k = (1.0 - lb) * nsig
    return sig, nsig, f, k


def _hg_tile(t):
    return min(t, 1024)


HG_HALF = HG_BLOCK // 2
NEG_BIG = -1e30


def _halves(a):
    return a[:HG_HALF, :], a[HG_HALF:, :]


def _causal_halves(s):
    return (0, 1) if s < HG_HALF else (1,)


def _decay_from(b_half, b_s, s, h, tidx):
    first = s - h * HG_HALF
    diff = b_half - b_s
    if first > 0:
        diff = jnp.where(tidx >= first, diff, NEG_BIG)
    return jnp.exp(diff)


def _hgrn_fwd(proj, logits, gn):
    t = proj.shape[1]
    ct = _hg_tile(t)
    nct = t // ct
    nblk = ct // HG_BLOCK
    nh = HG_HEADS
    mrows = min(ct, 256)

    def body(q_ref, fz_ref, iv_ref, gg_ref, lg_ref, gn_ref, mask_ref, oraw_ref, oa_ref, st_ref,
             state, qt_s, kt_s, k_s, b_s, dec_s):
        c = pl.program_id(1)

        @pl.when(c == 0)
        def _():
            state[...] = jnp.zeros_like(state)

        lb = _lower_bound(lg_ref[...])
        q = q_ref[...]
        _, _, f, k = _forget_terms(fz_ref[...], lb)
        logf = jnp.log(f)
        b = _mask_dot(mask_ref[0], logf)
        bend = _mask_dot(mask_ref[2], logf)
        qt_s[...] = (q * jnp.exp(b)).astype(BF16)
        kt_s[...] = (k * jnp.exp(bend - b)).astype(BF16)
        k_s[...] = k
        b_s[...] = b
        dec_s[...] = jnp.exp(bend)
        tidx = lax.broadcasted_iota(jnp.int32, (HG_HALF, HG_DIM), 0)

        def blk(i, carry):
            r0 = pl.multiple_of(i * HG_BLOCK, HG_BLOCK)
            rows = pl.ds(r0, HG_BLOCK)
            st = state[...]
            stb = st.astype(BF16)
            st_ref[i] = stb
            v = iv_ref[rows, :]
            qq = q_ref[rows, :]
            kk = k_s[rows, :]
            bb = b_s[rows, :]
            o = list(_halves(_dot_nt(qt_s[rows, :], stb)))
            qh, bh = _halves(qq), _halves(bb)
            for s in range(HG_BLOCK):
                ks, vs = kk[s:s + 1, :], v[s:s + 1, :]
                for h in _causal_halves(s):
                    e = _decay_from(bh[h], bb[s:s + 1, :], s, h, tidx)
                    acol = jnp.sum(qh[h] * (ks * e), axis=1, keepdims=True)
                    o[h] = o[h] + acol * vs
            oraw_ref[rows, :] = jnp.concatenate(o, axis=0)
            state[...] = st * dec_s[pl.ds(r0, 1), :] + _dot_tn(v.astype(BF16), kt_s[rows, :])
            return carry

        lax.fori_loop(0, nblk, blk, 0, unroll=HG_UNROLL)
        oraw = oraw_ref[...]
        r = lax.rsqrt(jnp.mean(oraw * oraw, axis=-1, keepdims=True) + LN_EPS)
        gg = gg_ref[...]
        oa_ref[...] = (oraw * r * gn_ref[...] * gg * _sigmoid(gg)).astype(BF16)

    def slab(off):
        return pl.BlockSpec((None, ct, HG_DIM), lambda h, c: (off + h, c, 0))

    return pl.pallas_call(
        body,
        grid=(nh, nct),
        in_specs=[slab(0), slab(nh), slab(2 * nh), slab(3 * nh),
                  pl.BlockSpec((None, 2, HG_DIM), lambda h, c: (h, 0, 0)),
                  pl.BlockSpec((1, HG_DIM), lambda h, c: (0, 0)),
                  pl.BlockSpec((3, mrows, mrows), lambda h, c: (0, 0, 0))],
        out_specs=[slab(0), pl.BlockSpec((ct, HG_DIM), lambda h, c: (c, h)),
                   pl.BlockSpec((None, nblk, HG_DIM, HG_DIM), lambda h, c: (h, c, 0, 0))],
        out_shape=[jax.ShapeDtypeStruct((nh, t, HG_DIM), F32),
                   jax.ShapeDtypeStruct((t, (nh + SG_GROUPS) * HG_DIM), BF16),
                   jax.ShapeDtypeStruct((nh, t // HG_BLOCK, HG_DIM, HG_DIM), BF16)],
        scratch_shapes=[pltpu.VMEM((HG_DIM, HG_DIM), F32), pltpu.VMEM((ct, HG_DIM), BF16),
                        pltpu.VMEM((ct, HG_DIM), BF16), pltpu.VMEM((ct, HG_DIM), F32),
                        pltpu.VMEM((ct, HG_DIM), F32), pltpu.VMEM((ct, HG_DIM), F32)],
        compiler_params=_params(2),
        name="hgrn_fwd",
    )(proj, proj, proj, proj, logits, gn, _block_masks(mrows))


def _sg_tile(t):
    return min(t, 512)


def _sgu_chunk_fwd(u, v, ln_g, ln_b, wm, bs):
    ua, dua = _gelu_and_grad(u)
    va, dva = _gelu_and_grad(v)
    vn, xhat, rstd = _ln_fwd(va, ln_g, ln_b)
    s = _dot(wm, vn.astype(BF16)) + bs
    return ua, dua, dva, vn, xhat, rstd, s


def _tril_weight(w):
    n = SG_CHUNK
    r = lax.broadcasted_iota(jnp.int32, (n, n), 0)
    c = lax.broadcasted_iota(jnp.int32, (n, n), 1)
    return jnp.where(c <= r, w, 0.0)


def _bias_by_position(b_row):
    return jnp.broadcast_to(b_row, (SG_CHUNK, SG_CHUNK)).T


def _sgu_fwd(proj, mix, ln_g, ln_b, w_s, b_row):
    t = proj.shape[1]
    ct = _sg_tile(t)
    ng = SG_GROUPS
    wide = ng * SG_DIM
    blk_u = 4 * HG_HEADS // ng

    def body(u_ref, v_ref, g_ref, b_ref, w_ref, bs_ref, mix_ref, o_ref):
        del mix_ref
        for g in range(ng):
            lanes = slice(g * SG_DIM, (g + 1) * SG_DIM)
            wm = _tril_weight(w_ref[g]).astype(BF16)
            bs = _bias_by_position(bs_ref[g])
            for n in range(ct // SG_CHUNK):
                rows = slice(n * SG_CHUNK, (n + 1) * SG_CHUNK)
                ua, _, _, _, _, _, s = _sgu_chunk_fwd(u_ref[g, rows, :], v_ref[g, rows, :], g_ref[g], b_ref[g], wm, bs)
                o_ref[rows, lanes] = (ua * s).astype(BF16)

    full = lambda a: pl.BlockSpec(a.shape, lambda c: (0,) * a.ndim)
    return pl.pallas_call(
        body,
        grid=(t // ct,),
        in_specs=[pl.BlockSpec((ng, ct, SG_DIM), lambda c: (blk_u, c, 0)),
                  pl.BlockSpec((ng, ct, SG_DIM), lambda c: (blk_u + 1, c, 0)),
                  full(ln_g), full(ln_b), full(w_s), full(b_row), ANY],
        out_specs=pl.BlockSpec((ct, wide), lambda c: (c, 1)),
        out_shape=jax.ShapeDtypeStruct(mix.shape, mix.dtype),
        input_output_aliases={6: 0},
        compiler_params=_params(1),
        name="sgu_fwd",
    )(proj, proj, ln_g, ln_b, w_s, b_row, mix)


def _mem_kv(mem, g, b, wk, wv):
    m_len, d = mem.shape

    def body(m_ref, g_ref, b_ref, wk_ref, wv_ref, mb_ref, xh_ref, rs_ref, k_ref, v_ref):
        m, xhat, rstd = _ln_fwd(m_ref[...], g_ref[...], b_ref[...])
        mb = m.astype(BF16)
        mb_ref[...] = mb
        xh_ref[...] = xhat
        rs_ref[...] = rstd
        k_ref[...] = _dot(mb, wk_ref[...]).astype(BF16)
        v_ref[...] = _dot(mb, wv_ref[...]).astype(BF16)

    return pl.pallas_call(
        body,
        out_shape=[jax.ShapeDtypeStruct((m_len, d), BF16), jax.ShapeDtypeStruct((m_len, d), F32),
                   jax.ShapeDtypeStruct((m_len, 1), F32), jax.ShapeDtypeStruct((m_len, d), BF16),
                   jax.ShapeDtypeStruct((m_len, d), BF16)],
        compiler_params=pltpu.CompilerParams(vmem_limit_bytes=VMEM_LIMIT_V7X),
        name="mem_kv",
    )(mem, g, b, wk, wv)


def _softmax_rows(s):
    m = jnp.max(s, axis=-1, keepdims=True)
    p = jnp.exp(s - m)
    return p / jnp.sum(p, axis=-1, keepdims=True)


def _attn_fwd(hb, wq, kb, vb):
    t, d = hb.shape
    tm = _row_tile(t)
    dh = d // X_HEADS
    scale = dh ** -0.5

    def body(h_ref, wq_ref, k_ref, v_ref, q_ref, o_ref):
        q = _dot(h_ref[...], wq_ref[...]).astype(BF16)
        q_ref[...] = q
        for hd in range(X_HEADS):
            sl = slice(hd * dh, (hd + 1) * dh)
            p = _softmax_rows(_dot_nt(q[:, sl], k_ref[:, sl]) * scale)
            o_ref[:, sl] = _dot(p.astype(BF16), v_ref[:, sl]).astype(BF16)

    row = pl.BlockSpec((tm, d), lambda i: (i, 0))
    full = lambda a: pl.BlockSpec(a.shape, lambda i: (0, 0))
    return pl.pallas_call(
        body,
        grid=(t // tm,),
        in_specs=[row, full(wq), full(kb), full(vb)],
        out_specs=[row, row],
        out_shape=[jax.ShapeDtypeStruct((t, d), BF16), jax.ShapeDtypeStruct((t, d), BF16)],
        compiler_params=_params(1),
        name="attn_fwd",
    )(hb, wq, kb, vb)


def _ffn_bwd_act(dyb, wd, a, b, coef, name, deps=()):
    t, d = dyb.shape
    f = wd.shape[0]
    tm = _row_tile(t)
    tn = _col_tile(f)

    def body(dy_ref, wd_ref, a_ref, b_ref, da_ref, db_ref):
        dy = dy_ref[...]
        for c in range(f // tn):
            cols = slice(c * tn, (c + 1) * tn)
            ds = _dot_nt(dy, wd_ref[cols, :]) * coef
            silu, dsilu = _silu_and_grad(a_ref[:, cols].astype(F32))
            da_ref[:, cols] = (ds * b_ref[:, cols].astype(F32) * dsilu).astype(BF16)
            db_ref[:, cols] = (ds * silu).astype(BF16)

    act = pl.BlockSpec((tm, f), lambda i: (i, 0))
    return pl.pallas_call(
        _drop_deps(body, 4, len(deps)),
        grid=(t // tm,),
        in_specs=[pl.BlockSpec((tm, d), lambda i: (i, 0)), _resident(wd), act, act] + [ANY] * len(deps),
        out_specs=[act, act],
        out_shape=[jax.ShapeDtypeStruct((t, f), BF16), jax.ShapeDtypeStruct((t, f), BF16)],
        compiler_params=_params(1),
        name=name,
    )(dyb, wd, a, b, *deps)


def _ffn_bwd_fused(dy, dyb, wd, wg, wu, a, b, coef, ln, name):
    t, d = dy.shape
    f = wd.shape[0]
    tm = min(t, 256)
    tn = _col_tile(f)

    def body(dy_ref, dyb_ref, wd_ref, wg_ref, wu_ref, a_ref, b_ref, xh_ref, rs_ref, g_ref,
             da_ref, db_ref, dyo_ref, dyob_ref, dg_ref, dbl_ref):
        dyb_v = dyb_ref[...]
        dh = ALPHA * dy_ref[...]
        for c in range(f // tn):
            cols = slice(c * tn, (c + 1) * tn)
            ds = _dot_nt(dyb_v, wd_ref[cols, :]) * coef
            silu, dsilu = _silu_and_grad(a_ref[:, cols].astype(F32))
            da = (ds * b_ref[:, cols].astype(F32) * dsilu).astype(BF16)
            db = (ds * silu).astype(BF16)
            da_ref[:, cols] = da
            db_ref[:, cols] = db
            dh = dh + _dot_nt(da, wg_ref[:, cols]) + _dot_nt(db, wu_ref[:, cols])

        @pl.when(pl.program_id(0) == 0)
        def _():
            dg_ref[...] = jnp.zeros_like(dg_ref)
            dbl_ref[...] = jnp.zeros_like(dbl_ref)

        dyp, dg, dbl = _ln_bwd(dh, xh_ref[...], rs_ref[...], g_ref[...])
        dyo_ref[...] = dyp
        dyob_ref[...] = dyp.astype(BF16)
        dg_ref[...] += dg
        dbl_ref[...] += dbl

    row = pl.BlockSpec((tm, d), lambda i: (i, 0))
    act = pl.BlockSpec((tm, f), lambda i: (i, 0))
    vec = pl.BlockSpec((1, d), lambda i: (0, 0))
    return pl.pallas_call(
        body,
        grid=(t // tm,),
        in_specs=[row, row, _resident(wd), _resident(wg), _resident(wu), act, act, row,
                  pl.BlockSpec((tm, 1), lambda i: (i, 0)), vec],
        out_specs=[act, act, row, row, vec, vec],
        out_shape=[jax.ShapeDtypeStruct((t, f), BF16), jax.ShapeDtypeStruct((t, f), BF16),
                   jax.ShapeDtypeStruct((t, d), F32), jax.ShapeDtypeStruct((t, d), BF16),
                   jax.ShapeDtypeStruct((1, d), F32), jax.ShapeDtypeStruct((1, d), F32)],
        compiler_params=_params(1),
        name=name,
    )(dy, dyb, wd, wg, wu, a, b, *ln)


def _mm_tn(a, b, name, scale=1.0, deps=()):
    t, m = a.shape
    bs = list(b) if isinstance(b, (list, tuple)) else [b]
    n = sum(piece.shape[1] for piece in bs)
    tt = _row_tile(t)
    nt = t // tt
    tm_o, tn_o = m, n

    def body(a_ref, *refs):
        b_refs, (o_ref, acc) = refs[:len(bs)], refs[len(bs):]
        k = pl.program_id(2)

        @pl.when(k == 0)
        def _():
            acc[...] = jnp.zeros_like(acc)

        first = 0
        for b_ref in b_refs:
            cols = slice(first, first + b_ref.shape[1])
            acc[:, cols] += _dot_tn(a_ref[...], b_ref[...])
            first = cols.stop

        @pl.when(k == nt - 1)
        def _():
            o_ref[...] = (acc[...] * scale).astype(BF16)

    return pl.pallas_call(
        _drop_deps(body, 1 + len(bs), len(deps)),
        grid=(m // tm_o, n // tn_o, nt),
        in_specs=[pl.BlockSpec((tt, tm_o), lambda i, j, k: (k, i))]
        + [pl.BlockSpec((tt, piece.shape[1]), lambda i, j, k: (k, j)) for piece in bs] + [ANY] * len(deps),
        out_specs=pl.BlockSpec((tm_o, tn_o), lambda i, j, k: (i, j)),
        out_shape=pltpu.HBM((m, n), BF16),
        scratch_shapes=[pltpu.VMEM((tm_o, tn_o), F32)],
        compiler_params=_params(3),
        name=name,
    )(a, *bs, *deps)


def _mm_nt(lhs, w, name):
    t, d = lhs.shape
    kd = w.shape[0]
    tm = _row_tile(t)

    def body(l_ref, w_ref, o_ref):
        _store_slabs(o_ref, 0, _dot_nt(l_ref[...], w_ref[...]))

    return pl.pallas_call(
        body,
        grid=(t // tm,),
        in_specs=[pl.BlockSpec((tm, d), lambda i: (i, 0)), _resident(w)],
        out_specs=pl.BlockSpec((kd // LANES, tm, LANES), lambda i: (0, i, 0)),
        out_shape=jax.ShapeDtypeStruct((kd // LANES, t, LANES), F32),
        compiler_params=_params(1),
        name=name,
    )(lhs, w)


def _dx_ln(dy, pairs, ln, name, deps=()):
    t, d = dy.shape
    npair = len(pairs)
    pairs = [(list(lhs) if isinstance(lhs, (list, tuple)) else [lhs], w) for lhs, w in pairs]
    tm = min(t, 512 // npair)
    nt = t // tm
    n_in = 1 + sum(len(pieces) + 1 for pieces, _ in pairs) + (3 if ln is not None else 0)

    def body(*refs):
        dy_ref = refs[0]
        pos = 1
        dh = ALPHA * dy_ref[...]
        for pieces, _ in pairs:
            w_ref = refs[pos + len(pieces)]
            first = 0
            for l_ref in refs[pos:pos + len(pieces)]:
                cols = slice(first, first + l_ref.shape[1])
                dh = dh + _dot_nt(l_ref[...], w_ref[:, cols])
                first = cols.stop
            pos += len(pieces) + 1
        if ln is not None:
            xh_ref, rs_ref, g_ref = refs[pos:pos + 3]
            dyo_ref, dyb_ref, dg_ref, db_ref = refs[pos + 3:pos + 7]

            @pl.when(pl.program_id(0) == 0)
            def _():
                dg_ref[...] = jnp.zeros_like(dg_ref)
                db_ref[...] = jnp.zeros_like(db_ref)

            dyp, dg, db = _ln_bwd(dh, xh_ref[...], rs_ref[...], g_ref[...])
            dyo_ref[...] = dyp
            dyb_ref[...] = dyp.astype(BF16)
            dg_ref[...] += dg
            db_ref[...] += db
        else:
            refs[pos][...] = dh

    row = pl.BlockSpec((tm, d), lambda i: (i, 0))
    vec = pl.BlockSpec((1, d), lambda i: (0, 0))
    in_specs = [row]
    args = [dy]
    for pieces, w in pairs:
        in_specs += [pl.BlockSpec((tm, piece.shape[1]), lambda i: (i, 0)) for piece in pieces] + [_resident(w)]
        args += pieces + [w]
    if ln is not None:
        in_specs += [row, pl.BlockSpec((tm, 1), lambda i: (i, 0)), vec]
        args += list(ln)
        out_specs = [row, row, vec, vec]
        out_shape = [jax.ShapeDtypeStruct((t, d), F32), jax.ShapeDtypeStruct((t, d), BF16),
                     jax.ShapeDtypeStruct((1, d), F32), jax.ShapeDtypeStruct((1, d), F32)]
    else:
        out_specs = row
        out_shape = jax.ShapeDtypeStruct((t, d), F32)
    return pl.pallas_call(
        _drop_deps(body, n_in, len(deps)),
        grid=(nt,),
        in_specs=in_specs + [ANY] * len(deps),
        out_specs=out_specs,
        out_shape=out_shape,
        compiler_params=_params(1),
        name=name,
    )(*args, *deps)


def _hgrn_bwd(proj, oraw, dmix, states, logits, gn):
    t = proj.shape[1]
    ct = _hg_tile(t)
    nct = t // ct
    nblk = ct // HG_BLOCK
    nh = HG_HEADS
    mrows = min(ct, 256)

    def body(q_ref, fz_ref, iv_ref, gg_ref, or_ref, do_ref, st_ref, lg_ref, gn_ref, mask_ref,
             dq_ref, dfz_ref, div_ref, dgg_ref, dlg_ref, dgn_ref,
             dstate, qt_s, kt_s, k_s, b_s, eb_s, ekb_s, dec_s, dor_s, dbl_s, gr_s, dk_s, dlb_acc):
        c = pl.program_id(1)

        @pl.when(c == 0)
        def _():
            dstate[...] = jnp.zeros_like(dstate)
            dlb_acc[...] = jnp.zeros_like(dlb_acc)
            dgn_ref[...] = jnp.zeros_like(dgn_ref)

        lb = _lower_bound(lg_ref[...])
        q = q_ref[...]
        sig, nsig, f, k = _forget_terms(fz_ref[...], lb)
        logf = jnp.log(f)
        b = _mask_dot(mask_ref[0], logf)
        bend = _mask_dot(mask_ref[2], logf)
        eb = jnp.exp(b)
        ekb = jnp.exp(bend - b)
        qt_s[...] = (q * eb).astype(BF16)
        kt_s[...] = (k * ekb).astype(BF16)
        k_s[...] = k
        b_s[...] = b
        eb_s[...] = eb
        ekb_s[...] = ekb
        dec_s[...] = jnp.exp(bend)
        oraw = or_ref[...]
        r = lax.rsqrt(jnp.mean(oraw * oraw, axis=-1, keepdims=True) + LN_EPS)
        on = oraw * r
        gg = gg_ref[...]
        silu, dsilu = _silu_and_grad(gg)
        doa = do_ref[...]
        gnv = gn_ref[...]
        dgg_ref[...] = (doa * on * gnv * dsilu).astype(BF16)
        dyn = doa * silu
        dgn_ref[...] += jnp.sum(dyn * on, axis=0, keepdims=True)
        don = dyn * gnv
        dor_s[...] = r * (don - on * jnp.mean(don * on, axis=-1, keepdims=True))
        tidx = lax.broadcasted_iota(jnp.int32, (HG_HALF, HG_DIM), 0)

        def blk(ii, carry):
            i = nblk - 1 - ii
            r0 = pl.multiple_of(i * HG_BLOCK, HG_BLOCK)
            rows = pl.ds(r0, HG_BLOCK)
            st = st_ref[i]
            dst = dstate[...]
            dstb = dst.astype(BF16)
            do = dor_s[rows, :]
            dob = do.astype(BF16)
            v = iv_ref[rows, :]
            vb = v.astype(BF16)
            qq = q_ref[rows, :]
            kk = k_s[rows, :]
            bb = b_s[rows, :]
            qt = qt_s[rows, :]
            kt = kt_s[rows, :]
            dec = dec_s[pl.ds(r0, 1), :]
            dkt = _dot(vb, dstb)
            dq = _dot(dob, st) * eb_s[rows, :]
            dk = dkt * ekb_s[rows, :]
            dv = _dot_nt(kt, dstb)
            gend = (jnp.sum(kk * dk, axis=0, keepdims=True)
                    + dec * jnp.sum(dst * st.astype(F32), axis=0, keepdims=True))
            qh, bh, doh = _halves(qq), _halves(bb), _halves(do)
            dqh, dkh, dvh = list(_halves(dq)), list(_halves(dk)), list(_halves(dv))
            for s in range(HG_BLOCK):
                ks, vs = kk[s:s + 1, :], v[s:s + 1, :]
                dk_part = dv_part = None
                for h in _causal_halves(s):
                    e = _decay_from(bh[h], bb[s:s + 1, :], s, h, tidx)
                    ke = ks * e
                    acol = jnp.sum(qh[h] * ke, axis=1, keepdims=True)
                    dacol = jnp.sum(doh[h] * vs, axis=1, keepdims=True)
                    dqh[h] = dqh[h] + dacol * ke
                    pk = dacol * (qh[h] * e)
                    pv = acol * doh[h]
                    dk_part = pk if dk_part is None else dk_part + pk
                    dv_part = pv if dv_part is None else dv_part + pv
                hs, row = divmod(s, HG_HALF)
                dkh[hs] = dkh[hs] + jnp.where(tidx == row, jnp.sum(dk_part, axis=0, keepdims=True), 0.0)
                dvh[hs] = dvh[hs] + jnp.where(tidx == row, jnp.sum(dv_part, axis=0, keepdims=True), 0.0)
            dq = jnp.concatenate(dqh, axis=0)
            dk = jnp.concatenate(dkh, axis=0)
            dv = jnp.concatenate(dvh, axis=0)
            dq_ref[rows, :] = dq.astype(BF16)
            div_ref[rows, :] = dv.astype(BF16)
            dk_s[rows, :] = dk
            dbl_s[rows, :] = qq * dq - kk * dk
            gr_s[rows, :] = jnp.zeros((HG_BLOCK, HG_DIM), F32) + gend
            dstate[...] = dst * dec + _dot_tn(dob, qt)
            return carry

        lax.fori_loop(0, nblk, blk, 0, unroll=HG_UNROLL)
        dlogf = _mask_dot(mask_ref[1], dbl_s[...]) + gr_s[...]
        dk = dk_s[...]
        dfz_ref[...] = ((dlogf / f - dk) * ((1.0 - lb) * sig * nsig)).astype(BF16)
        dlb_acc[...] += jnp.sum((dlogf / f - dk) * nsig, axis=0, keepdims=True)

        @pl.when(c == nct - 1)
        def _():
            dl0 = dlb_acc[...] * lb * (1.0 - lb)
            layer = lax.broadcasted_iota(jnp.int32, (2, HG_DIM), 0)
            dlg_ref[...] = jnp.where(layer == 0, dl0, -dl0)

    def slab(off):
        return pl.BlockSpec((None, ct, HG_DIM), lambda h, c: (off + h, nct - 1 - c, 0))

    out_slab = pl.BlockSpec((ct, HG_DIM), lambda h, c: (nct - 1 - c, h))
    tile_f32 = pltpu.VMEM((ct, HG_DIM), F32)
    tile_b16 = pltpu.VMEM((ct, HG_DIM), BF16)
    slab_shape = pltpu.HBM((t, nh * HG_DIM), BF16)
    return pl.pallas_call(
        body,
        grid=(nh, nct),
        in_specs=[slab(0), slab(nh), slab(2 * nh), slab(3 * nh), slab(0), slab(0),
                  pl.BlockSpec((None, nblk, HG_DIM, HG_DIM), lambda h, c: (h, nct - 1 - c, 0, 0)),
                  pl.BlockSpec((None, 2, HG_DIM), lambda h, c: (h, 0, 0)),
                  pl.BlockSpec((1, HG_DIM), lambda h, c: (0, 0)),
                  pl.BlockSpec((3, mrows, mrows), lambda h, c: (0, 0, 0))],
        out_specs=[out_slab, out_slab, out_slab, out_slab,
                   pl.BlockSpec((None, 2, HG_DIM), lambda h, c: (h, 0, 0)),
                   pl.BlockSpec((None, 1, HG_DIM), lambda h, c: (h, 0, 0))],
        out_shape=[slab_shape, slab_shape, slab_shape, slab_shape,
                   jax.ShapeDtypeStruct((nh, 2, HG_DIM), F32), jax.ShapeDtypeStruct((nh, 1, HG_DIM), F32)],
        scratch_shapes=[pltpu.VMEM((HG_DIM, HG_DIM), F32), tile_b16, tile_b16, tile_f32, tile_f32, tile_f32, tile_f32,
                        tile_f32, tile_f32, tile_f32, tile_f32, tile_f32, pltpu.VMEM((1, HG_DIM), F32)],
        compiler_params=_params(2),
        name="hgrn_bwd",
    )(proj, proj, proj, proj, oraw, dmix, states, logits, gn, _block_masks(mrows))


def _sgu_bwd(proj, dmix, ln_g, ln_b, w_s, b_row):
    t = proj.shape[1]
    ct = _sg_tile(t)
    nct = t // ct
    ng = SG_GROUPS
    off_u = 4 * HG_HEADS
    off_v = off_u + ng
    n = SG_CHUNK

    def body(u_ref, v_ref, do_ref, g_ref, b_ref, w_ref, bs_ref, du_ref, dv_ref, dg_ref, db_ref, dw_ref, dbs_ref):
        c = pl.program_id(1)

        @pl.when(c == 0)
        def _():
            dg_ref[...] = jnp.zeros_like(dg_ref)
            db_ref[...] = jnp.zeros_like(db_ref)
            dw_ref[...] = jnp.zeros_like(dw_ref)
            dbs_ref[...] = jnp.zeros_like(dbs_ref)

        r = lax.broadcasted_iota(jnp.int32, (n, n), 0)
        cc = lax.broadcasted_iota(jnp.int32, (n, n), 1)
        wm = jnp.where(cc <= r, w_ref[...], 0.0).astype(BF16)
        wmt = jnp.where(r <= cc, w_ref[...].T, 0.0).astype(BF16)
        bs = _bias_by_position(bs_ref[...])
        for ci in range(ct // n):
            rows = slice(ci * n, (ci + 1) * n)
            ua, dua, dva, vn, xhat, rstd, s = _sgu_chunk_fwd(u_ref[rows, :], v_ref[rows, :], g_ref[...], b_ref[...],
                                                             wm, bs)
            do = do_ref[rows, :]
            du_ref[rows, :] = (do * s * dua).astype(BF16)
            ds = do * ua
            dsb = ds.astype(BF16)
            dbs_ref[...] += jnp.sum(ds, axis=1, keepdims=True)
            dw_ref[...] += _dot_nt(dsb, vn.astype(BF16))
            dvn = _dot(wmt, dsb)
            dva_in, dg, db = _ln_bwd(dvn, xhat, rstd, g_ref[...])
            dg_ref[...] += dg
            db_ref[...] += db
            dv_ref[rows, :] = (dva_in * dva).astype(BF16)

        @pl.when(c == nct - 1)
        def _():
            dw_ref[...] = jnp.where(cc <= r, dw_ref[...], 0.0)

    vec = pl.BlockSpec((None, 1, SG_DIM), lambda g, c: (g, 0, 0))
    mat = pl.BlockSpec((None, n, n), lambda g, c: (g, 0, 0))
    col = pl.BlockSpec((None, n, 1), lambda g, c: (g, 0, 0))
    out_slab = pl.BlockSpec((ct, SG_DIM), lambda g, c: (c, g))
    return pl.pallas_call(
        body,
        grid=(ng, nct),
        in_specs=[pl.BlockSpec((None, ct, SG_DIM), lambda g, c: (off_u + g, c, 0)),
                  pl.BlockSpec((None, ct, SG_DIM), lambda g, c: (off_v + g, c, 0)),
                  pl.BlockSpec((None, ct, SG_DIM), lambda g, c: (ng + g, c, 0)), vec, vec, mat, vec],
        out_specs=[out_slab, out_slab, vec, vec, mat, col],
        out_shape=[pltpu.HBM((t, ng * SG_DIM), BF16), pltpu.HBM((t, ng * SG_DIM), BF16),
                   jax.ShapeDtypeStruct((ng, 1, SG_DIM), F32), jax.ShapeDtypeStruct((ng, 1, SG_DIM), F32),
                   jax.ShapeDtypeStruct((ng, n, n), F32), jax.ShapeDtypeStruct((ng, n, 1), F32)],
        compiler_params=_params(2),
        name="sgu_bwd",
    )(proj, proj, dmix, ln_g, ln_b, w_s, b_row)


def _attn_bwd(dyb, wo, qb, kb, vb):
    t, d = dyb.shape
    m_len = kb.shape[0]
    tm = _row_tile(t)
    dh = d // X_HEADS
    scale = dh ** -0.5

    def body(dy_ref, wo_ref, q_ref, k_ref, v_ref, dq_ref, dk_ref, dv_ref):
        i = pl.program_id(0)

        @pl.when(i == 0)
        def _():
            dk_ref[...] = jnp.zeros_like(dk_ref)
            dv_ref[...] = jnp.zeros_like(dv_ref)

        do = _dot_nt(dy_ref[...], wo_ref[...]).astype(BF16)
        for hd in range(X_HEADS):
            sl = slice(hd * dh, (hd + 1) * dh)
            qh = q_ref[:, sl]
            p = _softmax_rows(_dot_nt(qh, k_ref[:, sl]) * scale)
            doh = do[:, sl]
            dp = _dot_nt(doh, v_ref[:, sl])
            ds = (p * (dp - jnp.sum(dp * p, axis=-1, keepdims=True)) * scale).astype(BF16)
            dq_ref[:, sl] = _dot(ds, k_ref[:, sl]).astype(BF16)
            dk_ref[:, sl] += _dot_tn(ds, qh)
            dv_ref[:, sl] += _dot_tn(p.astype(BF16), doh)

    row = pl.BlockSpec((tm, d), lambda i: (i, 0))
    full = lambda a: pl.BlockSpec(a.shape, lambda i: (0, 0))
    kv = pl.BlockSpec((m_len, d), lambda i: (0, 0))
    return pl.pallas_call(
        body,
        grid=(t // tm,),
        in_specs=[row, full(wo), row, full(kb), full(vb)],
        out_specs=[row, kv, kv],
        out_shape=[jax.ShapeDtypeStruct((t, d), BF16), jax.ShapeDtypeStruct((m_len, d), F32),
                   jax.ShapeDtypeStruct((m_len, d), F32)],
        compiler_params=_params(1),
        name="attn_bwd",
    )(dyb, wo, qb, kb, vb)


def _mem_bwd(dk, dv, mb, xhat, rstd, g, wk, wv):
    m_len, d = dk.shape

    def body(dk_ref, dv_ref, mb_ref, xh_ref, rs_ref, g_ref, wk_ref, wv_ref, gwk_ref, gwv_ref, dg_ref, db_ref):
        dkb = dk_ref[...].astype(BF16)
        dvb = dv_ref[...].astype(BF16)
        mb_v = mb_ref[...]
        gwk_ref[...] = _dot_tn(mb_v, dkb).astype(BF16)
        gwv_ref[...] = _dot_tn(mb_v, dvb).astype(BF16)
        dm = _dot_nt(dkb, wk_ref[...]) + _dot_nt(dvb, wv_ref[...])
        _, dg, db = _ln_bwd(dm, xh_ref[...], rs_ref[...], g_ref[...])
        dg_ref[...] = dg
        db_ref[...] = db

    return pl.pallas_call(
        body,
        out_shape=[jax.ShapeDtypeStruct((d, d), BF16), jax.ShapeDtypeStruct((d, d), BF16),
                   jax.ShapeDtypeStruct((1, d), F32), jax.ShapeDtypeStruct((1, d), F32)],
        compiler_params=pltpu.CompilerParams(vmem_limit_bytes=VMEM_LIMIT_V7X),
        name="mem_bwd",
    )(dk, dv, mb, xhat, rstd, g, wk, wv)


def _adamw(w, g, m, v):
    m = ADAM_B1 * m + (1.0 - ADAM_B1) * g
    v = ADAM_B2 * v + (1.0 - ADAM_B2) * (g * g)
    m_hat = m / (1.0 - ADAM_B1 ** ADAM_STEP)
    v_hat = v / (1.0 - ADAM_B2 ** ADAM_STEP)
    delta = -ADAM_LR * (m_hat / (jnp.sqrt(v_hat) + ADAM_EPS) + ADAM_WD * w)
    return delta, m, v


def _slot_sum(ref):
    g = ref[0].astype(F32)
    for s in range(1, N_DEV):
        g = g + ref[s].astype(F32)
    return g


def _adam_sharded(lands, w, m, v, axis, name):
    rows, cols = w.shape
    nl = len(lands)
    transposed = axis == 1 and nl == 2
    if transposed:
        rows, cols = cols, rows
        tr = 256
        grid = (rows // tr,)
        wblk = pl.BlockSpec((cols, tr), lambda i: (0, i))
        lblk = [pl.BlockSpec((N_DEV, tr, a.shape[2]), lambda i: (0, i, 0)) for a in lands]
    elif axis == 1:
        tr = 256 if rows % 256 == 0 else rows
        grid = (rows // tr,)
        wblk = pl.BlockSpec((tr, cols), lambda i: (i, 0))
        lblk = [pl.BlockSpec((N_DEV, tr, a.shape[2]), lambda i: (0, i, 0)) for a in lands]
    else:
        tc = _col_tile(cols)
        grid = (cols // tc,)
        wblk = pl.BlockSpec((rows, tc), lambda i: (0, i))
        lblk = [pl.BlockSpec((N_DEV, a.shape[1], tc), lambda i: (0, 0, i)) for a in lands]

    def body(*refs):
        w_ref, m_ref, v_ref = refs[nl:nl + 3]
        g_ref, d_ref, nm_ref, nv_ref = refs[nl + 3:]
        g = _slot_sum(refs[0])
        if nl == 2:
            tail = _slot_sum(refs[1])
            if transposed:
                g = jnp.concatenate([g.T, tail.T[:cols - g.shape[1], :]], axis=0)
            elif axis == 1:
                g = jnp.concatenate([g, tail[:, :cols - g.shape[1]]], axis=1)
            else:
                g = jnp.concatenate([g, tail[:rows - g.shape[0], :]], axis=0)
        delta, nm, nv = _adamw(w_ref[...], g, m_ref[...], v_ref[...])
        g_ref[...] = g
        d_ref[...] = delta
        nm_ref[...] = nm
        nv_ref[...] = nv

    shp = pltpu.HBM(w.shape, F32)
    return pl.pallas_call(
        body,
        grid=grid,
        in_specs=lblk + [wblk, wblk, wblk],
        out_specs=[wblk, wblk, wblk, wblk],
        out_shape=[shp, shp, shp, shp],
        compiler_params=_params(1),
        name=name,
    )(*[pltpu.with_memory_space_constraint(a, pltpu.HBM) for a in (*lands, w, m, v)])


def _mesh_pos():
    return lax.axis_index("x"), lax.axis_index("y"), lax.axis_index("c")


def _peer(k):
    x, y, c = _mesh_pos()
    pos = (x ^ (k >> 2), y ^ ((k >> 1) & 1), c ^ (k & 1))
    return pos, 4 * pos[0] + 2 * pos[1] + pos[2]


def _sem_index(row, k):
    return row * (N_DEV - 1) + k - 1


def _window(ref, axis, start, size):
    align = 16 if axis == 0 else LANES
    start = pl.multiple_of(start, align)
    return ref.at[pl.ds(start, size), :] if axis == 0 else ref.at[:, pl.ds(start, size)]


def _piece_refs(piece, srcs, lands, me, peer):
    kind, si, li, axis, base, stride, shape = piece
    if kind == "gather":
        return srcs[si], _window(lands[li], axis, base + stride * me, shape[axis])
    return _window(srcs[si], axis, base + stride * peer, shape[axis]), lands[li].at[me]


def _place_own(srcs, land_shapes, pieces, name):
    ns, nl, npc = len(srcs), len(land_shapes), len(pieces)

    def body(*refs):
        s_refs = refs[:ns]
        l_refs = refs[ns:ns + nl]
        bufs = refs[ns + nl:ns + nl + npc]
        sems = refs[ns + nl + npc]
        x, y, c = _mesh_pos()
        me = 4 * x + 2 * y + c
        loads = []
        for p, piece in enumerate(pieces):
            src, dst = _piece_refs(piece, s_refs, l_refs, me, me)
            cp = pltpu.make_async_copy(src, bufs[p], sems.at[0, p])
            cp.start()
            loads.append((cp, dst))
        stores = []
        for p, (cp, dst) in enumerate(loads):
            cp.wait()
            out = pltpu.make_async_copy(bufs[p], dst, sems.at[1, p])
            out.start()
            stores.append(out)
        for out in stores:
            out.wait()

    out = pl.pallas_call(
        body,
        in_specs=[ANY] * ns,
        out_specs=[HBM] * nl,
        out_shape=[pltpu.HBM(s.shape, s.dtype) for s in land_shapes],
        scratch_shapes=[pltpu.VMEM(pc[6], srcs[pc[1]].dtype) for pc in pieces] + [pltpu.SemaphoreType.DMA((2, npc))],
        compiler_params=pltpu.CompilerParams(vmem_limit_bytes=VMEM_LIMIT_V7X),
        name=name,
    )(*srcs)
    return list(out)


def _comm_start(srcs, lands, pieces, groups, name, after=()):
    ns, nl, na, ng = len(srcs), len(lands), len(after), len(groups)

    def body(*refs):
        s_refs = refs[:ns]
        l_refs = refs[ns:ns + nl]
        outs = refs[ns + nl + na:]
        sems = outs[:2 * ng]
        token = outs[-1]
        x, y, c = _mesh_pos()
        me = 4 * x + 2 * y + c
        for g, members in enumerate(groups):
            for row, p in enumerate(members):
                for k in range(1, N_DEV):
                    pos, peer = _peer(k)
                    src, dst = _piece_refs(pieces[p], s_refs, l_refs, me, peer)
                    pltpu.make_async_remote_copy(src_ref=src, dst_ref=dst, send_sem=sems[2 * g].at[_sem_index(row, k)],
                                                 recv_sem=sems[2 * g + 1].at[_sem_index(row, k)], device_id=pos,
                                                 device_id_type=MESH_ID).start()
        token[...] = jnp.zeros_like(token)

    sem_shapes = []
    for members in groups:
        sem_shapes += [pltpu.SemaphoreType.DMA((len(members) * (N_DEV - 1),))] * 2
    hbm_of = lambda a: pltpu.HBM(a.shape, a.dtype)
    out = pl.pallas_call(
        body,
        in_specs=[HBM] * (ns + nl) + [ANY] * na,
        out_specs=[SEM] * (2 * ng) + [HBM] * (ns + nl) + [pl.BlockSpec(memory_space=pltpu.VMEM)],
        out_shape=sem_shapes + [hbm_of(a) for a in srcs] + [hbm_of(a) for a in lands]
        + [jax.ShapeDtypeStruct((8, LANES), F32)],
        input_output_aliases={i: 2 * ng + i for i in range(ns + nl)},
        compiler_params=pltpu.CompilerParams(has_side_effects=DATAFLOW),
        name=name,
    )(*[pltpu.with_memory_space_constraint(a, pltpu.HBM) for a in list(srcs) + list(lands)], *after)
    sems = [(out[2 * g], out[2 * g + 1]) for g in range(ng)]
    return sems, list(out[2 * ng:2 * ng + ns]), list(out[2 * ng + ns:2 * ng + ns + nl]), out[-1]


def _comm_wait(srcs, lands, pieces, members, sems, after, name):
    ns, nl, na = len(srcs), len(lands), len(after)

    def body(*refs):
        s_refs = refs[:ns]
        l_refs = refs[ns:ns + nl]
        send_sems, recv_sems = refs[ns + nl:ns + nl + 2]
        x, y, c = _mesh_pos()
        me = 4 * x + 2 * y + c
        for row, p in enumerate(members):
            for k in range(1, N_DEV):
                pos, peer = _peer(k)
                src, dst = _piece_refs(pieces[p], s_refs, l_refs, me, peer)
                cp = pltpu.make_async_remote_copy(src_ref=src, dst_ref=dst, send_sem=send_sems.at[_sem_index(row, k)],
                                                  recv_sem=recv_sems.at[_sem_index(row, k)], device_id=pos,
                                                  device_id_type=MESH_ID)
                cp.wait_send()
                cp.wait_recv()

    hbm_of = lambda a: pltpu.HBM(a.shape, a.dtype)
    out = pl.pallas_call(
        body,
        in_specs=[HBM] * (ns + nl) + [SEM, SEM] + [ANY] * na,
        out_specs=[HBM] * (ns + nl),
        out_shape=[hbm_of(a) for a in srcs] + [hbm_of(a) for a in lands],
        input_output_aliases={i: i for i in range(ns + nl)},
        compiler_params=pltpu.CompilerParams(has_side_effects=DATAFLOW),
        name=name,
    )(*srcs, *lands, sems[0], sems[1], *after)
    return list(out[ns:])


def _landed_block(piece, lands, owner):
    _, _, li, axis, base, stride, shape = piece
    return _window(lands[li], axis, base + stride * owner, shape[axis])


def _copy_stage(name, bufs, in_sems, out_sem_sizes, emit, after=()):
    nb, ni, no, na = len(bufs), len(in_sems), len(out_sem_sizes), len(after)

    def body(*refs):
        b_refs = refs[:nb]
        i_refs = refs[nb:nb + ni]
        o_refs = refs[nb + ni + na:nb + ni + na + no]
        emit(b_refs, i_refs, o_refs)
        refs[-1][...] = jnp.zeros_like(refs[-1])

    hbm_of = lambda a: pltpu.HBM(a.shape, a.dtype)
    out = pl.pallas_call(
        body,
        in_specs=[HBM] * nb + [SEM] * ni + [ANY] * na,
        out_specs=[SEM] * no + [HBM] * nb + [pl.BlockSpec(memory_space=pltpu.VMEM)],
        out_shape=[pltpu.SemaphoreType.DMA((n,)) for n in out_sem_sizes] + [hbm_of(a) for a in bufs]
        + [jax.ShapeDtypeStruct((8, LANES), F32)],
        input_output_aliases={i: no + i for i in range(nb)},
        compiler_params=pltpu.CompilerParams(has_side_effects=DATAFLOW),
        name=name,
    )(*[pltpu.with_memory_space_constraint(a, pltpu.HBM) for a in bufs], *in_sems, *after)
    return list(out[:no]), list(out[no:no + nb]), out[-1]


def _remote(src, dst, send, recv, to):
    return pltpu.make_async_remote_copy(src_ref=src, dst_ref=dst, send_sem=send, recv_sem=recv, device_id=to,
                                        device_id_type=MESH_ID)


def _routed_gather(srcs, lands, pieces, meanwhile, name):
    ns, npc = len(srcs), len(pieces)

    def places():
        x, y, c = _mesh_pos()
        index = lambda p: 4 * p[0] + 2 * p[1] + p[2]
        me, sib = (x, y, c), (x, y, 1 - c)
        xnb, ynb = (1 - x, y, c), (x, 1 - y, c)
        got_first = (x ^ (1 - c), y ^ c, c)
        pass_to = (x ^ c, y ^ (1 - c), c)
        diag = (1 - x, 1 - y, c)
        return index, me, sib, xnb, ynb, got_first, pass_to, diag

    def start(b, _, o):
        index, me, sib, xnb, ynb, *_rest = places()
        send_a, recv_sib, recv_nb = o
        for p, piece in enumerate(pieces):
            src, dst = _piece_refs(piece, b[:ns], b[ns:], index(me), 0)
            _remote(src, dst, send_a.at[3 * p], recv_sib.at[p], sib).start()
            _remote(src, dst, send_a.at[3 * p + 1], recv_nb.at[2 * p], xnb).start()
            _remote(src, dst, send_a.at[3 * p + 2], recv_nb.at[2 * p + 1], ynb).start()

    def pass_a(b, i, o):
        index, me, sib, xnb, ynb, got_first, pass_to, _diag = places()
        (recv_nb,) = i
        send_f, recv_f, send_d, recv_d = o
        for p, piece in enumerate(pieces):
            for j, nb in enumerate((xnb, ynb)):
                blk = _landed_block(piece, b, index(nb))
                _remote(blk, blk, send_f.at[2 * p + j], recv_nb.at[2 * p + j], sib).wait_recv()
                _remote(blk, blk, send_f.at[2 * p + j], recv_f.at[2 * p + j], sib).start()
            blk = _landed_block(piece, b, index(got_first))
            _remote(blk, blk, send_d.at[p], recv_d.at[p], pass_to).start()

    def pass_b(b, i, o):
        index, me, sib, *_mid, diag = places()
        (recv_d,) = i
        send_g, recv_g = o
        for p, piece in enumerate(pieces):
            blk = _landed_block(piece, b, index(diag))
            _remote(blk, blk, send_g.at[p], recv_d.at[p], sib).wait_recv()
            _remote(blk, blk, send_g.at[p], recv_g.at[p], sib).start()

    def last(b, i, _):
        index, me, sib, *_others = places()
        send_a, recv_sib, send_f, recv_f, send_d, send_g, recv_g = i
        for p, piece in enumerate(pieces):
            src, dst = _piece_refs(piece, b[:ns], b[ns:], index(me), 0)
            cp = lambda s_sem, r_sem: _remote(src, dst, s_sem, r_sem, sib)
            cp(send_a.at[3 * p], recv_sib.at[p]).wait_recv()
            cp(send_a.at[3 * p], recv_g.at[p]).wait_recv()
            for j in range(3):
                cp(send_a.at[3 * p + j], recv_sib.at[p]).wait_send()
            for j in range(2):
                cp(send_f.at[2 * p + j], recv_f.at[2 * p + j]).wait_recv()
                cp(send_f.at[2 * p + j], recv_f.at[2 * p + j]).wait_send()
            cp(send_d.at[p], recv_sib.at[p]).wait_send()
            cp(send_g.at[p], recv_sib.at[p]).wait_send()

    (send_a, recv_sib, recv_nb), bufs, started = _copy_stage(name + "_start", list(srcs) + list(lands), [],
                                                             [3 * npc, npc, 2 * npc], start)
    srcs, lands = bufs[:ns], bufs[ns:]
    (send_f, recv_f, send_d, recv_d), lands, _ = _copy_stage(name + "_pass_a", lands, [recv_nb],
                                                             [2 * npc, 2 * npc, npc, npc],
                                                             lambda b, i, o: pass_a(b, i, o), after=meanwhile(started))
    (send_g, recv_g), lands, tok = _copy_stage(name + "_pass_b", lands, [recv_d], [npc, npc], pass_b)
    _, bufs, _ = _copy_stage(name + "_last", list(srcs) + list(lands),
                             [send_a, recv_sib, send_f, recv_f, send_d, send_g, recv_g], [], last)
    return bufs[ns:], tok


def _paired_gather(srcs, lands, pieces, groups, after, name):
    ns, ng = len(srcs), len(groups)
    far = (1, 2, 3)

    def me_sib():
        x, y, c = _mesh_pos()
        return 4 * x + 2 * y + c, (x, y, 1 - c)

    def start(b, _, o):
        me, sib = me_sib()
        for g, members in enumerate(groups):
            send, recv_sib, recv_far = o[3 * g:3 * g + 3]
            for r, p in enumerate(members):
                src, dst = _piece_refs(pieces[p], b[:ns], b[ns:], me, 0)
                _remote(src, dst, send.at[4 * r], recv_sib.at[r], sib).start()
                for j in far:
                    _remote(src, dst, send.at[4 * r + j], recv_far.at[3 * r + j - 1], _peer(2 * j)[0]).start()

    def forward(b, i, o):
        _, sib = me_sib()
        for g, members in enumerate(groups):
            send_f, recv_f = o[2 * g:2 * g + 2]
            for r, p in enumerate(members):
                for j in far:
                    blk = _landed_block(pieces[p], b, _peer(2 * j)[1])
                    _remote(blk, blk, send_f.at[3 * r + j - 1], i[g].at[3 * r + j - 1], sib).wait_recv()
                    _remote(blk, blk, send_f.at[3 * r + j - 1], recv_f.at[3 * r + j - 1], sib).start()

    def last(sub):
        def emit(b, i, _):
            send, recv_sib, send_f, recv_f = i
            me, sib = me_sib()
            for r, piece in enumerate(sub):
                src, dst = _piece_refs(piece, b[:-1], b[-1:], me, 0)
                _remote(src, dst, send.at[4 * r], recv_sib.at[r], sib).wait_recv()
                for j in range(4):
                    _remote(src, dst, send.at[4 * r + j], recv_sib.at[r], sib).wait_send()
                for j in far:
                    blk = _landed_block(piece, b[-1:], _peer(2 * j + 1)[1])
                    _remote(blk, blk, send_f.at[3 * r + j - 1], recv_f.at[3 * r + j - 1], sib).wait_recv()
                    _remote(blk, blk, send_f.at[3 * r + j - 1], recv_f.at[3 * r + j - 1], sib).wait_send()
        return emit

    sizes = []
    for members in groups:
        sizes += [4 * len(members), len(members), 3 * len(members)]
    sems, bufs, started = _copy_stage(name + "_start", list(srcs) + list(lands), [], sizes, start, after=after)
    srcs = bufs[:ns]
    state = dict(lands=bufs[ns:])

    def finish(g, after):
        if "passed" not in state:
            sizes_f = []
            for members in groups:
                sizes_f += [3 * len(members)] * 2
            state["passed"], state["lands"], _ = _copy_stage(name + "_pass", state["lands"],
                                                             [sems[3 * k + 2] for k in range(ng)], sizes_f, forward,
                                                             after=after)
            after = ()
        members = groups[g]
        sub = [(pieces[p][0], r, 0) + pieces[p][3:] for r, p in enumerate(members)]
        _, bufs, _ = _copy_stage("%s_last_%d" % (name, g), [srcs[pieces[p][1]] for p in members] + [state["lands"][g]],
                                 [sems[3 * g], sems[3 * g + 1], state["passed"][2 * g], state["passed"][2 * g + 1]],
                                 [], last(sub), after=after)
        return bufs[-1]

    return finish, started


_SMALL_NAMES = ("ln1_g", "ln1_b", "hg_lb_logits", "hg_norm_g", "sg_ln_g", "sg_ln_b", "sg_w_s", "sg_b_s",
                "ln2_g", "ln2_b", "mem_ln_g", "mem_ln_b", "ln3_g", "ln3_b", "ln4_g", "ln4_b")


_VEC_NAMES = ("ln1_g", "ln1_b", "ln2_g", "ln2_b", "mem_ln_g", "mem_ln_b", "ln3_g", "ln3_b", "ln4_g", "ln4_b")
_ROW_NAMES = ("hg_lb_logits", "hg_norm_g", "sg_ln_g", "sg_ln_b", "sg_b_s", "sg_w_s")
VEC_ROWS = 16


def _row_plan(shapes):
    plan, pos = {}, 0
    for k in _ROW_NAMES:
        shp = shapes[k]
        slabs, off = [], pos
        for idx in itertools.product(*[range(dim) for dim in shp[:-2]]):
            slabs.append((idx, off, shp[-2]))
            off += shp[-2]
        plan[k] = (pos, slabs)
        pos = -(-off // 8) * 8
    return plan, -(-pos // 16) * 16


def _pack_small_grads(gs, shapes, loss):
    d = gs[_VEC_NAMES[0]].size
    vec = jnp.concatenate([gs[k].reshape(1, -1) for k in _VEC_NAMES] + [jnp.tile(loss, (1, d // LANES))], axis=0)
    vec = jnp.pad(vec, ((0, VEC_ROWS - vec.shape[0]), (0, 0)))
    plan, total = _row_plan(shapes)
    parts, pos = [], 0
    for k in _ROW_NAMES:
        first, slabs = plan[k]
        rows = gs[k].reshape(-1, LANES)
        end = slabs[-1][1] + slabs[-1][2]
        nxt = -(-end // 8) * 8
        parts.append(jnp.pad(rows, ((0, nxt - first - rows.shape[0]), (0, 0))))
        pos = nxt
    parts.append(jnp.zeros((total - pos, LANES), F32))
    return vec, jnp.concatenate(parts, axis=0)


def _adam_small(land_vec, land_rows, w, m, v):
    names = _VEC_NAMES + _ROW_NAMES
    n = len(names)
    shapes = {k: w[k].shape for k in names}
    plan, _ = _row_plan(shapes)

    def body(*refs):
        lv_ref, lr_ref = refs[:2]
        w_refs, m_refs, v_refs = refs[2:2 + n], refs[2 + n:2 + 2 * n], refs[2 + 2 * n:2 + 3 * n]
        outs = refs[2 + 3 * n:2 + 7 * n]
        loss_ref = refs[2 + 7 * n]
        gv_s, gr_s = refs[3 + 7 * n:]
        gv_s[...] = _slot_sum(lv_ref)
        gr_s[...] = _slot_sum(lr_ref)
        loss_ref[...] = gv_s[len(_VEC_NAMES):len(_VEC_NAMES) + 1, :LANES]
        for p, k in enumerate(names):
            if k in _VEC_NAMES:
                row = _VEC_NAMES.index(k)
                slabs = [((), None, None)]
            else:
                slabs = plan[k][1]
            for idx, off, rows in slabs:
                g = gv_s[row:row + 1, :] if off is None else gr_s[off:off + rows, :]
                sel = idx + (slice(None), slice(None))
                delta, nm, nv = _adamw(w_refs[p][sel], g, m_refs[p][sel], v_refs[p][sel])
                for o, val in zip(range(4), (g, delta, nm, nv)):
                    outs[o * n + p][sel] = val

    flat = lambda tree: [tree[k] for k in names]
    shp = [jax.ShapeDtypeStruct(shapes[k], F32) for k in names]
    out = pl.pallas_call(
        body,
        out_shape=shp * 4 + [jax.ShapeDtypeStruct((1, LANES), F32)],
        scratch_shapes=[pltpu.VMEM(land_vec.shape[1:], F32), pltpu.VMEM(land_rows.shape[1:], F32)],
        name="adam_small",
    )(land_vec, land_rows, *flat(w), *flat(m), *flat(v))
    return [dict(zip(names, out[o * n:(o + 1) * n])) for o in range(4)], out[4 * n]


_COL_FFN = ("ffn1_w_gate", "ffn1_w_up", "ffn2_w_gate", "ffn2_w_up")
_ROW_FFN = ("ffn1_w_down", "ffn2_w_down")
_ROW_SQ = ("w_out", "xa_w_q", "xa_w_k", "xa_w_v", "xa_w_o")
_BIG_NAMES = ("ffn1_w_gate", "ffn1_w_up", "ffn1_w_down", "w_in", "w_out", "xa_w_q", "xa_w_k", "xa_w_v", "xa_w_o",
              "ffn2_w_gate", "ffn2_w_up", "ffn2_w_down")


def _ffn_split(fs):
    main = (fs // MXU_WIDTH_V7X) * MXU_WIDTH_V7X
    tail = fs - main
    tail_pad = -(-tail // LANES) * LANES
    assert main > 0 and tail > 0
    return main, tail, tail_pad


def _layout(name, shard_shape):
    r, c = shard_shape
    if name in _COL_FFN:
        main, tail, pad = _ffn_split(c)
        return (r, N_DEV * (main + pad)), [(1, 0, main, (r, main), (0, main)),
                                           (1, N_DEV * main, pad, (r, pad), (main, c))]
    if name in _ROW_FFN:
        main, tail, pad = _ffn_split(r)
        return (N_DEV * (main + pad), c), [(0, 0, main, (main, c), (0, main)),
                                           (0, N_DEV * main, pad, (pad, c), (main, r))]
    if name == "w_in":
        return (r, N_DEV * c), [(1, 0, c, (r, c), (0, c))]
    return (N_DEV * r, c), [(0, 0, r, (r, c), (0, r))]


def _shard_pieces(name, shard):
    out = []
    for axis, _, _, shape, (lo, hi) in _layout(name, shard.shape)[1]:
        part = shard[lo:hi, :] if axis == 0 else shard[:, lo:hi]
        pad = [(0, shape[0] - part.shape[0]), (0, shape[1] - part.shape[1])]
        out.append(jnp.pad(part, pad).astype(BF16))
    return out


def _gather_plan(names, shards):
    srcs, land_shapes, pieces, index = [], [], [], {}
    for li, name in enumerate(names):
        shape2d, parts = _layout(name, shards[name].shape)
        land_shapes.append(jax.ShapeDtypeStruct(shape2d, BF16))
        index[name] = []
        for (axis, base, stride, shape, _), src in zip(parts, _shard_pieces(name, shards[name])):
            index[name].append(len(pieces))
            pieces.append(("gather", len(srcs), li, axis, base, stride, shape))
            srcs.append(src)
    return srcs, land_shapes, pieces, index


def _scatter_plan(names, grads, shard_shapes):
    srcs, land_shapes, pieces, index = [], [], [], {}
    for si, name in enumerate(names):
        _, parts = _layout(name, shard_shapes[name])
        srcs.append(grads[name])
        index[name] = []
        for axis, base, stride, shape, _ in parts:
            index[name].append(len(land_shapes))
            pieces.append(("scatter", si, len(land_shapes), axis, base, stride, shape))
            land_shapes.append(jax.ShapeDtypeStruct((N_DEV,) + shape, grads[name].dtype))
    return srcs, land_shapes, pieces, index


def _small_views(small):
    row = lambda a: a.reshape(1, -1)
    ln = {k: row(small[k]) for k in ("ln1_g", "ln1_b", "ln2_g", "ln2_b", "ln3_g", "ln3_b", "ln4_g", "ln4_b",
                                      "mem_ln_g", "mem_ln_b", "hg_norm_g")}
    sg_w = small["sg_w_s"].reshape(SG_GROUPS, SG_CHUNK, SG_CHUNK)
    sg = dict(logits=jnp.swapaxes(small["hg_lb_logits"], 0, 1),
              g=small["sg_ln_g"].reshape(SG_GROUPS, 1, SG_DIM), b=small["sg_ln_b"].reshape(SG_GROUPS, 1, SG_DIM),
              w=sg_w, bs=small["sg_b_s"].reshape(SG_GROUPS, 1, SG_CHUNK))
    return ln, sg


def _forward(x, xb, mem, target, get_w, small, first_deps=()):
    ln, sg = _small_views(small)
    a1, b1, s1 = _ffn_up(xb, get_w("ffn1_w_gate", ()), get_w("ffn1_w_up", ()), "ffn1_up", deps=first_deps)
    h1b, xh1, rs1 = _mm_res_ln(s1, get_w("ffn1_w_down", (s1,)), x, ln["ln1_g"], ln["ln1_b"], 0.5, "ffn1_down_ln")
    proj = _mm_nn(h1b, get_w("w_in", (h1b,)), "mix_in")
    oraw, mix, states = _hgrn_fwd(proj, sg["logits"], ln["hg_norm_g"])
    mix = _sgu_fwd(proj, mix, sg["g"], sg["b"], sg["w"], sg["bs"])
    h2b, xh2, rs2 = _mm_res_ln(mix, get_w("w_out", (mix,)), (xh1, ln["ln1_g"], ln["ln1_b"]), ln["ln2_g"], ln["ln2_b"],
                               1.0, "mix_out_ln")
    mb, mxh, mrs, kb, vb = _mem_kv(mem, ln["mem_ln_g"], ln["mem_ln_b"], get_w("xa_w_k", (h2b,)), get_w("xa_w_v", (h2b,)))
    qb, att = _attn_fwd(h2b, get_w("xa_w_q", (mrs,)), kb, vb)
    h3b, xh3, rs3 = _mm_res_ln(att, get_w("xa_w_o", (att,)), (xh2, ln["ln2_g"], ln["ln2_b"]), ln["ln3_g"], ln["ln3_b"],
                               1.0, "attn_out_ln")
    a2, b2, s2 = _ffn_up(h3b, get_w("ffn2_w_gate", (h3b,)), get_w("ffn2_w_up", (h3b,)), "ffn2_up")
    loss, dy4, dy4b, dg4, db4 = _mm_res_ln(s2, get_w("ffn2_w_down", (s2,)), (xh3, ln["ln3_g"], ln["ln3_b"]),
                                           ln["ln4_g"], ln["ln4_b"], 0.5, "ffn2_down_ln_loss", target=target)
    return dict(xb=xb, a1=a1, b1=b1, s1=s1, h1b=h1b, xh1=xh1, rs1=rs1, proj=proj, oraw=oraw, mix=mix, states=states,
                h2b=h2b, xh2=xh2, rs2=rs2, mb=mb, mxh=mxh, mrs=mrs, kb=kb, vb=vb, qb=qb, att=att, h3b=h3b, xh3=xh3,
                rs3=rs3, a2=a2, b2=b2, s2=s2, loss=loss, dy4=dy4, dy4b=dy4b, dg4=dg4, db4=db4)


def _backward(sv, wt, small, send):
    ln, sg = _small_views(small)
    gs = {"ln4_g": sv["dg4"], "ln4_b": sv["db4"]}
    loss, dy4, dy4b = sv["loss"], sv["dy4"], sv["dy4b"]
    g_down2 = _mm_tn(sv["s2"], dy4b, "g_ffn2_down", scale=0.5)
    da2, db2, dy3, dy3b, gs["ln3_g"], gs["ln3_b"] = _ffn_bwd_fused(
        dy4, dy4b, wt["ffn2_w_down"], wt["ffn2_w_gate"], wt["ffn2_w_up"], sv["a2"], sv["b2"], 0.5,
        (sv["xh3"], sv["rs3"], ln["ln3_g"]), "ffn2_bwd")
    g_gate2 = _mm_tn(sv["h3b"], da2, "g_ffn2_gate")
    g_up2 = _mm_tn(sv["h3b"], db2, "g_ffn2_up")
    tok = send(("ffn2_w_down", "ffn2_w_gate", "ffn2_w_up"), (g_down2, g_gate2, g_up2))

    g_o = _mm_tn(sv["att"], dy3b, "g_xa_o", deps=(tok,))
    dqb, dk, dv = _attn_bwd(dy3b, wt["xa_w_o"], sv["qb"], sv["kb"], sv["vb"])
    g_q = _mm_tn(sv["h2b"], dqb, "g_xa_q")
    g_k, g_v, gs["mem_ln_g"], gs["mem_ln_b"] = _mem_bwd(dk, dv, sv["mb"], sv["mxh"], sv["mrs"], ln["mem_ln_g"],
                                                        wt["xa_w_k"], wt["xa_w_v"])
    tok = send(("xa_w_o", "xa_w_q", "xa_w_k", "xa_w_v"), (g_o, g_q, g_k, g_v))
    dy2, dy2b, gs["ln2_g"], gs["ln2_b"] = _dx_ln(dy3, [(dqb, wt["xa_w_q"])], (sv["xh2"], sv["rs2"], ln["ln2_g"]),
                                                 "attn_dx_ln", deps=(tok,))

    g_out = _mm_tn(sv["mix"], dy2b, "g_w_out")
    dmix = _mm_nt(dy2b, wt["w_out"], "mix_out_bwd")
    dq, dfz, div, dgg, dlg, dgn = _hgrn_bwd(sv["proj"], sv["oraw"], dmix, sv["states"], sg["logits"], ln["hg_norm_g"])
    du, dvv, gs["sg_ln_g"], gs["sg_ln_b"], gs["sg_w_s"], gs["sg_b_s"] = _sgu_bwd(
        sv["proj"], dmix, sg["g"], sg["b"], sg["w"], sg["bs"])
    gs["hg_lb_logits"] = jnp.swapaxes(dlg, 0, 1)
    gs["hg_norm_g"] = jnp.sum(dgn, axis=0)
    dproj = [dq, dfz, div, dgg, du, dvv]
    g_in = _mm_tn(sv["h1b"], dproj, "g_w_in")
    tok = send(("w_out", "w_in"), (g_out, g_in))
    dy1, dy1b, gs["ln1_g"], gs["ln1_b"] = _dx_ln(dy2, [(dproj, wt["w_in"])], (sv["xh1"], sv["rs1"], ln["ln1_g"]),
                                                 "mix_dx_ln", deps=(tok,))

    g_down1 = _mm_tn(sv["s1"], dy1b, "g_ffn1_down", scale=0.5)
    tok = send(("ffn1_w_down",), (g_down1,))
    da1, db1 = _ffn_bwd_act(dy1b, wt["ffn1_w_down"], sv["a1"], sv["b1"], 0.5, "ffn1_bwd_act", deps=(tok,))
    g_gate1 = _mm_tn(sv["xb"], da1, "g_ffn1_gate")
    tok = send(("ffn1_w_gate",), (g_gate1,))
    g_up1 = _mm_tn(sv["xb"], db1, "g_ffn1_up", deps=(tok,))
    tok = send(("ffn1_w_up",), (g_up1,))
    grad_x = _dx_ln(dy1, [(da1, wt["ffn1_w_gate"]), (db1, wt["ffn1_w_up"])], None, "ffn1_dx", deps=(tok,))
    return loss, grad_x, gs


_WEIGHT_NAMES = ("ffn1_w_gate", "ffn1_w_up", "ffn1_w_down", "ln1_g", "ln1_b", "w_in", "hg_lb_logits", "hg_norm_g",
                 "sg_ln_g", "sg_ln_b", "sg_w_s", "sg_b_s", "w_out", "ln2_g", "ln2_b", "mem_ln_g", "mem_ln_b",
                 "xa_w_q", "xa_w_k", "xa_w_v", "xa_w_o", "ln3_g", "ln3_b", "ffn2_w_gate", "ffn2_w_up", "ffn2_w_down",
                 "ln4_g", "ln4_b")
_FIRST = ("ffn1_w_gate", "ffn1_w_up")
_SECOND = ("ffn1_w_down", "w_in")
_THIRD = ("w_out", "xa_w_k", "xa_w_v", "xa_w_q", "xa_w_o", "ffn2_w_gate", "ffn2_w_up", "ffn2_w_down")


def kernel(x, mem, ffn1_w_gate, ffn1_w_up, ffn1_w_down, ln1_g, ln1_b, w_in, hg_lb_logits, hg_norm_g, sg_ln_g, sg_ln_b, sg_w_s, sg_b_s, w_out, ln2_g, ln2_b, mem_ln_g, mem_ln_b, xa_w_q, xa_w_k, xa_w_v, xa_w_o, ln3_g, ln3_b, ffn2_w_gate, ffn2_w_up, ffn2_w_down, ln4_g, ln4_b, loss_target, m_ffn1_w_gate, m_ffn1_w_up, m_ffn1_w_down, m_ln1_g, m_ln1_b, m_w_in, m_hg_lb_logits, m_hg_norm_g, m_sg_ln_g, m_sg_ln_b, m_sg_w_s, m_sg_b_s, m_w_out, m_ln2_g, m_ln2_b, m_mem_ln_g, m_mem_ln_b, m_xa_w_q, m_xa_w_k, m_xa_w_v, m_xa_w_o, m_ln3_g, m_ln3_b, m_ffn2_w_gate, m_ffn2_w_up, m_ffn2_w_down, m_ln4_g, m_ln4_b, v_ffn1_w_gate, v_ffn1_w_up, v_ffn1_w_down, v_ln1_g, v_ln1_b, v_w_in, v_hg_lb_logits, v_hg_norm_g, v_sg_ln_g, v_sg_ln_b, v_sg_w_s, v_sg_b_s, v_w_out, v_ln2_g, v_ln2_b, v_mem_ln_g, v_mem_ln_b, v_xa_w_q, v_xa_w_k, v_xa_w_v, v_xa_w_o, v_ln3_g, v_ln3_b, v_ffn2_w_gate, v_ffn2_w_up, v_ffn2_w_down, v_ln4_g, v_ln4_b):
    args = dict(locals())
    w = {k: args[k] for k in _WEIGHT_NAMES}
    m = {k: args["m_" + k] for k in _WEIGHT_NAMES}
    v = {k: args["v_" + k] for k in _WEIGHT_NAMES}
    shards = {k: w[k][0] for k in _BIG_NAMES}
    shard_shapes = {k: shards[k].shape for k in _BIG_NAMES}
    small = {k: (w[k][0] if k != "hg_lb_logits" else w[k]) for k in _SMALL_NAMES}

    srcs1, shapes1, pieces1, idx1 = _gather_plan(_FIRST, shards)
    lands1 = _place_own(srcs1, shapes1, pieces1, "gather_first_own")
    prepared = {}

    def prepare_rest(started):
        later = {k: shards[k] + started[0, 0] for k in _SECOND + _THIRD}
        placed = ()
        for key, names in (("second", _SECOND), ("third", _THIRD)):
            srcs, shapes, pieces, idx = _gather_plan(names, later)
            lands = _place_own(srcs, shapes, pieces, "gather_%s_own" % key)
            prepared[key] = (srcs, lands, pieces, idx)
            placed += tuple(lands)
        prepared["xb"] = _to_bf16(x[0], "x_bf16", deps=(started,))
        return placed + (prepared["xb"],)

    lands1, tok1 = _routed_gather(srcs1, lands1, pieces1, prepare_rest, "gather_first")
    srcs_p, lands_p, pieces_p, idx_p = prepared["second"]
    finish_second, tok_p = _paired_gather(srcs_p, lands_p, pieces_p, [list(idx_p[k]) for k in _SECOND], (tok1,),
                                          "gather_second")
    srcs2, lands2, pieces2, idx2 = prepared["third"]
    groups2 = [list(idx2[k]) for k in _THIRD]
    sems2, srcs2, lands2, tok2 = _comm_start(srcs2, lands2, pieces2, groups2, "gather_third_start", after=(tok_p,))
    wt = dict(zip(_FIRST, lands1))
    pending = {k: gi for gi, k in enumerate(_THIRD)}

    def get_w(name, after):
        if name in _SECOND and name not in wt:
            wt[name] = finish_second(_SECOND.index(name), after)
        if name in pending:
            gi = pending.pop(name)
            si = [pieces2[p][1] for p in groups2[gi]]
            sub = [(pieces2[p][0], row, 0) + pieces2[p][3:] for row, p in enumerate(groups2[gi])]
            wt[name] = _comm_wait([srcs2[s] for s in si], [lands2[gi]], sub, list(range(len(sub))), sems2[gi],
                                  after, "gather_wait_" + name)[0]
        return pltpu.with_memory_space_constraint(wt[name], pltpu.HBM)

    sv = _forward(x[0], prepared["xb"], mem[0], loss_target[0], get_w, small, first_deps=(tok2,))

    sent = []

    def send(names, grads):
        srcs, shapes, pieces, idx = _scatter_plan(names, dict(zip(names, grads)), shard_shapes)
        lands = _place_own(srcs, shapes, pieces, "grads_own_%d" % len(sent))
        sems, srcs, lands, tok = _comm_start(srcs, lands, pieces, [list(range(len(pieces)))],
                                             "grads_start_%d" % len(sent))
        sent.append((names, srcs, lands, pieces, idx, sems[0]))
        return tok

    wt = {k: pltpu.with_memory_space_constraint(a, pltpu.HBM) for k, a in wt.items()}
    loss, grad_x, gs = _backward(sv, wt, small, send)

    ssrc = list(_pack_small_grads(gs, {k: w[k].shape for k in _SMALL_NAMES}, loss))
    sp = [("scatter", i, i, 0, 0, 0, a.shape) for i, a in enumerate(ssrc)]
    sshape = [jax.ShapeDtypeStruct((N_DEV,) + a.shape, F32) for a in ssrc]
    sl = _place_own(ssrc, sshape, sp, "small_own")
    ssem, ssrc, sl, _ = _comm_start(ssrc, sl, sp, [[0, 1]], "small_start")

    out_g, out_d, out_m, out_v = {}, {}, {}, {}
    after = (grad_x,)
    for n_sent, (names, srcs, lands, pieces, idx, sems) in enumerate(sent):
        lands = _comm_wait(srcs, lands, pieces, list(range(len(pieces))), sems, after, "grads_wait_%d" % n_sent)
        for k in names:
            axis = 1 if (k in _COL_FFN or k == "w_in") else 0
            if k in _COL_FFN:
                done = _adam_sharded([lands[i] for i in idx[k]], w[k][0].T, m[k][0].T, v[k][0].T, axis, "adam_" + k)
                res = [r.T for r in done]
            else:
                res = done = _adam_sharded([lands[i] for i in idx[k]], w[k][0], m[k][0], v[k][0], axis, "adam_" + k)
            out_g[k], out_d[k], out_m[k], out_v[k] = [r[None] for r in res]
        after = (done[3],)
    sl = _comm_wait(ssrc, sl, sp, [0, 1], ssem[0], after, "small_wait")
    small_out, loss_sum = _adam_small(sl[0], sl[1], w, m, v)
    for dst, res in zip((out_g, out_d, out_m, out_v), small_out):
        dst.update(res)
    loss_all = loss_sum[0, 0]
    return (loss_all, grad_x[None], *[out_g[k] for k in _WEIGHT_NAMES], *[out_d[k] for k in _WEIGHT_NAMES],
            *[out_m[k] for k in _WEIGHT_NAMES], *[out_v[k] for k in _WEIGHT_NAMES])
```

```python
import itertools

import jax
import jax.numpy as jnp
import numpy as np
from jax import lax
from jax.experimental import pallas as pl
from jax.experimental.pallas import tpu as pltpu

F32 = jnp.float32
BF16 = jnp.bfloat16

N_DEV = 8
ALPHA = 2.0 ** 0.25
LN_EPS = 1e-5
HG_HEADS = 4
HG_DIM = 128
SG_GROUPS = 4
SG_DIM = 128
SG_CHUNK = 128
X_HEADS = 4
HG_BLOCK = 16
HG_UNROLL = 16
ADAM_LR = 0.001
ADAM_B1 = 0.9
ADAM_B2 = 0.999
ADAM_EPS = 1e-08
ADAM_WD = 0.01
ADAM_STEP = 10
VMEM_LIMIT_V7X = 48 * 1024 * 1024
MXU_WIDTH_V7X = 256
LANES = 128
MESH_ID = pl.DeviceIdType.MESH
ANY = pl.BlockSpec(memory_space=pl.ANY)
HBM = pl.BlockSpec(memory_space=pltpu.HBM)
SEM = pl.BlockSpec(memory_space=pltpu.SEMAPHORE)
DATAFLOW = pltpu.SideEffectType.DATAFLOW_SIDE_EFFECTING


def _params(n_axes):
    return pltpu.CompilerParams(dimension_semantics=("arbitrary",) * n_axes, vmem_limit_bytes=VMEM_LIMIT_V7X)


def _dot(a, b):
    return jnp.dot(a, b, preferred_element_type=F32)


def _dot_nt(a, b):
    return lax.dot_general(a, b, (((1,), (1,)), ((), ())), preferred_element_type=F32)


def _dot_tn(a, b):
    return lax.dot_general(a, b, (((0,), (0,)), ((), ())), preferred_element_type=F32)


def _sigmoid(x):
    return 1.0 / (1.0 + jnp.exp(-x))


def _silu_and_grad(a):
    sig = _sigmoid(a)
    return a * sig, sig * (1.0 + a * (1.0 - sig))


_GELU_C = 0.7978845608028654


def _gelu_and_grad(x):
    inner = _GELU_C * (x + 0.044715 * x * x * x)
    t = jnp.tanh(inner)
    val = 0.5 * x * (1.0 + t)
    grad = 0.5 * (1.0 + t) + 0.5 * x * (1.0 - t * t) * _GELU_C * (1.0 + 3.0 * 0.044715 * x * x)
    return val, grad


def _ln_fwd(y, g, b):
    mu = jnp.mean(y, axis=-1, keepdims=True)
    yc = y - mu
    var = jnp.mean(yc * yc, axis=-1, keepdims=True)
    rstd = lax.rsqrt(var + LN_EPS)
    xhat = yc * rstd
    return xhat * g + b, xhat, rstd


def _ln_bwd(dh, xhat, rstd, g):
    dxh = dh * g
    m1 = jnp.mean(dxh, axis=-1, keepdims=True)
    m2 = jnp.mean(dxh * xhat, axis=-1, keepdims=True)
    dy = rstd * (dxh - m1 - xhat * m2)
    dg = jnp.sum(dh * xhat, axis=0, keepdims=True)
    db = jnp.sum(dh, axis=0, keepdims=True)
    return dy, dg, db


def _mask_dot(mask, x):
    hi = x.astype(BF16)
    lo = (x - hi.astype(F32)).astype(BF16)
    n = mask.shape[0]
    parts = [_dot(mask, hi[r:r + n, :]) + _dot(mask, lo[r:r + n, :]) for r in range(0, x.shape[0], n)]
    return parts[0] if len(parts) == 1 else jnp.concatenate(parts, axis=0)


def _block_masks(n):
    r = np.arange(n)[:, None]
    c = np.arange(n)[None, :]
    same = (r // HG_BLOCK) == (c // HG_BLOCK)
    return jnp.asarray(np.stack([same & (c <= r), same & (c >= r), same]), BF16)


def _row_tile(t):
    return min(t, 512)


def _col_tile(n):
    for cand in (512, 256, 128):
        if n % cand == 0:
            return cand
    return n


def _resident(w):
    return pl.BlockSpec(w.shape, lambda *_: (0, 0), pipeline_mode=pl.Buffered(1))


def _drop_deps(body, n_in, n_deps):
    if n_deps == 0:
        return body
    return lambda *refs: body(*refs[:n_in], *refs[n_in + n_deps:])


def _to_bf16(x, name, deps=()):
    t, d = x.shape
    tm = _row_tile(t)

    def body(x_ref, o_ref):
        o_ref[...] = x_ref[...].astype(BF16)

    row = pl.BlockSpec((tm, d), lambda i: (i, 0))
    return pl.pallas_call(
        _drop_deps(body, 1, len(deps)),
        grid=(t // tm,),
        in_specs=[row] + [ANY] * len(deps),
        out_specs=row,
        out_shape=jax.ShapeDtypeStruct((t, d), BF16),
        compiler_params=_params(1),
        name=name,
    )(x, *deps)


def _ffn_up(hb, wg, wu, name, deps=()):
    t, d = hb.shape
    f = wg.shape[1]
    tm = _row_tile(t)
    tn = _col_tile(f)

    def body(h_ref, wg_ref, wu_ref, a_ref, b_ref, s_ref):
        h = h_ref[...]
        for c in range(f // tn):
            cols = slice(c * tn, (c + 1) * tn)
            a = _dot(h, wg_ref[:, cols])
            b = _dot(h, wu_ref[:, cols])
            a_ref[:, cols] = a.astype(BF16)
            b_ref[:, cols] = b.astype(BF16)
            s_ref[:, cols] = (a * _sigmoid(a) * b).astype(BF16)

    act = pl.BlockSpec((tm, f), lambda i: (i, 0))
    return pl.pallas_call(
        _drop_deps(body, 3, len(deps)),
        grid=(t // tm,),
        in_specs=[pl.BlockSpec((tm, d), lambda i: (i, 0)), _resident(wg), _resident(wu)] + [ANY] * len(deps),
        out_specs=[act, act, act],
        out_shape=[jax.ShapeDtypeStruct((t, f), BF16)] * 3,
        compiler_params=_params(1),
        name=name,
    )(hb, wg, wu, *deps)


def _mm_res_ln(lhs, w, res, g, b, coef, name, target=None):
    t, kd = lhs.shape
    d = w.shape[1]
    tm = _row_tile(t)
    nt = t // tm
    from_norm = isinstance(res, tuple)
    n_res = 3 if from_norm else 1

    def body(*refs):
        l_ref, w_ref = refs[:2]
        r_refs = refs[2:2 + n_res]
        g_ref, b_ref = refs[2 + n_res:4 + n_res]
        rest = refs[4 + n_res:]
        prev = r_refs[0][...] * r_refs[1][...] + r_refs[2][...] if from_norm else r_refs[0][...]
        y = ALPHA * prev + coef * _dot(l_ref[...], w_ref[...])
        h, xhat, rstd = _ln_fwd(y, g_ref[...], b_ref[...])
        if target is None:
            hb_ref, xh_ref, rs_ref = rest
            hb_ref[...] = h.astype(BF16)
            xh_ref[...] = xhat
            rs_ref[...] = rstd
            return
        t_ref, loss_ref, dy_ref, dyb_ref, dg_ref, db_ref, lacc = rest
        i = pl.program_id(0)

        @pl.when(i == 0)
        def _():
            lacc[...] = jnp.zeros_like(lacc)
            dg_ref[...] = jnp.zeros_like(dg_ref)
            db_ref[...] = jnp.zeros_like(db_ref)

        err = h - t_ref[...]
        lacc[...] += jnp.sum(err * err, axis=0, keepdims=True)
        dy, dg, db = _ln_bwd(err * (1.0 / d), xhat, rstd, g_ref[...])
        dy_ref[...] = dy
        dyb_ref[...] = dy.astype(BF16)
        dg_ref[...] += dg
        db_ref[...] += db

        @pl.when(i == nt - 1)
        def _():
            loss_ref[...] = jnp.zeros_like(loss_ref) + jnp.sum(lacc[...], axis=1, keepdims=True) * (0.5 / d)

    row = pl.BlockSpec((tm, d), lambda i: (i, 0))
    vec = pl.BlockSpec((1, d), lambda i: (0, 0))
    res_specs = [row, vec, vec] if from_norm else [row]
    res_args = list(res) if from_norm else [res]
    in_specs = [pl.BlockSpec((tm, kd), lambda i: (i, 0)), _resident(w)] + res_specs + [vec, vec]
    args = [lhs, w] + res_args + [g, b]
    if target is None:
        out_specs = [row, row, pl.BlockSpec((tm, 1), lambda i: (i, 0))]
        out_shape = [jax.ShapeDtypeStruct((t, d), BF16), jax.ShapeDtypeStruct((t, d), F32), pltpu.HBM((t, 1), F32)]
        scratch = []
    else:
        in_specs.append(row)
        args.append(target)
        out_specs = [pl.BlockSpec((1, LANES), lambda i: (0, 0)), row, row, vec, vec]
        out_shape = [jax.ShapeDtypeStruct((1, LANES), F32), jax.ShapeDtypeStruct((t, d), F32),
                     jax.ShapeDtypeStruct((t, d), BF16), jax.ShapeDtypeStruct((1, d), F32),
                     jax.ShapeDtypeStruct((1, d), F32)]
        scratch = [pltpu.VMEM((1, d), F32)]
    return pl.pallas_call(
        body,
        grid=(nt,),
        in_specs=in_specs,
        out_specs=out_specs,
        out_shape=out_shape,
        scratch_shapes=scratch,
        compiler_params=_params(1),
        name=name,
    )(*args)


def _store_slabs(o_ref, first, tile):
    for s in range(tile.shape[1] // LANES):
        o_ref[first + s] = tile[:, s * LANES:(s + 1) * LANES]


def _mm_nn(lhs, w, name):
    t, kd = lhs.shape
    n = w.shape[1]
    tm = _row_tile(t)
    tn = _col_tile(n)

    def body(l_ref, w_ref, o_ref):
        lhs_v = l_ref[...]
        for c in range(n // tn):
            _store_slabs(o_ref, c * (tn // LANES), _dot(lhs_v, w_ref[:, c * tn:(c + 1) * tn]))

    return pl.pallas_call(
        body,
        grid=(t // tm,),
        in_specs=[pl.BlockSpec((tm, kd), lambda i: (i, 0)), _resident(w)],
        out_specs=pl.BlockSpec((n // LANES, tm, LANES), lambda i: (0, i, 0)),
        out_shape=jax.ShapeDtypeStruct((n // LANES, t, LANES), F32),
        compiler_params=_params(1),
        name=name,
    )(lhs, w)


def _lower_bound(lg):
    m = jnp.max(lg, axis=0, keepdims=True)
    e = jnp.exp(lg - m)
    return e[0:1, :] / jnp.sum(e, axis=0, keepdims=True)


def _forget_terms(fz, lb):
    e = jnp.exp(-jnp.abs(fz))
    r = 1.0 / (1.0 + e)
    pos = fz >= 0.0
    sig = jnp.where(pos, r, e * r)
    nsig = jnp.where(pos, e * r, r)
    f = lb + (1.0 - lb) * sig
    k = (1.0 - lb) * nsig
    return sig, nsig, f, k


def _hg_tile(t):
    return min(t, 1024)


HG_HALF = HG_BLOCK // 2
NEG_BIG = -1e30


def _halves(a):
    return a[:HG_HALF, :], a[HG_HALF:, :]


def _causal_halves(s):
    return (0, 1) if s < HG_HALF else (1,)


def _decay_from(b_half, b_s, s, h, tidx):
    first = s - h * HG_HALF
    diff = b_half - b_s
    if first > 0:
        diff = jnp.where(tidx >= first, diff, NEG_BIG)
    return jnp.exp(diff)


def _hgrn_fwd(proj, logits, gn):
    t = proj.shape[1]
    ct = _hg_tile(t)
    nct = t // ct
    nblk = ct // HG_BLOCK
    nh = HG_HEADS
    mrows = min(ct, 256)

    def body(q_ref, fz_ref, iv_ref, gg_ref, lg_ref, gn_ref, mask_ref, oraw_ref, oa_ref, st_ref,
             state, qt_s, kt_s, k_s, b_s, dec_s):
        c = pl.program_id(1)

        @pl.when(c == 0)
        def _():
            state[...] = jnp.zeros_like(state)

        lb = _lower_bound(lg_ref[...])
        q = q_ref[...]
        _, _, f, k = _forget_terms(fz_ref[...], lb)
        logf = jnp.log(f)
        b = _mask_dot(mask_ref[0], logf)
        bend = _mask_dot(mask_ref[2], logf)
        qt_s[...] = (q * jnp.exp(b)).astype(BF16)
        kt_s[...] = (k * jnp.exp(bend - b)).astype(BF16)
        k_s[...] = k
        b_s[...] = b
        dec_s[...] = jnp.exp(bend)
        tidx = lax.broadcasted_iota(jnp.int32, (HG_HALF, HG_DIM), 0)

        def blk(i, carry):
            r0 = pl.multiple_of(i * HG_BLOCK, HG_BLOCK)
            rows = pl.ds(r0, HG_BLOCK)
            st = state[...]
            stb = st.astype(BF16)
            st_ref[i] = stb
            v = iv_ref[rows, :]
            qq = q_ref[rows, :]
            kk = k_s[rows, :]
            bb = b_s[rows, :]
            o = list(_halves(_dot_nt(qt_s[rows, :], stb)))
            qh, bh = _halves(qq), _halves(bb)
            for s in range(HG_BLOCK):
                ks, vs = kk[s:s + 1, :], v[s:s + 1, :]
                for h in _causal_halves(s):
                    e = _decay_from(bh[h], bb[s:s + 1, :], s, h, tidx)
                    acol = jnp.sum(qh[h] * (ks * e), axis=1, keepdims=True)
                    o[h] = o[h] + acol * vs
            oraw_ref[rows, :] = jnp.concatenate(o, axis=0)
            state[...] = st * dec_s[pl.ds(r0, 1), :] + _dot_tn(v.astype(BF16), kt_s[rows, :])
            return carry

        lax.fori_loop(0, nblk, blk, 0, unroll=HG_UNROLL)
        oraw = oraw_ref[...]
        r = lax.rsqrt(jnp.mean(oraw * oraw, axis=-1, keepdims=True) + LN_EPS)
        gg = gg_ref[...]
        oa_ref[...] = (oraw * r * gn_ref[...] * gg * _sigmoid(gg)).astype(BF16)

    def slab(off):
        return pl.BlockSpec((None, ct, HG_DIM), lambda h, c: (off + h, c, 0))

    return pl.pallas_call(
        body,
        grid=(nh, nct),
        in_specs=[slab(0), slab(nh), slab(2 * nh), slab(3 * nh),
                  pl.BlockSpec((None, 2, HG_DIM), lambda h, c: (h, 0, 0)),
                  pl.BlockSpec((1, HG_DIM), lambda h, c: (0, 0)),
                  pl.BlockSpec((3, mrows, mrows), lambda h, c: (0, 0, 0))],
        out_specs=[slab(0), pl.BlockSpec((ct, HG_DIM), lambda h, c: (c, h)),
                   pl.BlockSpec((None, nblk, HG_DIM, HG_DIM), lambda h, c: (h, c, 0, 0))],
        out_shape=[jax.ShapeDtypeStruct((nh, t, HG_DIM), F32),
                   jax.ShapeDtypeStruct((t, (nh + SG_GROUPS) * HG_DIM), BF16),
                   jax.ShapeDtypeStruct((nh, t // HG_BLOCK, HG_DIM, HG_DIM), BF16)],
        scratch_shapes=[pltpu.VMEM((HG_DIM, HG_DIM), F32), pltpu.VMEM((ct, HG_DIM), BF16),
                        pltpu.VMEM((ct, HG_DIM), BF16), pltpu.VMEM((ct, HG_DIM), F32),
                        pltpu.VMEM((ct, HG_DIM), F32), pltpu.VMEM((ct, HG_DIM), F32)],
        compiler_params=_params(2),
        name="hgrn_fwd",
    )(proj, proj, proj, proj, logits, gn, _block_masks(mrows))


def _sg_tile(t):
    return min(t, 512)


def _sgu_chunk_fwd(u, v, ln_g, ln_b, wm, bs):
    ua, dua = _gelu_and_grad(u)
    va, dva = _gelu_and_grad(v)
    vn, xhat, rstd = _ln_fwd(va, ln_g, ln_b)
    s = _dot(wm, vn.astype(BF16)) + bs
    return ua, dua, dva, vn, xhat, rstd, s


def _tril_weight(w):
    n = SG_CHUNK
    r = lax.broadcasted_iota(jnp.int32, (n, n), 0)
    c = lax.broadcasted_iota(jnp.int32, (n, n), 1)
    return jnp.where(c <= r, w, 0.0)


def _bias_by_position(b_row):
    return jnp.broadcast_to(b_row, (SG_CHUNK, SG_CHUNK)).T


def _sgu_fwd(proj, mix, ln_g, ln_b, w_s, b_row):
    t = proj.shape[1]
    ct = _sg_tile(t)
    ng = SG_GROUPS
    wide = ng * SG_DIM
    blk_u = 4 * HG_HEADS // ng

    def body(u_ref, v_ref, g_ref, b_ref, w_ref, bs_ref, mix_ref, o_ref):
        del mix_ref
        for g in range(ng):
            lanes = slice(g * SG_DIM, (g + 1) * SG_DIM)
            wm = _tril_weight(w_ref[g]).astype(BF16)
            bs = _bias_by_position(bs_ref[g])
            for n in range(ct // SG_CHUNK):
                rows = slice(n * SG_CHUNK, (n + 1) * SG_CHUNK)
                ua, _, _, _, _, _, s = _sgu_chunk_fwd(u_ref[g, rows, :], v_ref[g, rows, :], g_ref[g], b_ref[g], wm, bs)
                o_ref[rows, lanes] = (ua * s).astype(BF16)

    full = lambda a: pl.BlockSpec(a.shape, lambda c: (0,) * a.ndim)
    return pl.pallas_call(
        body,
        grid=(t // ct,),
        in_specs=[pl.BlockSpec((ng, ct, SG_DIM), lambda c: (blk_u, c, 0)),
                  pl.BlockSpec((ng, ct, SG_DIM), lambda c: (blk_u + 1, c, 0)),
                  full(ln_g), full(ln_b), full(w_s), full(b_row), ANY],
        out_specs=pl.BlockSpec((ct, wide), lambda c: (c, 1)),
        out_shape=jax.ShapeDtypeStruct(mix.shape, mix.dtype),
        input_output_aliases={6: 0},
        compiler_params=_params(1),
        name="sgu_fwd",
    )(proj, proj, ln_g, ln_b, w_s, b_row, mix)


def _mem_kv(mem, g, b, wk, wv):
    m_len, d = mem.shape

    def body(m_ref, g_ref, b_ref, wk_ref, wv_ref, mb_ref, xh_ref, rs_ref, k_ref, v_ref):
        m, xhat, rstd = _ln_fwd(m_ref[...], g_ref[...], b_ref[...])
        mb = m.astype(BF16)
        mb_ref[...] = mb
        xh_ref[...] = xhat
        rs_ref[...] = rstd
        k_ref[...] = _dot(mb, wk_ref[...]).astype(BF16)
        v_ref[...] = _dot(mb, wv_ref[...]).astype(BF16)

    return pl.pallas_call(
        body,
        out_shape=[jax.ShapeDtypeStruct((m_len, d), BF16), jax.ShapeDtypeStruct((m_len, d), F32),
                   jax.ShapeDtypeStruct((m_len, 1), F32), jax.ShapeDtypeStruct((m_len, d), BF16),
                   jax.ShapeDtypeStruct((m_len, d), BF16)],
        compiler_params=pltpu.CompilerParams(vmem_limit_bytes=VMEM_LIMIT_V7X),
        name="mem_kv",
    )(mem, g, b, wk, wv)


def _softmax_rows(s):
    m = jnp.max(s, axis=-1, keepdims=True)
    p = jnp.exp(s - m)
    return p / jnp.sum(p, axis=-1, keepdims=True)


def _attn_fwd(hb, wq, kb, vb):
    t, d = hb.shape
    tm = _row_tile(t)
    dh = d // X_HEADS
    scale = dh ** -0.5

    def body(h_ref, wq_ref, k_ref, v_ref, q_ref, o_ref):
        q = _dot(h_ref[...], wq_ref[...]).astype(BF16)
        q_ref[...] = q
        for hd in range(X_HEADS):
            sl = slice(hd * dh, (hd + 1) * dh)
            p = _softmax_rows(_dot_nt(q[:, sl], k_ref[:, sl]) * scale)
            o_ref[:, sl] = _dot(p.astype(BF16), v_ref[:, sl]).astype(BF16)

    row = pl.BlockSpec((tm, d), lambda i: (i, 0))
    full = lambda a: pl.BlockSpec(a.shape, lambda i: (0, 0))
    return pl.pallas_call(
        body,
        grid=(t // tm,),
        in_specs=[row, full(wq), full(kb), full(vb)],
        out_specs=[row, row],
        out_shape=[jax.ShapeDtypeStruct((t, d), BF16), jax.ShapeDtypeStruct((t, d), BF16)],
        compiler_params=_params(1),
        name="attn_fwd",
    )(hb, wq, kb, vb)


def _ffn_bwd_act(dyb, wd, a, b, coef, name, deps=()):
    t, d = dyb.shape
    f = wd.shape[0]
    tm = _row_tile(t)
    tn = _col_tile(f)

    def body(dy_ref, wd_ref, a_ref, b_ref, da_ref, db_ref):
        dy = dy_ref[...]
        for c in range(f // tn):
            cols = slice(c * tn, (c + 1) * tn)
            ds = _dot_nt(dy, wd_ref[cols, :]) * coef
            silu, dsilu = _silu_and_grad(a_ref[:, cols].astype(F32))
            da_ref[:, cols] = (ds * b_ref[:, cols].astype(F32) * dsilu).astype(BF16)
            db_ref[:, cols] = (ds * silu).astype(BF16)

    act = pl.BlockSpec((tm, f), lambda i: (i, 0))
    return pl.pallas_call(
        _drop_deps(body, 4, len(deps)),
        grid=(t // tm,),
        in_specs=[pl.BlockSpec((tm, d), lambda i: (i, 0)), _resident(wd), act, act] + [ANY] * len(deps),
        out_specs=[act, act],
        out_shape=[jax.ShapeDtypeStruct((t, f), BF16), jax.ShapeDtypeStruct((t, f), BF16)],
        compiler_params=_params(1),
        name=name,
    )(dyb, wd, a, b, *deps)


def _ffn_bwd_fused(dy, dyb, wd, wg, wu, a, b, coef, ln, name):
    t, d = dy.shape
    f = wd.shape[0]
    tm = min(t, 256)
    tn = _col_tile(f)

    def body(dy_ref, dyb_ref, wd_ref, wg_ref, wu_ref, a_ref, b_ref, xh_ref, rs_ref, g_ref,
             da_ref, db_ref, dyo_ref, dyob_ref, dg_ref, dbl_ref):
        dyb_v = dyb_ref[...]
        dh = ALPHA * dy_ref[...]
        for c in range(f // tn):
            cols = slice(c * tn, (c + 1) * tn)
            ds = _dot_nt(dyb_v, wd_ref[cols, :]) * coef
            silu, dsilu = _silu_and_grad(a_ref[:, cols].astype(F32))
            da = (ds * b_ref[:, cols].astype(F32) * dsilu).astype(BF16)
            db = (ds * silu).astype(BF16)
            da_ref[:, cols] = da
            db_ref[:, cols] = db
            dh = dh + _dot_nt(da, wg_ref[:, cols]) + _dot_nt(db, wu_ref[:, cols])

        @pl.when(pl.program_id(0) == 0)
        def _():
            dg_ref[...] = jnp.zeros_like(dg_ref)
            dbl_ref[...] = jnp.zeros_like(dbl_ref)

        dyp, dg, dbl = _ln_bwd(dh, xh_ref[...], rs_ref[...], g_ref[...])
        dyo_ref[...] = dyp
        dyob_ref[...] = dyp.astype(BF16)
        dg_ref[...] += dg
        dbl_ref[...] += dbl

    row = pl.BlockSpec((tm, d), lambda i: (i, 0))
    act = pl.BlockSpec((tm, f), lambda i: (i, 0))
    vec = pl.BlockSpec((1, d), lambda i: (0, 0))
    return pl.pallas_call(
        body,
        grid=(t // tm,),
        in_specs=[row, row, _resident(wd), _resident(wg), _resident(wu), act, act, row,
                  pl.BlockSpec((tm, 1), lambda i: (i, 0)), vec],
        out_specs=[act, act, row, row, vec, vec],
        out_shape=[jax.ShapeDtypeStruct((t, f), BF16), jax.ShapeDtypeStruct((t, f), BF16),
                   jax.ShapeDtypeStruct((t, d), F32), jax.ShapeDtypeStruct((t, d), BF16),
                   jax.ShapeDtypeStruct((1, d), F32), jax.ShapeDtypeStruct((1, d), F32)],
        compiler_params=_params(1),
        name=name,
    )(dy, dyb, wd, wg, wu, a, b, *ln)


def _mm_tn(a, b, name, scale=1.0, deps=()):
    t, m = a.shape
    bs = list(b) if isinstance(b, (list, tuple)) else [b]
    n = sum(piece.shape[1] for piece in bs)
    tt = _row_tile(t)
    nt = t // tt
    tm_o, tn_o = m, n

    def body(a_ref, *refs):
        b_refs, (o_ref, acc) = refs[:len(bs)], refs[len(bs):]
        k = pl.program_id(2)

        @pl.when(k == 0)
        def _():
            acc[...] = jnp.zeros_like(acc)

        first = 0
        for b_ref in b_refs:
            cols = slice(first, first + b_ref.shape[1])
            acc[:, cols] += _dot_tn(a_ref[...], b_ref[...])
            first = cols.stop

        @pl.when(k == nt - 1)
        def _():
            o_ref[...] = (acc[...] * scale).astype(BF16)

    return pl.pallas_call(
        _drop_deps(body, 1 + len(bs), len(deps)),
        grid=(m // tm_o, n // tn_o, nt),
        in_specs=[pl.BlockSpec((tt, tm_o), lambda i, j, k: (k, i))]
        + [pl.BlockSpec((tt, piece.shape[1]), lambda i, j, k: (k, j)) for piece in bs] + [ANY] * len(deps),
        out_specs=pl.BlockSpec((tm_o, tn_o), lambda i, j, k: (i, j)),
        out_shape=pltpu.HBM((m, n), BF16),
        scratch_shapes=[pltpu.VMEM((tm_o, tn_o), F32)],
        compiler_params=_params(3),
        name=name,
    )(a, *bs, *deps)


def _mm_nt(lhs, w, name):
    t, d = lhs.shape
    kd = w.shape[0]
    tm = _row_tile(t)

    def body(l_ref, w_ref, o_ref):
        _store_slabs(o_ref, 0, _dot_nt(l_ref[...], w_ref[...]))

    return pl.pallas_call(
        body,
        grid=(t // tm,),
        in_specs=[pl.BlockSpec((tm, d), lambda i: (i, 0)), _resident(w)],
        out_specs=pl.BlockSpec((kd // LANES, tm, LANES), lambda i: (0, i, 0)),
        out_shape=jax.ShapeDtypeStruct((kd // LANES, t, LANES), F32),
        compiler_params=_params(1),
        name=name,
    )(lhs, w)


def _dx_ln(dy, pairs, ln, name, deps=()):
    t, d = dy.shape
    npair = len(pairs)
    pairs = [(list(lhs) if isinstance(lhs, (list, tuple)) else [lhs], w) for lhs, w in pairs]
    tm = min(t, 512 // npair)
    nt = t // tm
    n_in = 1 + sum(len(pieces) + 1 for pieces, _ in pairs) + (3 if ln is not None else 0)

    def body(*refs):
        dy_ref = refs[0]
        pos = 1
        dh = ALPHA * dy_ref[...]
        for pieces, _ in pairs:
            w_ref = refs[pos + len(pieces)]
            first = 0
            for l_ref in refs[pos:pos + len(pieces)]:
                cols = slice(first, first + l_ref.shape[1])
                dh = dh + _dot_nt(l_ref[...], w_ref[:, cols])
                first = cols.stop
            pos += len(pieces) + 1
        if ln is not None:
            xh_ref, rs_ref, g_ref = refs[pos:pos + 3]
            dyo_ref, dyb_ref, dg_ref, db_ref = refs[pos + 3:pos + 7]

            @pl.when(pl.program_id(0) == 0)
            def _():
                dg_ref[...] = jnp.zeros_like(dg_ref)
                db_ref[...] = jnp.zeros_like(db_ref)

            dyp, dg, db = _ln_bwd(dh, xh_ref[...], rs_ref[...], g_ref[...])
            dyo_ref[...] = dyp
            dyb_ref[...] = dyp.astype(BF16)
            dg_ref[...] += dg
            db_ref[...] += db
        else:
            refs[pos][...] = dh

    row = pl.BlockSpec((tm, d), lambda i: (i, 0))
    vec = pl.BlockSpec((1, d), lambda i: (0, 0))
    in_specs = [row]
    args = [dy]
    for pieces, w in pairs:
        in_specs += [pl.BlockSpec((tm, piece.shape[1]), lambda i: (i, 0)) for piece in pieces] + [_resident(w)]
        args += pieces + [w]
    if ln is not None:
        in_specs += [row, pl.BlockSpec((tm, 1), lambda i: (i, 0)), vec]
        args += list(ln)
        out_specs = [row, row, vec, vec]
        out_shape = [jax.ShapeDtypeStruct((t, d), F32), jax.ShapeDtypeStruct((t, d), BF16),
                     jax.ShapeDtypeStruct((1, d), F32), jax.ShapeDtypeStruct((1, d), F32)]
    else:
        out_specs = row
        out_shape = jax.ShapeDtypeStruct((t, d), F32)
    return pl.pallas_call(
        _drop_deps(body, n_in, len(deps)),
        grid=(nt,),
        in_specs=in_specs + [ANY] * len(deps),
        out_specs=out_specs,
        out_shape=out_shape,
        compiler_params=_params(1),
        name=name,
    )(*args, *deps)


def _hgrn_bwd(proj, oraw, dmix, states, logits, gn):
    t = proj.shape[1]
    ct = _hg_tile(t)
    nct = t // ct
    nblk = ct // HG_BLOCK
    nh = HG_HEADS
    mrows = min(ct, 256)

    def body(q_ref, fz_ref, iv_ref, gg_ref, or_ref, do_ref, st_ref, lg_ref, gn_ref, mask_ref,
             dq_ref, dfz_ref, div_ref, dgg_ref, dlg_ref, dgn_ref,
             dstate, qt_s, kt_s, k_s, b_s, eb_s, ekb_s, dec_s, dor_s, dbl_s, gr_s, dk_s, dlb_acc):
        c = pl.program_id(1)

        @pl.when(c == 0)
        def _():
            dstate[...] = jnp.zeros_like(dstate)
            dlb_acc[...] = jnp.zeros_like(dlb_acc)
            dgn_ref[...] = jnp.zeros_like(dgn_ref)

        lb = _lower_bound(lg_ref[...])
        q = q_ref[...]
        sig, nsig, f, k = _forget_terms(fz_ref[...], lb)
        logf = jnp.log(f)
        b = _mask_dot(mask_ref[0], logf)
        bend = _mask_dot(mask_ref[2], logf)
        eb = jnp.exp(b)
        ekb = jnp.exp(bend - b)
        qt_s[...] = (q * eb).astype(BF16)
        kt_s[...] = (k * ekb).astype(BF16)
        k_s[...] = k
        b_s[...] = b
        eb_s[...] = eb
        ekb_s[...] = ekb
        dec_s[...] = jnp.exp(bend)
        oraw = or_ref[...]
        r = lax.rsqrt(jnp.mean(oraw * oraw, axis=-1, keepdims=True) + LN_EPS)
        on = oraw * r
        gg = gg_ref[...]
        silu, dsilu = _silu_and_grad(gg)
        doa = do_ref[...]
        gnv = gn_ref[...]
        dgg_ref[...] = (doa * on * gnv * dsilu).astype(BF16)
        dyn = doa * silu
        dgn_ref[...] += jnp.sum(dyn * on, axis=0, keepdims=True)
        don = dyn * gnv
        dor_s[...] = r * (don - on * jnp.mean(don * on, axis=-1, keepdims=True))
        tidx = lax.broadcasted_iota(jnp.int32, (HG_HALF, HG_DIM), 0)

        def blk(ii, carry):
            i = nblk - 1 - ii
            r0 = pl.multiple_of(i * HG_BLOCK, HG_BLOCK)
            rows = pl.ds(r0, HG_BLOCK)
            st = st_ref[i]
            dst = dstate[...]
            dstb = dst.astype(BF16)
            do = dor_s[rows, :]
            dob = do.astype(BF16)
            v = iv_ref[rows, :]
            vb = v.astype(BF16)
            qq = q_ref[rows, :]
            kk = k_s[rows, :]
            bb = b_s[rows, :]
            qt = qt_s[rows, :]
            kt = kt_s[rows, :]
            dec = dec_s[pl.ds(r0, 1), :]
            dkt = _dot(vb, dstb)
            dq = _dot(dob, st) * eb_s[rows, :]
            dk = dkt * ekb_s[rows, :]
            dv = _dot_nt(kt, dstb)
            gend = (jnp.sum(kk * dk, axis=0, keepdims=True)
                    + dec * jnp.sum(dst * st.astype(F32), axis=0, keepdims=True))
            qh, bh, doh = _halves(qq), _halves(bb), _halves(do)
            dqh, dkh, dvh = list(_halves(dq)), list(_halves(dk)), list(_halves(dv))
            for s in range(HG_BLOCK):
                ks, vs = kk[s:s + 1, :], v[s:s + 1, :]
                dk_part = dv_part = None
                for h in _causal_halves(s):
                    e = _decay_from(bh[h], bb[s:s + 1, :], s, h, tidx)
                    ke = ks * e
                    acol = jnp.sum(qh[h] * ke, axis=1, keepdims=True)
                    dacol = jnp.sum(doh[h] * vs, axis=1, keepdims=True)
                    dqh[h] = dqh[h] + dacol * ke
                    pk = dacol * (qh[h] * e)
                    pv = acol * doh[h]
                    dk_part = pk if dk_part is None else dk_part + pk
                    dv_part = pv if dv_part is None else dv_part + pv
                hs, row = divmod(s, HG_HALF)
                dkh[hs] = dkh[hs] + jnp.where(tidx == row, jnp.sum(dk_part, axis=0, keepdims=True), 0.0)
                dvh[hs] = dvh[hs] + jnp.where(tidx == row, jnp.sum(dv_part, axis=0, keepdims=True), 0.0)
            dq = jnp.concatenate(dqh, axis=0)
            dk = jnp.concatenate(dkh, axis=0)
            dv = jnp.concatenate(dvh, axis=0)
            dq_ref[rows, :] = dq.astype(BF16)
            div_ref[rows, :] = dv.astype(BF16)
            dk_s[rows, :] = dk
            dbl_s[rows, :] = qq * dq - kk * dk
            gr_s[rows, :] = jnp.zeros((HG_BLOCK, HG_DIM), F32) + gend
            dstate[...] = dst * dec + _dot_tn(dob, qt)
            return carry

        lax.fori_loop(0, nblk, blk, 0, unroll=HG_UNROLL)
        dlogf = _mask_dot(mask_ref[1], dbl_s[...]) + gr_s[...]
        dk = dk_s[...]
        dfz_ref[...] = ((dlogf / f - dk) * ((1.0 - lb) * sig * nsig)).astype(BF16)
        dlb_acc[...] += jnp.sum((dlogf / f - dk) * nsig, axis=0, keepdims=True)

        @pl.when(c == nct - 1)
        def _():
            dl0 = dlb_acc[...] * lb * (1.0 - lb)
            layer = lax.broadcasted_iota(jnp.int32, (2, HG_DIM), 0)
            dlg_ref[...] = jnp.where(layer == 0, dl0, -dl0)

    def slab(off):
        return pl.BlockSpec((None, ct, HG_DIM), lambda h, c: (off + h, nct - 1 - c, 0))

    out_slab = pl.BlockSpec((ct, HG_DIM), lambda h, c: (nct - 1 - c, h))
    tile_f32 = pltpu.VMEM((ct, HG_DIM), F32)
    tile_b16 = pltpu.VMEM((ct, HG_DIM), BF16)
    slab_shape = pltpu.HBM((t, nh * HG_DIM), BF16)
    return pl.pallas_call(
        body,
        grid=(nh, nct),
        in_specs=[slab(0), slab(nh), slab(2 * nh), slab(3 * nh), slab(0), slab(0),
                  pl.BlockSpec((None, nblk, HG_DIM, HG_DIM), lambda h, c: (h, nct - 1 - c, 0, 0)),
                  pl.BlockSpec((None, 2, HG_DIM), lambda h, c: (h, 0, 0)),
                  pl.BlockSpec((1, HG_DIM), lambda h, c: (0, 0)),
                  pl.BlockSpec((3, mrows, mrows), lambda h, c: (0, 0, 0))],
        out_specs=[out_slab, out_slab, out_slab, out_slab,
                   pl.BlockSpec((None, 2, HG_DIM), lambda h, c: (h, 0, 0)),
                   pl.BlockSpec((None, 1, HG_DIM), lambda h, c: (h, 0, 0))],
        out_shape=[slab_shape, slab_shape, slab_shape, slab_shape,
                   jax.ShapeDtypeStruct((nh, 2, HG_DIM), F32), jax.ShapeDtypeStruct((nh, 1, HG_DIM), F32)],
        scratch_shapes=[pltpu.VMEM((HG_DIM, HG_DIM), F32), tile_b16, tile_b16, tile_f32, tile_f32, tile_f32, tile_f32,
                        tile_f32, tile_f32, tile_f32, tile_f32, tile_f32, pltpu.VMEM((1, HG_DIM), F32)],
        compiler_params=_params(2),
        name="hgrn_bwd",
    )(proj, proj, proj, proj, oraw, dmix, states, logits, gn, _block_masks(mrows))


def _sgu_bwd(proj, dmix, ln_g, ln_b, w_s, b_row):
    t = proj.shape[1]
    ct = _sg_tile(t)
    nct = t // ct
    ng = SG_GROUPS
    off_u = 4 * HG_HEADS
    off_v = off_u + ng
    n = SG_CHUNK

    def body(u_ref, v_ref, do_ref, g_ref, b_ref, w_ref, bs_ref, du_ref, dv_ref, dg_ref, db_ref, dw_ref, dbs_ref):
        c = pl.program_id(1)

        @pl.when(c == 0)
        def _():
            dg_ref[...] = jnp.zeros_like(dg_ref)
            db_ref[...] = jnp.zeros_like(db_ref)
            dw_ref[...] = jnp.zeros_like(dw_ref)
            dbs_ref[...] = jnp.zeros_like(dbs_ref)

        r = lax.broadcasted_iota(jnp.int32, (n, n), 0)
        cc = lax.broadcasted_iota(jnp.int32, (n, n), 1)
        wm = jnp.where(cc <= r, w_ref[...], 0.0).astype(BF16)
        wmt = jnp.where(r <= cc, w_ref[...].T, 0.0).astype(BF16)
        bs = _bias_by_position(bs_ref[...])
        for ci in range(ct // n):
            rows = slice(ci * n, (ci + 1) * n)
            ua, dua, dva, vn, xhat, rstd, s = _sgu_chunk_fwd(u_ref[rows, :], v_ref[rows, :], g_ref[...], b_ref[...],
                                                             wm, bs)
            do = do_ref[rows, :]
            du_ref[rows, :] = (do * s * dua).astype(BF16)
            ds = do * ua
            dsb = ds.astype(BF16)
            dbs_ref[...] += jnp.sum(ds, axis=1, keepdims=True)
            dw_ref[...] += _dot_nt(dsb, vn.astype(BF16))
            dvn = _dot(wmt, dsb)
            dva_in, dg, db = _ln_bwd(dvn, xhat, rstd, g_ref[...])
            dg_ref[...] += dg
            db_ref[...] += db
            dv_ref[rows, :] = (dva_in * dva).astype(BF16)

        @pl.when(c == nct - 1)
        def _():
            dw_ref[...] = jnp.where(cc <= r, dw_ref[...], 0.0)

    vec = pl.BlockSpec((None, 1, SG_DIM), lambda g, c: (g, 0, 0))
    mat = pl.BlockSpec((None, n, n), lambda g, c: (g, 0, 0))
    col = pl.BlockSpec((None, n, 1), lambda g, c: (g, 0, 0))
    out_slab = pl.BlockSpec((ct, SG_DIM), lambda g, c: (c, g))
    return pl.pallas_call(
        body,
        grid=(ng, nct),
        in_specs=[pl.BlockSpec((None, ct, SG_DIM), lambda g, c: (off_u + g, c, 0)),
                  pl.BlockSpec((None, ct, SG_DIM), lambda g, c: (off_v + g, c, 0)),
                  pl.BlockSpec((None, ct, SG_DIM), lambda g, c: (ng + g, c, 0)), vec, vec, mat, vec],
        out_specs=[out_slab, out_slab, vec, vec, mat, col],
        out_shape=[pltpu.HBM((t, ng * SG_DIM), BF16), pltpu.HBM((t, ng * SG_DIM), BF16),
                   jax.ShapeDtypeStruct((ng, 1, SG_DIM), F32), jax.ShapeDtypeStruct((ng, 1, SG_DIM), F32),
                   jax.ShapeDtypeStruct((ng, n, n), F32), jax.ShapeDtypeStruct((ng, n, 1), F32)],
        compiler_params=_params(2),
        name="sgu_bwd",
    )(proj, proj, dmix, ln_g, ln_b, w_s, b_row)


def _attn_bwd(dyb, wo, qb, kb, vb):
    t, d = dyb.shape
    m_len = kb.shape[0]
    tm = _row_tile(t)
    dh = d // X_HEADS
    scale = dh ** -0.5

    def body(dy_ref, wo_ref, q_ref, k_ref, v_ref, dq_ref, dk_ref, dv_ref):
        i = pl.program_id(0)

        @pl.when(i == 0)
        def _():
            dk_ref[...] = jnp.zeros_like(dk_ref)
            dv_ref[...] = jnp.zeros_like(dv_ref)

        do = _dot_nt(dy_ref[...], wo_ref[...]).astype(BF16)
        for hd in range(X_HEADS):
            sl = slice(hd * dh, (hd + 1) * dh)
            qh = q_ref[:, sl]
            p = _softmax_rows(_dot_nt(qh, k_ref[:, sl]) * scale)
            doh = do[:, sl]
            dp = _dot_nt(doh, v_ref[:, sl])
            ds = (p * (dp - jnp.sum(dp * p, axis=-1, keepdims=True)) * scale).astype(BF16)
            dq_ref[:, sl] = _dot(ds, k_ref[:, sl]).astype(BF16)
            dk_ref[:, sl] += _dot_tn(ds, qh)
            dv_ref[:, sl] += _dot_tn(p.astype(BF16), doh)

    row = pl.BlockSpec((tm, d), lambda i: (i, 0))
    full = lambda a: pl.BlockSpec(a.shape, lambda i: (0, 0))
    kv = pl.BlockSpec((m_len, d), lambda i: (0, 0))
    return pl.pallas_call(
        body,
        grid=(t // tm,),
        in_specs=[row, full(wo), row, full(kb), full(vb)],
        out_specs=[row, kv, kv],
        out_shape=[jax.ShapeDtypeStruct((t, d), BF16), jax.ShapeDtypeStruct((m_len, d), F32),
                   jax.ShapeDtypeStruct((m_len, d), F32)],
        compiler_params=_params(1),
        name="attn_bwd",
    )(dyb, wo, qb, kb, vb)


def _mem_bwd(dk, dv, mb, xhat, rstd, g, wk, wv):
    m_len, d = dk.shape

    def body(dk_ref, dv_ref, mb_ref, xh_ref, rs_ref, g_ref, wk_ref, wv_ref, gwk_ref, gwv_ref, dg_ref, db_ref):
        dkb = dk_ref[...].astype(BF16)
        dvb = dv_ref[...].astype(BF16)
        mb_v = mb_ref[...]
        gwk_ref[...] = _dot_tn(mb_v, dkb).astype(BF16)
        gwv_ref[...] = _dot_tn(mb_v, dvb).astype(BF16)
        dm = _dot_nt(dkb, wk_ref[...]) + _dot_nt(dvb, wv_ref[...])
        _, dg, db = _ln_bwd(dm, xh_ref[...], rs_ref[...], g_ref[...])
        dg_ref[...] = dg
        db_ref[...] = db

    return pl.pallas_call(
        body,
        out_shape=[jax.ShapeDtypeStruct((d, d), BF16), jax.ShapeDtypeStruct((d, d), BF16),
                   jax.ShapeDtypeStruct((1, d), F32), jax.ShapeDtypeStruct((1, d), F32)],
        compiler_params=pltpu.CompilerParams(vmem_limit_bytes=VMEM_LIMIT_V7X),
        name="mem_bwd",
    )(dk, dv, mb, xhat, rstd, g, wk, wv)


def _adamw(w, g, m, v):
    m = ADAM_B1 * m + (1.0 - ADAM_B1) * g
    v = ADAM_B2 * v + (1.0 - ADAM_B2) * (g * g)
    m_hat = m / (1.0 - ADAM_B1 ** ADAM_STEP)
    v_hat = v / (1.0 - ADAM_B2 ** ADAM_STEP)
    delta = -ADAM_LR * (m_hat / (jnp.sqrt(v_hat) + ADAM_EPS) + ADAM_WD * w)
    return delta, m, v


def _slot_sum(ref):
    g = ref[0].astype(F32)
    for s in range(1, N_DEV):
        g = g + ref[s].astype(F32)
    return g


def _adam_sharded(lands, w, m, v, axis, name):
    rows, cols = w.shape
    nl = len(lands)
    transposed = axis == 1 and nl == 2
    if transposed:
        rows, cols = cols, rows
        tr = 256
        grid = (rows // tr,)
        wblk = pl.BlockSpec((cols, tr), lambda i: (0, i))
        lblk = [pl.BlockSpec((N_DEV, tr, a.shape[2]), lambda i: (0, i, 0)) for a in lands]
    elif axis == 1:
        tr = 256 if rows % 256 == 0 else rows
        grid = (rows // tr,)
        wblk = pl.BlockSpec((tr, cols), lambda i: (i, 0))
        lblk = [pl.BlockSpec((N_DEV, tr, a.shape[2]), lambda i: (0, i, 0)) for a in lands]
    else:
        tc = _col_tile(cols)
        grid = (cols // tc,)
        wblk = pl.BlockSpec((rows, tc), lambda i: (0, i))
        lblk = [pl.BlockSpec((N_DEV, a.shape[1], tc), lambda i: (0, 0, i)) for a in lands]

    def body(*refs):
        w_ref, m_ref, v_ref = refs[nl:nl + 3]
        g_ref, d_ref, nm_ref, nv_ref = refs[nl + 3:]
        g = _slot_sum(refs[0])
        if nl == 2:
            tail = _slot_sum(refs[1])
            if transposed:
                g = jnp.concatenate([g.T, tail.T[:cols - g.shape[1], :]], axis=0)
            elif axis == 1:
                g = jnp.concatenate([g, tail[:, :cols - g.shape[1]]], axis=1)
            else:
                g = jnp.concatenate([g, tail[:rows - g.shape[0], :]], axis=0)
        delta, nm, nv = _adamw(w_ref[...], g, m_ref[...], v_ref[...])
        g_ref[...] = g
        d_ref[...] = delta
        nm_ref[...] = nm
        nv_ref[...] = nv

    shp = pltpu.HBM(w.shape, F32)
    return pl.pallas_call(
        body,
        grid=grid,
        in_specs=lblk + [wblk, wblk, wblk],
        out_specs=[wblk, wblk, wblk, wblk],
        out_shape=[shp, shp, shp, shp],
        compiler_params=_params(1),
        name=name,
    )(*[pltpu.with_memory_space_constraint(a, pltpu.HBM) for a in (*lands, w, m, v)])


def _mesh_pos():
    return lax.axis_index("x"), lax.axis_index("y"), lax.axis_index("c")


def _peer(k):
    x, y, c = _mesh_pos()
    pos = (x ^ (k >> 2), y ^ ((k >> 1) & 1), c ^ (k & 1))
    return pos, 4 * pos[0] + 2 * pos[1] + pos[2]


def _sem_index(row, k):
    return row * (N_DEV - 1) + k - 1


def _window(ref, axis, start, size):
    align = 16 if axis == 0 else LANES
    start = pl.multiple_of(start, align)
    return ref.at[pl.ds(start, size), :] if axis == 0 else ref.at[:, pl.ds(start, size)]


def _piece_refs(piece, srcs, lands, me, peer):
    kind, si, li, axis, base, stride, shape = piece
    if kind == "gather":
        return srcs[si], _window(lands[li], axis, base + stride * me, shape[axis])
    return _window(srcs[si], axis, base + stride * peer, shape[axis]), lands[li].at[me]


def _place_own(srcs, land_shapes, pieces, name):
    ns, nl, npc = len(srcs), len(land_shapes), len(pieces)

    def body(*refs):
        s_refs = refs[:ns]
        l_refs = refs[ns:ns + nl]
        bufs = refs[ns + nl:ns + nl + npc]
        sems = refs[ns + nl + npc]
        x, y, c = _mesh_pos()
        me = 4 * x + 2 * y + c
        loads = []
        for p, piece in enumerate(pieces):
            src, dst = _piece_refs(piece, s_refs, l_refs, me, me)
            cp = pltpu.make_async_copy(src, bufs[p], sems.at[0, p])
            cp.start()
            loads.append((cp, dst))
        stores = []
        for p, (cp, dst) in enumerate(loads):
            cp.wait()
            out = pltpu.make_async_copy(bufs[p], dst, sems.at[1, p])
            out.start()
            stores.append(out)
        for out in stores:
            out.wait()

    out = pl.pallas_call(
        body,
        in_specs=[ANY] * ns,
        out_specs=[HBM] * nl,
        out_shape=[pltpu.HBM(s.shape, s.dtype) for s in land_shapes],
        scratch_shapes=[pltpu.VMEM(pc[6], srcs[pc[1]].dtype) for pc in pieces] + [pltpu.SemaphoreType.DMA((2, npc))],
        compiler_params=pltpu.CompilerParams(vmem_limit_bytes=VMEM_LIMIT_V7X),
        name=name,
    )(*srcs)
    return list(out)


def _comm_start(srcs, lands, pieces, groups, name, after=(), fresh=False):
    ns, nl, na, ng, npc = len(srcs), len(lands), len(after), len(groups), len(pieces)
    given = [] if fresh else list(lands)

    def body(*refs):
        s_refs = refs[:ns]
        outs = refs[ns + len(given) + na:]
        sems = outs[:2 * ng]
        l_refs = outs[2 * ng + ns:2 * ng + ns + nl] if fresh else refs[ns:ns + nl]
        token = outs[2 * ng + ns + nl]
        x, y, c = _mesh_pos()
        me = 4 * x + 2 * y + c
        for g, members in enumerate(groups):
            for row, p in enumerate(members):
                for k in range(1, N_DEV):
                    pos, peer = _peer(k)
                    src, dst = _piece_refs(pieces[p], s_refs, l_refs, me, peer)
                    pltpu.make_async_remote_copy(src_ref=src, dst_ref=dst, send_sem=sems[2 * g].at[_sem_index(row, k)],
                                                 recv_sem=sems[2 * g + 1].at[_sem_index(row, k)], device_id=pos,
                                                 device_id_type=MESH_ID).start()
        if fresh:
            bufs, local = outs[-npc - 1:-1], outs[-1]
            loads = []
            for p, piece in enumerate(pieces):
                src, dst = _piece_refs(piece, s_refs, l_refs, me, me)
                loads.append((pltpu.make_async_copy(src, bufs[p], local.at[0, p]), dst))
                loads[-1][0].start()
            stores = []
            for p, (load, dst) in enumerate(loads):
                load.wait()
                stores.append(pltpu.make_async_copy(bufs[p], dst, local.at[1, p]))
                stores[-1].start()
            for store in stores:
                store.wait()
        token[...] = jnp.zeros_like(token)

    sem_shapes = []
    for members in groups:
        sem_shapes += [pltpu.SemaphoreType.DMA((len(members) * (N_DEV - 1),))] * 2
    hbm_of = lambda a: pltpu.HBM(a.shape, a.dtype)
    staging = [pltpu.VMEM(pc[6], srcs[pc[1]].dtype) for pc in pieces] + [pltpu.SemaphoreType.DMA((2, npc))]
    out = pl.pallas_call(
        body,
        in_specs=[HBM] * (ns + len(given)) + [ANY] * na,
        out_specs=[SEM] * (2 * ng) + [HBM] * (ns + nl) + [pl.BlockSpec(memory_space=pltpu.VMEM)],
        out_shape=sem_shapes + [hbm_of(a) for a in srcs] + [hbm_of(a) for a in lands]
        + [jax.ShapeDtypeStruct((8, LANES), F32)],
        input_output_aliases={i: 2 * ng + i for i in range(ns + len(given))},
        scratch_shapes=staging if fresh else [],
        compiler_params=pltpu.CompilerParams(has_side_effects=DATAFLOW, vmem_limit_bytes=VMEM_LIMIT_V7X),
        name=name,
    )(*[pltpu.with_memory_space_constraint(a, pltpu.HBM) for a in list(srcs) + given], *after)
    sems = [(out[2 * g], out[2 * g + 1]) for g in range(ng)]
    return sems, list(out[2 * ng:2 * ng + ns]), list(out[2 * ng + ns:2 * ng + ns + nl]), out[-1]


def _comm_wait(srcs, lands, pieces, members, sems, after, name):
    ns, nl, na = len(srcs), len(lands), len(after)

    def body(*refs):
        s_refs = refs[:ns]
        l_refs = refs[ns:ns + nl]
        send_sems, recv_sems = refs[ns + nl:ns + nl + 2]
        x, y, c = _mesh_pos()
        me = 4 * x + 2 * y + c
        for row, p in enumerate(members):
            for k in range(1, N_DEV):
                pos, peer = _peer(k)
                src, dst = _piece_refs(pieces[p], s_refs, l_refs, me, peer)
                cp = pltpu.make_async_remote_copy(src_ref=src, dst_ref=dst, send_sem=send_sems.at[_sem_index(row, k)],
                                                  recv_sem=recv_sems.at[_sem_index(row, k)], device_id=pos,
                                                  device_id_type=MESH_ID)
                cp.wait_send()
                cp.wait_recv()

    hbm_of = lambda a: pltpu.HBM(a.shape, a.dtype)
    out = pl.pallas_call(
        body,
        in_specs=[HBM] * (ns + nl) + [SEM, SEM] + [ANY] * na,
        out_specs=[HBM] * (ns + nl),
        out_shape=[hbm_of(a) for a in srcs] + [hbm_of(a) for a in lands],
        input_output_aliases={i: i for i in range(ns + nl)},
        compiler_params=pltpu.CompilerParams(has_side_effects=DATAFLOW),
        name=name,
    )(*srcs, *lands, sems[0], sems[1], *after)
    return list(out[ns:])


def _landed_block(piece, lands, owner):
    _, _, li, axis, base, stride, shape = piece
    return _window(lands[li], axis, base + stride * owner, shape[axis])


def _copy_stage(name, bufs, in_sems, out_sem_sizes, emit, after=()):
    nb, ni, no, na = len(bufs), len(in_sems), len(out_sem_sizes), len(after)

    def body(*refs):
        b_refs = refs[:nb]
        i_refs = refs[nb:nb + ni]
        o_refs = refs[nb + ni + na:nb + ni + na + no]
        emit(b_refs, i_refs, o_refs)
        refs[-1][...] = jnp.zeros_like(refs[-1])

    hbm_of = lambda a: pltpu.HBM(a.shape, a.dtype)
    out = pl.pallas_call(
        body,
        in_specs=[HBM] * nb + [SEM] * ni + [ANY] * na,
        out_specs=[SEM] * no + [HBM] * nb + [pl.BlockSpec(memory_space=pltpu.VMEM)],
        out_shape=[pltpu.SemaphoreType.DMA((n,)) for n in out_sem_sizes] + [hbm_of(a) for a in bufs]
        + [jax.ShapeDtypeStruct((8, LANES), F32)],
        input_output_aliases={i: no + i for i in range(nb)},
        compiler_params=pltpu.CompilerParams(has_side_effects=DATAFLOW),
        name=name,
    )(*[pltpu.with_memory_space_constraint(a, pltpu.HBM) for a in bufs], *in_sems, *after)
    return list(out[:no]), list(out[no:no + nb]), out[-1]


def _remote(src, dst, send, recv, to):
    return pltpu.make_async_remote_copy(src_ref=src, dst_ref=dst, send_sem=send, recv_sem=recv, device_id=to,
                                        device_id_type=MESH_ID)


def _routed_gather(srcs, lands, pieces, meanwhile, name):
    ns, npc = len(srcs), len(pieces)

    def places():
        x, y, c = _mesh_pos()
        index = lambda p: 4 * p[0] + 2 * p[1] + p[2]
        me, sib = (x, y, c), (x, y, 1 - c)
        xnb, ynb = (1 - x, y, c), (x, 1 - y, c)
        got_first = (x ^ (1 - c), y ^ c, c)
        pass_to = (x ^ c, y ^ (1 - c), c)
        diag = (1 - x, 1 - y, c)
        return index, me, sib, xnb, ynb, got_first, pass_to, diag

    def start(b, _, o):
        index, me, sib, xnb, ynb, *_rest = places()
        send_a, recv_sib, recv_nb = o
        for p, piece in enumerate(pieces):
            src, dst = _piece_refs(piece, b[:ns], b[ns:], index(me), 0)
            _remote(src, dst, send_a.at[3 * p], recv_sib.at[p], sib).start()
            _remote(src, dst, send_a.at[3 * p + 1], recv_nb.at[2 * p], xnb).start()
            _remote(src, dst, send_a.at[3 * p + 2], recv_nb.at[2 * p + 1], ynb).start()

    def pass_a(b, i, o):
        index, me, sib, xnb, ynb, got_first, pass_to, _diag = places()
        (recv_nb,) = i
        send_f, recv_f, send_d, recv_d = o
        for p, piece in enumerate(pieces):
            for j, nb in enumerate((xnb, ynb)):
                blk = _landed_block(piece, b, index(nb))
                _remote(blk, blk, send_f.at[2 * p + j], recv_nb.at[2 * p + j], sib).wait_recv()
                _remote(blk, blk, send_f.at[2 * p + j], recv_f.at[2 * p + j], sib).start()
            blk = _landed_block(piece, b, index(got_first))
            _remote(blk, blk, send_d.at[p], recv_d.at[p], pass_to).start()

    def pass_b(b, i, o):
        index, me, sib, *_mid, diag = places()
        (recv_d,) = i
        send_g, recv_g = o
        for p, piece in enumerate(pieces):
            blk = _landed_block(piece, b, index(diag))
            _remote(blk, blk, send_g.at[p], recv_d.at[p], sib).wait_recv()
            _remote(blk, blk, send_g.at[p], recv_g.at[p], sib).start()

    def last(b, i, _):
        index, me, sib, *_others = places()
        send_a, recv_sib, send_f, recv_f, send_d, send_g, recv_g = i
        for p, piece in enumerate(pieces):
            src, dst = _piece_refs(piece, b[:ns], b[ns:], index(me), 0)
            cp = lambda s_sem, r_sem: _remote(src, dst, s_sem, r_sem, sib)
            cp(send_a.at[3 * p], recv_sib.at[p]).wait_recv()
            cp(send_a.at[3 * p], recv_g.at[p]).wait_recv()
            for j in range(3):
                cp(send_a.at[3 * p + j], recv_sib.at[p]).wait_send()
            for j in range(2):
                cp(send_f.at[2 * p + j], recv_f.at[2 * p + j]).wait_recv()
                cp(send_f.at[2 * p + j], recv_f.at[2 * p + j]).wait_send()
            cp(send_d.at[p], recv_sib.at[p]).wait_send()
            cp(send_g.at[p], recv_sib.at[p]).wait_send()

    (send_a, recv_sib, recv_nb), bufs, started = _copy_stage(name + "_start", list(srcs) + list(lands), [],
                                                             [3 * npc, npc, 2 * npc], start)
    srcs, lands = bufs[:ns], bufs[ns:]
    (send_f, recv_f, send_d, recv_d), lands, _ = _copy_stage(name + "_pass_a", lands, [recv_nb],
                                                             [2 * npc, 2 * npc, npc, npc],
                                                             lambda b, i, o: pass_a(b, i, o), after=meanwhile(started))
    (send_g, recv_g), lands, tok = _copy_stage(name + "_pass_b", lands, [recv_d], [npc, npc], pass_b)
    _, bufs, _ = _copy_stage(name + "_last", list(srcs) + list(lands),
                             [send_a, recv_sib, send_f, recv_f, send_d, send_g, recv_g], [], last)
    return bufs[ns:], tok


def _paired_gather(srcs, lands, pieces, groups, after, name):
    ns, ng = len(srcs), len(groups)
    far = (1, 2, 3)

    def me_sib():
        x, y, c = _mesh_pos()
        return 4 * x + 2 * y + c, (x, y, 1 - c)

    def start(b, _, o):
        me, sib = me_sib()
        for g, members in enumerate(groups):
            send, recv_sib, recv_far = o[3 * g:3 * g + 3]
            for r, p in enumerate(members):
                src, dst = _piece_refs(pieces[p], b[:ns], b[ns:], me, 0)
                _remote(src, dst, send.at[4 * r], recv_sib.at[r], sib).start()
                for j in far:
                    _remote(src, dst, send.at[4 * r + j], recv_far.at[3 * r + j - 1], _peer(2 * j)[0]).start()

    def forward(b, i, o):
        _, sib = me_sib()
        for g, members in enumerate(groups):
            send_f, recv_f = o[2 * g:2 * g + 2]
            for r, p in enumerate(members):
                for j in far:
                    blk = _landed_block(pieces[p], b, _peer(2 * j)[1])
                    _remote(blk, blk, send_f.at[3 * r + j - 1], i[g].at[3 * r + j - 1], sib).wait_recv()
                    _remote(blk, blk, send_f.at[3 * r + j - 1], recv_f.at[3 * r + j - 1], sib).start()

    def last(sub):
        def emit(b, i, _):
            send, recv_sib, send_f, recv_f = i
            me, sib = me_sib()
            for r, piece in enumerate(sub):
                src, dst = _piece_refs(piece, b[:-1], b[-1:], me, 0)
                _remote(src, dst, send.at[4 * r], recv_sib.at[r], sib).wait_recv()
                for j in range(4):
                    _remote(src, dst, send.at[4 * r + j], recv_sib.at[r], sib).wait_send()
                for j in far:
                    blk = _landed_block(piece, b[-1:], _peer(2 * j + 1)[1])
                    _remote(blk, blk, send_f.at[3 * r + j - 1], recv_f.at[3 * r + j - 1], sib).wait_recv()
                    _remote(blk, blk, send_f.at[3 * r + j - 1], recv_f.at[3 * r + j - 1], sib).wait_send()
        return emit

    sizes = []
    for members in groups:
        sizes += [4 * len(members), len(members), 3 * len(members)]
    sems, bufs, started = _copy_stage(name + "_start", list(srcs) + list(lands), [], sizes, start, after=after)
    srcs = bufs[:ns]
    state = dict(lands=bufs[ns:])

    def finish(g, after):
        if "passed" not in state:
            sizes_f = []
            for members in groups:
                sizes_f += [3 * len(members)] * 2
            state["passed"], state["lands"], _ = _copy_stage(name + "_pass", state["lands"],
                                                             [sems[3 * k + 2] for k in range(ng)], sizes_f, forward,
                                                             after=after)
            after = ()
        members = groups[g]
        sub = [(pieces[p][0], r, 0) + pieces[p][3:] for r, p in enumerate(members)]
        _, bufs, _ = _copy_stage("%s_last_%d" % (name, g), [srcs[pieces[p][1]] for p in members] + [state["lands"][g]],
                                 [sems[3 * g], sems[3 * g + 1], state["passed"][2 * g], state["passed"][2 * g + 1]],
                                 [], last(sub), after=after)
        return bufs[-1]

    return finish, started


_SMALL_NAMES = ("ln1_g", "ln1_b", "hg_lb_logits", "hg_norm_g", "sg_ln_g", "sg_ln_b", "sg_w_s", "sg_b_s",
                "ln2_g", "ln2_b", "mem_ln_g", "mem_ln_b", "ln3_g", "ln3_b", "ln4_g", "ln4_b")


_VEC_NAMES = ("ln1_g", "ln1_b", "ln2_g", "ln2_b", "mem_ln_g", "mem_ln_b", "ln3_g", "ln3_b", "ln4_g", "ln4_b")
_ROW_NAMES = ("hg_lb_logits", "hg_norm_g", "sg_ln_g", "sg_ln_b", "sg_b_s", "sg_w_s")
VEC_ROWS = 16


def _row_plan(shapes):
    plan, pos = {}, 0
    for k in _ROW_NAMES:
        shp = shapes[k]
        slabs, off = [], pos
        for idx in itertools.product(*[range(dim) for dim in shp[:-2]]):
            slabs.append((idx, off, shp[-2]))
            off += shp[-2]
        plan[k] = (pos, slabs)
        pos = -(-off // 8) * 8
    return plan, -(-pos // 16) * 16


def _pack_small_grads(gs, shapes, loss):
    d = gs[_VEC_NAMES[0]].size
    vec = jnp.concatenate([gs[k].reshape(1, -1) for k in _VEC_NAMES] + [jnp.tile(loss, (1, d // LANES))], axis=0)
    vec = jnp.pad(vec, ((0, VEC_ROWS - vec.shape[0]), (0, 0)))
    plan, total = _row_plan(shapes)
    parts, pos = [], 0
    for k in _ROW_NAMES:
        first, slabs = plan[k]
        rows = gs[k].reshape(-1, LANES)
        end = slabs[-1][1] + slabs[-1][2]
        nxt = -(-end // 8) * 8
        parts.append(jnp.pad(rows, ((0, nxt - first - rows.shape[0]), (0, 0))))
        pos = nxt
    parts.append(jnp.zeros((total - pos, LANES), F32))
    return vec, jnp.concatenate(parts, axis=0)


def _adam_small(land_vec, land_rows, w, m, v):
    names = _VEC_NAMES + _ROW_NAMES
    n = len(names)
    shapes = {k: w[k].shape for k in names}
    plan, _ = _row_plan(shapes)

    def body(*refs):
        lv_ref, lr_ref = refs[:2]
        w_refs, m_refs, v_refs = refs[2:2 + n], refs[2 + n:2 + 2 * n], refs[2 + 2 * n:2 + 3 * n]
        outs = refs[2 + 3 * n:2 + 7 * n]
        loss_ref = refs[2 + 7 * n]
        gv_s, gr_s = refs[3 + 7 * n:]
        gv_s[...] = _slot_sum(lv_ref)
        gr_s[...] = _slot_sum(lr_ref)
        loss_ref[...] = gv_s[len(_VEC_NAMES):len(_VEC_NAMES) + 1, :LANES]
        for p, k in enumerate(names):
            if k in _VEC_NAMES:
                row = _VEC_NAMES.index(k)
                slabs = [((), None, None)]
            else:
                slabs = plan[k][1]
            for idx, off, rows in slabs:
                g = gv_s[row:row + 1, :] if off is None else gr_s[off:off + rows, :]
                sel = idx + (slice(None), slice(None))
                delta, nm, nv = _adamw(w_refs[p][sel], g, m_refs[p][sel], v_refs[p][sel])
                for o, val in zip(range(4), (g, delta, nm, nv)):
                    outs[o * n + p][sel] = val

    flat = lambda tree: [tree[k] for k in names]
    shp = [jax.ShapeDtypeStruct(shapes[k], F32) for k in names]
    out = pl.pallas_call(
        body,
        out_shape=shp * 4 + [jax.ShapeDtypeStruct((1, LANES), F32)],
        scratch_shapes=[pltpu.VMEM(land_vec.shape[1:], F32), pltpu.VMEM(land_rows.shape[1:], F32)],
        name="adam_small",
    )(land_vec, land_rows, *flat(w), *flat(m), *flat(v))
    return [dict(zip(names, out[o * n:(o + 1) * n])) for o in range(4)], out[4 * n]


_COL_FFN = ("ffn1_w_gate", "ffn1_w_up", "ffn2_w_gate", "ffn2_w_up")
_ROW_FFN = ("ffn1_w_down", "ffn2_w_down")
_ROW_SQ = ("w_out", "xa_w_q", "xa_w_k", "xa_w_v", "xa_w_o")
_BIG_NAMES = ("ffn1_w_gate", "ffn1_w_up", "ffn1_w_down", "w_in", "w_out", "xa_w_q", "xa_w_k", "xa_w_v", "xa_w_o",
              "ffn2_w_gate", "ffn2_w_up", "ffn2_w_down")


def _ffn_split(fs):
    main = (fs // MXU_WIDTH_V7X) * MXU_WIDTH_V7X
    tail = fs - main
    tail_pad = -(-tail // LANES) * LANES
    assert main > 0 and tail > 0
    return main, tail, tail_pad


def _layout(name, shard_shape):
    r, c = shard_shape
    if name in _COL_FFN:
        main, tail, pad = _ffn_split(c)
        return (r, N_DEV * (main + pad)), [(1, 0, main, (r, main), (0, main)),
                                           (1, N_DEV * main, pad, (r, pad), (main, c))]
    if name in _ROW_FFN:
        main, tail, pad = _ffn_split(r)
        return (N_DEV * (main + pad), c), [(0, 0, main, (main, c), (0, main)),
                                           (0, N_DEV * main, pad, (pad, c), (main, r))]
    if name == "w_in":
        return (r, N_DEV * c), [(1, 0, c, (r, c), (0, c))]
    return (N_DEV * r, c), [(0, 0, r, (r, c), (0, r))]


def _shard_pieces(name, shard):
    out = []
    for axis, _, _, shape, (lo, hi) in _layout(name, shard.shape)[1]:
        part = shard[lo:hi, :] if axis == 0 else shard[:, lo:hi]
        pad = [(0, shape[0] - part.shape[0]), (0, shape[1] - part.shape[1])]
        out.append(jnp.pad(part, pad).astype(BF16))
    return out


def _gather_plan(names, shards):
    srcs, land_shapes, pieces, index = [], [], [], {}
    for li, name in enumerate(names):
        shape2d, parts = _layout(name, shards[name].shape)
        land_shapes.append(jax.ShapeDtypeStruct(shape2d, BF16))
        index[name] = []
        for (axis, base, stride, shape, _), src in zip(parts, _shard_pieces(name, shards[name])):
            index[name].append(len(pieces))
            pieces.append(("gather", len(srcs), li, axis, base, stride, shape))
            srcs.append(src)
    return srcs, land_shapes, pieces, index


def _scatter_plan(names, grads, shard_shapes):
    srcs, land_shapes, pieces, index = [], [], [], {}
    for si, name in enumerate(names):
        _, parts = _layout(name, shard_shapes[name])
        srcs.append(grads[name])
        index[name] = []
        for axis, base, stride, shape, _ in parts:
            index[name].append(len(land_shapes))
            pieces.append(("scatter", si, len(land_shapes), axis, base, stride, shape))
            land_shapes.append(jax.ShapeDtypeStruct((N_DEV,) + shape, grads[name].dtype))
    return srcs, land_shapes, pieces, index


def _small_views(small):
    row = lambda a: a.reshape(1, -1)
    ln = {k: row(small[k]) for k in ("ln1_g", "ln1_b", "ln2_g", "ln2_b", "ln3_g", "ln3_b", "ln4_g", "ln4_b",
                                      "mem_ln_g", "mem_ln_b", "hg_norm_g")}
    sg_w = small["sg_w_s"].reshape(SG_GROUPS, SG_CHUNK, SG_CHUNK)
    sg = dict(logits=jnp.swapaxes(small["hg_lb_logits"], 0, 1),
              g=small["sg_ln_g"].reshape(SG_GROUPS, 1, SG_DIM), b=small["sg_ln_b"].reshape(SG_GROUPS, 1, SG_DIM),
              w=sg_w, bs=small["sg_b_s"].reshape(SG_GROUPS, 1, SG_CHUNK))
    return ln, sg


def _forward(x, xb, mem, target, get_w, small, first_deps=()):
    ln, sg = _small_views(small)
    a1, b1, s1 = _ffn_up(xb, get_w("ffn1_w_gate", ()), get_w("ffn1_w_up", ()), "ffn1_up", deps=first_deps)
    h1b, xh1, rs1 = _mm_res_ln(s1, get_w("ffn1_w_down", (s1,)), x, ln["ln1_g"], ln["ln1_b"], 0.5, "ffn1_down_ln")
    proj = _mm_nn(h1b, get_w("w_in", (h1b,)), "mix_in")
    oraw, mix, states = _hgrn_fwd(proj, sg["logits"], ln["hg_norm_g"])
    mix = _sgu_fwd(proj, mix, sg["g"], sg["b"], sg["w"], sg["bs"])
    h2b, xh2, rs2 = _mm_res_ln(mix, get_w("w_out", (mix,)), (xh1, ln["ln1_g"], ln["ln1_b"]), ln["ln2_g"], ln["ln2_b"],
                               1.0, "mix_out_ln")
    mb, mxh, mrs, kb, vb = _mem_kv(mem, ln["mem_ln_g"], ln["mem_ln_b"], get_w("xa_w_k", (h2b,)), get_w("xa_w_v", (h2b,)))
    qb, att = _attn_fwd(h2b, get_w("xa_w_q", (mrs,)), kb, vb)
    h3b, xh3, rs3 = _mm_res_ln(att, get_w("xa_w_o", (att,)), (xh2, ln["ln2_g"], ln["ln2_b"]), ln["ln3_g"], ln["ln3_b"],
                               1.0, "attn_out_ln")
    a2, b2, s2 = _ffn_up(h3b, get_w("ffn2_w_gate", (h3b,)), get_w("ffn2_w_up", (h3b,)), "ffn2_up")
    loss, dy4, dy4b, dg4, db4 = _mm_res_ln(s2, get_w("ffn2_w_down", (s2,)), (xh3, ln["ln3_g"], ln["ln3_b"]),
                                           ln["ln4_g"], ln["ln4_b"], 0.5, "ffn2_down_ln_loss", target=target)
    return dict(xb=xb, a1=a1, b1=b1, s1=s1, h1b=h1b, xh1=xh1, rs1=rs1, proj=proj, oraw=oraw, mix=mix, states=states,
                h2b=h2b, xh2=xh2, rs2=rs2, mb=mb, mxh=mxh, mrs=mrs, kb=kb, vb=vb, qb=qb, att=att, h3b=h3b, xh3=xh3,
                rs3=rs3, a2=a2, b2=b2, s2=s2, loss=loss, dy4=dy4, dy4b=dy4b, dg4=dg4, db4=db4)


def _backward(sv, wt, small, send):
    ln, sg = _small_views(small)
    gs = {"ln4_g": sv["dg4"], "ln4_b": sv["db4"]}
    loss, dy4, dy4b = sv["loss"], sv["dy4"], sv["dy4b"]
    g_down2 = _mm_tn(sv["s2"], dy4b, "g_ffn2_down", scale=0.5)
    da2, db2, dy3, dy3b, gs["ln3_g"], gs["ln3_b"] = _ffn_bwd_fused(
        dy4, dy4b, wt["ffn2_w_down"], wt["ffn2_w_gate"], wt["ffn2_w_up"], sv["a2"], sv["b2"], 0.5,
        (sv["xh3"], sv["rs3"], ln["ln3_g"]), "ffn2_bwd")
    g_gate2 = _mm_tn(sv["h3b"], da2, "g_ffn2_gate")
    g_up2 = _mm_tn(sv["h3b"], db2, "g_ffn2_up")
    tok = send(("ffn2_w_down", "ffn2_w_gate", "ffn2_w_up"), (g_down2, g_gate2, g_up2))

    g_o = _mm_tn(sv["att"], dy3b, "g_xa_o", deps=(tok,))
    dqb, dk, dv = _attn_bwd(dy3b, wt["xa_w_o"], sv["qb"], sv["kb"], sv["vb"])
    g_q = _mm_tn(sv["h2b"], dqb, "g_xa_q")
    g_k, g_v, gs["mem_ln_g"], gs["mem_ln_b"] = _mem_bwd(dk, dv, sv["mb"], sv["mxh"], sv["mrs"], ln["mem_ln_g"],
                                                        wt["xa_w_k"], wt["xa_w_v"])
    tok = send(("xa_w_o", "xa_w_q", "xa_w_k", "xa_w_v"), (g_o, g_q, g_k, g_v))
    dy2, dy2b, gs["ln2_g"], gs["ln2_b"] = _dx_ln(dy3, [(dqb, wt["xa_w_q"])], (sv["xh2"], sv["rs2"], ln["ln2_g"]),
                                                 "attn_dx_ln", deps=(tok,))

    g_out = _mm_tn(sv["mix"], dy2b, "g_w_out")
    dmix = _mm_nt(dy2b, wt["w_out"], "mix_out_bwd")
    dq, dfz, div, dgg, dlg, dgn = _hgrn_bwd(sv["proj"], sv["oraw"], dmix, sv["states"], sg["logits"], ln["hg_norm_g"])
    du, dvv, gs["sg_ln_g"], gs["sg_ln_b"], gs["sg_w_s"], gs["sg_b_s"] = _sgu_bwd(
        sv["proj"], dmix, sg["g"], sg["b"], sg["w"], sg["bs"])
    gs["hg_lb_logits"] = jnp.swapaxes(dlg, 0, 1)
    gs["hg_norm_g"] = jnp.sum(dgn, axis=0)
    dproj = [dq, dfz, div, dgg, du, dvv]
    g_in = _mm_tn(sv["h1b"], dproj, "g_w_in")
    tok = send(("w_out", "w_in"), (g_out, g_in))
    dy1, dy1b, gs["ln1_g"], gs["ln1_b"] = _dx_ln(dy2, [(dproj, wt["w_in"])], (sv["xh1"], sv["rs1"], ln["ln1_g"]),
                                                 "mix_dx_ln", deps=(tok,))

    g_down1 = _mm_tn(sv["s1"], dy1b, "g_ffn1_down", scale=0.5)
    tok = send(("ffn1_w_down",), (g_down1,))
    da1, db1 = _ffn_bwd_act(dy1b, wt["ffn1_w_down"], sv["a1"], sv["b1"], 0.5, "ffn1_bwd_act", deps=(tok,))
    g_gate1 = _mm_tn(sv["xb"], da1, "g_ffn1_gate")
    tok = send(("ffn1_w_gate",), (g_gate1,))
    g_up1 = _mm_tn(sv["xb"], db1, "g_ffn1_up", deps=(tok,))
    tok = send(("ffn1_w_up",), (g_up1,))
    grad_x = _dx_ln(dy1, [(da1, wt["ffn1_w_gate"]), (db1, wt["ffn1_w_up"])], None, "ffn1_dx", deps=(tok,))
    return loss, grad_x, gs


_WEIGHT_NAMES = ("ffn1_w_gate", "ffn1_w_up", "ffn1_w_down", "ln1_g", "ln1_b", "w_in", "hg_lb_logits", "hg_norm_g",
                 "sg_ln_g", "sg_ln_b", "sg_w_s", "sg_b_s", "w_out", "ln2_g", "ln2_b", "mem_ln_g", "mem_ln_b",
                 "xa_w_q", "xa_w_k", "xa_w_v", "xa_w_o", "ln3_g", "ln3_b", "ffn2_w_gate", "ffn2_w_up", "ffn2_w_down",
                 "ln4_g", "ln4_b")
_FIRST = ("ffn1_w_gate", "ffn1_w_up")
_SECOND = ("ffn1_w_down", "w_in")
_THIRD = ("w_out", "xa_w_k", "xa_w_v", "xa_w_q", "xa_w_o", "ffn2_w_gate", "ffn2_w_up", "ffn2_w_down")


def kernel(x, mem, ffn1_w_gate, ffn1_w_up, ffn1_w_down, ln1_g, ln1_b, w_in, hg_lb_logits, hg_norm_g, sg_ln_g, sg_ln_b, sg_w_s, sg_b_s, w_out, ln2_g, ln2_b, mem_ln_g, mem_ln_b, xa_w_q, xa_w_k, xa_w_v, xa_w_o, ln3_g, ln3_b, ffn2_w_gate, ffn2_w_up, ffn2_w_down, ln4_g, ln4_b, loss_target, m_ffn1_w_gate, m_ffn1_w_up, m_ffn1_w_down, m_ln1_g, m_ln1_b, m_w_in, m_hg_lb_logits, m_hg_norm_g, m_sg_ln_g, m_sg_ln_b, m_sg_w_s, m_sg_b_s, m_w_out, m_ln2_g, m_ln2_b, m_mem_ln_g, m_mem_ln_b, m_xa_w_q, m_xa_w_k, m_xa_w_v, m_xa_w_o, m_ln3_g, m_ln3_b, m_ffn2_w_gate, m_ffn2_w_up, m_ffn2_w_down, m_ln4_g, m_ln4_b, v_ffn1_w_gate, v_ffn1_w_up, v_ffn1_w_down, v_ln1_g, v_ln1_b, v_w_in, v_hg_lb_logits, v_hg_norm_g, v_sg_ln_g, v_sg_ln_b, v_sg_w_s, v_sg_b_s, v_w_out, v_ln2_g, v_ln2_b, v_mem_ln_g, v_mem_ln_b, v_xa_w_q, v_xa_w_k, v_xa_w_v, v_xa_w_o, v_ln3_g, v_ln3_b, v_ffn2_w_gate, v_ffn2_w_up, v_ffn2_w_down, v_ln4_g, v_ln4_b):
    args = dict(locals())
    w = {k: args[k] for k in _WEIGHT_NAMES}
    m = {k: args["m_" + k] for k in _WEIGHT_NAMES}
    v = {k: args["v_" + k] for k in _WEIGHT_NAMES}
    shards = {k: w[k][0] for k in _BIG_NAMES}
    shard_shapes = {k: shards[k].shape for k in _BIG_NAMES}
    small = {k: (w[k][0] if k != "hg_lb_logits" else w[k]) for k in _SMALL_NAMES}

    srcs1, shapes1, pieces1, idx1 = _gather_plan(_FIRST, shards)
    lands1 = _place_own(srcs1, shapes1, pieces1, "gather_first_own")
    prepared = {}

    def prepare_rest(started):
        later = {k: shards[k] + started[0, 0] for k in _SECOND + _THIRD}
        placed = ()
        for key, names in (("second", _SECOND), ("third", _THIRD)):
            srcs, shapes, pieces, idx = _gather_plan(names, later)
            lands = _place_own(srcs, shapes, pieces, "gather_%s_own" % key)
            prepared[key] = (srcs, lands, pieces, idx)
            placed += tuple(lands)
        prepared["xb"] = _to_bf16(x[0], "x_bf16", deps=(started,))
        return placed + (prepared["xb"],)

    lands1, tok1 = _routed_gather(srcs1, lands1, pieces1, prepare_rest, "gather_first")
    srcs_p, lands_p, pieces_p, idx_p = prepared["second"]
    finish_second, tok_p = _paired_gather(srcs_p, lands_p, pieces_p, [list(idx_p[k]) for k in _SECOND], (tok1,),
                                          "gather_second")
    srcs2, lands2, pieces2, idx2 = prepared["third"]
    groups2 = [list(idx2[k]) for k in _THIRD]
    sems2, srcs2, lands2, tok2 = _comm_start(srcs2, lands2, pieces2, groups2, "gather_third_start", after=(tok_p,))
    wt = dict(zip(_FIRST, lands1))
    pending = {k: gi for gi, k in enumerate(_THIRD)}

    def get_w(name, after):
        if name in _SECOND and name not in wt:
            wt[name] = finish_second(_SECOND.index(name), after)
        if name in pending:
            gi = pending.pop(name)
            si = [pieces2[p][1] for p in groups2[gi]]
            sub = [(pieces2[p][0], row, 0) + pieces2[p][3:] for row, p in enumerate(groups2[gi])]
            wt[name] = _comm_wait([srcs2[s] for s in si], [lands2[gi]], sub, list(range(len(sub))), sems2[gi],
                                  after, "gather_wait_" + name)[0]
        return wt[name]

    sv = _forward(x[0], prepared["xb"], mem[0], loss_target[0], get_w, small, first_deps=(tok2,))

    sent = []

    def send(names, grads):
        srcs, shapes, pieces, idx = _scatter_plan(names, dict(zip(names, grads)), shard_shapes)
        sems, srcs, lands, tok = _comm_start(srcs, shapes, pieces, [list(range(len(pieces)))],
                                             "grads_start_%d" % len(sent), fresh=True)
        sent.append((names, srcs, lands, pieces, idx, sems[0]))
        return tok

    loss, grad_x, gs = _backward(sv, wt, small, send)

    ssrc = list(_pack_small_grads(gs, {k: w[k].shape for k in _SMALL_NAMES}, loss))
    sp = [("scatter", i, i, 0, 0, 0, a.shape) for i, a in enumerate(ssrc)]
    sshape = [jax.ShapeDtypeStruct((N_DEV,) + a.shape, F32) for a in ssrc]
    ssem, ssrc, sl, _ = _comm_start(ssrc, sshape, sp, [[0, 1]], "small_start", fresh=True)

    out_g, out_d, out_m, out_v = {}, {}, {}, {}
    after = (grad_x,)
    for n_sent, (names, srcs, lands, pieces, idx, sems) in enumerate(sent):
        lands = _comm_wait(srcs, lands, pieces, list(range(len(pieces))), sems, after, "grads_wait_%d" % n_sent)
        for k in names:
            axis = 1 if (k in _COL_FFN or k == "w_in") else 0
            if k in _COL_FFN:
                done = _adam_sharded([lands[i] for i in idx[k]], w[k][0].T, m[k][0].T, v[k][0].T, axis, "adam_" + k)
                res = [r.T for r in done]
            else:
                res = done = _adam_sharded([lands[i] for i in idx[k]], w[k][0], m[k][0], v[k][0], axis, "adam_" + k)
            out_g[k], out_d[k], out_m[k], out_v[k] = [r[None] for r in res]
        after = (done[3],)
    sl = _comm_wait(ssrc, sl, sp, [0, 1], ssem[0], after, "small_wait")
    small_out, loss_sum = _adam_small(sl[0], sl[1], w, m, v)
    for dst, res in zip((out_g, out_d, out_m, out_v), small_out):
        dst.update(res)
    loss_all = loss_sum[0, 0]
    return (loss_all, grad_x[None], *[out_g[k] for k in _WEIGHT_NAMES], *[out_d[k] for k in _WEIGHT_NAMES],
            *[out_m[k] for k in _WEIGHT_NAMES], *[out_v[k] for k in _WEIGHT_NAMES])
```

```python
import itertools

import jax
import jax.numpy as jnp
import numpy as np
from jax import lax
from jax.experimental import pallas as pl
from jax.experimental.pallas import tpu as pltpu

F32 = jnp.float32
BF16 = jnp.bfloat16

N_DEV = 8
ALPHA = 2.0 ** 0.25
LN_EPS = 1e-5
HG_HEADS = 4
HG_DIM = 128
SG_GROUPS = 4
SG_DIM = 128
SG_CHUNK = 128
X_HEADS = 4
HG_BLOCK = 16
HG_UNROLL = 16
ADAM_LR = 0.001
ADAM_B1 = 0.9
ADAM_B2 = 0.999
ADAM_EPS = 1e-08
ADAM_WD = 0.01
ADAM_STEP = 10
VMEM_LIMIT_V7X = 48 * 1024 * 1024
MXU_WIDTH_V7X = 256
LANES = 128
MESH_ID = pl.DeviceIdType.MESH
ANY = pl.BlockSpec(memory_space=pl.ANY)
HBM = pl.BlockSpec(memory_space=pltpu.HBM)
SEM = pl.BlockSpec(memory_space=pltpu.SEMAPHORE)
DATAFLOW = pltpu.SideEffectType.DATAFLOW_SIDE_EFFECTING


def _params(n_axes):
    return pltpu.CompilerParams(dimension_semantics=("arbitrary",) * n_axes, vmem_limit_bytes=VMEM_LIMIT_V7X)


def _dot(a, b):
    return jnp.dot(a, b, preferred_element_type=F32)


def _dot_nt(a, b):
    return lax.dot_general(a, b, (((1,), (1,)), ((), ())), preferred_element_type=F32)


def _dot_tn(a, b):
    return lax.dot_general(a, b, (((0,), (0,)), ((), ())), preferred_element_type=F32)


def _sigmoid(x):
    return 1.0 / (1.0 + jnp.exp(-x))


def _silu_and_grad(a):
    sig = _sigmoid(a)
    return a * sig, sig * (1.0 + a * (1.0 - sig))


_GELU_C = 0.7978845608028654


def _gelu_and_grad(x):
    inner = _GELU_C * (x + 0.044715 * x * x * x)
    t = jnp.tanh(inner)
    val = 0.5 * x * (1.0 + t)
    grad = 0.5 * (1.0 + t) + 0.5 * x * (1.0 - t * t) * _GELU_C * (1.0 + 3.0 * 0.044715 * x * x)
    return val, grad


def _ln_fwd(y, g, b):
    mu = jnp.mean(y, axis=-1, keepdims=True)
    yc = y - mu
    var = jnp.mean(yc * yc, axis=-1, keepdims=True)
    rstd = lax.rsqrt(var + LN_EPS)
    xhat = yc * rstd
    return xhat * g + b, xhat, rstd


def _ln_bwd(dh, xhat, rstd, g):
    dxh = dh * g
    m1 = jnp.mean(dxh, axis=-1, keepdims=True)
    m2 = jnp.mean(dxh * xhat, axis=-1, keepdims=True)
    dy = rstd * (dxh - m1 - xhat * m2)
    dg = jnp.sum(dh * xhat, axis=0, keepdims=True)
    db = jnp.sum(dh, axis=0, keepdims=True)
    return dy, dg, db


def _mask_dot(mask, x):
    hi = x.astype(BF16)
    lo = (x - hi.astype(F32)).astype(BF16)
    n = mask.shape[0]
    parts = [_dot(mask, hi[r:r + n, :]) + _dot(mask, lo[r:r + n, :]) for r in range(0, x.shape[0], n)]
    return parts[0] if len(parts) == 1 else jnp.concatenate(parts, axis=0)


def _block_masks(n):
    r = np.arange(n)[:, None]
    c = np.arange(n)[None, :]
    same = (r // HG_BLOCK) == (c // HG_BLOCK)
    return jnp.asarray(np.stack([same & (c <= r), same & (c >= r), same]), BF16)


def _row_tile(t):
    return min(t, 512)


def _col_tile(n):
    for cand in (512, 256, 128):
        if n % cand == 0:
            return cand
    return n


def _resident(w):
    return pl.BlockSpec(w.shape, lambda *_: (0, 0), pipeline_mode=pl.Buffered(1))


def _drop_deps(body, n_in, n_deps):
    if n_deps == 0:
        return body
    return lambda *refs: body(*refs[:n_in], *refs[n_in + n_deps:])


def _to_bf16(x, name, deps=()):
    t, d = x.shape
    tm = _row_tile(t)

    def body(x_ref, o_ref):
        o_ref[...] = x_ref[...].astype(BF16)

    row = pl.BlockSpec((tm, d), lambda i: (i, 0))
    return pl.pallas_call(
        _drop_deps(body, 1, len(deps)),
        grid=(t // tm,),
        in_specs=[row] + [ANY] * len(deps),
        out_specs=row,
        out_shape=jax.ShapeDtypeStruct((t, d), BF16),
        compiler_params=_params(1),
        name=name,
    )(x, *deps)


def _ffn_up(hb, wg, wu, name, deps=()):
    t, d = hb.shape
    f = wg.shape[1]
    tm = _row_tile(t)
    tn = _col_tile(f)

    def body(h_ref, wg_ref, wu_ref, a_ref, b_ref, s_ref):
        h = h_ref[...]
        for c in range(f // tn):
            cols = slice(c * tn, (c + 1) * tn)
            a = _dot(h, wg_ref[:, cols])
            b = _dot(h, wu_ref[:, cols])
            a_ref[:, cols] = a.astype(BF16)
            b_ref[:, cols] = b.astype(BF16)
            s_ref[:, cols] = (a * _sigmoid(a) * b).astype(BF16)

    act = pl.BlockSpec((tm, f), lambda i: (i, 0))
    return pl.pallas_call(
        _drop_deps(body, 3, len(deps)),
        grid=(t // tm,),
        in_specs=[pl.BlockSpec((tm, d), lambda i: (i, 0)), _resident(wg), _resident(wu)] + [ANY] * len(deps),
        out_specs=[act, act, act],
        out_shape=[jax.ShapeDtypeStruct((t, f), BF16)] * 3,
        compiler_params=_params(1),
        name=name,
    )(hb, wg, wu, *deps)


def _mm_res_ln(lhs, w, res, g, b, coef, name, target=None):
    t, kd = lhs.shape
    d = w.shape[1]
    tm = _row_tile(t)
    nt = t // tm
    from_norm = isinstance(res, tuple)
    n_res = 3 if from_norm else 1

    def body(*refs):
        l_ref, w_ref = refs[:2]
        r_refs = refs[2:2 + n_res]
        g_ref, b_ref = refs[2 + n_res:4 + n_res]
        rest = refs[4 + n_res:]
        prev = r_refs[0][...] * r_refs[1][...] + r_refs[2][...] if from_norm else r_refs[0][...]
        y = ALPHA * prev + coef * _dot(l_ref[...], w_ref[...])
        h, xhat, rstd = _ln_fwd(y, g_ref[...], b_ref[...])
        if target is None:
            hb_ref, xh_ref, rs_ref = rest
            hb_ref[...] = h.astype(BF16)
            xh_ref[...] = xhat
            rs_ref[...] = rstd
            return
        t_ref, loss_ref, dy_ref, dyb_ref, dg_ref, db_ref, lacc = rest
        i = pl.program_id(0)

        @pl.when(i == 0)
        def _():
            lacc[...] = jnp.zeros_like(lacc)
            dg_ref[...] = jnp.zeros_like(dg_ref)
            db_ref[...] = jnp.zeros_like(db_ref)

        err = h - t_ref[...]
        lacc[...] += jnp.sum(err * err, axis=0, keepdims=True)
        dy, dg, db = _ln_bwd(err * (1.0 / d), xhat, rstd, g_ref[...])
        dy_ref[...] = dy
        dyb_ref[...] = dy.astype(BF16)
        dg_ref[...] += dg
        db_ref[...] += db

        @pl.when(i == nt - 1)
        def _():
            loss_ref[...] = jnp.zeros_like(loss_ref) + jnp.sum(lacc[...], axis=1, keepdims=True) * (0.5 / d)

    row = pl.BlockSpec((tm, d), lambda i: (i, 0))
    vec = pl.BlockSpec((1, d), lambda i: (0, 0))
    res_specs = [row, vec, vec] if from_norm else [row]
    res_args = list(res) if from_norm else [res]
    in_specs = [pl.BlockSpec((tm, kd), lambda i: (i, 0)), _resident(w)] + res_specs + [vec, vec]
    args = [lhs, w] + res_args + [g, b]
    if target is None:
        out_specs = [row, row, pl.BlockSpec((tm, 1), lambda i: (i, 0))]
        out_shape = [jax.ShapeDtypeStruct((t, d), BF16), jax.ShapeDtypeStruct((t, d), F32), pltpu.HBM((t, 1), F32)]
        scratch = []
    else:
        in_specs.append(row)
        args.append(target)
        out_specs = [pl.BlockSpec((1, LANES), lambda i: (0, 0)), row, row, vec, vec]
        out_shape = [jax.ShapeDtypeStruct((1, LANES), F32), jax.ShapeDtypeStruct((t, d), F32),
                     jax.ShapeDtypeStruct((t, d), BF16), jax.ShapeDtypeStruct((1, d), F32),
                     jax.ShapeDtypeStruct((1, d), F32)]
        scratch = [pltpu.VMEM((1, d), F32)]
    return pl.pallas_call(
        body,
        grid=(nt,),
        in_specs=in_specs,
        out_specs=out_specs,
        out_shape=out_shape,
        scratch_shapes=scratch,
        compiler_params=_params(1),
        name=name,
    )(*args)


def _store_slabs(o_ref, first, tile):
    for s in range(tile.shape[1] // LANES):
        o_ref[first + s] = tile[:, s * LANES:(s + 1) * LANES]


def _mm_nn(lhs, w, name):
    t, kd = lhs.shape
    n = w.shape[1]
    tm = _row_tile(t)
    tn = _col_tile(n)

    def body(l_ref, w_ref, o_ref):
        lhs_v = l_ref[...]
        for c in range(n // tn):
            _store_slabs(o_ref, c * (tn // LANES), _dot(lhs_v, w_ref[:, c * tn:(c + 1) * tn]))

    return pl.pallas_call(
        body,
        grid=(t // tm,),
        in_specs=[pl.BlockSpec((tm, kd), lambda i: (i, 0)), _resident(w)],
        out_specs=pl.BlockSpec((n // LANES, tm, LANES), lambda i: (0, i, 0)),
        out_shape=jax.ShapeDtypeStruct((n // LANES, t, LANES), F32),
        compiler_params=_params(1),
        name=name,
    )(lhs, w)


def _lower_bound(lg):
    m = jnp.max(lg, axis=0, keepdims=True)
    e = jnp.exp(lg - m)
    return e[0:1, :] / jnp.sum(e, axis=0, keepdims=True)


def _forget_terms(fz, lb):
    e = jnp.exp(-jnp.abs(fz))
    r = 1.0 / (1.0 + e)
    pos = fz >= 0.0
    sig = jnp.where(pos, r, e * r)
    nsig = jnp.where(pos, e * r, r)
    f = lb + (1.0 - lb) * sig
    k = (1.0 - lb) * nsig
    return sig, nsig, f, k


def _hg_tile(t):
    return min(t, 1024)


HG_HALF = HG_BLOCK // 2
NEG_BIG = -1e30


def _halves(a):
    return a[:HG_HALF, :], a[HG_HALF:, :]


def _causal_halves(s):
    return (0, 1) if s < HG_HALF else (1,)


def _decay_from(b_half, b_s, s, h, tidx):
    first = s - h * HG_HALF
    diff = b_half - b_s
    if first > 0:
        diff = jnp.where(tidx >= first, diff, NEG_BIG)
    return jnp.exp(diff)


def _hgrn_fwd(proj, logits, gn):
    t = proj.shape[1]
    ct = _hg_tile(t)
    nct = t // ct
    nblk = ct // HG_BLOCK
    nh = HG_HEADS
    mrows = min(ct, 256)

    def body(q_ref, fz_ref, iv_ref, gg_ref, lg_ref, gn_ref, mask_ref, oraw_ref, oa_ref, st_ref,
             state, qt_s, kt_s, k_s, b_s, dec_s):
        c = pl.program_id(1)

        @pl.when(c == 0)
        def _():
            state[...] = jnp.zeros_like(state)

        lb = _lower_bound(lg_ref[...])
        q = q_ref[...]
        _, _, f, k = _forget_terms(fz_ref[...], lb)
        logf = jnp.log(f)
        b = _mask_dot(mask_ref[0], logf)
        bend = _mask_dot(mask_ref[2], logf)
        qt_s[...] = (q * jnp.exp(b)).astype(BF16)
        kt_s[...] = (k * jnp.exp(bend - b)).astype(BF16)
        k_s[...] = k
        b_s[...] = b
        dec_s[...] = jnp.exp(bend)
        tidx = lax.broadcasted_iota(jnp.int32, (HG_HALF, HG_DIM), 0)

        def blk(i, carry):
            r0 = pl.multiple_of(i * HG_BLOCK, HG_BLOCK)
            rows = pl.ds(r0, HG_BLOCK)
            st = state[...]
            stb = st.astype(BF16)
            st_ref[i] = stb
            v = iv_ref[rows, :]
            qq = q_ref[rows, :]
            kk = k_s[rows, :]
            bb = b_s[rows, :]
            o = list(_halves(_dot_nt(qt_s[rows, :], stb)))
            qh, bh = _halves(qq), _halves(bb)
            for s in range(HG_BLOCK):
                ks, vs = kk[s:s + 1, :], v[s:s + 1, :]
                for h in _causal_halves(s):
                    e = _decay_from(bh[h], bb[s:s + 1, :], s, h, tidx)
                    acol = jnp.sum(qh[h] * (ks * e), axis=1, keepdims=True)
                    o[h] = o[h] + acol * vs
            oraw_ref[rows, :] = jnp.concatenate(o, axis=0)
            state[...] = st * dec_s[pl.ds(r0, 1), :] + _dot_tn(v.astype(BF16), kt_s[rows, :])
            return carry

        lax.fori_loop(0, nblk, blk, 0, unroll=HG_UNROLL)
        oraw = oraw_ref[...]
        r = lax.rsqrt(jnp.mean(oraw * oraw, axis=-1, keepdims=True) + LN_EPS)
        gg = gg_ref[...]
        oa_ref[...] = (oraw * r * gn_ref[...] * gg * _sigmoid(gg)).astype(BF16)

    def slab(off):
        return pl.BlockSpec((None, ct, HG_DIM), lambda h, c: (off + h, c, 0))

    return pl.pallas_call(
        body,
        grid=(nh, nct),
        in_specs=[slab(0), slab(nh), slab(2 * nh), slab(3 * nh),
                  pl.BlockSpec((None, 2, HG_DIM), lambda h, c: (h, 0, 0)),
                  pl.BlockSpec((1, HG_DIM), lambda h, c: (0, 0)),
                  pl.BlockSpec((3, mrows, mrows), lambda h, c: (0, 0, 0))],
        out_specs=[slab(0), pl.BlockSpec((ct, HG_DIM), lambda h, c: (c, h)),
                   pl.BlockSpec((None, nblk, HG_DIM, HG_DIM), lambda h, c: (h, c, 0, 0))],
        out_shape=[jax.ShapeDtypeStruct((nh, t, HG_DIM), F32),
                   jax.ShapeDtypeStruct((t, (nh + SG_GROUPS) * HG_DIM), BF16),
                   jax.ShapeDtypeStruct((nh, t // HG_BLOCK, HG_DIM, HG_DIM), BF16)],
        scratch_shapes=[pltpu.VMEM((HG_DIM, HG_DIM), F32), pltpu.VMEM((ct, HG_DIM), BF16),
                        pltpu.VMEM((ct, HG_DIM), BF16), pltpu.VMEM((ct, HG_DIM), F32),
                        pltpu.VMEM((ct, HG_DIM), F32), pltpu.VMEM((ct, HG_DIM), F32)],
        compiler_params=_params(2),
        name="hgrn_fwd",
    )(proj, proj, proj, proj, logits, gn, _block_masks(mrows))


def _sg_tile(t):
    return min(t, 512)


def _sgu_chunk_fwd(u, v, ln_g, ln_b, wm, bs):
    ua, dua = _gelu_and_grad(u)
    va, dva = _gelu_and_grad(v)
    vn, xhat, rstd = _ln_fwd(va, ln_g, ln_b)
    s = _dot(wm, vn.astype(BF16)) + bs
    return ua, dua, dva, vn, xhat, rstd, s


def _tril_weight(w):
    n = SG_CHUNK
    r = lax.broadcasted_iota(jnp.int32, (n, n), 0)
    c = lax.broadcasted_iota(jnp.int32, (n, n), 1)
    return jnp.where(c <= r, w, 0.0)


def _bias_by_position(b_row):
    return jnp.broadcast_to(b_row, (SG_CHUNK, SG_CHUNK)).T


def _sgu_fwd(proj, mix, ln_g, ln_b, w_s, b_row):
    t = proj.shape[1]
    ct = _sg_tile(t)
    ng = SG_GROUPS
    wide = ng * SG_DIM
    blk_u = 4 * HG_HEADS // ng

    def body(u_ref, v_ref, g_ref, b_ref, w_ref, bs_ref, mix_ref, o_ref):
        del mix_ref
        for g in range(ng):
            lanes = slice(g * SG_DIM, (g + 1) * SG_DIM)
            wm = _tril_weight(w_ref[g]).astype(BF16)
            bs = _bias_by_position(bs_ref[g])
            for n in range(ct // SG_CHUNK):
                rows = slice(n * SG_CHUNK, (n + 1) * SG_CHUNK)
                ua, _, _, _, _, _, s = _sgu_chunk_fwd(u_ref[g, rows, :], v_ref[g, rows, :], g_ref[g], b_ref[g], wm, bs)
                o_ref[rows, lanes] = (ua * s).astype(BF16)

    full = lambda a: pl.BlockSpec(a.shape, lambda c: (0,) * a.ndim)
    return pl.pallas_call(
        body,
        grid=(t // ct,),
        in_specs=[pl.BlockSpec((ng, ct, SG_DIM), lambda c: (blk_u, c, 0)),
                  pl.BlockSpec((ng, ct, SG_DIM), lambda c: (blk_u + 1, c, 0)),
                  full(ln_g), full(ln_b), full(w_s), full(b_row), ANY],
        out_specs=pl.BlockSpec((ct, wide), lambda c: (c, 1)),
        out_shape=jax.ShapeDtypeStruct(mix.shape, mix.dtype),
        input_output_aliases={6: 0},
        compiler_params=_params(1),
        name="sgu_fwd",
    )(proj, proj, ln_g, ln_b, w_s, b_row, mix)


def _mem_kv(mem, g, b, wk, wv):
    m_len, d = mem.shape

    def body(m_ref, g_ref, b_ref, wk_ref, wv_ref, mb_ref, xh_ref, rs_ref, k_ref, v_ref):
        m, xhat, rstd = _ln_fwd(m_ref[...], g_ref[...], b_ref[...])
        mb = m.astype(BF16)
        mb_ref[...] = mb
        xh_ref[...] = xhat
        rs_ref[...] = rstd
        k_ref[...] = _dot(mb, wk_ref[...]).astype(BF16)
        v_ref[...] = _dot(mb, wv_ref[...]).astype(BF16)

    return pl.pallas_call(
        body,
        out_shape=[jax.ShapeDtypeStruct((m_len, d), BF16), jax.ShapeDtypeStruct((m_len, d), F32),
                   jax.ShapeDtypeStruct((m_len, 1), F32), jax.ShapeDtypeStruct((m_len, d), BF16),
                   jax.ShapeDtypeStruct((m_len, d), BF16)],
        compiler_params=pltpu.CompilerParams(vmem_limit_bytes=VMEM_LIMIT_V7X),
        name="mem_kv",
    )(mem, g, b, wk, wv)


def _softmax_rows(s):
    m = jnp.max(s, axis=-1, keepdims=True)
    p = jnp.exp(s - m)
    return p / jnp.sum(p, axis=-1, keepdims=True)


def _attn_fwd(hb, wq, kb, vb):
    t, d = hb.shape
    tm = _row_tile(t)
    dh = d // X_HEADS
    scale = dh ** -0.5

    def body(h_ref, wq_ref, k_ref, v_ref, q_ref, o_ref):
        q = _dot(h_ref[...], wq_ref[...]).astype(BF16)
        q_ref[...] = q
        for hd in range(X_HEADS):
            sl = slice(hd * dh, (hd + 1) * dh)
            p = _softmax_rows(_dot_nt(q[:, sl], k_ref[:, sl]) * scale)
            o_ref[:, sl] = _dot(p.astype(BF16), v_ref[:, sl]).astype(BF16)

    row = pl.BlockSpec((tm, d), lambda i: (i, 0))
    full = lambda a: pl.BlockSpec(a.shape, lambda i: (0, 0))
    return pl.pallas_call(
        body,
        grid=(t // tm,),
        in_specs=[row, full(wq), full(kb), full(vb)],
        out_specs=[row, row],
        out_shape=[jax.ShapeDtypeStruct((t, d), BF16), jax.ShapeDtypeStruct((t, d), BF16)],
        compiler_params=_params(1),
        name="attn_fwd",
    )(hb, wq, kb, vb)


def _ffn_bwd_act(dyb, wd, a, b, coef, name, deps=()):
    t, d = dyb.shape
    f = wd.shape[0]
    tm = _row_tile(t)
    tn = _col_tile(f)

    def body(dy_ref, wd_ref, a_ref, b_ref, da_ref, db_ref):
        dy = dy_ref[...]
        for c in range(f // tn):
            cols = slice(c * tn, (c + 1) * tn)
            ds = _dot_nt(dy, wd_ref[cols, :]) * coef
            silu, dsilu = _silu_and_grad(a_ref[:, cols].astype(F32))
            da_ref[:, cols] = (ds * b_ref[:, cols].astype(F32) * dsilu).astype(BF16)
            db_ref[:, cols] = (ds * silu).astype(BF16)

    act = pl.BlockSpec((tm, f), lambda i: (i, 0))
    return pl.pallas_call(
        _drop_deps(body, 4, len(deps)),
        grid=(t // tm,),
        in_specs=[pl.BlockSpec((tm, d), lambda i: (i, 0)), _resident(wd), act, act] + [ANY] * len(deps),
        out_specs=[act, act],
        out_shape=[jax.ShapeDtypeStruct((t, f), BF16), jax.ShapeDtypeStruct((t, f), BF16)],
        compiler_params=_params(1),
        name=name,
    )(dyb, wd, a, b, *deps)


def _ffn_bwd_fused(dy, dyb, wd, wg, wu, a, b, coef, ln, name):
    t, d = dy.shape
    f = wd.shape[0]
    tm = min(t, 256)
    tn = _col_tile(f)

    def body(dy_ref, dyb_ref, wd_ref, wg_ref, wu_ref, a_ref, b_ref, xh_ref, rs_ref, g_ref,
             da_ref, db_ref, dyo_ref, dyob_ref, dg_ref, dbl_ref):
        dyb_v = dyb_ref[...]
        dh = ALPHA * dy_ref[...]
        for c in range(f // tn):
            cols = slice(c * tn, (c + 1) * tn)
            ds = _dot_nt(dyb_v, wd_ref[cols, :]) * coef
            silu, dsilu = _silu_and_grad(a_ref[:, cols].astype(F32))
            da = (ds * b_ref[:, cols].astype(F32) * dsilu).astype(BF16)
            db = (ds * silu).astype(BF16)
            da_ref[:, cols] = da
            db_ref[:, cols] = db
            dh = dh + _dot_nt(da, wg_ref[:, cols]) + _dot_nt(db, wu_ref[:, cols])

        @pl.when(pl.program_id(0) == 0)
        def _():
            dg_ref[...] = jnp.zeros_like(dg_ref)
            dbl_ref[...] = jnp.zeros_like(dbl_ref)

        dyp, dg, dbl = _ln_bwd(dh, xh_ref[...], rs_ref[...], g_ref[...])
        dyo_ref[...] = dyp
        dyob_ref[...] = dyp.astype(BF16)
        dg_ref[...] += dg
        dbl_ref[...] += dbl

    row = pl.BlockSpec((tm, d), lambda i: (i, 0))
    act = pl.BlockSpec((tm, f), lambda i: (i, 0))
    vec = pl.BlockSpec((1, d), lambda i: (0, 0))
    return pl.pallas_call(
        body,
        grid=(t // tm,),
        in_specs=[row, row, _resident(wd), _resident(wg), _resident(wu), act, act, row,
                  pl.BlockSpec((tm, 1), lambda i: (i, 0)), vec],
        out_specs=[act, act, row, row, vec, vec],
        out_shape=[jax.ShapeDtypeStruct((t, f), BF16), jax.ShapeDtypeStruct((t, f), BF16),
                   jax.ShapeDtypeStruct((t, d), F32), jax.ShapeDtypeStruct((t, d), BF16),
                   jax.ShapeDtypeStruct((1, d), F32), jax.ShapeDtypeStruct((1, d), F32)],
        compiler_params=_params(1),
        name=name,
    )(dy, dyb, wd, wg, wu, a, b, *ln)


def _mm_tn(a, b, name, scale=1.0, deps=()):
    t, m = a.shape
    bs = list(b) if isinstance(b, (list, tuple)) else [b]
    n = sum(piece.shape[1] for piece in bs)
    tt = _row_tile(t)
    nt = t // tt
    tm_o, tn_o = m, n

    def body(a_ref, *refs):
        b_refs, (o_ref, acc) = refs[:len(bs)], refs[len(bs):]
        k = pl.program_id(2)

        @pl.when(k == 0)
        def _():
            acc[...] = jnp.zeros_like(acc)

        first = 0
        for b_ref in b_refs:
            cols = slice(first, first + b_ref.shape[1])
            acc[:, cols] += _dot_tn(a_ref[...], b_ref[...])
            first = cols.stop

        @pl.when(k == nt - 1)
        def _():
            o_ref[...] = (acc[...] * scale).astype(BF16)

    return pl.pallas_call(
        _drop_deps(body, 1 + len(bs), len(deps)),
        grid=(m // tm_o, n // tn_o, nt),
        in_specs=[pl.BlockSpec((tt, tm_o), lambda i, j, k: (k, i))]
        + [pl.BlockSpec((tt, piece.shape[1]), lambda i, j, k: (k, j)) for piece in bs] + [ANY] * len(deps),
        out_specs=pl.BlockSpec((tm_o, tn_o), lambda i, j, k: (i, j)),
        out_shape=pltpu.HBM((m, n), BF16),
        scratch_shapes=[pltpu.VMEM((tm_o, tn_o), F32)],
        compiler_params=_params(3),
        name=name,
    )(a, *bs, *deps)


def _mm_nt(lhs, w, name):
    t, d = lhs.shape
    kd = w.shape[0]
    tm = _row_tile(t)

    def body(l_ref, w_ref, o_ref):
        _store_slabs(o_ref, 0, _dot_nt(l_ref[...], w_ref[...]))

    return pl.pallas_call(
        body,
        grid=(t // tm,),
        in_specs=[pl.BlockSpec((tm, d), lambda i: (i, 0)), _resident(w)],
        out_specs=pl.BlockSpec((kd // LANES, tm, LANES), lambda i: (0, i, 0)),
        out_shape=jax.ShapeDtypeStruct((kd // LANES, t, LANES), F32),
        compiler_params=_params(1),
        name=name,
    )(lhs, w)


def _dx_ln(dy, pairs, ln, name, deps=()):
    t, d = dy.shape
    npair = len(pairs)
    pairs = [(list(lhs) if isinstance(lhs, (list, tuple)) else [lhs], w) for lhs, w in pairs]
    tm = _row_tile(t)
    nt = t // tm
    n_in = 1 + sum(len(pieces) + 1 for pieces, _ in pairs) + (3 if ln is not None else 0)

    def body(*refs):
        dy_ref = refs[0]
        pos = 1
        dh = ALPHA * dy_ref[...]
        for pieces, _ in pairs:
            w_ref = refs[pos + len(pieces)]
            first = 0
            for l_ref in refs[pos:pos + len(pieces)]:
                cols = slice(first, first + l_ref.shape[1])
                dh = dh + _dot_nt(l_ref[...], w_ref[:, cols])
                first = cols.stop
            pos += len(pieces) + 1
        if ln is not None:
            xh_ref, rs_ref, g_ref = refs[pos:pos + 3]
            dyo_ref, dyb_ref, dg_ref, db_ref = refs[pos + 3:pos + 7]

            @pl.when(pl.program_id(0) == 0)
            def _():
                dg_ref[...] = jnp.zeros_like(dg_ref)
                db_ref[...] = jnp.zeros_like(db_ref)

            dyp, dg, db = _ln_bwd(dh, xh_ref[...], rs_ref[...], g_ref[...])
            dyo_ref[...] = dyp
            dyb_ref[...] = dyp.astype(BF16)
            dg_ref[...] += dg
            db_ref[...] += db
        else:
            refs[pos][...] = dh

    row = pl.BlockSpec((tm, d), lambda i: (i, 0))
    vec = pl.BlockSpec((1, d), lambda i: (0, 0))
    in_specs = [row]
    args = [dy]
    for pieces, w in pairs:
        in_specs += [pl.BlockSpec((tm, piece.shape[1]), lambda i: (i, 0)) for piece in pieces] + [_resident(w)]
        args += pieces + [w]
    if ln is not None:
        in_specs += [row, pl.BlockSpec((tm, 1), lambda i: (i, 0)), vec]
        args += list(ln)
        out_specs = [row, row, vec, vec]
        out_shape = [jax.ShapeDtypeStruct((t, d), F32), jax.ShapeDtypeStruct((t, d), BF16),
                     jax.ShapeDtypeStruct((1, d), F32), jax.ShapeDtypeStruct((1, d), F32)]
    else:
        out_specs = row
        out_shape = jax.ShapeDtypeStruct((t, d), F32)
    return pl.pallas_call(
        _drop_deps(body, n_in, len(deps)),
        grid=(nt,),
        in_specs=in_specs + [ANY] * len(deps),
        out_specs=out_specs,
        out_shape=out_shape,
        compiler_params=_params(1),
        name=name,
    )(*args, *deps)


def _hgrn_bwd(proj, oraw, dmix, states, logits, gn):
    t = proj.shape[1]
    ct = _hg_tile(t)
    nct = t // ct
    nblk = ct // HG_BLOCK
    nh = HG_HEADS
    mrows = min(ct, 256)

    def body(q_ref, fz_ref, iv_ref, gg_ref, or_ref, do_ref, st_ref, lg_ref, gn_ref, mask_ref,
             dq_ref, dfz_ref, div_ref, dgg_ref, dlg_ref, dgn_ref,
             dstate, qt_s, kt_s, k_s, b_s, eb_s, ekb_s, dec_s, dor_s, dbl_s, gr_s, dk_s, dlb_acc):
        c = pl.program_id(1)

        @pl.when(c == 0)
        def _():
            dstate[...] = jnp.zeros_like(dstate)
            dlb_acc[...] = jnp.zeros_like(dlb_acc)
            dgn_ref[...] = jnp.zeros_like(dgn_ref)

        lb = _lower_bound(lg_ref[...])
        q = q_ref[...]
        sig, nsig, f, k = _forget_terms(fz_ref[...], lb)
        logf = jnp.log(f)
        b = _mask_dot(mask_ref[0], logf)
        bend = _mask_dot(mask_ref[2], logf)
        eb = jnp.exp(b)
        ekb = jnp.exp(bend - b)
        qt_s[...] = (q * eb).astype(BF16)
        kt_s[...] = (k * ekb).astype(BF16)
        k_s[...] = k
        b_s[...] = b
        eb_s[...] = eb
        ekb_s[...] = ekb
        dec_s[...] = jnp.exp(bend)
        oraw = or_ref[...]
        r = lax.rsqrt(jnp.mean(oraw * oraw, axis=-1, keepdims=True) + LN_EPS)
        on = oraw * r
        gg = gg_ref[...]
        silu, dsilu = _silu_and_grad(gg)
        doa = do_ref[...]
        gnv = gn_ref[...]
        dgg_ref[...] = (doa * on * gnv * dsilu).astype(BF16)
        dyn = doa * silu
        dgn_ref[...] += jnp.sum(dyn * on, axis=0, keepdims=True)
        don = dyn * gnv
        dor_s[...] = r * (don - on * jnp.mean(don * on, axis=-1, keepdims=True))
        tidx = lax.broadcasted_iota(jnp.int32, (HG_HALF, HG_DIM), 0)

        def blk(ii, carry):
            i = nblk - 1 - ii
            r0 = pl.multiple_of(i * HG_BLOCK, HG_BLOCK)
            rows = pl.ds(r0, HG_BLOCK)
            st = st_ref[i]
            dst = dstate[...]
            dstb = dst.astype(BF16)
            do = dor_s[rows, :]
            dob = do.astype(BF16)
            v = iv_ref[rows, :]
            vb = v.astype(BF16)
            qq = q_ref[rows, :]
            kk = k_s[rows, :]
            bb = b_s[rows, :]
            qt = qt_s[rows, :]
            kt = kt_s[rows, :]
            dec = dec_s[pl.ds(r0, 1), :]
            dkt = _dot(vb, dstb)
            dq = _dot(dob, st) * eb_s[rows, :]
            dk = dkt * ekb_s[rows, :]
            dv = _dot_nt(kt, dstb)
            gend = (jnp.sum(kk * dk, axis=0, keepdims=True)
                    + dec * jnp.sum(dst * st.astype(F32), axis=0, keepdims=True))
            qh, bh, doh = _halves(qq), _halves(bb), _halves(do)
            dqh, dkh, dvh = list(_halves(dq)), list(_halves(dk)), list(_halves(dv))
            for s in range(HG_BLOCK):
                ks, vs = kk[s:s + 1, :], v[s:s + 1, :]
                dk_part = dv_part = None
                for h in _causal_halves(s):
                    e = _decay_from(bh[h], bb[s:s + 1, :], s, h, tidx)
                    ke = ks * e
                    acol = jnp.sum(qh[h] * ke, axis=1, keepdims=True)
                    dacol = jnp.sum(doh[h] * vs, axis=1, keepdims=True)
                    dqh[h] = dqh[h] + dacol * ke
                    pk = dacol * (qh[h] * e)
                    pv = acol * doh[h]
                    dk_part = pk if dk_part is None else dk_part + pk
                    dv_part = pv if dv_part is None else dv_part + pv
                hs, row = divmod(s, HG_HALF)
                dkh[hs] = dkh[hs] + jnp.where(tidx == row, jnp.sum(dk_part, axis=0, keepdims=True), 0.0)
                dvh[hs] = dvh[hs] + jnp.where(tidx == row, jnp.sum(dv_part, axis=0, keepdims=True), 0.0)
            dq = jnp.concatenate(dqh, axis=0)
            dk = jnp.concatenate(dkh, axis=0)
            dv = jnp.concatenate(dvh, axis=0)
            dq_ref[rows, :] = dq.astype(BF16)
            div_ref[rows, :] = dv.astype(BF16)
            dk_s[rows, :] = dk
            dbl_s[rows, :] = qq * dq - kk * dk
            gr_s[rows, :] = jnp.zeros((HG_BLOCK, HG_DIM), F32) + gend
            dstate[...] = dst * dec + _dot_tn(dob, qt)
            return carry

        lax.fori_loop(0, nblk, blk, 0, unroll=HG_UNROLL)
        dlogf = _mask_dot(mask_ref[1], dbl_s[...]) + gr_s[...]
        dk = dk_s[...]
        dfz_ref[...] = ((dlogf / f - dk) * ((1.0 - lb) * sig * nsig)).astype(BF16)
        dlb_acc[...] += jnp.sum((dlogf / f - dk) * nsig, axis=0, keepdims=True)

        @pl.when(c == nct - 1)
        def _():
            dl0 = dlb_acc[...] * lb * (1.0 - lb)
            layer = lax.broadcasted_iota(jnp.int32, (2, HG_DIM), 0)
            dlg_ref[...] = jnp.where(layer == 0, dl0, -dl0)

    def slab(off):
        return pl.BlockSpec((None, ct, HG_DIM), lambda h, c: (off + h, nct - 1 - c, 0))

    out_slab = pl.BlockSpec((ct, HG_DIM), lambda h, c: (nct - 1 - c, h))
    tile_f32 = pltpu.VMEM((ct, HG_DIM), F32)
    tile_b16 = pltpu.VMEM((ct, HG_DIM), BF16)
    slab_shape = pltpu.HBM((t, nh * HG_DIM), BF16)
    return pl.pallas_call(
        body,
        grid=(nh, nct),
        in_specs=[slab(0), slab(nh), slab(2 * nh), slab(3 * nh), slab(0), slab(0),
                  pl.BlockSpec((None, nblk, HG_DIM, HG_DIM), lambda h, c: (h, nct - 1 - c, 0, 0)),
                  pl.BlockSpec((None, 2, HG_DIM), lambda h, c: (h, 0, 0)),
                  pl.BlockSpec((1, HG_DIM), lambda h, c: (0, 0)),
                  pl.BlockSpec((3, mrows, mrows), lambda h, c: (0, 0, 0))],
        out_specs=[out_slab, out_slab, out_slab, out_slab,
                   pl.BlockSpec((None, 2, HG_DIM), lambda h, c: (h, 0, 0)),
                   pl.BlockSpec((None, 1, HG_DIM), lambda h, c: (h, 0, 0))],
        out_shape=[slab_shape, slab_shape, slab_shape, slab_shape,
                   jax.ShapeDtypeStruct((nh, 2, HG_DIM), F32), jax.ShapeDtypeStruct((nh, 1, HG_DIM), F32)],
        scratch_shapes=[pltpu.VMEM((HG_DIM, HG_DIM), F32), tile_b16, tile_b16, tile_f32, tile_f32, tile_f32, tile_f32,
                        tile_f32, tile_f32, tile_f32, tile_f32, tile_f32, pltpu.VMEM((1, HG_DIM), F32)],
        compiler_params=_params(2),
        name="hgrn_bwd",
    )(proj, proj, proj, proj, oraw, dmix, states, logits, gn, _block_masks(mrows))


def _sgu_bwd(proj, dmix, ln_g, ln_b, w_s, b_row):
    t = proj.shape[1]
    ct = _sg_tile(t)
    nct = t // ct
    ng = SG_GROUPS
    off_u = 4 * HG_HEADS
    off_v = off_u + ng
    n = SG_CHUNK

    def body(u_ref, v_ref, do_ref, g_ref, b_ref, w_ref, bs_ref, du_ref, dv_ref, dg_ref, db_ref, dw_ref, dbs_ref):
        c = pl.program_id(1)

        @pl.when(c == 0)
        def _():
            dg_ref[...] = jnp.zeros_like(dg_ref)
            db_ref[...] = jnp.zeros_like(db_ref)
            dw_ref[...] = jnp.zeros_like(dw_ref)
            dbs_ref[...] = jnp.zeros_like(dbs_ref)

        r = lax.broadcasted_iota(jnp.int32, (n, n), 0)
        cc = lax.broadcasted_iota(jnp.int32, (n, n), 1)
        wm = jnp.where(cc <= r, w_ref[...], 0.0).astype(BF16)
        wmt = jnp.where(r <= cc, w_ref[...].T, 0.0).astype(BF16)
        bs = _bias_by_position(bs_ref[...])
        for ci in range(ct // n):
            rows = slice(ci * n, (ci + 1) * n)
            ua, dua, dva, vn, xhat, rstd, s = _sgu_chunk_fwd(u_ref[rows, :], v_ref[rows, :], g_ref[...], b_ref[...],
                                                             wm, bs)
            do = do_ref[rows, :]
            du_ref[rows, :] = (do * s * dua).astype(BF16)
            ds = do * ua
            dsb = ds.astype(BF16)
            dbs_ref[...] += jnp.sum(ds, axis=1, keepdims=True)
            dw_ref[...] += _dot_nt(dsb, vn.astype(BF16))
            dvn = _dot(wmt, dsb)
            dva_in, dg, db = _ln_bwd(dvn, xhat, rstd, g_ref[...])
            dg_ref[...] += dg
            db_ref[...] += db
            dv_ref[rows, :] = (dva_in * dva).astype(BF16)

        @pl.when(c == nct - 1)
        def _():
            dw_ref[...] = jnp.where(cc <= r, dw_ref[...], 0.0)

    vec = pl.BlockSpec((None, 1, SG_DIM), lambda g, c: (g, 0, 0))
    mat = pl.BlockSpec((None, n, n), lambda g, c: (g, 0, 0))
    col = pl.BlockSpec((None, n, 1), lambda g, c: (g, 0, 0))
    out_slab = pl.BlockSpec((ct, SG_DIM), lambda g, c: (c, g))
    return pl.pallas_call(
        body,
        grid=(ng, nct),
        in_specs=[pl.BlockSpec((None, ct, SG_DIM), lambda g, c: (off_u + g, c, 0)),
                  pl.BlockSpec((None, ct, SG_DIM), lambda g, c: (off_v + g, c, 0)),
                  pl.BlockSpec((None, ct, SG_DIM), lambda g, c: (ng + g, c, 0)), vec, vec, mat, vec],
        out_specs=[out_slab, out_slab, vec, vec, mat, col],
        out_shape=[pltpu.HBM((t, ng * SG_DIM), BF16), pltpu.HBM((t, ng * SG_DIM), BF16),
                   jax.ShapeDtypeStruct((ng, 1, SG_DIM), F32), jax.ShapeDtypeStruct((ng, 1, SG_DIM), F32),
                   jax.ShapeDtypeStruct((ng, n, n), F32), jax.ShapeDtypeStruct((ng, n, 1), F32)],
        compiler_params=_params(2),
        name="sgu_bwd",
    )(proj, proj, dmix, ln_g, ln_b, w_s, b_row)


def _attn_bwd(dyb, wo, qb, kb, vb):
    t, d = dyb.shape
    m_len = kb.shape[0]
    tm = _row_tile(t)
    dh = d // X_HEADS
    scale = dh ** -0.5

    def body(dy_ref, wo_ref, q_ref, k_ref, v_ref, dq_ref, dk_ref, dv_ref):
        i = pl.program_id(0)

        @pl.when(i == 0)
        def _():
            dk_ref[...] = jnp.zeros_like(dk_ref)
            dv_ref[...] = jnp.zeros_like(dv_ref)

        do = _dot_nt(dy_ref[...], wo_ref[...]).astype(BF16)
        for hd in range(X_HEADS):
            sl = slice(hd * dh, (hd + 1) * dh)
            qh = q_ref[:, sl]
            p = _softmax_rows(_dot_nt(qh, k_ref[:, sl]) * scale)
            doh = do[:, sl]
            dp = _dot_nt(doh, v_ref[:, sl])
            ds = (p * (dp - jnp.sum(dp * p, axis=-1, keepdims=True)) * scale).astype(BF16)
            dq_ref[:, sl] = _dot(ds, k_ref[:, sl]).astype(BF16)
            dk_ref[:, sl] += _dot_tn(ds, qh)
            dv_ref[:, sl] += _dot_tn(p.astype(BF16), doh)

    row = pl.BlockSpec((tm, d), lambda i: (i, 0))
    full = lambda a: pl.BlockSpec(a.shape, lambda i: (0, 0))
    kv = pl.BlockSpec((m_len, d), lambda i: (0, 0))
    return pl.pallas_call(
        body,
        grid=(t // tm,),
        in_specs=[row, full(wo), row, full(kb), full(vb)],
        out_specs=[row, kv, kv],
        out_shape=[jax.ShapeDtypeStruct((t, d), BF16), jax.ShapeDtypeStruct((m_len, d), F32),
                   jax.ShapeDtypeStruct((m_len, d), F32)],
        compiler_params=_params(1),
        name="attn_bwd",
    )(dyb, wo, qb, kb, vb)


def _mem_bwd(dk, dv, mb, xhat, rstd, g, wk, wv):
    m_len, d = dk.shape

    def body(dk_ref, dv_ref, mb_ref, xh_ref, rs_ref, g_ref, wk_ref, wv_ref, gwk_ref, gwv_ref, dg_ref, db_ref):
        dkb = dk_ref[...].astype(BF16)
        dvb = dv_ref[...].astype(BF16)
        mb_v = mb_ref[...]
        gwk_ref[...] = _dot_tn(mb_v, dkb).astype(BF16)
        gwv_ref[...] = _dot_tn(mb_v, dvb).astype(BF16)
        dm = _dot_nt(dkb, wk_ref[...]) + _dot_nt(dvb, wv_ref[...])
        _, dg, db = _ln_bwd(dm, xh_ref[...], rs_ref[...], g_ref[...])
        dg_ref[...] = dg
        db_ref[...] = db

    return pl.pallas_call(
        body,
        out_shape=[jax.ShapeDtypeStruct((d, d), BF16), jax.ShapeDtypeStruct((d, d), BF16),
                   jax.ShapeDtypeStruct((1, d), F32), jax.ShapeDtypeStruct((1, d), F32)],
        compiler_params=pltpu.CompilerParams(vmem_limit_bytes=VMEM_LIMIT_V7X),
        name="mem_bwd",
    )(dk, dv, mb, xhat, rstd, g, wk, wv)


def _adamw(w, g, m, v):
    m = ADAM_B1 * m + (1.0 - ADAM_B1) * g
    v = ADAM_B2 * v + (1.0 - ADAM_B2) * (g * g)
    m_hat = m / (1.0 - ADAM_B1 ** ADAM_STEP)
    v_hat = v / (1.0 - ADAM_B2 ** ADAM_STEP)
    delta = -ADAM_LR * (m_hat / (jnp.sqrt(v_hat) + ADAM_EPS) + ADAM_WD * w)
    return delta, m, v


def _slot_sum(ref):
    g = ref[0].astype(F32)
    for s in range(1, N_DEV):
        g = g + ref[s].astype(F32)
    return g


def _adam_sharded(lands, w, m, v, axis, name):
    rows, cols = w.shape
    nl = len(lands)
    transposed = axis == 1 and nl == 2
    if transposed:
        rows, cols = cols, rows
        tr = 256
        grid = (rows // tr,)
        wblk = pl.BlockSpec((cols, tr), lambda i: (0, i))
        lblk = [pl.BlockSpec((N_DEV, tr, a.shape[2]), lambda i: (0, i, 0)) for a in lands]
    elif axis == 1:
        tr = 256 if rows % 256 == 0 else rows
        grid = (rows // tr,)
        wblk = pl.BlockSpec((tr, cols), lambda i: (i, 0))
        lblk = [pl.BlockSpec((N_DEV, tr, a.shape[2]), lambda i: (0, i, 0)) for a in lands]
    else:
        tc = _col_tile(cols)
        grid = (cols // tc,)
        wblk = pl.BlockSpec((rows, tc), lambda i: (0, i))
        lblk = [pl.BlockSpec((N_DEV, a.shape[1], tc), lambda i: (0, 0, i)) for a in lands]

    def body(*refs):
        w_ref, m_ref, v_ref = refs[nl:nl + 3]
        g_ref, d_ref, nm_ref, nv_ref = refs[nl + 3:]
        g = _slot_sum(refs[0])
        if nl == 2:
            tail = _slot_sum(refs[1])
            if transposed:
                g = jnp.concatenate([g.T, tail.T[:cols - g.shape[1], :]], axis=0)
            elif axis == 1:
                g = jnp.concatenate([g, tail[:, :cols - g.shape[1]]], axis=1)
            else:
                g = jnp.concatenate([g, tail[:rows - g.shape[0], :]], axis=0)
        delta, nm, nv = _adamw(w_ref[...], g, m_ref[...], v_ref[...])
        g_ref[...] = g
        d_ref[...] = delta
        nm_ref[...] = nm
        nv_ref[...] = nv

    shp = pltpu.HBM(w.shape, F32)
    return pl.pallas_call(
        body,
        grid=grid,
        in_specs=lblk + [wblk, wblk, wblk],
        out_specs=[wblk, wblk, wblk, wblk],
        out_shape=[shp, shp, shp, shp],
        compiler_params=_params(1),
        name=name,
    )(*[pltpu.with_memory_space_constraint(a, pltpu.HBM) for a in (*lands, w, m, v)])


def _mesh_pos():
    return lax.axis_index("x"), lax.axis_index("y"), lax.axis_index("c")


def _peer(k):
    x, y, c = _mesh_pos()
    pos = (x ^ (k >> 2), y ^ ((k >> 1) & 1), c ^ (k & 1))
    return pos, 4 * pos[0] + 2 * pos[1] + pos[2]


def _sem_index(row, k):
    return row * (N_DEV - 1) + k - 1


def _window(ref, axis, start, size):
    align = 16 if axis == 0 else LANES
    start = pl.multiple_of(start, align)
    return ref.at[pl.ds(start, size), :] if axis == 0 else ref.at[:, pl.ds(start, size)]


def _piece_refs(piece, srcs, lands, me, peer):
    kind, si, li, axis, base, stride, shape = piece
    if kind == "gather":
        return srcs[si], _window(lands[li], axis, base + stride * me, shape[axis])
    return _window(srcs[si], axis, base + stride * peer, shape[axis]), lands[li].at[me]


def _place_own(srcs, land_shapes, pieces, name):
    ns, nl, npc = len(srcs), len(land_shapes), len(pieces)

    def body(*refs):
        s_refs = refs[:ns]
        l_refs = refs[ns:ns + nl]
        bufs = refs[ns + nl:ns + nl + npc]
        sems = refs[ns + nl + npc]
        x, y, c = _mesh_pos()
        me = 4 * x + 2 * y + c
        loads = []
        for p, piece in enumerate(pieces):
            src, dst = _piece_refs(piece, s_refs, l_refs, me, me)
            cp = pltpu.make_async_copy(src, bufs[p], sems.at[0, p])
            cp.start()
            loads.append((cp, dst))
        stores = []
        for p, (cp, dst) in enumerate(loads):
            cp.wait()
            out = pltpu.make_async_copy(bufs[p], dst, sems.at[1, p])
            out.start()
            stores.append(out)
        for out in stores:
            out.wait()

    out = pl.pallas_call(
        body,
        in_specs=[ANY] * ns,
        out_specs=[HBM] * nl,
        out_shape=[pltpu.HBM(s.shape, s.dtype) for s in land_shapes],
        scratch_shapes=[pltpu.VMEM(pc[6], srcs[pc[1]].dtype) for pc in pieces] + [pltpu.SemaphoreType.DMA((2, npc))],
        compiler_params=pltpu.CompilerParams(vmem_limit_bytes=VMEM_LIMIT_V7X),
        name=name,
    )(*srcs)
    return list(out)


def _comm_start(srcs, lands, pieces, groups, name, after=()):
    ns, nl, na, ng = len(srcs), len(lands), len(after), len(groups)

    def body(*refs):
        s_refs = refs[:ns]
        l_refs = refs[ns:ns + nl]
        outs = refs[ns + nl + na:]
        sems = outs[:2 * ng]
        token = outs[-1]
        x, y, c = _mesh_pos()
        me = 4 * x + 2 * y + c
        for g, members in enumerate(groups):
            for row, p in enumerate(members):
                for k in range(1, N_DEV):
                    pos, peer = _peer(k)
                    src, dst = _piece_refs(pieces[p], s_refs, l_refs, me, peer)
                    pltpu.make_async_remote_copy(src_ref=src, dst_ref=dst, send_sem=sems[2 * g].at[_sem_index(row, k)],
                                                 recv_sem=sems[2 * g + 1].at[_sem_index(row, k)], device_id=pos,
                                                 device_id_type=MESH_ID).start()
        token[...] = jnp.zeros_like(token)

    sem_shapes = []
    for members in groups:
        sem_shapes += [pltpu.SemaphoreType.DMA((len(members) * (N_DEV - 1),))] * 2
    hbm_of = lambda a: pltpu.HBM(a.shape, a.dtype)
    out = pl.pallas_call(
        body,
        in_specs=[HBM] * (ns + nl) + [ANY] * na,
        out_specs=[SEM] * (2 * ng) + [HBM] * (ns + nl) + [pl.BlockSpec(memory_space=pltpu.VMEM)],
        out_shape=sem_shapes + [hbm_of(a) for a in srcs] + [hbm_of(a) for a in lands]
        + [jax.ShapeDtypeStruct((8, LANES), F32)],
        input_output_aliases={i: 2 * ng + i for i in range(ns + nl)},
        compiler_params=pltpu.CompilerParams(has_side_effects=DATAFLOW),
        name=name,
    )(*[pltpu.with_memory_space_constraint(a, pltpu.HBM) for a in list(srcs) + list(lands)], *after)
    sems = [(out[2 * g], out[2 * g + 1]) for g in range(ng)]
    return sems, list(out[2 * ng:2 * ng + ns]), list(out[2 * ng + ns:2 * ng + ns + nl]), out[-1]


def _comm_wait(srcs, lands, pieces, members, sems, after, name):
    ns, nl, na = len(srcs), len(lands), len(after)

    def body(*refs):
        s_refs = refs[:ns]
        l_refs = refs[ns:ns + nl]
        send_sems, recv_sems = refs[ns + nl:ns + nl + 2]
        x, y, c = _mesh_pos()
        me = 4 * x + 2 * y + c
        for row, p in enumerate(members):
            for k in range(1, N_DEV):
                pos, peer = _peer(k)
                src, dst = _piece_refs(pieces[p], s_refs, l_refs, me, peer)
                cp = pltpu.make_async_remote_copy(src_ref=src, dst_ref=dst, send_sem=send_sems.at[_sem_index(row, k)],
                                                  recv_sem=recv_sems.at[_sem_index(row, k)], device_id=pos,
                                                  device_id_type=MESH_ID)
                cp.wait_send()
                cp.wait_recv()

    hbm_of = lambda a: pltpu.HBM(a.shape, a.dtype)
    out = pl.pallas_call(
        body,
        in_specs=[HBM] * (ns + nl) + [SEM, SEM] + [ANY] * na,
        out_specs=[HBM] * (ns + nl),
        out_shape=[hbm_of(a) for a in srcs] + [hbm_of(a) for a in lands],
        input_output_aliases={i: i for i in range(ns + nl)},
        compiler_params=pltpu.CompilerParams(has_side_effects=DATAFLOW),
        name=name,
    )(*srcs, *lands, sems[0], sems[1], *after)
    return list(out[ns:])


def _landed_block(piece, lands, owner):
    _, _, li, axis, base, stride, shape = piece
    return _window(lands[li], axis, base + stride * owner, shape[axis])


def _copy_stage(name, bufs, in_sems, out_sem_sizes, emit, after=()):
    nb, ni, no, na = len(bufs), len(in_sems), len(out_sem_sizes), len(after)

    def body(*refs):
        b_refs = refs[:nb]
        i_refs = refs[nb:nb + ni]
        o_refs = refs[nb + ni + na:nb + ni + na + no]
        emit(b_refs, i_refs, o_refs)
        refs[-1][...] = jnp.zeros_like(refs[-1])

    hbm_of = lambda a: pltpu.HBM(a.shape, a.dtype)
    out = pl.pallas_call(
        body,
        in_specs=[HBM] * nb + [SEM] * ni + [ANY] * na,
        out_specs=[SEM] * no + [HBM] * nb + [pl.BlockSpec(memory_space=pltpu.VMEM)],
        out_shape=[pltpu.SemaphoreType.DMA((n,)) for n in out_sem_sizes] + [hbm_of(a) for a in bufs]
        + [jax.ShapeDtypeStruct((8, LANES), F32)],
        input_output_aliases={i: no + i for i in range(nb)},
        compiler_params=pltpu.CompilerParams(has_side_effects=DATAFLOW),
        name=name,
    )(*[pltpu.with_memory_space_constraint(a, pltpu.HBM) for a in bufs], *in_sems, *after)
    return list(out[:no]), list(out[no:no + nb]), out[-1]


def _remote(src, dst, send, recv, to):
    return pltpu.make_async_remote_copy(src_ref=src, dst_ref=dst, send_sem=send, recv_sem=recv, device_id=to,
                                        device_id_type=MESH_ID)


def _routed_gather(srcs, lands, pieces, meanwhile, name):
    ns, npc = len(srcs), len(pieces)

    def places():
        x, y, c = _mesh_pos()
        index = lambda p: 4 * p[0] + 2 * p[1] + p[2]
        me, sib = (x, y, c), (x, y, 1 - c)
        xnb, ynb = (1 - x, y, c), (x, 1 - y, c)
        got_first = (x ^ (1 - c), y ^ c, c)
        pass_to = (x ^ c, y ^ (1 - c), c)
        diag = (1 - x, 1 - y, c)
        return index, me, sib, xnb, ynb, got_first, pass_to, diag

    def start(b, _, o):
        index, me, sib, xnb, ynb, *_rest = places()
        send_a, recv_sib, recv_nb = o
        for p, piece in enumerate(pieces):
            src, dst = _piece_refs(piece, b[:ns], b[ns:], index(me), 0)
            _remote(src, dst, send_a.at[3 * p], recv_sib.at[p], sib).start()
            _remote(src, dst, send_a.at[3 * p + 1], recv_nb.at[2 * p], xnb).start()
            _remote(src, dst, send_a.at[3 * p + 2], recv_nb.at[2 * p + 1], ynb).start()

    def pass_a(b, i, o):
        index, me, sib, xnb, ynb, got_first, pass_to, _diag = places()
        (recv_nb,) = i
        send_f, recv_f, send_d, recv_d = o
        for p, piece in enumerate(pieces):
            for j, nb in enumerate((xnb, ynb)):
                blk = _landed_block(piece, b, index(nb))
                _remote(blk, blk, send_f.at[2 * p + j], recv_nb.at[2 * p + j], sib).wait_recv()
                _remote(blk, blk, send_f.at[2 * p + j], recv_f.at[2 * p + j], sib).start()
            blk = _landed_block(piece, b, index(got_first))
            _remote(blk, blk, send_d.at[p], recv_d.at[p], pass_to).start()

    def pass_b(b, i, o):
        index, me, sib, *_mid, diag = places()
        (recv_d,) = i
        send_g, recv_g = o
        for p, piece in enumerate(pieces):
            blk = _landed_block(piece, b, index(diag))
            _remote(blk, blk, send_g.at[p], recv_d.at[p], sib).wait_recv()
            _remote(blk, blk, send_g.at[p], recv_g.at[p], sib).start()

    def last(b, i, _):
        index, me, sib, *_others = places()
        send_a, recv_sib, send_f, recv_f, send_d, send_g, recv_g = i
        for p, piece in enumerate(pieces):
            src, dst = _piece_refs(piece, b[:ns], b[ns:], index(me), 0)
            cp = lambda s_sem, r_sem: _remote(src, dst, s_sem, r_sem, sib)
            cp(send_a.at[3 * p], recv_sib.at[p]).wait_recv()
            cp(send_a.at[3 * p], recv_g.at[p]).wait_recv()
            for j in range(3):
                cp(send_a.at[3 * p + j], recv_sib.at[p]).wait_send()
            for j in range(2):
                cp(send_f.at[2 * p + j], recv_f.at[2 * p + j]).wait_recv()
                cp(send_f.at[2 * p + j], recv_f.at[2 * p + j]).wait_send()
            cp(send_d.at[p], recv_sib.at[p]).wait_send()
            cp(send_g.at[p], recv_sib.at[p]).wait_send()

    (send_a, recv_sib, recv_nb), bufs, started = _copy_stage(name + "_start", list(srcs) + list(lands), [],
                                                             [3 * npc, npc, 2 * npc], start)
    srcs, lands = bufs[:ns], bufs[ns:]
    (send_f, recv_f, send_d, recv_d), lands, _ = _copy_stage(name + "_pass_a", lands, [recv_nb],
                                                             [2 * npc, 2 * npc, npc, npc],
                                                             lambda b, i, o: pass_a(b, i, o), after=meanwhile(started))
    (send_g, recv_g), lands, tok = _copy_stage(name + "_pass_b", lands, [recv_d], [npc, npc], pass_b)
    _, bufs, _ = _copy_stage(name + "_last", list(srcs) + list(lands),
                             [send_a, recv_sib, send_f, recv_f, send_d, send_g, recv_g], [], last)
    return bufs[ns:], tok


def _paired_gather(srcs, lands, pieces, groups, after, name):
    ns, ng = len(srcs), len(groups)
    far = (1, 2, 3)

    def me_sib():
        x, y, c = _mesh_pos()
        return 4 * x + 2 * y + c, (x, y, 1 - c)

    def start(b, _, o):
        me, sib = me_sib()
        for g, members in enumerate(groups):
            send, recv_sib, recv_far = o[3 * g:3 * g + 3]
            for r, p in enumerate(members):
                src, dst = _piece_refs(pieces[p], b[:ns], b[ns:], me, 0)
                _remote(src, dst, send.at[4 * r], recv_sib.at[r], sib).start()
                for j in far:
                    _remote(src, dst, send.at[4 * r + j], recv_far.at[3 * r + j - 1], _peer(2 * j)[0]).start()

    def forward(b, i, o):
        _, sib = me_sib()
        for g, members in enumerate(groups):
            send_f, recv_f = o[2 * g:2 * g + 2]
            for r, p in enumerate(members):
                for j in far:
                    blk = _landed_block(pieces[p], b, _peer(2 * j)[1])
                    _remote(blk, blk, send_f.at[3 * r + j - 1], i[g].at[3 * r + j - 1], sib).wait_recv()
                    _remote(blk, blk, send_f.at[3 * r + j - 1], recv_f.at[3 * r + j - 1], sib).start()

    def last(sub):
        def emit(b, i, _):
            send, recv_sib, send_f, recv_f = i
            me, sib = me_sib()
            for r, piece in enumerate(sub):
                src, dst = _piece_refs(piece, b[:-1], b[-1:], me, 0)
                _remote(src, dst, send.at[4 * r], recv_sib.at[r], sib).wait_recv()
                for j in range(4):
                    _remote(src, dst, send.at[4 * r + j], recv_sib.at[r], sib).wait_send()
                for j in far:
                    blk = _landed_block(piece, b[-1:], _peer(2 * j + 1)[1])
                    _remote(blk, blk, send_f.at[3 * r + j - 1], recv_f.at[3 * r + j - 1], sib).wait_recv()
                    _remote(blk, blk, send_f.at[3 * r + j - 1], recv_f.at[3 * r + j - 1], sib).wait_send()
        return emit

    sizes = []
    for members in groups:
        sizes += [4 * len(members), len(members), 3 * len(members)]
    sems, bufs, started = _copy_stage(name + "_start", list(srcs) + list(lands), [], sizes, start, after=after)
    srcs = bufs[:ns]
    state = dict(lands=bufs[ns:])

    def finish(g, after):
        if "passed" not in state:
            sizes_f = []
            for members in groups:
                sizes_f += [3 * len(members)] * 2
            state["passed"], state["lands"], _ = _copy_stage(name + "_pass", state["lands"],
                                                             [sems[3 * k + 2] for k in range(ng)], sizes_f, forward,
                                                             after=after)
            after = ()
        members = groups[g]
        sub = [(pieces[p][0], r, 0) + pieces[p][3:] for r, p in enumerate(members)]
        _, bufs, _ = _copy_stage("%s_last_%d" % (name, g), [srcs[pieces[p][1]] for p in members] + [state["lands"][g]],
                                 [sems[3 * g], sems[3 * g + 1], state["passed"][2 * g], state["passed"][2 * g + 1]],
                                 [], last(sub), after=after)
        return bufs[-1]

    return finish, started


_SMALL_NAMES = ("ln1_g", "ln1_b", "hg_lb_logits", "hg_norm_g", "sg_ln_g", "sg_ln_b", "sg_w_s", "sg_b_s",
                "ln2_g", "ln2_b", "mem_ln_g", "mem_ln_b", "ln3_g", "ln3_b", "ln4_g", "ln4_b")


_VEC_NAMES = ("ln1_g", "ln1_b", "ln2_g", "ln2_b", "mem_ln_g", "mem_ln_b", "ln3_g", "ln3_b", "ln4_g", "ln4_b")
_ROW_NAMES = ("hg_lb_logits", "hg_norm_g", "sg_ln_g", "sg_ln_b", "sg_b_s", "sg_w_s")
VEC_ROWS = 16


def _row_plan(shapes):
    plan, pos = {}, 0
    for k in _ROW_NAMES:
        shp = shapes[k]
        slabs, off = [], pos
        for idx in itertools.product(*[range(dim) for dim in shp[:-2]]):
            slabs.append((idx, off, shp[-2]))
            off += shp[-2]
        plan[k] = (pos, slabs)
        pos = -(-off // 8) * 8
    return plan, -(-pos // 16) * 16


def _pack_small_grads(gs, shapes, loss):
    d = gs[_VEC_NAMES[0]].size
    vec = jnp.concatenate([gs[k].reshape(1, -1) for k in _VEC_NAMES] + [jnp.tile(loss, (1, d // LANES))], axis=0)
    vec = jnp.pad(vec, ((0, VEC_ROWS - vec.shape[0]), (0, 0)))
    plan, total = _row_plan(shapes)
    parts, pos = [], 0
    for k in _ROW_NAMES:
        first, slabs = plan[k]
        rows = gs[k].reshape(-1, LANES)
        end = slabs[-1][1] + slabs[-1][2]
        nxt = -(-end // 8) * 8
        parts.append(jnp.pad(rows, ((0, nxt - first - rows.shape[0]), (0, 0))))
        pos = nxt
    parts.append(jnp.zeros((total - pos, LANES), F32))
    return vec, jnp.concatenate(parts, axis=0)


def _adam_small(land_vec, land_rows, w, m, v):
    names = _VEC_NAMES + _ROW_NAMES
    n = len(names)
    shapes = {k: w[k].shape for k in names}
    plan, _ = _row_plan(shapes)

    def body(*refs):
        lv_ref, lr_ref = refs[:2]
        w_refs, m_refs, v_refs = refs[2:2 + n], refs[2 + n:2 + 2 * n], refs[2 + 2 * n:2 + 3 * n]
        outs = refs[2 + 3 * n:2 + 7 * n]
        loss_ref = refs[2 + 7 * n]
        gv_s, gr_s = refs[3 + 7 * n:]
        gv_s[...] = _slot_sum(lv_ref)
        gr_s[...] = _slot_sum(lr_ref)
        loss_ref[...] = gv_s[len(_VEC_NAMES):len(_VEC_NAMES) + 1, :LANES]
        for p, k in enumerate(names):
            if k in _VEC_NAMES:
                row = _VEC_NAMES.index(k)
                slabs = [((), None, None)]
            else:
                slabs = plan[k][1]
            for idx, off, rows in slabs:
                g = gv_s[row:row + 1, :] if off is None else gr_s[off:off + rows, :]
                sel = idx + (slice(None), slice(None))
                delta, nm, nv = _adamw(w_refs[p][sel], g, m_refs[p][sel], v_refs[p][sel])
                for o, val in zip(range(4), (g, delta, nm, nv)):
                    outs[o * n + p][sel] = val

    flat = lambda tree: [tree[k] for k in names]
    shp = [jax.ShapeDtypeStruct(shapes[k], F32) for k in names]
    out = pl.pallas_call(
        body,
        out_shape=shp * 4 + [jax.ShapeDtypeStruct((1, LANES), F32)],
        scratch_shapes=[pltpu.VMEM(land_vec.shape[1:], F32), pltpu.VMEM(land_rows.shape[1:], F32)],
        name="adam_small",
    )(land_vec, land_rows, *flat(w), *flat(m), *flat(v))
    return [dict(zip(names, out[o * n:(o + 1) * n])) for o in range(4)], out[4 * n]


_COL_FFN = ("ffn1_w_gate", "ffn1_w_up", "ffn2_w_gate", "ffn2_w_up")
_ROW_FFN = ("ffn1_w_down", "ffn2_w_down")
_ROW_SQ = ("w_out", "xa_w_q", "xa_w_k", "xa_w_v", "xa_w_o")
_BIG_NAMES = ("ffn1_w_gate", "ffn1_w_up", "ffn1_w_down", "w_in", "w_out", "xa_w_q", "xa_w_k", "xa_w_v", "xa_w_o",
              "ffn2_w_gate", "ffn2_w_up", "ffn2_w_down")


def _ffn_split(fs):
    main = (fs // MXU_WIDTH_V7X) * MXU_WIDTH_V7X
    tail = fs - main
    tail_pad = -(-tail // LANES) * LANES
    assert main > 0 and tail > 0
    return main, tail, tail_pad


def _layout(name, shard_shape):
    r, c = shard_shape
    if name in _COL_FFN:
        main, tail, pad = _ffn_split(c)
        return (r, N_DEV * (main + pad)), [(1, 0, main, (r, main), (0, main)),
                                           (1, N_DEV * main, pad, (r, pad), (main, c))]
    if name in _ROW_FFN:
        main, tail, pad = _ffn_split(r)
        return (N_DEV * (main + pad), c), [(0, 0, main, (main, c), (0, main)),
                                           (0, N_DEV * main, pad, (pad, c), (main, r))]
    if name == "w_in":
        return (r, N_DEV * c), [(1, 0, c, (r, c), (0, c))]
    return (N_DEV * r, c), [(0, 0, r, (r, c), (0, r))]


def _shard_pieces(name, shard):
    out = []
    for axis, _, _, shape, (lo, hi) in _layout(name, shard.shape)[1]:
        part = shard[lo:hi, :] if axis == 0 else shard[:, lo:hi]
        pad = [(0, shape[0] - part.shape[0]), (0, shape[1] - part.shape[1])]
        out.append(jnp.pad(part, pad).astype(BF16))
    return out


def _gather_plan(names, shards):
    srcs, land_shapes, pieces, index = [], [], [], {}
    for li, name in enumerate(names):
        shape2d, parts = _layout(name, shards[name].shape)
        land_shapes.append(jax.ShapeDtypeStruct(shape2d, BF16))
        index[name] = []
        for (axis, base, stride, shape, _), src in zip(parts, _shard_pieces(name, shards[name])):
            index[name].append(len(pieces))
            pieces.append(("gather", len(srcs), li, axis, base, stride, shape))
            srcs.append(src)
    return srcs, land_shapes, pieces, index


def _scatter_plan(names, grads, shard_shapes):
    srcs, land_shapes, pieces, index = [], [], [], {}
    for si, name in enumerate(names):
        _, parts = _layout(name, shard_shapes[name])
        srcs.append(grads[name])
        index[name] = []
        for axis, base, stride, shape, _ in parts:
            index[name].append(len(land_shapes))
            pieces.append(("scatter", si, len(land_shapes), axis, base, stride, shape))
            land_shapes.append(jax.ShapeDtypeStruct((N_DEV,) + shape, grads[name].dtype))
    return srcs, land_shapes, pieces, index


def _small_views(small):
    row = lambda a: a.reshape(1, -1)
    ln = {k: row(small[k]) for k in ("ln1_g", "ln1_b", "ln2_g", "ln2_b", "ln3_g", "ln3_b", "ln4_g", "ln4_b",
                                      "mem_ln_g", "mem_ln_b", "hg_norm_g")}
    sg_w = small["sg_w_s"].reshape(SG_GROUPS, SG_CHUNK, SG_CHUNK)
    sg = dict(logits=jnp.swapaxes(small["hg_lb_logits"], 0, 1),
              g=small["sg_ln_g"].reshape(SG_GROUPS, 1, SG_DIM), b=small["sg_ln_b"].reshape(SG_GROUPS, 1, SG_DIM),
              w=sg_w, bs=small["sg_b_s"].reshape(SG_GROUPS, 1, SG_CHUNK))
    return ln, sg


def _forward(x, xb, mem, target, get_w, small, first_deps=()):
    ln, sg = _small_views(small)
    a1, b1, s1 = _ffn_up(xb, get_w("ffn1_w_gate", ()), get_w("ffn1_w_up", ()), "ffn1_up", deps=first_deps)
    h1b, xh1, rs1 = _mm_res_ln(s1, get_w("ffn1_w_down", (s1,)), x, ln["ln1_g"], ln["ln1_b"], 0.5, "ffn1_down_ln")
    proj = _mm_nn(h1b, get_w("w_in", (h1b,)), "mix_in")
    oraw, mix, states = _hgrn_fwd(proj, sg["logits"], ln["hg_norm_g"])
    mix = _sgu_fwd(proj, mix, sg["g"], sg["b"], sg["w"], sg["bs"])
    h2b, xh2, rs2 = _mm_res_ln(mix, get_w("w_out", (mix,)), (xh1, ln["ln1_g"], ln["ln1_b"]), ln["ln2_g"], ln["ln2_b"],
                               1.0, "mix_out_ln")
    mb, mxh, mrs, kb, vb = _mem_kv(mem, ln["mem_ln_g"], ln["mem_ln_b"], get_w("xa_w_k", (h2b,)), get_w("xa_w_v", (h2b,)))
    qb, att = _attn_fwd(h2b, get_w("xa_w_q", (mrs,)), kb, vb)
    h3b, xh3, rs3 = _mm_res_ln(att, get_w("xa_w_o", (att,)), (xh2, ln["ln2_g"], ln["ln2_b"]), ln["ln3_g"], ln["ln3_b"],
                               1.0, "attn_out_ln")
    a2, b2, s2 = _ffn_up(h3b, get_w("ffn2_w_gate", (h3b,)), get_w("ffn2_w_up", (h3b,)), "ffn2_up")
    loss, dy4, dy4b, dg4, db4 = _mm_res_ln(s2, get_w("ffn2_w_down", (s2,)), (xh3, ln["ln3_g"], ln["ln3_b"]),
                                           ln["ln4_g"], ln["ln4_b"], 0.5, "ffn2_down_ln_loss", target=target)
    return dict(xb=xb, a1=a1, b1=b1, s1=s1, h1b=h1b, xh1=xh1, rs1=rs1, proj=proj, oraw=oraw, mix=mix, states=states,
                h2b=h2b, xh2=xh2, rs2=rs2, mb=mb, mxh=mxh, mrs=mrs, kb=kb, vb=vb, qb=qb, att=att, h3b=h3b, xh3=xh3,
                rs3=rs3, a2=a2, b2=b2, s2=s2, loss=loss, dy4=dy4, dy4b=dy4b, dg4=dg4, db4=db4)


def _backward(sv, wt, small, send):
    ln, sg = _small_views(small)
    gs = {"ln4_g": sv["dg4"], "ln4_b": sv["db4"]}
    loss, dy4, dy4b = sv["loss"], sv["dy4"], sv["dy4b"]
    g_down2 = _mm_tn(sv["s2"], dy4b, "g_ffn2_down", scale=0.5)
    da2, db2, dy3, dy3b, gs["ln3_g"], gs["ln3_b"] = _ffn_bwd_fused(
        dy4, dy4b, wt["ffn2_w_down"], wt["ffn2_w_gate"], wt["ffn2_w_up"], sv["a2"], sv["b2"], 0.5,
        (sv["xh3"], sv["rs3"], ln["ln3_g"]), "ffn2_bwd")
    g_gate2 = _mm_tn(sv["h3b"], da2, "g_ffn2_gate")
    g_up2 = _mm_tn(sv["h3b"], db2, "g_ffn2_up")
    tok = send(("ffn2_w_down", "ffn2_w_gate", "ffn2_w_up"), (g_down2, g_gate2, g_up2))

    g_o = _mm_tn(sv["att"], dy3b, "g_xa_o", deps=(tok,))
    dqb, dk, dv = _attn_bwd(dy3b, wt["xa_w_o"], sv["qb"], sv["kb"], sv["vb"])
    g_q = _mm_tn(sv["h2b"], dqb, "g_xa_q")
    g_k, g_v, gs["mem_ln_g"], gs["mem_ln_b"] = _mem_bwd(dk, dv, sv["mb"], sv["mxh"], sv["mrs"], ln["mem_ln_g"],
                                                        wt["xa_w_k"], wt["xa_w_v"])
    tok = send(("xa_w_o", "xa_w_q", "xa_w_k", "xa_w_v"), (g_o, g_q, g_k, g_v))
    dy2, dy2b, gs["ln2_g"], gs["ln2_b"] = _dx_ln(dy3, [(dqb, wt["xa_w_q"])], (sv["xh2"], sv["rs2"], ln["ln2_g"]),
                                                 "attn_dx_ln", deps=(tok,))

    g_out = _mm_tn(sv["mix"], dy2b, "g_w_out")
    dmix = _mm_nt(dy2b, wt["w_out"], "mix_out_bwd")
    dq, dfz, div, dgg, dlg, dgn = _hgrn_bwd(sv["proj"], sv["oraw"], dmix, sv["states"], sg["logits"], ln["hg_norm_g"])
    du, dvv, gs["sg_ln_g"], gs["sg_ln_b"], gs["sg_w_s"], gs["sg_b_s"] = _sgu_bwd(
        sv["proj"], dmix, sg["g"], sg["b"], sg["w"], sg["bs"])
    gs["hg_lb_logits"] = jnp.swapaxes(dlg, 0, 1)
    gs["hg_norm_g"] = jnp.sum(dgn, axis=0)
    dproj = [dq, dfz, div, dgg, du, dvv]
    g_in = _mm_tn(sv["h1b"], dproj, "g_w_in")
    tok = send(("w_out", "w_in"), (g_out, g_in))
    dy1, dy1b, gs["ln1_g"], gs["ln1_b"] = _dx_ln(dy2, [(dproj, wt["w_in"])], (sv["xh1"], sv["rs1"], ln["ln1_g"]),
                                                 "mix_dx_ln", deps=(tok,))

    g_down1 = _mm_tn(sv["s1"], dy1b, "g_ffn1_down", scale=0.5)
    tok = send(("ffn1_w_down",), (g_down1,))
    da1, db1 = _ffn_bwd_act(dy1b, wt["ffn1_w_down"], sv["a1"], sv["b1"], 0.5, "ffn1_bwd_act", deps=(tok,))
    g_gate1 = _mm_tn(sv["xb"], da1, "g_ffn1_gate")
    tok = send(("ffn1_w_gate",), (g_gate1,))
    g_up1 = _mm_tn(sv["xb"], db1, "g_ffn1_up", deps=(tok,))
    tok = send(("ffn1_w_up",), (g_up1,))
    grad_x = _dx_ln(dy1, [(da1, wt["ffn1_w_gate"]), (db1, wt["ffn1_w_up"])], None, "ffn1_dx", deps=(tok,))
    return loss, grad_x, gs


_WEIGHT_NAMES = ("ffn1_w_gate", "ffn1_w_up", "ffn1_w_down", "ln1_g", "ln1_b", "w_in", "hg_lb_logits", "hg_norm_g",
                 "sg_ln_g", "sg_ln_b", "sg_w_s", "sg_b_s", "w_out", "ln2_g", "ln2_b", "mem_ln_g", "mem_ln_b",
                 "xa_w_q", "xa_w_k", "xa_w_v", "xa_w_o", "ln3_g", "ln3_b", "ffn2_w_gate", "ffn2_w_up", "ffn2_w_down",
                 "ln4_g", "ln4_b")
_FIRST = ("ffn1_w_gate", "ffn1_w_up")
_SECOND = ("ffn1_w_down", "w_in")
_THIRD = ("w_out", "xa_w_k", "xa_w_v", "xa_w_q", "xa_w_o", "ffn2_w_gate", "ffn2_w_up", "ffn2_w_down")


def kernel(x, mem, ffn1_w_gate, ffn1_w_up, ffn1_w_down, ln1_g, ln1_b, w_in, hg_lb_logits, hg_norm_g, sg_ln_g, sg_ln_b, sg_w_s, sg_b_s, w_out, ln2_g, ln2_b, mem_ln_g, mem_ln_b, xa_w_q, xa_w_k, xa_w_v, xa_w_o, ln3_g, ln3_b, ffn2_w_gate, ffn2_w_up, ffn2_w_down, ln4_g, ln4_b, loss_target, m_ffn1_w_gate, m_ffn1_w_up, m_ffn1_w_down, m_ln1_g, m_ln1_b, m_w_in, m_hg_lb_logits, m_hg_norm_g, m_sg_ln_g, m_sg_ln_b, m_sg_w_s, m_sg_b_s, m_w_out, m_ln2_g, m_ln2_b, m_mem_ln_g, m_mem_ln_b, m_xa_w_q, m_xa_w_k, m_xa_w_v, m_xa_w_o, m_ln3_g, m_ln3_b, m_ffn2_w_gate, m_ffn2_w_up, m_ffn2_w_down, m_ln4_g, m_ln4_b, v_ffn1_w_gate, v_ffn1_w_up, v_ffn1_w_down, v_ln1_g, v_ln1_b, v_w_in, v_hg_lb_logits, v_hg_norm_g, v_sg_ln_g, v_sg_ln_b, v_sg_w_s, v_sg_b_s, v_w_out, v_ln2_g, v_ln2_b, v_mem_ln_g, v_mem_ln_b, v_xa_w_q, v_xa_w_k, v_xa_w_v, v_xa_w_o, v_ln3_g, v_ln3_b, v_ffn2_w_gate, v_ffn2_w_up, v_ffn2_w_down, v_ln4_g, v_ln4_b):
    args = dict(locals())
    w = {k: args[k] for k in _WEIGHT_NAMES}
    m = {k: args["m_" + k] for k in _WEIGHT_NAMES}
    v = {k: args["v_" + k] for k in _WEIGHT_NAMES}
    shards = {k: w[k][0] for k in _BIG_NAMES}
    shard_shapes = {k: shards[k].shape for k in _BIG_NAMES}
    small = {k: (w[k][0] if k != "hg_lb_logits" else w[k]) for k in _SMALL_NAMES}

    srcs1, shapes1, pieces1, idx1 = _gather_plan(_FIRST, shards)
    lands1 = _place_own(srcs1, shapes1, pieces1, "gather_first_own")
    prepared = {}

    def prepare_rest(started):
        later = {k: shards[k] + started[0, 0] for k in _SECOND + _THIRD}
        placed = ()
        for key, names in (("second", _SECOND), ("third", _THIRD)):
            srcs, shapes, pieces, idx = _gather_plan(names, later)
            lands = _place_own(srcs, shapes, pieces, "gather_%s_own" % key)
            prepared[key] = (srcs, lands, pieces, idx)
            placed += tuple(lands)
        prepared["xb"] = _to_bf16(x[0], "x_bf16", deps=(started,))
        return placed + (prepared["xb"],)

    lands1, tok1 = _routed_gather(srcs1, lands1, pieces1, prepare_rest, "gather_first")
    srcs_p, lands_p, pieces_p, idx_p = prepared["second"]
    finish_second, tok_p = _paired_gather(srcs_p, lands_p, pieces_p, [list(idx_p[k]) for k in _SECOND], (tok1,),
                                          "gather_second")
    srcs2, lands2, pieces2, idx2 = prepared["third"]
    groups2 = [list(idx2[k]) for k in _THIRD]
    sems2, srcs2, lands2, tok2 = _comm_start(srcs2, lands2, pieces2, groups2, "gather_third_start", after=(tok_p,))
    wt = dict(zip(_FIRST, lands1))
    pending = {k: gi for gi, k in enumerate(_THIRD)}

    def get_w(name, after):
        if name in _SECOND and name not in wt:
            wt[name] = finish_second(_SECOND.index(name), after)
        if name in pending:
            gi = pending.pop(name)
            si = [pieces2[p][1] for p in groups2[gi]]
            sub = [(pieces2[p][0], row, 0) + pieces2[p][3:] for row, p in enumerate(groups2[gi])]
            wt[name] = _comm_wait([srcs2[s] for s in si], [lands2[gi]], sub, list(range(len(sub))), sems2[gi],
                                  after, "gather_wait_" + name)[0]
        return wt[name]

    sv = _forward(x[0], prepared["xb"], mem[0], loss_target[0], get_w, small, first_deps=(tok2,))

    sent = []

    def send(names, grads):
        srcs, shapes, pieces, idx = _scatter_plan(names, dict(zip(names, grads)), shard_shapes)
        lands = _place_own(srcs, shapes, pieces, "grads_own_%d" % len(sent))
        sems, srcs, lands, tok = _comm_start(srcs, lands, pieces, [list(range(len(pieces)))],
                                             "grads_start_%d" % len(sent))
        sent.append((names, srcs, lands, pieces, idx, sems[0]))
        return tok

    loss, grad_x, gs = _backward(sv, wt, small, send)

    ssrc = list(_pack_small_grads(gs, {k: w[k].shape for k in _SMALL_NAMES}, loss))
    sp = [("scatter", i, i, 0, 0, 0, a.shape) for i, a in enumerate(ssrc)]
    sshape = [jax.ShapeDtypeStruct((N_DEV,) + a.shape, F32) for a in ssrc]
    sl = _place_own(ssrc, sshape, sp, "small_own")
    ssem, ssrc, sl, _ = _comm_start(ssrc, sl, sp, [[0, 1]], "small_start")

    out_g, out_d, out_m, out_v = {}, {}, {}, {}
    after = (grad_x,)
    for n_sent, (names, srcs, lands, pieces, idx, sems) in enumerate(sent):
        lands = _comm_wait(srcs, lands, pieces, list(range(len(pieces))), sems, after, "grads_wait_%d" % n_sent)
        for k in names:
            axis = 1 if (k in _COL_FFN or k == "w_in") else 0
            if k in _COL_FFN:
                done = _adam_sharded([lands[i] for i in idx[k]], w[k][0].T, m[k][0].T, v[k][0].T, axis, "adam_" + k)
                res = [r.T for r in done]
            else:
                res = done = _adam_sharded([lands[i] for i in idx[k]], w[k][0], m[k][0], v[k][0], axis, "adam_" + k)
            out_g[k], out_d[k], out_m[k], out_v[k] = [r[None] for r in res]
        after = (done[3],)
    sl = _comm_wait(ssrc, sl, sp, [0, 1], ssem[0], after, "small_wait")
    small_out, loss_sum = _adam_small(sl[0], sl[1], w, m, v)
    for dst, res in zip((out_g, out_d, out_m, out_v), small_out):
        dst.update(res)
    loss_all = loss_sum[0, 0]
    return (loss_all, grad_x[None], *[out_g[k] for k in _WEIGHT_NAMES], *[out_d[k] for k in _WEIGHT_NAMES],
            *[out_m[k] for k in _WEIGHT_NAMES], *[out_v[k] for k in _WEIGHT_NAMES])
```

```python
import itertools

import jax
import jax.numpy as jnp
import numpy as np
from jax import lax
from jax.experimental import pallas as pl
from jax.experimental.pallas import tpu as pltpu

F32 = jnp.float32
BF16 = jnp.bfloat16

N_DEV = 8
ALPHA = 2.0 ** 0.25
LN_EPS = 1e-5
HG_HEADS = 4
HG_DIM = 128
SG_GROUPS = 4
SG_DIM = 128
SG_CHUNK = 128
X_HEADS = 4
HG_BLOCK = 16
HG_UNROLL = 16
ADAM_LR = 0.001
ADAM_B1 = 0.9
ADAM_B2 = 0.999
ADAM_EPS = 1e-08
ADAM_WD = 0.01
ADAM_STEP = 10
VMEM_LIMIT_V7X = 48 * 1024 * 1024
MXU_WIDTH_V7X = 256
LANES = 128
MESH_ID = pl.DeviceIdType.MESH
ANY = pl.BlockSpec(memory_space=pl.ANY)
HBM = pl.BlockSpec(memory_space=pltpu.HBM)
SEM = pl.BlockSpec(memory_space=pltpu.SEMAPHORE)
DATAFLOW = pltpu.SideEffectType.DATAFLOW_SIDE_EFFECTING


def _params(n_axes):
    return pltpu.CompilerParams(dimension_semantics=("arbitrary",) * n_axes, vmem_limit_bytes=VMEM_LIMIT_V7X)


def _dot(a, b):
    return jnp.dot(a, b, preferred_element_type=F32)


def _dot_nt(a, b):
    return lax.dot_general(a, b, (((1,), (1,)), ((), ())), preferred_element_type=F32)


def _dot_tn(a, b):
    return lax.dot_general(a, b, (((0,), (0,)), ((), ())), preferred_element_type=F32)


def _sigmoid(x):
    return 1.0 / (1.0 + jnp.exp(-x))


def _silu_and_grad(a):
    sig = _sigmoid(a)
    return a * sig, sig * (1.0 + a * (1.0 - sig))


_GELU_C = 0.7978845608028654


def _gelu_and_grad(x):
    inner = _GELU_C * (x + 0.044715 * x * x * x)
    t = jnp.tanh(inner)
    val = 0.5 * x * (1.0 + t)
    grad = 0.5 * (1.0 + t) + 0.5 * x * (1.0 - t * t) * _GELU_C * (1.0 + 3.0 * 0.044715 * x * x)
    return val, grad


def _ln_fwd(y, g, b):
    mu = jnp.mean(y, axis=-1, keepdims=True)
    yc = y - mu
    var = jnp.mean(yc * yc, axis=-1, keepdims=True)
    rstd = lax.rsqrt(var + LN_EPS)
    xhat = yc * rstd
    return xhat * g + b, xhat, rstd


def _ln_bwd(dh, xhat, rstd, g):
    dxh = dh * g
    m1 = jnp.mean(dxh, axis=-1, keepdims=True)
    m2 = jnp.mean(dxh * xhat, axis=-1, keepdims=True)
    dy = rstd * (dxh - m1 - xhat * m2)
    dg = jnp.sum(dh * xhat, axis=0, keepdims=True)
    db = jnp.sum(dh, axis=0, keepdims=True)
    return dy, dg, db


def _mask_dot(mask, x):
    hi = x.astype(BF16)
    lo = (x - hi.astype(F32)).astype(BF16)
    n = mask.shape[0]
    parts = [_dot(mask, hi[r:r + n, :]) + _dot(mask, lo[r:r + n, :]) for r in range(0, x.shape[0], n)]
    return parts[0] if len(parts) == 1 else jnp.concatenate(parts, axis=0)


def _block_masks(n):
    r = np.arange(n)[:, None]
    c = np.arange(n)[None, :]
    same = (r // HG_BLOCK) == (c // HG_BLOCK)
    return jnp.asarray(np.stack([same & (c <= r), same & (c >= r), same]), BF16)


def _row_tile(t):
    return min(t, 512)


def _col_tile(n):
    for cand in (512, 256, 128):
        if n % cand == 0:
            return cand
    return n


def _resident(w):
    return pl.BlockSpec(w.shape, lambda *_: (0, 0), pipeline_mode=pl.Buffered(1))


def _drop_deps(body, n_in, n_deps):
    if n_deps == 0:
        return body
    return lambda *refs: body(*refs[:n_in], *refs[n_in + n_deps:])


def _to_bf16(x, name, deps=()):
    t, d = x.shape
    tm = _row_tile(t)

    def body(x_ref, o_ref):
        o_ref[...] = x_ref[...].astype(BF16)

    row = pl.BlockSpec((tm, d), lambda i: (i, 0))
    return pl.pallas_call(
        _drop_deps(body, 1, len(deps)),
        grid=(t // tm,),
        in_specs=[row] + [ANY] * len(deps),
        out_specs=row,
        out_shape=jax.ShapeDtypeStruct((t, d), BF16),
        compiler_params=_params(1),
        name=name,
    )(x, *deps)


def _ffn_up(hb, wg, wu, name, deps=()):
    t, d = hb.shape
    f = wg.shape[1]
    tm = _row_tile(t)
    tn = _col_tile(f)

    def body(h_ref, wg_ref, wu_ref, a_ref, b_ref, s_ref):
        h = h_ref[...]
        for c in range(f // tn):
            cols = slice(c * tn, (c + 1) * tn)
            a = _dot(h, wg_ref[:, cols])
            b = _dot(h, wu_ref[:, cols])
            a_ref[:, cols] = a.astype(BF16)
            b_ref[:, cols] = b.astype(BF16)
            s_ref[:, cols] = (a * _sigmoid(a) * b).astype(BF16)

    act = pl.BlockSpec((tm, f), lambda i: (i, 0))
    return pl.pallas_call(
        _drop_deps(body, 3, len(deps)),
        grid=(t // tm,),
        in_specs=[pl.BlockSpec((tm, d), lambda i: (i, 0)), _resident(wg), _resident(wu)] + [ANY] * len(deps),
        out_specs=[act, act, act],
        out_shape=[jax.ShapeDtypeStruct((t, f), BF16)] * 3,
        compiler_params=_params(1),
        name=name,
    )(hb, wg, wu, *deps)


def _mm_res_ln(lhs, w, res, g, b, coef, name, target=None):
    t, kd = lhs.shape
    d = w.shape[1]
    tm = _row_tile(t)
    nt = t // tm
    from_norm = isinstance(res, tuple)
    n_res = 3 if from_norm else 1

    def body(*refs):
        l_ref, w_ref = refs[:2]
        r_refs = refs[2:2 + n_res]
        g_ref, b_ref = refs[2 + n_res:4 + n_res]
        rest = refs[4 + n_res:]
        prev = r_refs[0][...] * r_refs[1][...] + r_refs[2][...] if from_norm else r_refs[0][...]
        y = ALPHA * prev + coef * _dot(l_ref[...], w_ref[...])
        h, xhat, rstd = _ln_fwd(y, g_ref[...], b_ref[...])
        if target is None:
            hb_ref, xh_ref, rs_ref = rest
            hb_ref[...] = h.astype(BF16)
            xh_ref[...] = xhat
            rs_ref[...] = rstd
            return
        t_ref, loss_ref, dy_ref, dyb_ref, dg_ref, db_ref, lacc = rest
        i = pl.program_id(0)

        @pl.when(i == 0)
        def _():
            lacc[...] = jnp.zeros_like(lacc)
            dg_ref[...] = jnp.zeros_like(dg_ref)
            db_ref[...] = jnp.zeros_like(db_ref)

        err = h - t_ref[...]
        lacc[...] += jnp.sum(err * err, axis=0, keepdims=True)
        dy, dg, db = _ln_bwd(err * (1.0 / d), xhat, rstd, g_ref[...])
        dy_ref[...] = dy
        dyb_ref[...] = dy.astype(BF16)
        dg_ref[...] += dg
        db_ref[...] += db

        @pl.when(i == nt - 1)
        def _():
            loss_ref[...] = jnp.zeros_like(loss_ref) + jnp.sum(lacc[...], axis=1, keepdims=True) * (0.5 / d)

    row = pl.BlockSpec((tm, d), lambda i: (i, 0))
    vec = pl.BlockSpec((1, d), lambda i: (0, 0))
    res_specs = [row, vec, vec] if from_norm else [row]
    res_args = list(res) if from_norm else [res]
    in_specs = [pl.BlockSpec((tm, kd), lambda i: (i, 0)), _resident(w)] + res_specs + [vec, vec]
    args = [lhs, w] + res_args + [g, b]
    if target is None:
        out_specs = [row, row, pl.BlockSpec((tm, 1), lambda i: (i, 0))]
        out_shape = [jax.ShapeDtypeStruct((t, d), BF16), jax.ShapeDtypeStruct((t, d), F32), pltpu.HBM((t, 1), F32)]
        scratch = []
    else:
        in_specs.append(row)
        args.append(target)
        out_specs = [pl.BlockSpec((1, LANES), lambda i: (0, 0)), row, row, vec, vec]
        out_shape = [jax.ShapeDtypeStruct((1, LANES), F32), jax.ShapeDtypeStruct((t, d), F32),
                     jax.ShapeDtypeStruct((t, d), BF16), jax.ShapeDtypeStruct((1, d), F32),
                     jax.ShapeDtypeStruct((1, d), F32)]
        scratch = [pltpu.VMEM((1, d), F32)]
    return pl.pallas_call(
        body,
        grid=(nt,),
        in_specs=in_specs,
        out_specs=out_specs,
        out_shape=out_shape,
        scratch_shapes=scratch,
        compiler_params=_params(1),
        name=name,
    )(*args)


def _store_slabs(o_ref, first, tile):
    for s in range(tile.shape[1] // LANES):
        o_ref[first + s] = tile[:, s * LANES:(s + 1) * LANES]


def _mm_nn(lhs, w, name):
    t, kd = lhs.shape
    n = w.shape[1]
    tm = _row_tile(t)
    tn = _col_tile(n)

    def body(l_ref, w_ref, o_ref):
        lhs_v = l_ref[...]
        for c in range(n // tn):
            _store_slabs(o_ref, c * (tn // LANES), _dot(lhs_v, w_ref[:, c * tn:(c + 1) * tn]))

    return pl.pallas_call(
        body,
        grid=(t // tm,),
        in_specs=[pl.BlockSpec((tm, kd), lambda i: (i, 0)), _resident(w)],
        out_specs=pl.BlockSpec((n // LANES, tm, LANES), lambda i: (0, i, 0)),
        out_shape=jax.ShapeDtypeStruct((n // LANES, t, LANES), F32),
        compiler_params=_params(1),
        name=name,
    )(lhs, w)


def _lower_bound(lg):
    m = jnp.max(lg, axis=0, keepdims=True)
    e = jnp.exp(lg - m)
    return e[0:1, :] / jnp.sum(e, axis=0, keepdims=True)


def _forget_terms(fz, lb):
    e = jnp.exp(-jnp.abs(fz))
    r = 1.0 / (1.0 + e)
    pos = fz >= 0.0
    sig = jnp.where(pos, r, e * r)
    nsig = jnp.where(pos, e * r, r)
    f = lb + (1.0 - lb) * sig
    k = (1.0 - lb) * nsig
    return sig, nsig, f, k


def _hg_tile(t):
    return min(t, 1024)


HG_HALF = HG_BLOCK // 2
NEG_BIG = -1e30


def _halves(a):
    return a[:HG_HALF, :], a[HG_HALF:, :]


def _causal_halves(s):
    return (0, 1) if s < HG_HALF else (1,)


def _decay_from(b_half, b_s, s, h, tidx):
    first = s - h * HG_HALF
    diff = b_half - b_s
    if first > 0:
        diff = jnp.where(tidx >= first, diff, NEG_BIG)
    return jnp.exp(diff)


def _hgrn_fwd(proj, logits, gn):
    t = proj.shape[1]
    ct = _hg_tile(t)
    nct = t // ct
    nblk = ct // HG_BLOCK
    nh = HG_HEADS
    mrows = min(ct, 256)

    def body(q_ref, fz_ref, iv_ref, gg_ref, lg_ref, gn_ref, mask_ref, oraw_ref, oa_ref, st_ref,
             state, qt_s, kt_s, k_s, b_s, dec_s):
        c = pl.program_id(1)

        @pl.when(c == 0)
        def _():
            state[...] = jnp.zeros_like(state)

        lb = _lower_bound(lg_ref[...])
        q = q_ref[...]
        _, _, f, k = _forget_terms(fz_ref[...], lb)
        logf = jnp.log(f)
        b = _mask_dot(mask_ref[0], logf)
        bend = _mask_dot(mask_ref[2], logf)
        qt_s[...] = (q * jnp.exp(b)).astype(BF16)
        kt_s[...] = (k * jnp.exp(bend - b)).astype(BF16)
        k_s[...] = k
        b_s[...] = b
        dec_s[...] = jnp.exp(bend)
        tidx = lax.broadcasted_iota(jnp.int32, (HG_HALF, HG_DIM), 0)

        def blk(i, carry):
            r0 = pl.multiple_of(i * HG_BLOCK, HG_BLOCK)
            rows = pl.ds(r0, HG_BLOCK)
            st = state[...]
            stb = st.astype(BF16)
            st_ref[i] = stb
            v = iv_ref[rows, :]
            qq = q_ref[rows, :]
            kk = k_s[rows, :]
            bb = b_s[rows, :]
            o = list(_halves(_dot_nt(qt_s[rows, :], stb)))
            qh, bh = _halves(qq), _halves(bb)
            for s in range(HG_BLOCK):
                ks, vs = kk[s:s + 1, :], v[s:s + 1, :]
                for h in _causal_halves(s):
                    e = _decay_from(bh[h], bb[s:s + 1, :], s, h, tidx)
                    acol = jnp.sum(qh[h] * (ks * e), axis=1, keepdims=True)
                    o[h] = o[h] + acol * vs
            oraw_ref[rows, :] = jnp.concatenate(o, axis=0)
            state[...] = st * dec_s[pl.ds(r0, 1), :] + _dot_tn(v.astype(BF16), kt_s[rows, :])
            return carry

        lax.fori_loop(0, nblk, blk, 0, unroll=HG_UNROLL)
        oraw = oraw_ref[...]
        r = lax.rsqrt(jnp.mean(oraw * oraw, axis=-1, keepdims=True) + LN_EPS)
        gg = gg_ref[...]
        oa_ref[...] = (oraw * r * gn_ref[...] * gg * _sigmoid(gg)).astype(BF16)

    def slab(off):
        return pl.BlockSpec((None, ct, HG_DIM), lambda h, c: (off + h, c, 0))

    return pl.pallas_call(
        body,
        grid=(nh, nct),
        in_specs=[slab(0), slab(nh), slab(2 * nh), slab(3 * nh),
                  pl.BlockSpec((None, 2, HG_DIM), lambda h, c: (h, 0, 0)),
                  pl.BlockSpec((1, HG_DIM), lambda h, c: (0, 0)),
                  pl.BlockSpec((3, mrows, mrows), lambda h, c: (0, 0, 0))],
        out_specs=[slab(0), pl.BlockSpec((ct, HG_DIM), lambda h, c: (c, h)),
                   pl.BlockSpec((None, nblk, HG_DIM, HG_DIM), lambda h, c: (h, c, 0, 0))],
        out_shape=[jax.ShapeDtypeStruct((nh, t, HG_DIM), F32),
                   jax.ShapeDtypeStruct((t, (nh + SG_GROUPS) * HG_DIM), BF16),
                   jax.ShapeDtypeStruct((nh, t // HG_BLOCK, HG_DIM, HG_DIM), BF16)],
        scratch_shapes=[pltpu.VMEM((HG_DIM, HG_DIM), F32), pltpu.VMEM((ct, HG_DIM), BF16),
                        pltpu.VMEM((ct, HG_DIM), BF16), pltpu.VMEM((ct, HG_DIM), F32),
                        pltpu.VMEM((ct, HG_DIM), F32), pltpu.VMEM((ct, HG_DIM), F32)],
        compiler_params=_params(2),
        name="hgrn_fwd",
    )(proj, proj, proj, proj, logits, gn, _block_masks(mrows))


def _sg_tile(t):
    return min(t, 512)


def _sgu_chunk_fwd(u, v, ln_g, ln_b, wm, bs):
    ua, dua = _gelu_and_grad(u)
    va, dva = _gelu_and_grad(v)
    vn, xhat, rstd = _ln_fwd(va, ln_g, ln_b)
    s = _dot(wm, vn.astype(BF16)) + bs
    return ua, dua, dva, vn, xhat, rstd, s


def _tril_weight(w):
    n = SG_CHUNK
    r = lax.broadcasted_iota(jnp.int32, (n, n), 0)
    c = lax.broadcasted_iota(jnp.int32, (n, n), 1)
    return jnp.where(c <= r, w, 0.0)


def _bias_by_position(b_row):
    return jnp.broadcast_to(b_row, (SG_CHUNK, SG_CHUNK)).T


def _sgu_fwd(proj, mix, ln_g, ln_b, w_s, b_row):
    t = proj.shape[1]
    ct = _sg_tile(t)
    ng = SG_GROUPS
    wide = ng * SG_DIM
    blk_u = 4 * HG_HEADS // ng

    def body(u_ref, v_ref, g_ref, b_ref, w_ref, bs_ref, mix_ref, o_ref):
        del mix_ref
        for g in range(ng):
            lanes = slice(g * SG_DIM, (g + 1) * SG_DIM)
            wm = _tril_weight(w_ref[g]).astype(BF16)
            bs = _bias_by_position(bs_ref[g])
            for n in range(ct // SG_CHUNK):
                rows = slice(n * SG_CHUNK, (n + 1) * SG_CHUNK)
                ua, _, _, _, _, _, s = _sgu_chunk_fwd(u_ref[g, rows, :], v_ref[g, rows, :], g_ref[g], b_ref[g], wm, bs)
                o_ref[rows, lanes] = (ua * s).astype(BF16)

    full = lambda a: pl.BlockSpec(a.shape, lambda c: (0,) * a.ndim)
    return pl.pallas_call(
        body,
        grid=(t // ct,),
        in_specs=[pl.BlockSpec((ng, ct, SG_DIM), lambda c: (blk_u, c, 0)),
                  pl.BlockSpec((ng, ct, SG_DIM), lambda c: (blk_u + 1, c, 0)),
                  full(ln_g), full(ln_b), full(w_s), full(b_row), ANY],
        out_specs=pl.BlockSpec((ct, wide), lambda c: (c, 1)),
        out_shape=jax.ShapeDtypeStruct(mix.shape, mix.dtype),
        input_output_aliases={6: 0},
        compiler_params=_params(1),
        name="sgu_fwd",
    )(proj, proj, ln_g, ln_b, w_s, b_row, mix)


def _mem_kv(mem, g, b, wk, wv):
    m_len, d = mem.shape

    def body(m_ref, g_ref, b_ref, wk_ref, wv_ref, mb_ref, xh_ref, rs_ref, k_ref, v_ref):
        m, xhat, rstd = _ln_fwd(m_ref[...], g_ref[...], b_ref[...])
        mb = m.astype(BF16)
        mb_ref[...] = mb
        xh_ref[...] = xhat
        rs_ref[...] = rstd
        k_ref[...] = _dot(mb, wk_ref[...]).astype(BF16)
        v_ref[...] = _dot(mb, wv_ref[...]).astype(BF16)

    return pl.pallas_call(
        body,
        out_shape=[jax.ShapeDtypeStruct((m_len, d), BF16), jax.ShapeDtypeStruct((m_len, d), F32),
                   jax.ShapeDtypeStruct((m_len, 1), F32), jax.ShapeDtypeStruct((m_len, d), BF16),
                   jax.ShapeDtypeStruct((m_len, d), BF16)],
        compiler_params=pltpu.CompilerParams(vmem_limit_bytes=VMEM_LIMIT_V7X),
        name="mem_kv",
    )(mem, g, b, wk, wv)


def _softmax_rows(s):
    m = jnp.max(s, axis=-1, keepdims=True)
    p = jnp.exp(s - m)
    return p / jnp.sum(p, axis=-1, keepdims=True)


def _attn_fwd(hb, wq, kb, vb):
    t, d = hb.shape
    tm = _row_tile(t)
    dh = d // X_HEADS
    scale = dh ** -0.5

    def body(h_ref, wq_ref, k_ref, v_ref, q_ref, o_ref):
        q = _dot(h_ref[...], wq_ref[...]).astype(BF16)
        q_ref[...] = q
        for hd in range(X_HEADS):
            sl = slice(hd * dh, (hd + 1) * dh)
            p = _softmax_rows(_dot_nt(q[:, sl], k_ref[:, sl]) * scale)
            o_ref[:, sl] = _dot(p.astype(BF16), v_ref[:, sl]).astype(BF16)

    row = pl.BlockSpec((tm, d), lambda i: (i, 0))
    full = lambda a: pl.BlockSpec(a.shape, lambda i: (0, 0))
    return pl.pallas_call(
        body,
        grid=(t // tm,),
        in_specs=[row, full(wq), full(kb), full(vb)],
        out_specs=[row, row],
        out_shape=[jax.ShapeDtypeStruct((t, d), BF16), jax.ShapeDtypeStruct((t, d), BF16)],
        compiler_params=_params(1),
        name="attn_fwd",
    )(hb, wq, kb, vb)


def _ffn_bwd_act(dyb, wd, a, b, coef, name, deps=()):
    t, d = dyb.shape
    f = wd.shape[0]
    tm = _row_tile(t)
    tn = _col_tile(f)

    def body(dy_ref, wd_ref, a_ref, b_ref, da_ref, db_ref):
        dy = dy_ref[...]
        for c in range(f // tn):
            cols = slice(c * tn, (c + 1) * tn)
            ds = _dot_nt(dy, wd_ref[cols, :]) * coef
            silu, dsilu = _silu_and_grad(a_ref[:, cols].astype(F32))
            da_ref[:, cols] = (ds * b_ref[:, cols].astype(F32) * dsilu).astype(BF16)
            db_ref[:, cols] = (ds * silu).astype(BF16)

    act = pl.BlockSpec((tm, f), lambda i: (i, 0))
    return pl.pallas_call(
        _drop_deps(body, 4, len(deps)),
        grid=(t // tm,),
        in_specs=[pl.BlockSpec((tm, d), lambda i: (i, 0)), _resident(wd), act, act] + [ANY] * len(deps),
        out_specs=[act, act],
        out_shape=[jax.ShapeDtypeStruct((t, f), BF16), jax.ShapeDtypeStruct((t, f), BF16)],
        compiler_params=_params(1),
        name=name,
    )(dyb, wd, a, b, *deps)


def _ffn_bwd_fused(dy, dyb, wd, wg, wu, a, b, coef, ln, name):
    t, d = dy.shape
    f = wd.shape[0]
    tm = min(t, 256)
    tn = _col_tile(f)

    def body(dy_ref, dyb_ref, wd_ref, wg_ref, wu_ref, a_ref, b_ref, xh_ref, rs_ref, g_ref,
             da_ref, db_ref, dyo_ref, dyob_ref, dg_ref, dbl_ref):
        dyb_v = dyb_ref[...]
        dh = ALPHA * dy_ref[...]
        for c in range(f // tn):
            cols = slice(c * tn, (c + 1) * tn)
            ds = _dot_nt(dyb_v, wd_ref[cols, :]) * coef
            silu, dsilu = _silu_and_grad(a_ref[:, cols].astype(F32))
            da = (ds * b_ref[:, cols].astype(F32) * dsilu).astype(BF16)
            db = (ds * silu).astype(BF16)
            da_ref[:, cols] = da
            db_ref[:, cols] = db
            dh = dh + _dot_nt(da, wg_ref[:, cols]) + _dot_nt(db, wu_ref[:, cols])

        @pl.when(pl.program_id(0) == 0)
        def _():
            dg_ref[...] = jnp.zeros_like(dg_ref)
            dbl_ref[...] = jnp.zeros_like(dbl_ref)

        dyp, dg, dbl = _ln_bwd(dh, xh_ref[...], rs_ref[...], g_ref[...])
        dyo_ref[...] = dyp
        dyob_ref[...] = dyp.astype(BF16)
        dg_ref[...] += dg
        dbl_ref[...] += dbl

    row = pl.BlockSpec((tm, d), lambda i: (i, 0))
    act = pl.BlockSpec((tm, f), lambda i: (i, 0))
    vec = pl.BlockSpec((1, d), lambda i: (0, 0))
    return pl.pallas_call(
        body,
        grid=(t // tm,),
        in_specs=[row, row, _resident(wd), _resident(wg), _resident(wu), act, act, row,
                  pl.BlockSpec((tm, 1), lambda i: (i, 0)), vec],
        out_specs=[act, act, row, row, vec, vec],
        out_shape=[jax.ShapeDtypeStruct((t, f), BF16), jax.ShapeDtypeStruct((t, f), BF16),
                   jax.ShapeDtypeStruct((t, d), F32), jax.ShapeDtypeStruct((t, d), BF16),
                   jax.ShapeDtypeStruct((1, d), F32), jax.ShapeDtypeStruct((1, d), F32)],
        compiler_params=_params(1),
        name=name,
    )(dy, dyb, wd, wg, wu, a, b, *ln)


def _mm_tn(a, b, name, scale=1.0, deps=()):
    t, m = a.shape
    bs = list(b) if isinstance(b, (list, tuple)) else [b]
    n = sum(piece.shape[1] for piece in bs)
    tt = min(t, 1024)
    nt = t // tt
    tm_o, tn_o = m, n

    def body(a_ref, *refs):
        b_refs, (o_ref, acc) = refs[:len(bs)], refs[len(bs):]
        k = pl.program_id(2)

        @pl.when(k == 0)
        def _():
            acc[...] = jnp.zeros_like(acc)

        first = 0
        for b_ref in b_refs:
            cols = slice(first, first + b_ref.shape[1])
            acc[:, cols] += _dot_tn(a_ref[...], b_ref[...])
            first = cols.stop

        @pl.when(k == nt - 1)
        def _():
            o_ref[...] = (acc[...] * scale).astype(BF16)

    return pl.pallas_call(
        _drop_deps(body, 1 + len(bs), len(deps)),
        grid=(m // tm_o, n // tn_o, nt),
        in_specs=[pl.BlockSpec((tt, tm_o), lambda i, j, k: (k, i))]
        + [pl.BlockSpec((tt, piece.shape[1]), lambda i, j, k: (k, j)) for piece in bs] + [ANY] * len(deps),
        out_specs=pl.BlockSpec((tm_o, tn_o), lambda i, j, k: (i, j)),
        out_shape=pltpu.HBM((m, n), BF16),
        scratch_shapes=[pltpu.VMEM((tm_o, tn_o), F32)],
        compiler_params=_params(3),
        name=name,
    )(a, *bs, *deps)


def _mm_nt(lhs, w, name):
    t, d = lhs.shape
    kd = w.shape[0]
    tm = _row_tile(t)

    def body(l_ref, w_ref, o_ref):
        _store_slabs(o_ref, 0, _dot_nt(l_ref[...], w_ref[...]))

    return pl.pallas_call(
        body,
        grid=(t // tm,),
        in_specs=[pl.BlockSpec((tm, d), lambda i: (i, 0)), _resident(w)],
        out_specs=pl.BlockSpec((kd // LANES, tm, LANES), lambda i: (0, i, 0)),
        out_shape=jax.ShapeDtypeStruct((kd // LANES, t, LANES), F32),
        compiler_params=_params(1),
        name=name,
    )(lhs, w)


def _dx_ln(dy, pairs, ln, name, deps=()):
    t, d = dy.shape
    npair = len(pairs)
    pairs = [(list(lhs) if isinstance(lhs, (list, tuple)) else [lhs], w) for lhs, w in pairs]
    tm = min(t, 512 // npair)
    nt = t // tm
    n_in = 1 + sum(len(pieces) + 1 for pieces, _ in pairs) + (3 if ln is not None else 0)

    def body(*refs):
        dy_ref = refs[0]
        pos = 1
        dh = ALPHA * dy_ref[...]
        for pieces, _ in pairs:
            w_ref = refs[pos + len(pieces)]
            first = 0
            for l_ref in refs[pos:pos + len(pieces)]:
                cols = slice(first, first + l_ref.shape[1])
                dh = dh + _dot_nt(l_ref[...], w_ref[:, cols])
                first = cols.stop
            pos += len(pieces) + 1
        if ln is not None:
            xh_ref, rs_ref, g_ref = refs[pos:pos + 3]
            dyo_ref, dyb_ref, dg_ref, db_ref = refs[pos + 3:pos + 7]

            @pl.when(pl.program_id(0) == 0)
            def _():
                dg_ref[...] = jnp.zeros_like(dg_ref)
                db_ref[...] = jnp.zeros_like(db_ref)

            dyp, dg, db = _ln_bwd(dh, xh_ref[...], rs_ref[...], g_ref[...])
            dyo_ref[...] = dyp
            dyb_ref[...] = dyp.astype(BF16)
            dg_ref[...] += dg
            db_ref[...] += db
        else:
            refs[pos][...] = dh

    row = pl.BlockSpec((tm, d), lambda i: (i, 0))
    vec = pl.BlockSpec((1, d), lambda i: (0, 0))
    in_specs = [row]
    args = [dy]
    for pieces, w in pairs:
        in_specs += [pl.BlockSpec((tm, piece.shape[1]), lambda i: (i, 0)) for piece in pieces] + [_resident(w)]
        args += pieces + [w]
    if ln is not None:
        in_specs += [row, pl.BlockSpec((tm, 1), lambda i: (i, 0)), vec]
        args += list(ln)
        out_specs = [row, row, vec, vec]
        out_shape = [jax.ShapeDtypeStruct((t, d), F32), jax.ShapeDtypeStruct((t, d), BF16),
                     jax.ShapeDtypeStruct((1, d), F32), jax.ShapeDtypeStruct((1, d), F32)]
    else:
        out_specs = row
        out_shape = jax.ShapeDtypeStruct((t, d), F32)
    return pl.pallas_call(
        _drop_deps(body, n_in, len(deps)),
        grid=(nt,),
        in_specs=in_specs + [ANY] * len(deps),
        out_specs=out_specs,
        out_shape=out_shape,
        compiler_params=_params(1),
        name=name,
    )(*args, *deps)


def _hgrn_bwd(proj, oraw, dmix, states, logits, gn):
    t = proj.shape[1]
    ct = _hg_tile(t)
    nct = t // ct
    nblk = ct // HG_BLOCK
    nh = HG_HEADS
    mrows = min(ct, 256)

    def body(q_ref, fz_ref, iv_ref, gg_ref, or_ref, do_ref, st_ref, lg_ref, gn_ref, mask_ref,
             dq_ref, dfz_ref, div_ref, dgg_ref, dlg_ref, dgn_ref,
             dstate, qt_s, kt_s, k_s, b_s, eb_s, ekb_s, dec_s, dor_s, dbl_s, gr_s, dk_s, dlb_acc):
        c = pl.program_id(1)

        @pl.when(c == 0)
        def _():
            dstate[...] = jnp.zeros_like(dstate)
            dlb_acc[...] = jnp.zeros_like(dlb_acc)
            dgn_ref[...] = jnp.zeros_like(dgn_ref)

        lb = _lower_bound(lg_ref[...])
        q = q_ref[...]
        sig, nsig, f, k = _forget_terms(fz_ref[...], lb)
        logf = jnp.log(f)
        b = _mask_dot(mask_ref[0], logf)
        bend = _mask_dot(mask_ref[2], logf)
        eb = jnp.exp(b)
        ekb = jnp.exp(bend - b)
        qt_s[...] = (q * eb).astype(BF16)
        kt_s[...] = (k * ekb).astype(BF16)
        k_s[...] = k
        b_s[...] = b
        eb_s[...] = eb
        ekb_s[...] = ekb
        dec_s[...] = jnp.exp(bend)
        oraw = or_ref[...]
        r = lax.rsqrt(jnp.mean(oraw * oraw, axis=-1, keepdims=True) + LN_EPS)
        on = oraw * r
        gg = gg_ref[...]
        silu, dsilu = _silu_and_grad(gg)
        doa = do_ref[...]
        gnv = gn_ref[...]
        dgg_ref[...] = (doa * on * gnv * dsilu).astype(BF16)
        dyn = doa * silu
        dgn_ref[...] += jnp.sum(dyn * on, axis=0, keepdims=True)
        don = dyn * gnv
        dor_s[...] = r * (don - on * jnp.mean(don * on, axis=-1, keepdims=True))
        tidx = lax.broadcasted_iota(jnp.int32, (HG_HALF, HG_DIM), 0)

        def blk(ii, carry):
            i = nblk - 1 - ii
            r0 = pl.multiple_of(i * HG_BLOCK, HG_BLOCK)
            rows = pl.ds(r0, HG_BLOCK)
            st = st_ref[i]
            dst = dstate[...]
            dstb = dst.astype(BF16)
            do = dor_s[rows, :]
            dob = do.astype(BF16)
            v = iv_ref[rows, :]
            vb = v.astype(BF16)
            qq = q_ref[rows, :]
            kk = k_s[rows, :]
            bb = b_s[rows, :]
            qt = qt_s[rows, :]
            kt = kt_s[rows, :]
            dec = dec_s[pl.ds(r0, 1), :]
            dkt = _dot(vb, dstb)
            dq = _dot(dob, st) * eb_s[rows, :]
            dk = dkt * ekb_s[rows, :]
            dv = _dot_nt(kt, dstb)
            gend = (jnp.sum(kk * dk, axis=0, keepdims=True)
                    + dec * jnp.sum(dst * st.astype(F32), axis=0, keepdims=True))
            qh, bh, doh = _halves(qq), _halves(bb), _halves(do)
            dqh, dkh, dvh = list(_halves(dq)), list(_halves(dk)), list(_halves(dv))
            for s in range(HG_BLOCK):
                ks, vs = kk[s:s + 1, :], v[s:s + 1, :]
                dk_part = dv_part = None
                for h in _causal_halves(s):
                    e = _decay_from(bh[h], bb[s:s + 1, :], s, h, tidx)
                    ke = ks * e
                    acol = jnp.sum(qh[h] * ke, axis=1, keepdims=True)
                    dacol = jnp.sum(doh[h] * vs, axis=1, keepdims=True)
                    dqh[h] = dqh[h] + dacol * ke
                    pk = dacol * (qh[h] * e)
                    pv = acol * doh[h]
                    dk_part = pk if dk_part is None else dk_part + pk
                    dv_part = pv if dv_part is None else dv_part + pv
                hs, row = divmod(s, HG_HALF)
                dkh[hs] = dkh[hs] + jnp.where(tidx == row, jnp.sum(dk_part, axis=0, keepdims=True), 0.0)
                dvh[hs] = dvh[hs] + jnp.where(tidx == row, jnp.sum(dv_part, axis=0, keepdims=True), 0.0)
            dq = jnp.concatenate(dqh, axis=0)
            dk = jnp.concatenate(dkh, axis=0)
            dv = jnp.concatenate(dvh, axis=0)
            dq_ref[rows, :] = dq.astype(BF16)
            div_ref[rows, :] = dv.astype(BF16)
            dk_s[rows, :] = dk
            dbl_s[rows, :] = qq * dq - kk * dk
            gr_s[rows, :] = jnp.zeros((HG_BLOCK, HG_DIM), F32) + gend
            dstate[...] = dst * dec + _dot_tn(dob, qt)
            return carry

        lax.fori_loop(0, nblk, blk, 0, unroll=HG_UNROLL)
        dlogf = _mask_dot(mask_ref[1], dbl_s[...]) + gr_s[...]
        dk = dk_s[...]
        dfz_ref[...] = ((dlogf / f - dk) * ((1.0 - lb) * sig * nsig)).astype(BF16)
        dlb_acc[...] += jnp.sum((dlogf / f - dk) * nsig, axis=0, keepdims=True)

        @pl.when(c == nct - 1)
        def _():
            dl0 = dlb_acc[...] * lb * (1.0 - lb)
            layer = lax.broadcasted_iota(jnp.int32, (2, HG_DIM), 0)
            dlg_ref[...] = jnp.where(layer == 0, dl0, -dl0)

    def slab(off):
        return pl.BlockSpec((None, ct, HG_DIM), lambda h, c: (off + h, nct - 1 - c, 0))

    out_slab = pl.BlockSpec((ct, HG_DIM), lambda h, c: (nct - 1 - c, h))
    tile_f32 = pltpu.VMEM((ct, HG_DIM), F32)
    tile_b16 = pltpu.VMEM((ct, HG_DIM), BF16)
    slab_shape = pltpu.HBM((t, nh * HG_DIM), BF16)
    return pl.pallas_call(
        body,
        grid=(nh, nct),
        in_specs=[slab(0), slab(nh), slab(2 * nh), slab(3 * nh), slab(0), slab(0),
                  pl.BlockSpec((None, nblk, HG_DIM, HG_DIM), lambda h, c: (h, nct - 1 - c, 0, 0)),
                  pl.BlockSpec((None, 2, HG_DIM), lambda h, c: (h, 0, 0)),
                  pl.BlockSpec((1, HG_DIM), lambda h, c: (0, 0)),
                  pl.BlockSpec((3, mrows, mrows), lambda h, c: (0, 0, 0))],
        out_specs=[out_slab, out_slab, out_slab, out_slab,
                   pl.BlockSpec((None, 2, HG_DIM), lambda h, c: (h, 0, 0)),
                   pl.BlockSpec((None, 1, HG_DIM), lambda h, c: (h, 0, 0))],
        out_shape=[slab_shape, slab_shape, slab_shape, slab_shape,
                   jax.ShapeDtypeStruct((nh, 2, HG_DIM), F32), jax.ShapeDtypeStruct((nh, 1, HG_DIM), F32)],
        scratch_shapes=[pltpu.VMEM((HG_DIM, HG_DIM), F32), tile_b16, tile_b16, tile_f32, tile_f32, tile_f32, tile_f32,
                        tile_f32, tile_f32, tile_f32, tile_f32, tile_f32, pltpu.VMEM((1, HG_DIM), F32)],
        compiler_params=_params(2),
        name="hgrn_bwd",
    )(proj, proj, proj, proj, oraw, dmix, states, logits, gn, _block_masks(mrows))


def _sgu_bwd(proj, dmix, ln_g, ln_b, w_s, b_row):
    t = proj.shape[1]
    ct = _sg_tile(t)
    nct = t // ct
    ng = SG_GROUPS
    off_u = 4 * HG_HEADS
    off_v = off_u + ng
    n = SG_CHUNK

    def body(u_ref, v_ref, do_ref, g_ref, b_ref, w_ref, bs_ref, du_ref, dv_ref, dg_ref, db_ref, dw_ref, dbs_ref):
        c = pl.program_id(1)

        @pl.when(c == 0)
        def _():
            dg_ref[...] = jnp.zeros_like(dg_ref)
            db_ref[...] = jnp.zeros_like(db_ref)
            dw_ref[...] = jnp.zeros_like(dw_ref)
            dbs_ref[...] = jnp.zeros_like(dbs_ref)

        r = lax.broadcasted_iota(jnp.int32, (n, n), 0)
        cc = lax.broadcasted_iota(jnp.int32, (n, n), 1)
        wm = jnp.where(cc <= r, w_ref[...], 0.0).astype(BF16)
        wmt = jnp.where(r <= cc, w_ref[...].T, 0.0).astype(BF16)
        bs = _bias_by_position(bs_ref[...])
        for ci in range(ct // n):
            rows = slice(ci * n, (ci + 1) * n)
            ua, dua, dva, vn, xhat, rstd, s = _sgu_chunk_fwd(u_ref[rows, :], v_ref[rows, :], g_ref[...], b_ref[...],
                                                             wm, bs)
            do = do_ref[rows, :]
            du_ref[rows, :] = (do * s * dua).astype(BF16)
            ds = do * ua
            dsb = ds.astype(BF16)
            dbs_ref[...] += jnp.sum(ds, axis=1, keepdims=True)
            dw_ref[...] += _dot_nt(dsb, vn.astype(BF16))
            dvn = _dot(wmt, dsb)
            dva_in, dg, db = _ln_bwd(dvn, xhat, rstd, g_ref[...])
            dg_ref[...] += dg
            db_ref[...] += db
            dv_ref[rows, :] = (dva_in * dva).astype(BF16)

        @pl.when(c == nct - 1)
        def _():
            dw_ref[...] = jnp.where(cc <= r, dw_ref[...], 0.0)

    vec = pl.BlockSpec((None, 1, SG_DIM), lambda g, c: (g, 0, 0))
    mat = pl.BlockSpec((None, n, n), lambda g, c: (g, 0, 0))
    col = pl.BlockSpec((None, n, 1), lambda g, c: (g, 0, 0))
    out_slab = pl.BlockSpec((ct, SG_DIM), lambda g, c: (c, g))
    return pl.pallas_call(
        body,
        grid=(ng, nct),
        in_specs=[pl.BlockSpec((None, ct, SG_DIM), lambda g, c: (off_u + g, c, 0)),
                  pl.BlockSpec((None, ct, SG_DIM), lambda g, c: (off_v + g, c, 0)),
                  pl.BlockSpec((None, ct, SG_DIM), lambda g, c: (ng + g, c, 0)), vec, vec, mat, vec],
        out_specs=[out_slab, out_slab, vec, vec, mat, col],
        out_shape=[pltpu.HBM((t, ng * SG_DIM), BF16), pltpu.HBM((t, ng * SG_DIM), BF16),
                   jax.ShapeDtypeStruct((ng, 1, SG_DIM), F32), jax.ShapeDtypeStruct((ng, 1, SG_DIM), F32),
                   jax.ShapeDtypeStruct((ng, n, n), F32), jax.ShapeDtypeStruct((ng, n, 1), F32)],
        compiler_params=_params(2),
        name="sgu_bwd",
    )(proj, proj, dmix, ln_g, ln_b, w_s, b_row)


def _attn_bwd(dyb, wo, qb, kb, vb):
    t, d = dyb.shape
    m_len = kb.shape[0]
    tm = _row_tile(t)
    dh = d // X_HEADS
    scale = dh ** -0.5

    def body(dy_ref, wo_ref, q_ref, k_ref, v_ref, dq_ref, dk_ref, dv_ref):
        i = pl.program_id(0)

        @pl.when(i == 0)
        def _():
            dk_ref[...] = jnp.zeros_like(dk_ref)
            dv_ref[...] = jnp.zeros_like(dv_ref)

        do = _dot_nt(dy_ref[...], wo_ref[...]).astype(BF16)
        for hd in range(X_HEADS):
            sl = slice(hd * dh, (hd + 1) * dh)
            qh = q_ref[:, sl]
            p = _softmax_rows(_dot_nt(qh, k_ref[:, sl]) * scale)
            doh = do[:, sl]
            dp = _dot_nt(doh, v_ref[:, sl])
            ds = (p * (dp - jnp.sum(dp * p, axis=-1, keepdims=True)) * scale).astype(BF16)
            dq_ref[:, sl] = _dot(ds, k_ref[:, sl]).astype(BF16)
            dk_ref[:, sl] += _dot_tn(ds, qh)
            dv_ref[:, sl] += _dot_tn(p.astype(BF16), doh)

    row = pl.BlockSpec((tm, d), lambda i: (i, 0))
    full = lambda a: pl.BlockSpec(a.shape, lambda i: (0, 0))
    kv = pl.BlockSpec((m_len, d), lambda i: (0, 0))
    return pl.pallas_call(
        body,
        grid=(t // tm,),
        in_specs=[row, full(wo), row, full(kb), full(vb)],
        out_specs=[row, kv, kv],
        out_shape=[jax.ShapeDtypeStruct((t, d), BF16), jax.ShapeDtypeStruct((m_len, d), F32),
                   jax.ShapeDtypeStruct((m_len, d), F32)],
        compiler_params=_params(1),
        name="attn_bwd",
    )(dyb, wo, qb, kb, vb)


def _mem_bwd(dk, dv, mb, xhat, rstd, g, wk, wv):
    m_len, d = dk.shape

    def body(dk_ref, dv_ref, mb_ref, xh_ref, rs_ref, g_ref, wk_ref, wv_ref, gwk_ref, gwv_ref, dg_ref, db_ref):
        dkb = dk_ref[...].astype(BF16)
        dvb = dv_ref[...].astype(BF16)
        mb_v = mb_ref[...]
        gwk_ref[...] = _dot_tn(mb_v, dkb).astype(BF16)
        gwv_ref[...] = _dot_tn(mb_v, dvb).astype(BF16)
        dm = _dot_nt(dkb, wk_ref[...]) + _dot_nt(dvb, wv_ref[...])
        _, dg, db = _ln_bwd(dm, xh_ref[...], rs_ref[...], g_ref[...])
        dg_ref[...] = dg
        db_ref[...] = db

    return pl.pallas_call(
        body,
        out_shape=[jax.ShapeDtypeStruct((d, d), BF16), jax.ShapeDtypeStruct((d, d), BF16),
                   jax.ShapeDtypeStruct((1, d), F32), jax.ShapeDtypeStruct((1, d), F32)],
        compiler_params=pltpu.CompilerParams(vmem_limit_bytes=VMEM_LIMIT_V7X),
        name="mem_bwd",
    )(dk, dv, mb, xhat, rstd, g, wk, wv)


def _adamw(w, g, m, v):
    m = ADAM_B1 * m + (1.0 - ADAM_B1) * g
    v = ADAM_B2 * v + (1.0 - ADAM_B2) * (g * g)
    m_hat = m / (1.0 - ADAM_B1 ** ADAM_STEP)
    v_hat = v / (1.0 - ADAM_B2 ** ADAM_STEP)
    delta = -ADAM_LR * (m_hat / (jnp.sqrt(v_hat) + ADAM_EPS) + ADAM_WD * w)
    return delta, m, v


def _slot_sum(ref):
    g = ref[0].astype(F32)
    for s in range(1, N_DEV):
        g = g + ref[s].astype(F32)
    return g


def _adam_sharded(lands, w, m, v, axis, name):
    rows, cols = w.shape
    nl = len(lands)
    transposed = axis == 1 and nl == 2
    if transposed:
        rows, cols = cols, rows
        tr = 256
        grid = (rows // tr,)
        wblk = pl.BlockSpec((cols, tr), lambda i: (0, i))
        lblk = [pl.BlockSpec((N_DEV, tr, a.shape[2]), lambda i: (0, i, 0)) for a in lands]
    elif axis == 1:
        tr = 256 if rows % 256 == 0 else rows
        grid = (rows // tr,)
        wblk = pl.BlockSpec((tr, cols), lambda i: (i, 0))
        lblk = [pl.BlockSpec((N_DEV, tr, a.shape[2]), lambda i: (0, i, 0)) for a in lands]
    else:
        tc = _col_tile(cols)
        grid = (cols // tc,)
        wblk = pl.BlockSpec((rows, tc), lambda i: (0, i))
        lblk = [pl.BlockSpec((N_DEV, a.shape[1], tc), lambda i: (0, 0, i)) for a in lands]

    def body(*refs):
        w_ref, m_ref, v_ref = refs[nl:nl + 3]
        g_ref, d_ref, nm_ref, nv_ref = refs[nl + 3:]
        g = _slot_sum(refs[0])
        if nl == 2:
            tail = _slot_sum(refs[1])
            if transposed:
                g = jnp.concatenate([g.T, tail.T[:cols - g.shape[1], :]], axis=0)
            elif axis == 1:
                g = jnp.concatenate([g, tail[:, :cols - g.shape[1]]], axis=1)
            else:
                g = jnp.concatenate([g, tail[:rows - g.shape[0], :]], axis=0)
        delta, nm, nv = _adamw(w_ref[...], g, m_ref[...], v_ref[...])
        g_ref[...] = g
        d_ref[...] = delta
        nm_ref[...] = nm
        nv_ref[...] = nv

    shp = pltpu.HBM(w.shape, F32)
    return pl.pallas_call(
        body,
        grid=grid,
        in_specs=lblk + [wblk, wblk, wblk],
        out_specs=[wblk, wblk, wblk, wblk],
        out_shape=[shp, shp, shp, shp],
        compiler_params=_params(1),
        name=name,
    )(*[pltpu.with_memory_space_constraint(a, pltpu.HBM) for a in (*lands, w, m, v)])


def _mesh_pos():
    return lax.axis_index("x"), lax.axis_index("y"), lax.axis_index("c")


def _peer(k):
    x, y, c = _mesh_pos()
    pos = (x ^ (k >> 2), y ^ ((k >> 1) & 1), c ^ (k & 1))
    return pos, 4 * pos[0] + 2 * pos[1] + pos[2]


def _sem_index(row, k):
    return row * (N_DEV - 1) + k - 1


def _window(ref, axis, start, size):
    align = 16 if axis == 0 else LANES
    start = pl.multiple_of(start, align)
    return ref.at[pl.ds(start, size), :] if axis == 0 else ref.at[:, pl.ds(start, size)]


def _piece_refs(piece, srcs, lands, me, peer):
    kind, si, li, axis, base, stride, shape = piece
    if kind == "gather":
        return srcs[si], _window(lands[li], axis, base + stride * me, shape[axis])
    return _window(srcs[si], axis, base + stride * peer, shape[axis]), lands[li].at[me]


def _place_own(srcs, land_shapes, pieces, name):
    ns, nl, npc = len(srcs), len(land_shapes), len(pieces)

    def body(*refs):
        s_refs = refs[:ns]
        l_refs = refs[ns:ns + nl]
        bufs = refs[ns + nl:ns + nl + npc]
        sems = refs[ns + nl + npc]
        x, y, c = _mesh_pos()
        me = 4 * x + 2 * y + c
        loads = []
        for p, piece in enumerate(pieces):
            src, dst = _piece_refs(piece, s_refs, l_refs, me, me)
            cp = pltpu.make_async_copy(src, bufs[p], sems.at[0, p])
            cp.start()
            loads.append((cp, dst))
        stores = []
        for p, (cp, dst) in enumerate(loads):
            cp.wait()
            out = pltpu.make_async_copy(bufs[p], dst, sems.at[1, p])
            out.start()
            stores.append(out)
        for out in stores:
            out.wait()

    out = pl.pallas_call(
        body,
        in_specs=[ANY] * ns,
        out_specs=[HBM] * nl,
        out_shape=[pltpu.HBM(s.shape, s.dtype) for s in land_shapes],
        scratch_shapes=[pltpu.VMEM(pc[6], srcs[pc[1]].dtype) for pc in pieces] + [pltpu.SemaphoreType.DMA((2, npc))],
        compiler_params=pltpu.CompilerParams(vmem_limit_bytes=VMEM_LIMIT_V7X),
        name=name,
    )(*srcs)
    return list(out)


def _comm_start(srcs, lands, pieces, groups, name, after=()):
    ns, nl, na, ng = len(srcs), len(lands), len(after), len(groups)

    def body(*refs):
        s_refs = refs[:ns]
        l_refs = refs[ns:ns + nl]
        outs = refs[ns + nl + na:]
        sems = outs[:2 * ng]
        token = outs[-1]
        x, y, c = _mesh_pos()
        me = 4 * x + 2 * y + c
        for g, members in enumerate(groups):
            for row, p in enumerate(members):
                for k in range(1, N_DEV):
                    pos, peer = _peer(k)
                    src, dst = _piece_refs(pieces[p], s_refs, l_refs, me, peer)
                    pltpu.make_async_remote_copy(src_ref=src, dst_ref=dst, send_sem=sems[2 * g].at[_sem_index(row, k)],
                                                 recv_sem=sems[2 * g + 1].at[_sem_index(row, k)], device_id=pos,
                                                 device_id_type=MESH_ID).start()
        token[...] = jnp.zeros_like(token)

    sem_shapes = []
    for members in groups:
        sem_shapes += [pltpu.SemaphoreType.DMA((len(members) * (N_DEV - 1),))] * 2
    hbm_of = lambda a: pltpu.HBM(a.shape, a.dtype)
    out = pl.pallas_call(
        body,
        in_specs=[HBM] * (ns + nl) + [ANY] * na,
        out_specs=[SEM] * (2 * ng) + [HBM] * (ns + nl) + [pl.BlockSpec(memory_space=pltpu.VMEM)],
        out_shape=sem_shapes + [hbm_of(a) for a in srcs] + [hbm_of(a) for a in lands]
        + [jax.ShapeDtypeStruct((8, LANES), F32)],
        input_output_aliases={i: 2 * ng + i for i in range(ns + nl)},
        compiler_params=pltpu.CompilerParams(has_side_effects=DATAFLOW),
        name=name,
    )(*[pltpu.with_memory_space_constraint(a, pltpu.HBM) for a in list(srcs) + list(lands)], *after)
    sems = [(out[2 * g], out[2 * g + 1]) for g in range(ng)]
    return sems, list(out[2 * ng:2 * ng + ns]), list(out[2 * ng + ns:2 * ng + ns + nl]), out[-1]


def _comm_wait(srcs, lands, pieces, members, sems, after, name):
    ns, nl, na = len(srcs), len(lands), len(after)

    def body(*refs):
        s_refs = refs[:ns]
        l_refs = refs[ns:ns + nl]
        send_sems, recv_sems = refs[ns + nl:ns + nl + 2]
        x, y, c = _mesh_pos()
        me = 4 * x + 2 * y + c
        for row, p in enumerate(members):
            for k in range(1, N_DEV):
                pos, peer = _peer(k)
                src, dst = _piece_refs(pieces[p], s_refs, l_refs, me, peer)
                cp = pltpu.make_async_remote_copy(src_ref=src, dst_ref=dst, send_sem=send_sems.at[_sem_index(row, k)],
                                                  recv_sem=recv_sems.at[_sem_index(row, k)], device_id=pos,
                                                  device_id_type=MESH_ID)
                cp.wait_send()
                cp.wait_recv()

    hbm_of = lambda a: pltpu.HBM(a.shape, a.dtype)
    out = pl.pallas_call(
        body,
        in_specs=[HBM] * (ns + nl) + [SEM, SEM] + [ANY] * na,
        out_specs=[HBM] * (ns + nl),
        out_shape=[hbm_of(a) for a in srcs] + [hbm_of(a) for a in lands],
        input_output_aliases={i: i for i in range(ns + nl)},
        compiler_params=pltpu.CompilerParams(has_side_effects=DATAFLOW),
        name=name,
    )(*srcs, *lands, sems[0], sems[1], *after)
    return list(out[ns:])


def _landed_block(piece, lands, owner):
    _, _, li, axis, base, stride, shape = piece
    return _window(lands[li], axis, base + stride * owner, shape[axis])


def _copy_stage(name, bufs, in_sems, out_sem_sizes, emit, after=()):
    nb, ni, no, na = len(bufs), len(in_sems), len(out_sem_sizes), len(after)

    def body(*refs):
        b_refs = refs[:nb]
        i_refs = refs[nb:nb + ni]
        o_refs = refs[nb + ni + na:nb + ni + na + no]
        emit(b_refs, i_refs, o_refs)
        refs[-1][...] = jnp.zeros_like(refs[-1])

    hbm_of = lambda a: pltpu.HBM(a.shape, a.dtype)
    out = pl.pallas_call(
        body,
        in_specs=[HBM] * nb + [SEM] * ni + [ANY] * na,
        out_specs=[SEM] * no + [HBM] * nb + [pl.BlockSpec(memory_space=pltpu.VMEM)],
        out_shape=[pltpu.SemaphoreType.DMA((n,)) for n in out_sem_sizes] + [hbm_of(a) for a in bufs]
        + [jax.ShapeDtypeStruct((8, LANES), F32)],
        input_output_aliases={i: no + i for i in range(nb)},
        compiler_params=pltpu.CompilerParams(has_side_effects=DATAFLOW),
        name=name,
    )(*[pltpu.with_memory_space_constraint(a, pltpu.HBM) for a in bufs], *in_sems, *after)
    return list(out[:no]), list(out[no:no + nb]), out[-1]


def _remote(src, dst, send, recv, to):
    return pltpu.make_async_remote_copy(src_ref=src, dst_ref=dst, send_sem=send, recv_sem=recv, device_id=to,
                                        device_id_type=MESH_ID)


def _routed_gather(srcs, lands, pieces, meanwhile, name):
    ns, npc = len(srcs), len(pieces)

    def places():
        x, y, c = _mesh_pos()
        index = lambda p: 4 * p[0] + 2 * p[1] + p[2]
        me, sib = (x, y, c), (x, y, 1 - c)
        xnb, ynb = (1 - x, y, c), (x, 1 - y, c)
        got_first = (x ^ (1 - c), y ^ c, c)
        pass_to = (x ^ c, y ^ (1 - c), c)
        diag = (1 - x, 1 - y, c)
        return index, me, sib, xnb, ynb, got_first, pass_to, diag

    def start(b, _, o):
        index, me, sib, xnb, ynb, *_rest = places()
        send_a, recv_sib, recv_nb = o
        for p, piece in enumerate(pieces):
            src, dst = _piece_refs(piece, b[:ns], b[ns:], index(me), 0)
            _remote(src, dst, send_a.at[3 * p], recv_sib.at[p], sib).start()
            _remote(src, dst, send_a.at[3 * p + 1], recv_nb.at[2 * p], xnb).start()
            _remote(src, dst, send_a.at[3 * p + 2], recv_nb.at[2 * p + 1], ynb).start()

    def pass_a(b, i, o):
        index, me, sib, xnb, ynb, got_first, pass_to, _diag = places()
        (recv_nb,) = i
        send_f, recv_f, send_d, recv_d = o
        for p, piece in enumerate(pieces):
            for j, nb in enumerate((xnb, ynb)):
                blk = _landed_block(piece, b, index(nb))
                _remote(blk, blk, send_f.at[2 * p + j], recv_nb.at[2 * p + j], sib).wait_recv()
                _remote(blk, blk, send_f.at[2 * p + j], recv_f.at[2 * p + j], sib).start()
            blk = _landed_block(piece, b, index(got_first))
            _remote(blk, blk, send_d.at[p], recv_d.at[p], pass_to).start()

    def pass_b(b, i, o):
        index, me, sib, *_mid, diag = places()
        (recv_d,) = i
        send_g, recv_g = o
        for p, piece in enumerate(pieces):
            blk = _landed_block(piece, b, index(diag))
            _remote(blk, blk, send_g.at[p], recv_d.at[p], sib).wait_recv()
            _remote(blk, blk, send_g.at[p], recv_g.at[p], sib).start()

    def last(b, i, _):
        index, me, sib, *_others = places()
        send_a, recv_sib, send_f, recv_f, send_d, send_g, recv_g = i
        for p, piece in enumerate(pieces):
            src, dst = _piece_refs(piece, b[:ns], b[ns:], index(me), 0)
            cp = lambda s_sem, r_sem: _remote(src, dst, s_sem, r_sem, sib)
            cp(send_a.at[3 * p], recv_sib.at[p]).wait_recv()
            cp(send_a.at[3 * p], recv_g.at[p]).wait_recv()
            for j in range(3):
                cp(send_a.at[3 * p + j], recv_sib.at[p]).wait_send()
            for j in range(2):
                cp(send_f.at[2 * p + j], recv_f.at[2 * p + j]).wait_recv()
                cp(send_f.at[2 * p + j], recv_f.at[2 * p + j]).wait_send()
            cp(send_d.at[p], recv_sib.at[p]).wait_send()
            cp(send_g.at[p], recv_sib.at[p]).wait_send()

    (send_a, recv_sib, recv_nb), bufs, started = _copy_stage(name + "_start", list(srcs) + list(lands), [],
                                                             [3 * npc, npc, 2 * npc], start)
    srcs, lands = bufs[:ns], bufs[ns:]
    (send_f, recv_f, send_d, recv_d), lands, _ = _copy_stage(name + "_pass_a", lands, [recv_nb],
                                                             [2 * npc, 2 * npc, npc, npc],
                                                             lambda b, i, o: pass_a(b, i, o), after=meanwhile(started))
    (send_g, recv_g), lands, tok = _copy_stage(name + "_pass_b", lands, [recv_d], [npc, npc], pass_b)
    _, bufs, _ = _copy_stage(name + "_last", list(srcs) + list(lands),
                             [send_a, recv_sib, send_f, recv_f, send_d, send_g, recv_g], [], last)
    return bufs[ns:], tok


def _paired_gather(srcs, lands, pieces, groups, after, name):
    ns, ng = len(srcs), len(groups)
    far = (1, 2, 3)

    def me_sib():
        x, y, c = _mesh_pos()
        return 4 * x + 2 * y + c, (x, y, 1 - c)

    def start(b, _, o):
        me, sib = me_sib()
        for g, members in enumerate(groups):
            send, recv_sib, recv_far = o[3 * g:3 * g + 3]
            for r, p in enumerate(members):
                src, dst = _piece_refs(pieces[p], b[:ns], b[ns:], me, 0)
                _remote(src, dst, send.at[4 * r], recv_sib.at[r], sib).start()
                for j in far:
                    _remote(src, dst, send.at[4 * r + j], recv_far.at[3 * r + j - 1], _peer(2 * j)[0]).start()

    def forward(b, i, o):
        _, sib = me_sib()
        for g, members in enumerate(groups):
            send_f, recv_f = o[2 * g:2 * g + 2]
            for r, p in enumerate(members):
                for j in far:
                    blk = _landed_block(pieces[p], b, _peer(2 * j)[1])
                    _remote(blk, blk, send_f.at[3 * r + j - 1], i[g].at[3 * r + j - 1], sib).wait_recv()
                    _remote(blk, blk, send_f.at[3 * r + j - 1], recv_f.at[3 * r + j - 1], sib).start()

    def last(sub):
        def emit(b, i, _):
            send, recv_sib, send_f, recv_f = i
            me, sib = me_sib()
            for r, piece in enumerate(sub):
                src, dst = _piece_refs(piece, b[:-1], b[-1:], me, 0)
                _remote(src, dst, send.at[4 * r], recv_sib.at[r], sib).wait_recv()
                for j in range(4):
                    _remote(src, dst, send.at[4 * r + j], recv_sib.at[r], sib).wait_send()
                for j in far:
                    blk = _landed_block(piece, b[-1:], _peer(2 * j + 1)[1])
                    _remote(blk, blk, send_f.at[3 * r + j - 1], recv_f.at[3 * r + j - 1], sib).wait_recv()
                    _remote(blk, blk, send_f.at[3 * r + j - 1], recv_f.at[3 * r + j - 1], sib).wait_send()
        return emit

    sizes = []
    for members in groups:
        sizes += [4 * len(members), len(members), 3 * len(members)]
    sems, bufs, started = _copy_stage(name + "_start", list(srcs) + list(lands), [], sizes, start, after=after)
    srcs = bufs[:ns]
    state = dict(lands=bufs[ns:])

    def finish(g, after):
        if "passed" not in state:
            sizes_f = []
            for members in groups:
                sizes_f += [3 * len(members)] * 2
            state["passed"], state["lands"], _ = _copy_stage(name + "_pass", state["lands"],
                                                             [sems[3 * k + 2] for k in range(ng)], sizes_f, forward,
                                                             after=after)
            after = ()
        members = groups[g]
        sub = [(pieces[p][0], r, 0) + pieces[p][3:] for r, p in enumerate(members)]
        _, bufs, _ = _copy_stage("%s_last_%d" % (name, g), [srcs[pieces[p][1]] for p in members] + [state["lands"][g]],
                                 [sems[3 * g], sems[3 * g + 1], state["passed"][2 * g], state["passed"][2 * g + 1]],
                                 [], last(sub), after=after)
        return bufs[-1]

    return finish, started


_SMALL_NAMES = ("ln1_g", "ln1_b", "hg_lb_logits", "hg_norm_g", "sg_ln_g", "sg_ln_b", "sg_w_s", "sg_b_s",
                "ln2_g", "ln2_b", "mem_ln_g", "mem_ln_b", "ln3_g", "ln3_b", "ln4_g", "ln4_b")


_VEC_NAMES = ("ln1_g", "ln1_b", "ln2_g", "ln2_b", "mem_ln_g", "mem_ln_b", "ln3_g", "ln3_b", "ln4_g", "ln4_b")
_ROW_NAMES = ("hg_lb_logits", "hg_norm_g", "sg_ln_g", "sg_ln_b", "sg_b_s", "sg_w_s")
VEC_ROWS = 16


def _row_plan(shapes):
    plan, pos = {}, 0
    for k in _ROW_NAMES:
        shp = shapes[k]
        slabs, off = [], pos
        for idx in itertools.product(*[range(dim) for dim in shp[:-2]]):
            slabs.append((idx, off, shp[-2]))
            off += shp[-2]
        plan[k] = (pos, slabs)
        pos = -(-off // 8) * 8
    return plan, -(-pos // 16) * 16


def _pack_small_grads(gs, shapes, loss):
    d = gs[_VEC_NAMES[0]].size
    vec = jnp.concatenate([gs[k].reshape(1, -1) for k in _VEC_NAMES] + [jnp.tile(loss, (1, d // LANES))], axis=0)
    vec = jnp.pad(vec, ((0, VEC_ROWS - vec.shape[0]), (0, 0)))
    plan, total = _row_plan(shapes)
    parts, pos = [], 0
    for k in _ROW_NAMES:
        first, slabs = plan[k]
        rows = gs[k].reshape(-1, LANES)
        end = slabs[-1][1] + slabs[-1][2]
        nxt = -(-end // 8) * 8
        parts.append(jnp.pad(rows, ((0, nxt - first - rows.shape[0]), (0, 0))))
        pos = nxt
    parts.append(jnp.zeros((total - pos, LANES), F32))
    return vec, jnp.concatenate(parts, axis=0)


def _adam_small(land_vec, land_rows, w, m, v):
    names = _VEC_NAMES + _ROW_NAMES
    n = len(names)
    shapes = {k: w[k].shape for k in names}
    plan, _ = _row_plan(shapes)

    def body(*refs):
        lv_ref, lr_ref = refs[:2]
        w_refs, m_refs, v_refs = refs[2:2 + n], refs[2 + n:2 + 2 * n], refs[2 + 2 * n:2 + 3 * n]
        outs = refs[2 + 3 * n:2 + 7 * n]
        loss_ref = refs[2 + 7 * n]
        gv_s, gr_s = refs[3 + 7 * n:]
        gv_s[...] = _slot_sum(lv_ref)
        gr_s[...] = _slot_sum(lr_ref)
        loss_ref[...] = gv_s[len(_VEC_NAMES):len(_VEC_NAMES) + 1, :LANES]
        for p, k in enumerate(names):
            if k in _VEC_NAMES:
                row = _VEC_NAMES.index(k)
                slabs = [((), None, None)]
            else:
                slabs = plan[k][1]
            for idx, off, rows in slabs:
                g = gv_s[row:row + 1, :] if off is None else gr_s[off:off + rows, :]
                sel = idx + (slice(None), slice(None))
                delta, nm, nv = _adamw(w_refs[p][sel], g, m_refs[p][sel], v_refs[p][sel])
                for o, val in zip(range(4), (g, delta, nm, nv)):
                    outs[o * n + p][sel] = val

    flat = lambda tree: [tree[k] for k in names]
    shp = [jax.ShapeDtypeStruct(shapes[k], F32) for k in names]
    out = pl.pallas_call(
        body,
        out_shape=shp * 4 + [jax.ShapeDtypeStruct((1, LANES), F32)],
        scratch_shapes=[pltpu.VMEM(land_vec.shape[1:], F32), pltpu.VMEM(land_rows.shape[1:], F32)],
        name="adam_small",
    )(land_vec, land_rows, *flat(w), *flat(m), *flat(v))
    return [dict(zip(names, out[o * n:(o + 1) * n])) for o in range(4)], out[4 * n]


_COL_FFN = ("ffn1_w_gate", "ffn1_w_up", "ffn2_w_gate", "ffn2_w_up")
_ROW_FFN = ("ffn1_w_down", "ffn2_w_down")
_ROW_SQ = ("w_out", "xa_w_q", "xa_w_k", "xa_w_v", "xa_w_o")
_BIG_NAMES = ("ffn1_w_gate", "ffn1_w_up", "ffn1_w_down", "w_in", "w_out", "xa_w_q", "xa_w_k", "xa_w_v", "xa_w_o",
              "ffn2_w_gate", "ffn2_w_up", "ffn2_w_down")


def _ffn_split(fs):
    main = (fs // MXU_WIDTH_V7X) * MXU_WIDTH_V7X
    tail = fs - main
    tail_pad = -(-tail // LANES) * LANES
    assert main > 0 and tail > 0
    return main, tail, tail_pad


def _layout(name, shard_shape):
    r, c = shard_shape
    if name in _COL_FFN:
        main, tail, pad = _ffn_split(c)
        return (r, N_DEV * (main + pad)), [(1, 0, main, (r, main), (0, main)),
                                           (1, N_DEV * main, pad, (r, pad), (main, c))]
    if name in _ROW_FFN:
        main, tail, pad = _ffn_split(r)
        return (N_DEV * (main + pad), c), [(0, 0, main, (main, c), (0, main)),
                                           (0, N_DEV * main, pad, (pad, c), (main, r))]
    if name == "w_in":
        return (r, N_DEV * c), [(1, 0, c, (r, c), (0, c))]
    return (N_DEV * r, c), [(0, 0, r, (r, c), (0, r))]


def _shard_pieces(name, shard):
    out = []
    for axis, _, _, shape, (lo, hi) in _layout(name, shard.shape)[1]:
        part = shard[lo:hi, :] if axis == 0 else shard[:, lo:hi]
        pad = [(0, shape[0] - part.shape[0]), (0, shape[1] - part.shape[1])]
        out.append(jnp.pad(part, pad).astype(BF16))
    return out


def _gather_plan(names, shards):
    srcs, land_shapes, pieces, index = [], [], [], {}
    for li, name in enumerate(names):
        shape2d, parts = _layout(name, shards[name].shape)
        land_shapes.append(jax.ShapeDtypeStruct(shape2d, BF16))
        index[name] = []
        for (axis, base, stride, shape, _), src in zip(parts, _shard_pieces(name, shards[name])):
            index[name].append(len(pieces))
            pieces.append(("gather", len(srcs), li, axis, base, stride, shape))
            srcs.append(src)
    return srcs, land_shapes, pieces, index


def _scatter_plan(names, grads, shard_shapes):
    srcs, land_shapes, pieces, index = [], [], [], {}
    for si, name in enumerate(names):
        _, parts = _layout(name, shard_shapes[name])
        srcs.append(grads[name])
        index[name] = []
        for axis, base, stride, shape, _ in parts:
            index[name].append(len(land_shapes))
            pieces.append(("scatter", si, len(land_shapes), axis, base, stride, shape))
            land_shapes.append(jax.ShapeDtypeStruct((N_DEV,) + shape, grads[name].dtype))
    return srcs, land_shapes, pieces, index


def _small_views(small):
    row = lambda a: a.reshape(1, -1)
    ln = {k: row(small[k]) for k in ("ln1_g", "ln1_b", "ln2_g", "ln2_b", "ln3_g", "ln3_b", "ln4_g", "ln4_b",
                                      "mem_ln_g", "mem_ln_b", "hg_norm_g")}
    sg_w = small["sg_w_s"].reshape(SG_GROUPS, SG_CHUNK, SG_CHUNK)
    sg = dict(logits=jnp.swapaxes(small["hg_lb_logits"], 0, 1),
              g=small["sg_ln_g"].reshape(SG_GROUPS, 1, SG_DIM), b=small["sg_ln_b"].reshape(SG_GROUPS, 1, SG_DIM),
              w=sg_w, bs=small["sg_b_s"].reshape(SG_GROUPS, 1, SG_CHUNK))
    return ln, sg


def _forward(x, xb, mem, target, get_w, small, first_deps=()):
    ln, sg = _small_views(small)
    a1, b1, s1 = _ffn_up(xb, get_w("ffn1_w_gate", ()), get_w("ffn1_w_up", ()), "ffn1_up", deps=first_deps)
    h1b, xh1, rs1 = _mm_res_ln(s1, get_w("ffn1_w_down", (s1,)), x, ln["ln1_g"], ln["ln1_b"], 0.5, "ffn1_down_ln")
    proj = _mm_nn(h1b, get_w("w_in", (h1b,)), "mix_in")
    oraw, mix, states = _hgrn_fwd(proj, sg["logits"], ln["hg_norm_g"])
    mix = _sgu_fwd(proj, mix, sg["g"], sg["b"], sg["w"], sg["bs"])
    h2b, xh2, rs2 = _mm_res_ln(mix, get_w("w_out", (mix,)), (xh1, ln["ln1_g"], ln["ln1_b"]), ln["ln2_g"], ln["ln2_b"],
                               1.0, "mix_out_ln")
    mb, mxh, mrs, kb, vb = _mem_kv(mem, ln["mem_ln_g"], ln["mem_ln_b"], get_w("xa_w_k", (h2b,)), get_w("xa_w_v", (h2b,)))
    qb, att = _attn_fwd(h2b, get_w("xa_w_q", (mrs,)), kb, vb)
    h3b, xh3, rs3 = _mm_res_ln(att, get_w("xa_w_o", (att,)), (xh2, ln["ln2_g"], ln["ln2_b"]), ln["ln3_g"], ln["ln3_b"],
                               1.0, "attn_out_ln")
    a2, b2, s2 = _ffn_up(h3b, get_w("ffn2_w_gate", (h3b,)), get_w("ffn2_w_up", (h3b,)), "ffn2_up")
    loss, dy4, dy4b, dg4, db4 = _mm_res_ln(s2, get_w("ffn2_w_down", (s2,)), (xh3, ln["ln3_g"], ln["ln3_b"]),
                                           ln["ln4_g"], ln["ln4_b"], 0.5, "ffn2_down_ln_loss", target=target)
    return dict(xb=xb, a1=a1, b1=b1, s1=s1, h1b=h1b, xh1=xh1, rs1=rs1, proj=proj, oraw=oraw, mix=mix, states=states,
                h2b=h2b, xh2=xh2, rs2=rs2, mb=mb, mxh=mxh, mrs=mrs, kb=kb, vb=vb, qb=qb, att=att, h3b=h3b, xh3=xh3,
                rs3=rs3, a2=a2, b2=b2, s2=s2, loss=loss, dy4=dy4, dy4b=dy4b, dg4=dg4, db4=db4)


def _backward(sv, wt, small, send):
    ln, sg = _small_views(small)
    gs = {"ln4_g": sv["dg4"], "ln4_b": sv["db4"]}
    loss, dy4, dy4b = sv["loss"], sv["dy4"], sv["dy4b"]
    g_down2 = _mm_tn(sv["s2"], dy4b, "g_ffn2_down", scale=0.5)
    da2, db2, dy3, dy3b, gs["ln3_g"], gs["ln3_b"] = _ffn_bwd_fused(
        dy4, dy4b, wt["ffn2_w_down"], wt["ffn2_w_gate"], wt["ffn2_w_up"], sv["a2"], sv["b2"], 0.5,
        (sv["xh3"], sv["rs3"], ln["ln3_g"]), "ffn2_bwd")
    g_gate2 = _mm_tn(sv["h3b"], da2, "g_ffn2_gate")
    g_up2 = _mm_tn(sv["h3b"], db2, "g_ffn2_up")
    tok = send(("ffn2_w_down", "ffn2_w_gate", "ffn2_w_up"), (g_down2, g_gate2, g_up2))

    g_o = _mm_tn(sv["att"], dy3b, "g_xa_o", deps=(tok,))
    dqb, dk, dv = _attn_bwd(dy3b, wt["xa_w_o"], sv["qb"], sv["kb"], sv["vb"])
    g_q = _mm_tn(sv["h2b"], dqb, "g_xa_q")
    g_k, g_v, gs["mem_ln_g"], gs["mem_ln_b"] = _mem_bwd(dk, dv, sv["mb"], sv["mxh"], sv["mrs"], ln["mem_ln_g"],
                                                        wt["xa_w_k"], wt["xa_w_v"])
    tok = send(("xa_w_o", "xa_w_q", "xa_w_k", "xa_w_v"), (g_o, g_q, g_k, g_v))
    dy2, dy2b, gs["ln2_g"], gs["ln2_b"] = _dx_ln(dy3, [(dqb, wt["xa_w_q"])], (sv["xh2"], sv["rs2"], ln["ln2_g"]),
                                                 "attn_dx_ln", deps=(tok,))

    g_out = _mm_tn(sv["mix"], dy2b, "g_w_out")
    dmix = _mm_nt(dy2b, wt["w_out"], "mix_out_bwd")
    dq, dfz, div, dgg, dlg, dgn = _hgrn_bwd(sv["proj"], sv["oraw"], dmix, sv["states"], sg["logits"], ln["hg_norm_g"])
    du, dvv, gs["sg_ln_g"], gs["sg_ln_b"], gs["sg_w_s"], gs["sg_b_s"] = _sgu_bwd(
        sv["proj"], dmix, sg["g"], sg["b"], sg["w"], sg["bs"])
    gs["hg_lb_logits"] = jnp.swapaxes(dlg, 0, 1)
    gs["hg_norm_g"] = jnp.sum(dgn, axis=0)
    dproj = [dq, dfz, div, dgg, du, dvv]
    g_in = _mm_tn(sv["h1b"], dproj, "g_w_in")
    tok = send(("w_out", "w_in"), (g_out, g_in))
    dy1, dy1b, gs["ln1_g"], gs["ln1_b"] = _dx_ln(dy2, [(dproj, wt["w_in"])], (sv["xh1"], sv["rs1"], ln["ln1_g"]),
                                                 "mix_dx_ln", deps=(tok,))

    g_down1 = _mm_tn(sv["s1"], dy1b, "g_ffn1_down", scale=0.5)
    tok = send(("ffn1_w_down",), (g_down1,))
    da1, db1 = _ffn_bwd_act(dy1b, wt["ffn1_w_down"], sv["a1"], sv["b1"], 0.5, "ffn1_bwd_act", deps=(tok,))
    g_gate1 = _mm_tn(sv["xb"], da1, "g_ffn1_gate")
    tok = send(("ffn1_w_gate",), (g_gate1,))
    g_up1 = _mm_tn(sv["xb"], db1, "g_ffn1_up", deps=(tok,))
    tok = send(("ffn1_w_up",), (g_up1,))
    grad_x = _dx_ln(dy1, [(da1, wt["ffn1_w_gate"]), (db1, wt["ffn1_w_up"])], None, "ffn1_dx", deps=(tok,))
    return loss, grad_x, gs


_WEIGHT_NAMES = ("ffn1_w_gate", "ffn1_w_up", "ffn1_w_down", "ln1_g", "ln1_b", "w_in", "hg_lb_logits", "hg_norm_g",
                 "sg_ln_g", "sg_ln_b", "sg_w_s", "sg_b_s", "w_out", "ln2_g", "ln2_b", "mem_ln_g", "mem_ln_b",
                 "xa_w_q", "xa_w_k", "xa_w_v", "xa_w_o", "ln3_g", "ln3_b", "ffn2_w_gate", "ffn2_w_up", "ffn2_w_down",
                 "ln4_g", "ln4_b")
_FIRST = ("ffn1_w_gate", "ffn1_w_up")
_SECOND = ("ffn1_w_down", "w_in")
_THIRD = ("w_out", "xa_w_k", "xa_w_v", "xa_w_q", "xa_w_o", "ffn2_w_gate", "ffn2_w_up", "ffn2_w_down")


def kernel(x, mem, ffn1_w_gate, ffn1_w_up, ffn1_w_down, ln1_g, ln1_b, w_in, hg_lb_logits, hg_norm_g, sg_ln_g, sg_ln_b, sg_w_s, sg_b_s, w_out, ln2_g, ln2_b, mem_ln_g, mem_ln_b, xa_w_q, xa_w_k, xa_w_v, xa_w_o, ln3_g, ln3_b, ffn2_w_gate, ffn2_w_up, ffn2_w_down, ln4_g, ln4_b, loss_target, m_ffn1_w_gate, m_ffn1_w_up, m_ffn1_w_down, m_ln1_g, m_ln1_b, m_w_in, m_hg_lb_logits, m_hg_norm_g, m_sg_ln_g, m_sg_ln_b, m_sg_w_s, m_sg_b_s, m_w_out, m_ln2_g, m_ln2_b, m_mem_ln_g, m_mem_ln_b, m_xa_w_q, m_xa_w_k, m_xa_w_v, m_xa_w_o, m_ln3_g, m_ln3_b, m_ffn2_w_gate, m_ffn2_w_up, m_ffn2_w_down, m_ln4_g, m_ln4_b, v_ffn1_w_gate, v_ffn1_w_up, v_ffn1_w_down, v_ln1_g, v_ln1_b, v_w_in, v_hg_lb_logits, v_hg_norm_g, v_sg_ln_g, v_sg_ln_b, v_sg_w_s, v_sg_b_s, v_w_out, v_ln2_g, v_ln2_b, v_mem_ln_g, v_mem_ln_b, v_xa_w_q, v_xa_w_k, v_xa_w_v, v_xa_w_o, v_ln3_g, v_ln3_b, v_ffn2_w_gate, v_ffn2_w_up, v_ffn2_w_down, v_ln4_g, v_ln4_b):
    args = dict(locals())
    w = {k: args[k] for k in _WEIGHT_NAMES}
    m = {k: args["m_" + k] for k in _WEIGHT_NAMES}
    v = {k: args["v_" + k] for k in _WEIGHT_NAMES}
    shards = {k: w[k][0] for k in _BIG_NAMES}
    shard_shapes = {k: shards[k].shape for k in _BIG_NAMES}
    small = {k: (w[k][0] if k != "hg_lb_logits" else w[k]) for k in _SMALL_NAMES}

    srcs1, shapes1, pieces1, idx1 = _gather_plan(_FIRST, shards)
    lands1 = _place_own(srcs1, shapes1, pieces1, "gather_first_own")
    prepared = {}

    def prepare_rest(started):
        later = {k: shards[k] + started[0, 0] for k in _SECOND + _THIRD}
        placed = ()
        for key, names in (("second", _SECOND), ("third", _THIRD)):
            srcs, shapes, pieces, idx = _gather_plan(names, later)
            lands = _place_own(srcs, shapes, pieces, "gather_%s_own" % key)
            prepared[key] = (srcs, lands, pieces, idx)
            placed += tuple(lands)
        prepared["xb"] = _to_bf16(x[0], "x_bf16", deps=(started,))
        return placed + (prepared["xb"],)

    lands1, tok1 = _routed_gather(srcs1, lands1, pieces1, prepare_rest, "gather_first")
    srcs_p, lands_p, pieces_p, idx_p = prepared["second"]
    finish_second, tok_p = _paired_gather(srcs_p, lands_p, pieces_p, [list(idx_p[k]) for k in _SECOND], (tok1,),
                                          "gather_second")
    srcs2, lands2, pieces2, idx2 = prepared["third"]
    groups2 = [list(idx2[k]) for k in _THIRD]
    sems2, srcs2, lands2, tok2 = _comm_start(srcs2, lands2, pieces2, groups2, "gather_third_start", after=(tok_p,))
    wt = dict(zip(_FIRST, lands1))
    pending = {k: gi for gi, k in enumerate(_THIRD)}

    def get_w(name, after):
        if name in _SECOND and name not in wt:
            wt[name] = finish_second(_SECOND.index(name), after)
        if name in pending:
            gi = pending.pop(name)
            si = [pieces2[p][1] for p in groups2[gi]]
            sub = [(pieces2[p][0], row, 0) + pieces2[p][3:] for row, p in enumerate(groups2[gi])]
            wt[name] = _comm_wait([srcs2[s] for s in si], [lands2[gi]], sub, list(range(len(sub))), sems2[gi],
                                  after, "gather_wait_" + name)[0]
        return wt[name]

    sv = _forward(x[0], prepared["xb"], mem[0], loss_target[0], get_w, small, first_deps=(tok2,))

    sent = []

    def send(names, grads):
        srcs, shapes, pieces, idx = _scatter_plan(names, dict(zip(names, grads)), shard_shapes)
        lands = _place_own(srcs, shapes, pieces, "grads_own_%d" % len(sent))
        sems, srcs, lands, tok = _comm_start(srcs, lands, pieces, [list(range(len(pieces)))],
                                             "grads_start_%d" % len(sent))
        sent.append((names, srcs, lands, pieces, idx, sems[0]))
        return tok

    loss, grad_x, gs = _backward(sv, wt, small, send)

    ssrc = list(_pack_small_grads(gs, {k: w[k].shape for k in _SMALL_NAMES}, loss))
    sp = [("scatter", i, i, 0, 0, 0, a.shape) for i, a in enumerate(ssrc)]
    sshape = [jax.ShapeDtypeStruct((N_DEV,) + a.shape, F32) for a in ssrc]
    sl = _place_own(ssrc, sshape, sp, "small_own")
    ssem, ssrc, sl, _ = _comm_start(ssrc, sl, sp, [[0, 1]], "small_start")

    out_g, out_d, out_m, out_v = {}, {}, {}, {}
    after = (grad_x,)
    for n_sent, (names, srcs, lands, pieces, idx, sems) in enumerate(sent):
        lands = _comm_wait(srcs, lands, pieces, list(range(len(pieces))), sems, after, "grads_wait_%d" % n_sent)
        for k in names:
            axis = 1 if (k in _COL_FFN or k == "w_in") else 0
            if k in _COL_FFN:
                done = _adam_sharded([lands[i] for i in idx[k]], w[k][0].T, m[k][0].T, v[k][0].T, axis, "adam_" + k)
                res = [r.T for r in done]
            else:
                res = done = _adam_sharded([lands[i] for i in idx[k]], w[k][0], m[k][0], v[k][0], axis, "adam_" + k)
            out_g[k], out_d[k], out_m[k], out_v[k] = [r[None] for r in res]
        after = (done[3],)
    sl = _comm_wait(ssrc, sl, sp, [0, 1], ssem[0], after, "small_wait")
    small_out, loss_sum = _adam_small(sl[0], sl[1], w, m, v)
    for dst, res in zip((out_g, out_d, out_m, out_v), small_out):
        dst.update(res)
    loss_all = loss_sum[0, 0]
    return (loss_all, grad_x[None], *[out_g[k] for k in _WEIGHT_NAMES], *[out_d[k] for k in _WEIGHT_NAMES],
            *[out_m[k] for k in _WEIGHT_NAMES], *[out_v[k] for k in _WEIGHT_NAMES])
```

```python
import itertools

import jax
import jax.numpy as jnp
import numpy as np
from jax import lax
from jax.experimental import pallas as pl
from jax.experimental.pallas import tpu as pltpu

F32 = jnp.float32
BF16 = jnp.bfloat16

N_DEV = 8
ALPHA = 2.0 ** 0.25
LN_EPS = 1e-5
HG_HEADS = 4
HG_DIM = 128
SG_GROUPS = 4
SG_DIM = 128
SG_CHUNK = 128
X_HEADS = 4
HG_BLOCK = 16
HG_UNROLL = 16
ADAM_LR = 0.001
ADAM_B1 = 0.9
ADAM_B2 = 0.999
ADAM_EPS = 1e-08
ADAM_WD = 0.01
ADAM_STEP = 10
VMEM_LIMIT_V7X = 48 * 1024 * 1024
MXU_WIDTH_V7X = 256
LANES = 128
MESH_ID = pl.DeviceIdType.MESH
ANY = pl.BlockSpec(memory_space=pl.ANY)
HBM = pl.BlockSpec(memory_space=pltpu.HBM)
SEM = pl.BlockSpec(memory_space=pltpu.SEMAPHORE)
DATAFLOW = pltpu.SideEffectType.DATAFLOW_SIDE_EFFECTING


def _params(n_axes):
    return pltpu.CompilerParams(dimension_semantics=("arbitrary",) * n_axes, vmem_limit_bytes=VMEM_LIMIT_V7X)


def _dot(a, b):
    return jnp.dot(a, b, preferred_element_type=F32)


def _dot_nt(a, b):
    return lax.dot_general(a, b, (((1,), (1,)), ((), ())), preferred_element_type=F32)


def _dot_tn(a, b):
    return lax.dot_general(a, b, (((0,), (0,)), ((), ())), preferred_element_type=F32)


def _sigmoid(x):
    return 1.0 / (1.0 + jnp.exp(-x))


def _silu_and_grad(a):
    sig = _sigmoid(a)
    return a * sig, sig * (1.0 + a * (1.0 - sig))


_GELU_C = 0.7978845608028654


def _gelu_and_grad(x):
    inner = _GELU_C * (x + 0.044715 * x * x * x)
    t = jnp.tanh(inner)
    val = 0.5 * x * (1.0 + t)
    grad = 0.5 * (1.0 + t) + 0.5 * x * (1.0 - t * t) * _GELU_C * (1.0 + 3.0 * 0.044715 * x * x)
    return val, grad


def _ln_fwd(y, g, b):
    mu = jnp.mean(y, axis=-1, keepdims=True)
    yc = y - mu
    var = jnp.mean(yc * yc, axis=-1, keepdims=True)
    rstd = lax.rsqrt(var + LN_EPS)
    xhat = yc * rstd
    return xhat * g + b, xhat, rstd


def _ln_bwd(dh, xhat, rstd, g):
    dxh = dh * g
    m1 = jnp.mean(dxh, axis=-1, keepdims=True)
    m2 = jnp.mean(dxh * xhat, axis=-1, keepdims=True)
    dy = rstd * (dxh - m1 - xhat * m2)
    dg = jnp.sum(dh * xhat, axis=0, keepdims=True)
    db = jnp.sum(dh, axis=0, keepdims=True)
    return dy, dg, db


def _mask_dot(mask, x):
    hi = x.astype(BF16)
    lo = (x - hi.astype(F32)).astype(BF16)
    n = mask.shape[0]
    parts = [_dot(mask, hi[r:r + n, :]) + _dot(mask, lo[r:r + n, :]) for r in range(0, x.shape[0], n)]
    return parts[0] if len(parts) == 1 else jnp.concatenate(parts, axis=0)


def _block_masks(n):
    r = np.arange(n)[:, None]
    c = np.arange(n)[None, :]
    same = (r // HG_BLOCK) == (c // HG_BLOCK)
    return jnp.asarray(np.stack([same & (c <= r), same & (c >= r), same]), BF16)


def _row_tile(t):
    return min(t, 512)


def _token_tile(t):
    return min(t, 1024)


def _col_tile(n):
    for cand in (512, 256, 128):
        if n % cand == 0:
            return cand
    return n


def _resident(w):
    return pl.BlockSpec(w.shape, lambda *_: (0, 0), pipeline_mode=pl.Buffered(1))


def _drop_deps(body, n_in, n_deps):
    if n_deps == 0:
        return body
    return lambda *refs: body(*refs[:n_in], *refs[n_in + n_deps:])


def _to_bf16(x, name, deps=()):
    t, d = x.shape
    tm = _row_tile(t)

    def body(x_ref, o_ref):
        o_ref[...] = x_ref[...].astype(BF16)

    row = pl.BlockSpec((tm, d), lambda i: (i, 0))
    return pl.pallas_call(
        _drop_deps(body, 1, len(deps)),
        grid=(t // tm,),
        in_specs=[row] + [ANY] * len(deps),
        out_specs=row,
        out_shape=jax.ShapeDtypeStruct((t, d), BF16),
        compiler_params=_params(1),
        name=name,
    )(x, *deps)


def _ffn_up(hb, wg, wu, name, deps=()):
    t, d = hb.shape
    f = wg.shape[1]
    tm = _row_tile(t)
    tn = _col_tile(f)

    def body(h_ref, wg_ref, wu_ref, a_ref, b_ref, s_ref):
        h = h_ref[...]
        for c in range(f // tn):
            cols = slice(c * tn, (c + 1) * tn)
            a = _dot(h, wg_ref[:, cols])
            b = _dot(h, wu_ref[:, cols])
            a_ref[:, cols] = a.astype(BF16)
            b_ref[:, cols] = b.astype(BF16)
            s_ref[:, cols] = (a * _sigmoid(a) * b).astype(BF16)

    act = pl.BlockSpec((tm, f), lambda i: (i, 0))
    return pl.pallas_call(
        _drop_deps(body, 3, len(deps)),
        grid=(t // tm,),
        in_specs=[pl.BlockSpec((tm, d), lambda i: (i, 0)), _resident(wg), _resident(wu)] + [ANY] * len(deps),
        out_specs=[act, act, act],
        out_shape=[jax.ShapeDtypeStruct((t, f), BF16)] * 3,
        compiler_params=_params(1),
        name=name,
    )(hb, wg, wu, *deps)


def _mm_res_ln(lhs, w, res, g, b, coef, name, target=None):
    t, kd = lhs.shape
    d = w.shape[1]
    tm = _row_tile(t)
    nt = t // tm
    from_norm = isinstance(res, tuple)
    n_res = 3 if from_norm else 1

    def body(*refs):
        l_ref, w_ref = refs[:2]
        r_refs = refs[2:2 + n_res]
        g_ref, b_ref = refs[2 + n_res:4 + n_res]
        rest = refs[4 + n_res:]
        prev = r_refs[0][...] * r_refs[1][...] + r_refs[2][...] if from_norm else r_refs[0][...]
        y = ALPHA * prev + coef * _dot(l_ref[...], w_ref[...])
        h, xhat, rstd = _ln_fwd(y, g_ref[...], b_ref[...])
        if target is None:
            hb_ref, xh_ref, rs_ref = rest
            hb_ref[...] = h.astype(BF16)
            xh_ref[...] = xhat
            rs_ref[...] = rstd
            return
        t_ref, loss_ref, dy_ref, dyb_ref, dg_ref, db_ref, lacc = rest
        i = pl.program_id(0)

        @pl.when(i == 0)
        def _():
            lacc[...] = jnp.zeros_like(lacc)
            dg_ref[...] = jnp.zeros_like(dg_ref)
            db_ref[...] = jnp.zeros_like(db_ref)

        err = h - t_ref[...]
        lacc[...] += jnp.sum(err * err, axis=0, keepdims=True)
        dy, dg, db = _ln_bwd(err * (1.0 / d), xhat, rstd, g_ref[...])
        dy_ref[...] = dy
        dyb_ref[...] = dy.astype(BF16)
        dg_ref[...] += dg
        db_ref[...] += db

        @pl.when(i == nt - 1)
        def _():
            loss_ref[...] = jnp.zeros_like(loss_ref) + jnp.sum(lacc[...], axis=1, keepdims=True) * (0.5 / d)

    row = pl.BlockSpec((tm, d), lambda i: (i, 0))
    vec = pl.BlockSpec((1, d), lambda i: (0, 0))
    res_specs = [row, vec, vec] if from_norm else [row]
    res_args = list(res) if from_norm else [res]
    in_specs = [pl.BlockSpec((tm, kd), lambda i: (i, 0)), _resident(w)] + res_specs + [vec, vec]
    args = [lhs, w] + res_args + [g, b]
    if target is None:
        out_specs = [row, row, pl.BlockSpec((tm, 1), lambda i: (i, 0))]
        out_shape = [jax.ShapeDtypeStruct((t, d), BF16), jax.ShapeDtypeStruct((t, d), F32), pltpu.HBM((t, 1), F32)]
        scratch = []
    else:
        in_specs.append(row)
        args.append(target)
        out_specs = [pl.BlockSpec((1, LANES), lambda i: (0, 0)), row, row, vec, vec]
        out_shape = [jax.ShapeDtypeStruct((1, LANES), F32), jax.ShapeDtypeStruct((t, d), F32),
                     jax.ShapeDtypeStruct((t, d), BF16), jax.ShapeDtypeStruct((1, d), F32),
                     jax.ShapeDtypeStruct((1, d), F32)]
        scratch = [pltpu.VMEM((1, d), F32)]
    return pl.pallas_call(
        body,
        grid=(nt,),
        in_specs=in_specs,
        out_specs=out_specs,
        out_shape=out_shape,
        scratch_shapes=scratch,
        compiler_params=_params(1),
        name=name,
    )(*args)


def _store_slabs(o_ref, first, tile):
    for s in range(tile.shape[1] // LANES):
        o_ref[first + s] = tile[:, s * LANES:(s + 1) * LANES]


def _mm_nn(lhs, w, name):
    t, kd = lhs.shape
    n = w.shape[1]
    tm = _row_tile(t)
    tn = _col_tile(n)

    def body(l_ref, w_ref, o_ref):
        lhs_v = l_ref[...]
        for c in range(n // tn):
            _store_slabs(o_ref, c * (tn // LANES), _dot(lhs_v, w_ref[:, c * tn:(c + 1) * tn]))

    return pl.pallas_call(
        body,
        grid=(t // tm,),
        in_specs=[pl.BlockSpec((tm, kd), lambda i: (i, 0)), _resident(w)],
        out_specs=pl.BlockSpec((n // LANES, tm, LANES), lambda i: (0, i, 0)),
        out_shape=jax.ShapeDtypeStruct((n // LANES, t, LANES), F32),
        compiler_params=_params(1),
        name=name,
    )(lhs, w)


def _lower_bound(lg):
    m = jnp.max(lg, axis=0, keepdims=True)
    e = jnp.exp(lg - m)
    return e[0:1, :] / jnp.sum(e, axis=0, keepdims=True)


def _forget_terms(fz, lb):
    e = jnp.exp(-jnp.abs(fz))
    r = 1.0 / (1.0 + e)
    pos = fz >= 0.0
    sig = jnp.where(pos, r, e * r)
    nsig = jnp.where(pos, e * r, r)
    f = lb + (1.0 - lb) * sig
    k = (1.0 - lb) * nsig
    return sig, nsig, f, k


def _hg_tile(t):
    return min(t, 1024)


HG_HALF = HG_BLOCK // 2
NEG_BIG = -1e30


def _halves(a):
    return a[:HG_HALF, :], a[HG_HALF:, :]


def _causal_halves(s):
    return (0, 1) if s < HG_HALF else (1,)


def _decay_from(b_half, b_s, s, h, tidx):
    first = s - h * HG_HALF
    diff = b_half - b_s
    if first > 0:
        diff = jnp.where(tidx >= first, diff, NEG_BIG)
    return jnp.exp(diff)


def _hgrn_fwd(proj, logits, gn):
    t = proj.shape[1]
    ct = _hg_tile(t)
    nct = t // ct
    nblk = ct // HG_BLOCK
    nh = HG_HEADS
    mrows = min(ct, 256)

    def body(q_ref, fz_ref, iv_ref, gg_ref, lg_ref, gn_ref, mask_ref, oraw_ref, oa_ref, st_ref,
             state, qt_s, kt_s, k_s, b_s, dec_s):
        c = pl.program_id(1)

        @pl.when(c == 0)
        def _():
            state[...] = jnp.zeros_like(state)

        lb = _lower_bound(lg_ref[...])
        q = q_ref[...]
        _, _, f, k = _forget_terms(fz_ref[...], lb)
        logf = jnp.log(f)
        b = _mask_dot(mask_ref[0], logf)
        bend = _mask_dot(mask_ref[2], logf)
        qt_s[...] = (q * jnp.exp(b)).astype(BF16)
        kt_s[...] = (k * jnp.exp(bend - b)).astype(BF16)
        k_s[...] = k
        b_s[...] = b
        dec_s[...] = jnp.exp(bend)
        tidx = lax.broadcasted_iota(jnp.int32, (HG_HALF, HG_DIM), 0)

        def blk(i, carry):
            r0 = pl.multiple_of(i * HG_BLOCK, HG_BLOCK)
            rows = pl.ds(r0, HG_BLOCK)
            st = state[...]
            stb = st.astype(BF16)
            st_ref[i] = stb
            v = iv_ref[rows, :]
            qq = q_ref[rows, :]
            kk = k_s[rows, :]
            bb = b_s[rows, :]
            o = list(_halves(_dot_nt(qt_s[rows, :], stb)))
            qh, bh = _halves(qq), _halves(bb)
            for s in range(HG_BLOCK):
                ks, vs = kk[s:s + 1, :], v[s:s + 1, :]
                for h in _causal_halves(s):
                    e = _decay_from(bh[h], bb[s:s + 1, :], s, h, tidx)
                    acol = jnp.sum(qh[h] * (ks * e), axis=1, keepdims=True)
                    o[h] = o[h] + acol * vs
            oraw_ref[rows, :] = jnp.concatenate(o, axis=0)
            state[...] = st * dec_s[pl.ds(r0, 1), :] + _dot_tn(v.astype(BF16), kt_s[rows, :])
            return carry

        lax.fori_loop(0, nblk, blk, 0, unroll=HG_UNROLL)
        oraw = oraw_ref[...]
        r = lax.rsqrt(jnp.mean(oraw * oraw, axis=-1, keepdims=True) + LN_EPS)
        gg = gg_ref[...]
        oa_ref[...] = (oraw * r * gn_ref[...] * gg * _sigmoid(gg)).astype(BF16)

    def slab(off):
        return pl.BlockSpec((None, ct, HG_DIM), lambda h, c: (off + h, c, 0))

    return pl.pallas_call(
        body,
        grid=(nh, nct),
        in_specs=[slab(0), slab(nh), slab(2 * nh), slab(3 * nh),
                  pl.BlockSpec((None, 2, HG_DIM), lambda h, c: (h, 0, 0)),
                  pl.BlockSpec((1, HG_DIM), lambda h, c: (0, 0)),
                  pl.BlockSpec((3, mrows, mrows), lambda h, c: (0, 0, 0))],
        out_specs=[slab(0), pl.BlockSpec((ct, HG_DIM), lambda h, c: (c, h)),
                   pl.BlockSpec((None, nblk, HG_DIM, HG_DIM), lambda h, c: (h, c, 0, 0))],
        out_shape=[jax.ShapeDtypeStruct((nh, t, HG_DIM), F32),
                   jax.ShapeDtypeStruct((t, (nh + SG_GROUPS) * HG_DIM), BF16),
                   jax.ShapeDtypeStruct((nh, t // HG_BLOCK, HG_DIM, HG_DIM), BF16)],
        scratch_shapes=[pltpu.VMEM((HG_DIM, HG_DIM), F32), pltpu.VMEM((ct, HG_DIM), BF16),
                        pltpu.VMEM((ct, HG_DIM), BF16), pltpu.VMEM((ct, HG_DIM), F32),
                        pltpu.VMEM((ct, HG_DIM), F32), pltpu.VMEM((ct, HG_DIM), F32)],
        compiler_params=_params(2),
        name="hgrn_fwd",
    )(proj, proj, proj, proj, logits, gn, _block_masks(mrows))


def _sg_tile(t):
    return min(t, 512)


def _sgu_chunk_fwd(u, v, ln_g, ln_b, wm, bs):
    ua, dua = _gelu_and_grad(u)
    va, dva = _gelu_and_grad(v)
    vn, xhat, rstd = _ln_fwd(va, ln_g, ln_b)
    s = _dot(wm, vn.astype(BF16)) + bs
    return ua, dua, dva, vn, xhat, rstd, s


def _tril_weight(w):
    n = SG_CHUNK
    r = lax.broadcasted_iota(jnp.int32, (n, n), 0)
    c = lax.broadcasted_iota(jnp.int32, (n, n), 1)
    return jnp.where(c <= r, w, 0.0)


def _bias_by_position(b_row):
    return jnp.broadcast_to(b_row, (SG_CHUNK, SG_CHUNK)).T


def _sgu_fwd(proj, mix, ln_g, ln_b, w_s, b_row):
    t = proj.shape[1]
    ct = _sg_tile(t)
    ng = SG_GROUPS
    wide = ng * SG_DIM
    blk_u = 4 * HG_HEADS // ng

    def body(u_ref, v_ref, g_ref, b_ref, w_ref, bs_ref, mix_ref, o_ref):
        del mix_ref
        for g in range(ng):
            lanes = slice(g * SG_DIM, (g + 1) * SG_DIM)
            wm = _tril_weight(w_ref[g]).astype(BF16)
            bs = _bias_by_position(bs_ref[g])
            for n in range(ct // SG_CHUNK):
                rows = slice(n * SG_CHUNK, (n + 1) * SG_CHUNK)
                ua, _, _, _, _, _, s = _sgu_chunk_fwd(u_ref[g, rows, :], v_ref[g, rows, :], g_ref[g], b_ref[g], wm, bs)
                o_ref[rows, lanes] = (ua * s).astype(BF16)

    full = lambda a: pl.BlockSpec(a.shape, lambda c: (0,) * a.ndim)
    return pl.pallas_call(
        body,
        grid=(t // ct,),
        in_specs=[pl.BlockSpec((ng, ct, SG_DIM), lambda c: (blk_u, c, 0)),
                  pl.BlockSpec((ng, ct, SG_DIM), lambda c: (blk_u + 1, c, 0)),
                  full(ln_g), full(ln_b), full(w_s), full(b_row), ANY],
        out_specs=pl.BlockSpec((ct, wide), lambda c: (c, 1)),
        out_shape=jax.ShapeDtypeStruct(mix.shape, mix.dtype),
        input_output_aliases={6: 0},
        compiler_params=_params(1),
        name="sgu_fwd",
    )(proj, proj, ln_g, ln_b, w_s, b_row, mix)


def _mem_kv(mem, g, b, wk, wv):
    m_len, d = mem.shape

    def body(m_ref, g_ref, b_ref, wk_ref, wv_ref, mb_ref, xh_ref, rs_ref, k_ref, v_ref):
        m, xhat, rstd = _ln_fwd(m_ref[...], g_ref[...], b_ref[...])
        mb = m.astype(BF16)
        mb_ref[...] = mb
        xh_ref[...] = xhat
        rs_ref[...] = rstd
        k_ref[...] = _dot(mb, wk_ref[...]).astype(BF16)
        v_ref[...] = _dot(mb, wv_ref[...]).astype(BF16)

    return pl.pallas_call(
        body,
        out_shape=[jax.ShapeDtypeStruct((m_len, d), BF16), jax.ShapeDtypeStruct((m_len, d), F32),
                   jax.ShapeDtypeStruct((m_len, 1), F32), jax.ShapeDtypeStruct((m_len, d), BF16),
                   jax.ShapeDtypeStruct((m_len, d), BF16)],
        compiler_params=pltpu.CompilerParams(vmem_limit_bytes=VMEM_LIMIT_V7X),
        name="mem_kv",
    )(mem, g, b, wk, wv)


def _softmax_rows(s):
    m = jnp.max(s, axis=-1, keepdims=True)
    p = jnp.exp(s - m)
    return p / jnp.sum(p, axis=-1, keepdims=True)


def _attn_fwd(hb, wq, kb, vb):
    t, d = hb.shape
    tm = _row_tile(t)
    dh = d // X_HEADS
    scale = dh ** -0.5

    def body(h_ref, wq_ref, k_ref, v_ref, q_ref, o_ref):
        q = _dot(h_ref[...], wq_ref[...]).astype(BF16)
        q_ref[...] = q
        for hd in range(X_HEADS):
            sl = slice(hd * dh, (hd + 1) * dh)
            p = _softmax_rows(_dot_nt(q[:, sl], k_ref[:, sl]) * scale)
            o_ref[:, sl] = _dot(p.astype(BF16), v_ref[:, sl]).astype(BF16)

    row = pl.BlockSpec((tm, d), lambda i: (i, 0))
    full = lambda a: pl.BlockSpec(a.shape, lambda i: (0, 0))
    return pl.pallas_call(
        body,
        grid=(t // tm,),
        in_specs=[row, full(wq), full(kb), full(vb)],
        out_specs=[row, row],
        out_shape=[jax.ShapeDtypeStruct((t, d), BF16), jax.ShapeDtypeStruct((t, d), BF16)],
        compiler_params=_params(1),
        name="attn_fwd",
    )(hb, wq, kb, vb)


def _ffn_bwd_act(dyb, wd, a, b, coef, name, deps=()):
    t, d = dyb.shape
    f = wd.shape[0]
    tm = _row_tile(t)
    tn = _col_tile(f)
    nt = t // tm
    ring = min(3, nt)

    def body(dy_ref, wd_ref, a_hbm, b_hbm, da_ref, db_ref, a_buf, b_buf, sems):
        i = pl.program_id(0)

        def loads(step, slot):
            rows = pl.ds(pl.multiple_of(step * tm, tm), tm)
            return (pltpu.make_async_copy(a_hbm.at[rows, :], a_buf.at[slot], sems.at[0, slot]),
                    pltpu.make_async_copy(b_hbm.at[rows, :], b_buf.at[slot], sems.at[1, slot]))

        @pl.when(i == 0)
        def _():
            for s in range(ring):
                for load in loads(s, s):
                    load.start()

        slot = i % ring
        for load in loads(i, slot):
            load.wait()
        dy = dy_ref[...]
        for c in range(f // tn):
            cols = slice(c * tn, (c + 1) * tn)
            ds = _dot_nt(dy, wd_ref[cols, :]) * coef
            silu, dsilu = _silu_and_grad(a_buf[slot, :, cols].astype(F32))
            da_ref[:, cols] = (ds * b_buf[slot, :, cols].astype(F32) * dsilu).astype(BF16)
            db_ref[:, cols] = (ds * silu).astype(BF16)

        @pl.when(i + ring < nt)
        def _():
            for load in loads(i + ring, slot):
                load.start()

    act = pl.BlockSpec((tm, f), lambda i: (i, 0))
    return pl.pallas_call(
        _drop_deps(body, 4, len(deps)),
        grid=(nt,),
        in_specs=[pl.BlockSpec((tm, d), lambda i: (i, 0)), _resident(wd), ANY, ANY] + [ANY] * len(deps),
        out_specs=[act, act],
        out_shape=[jax.ShapeDtypeStruct((t, f), BF16), jax.ShapeDtypeStruct((t, f), BF16)],
        scratch_shapes=[pltpu.VMEM((ring, tm, f), BF16), pltpu.VMEM((ring, tm, f), BF16),
                        pltpu.SemaphoreType.DMA((2, ring))],
        compiler_params=_params(1),
        name=name,
    )(dyb, wd, a, b, *deps)


def _ffn_bwd_fused(dy, dyb, wd, wg, wu, a, b, coef, ln, name):
    t, d = dy.shape
    f = wd.shape[0]
    tm = min(t, 256)
    tn = _col_tile(f)

    def body(dy_ref, dyb_ref, wd_ref, wg_ref, wu_ref, a_ref, b_ref, xh_ref, rs_ref, g_ref,
             da_ref, db_ref, dyo_ref, dyob_ref, dg_ref, dbl_ref):
        dyb_v = dyb_ref[...]
        dh = ALPHA * dy_ref[...]
        for c in range(f // tn):
            cols = slice(c * tn, (c + 1) * tn)
            ds = _dot_nt(dyb_v, wd_ref[cols, :]) * coef
            silu, dsilu = _silu_and_grad(a_ref[:, cols].astype(F32))
            da = (ds * b_ref[:, cols].astype(F32) * dsilu).astype(BF16)
            db = (ds * silu).astype(BF16)
            da_ref[:, cols] = da
            db_ref[:, cols] = db
            dh = dh + _dot_nt(da, wg_ref[:, cols]) + _dot_nt(db, wu_ref[:, cols])

        @pl.when(pl.program_id(0) == 0)
        def _():
            dg_ref[...] = jnp.zeros_like(dg_ref)
            dbl_ref[...] = jnp.zeros_like(dbl_ref)

        dyp, dg, dbl = _ln_bwd(dh, xh_ref[...], rs_ref[...], g_ref[...])
        dyo_ref[...] = dyp
        dyob_ref[...] = dyp.astype(BF16)
        dg_ref[...] += dg
        dbl_ref[...] += dbl

    row = pl.BlockSpec((tm, d), lambda i: (i, 0))
    act = pl.BlockSpec((tm, f), lambda i: (i, 0))
    vec = pl.BlockSpec((1, d), lambda i: (0, 0))
    return pl.pallas_call(
        body,
        grid=(t // tm,),
        in_specs=[row, row, _resident(wd), _resident(wg), _resident(wu), act, act, row,
                  pl.BlockSpec((tm, 1), lambda i: (i, 0)), vec],
        out_specs=[act, act, row, row, vec, vec],
        out_shape=[jax.ShapeDtypeStruct((t, f), BF16), jax.ShapeDtypeStruct((t, f), BF16),
                   jax.ShapeDtypeStruct((t, d), F32), jax.ShapeDtypeStruct((t, d), BF16),
                   jax.ShapeDtypeStruct((1, d), F32), jax.ShapeDtypeStruct((1, d), F32)],
        compiler_params=_params(1),
        name=name,
    )(dy, dyb, wd, wg, wu, a, b, *ln)


def _mm_tn(a, b, name, scale=1.0, deps=()):
    t, m = a.shape
    bs = list(b) if isinstance(b, (list, tuple)) else [b]
    n = sum(piece.shape[1] for piece in bs)
    tt = _token_tile(t)
    nt = t // tt
    tm_o, tn_o = m, n

    def body(a_ref, *refs):
        b_refs, (o_ref, acc) = refs[:len(bs)], refs[len(bs):]
        k = pl.program_id(2)

        @pl.when(k == 0)
        def _():
            acc[...] = jnp.zeros_like(acc)

        first = 0
        for b_ref in b_refs:
            cols = slice(first, first + b_ref.shape[1])
            acc[:, cols] += _dot_tn(a_ref[...], b_ref[...])
            first = cols.stop

        @pl.when(k == nt - 1)
        def _():
            o_ref[...] = (acc[...] * scale).astype(BF16)

    return pl.pallas_call(
        _drop_deps(body, 1 + len(bs), len(deps)),
        grid=(m // tm_o, n // tn_o, nt),
        in_specs=[pl.BlockSpec((tt, tm_o), lambda i, j, k: (k, i))]
        + [pl.BlockSpec((tt, piece.shape[1]), lambda i, j, k: (k, j)) for piece in bs] + [ANY] * len(deps),
        out_specs=pl.BlockSpec((tm_o, tn_o), lambda i, j, k: (i, j)),
        out_shape=pltpu.HBM((m, n), BF16),
        scratch_shapes=[pltpu.VMEM((tm_o, tn_o), F32)],
        compiler_params=_params(3),
        name=name,
    )(a, *bs, *deps)


def _mm_nt(lhs, w, name):
    t, d = lhs.shape
    kd = w.shape[0]
    tm = _row_tile(t)

    def body(l_ref, w_ref, o_ref):
        _store_slabs(o_ref, 0, _dot_nt(l_ref[...], w_ref[...]))

    return pl.pallas_call(
        body,
        grid=(t // tm,),
        in_specs=[pl.BlockSpec((tm, d), lambda i: (i, 0)), _resident(w)],
        out_specs=pl.BlockSpec((kd // LANES, tm, LANES), lambda i: (0, i, 0)),
        out_shape=jax.ShapeDtypeStruct((kd // LANES, t, LANES), F32),
        compiler_params=_params(1),
        name=name,
    )(lhs, w)


def _dx_ln(dy, pairs, ln, name, deps=()):
    t, d = dy.shape
    npair = len(pairs)
    pairs = [(list(lhs) if isinstance(lhs, (list, tuple)) else [lhs], w) for lhs, w in pairs]
    tm = min(t, 512 // npair)
    nt = t // tm
    n_in = 1 + sum(len(pieces) + 1 for pieces, _ in pairs) + (3 if ln is not None else 0)

    def body(*refs):
        dy_ref = refs[0]
        pos = 1
        dh = ALPHA * dy_ref[...]
        for pieces, _ in pairs:
            w_ref = refs[pos + len(pieces)]
            first = 0
            for l_ref in refs[pos:pos + len(pieces)]:
                cols = slice(first, first + l_ref.shape[1])
                dh = dh + _dot_nt(l_ref[...], w_ref[:, cols])
                first = cols.stop
            pos += len(pieces) + 1
        if ln is not None:
            xh_ref, rs_ref, g_ref = refs[pos:pos + 3]
            dyo_ref, dyb_ref, dg_ref, db_ref = refs[pos + 3:pos + 7]

            @pl.when(pl.program_id(0) == 0)
            def _():
                dg_ref[...] = jnp.zeros_like(dg_ref)
                db_ref[...] = jnp.zeros_like(db_ref)

            dyp, dg, db = _ln_bwd(dh, xh_ref[...], rs_ref[...], g_ref[...])
            dyo_ref[...] = dyp
            dyb_ref[...] = dyp.astype(BF16)
            dg_ref[...] += dg
            db_ref[...] += db
        else:
            refs[pos][...] = dh

    row = pl.BlockSpec((tm, d), lambda i: (i, 0))
    vec = pl.BlockSpec((1, d), lambda i: (0, 0))
    in_specs = [row]
    args = [dy]
    for pieces, w in pairs:
        in_specs += [pl.BlockSpec((tm, piece.shape[1]), lambda i: (i, 0)) for piece in pieces] + [_resident(w)]
        args += pieces + [w]
    if ln is not None:
        in_specs += [row, pl.BlockSpec((tm, 1), lambda i: (i, 0)), vec]
        args += list(ln)
        out_specs = [row, row, vec, vec]
        out_shape = [jax.ShapeDtypeStruct((t, d), F32), jax.ShapeDtypeStruct((t, d), BF16),
                     jax.ShapeDtypeStruct((1, d), F32), jax.ShapeDtypeStruct((1, d), F32)]
    else:
        out_specs = row
        out_shape = jax.ShapeDtypeStruct((t, d), F32)
    return pl.pallas_call(
        _drop_deps(body, n_in, len(deps)),
        grid=(nt,),
        in_specs=in_specs + [ANY] * len(deps),
        out_specs=out_specs,
        out_shape=out_shape,
        compiler_params=_params(1),
        name=name,
    )(*args, *deps)


def _hgrn_bwd(proj, oraw, dmix, states, logits, gn):
    t = proj.shape[1]
    ct = _hg_tile(t)
    nct = t // ct
    nblk = ct // HG_BLOCK
    nh = HG_HEADS
    mrows = min(ct, 256)

    def body(q_ref, fz_ref, iv_ref, gg_ref, or_ref, do_ref, st_ref, lg_ref, gn_ref, mask_ref,
             dq_ref, dfz_ref, div_ref, dgg_ref, dlg_ref, dgn_ref,
             dstate, qt_s, kt_s, k_s, b_s, eb_s, ekb_s, dec_s, dor_s, dbl_s, gr_s, dk_s, dlb_acc):
        c = pl.program_id(1)

        @pl.when(c == 0)
        def _():
            dstate[...] = jnp.zeros_like(dstate)
            dlb_acc[...] = jnp.zeros_like(dlb_acc)
            dgn_ref[...] = jnp.zeros_like(dgn_ref)

        lb = _lower_bound(lg_ref[...])
        q = q_ref[...]
        sig, nsig, f, k = _forget_terms(fz_ref[...], lb)
        logf = jnp.log(f)
        b = _mask_dot(mask_ref[0], logf)
        bend = _mask_dot(mask_ref[2], logf)
        eb = jnp.exp(b)
        ekb = jnp.exp(bend - b)
        qt_s[...] = (q * eb).astype(BF16)
        kt_s[...] = (k * ekb).astype(BF16)
        k_s[...] = k
        b_s[...] = b
        eb_s[...] = eb
        ekb_s[...] = ekb
        dec_s[...] = jnp.exp(bend)
        oraw = or_ref[...]
        r = lax.rsqrt(jnp.mean(oraw * oraw, axis=-1, keepdims=True) + LN_EPS)
        on = oraw * r
        gg = gg_ref[...]
        silu, dsilu = _silu_and_grad(gg)
        doa = do_ref[...]
        gnv = gn_ref[...]
        dgg_ref[...] = (doa * on * gnv * dsilu).astype(BF16)
        dyn = doa * silu
        dgn_ref[...] += jnp.sum(dyn * on, axis=0, keepdims=True)
        don = dyn * gnv
        dor_s[...] = r * (don - on * jnp.mean(don * on, axis=-1, keepdims=True))
        tidx = lax.broadcasted_iota(jnp.int32, (HG_HALF, HG_DIM), 0)

        def blk(ii, carry):
            i = nblk - 1 - ii
            r0 = pl.multiple_of(i * HG_BLOCK, HG_BLOCK)
            rows = pl.ds(r0, HG_BLOCK)
            st = st_ref[i]
            dst = dstate[...]
            dstb = dst.astype(BF16)
            do = dor_s[rows, :]
            dob = do.astype(BF16)
            v = iv_ref[rows, :]
            vb = v.astype(BF16)
            qq = q_ref[rows, :]
            kk = k_s[rows, :]
            bb = b_s[rows, :]
            qt = qt_s[rows, :]
            kt = kt_s[rows, :]
            dec = dec_s[pl.ds(r0, 1), :]
            dkt = _dot(vb, dstb)
            dq = _dot(dob, st) * eb_s[rows, :]
            dk = dkt * ekb_s[rows, :]
            dv = _dot_nt(kt, dstb)
            gend = (jnp.sum(kk * dk, axis=0, keepdims=True)
                    + dec * jnp.sum(dst * st.astype(F32), axis=0, keepdims=True))
            qh, bh, doh = _halves(qq), _halves(bb), _halves(do)
            dqh, dkh, dvh = list(_halves(dq)), list(_halves(dk)), list(_halves(dv))
            for s in range(HG_BLOCK):
                ks, vs = kk[s:s + 1, :], v[s:s + 1, :]
                dk_part = dv_part = None
                for h in _causal_halves(s):
                    e = _decay_from(bh[h], bb[s:s + 1, :], s, h, tidx)
                    ke = ks * e
                    acol = jnp.sum(qh[h] * ke, axis=1, keepdims=True)
                    dacol = jnp.sum(doh[h] * vs, axis=1, keepdims=True)
                    dqh[h] = dqh[h] + dacol * ke
                    pk = dacol * (qh[h] * e)
                    pv = acol * doh[h]
                    dk_part = pk if dk_part is None else dk_part + pk
                    dv_part = pv if dv_part is None else dv_part + pv
                hs, row = divmod(s, HG_HALF)
                dkh[hs] = dkh[hs] + jnp.where(tidx == row, jnp.sum(dk_part, axis=0, keepdims=True), 0.0)
                dvh[hs] = dvh[hs] + jnp.where(tidx == row, jnp.sum(dv_part, axis=0, keepdims=True), 0.0)
            dq = jnp.concatenate(dqh, axis=0)
            dk = jnp.concatenate(dkh, axis=0)
            dv = jnp.concatenate(dvh, axis=0)
            dq_ref[rows, :] = dq.astype(BF16)
            div_ref[rows, :] = dv.astype(BF16)
            dk_s[rows, :] = dk
            dbl_s[rows, :] = qq * dq - kk * dk
            gr_s[rows, :] = jnp.zeros((HG_BLOCK, HG_DIM), F32) + gend
            dstate[...] = dst * dec + _dot_tn(dob, qt)
            return carry

        lax.fori_loop(0, nblk, blk, 0, unroll=HG_UNROLL)
        dlogf = _mask_dot(mask_ref[1], dbl_s[...]) + gr_s[...]
        dk = dk_s[...]
        dfz_ref[...] = ((dlogf / f - dk) * ((1.0 - lb) * sig * nsig)).astype(BF16)
        dlb_acc[...] += jnp.sum((dlogf / f - dk) * nsig, axis=0, keepdims=True)

        @pl.when(c == nct - 1)
        def _():
            dl0 = dlb_acc[...] * lb * (1.0 - lb)
            layer = lax.broadcasted_iota(jnp.int32, (2, HG_DIM), 0)
            dlg_ref[...] = jnp.where(layer == 0, dl0, -dl0)

    def slab(off):
        return pl.BlockSpec((None, ct, HG_DIM), lambda h, c: (off + h, nct - 1 - c, 0))

    out_slab = pl.BlockSpec((ct, HG_DIM), lambda h, c: (nct - 1 - c, h))
    tile_f32 = pltpu.VMEM((ct, HG_DIM), F32)
    tile_b16 = pltpu.VMEM((ct, HG_DIM), BF16)
    slab_shape = pltpu.HBM((t, nh * HG_DIM), BF16)
    return pl.pallas_call(
        body,
        grid=(nh, nct),
        in_specs=[slab(0), slab(nh), slab(2 * nh), slab(3 * nh), slab(0), slab(0),
                  pl.BlockSpec((None, nblk, HG_DIM, HG_DIM), lambda h, c: (h, nct - 1 - c, 0, 0)),
                  pl.BlockSpec((None, 2, HG_DIM), lambda h, c: (h, 0, 0)),
                  pl.BlockSpec((1, HG_DIM), lambda h, c: (0, 0)),
                  pl.BlockSpec((3, mrows, mrows), lambda h, c: (0, 0, 0))],
        out_specs=[out_slab, out_slab, out_slab, out_slab,
                   pl.BlockSpec((None, 2, HG_DIM), lambda h, c: (h, 0, 0)),
                   pl.BlockSpec((None, 1, HG_DIM), lambda h, c: (h, 0, 0))],
        out_shape=[slab_shape, slab_shape, slab_shape, slab_shape,
                   jax.ShapeDtypeStruct((nh, 2, HG_DIM), F32), jax.ShapeDtypeStruct((nh, 1, HG_DIM), F32)],
        scratch_shapes=[pltpu.VMEM((HG_DIM, HG_DIM), F32), tile_b16, tile_b16, tile_f32, tile_f32, tile_f32, tile_f32,
                        tile_f32, tile_f32, tile_f32, tile_f32, tile_f32, pltpu.VMEM((1, HG_DIM), F32)],
        compiler_params=_params(2),
        name="hgrn_bwd",
    )(proj, proj, proj, proj, oraw, dmix, states, logits, gn, _block_masks(mrows))


def _sgu_bwd(proj, dmix, ln_g, ln_b, w_s, b_row):
    t = proj.shape[1]
    ct = _sg_tile(t)
    nct = t // ct
    ng = SG_GROUPS
    off_u = 4 * HG_HEADS
    off_v = off_u + ng
    n = SG_CHUNK

    def body(u_ref, v_ref, do_ref, g_ref, b_ref, w_ref, bs_ref, du_ref, dv_ref, dg_ref, db_ref, dw_ref, dbs_ref):
        c = pl.program_id(1)

        @pl.when(c == 0)
        def _():
            dg_ref[...] = jnp.zeros_like(dg_ref)
            db_ref[...] = jnp.zeros_like(db_ref)
            dw_ref[...] = jnp.zeros_like(dw_ref)
            dbs_ref[...] = jnp.zeros_like(dbs_ref)

        r = lax.broadcasted_iota(jnp.int32, (n, n), 0)
        cc = lax.broadcasted_iota(jnp.int32, (n, n), 1)
        wm = jnp.where(cc <= r, w_ref[...], 0.0).astype(BF16)
        wmt = jnp.where(r <= cc, w_ref[...].T, 0.0).astype(BF16)
        bs = _bias_by_position(bs_ref[...])
        for ci in range(ct // n):
            rows = slice(ci * n, (ci + 1) * n)
            ua, dua, dva, vn, xhat, rstd, s = _sgu_chunk_fwd(u_ref[rows, :], v_ref[rows, :], g_ref[...], b_ref[...],
                                                             wm, bs)
            do = do_ref[rows, :]
            du_ref[rows, :] = (do * s * dua).astype(BF16)
            ds = do * ua
            dsb = ds.astype(BF16)
            dbs_ref[...] += jnp.sum(ds, axis=1, keepdims=True)
            dw_ref[...] += _dot_nt(dsb, vn.astype(BF16))
            dvn = _dot(wmt, dsb)
            dva_in, dg, db = _ln_bwd(dvn, xhat, rstd, g_ref[...])
            dg_ref[...] += dg
            db_ref[...] += db
            dv_ref[rows, :] = (dva_in * dva).astype(BF16)

        @pl.when(c == nct - 1)
        def _():
            dw_ref[...] = jnp.where(cc <= r, dw_ref[...], 0.0)

    vec = pl.BlockSpec((None, 1, SG_DIM), lambda g, c: (g, 0, 0))
    mat = pl.BlockSpec((None, n, n), lambda g, c: (g, 0, 0))
    col = pl.BlockSpec((None, n, 1), lambda g, c: (g, 0, 0))
    out_slab = pl.BlockSpec((ct, SG_DIM), lambda g, c: (c, g))
    return pl.pallas_call(
        body,
        grid=(ng, nct),
        in_specs=[pl.BlockSpec((None, ct, SG_DIM), lambda g, c: (off_u + g, c, 0)),
                  pl.BlockSpec((None, ct, SG_DIM), lambda g, c: (off_v + g, c, 0)),
                  pl.BlockSpec((None, ct, SG_DIM), lambda g, c: (ng + g, c, 0)), vec, vec, mat, vec],
        out_specs=[out_slab, out_slab, vec, vec, mat, col],
        out_shape=[pltpu.HBM((t, ng * SG_DIM), BF16), pltpu.HBM((t, ng * SG_DIM), BF16),
                   jax.ShapeDtypeStruct((ng, 1, SG_DIM), F32), jax.ShapeDtypeStruct((ng, 1, SG_DIM), F32),
                   jax.ShapeDtypeStruct((ng, n, n), F32), jax.ShapeDtypeStruct((ng, n, 1), F32)],
        compiler_params=_params(2),
        name="sgu_bwd",
    )(proj, proj, dmix, ln_g, ln_b, w_s, b_row)


def _attn_bwd(dyb, wo, qb, kb, vb):
    t, d = dyb.shape
    m_len = kb.shape[0]
    tm = _row_tile(t)
    dh = d // X_HEADS
    scale = dh ** -0.5

    def body(dy_ref, wo_ref, q_ref, k_ref, v_ref, dq_ref, dk_ref, dv_ref):
        i = pl.program_id(0)

        @pl.when(i == 0)
        def _():
            dk_ref[...] = jnp.zeros_like(dk_ref)
            dv_ref[...] = jnp.zeros_like(dv_ref)

        do = _dot_nt(dy_ref[...], wo_ref[...]).astype(BF16)
        for hd in range(X_HEADS):
            sl = slice(hd * dh, (hd + 1) * dh)
            qh = q_ref[:, sl]
            p = _softmax_rows(_dot_nt(qh, k_ref[:, sl]) * scale)
            doh = do[:, sl]
            dp = _dot_nt(doh, v_ref[:, sl])
            ds = (p * (dp - jnp.sum(dp * p, axis=-1, keepdims=True)) * scale).astype(BF16)
            dq_ref[:, sl] = _dot(ds, k_ref[:, sl]).astype(BF16)
            dk_ref[:, sl] += _dot_tn(ds, qh)
            dv_ref[:, sl] += _dot_tn(p.astype(BF16), doh)

    row = pl.BlockSpec((tm, d), lambda i: (i, 0))
    full = lambda a: pl.BlockSpec(a.shape, lambda i: (0, 0))
    kv = pl.BlockSpec((m_len, d), lambda i: (0, 0))
    return pl.pallas_call(
        body,
        grid=(t // tm,),
        in_specs=[row, full(wo), row, full(kb), full(vb)],
        out_specs=[row, kv, kv],
        out_shape=[jax.ShapeDtypeStruct((t, d), BF16), jax.ShapeDtypeStruct((m_len, d), F32),
                   jax.ShapeDtypeStruct((m_len, d), F32)],
        compiler_params=_params(1),
        name="attn_bwd",
    )(dyb, wo, qb, kb, vb)


def _mem_bwd(dk, dv, mb, xhat, rstd, g, wk, wv):
    m_len, d = dk.shape

    def body(dk_ref, dv_ref, mb_ref, xh_ref, rs_ref, g_ref, wk_ref, wv_ref, gwk_ref, gwv_ref, dg_ref, db_ref):
        dkb = dk_ref[...].astype(BF16)
        dvb = dv_ref[...].astype(BF16)
        mb_v = mb_ref[...]
        gwk_ref[...] = _dot_tn(mb_v, dkb).astype(BF16)
        gwv_ref[...] = _dot_tn(mb_v, dvb).astype(BF16)
        dm = _dot_nt(dkb, wk_ref[...]) + _dot_nt(dvb, wv_ref[...])
        _, dg, db = _ln_bwd(dm, xh_ref[...], rs_ref[...], g_ref[...])
        dg_ref[...] = dg
        db_ref[...] = db

    return pl.pallas_call(
        body,
        out_shape=[jax.ShapeDtypeStruct((d, d), BF16), jax.ShapeDtypeStruct((d, d), BF16),
                   jax.ShapeDtypeStruct((1, d), F32), jax.ShapeDtypeStruct((1, d), F32)],
        compiler_params=pltpu.CompilerParams(vmem_limit_bytes=VMEM_LIMIT_V7X),
        name="mem_bwd",
    )(dk, dv, mb, xhat, rstd, g, wk, wv)


def _adamw(w, g, m, v):
    m = ADAM_B1 * m + (1.0 - ADAM_B1) * g
    v = ADAM_B2 * v + (1.0 - ADAM_B2) * (g * g)
    m_hat = m / (1.0 - ADAM_B1 ** ADAM_STEP)
    v_hat = v / (1.0 - ADAM_B2 ** ADAM_STEP)
    delta = -ADAM_LR * (m_hat / (jnp.sqrt(v_hat) + ADAM_EPS) + ADAM_WD * w)
    return delta, m, v


def _slot_sum(ref):
    g = ref[0].astype(F32)
    for s in range(1, N_DEV):
        g = g + ref[s].astype(F32)
    return g


def _adam_sharded(lands, w, m, v, axis, name):
    rows, cols = w.shape
    nl = len(lands)
    transposed = axis == 1 and nl == 2
    if transposed:
        rows, cols = cols, rows
        tr = 256
        grid = (rows // tr,)
        wblk = pl.BlockSpec((cols, tr), lambda i: (0, i))
        lblk = [pl.BlockSpec((N_DEV, tr, a.shape[2]), lambda i: (0, i, 0)) for a in lands]
    elif axis == 1:
        tr = 256 if rows % 256 == 0 else rows
        grid = (rows // tr,)
        wblk = pl.BlockSpec((tr, cols), lambda i: (i, 0))
        lblk = [pl.BlockSpec((N_DEV, tr, a.shape[2]), lambda i: (0, i, 0)) for a in lands]
    else:
        tc = _col_tile(cols)
        grid = (cols // tc,)
        wblk = pl.BlockSpec((rows, tc), lambda i: (0, i))
        lblk = [pl.BlockSpec((N_DEV, a.shape[1], tc), lambda i: (0, 0, i)) for a in lands]

    def body(*refs):
        w_ref, m_ref, v_ref = refs[nl:nl + 3]
        g_ref, d_ref, nm_ref, nv_ref = refs[nl + 3:]
        g = _slot_sum(refs[0])
        if nl == 2:
            tail = _slot_sum(refs[1])
            if transposed:
                g = jnp.concatenate([g.T, tail.T[:cols - g.shape[1], :]], axis=0)
            elif axis == 1:
                g = jnp.concatenate([g, tail[:, :cols - g.shape[1]]], axis=1)
            else:
                g = jnp.concatenate([g, tail[:rows - g.shape[0], :]], axis=0)
        delta, nm, nv = _adamw(w_ref[...], g, m_ref[...], v_ref[...])
        g_ref[...] = g
        d_ref[...] = delta
        nm_ref[...] = nm
        nv_ref[...] = nv

    shp = pltpu.HBM(w.shape, F32)
    return pl.pallas_call(
        body,
        grid=grid,
        in_specs=lblk + [wblk, wblk, wblk],
        out_specs=[wblk, wblk, wblk, wblk],
        out_shape=[shp, shp, shp, shp],
        compiler_params=_params(1),
        name=name,
    )(*[pltpu.with_memory_space_constraint(a, pltpu.HBM) for a in (*lands, w, m, v)])


def _mesh_pos():
    return lax.axis_index("x"), lax.axis_index("y"), lax.axis_index("c")


def _peer(k):
    x, y, c = _mesh_pos()
    pos = (x ^ (k >> 2), y ^ ((k >> 1) & 1), c ^ (k & 1))
    return pos, 4 * pos[0] + 2 * pos[1] + pos[2]


def _sem_index(row, k):
    return row * (N_DEV - 1) + k - 1


def _window(ref, axis, start, size):
    align = 16 if axis == 0 else LANES
    start = pl.multiple_of(start, align)
    return ref.at[pl.ds(start, size), :] if axis == 0 else ref.at[:, pl.ds(start, size)]


def _piece_refs(piece, srcs, lands, me, peer):
    kind, si, li, axis, base, stride, shape = piece
    if kind == "gather":
        return srcs[si], _window(lands[li], axis, base + stride * me, shape[axis])
    return _window(srcs[si], axis, base + stride * peer, shape[axis]), lands[li].at[me]


def _place_own(srcs, land_shapes, pieces, name):
    ns, nl, npc = len(srcs), len(land_shapes), len(pieces)

    def body(*refs):
        s_refs = refs[:ns]
        l_refs = refs[ns:ns + nl]
        bufs = refs[ns + nl:ns + nl + npc]
        sems = refs[ns + nl + npc]
        x, y, c = _mesh_pos()
        me = 4 * x + 2 * y + c
        loads = []
        for p, piece in enumerate(pieces):
            src, dst = _piece_refs(piece, s_refs, l_refs, me, me)
            cp = pltpu.make_async_copy(src, bufs[p], sems.at[0, p])
            cp.start()
            loads.append((cp, dst))
        stores = []
        for p, (cp, dst) in enumerate(loads):
            cp.wait()
            out = pltpu.make_async_copy(bufs[p], dst, sems.at[1, p])
            out.start()
            stores.append(out)
        for out in stores:
            out.wait()

    out = pl.pallas_call(
        body,
        in_specs=[ANY] * ns,
        out_specs=[HBM] * nl,
        out_shape=[pltpu.HBM(s.shape, s.dtype) for s in land_shapes],
        scratch_shapes=[pltpu.VMEM(pc[6], srcs[pc[1]].dtype) for pc in pieces] + [pltpu.SemaphoreType.DMA((2, npc))],
        compiler_params=pltpu.CompilerParams(vmem_limit_bytes=VMEM_LIMIT_V7X),
        name=name,
    )(*srcs)
    return list(out)


def _comm_start(srcs, lands, pieces, groups, name, after=()):
    ns, nl, na, ng = len(srcs), len(lands), len(after), len(groups)

    def body(*refs):
        s_refs = refs[:ns]
        l_refs = refs[ns:ns + nl]
        outs = refs[ns + nl + na:]
        sems = outs[:2 * ng]
        token = outs[-1]
        x, y, c = _mesh_pos()
        me = 4 * x + 2 * y + c
        for g, members in enumerate(groups):
            for row, p in enumerate(members):
                for k in range(1, N_DEV):
                    pos, peer = _peer(k)
                    src, dst = _piece_refs(pieces[p], s_refs, l_refs, me, peer)
                    pltpu.make_async_remote_copy(src_ref=src, dst_ref=dst, send_sem=sems[2 * g].at[_sem_index(row, k)],
                                                 recv_sem=sems[2 * g + 1].at[_sem_index(row, k)], device_id=pos,
                                                 device_id_type=MESH_ID).start()
        token[...] = jnp.zeros_like(token)

    sem_shapes = []
    for members in groups:
        sem_shapes += [pltpu.SemaphoreType.DMA((len(members) * (N_DEV - 1),))] * 2
    hbm_of = lambda a: pltpu.HBM(a.shape, a.dtype)
    out = pl.pallas_call(
        body,
        in_specs=[HBM] * (ns + nl) + [ANY] * na,
        out_specs=[SEM] * (2 * ng) + [HBM] * (ns + nl) + [pl.BlockSpec(memory_space=pltpu.VMEM)],
        out_shape=sem_shapes + [hbm_of(a) for a in srcs] + [hbm_of(a) for a in lands]
        + [jax.ShapeDtypeStruct((8, LANES), F32)],
        input_output_aliases={i: 2 * ng + i for i in range(ns + nl)},
        compiler_params=pltpu.CompilerParams(has_side_effects=DATAFLOW),
        name=name,
    )(*[pltpu.with_memory_space_constraint(a, pltpu.HBM) for a in list(srcs) + list(lands)], *after)
    sems = [(out[2 * g], out[2 * g + 1]) for g in range(ng)]
    return sems, list(out[2 * ng:2 * ng + ns]), list(out[2 * ng + ns:2 * ng + ns + nl]), out[-1]


def _comm_wait(srcs, lands, pieces, members, sems, after, name):
    ns, nl, na = len(srcs), len(lands), len(after)

    def body(*refs):
        s_refs = refs[:ns]
        l_refs = refs[ns:ns + nl]
        send_sems, recv_sems = refs[ns + nl:ns + nl + 2]
        x, y, c = _mesh_pos()
        me = 4 * x + 2 * y + c
        for row, p in enumerate(members):
            for k in range(1, N_DEV):
                pos, peer = _peer(k)
                src, dst = _piece_refs(pieces[p], s_refs, l_refs, me, peer)
                cp = pltpu.make_async_remote_copy(src_ref=src, dst_ref=dst, send_sem=send_sems.at[_sem_index(row, k)],
                                                  recv_sem=recv_sems.at[_sem_index(row, k)], device_id=pos,
                                                  device_id_type=MESH_ID)
                cp.wait_send()
                cp.wait_recv()

    hbm_of = lambda a: pltpu.HBM(a.shape, a.dtype)
    out = pl.pallas_call(
        body,
        in_specs=[HBM] * (ns + nl) + [SEM, SEM] + [ANY] * na,
        out_specs=[HBM] * (ns + nl),
        out_shape=[hbm_of(a) for a in srcs] + [hbm_of(a) for a in lands],
        input_output_aliases={i: i for i in range(ns + nl)},
        compiler_params=pltpu.CompilerParams(has_side_effects=DATAFLOW),
        name=name,
    )(*srcs, *lands, sems[0], sems[1], *after)
    return list(out[ns:])


def _landed_block(piece, lands, owner):
    _, _, li, axis, base, stride, shape = piece
    return _window(lands[li], axis, base + stride * owner, shape[axis])


def _copy_stage(name, bufs, in_sems, out_sem_sizes, emit, after=()):
    nb, ni, no, na = len(bufs), len(in_sems), len(out_sem_sizes), len(after)

    def body(*refs):
        b_refs = refs[:nb]
        i_refs = refs[nb:nb + ni]
        o_refs = refs[nb + ni + na:nb + ni + na + no]
        emit(b_refs, i_refs, o_refs)
        refs[-1][...] = jnp.zeros_like(refs[-1])

    hbm_of = lambda a: pltpu.HBM(a.shape, a.dtype)
    out = pl.pallas_call(
        body,
        in_specs=[HBM] * nb + [SEM] * ni + [ANY] * na,
        out_specs=[SEM] * no + [HBM] * nb + [pl.BlockSpec(memory_space=pltpu.VMEM)],
        out_shape=[pltpu.SemaphoreType.DMA((n,)) for n in out_sem_sizes] + [hbm_of(a) for a in bufs]
        + [jax.ShapeDtypeStruct((8, LANES), F32)],
        input_output_aliases={i: no + i for i in range(nb)},
        compiler_params=pltpu.CompilerParams(has_side_effects=DATAFLOW),
        name=name,
    )(*[pltpu.with_memory_space_constraint(a, pltpu.HBM) for a in bufs], *in_sems, *after)
    return list(out[:no]), list(out[no:no + nb]), out[-1]


def _remote(src, dst, send, recv, to):
    return pltpu.make_async_remote_copy(src_ref=src, dst_ref=dst, send_sem=send, recv_sem=recv, device_id=to,
                                        device_id_type=MESH_ID)


def _routed_gather(srcs, lands, pieces, meanwhile, name):
    ns, npc = len(srcs), len(pieces)

    def places():
        x, y, c = _mesh_pos()
        index = lambda p: 4 * p[0] + 2 * p[1] + p[2]
        me, sib = (x, y, c), (x, y, 1 - c)
        xnb, ynb = (1 - x, y, c), (x, 1 - y, c)
        got_first = (x ^ (1 - c), y ^ c, c)
        pass_to = (x ^ c, y ^ (1 - c), c)
        diag = (1 - x, 1 - y, c)
        return index, me, sib, xnb, ynb, got_first, pass_to, diag

    def start(b, _, o):
        index, me, sib, xnb, ynb, *_rest = places()
        send_a, recv_sib, recv_nb = o
        for p, piece in enumerate(pieces):
            src, dst = _piece_refs(piece, b[:ns], b[ns:], index(me), 0)
            _remote(src, dst, send_a.at[3 * p], recv_sib.at[p], sib).start()
            _remote(src, dst, send_a.at[3 * p + 1], recv_nb.at[2 * p], xnb).start()
            _remote(src, dst, send_a.at[3 * p + 2], recv_nb.at[2 * p + 1], ynb).start()

    def pass_a(b, i, o):
        index, me, sib, xnb, ynb, got_first, pass_to, _diag = places()
        (recv_nb,) = i
        send_f, recv_f, send_d, recv_d = o
        for p, piece in enumerate(pieces):
            for j, nb in enumerate((xnb, ynb)):
                blk = _landed_block(piece, b, index(nb))
                _remote(blk, blk, send_f.at[2 * p + j], recv_nb.at[2 * p + j], sib).wait_recv()
                _remote(blk, blk, send_f.at[2 * p + j], recv_f.at[2 * p + j], sib).start()
            blk = _landed_block(piece, b, index(got_first))
            _remote(blk, blk, send_d.at[p], recv_d.at[p], pass_to).start()

    def pass_b(b, i, o):
        index, me, sib, *_mid, diag = places()
        (recv_d,) = i
        send_g, recv_g = o
        for p, piece in enumerate(pieces):
            blk = _landed_block(piece, b, index(diag))
            _remote(blk, blk, send_g.at[p], recv_d.at[p], sib).wait_recv()
            _remote(blk, blk, send_g.at[p], recv_g.at[p], sib).start()

    def last(b, i, _):
        index, me, sib, *_others = places()
        send_a, recv_sib, send_f, recv_f, send_d, send_g, recv_g = i
        for p, piece in enumerate(pieces):
            src, dst = _piece_refs(piece, b[:ns], b[ns:], index(me), 0)
            cp = lambda s_sem, r_sem: _remote(src, dst, s_sem, r_sem, sib)
            cp(send_a.at[3 * p], recv_sib.at[p]).wait_recv()
            cp(send_a.at[3 * p], recv_g.at[p]).wait_recv()
            for j in range(3):
                cp(send_a.at[3 * p + j], recv_sib.at[p]).wait_send()
            for j in range(2):
                cp(send_f.at[2 * p + j], recv_f.at[2 * p + j]).wait_recv()
                cp(send_f.at[2 * p + j], recv_f.at[2 * p + j]).wait_send()
            cp(send_d.at[p], recv_sib.at[p]).wait_send()
            cp(send_g.at[p], recv_sib.at[p]).wait_send()

    (send_a, recv_sib, recv_nb), bufs, started = _copy_stage(name + "_start", list(srcs) + list(lands), [],
                                                             [3 * npc, npc, 2 * npc], start)
    srcs, lands = bufs[:ns], bufs[ns:]
    (send_f, recv_f, send_d, recv_d), lands, _ = _copy_stage(name + "_pass_a", lands, [recv_nb],
                                                             [2 * npc, 2 * npc, npc, npc],
                                                             lambda b, i, o: pass_a(b, i, o), after=meanwhile(started))
    (send_g, recv_g), lands, tok = _copy_stage(name + "_pass_b", lands, [recv_d], [npc, npc], pass_b)
    _, bufs, _ = _copy_stage(name + "_last", list(srcs) + list(lands),
                             [send_a, recv_sib, send_f, recv_f, send_d, send_g, recv_g], [], last)
    return bufs[ns:], tok


def _paired_gather(srcs, lands, pieces, groups, after, name):
    ns, ng = len(srcs), len(groups)
    far = (1, 2, 3)

    def me_sib():
        x, y, c = _mesh_pos()
        return 4 * x + 2 * y + c, (x, y, 1 - c)

    def start(b, _, o):
        me, sib = me_sib()
        for g, members in enumerate(groups):
            send, recv_sib, recv_far = o[3 * g:3 * g + 3]
            for r, p in enumerate(members):
                src, dst = _piece_refs(pieces[p], b[:ns], b[ns:], me, 0)
                _remote(src, dst, send.at[4 * r], recv_sib.at[r], sib).start()
                for j in far:
                    _remote(src, dst, send.at[4 * r + j], recv_far.at[3 * r + j - 1], _peer(2 * j)[0]).start()

    def forward(b, i, o):
        _, sib = me_sib()
        for g, members in enumerate(groups):
            send_f, recv_f = o[2 * g:2 * g + 2]
            for r, p in enumerate(members):
                for j in far:
                    blk = _landed_block(pieces[p], b, _peer(2 * j)[1])
                    _remote(blk, blk, send_f.at[3 * r + j - 1], i[g].at[3 * r + j - 1], sib).wait_recv()
                    _remote(blk, blk, send_f.at[3 * r + j - 1], recv_f.at[3 * r + j - 1], sib).start()

    def last(sub):
        def emit(b, i, _):
            send, recv_sib, send_f, recv_f = i
            me, sib = me_sib()
            for r, piece in enumerate(sub):
                src, dst = _piece_refs(piece, b[:-1], b[-1:], me, 0)
                _remote(src, dst, send.at[4 * r], recv_sib.at[r], sib).wait_recv()
                for j in range(4):
                    _remote(src, dst, send.at[4 * r + j], recv_sib.at[r], sib).wait_send()
                for j in far:
                    blk = _landed_block(piece, b[-1:], _peer(2 * j + 1)[1])
                    _remote(blk, blk, send_f.at[3 * r + j - 1], recv_f.at[3 * r + j - 1], sib).wait_recv()
                    _remote(blk, blk, send_f.at[3 * r + j - 1], recv_f.at[3 * r + j - 1], sib).wait_send()
        return emit

    sizes = []
    for members in groups:
        sizes += [4 * len(members), len(members), 3 * len(members)]
    sems, bufs, started = _copy_stage(name + "_start", list(srcs) + list(lands), [], sizes, start, after=after)
    srcs = bufs[:ns]
    state = dict(lands=bufs[ns:])

    def finish(g, after):
        if "passed" not in state:
            sizes_f = []
            for members in groups:
                sizes_f += [3 * len(members)] * 2
            state["passed"], state["lands"], _ = _copy_stage(name + "_pass", state["lands"],
                                                             [sems[3 * k + 2] for k in range(ng)], sizes_f, forward,
                                                             after=after)
            after = ()
        members = groups[g]
        sub = [(pieces[p][0], r, 0) + pieces[p][3:] for r, p in enumerate(members)]
        _, bufs, _ = _copy_stage("%s_last_%d" % (name, g), [srcs[pieces[p][1]] for p in members] + [state["lands"][g]],
                                 [sems[3 * g], sems[3 * g + 1], state["passed"][2 * g], state["passed"][2 * g + 1]],
                                 [], last(sub), after=after)
        return bufs[-1]

    return finish, started


_SMALL_NAMES = ("ln1_g", "ln1_b", "hg_lb_logits", "hg_norm_g", "sg_ln_g", "sg_ln_b", "sg_w_s", "sg_b_s",
                "ln2_g", "ln2_b", "mem_ln_g", "mem_ln_b", "ln3_g", "ln3_b", "ln4_g", "ln4_b")


_VEC_NAMES = ("ln1_g", "ln1_b", "ln2_g", "ln2_b", "mem_ln_g", "mem_ln_b", "ln3_g", "ln3_b", "ln4_g", "ln4_b")
_ROW_NAMES = ("hg_lb_logits", "hg_norm_g", "sg_ln_g", "sg_ln_b", "sg_b_s", "sg_w_s")
VEC_ROWS = 16


def _row_plan(shapes):
    plan, pos = {}, 0
    for k in _ROW_NAMES:
        shp = shapes[k]
        slabs, off = [], pos
        for idx in itertools.product(*[range(dim) for dim in shp[:-2]]):
            slabs.append((idx, off, shp[-2]))
            off += shp[-2]
        plan[k] = (pos, slabs)
        pos = -(-off // 8) * 8
    return plan, -(-pos // 16) * 16


def _pack_small_grads(gs, shapes, loss):
    d = gs[_VEC_NAMES[0]].size
    vec = jnp.concatenate([gs[k].reshape(1, -1) for k in _VEC_NAMES] + [jnp.tile(loss, (1, d // LANES))], axis=0)
    vec = jnp.pad(vec, ((0, VEC_ROWS - vec.shape[0]), (0, 0)))
    plan, total = _row_plan(shapes)
    parts, pos = [], 0
    for k in _ROW_NAMES:
        first, slabs = plan[k]
        rows = gs[k].reshape(-1, LANES)
        end = slabs[-1][1] + slabs[-1][2]
        nxt = -(-end // 8) * 8
        parts.append(jnp.pad(rows, ((0, nxt - first - rows.shape[0]), (0, 0))))
        pos = nxt
    parts.append(jnp.zeros((total - pos, LANES), F32))
    return vec, jnp.concatenate(parts, axis=0)


def _adam_small(land_vec, land_rows, w, m, v):
    names = _VEC_NAMES + _ROW_NAMES
    n = len(names)
    shapes = {k: w[k].shape for k in names}
    plan, _ = _row_plan(shapes)

    def body(*refs):
        lv_ref, lr_ref = refs[:2]
        w_refs, m_refs, v_refs = refs[2:2 + n], refs[2 + n:2 + 2 * n], refs[2 + 2 * n:2 + 3 * n]
        outs = refs[2 + 3 * n:2 + 7 * n]
        loss_ref = refs[2 + 7 * n]
        gv_s, gr_s = refs[3 + 7 * n:]
        gv_s[...] = _slot_sum(lv_ref)
        gr_s[...] = _slot_sum(lr_ref)
        loss_ref[...] = gv_s[len(_VEC_NAMES):len(_VEC_NAMES) + 1, :LANES]
        for p, k in enumerate(names):
            if k in _VEC_NAMES:
                row = _VEC_NAMES.index(k)
                slabs = [((), None, None)]
            else:
                slabs = plan[k][1]
            for idx, off, rows in slabs:
                g = gv_s[row:row + 1, :] if off is None else gr_s[off:off + rows, :]
                sel = idx + (slice(None), slice(None))
                delta, nm, nv = _adamw(w_refs[p][sel], g, m_refs[p][sel], v_refs[p][sel])
                for o, val in zip(range(4), (g, delta, nm, nv)):
                    outs[o * n + p][sel] = val

    flat = lambda tree: [tree[k] for k in names]
    shp = [jax.ShapeDtypeStruct(shapes[k], F32) for k in names]
    out = pl.pallas_call(
        body,
        out_shape=shp * 4 + [jax.ShapeDtypeStruct((1, LANES), F32)],
        scratch_shapes=[pltpu.VMEM(land_vec.shape[1:], F32), pltpu.VMEM(land_rows.shape[1:], F32)],
        name="adam_small",
    )(land_vec, land_rows, *flat(w), *flat(m), *flat(v))
    return [dict(zip(names, out[o * n:(o + 1) * n])) for o in range(4)], out[4 * n]


_COL_FFN = ("ffn1_w_gate", "ffn1_w_up", "ffn2_w_gate", "ffn2_w_up")
_ROW_FFN = ("ffn1_w_down", "ffn2_w_down")
_ROW_SQ = ("w_out", "xa_w_q", "xa_w_k", "xa_w_v", "xa_w_o")
_BIG_NAMES = ("ffn1_w_gate", "ffn1_w_up", "ffn1_w_down", "w_in", "w_out", "xa_w_q", "xa_w_k", "xa_w_v", "xa_w_o",
              "ffn2_w_gate", "ffn2_w_up", "ffn2_w_down")


def _ffn_split(fs):
    main = (fs // MXU_WIDTH_V7X) * MXU_WIDTH_V7X
    tail = fs - main
    tail_pad = -(-tail // LANES) * LANES
    assert main > 0 and tail > 0
    return main, tail, tail_pad


def _layout(name, shard_shape):
    r, c = shard_shape
    if name in _COL_FFN:
        main, tail, pad = _ffn_split(c)
        return (r, N_DEV * (main + pad)), [(1, 0, main, (r, main), (0, main)),
                                           (1, N_DEV * main, pad, (r, pad), (main, c))]
    if name in _ROW_FFN:
        main, tail, pad = _ffn_split(r)
        return (N_DEV * (main + pad), c), [(0, 0, main, (main, c), (0, main)),
                                           (0, N_DEV * main, pad, (pad, c), (main, r))]
    if name == "w_in":
        return (r, N_DEV * c), [(1, 0, c, (r, c), (0, c))]
    return (N_DEV * r, c), [(0, 0, r, (r, c), (0, r))]


def _shard_pieces(name, shard):
    out = []
    for axis, _, _, shape, (lo, hi) in _layout(name, shard.shape)[1]:
        part = shard[lo:hi, :] if axis == 0 else shard[:, lo:hi]
        pad = [(0, shape[0] - part.shape[0]), (0, shape[1] - part.shape[1])]
        out.append(jnp.pad(part, pad).astype(BF16))
    return out


def _gather_plan(names, shards):
    srcs, land_shapes, pieces, index = [], [], [], {}
    for li, name in enumerate(names):
        shape2d, parts = _layout(name, shards[name].shape)
        land_shapes.append(jax.ShapeDtypeStruct(shape2d, BF16))
        index[name] = []
        for (axis, base, stride, shape, _), src in zip(parts, _shard_pieces(name, shards[name])):
            index[name].append(len(pieces))
            pieces.append(("gather", len(srcs), li, axis, base, stride, shape))
            srcs.append(src)
    return srcs, land_shapes, pieces, index


def _scatter_plan(names, grads, shard_shapes):
    srcs, land_shapes, pieces, index = [], [], [], {}
    for si, name in enumerate(names):
        _, parts = _layout(name, shard_shapes[name])
        srcs.append(grads[name])
        index[name] = []
        for axis, base, stride, shape, _ in parts:
            index[name].append(len(land_shapes))
            pieces.append(("scatter", si, len(land_shapes), axis, base, stride, shape))
            land_shapes.append(jax.ShapeDtypeStruct((N_DEV,) + shape, grads[name].dtype))
    return srcs, land_shapes, pieces, index


def _small_views(small):
    row = lambda a: a.reshape(1, -1)
    ln = {k: row(small[k]) for k in ("ln1_g", "ln1_b", "ln2_g", "ln2_b", "ln3_g", "ln3_b", "ln4_g", "ln4_b",
                                      "mem_ln_g", "mem_ln_b", "hg_norm_g")}
    sg_w = small["sg_w_s"].reshape(SG_GROUPS, SG_CHUNK, SG_CHUNK)
    sg = dict(logits=jnp.swapaxes(small["hg_lb_logits"], 0, 1),
              g=small["sg_ln_g"].reshape(SG_GROUPS, 1, SG_DIM), b=small["sg_ln_b"].reshape(SG_GROUPS, 1, SG_DIM),
              w=sg_w, bs=small["sg_b_s"].reshape(SG_GROUPS, 1, SG_CHUNK))
    return ln, sg


def _forward(x, xb, mem, target, get_w, small, first_deps=()):
    ln, sg = _small_views(small)
    a1, b1, s1 = _ffn_up(xb, get_w("ffn1_w_gate", ()), get_w("ffn1_w_up", ()), "ffn1_up", deps=first_deps)
    h1b, xh1, rs1 = _mm_res_ln(s1, get_w("ffn1_w_down", (s1,)), x, ln["ln1_g"], ln["ln1_b"], 0.5, "ffn1_down_ln")
    proj = _mm_nn(h1b, get_w("w_in", (h1b,)), "mix_in")
    oraw, mix, states = _hgrn_fwd(proj, sg["logits"], ln["hg_norm_g"])
    mix = _sgu_fwd(proj, mix, sg["g"], sg["b"], sg["w"], sg["bs"])
    h2b, xh2, rs2 = _mm_res_ln(mix, get_w("w_out", (mix,)), (xh1, ln["ln1_g"], ln["ln1_b"]), ln["ln2_g"], ln["ln2_b"],
                               1.0, "mix_out_ln")
    mb, mxh, mrs, kb, vb = _mem_kv(mem, ln["mem_ln_g"], ln["mem_ln_b"], get_w("xa_w_k", (h2b,)), get_w("xa_w_v", (h2b,)))
    qb, att = _attn_fwd(h2b, get_w("xa_w_q", (mrs,)), kb, vb)
    h3b, xh3, rs3 = _mm_res_ln(att, get_w("xa_w_o", (att,)), (xh2, ln["ln2_g"], ln["ln2_b"]), ln["ln3_g"], ln["ln3_b"],
                               1.0, "attn_out_ln")
    a2, b2, s2 = _ffn_up(h3b, get_w("ffn2_w_gate", (h3b,)), get_w("ffn2_w_up", (h3b,)), "ffn2_up")
    loss, dy4, dy4b, dg4, db4 = _mm_res_ln(s2, get_w("ffn2_w_down", (s2,)), (xh3, ln["ln3_g"], ln["ln3_b"]),
                                           ln["ln4_g"], ln["ln4_b"], 0.5, "ffn2_down_ln_loss", target=target)
    return dict(xb=xb, a1=a1, b1=b1, s1=s1, h1b=h1b, xh1=xh1, rs1=rs1, proj=proj, oraw=oraw, mix=mix, states=states,
                h2b=h2b, xh2=xh2, rs2=rs2, mb=mb, mxh=mxh, mrs=mrs, kb=kb, vb=vb, qb=qb, att=att, h3b=h3b, xh3=xh3,
                rs3=rs3, a2=a2, b2=b2, s2=s2, loss=loss, dy4=dy4, dy4b=dy4b, dg4=dg4, db4=db4)


def _backward(sv, wt, small, send):
    ln, sg = _small_views(small)
    gs = {"ln4_g": sv["dg4"], "ln4_b": sv["db4"]}
    loss, dy4, dy4b = sv["loss"], sv["dy4"], sv["dy4b"]
    g_down2 = _mm_tn(sv["s2"], dy4b, "g_ffn2_down", scale=0.5)
    da2, db2, dy3, dy3b, gs["ln3_g"], gs["ln3_b"] = _ffn_bwd_fused(
        dy4, dy4b, wt["ffn2_w_down"], wt["ffn2_w_gate"], wt["ffn2_w_up"], sv["a2"], sv["b2"], 0.5,
        (sv["xh3"], sv["rs3"], ln["ln3_g"]), "ffn2_bwd")
    g_gate2 = _mm_tn(sv["h3b"], da2, "g_ffn2_gate")
    g_up2 = _mm_tn(sv["h3b"], db2, "g_ffn2_up")
    tok = send(("ffn2_w_down", "ffn2_w_gate", "ffn2_w_up"), (g_down2, g_gate2, g_up2))

    g_o = _mm_tn(sv["att"], dy3b, "g_xa_o", deps=(tok,))
    dqb, dk, dv = _attn_bwd(dy3b, wt["xa_w_o"], sv["qb"], sv["kb"], sv["vb"])
    g_q = _mm_tn(sv["h2b"], dqb, "g_xa_q")
    g_k, g_v, gs["mem_ln_g"], gs["mem_ln_b"] = _mem_bwd(dk, dv, sv["mb"], sv["mxh"], sv["mrs"], ln["mem_ln_g"],
                                                        wt["xa_w_k"], wt["xa_w_v"])
    tok = send(("xa_w_o", "xa_w_q", "xa_w_k", "xa_w_v"), (g_o, g_q, g_k, g_v))
    dy2, dy2b, gs["ln2_g"], gs["ln2_b"] = _dx_ln(dy3, [(dqb, wt["xa_w_q"])], (sv["xh2"], sv["rs2"], ln["ln2_g"]),
                                                 "attn_dx_ln", deps=(tok,))

    g_out = _mm_tn(sv["mix"], dy2b, "g_w_out")
    dmix = _mm_nt(dy2b, wt["w_out"], "mix_out_bwd")
    dq, dfz, div, dgg, dlg, dgn = _hgrn_bwd(sv["proj"], sv["oraw"], dmix, sv["states"], sg["logits"], ln["hg_norm_g"])
    du, dvv, gs["sg_ln_g"], gs["sg_ln_b"], gs["sg_w_s"], gs["sg_b_s"] = _sgu_bwd(
        sv["proj"], dmix, sg["g"], sg["b"], sg["w"], sg["bs"])
    gs["hg_lb_logits"] = jnp.swapaxes(dlg, 0, 1)
    gs["hg_norm_g"] = jnp.sum(dgn, axis=0)
    dproj = [dq, dfz, div, dgg, du, dvv]
    g_in = _mm_tn(sv["h1b"], dproj, "g_w_in")
    tok = send(("w_out", "w_in"), (g_out, g_in))
    dy1, dy1b, gs["ln1_g"], gs["ln1_b"] = _dx_ln(dy2, [(dproj, wt["w_in"])], (sv["xh1"], sv["rs1"], ln["ln1_g"]),
                                                 "mix_dx_ln", deps=(tok,))

    g_down1 = _mm_tn(sv["s1"], dy1b, "g_ffn1_down", scale=0.5)
    tok = send(("ffn1_w_down",), (g_down1,))
    da1, db1 = _ffn_bwd_act(dy1b, wt["ffn1_w_down"], sv["a1"], sv["b1"], 0.5, "ffn1_bwd_act", deps=(tok,))
    g_gate1 = _mm_tn(sv["xb"], da1, "g_ffn1_gate")
    tok = send(("ffn1_w_gate",), (g_gate1,))
    g_up1 = _mm_tn(sv["xb"], db1, "g_ffn1_up", deps=(tok,))
    tok = send(("ffn1_w_up",), (g_up1,))
    grad_x = _dx_ln(dy1, [(da1, wt["ffn1_w_gate"]), (db1, wt["ffn1_w_up"])], None, "ffn1_dx", deps=(tok,))
    return loss, grad_x, gs


_WEIGHT_NAMES = ("ffn1_w_gate", "ffn1_w_up", "ffn1_w_down", "ln1_g", "ln1_b", "w_in", "hg_lb_logits", "hg_norm_g",
                 "sg_ln_g", "sg_ln_b", "sg_w_s", "sg_b_s", "w_out", "ln2_g", "ln2_b", "mem_ln_g", "mem_ln_b",
                 "xa_w_q", "xa_w_k", "xa_w_v", "xa_w_o", "ln3_g", "ln3_b", "ffn2_w_gate", "ffn2_w_up", "ffn2_w_down",
                 "ln4_g", "ln4_b")
_FIRST = ("ffn1_w_gate", "ffn1_w_up")
_SECOND = ("ffn1_w_down", "w_in")
_THIRD = ("w_out", "xa_w_k", "xa_w_v", "xa_w_q", "xa_w_o", "ffn2_w_gate", "ffn2_w_up", "ffn2_w_down")


def kernel(x, mem, ffn1_w_gate, ffn1_w_up, ffn1_w_down, ln1_g, ln1_b, w_in, hg_lb_logits, hg_norm_g, sg_ln_g, sg_ln_b, sg_w_s, sg_b_s, w_out, ln2_g, ln2_b, mem_ln_g, mem_ln_b, xa_w_q, xa_w_k, xa_w_v, xa_w_o, ln3_g, ln3_b, ffn2_w_gate, ffn2_w_up, ffn2_w_down, ln4_g, ln4_b, loss_target, m_ffn1_w_gate, m_ffn1_w_up, m_ffn1_w_down, m_ln1_g, m_ln1_b, m_w_in, m_hg_lb_logits, m_hg_norm_g, m_sg_ln_g, m_sg_ln_b, m_sg_w_s, m_sg_b_s, m_w_out, m_ln2_g, m_ln2_b, m_mem_ln_g, m_mem_ln_b, m_xa_w_q, m_xa_w_k, m_xa_w_v, m_xa_w_o, m_ln3_g, m_ln3_b, m_ffn2_w_gate, m_ffn2_w_up, m_ffn2_w_down, m_ln4_g, m_ln4_b, v_ffn1_w_gate, v_ffn1_w_up, v_ffn1_w_down, v_ln1_g, v_ln1_b, v_w_in, v_hg_lb_logits, v_hg_norm_g, v_sg_ln_g, v_sg_ln_b, v_sg_w_s, v_sg_b_s, v_w_out, v_ln2_g, v_ln2_b, v_mem_ln_g, v_mem_ln_b, v_xa_w_q, v_xa_w_k, v_xa_w_v, v_xa_w_o, v_ln3_g, v_ln3_b, v_ffn2_w_gate, v_ffn2_w_up, v_ffn2_w_down, v_ln4_g, v_ln4_b):
    args = dict(locals())
    w = {k: args[k] for k in _WEIGHT_NAMES}
    m = {k: args["m_" + k] for k in _WEIGHT_NAMES}
    v = {k: args["v_" + k] for k in _WEIGHT_NAMES}
    shards = {k: w[k][0] for k in _BIG_NAMES}
    shard_shapes = {k: shards[k].shape for k in _BIG_NAMES}
    small = {k: (w[k][0] if k != "hg_lb_logits" else w[k]) for k in _SMALL_NAMES}

    srcs1, shapes1, pieces1, idx1 = _gather_plan(_FIRST, shards)
    lands1 = _place_own(srcs1, shapes1, pieces1, "gather_first_own")
    prepared = {}

    def prepare_rest(started):
        later = {k: shards[k] + started[0, 0] for k in _SECOND + _THIRD}
        placed = ()
        for key, names in (("second", _SECOND), ("third", _THIRD)):
            srcs, shapes, pieces, idx = _gather_plan(names, later)
            lands = _place_own(srcs, shapes, pieces, "gather_%s_own" % key)
            prepared[key] = (srcs, lands, pieces, idx)
            placed += tuple(lands)
        prepared["xb"] = _to_bf16(x[0], "x_bf16", deps=(started,))
        return placed + (prepared["xb"],)

    lands1, tok1 = _routed_gather(srcs1, lands1, pieces1, prepare_rest, "gather_first")
    srcs_p, lands_p, pieces_p, idx_p = prepared["second"]
    finish_second, tok_p = _paired_gather(srcs_p, lands_p, pieces_p, [list(idx_p[k]) for k in _SECOND], (tok1,),
                                          "gather_second")
    srcs2, lands2, pieces2, idx2 = prepared["third"]
    groups2 = [list(idx2[k]) for k in _THIRD]
    sems2, srcs2, lands2, tok2 = _comm_start(srcs2, lands2, pieces2, groups2, "gather_third_start", after=(tok_p,))
    wt = dict(zip(_FIRST, lands1))
    pending = {k: gi for gi, k in enumerate(_THIRD)}

    def get_w(name, after):
        if name in _SECOND and name not in wt:
            wt[name] = finish_second(_SECOND.index(name), after)
        if name in pending:
            gi = pending.pop(name)
            si = [pieces2[p][1] for p in groups2[gi]]
            sub = [(pieces2[p][0], row, 0) + pieces2[p][3:] for row, p in enumerate(groups2[gi])]
            wt[name] = _comm_wait([srcs2[s] for s in si], [lands2[gi]], sub, list(range(len(sub))), sems2[gi],
                                  after, "gather_wait_" + name)[0]
        return wt[name]

    sv = _forward(x[0], prepared["xb"], mem[0], loss_target[0], get_w, small, first_deps=(tok2,))

    sent = []

    def send(names, grads):
        srcs, shapes, pieces, idx = _scatter_plan(names, dict(zip(names, grads)), shard_shapes)
        lands = _place_own(srcs, shapes, pieces, "grads_own_%d" % len(sent))
        sems, srcs, lands, tok = _comm_start(srcs, lands, pieces, [list(range(len(pieces)))],
                                             "grads_start_%d" % len(sent))
        sent.append((names, srcs, lands, pieces, idx, sems[0]))
        return tok

    loss, grad_x, gs = _backward(sv, wt, small, send)

    ssrc = list(_pack_small_grads(gs, {k: w[k].shape for k in _SMALL_NAMES}, loss))
    sp = [("scatter", i, i, 0, 0, 0, a.shape) for i, a in enumerate(ssrc)]
    sshape = [jax.ShapeDtypeStruct((N_DEV,) + a.shape, F32) for a in ssrc]
    sl = _place_own(ssrc, sshape, sp, "small_own")
    ssem, ssrc, sl, _ = _comm_start(ssrc, sl, sp, [[0, 1]], "small_start")

    out_g, out_d, out_m, out_v = {}, {}, {}, {}
    after = (grad_x,)
    for n_sent, (names, srcs, lands, pieces, idx, sems) in enumerate(sent):
        lands = _comm_wait(srcs, lands, pieces, list(range(len(pieces))), sems, after, "grads_wait_%d" % n_sent)
        for k in names:
            axis = 1 if (k in _COL_FFN or k == "w_in") else 0
            if k in _COL_FFN:
                done = _adam_sharded([lands[i] for i in idx[k]], w[k][0].T, m[k][0].T, v[k][0].T, axis, "adam_" + k)
                res = [r.T for r in done]
            else:
                res = done = _adam_sharded([lands[i] for i in idx[k]], w[k][0], m[k][0], v[k][0], axis, "adam_" + k)
            out_g[k], out_d[k], out_m[k], out_v[k] = [r[None] for r in res]
        after = (done[3],)
    sl = _comm_wait(ssrc, sl, sp, [0, 1], ssem[0], after, "small_wait")
    small_out, loss_sum = _adam_small(sl[0], sl[1], w, m, v)
    for dst, res in zip((out_g, out_d, out_m, out_v), small_out):
        dst.update(res)
    loss_all = loss_sum[0, 0]
    return (loss_all, grad_x[None], *[out_g[k] for k in _WEIGHT_NAMES], *[out_d[k] for k in _WEIGHT_NAMES],
            *[out_m[k] for k in _WEIGHT_NAMES], *[out_v[k] for k in _WEIGHT_NAMES])
```

```python
import itertools

import jax
import jax.numpy as jnp
import numpy as np
from jax import lax
from jax.experimental import pallas as pl
from jax.experimental.pallas import tpu as pltpu

F32 = jnp.float32
BF16 = jnp.bfloat16

N_DEV = 8
ALPHA = 2.0 ** 0.25
LN_EPS = 1e-5
HG_HEADS = 4
HG_DIM = 128
SG_GROUPS = 4
SG_DIM = 128
SG_CHUNK = 128
X_HEADS = 4
HG_BLOCK = 16
HG_UNROLL = 16
ADAM_LR = 0.001
ADAM_B1 = 0.9
ADAM_B2 = 0.999
ADAM_EPS = 1e-08
ADAM_WD = 0.01
ADAM_STEP = 10
VMEM_LIMIT_V7X = 48 * 1024 * 1024
MXU_WIDTH_V7X = 256
LANES = 128
MESH_ID = pl.DeviceIdType.MESH
ANY = pl.BlockSpec(memory_space=pl.ANY)
HBM = pl.BlockSpec(memory_space=pltpu.HBM)
SEM = pl.BlockSpec(memory_space=pltpu.SEMAPHORE)
DATAFLOW = pltpu.SideEffectType.DATAFLOW_SIDE_EFFECTING


def _params(n_axes):
    return pltpu.CompilerParams(dimension_semantics=("arbitrary",) * n_axes, vmem_limit_bytes=VMEM_LIMIT_V7X)


def _dot(a, b):
    return jnp.dot(a, b, preferred_element_type=F32)


def _dot_nt(a, b):
    return lax.dot_general(a, b, (((1,), (1,)), ((), ())), preferred_element_type=F32)


def _dot_tn(a, b):
    return lax.dot_general(a, b, (((0,), (0,)), ((), ())), preferred_element_type=F32)


def _sigmoid(x):
    return 1.0 / (1.0 + jnp.exp(-x))


def _silu_and_grad(a):
    sig = _sigmoid(a)
    return a * sig, sig * (1.0 + a * (1.0 - sig))


_GELU_C = 0.7978845608028654


def _gelu_and_grad(x):
    inner = _GELU_C * (x + 0.044715 * x * x * x)
    t = jnp.tanh(inner)
    val = 0.5 * x * (1.0 + t)
    grad = 0.5 * (1.0 + t) + 0.5 * x * (1.0 - t * t) * _GELU_C * (1.0 + 3.0 * 0.044715 * x * x)
    return val, grad


def _ln_fwd(y, g, b):
    mu = jnp.mean(y, axis=-1, keepdims=True)
    yc = y - mu
    var = jnp.mean(yc * yc, axis=-1, keepdims=True)
    rstd = lax.rsqrt(var + LN_EPS)
    xhat = yc * rstd
    return xhat * g + b, xhat, rstd


def _ln_bwd(dh, xhat, rstd, g):
    dxh = dh * g
    m1 = jnp.mean(dxh, axis=-1, keepdims=True)
    m2 = jnp.mean(dxh * xhat, axis=-1, keepdims=True)
    dy = rstd * (dxh - m1 - xhat * m2)
    dg = jnp.sum(dh * xhat, axis=0, keepdims=True)
    db = jnp.sum(dh, axis=0, keepdims=True)
    return dy, dg, db


def _mask_dot(mask, x):
    hi = x.astype(BF16)
    lo = (x - hi.astype(F32)).astype(BF16)
    n = mask.shape[0]
    parts = [_dot(mask, hi[r:r + n, :]) + _dot(mask, lo[r:r + n, :]) for r in range(0, x.shape[0], n)]
    return parts[0] if len(parts) == 1 else jnp.concatenate(parts, axis=0)


def _block_masks(n):
    r = np.arange(n)[:, None]
    c = np.arange(n)[None, :]
    same = (r // HG_BLOCK) == (c // HG_BLOCK)
    return jnp.asarray(np.stack([same & (c <= r), same & (c >= r), same]), BF16)


def _row_tile(t):
    return min(t, 512)


def _token_tile(t):
    return min(t, 1024)


def _col_tile(n):
    for cand in (512, 256, 128):
        if n % cand == 0:
            return cand
    return n


def _resident(w):
    return pl.BlockSpec(w.shape, lambda *_: (0, 0), pipeline_mode=pl.Buffered(1))


def _drop_deps(body, n_in, n_deps):
    if n_deps == 0:
        return body
    return lambda *refs: body(*refs[:n_in], *refs[n_in + n_deps:])


def _to_bf16(x, name, deps=()):
    t, d = x.shape
    tm = _row_tile(t)

    def body(x_ref, o_ref):
        o_ref[...] = x_ref[...].astype(BF16)

    row = pl.BlockSpec((tm, d), lambda i: (i, 0))
    return pl.pallas_call(
        _drop_deps(body, 1, len(deps)),
        grid=(t // tm,),
        in_specs=[row] + [ANY] * len(deps),
        out_specs=row,
        out_shape=jax.ShapeDtypeStruct((t, d), BF16),
        compiler_params=_params(1),
        name=name,
    )(x, *deps)


def _ffn_up(hb, wg, wu, name, deps=()):
    t, d = hb.shape
    f = wg.shape[1]
    tm = _row_tile(t)
    tn = _col_tile(f)

    def body(h_ref, wg_ref, wu_ref, a_ref, b_ref, s_ref):
        h = h_ref[...]
        for c in range(f // tn):
            cols = slice(c * tn, (c + 1) * tn)
            a = _dot(h, wg_ref[:, cols])
            b = _dot(h, wu_ref[:, cols])
            a_ref[:, cols] = a.astype(BF16)
            b_ref[:, cols] = b.astype(BF16)
            s_ref[:, cols] = (a * _sigmoid(a) * b).astype(BF16)

    act = pl.BlockSpec((tm, f), lambda i: (i, 0))
    return pl.pallas_call(
        _drop_deps(body, 3, len(deps)),
        grid=(t // tm,),
        in_specs=[pl.BlockSpec((tm, d), lambda i: (i, 0)), _resident(wg), _resident(wu)] + [ANY] * len(deps),
        out_specs=[act, act, act],
        out_shape=[jax.ShapeDtypeStruct((t, f), BF16)] * 3,
        compiler_params=_params(1),
        name=name,
    )(hb, wg, wu, *deps)


def _mm_res_ln(lhs, w, res, g, b, coef, name, target=None):
    t, kd = lhs.shape
    d = w.shape[1]
    tm = _row_tile(t)
    nt = t // tm
    from_norm = isinstance(res, tuple)
    n_res = 3 if from_norm else 1

    def body(*refs):
        l_ref, w_ref = refs[:2]
        r_refs = refs[2:2 + n_res]
        g_ref, b_ref = refs[2 + n_res:4 + n_res]
        rest = refs[4 + n_res:]
        prev = r_refs[0][...] * r_refs[1][...] + r_refs[2][...] if from_norm else r_refs[0][...]
        y = ALPHA * prev + coef * _dot(l_ref[...], w_ref[...])
        h, xhat, rstd = _ln_fwd(y, g_ref[...], b_ref[...])
        if target is None:
            hb_ref, xh_ref, rs_ref = rest
            hb_ref[...] = h.astype(BF16)
            xh_ref[...] = xhat
            rs_ref[...] = rstd
            return
        t_ref, loss_ref, dy_ref, dyb_ref, dg_ref, db_ref, lacc = rest
        i = pl.program_id(0)

        @pl.when(i == 0)
        def _():
            lacc[...] = jnp.zeros_like(lacc)
            dg_ref[...] = jnp.zeros_like(dg_ref)
            db_ref[...] = jnp.zeros_like(db_ref)

        err = h - t_ref[...]
        lacc[...] += jnp.sum(err * err, axis=0, keepdims=True)
        dy, dg, db = _ln_bwd(err * (1.0 / d), xhat, rstd, g_ref[...])
        dy_ref[...] = dy
        dyb_ref[...] = dy.astype(BF16)
        dg_ref[...] += dg
        db_ref[...] += db

        @pl.when(i == nt - 1)
        def _():
            loss_ref[...] = jnp.zeros_like(loss_ref) + jnp.sum(lacc[...], axis=1, keepdims=True) * (0.5 / d)

    row = pl.BlockSpec((tm, d), lambda i: (i, 0))
    vec = pl.BlockSpec((1, d), lambda i: (0, 0))
    res_specs = [row, vec, vec] if from_norm else [row]
    res_args = list(res) if from_norm else [res]
    in_specs = [pl.BlockSpec((tm, kd), lambda i: (i, 0)), _resident(w)] + res_specs + [vec, vec]
    args = [lhs, w] + res_args + [g, b]
    if target is None:
        out_specs = [row, row, pl.BlockSpec((tm, 1), lambda i: (i, 0))]
        out_shape = [jax.ShapeDtypeStruct((t, d), BF16), jax.ShapeDtypeStruct((t, d), F32), pltpu.HBM((t, 1), F32)]
        scratch = []
    else:
        in_specs.append(row)
        args.append(target)
        out_specs = [pl.BlockSpec((1, LANES), lambda i: (0, 0)), row, row, vec, vec]
        out_shape = [jax.ShapeDtypeStruct((1, LANES), F32), jax.ShapeDtypeStruct((t, d), F32),
                     jax.ShapeDtypeStruct((t, d), BF16), jax.ShapeDtypeStruct((1, d), F32),
                     jax.ShapeDtypeStruct((1, d), F32)]
        scratch = [pltpu.VMEM((1, d), F32)]
    return pl.pallas_call(
        body,
        grid=(nt,),
        in_specs=in_specs,
        out_specs=out_specs,
        out_shape=out_shape,
        scratch_shapes=scratch,
        compiler_params=_params(1),
        name=name,
    )(*args)


def _store_slabs(o_ref, first, tile):
    for s in range(tile.shape[1] // LANES):
        o_ref[first + s] = tile[:, s * LANES:(s + 1) * LANES]


def _mm_nn(lhs, w, name):
    t, kd = lhs.shape
    n = w.shape[1]
    tm = _row_tile(t)
    tn = _col_tile(n)

    def body(l_ref, w_ref, o_ref):
        lhs_v = l_ref[...]
        for c in range(n // tn):
            _store_slabs(o_ref, c * (tn // LANES), _dot(lhs_v, w_ref[:, c * tn:(c + 1) * tn]))

    return pl.pallas_call(
        body,
        grid=(t // tm,),
        in_specs=[pl.BlockSpec((tm, kd), lambda i: (i, 0)), _resident(w)],
        out_specs=pl.BlockSpec((n // LANES, tm, LANES), lambda i: (0, i, 0)),
        out_shape=jax.ShapeDtypeStruct((n // LANES, t, LANES), F32),
        compiler_params=_params(1),
        name=name,
    )(lhs, w)


def _lower_bound(lg):
    m = jnp.max(lg, axis=0, keepdims=True)
    e = jnp.exp(lg - m)
    return e[0:1, :] / jnp.sum(e, axis=0, keepdims=True)


def _forget_terms(fz, lb):
    e = jnp.exp(-jnp.abs(fz))
    r = 1.0 / (1.0 + e)
    pos = fz >= 0.0
    sig = jnp.where(pos, r, e * r)
    nsig = jnp.where(pos, e * r, r)
    f = lb + (1.0 - lb) * sig
    k = (1.0 - lb) * nsig
    return sig, nsig, f, k


def _hg_tile(t):
    return min(t, 1024)


HG_HALF = HG_BLOCK // 2
NEG_BIG = -1e30


def _halves(a):
    return a[:HG_HALF, :], a[HG_HALF:, :]


def _causal_halves(s):
    return (0, 1) if s < HG_HALF else (1,)


def _decay_from(b_half, b_s, s, h, tidx):
    first = s - h * HG_HALF
    diff = b_half - b_s
    if first > 0:
        diff = jnp.where(tidx >= first, diff, NEG_BIG)
    return jnp.exp(diff)


def _hgrn_fwd(proj, logits, gn):
    t = proj.shape[1]
    ct = _hg_tile(t)
    nct = t // ct
    nblk = ct // HG_BLOCK
    nh = HG_HEADS
    mrows = min(ct, 256)

    def body(q_ref, fz_ref, iv_ref, gg_ref, lg_ref, gn_ref, mask_ref, oraw_ref, oa_ref, st_ref,
             state, qt_s, kt_s, k_s, b_s, dec_s):
        c = pl.program_id(1)

        @pl.when(c == 0)
        def _():
            state[...] = jnp.zeros_like(state)

        lb = _lower_bound(lg_ref[...])
        q = q_ref[...]
        _, _, f, k = _forget_terms(fz_ref[...], lb)
        logf = jnp.log(f)
        b = _mask_dot(mask_ref[0], logf)
        bend = _mask_dot(mask_ref[2], logf)
        qt_s[...] = (q * jnp.exp(b)).astype(BF16)
        kt_s[...] = (k * jnp.exp(bend - b)).astype(BF16)
        k_s[...] = k
        b_s[...] = b
        dec_s[...] = jnp.exp(bend)
        tidx = lax.broadcasted_iota(jnp.int32, (HG_HALF, HG_DIM), 0)

        def blk(i, carry):
            r0 = pl.multiple_of(i * HG_BLOCK, HG_BLOCK)
            rows = pl.ds(r0, HG_BLOCK)
            st = state[...]
            stb = st.astype(BF16)
            st_ref[i] = stb
            v = iv_ref[rows, :]
            qq = q_ref[rows, :]
            kk = k_s[rows, :]
            bb = b_s[rows, :]
            o = list(_halves(_dot_nt(qt_s[rows, :], stb)))
            qh, bh = _halves(qq), _halves(bb)
            for s in range(HG_BLOCK):
                ks, vs = kk[s:s + 1, :], v[s:s + 1, :]
                for h in _causal_halves(s):
                    e = _decay_from(bh[h], bb[s:s + 1, :], s, h, tidx)
                    acol = jnp.sum(qh[h] * (ks * e), axis=1, keepdims=True)
                    o[h] = o[h] + acol * vs
            oraw_ref[rows, :] = jnp.concatenate(o, axis=0)
            state[...] = st * dec_s[pl.ds(r0, 1), :] + _dot_tn(v.astype(BF16), kt_s[rows, :])
            return carry

        lax.fori_loop(0, nblk, blk, 0, unroll=HG_UNROLL)
        oraw = oraw_ref[...]
        r = lax.rsqrt(jnp.mean(oraw * oraw, axis=-1, keepdims=True) + LN_EPS)
        gg = gg_ref[...]
        oa_ref[...] = (oraw * r * gn_ref[...] * gg * _sigmoid(gg)).astype(BF16)

    def slab(off):
        return pl.BlockSpec((None, ct, HG_DIM), lambda h, c: (off + h, c, 0))

    return pl.pallas_call(
        body,
        grid=(nh, nct),
        in_specs=[slab(0), slab(nh), slab(2 * nh), slab(3 * nh),
                  pl.BlockSpec((None, 2, HG_DIM), lambda h, c: (h, 0, 0)),
                  pl.BlockSpec((1, HG_DIM), lambda h, c: (0, 0)),
                  pl.BlockSpec((3, mrows, mrows), lambda h, c: (0, 0, 0))],
        out_specs=[slab(0), pl.BlockSpec((ct, HG_DIM), lambda h, c: (c, h)),
                   pl.BlockSpec((None, nblk, HG_DIM, HG_DIM), lambda h, c: (h, c, 0, 0))],
        out_shape=[jax.ShapeDtypeStruct((nh, t, HG_DIM), F32),
                   jax.ShapeDtypeStruct((t, (nh + SG_GROUPS) * HG_DIM), BF16),
                   jax.ShapeDtypeStruct((nh, t // HG_BLOCK, HG_DIM, HG_DIM), BF16)],
        scratch_shapes=[pltpu.VMEM((HG_DIM, HG_DIM), F32), pltpu.VMEM((ct, HG_DIM), BF16),
                        pltpu.VMEM((ct, HG_DIM), BF16), pltpu.VMEM((ct, HG_DIM), F32),
                        pltpu.VMEM((ct, HG_DIM), F32), pltpu.VMEM((ct, HG_DIM), F32)],
        compiler_params=_params(2),
        name="hgrn_fwd",
    )(proj, proj, proj, proj, logits, gn, _block_masks(mrows))


def _sg_tile(t):
    return min(t, 512)


def _sgu_chunk_fwd(u, v, ln_g, ln_b, wm, bs):
    ua, dua = _gelu_and_grad(u)
    va, dva = _gelu_and_grad(v)
    vn, xhat, rstd = _ln_fwd(va, ln_g, ln_b)
    s = _dot(wm, vn.astype(BF16)) + bs
    return ua, dua, dva, vn, xhat, rstd, s


def _tril_weight(w):
    n = SG_CHUNK
    r = lax.broadcasted_iota(jnp.int32, (n, n), 0)
    c = lax.broadcasted_iota(jnp.int32, (n, n), 1)
    return jnp.where(c <= r, w, 0.0)


def _bias_by_position(b_row):
    return jnp.broadcast_to(b_row, (SG_CHUNK, SG_CHUNK)).T


def _sgu_fwd(proj, mix, ln_g, ln_b, w_s, b_row):
    t = proj.shape[1]
    ct = _sg_tile(t)
    ng = SG_GROUPS
    wide = ng * SG_DIM
    blk_u = 4 * HG_HEADS // ng

    def body(u_ref, v_ref, g_ref, b_ref, w_ref, bs_ref, mix_ref, o_ref):
        del mix_ref
        for g in range(ng):
            lanes = slice(g * SG_DIM, (g + 1) * SG_DIM)
            wm = _tril_weight(w_ref[g]).astype(BF16)
            bs = _bias_by_position(bs_ref[g])
            for n in range(ct // SG_CHUNK):
                rows = slice(n * SG_CHUNK, (n + 1) * SG_CHUNK)
                ua, _, _, _, _, _, s = _sgu_chunk_fwd(u_ref[g, rows, :], v_ref[g, rows, :], g_ref[g], b_ref[g], wm, bs)
                o_ref[rows, lanes] = (ua * s).astype(BF16)

    full = lambda a: pl.BlockSpec(a.shape, lambda c: (0,) * a.ndim)
    return pl.pallas_call(
        body,
        grid=(t // ct,),
        in_specs=[pl.BlockSpec((ng, ct, SG_DIM), lambda c: (blk_u, c, 0)),
                  pl.BlockSpec((ng, ct, SG_DIM), lambda c: (blk_u + 1, c, 0)),
                  full(ln_g), full(ln_b), full(w_s), full(b_row), ANY],
        out_specs=pl.BlockSpec((ct, wide), lambda c: (c, 1)),
        out_shape=jax.ShapeDtypeStruct(mix.shape, mix.dtype),
        input_output_aliases={6: 0},
        compiler_params=_params(1),
        name="sgu_fwd",
    )(proj, proj, ln_g, ln_b, w_s, b_row, mix)


def _mem_kv(mem, g, b, wk, wv):
    m_len, d = mem.shape

    def body(m_ref, g_ref, b_ref, wk_ref, wv_ref, mb_ref, xh_ref, rs_ref, k_ref, v_ref):
        m, xhat, rstd = _ln_fwd(m_ref[...], g_ref[...], b_ref[...])
        mb = m.astype(BF16)
        mb_ref[...] = mb
        xh_ref[...] = xhat
        rs_ref[...] = rstd
        k_ref[...] = _dot(mb, wk_ref[...]).astype(BF16)
        v_ref[...] = _dot(mb, wv_ref[...]).astype(BF16)

    return pl.pallas_call(
        body,
        out_shape=[jax.ShapeDtypeStruct((m_len, d), BF16), jax.ShapeDtypeStruct((m_len, d), F32),
                   jax.ShapeDtypeStruct((m_len, 1), F32), jax.ShapeDtypeStruct((m_len, d), BF16),
                   jax.ShapeDtypeStruct((m_len, d), BF16)],
        compiler_params=pltpu.CompilerParams(vmem_limit_bytes=VMEM_LIMIT_V7X),
        name="mem_kv",
    )(mem, g, b, wk, wv)


def _softmax_rows(s):
    m = jnp.max(s, axis=-1, keepdims=True)
    p = jnp.exp(s - m)
    return p / jnp.sum(p, axis=-1, keepdims=True)


def _attn_fwd(hb, wq, kb, vb):
    t, d = hb.shape
    tm = _row_tile(t)
    dh = d // X_HEADS
    scale = dh ** -0.5

    def body(h_ref, wq_ref, k_ref, v_ref, q_ref, o_ref):
        q = _dot(h_ref[...], wq_ref[...]).astype(BF16)
        q_ref[...] = q
        for hd in range(X_HEADS):
            sl = slice(hd * dh, (hd + 1) * dh)
            p = _softmax_rows(_dot_nt(q[:, sl], k_ref[:, sl]) * scale)
            o_ref[:, sl] = _dot(p.astype(BF16), v_ref[:, sl]).astype(BF16)

    row = pl.BlockSpec((tm, d), lambda i: (i, 0))
    full = lambda a: pl.BlockSpec(a.shape, lambda i: (0, 0))
    return pl.pallas_call(
        body,
        grid=(t // tm,),
        in_specs=[row, full(wq), full(kb), full(vb)],
        out_specs=[row, row],
        out_shape=[jax.ShapeDtypeStruct((t, d), BF16), jax.ShapeDtypeStruct((t, d), BF16)],
        compiler_params=_params(1),
        name="attn_fwd",
    )(hb, wq, kb, vb)


def _ffn_bwd_act(dyb, wd, a, b, coef, name, deps=()):
    t, d = dyb.shape
    f = wd.shape[0]
    tm = _row_tile(t)
    tn = _col_tile(f)

    def body(dy_ref, wd_ref, a_ref, b_ref, da_ref, db_ref):
        dy = dy_ref[...]
        for c in range(f // tn):
            cols = slice(c * tn, (c + 1) * tn)
            ds = _dot_nt(dy, wd_ref[cols, :]) * coef
            silu, dsilu = _silu_and_grad(a_ref[:, cols].astype(F32))
            da_ref[:, cols] = (ds * b_ref[:, cols].astype(F32) * dsilu).astype(BF16)
            db_ref[:, cols] = (ds * silu).astype(BF16)

    act = pl.BlockSpec((tm, f), lambda i: (i, 0))
    return pl.pallas_call(
        _drop_deps(body, 4, len(deps)),
        grid=(t // tm,),
        in_specs=[pl.BlockSpec((tm, d), lambda i: (i, 0)), _resident(wd), act, act] + [ANY] * len(deps),
        out_specs=[act, act],
        out_shape=[jax.ShapeDtypeStruct((t, f), BF16), jax.ShapeDtypeStruct((t, f), BF16)],
        compiler_params=_params(1),
        name=name,
    )(dyb, wd, a, b, *deps)


def _ffn_bwd_fused(dy, dyb, wd, wg, wu, a, b, coef, ln, name):
    t, d = dy.shape
    f = wd.shape[0]
    tm = min(t, 256)
    tn = _col_tile(f)

    def body(dy_ref, dyb_ref, wd_ref, wg_ref, wu_ref, a_ref, b_ref, xh_ref, rs_ref, g_ref,
             da_ref, db_ref, dyo_ref, dyob_ref, dg_ref, dbl_ref):
        dyb_v = dyb_ref[...]
        dh = ALPHA * dy_ref[...]
        for c in range(f // tn):
            cols = slice(c * tn, (c + 1) * tn)
            ds = _dot_nt(dyb_v, wd_ref[cols, :]) * coef
            silu, dsilu = _silu_and_grad(a_ref[:, cols].astype(F32))
            da = (ds * b_ref[:, cols].astype(F32) * dsilu).astype(BF16)
            db = (ds * silu).astype(BF16)
            da_ref[:, cols] = da
            db_ref[:, cols] = db
            dh = dh + _dot_nt(da, wg_ref[:, cols]) + _dot_nt(db, wu_ref[:, cols])

        @pl.when(pl.program_id(0) == 0)
        def _():
            dg_ref[...] = jnp.zeros_like(dg_ref)
            dbl_ref[...] = jnp.zeros_like(dbl_ref)

        dyp, dg, dbl = _ln_bwd(dh, xh_ref[...], rs_ref[...], g_ref[...])
        dyo_ref[...] = dyp
        dyob_ref[...] = dyp.astype(BF16)
        dg_ref[...] += dg
        dbl_ref[...] += dbl

    row = pl.BlockSpec((tm, d), lambda i: (i, 0))
    act = pl.BlockSpec((tm, f), lambda i: (i, 0))
    vec = pl.BlockSpec((1, d), lambda i: (0, 0))
    return pl.pallas_call(
        body,
        grid=(t // tm,),
        in_specs=[row, row, _resident(wd), _resident(wg), _resident(wu), act, act, row,
                  pl.BlockSpec((tm, 1), lambda i: (i, 0)), vec],
        out_specs=[act, act, row, row, vec, vec],
        out_shape=[jax.ShapeDtypeStruct((t, f), BF16), jax.ShapeDtypeStruct((t, f), BF16),
                   jax.ShapeDtypeStruct((t, d), F32), jax.ShapeDtypeStruct((t, d), BF16),
                   jax.ShapeDtypeStruct((1, d), F32), jax.ShapeDtypeStruct((1, d), F32)],
        compiler_params=_params(1),
        name=name,
    )(dy, dyb, wd, wg, wu, a, b, *ln)


def _mm_tn(a, b, name, scale=1.0, deps=()):
    t, m = a.shape
    bs = list(b) if isinstance(b, (list, tuple)) else [b]
    n = sum(piece.shape[1] for piece in bs)
    tt = _token_tile(t)
    nt = t // tt
    tm_o, tn_o = m, n

    def body(a_ref, *refs):
        b_refs, (o_ref, acc) = refs[:len(bs)], refs[len(bs):]
        k = pl.program_id(2)

        @pl.when(k == 0)
        def _():
            acc[...] = jnp.zeros_like(acc)

        first = 0
        for b_ref in b_refs:
            cols = slice(first, first + b_ref.shape[1])
            acc[:, cols] += _dot_tn(a_ref[...], b_ref[...])
            first = cols.stop

        @pl.when(k == nt - 1)
        def _():
            o_ref[...] = (acc[...] * scale).astype(BF16)

    return pl.pallas_call(
        _drop_deps(body, 1 + len(bs), len(deps)),
        grid=(m // tm_o, n // tn_o, nt),
        in_specs=[pl.BlockSpec((tt, tm_o), lambda i, j, k: (k, i))]
        + [pl.BlockSpec((tt, piece.shape[1]), lambda i, j, k: (k, j)) for piece in bs] + [ANY] * len(deps),
        out_specs=pl.BlockSpec((tm_o, tn_o), lambda i, j, k: (i, j)),
        out_shape=pltpu.HBM((m, n), BF16),
        scratch_shapes=[pltpu.VMEM((tm_o, tn_o), F32)],
        compiler_params=_params(3),
        name=name,
    )(a, *bs, *deps)


def _mm_nt(lhs, w, name):
    t, d = lhs.shape
    kd = w.shape[0]
    tm = _row_tile(t)

    def body(l_ref, w_ref, o_ref):
        _store_slabs(o_ref, 0, _dot_nt(l_ref[...], w_ref[...]))

    return pl.pallas_call(
        body,
        grid=(t // tm,),
        in_specs=[pl.BlockSpec((tm, d), lambda i: (i, 0)), _resident(w)],
        out_specs=pl.BlockSpec((kd // LANES, tm, LANES), lambda i: (0, i, 0)),
        out_shape=jax.ShapeDtypeStruct((kd // LANES, t, LANES), F32),
        compiler_params=_params(1),
        name=name,
    )(lhs, w)


def _dx_ln(dy, pairs, ln, name, deps=()):
    t, d = dy.shape
    npair = len(pairs)
    pairs = [(list(lhs) if isinstance(lhs, (list, tuple)) else [lhs], w) for lhs, w in pairs]
    tm = min(t, 512 // npair)
    nt = t // tm
    n_in = 1 + sum(len(pieces) + 1 for pieces, _ in pairs) + (3 if ln is not None else 0)

    def body(*refs):
        dy_ref = refs[0]
        pos = 1
        dh = ALPHA * dy_ref[...]
        for pieces, _ in pairs:
            w_ref = refs[pos + len(pieces)]
            first = 0
            for l_ref in refs[pos:pos + len(pieces)]:
                cols = slice(first, first + l_ref.shape[1])
                dh = dh + _dot_nt(l_ref[...], w_ref[:, cols])
                first = cols.stop
            pos += len(pieces) + 1
        if ln is not None:
            xh_ref, rs_ref, g_ref = refs[pos:pos + 3]
            dyo_ref, dyb_ref, dg_ref, db_ref = refs[pos + 3:pos + 7]

            @pl.when(pl.program_id(0) == 0)
            def _():
                dg_ref[...] = jnp.zeros_like(dg_ref)
                db_ref[...] = jnp.zeros_like(db_ref)

            dyp, dg, db = _ln_bwd(dh, xh_ref[...], rs_ref[...], g_ref[...])
            dyo_ref[...] = dyp
            dyb_ref[...] = dyp.astype(BF16)
            dg_ref[...] += dg
            db_ref[...] += db
        else:
            refs[pos][...] = dh

    row = pl.BlockSpec((tm, d), lambda i: (i, 0))
    vec = pl.BlockSpec((1, d), lambda i: (0, 0))
    in_specs = [row]
    args = [dy]
    for pieces, w in pairs:
        in_specs += [pl.BlockSpec((tm, piece.shape[1]), lambda i: (i, 0)) for piece in pieces] + [_resident(w)]
        args += pieces + [w]
    if ln is not None:
        in_specs += [row, pl.BlockSpec((tm, 1), lambda i: (i, 0)), vec]
        args += list(ln)
        out_specs = [row, row, vec, vec]
        out_shape = [jax.ShapeDtypeStruct((t, d), F32), jax.ShapeDtypeStruct((t, d), BF16),
                     jax.ShapeDtypeStruct((1, d), F32), jax.ShapeDtypeStruct((1, d), F32)]
    else:
        out_specs = row
        out_shape = jax.ShapeDtypeStruct((t, d), F32)
    return pl.pallas_call(
        _drop_deps(body, n_in, len(deps)),
        grid=(nt,),
        in_specs=in_specs + [ANY] * len(deps),
        out_specs=out_specs,
        out_shape=out_shape,
        compiler_params=_params(1),
        name=name,
    )(*args, *deps)


def _hgrn_bwd(proj, oraw, dmix, states, logits, gn):
    t = proj.shape[1]
    ct = _hg_tile(t)
    nct = t // ct
    nblk = ct // HG_BLOCK
    nh = HG_HEADS
    mrows = min(ct, 256)

    def body(q_ref, fz_ref, iv_ref, gg_ref, or_ref, do_ref, st_ref, lg_ref, gn_ref, mask_ref,
             dq_ref, dfz_ref, div_ref, dgg_ref, dlg_ref, dgn_ref,
             dstate, qt_s, kt_s, k_s, b_s, eb_s, ekb_s, dec_s, dor_s, dbl_s, gr_s, dk_s, dlb_acc):
        c = pl.program_id(1)

        @pl.when(c == 0)
        def _():
            dstate[...] = jnp.zeros_like(dstate)
            dlb_acc[...] = jnp.zeros_like(dlb_acc)
            dgn_ref[...] = jnp.zeros_like(dgn_ref)

        lb = _lower_bound(lg_ref[...])
        q = q_ref[...]
        sig, nsig, f, k = _forget_terms(fz_ref[...], lb)
        logf = jnp.log(f)
        b = _mask_dot(mask_ref[0], logf)
        bend = _mask_dot(mask_ref[2], logf)
        eb = jnp.exp(b)
        ekb = jnp.exp(bend - b)
        qt_s[...] = (q * eb).astype(BF16)
        kt_s[...] = (k * ekb).astype(BF16)
        k_s[...] = k
        b_s[...] = b
        eb_s[...] = eb
        ekb_s[...] = ekb
        dec_s[...] = jnp.exp(bend)
        oraw = or_ref[...]
        r = lax.rsqrt(jnp.mean(oraw * oraw, axis=-1, keepdims=True) + LN_EPS)
        on = oraw * r
        gg = gg_ref[...]
        silu, dsilu = _silu_and_grad(gg)
        doa = do_ref[...]
        gnv = gn_ref[...]
        dgg_ref[...] = (doa * on * gnv * dsilu).astype(BF16)
        dyn = doa * silu
        dgn_ref[...] += jnp.sum(dyn * on, axis=0, keepdims=True)
        don = dyn * gnv
        dor_s[...] = r * (don - on * jnp.mean(don * on, axis=-1, keepdims=True))
        tidx = lax.broadcasted_iota(jnp.int32, (HG_HALF, HG_DIM), 0)

        def blk(ii, carry):
            i = nblk - 1 - ii
            r0 = pl.multiple_of(i * HG_BLOCK, HG_BLOCK)
            rows = pl.ds(r0, HG_BLOCK)
            st = st_ref[i]
            dst = dstate[...]
            dstb = dst.astype(BF16)
            do = dor_s[rows, :]
            dob = do.astype(BF16)
            v = iv_ref[rows, :]
            vb = v.astype(BF16)
            qq = q_ref[rows, :]
            kk = k_s[rows, :]
            bb = b_s[rows, :]
            qt = qt_s[rows, :]
            kt = kt_s[rows, :]
            dec = dec_s[pl.ds(r0, 1), :]
            dkt = _dot(vb, dstb)
            dq = _dot(dob, st) * eb_s[rows, :]
            dk = dkt * ekb_s[rows, :]
            dv = _dot_nt(kt, dstb)
            gend = (jnp.sum(kk * dk, axis=0, keepdims=True)
                    + dec * jnp.sum(dst * st.astype(F32), axis=0, keepdims=True))
            qh, bh, doh = _halves(qq), _halves(bb), _halves(do)
            dqh, dkh, dvh = list(_halves(dq)), list(_halves(dk)), list(_halves(dv))
            for s in range(HG_BLOCK):
                ks, vs = kk[s:s + 1, :], v[s:s + 1, :]
                dk_part = dv_part = None
                for h in _causal_halves(s):
                    e = _decay_from(bh[h], bb[s:s + 1, :], s, h, tidx)
                    ke = ks * e
                    acol = jnp.sum(qh[h] * ke, axis=1, keepdims=True)
                    dacol = jnp.sum(doh[h] * vs, axis=1, keepdims=True)
                    dqh[h] = dqh[h] + dacol * ke
                    pk = dacol * (qh[h] * e)
                    pv = acol * doh[h]
                    dk_part = pk if dk_part is None else dk_part + pk
                    dv_part = pv if dv_part is None else dv_part + pv
                hs, row = divmod(s, HG_HALF)
                dkh[hs] = dkh[hs] + jnp.where(tidx == row, jnp.sum(dk_part, axis=0, keepdims=True), 0.0)
                dvh[hs] = dvh[hs] + jnp.where(tidx == row, jnp.sum(dv_part, axis=0, keepdims=True), 0.0)
            dq = jnp.concatenate(dqh, axis=0)
            dk = jnp.concatenate(dkh, axis=0)
            dv = jnp.concatenate(dvh, axis=0)
            dq_ref[rows, :] = dq.astype(BF16)
            div_ref[rows, :] = dv.astype(BF16)
            dk_s[rows, :] = dk
            dbl_s[rows, :] = qq * dq - kk * dk
            gr_s[rows, :] = jnp.zeros((HG_BLOCK, HG_DIM), F32) + gend
            dstate[...] = dst * dec + _dot_tn(dob, qt)
            return carry

        lax.fori_loop(0, nblk, blk, 0, unroll=HG_UNROLL)
        dlogf = _mask_dot(mask_ref[1], dbl_s[...]) + gr_s[...]
        dk = dk_s[...]
        dfz_ref[...] = ((dlogf / f - dk) * ((1.0 - lb) * sig * nsig)).astype(BF16)
        dlb_acc[...] += jnp.sum((dlogf / f - dk) * nsig, axis=0, keepdims=True)

        @pl.when(c == nct - 1)
        def _():
            dl0 = dlb_acc[...] * lb * (1.0 - lb)
            layer = lax.broadcasted_iota(jnp.int32, (2, HG_DIM), 0)
            dlg_ref[...] = jnp.where(layer == 0, dl0, -dl0)

    def slab(off):
        return pl.BlockSpec((None, ct, HG_DIM), lambda h, c: (off + h, nct - 1 - c, 0))

    out_slab = pl.BlockSpec((ct, HG_DIM), lambda h, c: (nct - 1 - c, h))
    tile_f32 = pltpu.VMEM((ct, HG_DIM), F32)
    tile_b16 = pltpu.VMEM((ct, HG_DIM), BF16)
    slab_shape = pltpu.HBM((t, nh * HG_DIM), BF16)
    return pl.pallas_call(
        body,
        grid=(nh, nct),
        in_specs=[slab(0), slab(nh), slab(2 * nh), slab(3 * nh), slab(0), slab(0),
                  pl.BlockSpec((None, nblk, HG_DIM, HG_DIM), lambda h, c: (h, nct - 1 - c, 0, 0)),
                  pl.BlockSpec((None, 2, HG_DIM), lambda h, c: (h, 0, 0)),
                  pl.BlockSpec((1, HG_DIM), lambda h, c: (0, 0)),
                  pl.BlockSpec((3, mrows, mrows), lambda h, c: (0, 0, 0))],
        out_specs=[out_slab, out_slab, out_slab, out_slab,
                   pl.BlockSpec((None, 2, HG_DIM), lambda h, c: (h, 0, 0)),
                   pl.BlockSpec((None, 1, HG_DIM), lambda h, c: (h, 0, 0))],
        out_shape=[slab_shape, slab_shape, slab_shape, slab_shape,
                   jax.ShapeDtypeStruct((nh, 2, HG_DIM), F32), jax.ShapeDtypeStruct((nh, 1, HG_DIM), F32)],
        scratch_shapes=[pltpu.VMEM((HG_DIM, HG_DIM), F32), tile_b16, tile_b16, tile_f32, tile_f32, tile_f32, tile_f32,
                        tile_f32, tile_f32, tile_f32, tile_f32, tile_f32, pltpu.VMEM((1, HG_DIM), F32)],
        compiler_params=_params(2),
        name="hgrn_bwd",
    )(proj, proj, proj, proj, oraw, dmix, states, logits, gn, _block_masks(mrows))


def _sgu_bwd(proj, dmix, ln_g, ln_b, w_s, b_row):
    t = proj.shape[1]
    ct = _sg_tile(t)
    nct = t // ct
    ng = SG_GROUPS
    off_u = 4 * HG_HEADS
    off_v = off_u + ng
    n = SG_CHUNK

    def body(u_ref, v_ref, do_ref, g_ref, b_ref, w_ref, bs_ref, du_ref, dv_ref, dg_ref, db_ref, dw_ref, dbs_ref):
        c = pl.program_id(1)

        @pl.when(c == 0)
        def _():
            dg_ref[...] = jnp.zeros_like(dg_ref)
            db_ref[...] = jnp.zeros_like(db_ref)
            dw_ref[...] = jnp.zeros_like(dw_ref)
            dbs_ref[...] = jnp.zeros_like(dbs_ref)

        r = lax.broadcasted_iota(jnp.int32, (n, n), 0)
        cc = lax.broadcasted_iota(jnp.int32, (n, n), 1)
        wm = jnp.where(cc <= r, w_ref[...], 0.0).astype(BF16)
        wmt = jnp.where(r <= cc, w_ref[...].T, 0.0).astype(BF16)
        bs = _bias_by_position(bs_ref[...])
        for ci in range(ct // n):
            rows = slice(ci * n, (ci + 1) * n)
            ua, dua, dva, vn, xhat, rstd, s = _sgu_chunk_fwd(u_ref[rows, :], v_ref[rows, :], g_ref[...], b_ref[...],
                                                             wm, bs)
            do = do_ref[rows, :]
            du_ref[rows, :] = (do * s * dua).astype(BF16)
            ds = do * ua
            dsb = ds.astype(BF16)
            dbs_ref[...] += jnp.sum(ds, axis=1, keepdims=True)
            dw_ref[...] += _dot_nt(dsb, vn.astype(BF16))
            dvn = _dot(wmt, dsb)
            dva_in, dg, db = _ln_bwd(dvn, xhat, rstd, g_ref[...])
            dg_ref[...] += dg
            db_ref[...] += db
            dv_ref[rows, :] = (dva_in * dva).astype(BF16)

        @pl.when(c == nct - 1)
        def _():
            dw_ref[...] = jnp.where(cc <= r, dw_ref[...], 0.0)

    vec = pl.BlockSpec((None, 1, SG_DIM), lambda g, c: (g, 0, 0))
    mat = pl.BlockSpec((None, n, n), lambda g, c: (g, 0, 0))
    col = pl.BlockSpec((None, n, 1), lambda g, c: (g, 0, 0))
    out_slab = pl.BlockSpec((ct, SG_DIM), lambda g, c: (c, g))
    return pl.pallas_call(
        body,
        grid=(ng, nct),
        in_specs=[pl.BlockSpec((None, ct, SG_DIM), lambda g, c: (off_u + g, c, 0)),
                  pl.BlockSpec((None, ct, SG_DIM), lambda g, c: (off_v + g, c, 0)),
                  pl.BlockSpec((None, ct, SG_DIM), lambda g, c: (ng + g, c, 0)), vec, vec, mat, vec],
        out_specs=[out_slab, out_slab, vec, vec, mat, col],
        out_shape=[pltpu.HBM((t, ng * SG_DIM), BF16), pltpu.HBM((t, ng * SG_DIM), BF16),
                   jax.ShapeDtypeStruct((ng, 1, SG_DIM), F32), jax.ShapeDtypeStruct((ng, 1, SG_DIM), F32),
                   jax.ShapeDtypeStruct((ng, n, n), F32), jax.ShapeDtypeStruct((ng, n, 1), F32)],
        compiler_params=_params(2),
        name="sgu_bwd",
    )(proj, proj, dmix, ln_g, ln_b, w_s, b_row)


def _attn_bwd(dyb, wo, qb, kb, vb):
    t, d = dyb.shape
    m_len = kb.shape[0]
    tm = _row_tile(t)
    dh = d // X_HEADS
    scale = dh ** -0.5

    def body(dy_ref, wo_ref, q_ref, k_ref, v_ref, dq_ref, dk_ref, dv_ref):
        i = pl.program_id(0)

        @pl.when(i == 0)
        def _():
            dk_ref[...] = jnp.zeros_like(dk_ref)
            dv_ref[...] = jnp.zeros_like(dv_ref)

        do = _dot_nt(dy_ref[...], wo_ref[...]).astype(BF16)
        for hd in range(X_HEADS):
            sl = slice(hd * dh, (hd + 1) * dh)
            qh = q_ref[:, sl]
            p = _softmax_rows(_dot_nt(qh, k_ref[:, sl]) * scale)
            doh = do[:, sl]
            dp = _dot_nt(doh, v_ref[:, sl])
            ds = (p * (dp - jnp.sum(dp * p, axis=-1, keepdims=True)) * scale).astype(BF16)
            dq_ref[:, sl] = _dot(ds, k_ref[:, sl]).astype(BF16)
            dk_ref[:, sl] += _dot_tn(ds, qh)
            dv_ref[:, sl] += _dot_tn(p.astype(BF16), doh)

    row = pl.BlockSpec((tm, d), lambda i: (i, 0))
    full = lambda a: pl.BlockSpec(a.shape, lambda i: (0, 0))
    kv = pl.BlockSpec((m_len, d), lambda i: (0, 0))
    return pl.pallas_call(
        body,
        grid=(t // tm,),
        in_specs=[row, full(wo), row, full(kb), full(vb)],
        out_specs=[row, kv, kv],
        out_shape=[jax.ShapeDtypeStruct((t, d), BF16), jax.ShapeDtypeStruct((m_len, d), F32),
                   jax.ShapeDtypeStruct((m_len, d), F32)],
        compiler_params=_params(1),
        name="attn_bwd",
    )(dyb, wo, qb, kb, vb)


def _mem_bwd(dk, dv, mb, xhat, rstd, g, wk, wv):
    m_len, d = dk.shape

    def body(dk_ref, dv_ref, mb_ref, xh_ref, rs_ref, g_ref, wk_ref, wv_ref, gwk_ref, gwv_ref, dg_ref, db_ref):
        dkb = dk_ref[...].astype(BF16)
        dvb = dv_ref[...].astype(BF16)
        mb_v = mb_ref[...]
        gwk_ref[...] = _dot_tn(mb_v, dkb).astype(BF16)
        gwv_ref[...] = _dot_tn(mb_v, dvb).astype(BF16)
        dm = _dot_nt(dkb, wk_ref[...]) + _dot_nt(dvb, wv_ref[...])
        _, dg, db = _ln_bwd(dm, xh_ref[...], rs_ref[...], g_ref[...])
        dg_ref[...] = dg
        db_ref[...] = db

    return pl.pallas_call(
        body,
        out_shape=[jax.ShapeDtypeStruct((d, d), BF16), jax.ShapeDtypeStruct((d, d), BF16),
                   jax.ShapeDtypeStruct((1, d), F32), jax.ShapeDtypeStruct((1, d), F32)],
        compiler_params=pltpu.CompilerParams(vmem_limit_bytes=VMEM_LIMIT_V7X),
        name="mem_bwd",
    )(dk, dv, mb, xhat, rstd, g, wk, wv)


def _adamw(w, g, m, v):
    m = ADAM_B1 * m + (1.0 - ADAM_B1) * g
    v = ADAM_B2 * v + (1.0 - ADAM_B2) * (g * g)
    m_hat = m / (1.0 - ADAM_B1 ** ADAM_STEP)
    v_hat = v / (1.0 - ADAM_B2 ** ADAM_STEP)
    delta = -ADAM_LR * (m_hat / (jnp.sqrt(v_hat) + ADAM_EPS) + ADAM_WD * w)
    return delta, m, v


def _slot_sum(ref):
    g = ref[0].astype(F32)
    for s in range(1, N_DEV):
        g = g + ref[s].astype(F32)
    return g


def _adam_sharded(lands, w, m, v, axis, name):
    rows, cols = w.shape
    nl = len(lands)
    transposed = axis == 1 and nl == 2
    if transposed:
        rows, cols = cols, rows
        tr = 256
        grid = (rows // tr,)
        wblk = pl.BlockSpec((cols, tr), lambda i: (0, i))
        lblk = [pl.BlockSpec((N_DEV, tr, a.shape[2]), lambda i: (0, i, 0)) for a in lands]
    elif axis == 1:
        tr = 256 if rows % 256 == 0 else rows
        grid = (rows // tr,)
        wblk = pl.BlockSpec((tr, cols), lambda i: (i, 0))
        lblk = [pl.BlockSpec((N_DEV, tr, a.shape[2]), lambda i: (0, i, 0)) for a in lands]
    else:
        tc = _col_tile(cols)
        grid = (cols // tc,)
        wblk = pl.BlockSpec((rows, tc), lambda i: (0, i))
        lblk = [pl.BlockSpec((N_DEV, a.shape[1], tc), lambda i: (0, 0, i)) for a in lands]

    def body(*refs):
        w_ref, m_ref, v_ref = refs[nl:nl + 3]
        g_ref, d_ref, nm_ref, nv_ref = refs[nl + 3:]
        g = _slot_sum(refs[0])
        if nl == 2:
            tail = _slot_sum(refs[1])
            if transposed:
                g = jnp.concatenate([g.T, tail.T[:cols - g.shape[1], :]], axis=0)
            elif axis == 1:
                g = jnp.concatenate([g, tail[:, :cols - g.shape[1]]], axis=1)
            else:
                g = jnp.concatenate([g, tail[:rows - g.shape[0], :]], axis=0)
        delta, nm, nv = _adamw(w_ref[...], g, m_ref[...], v_ref[...])
        g_ref[...] = g
        d_ref[...] = delta
        nm_ref[...] = nm
        nv_ref[...] = nv

    shp = pltpu.HBM(w.shape, F32)
    return pl.pallas_call(
        body,
        grid=grid,
        in_specs=lblk + [wblk, wblk, wblk],
        out_specs=[wblk, wblk, wblk, wblk],
        out_shape=[shp, shp, shp, shp],
        compiler_params=_params(1),
        name=name,
    )(*[pltpu.with_memory_space_constraint(a, pltpu.HBM) for a in (*lands, w, m, v)])


def _mesh_pos():
    return lax.axis_index("x"), lax.axis_index("y"), lax.axis_index("c")


def _peer(k):
    x, y, c = _mesh_pos()
    pos = (x ^ (k >> 2), y ^ ((k >> 1) & 1), c ^ (k & 1))
    return pos, 4 * pos[0] + 2 * pos[1] + pos[2]


def _sem_index(row, k):
    return row * (N_DEV - 1) + k - 1


def _window(ref, axis, start, size):
    align = 16 if axis == 0 else LANES
    start = pl.multiple_of(start, align)
    return ref.at[pl.ds(start, size), :] if axis == 0 else ref.at[:, pl.ds(start, size)]


def _piece_refs(piece, srcs, lands, me, peer):
    kind, si, li, axis, base, stride, shape = piece
    if kind == "gather":
        return srcs[si], _window(lands[li], axis, base + stride * me, shape[axis])
    return _window(srcs[si], axis, base + stride * peer, shape[axis]), lands[li].at[me]


def _place_own(srcs, land_shapes, pieces, name):
    ns, nl, npc = len(srcs), len(land_shapes), len(pieces)

    def body(*refs):
        s_refs = refs[:ns]
        l_refs = refs[ns:ns + nl]
        bufs = refs[ns + nl:ns + nl + npc]
        sems = refs[ns + nl + npc]
        x, y, c = _mesh_pos()
        me = 4 * x + 2 * y + c
        loads = []
        for p, piece in enumerate(pieces):
            src, dst = _piece_refs(piece, s_refs, l_refs, me, me)
            cp = pltpu.make_async_copy(src, bufs[p], sems.at[0, p])
            cp.start()
            loads.append((cp, dst))
        stores = []
        for p, (cp, dst) in enumerate(loads):
            cp.wait()
            out = pltpu.make_async_copy(bufs[p], dst, sems.at[1, p])
            out.start()
            stores.append(out)
        for out in stores:
            out.wait()

    out = pl.pallas_call(
        body,
        in_specs=[ANY] * ns,
        out_specs=[HBM] * nl,
        out_shape=[pltpu.HBM(s.shape, s.dtype) for s in land_shapes],
        scratch_shapes=[pltpu.VMEM(pc[6], srcs[pc[1]].dtype) for pc in pieces] + [pltpu.SemaphoreType.DMA((2, npc))],
        compiler_params=pltpu.CompilerParams(vmem_limit_bytes=VMEM_LIMIT_V7X),
        name=name,
    )(*srcs)
    return list(out)


def _comm_start(srcs, lands, pieces, groups, name, after=()):
    ns, nl, na, ng = len(srcs), len(lands), len(after), len(groups)

    def body(*refs):
        s_refs = refs[:ns]
        l_refs = refs[ns:ns + nl]
        outs = refs[ns + nl + na:]
        sems = outs[:2 * ng]
        token = outs[-1]
        x, y, c = _mesh_pos()
        me = 4 * x + 2 * y + c
        for g, members in enumerate(groups):
            for row, p in enumerate(members):
                for k in range(1, N_DEV):
                    pos, peer = _peer(k)
                    src, dst = _piece_refs(pieces[p], s_refs, l_refs, me, peer)
                    pltpu.make_async_remote_copy(src_ref=src, dst_ref=dst, send_sem=sems[2 * g].at[_sem_index(row, k)],
                                                 recv_sem=sems[2 * g + 1].at[_sem_index(row, k)], device_id=pos,
                                                 device_id_type=MESH_ID).start()
        token[...] = jnp.zeros_like(token)

    sem_shapes = []
    for members in groups:
        sem_shapes += [pltpu.SemaphoreType.DMA((len(members) * (N_DEV - 1),))] * 2
    hbm_of = lambda a: pltpu.HBM(a.shape, a.dtype)
    out = pl.pallas_call(
        body,
        in_specs=[HBM] * (ns + nl) + [ANY] * na,
        out_specs=[SEM] * (2 * ng) + [HBM] * (ns + nl) + [pl.BlockSpec(memory_space=pltpu.VMEM)],
        out_shape=sem_shapes + [hbm_of(a) for a in srcs] + [hbm_of(a) for a in lands]
        + [jax.ShapeDtypeStruct((8, LANES), F32)],
        input_output_aliases={i: 2 * ng + i for i in range(ns + nl)},
        compiler_params=pltpu.CompilerParams(has_side_effects=DATAFLOW),
        name=name,
    )(*[pltpu.with_memory_space_constraint(a, pltpu.HBM) for a in list(srcs) + list(lands)], *after)
    sems = [(out[2 * g], out[2 * g + 1]) for g in range(ng)]
    return sems, list(out[2 * ng:2 * ng + ns]), list(out[2 * ng + ns:2 * ng + ns + nl]), out[-1]


def _comm_wait(srcs, lands, pieces, members, sems, after, name):
    ns, nl, na = len(srcs), len(lands), len(after)

    def body(*refs):
        s_refs = refs[:ns]
        l_refs = refs[ns:ns + nl]
        send_sems, recv_sems = refs[ns + nl:ns + nl + 2]
        x, y, c = _mesh_pos()
        me = 4 * x + 2 * y + c
        for row, p in enumerate(members):
            for k in range(1, N_DEV):
                pos, peer = _peer(k)
                src, dst = _piece_refs(pieces[p], s_refs, l_refs, me, peer)
                cp = pltpu.make_async_remote_copy(src_ref=src, dst_ref=dst, send_sem=send_sems.at[_sem_index(row, k)],
                                                  recv_sem=recv_sems.at[_sem_index(row, k)], device_id=pos,
                                                  device_id_type=MESH_ID)
                cp.wait_send()
                cp.wait_recv()

    hbm_of = lambda a: pltpu.HBM(a.shape, a.dtype)
    out = pl.pallas_call(
        body,
        in_specs=[HBM] * (ns + nl) + [SEM, SEM] + [ANY] * na,
        out_specs=[HBM] * (ns + nl),
        out_shape=[hbm_of(a) for a in srcs] + [hbm_of(a) for a in lands],
        input_output_aliases={i: i for i in range(ns + nl)},
        compiler_params=pltpu.CompilerParams(has_side_effects=DATAFLOW),
        name=name,
    )(*srcs, *lands, sems[0], sems[1], *after)
    return list(out[ns:])


def _landed_block(piece, lands, owner):
    _, _, li, axis, base, stride, shape = piece
    return _window(lands[li], axis, base + stride * owner, shape[axis])


def _copy_stage(name, bufs, in_sems, out_sem_sizes, emit, after=()):
    nb, ni, no, na = len(bufs), len(in_sems), len(out_sem_sizes), len(after)

    def body(*refs):
        b_refs = refs[:nb]
        i_refs = refs[nb:nb + ni]
        o_refs = refs[nb + ni + na:nb + ni + na + no]
        emit(b_refs, i_refs, o_refs)
        refs[-1][...] = jnp.zeros_like(refs[-1])

    hbm_of = lambda a: pltpu.HBM(a.shape, a.dtype)
    out = pl.pallas_call(
        body,
        in_specs=[HBM] * nb + [SEM] * ni + [ANY] * na,
        out_specs=[SEM] * no + [HBM] * nb + [pl.BlockSpec(memory_space=pltpu.VMEM)],
        out_shape=[pltpu.SemaphoreType.DMA((n,)) for n in out_sem_sizes] + [hbm_of(a) for a in bufs]
        + [jax.ShapeDtypeStruct((8, LANES), F32)],
        input_output_aliases={i: no + i for i in range(nb)},
        compiler_params=pltpu.CompilerParams(has_side_effects=DATAFLOW),
        name=name,
    )(*[pltpu.with_memory_space_constraint(a, pltpu.HBM) for a in bufs], *in_sems, *after)
    return list(out[:no]), list(out[no:no + nb]), out[-1]


def _remote(src, dst, send, recv, to):
    return pltpu.make_async_remote_copy(src_ref=src, dst_ref=dst, send_sem=send, recv_sem=recv, device_id=to,
                                        device_id_type=MESH_ID)


def _routed_gather(srcs, lands, pieces, meanwhile, name):
    ns, npc = len(srcs), len(pieces)

    def places():
        x, y, c = _mesh_pos()
        index = lambda p: 4 * p[0] + 2 * p[1] + p[2]
        me, sib = (x, y, c), (x, y, 1 - c)
        xnb, ynb = (1 - x, y, c), (x, 1 - y, c)
        got_first = (x ^ (1 - c), y ^ c, c)
        pass_to = (x ^ c, y ^ (1 - c), c)
        diag = (1 - x, 1 - y, c)
        return index, me, sib, xnb, ynb, got_first, pass_to, diag

    def start(b, _, o):
        index, me, sib, xnb, ynb, *_rest = places()
        send_a, recv_sib, recv_nb = o
        for p, piece in enumerate(pieces):
            src, dst = _piece_refs(piece, b[:ns], b[ns:], index(me), 0)
            _remote(src, dst, send_a.at[3 * p], recv_sib.at[p], sib).start()
            _remote(src, dst, send_a.at[3 * p + 1], recv_nb.at[2 * p], xnb).start()
            _remote(src, dst, send_a.at[3 * p + 2], recv_nb.at[2 * p + 1], ynb).start()

    def pass_a(b, i, o):
        index, me, sib, xnb, ynb, got_first, pass_to, _diag = places()
        (recv_nb,) = i
        send_f, recv_f, send_d, recv_d = o
        x, y, c = me
        for p, piece in enumerate(pieces):
            for turn in range(2):
                j = c if turn == 0 else 1 - c
                blk = _landed_block(piece, b, index((x ^ (1 - j), y ^ j, c)))
                _remote(blk, blk, send_f.at[2 * p + j], recv_nb.at[2 * p + j], sib).wait_recv()
                if turn == 0:
                    _remote(blk, blk, send_d.at[p], recv_d.at[p], pass_to).start()
                _remote(blk, blk, send_f.at[2 * p + j], recv_f.at[2 * p + j], sib).start()

    def pass_b(b, i, o):
        index, me, sib, *_mid, diag = places()
        (recv_d,) = i
        send_g, recv_g = o
        for p, piece in enumerate(pieces):
            blk = _landed_block(piece, b, index(diag))
            _remote(blk, blk, send_g.at[p], recv_d.at[p], sib).wait_recv()
            _remote(blk, blk, send_g.at[p], recv_g.at[p], sib).start()

    def last(b, i, _):
        index, me, sib, *_others = places()
        send_a, recv_sib, send_f, recv_f, send_d, send_g, recv_g = i
        for p, piece in enumerate(pieces):
            src, dst = _piece_refs(piece, b[:ns], b[ns:], index(me), 0)
            cp = lambda s_sem, r_sem: _remote(src, dst, s_sem, r_sem, sib)
            cp(send_a.at[3 * p], recv_sib.at[p]).wait_recv()
            cp(send_a.at[3 * p], recv_g.at[p]).wait_recv()
            for j in range(3):
                cp(send_a.at[3 * p + j], recv_sib.at[p]).wait_send()
            for j in range(2):
                cp(send_f.at[2 * p + j], recv_f.at[2 * p + j]).wait_recv()
                cp(send_f.at[2 * p + j], recv_f.at[2 * p + j]).wait_send()
            cp(send_d.at[p], recv_sib.at[p]).wait_send()
            cp(send_g.at[p], recv_sib.at[p]).wait_send()

    (send_a, recv_sib, recv_nb), bufs, started = _copy_stage(name + "_start", list(srcs) + list(lands), [],
                                                             [3 * npc, npc, 2 * npc], start)
    srcs, lands = bufs[:ns], bufs[ns:]
    (send_f, recv_f, send_d, recv_d), lands, _ = _copy_stage(name + "_pass_a", lands, [recv_nb],
                                                             [2 * npc, 2 * npc, npc, npc],
                                                             lambda b, i, o: pass_a(b, i, o), after=meanwhile(started))
    (send_g, recv_g), lands, tok = _copy_stage(name + "_pass_b", lands, [recv_d], [npc, npc], pass_b)
    _, bufs, _ = _copy_stage(name + "_last", list(srcs) + list(lands),
                             [send_a, recv_sib, send_f, recv_f, send_d, send_g, recv_g], [], last)
    return bufs[ns:], tok


def _paired_gather(srcs, lands, pieces, groups, after, name):
    ns, ng = len(srcs), len(groups)
    far = (1, 2, 3)

    def me_sib():
        x, y, c = _mesh_pos()
        return 4 * x + 2 * y + c, (x, y, 1 - c)

    def start(b, _, o):
        me, sib = me_sib()
        for g, members in enumerate(groups):
            send, recv_sib, recv_far = o[3 * g:3 * g + 3]
            for r, p in enumerate(members):
                src, dst = _piece_refs(pieces[p], b[:ns], b[ns:], me, 0)
                _remote(src, dst, send.at[4 * r], recv_sib.at[r], sib).start()
                for j in far:
                    _remote(src, dst, send.at[4 * r + j], recv_far.at[3 * r + j - 1], _peer(2 * j)[0]).start()

    def forward(b, i, o):
        _, sib = me_sib()
        for g, members in enumerate(groups):
            send_f, recv_f = o[2 * g:2 * g + 2]
            for r, p in enumerate(members):
                for j in far:
                    blk = _landed_block(pieces[p], b, _peer(2 * j)[1])
                    _remote(blk, blk, send_f.at[3 * r + j - 1], i[g].at[3 * r + j - 1], sib).wait_recv()
                    _remote(blk, blk, send_f.at[3 * r + j - 1], recv_f.at[3 * r + j - 1], sib).start()

    def last(sub):
        def emit(b, i, _):
            send, recv_sib, send_f, recv_f = i
            me, sib = me_sib()
            for r, piece in enumerate(sub):
                src, dst = _piece_refs(piece, b[:-1], b[-1:], me, 0)
                _remote(src, dst, send.at[4 * r], recv_sib.at[r], sib).wait_recv()
                for j in range(4):
                    _remote(src, dst, send.at[4 * r + j], recv_sib.at[r], sib).wait_send()
                for j in far:
                    blk = _landed_block(piece, b[-1:], _peer(2 * j + 1)[1])
                    _remote(blk, blk, send_f.at[3 * r + j - 1], recv_f.at[3 * r + j - 1], sib).wait_recv()
                    _remote(blk, blk, send_f.at[3 * r + j - 1], recv_f.at[3 * r + j - 1], sib).wait_send()
        return emit

    sizes = []
    for members in groups:
        sizes += [4 * len(members), len(members), 3 * len(members)]
    sems, bufs, started = _copy_stage(name + "_start", list(srcs) + list(lands), [], sizes, start, after=after)
    srcs = bufs[:ns]
    state = dict(lands=bufs[ns:])

    def finish(g, after):
        if "passed" not in state:
            sizes_f = []
            for members in groups:
                sizes_f += [3 * len(members)] * 2
            state["passed"], state["lands"], _ = _copy_stage(name + "_pass", state["lands"],
                                                             [sems[3 * k + 2] for k in range(ng)], sizes_f, forward,
                                                             after=after)
            after = ()
        members = groups[g]
        sub = [(pieces[p][0], r, 0) + pieces[p][3:] for r, p in enumerate(members)]
        _, bufs, _ = _copy_stage("%s_last_%d" % (name, g), [srcs[pieces[p][1]] for p in members] + [state["lands"][g]],
                                 [sems[3 * g], sems[3 * g + 1], state["passed"][2 * g], state["passed"][2 * g + 1]],
                                 [], last(sub), after=after)
        return bufs[-1]

    return finish, started


_SMALL_NAMES = ("ln1_g", "ln1_b", "hg_lb_logits", "hg_norm_g", "sg_ln_g", "sg_ln_b", "sg_w_s", "sg_b_s",
                "ln2_g", "ln2_b", "mem_ln_g", "mem_ln_b", "ln3_g", "ln3_b", "ln4_g", "ln4_b")


_VEC_NAMES = ("ln1_g", "ln1_b", "ln2_g", "ln2_b", "mem_ln_g", "mem_ln_b", "ln3_g", "ln3_b", "ln4_g", "ln4_b")
_ROW_NAMES = ("hg_lb_logits", "hg_norm_g", "sg_ln_g", "sg_ln_b", "sg_b_s", "sg_w_s")
VEC_ROWS = 16


def _row_plan(shapes):
    plan, pos = {}, 0
    for k in _ROW_NAMES:
        shp = shapes[k]
        slabs, off = [], pos
        for idx in itertools.product(*[range(dim) for dim in shp[:-2]]):
            slabs.append((idx, off, shp[-2]))
            off += shp[-2]
        plan[k] = (pos, slabs)
        pos = -(-off // 8) * 8
    return plan, -(-pos // 16) * 16


def _pack_small_grads(gs, shapes, loss):
    d = gs[_VEC_NAMES[0]].size
    vec = jnp.concatenate([gs[k].reshape(1, -1) for k in _VEC_NAMES] + [jnp.tile(loss, (1, d // LANES))], axis=0)
    vec = jnp.pad(vec, ((0, VEC_ROWS - vec.shape[0]), (0, 0)))
    plan, total = _row_plan(shapes)
    parts, pos = [], 0
    for k in _ROW_NAMES:
        first, slabs = plan[k]
        rows = gs[k].reshape(-1, LANES)
        end = slabs[-1][1] + slabs[-1][2]
        nxt = -(-end // 8) * 8
        parts.append(jnp.pad(rows, ((0, nxt - first - rows.shape[0]), (0, 0))))
        pos = nxt
    parts.append(jnp.zeros((total - pos, LANES), F32))
    return vec, jnp.concatenate(parts, axis=0)


def _adam_small(land_vec, land_rows, w, m, v):
    names = _VEC_NAMES + _ROW_NAMES
    n = len(names)
    shapes = {k: w[k].shape for k in names}
    plan, _ = _row_plan(shapes)

    def body(*refs):
        lv_ref, lr_ref = refs[:2]
        w_refs, m_refs, v_refs = refs[2:2 + n], refs[2 + n:2 + 2 * n], refs[2 + 2 * n:2 + 3 * n]
        outs = refs[2 + 3 * n:2 + 7 * n]
        loss_ref = refs[2 + 7 * n]
        gv_s, gr_s = refs[3 + 7 * n:]
        gv_s[...] = _slot_sum(lv_ref)
        gr_s[...] = _slot_sum(lr_ref)
        loss_ref[...] = gv_s[len(_VEC_NAMES):len(_VEC_NAMES) + 1, :LANES]
        for p, k in enumerate(names):
            if k in _VEC_NAMES:
                row = _VEC_NAMES.index(k)
                slabs = [((), None, None)]
            else:
                slabs = plan[k][1]
            for idx, off, rows in slabs:
                g = gv_s[row:row + 1, :] if off is None else gr_s[off:off + rows, :]
                sel = idx + (slice(None), slice(None))
                delta, nm, nv = _adamw(w_refs[p][sel], g, m_refs[p][sel], v_refs[p][sel])
                for o, val in zip(range(4), (g, delta, nm, nv)):
                    outs[o * n + p][sel] = val

    flat = lambda tree: [tree[k] for k in names]
    shp = [jax.ShapeDtypeStruct(shapes[k], F32) for k in names]
    out = pl.pallas_call(
        body,
        out_shape=shp * 4 + [jax.ShapeDtypeStruct((1, LANES), F32)],
        scratch_shapes=[pltpu.VMEM(land_vec.shape[1:], F32), pltpu.VMEM(land_rows.shape[1:], F32)],
        name="adam_small",
    )(land_vec, land_rows, *flat(w), *flat(m), *flat(v))
    return [dict(zip(names, out[o * n:(o + 1) * n])) for o in range(4)], out[4 * n]


_COL_FFN = ("ffn1_w_gate", "ffn1_w_up", "ffn2_w_gate", "ffn2_w_up")
_ROW_FFN = ("ffn1_w_down", "ffn2_w_down")
_ROW_SQ = ("w_out", "xa_w_q", "xa_w_k", "xa_w_v", "xa_w_o")
_BIG_NAMES = ("ffn1_w_gate", "ffn1_w_up", "ffn1_w_down", "w_in", "w_out", "xa_w_q", "xa_w_k", "xa_w_v", "xa_w_o",
              "ffn2_w_gate", "ffn2_w_up", "ffn2_w_down")


def _ffn_split(fs):
    main = (fs // MXU_WIDTH_V7X) * MXU_WIDTH_V7X
    tail = fs - main
    tail_pad = -(-tail // LANES) * LANES
    assert main > 0 and tail > 0
    return main, tail, tail_pad


def _layout(name, shard_shape):
    r, c = shard_shape
    if name in _COL_FFN:
        main, tail, pad = _ffn_split(c)
        return (r, N_DEV * (main + pad)), [(1, 0, main, (r, main), (0, main)),
                                           (1, N_DEV * main, pad, (r, pad), (main, c))]
    if name in _ROW_FFN:
        main, tail, pad = _ffn_split(r)
        return (N_DEV * (main + pad), c), [(0, 0, main, (main, c), (0, main)),
                                           (0, N_DEV * main, pad, (pad, c), (main, r))]
    if name == "w_in":
        return (r, N_DEV * c), [(1, 0, c, (r, c), (0, c))]
    return (N_DEV * r, c), [(0, 0, r, (r, c), (0, r))]


def _shard_pieces(name, shard):
    out = []
    for axis, _, _, shape, (lo, hi) in _layout(name, shard.shape)[1]:
        part = shard[lo:hi, :] if axis == 0 else shard[:, lo:hi]
        pad = [(0, shape[0] - part.shape[0]), (0, shape[1] - part.shape[1])]
        out.append(jnp.pad(part, pad).astype(BF16))
    return out


def _gather_plan(names, shards):
    srcs, land_shapes, pieces, index = [], [], [], {}
    for li, name in enumerate(names):
        shape2d, parts = _layout(name, shards[name].shape)
        land_shapes.append(jax.ShapeDtypeStruct(shape2d, BF16))
        index[name] = []
        for (axis, base, stride, shape, _), src in zip(parts, _shard_pieces(name, shards[name])):
            index[name].append(len(pieces))
            pieces.append(("gather", len(srcs), li, axis, base, stride, shape))
            srcs.append(src)
    return srcs, land_shapes, pieces, index


def _scatter_plan(names, grads, shard_shapes):
    srcs, land_shapes, pieces, index = [], [], [], {}
    for si, name in enumerate(names):
        _, parts = _layout(name, shard_shapes[name])
        srcs.append(grads[name])
        index[name] = []
        for axis, base, stride, shape, _ in parts:
            index[name].append(len(land_shapes))
            pieces.append(("scatter", si, len(land_shapes), axis, base, stride, shape))
            land_shapes.append(jax.ShapeDtypeStruct((N_DEV,) + shape, grads[name].dtype))
    return srcs, land_shapes, pieces, index


def _small_views(small):
    row = lambda a: a.reshape(1, -1)
    ln = {k: row(small[k]) for k in ("ln1_g", "ln1_b", "ln2_g", "ln2_b", "ln3_g", "ln3_b", "ln4_g", "ln4_b",
                                      "mem_ln_g", "mem_ln_b", "hg_norm_g")}
    sg_w = small["sg_w_s"].reshape(SG_GROUPS, SG_CHUNK, SG_CHUNK)
    sg = dict(logits=jnp.swapaxes(small["hg_lb_logits"], 0, 1),
              g=small["sg_ln_g"].reshape(SG_GROUPS, 1, SG_DIM), b=small["sg_ln_b"].reshape(SG_GROUPS, 1, SG_DIM),
              w=sg_w, bs=small["sg_b_s"].reshape(SG_GROUPS, 1, SG_CHUNK))
    return ln, sg


def _forward(x, xb, mem, target, get_w, small, first_deps=()):
    ln, sg = _small_views(small)
    a1, b1, s1 = _ffn_up(xb, get_w("ffn1_w_gate", ()), get_w("ffn1_w_up", ()), "ffn1_up", deps=first_deps)
    h1b, xh1, rs1 = _mm_res_ln(s1, get_w("ffn1_w_down", (s1,)), x, ln["ln1_g"], ln["ln1_b"], 0.5, "ffn1_down_ln")
    proj = _mm_nn(h1b, get_w("w_in", (h1b,)), "mix_in")
    oraw, mix, states = _hgrn_fwd(proj, sg["logits"], ln["hg_norm_g"])
    mix = _sgu_fwd(proj, mix, sg["g"], sg["b"], sg["w"], sg["bs"])
    h2b, xh2, rs2 = _mm_res_ln(mix, get_w("w_out", (mix,)), (xh1, ln["ln1_g"], ln["ln1_b"]), ln["ln2_g"], ln["ln2_b"],
                               1.0, "mix_out_ln")
    mb, mxh, mrs, kb, vb = _mem_kv(mem, ln["mem_ln_g"], ln["mem_ln_b"], get_w("xa_w_k", (h2b,)), get_w("xa_w_v", (h2b,)))
    qb, att = _attn_fwd(h2b, get_w("xa_w_q", (mrs,)), kb, vb)
    h3b, xh3, rs3 = _mm_res_ln(att, get_w("xa_w_o", (att,)), (xh2, ln["ln2_g"], ln["ln2_b"]), ln["ln3_g"], ln["ln3_b"],
                               1.0, "attn_out_ln")
    a2, b2, s2 = _ffn_up(h3b, get_w("ffn2_w_gate", (h3b,)), get_w("ffn2_w_up", (h3b,)), "ffn2_up")
    loss, dy4, dy4b, dg4, db4 = _mm_res_ln(s2, get_w("ffn2_w_down", (s2,)), (xh3, ln["ln3_g"], ln["ln3_b"]),
                                           ln["ln4_g"], ln["ln4_b"], 0.5, "ffn2_down_ln_loss", target=target)
    return dict(xb=xb, a1=a1, b1=b1, s1=s1, h1b=h1b, xh1=xh1, rs1=rs1, proj=proj, oraw=oraw, mix=mix, states=states,
                h2b=h2b, xh2=xh2, rs2=rs2, mb=mb, mxh=mxh, mrs=mrs, kb=kb, vb=vb, qb=qb, att=att, h3b=h3b, xh3=xh3,
                rs3=rs3, a2=a2, b2=b2, s2=s2, loss=loss, dy4=dy4, dy4b=dy4b, dg4=dg4, db4=db4)


def _backward(sv, wt, small, send):
    ln, sg = _small_views(small)
    gs = {"ln4_g": sv["dg4"], "ln4_b": sv["db4"]}
    loss, dy4, dy4b = sv["loss"], sv["dy4"], sv["dy4b"]
    g_down2 = _mm_tn(sv["s2"], dy4b, "g_ffn2_down", scale=0.5)
    da2, db2, dy3, dy3b, gs["ln3_g"], gs["ln3_b"] = _ffn_bwd_fused(
        dy4, dy4b, wt["ffn2_w_down"], wt["ffn2_w_gate"], wt["ffn2_w_up"], sv["a2"], sv["b2"], 0.5,
        (sv["xh3"], sv["rs3"], ln["ln3_g"]), "ffn2_bwd")
    g_gate2 = _mm_tn(sv["h3b"], da2, "g_ffn2_gate")
    g_up2 = _mm_tn(sv["h3b"], db2, "g_ffn2_up")
    tok = send(("ffn2_w_down", "ffn2_w_gate", "ffn2_w_up"), (g_down2, g_gate2, g_up2))

    g_o = _mm_tn(sv["att"], dy3b, "g_xa_o", deps=(tok,))
    dqb, dk, dv = _attn_bwd(dy3b, wt["xa_w_o"], sv["qb"], sv["kb"], sv["vb"])
    g_q = _mm_tn(sv["h2b"], dqb, "g_xa_q")
    g_k, g_v, gs["mem_ln_g"], gs["mem_ln_b"] = _mem_bwd(dk, dv, sv["mb"], sv["mxh"], sv["mrs"], ln["mem_ln_g"],
                                                        wt["xa_w_k"], wt["xa_w_v"])
    tok = send(("xa_w_o", "xa_w_q", "xa_w_k", "xa_w_v"), (g_o, g_q, g_k, g_v))
    dy2, dy2b, gs["ln2_g"], gs["ln2_b"] = _dx_ln(dy3, [(dqb, wt["xa_w_q"])], (sv["xh2"], sv["rs2"], ln["ln2_g"]),
                                                 "attn_dx_ln", deps=(tok,))

    g_out = _mm_tn(sv["mix"], dy2b, "g_w_out")
    dmix = _mm_nt(dy2b, wt["w_out"], "mix_out_bwd")
    dq, dfz, div, dgg, dlg, dgn = _hgrn_bwd(sv["proj"], sv["oraw"], dmix, sv["states"], sg["logits"], ln["hg_norm_g"])
    du, dvv, gs["sg_ln_g"], gs["sg_ln_b"], gs["sg_w_s"], gs["sg_b_s"] = _sgu_bwd(
        sv["proj"], dmix, sg["g"], sg["b"], sg["w"], sg["bs"])
    gs["hg_lb_logits"] = jnp.swapaxes(dlg, 0, 1)
    gs["hg_norm_g"] = jnp.sum(dgn, axis=0)
    dproj = [dq, dfz, div, dgg, du, dvv]
    g_in = _mm_tn(sv["h1b"], dproj, "g_w_in")
    tok = send(("w_out", "w_in"), (g_out, g_in))
    dy1, dy1b, gs["ln1_g"], gs["ln1_b"] = _dx_ln(dy2, [(dproj, wt["w_in"])], (sv["xh1"], sv["rs1"], ln["ln1_g"]),
                                                 "mix_dx_ln", deps=(tok,))

    g_down1 = _mm_tn(sv["s1"], dy1b, "g_ffn1_down", scale=0.5)
    tok = send(("ffn1_w_down",), (g_down1,))
    da1, db1 = _ffn_bwd_act(dy1b, wt["ffn1_w_down"], sv["a1"], sv["b1"], 0.5, "ffn1_bwd_act", deps=(tok,))
    g_gate1 = _mm_tn(sv["xb"], da1, "g_ffn1_gate")
    tok = send(("ffn1_w_gate",), (g_gate1,))
    g_up1 = _mm_tn(sv["xb"], db1, "g_ffn1_up", deps=(tok,))
    tok = send(("ffn1_w_up",), (g_up1,))
    grad_x = _dx_ln(dy1, [(da1, wt["ffn1_w_gate"]), (db1, wt["ffn1_w_up"])], None, "ffn1_dx", deps=(tok,))
    return loss, grad_x, gs


_WEIGHT_NAMES = ("ffn1_w_gate", "ffn1_w_up", "ffn1_w_down", "ln1_g", "ln1_b", "w_in", "hg_lb_logits", "hg_norm_g",
                 "sg_ln_g", "sg_ln_b", "sg_w_s", "sg_b_s", "w_out", "ln2_g", "ln2_b", "mem_ln_g", "mem_ln_b",
                 "xa_w_q", "xa_w_k", "xa_w_v", "xa_w_o", "ln3_g", "ln3_b", "ffn2_w_gate", "ffn2_w_up", "ffn2_w_down",
                 "ln4_g", "ln4_b")
_FIRST = ("ffn1_w_gate", "ffn1_w_up")
_SECOND = ("ffn1_w_down", "w_in")
_THIRD = ("w_out", "xa_w_k", "xa_w_v", "xa_w_q", "xa_w_o", "ffn2_w_gate", "ffn2_w_up", "ffn2_w_down")


def kernel(x, mem, ffn1_w_gate, ffn1_w_up, ffn1_w_down, ln1_g, ln1_b, w_in, hg_lb_logits, hg_norm_g, sg_ln_g, sg_ln_b, sg_w_s, sg_b_s, w_out, ln2_g, ln2_b, mem_ln_g, mem_ln_b, xa_w_q, xa_w_k, xa_w_v, xa_w_o, ln3_g, ln3_b, ffn2_w_gate, ffn2_w_up, ffn2_w_down, ln4_g, ln4_b, loss_target, m_ffn1_w_gate, m_ffn1_w_up, m_ffn1_w_down, m_ln1_g, m_ln1_b, m_w_in, m_hg_lb_logits, m_hg_norm_g, m_sg_ln_g, m_sg_ln_b, m_sg_w_s, m_sg_b_s, m_w_out, m_ln2_g, m_ln2_b, m_mem_ln_g, m_mem_ln_b, m_xa_w_q, m_xa_w_k, m_xa_w_v, m_xa_w_o, m_ln3_g, m_ln3_b, m_ffn2_w_gate, m_ffn2_w_up, m_ffn2_w_down, m_ln4_g, m_ln4_b, v_ffn1_w_gate, v_ffn1_w_up, v_ffn1_w_down, v_ln1_g, v_ln1_b, v_w_in, v_hg_lb_logits, v_hg_norm_g, v_sg_ln_g, v_sg_ln_b, v_sg_w_s, v_sg_b_s, v_w_out, v_ln2_g, v_ln2_b, v_mem_ln_g, v_mem_ln_b, v_xa_w_q, v_xa_w_k, v_xa_w_v, v_xa_w_o, v_ln3_g, v_ln3_b, v_ffn2_w_gate, v_ffn2_w_up, v_ffn2_w_down, v_ln4_g, v_ln4_b):
    args = dict(locals())
    w = {k: args[k] for k in _WEIGHT_NAMES}
    m = {k: args["m_" + k] for k in _WEIGHT_NAMES}
    v = {k: args["v_" + k] for k in _WEIGHT_NAMES}
    shards = {k: w[k][0] for k in _BIG_NAMES}
    shard_shapes = {k: shards[k].shape for k in _BIG_NAMES}
    small = {k: (w[k][0] if k != "hg_lb_logits" else w[k]) for k in _SMALL_NAMES}

    srcs1, shapes1, pieces1, idx1 = _gather_plan(_FIRST, shards)
    lands1 = _place_own(srcs1, shapes1, pieces1, "gather_first_own")
    prepared = {}

    def prepare_rest(started):
        later = {k: shards[k] + started[0, 0] for k in _SECOND + _THIRD}
        placed = ()
        for key, names in (("second", _SECOND), ("third", _THIRD)):
            srcs, shapes, pieces, idx = _gather_plan(names, later)
            lands = _place_own(srcs, shapes, pieces, "gather_%s_own" % key)
            prepared[key] = (srcs, lands, pieces, idx)
            placed += tuple(lands)
        prepared["xb"] = _to_bf16(x[0], "x_bf16", deps=(started,))
        return placed + (prepared["xb"],)

    lands1, tok1 = _routed_gather(srcs1, lands1, pieces1, prepare_rest, "gather_first")
    srcs_p, lands_p, pieces_p, idx_p = prepared["second"]
    finish_second, tok_p = _paired_gather(srcs_p, lands_p, pieces_p, [list(idx_p[k]) for k in _SECOND], (tok1,),
                                          "gather_second")
    srcs2, lands2, pieces2, idx2 = prepared["third"]
    groups2 = [list(idx2[k]) for k in _THIRD]
    sems2, srcs2, lands2, tok2 = _comm_start(srcs2, lands2, pieces2, groups2, "gather_third_start", after=(tok_p,))
    wt = dict(zip(_FIRST, lands1))
    pending = {k: gi for gi, k in enumerate(_THIRD)}

    def get_w(name, after):
        if name in _SECOND and name not in wt:
            wt[name] = finish_second(_SECOND.index(name), after)
        if name in pending:
            gi = pending.pop(name)
            si = [pieces2[p][1] for p in groups2[gi]]
            sub = [(pieces2[p][0], row, 0) + pieces2[p][3:] for row, p in enumerate(groups2[gi])]
            wt[name] = _comm_wait([srcs2[s] for s in si], [lands2[gi]], sub, list(range(len(sub))), sems2[gi],
                                  after, "gather_wait_" + name)[0]
        return wt[name]

    sv = _forward(x[0], prepared["xb"], mem[0], loss_target[0], get_w, small, first_deps=(tok2,))

    sent = []

    def send(names, grads):
        srcs, shapes, pieces, idx = _scatter_plan(names, dict(zip(names, grads)), shard_shapes)
        lands = _place_own(srcs, shapes, pieces, "grads_own_%d" % len(sent))
        sems, srcs, lands, tok = _comm_start(srcs, lands, pieces, [list(range(len(pieces)))],
                                             "grads_start_%d" % len(sent))
        sent.append((names, srcs, lands, pieces, idx, sems[0]))
        return tok

    loss, grad_x, gs = _backward(sv, wt, small, send)

    ssrc = list(_pack_small_grads(gs, {k: w[k].shape for k in _SMALL_NAMES}, loss))
    sp = [("scatter", i, i, 0, 0, 0, a.shape) for i, a in enumerate(ssrc)]
    sshape = [jax.ShapeDtypeStruct((N_DEV,) + a.shape, F32) for a in ssrc]
    sl = _place_own(ssrc, sshape, sp, "small_own")
    ssem, ssrc, sl, _ = _comm_start(ssrc, sl, sp, [[0, 1]], "small_start")

    out_g, out_d, out_m, out_v = {}, {}, {}, {}
    after = (grad_x,)
    for n_sent, (names, srcs, lands, pieces, idx, sems) in enumerate(sent):
        lands = _comm_wait(srcs, lands, pieces, list(range(len(pieces))), sems, after, "grads_wait_%d" % n_sent)
        for k in names:
            axis = 1 if (k in _COL_FFN or k == "w_in") else 0
            if k in _COL_FFN:
                done = _adam_sharded([lands[i] for i in idx[k]], w[k][0].T, m[k][0].T, v[k][0].T, axis, "adam_" + k)
                res = [r.T for r in done]
            else:
                res = done = _adam_sharded([lands[i] for i in idx[k]], w[k][0], m[k][0], v[k][0], axis, "adam_" + k)
            out_g[k], out_d[k], out_m[k], out_v[k] = [r[None] for r in res]
        after = (done[3],)
    sl = _comm_wait(ssrc, sl, sp, [0, 1], ssem[0], after, "small_wait")
    small_out, loss_sum = _adam_small(sl[0], sl[1], w, m, v)
    for dst, res in zip((out_g, out_d, out_m, out_v), small_out):
        dst.update(res)
    loss_all = loss_sum[0, 0]
    return (loss_all, grad_x[None], *[out_g[k] for k in _WEIGHT_NAMES], *[out_d[k] for k in _WEIGHT_NAMES],
            *[out_m[k] for k in _WEIGHT_NAMES], *[out_v[k] for k in _WEIGHT_NAMES])
```
